```python
import math
import jax, jax.numpy as jnp
from jax import lax
import numpy as np

D_MODEL = 1024
BATCH = 8
SEQ = 4096
DEPTH = 1

D_MIX = D_MODEL
D_RNN = D_MIX // 2
RNN_BLOCKS = 8
RNN_BW = D_RNN // RNN_BLOCKS
CONV_W = 4
RG_C = 8.0
D_ATT = D_MIX - D_RNN
HEAD_DIM = 64
N_HEADS = D_ATT // HEAD_DIM
Q_BLOCK = 128
D_FF = 2816
N_IN = 2 * D_RNN + 3 * D_ATT
EPS = 1e-6

kernel_name = "hybrid_rglru_stickbreaking_macaron"


def _rms_norm(x, g):
    xf = x.astype(jnp.float32)
    r = lax.rsqrt(jnp.mean(xf * xf, axis=-1, keepdims=True) + EPS)
    return (xf * r * g.astype(jnp.float32)).astype(x.dtype)


def _swiglu(x, w_gate, w_up, w_down):
    return (jax.nn.silu(x @ w_gate) * (x @ w_up)) @ w_down


def _lin_combine(left, right):
    a1, b1 = left
    a2, b2 = right
    return a1 * a2, a2 * b1 + b2


def _rg_lru_group(xr, gate, conv_w, conv_b, w_a, b_a, w_x, b_x, lam):
    bsz, seq, _ = xr.shape
    kern = conv_w.astype(xr.dtype)[:, None, :]
    xc = lax.conv_general_dilated(
        xr, kern, window_strides=(1,), padding=[(CONV_W - 1, 0)],
        dimension_numbers=("NWC", "WIO", "NWC"), feature_group_count=D_RNN,
    ) + conv_b
    xb = xc.reshape(bsz, seq, RNN_BLOCKS, RNN_BW)
    r = jax.nn.sigmoid(jnp.einsum("bsnc,ncd->bsnd", xb, w_a).reshape(bsz, seq, D_RNN) + b_a)
    i = jax.nn.sigmoid(jnp.einsum("bsnc,ncd->bsnd", xb, w_x).reshape(bsz, seq, D_RNN) + b_x)
    log_a = RG_C * r.astype(jnp.float32) * jax.nn.log_sigmoid(lam.astype(jnp.float32))
    a = jnp.exp(log_a)
    mult = jnp.sqrt(-jnp.expm1(2.0 * log_a))
    b = mult * (i * xc).astype(jnp.float32)
    _, h = lax.associative_scan(_lin_combine, (a, b), axis=1)
    return h.astype(xr.dtype) * jax.nn.gelu(gate)


def _stick_breaking(q, k, v):
    seq = q.shape[2]
    k_pos = jnp.arange(seq)
    kf = k.astype(jnp.float32)
    vf = v.astype(jnp.float32)

    def block(i):
        start = i * Q_BLOCK
        qb = lax.dynamic_slice_in_dim(q, start, Q_BLOCK, axis=2).astype(jnp.float32)
        z = jnp.einsum("bhqd,bhkd->bhqk", qb, kf)
        q_pos = start + jnp.arange(Q_BLOCK)
        causal = k_pos[None, :] < q_pos[:, None]
        log_beta = jax.nn.log_sigmoid(z)
        log_1m = jnp.where(causal, jax.nn.log_sigmoid(-z), 0.0)
        tail = lax.cumsum(log_1m, axis=log_1m.ndim - 1, reverse=True) - log_1m
        w = jnp.where(causal, jnp.exp(log_beta + tail), 0.0)
        return jnp.einsum("bhqk,bhkd->bhqd", w, vf)

    out = lax.map(block, jnp.arange(seq // Q_BLOCK))
    nb, bsz, nh, qb, dh = out.shape
    out = jnp.transpose(out, (1, 0, 3, 2, 4)).reshape(bsz, seq, nh * dh)
    return out.astype(q.dtype)


def _fwd_setup_inputs(seed: int = 0) -> dict:
    key = jax.random.key(seed)
    ks = jax.random.split(key, 24)
    f32 = jnp.float32

    def nrm(k, shape, scale):
        return jax.random.normal(k, shape, f32) * scale

    def gain(k, n):
        return 1.0 + 0.02 * jax.random.normal(k, (DEPTH, n), f32)

    a_base = jax.random.uniform(ks[13], (DEPTH, D_RNN), f32, 0.9, 0.999)
    s = a_base ** (1.0 / RG_C)
    rg_lambda = jnp.log(s) - jnp.log1p(-s)
    return {
        "x": nrm(ks[0], (BATCH, SEQ, D_MODEL), 1.0),
        "ffn1_norm": gain(ks[1], D_MODEL),
        "ffn1_w_gate": nrm(ks[2], (DEPTH, D_MODEL, D_FF), D_MODEL ** -0.5),
        "ffn1_w_up": nrm(ks[3], (DEPTH, D_MODEL, D_FF), D_MODEL ** -0.5),
        "ffn1_w_down": nrm(ks[4], (DEPTH, D_FF, D_MODEL), D_FF ** -0.5),
        "mix_norm": gain(ks[5], D_MODEL),
        "w_in": nrm(ks[6], (DEPTH, D_MODEL, N_IN), D_MODEL ** -0.5),
        "conv_w": nrm(ks[7], (DEPTH, CONV_W, D_RNN), CONV_W ** -0.5),
        "conv_b": nrm(ks[8], (DEPTH, D_RNN), 0.01),
        "rg_w_a": nrm(ks[9], (DEPTH, RNN_BLOCKS, RNN_BW, RNN_BW), RNN_BW ** -0.5),
        "rg_b_a": nrm(ks[10], (DEPTH, D_RNN), 0.01),
        "rg_w_x": nrm(ks[11], (DEPTH, RNN_BLOCKS, RNN_BW, RNN_BW), RNN_BW ** -0.5),
        "rg_b_x": nrm(ks[12], (DEPTH, D_RNN), 0.01),
        "rg_lambda": rg_lambda,
        "q_norm": gain(ks[14], HEAD_DIM),
        "k_norm": gain(ks[15], HEAD_DIM),
        "rnn_out_norm": gain(ks[16], D_RNN),
        "attn_out_norm": gain(ks[17], D_ATT),
        "w_out": nrm(ks[18], (DEPTH, D_MIX, D_MODEL), D_MIX ** -0.5),
        "ffn2_norm": gain(ks[19], D_MODEL),
        "ffn2_w_gate": nrm(ks[20], (DEPTH, D_MODEL, D_FF), D_MODEL ** -0.5),
        "ffn2_w_up": nrm(ks[21], (DEPTH, D_MODEL, D_FF), D_MODEL ** -0.5),
        "ffn2_w_down": nrm(ks[22], (DEPTH, D_FF, D_MODEL), D_FF ** -0.5),
    }


def _fwd_reference(x, ffn1_norm, ffn1_w_gate, ffn1_w_up, ffn1_w_down, mix_norm, w_in,
              conv_w, conv_b, rg_w_a, rg_b_a, rg_w_x, rg_b_x, rg_lambda,
              q_norm, k_norm, rnn_out_norm, attn_out_norm, w_out,
              ffn2_norm, ffn2_w_gate, ffn2_w_up, ffn2_w_down):
    bsz, seq, _ = x.shape
    scale = 1.0 / math.sqrt(HEAD_DIM)
    for l in range(DEPTH):
        x = x + 0.5 * _swiglu(_rms_norm(x, ffn1_norm[l]), ffn1_w_gate[l], ffn1_w_up[l], ffn1_w_down[l])

        h = _rms_norm(x, mix_norm[l])
        proj = h @ w_in[l]
        xr, gate, q, k, v = jnp.split(
            proj, [D_RNN, 2 * D_RNN, 2 * D_RNN + D_ATT, 2 * D_RNN + 2 * D_ATT], axis=-1)

        y_rnn = _rg_lru_group(xr, gate, conv_w[l], conv_b[l], rg_w_a[l], rg_b_a[l],
                              rg_w_x[l], rg_b_x[l], rg_lambda[l])

        def heads(t):
            return jnp.transpose(t.reshape(bsz, seq, N_HEADS, HEAD_DIM), (0, 2, 1, 3))
        qh = _rms_norm(heads(q), q_norm[l]) * scale
        kh = _rms_norm(heads(k), k_norm[l])
        y_att = _stick_breaking(qh, kh, heads(v))

        y = jnp.concatenate([_rms_norm(y_rnn, rnn_out_norm[l]),
                             _rms_norm(y_att, attn_out_norm[l])], axis=-1)
        x = x + y @ w_out[l]

        x = x + 0.5 * _swiglu(_rms_norm(x, ffn2_norm[l]), ffn2_w_gate[l], ffn2_w_up[l], ffn2_w_down[l])
    return x


import jax as _jax
import jax.numpy as _jnp

TWIN_FORMAT = 'train_step'
FWD_PARAMS = ['x', 'ffn1_norm', 'ffn1_w_gate', 'ffn1_w_up', 'ffn1_w_down', 'mix_norm', 'w_in', 'conv_w', 'conv_b', 'rg_w_a', 'rg_b_a', 'rg_w_x', 'rg_b_x', 'rg_lambda', 'q_norm', 'k_norm', 'rnn_out_norm', 'attn_out_norm', 'w_out', 'ffn2_norm', 'ffn2_w_gate', 'ffn2_w_up', 'ffn2_w_down']
TWIN_WEIGHTS = ['ffn1_norm', 'ffn1_w_gate', 'ffn1_w_up', 'ffn1_w_down', 'mix_norm', 'w_in', 'conv_w', 'conv_b', 'rg_w_a', 'rg_b_a', 'rg_w_x', 'rg_b_x', 'rg_lambda', 'q_norm', 'k_norm', 'rnn_out_norm', 'attn_out_norm', 'w_out', 'ffn2_norm', 'ffn2_w_gate', 'ffn2_w_up', 'ffn2_w_down']
TWIN_DIFF_INPUT = 'x'
TWIN_INPUTS = ['x', 'ffn1_norm', 'ffn1_w_gate', 'ffn1_w_up', 'ffn1_w_down', 'mix_norm', 'w_in', 'conv_w', 'conv_b', 'rg_w_a', 'rg_b_a', 'rg_w_x', 'rg_b_x', 'rg_lambda', 'q_norm', 'k_norm', 'rnn_out_norm', 'attn_out_norm', 'w_out', 'ffn2_norm', 'ffn2_w_gate', 'ffn2_w_up', 'ffn2_w_down', 'loss_target', 'm_ffn1_norm', 'm_ffn1_w_gate', 'm_ffn1_w_up', 'm_ffn1_w_down', 'm_mix_norm', 'm_w_in', 'm_conv_w', 'm_conv_b', 'm_rg_w_a', 'm_rg_b_a', 'm_rg_w_x', 'm_rg_b_x', 'm_rg_lambda', 'm_q_norm', 'm_k_norm', 'm_rnn_out_norm', 'm_attn_out_norm', 'm_w_out', 'm_ffn2_norm', 'm_ffn2_w_gate', 'm_ffn2_w_up', 'm_ffn2_w_down', 'v_ffn1_norm', 'v_ffn1_w_gate', 'v_ffn1_w_up', 'v_ffn1_w_down', 'v_mix_norm', 'v_w_in', 'v_conv_w', 'v_conv_b', 'v_rg_w_a', 'v_rg_b_a', 'v_rg_w_x', 'v_rg_b_x', 'v_rg_lambda', 'v_q_norm', 'v_k_norm', 'v_rnn_out_norm', 'v_attn_out_norm', 'v_w_out', 'v_ffn2_norm', 'v_ffn2_w_gate', 'v_ffn2_w_up', 'v_ffn2_w_down']
TWIN_OUTPUTS = ['loss', 'grad_x', 'grad_ffn1_norm', 'grad_ffn1_w_gate', 'grad_ffn1_w_up', 'grad_ffn1_w_down', 'grad_mix_norm', 'grad_w_in', 'grad_conv_w', 'grad_conv_b', 'grad_rg_w_a', 'grad_rg_b_a', 'grad_rg_w_x', 'grad_rg_b_x', 'grad_rg_lambda', 'grad_q_norm', 'grad_k_norm', 'grad_rnn_out_norm', 'grad_attn_out_norm', 'grad_w_out', 'grad_ffn2_norm', 'grad_ffn2_w_gate', 'grad_ffn2_w_up', 'grad_ffn2_w_down', 'delta_ffn1_norm', 'delta_ffn1_w_gate', 'delta_ffn1_w_up', 'delta_ffn1_w_down', 'delta_mix_norm', 'delta_w_in', 'delta_conv_w', 'delta_conv_b', 'delta_rg_w_a', 'delta_rg_b_a', 'delta_rg_w_x', 'delta_rg_b_x', 'delta_rg_lambda', 'delta_q_norm', 'delta_k_norm', 'delta_rnn_out_norm', 'delta_attn_out_norm', 'delta_w_out', 'delta_ffn2_norm', 'delta_ffn2_w_gate', 'delta_ffn2_w_up', 'delta_ffn2_w_down', 'new_m_ffn1_norm', 'new_m_ffn1_w_gate', 'new_m_ffn1_w_up', 'new_m_ffn1_w_down', 'new_m_mix_norm', 'new_m_w_in', 'new_m_conv_w', 'new_m_conv_b', 'new_m_rg_w_a', 'new_m_rg_b_a', 'new_m_rg_w_x', 'new_m_rg_b_x', 'new_m_rg_lambda', 'new_m_q_norm', 'new_m_k_norm', 'new_m_rnn_out_norm', 'new_m_attn_out_norm', 'new_m_w_out', 'new_m_ffn2_norm', 'new_m_ffn2_w_gate', 'new_m_ffn2_w_up', 'new_m_ffn2_w_down', 'new_v_ffn1_norm', 'new_v_ffn1_w_gate', 'new_v_ffn1_w_up', 'new_v_ffn1_w_down', 'new_v_mix_norm', 'new_v_w_in', 'new_v_conv_w', 'new_v_conv_b', 'new_v_rg_w_a', 'new_v_rg_b_a', 'new_v_rg_w_x', 'new_v_rg_b_x', 'new_v_rg_lambda', 'new_v_q_norm', 'new_v_k_norm', 'new_v_rnn_out_norm', 'new_v_attn_out_norm', 'new_v_w_out', 'new_v_ffn2_norm', 'new_v_ffn2_w_gate', 'new_v_ffn2_w_up', 'new_v_ffn2_w_down']
TWIN_LEAF_KINDS = {'loss': 'loss', 'grad_x': 'grad_x', 'grad_ffn1_norm': 'grad_w', 'grad_ffn1_w_gate': 'grad_w', 'grad_ffn1_w_up': 'grad_w', 'grad_ffn1_w_down': 'grad_w', 'grad_mix_norm': 'grad_w', 'grad_w_in': 'grad_w', 'grad_conv_w': 'grad_w', 'grad_conv_b': 'grad_w', 'grad_rg_w_a': 'grad_w', 'grad_rg_b_a': 'grad_w', 'grad_rg_w_x': 'grad_w', 'grad_rg_b_x': 'grad_w', 'grad_rg_lambda': 'grad_w', 'grad_q_norm': 'grad_w', 'grad_k_norm': 'grad_w', 'grad_rnn_out_norm': 'grad_w', 'grad_attn_out_norm': 'grad_w', 'grad_w_out': 'grad_w', 'grad_ffn2_norm': 'grad_w', 'grad_ffn2_w_gate': 'grad_w', 'grad_ffn2_w_up': 'grad_w', 'grad_ffn2_w_down': 'grad_w', 'delta_ffn1_norm': 'delta_w', 'delta_ffn1_w_gate': 'delta_w', 'delta_ffn1_w_up': 'delta_w', 'delta_ffn1_w_down': 'delta_w', 'delta_mix_norm': 'delta_w', 'delta_w_in': 'delta_w', 'delta_conv_w': 'delta_w', 'delta_conv_b': 'delta_w', 'delta_rg_w_a': 'delta_w', 'delta_rg_b_a': 'delta_w', 'delta_rg_w_x': 'delta_w', 'delta_rg_b_x': 'delta_w', 'delta_rg_lambda': 'delta_w', 'delta_q_norm': 'delta_w', 'delta_k_norm': 'delta_w', 'delta_rnn_out_norm': 'delta_w', 'delta_attn_out_norm': 'delta_w', 'delta_w_out': 'delta_w', 'delta_ffn2_norm': 'delta_w', 'delta_ffn2_w_gate': 'delta_w', 'delta_ffn2_w_up': 'delta_w', 'delta_ffn2_w_down': 'delta_w', 'new_m_ffn1_norm': 'new_m', 'new_m_ffn1_w_gate': 'new_m', 'new_m_ffn1_w_up': 'new_m', 'new_m_ffn1_w_down': 'new_m', 'new_m_mix_norm': 'new_m', 'new_m_w_in': 'new_m', 'new_m_conv_w': 'new_m', 'new_m_conv_b': 'new_m', 'new_m_rg_w_a': 'new_m', 'new_m_rg_b_a': 'new_m', 'new_m_rg_w_x': 'new_m', 'new_m_rg_b_x': 'new_m', 'new_m_rg_lambda': 'new_m', 'new_m_q_norm': 'new_m', 'new_m_k_norm': 'new_m', 'new_m_rnn_out_norm': 'new_m', 'new_m_attn_out_norm': 'new_m', 'new_m_w_out': 'new_m', 'new_m_ffn2_norm': 'new_m', 'new_m_ffn2_w_gate': 'new_m', 'new_m_ffn2_w_up': 'new_m', 'new_m_ffn2_w_down': 'new_m', 'new_v_ffn1_norm': 'new_v', 'new_v_ffn1_w_gate': 'new_v', 'new_v_ffn1_w_up': 'new_v', 'new_v_ffn1_w_down': 'new_v', 'new_v_mix_norm': 'new_v', 'new_v_w_in': 'new_v', 'new_v_conv_w': 'new_v', 'new_v_conv_b': 'new_v', 'new_v_rg_w_a': 'new_v', 'new_v_rg_b_a': 'new_v', 'new_v_rg_w_x': 'new_v', 'new_v_rg_b_x': 'new_v', 'new_v_rg_lambda': 'new_v', 'new_v_q_norm': 'new_v', 'new_v_k_norm': 'new_v', 'new_v_rnn_out_norm': 'new_v', 'new_v_attn_out_norm': 'new_v', 'new_v_w_out': 'new_v', 'new_v_ffn2_norm': 'new_v', 'new_v_ffn2_w_gate': 'new_v', 'new_v_ffn2_w_up': 'new_v', 'new_v_ffn2_w_down': 'new_v'}


def _forward(args):
    return _fwd_reference(*[args[k] for k in FWD_PARAMS])


def _output_shape():
    def fwd():
        inp = _fwd_setup_inputs(0)
        return _fwd_reference(*[inp[k] for k in FWD_PARAMS])
    out = _jax.eval_shape(fwd)
    return out.shape, out.dtype

N_MICROBATCH = 1
ADAM_LR = 0.001
ADAM_B1 = 0.9
ADAM_B2 = 0.999
ADAM_EPS = 1e-08
ADAM_WD = 0.01
ADAM_STEP = 10
PER_EXAMPLE_BATCH_AXIS = {'x': 0, 'loss_target': 0}
SHARED_INPUTS = []
_WEIGHT_DTYPES = {'ffn1_norm': _jnp.float32, 'ffn1_w_gate': _jnp.float32, 'ffn1_w_up': _jnp.float32, 'ffn1_w_down': _jnp.float32, 'mix_norm': _jnp.float32, 'w_in': _jnp.float32, 'conv_w': _jnp.float32, 'conv_b': _jnp.float32, 'rg_w_a': _jnp.float32, 'rg_b_a': _jnp.float32, 'rg_w_x': _jnp.float32, 'rg_b_x': _jnp.float32, 'rg_lambda': _jnp.float32, 'q_norm': _jnp.float32, 'k_norm': _jnp.float32, 'rnn_out_norm': _jnp.float32, 'attn_out_norm': _jnp.float32, 'w_out': _jnp.float32, 'ffn2_norm': _jnp.float32, 'ffn2_w_gate': _jnp.float32, 'ffn2_w_up': _jnp.float32, 'ffn2_w_down': _jnp.float32}
MOMENT_SCALE = {'ffn1_norm': 6.187867e+00, 'ffn1_w_gate': 1.259016e-01, 'ffn1_w_up': 1.342533e-01, 'ffn1_w_down': 2.233405e-01, 'mix_norm': 7.969085e-01, 'w_in': 4.221392e-01, 'conv_w': 2.227878e+00, 'conv_b': 1.730526e+01, 'rg_w_a': 7.205715e-01, 'rg_b_a': 4.528127e-01, 'rg_w_x': 1.286706e+00, 'rg_b_x': 7.228329e-01, 'rg_lambda': 5.853905e-01, 'q_norm': 4.412976e-01, 'k_norm': 4.448132e-01, 'rnn_out_norm': 4.967979e+01, 'attn_out_norm': 3.199186e+01, 'w_out': 2.098822e+00, 'ffn2_norm': 6.153290e+00, 'ffn2_w_gate': 1.229172e-01, 'ffn2_w_up': 1.358153e-01, 'ffn2_w_down': 2.187454e-01}


def _to_microbatches(a, axis):
    t = _jnp.moveaxis(a, axis, 0)
    t = t.reshape((N_MICROBATCH, t.shape[0] // N_MICROBATCH) + t.shape[1:])
    return _jnp.moveaxis(t, 1, axis + 1)


def setup_inputs(seed: int = 0) -> dict:
    inp = _fwd_setup_inputs(seed)
    key = _jax.random.fold_in(_jax.random.key(seed), 7919)
    shape, _ = _output_shape()
    out = dict(inp)
    out["loss_target"] = _jax.random.normal(_jax.random.fold_in(key, 0), shape, _jnp.float32)
    for i, name in enumerate(TWIN_WEIGHTS):
        w = inp[name].astype(_jnp.float32)
        if MOMENT_SCALE is None:
            s = _jnp.sqrt(_jnp.mean(_jnp.square(w)) + 1e-30)
        else:
            s = MOMENT_SCALE[name]
        km, kv = _jax.random.split(_jax.random.fold_in(key, i + 1))
        out[name] = w
        out["m_" + name] = s * _jax.random.normal(km, w.shape, _jnp.float32)
        out["v_" + name] = (s * s) * _jax.random.uniform(kv, w.shape, _jnp.float32, 0.5, 1.5)
    if N_MICROBATCH > 1:
        for name, axis in PER_EXAMPLE_BATCH_AXIS.items():
            out[name] = _to_microbatches(out[name], axis)
    return {'x': out['x'], 'ffn1_norm': out['ffn1_norm'], 'ffn1_w_gate': out['ffn1_w_gate'], 'ffn1_w_up': out['ffn1_w_up'], 'ffn1_w_down': out['ffn1_w_down'], 'mix_norm': out['mix_norm'], 'w_in': out['w_in'], 'conv_w': out['conv_w'], 'conv_b': out['conv_b'], 'rg_w_a': out['rg_w_a'], 'rg_b_a': out['rg_b_a'], 'rg_w_x': out['rg_w_x'], 'rg_b_x': out['rg_b_x'], 'rg_lambda': out['rg_lambda'], 'q_norm': out['q_norm'], 'k_norm': out['k_norm'], 'rnn_out_norm': out['rnn_out_norm'], 'attn_out_norm': out['attn_out_norm'], 'w_out': out['w_out'], 'ffn2_norm': out['ffn2_norm'], 'ffn2_w_gate': out['ffn2_w_gate'], 'ffn2_w_up': out['ffn2_w_up'], 'ffn2_w_down': out['ffn2_w_down'], 'loss_target': out['loss_target'], 'm_ffn1_norm': out['m_ffn1_norm'], 'm_ffn1_w_gate': out['m_ffn1_w_gate'], 'm_ffn1_w_up': out['m_ffn1_w_up'], 'm_ffn1_w_down': out['m_ffn1_w_down'], 'm_mix_norm': out['m_mix_norm'], 'm_w_in': out['m_w_in'], 'm_conv_w': out['m_conv_w'], 'm_conv_b': out['m_conv_b'], 'm_rg_w_a': out['m_rg_w_a'], 'm_rg_b_a': out['m_rg_b_a'], 'm_rg_w_x': out['m_rg_w_x'], 'm_rg_b_x': out['m_rg_b_x'], 'm_rg_lambda': out['m_rg_lambda'], 'm_q_norm': out['m_q_norm'], 'm_k_norm': out['m_k_norm'], 'm_rnn_out_norm': out['m_rnn_out_norm'], 'm_attn_out_norm': out['m_attn_out_norm'], 'm_w_out': out['m_w_out'], 'm_ffn2_norm': out['m_ffn2_norm'], 'm_ffn2_w_gate': out['m_ffn2_w_gate'], 'm_ffn2_w_up': out['m_ffn2_w_up'], 'm_ffn2_w_down': out['m_ffn2_w_down'], 'v_ffn1_norm': out['v_ffn1_norm'], 'v_ffn1_w_gate': out['v_ffn1_w_gate'], 'v_ffn1_w_up': out['v_ffn1_w_up'], 'v_ffn1_w_down': out['v_ffn1_w_down'], 'v_mix_norm': out['v_mix_norm'], 'v_w_in': out['v_w_in'], 'v_conv_w': out['v_conv_w'], 'v_conv_b': out['v_conv_b'], 'v_rg_w_a': out['v_rg_w_a'], 'v_rg_b_a': out['v_rg_b_a'], 'v_rg_w_x': out['v_rg_w_x'], 'v_rg_b_x': out['v_rg_b_x'], 'v_rg_lambda': out['v_rg_lambda'], 'v_q_norm': out['v_q_norm'], 'v_k_norm': out['v_k_norm'], 'v_rnn_out_norm': out['v_rnn_out_norm'], 'v_attn_out_norm': out['v_attn_out_norm'], 'v_w_out': out['v_w_out'], 'v_ffn2_norm': out['v_ffn2_norm'], 'v_ffn2_w_gate': out['v_ffn2_w_gate'], 'v_ffn2_w_up': out['v_ffn2_w_up'], 'v_ffn2_w_down': out['v_ffn2_w_down']}


def _loss(weights, diff, rest, loss_target):
    with _jax.named_scope("forward"):
        args = {**rest, TWIN_DIFF_INPUT: diff, **{k: w.astype(_WEIGHT_DTYPES[k]) for k, w in weights.items()}}
        y = _forward(args)
    with _jax.named_scope("loss_head"):
        err = _jnp.square(y.astype(_jnp.float32) - loss_target)
        return 0.5 * _jnp.sum(_jnp.mean(err, axis=-1)) if err.ndim else 0.5 * err


def _adamw(w, g, m, v):
    m = ADAM_B1 * m + (1.0 - ADAM_B1) * g
    v = ADAM_B2 * v + (1.0 - ADAM_B2) * _jnp.square(g)
    m_hat = m / (1.0 - ADAM_B1 ** ADAM_STEP)
    v_hat = v / (1.0 - ADAM_B2 ** ADAM_STEP)
    delta = -ADAM_LR * (m_hat / (_jnp.sqrt(v_hat) + ADAM_EPS) + ADAM_WD * w)
    return delta, m, v


def reference(x, ffn1_norm, ffn1_w_gate, ffn1_w_up, ffn1_w_down, mix_norm, w_in, conv_w, conv_b, rg_w_a, rg_b_a, rg_w_x, rg_b_x, rg_lambda, q_norm, k_norm, rnn_out_norm, attn_out_norm, w_out, ffn2_norm, ffn2_w_gate, ffn2_w_up, ffn2_w_down, loss_target, m_ffn1_norm, m_ffn1_w_gate, m_ffn1_w_up, m_ffn1_w_down, m_mix_norm, m_w_in, m_conv_w, m_conv_b, m_rg_w_a, m_rg_b_a, m_rg_w_x, m_rg_b_x, m_rg_lambda, m_q_norm, m_k_norm, m_rnn_out_norm, m_attn_out_norm, m_w_out, m_ffn2_norm, m_ffn2_w_gate, m_ffn2_w_up, m_ffn2_w_down, v_ffn1_norm, v_ffn1_w_gate, v_ffn1_w_up, v_ffn1_w_down, v_mix_norm, v_w_in, v_conv_w, v_conv_b, v_rg_w_a, v_rg_b_a, v_rg_w_x, v_rg_b_x, v_rg_lambda, v_q_norm, v_k_norm, v_rnn_out_norm, v_attn_out_norm, v_w_out, v_ffn2_norm, v_ffn2_w_gate, v_ffn2_w_up, v_ffn2_w_down):
    given = dict(x=x, ffn1_norm=ffn1_norm, ffn1_w_gate=ffn1_w_gate, ffn1_w_up=ffn1_w_up, ffn1_w_down=ffn1_w_down, mix_norm=mix_norm, w_in=w_in, conv_w=conv_w, conv_b=conv_b, rg_w_a=rg_w_a, rg_b_a=rg_b_a, rg_w_x=rg_w_x, rg_b_x=rg_b_x, rg_lambda=rg_lambda, q_norm=q_norm, k_norm=k_norm, rnn_out_norm=rnn_out_norm, attn_out_norm=attn_out_norm, w_out=w_out, ffn2_norm=ffn2_norm, ffn2_w_gate=ffn2_w_gate, ffn2_w_up=ffn2_w_up, ffn2_w_down=ffn2_w_down, loss_target=loss_target, m_ffn1_norm=m_ffn1_norm, m_ffn1_w_gate=m_ffn1_w_gate, m_ffn1_w_up=m_ffn1_w_up, m_ffn1_w_down=m_ffn1_w_down, m_mix_norm=m_mix_norm, m_w_in=m_w_in, m_conv_w=m_conv_w, m_conv_b=m_conv_b, m_rg_w_a=m_rg_w_a, m_rg_b_a=m_rg_b_a, m_rg_w_x=m_rg_w_x, m_rg_b_x=m_rg_b_x, m_rg_lambda=m_rg_lambda, m_q_norm=m_q_norm, m_k_norm=m_k_norm, m_rnn_out_norm=m_rnn_out_norm, m_attn_out_norm=m_attn_out_norm, m_w_out=m_w_out, m_ffn2_norm=m_ffn2_norm, m_ffn2_w_gate=m_ffn2_w_gate, m_ffn2_w_up=m_ffn2_w_up, m_ffn2_w_down=m_ffn2_w_down, v_ffn1_norm=v_ffn1_norm, v_ffn1_w_gate=v_ffn1_w_gate, v_ffn1_w_up=v_ffn1_w_up, v_ffn1_w_down=v_ffn1_w_down, v_mix_norm=v_mix_norm, v_w_in=v_w_in, v_conv_w=v_conv_w, v_conv_b=v_conv_b, v_rg_w_a=v_rg_w_a, v_rg_b_a=v_rg_b_a, v_rg_w_x=v_rg_w_x, v_rg_b_x=v_rg_b_x, v_rg_lambda=v_rg_lambda, v_q_norm=v_q_norm, v_k_norm=v_k_norm, v_rnn_out_norm=v_rnn_out_norm, v_attn_out_norm=v_attn_out_norm, v_w_out=v_w_out, v_ffn2_norm=v_ffn2_norm, v_ffn2_w_gate=v_ffn2_w_gate, v_ffn2_w_up=v_ffn2_w_up, v_ffn2_w_down=v_ffn2_w_down)
    weights = {n: given[n] for n in TWIN_WEIGHTS}
    shared = {n: given[n] for n in SHARED_INPUTS}
    per_example = {n: given[n] for n in ['x']}
    grad_fn = _jax.value_and_grad(_loss, argnums=(0, 1))

    def one_microbatch(ex, loss_target):
        ex = dict(ex)
        diff = ex.pop(TWIN_DIFF_INPUT)
        return grad_fn(weights, diff, {**shared, **ex}, loss_target)

    if N_MICROBATCH == 1:
        loss, (grad_w, grad_x) = one_microbatch(per_example, given["loss_target"])
    else:
        def body(carry, xs):
            loss_sum, grad_sum = carry
            l_k, (gw_k, gx_k) = one_microbatch(xs[0], xs[1])
            with _jax.named_scope("update"):
                return (loss_sum + l_k, _jax.tree.map(_jnp.add, grad_sum, gw_k)), gx_k

        init = (_jnp.zeros((), _jnp.float32), _jax.tree.map(_jnp.zeros_like, weights))
        (loss, grad_w), grad_x = _jax.lax.scan(body, init, (per_example, given["loss_target"]))
    with _jax.named_scope("update"):
        delta_w, new_m, new_v = {}, {}, {}
        for n in TWIN_WEIGHTS:
            delta_w[n], new_m[n], new_v[n] = _adamw(weights[n], grad_w[n], given["m_" + n], given["v_" + n])
    return (loss, grad_x, *[grad_w[n] for n in TWIN_WEIGHTS], *[delta_w[n] for n in TWIN_WEIGHTS],
            *[new_m[n] for n in TWIN_WEIGHTS], *[new_v[n] for n in TWIN_WEIGHTS])
```

```python
import functools
import math

import jax
import jax.numpy as jnp
from jax import lax
from jax.experimental import pallas as pl
from jax.experimental.pallas import tpu as pltpu

F32 = jnp.float32
BF16 = jnp.bfloat16
MESH = pl.DeviceIdType.MESH

D_MODEL = 1024
N_CHIPS = 4
D_RNN = 512
D_ATT = 512
N_HEADS = 8
HEAD_DIM = 64
RNN_BLOCKS = 8
CONV_W = 4
RG_C = 8.0
N_IN = 2 * D_RNN + 3 * D_ATT
EPS = 1e-6
ATT_BLOCK = 128

ADAM_LR = 0.001
ADAM_B1 = 0.9
ADAM_B2 = 0.999
ADAM_EPS = 1e-08
ADAM_WD = 0.01
ADAM_STEP = 10

V7X_VMEM_LIMIT = 56 * 1024 * 1024
TOKEN_TILE = 512

GELU_K0 = math.sqrt(2.0 / math.pi)
GELU_K1 = 0.044715


def _params(sem=None):
    return pltpu.CompilerParams(dimension_semantics=sem, vmem_limit_bytes=V7X_VMEM_LIMIT)


def _dot(a, b):
    return jnp.dot(a, b, preferred_element_type=F32)


def _dot_nt(a, b):
    return lax.dot_general(a, b, (((1,), (1,)), ((), ())), preferred_element_type=F32)


def _dot_tn(a, b):
    return lax.dot_general(a, b, (((0,), (0,)), ((), ())), preferred_element_type=F32)


def _sigmoid(x):
    return 1.0 / (1.0 + jnp.exp(-x))


def _rms_r(xv):
    return lax.rsqrt(jnp.mean(xv * xv, axis=-1, keepdims=True) + EPS)


def _rms_bwd(xv, r, nw, dh):
    t = dh * nw
    dx = r * t - xv * (r * r * r * jnp.mean(t * xv, axis=-1, keepdims=True))
    dn = jnp.sum(dh * xv * r, axis=0, keepdims=True)
    return dx, dn


def _gelu(x):
    t = jnp.tanh(GELU_K0 * (x + GELU_K1 * x * x * x))
    return 0.5 * x * (1.0 + t)


def _gelu_grad(x):
    t = jnp.tanh(GELU_K0 * (x + GELU_K1 * x * x * x))
    return 0.5 * (1.0 + t) + 0.5 * x * (1.0 - t * t) * (GELU_K0 * (1.0 + 3.0 * GELU_K1 * x * x))


def _expm1_neg(x):
    p = 1.0 + x * (1.0 / 8.0)
    for k in (7.0, 6.0, 5.0, 4.0, 3.0, 2.0):
        p = 1.0 + x * (1.0 / k) * p
    return jnp.where(x > -0.25, x * p, jnp.exp(x) - 1.0)


def _log_sigmoid(x):
    return jnp.minimum(x, 0.0) - jnp.log(1.0 + jnp.exp(-jnp.abs(x)))


def _tile(s):
    return min(TOKEN_TILE, s)


def _ffn_fwd(x, nw, wg, wu, wd, tgt=None):
    s, d = x.shape
    nb, _, fb = wg.shape
    tm = _tile(s)
    with_loss = tgt is not None

    def body(*refs):
        if with_loss:
            x_ref, nw_ref, wg_ref, wu_ref, wd_ref, tgt_ref, out_ref, g_ref, u_ref, hb_ref, ab_ref, loss_ref, hs, acc = refs
        else:
            x_ref, nw_ref, wg_ref, wu_ref, wd_ref, out_ref, g_ref, u_ref, hb_ref, ab_ref, hs, acc = refs
        i = pl.program_id(0)
        j = pl.program_id(1)

        @pl.when(j == 0)
        def _():
            xv = x_ref[...]
            hb = (xv * _rms_r(xv) * nw_ref[...]).astype(BF16)
            hs[...] = hb
            hb_ref[...] = hb
            acc[...] = jnp.zeros_like(acc)

        hb = hs[...]
        g = _dot(hb, wg_ref[...])
        u = _dot(hb, wu_ref[...])
        g_ref[...] = g
        u_ref[...] = u
        ab = (g * _sigmoid(g) * u).astype(BF16)
        ab_ref[...] = ab
        acc[...] += _dot(ab, wd_ref[...])

        @pl.when(j == nb - 1)
        def _():
            y = x_ref[...] + 0.5 * acc[...]
            if with_loss:
                diff = y - tgt_ref[...]
                out_ref[...] = diff * (1.0 / d)

                @pl.when(i == 0)
                def _():
                    loss_ref[...] = jnp.zeros_like(loss_ref)

                loss_ref[...] += jnp.sum(diff * diff) * (0.5 / d)
            else:
                out_ref[...] = y

    row = pl.BlockSpec((tm, d), lambda i, j: (i, 0))
    in_specs = [row, pl.BlockSpec((1, d), lambda i, j: (0, 0)),
                pl.BlockSpec((None, d, fb), lambda i, j: (j, 0, 0)),
                pl.BlockSpec((None, d, fb), lambda i, j: (j, 0, 0)),
                pl.BlockSpec((None, fb, d), lambda i, j: (j, 0, 0))]
    args = [x, nw, wg, wu, wd]
    if with_loss:
        in_specs.append(row)
        args.append(tgt)
    blk = pl.BlockSpec((None, tm, fb), lambda i, j: (j, i, 0))
    out_shape = [jax.ShapeDtypeStruct((s, d), F32), jax.ShapeDtypeStruct((nb, s, fb), F32),
                 jax.ShapeDtypeStruct((nb, s, fb), F32), jax.ShapeDtypeStruct((s, d), BF16),
                 jax.ShapeDtypeStruct((nb, s, fb), BF16)]
    out_specs = [row, blk, blk, row, blk]
    if with_loss:
        out_shape.append(jax.ShapeDtypeStruct((1, 128), F32))
        out_specs.append(pl.BlockSpec((1, 128), lambda i, j: (0, 0)))
    return pl.pallas_call(
        body, name="ffn_fwd_loss" if with_loss else "ffn_fwd",
        grid=(s // tm, nb), in_specs=in_specs, out_specs=out_specs, out_shape=out_shape,
        scratch_shapes=[pltpu.VMEM((tm, d), BF16), pltpu.VMEM((tm, d), F32)],
        compiler_params=_params(("arbitrary", "arbitrary")),
    )(*args)


def _ffn_bwd_act(x, nw, dy, g, u, wg, wu, wd, name):
    s, d = x.shape
    nb, _, fb = wg.shape
    tm = _tile(s)

    def body(x_ref, nw_ref, dy_ref, g_ref, u_ref, wg_ref, wu_ref, wd_ref,
             dx_ref, dg_ref, du_ref, dyb_ref, dnw_ref, dys, acc):
        i = pl.program_id(0)
        j = pl.program_id(1)

        @pl.when(j == 0)
        def _():
            dyb = dy_ref[...].astype(BF16)
            dys[...] = dyb
            dyb_ref[...] = dyb
            acc[...] = jnp.zeros_like(acc)

        da = 0.5 * _dot_nt(dys[...], wd_ref[...])
        gv = g_ref[...]
        sg = _sigmoid(gv)
        dub = (da * (gv * sg)).astype(BF16)
        dgb = (da * u_ref[...] * (sg * (1.0 + gv * (1.0 - sg)))).astype(BF16)
        dg_ref[...] = dgb
        du_ref[...] = dub
        acc[...] += _dot_nt(dgb, wg_ref[...]) + _dot_nt(dub, wu_ref[...])

        @pl.when(j == nb - 1)
        def _():
            xv = x_ref[...]
            dx, dn = _rms_bwd(xv, _rms_r(xv), nw_ref[...], acc[...])
            dx_ref[...] = dy_ref[...] + dx

            @pl.when(i == 0)
            def _():
                dnw_ref[...] = jnp.zeros_like(dnw_ref)

            dnw_ref[...] += dn

    row = pl.BlockSpec((tm, d), lambda i, j: (i, 0))
    vec = pl.BlockSpec((1, d), lambda i, j: (0, 0))
    blk = pl.BlockSpec((None, tm, fb), lambda i, j: (j, i, 0))
    wcol = pl.BlockSpec((None, d, fb), lambda i, j: (j, 0, 0))
    wrow = pl.BlockSpec((None, fb, d), lambda i, j: (j, 0, 0))
    return pl.pallas_call(
        body, name=name, grid=(s // tm, nb),
        in_specs=[row, vec, row, blk, blk, wcol, wcol, wrow],
        out_specs=[row, blk, blk, row, vec],
        out_shape=[jax.ShapeDtypeStruct((s, d), F32), jax.ShapeDtypeStruct((nb, s, fb), BF16),
                   jax.ShapeDtypeStruct((nb, s, fb), BF16), jax.ShapeDtypeStruct((s, d), BF16),
                   jax.ShapeDtypeStruct((1, d), F32)],
        scratch_shapes=[pltpu.VMEM((tm, d), BF16), pltpu.VMEM((tm, d), F32)],
        compiler_params=_params(("arbitrary", "arbitrary")),
    )(x, nw, dy, g, u, wg, wu, wd)


def _wgrad(a, b, a_spec, b_spec, out_rows, out_cols, scale, name):
    s = a.shape[-2]
    tk = _tile(s)
    nk = s // tk

    def body(a_ref, b_ref, out_ref, acc):
        k = pl.program_id(1)

        @pl.when(k == 0)
        def _():
            acc[...] = jnp.zeros_like(acc)

        acc[...] += _dot_tn(a_ref[...], b_ref[...])

        @pl.when(k == nk - 1)
        def _():
            out_ref[...] = (acc[...] * scale).astype(BF16)

    return pl.pallas_call(
        body, name=name, grid=(N_CHIPS, nk),
        in_specs=[a_spec(tk), b_spec(tk)],
        out_specs=pl.BlockSpec((None, out_rows, out_cols), lambda j, k: (j, 0, 0)),
        out_shape=jax.ShapeDtypeStruct((N_CHIPS, out_rows, out_cols), BF16),
        scratch_shapes=[pltpu.VMEM((out_rows, out_cols), F32)],
        compiler_params=_params(("arbitrary", "arbitrary")),
    )(a, b)


def _ffn_wgrads(hb, ab, dg, du, dyb, tag):
    s, d = hb.shape
    fb = ab.shape[-1]
    shared = lambda cols: (lambda tk: pl.BlockSpec((tk, cols), lambda j, k: (k, 0)))
    stacked = lambda cols: (lambda tk: pl.BlockSpec((None, tk, cols), lambda j, k: (j, k, 0)))
    dwg = _wgrad(hb, dg, shared(d), stacked(fb), d, fb, 1.0, "wgrad_gate_" + tag)
    dwu = _wgrad(hb, du, shared(d), stacked(fb), d, fb, 1.0, "wgrad_up_" + tag)
    dwd = _wgrad(ab, dyb, stacked(fb), shared(d), fb, d, 0.5, "wgrad_down_" + tag)
    return dwg, dwu, dwd


def _mix_pre(x, nw, win):
    s, d = x.shape
    nb, _, cb = win.shape
    tm = _tile(s)

    def body(x_ref, nw_ref, w_ref, p_ref, hb_ref, hs):
        @pl.when(pl.program_id(1) == 0)
        def _():
            xv = x_ref[...]
            hb = (xv * _rms_r(xv) * nw_ref[...]).astype(BF16)
            hs[...] = hb
            hb_ref[...] = hb

        p_ref[...] = _dot(hs[...], w_ref[...])

    row = pl.BlockSpec((tm, d), lambda i, j: (i, 0))
    return pl.pallas_call(
        body, name="mix_pre", grid=(s // tm, nb),
        in_specs=[row, pl.BlockSpec((1, d), lambda i, j: (0, 0)),
                  pl.BlockSpec((None, d, cb), lambda i, j: (j, 0, 0))],
        out_specs=[pl.BlockSpec((tm, cb), lambda i, j: (i, j)), row],
        out_shape=[jax.ShapeDtypeStruct((s, nb * cb), F32), jax.ShapeDtypeStruct((s, d), BF16)],
        scratch_shapes=[pltpu.VMEM((tm, d), BF16)],
        compiler_params=_params(("arbitrary", "arbitrary")),
    )(x, nw, win)


def _mix_pre_bwd(x, nw, dres, dpb, win):
    s, d = x.shape
    nb, _, cb = win.shape
    tm = _tile(s)

    def body(x_ref, nw_ref, dres_ref, dp_ref, w_ref, dx_ref, dnw_ref, acc):
        i = pl.program_id(0)
        j = pl.program_id(1)

        @pl.when(j == 0)
        def _():
            acc[...] = jnp.zeros_like(acc)

        acc[...] += _dot_nt(dp_ref[...], w_ref[...])

        @pl.when(j == nb - 1)
        def _():
            xv = x_ref[...]
            dx, dn = _rms_bwd(xv, _rms_r(xv), nw_ref[...], acc[...])
            dx_ref[...] = dres_ref[...] + dx

            @pl.when(i == 0)
            def _():
                dnw_ref[...] = jnp.zeros_like(dnw_ref)

            dnw_ref[...] += dn

    row = pl.BlockSpec((tm, d), lambda i, j: (i, 0))
    vec = pl.BlockSpec((1, d), lambda i, j: (0, 0))
    return pl.pallas_call(
        body, name="mix_pre_bwd", grid=(s // tm, nb),
        in_specs=[row, vec, row, pl.BlockSpec((tm, cb), lambda i, j: (i, j)),
                  pl.BlockSpec((None, d, cb), lambda i, j: (j, 0, 0))],
        out_specs=[row, vec],
        out_shape=[jax.ShapeDtypeStruct((s, d), F32), jax.ShapeDtypeStruct((1, d), F32)],
        scratch_shapes=[pltpu.VMEM((tm, d), F32)],
        compiler_params=_params(("arbitrary", "arbitrary")),
    )(x, nw, dres, dpb, win)


def _mix_post(x, yr, ya, nr, na, wout):
    s, d = x.shape
    h = yr.shape[1]
    tm = _tile(s)

    def body(x_ref, yr_ref, ya_ref, nr_ref, na_ref, w_ref, out_ref):
        yrv = yr_ref[...]
        yav = ya_ref[...]
        onb = (yrv * _rms_r(yrv) * nr_ref[...]).astype(BF16)
        oab = (yav * _rms_r(yav) * na_ref[...]).astype(BF16)
        out_ref[...] = x_ref[...] + _dot(onb, w_ref[0:h, :]) + _dot(oab, w_ref[h:2 * h, :])

    row = pl.BlockSpec((tm, d), lambda i: (i, 0))
    half = pl.BlockSpec((tm, h), lambda i: (i, 0))
    vec = pl.BlockSpec((1, h), lambda i: (0, 0))
    return pl.pallas_call(
        body, name="mix_post", grid=(s // tm,),
        in_specs=[row, half, half, vec, vec, pl.BlockSpec((2 * h, d), lambda i: (0, 0))],
        out_specs=row, out_shape=jax.ShapeDtypeStruct((s, d), F32),
        compiler_params=_params(("arbitrary",)),
    )(x, yr, ya, nr, na, wout)


def _mix_post_bwd(dx, yr, ya, nr, na, wout):
    s, d = dx.shape
    h = yr.shape[1]
    tm = _tile(s)

    def body(dx_ref, yr_ref, ya_ref, nr_ref, na_ref, w_ref,
             dyr_ref, dya_ref, yc_ref, dxb_ref, dnr_ref, dna_ref):
        i = pl.program_id(0)
        dxb = dx_ref[...].astype(BF16)
        dxb_ref[...] = dxb
        dyc = _dot_nt(dxb, w_ref[...])
        yrv = yr_ref[...]
        yav = ya_ref[...]
        rr = _rms_r(yrv)
        ra = _rms_r(yav)
        yc_ref[:, 0:h] = (yrv * rr * nr_ref[...]).astype(BF16)
        yc_ref[:, h:2 * h] = (yav * ra * na_ref[...]).astype(BF16)
        dyr, dnr = _rms_bwd(yrv, rr, nr_ref[...], dyc[:, 0:h])
        dya, dna = _rms_bwd(yav, ra, na_ref[...], dyc[:, h:2 * h])
        dyr_ref[...] = dyr
        dya_ref[...] = dya

        @pl.when(i == 0)
        def _():
            dnr_ref[...] = jnp.zeros_like(dnr_ref)
            dna_ref[...] = jnp.zeros_like(dna_ref)

        dnr_ref[...] += dnr
        dna_ref[...] += dna

    row = pl.BlockSpec((tm, d), lambda i: (i, 0))
    half = pl.BlockSpec((tm, h), lambda i: (i, 0))
    vec = pl.BlockSpec((1, h), lambda i: (0, 0))
    return pl.pallas_call(
        body, name="mix_post_bwd", grid=(s // tm,),
        in_specs=[row, half, half, vec, vec, pl.BlockSpec((2 * h, d), lambda i: (0, 0))],
        out_specs=[half, half, pl.BlockSpec((tm, 2 * h), lambda i: (i, 0)), row, vec, vec],
        out_shape=[jax.ShapeDtypeStruct((s, h), F32), jax.ShapeDtypeStruct((s, h), F32),
                   jax.ShapeDtypeStruct((s, 2 * h), BF16), jax.ShapeDtypeStruct((s, d), BF16),
                   jax.ShapeDtypeStruct((1, h), F32), jax.ShapeDtypeStruct((1, h), F32)],
        compiler_params=_params(("arbitrary",)),
    )(dx, yr, ya, nr, na, wout)


def _shift_down(xv, s, prev8):
    rolled = pltpu.roll(xv, s, 0)
    row8 = lax.broadcasted_iota(jnp.int32, prev8.shape, 0)
    head = jnp.where(row8 < s, pltpu.roll(prev8, s, 0), rolled[0:8, :])
    return jnp.concatenate([head, rolled[8:, :]], axis=0)


def _shift_up(xv, s, next8):
    n = xv.shape[0]
    rolled = pltpu.roll(xv, n - s, 0)
    row8 = lax.broadcasted_iota(jnp.int32, next8.shape, 0)
    tail = jnp.where(row8 >= 8 - s, pltpu.roll(next8, 8 - s, 0), rolled[n - 8:, :])
    return jnp.concatenate([rolled[:n - 8, :], tail], axis=0)


def _scan_fwd(a, b):
    n = a.shape[0]
    row = lax.broadcasted_iota(jnp.int32, a.shape, 0)
    s = 1
    while s < n:
        ok = row >= s
        b = jnp.where(ok, a * pltpu.roll(b, s, 0) + b, b)
        a = jnp.where(ok, a * pltpu.roll(a, s, 0), a)
        s *= 2
    return b


def _scan_bwd(a, b):
    n = a.shape[0]
    row = lax.broadcasted_iota(jnp.int32, a.shape, 0)
    s = 1
    while s < n:
        ok = row < n - s
        b = jnp.where(ok, a * pltpu.roll(b, n - s, 0) + b, b)
        a = jnp.where(ok, a * pltpu.roll(a, n - s, 0), a)
        s *= 2
    return b


def _rglru_gates(xv, prev8, cw_ref, cb_ref, wa_ref, ba_ref, wx_ref, bx_ref, lam_ref):
    x1 = _shift_down(xv, 1, prev8)
    x2 = _shift_down(xv, 2, prev8)
    x3 = _shift_down(xv, 3, prev8)
    xc = cw_ref[3:4, :] * xv + cw_ref[2:3, :] * x1 + cw_ref[1:2, :] * x2 + cw_ref[0:1, :] * x3 + cb_ref[...]
    xcb = xc.astype(BF16)
    r = _sigmoid(_dot(xcb, wa_ref[...]) + ba_ref[...])
    ig = _sigmoid(_dot(xcb, wx_ref[...]) + bx_ref[...])
    c = RG_C * _log_sigmoid(lam_ref[...])
    la = r * c
    a = jnp.exp(la)
    m = jnp.sqrt(-_expm1_neg(2.0 * la))
    return (x1, x2, x3), xc, xcb, r, ig, c, a, m


def _rglru_fwd(proj, cw, cb, wa, ba, wx, bx, lam):
    s = proj.shape[0]
    w = D_RNN
    tm = _tile(s)

    def body(xr_ref, gate_ref, cw_ref, cb_ref, wa_ref, ba_ref, wx_ref, bx_ref, lam_ref,
             y_ref, h_ref, prev, hlast):
        @pl.when(pl.program_id(0) == 0)
        def _():
            prev[...] = jnp.zeros_like(prev)
            hlast[...] = jnp.zeros_like(hlast)

        xv = xr_ref[...]
        _, xc, _, _, ig, _, a, m = _rglru_gates(xv, prev[...], cw_ref, cb_ref, wa_ref, ba_ref,
                                                wx_ref, bx_ref, lam_ref)
        b = m * (ig * xc)
        row = lax.broadcasted_iota(jnp.int32, b.shape, 0)
        b = jnp.where(row == 0, b + a * hlast[...], b)
        h = _scan_fwd(a, b)
        h_ref[...] = h
        y_ref[...] = h * _gelu(gate_ref[...])
        prev[...] = xv[tm - 8:, :]
        hlast[...] = h[tm - 1:tm, :]

    vec = pl.BlockSpec((1, w), lambda i: (0, 0))
    sq = pl.BlockSpec((w, w), lambda i: (0, 0))
    out = pl.BlockSpec((tm, w), lambda i: (i, 0))
    return pl.pallas_call(
        body, name="rglru_fwd", grid=(s // tm,),
        in_specs=[pl.BlockSpec((tm, w), lambda i: (i, 0)), pl.BlockSpec((tm, w), lambda i: (i, 1)),
                  pl.BlockSpec((CONV_W, w), lambda i: (0, 0)), vec, sq, vec, sq, vec, vec],
        out_specs=[out, out],
        out_shape=[jax.ShapeDtypeStruct((s, w), F32), jax.ShapeDtypeStruct((s, w), F32)],
        scratch_shapes=[pltpu.VMEM((8, w), F32), pltpu.VMEM((1, w), F32)],
        compiler_params=_params(("arbitrary",)),
    )(proj, proj, cw, cb, wa, ba, wx, bx, lam)


def _rglru_bwd(proj, hseq, dyr, cw, cb, wa, ba, wx, bx, lam):
    s = proj.shape[0]
    w = D_RNN
    tm = _tile(s)
    nt = s // tm
    t8 = tm // 8

    def body(xr_ref, xp_ref, gate_ref, h_ref, hp_ref, dy_ref, cw_ref, cb_ref, wa_ref, ba_ref,
             wx_ref, bx_ref, lam_ref,
             dxr_ref, dgate_ref, dcw_ref, dcb_ref, dwa_ref, dba_ref, dwx_ref, dbx_ref, dlam_ref,
             carry, dxc_next):
        i = pl.program_id(0)
        first_tile = i == nt - 1

        @pl.when(i == 0)
        def _():
            carry[...] = jnp.zeros_like(carry)
            dxc_next[...] = jnp.zeros_like(dxc_next)
            for ref in (dcw_ref, dcb_ref, dwa_ref, dba_ref, dwx_ref, dbx_ref, dlam_ref):
                ref[...] = jnp.zeros_like(ref)

        xv = xr_ref[...]
        prev8 = jnp.where(first_tile, 0.0, xp_ref[...])
        hprev8 = jnp.where(first_tile, 0.0, hp_ref[...])
        (x1, x2, x3), xc, xcb, r, ig, c, a, m = _rglru_gates(
            xv, prev8, cw_ref, cb_ref, wa_ref, ba_ref, wx_ref, bx_ref, lam_ref)
        gv = gate_ref[...]
        hv = h_ref[...]
        dy = dy_ref[...]
        dgate_ref[...] = (dy * hv * _gelu_grad(gv)).astype(BF16)
        dh = dy * _gelu(gv)
        row = lax.broadcasted_iota(jnp.int32, dh.shape, 0)
        dh = jnp.where(row == tm - 1, dh + carry[...], dh)
        a_up = jnp.where(row == tm - 1, 0.0, pltpu.roll(a, tm - 1, 0))
        lam_t = _scan_bwd(a_up, dh)
        carry[...] = a[0:1, :] * lam_t[0:1, :]
        hm1 = _shift_down(hv, 1, hprev8)
        da = lam_t * hm1
        ixc = ig * xc
        dm = lam_t * ixc
        dig = lam_t * m * xc
        dxc = lam_t * m * ig
        dla = da * a - dm * (a * a) / m
        dr = dla * c
        dlam_ref[...] += jnp.sum(dla * r, axis=0, keepdims=True)
        dpa = dr * r * (1.0 - r)
        dpi = dig * ig * (1.0 - ig)
        dba_ref[...] += jnp.sum(dpa, axis=0, keepdims=True)
        dbx_ref[...] += jnp.sum(dpi, axis=0, keepdims=True)
        dpab = dpa.astype(BF16)
        dpib = dpi.astype(BF16)
        dwa_ref[...] += _dot_tn(xcb, dpab)
        dwx_ref[...] += _dot_tn(xcb, dpib)
        dxc = dxc + _dot_nt(dpab, wa_ref[...]) + _dot_nt(dpib, wx_ref[...])
        dcb_ref[...] += jnp.sum(dxc, axis=0, keepdims=True)
        dcw_ref[3:4, :] += jnp.sum(dxc * xv, axis=0, keepdims=True)
        dcw_ref[2:3, :] += jnp.sum(dxc * x1, axis=0, keepdims=True)
        dcw_ref[1:2, :] += jnp.sum(dxc * x2, axis=0, keepdims=True)
        dcw_ref[0:1, :] += jnp.sum(dxc * x3, axis=0, keepdims=True)
        nxt = dxc_next[...]
        dxr = (cw_ref[3:4, :] * dxc + cw_ref[2:3, :] * _shift_up(dxc, 1, nxt)
               + cw_ref[1:2, :] * _shift_up(dxc, 2, nxt) + cw_ref[0:1, :] * _shift_up(dxc, 3, nxt))
        dxr_ref[...] = dxr.astype(BF16)
        dxc_next[...] = dxc[0:8, :]

        @pl.when(first_tile)
        def _():
            lv = lam_ref[...]
            dlam_ref[...] = dlam_ref[...] * (RG_C * _sigmoid(-lv))

    rev = lambda i: nt - 1 - i
    vec = pl.BlockSpec((1, w), lambda i: (0, 0))
    sq = pl.BlockSpec((w, w), lambda i: (0, 0))
    cur = lambda col: pl.BlockSpec((tm, w), lambda i: (rev(i), col))
    before = lambda cols: pl.BlockSpec((8, w), lambda i: (jnp.maximum(rev(i) * t8 - 1, 0), 0))
    return pl.pallas_call(
        body, name="rglru_bwd", grid=(nt,),
        in_specs=[cur(0), before(None), cur(1), cur(0), before(None), cur(0),
                  pl.BlockSpec((CONV_W, w), lambda i: (0, 0)), vec, sq, vec, sq, vec, vec],
        out_specs=[cur(0), cur(0), pl.BlockSpec((CONV_W, w), lambda i: (0, 0)), vec, sq, vec, sq, vec, vec],
        out_shape=[jax.ShapeDtypeStruct((s, w), BF16), jax.ShapeDtypeStruct((s, w), BF16),
                   jax.ShapeDtypeStruct((CONV_W, w), F32), jax.ShapeDtypeStruct((1, w), F32),
                   jax.ShapeDtypeStruct((w, w), F32), jax.ShapeDtypeStruct((1, w), F32),
                   jax.ShapeDtypeStruct((w, w), F32), jax.ShapeDtypeStruct((1, w), F32),
                   jax.ShapeDtypeStruct((1, w), F32)],
        scratch_shapes=[pltpu.VMEM((1, w), F32), pltpu.VMEM((8, w), F32)],
        compiler_params=_params(("arbitrary",)),
    )(proj, proj, proj, hseq, hseq, dyr, cw, cb, wa, ba, wx, bx, lam)


def _qk_prep(q_ref, k_ref, v_ref, qg_ref, kg_ref, qn, kn, vb, scale):
    qv = q_ref[...]
    qn[...] = (qv * _rms_r(qv) * qg_ref[...] * scale).astype(BF16)
    kv = k_ref[...]
    kn[...] = (kv * _rms_r(kv) * kg_ref[...]).astype(BF16)
    vb[...] = v_ref[...].astype(BF16)


def _sb_logs(z, valid):
    l1p = jnp.log(1.0 + jnp.exp(-jnp.abs(z)))
    lb = jnp.minimum(z, 0.0) - l1p
    lm = jnp.where(valid, -jnp.maximum(z, 0.0) - l1p, 0.0)
    return lb, lm


def _split_dot(xv, tri):
    hi = xv.astype(BF16)
    lo = (xv - hi.astype(F32)).astype(BF16)
    return _dot(hi, tri) + _dot(lo, tri)


def _attn_fwd(q, k, v, qg, kg):
    nh, s, dh = q.shape
    blk = ATT_BLOCK
    nq = s // blk
    scale = 1.0 / math.sqrt(dh)

    def body(q_ref, k_ref, v_ref, qg_ref, kg_ref, o_ref, qn, kn, vb):
        _qk_prep(q_ref, k_ref, v_ref, qg_ref, kg_ref, qn, kn, vb, scale)
        row = lax.broadcasted_iota(jnp.int32, (blk, blk), 0)
        col = lax.broadcasted_iota(jnp.int32, (blk, blk), 1)
        later = jnp.where(row > col, 1.0, 0.0).astype(BF16)

        def q_step(qi, _):
            qoff = pl.multiple_of(qi * blk, blk)
            qt = qn[pl.ds(qoff, blk), :]

            def k_step(it, carry):
                acc, run = carry
                koff = pl.multiple_of((qi - it) * blk, blk)
                kt = kn[pl.ds(koff, blk), :]
                vt = vb[pl.ds(koff, blk), :]
                z = _dot_nt(qt, kt)
                valid = col < row + jnp.where(it > 0, blk, 0)
                lb, lm = _sb_logs(z, valid)
                tail = _split_dot(lm, later) + run
                wgt = jnp.where(valid, jnp.exp(lb + tail), 0.0)
                acc = acc + _dot(wgt.astype(BF16), vt)
                run = run + jnp.sum(lm, axis=1, keepdims=True)
                return acc, run

            acc, _ = lax.fori_loop(0, qi + 1, k_step,
                                   (jnp.zeros((blk, dh), F32), jnp.zeros((blk, 1), F32)))
            o_ref[pl.ds(qoff, blk), :] = acc
            return 0

        lax.fori_loop(0, nq, q_step, 0)

    head = pl.BlockSpec((None, s, dh), lambda h: (h, 0, 0))
    vec = pl.BlockSpec((1, dh), lambda h: (0, 0))
    return pl.pallas_call(
        body, name="attn_fwd", grid=(nh,),
        in_specs=[head, head, head, vec, vec], out_specs=head,
        out_shape=jax.ShapeDtypeStruct((nh, s, dh), F32),
        scratch_shapes=[pltpu.VMEM((s, dh), BF16)] * 3,
        compiler_params=_params(("arbitrary",)),
    )(q, k, v, qg, kg)


def _attn_bwd(q, k, v, do, qg, kg):
    nh, s, dh = q.shape
    blk = ATT_BLOCK
    nq = s // blk
    scale = 1.0 / math.sqrt(dh)

    def body(q_ref, k_ref, v_ref, do_ref, qg_ref, kg_ref,
             dq_ref, dk_ref, dv_ref, dqg_ref, dkg_ref, qn, kn, vb, dob, runs, dqn, dkn):
        _qk_prep(q_ref, k_ref, v_ref, qg_ref, kg_ref, qn, kn, vb, scale)
        dob[...] = do_ref[...].astype(BF16)
        dkn[...] = jnp.zeros_like(dkn)
        dv_ref[...] = jnp.zeros_like(dv_ref)
        row = lax.broadcasted_iota(jnp.int32, (blk, blk), 0)
        col = lax.broadcasted_iota(jnp.int32, (blk, blk), 1)
        later = jnp.where(row > col, 1.0, 0.0).astype(BF16)
        earlier = jnp.where(row < col, 1.0, 0.0).astype(BF16)

        def q_step(qi, _):
            qoff = pl.multiple_of(qi * blk, blk)
            qt = qn[pl.ds(qoff, blk), :]
            dot = dob[pl.ds(qoff, blk), :]

            def run_step(it, run):
                kj = qi - it
                koff = pl.multiple_of(kj * blk, blk)
                runs[kj] = jnp.broadcast_to(run, (blk, blk))
                z = _dot_nt(qt, kn[pl.ds(koff, blk), :])
                valid = col < row + jnp.where(it > 0, blk, 0)
                _, lm = _sb_logs(z, valid)
                return run + jnp.sum(lm, axis=1, keepdims=True)

            lax.fori_loop(0, qi + 1, run_step, jnp.zeros((blk, 1), F32))

            def k_step(kj, carry):
                dq_acc, esum = carry
                koff = pl.multiple_of(kj * blk, blk)
                kt = kn[pl.ds(koff, blk), :]
                vt = vb[pl.ds(koff, blk), :]
                z = _dot_nt(qt, kt)
                valid = col < row + jnp.where(kj < qi, blk, 0)
                lb, lm = _sb_logs(z, valid)
                tail = _split_dot(lm, later) + runs[kj]
                wgt = jnp.where(valid, jnp.exp(lb + tail), 0.0)
                e = _dot_nt(dot, vt) * wgt
                before = _split_dot(e, earlier) + esum
                beta = jnp.exp(lb)
                dz = jnp.where(valid, e * (1.0 - beta) - before * beta, 0.0)
                dzb = dz.astype(BF16)
                dq_acc = dq_acc + _dot(dzb, kt)
                dkn[pl.ds(koff, blk), :] += _dot_tn(dzb, qt)
                dv_ref[pl.ds(koff, blk), :] += _dot_tn(wgt.astype(BF16), dot)
                return dq_acc, esum + jnp.sum(e, axis=1, keepdims=True)

            dq_acc, _ = lax.fori_loop(0, qi + 1, k_step,
                                      (jnp.zeros((blk, dh), F32), jnp.zeros((blk, 1), F32)))
            dqn[pl.ds(qoff, blk), :] = dq_acc
            return 0

        lax.fori_loop(0, nq, q_step, 0)

        @pl.when(pl.program_id(0) == 0)
        def _():
            dqg_ref[...] = jnp.zeros_like(dqg_ref)
            dkg_ref[...] = jnp.zeros_like(dkg_ref)

        qv = q_ref[...]
        dq, dqg = _rms_bwd(qv, _rms_r(qv), qg_ref[...] * scale, dqn[...])
        dq_ref[...] = dq
        dqg_ref[...] += dqg * scale
        kv = k_ref[...]
        dk, dkg = _rms_bwd(kv, _rms_r(kv), kg_ref[...], dkn[...])
        dk_ref[...] = dk
        dkg_ref[...] += dkg

    head = pl.BlockSpec((None, s, dh), lambda h: (h, 0, 0))
    vec = pl.BlockSpec((1, dh), lambda h: (0, 0))
    return pl.pallas_call(
        body, name="attn_bwd", grid=(nh,),
        in_specs=[head, head, head, head, vec, vec], out_specs=[head, head, head, vec, vec],
        out_shape=[jax.ShapeDtypeStruct((nh, s, dh), F32)] * 3 + [jax.ShapeDtypeStruct((1, dh), F32)] * 2,
        scratch_shapes=[pltpu.VMEM((s, dh), BF16)] * 4 + [pltpu.VMEM((nq, blk, blk), F32)]
        + [pltpu.VMEM((s, dh), F32)] * 2,
        compiler_params=_params(("arbitrary",)),
    )(q, k, v, do, qg, kg)


def _block_diag(w):
    n, c, d = w.shape
    return jnp.einsum("ncd,nm->ncmd", w, jnp.eye(n, dtype=w.dtype)).reshape(n * c, n * d)


def _diag_blocks(full, n):
    c = full.shape[0] // n
    return jnp.stack([full[i * c:(i + 1) * c, i * c:(i + 1) * c] for i in range(n)])


def _to_heads(t):
    s = t.shape[0]
    return jnp.transpose(t.reshape(s, N_HEADS, HEAD_DIM), (1, 0, 2))


def _from_heads(t):
    s = t.shape[1]
    return jnp.transpose(t, (1, 0, 2)).reshape(s, N_HEADS * HEAD_DIM)


def _local_step(x, tgt, big, small):
    wa = _block_diag(small["rg_w_a"]).astype(BF16)
    wx = _block_diag(small["rg_w_x"]).astype(BF16)
    wout = big["w_out"].reshape(D_MODEL, D_MODEL)
    rg = (small["conv_w"], small["conv_b"], wa, small["rg_b_a"], wx, small["rg_b_x"], small["rg_lambda"])

    x1, g1, u1, hb1, ab1 = _ffn_fwd(x, small["ffn1_norm"], big["ffn1_w_gate"], big["ffn1_w_up"], big["ffn1_w_down"])
    proj, hb2 = _mix_pre(x1, small["mix_norm"], big["w_in"])
    yr, hseq = _rglru_fwd(proj, *rg)
    qh = _to_heads(proj[:, 2 * D_RNN:2 * D_RNN + D_ATT])
    kh = _to_heads(proj[:, 2 * D_RNN + D_ATT:2 * D_RNN + 2 * D_ATT])
    vh = _to_heads(proj[:, 2 * D_RNN + 2 * D_ATT:])
    ya = _from_heads(_attn_fwd(qh, kh, vh, small["q_norm"], small["k_norm"]))
    x2 = _mix_post(x1, yr, ya, small["rnn_out_norm"], small["attn_out_norm"], wout)
    dx3, g2, u2, hb3, ab3, loss = _ffn_fwd(x2, small["ffn2_norm"], big["ffn2_w_gate"], big["ffn2_w_up"],
                                          big["ffn2_w_down"], tgt)

    gb, gs = {}, {}
    dx2, dg2, du2, dyb2, gs["ffn2_norm"] = _ffn_bwd_act(
        x2, small["ffn2_norm"], dx3, g2, u2, big["ffn2_w_gate"], big["ffn2_w_up"], big["ffn2_w_down"], "ffn2_bwd")
    gb["ffn2_w_gate"], gb["ffn2_w_up"], gb["ffn2_w_down"] = _ffn_wgrads(hb3, ab3, dg2, du2, dyb2, "ffn2")

    dyr, dya, ycat, dxb2, gs["rnn_out_norm"], gs["attn_out_norm"] = _mix_post_bwd(
        dx2, yr, ya, small["rnn_out_norm"], small["attn_out_norm"], wout)
    quarter = D_MODEL // N_CHIPS
    gb["w_out"] = _wgrad(ycat, dxb2, lambda tk: pl.BlockSpec((tk, quarter), lambda j, k: (k, j)),
                         lambda tk: pl.BlockSpec((tk, D_MODEL), lambda j, k: (k, 0)),
                         quarter, D_MODEL, 1.0, "wgrad_out")
    dqh, dkh, dvh, gs["q_norm"], gs["k_norm"] = _attn_bwd(qh, kh, vh, _to_heads(dya), small["q_norm"], small["k_norm"])
    dxr, dgate, gs["conv_w"], gs["conv_b"], dwa, gs["rg_b_a"], dwx, gs["rg_b_x"], gs["rg_lambda"] = _rglru_bwd(
        proj, hseq, dyr, *rg)
    gs["rg_w_a"] = _diag_blocks(dwa, RNN_BLOCKS)
    gs["rg_w_x"] = _diag_blocks(dwx, RNN_BLOCKS)
    dpb = jnp.concatenate([dxr, dgate, _from_heads(dqh).astype(BF16), _from_heads(dkh).astype(BF16),
                           _from_heads(dvh).astype(BF16)], axis=1)
    cb = N_IN // N_CHIPS
    gb["w_in"] = _wgrad(hb2, dpb, lambda tk: pl.BlockSpec((tk, D_MODEL), lambda j, k: (k, 0)),
                        lambda tk: pl.BlockSpec((tk, cb), lambda j, k: (k, j)),
                        D_MODEL, cb, 1.0, "wgrad_in")
    dx1, gs["mix_norm"] = _mix_pre_bwd(x1, small["mix_norm"], dx2, dpb, big["w_in"])

    dx0, dg1, du1, dyb1, gs["ffn1_norm"] = _ffn_bwd_act(
        x, small["ffn1_norm"], dx1, g1, u1, big["ffn1_w_gate"], big["ffn1_w_up"], big["ffn1_w_down"], "ffn1_bwd")
    gb["ffn1_w_gate"], gb["ffn1_w_up"], gb["ffn1_w_down"] = _ffn_wgrads(hb1, ab1, dg1, du1, dyb1, "ffn1")
    return loss[0, 0], dx0, gb, gs


ANY = pl.BlockSpec(memory_space=pl.ANY)


def _place():
    x, y, c = lax.axis_index("x"), lax.axis_index("y"), lax.axis_index("c")
    other_chips = [(1 - x, y), (x, 1 - y), (1 - x, 1 - y)]
    return x, y, c, 2 * x + y, other_chips


def _remote(src, dst, send_sem, recv_sem, to):
    return pltpu.make_async_remote_copy(src_ref=src, dst_ref=dst, send_sem=send_sem, recv_sem=recv_sem,
                                        device_id=to, device_id_type=MESH)


def _half(rows, c):
    return pl.ds(pl.multiple_of(c * rows, 16), rows)


def _gather_weights(split, whole):
    arrs = list(split) + list(whole)
    n, ns = len(arrs), len(split)

    def body(*refs):
        ins, outs = refs[:n], refs[n:2 * n]
        send_sems, recv_sems, fsend_sems, frecv_sems, loc_sems = refs[2 * n:]
        x, y, c, me, chips = _place()
        sibling = (x, y, 1 - c)

        def region(i, chip, half):
            if i < ns:
                return outs[i].at[chip, _half(arrs[i].shape[0] // 2, half)]
            return outs[i].at[chip]

        local = [pltpu.make_async_copy(ins[i], outs[i].at[me], loc_sems.at[i]) for i in range(n)]
        for cp in local:
            cp.start()
        sends = []
        for i in range(n):
            src = ins[i].at[_half(arrs[i].shape[0] // 2, c)] if i < ns else ins[i]
            for p, chip in enumerate(chips):
                k = 3 * i + p
                sends.append(_remote(src, region(i, me, c), send_sems.at[k], recv_sems.at[k], (*chip, c)))
        for cp in sends:
            cp.start()
        passed = []
        for i in range(n):
            for p, (cx, cy) in enumerate(chips):
                k = 3 * i + p
                got = region(i, 2 * cx + cy, c)
                _remote(got, got, send_sems.at[k], recv_sems.at[k], (cx, cy, c)).wait_recv()
                if i < ns:
                    fwd = _remote(got, got, fsend_sems.at[k], frecv_sems.at[k], sibling)
                    fwd.start()
                    passed.append(fwd)
        for i in range(ns):
            for p, (cx, cy) in enumerate(chips):
                k = 3 * i + p
                got = region(i, 2 * cx + cy, 1 - c)
                _remote(got, got, fsend_sems.at[k], frecv_sems.at[k], sibling).wait_recv()
        for cp in sends + passed:
            cp.wait_send()
        for cp in local:
            cp.wait()

    return pl.pallas_call(
        body, name="gather_weights",
        in_specs=[ANY] * n, out_specs=[ANY] * n,
        out_shape=[jax.ShapeDtypeStruct((N_CHIPS,) + a.shape, a.dtype) for a in arrs],
        scratch_shapes=[pltpu.SemaphoreType.DMA((3 * n,)), pltpu.SemaphoreType.DMA((3 * n,)),
                        pltpu.SemaphoreType.DMA((3 * ns,)), pltpu.SemaphoreType.DMA((3 * ns,)),
                        pltpu.SemaphoreType.DMA((n,))],
    )(*arrs)


def _pair_exchange(grads):
    n = len(grads)

    def body(*refs):
        ins, theirs, mine = refs[:n], refs[n:2 * n], refs[2 * n:3 * n]
        send_sems, recv_sems, loc_sems = refs[3 * n:]
        x, y, c, _, _ = _place()
        sibling = (x, y, 1 - c)
        local, sends = [], []
        for k in range(n):
            rows = grads[k].shape[1] // 2
            local.append(pltpu.make_async_copy(ins[k].at[:, _half(rows, c)], mine[k], loc_sems.at[k]))
            sends.append(_remote(ins[k].at[:, _half(rows, 1 - c)], theirs[k], send_sems.at[k], recv_sems.at[k], sibling))
        for cp in local + sends:
            cp.start()
        for k in range(n):
            _remote(theirs[k], theirs[k], send_sems.at[k], recv_sems.at[k], sibling).wait_recv()
        for cp in sends:
            cp.wait_send()
        for cp in local:
            cp.wait()

    halves = [jax.ShapeDtypeStruct((g.shape[0], g.shape[1] // 2, g.shape[2]), g.dtype) for g in grads]
    outs = pl.pallas_call(
        body, name="grad_pair_exchange",
        in_specs=[ANY] * n, out_specs=[ANY] * (2 * n), out_shape=halves + halves,
        scratch_shapes=[pltpu.SemaphoreType.DMA((n,))] * 3,
    )(*grads)
    return outs[:n], outs[n:]


def _chip_exchange(sums):
    n = len(sums)

    def body(*refs):
        ins, outs = refs[:n], refs[n:2 * n]
        send_sems, recv_sems, loc_sems = refs[2 * n:]
        x, y, c, me, chips = _place()
        local = [pltpu.make_async_copy(ins[k].at[me], outs[k].at[me], loc_sems.at[k]) for k in range(n)]
        sends = []
        for k in range(n):
            for p, (cx, cy) in enumerate(chips):
                j = 3 * k + p
                sends.append(_remote(ins[k].at[2 * cx + cy], outs[k].at[me], send_sems.at[j], recv_sems.at[j], (cx, cy, c)))
        for cp in local + sends:
            cp.start()
        for k in range(n):
            for p, (cx, cy) in enumerate(chips):
                j = 3 * k + p
                got = outs[k].at[2 * cx + cy]
                _remote(got, got, send_sems.at[j], recv_sems.at[j], (cx, cy, c)).wait_recv()
        for cp in sends:
            cp.wait_send()
        for cp in local:
            cp.wait()

    return pl.pallas_call(
        body, name="grad_chip_exchange",
        in_specs=[ANY] * n, out_specs=[ANY] * n,
        out_shape=[jax.ShapeDtypeStruct(a.shape, a.dtype) for a in sums],
        scratch_shapes=[pltpu.SemaphoreType.DMA((3 * n,)), pltpu.SemaphoreType.DMA((3 * n,)),
                        pltpu.SemaphoreType.DMA((n,))],
    )(*sums)


def _half_swap(totals):
    n = len(totals)

    def body(*refs):
        ins, outs = refs[:n], refs[n:2 * n]
        send_sems, recv_sems, loc_sems = refs[2 * n:]
        x, y, c, _, _ = _place()
        sibling = (x, y, 1 - c)
        local = [pltpu.make_async_copy(ins[k], outs[k].at[c], loc_sems.at[k]) for k in range(n)]
        sends = [_remote(ins[k], outs[k].at[c], send_sems.at[k], recv_sems.at[k], sibling) for k in range(n)]
        for cp in local + sends:
            cp.start()
        for k in range(n):
            got = outs[k].at[1 - c]
            _remote(got, got, send_sems.at[k], recv_sems.at[k], sibling).wait_recv()
        for cp in sends:
            cp.wait_send()
        for cp in local:
            cp.wait()

    return pl.pallas_call(
        body, name="grad_half_swap",
        in_specs=[ANY] * n, out_specs=[ANY] * n,
        out_shape=[jax.ShapeDtypeStruct((2,) + a.shape, a.dtype) for a in totals],
        scratch_shapes=[pltpu.SemaphoreType.DMA((n,))] * 3,
    )(*totals)


def _gather_small(packed):
    n_dev = 8

    def body(in_ref, out_ref, send_sems, recv_sems, loc_sem):
        x, y, c, _, _ = _place()
        me = 4 * x + 2 * y + c
        local = pltpu.make_async_copy(in_ref, out_ref.at[me], loc_sem)
        local.start()
        peers = []
        for k in range(1, n_dev):
            fx, fy, fc = (k >> 2) & 1, (k >> 1) & 1, k & 1
            peers.append((x ^ fx, y ^ fy, c ^ fc))
        sends = [_remote(in_ref, out_ref.at[me], send_sems.at[k], recv_sems.at[k], peer)
                 for k, peer in enumerate(peers)]
        for cp in sends:
            cp.start()
        for k, (px, py, pc) in enumerate(peers):
            got = out_ref.at[4 * px + 2 * py + pc]
            _remote(got, got, send_sems.at[k], recv_sems.at[k], (px, py, pc)).wait_recv()
        for cp in sends:
            cp.wait_send()
        local.wait()

    return pl.pallas_call(
        body, name="gather_small_grads",
        in_specs=[ANY], out_specs=ANY,
        out_shape=jax.ShapeDtypeStruct((n_dev,) + packed.shape, packed.dtype),
        scratch_shapes=[pltpu.SemaphoreType.DMA((n_dev - 1,)), pltpu.SemaphoreType.DMA((n_dev - 1,)),
                        pltpu.SemaphoreType.DMA],
    )(packed)


def _row_tile(r):
    return r // 4 if r >= 256 and (r // 4) % 16 == 0 else r


def _pair_sum(a, b, name):
    nb, r, c = a.shape

    def body(a_ref, b_ref, out_ref):
        out_ref[...] = (a_ref[...].astype(F32) + b_ref[...].astype(F32)).astype(BF16)

    blk = pl.BlockSpec((None, r, c), lambda j: (j, 0, 0))
    return pl.pallas_call(
        body, name=name, grid=(nb,), in_specs=[blk, blk], out_specs=blk,
        out_shape=jax.ShapeDtypeStruct(a.shape, BF16), compiler_params=_params(("arbitrary",)),
    )(a, b)


def _slot_sum(a, name):
    nb, r, c = a.shape
    tr = _row_tile(r)

    def body(a_ref, out_ref):
        total = a_ref[0].astype(F32)
        for j in range(1, nb):
            total = total + a_ref[j].astype(F32)
        out_ref[...] = total

    return pl.pallas_call(
        body, name=name, grid=(r // tr,),
        in_specs=[pl.BlockSpec((nb, tr, c), lambda i: (0, i, 0))],
        out_specs=pl.BlockSpec((tr, c), lambda i: (i, 0)),
        out_shape=jax.ShapeDtypeStruct((r, c), F32), compiler_params=_params(("arbitrary",)),
    )(a)


def _adamw(w, g, m, v, name):
    r, c = w.shape
    tr = _row_tile(r)
    c1 = 1.0 - ADAM_B1 ** ADAM_STEP
    c2 = 1.0 - ADAM_B2 ** ADAM_STEP

    def body(w_ref, g_ref, m_ref, v_ref, d_ref, m2_ref, v2_ref):
        gv = g_ref[...]
        m2 = ADAM_B1 * m_ref[...] + (1.0 - ADAM_B1) * gv
        v2 = ADAM_B2 * v_ref[...] + (1.0 - ADAM_B2) * (gv * gv)
        m2_ref[...] = m2
        v2_ref[...] = v2
        d_ref[...] = -ADAM_LR * ((m2 / c1) / (jnp.sqrt(v2 / c2) + ADAM_EPS) + ADAM_WD * w_ref[...])

    blk = pl.BlockSpec((tr, c), lambda i: (i, 0))
    return pl.pallas_call(
        body, name=name, grid=(r // tr,), in_specs=[blk] * 4, out_specs=[blk] * 3,
        out_shape=[jax.ShapeDtypeStruct((r, c), F32)] * 3, compiler_params=_params(("arbitrary",)),
    )(w, g, m, v)


WEIGHTS = ["ffn1_norm", "ffn1_w_gate", "ffn1_w_up", "ffn1_w_down", "mix_norm", "w_in", "conv_w", "conv_b",
           "rg_w_a", "rg_b_a", "rg_w_x", "rg_b_x", "rg_lambda", "q_norm", "k_norm", "rnn_out_norm",
           "attn_out_norm", "w_out", "ffn2_norm", "ffn2_w_gate", "ffn2_w_up", "ffn2_w_down"]
BIG = ["ffn1_w_gate", "ffn1_w_up", "ffn1_w_down", "w_in", "w_out", "ffn2_w_gate", "ffn2_w_up", "ffn2_w_down"]
SMALL = [n for n in WEIGHTS if n not in BIG]
PACK_LANES = 128
PACK_ROW_ALIGN = 8


def _pack(parts):
    flat = jnp.concatenate([p.reshape(-1) for p in parts])
    unit = PACK_LANES * PACK_ROW_ALIGN
    padded = -(-flat.shape[0] // unit) * unit
    return jnp.pad(flat, (0, padded - flat.shape[0])).reshape(-1, PACK_LANES)


def _unpack(packed, shapes):
    flat = packed.reshape(-1)
    out, at = [], 0
    for shp in shapes:
        size = math.prod(shp)
        out.append(flat[at:at + size].reshape(shp))
        at += size
    return out


def kernel(x, ffn1_norm, ffn1_w_gate, ffn1_w_up, ffn1_w_down, mix_norm, w_in, conv_w, conv_b, rg_w_a, rg_b_a, rg_w_x, rg_b_x, rg_lambda, q_norm, k_norm, rnn_out_norm, attn_out_norm, w_out, ffn2_norm, ffn2_w_gate, ffn2_w_up, ffn2_w_down, loss_target, m_ffn1_norm, m_ffn1_w_gate, m_ffn1_w_up, m_ffn1_w_down, m_mix_norm, m_w_in, m_conv_w, m_conv_b, m_rg_w_a, m_rg_b_a, m_rg_w_x, m_rg_b_x, m_rg_lambda, m_q_norm, m_k_norm, m_rnn_out_norm, m_attn_out_norm, m_w_out, m_ffn2_norm, m_ffn2_w_gate, m_ffn2_w_up, m_ffn2_w_down, v_ffn1_norm, v_ffn1_w_gate, v_ffn1_w_up, v_ffn1_w_down, v_mix_norm, v_w_in, v_conv_w, v_conv_b, v_rg_w_a, v_rg_b_a, v_rg_w_x, v_rg_b_x, v_rg_lambda, v_q_norm, v_k_norm, v_rnn_out_norm, v_attn_out_norm, v_w_out, v_ffn2_norm, v_ffn2_w_gate, v_ffn2_w_up, v_ffn2_w_down):
    given = dict(locals())
    w = {n: given[n] for n in WEIGHTS}
    m = {n: given["m_" + n] for n in WEIGHTS}
    v = {n: given["v_" + n] for n in WEIGHTS}
    chip = 2 * lax.axis_index("x") + lax.axis_index("y")

    shards = [w[n][0].astype(BF16) for n in BIG]
    gathered = _gather_weights(shards, [w["conv_w"][0]])
    big = dict(zip(BIG, gathered[:len(BIG)]))
    small = {n: (w[n][0] if w[n].ndim > 2 else w[n]) for n in SMALL}
    small["conv_w"] = jnp.transpose(gathered[-1], (1, 0, 2)).reshape(CONV_W, D_RNN)

    loss, grad_x, gb, gs = _local_step(x[0], loss_target[0], big, small)
    loss = lax.psum(loss, ("x", "y", "c"))

    theirs, mine = _pair_exchange([gb[n] for n in BIG])
    pair = [_pair_sum(a, b, "pair_sum_" + n) for n, a, b in zip(BIG, mine, theirs)]
    slots = _chip_exchange(pair)
    totals = [_slot_sum(a, "chip_sum_" + n) for n, a in zip(BIG, slots)]
    swapped = _half_swap(totals)
    grads, deltas, new_m, new_v = {}, {}, {}, {}
    for n, t in zip(BIG, swapped):
        shp = w[n].shape
        g2 = t.reshape(shp[1], shp[2])
        d2, m2, v2 = _adamw(w[n][0], g2, m[n][0], v[n][0], "adamw_" + n)
        grads[n], deltas[n], new_m[n], new_v[n] = g2.reshape(shp), d2.reshape(shp), m2.reshape(shp), v2.reshape(shp)

    full_shapes = [gs[n].shape for n in SMALL]
    everyone = _gather_small(_pack([gs[n] for n in SMALL]))
    g_small = _slot_sum(everyone, "small_grad_sum")
    g_parts = dict(zip(SMALL, _unpack(g_small, full_shapes)))
    quarter = D_RNN // N_CHIPS
    g_parts["conv_w"] = lax.dynamic_slice_in_dim(g_parts["conv_w"], chip * quarter, quarter, axis=1)
    local_shapes = [w[n].shape for n in SMALL]
    pk = lambda tree: _pack([tree[n] for n in SMALL])
    d_s, m_s, v_s = _adamw(pk(w), pk(g_parts), pk(m), pk(v), "adamw_small")
    for tree, packed in ((grads, pk(g_parts)), (deltas, d_s), (new_m, m_s), (new_v, v_s)):
        tree.update(zip(SMALL, _unpack(packed, local_shapes)))

    return (loss, grad_x.reshape(x.shape), *[grads[n] for n in WEIGHTS], *[deltas[n] for n in WEIGHTS],
            *[new_m[n] for n in WEIGHTS], *[new_v[n] for n in WEIGHTS])
```

```python
import functools
import math

import jax
import jax.numpy as jnp
from jax import lax
from jax.experimental import pallas as pl
from jax.experimental.pallas import tpu as pltpu

F32 = jnp.float32
BF16 = jnp.bfloat16
MESH = pl.DeviceIdType.MESH

D_MODEL = 1024
N_CHIPS = 4
D_RNN = 512
D_ATT = 512
N_HEADS = 8
HEAD_DIM = 64
RNN_BLOCKS = 8
CONV_W = 4
RG_C = 8.0
N_IN = 2 * D_RNN + 3 * D_ATT
EPS = 1e-6
ATT_BLOCK = 128
ATT_GROUP = 3
EXP_ZERO = -105.0

ADAM_LR = 0.001
ADAM_B1 = 0.9
ADAM_B2 = 0.999
ADAM_EPS = 1e-08
ADAM_WD = 0.01
ADAM_STEP = 10

V7X_VMEM_LIMIT = 56 * 1024 * 1024
TOKEN_TILE = 512

GELU_K0 = math.sqrt(2.0 / math.pi)
GELU_K1 = 0.044715


def _params(sem=None):
    return pltpu.CompilerParams(dimension_semantics=sem, vmem_limit_bytes=V7X_VMEM_LIMIT)


def _dot(a, b):
    return jnp.dot(a, b, preferred_element_type=F32)


def _dot_nt(a, b):
    return lax.dot_general(a, b, (((1,), (1,)), ((), ())), preferred_element_type=F32)


def _dot_tn(a, b):
    return lax.dot_general(a, b, (((0,), (0,)), ((), ())), preferred_element_type=F32)


def _sigmoid(x):
    return 1.0 / (1.0 + jnp.exp(-x))


def _rms_r(xv):
    return lax.rsqrt(jnp.mean(xv * xv, axis=-1, keepdims=True) + EPS)


def _rms_bwd(xv, r, nw, dh):
    t = dh * nw
    dx = r * t - xv * (r * r * r * jnp.mean(t * xv, axis=-1, keepdims=True))
    dn = jnp.sum(dh * xv * r, axis=0, keepdims=True)
    return dx, dn


def _gelu(x):
    t = jnp.tanh(GELU_K0 * (x + GELU_K1 * x * x * x))
    return 0.5 * x * (1.0 + t)


def _gelu_grad(x):
    t = jnp.tanh(GELU_K0 * (x + GELU_K1 * x * x * x))
    return 0.5 * (1.0 + t) + 0.5 * x * (1.0 - t * t) * (GELU_K0 * (1.0 + 3.0 * GELU_K1 * x * x))


def _expm1_neg(x):
    p = 1.0 + x * (1.0 / 8.0)
    for k in (7.0, 6.0, 5.0, 4.0, 3.0, 2.0):
        p = 1.0 + x * (1.0 / k) * p
    return jnp.where(x > -0.25, x * p, jnp.exp(x) - 1.0)


def _log_sigmoid(x):
    return jnp.minimum(x, 0.0) - jnp.log(1.0 + jnp.exp(-jnp.abs(x)))


def _tile(s):
    return min(TOKEN_TILE, s)


def _ffn_fwd(x, nw, wg, wu, wd, tgt=None):
    s, d = x.shape
    nb, _, fb = wg.shape
    tm = _tile(s)
    with_loss = tgt is not None

    def body(*refs):
        if with_loss:
            x_ref, nw_ref, wg_ref, wu_ref, wd_ref, tgt_ref, out_ref, g_ref, u_ref, hb_ref, ab_ref, loss_ref, hs, acc = refs
        else:
            x_ref, nw_ref, wg_ref, wu_ref, wd_ref, out_ref, g_ref, u_ref, hb_ref, ab_ref, hs, acc = refs
        i = pl.program_id(0)
        j = pl.program_id(1)

        @pl.when(j == 0)
        def _():
            xv = x_ref[...]
            hb = (xv * _rms_r(xv) * nw_ref[...]).astype(BF16)
            hs[...] = hb
            hb_ref[...] = hb
            acc[...] = jnp.zeros_like(acc)

        hb = hs[...]
        g = _dot(hb, wg_ref[...])
        u = _dot(hb, wu_ref[...])
        g_ref[...] = g
        u_ref[...] = u
        ab = (g * _sigmoid(g) * u).astype(BF16)
        ab_ref[...] = ab
        acc[...] += _dot(ab, wd_ref[...])

        @pl.when(j == nb - 1)
        def _():
            y = x_ref[...] + 0.5 * acc[...]
            if with_loss:
                diff = y - tgt_ref[...]
                out_ref[...] = diff * (1.0 / d)

                @pl.when(i == 0)
                def _():
                    loss_ref[...] = jnp.zeros_like(loss_ref)

                loss_ref[...] += jnp.sum(diff * diff) * (0.5 / d)
            else:
                out_ref[...] = y

    row = pl.BlockSpec((tm, d), lambda i, j: (i, 0))
    in_specs = [row, pl.BlockSpec((1, d), lambda i, j: (0, 0)),
                pl.BlockSpec((None, d, fb), lambda i, j: (j, 0, 0)),
                pl.BlockSpec((None, d, fb), lambda i, j: (j, 0, 0)),
                pl.BlockSpec((None, fb, d), lambda i, j: (j, 0, 0))]
    args = [x, nw, wg, wu, wd]
    if with_loss:
        in_specs.append(row)
        args.append(tgt)
    blk = pl.BlockSpec((None, tm, fb), lambda i, j: (j, i, 0))
    out_shape = [jax.ShapeDtypeStruct((s, d), F32), jax.ShapeDtypeStruct((nb, s, fb), F32),
                 jax.ShapeDtypeStruct((nb, s, fb), F32), jax.ShapeDtypeStruct((s, d), BF16),
                 jax.ShapeDtypeStruct((nb, s, fb), BF16)]
    out_specs = [row, blk, blk, row, blk]
    if with_loss:
        out_shape.append(jax.ShapeDtypeStruct((1, 128), F32))
        out_specs.append(pl.BlockSpec((1, 128), lambda i, j: (0, 0)))
    return pl.pallas_call(
        body, name="ffn_fwd_loss" if with_loss else "ffn_fwd",
        grid=(s // tm, nb), in_specs=in_specs, out_specs=out_specs, out_shape=out_shape,
        scratch_shapes=[pltpu.VMEM((tm, d), BF16), pltpu.VMEM((tm, d), F32)],
        compiler_params=_params(("arbitrary", "arbitrary")),
    )(*args)


def _ffn_bwd_act(x, nw, dy, g, u, wg, wu, wd, name):
    s, d = x.shape
    nb, _, fb = wg.shape
    tm = _tile(s)

    def body(x_ref, nw_ref, dy_ref, g_ref, u_ref, wg_ref, wu_ref, wd_ref,
             dx_ref, dg_ref, du_ref, dyb_ref, dnw_ref, dys, acc):
        i = pl.program_id(0)
        j = pl.program_id(1)

        @pl.when(j == 0)
        def _():
            dyb = dy_ref[...].astype(BF16)
            dys[...] = dyb
            dyb_ref[...] = dyb
            acc[...] = jnp.zeros_like(acc)

        da = 0.5 * _dot_nt(dys[...], wd_ref[...])
        gv = g_ref[...]
        sg = _sigmoid(gv)
        dub = (da * (gv * sg)).astype(BF16)
        dgb = (da * u_ref[...] * (sg * (1.0 + gv * (1.0 - sg)))).astype(BF16)
        dg_ref[...] = dgb
        du_ref[...] = dub
        acc[...] += _dot_nt(dgb, wg_ref[...]) + _dot_nt(dub, wu_ref[...])

        @pl.when(j == nb - 1)
        def _():
            xv = x_ref[...]
            dx, dn = _rms_bwd(xv, _rms_r(xv), nw_ref[...], acc[...])
            dx_ref[...] = dy_ref[...] + dx

            @pl.when(i == 0)
            def _():
                dnw_ref[...] = jnp.zeros_like(dnw_ref)

            dnw_ref[...] += dn

    row = pl.BlockSpec((tm, d), lambda i, j: (i, 0))
    vec = pl.BlockSpec((1, d), lambda i, j: (0, 0))
    blk = pl.BlockSpec((None, tm, fb), lambda i, j: (j, i, 0))
    wcol = pl.BlockSpec((None, d, fb), lambda i, j: (j, 0, 0))
    wrow = pl.BlockSpec((None, fb, d), lambda i, j: (j, 0, 0))
    return pl.pallas_call(
        body, name=name, grid=(s // tm, nb),
        in_specs=[row, vec, row, blk, blk, wcol, wcol, wrow],
        out_specs=[row, blk, blk, row, vec],
        out_shape=[jax.ShapeDtypeStruct((s, d), F32), jax.ShapeDtypeStruct((nb, s, fb), BF16),
                   jax.ShapeDtypeStruct((nb, s, fb), BF16), jax.ShapeDtypeStruct((s, d), BF16),
                   jax.ShapeDtypeStruct((1, d), F32)],
        scratch_shapes=[pltpu.VMEM((tm, d), BF16), pltpu.VMEM((tm, d), F32)],
        compiler_params=_params(("arbitrary", "arbitrary")),
    )(x, nw, dy, g, u, wg, wu, wd)


def _wgrad(a, b, a_spec, b_spec, out_rows, out_cols, scale, name):
    s = a.shape[-2]
    tk = _tile(s)
    nk = s // tk

    def body(a_ref, b_ref, out_ref, acc):
        k = pl.program_id(1)

        @pl.when(k == 0)
        def _():
            acc[...] = jnp.zeros_like(acc)

        acc[...] += _dot_tn(a_ref[...], b_ref[...])

        @pl.when(k == nk - 1)
        def _():
            out_ref[...] = (acc[...] * scale).astype(BF16)

    return pl.pallas_call(
        body, name=name, grid=(N_CHIPS, nk),
        in_specs=[a_spec(tk), b_spec(tk)],
        out_specs=pl.BlockSpec((None, out_rows, out_cols), lambda j, k: (j, 0, 0)),
        out_shape=jax.ShapeDtypeStruct((N_CHIPS, out_rows, out_cols), BF16),
        scratch_shapes=[pltpu.VMEM((out_rows, out_cols), F32)],
        compiler_params=_params(("arbitrary", "arbitrary")),
    )(a, b)


def _ffn_wgrads(hb, ab, dg, du, dyb, tag):
    s, d = hb.shape
    fb = ab.shape[-1]
    shared = lambda cols: (lambda tk: pl.BlockSpec((tk, cols), lambda j, k: (k, 0)))
    stacked = lambda cols: (lambda tk: pl.BlockSpec((None, tk, cols), lambda j, k: (j, k, 0)))
    dwg = _wgrad(hb, dg, shared(d), stacked(fb), d, fb, 1.0, "wgrad_gate_" + tag)
    dwu = _wgrad(hb, du, shared(d), stacked(fb), d, fb, 1.0, "wgrad_up_" + tag)
    dwd = _wgrad(ab, dyb, stacked(fb), shared(d), fb, d, 0.5, "wgrad_down_" + tag)
    return dwg, dwu, dwd


def _mix_pre(x, nw, win):
    s, d = x.shape
    nb, _, cb = win.shape
    tm = _tile(s)

    def body(x_ref, nw_ref, w_ref, p_ref, hb_ref, hs):
        @pl.when(pl.program_id(1) == 0)
        def _():
            xv = x_ref[...]
            hb = (xv * _rms_r(xv) * nw_ref[...]).astype(BF16)
            hs[...] = hb
            hb_ref[...] = hb

        p_ref[...] = _dot(hs[...], w_ref[...])

    row = pl.BlockSpec((tm, d), lambda i, j: (i, 0))
    return pl.pallas_call(
        body, name="mix_pre", grid=(s // tm, nb),
        in_specs=[row, pl.BlockSpec((1, d), lambda i, j: (0, 0)),
                  pl.BlockSpec((None, d, cb), lambda i, j: (j, 0, 0))],
        out_specs=[pl.BlockSpec((tm, cb), lambda i, j: (i, j)), row],
        out_shape=[jax.ShapeDtypeStruct((s, nb * cb), F32), jax.ShapeDtypeStruct((s, d), BF16)],
        scratch_shapes=[pltpu.VMEM((tm, d), BF16)],
        compiler_params=_params(("arbitrary", "arbitrary")),
    )(x, nw, win)


def _mix_pre_bwd(x, nw, dres, dpb, win):
    s, d = x.shape
    nb, _, cb = win.shape
    tm = _tile(s)

    def body(x_ref, nw_ref, dres_ref, dp_ref, w_ref, dx_ref, dnw_ref, acc):
        i = pl.program_id(0)
        j = pl.program_id(1)

        @pl.when(j == 0)
        def _():
            acc[...] = jnp.zeros_like(acc)

        acc[...] += _dot_nt(dp_ref[...], w_ref[...])

        @pl.when(j == nb - 1)
        def _():
            xv = x_ref[...]
            dx, dn = _rms_bwd(xv, _rms_r(xv), nw_ref[...], acc[...])
            dx_ref[...] = dres_ref[...] + dx

            @pl.when(i == 0)
            def _():
                dnw_ref[...] = jnp.zeros_like(dnw_ref)

            dnw_ref[...] += dn

    row = pl.BlockSpec((tm, d), lambda i, j: (i, 0))
    vec = pl.BlockSpec((1, d), lambda i, j: (0, 0))
    return pl.pallas_call(
        body, name="mix_pre_bwd", grid=(s // tm, nb),
        in_specs=[row, vec, row, pl.BlockSpec((tm, cb), lambda i, j: (i, j)),
                  pl.BlockSpec((None, d, cb), lambda i, j: (j, 0, 0))],
        out_specs=[row, vec],
        out_shape=[jax.ShapeDtypeStruct((s, d), F32), jax.ShapeDtypeStruct((1, d), F32)],
        scratch_shapes=[pltpu.VMEM((tm, d), F32)],
        compiler_params=_params(("arbitrary", "arbitrary")),
    )(x, nw, dres, dpb, win)


def _mix_post(x, yr, ya, nr, na, wout):
    s, d = x.shape
    h = yr.shape[1]
    tm = _tile(s)

    def body(x_ref, yr_ref, ya_ref, nr_ref, na_ref, w_ref, out_ref):
        yrv = yr_ref[...]
        yav = ya_ref[...]
        onb = (yrv * _rms_r(yrv) * nr_ref[...]).astype(BF16)
        oab = (yav * _rms_r(yav) * na_ref[...]).astype(BF16)
        out_ref[...] = x_ref[...] + _dot(onb, w_ref[0:h, :]) + _dot(oab, w_ref[h:2 * h, :])

    row = pl.BlockSpec((tm, d), lambda i: (i, 0))
    half = pl.BlockSpec((tm, h), lambda i: (i, 0))
    vec = pl.BlockSpec((1, h), lambda i: (0, 0))
    return pl.pallas_call(
        body, name="mix_post", grid=(s // tm,),
        in_specs=[row, half, half, vec, vec, pl.BlockSpec((2 * h, d), lambda i: (0, 0))],
        out_specs=row, out_shape=jax.ShapeDtypeStruct((s, d), F32),
        compiler_params=_params(("arbitrary",)),
    )(x, yr, ya, nr, na, wout)


def _mix_post_bwd(dx, yr, ya, nr, na, wout):
    s, d = dx.shape
    h = yr.shape[1]
    tm = _tile(s)

    def body(dx_ref, yr_ref, ya_ref, nr_ref, na_ref, w_ref,
             dyr_ref, dya_ref, yc_ref, dxb_ref, dnr_ref, dna_ref):
        i = pl.program_id(0)
        dxb = dx_ref[...].astype(BF16)
        dxb_ref[...] = dxb
        dyc = _dot_nt(dxb, w_ref[...])
        yrv = yr_ref[...]
        yav = ya_ref[...]
        rr = _rms_r(yrv)
        ra = _rms_r(yav)
        yc_ref[:, 0:h] = (yrv * rr * nr_ref[...]).astype(BF16)
        yc_ref[:, h:2 * h] = (yav * ra * na_ref[...]).astype(BF16)
        dyr, dnr = _rms_bwd(yrv, rr, nr_ref[...], dyc[:, 0:h])
        dya, dna = _rms_bwd(yav, ra, na_ref[...], dyc[:, h:2 * h])
        dyr_ref[...] = dyr
        dya_ref[...] = dya

        @pl.when(i == 0)
        def _():
            dnr_ref[...] = jnp.zeros_like(dnr_ref)
            dna_ref[...] = jnp.zeros_like(dna_ref)

        dnr_ref[...] += dnr
        dna_ref[...] += dna

    row = pl.BlockSpec((tm, d), lambda i: (i, 0))
    half = pl.BlockSpec((tm, h), lambda i: (i, 0))
    vec = pl.BlockSpec((1, h), lambda i: (0, 0))
    return pl.pallas_call(
        body, name="mix_post_bwd", grid=(s // tm,),
        in_specs=[row, half, half, vec, vec, pl.BlockSpec((2 * h, d), lambda i: (0, 0))],
        out_specs=[half, half, pl.BlockSpec((tm, 2 * h), lambda i: (i, 0)), row, vec, vec],
        out_shape=[jax.ShapeDtypeStruct((s, h), F32), jax.ShapeDtypeStruct((s, h), F32),
                   jax.ShapeDtypeStruct((s, 2 * h), BF16), jax.ShapeDtypeStruct((s, d), BF16),
                   jax.ShapeDtypeStruct((1, h), F32), jax.ShapeDtypeStruct((1, h), F32)],
        compiler_params=_params(("arbitrary",)),
    )(dx, yr, ya, nr, na, wout)


def _shift_down(xv, s, prev8):
    rolled = pltpu.roll(xv, s, 0)
    row8 = lax.broadcasted_iota(jnp.int32, prev8.shape, 0)
    head = jnp.where(row8 < s, pltpu.roll(prev8, s, 0), rolled[0:8, :])
    return jnp.concatenate([head, rolled[8:, :]], axis=0)


def _shift_up(xv, s, next8):
    n = xv.shape[0]
    rolled = pltpu.roll(xv, n - s, 0)
    row8 = lax.broadcasted_iota(jnp.int32, next8.shape, 0)
    tail = jnp.where(row8 >= 8 - s, pltpu.roll(next8, 8 - s, 0), rolled[n - 8:, :])
    return jnp.concatenate([rolled[:n - 8, :], tail], axis=0)


def _scan_fwd(a, b):
    n = a.shape[0]
    row = lax.broadcasted_iota(jnp.int32, a.shape, 0)
    s = 1
    while s < n:
        ok = row >= s
        b = jnp.where(ok, a * pltpu.roll(b, s, 0) + b, b)
        a = jnp.where(ok, a * pltpu.roll(a, s, 0), a)
        s *= 2
    return b


def _scan_bwd(a, b):
    n = a.shape[0]
    row = lax.broadcasted_iota(jnp.int32, a.shape, 0)
    s = 1
    while s < n:
        ok = row < n - s
        b = jnp.where(ok, a * pltpu.roll(b, n - s, 0) + b, b)
        a = jnp.where(ok, a * pltpu.roll(a, n - s, 0), a)
        s *= 2
    return b


def _rglru_gates(xv, prev8, cw_ref, cb_ref, wa_ref, ba_ref, wx_ref, bx_ref, lam_ref):
    x1 = _shift_down(xv, 1, prev8)
    x2 = _shift_down(xv, 2, prev8)
    x3 = _shift_down(xv, 3, prev8)
    xc = cw_ref[3:4, :] * xv + cw_ref[2:3, :] * x1 + cw_ref[1:2, :] * x2 + cw_ref[0:1, :] * x3 + cb_ref[...]
    xcb = xc.astype(BF16)
    r = _sigmoid(_dot(xcb, wa_ref[...]) + ba_ref[...])
    ig = _sigmoid(_dot(xcb, wx_ref[...]) + bx_ref[...])
    c = RG_C * _log_sigmoid(lam_ref[...])
    la = r * c
    a = jnp.exp(la)
    m = jnp.sqrt(-_expm1_neg(2.0 * la))
    return (x1, x2, x3), xc, xcb, r, ig, c, a, m


def _rglru_fwd(proj, cw, cb, wa, ba, wx, bx, lam):
    s = proj.shape[0]
    w = D_RNN
    tm = _tile(s)

    def body(xr_ref, gate_ref, cw_ref, cb_ref, wa_ref, ba_ref, wx_ref, bx_ref, lam_ref,
             y_ref, h_ref, prev, hlast):
        @pl.when(pl.program_id(0) == 0)
        def _():
            prev[...] = jnp.zeros_like(prev)
            hlast[...] = jnp.zeros_like(hlast)

        xv = xr_ref[...]
        _, xc, _, _, ig, _, a, m = _rglru_gates(xv, prev[...], cw_ref, cb_ref, wa_ref, ba_ref,
                                                wx_ref, bx_ref, lam_ref)
        b = m * (ig * xc)
        row = lax.broadcasted_iota(jnp.int32, b.shape, 0)
        b = jnp.where(row == 0, b + a * hlast[...], b)
        h = _scan_fwd(a, b)
        h_ref[...] = h
        y_ref[...] = h * _gelu(gate_ref[...])
        prev[...] = xv[tm - 8:, :]
        hlast[...] = h[tm - 1:tm, :]

    vec = pl.BlockSpec((1, w), lambda i: (0, 0))
    sq = pl.BlockSpec((w, w), lambda i: (0, 0))
    out = pl.BlockSpec((tm, w), lambda i: (i, 0))
    return pl.pallas_call(
        body, name="rglru_fwd", grid=(s // tm,),
        in_specs=[pl.BlockSpec((tm, w), lambda i: (i, 0)), pl.BlockSpec((tm, w), lambda i: (i, 1)),
                  pl.BlockSpec((CONV_W, w), lambda i: (0, 0)), vec, sq, vec, sq, vec, vec],
        out_specs=[out, out],
        out_shape=[jax.ShapeDtypeStruct((s, w), F32), jax.ShapeDtypeStruct((s, w), F32)],
        scratch_shapes=[pltpu.VMEM((8, w), F32), pltpu.VMEM((1, w), F32)],
        compiler_params=_params(("arbitrary",)),
    )(proj, proj, cw, cb, wa, ba, wx, bx, lam)


def _rglru_bwd(proj, hseq, dyr, cw, cb, wa, ba, wx, bx, lam):
    s = proj.shape[0]
    w = D_RNN
    tm = _tile(s)
    nt = s // tm
    t8 = tm // 8

    def body(xr_ref, xp_ref, gate_ref, h_ref, hp_ref, dy_ref, cw_ref, cb_ref, wa_ref, ba_ref,
             wx_ref, bx_ref, lam_ref,
             dxr_ref, dgate_ref, dcw_ref, dcb_ref, dwa_ref, dba_ref, dwx_ref, dbx_ref, dlam_ref,
             carry, dxc_next):
        i = pl.program_id(0)
        first_tile = i == nt - 1

        @pl.when(i == 0)
        def _():
            carry[...] = jnp.zeros_like(carry)
            dxc_next[...] = jnp.zeros_like(dxc_next)
            for ref in (dcw_ref, dcb_ref, dwa_ref, dba_ref, dwx_ref, dbx_ref, dlam_ref):
                ref[...] = jnp.zeros_like(ref)

        xv = xr_ref[...]
        prev8 = jnp.where(first_tile, 0.0, xp_ref[...])
        hprev8 = jnp.where(first_tile, 0.0, hp_ref[...])
        (x1, x2, x3), xc, xcb, r, ig, c, a, m = _rglru_gates(
            xv, prev8, cw_ref, cb_ref, wa_ref, ba_ref, wx_ref, bx_ref, lam_ref)
        gv = gate_ref[...]
        hv = h_ref[...]
        dy = dy_ref[...]
        dgate_ref[...] = (dy * hv * _gelu_grad(gv)).astype(BF16)
        dh = dy * _gelu(gv)
        row = lax.broadcasted_iota(jnp.int32, dh.shape, 0)
        dh = jnp.where(row == tm - 1, dh + carry[...], dh)
        a_up = jnp.where(row == tm - 1, 0.0, pltpu.roll(a, tm - 1, 0))
        lam_t = _scan_bwd(a_up, dh)
        carry[...] = a[0:1, :] * lam_t[0:1, :]
        hm1 = _shift_down(hv, 1, hprev8)
        da = lam_t * hm1
        ixc = ig * xc
        dm = lam_t * ixc
        dig = lam_t * m * xc
        dxc = lam_t * m * ig
        dla = da * a - dm * (a * a) / m
        dr = dla * c
        dlam_ref[...] += jnp.sum(dla * r, axis=0, keepdims=True)
        dpa = dr * r * (1.0 - r)
        dpi = dig * ig * (1.0 - ig)
        dba_ref[...] += jnp.sum(dpa, axis=0, keepdims=True)
        dbx_ref[...] += jnp.sum(dpi, axis=0, keepdims=True)
        dpab = dpa.astype(BF16)
        dpib = dpi.astype(BF16)
        dwa_ref[...] += _dot_tn(xcb, dpab)
        dwx_ref[...] += _dot_tn(xcb, dpib)
        dxc = dxc + _dot_nt(dpab, wa_ref[...]) + _dot_nt(dpib, wx_ref[...])
        dcb_ref[...] += jnp.sum(dxc, axis=0, keepdims=True)
        dcw_ref[3:4, :] += jnp.sum(dxc * xv, axis=0, keepdims=True)
        dcw_ref[2:3, :] += jnp.sum(dxc * x1, axis=0, keepdims=True)
        dcw_ref[1:2, :] += jnp.sum(dxc * x2, axis=0, keepdims=True)
        dcw_ref[0:1, :] += jnp.sum(dxc * x3, axis=0, keepdims=True)
        nxt = dxc_next[...]
        dxr = (cw_ref[3:4, :] * dxc + cw_ref[2:3, :] * _shift_up(dxc, 1, nxt)
               + cw_ref[1:2, :] * _shift_up(dxc, 2, nxt) + cw_ref[0:1, :] * _shift_up(dxc, 3, nxt))
        dxr_ref[...] = dxr.astype(BF16)
        dxc_next[...] = dxc[0:8, :]

        @pl.when(first_tile)
        def _():
            lv = lam_ref[...]
            dlam_ref[...] = dlam_ref[...] * (RG_C * _sigmoid(-lv))

    rev = lambda i: nt - 1 - i
    vec = pl.BlockSpec((1, w), lambda i: (0, 0))
    sq = pl.BlockSpec((w, w), lambda i: (0, 0))
    cur = lambda col: pl.BlockSpec((tm, w), lambda i: (rev(i), col))
    before = lambda cols: pl.BlockSpec((8, w), lambda i: (jnp.maximum(rev(i) * t8 - 1, 0), 0))
    return pl.pallas_call(
        body, name="rglru_bwd", grid=(nt,),
        in_specs=[cur(0), before(None), cur(1), cur(0), before(None), cur(0),
                  pl.BlockSpec((CONV_W, w), lambda i: (0, 0)), vec, sq, vec, sq, vec, vec],
        out_specs=[cur(0), cur(0), pl.BlockSpec((CONV_W, w), lambda i: (0, 0)), vec, sq, vec, sq, vec, vec],
        out_shape=[jax.ShapeDtypeStruct((s, w), BF16), jax.ShapeDtypeStruct((s, w), BF16),
                   jax.ShapeDtypeStruct((CONV_W, w), F32), jax.ShapeDtypeStruct((1, w), F32),
                   jax.ShapeDtypeStruct((w, w), F32), jax.ShapeDtypeStruct((1, w), F32),
                   jax.ShapeDtypeStruct((w, w), F32), jax.ShapeDtypeStruct((1, w), F32),
                   jax.ShapeDtypeStruct((1, w), F32)],
        scratch_shapes=[pltpu.VMEM((1, w), F32), pltpu.VMEM((8, w), F32)],
        compiler_params=_params(("arbitrary",)),
    )(proj, proj, proj, hseq, hseq, dyr, cw, cb, wa, ba, wx, bx, lam)


def _qk_prep(q_ref, k_ref, v_ref, qg_ref, kg_ref, qn, kn, vb, scale):
    qv = q_ref[...]
    qn[...] = (qv * _rms_r(qv) * qg_ref[...] * scale).astype(BF16)
    kv = k_ref[...]
    kn[...] = (kv * _rms_r(kv) * kg_ref[...]).astype(BF16)
    vb[...] = v_ref[...].astype(BF16)


def _sb_logs(z, valid):
    l1p = jnp.log(1.0 + jnp.exp(-jnp.abs(z)))
    lb = jnp.minimum(z, 0.0) - l1p
    lm = jnp.where(valid, -jnp.maximum(z, 0.0) - l1p, 0.0)
    return lb, lm


def _split_dot(xv, tri):
    hi = xv.astype(BF16)
    lo = (xv - hi.astype(F32)).astype(BF16)
    return _dot(hi, tri) + _dot(lo, tri)


def _key_mask(row, col, kj, qi, blk):
    shift = jnp.where(kj < qi, blk, 0)
    shift = jnp.where(kj >= 0, shift, -blk)
    return col < row + shift


def _attn_fwd(q, k, v, qg, kg):
    nh, s, dh = q.shape
    blk, grp = ATT_BLOCK, ATT_GROUP
    nq = s // blk
    scale = 1.0 / math.sqrt(dh)

    def body(q_ref, k_ref, v_ref, qg_ref, kg_ref, o_ref, qn, kn, vb):
        _qk_prep(q_ref, k_ref, v_ref, qg_ref, kg_ref, qn, kn, vb, scale)
        row = lax.broadcasted_iota(jnp.int32, (blk, blk), 0)
        col = lax.broadcasted_iota(jnp.int32, (blk, blk), 1)
        later = jnp.where(row > col, 1.0, 0.0).astype(BF16)

        def q_step(qi, _):
            qoff = pl.multiple_of(qi * blk, blk)
            qt = qn[pl.ds(qoff, blk), :]

            def more(carry):
                g, live, _, _ = carry
                return jnp.logical_and(g * grp <= qi, live > 0)

            def group(carry):
                g, _, acc, run = carry
                for b in range(grp):
                    kj = qi - g * grp - b
                    koff = pl.multiple_of(jnp.maximum(kj, 0) * blk, blk)
                    z = _dot_nt(qt, kn[pl.ds(koff, blk), :])
                    valid = _key_mask(row, col, kj, qi, blk)
                    lb, lm = _sb_logs(z, valid)
                    tail = _split_dot(lm, later) + run
                    wgt = jnp.where(valid, jnp.exp(lb + tail), 0.0)
                    acc = acc + _dot(wgt.astype(BF16), vb[pl.ds(koff, blk), :])
                    run = run + jnp.sum(lm, axis=1, keepdims=True)
                live = (jnp.max(run) > EXP_ZERO).astype(jnp.int32)
                return g + 1, live, acc, run

            _, _, acc, _ = lax.while_loop(
                more, group, (jnp.int32(0), jnp.int32(1), jnp.zeros((blk, dh), F32), jnp.zeros((blk, 1), F32)))
            o_ref[pl.ds(qoff, blk), :] = acc
            return 0

        lax.fori_loop(0, nq, q_step, 0)

    head = pl.BlockSpec((None, s, dh), lambda h: (h, 0, 0))
    vec = pl.BlockSpec((1, dh), lambda h: (0, 0))
    return pl.pallas_call(
        body, name="attn_fwd", grid=(nh,),
        in_specs=[head, head, head, vec, vec], out_specs=head,
        out_shape=jax.ShapeDtypeStruct((nh, s, dh), F32),
        scratch_shapes=[pltpu.VMEM((s, dh), BF16)] * 3,
        compiler_params=_params(("arbitrary",)),
    )(q, k, v, qg, kg)


def _attn_bwd(q, k, v, do, qg, kg):
    nh, s, dh = q.shape
    blk, grp = ATT_BLOCK, ATT_GROUP
    nq = s // blk
    scale = 1.0 / math.sqrt(dh)

    def body(q_ref, k_ref, v_ref, do_ref, qg_ref, kg_ref,
             dq_ref, dk_ref, dv_ref, dqg_ref, dkg_ref, qn, kn, vb, dob, runs, dqn, dkn):
        _qk_prep(q_ref, k_ref, v_ref, qg_ref, kg_ref, qn, kn, vb, scale)
        dob[...] = do_ref[...].astype(BF16)
        dkn[...] = jnp.zeros_like(dkn)
        dv_ref[...] = jnp.zeros_like(dv_ref)
        row = lax.broadcasted_iota(jnp.int32, (blk, blk), 0)
        col = lax.broadcasted_iota(jnp.int32, (blk, blk), 1)
        later = jnp.where(row > col, 1.0, 0.0).astype(BF16)
        earlier = jnp.where(row < col, 1.0, 0.0).astype(BF16)

        def q_step(qi, _):
            qoff = pl.multiple_of(qi * blk, blk)
            qt = qn[pl.ds(qoff, blk), :]
            dot = dob[pl.ds(qoff, blk), :]

            def more(carry):
                g, live, _ = carry
                return jnp.logical_and(g * grp <= qi, live > 0)

            def run_group(carry):
                g, _, run = carry
                for b in range(grp):
                    kj = qi - g * grp - b
                    koff = pl.multiple_of(jnp.maximum(kj, 0) * blk, blk)
                    runs[kj + grp - 1] = jnp.broadcast_to(run, (blk, blk))
                    z = _dot_nt(qt, kn[pl.ds(koff, blk), :])
                    _, lm = _sb_logs(z, _key_mask(row, col, kj, qi, blk))
                    run = run + jnp.sum(lm, axis=1, keepdims=True)
                live = (jnp.max(run) > EXP_ZERO).astype(jnp.int32)
                return g + 1, live, run

            groups, _, _ = lax.while_loop(more, run_group, (jnp.int32(0), jnp.int32(1), jnp.zeros((blk, 1), F32)))

            def k_group(gg, carry):
                dq_acc, esum = carry
                g = groups - 1 - gg
                for b in reversed(range(grp)):
                    kj = qi - g * grp - b
                    koff = pl.multiple_of(jnp.maximum(kj, 0) * blk, blk)
                    kt = kn[pl.ds(koff, blk), :]
                    vt = vb[pl.ds(koff, blk), :]
                    z = _dot_nt(qt, kt)
                    valid = _key_mask(row, col, kj, qi, blk)
                    lb, lm = _sb_logs(z, valid)
                    tail = _split_dot(lm, later) + runs[kj + grp - 1]
                    wgt = jnp.where(valid, jnp.exp(lb + tail), 0.0)
                    e = _dot_nt(dot, vt) * wgt
                    before = _split_dot(e, earlier) + esum
                    beta = jnp.exp(lb)
                    dz = jnp.where(valid, e * (1.0 - beta) - before * beta, 0.0)
                    dzb = dz.astype(BF16)
                    dq_acc = dq_acc + _dot(dzb, kt)
                    dkn[pl.ds(koff, blk), :] += _dot_tn(dzb, qt)
                    dv_ref[pl.ds(koff, blk), :] += _dot_tn(wgt.astype(BF16), dot)
                    esum = esum + jnp.sum(e, axis=1, keepdims=True)
                return dq_acc, esum

            dq_acc, _ = lax.fori_loop(0, groups, k_group,
                                      (jnp.zeros((blk, dh), F32), jnp.zeros((blk, 1), F32)))
            dqn[pl.ds(qoff, blk), :] = dq_acc
            return 0

        lax.fori_loop(0, nq, q_step, 0)

        @pl.when(pl.program_id(0) == 0)
        def _():
            dqg_ref[...] = jnp.zeros_like(dqg_ref)
            dkg_ref[...] = jnp.zeros_like(dkg_ref)

        qv = q_ref[...]
        dq, dqg = _rms_bwd(qv, _rms_r(qv), qg_ref[...] * scale, dqn[...])
        dq_ref[...] = dq
        dqg_ref[...] += dqg * scale
        kv = k_ref[...]
        dk, dkg = _rms_bwd(kv, _rms_r(kv), kg_ref[...], dkn[...])
        dk_ref[...] = dk
        dkg_ref[...] += dkg

    head = pl.BlockSpec((None, s, dh), lambda h: (h, 0, 0))
    vec = pl.BlockSpec((1, dh), lambda h: (0, 0))
    return pl.pallas_call(
        body, name="attn_bwd", grid=(nh,),
        in_specs=[head, head, head, head, vec, vec], out_specs=[head, head, head, vec, vec],
        out_shape=[jax.ShapeDtypeStruct((nh, s, dh), F32)] * 3 + [jax.ShapeDtypeStruct((1, dh), F32)] * 2,
        scratch_shapes=[pltpu.VMEM((s, dh), BF16)] * 4 + [pltpu.VMEM((nq + grp - 1, blk, blk), F32)]
        + [pltpu.VMEM((s, dh), F32)] * 2,
        compiler_params=_params(("arbitrary",)),
    )(q, k, v, do, qg, kg)


def _block_diag(w):
    n, c, d = w.shape
    return jnp.einsum("ncd,nm->ncmd", w, jnp.eye(n, dtype=w.dtype)).reshape(n * c, n * d)


def _diag_blocks(full, n):
    c = full.shape[0] // n
    return jnp.stack([full[i * c:(i + 1) * c, i * c:(i + 1) * c] for i in range(n)])


def _to_heads(t):
    s = t.shape[0]
    return jnp.transpose(t.reshape(s, N_HEADS, HEAD_DIM), (1, 0, 2))


def _from_heads(t):
    s = t.shape[1]
    return jnp.transpose(t, (1, 0, 2)).reshape(s, N_HEADS * HEAD_DIM)


def _local_step(x, tgt, big, small):
    wa = _block_diag(small["rg_w_a"]).astype(BF16)
    wx = _block_diag(small["rg_w_x"]).astype(BF16)
    wout = big["w_out"].reshape(D_MODEL, D_MODEL)
    rg = (small["conv_w"], small["conv_b"], wa, small["rg_b_a"], wx, small["rg_b_x"], small["rg_lambda"])

    x1, g1, u1, hb1, ab1 = _ffn_fwd(x, small["ffn1_norm"], big["ffn1_w_gate"], big["ffn1_w_up"], big["ffn1_w_down"])
    proj, hb2 = _mix_pre(x1, small["mix_norm"], big["w_in"])
    yr, hseq = _rglru_fwd(proj, *rg)
    qh = _to_heads(proj[:, 2 * D_RNN:2 * D_RNN + D_ATT])
    kh = _to_heads(proj[:, 2 * D_RNN + D_ATT:2 * D_RNN + 2 * D_ATT])
    vh = _to_heads(proj[:, 2 * D_RNN + 2 * D_ATT:])
    ya = _from_heads(_attn_fwd(qh, kh, vh, small["q_norm"], small["k_norm"]))
    x2 = _mix_post(x1, yr, ya, small["rnn_out_norm"], small["attn_out_norm"], wout)
    dx3, g2, u2, hb3, ab3, loss = _ffn_fwd(x2, small["ffn2_norm"], big["ffn2_w_gate"], big["ffn2_w_up"],
                                          big["ffn2_w_down"], tgt)

    gb, gs = {}, {}
    dx2, dg2, du2, dyb2, gs["ffn2_norm"] = _ffn_bwd_act(
        x2, small["ffn2_norm"], dx3, g2, u2, big["ffn2_w_gate"], big["ffn2_w_up"], big["ffn2_w_down"], "ffn2_bwd")
    gb["ffn2_w_gate"], gb["ffn2_w_up"], gb["ffn2_w_down"] = _ffn_wgrads(hb3, ab3, dg2, du2, dyb2, "ffn2")

    dyr, dya, ycat, dxb2, gs["rnn_out_norm"], gs["attn_out_norm"] = _mix_post_bwd(
        dx2, yr, ya, small["rnn_out_norm"], small["attn_out_norm"], wout)
    quarter = D_MODEL // N_CHIPS
    gb["w_out"] = _wgrad(ycat, dxb2, lambda tk: pl.BlockSpec((tk, quarter), lambda j, k: (k, j)),
                         lambda tk: pl.BlockSpec((tk, D_MODEL), lambda j, k: (k, 0)),
                         quarter, D_MODEL, 1.0, "wgrad_out")
    dqh, dkh, dvh, gs["q_norm"], gs["k_norm"] = _attn_bwd(qh, kh, vh, _to_heads(dya), small["q_norm"], small["k_norm"])
    dxr, dgate, gs["conv_w"], gs["conv_b"], dwa, gs["rg_b_a"], dwx, gs["rg_b_x"], gs["rg_lambda"] = _rglru_bwd(
        proj, hseq, dyr, *rg)
    gs["rg_w_a"] = _diag_blocks(dwa, RNN_BLOCKS)
    gs["rg_w_x"] = _diag_blocks(dwx, RNN_BLOCKS)
    dpb = jnp.concatenate([dxr, dgate, _from_heads(dqh).astype(BF16), _from_heads(dkh).astype(BF16),
                           _from_heads(dvh).astype(BF16)], axis=1)
    cb = N_IN // N_CHIPS
    gb["w_in"] = _wgrad(hb2, dpb, lambda tk: pl.BlockSpec((tk, D_MODEL), lambda j, k: (k, 0)),
                        lambda tk: pl.BlockSpec((tk, cb), lambda j, k: (k, j)),
                        D_MODEL, cb, 1.0, "wgrad_in")
    dx1, gs["mix_norm"] = _mix_pre_bwd(x1, small["mix_norm"], dx2, dpb, big["w_in"])

    dx0, dg1, du1, dyb1, gs["ffn1_norm"] = _ffn_bwd_act(
        x, small["ffn1_norm"], dx1, g1, u1, big["ffn1_w_gate"], big["ffn1_w_up"], big["ffn1_w_down"], "ffn1_bwd")
    gb["ffn1_w_gate"], gb["ffn1_w_up"], gb["ffn1_w_down"] = _ffn_wgrads(hb1, ab1, dg1, du1, dyb1, "ffn1")
    return loss[0, 0], dx0, gb, gs


ANY = pl.BlockSpec(memory_space=pl.ANY)


def _place():
    x, y, c = lax.axis_index("x"), lax.axis_index("y"), lax.axis_index("c")
    other_chips = [(1 - x, y), (x, 1 - y), (1 - x, 1 - y)]
    return x, y, c, 2 * x + y, other_chips


def _remote(src, dst, send_sem, recv_sem, to):
    return pltpu.make_async_remote_copy(src_ref=src, dst_ref=dst, send_sem=send_sem, recv_sem=recv_sem,
                                        device_id=to, device_id_type=MESH)


def _half(rows, c):
    return pl.ds(pl.multiple_of(c * rows, 16), rows)


def _gather_weights(split, whole):
    arrs = list(split) + list(whole)
    n, ns = len(arrs), len(split)

    def body(*refs):
        ins, outs = refs[:n], refs[n:2 * n]
        send_sems, recv_sems, fsend_sems, frecv_sems, loc_sems = refs[2 * n:]
        x, y, c, me, chips = _place()
        sibling = (x, y, 1 - c)

        def region(i, chip, half):
            if i < ns:
                return outs[i].at[chip, _half(arrs[i].shape[0] // 2, half)]
            return outs[i].at[chip]

        local = [pltpu.make_async_copy(ins[i], outs[i].at[me], loc_sems.at[i]) for i in range(n)]
        for cp in local:
            cp.start()
        sends = []
        for i in range(n):
            src = ins[i].at[_half(arrs[i].shape[0] // 2, c)] if i < ns else ins[i]
            for p, chip in enumerate(chips):
                k = 3 * i + p
                sends.append(_remote(src, region(i, me, c), send_sems.at[k], recv_sems.at[k], (*chip, c)))
        for cp in sends:
            cp.start()
        passed = []
        for i in range(n):
            for p, (cx, cy) in enumerate(chips):
                k = 3 * i + p
                got = region(i, 2 * cx + cy, c)
                _remote(got, got, send_sems.at[k], recv_sems.at[k], (cx, cy, c)).wait_recv()
                if i < ns:
                    fwd = _remote(got, got, fsend_sems.at[k], frecv_sems.at[k], sibling)
                    fwd.start()
                    passed.append(fwd)
        for i in range(ns):
            for p, (cx, cy) in enumerate(chips):
                k = 3 * i + p
                got = region(i, 2 * cx + cy, 1 - c)
                _remote(got, got, fsend_sems.at[k], frecv_sems.at[k], sibling).wait_recv()
        for cp in sends + passed:
            cp.wait_send()
        for cp in local:
            cp.wait()

    return pl.pallas_call(
        body, name="gather_weights",
        in_specs=[ANY] * n, out_specs=[ANY] * n,
        out_shape=[jax.ShapeDtypeStruct((N_CHIPS,) + a.shape, a.dtype) for a in arrs],
        scratch_shapes=[pltpu.SemaphoreType.DMA((3 * n,)), pltpu.SemaphoreType.DMA((3 * n,)),
                        pltpu.SemaphoreType.DMA((3 * ns,)), pltpu.SemaphoreType.DMA((3 * ns,)),
                        pltpu.SemaphoreType.DMA((n,))],
    )(*arrs)


def _pair_exchange(grads):
    n = len(grads)

    def body(*refs):
        ins, theirs, mine = refs[:n], refs[n:2 * n], refs[2 * n:3 * n]
        send_sems, recv_sems, loc_sems = refs[3 * n:]
        x, y, c, _, _ = _place()
        sibling = (x, y, 1 - c)
        local, sends = [], []
        for k in range(n):
            rows = grads[k].shape[1] // 2
            local.append(pltpu.make_async_copy(ins[k].at[:, _half(rows, c)], mine[k], loc_sems.at[k]))
            sends.append(_remote(ins[k].at[:, _half(rows, 1 - c)], theirs[k], send_sems.at[k], recv_sems.at[k], sibling))
        for cp in local + sends:
            cp.start()
        for k in range(n):
            _remote(theirs[k], theirs[k], send_sems.at[k], recv_sems.at[k], sibling).wait_recv()
        for cp in sends:
            cp.wait_send()
        for cp in local:
            cp.wait()

    halves = [jax.ShapeDtypeStruct((g.shape[0], g.shape[1] // 2, g.shape[2]), g.dtype) for g in grads]
    outs = pl.pallas_call(
        body, name="grad_pair_exchange",
        in_specs=[ANY] * n, out_specs=[ANY] * (2 * n), out_shape=halves + halves,
        scratch_shapes=[pltpu.SemaphoreType.DMA((n,))] * 3,
    )(*grads)
    return outs[:n], outs[n:]


def _chip_exchange(sums):
    n = len(sums)

    def body(*refs):
        ins, outs = refs[:n], refs[n:2 * n]
        send_sems, recv_sems, loc_sems = refs[2 * n:]
        x, y, c, me, chips = _place()
        local = [pltpu.make_async_copy(ins[k].at[me], outs[k].at[me], loc_sems.at[k]) for k in range(n)]
        sends = []
        for k in range(n):
            for p, (cx, cy) in enumerate(chips):
                j = 3 * k + p
                sends.append(_remote(ins[k].at[2 * cx + cy], outs[k].at[me], send_sems.at[j], recv_sems.at[j], (cx, cy, c)))
        for cp in local + sends:
            cp.start()
        for k in range(n):
            for p, (cx, cy) in enumerate(chips):
                j = 3 * k + p
                got = outs[k].at[2 * cx + cy]
                _remote(got, got, send_sems.at[j], recv_sems.at[j], (cx, cy, c)).wait_recv()
        for cp in sends:
            cp.wait_send()
        for cp in local:
            cp.wait()

    return pl.pallas_call(
        body, name="grad_chip_exchange",
        in_specs=[ANY] * n, out_specs=[ANY] * n,
        out_shape=[jax.ShapeDtypeStruct(a.shape, a.dtype) for a in sums],
        scratch_shapes=[pltpu.SemaphoreType.DMA((3 * n,)), pltpu.SemaphoreType.DMA((3 * n,)),
                        pltpu.SemaphoreType.DMA((n,))],
    )(*sums)


def _half_swap(totals):
    n = len(totals)

    def body(*refs):
        ins, outs = refs[:n], refs[n:2 * n]
        send_sems, recv_sems, loc_sems = refs[2 * n:]
        x, y, c, _, _ = _place()
        sibling = (x, y, 1 - c)
        local = [pltpu.make_async_copy(ins[k], outs[k].at[c], loc_sems.at[k]) for k in range(n)]
        sends = [_remote(ins[k], outs[k].at[c], send_sems.at[k], recv_sems.at[k], sibling) for k in range(n)]
        for cp in local + sends:
            cp.start()
        for k in range(n):
            got = outs[k].at[1 - c]
            _remote(got, got, send_sems.at[k], recv_sems.at[k], sibling).wait_recv()
        for cp in sends:
            cp.wait_send()
        for cp in local:
            cp.wait()

    return pl.pallas_call(
        body, name="grad_half_swap",
        in_specs=[ANY] * n, out_specs=[ANY] * n,
        out_shape=[jax.ShapeDtypeStruct((2,) + a.shape, a.dtype) for a in totals],
        scratch_shapes=[pltpu.SemaphoreType.DMA((n,))] * 3,
    )(*totals)


def _gather_small(packed):
    n_dev = 8

    def body(in_ref, out_ref, send_sems, recv_sems, loc_sem):
        x, y, c, _, _ = _place()
        me = 4 * x + 2 * y + c
        local = pltpu.make_async_copy(in_ref, out_ref.at[me], loc_sem)
        local.start()
        peers = []
        for k in range(1, n_dev):
            fx, fy, fc = (k >> 2) & 1, (k >> 1) & 1, k & 1
            peers.append((x ^ fx, y ^ fy, c ^ fc))
        sends = [_remote(in_ref, out_ref.at[me], send_sems.at[k], recv_sems.at[k], peer)
                 for k, peer in enumerate(peers)]
        for cp in sends:
            cp.start()
        for k, (px, py, pc) in enumerate(peers):
            got = out_ref.at[4 * px + 2 * py + pc]
            _remote(got, got, send_sems.at[k], recv_sems.at[k], (px, py, pc)).wait_recv()
        for cp in sends:
            cp.wait_send()
        local.wait()

    return pl.pallas_call(
        body, name="gather_small_grads",
        in_specs=[ANY], out_specs=ANY,
        out_shape=jax.ShapeDtypeStruct((n_dev,) + packed.shape, packed.dtype),
        scratch_shapes=[pltpu.SemaphoreType.DMA((n_dev - 1,)), pltpu.SemaphoreType.DMA((n_dev - 1,)),
                        pltpu.SemaphoreType.DMA],
    )(packed)


def _row_tile(r):
    return r // 4 if r >= 256 and (r // 4) % 16 == 0 else r


def _pair_sum(a, b, name):
    nb, r, c = a.shape

    def body(a_ref, b_ref, out_ref):
        out_ref[...] = (a_ref[...].astype(F32) + b_ref[...].astype(F32)).astype(BF16)

    blk = pl.BlockSpec((None, r, c), lambda j: (j, 0, 0))
    return pl.pallas_call(
        body, name=name, grid=(nb,), in_specs=[blk, blk], out_specs=blk,
        out_shape=jax.ShapeDtypeStruct(a.shape, BF16), compiler_params=_params(("arbitrary",)),
    )(a, b)


def _slot_sum(a, name):
    nb, r, c = a.shape
    tr = _row_tile(r)

    def body(a_ref, out_ref):
        total = a_ref[0].astype(F32)
        for j in range(1, nb):
            total = total + a_ref[j].astype(F32)
        out_ref[...] = total

    return pl.pallas_call(
        body, name=name, grid=(r // tr,),
        in_specs=[pl.BlockSpec((nb, tr, c), lambda i: (0, i, 0))],
        out_specs=pl.BlockSpec((tr, c), lambda i: (i, 0)),
        out_shape=jax.ShapeDtypeStruct((r, c), F32), compiler_params=_params(("arbitrary",)),
    )(a)


def _adamw(w, g, m, v, name):
    r, c = w.shape
    tr = _row_tile(r)
    c1 = 1.0 - ADAM_B1 ** ADAM_STEP
    c2 = 1.0 - ADAM_B2 ** ADAM_STEP

    def body(w_ref, g_ref, m_ref, v_ref, d_ref, m2_ref, v2_ref):
        gv = g_ref[...]
        m2 = ADAM_B1 * m_ref[...] + (1.0 - ADAM_B1) * gv
        v2 = ADAM_B2 * v_ref[...] + (1.0 - ADAM_B2) * (gv * gv)
        m2_ref[...] = m2
        v2_ref[...] = v2
        d_ref[...] = -ADAM_LR * ((m2 / c1) / (jnp.sqrt(v2 / c2) + ADAM_EPS) + ADAM_WD * w_ref[...])

    blk = pl.BlockSpec((tr, c), lambda i: (i, 0))
    return pl.pallas_call(
        body, name=name, grid=(r // tr,), in_specs=[blk] * 4, out_specs=[blk] * 3,
        out_shape=[jax.ShapeDtypeStruct((r, c), F32)] * 3, compiler_params=_params(("arbitrary",)),
    )(w, g, m, v)


WEIGHTS = ["ffn1_norm", "ffn1_w_gate", "ffn1_w_up", "ffn1_w_down", "mix_norm", "w_in", "conv_w", "conv_b",
           "rg_w_a", "rg_b_a", "rg_w_x", "rg_b_x", "rg_lambda", "q_norm", "k_norm", "rnn_out_norm",
           "attn_out_norm", "w_out", "ffn2_norm", "ffn2_w_gate", "ffn2_w_up", "ffn2_w_down"]
BIG = ["ffn1_w_gate", "ffn1_w_up", "ffn1_w_down", "w_in", "w_out", "ffn2_w_gate", "ffn2_w_up", "ffn2_w_down"]
SMALL = [n for n in WEIGHTS if n not in BIG]
PACK_LANES = 128
PACK_ROW_ALIGN = 8


def _pack(parts):
    flat = jnp.concatenate([p.reshape(-1) for p in parts])
    unit = PACK_LANES * PACK_ROW_ALIGN
    padded = -(-flat.shape[0] // unit) * unit
    return jnp.pad(flat, (0, padded - flat.shape[0])).reshape(-1, PACK_LANES)


def _unpack(packed, shapes):
    flat = packed.reshape(-1)
    out, at = [], 0
    for shp in shapes:
        size = math.prod(shp)
        out.append(flat[at:at + size].reshape(shp))
        at += size
    return out


def kernel(x, ffn1_norm, ffn1_w_gate, ffn1_w_up, ffn1_w_down, mix_norm, w_in, conv_w, conv_b, rg_w_a, rg_b_a, rg_w_x, rg_b_x, rg_lambda, q_norm, k_norm, rnn_out_norm, attn_out_norm, w_out, ffn2_norm, ffn2_w_gate, ffn2_w_up, ffn2_w_down, loss_target, m_ffn1_norm, m_ffn1_w_gate, m_ffn1_w_up, m_ffn1_w_down, m_mix_norm, m_w_in, m_conv_w, m_conv_b, m_rg_w_a, m_rg_b_a, m_rg_w_x, m_rg_b_x, m_rg_lambda, m_q_norm, m_k_norm, m_rnn_out_norm, m_attn_out_norm, m_w_out, m_ffn2_norm, m_ffn2_w_gate, m_ffn2_w_up, m_ffn2_w_down, v_ffn1_norm, v_ffn1_w_gate, v_ffn1_w_up, v_ffn1_w_down, v_mix_norm, v_w_in, v_conv_w, v_conv_b, v_rg_w_a, v_rg_b_a, v_rg_w_x, v_rg_b_x, v_rg_lambda, v_q_norm, v_k_norm, v_rnn_out_norm, v_attn_out_norm, v_w_out, v_ffn2_norm, v_ffn2_w_gate, v_ffn2_w_up, v_ffn2_w_down):
    given = dict(locals())
    w = {n: given[n] for n in WEIGHTS}
    m = {n: given["m_" + n] for n in WEIGHTS}
    v = {n: given["v_" + n] for n in WEIGHTS}
    chip = 2 * lax.axis_index("x") + lax.axis_index("y")

    shards = [w[n][0].astype(BF16) for n in BIG]
    gathered = _gather_weights(shards, [w["conv_w"][0]])
    big = dict(zip(BIG, gathered[:len(BIG)]))
    small = {n: (w[n][0] if w[n].ndim > 2 else w[n]) for n in SMALL}
    small["conv_w"] = jnp.transpose(gathered[-1], (1, 0, 2)).reshape(CONV_W, D_RNN)

    loss, grad_x, gb, gs = _local_step(x[0], loss_target[0], big, small)
    loss = lax.psum(loss, ("x", "y", "c"))

    theirs, mine = _pair_exchange([gb[n] for n in BIG])
    pair = [_pair_sum(a, b, "pair_sum_" + n) for n, a, b in zip(BIG, mine, theirs)]
    slots = _chip_exchange(pair)
    totals = [_slot_sum(a, "chip_sum_" + n) for n, a in zip(BIG, slots)]
    swapped = _half_swap(totals)
    grads, deltas, new_m, new_v = {}, {}, {}, {}
    for n, t in zip(BIG, swapped):
        shp = w[n].shape
        g2 = t.reshape(shp[1], shp[2])
        d2, m2, v2 = _adamw(w[n][0], g2, m[n][0], v[n][0], "adamw_" + n)
        grads[n], deltas[n], new_m[n], new_v[n] = g2.reshape(shp), d2.reshape(shp), m2.reshape(shp), v2.reshape(shp)

    full_shapes = [gs[n].shape for n in SMALL]
    everyone = _gather_small(_pack([gs[n] for n in SMALL]))
    g_small = _slot_sum(everyone, "small_grad_sum")
    g_parts = dict(zip(SMALL, _unpack(g_small, full_shapes)))
    quarter = D_RNN // N_CHIPS
    g_parts["conv_w"] = lax.dynamic_slice_in_dim(g_parts["conv_w"], chip * quarter, quarter, axis=1)
    local_shapes = [w[n].shape for n in SMALL]
    pk = lambda tree: _pack([tree[n] for n in SMALL])
    d_s, m_s, v_s = _adamw(pk(w), pk(g_parts), pk(m), pk(v), "adamw_small")
    for tree, packed in ((grads, pk(g_parts)), (deltas, d_s), (new_m, m_s), (new_v, v_s)):
        tree.update(zip(SMALL, _unpack(packed, local_shapes)))

    return (loss, grad_x.reshape(x.shape), *[grads[n] for n in WEIGHTS], *[deltas[n] for n in WEIGHTS],
            *[new_m[n] for n in WEIGHTS], *[new_v[n] for n in WEIGHTS])
```

```python
import functools
import math

import jax
import jax.numpy as jnp
from jax import lax
from jax.experimental import pallas as pl
from jax.experimental.pallas import tpu as pltpu

F32 = jnp.float32
BF16 = jnp.bfloat16
MESH = pl.DeviceIdType.MESH

D_MODEL = 1024
N_CHIPS = 4
D_RNN = 512
D_ATT = 512
N_HEADS = 8
HEAD_DIM = 64
RNN_BLOCKS = 8
CONV_W = 4
RG_C = 8.0
N_IN = 2 * D_RNN + 3 * D_ATT
EPS = 1e-6
ATT_BLOCK = 128
ATT_GROUP = 3
EXP_ZERO = -105.0

ADAM_LR = 0.001
ADAM_B1 = 0.9
ADAM_B2 = 0.999
ADAM_EPS = 1e-08
ADAM_WD = 0.01
ADAM_STEP = 10

V7X_VMEM_LIMIT = 56 * 1024 * 1024
TOKEN_TILE = 512

GELU_K0 = math.sqrt(2.0 / math.pi)
GELU_K1 = 0.044715


def _params(sem=None):
    return pltpu.CompilerParams(dimension_semantics=sem, vmem_limit_bytes=V7X_VMEM_LIMIT)


def _dot(a, b):
    return jnp.dot(a, b, preferred_element_type=F32)


def _dot_nt(a, b):
    return lax.dot_general(a, b, (((1,), (1,)), ((), ())), preferred_element_type=F32)


def _dot_tn(a, b):
    return lax.dot_general(a, b, (((0,), (0,)), ((), ())), preferred_element_type=F32)


def _sigmoid(x):
    return 1.0 / (1.0 + jnp.exp(-x))


def _rms_r(xv):
    return lax.rsqrt(jnp.mean(xv * xv, axis=-1, keepdims=True) + EPS)


def _rms_bwd(xv, r, nw, dh):
    t = dh * nw
    dx = r * t - xv * (r * r * r * jnp.mean(t * xv, axis=-1, keepdims=True))
    dn = jnp.sum(dh * xv * r, axis=0, keepdims=True)
    return dx, dn


def _gelu(x):
    t = jnp.tanh(GELU_K0 * (x + GELU_K1 * x * x * x))
    return 0.5 * x * (1.0 + t)


def _gelu_grad(x):
    t = jnp.tanh(GELU_K0 * (x + GELU_K1 * x * x * x))
    return 0.5 * (1.0 + t) + 0.5 * x * (1.0 - t * t) * (GELU_K0 * (1.0 + 3.0 * GELU_K1 * x * x))


def _expm1_neg(x):
    p = 1.0 + x * (1.0 / 8.0)
    for k in (7.0, 6.0, 5.0, 4.0, 3.0, 2.0):
        p = 1.0 + x * (1.0 / k) * p
    return jnp.where(x > -0.25, x * p, jnp.exp(x) - 1.0)


def _log_sigmoid(x):
    return jnp.minimum(x, 0.0) - jnp.log(1.0 + jnp.exp(-jnp.abs(x)))


def _tile(s):
    return min(TOKEN_TILE, s)


def _ffn_fwd(x, nw, wg, wu, wd, tgt=None):
    s, d = x.shape
    nb, _, fb = wg.shape
    tm = _tile(s)
    with_loss = tgt is not None

    def body(*refs):
        if with_loss:
            x_ref, nw_ref, wg_ref, wu_ref, wd_ref, tgt_ref, out_ref, g_ref, u_ref, hb_ref, ab_ref, loss_ref, hs, acc = refs
        else:
            x_ref, nw_ref, wg_ref, wu_ref, wd_ref, out_ref, g_ref, u_ref, hb_ref, ab_ref, hs, acc = refs
        i = pl.program_id(0)
        j = pl.program_id(1)

        @pl.when(j == 0)
        def _():
            xv = x_ref[...]
            hb = (xv * _rms_r(xv) * nw_ref[...]).astype(BF16)
            hs[...] = hb
            hb_ref[...] = hb
            acc[...] = jnp.zeros_like(acc)

        hb = hs[...]
        g = _dot(hb, wg_ref[...])
        u = _dot(hb, wu_ref[...])
        g_ref[...] = g
        u_ref[...] = u
        ab = (g * _sigmoid(g) * u).astype(BF16)
        ab_ref[...] = ab
        acc[...] += _dot(ab, wd_ref[...])

        @pl.when(j == nb - 1)
        def _():
            y = x_ref[...] + 0.5 * acc[...]
            if with_loss:
                diff = y - tgt_ref[...]
                out_ref[...] = diff * (1.0 / d)

                @pl.when(i == 0)
                def _():
                    loss_ref[...] = jnp.zeros_like(loss_ref)

                loss_ref[...] += jnp.sum(diff * diff) * (0.5 / d)
            else:
                out_ref[...] = y

    row = pl.BlockSpec((tm, d), lambda i, j: (i, 0))
    in_specs = [row, pl.BlockSpec((1, d), lambda i, j: (0, 0)),
                pl.BlockSpec((None, d, fb), lambda i, j: (j, 0, 0)),
                pl.BlockSpec((None, d, fb), lambda i, j: (j, 0, 0)),
                pl.BlockSpec((None, fb, d), lambda i, j: (j, 0, 0))]
    args = [x, nw, wg, wu, wd]
    if with_loss:
        in_specs.append(row)
        args.append(tgt)
    blk = pl.BlockSpec((None, tm, fb), lambda i, j: (j, i, 0))
    out_shape = [jax.ShapeDtypeStruct((s, d), F32), jax.ShapeDtypeStruct((nb, s, fb), F32),
                 jax.ShapeDtypeStruct((nb, s, fb), F32), jax.ShapeDtypeStruct((s, d), BF16),
                 jax.ShapeDtypeStruct((nb, s, fb), BF16)]
    out_specs = [row, blk, blk, row, blk]
    if with_loss:
        out_shape.append(jax.ShapeDtypeStruct((1, 128), F32))
        out_specs.append(pl.BlockSpec((1, 128), lambda i, j: (0, 0)))
    return pl.pallas_call(
        body, name="ffn_fwd_loss" if with_loss else "ffn_fwd",
        grid=(s // tm, nb), in_specs=in_specs, out_specs=out_specs, out_shape=out_shape,
        scratch_shapes=[pltpu.VMEM((tm, d), BF16), pltpu.VMEM((tm, d), F32)],
        compiler_params=_params(("arbitrary", "arbitrary")),
    )(*args)


def _ffn_bwd_act(x, nw, dy, g, u, wg, wu, wd, name):
    s, d = x.shape
    nb, _, fb = wg.shape
    tm = _tile(s)

    def body(x_ref, nw_ref, dy_ref, g_ref, u_ref, wg_ref, wu_ref, wd_ref,
             dx_ref, dg_ref, du_ref, dyb_ref, dnw_ref, dys, acc):
        i = pl.program_id(0)
        j = pl.program_id(1)

        @pl.when(j == 0)
        def _():
            dyb = dy_ref[...].astype(BF16)
            dys[...] = dyb
            dyb_ref[...] = dyb
            acc[...] = jnp.zeros_like(acc)

        da = 0.5 * _dot_nt(dys[...], wd_ref[...])
        gv = g_ref[...]
        sg = _sigmoid(gv)
        dub = (da * (gv * sg)).astype(BF16)
        dgb = (da * u_ref[...] * (sg * (1.0 + gv * (1.0 - sg)))).astype(BF16)
        dg_ref[...] = dgb
        du_ref[...] = dub
        acc[...] += _dot_nt(dgb, wg_ref[...]) + _dot_nt(dub, wu_ref[...])

        @pl.when(j == nb - 1)
        def _():
            xv = x_ref[...]
            dx, dn = _rms_bwd(xv, _rms_r(xv), nw_ref[...], acc[...])
            dx_ref[...] = dy_ref[...] + dx

            @pl.when(i == 0)
            def _():
                dnw_ref[...] = jnp.zeros_like(dnw_ref)

            dnw_ref[...] += dn

    row = pl.BlockSpec((tm, d), lambda i, j: (i, 0))
    vec = pl.BlockSpec((1, d), lambda i, j: (0, 0))
    blk = pl.BlockSpec((None, tm, fb), lambda i, j: (j, i, 0))
    wcol = pl.BlockSpec((None, d, fb), lambda i, j: (j, 0, 0))
    wrow = pl.BlockSpec((None, fb, d), lambda i, j: (j, 0, 0))
    return pl.pallas_call(
        body, name=name, grid=(s // tm, nb),
        in_specs=[row, vec, row, blk, blk, wcol, wcol, wrow],
        out_specs=[row, blk, blk, row, vec],
        out_shape=[jax.ShapeDtypeStruct((s, d), F32), jax.ShapeDtypeStruct((nb, s, fb), BF16),
                   jax.ShapeDtypeStruct((nb, s, fb), BF16), jax.ShapeDtypeStruct((s, d), BF16),
                   jax.ShapeDtypeStruct((1, d), F32)],
        scratch_shapes=[pltpu.VMEM((tm, d), BF16), pltpu.VMEM((tm, d), F32)],
        compiler_params=_params(("arbitrary", "arbitrary")),
    )(x, nw, dy, g, u, wg, wu, wd)


def _wgrad(a, b, a_spec, b_spec, out_rows, out_cols, scale, name):
    s = a.shape[-2]
    tk = _tile(s)
    nk = s // tk

    def body(a_ref, b_ref, out_ref, acc):
        k = pl.program_id(1)

        @pl.when(k == 0)
        def _():
            acc[...] = jnp.zeros_like(acc)

        acc[...] += _dot_tn(a_ref[...], b_ref[...])

        @pl.when(k == nk - 1)
        def _():
            out_ref[...] = (acc[...] * scale).astype(BF16)

    return pl.pallas_call(
        body, name=name, grid=(N_CHIPS, nk),
        in_specs=[a_spec(tk), b_spec(tk)],
        out_specs=pl.BlockSpec((None, out_rows, out_cols), lambda j, k: (j, 0, 0)),
        out_shape=jax.ShapeDtypeStruct((N_CHIPS, out_rows, out_cols), BF16),
        scratch_shapes=[pltpu.VMEM((out_rows, out_cols), F32)],
        compiler_params=_params(("arbitrary", "arbitrary")),
    )(a, b)


def _ffn_wgrads(hb, ab, dg, du, dyb, tag):
    s, d = hb.shape
    fb = ab.shape[-1]
    shared = lambda cols: (lambda tk: pl.BlockSpec((tk, cols), lambda j, k: (k, 0)))
    stacked = lambda cols: (lambda tk: pl.BlockSpec((None, tk, cols), lambda j, k: (j, k, 0)))
    dwg = _wgrad(hb, dg, shared(d), stacked(fb), d, fb, 1.0, "wgrad_gate_" + tag)
    dwu = _wgrad(hb, du, shared(d), stacked(fb), d, fb, 1.0, "wgrad_up_" + tag)
    dwd = _wgrad(ab, dyb, stacked(fb), shared(d), fb, d, 0.5, "wgrad_down_" + tag)
    return dwg, dwu, dwd


def _mix_pre(x, nw, win):
    s, d = x.shape
    nb, _, cb = win.shape
    tm = _tile(s)

    def body(x_ref, nw_ref, w_ref, p_ref, hb_ref, hs):
        @pl.when(pl.program_id(1) == 0)
        def _():
            xv = x_ref[...]
            hb = (xv * _rms_r(xv) * nw_ref[...]).astype(BF16)
            hs[...] = hb
            hb_ref[...] = hb

        p_ref[...] = _dot(hs[...], w_ref[...])

    row = pl.BlockSpec((tm, d), lambda i, j: (i, 0))
    return pl.pallas_call(
        body, name="mix_pre", grid=(s // tm, nb),
        in_specs=[row, pl.BlockSpec((1, d), lambda i, j: (0, 0)),
                  pl.BlockSpec((None, d, cb), lambda i, j: (j, 0, 0))],
        out_specs=[pl.BlockSpec((tm, cb), lambda i, j: (i, j)), row],
        out_shape=[jax.ShapeDtypeStruct((s, nb * cb), F32), jax.ShapeDtypeStruct((s, d), BF16)],
        scratch_shapes=[pltpu.VMEM((tm, d), BF16)],
        compiler_params=_params(("arbitrary", "arbitrary")),
    )(x, nw, win)


def _mix_pre_bwd(x, nw, dres, dpb, win):
    s, d = x.shape
    nb, _, cb = win.shape
    tm = _tile(s)

    def body(x_ref, nw_ref, dres_ref, dp_ref, w_ref, dx_ref, dnw_ref, acc):
        i = pl.program_id(0)
        j = pl.program_id(1)

        @pl.when(j == 0)
        def _():
            acc[...] = jnp.zeros_like(acc)

        acc[...] += _dot_nt(dp_ref[...], w_ref[...])

        @pl.when(j == nb - 1)
        def _():
            xv = x_ref[...]
            dx, dn = _rms_bwd(xv, _rms_r(xv), nw_ref[...], acc[...])
            dx_ref[...] = dres_ref[...] + dx

            @pl.when(i == 0)
            def _():
                dnw_ref[...] = jnp.zeros_like(dnw_ref)

            dnw_ref[...] += dn

    row = pl.BlockSpec((tm, d), lambda i, j: (i, 0))
    vec = pl.BlockSpec((1, d), lambda i, j: (0, 0))
    return pl.pallas_call(
        body, name="mix_pre_bwd", grid=(s // tm, nb),
        in_specs=[row, vec, row, pl.BlockSpec((tm, cb), lambda i, j: (i, j)),
                  pl.BlockSpec((None, d, cb), lambda i, j: (j, 0, 0))],
        out_specs=[row, vec],
        out_shape=[jax.ShapeDtypeStruct((s, d), F32), jax.ShapeDtypeStruct((1, d), F32)],
        scratch_shapes=[pltpu.VMEM((tm, d), F32)],
        compiler_params=_params(("arbitrary", "arbitrary")),
    )(x, nw, dres, dpb, win)


def _mix_post(x, yr, ya, nr, na, wout):
    s, d = x.shape
    h = yr.shape[1]
    tm = _tile(s)

    def body(x_ref, yr_ref, ya_ref, nr_ref, na_ref, w_ref, out_ref):
        yrv = yr_ref[...]
        yav = ya_ref[...]
        onb = (yrv * _rms_r(yrv) * nr_ref[...]).astype(BF16)
        oab = (yav * _rms_r(yav) * na_ref[...]).astype(BF16)
        out_ref[...] = x_ref[...] + _dot(onb, w_ref[0:h, :]) + _dot(oab, w_ref[h:2 * h, :])

    row = pl.BlockSpec((tm, d), lambda i: (i, 0))
    half = pl.BlockSpec((tm, h), lambda i: (i, 0))
    vec = pl.BlockSpec((1, h), lambda i: (0, 0))
    return pl.pallas_call(
        body, name="mix_post", grid=(s // tm,),
        in_specs=[row, half, half, vec, vec, pl.BlockSpec((2 * h, d), lambda i: (0, 0))],
        out_specs=row, out_shape=jax.ShapeDtypeStruct((s, d), F32),
        compiler_params=_params(("arbitrary",)),
    )(x, yr, ya, nr, na, wout)


def _mix_post_bwd(dx, yr, ya, nr, na, wout):
    s, d = dx.shape
    h = yr.shape[1]
    tm = _tile(s)

    def body(dx_ref, yr_ref, ya_ref, nr_ref, na_ref, w_ref,
             dyr_ref, dya_ref, yc_ref, dxb_ref, dnr_ref, dna_ref):
        i = pl.program_id(0)
        dxb = dx_ref[...].astype(BF16)
        dxb_ref[...] = dxb
        dyc = _dot_nt(dxb, w_ref[...])
        yrv = yr_ref[...]
        yav = ya_ref[...]
        rr = _rms_r(yrv)
        ra = _rms_r(yav)
        yc_ref[:, 0:h] = (yrv * rr * nr_ref[...]).astype(BF16)
        yc_ref[:, h:2 * h] = (yav * ra * na_ref[...]).astype(BF16)
        dyr, dnr = _rms_bwd(yrv, rr, nr_ref[...], dyc[:, 0:h])
        dya, dna = _rms_bwd(yav, ra, na_ref[...], dyc[:, h:2 * h])
        dyr_ref[...] = dyr
        dya_ref[...] = dya

        @pl.when(i == 0)
        def _():
            dnr_ref[...] = jnp.zeros_like(dnr_ref)
            dna_ref[...] = jnp.zeros_like(dna_ref)

        dnr_ref[...] += dnr
        dna_ref[...] += dna

    row = pl.BlockSpec((tm, d), lambda i: (i, 0))
    half = pl.BlockSpec((tm, h), lambda i: (i, 0))
    vec = pl.BlockSpec((1, h), lambda i: (0, 0))
    return pl.pallas_call(
        body, name="mix_post_bwd", grid=(s // tm,),
        in_specs=[row, half, half, vec, vec, pl.BlockSpec((2 * h, d), lambda i: (0, 0))],
        out_specs=[half, half, pl.BlockSpec((tm, 2 * h), lambda i: (i, 0)), row, vec, vec],
        out_shape=[jax.ShapeDtypeStruct((s, h), F32), jax.ShapeDtypeStruct((s, h), F32),
                   jax.ShapeDtypeStruct((s, 2 * h), BF16), jax.ShapeDtypeStruct((s, d), BF16),
                   jax.ShapeDtypeStruct((1, h), F32), jax.ShapeDtypeStruct((1, h), F32)],
        compiler_params=_params(("arbitrary",)),
    )(dx, yr, ya, nr, na, wout)


def _shift_down(xv, s, prev8):
    rolled = pltpu.roll(xv, s, 0)
    row8 = lax.broadcasted_iota(jnp.int32, prev8.shape, 0)
    head = jnp.where(row8 < s, pltpu.roll(prev8, s, 0), rolled[0:8, :])
    return jnp.concatenate([head, rolled[8:, :]], axis=0)


def _shift_up(xv, s, next8):
    n = xv.shape[0]
    rolled = pltpu.roll(xv, n - s, 0)
    row8 = lax.broadcasted_iota(jnp.int32, next8.shape, 0)
    tail = jnp.where(row8 >= 8 - s, pltpu.roll(next8, 8 - s, 0), rolled[n - 8:, :])
    return jnp.concatenate([rolled[:n - 8, :], tail], axis=0)


def _scan_fwd(a, b):
    n = a.shape[0]
    row = lax.broadcasted_iota(jnp.int32, a.shape, 0)
    s = 1
    while s < n:
        ok = row >= s
        b = jnp.where(ok, a * pltpu.roll(b, s, 0) + b, b)
        a = jnp.where(ok, a * pltpu.roll(a, s, 0), a)
        s *= 2
    return b


def _scan_bwd(a, b):
    n = a.shape[0]
    row = lax.broadcasted_iota(jnp.int32, a.shape, 0)
    s = 1
    while s < n:
        ok = row < n - s
        b = jnp.where(ok, a * pltpu.roll(b, n - s, 0) + b, b)
        a = jnp.where(ok, a * pltpu.roll(a, n - s, 0), a)
        s *= 2
    return b


def _rglru_gates(xv, prev8, cw_ref, cb_ref, wa_ref, ba_ref, wx_ref, bx_ref, lam_ref):
    x1 = _shift_down(xv, 1, prev8)
    x2 = _shift_down(xv, 2, prev8)
    x3 = _shift_down(xv, 3, prev8)
    xc = cw_ref[3:4, :] * xv + cw_ref[2:3, :] * x1 + cw_ref[1:2, :] * x2 + cw_ref[0:1, :] * x3 + cb_ref[...]
    xcb = xc.astype(BF16)
    r = _sigmoid(_dot(xcb, wa_ref[...]) + ba_ref[...])
    ig = _sigmoid(_dot(xcb, wx_ref[...]) + bx_ref[...])
    c = RG_C * _log_sigmoid(lam_ref[...])
    la = r * c
    a = jnp.exp(la)
    m = jnp.sqrt(-_expm1_neg(2.0 * la))
    return (x1, x2, x3), xc, xcb, r, ig, c, a, m


def _rglru_fwd(proj, cw, cb, wa, ba, wx, bx, lam):
    s = proj.shape[0]
    w = D_RNN
    tm = _tile(s)

    def body(xr_ref, gate_ref, cw_ref, cb_ref, wa_ref, ba_ref, wx_ref, bx_ref, lam_ref,
             y_ref, h_ref, prev, hlast):
        @pl.when(pl.program_id(0) == 0)
        def _():
            prev[...] = jnp.zeros_like(prev)
            hlast[...] = jnp.zeros_like(hlast)

        xv = xr_ref[...]
        _, xc, _, _, ig, _, a, m = _rglru_gates(xv, prev[...], cw_ref, cb_ref, wa_ref, ba_ref,
                                                wx_ref, bx_ref, lam_ref)
        b = m * (ig * xc)
        row = lax.broadcasted_iota(jnp.int32, b.shape, 0)
        b = jnp.where(row == 0, b + a * hlast[...], b)
        h = _scan_fwd(a, b)
        h_ref[...] = h
        y_ref[...] = h * _gelu(gate_ref[...])
        prev[...] = xv[tm - 8:, :]
        hlast[...] = h[tm - 1:tm, :]

    vec = pl.BlockSpec((1, w), lambda i: (0, 0))
    sq = pl.BlockSpec((w, w), lambda i: (0, 0))
    out = pl.BlockSpec((tm, w), lambda i: (i, 0))
    return pl.pallas_call(
        body, name="rglru_fwd", grid=(s // tm,),
        in_specs=[pl.BlockSpec((tm, w), lambda i: (i, 0)), pl.BlockSpec((tm, w), lambda i: (i, 1)),
                  pl.BlockSpec((CONV_W, w), lambda i: (0, 0)), vec, sq, vec, sq, vec, vec],
        out_specs=[out, out],
        out_shape=[jax.ShapeDtypeStruct((s, w), F32), jax.ShapeDtypeStruct((s, w), F32)],
        scratch_shapes=[pltpu.VMEM((8, w), F32), pltpu.VMEM((1, w), F32)],
        compiler_params=_params(("arbitrary",)),
    )(proj, proj, cw, cb, wa, ba, wx, bx, lam)


def _rglru_bwd(proj, hseq, dyr, cw, cb, wa, ba, wx, bx, lam):
    s = proj.shape[0]
    w = D_RNN
    tm = _tile(s)
    nt = s // tm
    t8 = tm // 8

    def body(xr_ref, xp_ref, gate_ref, h_ref, hp_ref, dy_ref, cw_ref, cb_ref, wa_ref, ba_ref,
             wx_ref, bx_ref, lam_ref,
             dxr_ref, dgate_ref, dcw_ref, dcb_ref, dwa_ref, dba_ref, dwx_ref, dbx_ref, dlam_ref,
             carry, dxc_next):
        i = pl.program_id(0)
        first_tile = i == nt - 1

        @pl.when(i == 0)
        def _():
            carry[...] = jnp.zeros_like(carry)
            dxc_next[...] = jnp.zeros_like(dxc_next)
            for ref in (dcw_ref, dcb_ref, dwa_ref, dba_ref, dwx_ref, dbx_ref, dlam_ref):
                ref[...] = jnp.zeros_like(ref)

        xv = xr_ref[...]
        prev8 = jnp.where(first_tile, 0.0, xp_ref[...])
        hprev8 = jnp.where(first_tile, 0.0, hp_ref[...])
        (x1, x2, x3), xc, xcb, r, ig, c, a, m = _rglru_gates(
            xv, prev8, cw_ref, cb_ref, wa_ref, ba_ref, wx_ref, bx_ref, lam_ref)
        gv = gate_ref[...]
        hv = h_ref[...]
        dy = dy_ref[...]
        dgate_ref[...] = (dy * hv * _gelu_grad(gv)).astype(BF16)
        dh = dy * _gelu(gv)
        row = lax.broadcasted_iota(jnp.int32, dh.shape, 0)
        dh = jnp.where(row == tm - 1, dh + carry[...], dh)
        a_up = jnp.where(row == tm - 1, 0.0, pltpu.roll(a, tm - 1, 0))
        lam_t = _scan_bwd(a_up, dh)
        carry[...] = a[0:1, :] * lam_t[0:1, :]
        hm1 = _shift_down(hv, 1, hprev8)
        da = lam_t * hm1
        ixc = ig * xc
        dm = lam_t * ixc
        dig = lam_t * m * xc
        dxc = lam_t * m * ig
        dla = da * a - dm * (a * a) / m
        dr = dla * c
        dlam_ref[...] += jnp.sum(dla * r, axis=0, keepdims=True)
        dpa = dr * r * (1.0 - r)
        dpi = dig * ig * (1.0 - ig)
        dba_ref[...] += jnp.sum(dpa, axis=0, keepdims=True)
        dbx_ref[...] += jnp.sum(dpi, axis=0, keepdims=True)
        dpab = dpa.astype(BF16)
        dpib = dpi.astype(BF16)
        dwa_ref[...] += _dot_tn(xcb, dpab)
        dwx_ref[...] += _dot_tn(xcb, dpib)
        dxc = dxc + _dot_nt(dpab, wa_ref[...]) + _dot_nt(dpib, wx_ref[...])
        dcb_ref[...] += jnp.sum(dxc, axis=0, keepdims=True)
        dcw_ref[3:4, :] += jnp.sum(dxc * xv, axis=0, keepdims=True)
        dcw_ref[2:3, :] += jnp.sum(dxc * x1, axis=0, keepdims=True)
        dcw_ref[1:2, :] += jnp.sum(dxc * x2, axis=0, keepdims=True)
        dcw_ref[0:1, :] += jnp.sum(dxc * x3, axis=0, keepdims=True)
        nxt = dxc_next[...]
        dxr = (cw_ref[3:4, :] * dxc + cw_ref[2:3, :] * _shift_up(dxc, 1, nxt)
               + cw_ref[1:2, :] * _shift_up(dxc, 2, nxt) + cw_ref[0:1, :] * _shift_up(dxc, 3, nxt))
        dxr_ref[...] = dxr.astype(BF16)
        dxc_next[...] = dxc[0:8, :]

        @pl.when(first_tile)
        def _():
            lv = lam_ref[...]
            dlam_ref[...] = dlam_ref[...] * (RG_C * _sigmoid(-lv))

    rev = lambda i: nt - 1 - i
    vec = pl.BlockSpec((1, w), lambda i: (0, 0))
    sq = pl.BlockSpec((w, w), lambda i: (0, 0))
    cur = lambda col: pl.BlockSpec((tm, w), lambda i: (rev(i), col))
    before = lambda cols: pl.BlockSpec((8, w), lambda i: (jnp.maximum(rev(i) * t8 - 1, 0), 0))
    return pl.pallas_call(
        body, name="rglru_bwd", grid=(nt,),
        in_specs=[cur(0), before(None), cur(1), cur(0), before(None), cur(0),
                  pl.BlockSpec((CONV_W, w), lambda i: (0, 0)), vec, sq, vec, sq, vec, vec],
        out_specs=[cur(0), cur(0), pl.BlockSpec((CONV_W, w), lambda i: (0, 0)), vec, sq, vec, sq, vec, vec],
        out_shape=[jax.ShapeDtypeStruct((s, w), BF16), jax.ShapeDtypeStruct((s, w), BF16),
                   jax.ShapeDtypeStruct((CONV_W, w), F32), jax.ShapeDtypeStruct((1, w), F32),
                   jax.ShapeDtypeStruct((w, w), F32), jax.ShapeDtypeStruct((1, w), F32),
                   jax.ShapeDtypeStruct((w, w), F32), jax.ShapeDtypeStruct((1, w), F32),
                   jax.ShapeDtypeStruct((1, w), F32)],
        scratch_shapes=[pltpu.VMEM((1, w), F32), pltpu.VMEM((8, w), F32)],
        compiler_params=_params(("arbitrary",)),
    )(proj, proj, proj, hseq, hseq, dyr, cw, cb, wa, ba, wx, bx, lam)


def _qk_prep(q_ref, k_ref, v_ref, qg_ref, kg_ref, qn, kn, vb, scale):
    qv = q_ref[...]
    qn[...] = (qv * _rms_r(qv) * qg_ref[...] * scale).astype(BF16)
    kv = k_ref[...]
    kn[...] = (kv * _rms_r(kv) * kg_ref[...]).astype(BF16)
    vb[...] = v_ref[...].astype(BF16)


def _sb_logs(z, valid):
    l1p = jnp.log(1.0 + jnp.exp(-jnp.abs(z)))
    lb = jnp.minimum(z, 0.0) - l1p
    lm = jnp.where(valid, -jnp.maximum(z, 0.0) - l1p, 0.0)
    return lb, lm


def _split_dot(xv, tri):
    hi = xv.astype(BF16)
    lo = (xv - hi.astype(F32)).astype(BF16)
    return _dot(hi, tri) + _dot(lo, tri)


def _key_mask(row, col, kj, qi, blk):
    shift = jnp.where(kj < qi, blk, 0)
    shift = jnp.where(kj >= 0, shift, -blk)
    return col < row + shift


def _attn_fwd(q, k, v, qg, kg):
    nh, s, dh = q.shape
    blk, grp = ATT_BLOCK, ATT_GROUP
    nq = s // blk
    scale = 1.0 / math.sqrt(dh)

    def body(q_ref, k_ref, v_ref, qg_ref, kg_ref, o_ref, qn, kn, vb):
        _qk_prep(q_ref, k_ref, v_ref, qg_ref, kg_ref, qn, kn, vb, scale)
        row = lax.broadcasted_iota(jnp.int32, (blk, blk), 0)
        col = lax.broadcasted_iota(jnp.int32, (blk, blk), 1)
        later = jnp.where(row > col, 1.0, 0.0).astype(BF16)

        def q_step(qi, _):
            qoff = pl.multiple_of(qi * blk, blk)
            qt = qn[pl.ds(qoff, blk), :]

            def more(carry):
                g, live, _, _ = carry
                return jnp.logical_and(g * grp <= qi, live > 0)

            def group(carry):
                g, _, acc, run = carry
                for b in range(grp):
                    kj = qi - g * grp - b
                    koff = pl.multiple_of(jnp.maximum(kj, 0) * blk, blk)
                    z = _dot_nt(qt, kn[pl.ds(koff, blk), :])
                    valid = _key_mask(row, col, kj, qi, blk)
                    lb, lm = _sb_logs(z, valid)
                    tail = _split_dot(lm, later) + run
                    wgt = jnp.where(valid, jnp.exp(lb + tail), 0.0)
                    acc = acc + _dot(wgt.astype(BF16), vb[pl.ds(koff, blk), :])
                    run = run + jnp.sum(lm, axis=1, keepdims=True)
                live = (jnp.max(run) > EXP_ZERO).astype(jnp.int32)
                return g + 1, live, acc, run

            _, _, acc, _ = lax.while_loop(
                more, group, (jnp.int32(0), jnp.int32(1), jnp.zeros((blk, dh), F32), jnp.zeros((blk, 1), F32)))
            o_ref[pl.ds(qoff, blk), :] = acc
            return 0

        lax.fori_loop(0, nq, q_step, 0)

    head = pl.BlockSpec((None, s, dh), lambda h: (h, 0, 0))
    vec = pl.BlockSpec((1, dh), lambda h: (0, 0))
    return pl.pallas_call(
        body, name="attn_fwd", grid=(nh,),
        in_specs=[head, head, head, vec, vec], out_specs=head,
        out_shape=jax.ShapeDtypeStruct((nh, s, dh), F32),
        scratch_shapes=[pltpu.VMEM((s, dh), BF16)] * 3,
        compiler_params=_params(("arbitrary",)),
    )(q, k, v, qg, kg)


def _attn_bwd(q, k, v, do, qg, kg):
    nh, s, dh = q.shape
    blk, grp = ATT_BLOCK, ATT_GROUP
    nq = s // blk
    scale = 1.0 / math.sqrt(dh)

    def body(q_ref, k_ref, v_ref, do_ref, qg_ref, kg_ref,
             dq_ref, dk_ref, dv_ref, dqg_ref, dkg_ref, qn, kn, vb, dob, runs, dqn, dkn):
        _qk_prep(q_ref, k_ref, v_ref, qg_ref, kg_ref, qn, kn, vb, scale)
        dob[...] = do_ref[...].astype(BF16)
        dkn[...] = jnp.zeros_like(dkn)
        dv_ref[...] = jnp.zeros_like(dv_ref)
        row = lax.broadcasted_iota(jnp.int32, (blk, blk), 0)
        col = lax.broadcasted_iota(jnp.int32, (blk, blk), 1)
        later = jnp.where(row > col, 1.0, 0.0).astype(BF16)
        earlier = jnp.where(row < col, 1.0, 0.0).astype(BF16)

        def q_step(qi, _):
            qoff = pl.multiple_of(qi * blk, blk)
            qt = qn[pl.ds(qoff, blk), :]
            dot = dob[pl.ds(qoff, blk), :]

            def more(carry):
                g, live, _ = carry
                return jnp.logical_and(g * grp <= qi, live > 0)

            def run_group(carry):
                g, _, run = carry
                for b in range(grp):
                    kj = qi - g * grp - b
                    koff = pl.multiple_of(jnp.maximum(kj, 0) * blk, blk)
                    runs[kj + grp - 1] = jnp.broadcast_to(run, (blk, blk))
                    z = _dot_nt(qt, kn[pl.ds(koff, blk), :])
                    _, lm = _sb_logs(z, _key_mask(row, col, kj, qi, blk))
                    run = run + jnp.sum(lm, axis=1, keepdims=True)
                live = (jnp.max(run) > EXP_ZERO).astype(jnp.int32)
                return g + 1, live, run

            groups, _, _ = lax.while_loop(more, run_group, (jnp.int32(0), jnp.int32(1), jnp.zeros((blk, 1), F32)))

            def k_group(gg, carry):
                dq_acc, esum = carry
                g = groups - 1 - gg
                for b in reversed(range(grp)):
                    kj = qi - g * grp - b
                    koff = pl.multiple_of(jnp.maximum(kj, 0) * blk, blk)
                    kt = kn[pl.ds(koff, blk), :]
                    vt = vb[pl.ds(koff, blk), :]
                    z = _dot_nt(qt, kt)
                    valid = _key_mask(row, col, kj, qi, blk)
                    lb, lm = _sb_logs(z, valid)
                    tail = _split_dot(lm, later) + runs[kj + grp - 1]
                    wgt = jnp.where(valid, jnp.exp(lb + tail), 0.0)
                    e = _dot_nt(dot, vt) * wgt
                    before = _split_dot(e, earlier) + esum
                    beta = jnp.exp(lb)
                    dz = jnp.where(valid, e * (1.0 - beta) - before * beta, 0.0)
                    dzb = dz.astype(BF16)
                    dq_acc = dq_acc + _dot(dzb, kt)
                    dkn[pl.ds(koff, blk), :] += _dot_tn(dzb, qt)
                    dv_ref[pl.ds(koff, blk), :] += _dot_tn(wgt.astype(BF16), dot)
                    esum = esum + jnp.sum(e, axis=1, keepdims=True)
                return dq_acc, esum

            dq_acc, _ = lax.fori_loop(0, groups, k_group,
                                      (jnp.zeros((blk, dh), F32), jnp.zeros((blk, 1), F32)))
            dqn[pl.ds(qoff, blk), :] = dq_acc
            return 0

        lax.fori_loop(0, nq, q_step, 0)

        @pl.when(pl.program_id(0) == 0)
        def _():
            dqg_ref[...] = jnp.zeros_like(dqg_ref)
            dkg_ref[...] = jnp.zeros_like(dkg_ref)

        qv = q_ref[...]
        dq, dqg = _rms_bwd(qv, _rms_r(qv), qg_ref[...] * scale, dqn[...])
        dq_ref[...] = dq
        dqg_ref[...] += dqg * scale
        kv = k_ref[...]
        dk, dkg = _rms_bwd(kv, _rms_r(kv), kg_ref[...], dkn[...])
        dk_ref[...] = dk
        dkg_ref[...] += dkg

    head = pl.BlockSpec((None, s, dh), lambda h: (h, 0, 0))
    vec = pl.BlockSpec((1, dh), lambda h: (0, 0))
    return pl.pallas_call(
        body, name="attn_bwd", grid=(nh,),
        in_specs=[head, head, head, head, vec, vec], out_specs=[head, head, head, vec, vec],
        out_shape=[jax.ShapeDtypeStruct((nh, s, dh), F32)] * 3 + [jax.ShapeDtypeStruct((1, dh), F32)] * 2,
        scratch_shapes=[pltpu.VMEM((s, dh), BF16)] * 4 + [pltpu.VMEM((nq + grp - 1, blk, blk), F32)]
        + [pltpu.VMEM((s, dh), F32)] * 2,
        compiler_params=_params(("arbitrary",)),
    )(q, k, v, do, qg, kg)


def _block_diag(w):
    n, c, d = w.shape
    return jnp.einsum("ncd,nm->ncmd", w, jnp.eye(n, dtype=w.dtype)).reshape(n * c, n * d)


def _diag_blocks(full, n):
    c = full.shape[0] // n
    return jnp.stack([full[i * c:(i + 1) * c, i * c:(i + 1) * c] for i in range(n)])


def _to_heads(t):
    s = t.shape[0]
    return jnp.transpose(t.reshape(s, N_HEADS, HEAD_DIM), (1, 0, 2))


def _from_heads(t):
    s = t.shape[1]
    return jnp.transpose(t, (1, 0, 2)).reshape(s, N_HEADS * HEAD_DIM)


def _local_step(x, tgt, big, small):
    wa = _block_diag(small["rg_w_a"]).astype(BF16)
    wx = _block_diag(small["rg_w_x"]).astype(BF16)
    wout = big["w_out"].reshape(D_MODEL, D_MODEL)
    rg = (small["conv_w"], small["conv_b"], wa, small["rg_b_a"], wx, small["rg_b_x"], small["rg_lambda"])

    x1, g1, u1, hb1, ab1 = _ffn_fwd(x, small["ffn1_norm"], big["ffn1_w_gate"], big["ffn1_w_up"], big["ffn1_w_down"])
    proj, hb2 = _mix_pre(x1, small["mix_norm"], big["w_in"])
    yr, hseq = _rglru_fwd(proj, *rg)
    qh = _to_heads(proj[:, 2 * D_RNN:2 * D_RNN + D_ATT])
    kh = _to_heads(proj[:, 2 * D_RNN + D_ATT:2 * D_RNN + 2 * D_ATT])
    vh = _to_heads(proj[:, 2 * D_RNN + 2 * D_ATT:])
    ya = _from_heads(_attn_fwd(qh, kh, vh, small["q_norm"], small["k_norm"]))
    x2 = _mix_post(x1, yr, ya, small["rnn_out_norm"], small["attn_out_norm"], wout)
    dx3, g2, u2, hb3, ab3, loss = _ffn_fwd(x2, small["ffn2_norm"], big["ffn2_w_gate"], big["ffn2_w_up"],
                                          big["ffn2_w_down"], tgt)

    gb, gs = {}, {}
    dx2, dg2, du2, dyb2, gs["ffn2_norm"] = _ffn_bwd_act(
        x2, small["ffn2_norm"], dx3, g2, u2, big["ffn2_w_gate"], big["ffn2_w_up"], big["ffn2_w_down"], "ffn2_bwd")
    gb["ffn2_w_gate"], gb["ffn2_w_up"], gb["ffn2_w_down"] = _ffn_wgrads(hb3, ab3, dg2, du2, dyb2, "ffn2")

    dyr, dya, ycat, dxb2, gs["rnn_out_norm"], gs["attn_out_norm"] = _mix_post_bwd(
        dx2, yr, ya, small["rnn_out_norm"], small["attn_out_norm"], wout)
    quarter = D_MODEL // N_CHIPS
    gb["w_out"] = _wgrad(ycat, dxb2, lambda tk: pl.BlockSpec((tk, quarter), lambda j, k: (k, j)),
                         lambda tk: pl.BlockSpec((tk, D_MODEL), lambda j, k: (k, 0)),
                         quarter, D_MODEL, 1.0, "wgrad_out")
    dqh, dkh, dvh, gs["q_norm"], gs["k_norm"] = _attn_bwd(qh, kh, vh, _to_heads(dya), small["q_norm"], small["k_norm"])
    dxr, dgate, gs["conv_w"], gs["conv_b"], dwa, gs["rg_b_a"], dwx, gs["rg_b_x"], gs["rg_lambda"] = _rglru_bwd(
        proj, hseq, dyr, *rg)
    gs["rg_w_a"] = _diag_blocks(dwa, RNN_BLOCKS)
    gs["rg_w_x"] = _diag_blocks(dwx, RNN_BLOCKS)
    dpb = jnp.concatenate([dxr, dgate, _from_heads(dqh).astype(BF16), _from_heads(dkh).astype(BF16),
                           _from_heads(dvh).astype(BF16)], axis=1)
    cb = N_IN // N_CHIPS
    gb["w_in"] = _wgrad(hb2, dpb, lambda tk: pl.BlockSpec((tk, D_MODEL), lambda j, k: (k, 0)),
                        lambda tk: pl.BlockSpec((tk, cb), lambda j, k: (k, j)),
                        D_MODEL, cb, 1.0, "wgrad_in")
    dx1, gs["mix_norm"] = _mix_pre_bwd(x1, small["mix_norm"], dx2, dpb, big["w_in"])

    dx0, dg1, du1, dyb1, gs["ffn1_norm"] = _ffn_bwd_act(
        x, small["ffn1_norm"], dx1, g1, u1, big["ffn1_w_gate"], big["ffn1_w_up"], big["ffn1_w_down"], "ffn1_bwd")
    gb["ffn1_w_gate"], gb["ffn1_w_up"], gb["ffn1_w_down"] = _ffn_wgrads(hb1, ab1, dg1, du1, dyb1, "ffn1")
    return loss[0, 0], dx0, gb, gs


ANY = pl.BlockSpec(memory_space=pl.ANY)


def _place():
    x, y, c = lax.axis_index("x"), lax.axis_index("y"), lax.axis_index("c")
    other_chips = [(1 - x, y), (x, 1 - y), (1 - x, 1 - y)]
    return x, y, c, 2 * x + y, other_chips


def _remote(src, dst, send_sem, recv_sem, to):
    return pltpu.make_async_remote_copy(src_ref=src, dst_ref=dst, send_sem=send_sem, recv_sem=recv_sem,
                                        device_id=to, device_id_type=MESH)


def _half(rows, c):
    return pl.ds(pl.multiple_of(c * rows, 16), rows)


def _gather_weights(split, whole):
    arrs = list(split) + list(whole)
    n, ns = len(arrs), len(split)

    def body(*refs):
        outs = refs[n:2 * n]
        send_sems, recv_sems, fsend_sems, frecv_sems = refs[2 * n:]
        x, y, c, me, chips = _place()
        sibling = (x, y, 1 - c)

        def region(i, chip, half):
            if i < ns:
                return outs[i].at[chip, _half(arrs[i].shape[1] // 2, half)]
            return outs[i].at[chip]

        sends = []
        for i in range(n):
            for p, chip in enumerate(chips):
                k = 3 * i + p
                mine = region(i, me, c)
                sends.append(_remote(mine, mine, send_sems.at[k], recv_sems.at[k], (*chip, c)))
        for cp in sends:
            cp.start()
        passed = []
        for i in range(n):
            for p, (cx, cy) in enumerate(chips):
                k = 3 * i + p
                got = region(i, 2 * cx + cy, c)
                _remote(got, got, send_sems.at[k], recv_sems.at[k], (cx, cy, c)).wait_recv()
                if i < ns:
                    fwd = _remote(got, got, fsend_sems.at[k], frecv_sems.at[k], sibling)
                    fwd.start()
                    passed.append(fwd)
        for i in range(ns):
            for p, (cx, cy) in enumerate(chips):
                k = 3 * i + p
                got = region(i, 2 * cx + cy, 1 - c)
                _remote(got, got, fsend_sems.at[k], frecv_sems.at[k], sibling).wait_recv()
        for cp in sends + passed:
            cp.wait_send()

    return pl.pallas_call(
        body, name="gather_weights",
        in_specs=[ANY] * n, out_specs=[ANY] * n,
        out_shape=[jax.ShapeDtypeStruct(a.shape, a.dtype) for a in arrs],
        input_output_aliases={i: i for i in range(n)},
        scratch_shapes=[pltpu.SemaphoreType.DMA((3 * n,)), pltpu.SemaphoreType.DMA((3 * n,)),
                        pltpu.SemaphoreType.DMA((3 * ns,)), pltpu.SemaphoreType.DMA((3 * ns,))],
    )(*arrs)


def _pair_exchange(grads):
    n = len(grads)

    def body(*refs):
        ins, theirs = refs[:n], refs[n:2 * n]
        send_sems, recv_sems = refs[2 * n:]
        x, y, c, _, _ = _place()
        sibling = (x, y, 1 - c)
        sends = [_remote(ins[k].at[:, _half(grads[k].shape[1] // 2, 1 - c)], theirs[k],
                         send_sems.at[k], recv_sems.at[k], sibling) for k in range(n)]
        for cp in sends:
            cp.start()
        for k in range(n):
            _remote(theirs[k], theirs[k], send_sems.at[k], recv_sems.at[k], sibling).wait_recv()
        for cp in sends:
            cp.wait_send()

    return pl.pallas_call(
        body, name="grad_pair_exchange",
        in_specs=[ANY] * n, out_specs=[ANY] * n,
        out_shape=[jax.ShapeDtypeStruct((g.shape[0], g.shape[1] // 2, g.shape[2]), g.dtype) for g in grads],
        scratch_shapes=[pltpu.SemaphoreType.DMA((n,))] * 2,
    )(*grads)


def _chip_exchange(sums, slots):
    n = len(sums)

    def body(*refs):
        ins, outs = refs[:n], refs[2 * n:3 * n]
        send_sems, recv_sems = refs[3 * n:]
        x, y, c, me, chips = _place()
        sends = []
        for k in range(n):
            for p, (cx, cy) in enumerate(chips):
                j = 3 * k + p
                sends.append(_remote(ins[k].at[2 * cx + cy], outs[k].at[me], send_sems.at[j], recv_sems.at[j], (cx, cy, c)))
        for cp in sends:
            cp.start()
        for k in range(n):
            for p, (cx, cy) in enumerate(chips):
                j = 3 * k + p
                got = outs[k].at[2 * cx + cy]
                _remote(got, got, send_sems.at[j], recv_sems.at[j], (cx, cy, c)).wait_recv()
        for cp in sends:
            cp.wait_send()

    return pl.pallas_call(
        body, name="grad_chip_exchange",
        in_specs=[ANY] * (2 * n), out_specs=[ANY] * n,
        out_shape=[jax.ShapeDtypeStruct(a.shape, a.dtype) for a in slots],
        input_output_aliases={n + k: k for k in range(n)},
        scratch_shapes=[pltpu.SemaphoreType.DMA((3 * n,)), pltpu.SemaphoreType.DMA((3 * n,))],
    )(*sums, *slots)


def _half_swap(halves):
    n = len(halves)

    def body(*refs):
        outs = refs[n:2 * n]
        send_sems, recv_sems = refs[2 * n:]
        x, y, c, _, _ = _place()
        sibling = (x, y, 1 - c)
        sends = [_remote(outs[k].at[c], outs[k].at[c], send_sems.at[k], recv_sems.at[k], sibling) for k in range(n)]
        for cp in sends:
            cp.start()
        for k in range(n):
            got = outs[k].at[1 - c]
            _remote(got, got, send_sems.at[k], recv_sems.at[k], sibling).wait_recv()
        for cp in sends:
            cp.wait_send()

    return pl.pallas_call(
        body, name="grad_half_swap",
        in_specs=[ANY] * n, out_specs=[ANY] * n,
        out_shape=[jax.ShapeDtypeStruct(a.shape, a.dtype) for a in halves],
        input_output_aliases={k: k for k in range(n)},
        scratch_shapes=[pltpu.SemaphoreType.DMA((n,))] * 2,
    )(*halves)


def _gather_small(packed):
    n_dev = 8

    def body(in_ref, out_ref, send_sems, recv_sems, loc_sem):
        x, y, c, _, _ = _place()
        me = 4 * x + 2 * y + c
        local = pltpu.make_async_copy(in_ref, out_ref.at[me], loc_sem)
        local.start()
        peers = []
        for k in range(1, n_dev):
            fx, fy, fc = (k >> 2) & 1, (k >> 1) & 1, k & 1
            peers.append((x ^ fx, y ^ fy, c ^ fc))
        sends = [_remote(in_ref, out_ref.at[me], send_sems.at[k], recv_sems.at[k], peer)
                 for k, peer in enumerate(peers)]
        for cp in sends:
            cp.start()
        for k, (px, py, pc) in enumerate(peers):
            got = out_ref.at[4 * px + 2 * py + pc]
            _remote(got, got, send_sems.at[k], recv_sems.at[k], (px, py, pc)).wait_recv()
        for cp in sends:
            cp.wait_send()
        local.wait()

    return pl.pallas_call(
        body, name="gather_small_grads",
        in_specs=[ANY], out_specs=ANY,
        out_shape=jax.ShapeDtypeStruct((n_dev,) + packed.shape, packed.dtype),
        scratch_shapes=[pltpu.SemaphoreType.DMA((n_dev - 1,)), pltpu.SemaphoreType.DMA((n_dev - 1,)),
                        pltpu.SemaphoreType.DMA],
    )(packed)


def _row_tile(r):
    return r // 4 if r >= 256 and (r // 4) % 16 == 0 else r


def _prefetch_call(body, name, grid, in_specs, out_specs, out_shape):
    spec = pltpu.PrefetchScalarGridSpec(num_scalar_prefetch=1, grid=grid, in_specs=in_specs, out_specs=out_specs)
    return pl.pallas_call(body, name=name, grid_spec=spec, out_shape=out_shape,
                          compiler_params=_params(("arbitrary",) * len(grid)))


def _place_shard(w2d, where, dtype, name):
    r, c = w2d.shape
    tr = _row_tile(r)

    def body(where_ref, w_ref, out_ref):
        out_ref[...] = w_ref[...].astype(dtype)

    return _prefetch_call(
        body, name, (r // tr,), [pl.BlockSpec((tr, c), lambda i, s: (i, 0))],
        pl.BlockSpec((None, tr, c), lambda i, s: (s[1], i, 0)),
        jax.ShapeDtypeStruct((N_CHIPS, r, c), dtype))(where, w2d)


def _pair_sum(full, theirs, where, name):
    nb, hs, c = theirs.shape

    def body(where_ref, a_ref, b_ref, out_ref, own_ref):
        total = (a_ref[...].astype(F32) + b_ref[...].astype(F32)).astype(BF16)
        out_ref[...] = total

        @pl.when(pl.program_id(0) == where_ref[1])
        def _():
            own_ref[...] = total

    blk = pl.BlockSpec((None, hs, c), lambda j, s: (j, 0, 0))
    shape = jax.ShapeDtypeStruct(theirs.shape, BF16)
    return _prefetch_call(
        body, name, (nb,), [pl.BlockSpec((None, hs, c), lambda j, s: (j, s[0], 0)), blk],
        [blk, pl.BlockSpec((None, hs, c), lambda j, s: (s[1], 0, 0))], [shape, shape])(where, full, theirs)


def _chip_sum(slots, where, name):
    nb, hs, c = slots.shape
    tr = _row_tile(hs)

    def body(where_ref, a_ref, out_ref):
        total = a_ref[0].astype(F32)
        for j in range(1, nb):
            total = total + a_ref[j].astype(F32)
        out_ref[...] = total

    return _prefetch_call(
        body, name, (hs // tr,), [pl.BlockSpec((nb, tr, c), lambda i, s: (0, i, 0))],
        pl.BlockSpec((None, tr, c), lambda i, s: (s[0], i, 0)),
        jax.ShapeDtypeStruct((2, hs, c), F32))(where, slots)


def _slot_sum(a, name):
    nb, r, c = a.shape
    tr = _row_tile(r)

    def body(a_ref, out_ref):
        total = a_ref[0].astype(F32)
        for j in range(1, nb):
            total = total + a_ref[j].astype(F32)
        out_ref[...] = total

    return pl.pallas_call(
        body, name=name, grid=(r // tr,),
        in_specs=[pl.BlockSpec((nb, tr, c), lambda i: (0, i, 0))],
        out_specs=pl.BlockSpec((tr, c), lambda i: (i, 0)),
        out_shape=jax.ShapeDtypeStruct((r, c), F32), compiler_params=_params(("arbitrary",)),
    )(a)


def _adamw(w, g, m, v, name):
    r, c = w.shape
    tr = _row_tile(r)
    c1 = 1.0 - ADAM_B1 ** ADAM_STEP
    c2 = 1.0 - ADAM_B2 ** ADAM_STEP

    def body(w_ref, g_ref, m_ref, v_ref, d_ref, m2_ref, v2_ref):
        gv = g_ref[...]
        m2 = ADAM_B1 * m_ref[...] + (1.0 - ADAM_B1) * gv
        v2 = ADAM_B2 * v_ref[...] + (1.0 - ADAM_B2) * (gv * gv)
        m2_ref[...] = m2
        v2_ref[...] = v2
        d_ref[...] = -ADAM_LR * ((m2 / c1) / (jnp.sqrt(v2 / c2) + ADAM_EPS) + ADAM_WD * w_ref[...])

    blk = pl.BlockSpec((tr, c), lambda i: (i, 0))
    return pl.pallas_call(
        body, name=name, grid=(r // tr,), in_specs=[blk] * 4, out_specs=[blk] * 3,
        out_shape=[jax.ShapeDtypeStruct((r, c), F32)] * 3, compiler_params=_params(("arbitrary",)),
    )(w, g, m, v)


WEIGHTS = ["ffn1_norm", "ffn1_w_gate", "ffn1_w_up", "ffn1_w_down", "mix_norm", "w_in", "conv_w", "conv_b",
           "rg_w_a", "rg_b_a", "rg_w_x", "rg_b_x", "rg_lambda", "q_norm", "k_norm", "rnn_out_norm",
           "attn_out_norm", "w_out", "ffn2_norm", "ffn2_w_gate", "ffn2_w_up", "ffn2_w_down"]
BIG = ["ffn1_w_gate", "ffn1_w_up", "ffn1_w_down", "w_in", "w_out", "ffn2_w_gate", "ffn2_w_up", "ffn2_w_down"]
SMALL = [n for n in WEIGHTS if n not in BIG]
PACK_LANES = 128
PACK_ROW_ALIGN = 8


def _pack(parts):
    flat = jnp.concatenate([p.reshape(-1) for p in parts])
    unit = PACK_LANES * PACK_ROW_ALIGN
    padded = -(-flat.shape[0] // unit) * unit
    return jnp.pad(flat, (0, padded - flat.shape[0])).reshape(-1, PACK_LANES)


def _unpack(packed, shapes):
    flat = packed.reshape(-1)
    out, at = [], 0
    for shp in shapes:
        size = math.prod(shp)
        out.append(flat[at:at + size].reshape(shp))
        at += size
    return out


def kernel(x, ffn1_norm, ffn1_w_gate, ffn1_w_up, ffn1_w_down, mix_norm, w_in, conv_w, conv_b, rg_w_a, rg_b_a, rg_w_x, rg_b_x, rg_lambda, q_norm, k_norm, rnn_out_norm, attn_out_norm, w_out, ffn2_norm, ffn2_w_gate, ffn2_w_up, ffn2_w_down, loss_target, m_ffn1_norm, m_ffn1_w_gate, m_ffn1_w_up, m_ffn1_w_down, m_mix_norm, m_w_in, m_conv_w, m_conv_b, m_rg_w_a, m_rg_b_a, m_rg_w_x, m_rg_b_x, m_rg_lambda, m_q_norm, m_k_norm, m_rnn_out_norm, m_attn_out_norm, m_w_out, m_ffn2_norm, m_ffn2_w_gate, m_ffn2_w_up, m_ffn2_w_down, v_ffn1_norm, v_ffn1_w_gate, v_ffn1_w_up, v_ffn1_w_down, v_mix_norm, v_w_in, v_conv_w, v_conv_b, v_rg_w_a, v_rg_b_a, v_rg_w_x, v_rg_b_x, v_rg_lambda, v_q_norm, v_k_norm, v_rnn_out_norm, v_attn_out_norm, v_w_out, v_ffn2_norm, v_ffn2_w_gate, v_ffn2_w_up, v_ffn2_w_down):
    given = dict(locals())
    w = {n: given[n] for n in WEIGHTS}
    m = {n: given["m_" + n] for n in WEIGHTS}
    v = {n: given["v_" + n] for n in WEIGHTS}
    chip = 2 * lax.axis_index("x") + lax.axis_index("y")

    where = jnp.stack([lax.axis_index("c"), chip]).astype(jnp.int32)

    stacks = [_place_shard(w[n][0], where, BF16, "place_" + n) for n in BIG]
    gathered = _gather_weights(stacks, [_place_shard(w["conv_w"][0], where, F32, "place_conv_w")])
    big = dict(zip(BIG, gathered[:len(BIG)]))
    small = {n: (w[n][0] if w[n].ndim > 2 else w[n]) for n in SMALL}
    small["conv_w"] = jnp.transpose(gathered[-1], (1, 0, 2)).reshape(CONV_W, D_RNN)

    loss, grad_x, gb, gs = _local_step(x[0], loss_target[0], big, small)
    loss = lax.psum(loss, ("x", "y", "c"))

    theirs = _pair_exchange([gb[n] for n in BIG])
    pair, own = zip(*[_pair_sum(gb[n], t, where, "pair_sum_" + n) for n, t in zip(BIG, theirs)])
    slots = _chip_exchange(list(pair), list(own))
    swapped = _half_swap([_chip_sum(a, where, "chip_sum_" + n) for n, a in zip(BIG, slots)])
    grads, deltas, new_m, new_v = {}, {}, {}, {}
    for n, t in zip(BIG, swapped):
        shp = w[n].shape
        g2 = t.reshape(shp[1], shp[2])
        d2, m2, v2 = _adamw(w[n][0], g2, m[n][0], v[n][0], "adamw_" + n)
        grads[n], deltas[n], new_m[n], new_v[n] = g2.reshape(shp), d2.reshape(shp), m2.reshape(shp), v2.reshape(shp)

    full_shapes = [gs[n].shape for n in SMALL]
    everyone = _gather_small(_pack([gs[n] for n in SMALL]))
    g_small = _slot_sum(everyone, "small_grad_sum")
    g_parts = dict(zip(SMALL, _unpack(g_small, full_shapes)))
    quarter = D_RNN // N_CHIPS
    g_parts["conv_w"] = lax.dynamic_slice_in_dim(g_parts["conv_w"], chip * quarter, quarter, axis=1)
    local_shapes = [w[n].shape for n in SMALL]
    pk = lambda tree: _pack([tree[n] for n in SMALL])
    d_s, m_s, v_s = _adamw(pk(w), pk(g_parts), pk(m), pk(v), "adamw_small")
    for tree, packed in ((grads, pk(g_parts)), (deltas, d_s), (new_m, m_s), (new_v, v_s)):
        tree.update(zip(SMALL, _unpack(packed, local_shapes)))

    return (loss, grad_x.reshape(x.shape), *[grads[n] for n in WEIGHTS], *[deltas[n] for n in WEIGHTS],
            *[new_m[n] for n in WEIGHTS], *[new_v[n] for n in WEIGHTS])
```

```python
import functools
import math

import jax
import jax.numpy as jnp
from jax import lax
from jax.experimental import pallas as pl
from jax.experimental.pallas import tpu as pltpu

F32 = jnp.float32
BF16 = jnp.bfloat16
MESH = pl.DeviceIdType.MESH

D_MODEL = 1024
N_CHIPS = 4
D_RNN = 512
D_ATT = 512
N_HEADS = 8
HEAD_DIM = 64
RNN_BLOCKS = 8
CONV_W = 4
RG_C = 8.0
N_IN = 2 * D_RNN + 3 * D_ATT
EPS = 1e-6
ATT_BLOCK = 128
ATT_WINDOW = 384
ATT_SPLIT = 256
EXP_ZERO = -105.0

ADAM_LR = 0.001
ADAM_B1 = 0.9
ADAM_B2 = 0.999
ADAM_EPS = 1e-08
ADAM_WD = 0.01
ADAM_STEP = 10

V7X_VMEM_LIMIT = 56 * 1024 * 1024
TOKEN_TILE = 512

GELU_K0 = math.sqrt(2.0 / math.pi)
GELU_K1 = 0.044715


def _params(sem=None):
    return pltpu.CompilerParams(dimension_semantics=sem, vmem_limit_bytes=V7X_VMEM_LIMIT)


def _dot(a, b):
    return jnp.dot(a, b, preferred_element_type=F32)


def _dot_nt(a, b):
    return lax.dot_general(a, b, (((1,), (1,)), ((), ())), preferred_element_type=F32)


def _dot_tn(a, b):
    return lax.dot_general(a, b, (((0,), (0,)), ((), ())), preferred_element_type=F32)


def _sigmoid(x):
    return 1.0 / (1.0 + jnp.exp(-x))


def _rms_r(xv):
    return lax.rsqrt(jnp.mean(xv * xv, axis=-1, keepdims=True) + EPS)


def _rms_bwd(xv, r, nw, dh):
    t = dh * nw
    dx = r * t - xv * (r * r * r * jnp.mean(t * xv, axis=-1, keepdims=True))
    dn = jnp.sum(dh * xv * r, axis=0, keepdims=True)
    return dx, dn


def _gelu(x):
    t = jnp.tanh(GELU_K0 * (x + GELU_K1 * x * x * x))
    return 0.5 * x * (1.0 + t)


def _gelu_grad(x):
    t = jnp.tanh(GELU_K0 * (x + GELU_K1 * x * x * x))
    return 0.5 * (1.0 + t) + 0.5 * x * (1.0 - t * t) * (GELU_K0 * (1.0 + 3.0 * GELU_K1 * x * x))


def _expm1_neg(x):
    p = 1.0 + x * (1.0 / 8.0)
    for k in (7.0, 6.0, 5.0, 4.0, 3.0, 2.0):
        p = 1.0 + x * (1.0 / k) * p
    return jnp.where(x > -0.25, x * p, jnp.exp(x) - 1.0)


def _log_sigmoid(x):
    return jnp.minimum(x, 0.0) - jnp.log(1.0 + jnp.exp(-jnp.abs(x)))


def _tile(s):
    return min(TOKEN_TILE, s)


def _ffn_fwd(x, nw, wg, wu, wd, tgt=None):
    s, d = x.shape
    nb, _, fb = wg.shape
    tm = _tile(s)
    with_loss = tgt is not None

    def body(*refs):
        if with_loss:
            x_ref, nw_ref, wg_ref, wu_ref, wd_ref, tgt_ref, out_ref, g_ref, u_ref, hb_ref, ab_ref, loss_ref, hs, acc = refs
        else:
            x_ref, nw_ref, wg_ref, wu_ref, wd_ref, out_ref, g_ref, u_ref, hb_ref, ab_ref, hs, acc = refs
        i = pl.program_id(0)
        j = pl.program_id(1)

        @pl.when(j == 0)
        def _():
            xv = x_ref[...]
            hb = (xv * _rms_r(xv) * nw_ref[...]).astype(BF16)
            hs[...] = hb
            hb_ref[...] = hb
            acc[...] = jnp.zeros_like(acc)

        hb = hs[...]
        g = _dot(hb, wg_ref[...])
        u = _dot(hb, wu_ref[...])
        g_ref[...] = g
        u_ref[...] = u
        ab = (g * _sigmoid(g) * u).astype(BF16)
        ab_ref[...] = ab
        acc[...] += _dot(ab, wd_ref[...])

        @pl.when(j == nb - 1)
        def _():
            y = x_ref[...] + 0.5 * acc[...]
            if with_loss:
                diff = y - tgt_ref[...]
                out_ref[...] = diff * (1.0 / d)

                @pl.when(i == 0)
                def _():
                    loss_ref[...] = jnp.zeros_like(loss_ref)

                loss_ref[...] += jnp.sum(diff * diff) * (0.5 / d)
            else:
                out_ref[...] = y

    row = pl.BlockSpec((tm, d), lambda i, j: (i, 0))
    in_specs = [row, pl.BlockSpec((1, d), lambda i, j: (0, 0)),
                pl.BlockSpec((None, d, fb), lambda i, j: (j, 0, 0)),
                pl.BlockSpec((None, d, fb), lambda i, j: (j, 0, 0)),
                pl.BlockSpec((None, fb, d), lambda i, j: (j, 0, 0))]
    args = [x, nw, wg, wu, wd]
    if with_loss:
        in_specs.append(row)
        args.append(tgt)
    blk = pl.BlockSpec((None, tm, fb), lambda i, j: (j, i, 0))
    out_shape = [jax.ShapeDtypeStruct((s, d), F32), jax.ShapeDtypeStruct((nb, s, fb), F32),
                 jax.ShapeDtypeStruct((nb, s, fb), F32), jax.ShapeDtypeStruct((s, d), BF16),
                 jax.ShapeDtypeStruct((nb, s, fb), BF16)]
    out_specs = [row, blk, blk, row, blk]
    if with_loss:
        out_shape.append(jax.ShapeDtypeStruct((1, 128), F32))
        out_specs.append(pl.BlockSpec((1, 128), lambda i, j: (0, 0)))
    return pl.pallas_call(
        body, name="ffn_fwd_loss" if with_loss else "ffn_fwd",
        grid=(s // tm, nb), in_specs=in_specs, out_specs=out_specs, out_shape=out_shape,
        scratch_shapes=[pltpu.VMEM((tm, d), BF16), pltpu.VMEM((tm, d), F32)],
        compiler_params=_params(("arbitrary", "arbitrary")),
    )(*args)


def _ffn_bwd_act(x, nw, dy, g, u, wg, wu, wd, name):
    s, d = x.shape
    nb, _, fb = wg.shape
    tm = _tile(s)

    def body(x_ref, nw_ref, dy_ref, g_ref, u_ref, wg_ref, wu_ref, wd_ref,
             dx_ref, dg_ref, du_ref, dyb_ref, dnw_ref, dys, acc):
        i = pl.program_id(0)
        j = pl.program_id(1)

        @pl.when(j == 0)
        def _():
            dyb = dy_ref[...].astype(BF16)
            dys[...] = dyb
            dyb_ref[...] = dyb
            acc[...] = jnp.zeros_like(acc)

        da = 0.5 * _dot_nt(dys[...], wd_ref[...])
        gv = g_ref[...]
        sg = _sigmoid(gv)
        dub = (da * (gv * sg)).astype(BF16)
        dgb = (da * u_ref[...] * (sg * (1.0 + gv * (1.0 - sg)))).astype(BF16)
        dg_ref[...] = dgb
        du_ref[...] = dub
        acc[...] += _dot_nt(dgb, wg_ref[...]) + _dot_nt(dub, wu_ref[...])

        @pl.when(j == nb - 1)
        def _():
            xv = x_ref[...]
            dx, dn = _rms_bwd(xv, _rms_r(xv), nw_ref[...], acc[...])
            dx_ref[...] = dy_ref[...] + dx

            @pl.when(i == 0)
            def _():
                dnw_ref[...] = jnp.zeros_like(dnw_ref)

            dnw_ref[...] += dn

    row = pl.BlockSpec((tm, d), lambda i, j: (i, 0))
    vec = pl.BlockSpec((1, d), lambda i, j: (0, 0))
    blk = pl.BlockSpec((None, tm, fb), lambda i, j: (j, i, 0))
    wcol = pl.BlockSpec((None, d, fb), lambda i, j: (j, 0, 0))
    wrow = pl.BlockSpec((None, fb, d), lambda i, j: (j, 0, 0))
    return pl.pallas_call(
        body, name=name, grid=(s // tm, nb),
        in_specs=[row, vec, row, blk, blk, wcol, wcol, wrow],
        out_specs=[row, blk, blk, row, vec],
        out_shape=[jax.ShapeDtypeStruct((s, d), F32), jax.ShapeDtypeStruct((nb, s, fb), BF16),
                   jax.ShapeDtypeStruct((nb, s, fb), BF16), jax.ShapeDtypeStruct((s, d), BF16),
                   jax.ShapeDtypeStruct((1, d), F32)],
        scratch_shapes=[pltpu.VMEM((tm, d), BF16), pltpu.VMEM((tm, d), F32)],
        compiler_params=_params(("arbitrary", "arbitrary")),
    )(x, nw, dy, g, u, wg, wu, wd)


def _wgrad(a, b, a_spec, b_spec, out_rows, out_cols, scale, name):
    s = a.shape[-2]
    tk = _tile(s)
    nk = s // tk

    def body(a_ref, b_ref, out_ref, acc):
        k = pl.program_id(1)

        @pl.when(k == 0)
        def _():
            acc[...] = jnp.zeros_like(acc)

        acc[...] += _dot_tn(a_ref[...], b_ref[...])

        @pl.when(k == nk - 1)
        def _():
            out_ref[...] = (acc[...] * scale).astype(BF16)

    return pl.pallas_call(
        body, name=name, grid=(N_CHIPS, nk),
        in_specs=[a_spec(tk), b_spec(tk)],
        out_specs=pl.BlockSpec((None, out_rows, out_cols), lambda j, k: (j, 0, 0)),
        out_shape=jax.ShapeDtypeStruct((N_CHIPS, out_rows, out_cols), BF16),
        scratch_shapes=[pltpu.VMEM((out_rows, out_cols), F32)],
        compiler_params=_params(("arbitrary", "arbitrary")),
    )(a, b)


def _ffn_wgrads(hb, ab, dg, du, dyb, tag):
    s, d = hb.shape
    fb = ab.shape[-1]
    shared = lambda cols: (lambda tk: pl.BlockSpec((tk, cols), lambda j, k: (k, 0)))
    stacked = lambda cols: (lambda tk: pl.BlockSpec((None, tk, cols), lambda j, k: (j, k, 0)))
    dwg = _wgrad(hb, dg, shared(d), stacked(fb), d, fb, 1.0, "wgrad_gate_" + tag)
    dwu = _wgrad(hb, du, shared(d), stacked(fb), d, fb, 1.0, "wgrad_up_" + tag)
    dwd = _wgrad(ab, dyb, stacked(fb), shared(d), fb, d, 0.5, "wgrad_down_" + tag)
    return dwg, dwu, dwd


def _mix_pre(x, nw, win):
    s, d = x.shape
    nb, _, cb = win.shape
    tm = _tile(s)

    def body(x_ref, nw_ref, w_ref, p_ref, hb_ref, hs):
        @pl.when(pl.program_id(1) == 0)
        def _():
            xv = x_ref[...]
            hb = (xv * _rms_r(xv) * nw_ref[...]).astype(BF16)
            hs[...] = hb
            hb_ref[...] = hb

        p_ref[...] = _dot(hs[...], w_ref[...])

    row = pl.BlockSpec((tm, d), lambda i, j: (i, 0))
    return pl.pallas_call(
        body, name="mix_pre", grid=(s // tm, nb),
        in_specs=[row, pl.BlockSpec((1, d), lambda i, j: (0, 0)),
                  pl.BlockSpec((None, d, cb), lambda i, j: (j, 0, 0))],
        out_specs=[pl.BlockSpec((tm, cb), lambda i, j: (i, j)), row],
        out_shape=[jax.ShapeDtypeStruct((s, nb * cb), F32), jax.ShapeDtypeStruct((s, d), BF16)],
        scratch_shapes=[pltpu.VMEM((tm, d), BF16)],
        compiler_params=_params(("arbitrary", "arbitrary")),
    )(x, nw, win)


def _mix_pre_bwd(x, nw, dres, dpb, win):
    s, d = x.shape
    nb, _, cb = win.shape
    tm = _tile(s)

    def body(x_ref, nw_ref, dres_ref, dp_ref, w_ref, dx_ref, dnw_ref, acc):
        i = pl.program_id(0)
        j = pl.program_id(1)

        @pl.when(j == 0)
        def _():
            acc[...] = jnp.zeros_like(acc)

        acc[...] += _dot_nt(dp_ref[...], w_ref[...])

        @pl.when(j == nb - 1)
        def _():
            xv = x_ref[...]
            dx, dn = _rms_bwd(xv, _rms_r(xv), nw_ref[...], acc[...])
            dx_ref[...] = dres_ref[...] + dx

            @pl.when(i == 0)
            def _():
                dnw_ref[...] = jnp.zeros_like(dnw_ref)

            dnw_ref[...] += dn

    row = pl.BlockSpec((tm, d), lambda i, j: (i, 0))
    vec = pl.BlockSpec((1, d), lambda i, j: (0, 0))
    return pl.pallas_call(
        body, name="mix_pre_bwd", grid=(s // tm, nb),
        in_specs=[row, vec, row, pl.BlockSpec((tm, cb), lambda i, j: (i, j)),
                  pl.BlockSpec((None, d, cb), lambda i, j: (j, 0, 0))],
        out_specs=[row, vec],
        out_shape=[jax.ShapeDtypeStruct((s, d), F32), jax.ShapeDtypeStruct((1, d), F32)],
        scratch_shapes=[pltpu.VMEM((tm, d), F32)],
        compiler_params=_params(("arbitrary", "arbitrary")),
    )(x, nw, dres, dpb, win)


def _mix_post(x, yr, ya, nr, na, wout):
    s, d = x.shape
    h = yr.shape[1]
    tm = _tile(s)

    def body(x_ref, yr_ref, ya_ref, nr_ref, na_ref, w_ref, out_ref):
        yrv = yr_ref[...]
        yav = ya_ref[...]
        onb = (yrv * _rms_r(yrv) * nr_ref[...]).astype(BF16)
        oab = (yav * _rms_r(yav) * na_ref[...]).astype(BF16)
        out_ref[...] = x_ref[...] + _dot(onb, w_ref[0:h, :]) + _dot(oab, w_ref[h:2 * h, :])

    row = pl.BlockSpec((tm, d), lambda i: (i, 0))
    half = pl.BlockSpec((tm, h), lambda i: (i, 0))
    vec = pl.BlockSpec((1, h), lambda i: (0, 0))
    return pl.pallas_call(
        body, name="mix_post", grid=(s // tm,),
        in_specs=[row, half, half, vec, vec, pl.BlockSpec((2 * h, d), lambda i: (0, 0))],
        out_specs=row, out_shape=jax.ShapeDtypeStruct((s, d), F32),
        compiler_params=_params(("arbitrary",)),
    )(x, yr, ya, nr, na, wout)


def _mix_post_bwd(dx, yr, ya, nr, na, wout):
    s, d = dx.shape
    h = yr.shape[1]
    tm = _tile(s)

    def body(dx_ref, yr_ref, ya_ref, nr_ref, na_ref, w_ref,
             dyr_ref, dya_ref, yc_ref, dxb_ref, dnr_ref, dna_ref):
        i = pl.program_id(0)
        dxb = dx_ref[...].astype(BF16)
        dxb_ref[...] = dxb
        dyc = _dot_nt(dxb, w_ref[...])
        yrv = yr_ref[...]
        yav = ya_ref[...]
        rr = _rms_r(yrv)
        ra = _rms_r(yav)
        yc_ref[:, 0:h] = (yrv * rr * nr_ref[...]).astype(BF16)
        yc_ref[:, h:2 * h] = (yav * ra * na_ref[...]).astype(BF16)
        dyr, dnr = _rms_bwd(yrv, rr, nr_ref[...], dyc[:, 0:h])
        dya, dna = _rms_bwd(yav, ra, na_ref[...], dyc[:, h:2 * h])
        dyr_ref[...] = dyr
        dya_ref[...] = dya

        @pl.when(i == 0)
        def _():
            dnr_ref[...] = jnp.zeros_like(dnr_ref)
            dna_ref[...] = jnp.zeros_like(dna_ref)

        dnr_ref[...] += dnr
        dna_ref[...] += dna

    row = pl.BlockSpec((tm, d), lambda i: (i, 0))
    half = pl.BlockSpec((tm, h), lambda i: (i, 0))
    vec = pl.BlockSpec((1, h), lambda i: (0, 0))
    return pl.pallas_call(
        body, name="mix_post_bwd", grid=(s // tm,),
        in_specs=[row, half, half, vec, vec, pl.BlockSpec((2 * h, d), lambda i: (0, 0))],
        out_specs=[half, half, pl.BlockSpec((tm, 2 * h), lambda i: (i, 0)), row, vec, vec],
        out_shape=[jax.ShapeDtypeStruct((s, h), F32), jax.ShapeDtypeStruct((s, h), F32),
                   jax.ShapeDtypeStruct((s, 2 * h), BF16), jax.ShapeDtypeStruct((s, d), BF16),
                   jax.ShapeDtypeStruct((1, h), F32), jax.ShapeDtypeStruct((1, h), F32)],
        compiler_params=_params(("arbitrary",)),
    )(dx, yr, ya, nr, na, wout)


def _shift_down(xv, s, prev8):
    rolled = pltpu.roll(xv, s, 0)
    row8 = lax.broadcasted_iota(jnp.int32, prev8.shape, 0)
    head = jnp.where(row8 < s, pltpu.roll(prev8, s, 0), rolled[0:8, :])
    return jnp.concatenate([head, rolled[8:, :]], axis=0)


def _shift_up(xv, s, next8):
    n = xv.shape[0]
    rolled = pltpu.roll(xv, n - s, 0)
    row8 = lax.broadcasted_iota(jnp.int32, next8.shape, 0)
    tail = jnp.where(row8 >= 8 - s, pltpu.roll(next8, 8 - s, 0), rolled[n - 8:, :])
    return jnp.concatenate([rolled[:n - 8, :], tail], axis=0)


def _scan_fwd(a, b):
    n = a.shape[0]
    row = lax.broadcasted_iota(jnp.int32, a.shape, 0)
    s = 1
    while s < n:
        ok = row >= s
        b = jnp.where(ok, a * pltpu.roll(b, s, 0) + b, b)
        a = jnp.where(ok, a * pltpu.roll(a, s, 0), a)
        s *= 2
    return b


def _scan_bwd(a, b):
    n = a.shape[0]
    row = lax.broadcasted_iota(jnp.int32, a.shape, 0)
    s = 1
    while s < n:
        ok = row < n - s
        b = jnp.where(ok, a * pltpu.roll(b, n - s, 0) + b, b)
        a = jnp.where(ok, a * pltpu.roll(a, n - s, 0), a)
        s *= 2
    return b


def _rglru_gates(xv, prev8, cw_ref, cb_ref, wa_ref, ba_ref, wx_ref, bx_ref, lam_ref):
    x1 = _shift_down(xv, 1, prev8)
    x2 = _shift_down(xv, 2, prev8)
    x3 = _shift_down(xv, 3, prev8)
    xc = cw_ref[3:4, :] * xv + cw_ref[2:3, :] * x1 + cw_ref[1:2, :] * x2 + cw_ref[0:1, :] * x3 + cb_ref[...]
    xcb = xc.astype(BF16)
    r = _sigmoid(_dot(xcb, wa_ref[...]) + ba_ref[...])
    ig = _sigmoid(_dot(xcb, wx_ref[...]) + bx_ref[...])
    c = RG_C * _log_sigmoid(lam_ref[...])
    la = r * c
    a = jnp.exp(la)
    m = jnp.sqrt(-_expm1_neg(2.0 * la))
    return (x1, x2, x3), xc, xcb, r, ig, c, a, m


def _rglru_fwd(proj, cw, cb, wa, ba, wx, bx, lam):
    s = proj.shape[0]
    w = D_RNN
    tm = _tile(s)

    def body(xr_ref, gate_ref, cw_ref, cb_ref, wa_ref, ba_ref, wx_ref, bx_ref, lam_ref,
             y_ref, h_ref, prev, hlast):
        @pl.when(pl.program_id(0) == 0)
        def _():
            prev[...] = jnp.zeros_like(prev)
            hlast[...] = jnp.zeros_like(hlast)

        xv = xr_ref[...]
        _, xc, _, _, ig, _, a, m = _rglru_gates(xv, prev[...], cw_ref, cb_ref, wa_ref, ba_ref,
                                                wx_ref, bx_ref, lam_ref)
        b = m * (ig * xc)
        row = lax.broadcasted_iota(jnp.int32, b.shape, 0)
        b = jnp.where(row == 0, b + a * hlast[...], b)
        h = _scan_fwd(a, b)
        h_ref[...] = h
        y_ref[...] = h * _gelu(gate_ref[...])
        prev[...] = xv[tm - 8:, :]
        hlast[...] = h[tm - 1:tm, :]

    vec = pl.BlockSpec((1, w), lambda i: (0, 0))
    sq = pl.BlockSpec((w, w), lambda i: (0, 0))
    out = pl.BlockSpec((tm, w), lambda i: (i, 0))
    return pl.pallas_call(
        body, name="rglru_fwd", grid=(s // tm,),
        in_specs=[pl.BlockSpec((tm, w), lambda i: (i, 0)), pl.BlockSpec((tm, w), lambda i: (i, 1)),
                  pl.BlockSpec((CONV_W, w), lambda i: (0, 0)), vec, sq, vec, sq, vec, vec],
        out_specs=[out, out],
        out_shape=[jax.ShapeDtypeStruct((s, w), F32), jax.ShapeDtypeStruct((s, w), F32)],
        scratch_shapes=[pltpu.VMEM((8, w), F32), pltpu.VMEM((1, w), F32)],
        compiler_params=_params(("arbitrary",)),
    )(proj, proj, cw, cb, wa, ba, wx, bx, lam)


def _rglru_bwd(proj, hseq, dyr, cw, cb, wa, ba, wx, bx, lam):
    s = proj.shape[0]
    w = D_RNN
    tm = _tile(s)
    nt = s // tm
    t8 = tm // 8

    def body(xr_ref, xp_ref, gate_ref, h_ref, hp_ref, dy_ref, cw_ref, cb_ref, wa_ref, ba_ref,
             wx_ref, bx_ref, lam_ref,
             dxr_ref, dgate_ref, dcw_ref, dcb_ref, dwa_ref, dba_ref, dwx_ref, dbx_ref, dlam_ref,
             carry, dxc_next):
        i = pl.program_id(0)
        first_tile = i == nt - 1

        @pl.when(i == 0)
        def _():
            carry[...] = jnp.zeros_like(carry)
            dxc_next[...] = jnp.zeros_like(dxc_next)
            for ref in (dcw_ref, dcb_ref, dwa_ref, dba_ref, dwx_ref, dbx_ref, dlam_ref):
                ref[...] = jnp.zeros_like(ref)

        xv = xr_ref[...]
        prev8 = jnp.where(first_tile, 0.0, xp_ref[...])
        hprev8 = jnp.where(first_tile, 0.0, hp_ref[...])
        (x1, x2, x3), xc, xcb, r, ig, c, a, m = _rglru_gates(
            xv, prev8, cw_ref, cb_ref, wa_ref, ba_ref, wx_ref, bx_ref, lam_ref)
        gv = gate_ref[...]
        hv = h_ref[...]
        dy = dy_ref[...]
        dgate_ref[...] = (dy * hv * _gelu_grad(gv)).astype(BF16)
        dh = dy * _gelu(gv)
        row = lax.broadcasted_iota(jnp.int32, dh.shape, 0)
        dh = jnp.where(row == tm - 1, dh + carry[...], dh)
        a_up = jnp.where(row == tm - 1, 0.0, pltpu.roll(a, tm - 1, 0))
        lam_t = _scan_bwd(a_up, dh)
        carry[...] = a[0:1, :] * lam_t[0:1, :]
        hm1 = _shift_down(hv, 1, hprev8)
        da = lam_t * hm1
        ixc = ig * xc
        dm = lam_t * ixc
        dig = lam_t * m * xc
        dxc = lam_t * m * ig
        dla = da * a - dm * (a * a) / m
        dr = dla * c
        dlam_ref[...] += jnp.sum(dla * r, axis=0, keepdims=True)
        dpa = dr * r * (1.0 - r)
        dpi = dig * ig * (1.0 - ig)
        dba_ref[...] += jnp.sum(dpa, axis=0, keepdims=True)
        dbx_ref[...] += jnp.sum(dpi, axis=0, keepdims=True)
        dpab = dpa.astype(BF16)
        dpib = dpi.astype(BF16)
        dwa_ref[...] += _dot_tn(xcb, dpab)
        dwx_ref[...] += _dot_tn(xcb, dpib)
        dxc = dxc + _dot_nt(dpab, wa_ref[...]) + _dot_nt(dpib, wx_ref[...])
        dcb_ref[...] += jnp.sum(dxc, axis=0, keepdims=True)
        dcw_ref[3:4, :] += jnp.sum(dxc * xv, axis=0, keepdims=True)
        dcw_ref[2:3, :] += jnp.sum(dxc * x1, axis=0, keepdims=True)
        dcw_ref[1:2, :] += jnp.sum(dxc * x2, axis=0, keepdims=True)
        dcw_ref[0:1, :] += jnp.sum(dxc * x3, axis=0, keepdims=True)
        nxt = dxc_next[...]
        dxr = (cw_ref[3:4, :] * dxc + cw_ref[2:3, :] * _shift_up(dxc, 1, nxt)
               + cw_ref[1:2, :] * _shift_up(dxc, 2, nxt) + cw_ref[0:1, :] * _shift_up(dxc, 3, nxt))
        dxr_ref[...] = dxr.astype(BF16)
        dxc_next[...] = dxc[0:8, :]

        @pl.when(first_tile)
        def _():
            lv = lam_ref[...]
            dlam_ref[...] = dlam_ref[...] * (RG_C * _sigmoid(-lv))

    rev = lambda i: nt - 1 - i
    vec = pl.BlockSpec((1, w), lambda i: (0, 0))
    sq = pl.BlockSpec((w, w), lambda i: (0, 0))
    cur = lambda col: pl.BlockSpec((tm, w), lambda i: (rev(i), col))
    before = lambda cols: pl.BlockSpec((8, w), lambda i: (jnp.maximum(rev(i) * t8 - 1, 0), 0))
    return pl.pallas_call(
        body, name="rglru_bwd", grid=(nt,),
        in_specs=[cur(0), before(None), cur(1), cur(0), before(None), cur(0),
                  pl.BlockSpec((CONV_W, w), lambda i: (0, 0)), vec, sq, vec, sq, vec, vec],
        out_specs=[cur(0), cur(0), pl.BlockSpec((CONV_W, w), lambda i: (0, 0)), vec, sq, vec, sq, vec, vec],
        out_shape=[jax.ShapeDtypeStruct((s, w), BF16), jax.ShapeDtypeStruct((s, w), BF16),
                   jax.ShapeDtypeStruct((CONV_W, w), F32), jax.ShapeDtypeStruct((1, w), F32),
                   jax.ShapeDtypeStruct((w, w), F32), jax.ShapeDtypeStruct((1, w), F32),
                   jax.ShapeDtypeStruct((w, w), F32), jax.ShapeDtypeStruct((1, w), F32),
                   jax.ShapeDtypeStruct((1, w), F32)],
        scratch_shapes=[pltpu.VMEM((1, w), F32), pltpu.VMEM((8, w), F32)],
        compiler_params=_params(("arbitrary",)),
    )(proj, proj, proj, hseq, hseq, dyr, cw, cb, wa, ba, wx, bx, lam)


def _sb_logs(z, valid):
    l1p = jnp.log(1.0 + jnp.exp(-jnp.abs(z)))
    lb = jnp.minimum(z, 0.0) - l1p
    lm = jnp.where(valid, -jnp.maximum(z, 0.0) - l1p, 0.0)
    return lb, lm


class _Window:
    def __init__(self):
        blk, win, cut = ATT_BLOCK, ATT_WINDOW, ATT_SPLIT
        self.row = lax.broadcasted_iota(jnp.int32, (blk, win), 0)
        self.col = lax.broadcasted_iota(jnp.int32, (blk, win), 1)

        def tri(n, later):
            j = lax.broadcasted_iota(jnp.int32, (n, n), 0)
            s = lax.broadcasted_iota(jnp.int32, (n, n), 1)
            return jnp.where((j > s) if later else (j < s), 1.0, 0.0).astype(BF16)

        self.later = (tri(cut, True), tri(win - cut, True))
        self.earlier = (tri(cut, False), tri(win - cut, False))

    def place(self, qi, g):
        end = (qi + 1) * ATT_BLOCK - g * ATT_WINDOW
        start = pl.multiple_of(jnp.maximum(end - ATT_WINDOW, 0), ATT_BLOCK)
        valid = start + self.col < jnp.minimum(qi * ATT_BLOCK + self.row, end)
        return start, valid

    @staticmethod
    def _parts(xv):
        hi = xv.astype(BF16)
        lo = (xv - hi.astype(F32)).astype(BF16)
        cut = ATT_SPLIT
        sums = (jnp.sum(xv[:, :cut], axis=1, keepdims=True), jnp.sum(xv[:, cut:], axis=1, keepdims=True))
        return (hi[:, :cut], lo[:, :cut]), (hi[:, cut:], lo[:, cut:]), sums

    def sums_after(self, xv, carry):
        (h0, l0), (h1, l1), (s0, s1) = self._parts(xv)
        first = _dot(h0, self.later[0]) + _dot(l0, self.later[0]) + (s1 + carry)
        last = _dot(h1, self.later[1]) + _dot(l1, self.later[1]) + carry
        return jnp.concatenate([first, last], axis=1), s0 + s1

    def sums_before(self, xv, carry):
        (h0, l0), (h1, l1), (s0, s1) = self._parts(xv)
        first = _dot(h0, self.earlier[0]) + _dot(l0, self.earlier[0]) + carry
        last = _dot(h1, self.earlier[1]) + _dot(l1, self.earlier[1]) + (s0 + carry)
        return jnp.concatenate([first, last], axis=1), s0 + s1


def _head_lanes(hh):
    return slice(hh * HEAD_DIM, (hh + 1) * HEAD_DIM)


def _attn_fwd(proj, qg, kg):
    s = proj.shape[0]
    blk, win, dh = ATT_BLOCK, ATT_WINDOW, HEAD_DIM
    nq = s // blk
    scale = 1.0 / math.sqrt(dh)
    assert s >= win and s % blk == 0

    def body(q_ref, k_ref, v_ref, qg_ref, kg_ref, o_ref, qn, kn, vb, ob):
        wd = _Window()
        for hh in range(2):
            lanes = _head_lanes(hh)
            qv = q_ref[:, lanes]
            qn[...] = (qv * _rms_r(qv) * qg_ref[...] * scale).astype(BF16)
            kv = k_ref[:, lanes]
            kn[...] = (kv * _rms_r(kv) * kg_ref[...]).astype(BF16)
            vb[...] = v_ref[:, lanes].astype(BF16)

            def q_step(qi, _):
                qoff = pl.multiple_of(qi * blk, blk)
                qt = qn[pl.ds(qoff, blk), :]

                def more(carry):
                    g, live, _, _ = carry
                    return jnp.logical_and((qi + 1) * blk - g * win > 0, live > 0)

                def window(carry):
                    g, _, acc, run = carry
                    start, valid = wd.place(qi, g)
                    z = _dot_nt(qt, kn[pl.ds(start, win), :])
                    lb, lm = _sb_logs(z, valid)
                    tail, total = wd.sums_after(lm, run)
                    wgt = jnp.where(valid, jnp.exp(lb + tail), 0.0)
                    acc = acc + _dot(wgt.astype(BF16), vb[pl.ds(start, win), :])
                    run = run + total
                    live = (jnp.max(run) > EXP_ZERO).astype(jnp.int32)
                    return g + 1, live, acc, run

                _, _, acc, _ = lax.while_loop(
                    more, window, (jnp.int32(0), jnp.int32(1), jnp.zeros((blk, dh), F32), jnp.zeros((blk, 1), F32)))
                ob[pl.ds(qoff, blk), :] = acc
                return 0

            lax.fori_loop(0, nq, q_step, 0)
            o_ref[:, lanes] = ob[...]

    pair = lambda group: pl.BlockSpec((s, 2 * dh), lambda h: (0, group * (D_ATT // (2 * dh)) + h))
    vec = pl.BlockSpec((1, dh), lambda h: (0, 0))
    return pl.pallas_call(
        body, name="attn_fwd", grid=(N_HEADS // 2,),
        in_specs=[pair(2), pair(3), pair(4), vec, vec], out_specs=pair(0),
        out_shape=jax.ShapeDtypeStruct((s, D_ATT), F32),
        scratch_shapes=[pltpu.VMEM((s, dh), BF16)] * 3 + [pltpu.VMEM((s, dh), F32)],
        compiler_params=_params(("arbitrary",)),
    )(proj, proj, proj, qg, kg)


def _attn_bwd(proj, dya, qg, kg):
    s = proj.shape[0]
    blk, win, dh = ATT_BLOCK, ATT_WINDOW, HEAD_DIM
    nq = s // blk
    max_windows = -(-s // win) + 1
    scale = 1.0 / math.sqrt(dh)
    assert s >= win and s % blk == 0

    def body(q_ref, k_ref, v_ref, do_ref, qg_ref, kg_ref,
             dq_ref, dk_ref, dv_ref, dqg_ref, dkg_ref, qn, kn, vb, dob, runs, dqn, dkn, dvn):
        wd = _Window()

        @pl.when(pl.program_id(0) == 0)
        def _():
            dqg_ref[...] = jnp.zeros_like(dqg_ref)
            dkg_ref[...] = jnp.zeros_like(dkg_ref)

        for hh in range(2):
            lanes = _head_lanes(hh)
            qv = q_ref[:, lanes]
            qn[...] = (qv * _rms_r(qv) * qg_ref[...] * scale).astype(BF16)
            kv = k_ref[:, lanes]
            kn[...] = (kv * _rms_r(kv) * kg_ref[...]).astype(BF16)
            vb[...] = v_ref[:, lanes].astype(BF16)
            dob[...] = do_ref[:, lanes].astype(BF16)
            dkn[...] = jnp.zeros_like(dkn)
            dvn[...] = jnp.zeros_like(dvn)

            def q_step(qi, _):
                qoff = pl.multiple_of(qi * blk, blk)
                qt = qn[pl.ds(qoff, blk), :]
                dot = dob[pl.ds(qoff, blk), :]

                def more(carry):
                    g, live, _ = carry
                    return jnp.logical_and((qi + 1) * blk - g * win > 0, live > 0)

                def run_window(carry):
                    g, _, run = carry
                    runs[g] = run
                    start, valid = wd.place(qi, g)
                    z = _dot_nt(qt, kn[pl.ds(start, win), :])
                    _, lm = _sb_logs(z, valid)
                    run = run + jnp.sum(lm, axis=1, keepdims=True)
                    live = (jnp.max(run) > EXP_ZERO).astype(jnp.int32)
                    return g + 1, live, run

                windows, _, _ = lax.while_loop(
                    more, run_window, (jnp.int32(0), jnp.int32(1), jnp.zeros((blk, 1), F32)))

                def k_window(gg, carry):
                    dq_acc, esum = carry
                    g = windows - 1 - gg
                    start, valid = wd.place(qi, g)
                    kt = kn[pl.ds(start, win), :]
                    vt = vb[pl.ds(start, win), :]
                    z = _dot_nt(qt, kt)
                    lb, lm = _sb_logs(z, valid)
                    tail, _ = wd.sums_after(lm, runs[g])
                    wgt = jnp.where(valid, jnp.exp(lb + tail), 0.0)
                    e = _dot_nt(dot, vt) * wgt
                    before, etotal = wd.sums_before(e, esum)
                    beta = jnp.exp(lb)
                    dz = jnp.where(valid, e * (1.0 - beta) - before * beta, 0.0)
                    dzb = dz.astype(BF16)
                    dq_acc = dq_acc + _dot(dzb, kt)
                    dkn[pl.ds(start, win), :] += _dot_tn(dzb, qt)
                    dvn[pl.ds(start, win), :] += _dot_tn(wgt.astype(BF16), dot)
                    return dq_acc, esum + etotal

                dq_acc, _ = lax.fori_loop(0, windows, k_window,
                                          (jnp.zeros((blk, dh), F32), jnp.zeros((blk, 1), F32)))
                dqn[pl.ds(qoff, blk), :] = dq_acc
                return 0

            lax.fori_loop(0, nq, q_step, 0)

            dq, dqg = _rms_bwd(qv, _rms_r(qv), qg_ref[...] * scale, dqn[...])
            dq_ref[:, lanes] = dq.astype(BF16)
            dqg_ref[...] += dqg * scale
            dk, dkg = _rms_bwd(kv, _rms_r(kv), kg_ref[...], dkn[...])
            dk_ref[:, lanes] = dk.astype(BF16)
            dkg_ref[...] += dkg
            dv_ref[:, lanes] = dvn[...].astype(BF16)

    pair = lambda group: pl.BlockSpec((s, 2 * dh), lambda h: (0, group * (D_ATT // (2 * dh)) + h))
    vec = pl.BlockSpec((1, dh), lambda h: (0, 0))
    return pl.pallas_call(
        body, name="attn_bwd", grid=(N_HEADS // 2,),
        in_specs=[pair(2), pair(3), pair(4), pair(0), vec, vec], out_specs=[pair(0), pair(0), pair(0), vec, vec],
        out_shape=[jax.ShapeDtypeStruct((s, D_ATT), BF16)] * 3 + [jax.ShapeDtypeStruct((1, dh), F32)] * 2,
        scratch_shapes=[pltpu.VMEM((s, dh), BF16)] * 4 + [pltpu.VMEM((max_windows, blk, 1), F32)]
        + [pltpu.VMEM((s, dh), F32)] * 3,
        compiler_params=_params(("arbitrary",)),
    )(proj, proj, proj, dya, qg, kg)


def _block_diag(w):
    n, c, d = w.shape
    return jnp.einsum("ncd,nm->ncmd", w, jnp.eye(n, dtype=w.dtype)).reshape(n * c, n * d)


def _diag_blocks(full, n):
    c = full.shape[0] // n
    return jnp.stack([full[i * c:(i + 1) * c, i * c:(i + 1) * c] for i in range(n)])


def _local_step(x, tgt, big, small):
    wa = _block_diag(small["rg_w_a"]).astype(BF16)
    wx = _block_diag(small["rg_w_x"]).astype(BF16)
    wout = big["w_out"].reshape(D_MODEL, D_MODEL)
    rg = (small["conv_w"], small["conv_b"], wa, small["rg_b_a"], wx, small["rg_b_x"], small["rg_lambda"])

    x1, g1, u1, hb1, ab1 = _ffn_fwd(x, small["ffn1_norm"], big["ffn1_w_gate"], big["ffn1_w_up"], big["ffn1_w_down"])
    proj, hb2 = _mix_pre(x1, small["mix_norm"], big["w_in"])
    yr, hseq = _rglru_fwd(proj, *rg)
    ya = _attn_fwd(proj, small["q_norm"], small["k_norm"])
    x2 = _mix_post(x1, yr, ya, small["rnn_out_norm"], small["attn_out_norm"], wout)
    dx3, g2, u2, hb3, ab3, loss = _ffn_fwd(x2, small["ffn2_norm"], big["ffn2_w_gate"], big["ffn2_w_up"],
                                          big["ffn2_w_down"], tgt)

    gb, gs = {}, {}
    dx2, dg2, du2, dyb2, gs["ffn2_norm"] = _ffn_bwd_act(
        x2, small["ffn2_norm"], dx3, g2, u2, big["ffn2_w_gate"], big["ffn2_w_up"], big["ffn2_w_down"], "ffn2_bwd")
    gb["ffn2_w_gate"], gb["ffn2_w_up"], gb["ffn2_w_down"] = _ffn_wgrads(hb3, ab3, dg2, du2, dyb2, "ffn2")

    dyr, dya, ycat, dxb2, gs["rnn_out_norm"], gs["attn_out_norm"] = _mix_post_bwd(
        dx2, yr, ya, small["rnn_out_norm"], small["attn_out_norm"], wout)
    quarter = D_MODEL // N_CHIPS
    gb["w_out"] = _wgrad(ycat, dxb2, lambda tk: pl.BlockSpec((tk, quarter), lambda j, k: (k, j)),
                         lambda tk: pl.BlockSpec((tk, D_MODEL), lambda j, k: (k, 0)),
                         quarter, D_MODEL, 1.0, "wgrad_out")
    dq, dk, dv, gs["q_norm"], gs["k_norm"] = _attn_bwd(proj, dya, small["q_norm"], small["k_norm"])
    dxr, dgate, gs["conv_w"], gs["conv_b"], dwa, gs["rg_b_a"], dwx, gs["rg_b_x"], gs["rg_lambda"] = _rglru_bwd(
        proj, hseq, dyr, *rg)
    gs["rg_w_a"] = _diag_blocks(dwa, RNN_BLOCKS)
    gs["rg_w_x"] = _diag_blocks(dwx, RNN_BLOCKS)
    dpb = jnp.concatenate([dxr, dgate, dq, dk, dv], axis=1)
    cb = N_IN // N_CHIPS
    gb["w_in"] = _wgrad(hb2, dpb, lambda tk: pl.BlockSpec((tk, D_MODEL), lambda j, k: (k, 0)),
                        lambda tk: pl.BlockSpec((tk, cb), lambda j, k: (k, j)),
                        D_MODEL, cb, 1.0, "wgrad_in")
    dx1, gs["mix_norm"] = _mix_pre_bwd(x1, small["mix_norm"], dx2, dpb, big["w_in"])

    dx0, dg1, du1, dyb1, gs["ffn1_norm"] = _ffn_bwd_act(
        x, small["ffn1_norm"], dx1, g1, u1, big["ffn1_w_gate"], big["ffn1_w_up"], big["ffn1_w_down"], "ffn1_bwd")
    gb["ffn1_w_gate"], gb["ffn1_w_up"], gb["ffn1_w_down"] = _ffn_wgrads(hb1, ab1, dg1, du1, dyb1, "ffn1")
    return loss[0, 0], dx0, gb, gs


ANY = pl.BlockSpec(memory_space=pl.ANY)


def _place():
    x, y, c = lax.axis_index("x"), lax.axis_index("y"), lax.axis_index("c")
    other_chips = [(1 - x, y), (x, 1 - y), (1 - x, 1 - y)]
    return x, y, c, 2 * x + y, other_chips


def _remote(src, dst, send_sem, recv_sem, to):
    return pltpu.make_async_remote_copy(src_ref=src, dst_ref=dst, send_sem=send_sem, recv_sem=recv_sem,
                                        device_id=to, device_id_type=MESH)


def _half(rows, c):
    return pl.ds(pl.multiple_of(c * rows, 16), rows)


def _gather_weights(split, whole):
    arrs = list(split) + list(whole)
    n, ns = len(arrs), len(split)

    def body(*refs):
        outs = refs[n:2 * n]
        send_sems, recv_sems, fsend_sems, frecv_sems = refs[2 * n:]
        x, y, c, me, chips = _place()
        sibling = (x, y, 1 - c)

        def region(i, chip, half):
            if i < ns:
                return outs[i].at[chip, _half(arrs[i].shape[1] // 2, half)]
            return outs[i].at[chip]

        sends = []
        for i in range(n):
            for p, chip in enumerate(chips):
                k = 3 * i + p
                mine = region(i, me, c)
                sends.append(_remote(mine, mine, send_sems.at[k], recv_sems.at[k], (*chip, c)))
        for cp in sends:
            cp.start()
        passed = []
        for i in range(n):
            for p, (cx, cy) in enumerate(chips):
                k = 3 * i + p
                got = region(i, 2 * cx + cy, c)
                _remote(got, got, send_sems.at[k], recv_sems.at[k], (cx, cy, c)).wait_recv()
                if i < ns:
                    fwd = _remote(got, got, fsend_sems.at[k], frecv_sems.at[k], sibling)
                    fwd.start()
                    passed.append(fwd)
        for i in range(ns):
            for p, (cx, cy) in enumerate(chips):
                k = 3 * i + p
                got = region(i, 2 * cx + cy, 1 - c)
                _remote(got, got, fsend_sems.at[k], frecv_sems.at[k], sibling).wait_recv()
        for cp in sends + passed:
            cp.wait_send()

    return pl.pallas_call(
        body, name="gather_weights",
        in_specs=[ANY] * n, out_specs=[ANY] * n,
        out_shape=[jax.ShapeDtypeStruct(a.shape, a.dtype) for a in arrs],
        input_output_aliases={i: i for i in range(n)},
        scratch_shapes=[pltpu.SemaphoreType.DMA((3 * n,)), pltpu.SemaphoreType.DMA((3 * n,)),
                        pltpu.SemaphoreType.DMA((3 * ns,)), pltpu.SemaphoreType.DMA((3 * ns,))],
    )(*arrs)


def _pair_exchange(grads):
    n = len(grads)

    def body(*refs):
        ins, theirs = refs[:n], refs[n:2 * n]
        send_sems, recv_sems = refs[2 * n:]
        x, y, c, _, _ = _place()
        sibling = (x, y, 1 - c)
        sends = [_remote(ins[k].at[:, _half(grads[k].shape[1] // 2, 1 - c)], theirs[k],
                         send_sems.at[k], recv_sems.at[k], sibling) for k in range(n)]
        for cp in sends:
            cp.start()
        for k in range(n):
            _remote(theirs[k], theirs[k], send_sems.at[k], recv_sems.at[k], sibling).wait_recv()
        for cp in sends:
            cp.wait_send()

    return pl.pallas_call(
        body, name="grad_pair_exchange",
        in_specs=[ANY] * n, out_specs=[ANY] * n,
        out_shape=[jax.ShapeDtypeStruct((g.shape[0], g.shape[1] // 2, g.shape[2]), g.dtype) for g in grads],
        scratch_shapes=[pltpu.SemaphoreType.DMA((n,))] * 2,
    )(*grads)


def _chip_exchange(sums, slots):
    n = len(sums)

    def body(*refs):
        ins, outs = refs[:n], refs[2 * n:3 * n]
        send_sems, recv_sems = refs[3 * n:]
        x, y, c, me, chips = _place()
        sends = []
        for k in range(n):
            for p, (cx, cy) in enumerate(chips):
                j = 3 * k + p
                sends.append(_remote(ins[k].at[2 * cx + cy], outs[k].at[me], send_sems.at[j], recv_sems.at[j], (cx, cy, c)))
        for cp in sends:
            cp.start()
        for k in range(n):
            for p, (cx, cy) in enumerate(chips):
                j = 3 * k + p
                got = outs[k].at[2 * cx + cy]
                _remote(got, got, send_sems.at[j], recv_sems.at[j], (cx, cy, c)).wait_recv()
        for cp in sends:
            cp.wait_send()

    return pl.pallas_call(
        body, name="grad_chip_exchange",
        in_specs=[ANY] * (2 * n), out_specs=[ANY] * n,
        out_shape=[jax.ShapeDtypeStruct(a.shape, a.dtype) for a in slots],
        input_output_aliases={n + k: k for k in range(n)},
        scratch_shapes=[pltpu.SemaphoreType.DMA((3 * n,)), pltpu.SemaphoreType.DMA((3 * n,))],
    )(*sums, *slots)


def _half_swap(halves):
    n = len(halves)

    def body(*refs):
        outs = refs[n:2 * n]
        send_sems, recv_sems = refs[2 * n:]
        x, y, c, _, _ = _place()
        sibling = (x, y, 1 - c)
        sends = [_remote(outs[k].at[c], outs[k].at[c], send_sems.at[k], recv_sems.at[k], sibling) for k in range(n)]
        for cp in sends:
            cp.start()
        for k in range(n):
            got = outs[k].at[1 - c]
            _remote(got, got, send_sems.at[k], recv_sems.at[k], sibling).wait_recv()
        for cp in sends:
            cp.wait_send()

    return pl.pallas_call(
        body, name="grad_half_swap",
        in_specs=[ANY] * n, out_specs=[ANY] * n,
        out_shape=[jax.ShapeDtypeStruct(a.shape, a.dtype) for a in halves],
        input_output_aliases={k: k for k in range(n)},
        scratch_shapes=[pltpu.SemaphoreType.DMA((n,))] * 2,
    )(*halves)


def _gather_small(packed):
    n_dev = 8

    def body(in_ref, out_ref, send_sems, recv_sems, loc_sem):
        x, y, c, _, _ = _place()
        me = 4 * x + 2 * y + c
        local = pltpu.make_async_copy(in_ref, out_ref.at[me], loc_sem)
        local.start()
        peers = []
        for k in range(1, n_dev):
            fx, fy, fc = (k >> 2) & 1, (k >> 1) & 1, k & 1
            peers.append((x ^ fx, y ^ fy, c ^ fc))
        sends = [_remote(in_ref, out_ref.at[me], send_sems.at[k], recv_sems.at[k], peer)
                 for k, peer in enumerate(peers)]
        for cp in sends:
            cp.start()
        for k, (px, py, pc) in enumerate(peers):
            got = out_ref.at[4 * px + 2 * py + pc]
            _remote(got, got, send_sems.at[k], recv_sems.at[k], (px, py, pc)).wait_recv()
        for cp in sends:
            cp.wait_send()
        local.wait()

    return pl.pallas_call(
        body, name="gather_small_grads",
        in_specs=[ANY], out_specs=ANY,
        out_shape=jax.ShapeDtypeStruct((n_dev,) + packed.shape, packed.dtype),
        scratch_shapes=[pltpu.SemaphoreType.DMA((n_dev - 1,)), pltpu.SemaphoreType.DMA((n_dev - 1,)),
                        pltpu.SemaphoreType.DMA],
    )(packed)


def _row_tile(r):
    return r // 4 if r >= 256 and (r // 4) % 16 == 0 else r


def _prefetch_call(body, name, grid, in_specs, out_specs, out_shape):
    spec = pltpu.PrefetchScalarGridSpec(num_scalar_prefetch=1, grid=grid, in_specs=in_specs, out_specs=out_specs)
    return pl.pallas_call(body, name=name, grid_spec=spec, out_shape=out_shape,
                          compiler_params=_params(("arbitrary",) * len(grid)))


def _place_shard(w2d, where, dtype, name):
    r, c = w2d.shape
    tr = _row_tile(r)

    def body(where_ref, w_ref, out_ref):
        out_ref[...] = w_ref[...].astype(dtype)

    return _prefetch_call(
        body, name, (r // tr,), [pl.BlockSpec((tr, c), lambda i, s: (i, 0))],
        pl.BlockSpec((None, tr, c), lambda i, s: (s[1], i, 0)),
        jax.ShapeDtypeStruct((N_CHIPS, r, c), dtype))(where, w2d)


def _pair_sum(full, theirs, where, name):
    nb, hs, c = theirs.shape

    def body(where_ref, a_ref, b_ref, out_ref, own_ref):
        total = (a_ref[...].astype(F32) + b_ref[...].astype(F32)).astype(BF16)
        out_ref[...] = total

        @pl.when(pl.program_id(0) == where_ref[1])
        def _():
            own_ref[...] = total

    blk = pl.BlockSpec((None, hs, c), lambda j, s: (j, 0, 0))
    shape = jax.ShapeDtypeStruct(theirs.shape, BF16)
    return _prefetch_call(
        body, name, (nb,), [pl.BlockSpec((None, hs, c), lambda j, s: (j, s[0], 0)), blk],
        [blk, pl.BlockSpec((None, hs, c), lambda j, s: (s[1], 0, 0))], [shape, shape])(where, full, theirs)


def _chip_sum(slots, where, name):
    nb, hs, c = slots.shape
    tr = _row_tile(hs)

    def body(where_ref, a_ref, out_ref):
        total = a_ref[0].astype(F32)
        for j in range(1, nb):
            total = total + a_ref[j].astype(F32)
        out_ref[...] = total

    return _prefetch_call(
        body, name, (hs // tr,), [pl.BlockSpec((nb, tr, c), lambda i, s: (0, i, 0))],
        pl.BlockSpec((None, tr, c), lambda i, s: (s[0], i, 0)),
        jax.ShapeDtypeStruct((2, hs, c), F32))(where, slots)


def _slot_sum(a, name):
    nb, r, c = a.shape
    tr = _row_tile(r)

    def body(a_ref, out_ref):
        total = a_ref[0].astype(F32)
        for j in range(1, nb):
            total = total + a_ref[j].astype(F32)
        out_ref[...] = total

    return pl.pallas_call(
        body, name=name, grid=(r // tr,),
        in_specs=[pl.BlockSpec((nb, tr, c), lambda i: (0, i, 0))],
        out_specs=pl.BlockSpec((tr, c), lambda i: (i, 0)),
        out_shape=jax.ShapeDtypeStruct((r, c), F32), compiler_params=_params(("arbitrary",)),
    )(a)


def _adamw(w, g, m, v, name):
    r, c = w.shape
    tr = _row_tile(r)
    c1 = 1.0 - ADAM_B1 ** ADAM_STEP
    c2 = 1.0 - ADAM_B2 ** ADAM_STEP

    def body(w_ref, g_ref, m_ref, v_ref, d_ref, m2_ref, v2_ref):
        gv = g_ref[...]
        m2 = ADAM_B1 * m_ref[...] + (1.0 - ADAM_B1) * gv
        v2 = ADAM_B2 * v_ref[...] + (1.0 - ADAM_B2) * (gv * gv)
        m2_ref[...] = m2
        v2_ref[...] = v2
        d_ref[...] = -ADAM_LR * ((m2 / c1) / (jnp.sqrt(v2 / c2) + ADAM_EPS) + ADAM_WD * w_ref[...])

    blk = pl.BlockSpec((tr, c), lambda i: (i, 0))
    return pl.pallas_call(
        body, name=name, grid=(r // tr,), in_specs=[blk] * 4, out_specs=[blk] * 3,
        out_shape=[jax.ShapeDtypeStruct((r, c), F32)] * 3, compiler_params=_params(("arbitrary",)),
    )(w, g, m, v)


WEIGHTS = ["ffn1_norm", "ffn1_w_gate", "ffn1_w_up", "ffn1_w_down", "mix_norm", "w_in", "conv_w", "conv_b",
           "rg_w_a", "rg_b_a", "rg_w_x", "rg_b_x", "rg_lambda", "q_norm", "k_norm", "rnn_out_norm",
           "attn_out_norm", "w_out", "ffn2_norm", "ffn2_w_gate", "ffn2_w_up", "ffn2_w_down"]
BIG = ["ffn1_w_gate", "ffn1_w_up", "ffn1_w_down", "w_in", "w_out", "ffn2_w_gate", "ffn2_w_up", "ffn2_w_down"]
SMALL = [n for n in WEIGHTS if n not in BIG]
PACK_LANES = 128
PACK_ROW_ALIGN = 8


def _pack(parts):
    flat = jnp.concatenate([p.reshape(-1) for p in parts])
    unit = PACK_LANES * PACK_ROW_ALIGN
    padded = -(-flat.shape[0] // unit) * unit
    return jnp.pad(flat, (0, padded - flat.shape[0])).reshape(-1, PACK_LANES)


def _unpack(packed, shapes):
    flat = packed.reshape(-1)
    out, at = [], 0
    for shp in shapes:
        size = math.prod(shp)
        out.append(flat[at:at + size].reshape(shp))
        at += size
    return out


def kernel(x, ffn1_norm, ffn1_w_gate, ffn1_w_up, ffn1_w_down, mix_norm, w_in, conv_w, conv_b, rg_w_a, rg_b_a, rg_w_x, rg_b_x, rg_lambda, q_norm, k_norm, rnn_out_norm, attn_out_norm, w_out, ffn2_norm, ffn2_w_gate, ffn2_w_up, ffn2_w_down, loss_target, m_ffn1_norm, m_ffn1_w_gate, m_ffn1_w_up, m_ffn1_w_down, m_mix_norm, m_w_in, m_conv_w, m_conv_b, m_rg_w_a, m_rg_b_a, m_rg_w_x, m_rg_b_x, m_rg_lambda, m_q_norm, m_k_norm, m_rnn_out_norm, m_attn_out_norm, m_w_out, m_ffn2_norm, m_ffn2_w_gate, m_ffn2_w_up, m_ffn2_w_down, v_ffn1_norm, v_ffn1_w_gate, v_ffn1_w_up, v_ffn1_w_down, v_mix_norm, v_w_in, v_conv_w, v_conv_b, v_rg_w_a, v_rg_b_a, v_rg_w_x, v_rg_b_x, v_rg_lambda, v_q_norm, v_k_norm, v_rnn_out_norm, v_attn_out_norm, v_w_out, v_ffn2_norm, v_ffn2_w_gate, v_ffn2_w_up, v_ffn2_w_down):
    given = dict(locals())
    w = {n: given[n] for n in WEIGHTS}
    m = {n: given["m_" + n] for n in WEIGHTS}
    v = {n: given["v_" + n] for n in WEIGHTS}
    chip = 2 * lax.axis_index("x") + lax.axis_index("y")

    where = jnp.stack([lax.axis_index("c"), chip]).astype(jnp.int32)

    stacks = [_place_shard(w[n][0], where, BF16, "place_" + n) for n in BIG]
    gathered = _gather_weights(stacks, [_place_shard(w["conv_w"][0], where, F32, "place_conv_w")])
    big = dict(zip(BIG, gathered[:len(BIG)]))
    small = {n: (w[n][0] if w[n].ndim > 2 else w[n]) for n in SMALL}
    small["conv_w"] = jnp.transpose(gathered[-1], (1, 0, 2)).reshape(CONV_W, D_RNN)

    loss, grad_x, gb, gs = _local_step(x[0], loss_target[0], big, small)
    loss = lax.psum(loss, ("x", "y", "c"))

    theirs = _pair_exchange([gb[n] for n in BIG])
    pair, own = zip(*[_pair_sum(gb[n], t, where, "pair_sum_" + n) for n, t in zip(BIG, theirs)])
    slots = _chip_exchange(list(pair), list(own))
    swapped = _half_swap([_chip_sum(a, where, "chip_sum_" + n) for n, a in zip(BIG, slots)])
    grads, deltas, new_m, new_v = {}, {}, {}, {}
    for n, t in zip(BIG, swapped):
        shp = w[n].shape
        g2 = t.reshape(shp[1], shp[2])
        d2, m2, v2 = _adamw(w[n][0], g2, m[n][0], v[n][0], "adamw_" + n)
        grads[n], deltas[n], new_m[n], new_v[n] = g2.reshape(shp), d2.reshape(shp), m2.reshape(shp), v2.reshape(shp)

    full_shapes = [gs[n].shape for n in SMALL]
    everyone = _gather_small(_pack([gs[n] for n in SMALL]))
    g_small = _slot_sum(everyone, "small_grad_sum")
    g_parts = dict(zip(SMALL, _unpack(g_small, full_shapes)))
    quarter = D_RNN // N_CHIPS
    g_parts["conv_w"] = lax.dynamic_slice_in_dim(g_parts["conv_w"], chip * quarter, quarter, axis=1)
    local_shapes = [w[n].shape for n in SMALL]
    pk = lambda tree: _pack([tree[n] for n in SMALL])
    d_s, m_s, v_s = _adamw(pk(w), pk(g_parts), pk(m), pk(v), "adamw_small")
    for tree, packed in ((grads, pk(g_parts)), (deltas, d_s), (new_m, m_s), (new_v, v_s)):
        tree.update(zip(SMALL, _unpack(packed, local_shapes)))

    return (loss, grad_x.reshape(x.shape), *[grads[n] for n in WEIGHTS], *[deltas[n] for n in WEIGHTS],
            *[new_m[n] for n in WEIGHTS], *[new_v[n] for n in WEIGHTS])
```

```python
import functools
import math

import jax
import jax.numpy as jnp
from jax import lax
from jax.experimental import pallas as pl
from jax.experimental.pallas import tpu as pltpu

F32 = jnp.float32
BF16 = jnp.bfloat16
MESH = pl.DeviceIdType.MESH

D_MODEL = 1024
N_CHIPS = 4
D_RNN = 512
D_ATT = 512
N_HEADS = 8
HEAD_DIM = 64
RNN_BLOCKS = 8
CONV_W = 4
RG_C = 8.0
N_IN = 2 * D_RNN + 3 * D_ATT
EPS = 1e-6
ATT_BLOCK = 128
ATT_WINDOW = 384
ATT_SPLIT = 256
EXP_ZERO = -105.0

ADAM_LR = 0.001
ADAM_B1 = 0.9
ADAM_B2 = 0.999
ADAM_EPS = 1e-08
ADAM_WD = 0.01
ADAM_STEP = 10

V7X_VMEM_LIMIT = 56 * 1024 * 1024
TOKEN_TILE = 512

GELU_K0 = math.sqrt(2.0 / math.pi)
GELU_K1 = 0.044715


def _params(sem=None):
    return pltpu.CompilerParams(dimension_semantics=sem, vmem_limit_bytes=V7X_VMEM_LIMIT)


def _dot(a, b):
    return jnp.dot(a, b, preferred_element_type=F32)


def _dot_nt(a, b):
    return lax.dot_general(a, b, (((1,), (1,)), ((), ())), preferred_element_type=F32)


def _dot_tn(a, b):
    return lax.dot_general(a, b, (((0,), (0,)), ((), ())), preferred_element_type=F32)


def _sigmoid(x):
    return 1.0 / (1.0 + jnp.exp(-x))


def _rms_r(xv):
    return lax.rsqrt(jnp.mean(xv * xv, axis=-1, keepdims=True) + EPS)


def _rms_bwd(xv, r, nw, dh):
    t = dh * nw
    dx = r * t - xv * (r * r * r * jnp.mean(t * xv, axis=-1, keepdims=True))
    dn = jnp.sum(dh * xv * r, axis=0, keepdims=True)
    return dx, dn


def _gelu(x):
    t = jnp.tanh(GELU_K0 * (x + GELU_K1 * x * x * x))
    return 0.5 * x * (1.0 + t)


def _gelu_grad(x):
    t = jnp.tanh(GELU_K0 * (x + GELU_K1 * x * x * x))
    return 0.5 * (1.0 + t) + 0.5 * x * (1.0 - t * t) * (GELU_K0 * (1.0 + 3.0 * GELU_K1 * x * x))


def _expm1_neg(x):
    p = 1.0 + x * (1.0 / 8.0)
    for k in (7.0, 6.0, 5.0, 4.0, 3.0, 2.0):
        p = 1.0 + x * (1.0 / k) * p
    return jnp.where(x > -0.25, x * p, jnp.exp(x) - 1.0)


def _log_sigmoid(x):
    return jnp.minimum(x, 0.0) - jnp.log(1.0 + jnp.exp(-jnp.abs(x)))


def _tile(s):
    return min(TOKEN_TILE, s)


def _ffn_fwd(x, nw, wg, wu, wd, tgt=None, gather=None):
    s, d = x.shape
    nb, _, fb = wg.shape
    tm = _tile(s)
    ni = s // tm
    with_loss = tgt is not None
    carried = list(gather[0]) + list(gather[1]) if gather else []
    n_split = len(gather[0]) if gather else 0
    nc = len(carried)

    def body(*refs):
        x_ref, nw_ref, wg_ref, wu_ref, wd_ref = refs[:5]
        at = 5
        if with_loss:
            tgt_ref = refs[at]
            at += 1
        at += nc
        out_ref, g_ref, u_ref, hb_ref, ab_ref = refs[at:at + 5]
        at += 5
        if with_loss:
            loss_ref = refs[at]
            at += 1
        stacks = refs[at:at + nc]
        hs, acc = refs[at + nc:at + nc + 2]
        sems = refs[at + nc + 2:]
        i = pl.program_id(0)
        j = pl.program_id(1)

        if nc:
            @pl.when(jnp.logical_and(i == 0, j == 0))
            def _():
                _start(_gather_ici(stacks, n_split, *sems)[0])

            @pl.when(jnp.logical_and(i == ni - 1, j == nb - 1))
            def _():
                _finish(*_gather_ici(stacks, n_split, *sems))

        @pl.when(j == 0)
        def _():
            xv = x_ref[...]
            hb = (xv * _rms_r(xv) * nw_ref[...]).astype(BF16)
            hs[...] = hb
            hb_ref[...] = hb
            acc[...] = jnp.zeros_like(acc)

        hb = hs[...]
        g = _dot(hb, wg_ref[...])
        u = _dot(hb, wu_ref[...])
        g_ref[...] = g
        u_ref[...] = u
        ab = (g * _sigmoid(g) * u).astype(BF16)
        ab_ref[...] = ab
        acc[...] += _dot(ab, wd_ref[...])

        @pl.when(j == nb - 1)
        def _():
            y = x_ref[...] + 0.5 * acc[...]
            if with_loss:
                diff = y - tgt_ref[...]
                out_ref[...] = diff * (1.0 / d)

                @pl.when(i == 0)
                def _():
                    loss_ref[...] = jnp.zeros_like(loss_ref)

                loss_ref[...] += jnp.sum(diff * diff) * (0.5 / d)
            else:
                out_ref[...] = y

    row = pl.BlockSpec((tm, d), lambda i, j: (i, 0))
    in_specs = [row, pl.BlockSpec((1, d), lambda i, j: (0, 0)),
                pl.BlockSpec((None, d, fb), lambda i, j: (j, 0, 0)),
                pl.BlockSpec((None, d, fb), lambda i, j: (j, 0, 0)),
                pl.BlockSpec((None, fb, d), lambda i, j: (j, 0, 0))]
    args = [x, nw, wg, wu, wd]
    if with_loss:
        in_specs.append(row)
        args.append(tgt)
    blk = pl.BlockSpec((None, tm, fb), lambda i, j: (j, i, 0))
    out_shape = [jax.ShapeDtypeStruct((s, d), F32), jax.ShapeDtypeStruct((nb, s, fb), F32),
                 jax.ShapeDtypeStruct((nb, s, fb), F32), jax.ShapeDtypeStruct((s, d), BF16),
                 jax.ShapeDtypeStruct((nb, s, fb), BF16)]
    out_specs = [row, blk, blk, row, blk]
    if with_loss:
        out_shape.append(jax.ShapeDtypeStruct((1, 128), F32))
        out_specs.append(pl.BlockSpec((1, 128), lambda i, j: (0, 0)))
    aliases = {len(args) + k: len(out_shape) + k for k in range(nc)}
    out_shape += [jax.ShapeDtypeStruct(a.shape, a.dtype) for a in carried]
    scratch = [pltpu.VMEM((tm, d), BF16), pltpu.VMEM((tm, d), F32)]
    if nc:
        scratch += [pltpu.SemaphoreType.DMA((3 * nc,))] * 2
    return pl.pallas_call(
        body, name="ffn_fwd_loss" if with_loss else "ffn_fwd",
        grid=(ni, nb), in_specs=in_specs + [ANY] * nc, out_specs=out_specs + [ANY] * nc, out_shape=out_shape,
        input_output_aliases=aliases, scratch_shapes=scratch,
        compiler_params=_params(("arbitrary", "arbitrary")),
    )(*args, *carried)


def _ffn_bwd_act(x, nw, dy, g, u, wg, wu, wd, name):
    s, d = x.shape
    nb, _, fb = wg.shape
    tm = _tile(s)

    def body(x_ref, nw_ref, dy_ref, g_ref, u_ref, wg_ref, wu_ref, wd_ref,
             dx_ref, dg_ref, du_ref, dyb_ref, dnw_ref, dys, acc):
        i = pl.program_id(0)
        j = pl.program_id(1)

        @pl.when(j == 0)
        def _():
            dyb = dy_ref[...].astype(BF16)
            dys[...] = dyb
            dyb_ref[...] = dyb
            acc[...] = jnp.zeros_like(acc)

        da = 0.5 * _dot_nt(dys[...], wd_ref[...])
        gv = g_ref[...]
        sg = _sigmoid(gv)
        dub = (da * (gv * sg)).astype(BF16)
        dgb = (da * u_ref[...] * (sg * (1.0 + gv * (1.0 - sg)))).astype(BF16)
        dg_ref[...] = dgb
        du_ref[...] = dub
        acc[...] += _dot_nt(dgb, wg_ref[...]) + _dot_nt(dub, wu_ref[...])

        @pl.when(j == nb - 1)
        def _():
            xv = x_ref[...]
            dx, dn = _rms_bwd(xv, _rms_r(xv), nw_ref[...], acc[...])
            dx_ref[...] = dy_ref[...] + dx

            @pl.when(i == 0)
            def _():
                dnw_ref[...] = jnp.zeros_like(dnw_ref)

            dnw_ref[...] += dn

    row = pl.BlockSpec((tm, d), lambda i, j: (i, 0))
    vec = pl.BlockSpec((1, d), lambda i, j: (0, 0))
    blk = pl.BlockSpec((None, tm, fb), lambda i, j: (j, i, 0))
    wcol = pl.BlockSpec((None, d, fb), lambda i, j: (j, 0, 0))
    wrow = pl.BlockSpec((None, fb, d), lambda i, j: (j, 0, 0))
    return pl.pallas_call(
        body, name=name, grid=(s // tm, nb),
        in_specs=[row, vec, row, blk, blk, wcol, wcol, wrow],
        out_specs=[row, blk, blk, row, vec],
        out_shape=[jax.ShapeDtypeStruct((s, d), F32), jax.ShapeDtypeStruct((nb, s, fb), BF16),
                   jax.ShapeDtypeStruct((nb, s, fb), BF16), jax.ShapeDtypeStruct((s, d), BF16),
                   jax.ShapeDtypeStruct((1, d), F32)],
        scratch_shapes=[pltpu.VMEM((tm, d), BF16), pltpu.VMEM((tm, d), F32)],
        compiler_params=_params(("arbitrary", "arbitrary")),
    )(x, nw, dy, g, u, wg, wu, wd)


def _wgrad(a, b, a_spec, b_spec, out_rows, out_cols, scale, name):
    s = a.shape[-2]
    tk = _tile(s)
    nk = s // tk

    def body(a_ref, b_ref, out_ref, acc):
        k = pl.program_id(1)

        @pl.when(k == 0)
        def _():
            acc[...] = jnp.zeros_like(acc)

        acc[...] += _dot_tn(a_ref[...], b_ref[...])

        @pl.when(k == nk - 1)
        def _():
            out_ref[...] = (acc[...] * scale).astype(BF16)

    return pl.pallas_call(
        body, name=name, grid=(N_CHIPS, nk),
        in_specs=[a_spec(tk), b_spec(tk)],
        out_specs=pl.BlockSpec((None, out_rows, out_cols), lambda j, k: (j, 0, 0)),
        out_shape=jax.ShapeDtypeStruct((N_CHIPS, out_rows, out_cols), BF16),
        scratch_shapes=[pltpu.VMEM((out_rows, out_cols), F32)],
        compiler_params=_params(("arbitrary", "arbitrary")),
    )(a, b)


def _ffn_wgrads(hb, ab, dg, du, dyb, tag):
    s, d = hb.shape
    fb = ab.shape[-1]
    shared = lambda cols: (lambda tk: pl.BlockSpec((tk, cols), lambda j, k: (k, 0)))
    stacked = lambda cols: (lambda tk: pl.BlockSpec((None, tk, cols), lambda j, k: (j, k, 0)))
    dwg = _wgrad(hb, dg, shared(d), stacked(fb), d, fb, 1.0, "wgrad_gate_" + tag)
    dwu = _wgrad(hb, du, shared(d), stacked(fb), d, fb, 1.0, "wgrad_up_" + tag)
    dwd = _wgrad(ab, dyb, stacked(fb), shared(d), fb, d, 0.5, "wgrad_down_" + tag)
    return dwg, dwu, dwd


def _mix_pre(x, nw, win):
    s, d = x.shape
    nb, _, cb = win.shape
    tm = _tile(s)

    def body(x_ref, nw_ref, w_ref, p_ref, hb_ref, hs):
        @pl.when(pl.program_id(1) == 0)
        def _():
            xv = x_ref[...]
            hb = (xv * _rms_r(xv) * nw_ref[...]).astype(BF16)
            hs[...] = hb
            hb_ref[...] = hb

        p_ref[...] = _dot(hs[...], w_ref[...])

    row = pl.BlockSpec((tm, d), lambda i, j: (i, 0))
    return pl.pallas_call(
        body, name="mix_pre", grid=(s // tm, nb),
        in_specs=[row, pl.BlockSpec((1, d), lambda i, j: (0, 0)),
                  pl.BlockSpec((None, d, cb), lambda i, j: (j, 0, 0))],
        out_specs=[pl.BlockSpec((tm, cb), lambda i, j: (i, j)), row],
        out_shape=[jax.ShapeDtypeStruct((s, nb * cb), F32), jax.ShapeDtypeStruct((s, d), BF16)],
        scratch_shapes=[pltpu.VMEM((tm, d), BF16)],
        compiler_params=_params(("arbitrary", "arbitrary")),
    )(x, nw, win)


def _mix_pre_bwd(x, nw, dres, dpb, win):
    s, d = x.shape
    nb, _, cb = win.shape
    tm = _tile(s)

    def body(x_ref, nw_ref, dres_ref, dp_ref, w_ref, dx_ref, dnw_ref, acc):
        i = pl.program_id(0)
        j = pl.program_id(1)

        @pl.when(j == 0)
        def _():
            acc[...] = jnp.zeros_like(acc)

        acc[...] += _dot_nt(dp_ref[...], w_ref[...])

        @pl.when(j == nb - 1)
        def _():
            xv = x_ref[...]
            dx, dn = _rms_bwd(xv, _rms_r(xv), nw_ref[...], acc[...])
            dx_ref[...] = dres_ref[...] + dx

            @pl.when(i == 0)
            def _():
                dnw_ref[...] = jnp.zeros_like(dnw_ref)

            dnw_ref[...] += dn

    row = pl.BlockSpec((tm, d), lambda i, j: (i, 0))
    vec = pl.BlockSpec((1, d), lambda i, j: (0, 0))
    return pl.pallas_call(
        body, name="mix_pre_bwd", grid=(s // tm, nb),
        in_specs=[row, vec, row, pl.BlockSpec((tm, cb), lambda i, j: (i, j)),
                  pl.BlockSpec((None, d, cb), lambda i, j: (j, 0, 0))],
        out_specs=[row, vec],
        out_shape=[jax.ShapeDtypeStruct((s, d), F32), jax.ShapeDtypeStruct((1, d), F32)],
        scratch_shapes=[pltpu.VMEM((tm, d), F32)],
        compiler_params=_params(("arbitrary", "arbitrary")),
    )(x, nw, dres, dpb, win)


def _mix_post(x, yr, ya, nr, na, wout):
    s, d = x.shape
    h = yr.shape[1]
    tm = _tile(s)

    def body(x_ref, yr_ref, ya_ref, nr_ref, na_ref, w_ref, out_ref):
        yrv = yr_ref[...]
        yav = ya_ref[...]
        onb = (yrv * _rms_r(yrv) * nr_ref[...]).astype(BF16)
        oab = (yav * _rms_r(yav) * na_ref[...]).astype(BF16)
        out_ref[...] = x_ref[...] + _dot(onb, w_ref[0:h, :]) + _dot(oab, w_ref[h:2 * h, :])

    row = pl.BlockSpec((tm, d), lambda i: (i, 0))
    half = pl.BlockSpec((tm, h), lambda i: (i, 0))
    vec = pl.BlockSpec((1, h), lambda i: (0, 0))
    return pl.pallas_call(
        body, name="mix_post", grid=(s // tm,),
        in_specs=[row, half, half, vec, vec, pl.BlockSpec((2 * h, d), lambda i: (0, 0))],
        out_specs=row, out_shape=jax.ShapeDtypeStruct((s, d), F32),
        compiler_params=_params(("arbitrary",)),
    )(x, yr, ya, nr, na, wout)


def _mix_post_bwd(dx, yr, ya, nr, na, wout):
    s, d = dx.shape
    h = yr.shape[1]
    tm = _tile(s)

    def body(dx_ref, yr_ref, ya_ref, nr_ref, na_ref, w_ref,
             dyr_ref, dya_ref, yc_ref, dxb_ref, dnr_ref, dna_ref):
        i = pl.program_id(0)
        dxb = dx_ref[...].astype(BF16)
        dxb_ref[...] = dxb
        dyc = _dot_nt(dxb, w_ref[...])
        yrv = yr_ref[...]
        yav = ya_ref[...]
        rr = _rms_r(yrv)
        ra = _rms_r(yav)
        yc_ref[:, 0:h] = (yrv * rr * nr_ref[...]).astype(BF16)
        yc_ref[:, h:2 * h] = (yav * ra * na_ref[...]).astype(BF16)
        dyr, dnr = _rms_bwd(yrv, rr, nr_ref[...], dyc[:, 0:h])
        dya, dna = _rms_bwd(yav, ra, na_ref[...], dyc[:, h:2 * h])
        dyr_ref[...] = dyr
        dya_ref[...] = dya

        @pl.when(i == 0)
        def _():
            dnr_ref[...] = jnp.zeros_like(dnr_ref)
            dna_ref[...] = jnp.zeros_like(dna_ref)

        dnr_ref[...] += dnr
        dna_ref[...] += dna

    row = pl.BlockSpec((tm, d), lambda i: (i, 0))
    half = pl.BlockSpec((tm, h), lambda i: (i, 0))
    vec = pl.BlockSpec((1, h), lambda i: (0, 0))
    return pl.pallas_call(
        body, name="mix_post_bwd", grid=(s // tm,),
        in_specs=[row, half, half, vec, vec, pl.BlockSpec((2 * h, d), lambda i: (0, 0))],
        out_specs=[half, half, pl.BlockSpec((tm, 2 * h), lambda i: (i, 0)), row, vec, vec],
        out_shape=[jax.ShapeDtypeStruct((s, h), F32), jax.ShapeDtypeStruct((s, h), F32),
                   jax.ShapeDtypeStruct((s, 2 * h), BF16), jax.ShapeDtypeStruct((s, d), BF16),
                   jax.ShapeDtypeStruct((1, h), F32), jax.ShapeDtypeStruct((1, h), F32)],
        compiler_params=_params(("arbitrary",)),
    )(dx, yr, ya, nr, na, wout)


def _shift_down(xv, s, prev8):
    rolled = pltpu.roll(xv, s, 0)
    row8 = lax.broadcasted_iota(jnp.int32, prev8.shape, 0)
    head = jnp.where(row8 < s, pltpu.roll(prev8, s, 0), rolled[0:8, :])
    return jnp.concatenate([head, rolled[8:, :]], axis=0)


def _shift_up(xv, s, next8):
    n = xv.shape[0]
    rolled = pltpu.roll(xv, n - s, 0)
    row8 = lax.broadcasted_iota(jnp.int32, next8.shape, 0)
    tail = jnp.where(row8 >= 8 - s, pltpu.roll(next8, 8 - s, 0), rolled[n - 8:, :])
    return jnp.concatenate([rolled[:n - 8, :], tail], axis=0)


def _scan_fwd(a, b):
    n = a.shape[0]
    row = lax.broadcasted_iota(jnp.int32, a.shape, 0)
    s = 1
    while s < n:
        ok = row >= s
        b = jnp.where(ok, a * pltpu.roll(b, s, 0) + b, b)
        a = jnp.where(ok, a * pltpu.roll(a, s, 0), a)
        s *= 2
    return b


def _scan_bwd(a, b):
    n = a.shape[0]
    row = lax.broadcasted_iota(jnp.int32, a.shape, 0)
    s = 1
    while s < n:
        ok = row < n - s
        b = jnp.where(ok, a * pltpu.roll(b, n - s, 0) + b, b)
        a = jnp.where(ok, a * pltpu.roll(a, n - s, 0), a)
        s *= 2
    return b


def _rglru_gates(xv, prev8, cw_ref, cb_ref, wa_ref, ba_ref, wx_ref, bx_ref, lam_ref):
    x1 = _shift_down(xv, 1, prev8)
    x2 = _shift_down(xv, 2, prev8)
    x3 = _shift_down(xv, 3, prev8)
    xc = cw_ref[3:4, :] * xv + cw_ref[2:3, :] * x1 + cw_ref[1:2, :] * x2 + cw_ref[0:1, :] * x3 + cb_ref[...]
    xcb = xc.astype(BF16)
    r = _sigmoid(_dot(xcb, wa_ref[...]) + ba_ref[...])
    ig = _sigmoid(_dot(xcb, wx_ref[...]) + bx_ref[...])
    c = RG_C * _log_sigmoid(lam_ref[...])
    la = r * c
    a = jnp.exp(la)
    m = jnp.sqrt(-_expm1_neg(2.0 * la))
    return (x1, x2, x3), xc, xcb, r, ig, c, a, m


def _rglru_fwd(proj, cw, cb, wa, ba, wx, bx, lam):
    s = proj.shape[0]
    w = D_RNN
    tm = _tile(s)

    def body(xr_ref, gate_ref, cw_ref, cb_ref, wa_ref, ba_ref, wx_ref, bx_ref, lam_ref,
             y_ref, h_ref, prev, hlast):
        @pl.when(pl.program_id(0) == 0)
        def _():
            prev[...] = jnp.zeros_like(prev)
            hlast[...] = jnp.zeros_like(hlast)

        xv = xr_ref[...]
        _, xc, _, _, ig, _, a, m = _rglru_gates(xv, prev[...], cw_ref, cb_ref, wa_ref, ba_ref,
                                                wx_ref, bx_ref, lam_ref)
        b = m * (ig * xc)
        row = lax.broadcasted_iota(jnp.int32, b.shape, 0)
        b = jnp.where(row == 0, b + a * hlast[...], b)
        h = _scan_fwd(a, b)
        h_ref[...] = h
        y_ref[...] = h * _gelu(gate_ref[...])
        prev[...] = xv[tm - 8:, :]
        hlast[...] = h[tm - 1:tm, :]

    vec = pl.BlockSpec((1, w), lambda i: (0, 0))
    sq = pl.BlockSpec((w, w), lambda i: (0, 0))
    out = pl.BlockSpec((tm, w), lambda i: (i, 0))
    return pl.pallas_call(
        body, name="rglru_fwd", grid=(s // tm,),
        in_specs=[pl.BlockSpec((tm, w), lambda i: (i, 0)), pl.BlockSpec((tm, w), lambda i: (i, 1)),
                  pl.BlockSpec((CONV_W, w), lambda i: (0, 0)), vec, sq, vec, sq, vec, vec],
        out_specs=[out, out],
        out_shape=[jax.ShapeDtypeStruct((s, w), F32), jax.ShapeDtypeStruct((s, w), F32)],
        scratch_shapes=[pltpu.VMEM((8, w), F32), pltpu.VMEM((1, w), F32)],
        compiler_params=_params(("arbitrary",)),
    )(proj, proj, cw, cb, wa, ba, wx, bx, lam)


def _rglru_bwd(proj, hseq, dyr, cw, cb, wa, ba, wx, bx, lam):
    s = proj.shape[0]
    w = D_RNN
    tm = _tile(s)
    nt = s // tm
    t8 = tm // 8

    def body(xr_ref, xp_ref, gate_ref, h_ref, hp_ref, dy_ref, cw_ref, cb_ref, wa_ref, ba_ref,
             wx_ref, bx_ref, lam_ref,
             dxr_ref, dgate_ref, dcw_ref, dcb_ref, dwa_ref, dba_ref, dwx_ref, dbx_ref, dlam_ref,
             carry, dxc_next):
        i = pl.program_id(0)
        first_tile = i == nt - 1

        @pl.when(i == 0)
        def _():
            carry[...] = jnp.zeros_like(carry)
            dxc_next[...] = jnp.zeros_like(dxc_next)
            for ref in (dcw_ref, dcb_ref, dwa_ref, dba_ref, dwx_ref, dbx_ref, dlam_ref):
                ref[...] = jnp.zeros_like(ref)

        xv = xr_ref[...]
        prev8 = jnp.where(first_tile, 0.0, xp_ref[...])
        hprev8 = jnp.where(first_tile, 0.0, hp_ref[...])
        (x1, x2, x3), xc, xcb, r, ig, c, a, m = _rglru_gates(
            xv, prev8, cw_ref, cb_ref, wa_ref, ba_ref, wx_ref, bx_ref, lam_ref)
        gv = gate_ref[...]
        hv = h_ref[...]
        dy = dy_ref[...]
        dgate_ref[...] = (dy * hv * _gelu_grad(gv)).astype(BF16)
        dh = dy * _gelu(gv)
        row = lax.broadcasted_iota(jnp.int32, dh.shape, 0)
        dh = jnp.where(row == tm - 1, dh + carry[...], dh)
        a_up = jnp.where(row == tm - 1, 0.0, pltpu.roll(a, tm - 1, 0))
        lam_t = _scan_bwd(a_up, dh)
        carry[...] = a[0:1, :] * lam_t[0:1, :]
        hm1 = _shift_down(hv, 1, hprev8)
        da = lam_t * hm1
        ixc = ig * xc
        dm = lam_t * ixc
        dig = lam_t * m * xc
        dxc = lam_t * m * ig
        dla = da * a - dm * (a * a) / m
        dr = dla * c
        dlam_ref[...] += jnp.sum(dla * r, axis=0, keepdims=True)
        dpa = dr * r * (1.0 - r)
        dpi = dig * ig * (1.0 - ig)
        dba_ref[...] += jnp.sum(dpa, axis=0, keepdims=True)
        dbx_ref[...] += jnp.sum(dpi, axis=0, keepdims=True)
        dpab = dpa.astype(BF16)
        dpib = dpi.astype(BF16)
        dwa_ref[...] += _dot_tn(xcb, dpab)
        dwx_ref[...] += _dot_tn(xcb, dpib)
        dxc = dxc + _dot_nt(dpab, wa_ref[...]) + _dot_nt(dpib, wx_ref[...])
        dcb_ref[...] += jnp.sum(dxc, axis=0, keepdims=True)
        dcw_ref[3:4, :] += jnp.sum(dxc * xv, axis=0, keepdims=True)
        dcw_ref[2:3, :] += jnp.sum(dxc * x1, axis=0, keepdims=True)
        dcw_ref[1:2, :] += jnp.sum(dxc * x2, axis=0, keepdims=True)
        dcw_ref[0:1, :] += jnp.sum(dxc * x3, axis=0, keepdims=True)
        nxt = dxc_next[...]
        dxr = (cw_ref[3:4, :] * dxc + cw_ref[2:3, :] * _shift_up(dxc, 1, nxt)
               + cw_ref[1:2, :] * _shift_up(dxc, 2, nxt) + cw_ref[0:1, :] * _shift_up(dxc, 3, nxt))
        dxr_ref[...] = dxr.astype(BF16)
        dxc_next[...] = dxc[0:8, :]

        @pl.when(first_tile)
        def _():
            lv = lam_ref[...]
            dlam_ref[...] = dlam_ref[...] * (RG_C * _sigmoid(-lv))

    rev = lambda i: nt - 1 - i
    vec = pl.BlockSpec((1, w), lambda i: (0, 0))
    sq = pl.BlockSpec((w, w), lambda i: (0, 0))
    cur = lambda col: pl.BlockSpec((tm, w), lambda i: (rev(i), col))
    before = lambda cols: pl.BlockSpec((8, w), lambda i: (jnp.maximum(rev(i) * t8 - 1, 0), 0))
    return pl.pallas_call(
        body, name="rglru_bwd", grid=(nt,),
        in_specs=[cur(0), before(None), cur(1), cur(0), before(None), cur(0),
                  pl.BlockSpec((CONV_W, w), lambda i: (0, 0)), vec, sq, vec, sq, vec, vec],
        out_specs=[cur(0), cur(0), pl.BlockSpec((CONV_W, w), lambda i: (0, 0)), vec, sq, vec, sq, vec, vec],
        out_shape=[jax.ShapeDtypeStruct((s, w), BF16), jax.ShapeDtypeStruct((s, w), BF16),
                   jax.ShapeDtypeStruct((CONV_W, w), F32), jax.ShapeDtypeStruct((1, w), F32),
                   jax.ShapeDtypeStruct((w, w), F32), jax.ShapeDtypeStruct((1, w), F32),
                   jax.ShapeDtypeStruct((w, w), F32), jax.ShapeDtypeStruct((1, w), F32),
                   jax.ShapeDtypeStruct((1, w), F32)],
        scratch_shapes=[pltpu.VMEM((1, w), F32), pltpu.VMEM((8, w), F32)],
        compiler_params=_params(("arbitrary",)),
    )(proj, proj, proj, hseq, hseq, dyr, cw, cb, wa, ba, wx, bx, lam)


def _sb_logs(z, valid):
    l1p = jnp.log(1.0 + jnp.exp(-jnp.abs(z)))
    lb = jnp.minimum(z, 0.0) - l1p
    lm = jnp.where(valid, -jnp.maximum(z, 0.0) - l1p, 0.0)
    return lb, lm


class _Window:
    def __init__(self):
        blk, win, cut = ATT_BLOCK, ATT_WINDOW, ATT_SPLIT
        self.row = lax.broadcasted_iota(jnp.int32, (blk, win), 0)
        self.col = lax.broadcasted_iota(jnp.int32, (blk, win), 1)

        def tri(n, later):
            j = lax.broadcasted_iota(jnp.int32, (n, n), 0)
            s = lax.broadcasted_iota(jnp.int32, (n, n), 1)
            return jnp.where((j > s) if later else (j < s), 1.0, 0.0).astype(BF16)

        self.later = (tri(cut, True), tri(win - cut, True))
        self.earlier = (tri(cut, False), tri(win - cut, False))

    def place(self, qi, g):
        end = (qi + 1) * ATT_BLOCK - g * ATT_WINDOW
        start = pl.multiple_of(jnp.maximum(end - ATT_WINDOW, 0), ATT_BLOCK)
        valid = start + self.col < jnp.minimum(qi * ATT_BLOCK + self.row, end)
        return start, valid

    @staticmethod
    def _parts(xv):
        hi = xv.astype(BF16)
        lo = (xv - hi.astype(F32)).astype(BF16)
        cut = ATT_SPLIT
        sums = (jnp.sum(xv[:, :cut], axis=1, keepdims=True), jnp.sum(xv[:, cut:], axis=1, keepdims=True))
        return (hi[:, :cut], lo[:, :cut]), (hi[:, cut:], lo[:, cut:]), sums

    def sums_after(self, xv, carry):
        (h0, l0), (h1, l1), (s0, s1) = self._parts(xv)
        first = _dot(h0, self.later[0]) + _dot(l0, self.later[0]) + (s1 + carry)
        last = _dot(h1, self.later[1]) + _dot(l1, self.later[1]) + carry
        return jnp.concatenate([first, last], axis=1), s0 + s1

    def sums_before(self, xv, carry):
        (h0, l0), (h1, l1), (s0, s1) = self._parts(xv)
        first = _dot(h0, self.earlier[0]) + _dot(l0, self.earlier[0]) + carry
        last = _dot(h1, self.earlier[1]) + _dot(l1, self.earlier[1]) + (s0 + carry)
        return jnp.concatenate([first, last], axis=1), s0 + s1


def _head_lanes(hh):
    return slice(hh * HEAD_DIM, (hh + 1) * HEAD_DIM)


def _attn_fwd(proj, qg, kg):
    s = proj.shape[0]
    blk, win, dh = ATT_BLOCK, ATT_WINDOW, HEAD_DIM
    nq = s // blk
    scale = 1.0 / math.sqrt(dh)
    assert s >= win and s % blk == 0

    def body(q_ref, k_ref, v_ref, qg_ref, kg_ref, o_ref, qn, kn, vb, ob):
        wd = _Window()
        for hh in range(2):
            lanes = _head_lanes(hh)
            qv = q_ref[:, lanes]
            qn[...] = (qv * _rms_r(qv) * qg_ref[...] * scale).astype(BF16)
            kv = k_ref[:, lanes]
            kn[...] = (kv * _rms_r(kv) * kg_ref[...]).astype(BF16)
            vb[...] = v_ref[:, lanes].astype(BF16)

            def q_step(qi, _):
                qoff = pl.multiple_of(qi * blk, blk)
                qt = qn[pl.ds(qoff, blk), :]

                def more(carry):
                    g, live, _, _ = carry
                    return jnp.logical_and((qi + 1) * blk - g * win > 0, live > 0)

                def window(carry):
                    g, _, acc, run = carry
                    start, valid = wd.place(qi, g)
                    z = _dot_nt(qt, kn[pl.ds(start, win), :])
                    lb, lm = _sb_logs(z, valid)
                    tail, total = wd.sums_after(lm, run)
                    wgt = jnp.where(valid, jnp.exp(lb + tail), 0.0)
                    acc = acc + _dot(wgt.astype(BF16), vb[pl.ds(start, win), :])
                    run = run + total
                    live = (jnp.max(run) > EXP_ZERO).astype(jnp.int32)
                    return g + 1, live, acc, run

                _, _, acc, _ = lax.while_loop(
                    more, window, (jnp.int32(0), jnp.int32(1), jnp.zeros((blk, dh), F32), jnp.zeros((blk, 1), F32)))
                ob[pl.ds(qoff, blk), :] = acc
                return 0

            lax.fori_loop(0, nq, q_step, 0)
            o_ref[:, lanes] = ob[...]

    pair = lambda group: pl.BlockSpec((s, 2 * dh), lambda h: (0, group * (D_ATT // (2 * dh)) + h))
    vec = pl.BlockSpec((1, dh), lambda h: (0, 0))
    return pl.pallas_call(
        body, name="attn_fwd", grid=(N_HEADS // 2,),
        in_specs=[pair(2), pair(3), pair(4), vec, vec], out_specs=pair(0),
        out_shape=jax.ShapeDtypeStruct((s, D_ATT), F32),
        scratch_shapes=[pltpu.VMEM((s, dh), BF16)] * 3 + [pltpu.VMEM((s, dh), F32)],
        compiler_params=_params(("arbitrary",)),
    )(proj, proj, proj, qg, kg)


def _attn_bwd(proj, dya, qg, kg, sums, slots):
    s = proj.shape[0]
    blk, win, dh = ATT_BLOCK, ATT_WINDOW, HEAD_DIM
    nq = s // blk
    max_windows = -(-s // win) + 1
    scale = 1.0 / math.sqrt(dh)
    steps = N_HEADS // 2
    nx = len(sums)
    assert s >= win and s % blk == 0

    def body(*refs):
        q_ref, k_ref, v_ref, do_ref, qg_ref, kg_ref = refs[:6]
        sum_refs = refs[6:6 + nx]
        at = 6 + 2 * nx
        dq_ref, dk_ref, dv_ref, dqg_ref, dkg_ref = refs[at:at + 5]
        slot_refs = refs[at + 5:at + 5 + nx]
        qn, kn, vb, dob, runs, dqn, dkn, dvn, send_sems, recv_sems = refs[at + 5 + nx:]
        wd = _Window()

        @pl.when(pl.program_id(0) == 0)
        def _():
            dqg_ref[...] = jnp.zeros_like(dqg_ref)
            dkg_ref[...] = jnp.zeros_like(dkg_ref)
            _start(_chip_copies(sum_refs, slot_refs, send_sems, recv_sems)[0])

        for hh in range(2):
            lanes = _head_lanes(hh)
            qv = q_ref[:, lanes]
            qn[...] = (qv * _rms_r(qv) * qg_ref[...] * scale).astype(BF16)
            kv = k_ref[:, lanes]
            kn[...] = (kv * _rms_r(kv) * kg_ref[...]).astype(BF16)
            vb[...] = v_ref[:, lanes].astype(BF16)
            dob[...] = do_ref[:, lanes].astype(BF16)
            dkn[...] = jnp.zeros_like(dkn)
            dvn[...] = jnp.zeros_like(dvn)

            def q_step(qi, _):
                qoff = pl.multiple_of(qi * blk, blk)
                qt = qn[pl.ds(qoff, blk), :]
                dot = dob[pl.ds(qoff, blk), :]

                def more(carry):
                    g, live, _ = carry
                    return jnp.logical_and((qi + 1) * blk - g * win > 0, live > 0)

                def run_window(carry):
                    g, _, run = carry
                    runs[g] = run
                    start, valid = wd.place(qi, g)
                    z = _dot_nt(qt, kn[pl.ds(start, win), :])
                    _, lm = _sb_logs(z, valid)
                    run = run + jnp.sum(lm, axis=1, keepdims=True)
                    live = (jnp.max(run) > EXP_ZERO).astype(jnp.int32)
                    return g + 1, live, run

                windows, _, _ = lax.while_loop(
                    more, run_window, (jnp.int32(0), jnp.int32(1), jnp.zeros((blk, 1), F32)))

                def k_window(gg, carry):
                    dq_acc, esum = carry
                    g = windows - 1 - gg
                    start, valid = wd.place(qi, g)
                    kt = kn[pl.ds(start, win), :]
                    vt = vb[pl.ds(start, win), :]
                    z = _dot_nt(qt, kt)
                    lb, lm = _sb_logs(z, valid)
                    tail, _ = wd.sums_after(lm, runs[g])
                    wgt = jnp.where(valid, jnp.exp(lb + tail), 0.0)
                    e = _dot_nt(dot, vt) * wgt
                    before, etotal = wd.sums_before(e, esum)
                    beta = jnp.exp(lb)
                    dz = jnp.where(valid, e * (1.0 - beta) - before * beta, 0.0)
                    dzb = dz.astype(BF16)
                    dq_acc = dq_acc + _dot(dzb, kt)
                    dkn[pl.ds(start, win), :] += _dot_tn(dzb, qt)
                    dvn[pl.ds(start, win), :] += _dot_tn(wgt.astype(BF16), dot)
                    return dq_acc, esum + etotal

                dq_acc, _ = lax.fori_loop(0, windows, k_window,
                                          (jnp.zeros((blk, dh), F32), jnp.zeros((blk, 1), F32)))
                dqn[pl.ds(qoff, blk), :] = dq_acc
                return 0

            lax.fori_loop(0, nq, q_step, 0)

            dq, dqg = _rms_bwd(qv, _rms_r(qv), qg_ref[...] * scale, dqn[...])
            dq_ref[:, lanes] = dq.astype(BF16)
            dqg_ref[...] += dqg * scale
            dk, dkg = _rms_bwd(kv, _rms_r(kv), kg_ref[...], dkn[...])
            dk_ref[:, lanes] = dk.astype(BF16)
            dkg_ref[...] += dkg
            dv_ref[:, lanes] = dvn[...].astype(BF16)

        @pl.when(pl.program_id(0) == steps - 1)
        def _():
            _finish(*_chip_copies(sum_refs, slot_refs, send_sems, recv_sems))

    pair = lambda group: pl.BlockSpec((s, 2 * dh), lambda h: (0, group * (D_ATT // (2 * dh)) + h))
    vec = pl.BlockSpec((1, dh), lambda h: (0, 0))
    outs = pl.pallas_call(
        body, name="attn_bwd", grid=(steps,),
        in_specs=[pair(2), pair(3), pair(4), pair(0), vec, vec] + [ANY] * (2 * nx),
        out_specs=[pair(0), pair(0), pair(0), vec, vec] + [ANY] * nx,
        out_shape=[jax.ShapeDtypeStruct((s, D_ATT), BF16)] * 3 + [jax.ShapeDtypeStruct((1, dh), F32)] * 2
        + [jax.ShapeDtypeStruct(a.shape, a.dtype) for a in slots],
        input_output_aliases={6 + nx + k: 5 + k for k in range(nx)},
        scratch_shapes=[pltpu.VMEM((s, dh), BF16)] * 4 + [pltpu.VMEM((max_windows, blk, 1), F32)]
        + [pltpu.VMEM((s, dh), F32)] * 3 + [pltpu.SemaphoreType.DMA((3 * nx,))] * 2,
        compiler_params=_params(("arbitrary",)),
    )(proj, proj, proj, dya, qg, kg, *sums, *slots)
    return outs[:5], outs[5:]


def _block_diag(w):
    n, c, d = w.shape
    return jnp.einsum("ncd,nm->ncmd", w, jnp.eye(n, dtype=w.dtype)).reshape(n * c, n * d)


def _diag_blocks(full, n):
    c = full.shape[0] // n
    return jnp.stack([full[i * c:(i + 1) * c, i * c:(i + 1) * c] for i in range(n)])


FIRST = ["ffn1_w_gate", "ffn1_w_up", "ffn1_w_down"]
LATER = ["w_in", "w_out", "ffn2_w_gate", "ffn2_w_up", "ffn2_w_down"]
EARLY_GRADS = ["ffn2_w_gate", "ffn2_w_up", "ffn2_w_down", "w_out"]
LATE_GRADS = ["w_in", "ffn1_w_gate", "ffn1_w_up", "ffn1_w_down"]


def _pair_sums(gb, names, where):
    theirs = _pair_exchange([gb[n] for n in names])
    return zip(*[_pair_sum(gb[n], t, where, "pair_sum_" + n) for n, t in zip(names, theirs)])


def _local_step(x, tgt, stacks, conv_stack, small, where):
    big = dict(zip(FIRST, _gather_weights([stacks[n] for n in FIRST], [])))
    wa = _block_diag(small["rg_w_a"]).astype(BF16)
    wx = _block_diag(small["rg_w_x"]).astype(BF16)

    x1, g1, u1, hb1, ab1, *landed = _ffn_fwd(x, small["ffn1_norm"], big["ffn1_w_gate"], big["ffn1_w_up"],
                                             big["ffn1_w_down"], gather=([stacks[n] for n in LATER], [conv_stack]))
    big.update(zip(LATER, _forward_weights(landed[:len(LATER)])))
    conv_w = jnp.transpose(landed[-1], (1, 0, 2)).reshape(CONV_W, D_RNN)
    wout = big["w_out"].reshape(D_MODEL, D_MODEL)
    rg = (conv_w, small["conv_b"], wa, small["rg_b_a"], wx, small["rg_b_x"], small["rg_lambda"])
    proj, hb2 = _mix_pre(x1, small["mix_norm"], big["w_in"])
    yr, hseq = _rglru_fwd(proj, *rg)
    ya = _attn_fwd(proj, small["q_norm"], small["k_norm"])
    x2 = _mix_post(x1, yr, ya, small["rnn_out_norm"], small["attn_out_norm"], wout)
    dx3, g2, u2, hb3, ab3, loss = _ffn_fwd(x2, small["ffn2_norm"], big["ffn2_w_gate"], big["ffn2_w_up"],
                                          big["ffn2_w_down"], tgt)

    gb, gs = {}, {}
    dx2, dg2, du2, dyb2, gs["ffn2_norm"] = _ffn_bwd_act(
        x2, small["ffn2_norm"], dx3, g2, u2, big["ffn2_w_gate"], big["ffn2_w_up"], big["ffn2_w_down"], "ffn2_bwd")
    gb["ffn2_w_gate"], gb["ffn2_w_up"], gb["ffn2_w_down"] = _ffn_wgrads(hb3, ab3, dg2, du2, dyb2, "ffn2")

    dyr, dya, ycat, dxb2, gs["rnn_out_norm"], gs["attn_out_norm"] = _mix_post_bwd(
        dx2, yr, ya, small["rnn_out_norm"], small["attn_out_norm"], wout)
    quarter = D_MODEL // N_CHIPS
    gb["w_out"] = _wgrad(ycat, dxb2, lambda tk: pl.BlockSpec((tk, quarter), lambda j, k: (k, j)),
                         lambda tk: pl.BlockSpec((tk, D_MODEL), lambda j, k: (k, 0)),
                         quarter, D_MODEL, 1.0, "wgrad_out")
    pair, own = _pair_sums(gb, EARLY_GRADS, where)
    (dq, dk, dv, gs["q_norm"], gs["k_norm"]), early = _attn_bwd(
        proj, dya, small["q_norm"], small["k_norm"], list(pair), list(own))
    dxr, dgate, gs["conv_w"], gs["conv_b"], dwa, gs["rg_b_a"], dwx, gs["rg_b_x"], gs["rg_lambda"] = _rglru_bwd(
        proj, hseq, dyr, *rg)
    gs["rg_w_a"] = _diag_blocks(dwa, RNN_BLOCKS)
    gs["rg_w_x"] = _diag_blocks(dwx, RNN_BLOCKS)
    dpb = jnp.concatenate([dxr, dgate, dq, dk, dv], axis=1)
    cb = N_IN // N_CHIPS
    gb["w_in"] = _wgrad(hb2, dpb, lambda tk: pl.BlockSpec((tk, D_MODEL), lambda j, k: (k, 0)),
                        lambda tk: pl.BlockSpec((tk, cb), lambda j, k: (k, j)),
                        D_MODEL, cb, 1.0, "wgrad_in")
    dx1, gs["mix_norm"] = _mix_pre_bwd(x1, small["mix_norm"], dx2, dpb, big["w_in"])

    dx0, dg1, du1, dyb1, gs["ffn1_norm"] = _ffn_bwd_act(
        x, small["ffn1_norm"], dx1, g1, u1, big["ffn1_w_gate"], big["ffn1_w_up"], big["ffn1_w_down"], "ffn1_bwd")
    gb["ffn1_w_gate"], gb["ffn1_w_up"], gb["ffn1_w_down"] = _ffn_wgrads(hb1, ab1, dg1, du1, dyb1, "ffn1")
    pair, own = _pair_sums(gb, LATE_GRADS, where)
    slots = dict(zip(EARLY_GRADS, early))
    slots.update(zip(LATE_GRADS, _chip_exchange(list(pair), list(own))))
    return loss[0, 0], dx0, slots, gs


ANY = pl.BlockSpec(memory_space=pl.ANY)


def _place():
    x, y, c = lax.axis_index("x"), lax.axis_index("y"), lax.axis_index("c")
    other_chips = [(1 - x, y), (x, 1 - y), (1 - x, 1 - y)]
    return x, y, c, 2 * x + y, other_chips


def _remote(src, dst, send_sem, recv_sem, to):
    return pltpu.make_async_remote_copy(src_ref=src, dst_ref=dst, send_sem=send_sem, recv_sem=recv_sem,
                                        device_id=to, device_id_type=MESH)


def _copy_plan(pairs):
    sends = [functools.partial(_remote, *a) for a, _ in pairs]
    arrivals = [functools.partial(_remote, *b) for _, b in pairs]
    return sends, arrivals


def _start(makers):
    for make in makers:
        make().start()


def _finish(sends, arrivals):
    for make in arrivals:
        make().wait_recv()
    for make in sends:
        make().wait_send()


def _half(rows, c):
    return pl.ds(pl.multiple_of(c * rows, 16), rows)


def _gather_weights(split, whole):
    arrs = list(split) + list(whole)
    n, ns = len(arrs), len(split)

    def body(*refs):
        outs = refs[n:2 * n]
        send_sems, recv_sems, fsend_sems, frecv_sems = refs[2 * n:]
        sends, arrivals = _gather_ici(outs, ns, send_sems, recv_sems)
        passes, passed = _gather_d2d(outs[:ns], fsend_sems, frecv_sems)
        _start(sends)
        for k, make in enumerate(arrivals):
            make().wait_recv()
            if k < 3 * ns:
                passes[k]().start()
        _finish(sends + passes, passed)

    return pl.pallas_call(
        body, name="gather_weights",
        in_specs=[ANY] * n, out_specs=[ANY] * n,
        out_shape=[jax.ShapeDtypeStruct(a.shape, a.dtype) for a in arrs],
        input_output_aliases={i: i for i in range(n)},
        scratch_shapes=[pltpu.SemaphoreType.DMA((3 * n,)), pltpu.SemaphoreType.DMA((3 * n,)),
                        pltpu.SemaphoreType.DMA((3 * ns,)), pltpu.SemaphoreType.DMA((3 * ns,))],
    )(*arrs)


def _gather_ici(stacks, n_split, send_sems, recv_sems):
    x, y, c, me, chips = _place()

    def region(i, chip):
        if i < n_split:
            return stacks[i].at[chip, _half(stacks[i].shape[1] // 2, c)]
        return stacks[i].at[chip]

    pairs = []
    for i in range(len(stacks)):
        for p, (cx, cy) in enumerate(chips):
            k = 3 * i + p
            mine, got = region(i, me), region(i, 2 * cx + cy)
            sems, to = (send_sems.at[k], recv_sems.at[k]), (cx, cy, c)
            pairs.append(((mine, mine, *sems, to), (got, got, *sems, to)))
    return _copy_plan(pairs)


def _gather_d2d(stacks, send_sems, recv_sems):
    x, y, c, _, chips = _place()
    sibling = (x, y, 1 - c)
    pairs = []
    for i, stack in enumerate(stacks):
        rows = stack.shape[1] // 2
        for p, (cx, cy) in enumerate(chips):
            k = 3 * i + p
            got, theirs = stack.at[2 * cx + cy, _half(rows, c)], stack.at[2 * cx + cy, _half(rows, 1 - c)]
            sems = (send_sems.at[k], recv_sems.at[k])
            pairs.append(((got, got, *sems, sibling), (theirs, theirs, *sems, sibling)))
    return _copy_plan(pairs)


def _forward_weights(split):
    n = len(split)

    def body(*refs):
        sends, arrivals = _gather_d2d(refs[n:2 * n], *refs[2 * n:])
        _start(sends)
        _finish(sends, arrivals)

    return pl.pallas_call(
        body, name="forward_weights",
        in_specs=[ANY] * n, out_specs=[ANY] * n,
        out_shape=[jax.ShapeDtypeStruct(a.shape, a.dtype) for a in split],
        input_output_aliases={i: i for i in range(n)},
        scratch_shapes=[pltpu.SemaphoreType.DMA((3 * n,))] * 2,
    )(*split)


def _pair_exchange(grads):
    n = len(grads)

    def body(*refs):
        ins, theirs = refs[:n], refs[n:2 * n]
        send_sems, recv_sems = refs[2 * n:]
        x, y, c, _, _ = _place()
        sibling = (x, y, 1 - c)
        sends = [_remote(ins[k].at[:, _half(grads[k].shape[1] // 2, 1 - c)], theirs[k],
                         send_sems.at[k], recv_sems.at[k], sibling) for k in range(n)]
        for cp in sends:
            cp.start()
        for k in range(n):
            _remote(theirs[k], theirs[k], send_sems.at[k], recv_sems.at[k], sibling).wait_recv()
        for cp in sends:
            cp.wait_send()

    return pl.pallas_call(
        body, name="grad_pair_exchange",
        in_specs=[ANY] * n, out_specs=[ANY] * n,
        out_shape=[jax.ShapeDtypeStruct((g.shape[0], g.shape[1] // 2, g.shape[2]), g.dtype) for g in grads],
        scratch_shapes=[pltpu.SemaphoreType.DMA((n,))] * 2,
    )(*grads)


def _chip_exchange(sums, slots):
    n = len(sums)

    def body(*refs):
        sends, arrivals = _chip_copies(refs[:n], refs[2 * n:3 * n], *refs[3 * n:])
        _start(sends)
        _finish(sends, arrivals)

    return pl.pallas_call(
        body, name="grad_chip_exchange",
        in_specs=[ANY] * (2 * n), out_specs=[ANY] * n,
        out_shape=[jax.ShapeDtypeStruct(a.shape, a.dtype) for a in slots],
        input_output_aliases={n + k: k for k in range(n)},
        scratch_shapes=[pltpu.SemaphoreType.DMA((3 * n,)), pltpu.SemaphoreType.DMA((3 * n,))],
    )(*sums, *slots)


def _chip_copies(sums, slots, send_sems, recv_sems):
    x, y, c, me, chips = _place()
    pairs = []
    for k in range(len(sums)):
        for p, (cx, cy) in enumerate(chips):
            j = 3 * k + p
            got = slots[k].at[2 * cx + cy]
            sems, to = (send_sems.at[j], recv_sems.at[j]), (cx, cy, c)
            pairs.append(((sums[k].at[2 * cx + cy], slots[k].at[me], *sems, to), (got, got, *sems, to)))
    return _copy_plan(pairs)


def _half_swap(halves):
    n = len(halves)

    def body(*refs):
        outs = refs[n:2 * n]
        send_sems, recv_sems = refs[2 * n:]
        x, y, c, _, _ = _place()
        sibling = (x, y, 1 - c)
        sends = [_remote(outs[k].at[c], outs[k].at[c], send_sems.at[k], recv_sems.at[k], sibling) for k in range(n)]
        for cp in sends:
            cp.start()
        for k in range(n):
            got = outs[k].at[1 - c]
            _remote(got, got, send_sems.at[k], recv_sems.at[k], sibling).wait_recv()
        for cp in sends:
            cp.wait_send()

    return pl.pallas_call(
        body, name="grad_half_swap",
        in_specs=[ANY] * n, out_specs=[ANY] * n,
        out_shape=[jax.ShapeDtypeStruct(a.shape, a.dtype) for a in halves],
        input_output_aliases={k: k for k in range(n)},
        scratch_shapes=[pltpu.SemaphoreType.DMA((n,))] * 2,
    )(*halves)


def _gather_small(packed):
    n_dev = 8

    def body(in_ref, out_ref, send_sems, recv_sems, loc_sem):
        x, y, c, _, _ = _place()
        me = 4 * x + 2 * y + c
        local = pltpu.make_async_copy(in_ref, out_ref.at[me], loc_sem)
        local.start()
        peers = []
        for k in range(1, n_dev):
            fx, fy, fc = (k >> 2) & 1, (k >> 1) & 1, k & 1
            peers.append((x ^ fx, y ^ fy, c ^ fc))
        sends = [_remote(in_ref, out_ref.at[me], send_sems.at[k], recv_sems.at[k], peer)
                 for k, peer in enumerate(peers)]
        for cp in sends:
            cp.start()
        for k, (px, py, pc) in enumerate(peers):
            got = out_ref.at[4 * px + 2 * py + pc]
            _remote(got, got, send_sems.at[k], recv_sems.at[k], (px, py, pc)).wait_recv()
        for cp in sends:
            cp.wait_send()
        local.wait()

    return pl.pallas_call(
        body, name="gather_small_grads",
        in_specs=[ANY], out_specs=ANY,
        out_shape=jax.ShapeDtypeStruct((n_dev,) + packed.shape, packed.dtype),
        scratch_shapes=[pltpu.SemaphoreType.DMA((n_dev - 1,)), pltpu.SemaphoreType.DMA((n_dev - 1,)),
                        pltpu.SemaphoreType.DMA],
    )(packed)


def _row_tile(r):
    return r // 4 if r >= 256 and (r // 4) % 16 == 0 else r


def _prefetch_call(body, name, grid, in_specs, out_specs, out_shape):
    spec = pltpu.PrefetchScalarGridSpec(num_scalar_prefetch=1, grid=grid, in_specs=in_specs, out_specs=out_specs)
    return pl.pallas_call(body, name=name, grid_spec=spec, out_shape=out_shape,
                          compiler_params=_params(("arbitrary",) * len(grid)))


def _place_shard(w2d, where, dtype, name):
    r, c = w2d.shape
    tr = _row_tile(r)

    def body(where_ref, w_ref, out_ref):
        out_ref[...] = w_ref[...].astype(dtype)

    return _prefetch_call(
        body, name, (r // tr,), [pl.BlockSpec((tr, c), lambda i, s: (i, 0))],
        pl.BlockSpec((None, tr, c), lambda i, s: (s[1], i, 0)),
        jax.ShapeDtypeStruct((N_CHIPS, r, c), dtype))(where, w2d)


def _pair_sum(full, theirs, where, name):
    nb, hs, c = theirs.shape

    def body(where_ref, a_ref, b_ref, out_ref, own_ref):
        total = (a_ref[...].astype(F32) + b_ref[...].astype(F32)).astype(BF16)
        out_ref[...] = total

        @pl.when(pl.program_id(0) == where_ref[1])
        def _():
            own_ref[...] = total

    blk = pl.BlockSpec((None, hs, c), lambda j, s: (j, 0, 0))
    shape = jax.ShapeDtypeStruct(theirs.shape, BF16)
    return _prefetch_call(
        body, name, (nb,), [pl.BlockSpec((None, hs, c), lambda j, s: (j, s[0], 0)), blk],
        [blk, pl.BlockSpec((None, hs, c), lambda j, s: (s[1], 0, 0))], [shape, shape])(where, full, theirs)


def _chip_sum(slots, where, name):
    nb, hs, c = slots.shape
    tr = _row_tile(hs)

    def body(where_ref, a_ref, out_ref):
        total = a_ref[0].astype(F32)
        for j in range(1, nb):
            total = total + a_ref[j].astype(F32)
        out_ref[...] = total

    return _prefetch_call(
        body, name, (hs // tr,), [pl.BlockSpec((nb, tr, c), lambda i, s: (0, i, 0))],
        pl.BlockSpec((None, tr, c), lambda i, s: (s[0], i, 0)),
        jax.ShapeDtypeStruct((2, hs, c), F32))(where, slots)


def _slot_sum(a, name):
    nb, r, c = a.shape
    tr = _row_tile(r)

    def body(a_ref, out_ref):
        total = a_ref[0].astype(F32)
        for j in range(1, nb):
            total = total + a_ref[j].astype(F32)
        out_ref[...] = total

    return pl.pallas_call(
        body, name=name, grid=(r // tr,),
        in_specs=[pl.BlockSpec((nb, tr, c), lambda i: (0, i, 0))],
        out_specs=pl.BlockSpec((tr, c), lambda i: (i, 0)),
        out_shape=jax.ShapeDtypeStruct((r, c), F32), compiler_params=_params(("arbitrary",)),
    )(a)


def _adamw(w, g, m, v, name):
    r, c = w.shape
    tr = _row_tile(r)
    c1 = 1.0 - ADAM_B1 ** ADAM_STEP
    c2 = 1.0 - ADAM_B2 ** ADAM_STEP

    def body(w_ref, g_ref, m_ref, v_ref, d_ref, m2_ref, v2_ref):
        gv = g_ref[...]
        m2 = ADAM_B1 * m_ref[...] + (1.0 - ADAM_B1) * gv
        v2 = ADAM_B2 * v_ref[...] + (1.0 - ADAM_B2) * (gv * gv)
        m2_ref[...] = m2
        v2_ref[...] = v2
        d_ref[...] = -ADAM_LR * ((m2 / c1) / (jnp.sqrt(v2 / c2) + ADAM_EPS) + ADAM_WD * w_ref[...])

    blk = pl.BlockSpec((tr, c), lambda i: (i, 0))
    return pl.pallas_call(
        body, name=name, grid=(r // tr,), in_specs=[blk] * 4, out_specs=[blk] * 3,
        out_shape=[jax.ShapeDtypeStruct((r, c), F32)] * 3, compiler_params=_params(("arbitrary",)),
    )(w, g, m, v)


WEIGHTS = ["ffn1_norm", "ffn1_w_gate", "ffn1_w_up", "ffn1_w_down", "mix_norm", "w_in", "conv_w", "conv_b",
           "rg_w_a", "rg_b_a", "rg_w_x", "rg_b_x", "rg_lambda", "q_norm", "k_norm", "rnn_out_norm",
           "attn_out_norm", "w_out", "ffn2_norm", "ffn2_w_gate", "ffn2_w_up", "ffn2_w_down"]
BIG = ["ffn1_w_gate", "ffn1_w_up", "ffn1_w_down", "w_in", "w_out", "ffn2_w_gate", "ffn2_w_up", "ffn2_w_down"]
SMALL = [n for n in WEIGHTS if n not in BIG]
PACK_LANES = 128
PACK_ROW_ALIGN = 8


def _pack(parts):
    flat = jnp.concatenate([p.reshape(-1) for p in parts])
    unit = PACK_LANES * PACK_ROW_ALIGN
    padded = -(-flat.shape[0] // unit) * unit
    return jnp.pad(flat, (0, padded - flat.shape[0])).reshape(-1, PACK_LANES)


def _unpack(packed, shapes):
    flat = packed.reshape(-1)
    out, at = [], 0
    for shp in shapes:
        size = math.prod(shp)
        out.append(flat[at:at + size].reshape(shp))
        at += size
    return out


def kernel(x, ffn1_norm, ffn1_w_gate, ffn1_w_up, ffn1_w_down, mix_norm, w_in, conv_w, conv_b, rg_w_a, rg_b_a, rg_w_x, rg_b_x, rg_lambda, q_norm, k_norm, rnn_out_norm, attn_out_norm, w_out, ffn2_norm, ffn2_w_gate, ffn2_w_up, ffn2_w_down, loss_target, m_ffn1_norm, m_ffn1_w_gate, m_ffn1_w_up, m_ffn1_w_down, m_mix_norm, m_w_in, m_conv_w, m_conv_b, m_rg_w_a, m_rg_b_a, m_rg_w_x, m_rg_b_x, m_rg_lambda, m_q_norm, m_k_norm, m_rnn_out_norm, m_attn_out_norm, m_w_out, m_ffn2_norm, m_ffn2_w_gate, m_ffn2_w_up, m_ffn2_w_down, v_ffn1_norm, v_ffn1_w_gate, v_ffn1_w_up, v_ffn1_w_down, v_mix_norm, v_w_in, v_conv_w, v_conv_b, v_rg_w_a, v_rg_b_a, v_rg_w_x, v_rg_b_x, v_rg_lambda, v_q_norm, v_k_norm, v_rnn_out_norm, v_attn_out_norm, v_w_out, v_ffn2_norm, v_ffn2_w_gate, v_ffn2_w_up, v_ffn2_w_down):
    given = dict(locals())
    w = {n: given[n] for n in WEIGHTS}
    m = {n: given["m_" + n] for n in WEIGHTS}
    v = {n: given["v_" + n] for n in WEIGHTS}
    chip = 2 * lax.axis_index("x") + lax.axis_index("y")

    where = jnp.stack([lax.axis_index("c"), chip]).astype(jnp.int32)

    stacks = {n: _place_shard(w[n][0], where, BF16, "place_" + n) for n in BIG}
    conv_stack = _place_shard(w["conv_w"][0], where, F32, "place_conv_w")
    small = {n: (w[n][0] if w[n].ndim > 2 else w[n]) for n in SMALL if n != "conv_w"}

    loss, grad_x, slots, gs = _local_step(x[0], loss_target[0], stacks, conv_stack, small, where)
    loss = lax.psum(loss, ("x", "y", "c"))

    swapped = _half_swap([_chip_sum(slots[n], where, "chip_sum_" + n) for n in BIG])
    grads, deltas, new_m, new_v = {}, {}, {}, {}
    for n, t in zip(BIG, swapped):
        shp = w[n].shape
        g2 = t.reshape(shp[1], shp[2])
        d2, m2, v2 = _adamw(w[n][0], g2, m[n][0], v[n][0], "adamw_" + n)
        grads[n], deltas[n], new_m[n], new_v[n] = g2.reshape(shp), d2.reshape(shp), m2.reshape(shp), v2.reshape(shp)

    full_shapes = [gs[n].shape for n in SMALL]
    everyone = _gather_small(_pack([gs[n] for n in SMALL]))
    g_small = _slot_sum(everyone, "small_grad_sum")
    g_parts = dict(zip(SMALL, _unpack(g_small, full_shapes)))
    quarter = D_RNN // N_CHIPS
    g_parts["conv_w"] = lax.dynamic_slice_in_dim(g_parts["conv_w"], chip * quarter, quarter, axis=1)
    local_shapes = [w[n].shape for n in SMALL]
    pk = lambda tree: _pack([tree[n] for n in SMALL])
    d_s, m_s, v_s = _adamw(pk(w), pk(g_parts), pk(m), pk(v), "adamw_small")
    for tree, packed in ((grads, pk(g_parts)), (deltas, d_s), (new_m, m_s), (new_v, v_s)):
        tree.update(zip(SMALL, _unpack(packed, local_shapes)))

    return (loss, grad_x.reshape(x.shape), *[grads[n] for n in WEIGHTS], *[deltas[n] for n in WEIGHTS],
            *[new_m[n] for n in WEIGHTS], *[new_v[n] for n in WEIGHTS])
```

```python
import functools
import math

import jax
import jax.numpy as jnp
from jax import lax
from jax.experimental import pallas as pl
from jax.experimental.pallas import tpu as pltpu

F32 = jnp.float32
BF16 = jnp.bfloat16
MESH = pl.DeviceIdType.MESH

D_MODEL = 1024
N_CHIPS = 4
D_RNN = 512
D_ATT = 512
N_HEADS = 8
HEAD_DIM = 64
RNN_BLOCKS = 8
CONV_W = 4
RG_C = 8.0
N_IN = 2 * D_RNN + 3 * D_ATT
EPS = 1e-6
ATT_BLOCK = 128
ATT_WINDOW = 384
ATT_SPLIT = 256
EXP_ZERO = -105.0

ADAM_LR = 0.001
ADAM_B1 = 0.9
ADAM_B2 = 0.999
ADAM_EPS = 1e-08
ADAM_WD = 0.01
ADAM_STEP = 10

V7X_VMEM_LIMIT = 56 * 1024 * 1024
TOKEN_TILE = 512
FFN_TILE = 256
WGRAD_TILE = 2048

GELU_K0 = math.sqrt(2.0 / math.pi)
GELU_K1 = 0.044715


def _params(sem=None):
    return pltpu.CompilerParams(dimension_semantics=sem, vmem_limit_bytes=V7X_VMEM_LIMIT)


def _dot(a, b):
    return jnp.dot(a, b, preferred_element_type=F32)


def _dot_nt(a, b):
    return lax.dot_general(a, b, (((1,), (1,)), ((), ())), preferred_element_type=F32)


def _dot_tn(a, b):
    return lax.dot_general(a, b, (((0,), (0,)), ((), ())), preferred_element_type=F32)


def _sigmoid(x):
    return 1.0 / (1.0 + jnp.exp(-x))


def _rms_r(xv):
    return lax.rsqrt(jnp.mean(xv * xv, axis=-1, keepdims=True) + EPS)


def _rms_bwd(xv, r, nw, dh):
    t = dh * nw
    dx = r * t - xv * (r * r * r * jnp.mean(t * xv, axis=-1, keepdims=True))
    dn = jnp.sum(dh * xv * r, axis=0, keepdims=True)
    return dx, dn


def _gelu(x):
    t = jnp.tanh(GELU_K0 * (x + GELU_K1 * x * x * x))
    return 0.5 * x * (1.0 + t)


def _gelu_grad(x):
    t = jnp.tanh(GELU_K0 * (x + GELU_K1 * x * x * x))
    return 0.5 * (1.0 + t) + 0.5 * x * (1.0 - t * t) * (GELU_K0 * (1.0 + 3.0 * GELU_K1 * x * x))


def _expm1_neg(x):
    p = 1.0 + x * (1.0 / 8.0)
    for k in (7.0, 6.0, 5.0, 4.0, 3.0, 2.0):
        p = 1.0 + x * (1.0 / k) * p
    return jnp.where(x > -0.25, x * p, jnp.exp(x) - 1.0)


def _log_sigmoid(x):
    return jnp.minimum(x, 0.0) - jnp.log(1.0 + jnp.exp(-jnp.abs(x)))


def _tile(s):
    return min(TOKEN_TILE, s)


def _ffn_fwd(x, nw, wg, wu, wd, tgt=None, gather=None):
    s, d = x.shape
    nb, fb, _ = wg.shape
    tm = min(FFN_TILE, s)
    ni = s // tm
    assert s % tm == 0
    with_loss = tgt is not None
    carried = list(gather[0]) + list(gather[1]) if gather else []
    n_split = len(gather[0]) if gather else 0
    nc = len(carried)

    def body(*refs):
        x_ref, nw_ref, wg_ref, wu_ref, wd_ref = refs[:5]
        at = 5
        if with_loss:
            tgt_ref = refs[at]
            at += 1
        at += nc
        out_ref, g_ref, u_ref, hb_ref, ab_ref = refs[at:at + 5]
        at += 5
        if with_loss:
            loss_ref = refs[at]
            at += 1
        stacks = refs[at:at + nc]
        sems = refs[at + nc:]
        i = pl.program_id(0)

        if nc:
            @pl.when(i == 0)
            def _():
                _start(_gather_ici(stacks, n_split, *sems)[0])

        xv = x_ref[...]
        hb = (xv * _rms_r(xv) * nw_ref[...]).astype(BF16)
        hb_ref[...] = hb
        y = jnp.zeros((tm, d), F32)
        for jb in range(nb):
            g = _dot_nt(hb, wg_ref[jb])
            u = _dot_nt(hb, wu_ref[jb])
            g_ref[jb] = g.astype(BF16)
            u_ref[jb] = u.astype(BF16)
            ab = (g * _sigmoid(g) * u).astype(BF16)
            ab_ref[jb] = ab
            y = y + _dot(ab, wd_ref[jb])
        y = xv + 0.5 * y
        if with_loss:
            diff = y - tgt_ref[...]
            out_ref[...] = diff * (1.0 / d)

            @pl.when(i == 0)
            def _():
                loss_ref[...] = jnp.zeros_like(loss_ref)

            loss_ref[...] += jnp.sum(diff * diff) * (0.5 / d)
        else:
            out_ref[...] = y

        if nc:
            @pl.when(i == ni - 1)
            def _():
                _finish(*_gather_ici(stacks, n_split, *sems))

    row = pl.BlockSpec((tm, d), lambda i: (i, 0))
    weight = pl.BlockSpec((nb, fb, d), lambda i: (0, 0, 0), pipeline_mode=pl.Buffered(1))
    in_specs = [row, pl.BlockSpec((1, d), lambda i: (0, 0)), weight, weight, weight]
    args = [x, nw, wg, wu, wd]
    if with_loss:
        in_specs.append(row)
        args.append(tgt)
    blk = pl.BlockSpec((nb, tm, fb), lambda i: (0, i, 0))
    out_shape = [jax.ShapeDtypeStruct((s, d), F32), jax.ShapeDtypeStruct((nb, s, fb), BF16),
                 jax.ShapeDtypeStruct((nb, s, fb), BF16), jax.ShapeDtypeStruct((s, d), BF16),
                 jax.ShapeDtypeStruct((nb, s, fb), BF16)]
    out_specs = [row, blk, blk, row, blk]
    if with_loss:
        out_shape.append(jax.ShapeDtypeStruct((1, 128), F32))
        out_specs.append(pl.BlockSpec((1, 128), lambda i: (0, 0)))
    aliases = {len(args) + k: len(out_shape) + k for k in range(nc)}
    out_shape += [jax.ShapeDtypeStruct(a.shape, a.dtype) for a in carried]
    return pl.pallas_call(
        body, name="ffn_fwd_loss" if with_loss else "ffn_fwd",
        grid=(ni,), in_specs=in_specs + [ANY] * nc, out_specs=out_specs + [ANY] * nc, out_shape=out_shape,
        input_output_aliases=aliases,
        scratch_shapes=[pltpu.SemaphoreType.DMA((3 * nc,))] * 2 if nc else [],
        compiler_params=_params(("arbitrary",)),
    )(*args, *carried)


def _ffn_bwd_act(x, nw, dy, g, u, wg, wu, wd, name):
    s, d = x.shape
    nb, fb, _ = wg.shape
    tm = min(FFN_TILE, s)
    assert s % tm == 0

    def body(x_ref, nw_ref, dy_ref, g_ref, u_ref, wg_ref, wu_ref, wd_ref,
             dx_ref, dg_ref, du_ref, dyb_ref, dnw_ref):
        dyv = dy_ref[...]
        dyb = dyv.astype(BF16)
        dyb_ref[...] = dyb
        dh = jnp.zeros((tm, d), F32)
        for jb in range(nb):
            da = 0.5 * _dot_nt(dyb, wd_ref[jb])
            gv = g_ref[jb].astype(F32)
            sg = _sigmoid(gv)
            dub = (da * (gv * sg)).astype(BF16)
            dgb = (da * u_ref[jb].astype(F32) * (sg * (1.0 + gv * (1.0 - sg)))).astype(BF16)
            dg_ref[jb] = dgb
            du_ref[jb] = dub
            dh = dh + _dot(dgb, wg_ref[jb]) + _dot(dub, wu_ref[jb])
        xv = x_ref[...]
        dx, dn = _rms_bwd(xv, _rms_r(xv), nw_ref[...], dh)
        dx_ref[...] = dyv + dx

        @pl.when(pl.program_id(0) == 0)
        def _():
            dnw_ref[...] = jnp.zeros_like(dnw_ref)

        dnw_ref[...] += dn

    row = pl.BlockSpec((tm, d), lambda i: (i, 0))
    vec = pl.BlockSpec((1, d), lambda i: (0, 0))
    blk = pl.BlockSpec((nb, tm, fb), lambda i: (0, i, 0))
    weight = pl.BlockSpec((nb, fb, d), lambda i: (0, 0, 0), pipeline_mode=pl.Buffered(1))
    return pl.pallas_call(
        body, name=name, grid=(s // tm,),
        in_specs=[row, vec, row, blk, blk, weight, weight, weight],
        out_specs=[row, blk, blk, row, vec],
        out_shape=[jax.ShapeDtypeStruct((s, d), F32), jax.ShapeDtypeStruct((nb, s, fb), BF16),
                   jax.ShapeDtypeStruct((nb, s, fb), BF16), jax.ShapeDtypeStruct((s, d), BF16),
                   jax.ShapeDtypeStruct((1, d), F32)],
        compiler_params=_params(("arbitrary",)),
    )(x, nw, dy, g, u, wg, wu, wd)


def _wgrad(a, b, a_spec, b_spec, out_rows, out_cols, scale, name, tk=None):
    s = a.shape[-2]
    tk = tk or _tile(s)
    nk = s // tk
    assert s % tk == 0

    def body(a_ref, b_ref, out_ref, acc):
        k = pl.program_id(1)

        @pl.when(k == 0)
        def _():
            acc[...] = jnp.zeros_like(acc)

        acc[...] += _dot_tn(a_ref[...], b_ref[...])

        @pl.when(k == nk - 1)
        def _():
            out_ref[...] = (acc[...] * scale).astype(BF16)

    return pl.pallas_call(
        body, name=name, grid=(N_CHIPS, nk),
        in_specs=[a_spec(tk), b_spec(tk)],
        out_specs=pl.BlockSpec((None, out_rows, out_cols), lambda j, k: (j, 0, 0)),
        out_shape=jax.ShapeDtypeStruct((N_CHIPS, out_rows, out_cols), BF16),
        scratch_shapes=[pltpu.VMEM((out_rows, out_cols), F32)],
        compiler_params=_params(("arbitrary", "arbitrary")),
    )(a, b)


def _ffn_wgrads(hb, ab, dg, du, dyb, tag):
    s, d = hb.shape
    fb = ab.shape[-1]
    tk = min(WGRAD_TILE, s)
    shared = lambda tk: pl.BlockSpec((tk, d), lambda j, k: (k, 0))
    stacked = lambda tk: pl.BlockSpec((None, tk, fb), lambda j, k: (j, k, 0))
    dwg = _wgrad(dg, hb, stacked, shared, fb, d, 1.0, "wgrad_gate_" + tag, tk)
    dwu = _wgrad(du, hb, stacked, shared, fb, d, 1.0, "wgrad_up_" + tag, tk)
    dwd = _wgrad(ab, dyb, stacked, shared, fb, d, 0.5, "wgrad_down_" + tag, tk)
    return dwg, dwu, dwd


def _mix_pre(x, nw, win):
    s, d = x.shape
    nb, _, cb = win.shape
    tm = _tile(s)

    def body(x_ref, nw_ref, w_ref, p_ref, hb_ref, hs):
        @pl.when(pl.program_id(1) == 0)
        def _():
            xv = x_ref[...]
            hb = (xv * _rms_r(xv) * nw_ref[...]).astype(BF16)
            hs[...] = hb
            hb_ref[...] = hb

        p_ref[...] = _dot(hs[...], w_ref[...])

    row = pl.BlockSpec((tm, d), lambda i, j: (i, 0))
    return pl.pallas_call(
        body, name="mix_pre", grid=(s // tm, nb),
        in_specs=[row, pl.BlockSpec((1, d), lambda i, j: (0, 0)),
                  pl.BlockSpec((None, d, cb), lambda i, j: (j, 0, 0))],
        out_specs=[pl.BlockSpec((tm, cb), lambda i, j: (i, j)), row],
        out_shape=[jax.ShapeDtypeStruct((s, nb * cb), F32), jax.ShapeDtypeStruct((s, d), BF16)],
        scratch_shapes=[pltpu.VMEM((tm, d), BF16)],
        compiler_params=_params(("arbitrary", "arbitrary")),
    )(x, nw, win)


def _mix_pre_bwd(x, nw, dres, dpb, win):
    s, d = x.shape
    nb, _, cb = win.shape
    tm = _tile(s)

    def body(x_ref, nw_ref, dres_ref, dp_ref, w_ref, dx_ref, dnw_ref, acc):
        i = pl.program_id(0)
        j = pl.program_id(1)

        @pl.when(j == 0)
        def _():
            acc[...] = jnp.zeros_like(acc)

        acc[...] += _dot_nt(dp_ref[...], w_ref[...])

        @pl.when(j == nb - 1)
        def _():
            xv = x_ref[...]
            dx, dn = _rms_bwd(xv, _rms_r(xv), nw_ref[...], acc[...])
            dx_ref[...] = dres_ref[...] + dx

            @pl.when(i == 0)
            def _():
                dnw_ref[...] = jnp.zeros_like(dnw_ref)

            dnw_ref[...] += dn

    row = pl.BlockSpec((tm, d), lambda i, j: (i, 0))
    vec = pl.BlockSpec((1, d), lambda i, j: (0, 0))
    return pl.pallas_call(
        body, name="mix_pre_bwd", grid=(s // tm, nb),
        in_specs=[row, vec, row, pl.BlockSpec((tm, cb), lambda i, j: (i, j)),
                  pl.BlockSpec((None, d, cb), lambda i, j: (j, 0, 0))],
        out_specs=[row, vec],
        out_shape=[jax.ShapeDtypeStruct((s, d), F32), jax.ShapeDtypeStruct((1, d), F32)],
        scratch_shapes=[pltpu.VMEM((tm, d), F32)],
        compiler_params=_params(("arbitrary", "arbitrary")),
    )(x, nw, dres, dpb, win)


def _mix_post(x, yr, ya, nr, na, wout):
    s, d = x.shape
    h = yr.shape[1]
    tm = _tile(s)

    def body(x_ref, yr_ref, ya_ref, nr_ref, na_ref, w_ref, out_ref):
        yrv = yr_ref[...]
        yav = ya_ref[...]
        onb = (yrv * _rms_r(yrv) * nr_ref[...]).astype(BF16)
        oab = (yav * _rms_r(yav) * na_ref[...]).astype(BF16)
        out_ref[...] = x_ref[...] + _dot(onb, w_ref[0:h, :]) + _dot(oab, w_ref[h:2 * h, :])

    row = pl.BlockSpec((tm, d), lambda i: (i, 0))
    half = pl.BlockSpec((tm, h), lambda i: (i, 0))
    vec = pl.BlockSpec((1, h), lambda i: (0, 0))
    return pl.pallas_call(
        body, name="mix_post", grid=(s // tm,),
        in_specs=[row, half, half, vec, vec, pl.BlockSpec((2 * h, d), lambda i: (0, 0))],
        out_specs=row, out_shape=jax.ShapeDtypeStruct((s, d), F32),
        compiler_params=_params(("arbitrary",)),
    )(x, yr, ya, nr, na, wout)


def _mix_post_bwd(dx, yr, ya, nr, na, wout):
    s, d = dx.shape
    h = yr.shape[1]
    tm = _tile(s)

    def body(dx_ref, yr_ref, ya_ref, nr_ref, na_ref, w_ref,
             dyr_ref, dya_ref, yc_ref, dxb_ref, dnr_ref, dna_ref):
        i = pl.program_id(0)
        dxb = dx_ref[...].astype(BF16)
        dxb_ref[...] = dxb
        dyc = _dot_nt(dxb, w_ref[...])
        yrv = yr_ref[...]
        yav = ya_ref[...]
        rr = _rms_r(yrv)
        ra = _rms_r(yav)
        yc_ref[:, 0:h] = (yrv * rr * nr_ref[...]).astype(BF16)
        yc_ref[:, h:2 * h] = (yav * ra * na_ref[...]).astype(BF16)
        dyr, dnr = _rms_bwd(yrv, rr, nr_ref[...], dyc[:, 0:h])
        dya, dna = _rms_bwd(yav, ra, na_ref[...], dyc[:, h:2 * h])
        dyr_ref[...] = dyr
        dya_ref[...] = dya

        @pl.when(i == 0)
        def _():
            dnr_ref[...] = jnp.zeros_like(dnr_ref)
            dna_ref[...] = jnp.zeros_like(dna_ref)

        dnr_ref[...] += dnr
        dna_ref[...] += dna

    row = pl.BlockSpec((tm, d), lambda i: (i, 0))
    half = pl.BlockSpec((tm, h), lambda i: (i, 0))
    vec = pl.BlockSpec((1, h), lambda i: (0, 0))
    return pl.pallas_call(
        body, name="mix_post_bwd", grid=(s // tm,),
        in_specs=[row, half, half, vec, vec, pl.BlockSpec((2 * h, d), lambda i: (0, 0))],
        out_specs=[half, half, pl.BlockSpec((tm, 2 * h), lambda i: (i, 0)), row, vec, vec],
        out_shape=[jax.ShapeDtypeStruct((s, h), F32), jax.ShapeDtypeStruct((s, h), F32),
                   jax.ShapeDtypeStruct((s, 2 * h), BF16), jax.ShapeDtypeStruct((s, d), BF16),
                   jax.ShapeDtypeStruct((1, h), F32), jax.ShapeDtypeStruct((1, h), F32)],
        compiler_params=_params(("arbitrary",)),
    )(dx, yr, ya, nr, na, wout)


def _shift_down(xv, s, prev8):
    rolled = pltpu.roll(xv, s, 0)
    row8 = lax.broadcasted_iota(jnp.int32, prev8.shape, 0)
    head = jnp.where(row8 < s, pltpu.roll(prev8, s, 0), rolled[0:8, :])
    return jnp.concatenate([head, rolled[8:, :]], axis=0)


def _shift_up(xv, s, next8):
    n = xv.shape[0]
    rolled = pltpu.roll(xv, n - s, 0)
    row8 = lax.broadcasted_iota(jnp.int32, next8.shape, 0)
    tail = jnp.where(row8 >= 8 - s, pltpu.roll(next8, 8 - s, 0), rolled[n - 8:, :])
    return jnp.concatenate([rolled[:n - 8, :], tail], axis=0)


def _scan_fwd(a, b):
    n = a.shape[0]
    row = lax.broadcasted_iota(jnp.int32, a.shape, 0)
    s = 1
    while s < n:
        ok = row >= s
        b = jnp.where(ok, a * pltpu.roll(b, s, 0) + b, b)
        a = jnp.where(ok, a * pltpu.roll(a, s, 0), a)
        s *= 2
    return b


def _scan_bwd(a, b):
    n = a.shape[0]
    row = lax.broadcasted_iota(jnp.int32, a.shape, 0)
    s = 1
    while s < n:
        ok = row < n - s
        b = jnp.where(ok, a * pltpu.roll(b, n - s, 0) + b, b)
        a = jnp.where(ok, a * pltpu.roll(a, n - s, 0), a)
        s *= 2
    return b


def _rglru_gates(xv, prev8, cw_ref, cb_ref, wa_ref, ba_ref, wx_ref, bx_ref, lam_ref):
    x1 = _shift_down(xv, 1, prev8)
    x2 = _shift_down(xv, 2, prev8)
    x3 = _shift_down(xv, 3, prev8)
    xc = cw_ref[3:4, :] * xv + cw_ref[2:3, :] * x1 + cw_ref[1:2, :] * x2 + cw_ref[0:1, :] * x3 + cb_ref[...]
    xcb = xc.astype(BF16)
    r = _sigmoid(_dot(xcb, wa_ref[...]) + ba_ref[...])
    ig = _sigmoid(_dot(xcb, wx_ref[...]) + bx_ref[...])
    c = RG_C * _log_sigmoid(lam_ref[...])
    la = r * c
    a = jnp.exp(la)
    m = jnp.sqrt(-_expm1_neg(2.0 * la))
    return (x1, x2, x3), xc, xcb, r, ig, c, a, m


def _rglru_fwd(proj, cw, cb, wa, ba, wx, bx, lam):
    s = proj.shape[0]
    w = D_RNN
    tm = _tile(s)

    def body(xr_ref, gate_ref, cw_ref, cb_ref, wa_ref, ba_ref, wx_ref, bx_ref, lam_ref,
             y_ref, h_ref, prev, hlast):
        @pl.when(pl.program_id(0) == 0)
        def _():
            prev[...] = jnp.zeros_like(prev)
            hlast[...] = jnp.zeros_like(hlast)

        xv = xr_ref[...]
        _, xc, _, _, ig, _, a, m = _rglru_gates(xv, prev[...], cw_ref, cb_ref, wa_ref, ba_ref,
                                                wx_ref, bx_ref, lam_ref)
        b = m * (ig * xc)
        row = lax.broadcasted_iota(jnp.int32, b.shape, 0)
        b = jnp.where(row == 0, b + a * hlast[...], b)
        h = _scan_fwd(a, b)
        h_ref[...] = h
        y_ref[...] = h * _gelu(gate_ref[...])
        prev[...] = xv[tm - 8:, :]
        hlast[...] = h[tm - 1:tm, :]

    vec = pl.BlockSpec((1, w), lambda i: (0, 0))
    sq = pl.BlockSpec((w, w), lambda i: (0, 0))
    out = pl.BlockSpec((tm, w), lambda i: (i, 0))
    return pl.pallas_call(
        body, name="rglru_fwd", grid=(s // tm,),
        in_specs=[pl.BlockSpec((tm, w), lambda i: (i, 0)), pl.BlockSpec((tm, w), lambda i: (i, 1)),
                  pl.BlockSpec((CONV_W, w), lambda i: (0, 0)), vec, sq, vec, sq, vec, vec],
        out_specs=[out, out],
        out_shape=[jax.ShapeDtypeStruct((s, w), F32), jax.ShapeDtypeStruct((s, w), F32)],
        scratch_shapes=[pltpu.VMEM((8, w), F32), pltpu.VMEM((1, w), F32)],
        compiler_params=_params(("arbitrary",)),
    )(proj, proj, cw, cb, wa, ba, wx, bx, lam)


def _rglru_bwd(proj, hseq, dyr, cw, cb, wa, ba, wx, bx, lam):
    s = proj.shape[0]
    w = D_RNN
    tm = _tile(s)
    nt = s // tm
    t8 = tm // 8

    def body(xr_ref, xp_ref, gate_ref, h_ref, hp_ref, dy_ref, cw_ref, cb_ref, wa_ref, ba_ref,
             wx_ref, bx_ref, lam_ref,
             dxr_ref, dgate_ref, dcw_ref, dcb_ref, dwa_ref, dba_ref, dwx_ref, dbx_ref, dlam_ref,
             carry, dxc_next):
        i = pl.program_id(0)
        first_tile = i == nt - 1

        @pl.when(i == 0)
        def _():
            carry[...] = jnp.zeros_like(carry)
            dxc_next[...] = jnp.zeros_like(dxc_next)
            for ref in (dcw_ref, dcb_ref, dwa_ref, dba_ref, dwx_ref, dbx_ref, dlam_ref):
                ref[...] = jnp.zeros_like(ref)

        xv = xr_ref[...]
        prev8 = jnp.where(first_tile, 0.0, xp_ref[...])
        hprev8 = jnp.where(first_tile, 0.0, hp_ref[...])
        (x1, x2, x3), xc, xcb, r, ig, c, a, m = _rglru_gates(
            xv, prev8, cw_ref, cb_ref, wa_ref, ba_ref, wx_ref, bx_ref, lam_ref)
        gv = gate_ref[...]
        hv = h_ref[...]
        dy = dy_ref[...]
        dgate_ref[...] = (dy * hv * _gelu_grad(gv)).astype(BF16)
        dh = dy * _gelu(gv)
        row = lax.broadcasted_iota(jnp.int32, dh.shape, 0)
        dh = jnp.where(row == tm - 1, dh + carry[...], dh)
        a_up = jnp.where(row == tm - 1, 0.0, pltpu.roll(a, tm - 1, 0))
        lam_t = _scan_bwd(a_up, dh)
        carry[...] = a[0:1, :] * lam_t[0:1, :]
        hm1 = _shift_down(hv, 1, hprev8)
        da = lam_t * hm1
        ixc = ig * xc
        dm = lam_t * ixc
        dig = lam_t * m * xc
        dxc = lam_t * m * ig
        dla = da * a - dm * (a * a) / m
        dr = dla * c
        dlam_ref[...] += jnp.sum(dla * r, axis=0, keepdims=True)
        dpa = dr * r * (1.0 - r)
        dpi = dig * ig * (1.0 - ig)
        dba_ref[...] += jnp.sum(dpa, axis=0, keepdims=True)
        dbx_ref[...] += jnp.sum(dpi, axis=0, keepdims=True)
        dpab = dpa.astype(BF16)
        dpib = dpi.astype(BF16)
        dwa_ref[...] += _dot_tn(xcb, dpab)
        dwx_ref[...] += _dot_tn(xcb, dpib)
        dxc = dxc + _dot_nt(dpab, wa_ref[...]) + _dot_nt(dpib, wx_ref[...])
        dcb_ref[...] += jnp.sum(dxc, axis=0, keepdims=True)
        dcw_ref[3:4, :] += jnp.sum(dxc * xv, axis=0, keepdims=True)
        dcw_ref[2:3, :] += jnp.sum(dxc * x1, axis=0, keepdims=True)
        dcw_ref[1:2, :] += jnp.sum(dxc * x2, axis=0, keepdims=True)
        dcw_ref[0:1, :] += jnp.sum(dxc * x3, axis=0, keepdims=True)
        nxt = dxc_next[...]
        dxr = (cw_ref[3:4, :] * dxc + cw_ref[2:3, :] * _shift_up(dxc, 1, nxt)
               + cw_ref[1:2, :] * _shift_up(dxc, 2, nxt) + cw_ref[0:1, :] * _shift_up(dxc, 3, nxt))
        dxr_ref[...] = dxr.astype(BF16)
        dxc_next[...] = dxc[0:8, :]

        @pl.when(first_tile)
        def _():
            lv = lam_ref[...]
            dlam_ref[...] = dlam_ref[...] * (RG_C * _sigmoid(-lv))

    rev = lambda i: nt - 1 - i
    vec = pl.BlockSpec((1, w), lambda i: (0, 0))
    sq = pl.BlockSpec((w, w), lambda i: (0, 0))
    cur = lambda col: pl.BlockSpec((tm, w), lambda i: (rev(i), col))
    before = lambda cols: pl.BlockSpec((8, w), lambda i: (jnp.maximum(rev(i) * t8 - 1, 0), 0))
    return pl.pallas_call(
        body, name="rglru_bwd", grid=(nt,),
        in_specs=[cur(0), before(None), cur(1), cur(0), before(None), cur(0),
                  pl.BlockSpec((CONV_W, w), lambda i: (0, 0)), vec, sq, vec, sq, vec, vec],
        out_specs=[cur(0), cur(0), pl.BlockSpec((CONV_W, w), lambda i: (0, 0)), vec, sq, vec, sq, vec, vec],
        out_shape=[jax.ShapeDtypeStruct((s, w), BF16), jax.ShapeDtypeStruct((s, w), BF16),
                   jax.ShapeDtypeStruct((CONV_W, w), F32), jax.ShapeDtypeStruct((1, w), F32),
                   jax.ShapeDtypeStruct((w, w), F32), jax.ShapeDtypeStruct((1, w), F32),
                   jax.ShapeDtypeStruct((w, w), F32), jax.ShapeDtypeStruct((1, w), F32),
                   jax.ShapeDtypeStruct((1, w), F32)],
        scratch_shapes=[pltpu.VMEM((1, w), F32), pltpu.VMEM((8, w), F32)],
        compiler_params=_params(("arbitrary",)),
    )(proj, proj, proj, hseq, hseq, dyr, cw, cb, wa, ba, wx, bx, lam)


def _sb_logs(z, valid):
    l1p = jnp.log(1.0 + jnp.exp(-jnp.abs(z)))
    lb = jnp.minimum(z, 0.0) - l1p
    lm = jnp.where(valid, -jnp.maximum(z, 0.0) - l1p, 0.0)
    return lb, lm


class _Window:
    def __init__(self):
        blk, win, cut = ATT_BLOCK, ATT_WINDOW, ATT_SPLIT
        self.row = lax.broadcasted_iota(jnp.int32, (blk, win), 0)
        self.col = lax.broadcasted_iota(jnp.int32, (blk, win), 1)

        def tri(n, later):
            j = lax.broadcasted_iota(jnp.int32, (n, n), 0)
            s = lax.broadcasted_iota(jnp.int32, (n, n), 1)
            return jnp.where((j > s) if later else (j < s), 1.0, 0.0).astype(BF16)

        self.later = (tri(cut, True), tri(win - cut, True))
        self.earlier = (tri(cut, False), tri(win - cut, False))

    def place(self, qi, g):
        end = (qi + 1) * ATT_BLOCK - g * ATT_WINDOW
        start = pl.multiple_of(jnp.maximum(end - ATT_WINDOW, 0), ATT_BLOCK)
        valid = start + self.col < jnp.minimum(qi * ATT_BLOCK + self.row, end)
        return start, valid

    @staticmethod
    def _parts(xv):
        hi = xv.astype(BF16)
        lo = (xv - hi.astype(F32)).astype(BF16)
        cut = ATT_SPLIT
        sums = (jnp.sum(xv[:, :cut], axis=1, keepdims=True), jnp.sum(xv[:, cut:], axis=1, keepdims=True))
        return (hi[:, :cut], lo[:, :cut]), (hi[:, cut:], lo[:, cut:]), sums

    def sums_after(self, xv, carry):
        (h0, l0), (h1, l1), (s0, s1) = self._parts(xv)
        first = _dot(h0, self.later[0]) + _dot(l0, self.later[0]) + (s1 + carry)
        last = _dot(h1, self.later[1]) + _dot(l1, self.later[1]) + carry
        return jnp.concatenate([first, last], axis=1), s0 + s1

    def sums_before(self, xv, carry):
        (h0, l0), (h1, l1), (s0, s1) = self._parts(xv)
        first = _dot(h0, self.earlier[0]) + _dot(l0, self.earlier[0]) + carry
        last = _dot(h1, self.earlier[1]) + _dot(l1, self.earlier[1]) + (s0 + carry)
        return jnp.concatenate([first, last], axis=1), s0 + s1


def _head_lanes(hh):
    return slice(hh * HEAD_DIM, (hh + 1) * HEAD_DIM)


def _attn_fwd(proj, qg, kg):
    s = proj.shape[0]
    blk, win, dh = ATT_BLOCK, ATT_WINDOW, HEAD_DIM
    nq = s // blk
    scale = 1.0 / math.sqrt(dh)
    assert s >= win and s % blk == 0

    def body(q_ref, k_ref, v_ref, qg_ref, kg_ref, o_ref, qn, kn, vb, ob):
        wd = _Window()
        for hh in range(2):
            lanes = _head_lanes(hh)
            qv = q_ref[:, lanes]
            qn[...] = (qv * _rms_r(qv) * qg_ref[...] * scale).astype(BF16)
            kv = k_ref[:, lanes]
            kn[...] = (kv * _rms_r(kv) * kg_ref[...]).astype(BF16)
            vb[...] = v_ref[:, lanes].astype(BF16)

            def q_step(qi, _):
                qoff = pl.multiple_of(qi * blk, blk)
                qt = qn[pl.ds(qoff, blk), :]

                def more(carry):
                    g, live, _, _ = carry
                    return jnp.logical_and((qi + 1) * blk - g * win > 0, live > 0)

                def window(carry):
                    g, _, acc, run = carry
                    start, valid = wd.place(qi, g)
                    z = _dot_nt(qt, kn[pl.ds(start, win), :])
                    lb, lm = _sb_logs(z, valid)
                    tail, total = wd.sums_after(lm, run)
                    wgt = jnp.where(valid, jnp.exp(lb + tail), 0.0)
                    acc = acc + _dot(wgt.astype(BF16), vb[pl.ds(start, win), :])
                    run = run + total
                    live = (jnp.max(run) > EXP_ZERO).astype(jnp.int32)
                    return g + 1, live, acc, run

                _, _, acc, _ = lax.while_loop(
                    more, window, (jnp.int32(0), jnp.int32(1), jnp.zeros((blk, dh), F32), jnp.zeros((blk, 1), F32)))
                ob[pl.ds(qoff, blk), :] = acc
                return 0

            lax.fori_loop(0, nq, q_step, 0)
            o_ref[:, lanes] = ob[...]

    pair = lambda group: pl.BlockSpec((s, 2 * dh), lambda h: (0, group * (D_ATT // (2 * dh)) + h))
    vec = pl.BlockSpec((1, dh), lambda h: (0, 0))
    return pl.pallas_call(
        body, name="attn_fwd", grid=(N_HEADS // 2,),
        in_specs=[pair(2), pair(3), pair(4), vec, vec], out_specs=pair(0),
        out_shape=jax.ShapeDtypeStruct((s, D_ATT), F32),
        scratch_shapes=[pltpu.VMEM((s, dh), BF16)] * 3 + [pltpu.VMEM((s, dh), F32)],
        compiler_params=_params(("arbitrary",)),
    )(proj, proj, proj, qg, kg)


def _attn_bwd(proj, dya, qg, kg, sums, slots):
    s = proj.shape[0]
    blk, win, dh = ATT_BLOCK, ATT_WINDOW, HEAD_DIM
    nq = s // blk
    max_windows = -(-s // win) + 1
    scale = 1.0 / math.sqrt(dh)
    steps = N_HEADS // 2
    nx = len(sums)
    assert s >= win and s % blk == 0

    def body(*refs):
        q_ref, k_ref, v_ref, do_ref, qg_ref, kg_ref = refs[:6]
        sum_refs = refs[6:6 + nx]
        at = 6 + 2 * nx
        dq_ref, dk_ref, dv_ref, dqg_ref, dkg_ref = refs[at:at + 5]
        slot_refs = refs[at + 5:at + 5 + nx]
        qn, kn, vb, dob, runs, dqn, dkn, dvn, send_sems, recv_sems = refs[at + 5 + nx:]
        wd = _Window()

        @pl.when(pl.program_id(0) == 0)
        def _():
            dqg_ref[...] = jnp.zeros_like(dqg_ref)
            dkg_ref[...] = jnp.zeros_like(dkg_ref)
            _start(_chip_copies(sum_refs, slot_refs, send_sems, recv_sems)[0])

        for hh in range(2):
            lanes = _head_lanes(hh)
            qv = q_ref[:, lanes]
            qn[...] = (qv * _rms_r(qv) * qg_ref[...] * scale).astype(BF16)
            kv = k_ref[:, lanes]
            kn[...] = (kv * _rms_r(kv) * kg_ref[...]).astype(BF16)
            vb[...] = v_ref[:, lanes].astype(BF16)
            dob[...] = do_ref[:, lanes].astype(BF16)
            dkn[...] = jnp.zeros_like(dkn)
            dvn[...] = jnp.zeros_like(dvn)

            def q_step(qi, _):
                qoff = pl.multiple_of(qi * blk, blk)
                qt = qn[pl.ds(qoff, blk), :]
                dot = dob[pl.ds(qoff, blk), :]

                def more(carry):
                    g, live, _ = carry
                    return jnp.logical_and((qi + 1) * blk - g * win > 0, live > 0)

                def run_window(carry):
                    g, _, run = carry
                    runs[g] = run
                    start, valid = wd.place(qi, g)
                    z = _dot_nt(qt, kn[pl.ds(start, win), :])
                    _, lm = _sb_logs(z, valid)
                    run = run + jnp.sum(lm, axis=1, keepdims=True)
                    live = (jnp.max(run) > EXP_ZERO).astype(jnp.int32)
                    return g + 1, live, run

                windows, _, _ = lax.while_loop(
                    more, run_window, (jnp.int32(0), jnp.int32(1), jnp.zeros((blk, 1), F32)))

                def k_window(gg, carry):
                    dq_acc, esum = carry
                    g = windows - 1 - gg
                    start, valid = wd.place(qi, g)
                    kt = kn[pl.ds(start, win), :]
                    vt = vb[pl.ds(start, win), :]
                    z = _dot_nt(qt, kt)
                    lb, lm = _sb_logs(z, valid)
                    tail, _ = wd.sums_after(lm, runs[g])
                    wgt = jnp.where(valid, jnp.exp(lb + tail), 0.0)
                    e = _dot_nt(dot, vt) * wgt
                    before, etotal = wd.sums_before(e, esum)
                    beta = jnp.exp(lb)
                    dz = jnp.where(valid, e * (1.0 - beta) - before * beta, 0.0)
                    dzb = dz.astype(BF16)
                    dq_acc = dq_acc + _dot(dzb, kt)
                    dkn[pl.ds(start, win), :] += _dot_tn(dzb, qt)
                    dvn[pl.ds(start, win), :] += _dot_tn(wgt.astype(BF16), dot)
                    return dq_acc, esum + etotal

                dq_acc, _ = lax.fori_loop(0, windows, k_window,
                                          (jnp.zeros((blk, dh), F32), jnp.zeros((blk, 1), F32)))
                dqn[pl.ds(qoff, blk), :] = dq_acc
                return 0

            lax.fori_loop(0, nq, q_step, 0)

            dq, dqg = _rms_bwd(qv, _rms_r(qv), qg_ref[...] * scale, dqn[...])
            dq_ref[:, lanes] = dq.astype(BF16)
            dqg_ref[...] += dqg * scale
            dk, dkg = _rms_bwd(kv, _rms_r(kv), kg_ref[...], dkn[...])
            dk_ref[:, lanes] = dk.astype(BF16)
            dkg_ref[...] += dkg
            dv_ref[:, lanes] = dvn[...].astype(BF16)

        @pl.when(pl.program_id(0) == steps - 1)
        def _():
            _finish(*_chip_copies(sum_refs, slot_refs, send_sems, recv_sems))

    pair = lambda group: pl.BlockSpec((s, 2 * dh), lambda h: (0, group * (D_ATT // (2 * dh)) + h))
    vec = pl.BlockSpec((1, dh), lambda h: (0, 0))
    outs = pl.pallas_call(
        body, name="attn_bwd", grid=(steps,),
        in_specs=[pair(2), pair(3), pair(4), pair(0), vec, vec] + [ANY] * (2 * nx),
        out_specs=[pair(0), pair(0), pair(0), vec, vec] + [ANY] * nx,
        out_shape=[jax.ShapeDtypeStruct((s, D_ATT), BF16)] * 3 + [jax.ShapeDtypeStruct((1, dh), F32)] * 2
        + [jax.ShapeDtypeStruct(a.shape, a.dtype) for a in slots],
        input_output_aliases={6 + nx + k: 5 + k for k in range(nx)},
        scratch_shapes=[pltpu.VMEM((s, dh), BF16)] * 4 + [pltpu.VMEM((max_windows, blk, 1), F32)]
        + [pltpu.VMEM((s, dh), F32)] * 3 + [pltpu.SemaphoreType.DMA((3 * nx,))] * 2,
        compiler_params=_params(("arbitrary",)),
    )(proj, proj, proj, dya, qg, kg, *sums, *slots)
    return outs[:5], outs[5:]


def _block_diag(w):
    n, c, d = w.shape
    return jnp.einsum("ncd,nm->ncmd", w, jnp.eye(n, dtype=w.dtype)).reshape(n * c, n * d)


def _diag_blocks(full, n):
    c = full.shape[0] // n
    return jnp.stack([full[i * c:(i + 1) * c, i * c:(i + 1) * c] for i in range(n)])


FIRST = ["ffn1_w_gate", "ffn1_w_up", "ffn1_w_down"]
LATER = ["w_in", "w_out", "ffn2_w_gate", "ffn2_w_up", "ffn2_w_down"]
EARLY_GRADS = ["ffn2_w_gate", "ffn2_w_up", "ffn2_w_down", "w_out"]
LATE_GRADS = ["w_in", "ffn1_w_gate", "ffn1_w_up", "ffn1_w_down"]


def _pair_sums(gb, names, where):
    theirs = _pair_exchange([gb[n] for n in names])
    return zip(*[_pair_sum(gb[n], t, where, "pair_sum_" + n) for n, t in zip(names, theirs)])


def _local_step(x, tgt, stacks, conv_stack, small, where):
    big = dict(zip(FIRST, _gather_weights([stacks[n] for n in FIRST], [])))
    wa = _block_diag(small["rg_w_a"]).astype(BF16)
    wx = _block_diag(small["rg_w_x"]).astype(BF16)

    x1, g1, u1, hb1, ab1, *landed = _ffn_fwd(x, small["ffn1_norm"], big["ffn1_w_gate"], big["ffn1_w_up"],
                                             big["ffn1_w_down"], gather=([stacks[n] for n in LATER], [conv_stack]))
    big.update(zip(LATER, _forward_weights(landed[:len(LATER)])))
    conv_w = jnp.transpose(landed[-1], (1, 0, 2)).reshape(CONV_W, D_RNN)
    wout = big["w_out"].reshape(D_MODEL, D_MODEL)
    rg = (conv_w, small["conv_b"], wa, small["rg_b_a"], wx, small["rg_b_x"], small["rg_lambda"])
    proj, hb2 = _mix_pre(x1, small["mix_norm"], big["w_in"])
    yr, hseq = _rglru_fwd(proj, *rg)
    ya = _attn_fwd(proj, small["q_norm"], small["k_norm"])
    x2 = _mix_post(x1, yr, ya, small["rnn_out_norm"], small["attn_out_norm"], wout)
    dx3, g2, u2, hb3, ab3, loss = _ffn_fwd(x2, small["ffn2_norm"], big["ffn2_w_gate"], big["ffn2_w_up"],
                                          big["ffn2_w_down"], tgt)

    gb, gs = {}, {}
    dx2, dg2, du2, dyb2, gs["ffn2_norm"] = _ffn_bwd_act(
        x2, small["ffn2_norm"], dx3, g2, u2, big["ffn2_w_gate"], big["ffn2_w_up"], big["ffn2_w_down"], "ffn2_bwd")
    gb["ffn2_w_gate"], gb["ffn2_w_up"], gb["ffn2_w_down"] = _ffn_wgrads(hb3, ab3, dg2, du2, dyb2, "ffn2")

    dyr, dya, ycat, dxb2, gs["rnn_out_norm"], gs["attn_out_norm"] = _mix_post_bwd(
        dx2, yr, ya, small["rnn_out_norm"], small["attn_out_norm"], wout)
    quarter = D_MODEL // N_CHIPS
    gb["w_out"] = _wgrad(ycat, dxb2, lambda tk: pl.BlockSpec((tk, quarter), lambda j, k: (k, j)),
                         lambda tk: pl.BlockSpec((tk, D_MODEL), lambda j, k: (k, 0)),
                         quarter, D_MODEL, 1.0, "wgrad_out")
    pair, own = _pair_sums(gb, EARLY_GRADS, where)
    (dq, dk, dv, gs["q_norm"], gs["k_norm"]), early = _attn_bwd(
        proj, dya, small["q_norm"], small["k_norm"], list(pair), list(own))
    dxr, dgate, gs["conv_w"], gs["conv_b"], dwa, gs["rg_b_a"], dwx, gs["rg_b_x"], gs["rg_lambda"] = _rglru_bwd(
        proj, hseq, dyr, *rg)
    gs["rg_w_a"] = _diag_blocks(dwa, RNN_BLOCKS)
    gs["rg_w_x"] = _diag_blocks(dwx, RNN_BLOCKS)
    dpb = jnp.concatenate([dxr, dgate, dq, dk, dv], axis=1)
    cb = N_IN // N_CHIPS
    gb["w_in"] = _wgrad(hb2, dpb, lambda tk: pl.BlockSpec((tk, D_MODEL), lambda j, k: (k, 0)),
                        lambda tk: pl.BlockSpec((tk, cb), lambda j, k: (k, j)),
                        D_MODEL, cb, 1.0, "wgrad_in")
    dx1, gs["mix_norm"] = _mix_pre_bwd(x1, small["mix_norm"], dx2, dpb, big["w_in"])

    dx0, dg1, du1, dyb1, gs["ffn1_norm"] = _ffn_bwd_act(
        x, small["ffn1_norm"], dx1, g1, u1, big["ffn1_w_gate"], big["ffn1_w_up"], big["ffn1_w_down"], "ffn1_bwd")
    gb["ffn1_w_gate"], gb["ffn1_w_up"], gb["ffn1_w_down"] = _ffn_wgrads(hb1, ab1, dg1, du1, dyb1, "ffn1")
    pair, own = _pair_sums(gb, LATE_GRADS, where)
    slots = dict(zip(EARLY_GRADS, early))
    slots.update(zip(LATE_GRADS, _chip_exchange(list(pair), list(own))))
    return loss[0, 0], dx0, slots, gs


ANY = pl.BlockSpec(memory_space=pl.ANY)


def _place():
    x, y, c = lax.axis_index("x"), lax.axis_index("y"), lax.axis_index("c")
    other_chips = [(1 - x, y), (x, 1 - y), (1 - x, 1 - y)]
    return x, y, c, 2 * x + y, other_chips


def _remote(src, dst, send_sem, recv_sem, to):
    return pltpu.make_async_remote_copy(src_ref=src, dst_ref=dst, send_sem=send_sem, recv_sem=recv_sem,
                                        device_id=to, device_id_type=MESH)


def _copy_plan(pairs):
    sends = [functools.partial(_remote, *a) for a, _ in pairs]
    arrivals = [functools.partial(_remote, *b) for _, b in pairs]
    return sends, arrivals


def _start(makers):
    for make in makers:
        make().start()


def _finish(sends, arrivals):
    for make in arrivals:
        make().wait_recv()
    for make in sends:
        make().wait_send()


def _half(rows, c):
    return pl.ds(pl.multiple_of(c * rows, 16), rows)


def _gather_weights(split, whole):
    arrs = list(split) + list(whole)
    n, ns = len(arrs), len(split)

    def body(*refs):
        outs = refs[n:2 * n]
        send_sems, recv_sems, fsend_sems, frecv_sems = refs[2 * n:]
        sends, arrivals = _gather_ici(outs, ns, send_sems, recv_sems)
        passes, passed = _gather_d2d(outs[:ns], fsend_sems, frecv_sems)
        _start(sends)
        for k, make in enumerate(arrivals):
            make().wait_recv()
            if k < 3 * ns:
                passes[k]().start()
        _finish(sends + passes, passed)

    return pl.pallas_call(
        body, name="gather_weights",
        in_specs=[ANY] * n, out_specs=[ANY] * n,
        out_shape=[jax.ShapeDtypeStruct(a.shape, a.dtype) for a in arrs],
        input_output_aliases={i: i for i in range(n)},
        scratch_shapes=[pltpu.SemaphoreType.DMA((3 * n,)), pltpu.SemaphoreType.DMA((3 * n,)),
                        pltpu.SemaphoreType.DMA((3 * ns,)), pltpu.SemaphoreType.DMA((3 * ns,))],
    )(*arrs)


def _gather_ici(stacks, n_split, send_sems, recv_sems):
    x, y, c, me, chips = _place()

    def region(i, chip):
        if i < n_split:
            return stacks[i].at[chip, _half(stacks[i].shape[1] // 2, c)]
        return stacks[i].at[chip]

    pairs = []
    for i in range(len(stacks)):
        for p, (cx, cy) in enumerate(chips):
            k = 3 * i + p
            mine, got = region(i, me), region(i, 2 * cx + cy)
            sems, to = (send_sems.at[k], recv_sems.at[k]), (cx, cy, c)
            pairs.append(((mine, mine, *sems, to), (got, got, *sems, to)))
    return _copy_plan(pairs)


def _gather_d2d(stacks, send_sems, recv_sems):
    x, y, c, _, chips = _place()
    sibling = (x, y, 1 - c)
    pairs = []
    for i, stack in enumerate(stacks):
        rows = stack.shape[1] // 2
        for p, (cx, cy) in enumerate(chips):
            k = 3 * i + p
            got, theirs = stack.at[2 * cx + cy, _half(rows, c)], stack.at[2 * cx + cy, _half(rows, 1 - c)]
            sems = (send_sems.at[k], recv_sems.at[k])
            pairs.append(((got, got, *sems, sibling), (theirs, theirs, *sems, sibling)))
    return _copy_plan(pairs)


def _forward_weights(split):
    n = len(split)

    def body(*refs):
        sends, arrivals = _gather_d2d(refs[n:2 * n], *refs[2 * n:])
        _start(sends)
        _finish(sends, arrivals)

    return pl.pallas_call(
        body, name="forward_weights",
        in_specs=[ANY] * n, out_specs=[ANY] * n,
        out_shape=[jax.ShapeDtypeStruct(a.shape, a.dtype) for a in split],
        input_output_aliases={i: i for i in range(n)},
        scratch_shapes=[pltpu.SemaphoreType.DMA((3 * n,))] * 2,
    )(*split)


def _pair_exchange(grads):
    n = len(grads)

    def body(*refs):
        ins, theirs = refs[:n], refs[n:2 * n]
        send_sems, recv_sems = refs[2 * n:]
        x, y, c, _, _ = _place()
        sibling = (x, y, 1 - c)
        sends = [_remote(ins[k].at[:, _half(grads[k].shape[1] // 2, 1 - c)], theirs[k],
                         send_sems.at[k], recv_sems.at[k], sibling) for k in range(n)]
        for cp in sends:
            cp.start()
        for k in range(n):
            _remote(theirs[k], theirs[k], send_sems.at[k], recv_sems.at[k], sibling).wait_recv()
        for cp in sends:
            cp.wait_send()

    return pl.pallas_call(
        body, name="grad_pair_exchange",
        in_specs=[ANY] * n, out_specs=[ANY] * n,
        out_shape=[jax.ShapeDtypeStruct((g.shape[0], g.shape[1] // 2, g.shape[2]), g.dtype) for g in grads],
        scratch_shapes=[pltpu.SemaphoreType.DMA((n,))] * 2,
    )(*grads)


def _chip_exchange(sums, slots):
    n = len(sums)

    def body(*refs):
        sends, arrivals = _chip_copies(refs[:n], refs[2 * n:3 * n], *refs[3 * n:])
        _start(sends)
        _finish(sends, arrivals)

    return pl.pallas_call(
        body, name="grad_chip_exchange",
        in_specs=[ANY] * (2 * n), out_specs=[ANY] * n,
        out_shape=[jax.ShapeDtypeStruct(a.shape, a.dtype) for a in slots],
        input_output_aliases={n + k: k for k in range(n)},
        scratch_shapes=[pltpu.SemaphoreType.DMA((3 * n,)), pltpu.SemaphoreType.DMA((3 * n,))],
    )(*sums, *slots)


def _chip_copies(sums, slots, send_sems, recv_sems):
    x, y, c, me, chips = _place()
    pairs = []
    for k in range(len(sums)):
        for p, (cx, cy) in enumerate(chips):
            j = 3 * k + p
            got = slots[k].at[2 * cx + cy]
            sems, to = (send_sems.at[j], recv_sems.at[j]), (cx, cy, c)
            pairs.append(((sums[k].at[2 * cx + cy], slots[k].at[me], *sems, to), (got, got, *sems, to)))
    return _copy_plan(pairs)


def _half_swap(halves):
    n = len(halves)

    def body(*refs):
        outs = refs[n:2 * n]
        send_sems, recv_sems = refs[2 * n:]
        x, y, c, _, _ = _place()
        sibling = (x, y, 1 - c)
        sends = [_remote(outs[k].at[c], outs[k].at[c], send_sems.at[k], recv_sems.at[k], sibling) for k in range(n)]
        for cp in sends:
            cp.start()
        for k in range(n):
            got = outs[k].at[1 - c]
            _remote(got, got, send_sems.at[k], recv_sems.at[k], sibling).wait_recv()
        for cp in sends:
            cp.wait_send()

    return pl.pallas_call(
        body, name="grad_half_swap",
        in_specs=[ANY] * n, out_specs=[ANY] * n,
        out_shape=[jax.ShapeDtypeStruct(a.shape, a.dtype) for a in halves],
        input_output_aliases={k: k for k in range(n)},
        scratch_shapes=[pltpu.SemaphoreType.DMA((n,))] * 2,
    )(*halves)


def _gather_small(packed):
    n_dev = 8

    def body(in_ref, out_ref, send_sems, recv_sems, loc_sem):
        x, y, c, _, _ = _place()
        me = 4 * x + 2 * y + c
        local = pltpu.make_async_copy(in_ref, out_ref.at[me], loc_sem)
        local.start()
        peers = []
        for k in range(1, n_dev):
            fx, fy, fc = (k >> 2) & 1, (k >> 1) & 1, k & 1
            peers.append((x ^ fx, y ^ fy, c ^ fc))
        sends = [_remote(in_ref, out_ref.at[me], send_sems.at[k], recv_sems.at[k], peer)
                 for k, peer in enumerate(peers)]
        for cp in sends:
            cp.start()
        for k, (px, py, pc) in enumerate(peers):
            got = out_ref.at[4 * px + 2 * py + pc]
            _remote(got, got, send_sems.at[k], recv_sems.at[k], (px, py, pc)).wait_recv()
        for cp in sends:
            cp.wait_send()
        local.wait()

    return pl.pallas_call(
        body, name="gather_small_grads",
        in_specs=[ANY], out_specs=ANY,
        out_shape=jax.ShapeDtypeStruct((n_dev,) + packed.shape, packed.dtype),
        scratch_shapes=[pltpu.SemaphoreType.DMA((n_dev - 1,)), pltpu.SemaphoreType.DMA((n_dev - 1,)),
                        pltpu.SemaphoreType.DMA],
    )(packed)


def _row_tile(r):
    return r // 4 if r >= 256 and (r // 4) % 16 == 0 else r


def _prefetch_call(body, name, grid, in_specs, out_specs, out_shape):
    spec = pltpu.PrefetchScalarGridSpec(num_scalar_prefetch=1, grid=grid, in_specs=in_specs, out_specs=out_specs)
    return pl.pallas_call(body, name=name, grid_spec=spec, out_shape=out_shape,
                          compiler_params=_params(("arbitrary",) * len(grid)))


def _place_shard(w2d, where, dtype, name):
    r, c = w2d.shape
    tr = _row_tile(r)

    def body(where_ref, w_ref, out_ref):
        out_ref[...] = w_ref[...].astype(dtype)

    return _prefetch_call(
        body, name, (r // tr,), [pl.BlockSpec((tr, c), lambda i, s: (i, 0))],
        pl.BlockSpec((None, tr, c), lambda i, s: (s[1], i, 0)),
        jax.ShapeDtypeStruct((N_CHIPS, r, c), dtype))(where, w2d)


def _pair_sum(full, theirs, where, name):
    nb, hs, c = theirs.shape

    def body(where_ref, a_ref, b_ref, out_ref, own_ref):
        total = (a_ref[...].astype(F32) + b_ref[...].astype(F32)).astype(BF16)
        out_ref[...] = total

        @pl.when(pl.program_id(0) == where_ref[1])
        def _():
            own_ref[...] = total

    blk = pl.BlockSpec((None, hs, c), lambda j, s: (j, 0, 0))
    shape = jax.ShapeDtypeStruct(theirs.shape, BF16)
    return _prefetch_call(
        body, name, (nb,), [pl.BlockSpec((None, hs, c), lambda j, s: (j, s[0], 0)), blk],
        [blk, pl.BlockSpec((None, hs, c), lambda j, s: (s[1], 0, 0))], [shape, shape])(where, full, theirs)


def _chip_sum(slots, where, name):
    nb, hs, c = slots.shape
    tr = _row_tile(hs)

    def body(where_ref, a_ref, out_ref):
        total = a_ref[0].astype(F32)
        for j in range(1, nb):
            total = total + a_ref[j].astype(F32)
        out_ref[...] = total

    return _prefetch_call(
        body, name, (hs // tr,), [pl.BlockSpec((nb, tr, c), lambda i, s: (0, i, 0))],
        pl.BlockSpec((None, tr, c), lambda i, s: (s[0], i, 0)),
        jax.ShapeDtypeStruct((2, hs, c), F32))(where, slots)


def _slot_sum(a, name):
    nb, r, c = a.shape
    tr = _row_tile(r)

    def body(a_ref, out_ref):
        total = a_ref[0].astype(F32)
        for j in range(1, nb):
            total = total + a_ref[j].astype(F32)
        out_ref[...] = total

    return pl.pallas_call(
        body, name=name, grid=(r // tr,),
        in_specs=[pl.BlockSpec((nb, tr, c), lambda i: (0, i, 0))],
        out_specs=pl.BlockSpec((tr, c), lambda i: (i, 0)),
        out_shape=jax.ShapeDtypeStruct((r, c), F32), compiler_params=_params(("arbitrary",)),
    )(a)


def _adamw(w, g, m, v, name):
    r, c = w.shape
    tr = _row_tile(r)
    c1 = 1.0 - ADAM_B1 ** ADAM_STEP
    c2 = 1.0 - ADAM_B2 ** ADAM_STEP

    def body(w_ref, g_ref, m_ref, v_ref, d_ref, m2_ref, v2_ref):
        gv = g_ref[...]
        m2 = ADAM_B1 * m_ref[...] + (1.0 - ADAM_B1) * gv
        v2 = ADAM_B2 * v_ref[...] + (1.0 - ADAM_B2) * (gv * gv)
        m2_ref[...] = m2
        v2_ref[...] = v2
        d_ref[...] = -ADAM_LR * ((m2 / c1) / (jnp.sqrt(v2 / c2) + ADAM_EPS) + ADAM_WD * w_ref[...])

    blk = pl.BlockSpec((tr, c), lambda i: (i, 0))
    return pl.pallas_call(
        body, name=name, grid=(r // tr,), in_specs=[blk] * 4, out_specs=[blk] * 3,
        out_shape=[jax.ShapeDtypeStruct((r, c), F32)] * 3, compiler_params=_params(("arbitrary",)),
    )(w, g, m, v)


WEIGHTS = ["ffn1_norm", "ffn1_w_gate", "ffn1_w_up", "ffn1_w_down", "mix_norm", "w_in", "conv_w", "conv_b",
           "rg_w_a", "rg_b_a", "rg_w_x", "rg_b_x", "rg_lambda", "q_norm", "k_norm", "rnn_out_norm",
           "attn_out_norm", "w_out", "ffn2_norm", "ffn2_w_gate", "ffn2_w_up", "ffn2_w_down"]
BIG = ["ffn1_w_gate", "ffn1_w_up", "ffn1_w_down", "w_in", "w_out", "ffn2_w_gate", "ffn2_w_up", "ffn2_w_down"]
SMALL = [n for n in WEIGHTS if n not in BIG]
PACK_LANES = 128
PACK_ROW_ALIGN = 8


def _hidden_major(name, a):
    return jnp.transpose(a) if name.endswith(("w_gate", "w_up")) else a


def _pack(parts):
    flat = jnp.concatenate([p.reshape(-1) for p in parts])
    unit = PACK_LANES * PACK_ROW_ALIGN
    padded = -(-flat.shape[0] // unit) * unit
    return jnp.pad(flat, (0, padded - flat.shape[0])).reshape(-1, PACK_LANES)


def _unpack(packed, shapes):
    flat = packed.reshape(-1)
    out, at = [], 0
    for shp in shapes:
        size = math.prod(shp)
        out.append(flat[at:at + size].reshape(shp))
        at += size
    return out


def kernel(x, ffn1_norm, ffn1_w_gate, ffn1_w_up, ffn1_w_down, mix_norm, w_in, conv_w, conv_b, rg_w_a, rg_b_a, rg_w_x, rg_b_x, rg_lambda, q_norm, k_norm, rnn_out_norm, attn_out_norm, w_out, ffn2_norm, ffn2_w_gate, ffn2_w_up, ffn2_w_down, loss_target, m_ffn1_norm, m_ffn1_w_gate, m_ffn1_w_up, m_ffn1_w_down, m_mix_norm, m_w_in, m_conv_w, m_conv_b, m_rg_w_a, m_rg_b_a, m_rg_w_x, m_rg_b_x, m_rg_lambda, m_q_norm, m_k_norm, m_rnn_out_norm, m_attn_out_norm, m_w_out, m_ffn2_norm, m_ffn2_w_gate, m_ffn2_w_up, m_ffn2_w_down, v_ffn1_norm, v_ffn1_w_gate, v_ffn1_w_up, v_ffn1_w_down, v_mix_norm, v_w_in, v_conv_w, v_conv_b, v_rg_w_a, v_rg_b_a, v_rg_w_x, v_rg_b_x, v_rg_lambda, v_q_norm, v_k_norm, v_rnn_out_norm, v_attn_out_norm, v_w_out, v_ffn2_norm, v_ffn2_w_gate, v_ffn2_w_up, v_ffn2_w_down):
    given = dict(locals())
    w = {n: given[n] for n in WEIGHTS}
    m = {n: given["m_" + n] for n in WEIGHTS}
    v = {n: given["v_" + n] for n in WEIGHTS}
    chip = 2 * lax.axis_index("x") + lax.axis_index("y")

    where = jnp.stack([lax.axis_index("c"), chip]).astype(jnp.int32)

    stacks = {n: _place_shard(_hidden_major(n, w[n][0]), where, BF16, "place_" + n) for n in BIG}
    conv_stack = _place_shard(w["conv_w"][0], where, F32, "place_conv_w")
    small = {n: (w[n][0] if w[n].ndim > 2 else w[n]) for n in SMALL if n != "conv_w"}

    loss, grad_x, slots, gs = _local_step(x[0], loss_target[0], stacks, conv_stack, small, where)
    loss = lax.psum(loss, ("x", "y", "c"))

    swapped = _half_swap([_chip_sum(slots[n], where, "chip_sum_" + n) for n in BIG])
    grads, deltas, new_m, new_v = {}, {}, {}, {}
    for n, t in zip(BIG, swapped):
        g2 = t.reshape(t.shape[0] * t.shape[1], t.shape[2])
        d2, m2, v2 = _adamw(_hidden_major(n, w[n][0]), g2, _hidden_major(n, m[n][0]), _hidden_major(n, v[n][0]),
                            "adamw_" + n)
        back = lambda a: _hidden_major(n, a).reshape(w[n].shape)
        grads[n], deltas[n], new_m[n], new_v[n] = back(g2), back(d2), back(m2), back(v2)

    full_shapes = [gs[n].shape for n in SMALL]
    everyone = _gather_small(_pack([gs[n] for n in SMALL]))
    g_small = _slot_sum(everyone, "small_grad_sum")
    g_parts = dict(zip(SMALL, _unpack(g_small, full_shapes)))
    quarter = D_RNN // N_CHIPS
    g_parts["conv_w"] = lax.dynamic_slice_in_dim(g_parts["conv_w"], chip * quarter, quarter, axis=1)
    local_shapes = [w[n].shape for n in SMALL]
    pk = lambda tree: _pack([tree[n] for n in SMALL])
    d_s, m_s, v_s = _adamw(pk(w), pk(g_parts), pk(m), pk(v), "adamw_small")
    for tree, packed in ((grads, pk(g_parts)), (deltas, d_s), (new_m, m_s), (new_v, v_s)):
        tree.update(zip(SMALL, _unpack(packed, local_shapes)))

    return (loss, grad_x.reshape(x.shape), *[grads[n] for n in WEIGHTS], *[deltas[n] for n in WEIGHTS],
            *[new_m[n] for n in WEIGHTS], *[new_v[n] for n in WEIGHTS])
```

```python
import functools
import math

import jax
import jax.numpy as jnp
from jax import lax
from jax.experimental import pallas as pl
from jax.experimental.pallas import tpu as pltpu

F32 = jnp.float32
BF16 = jnp.bfloat16
MESH = pl.DeviceIdType.MESH

D_MODEL = 1024
N_CHIPS = 4
D_RNN = 512
D_ATT = 512
N_HEADS = 8
HEAD_DIM = 64
RNN_BLOCKS = 8
CONV_W = 4
RG_C = 8.0
N_IN = 2 * D_RNN + 3 * D_ATT
EPS = 1e-6
ATT_BLOCK = 128
ATT_WINDOW = 384
ATT_SPLIT = 256
EXP_ZERO = -105.0

ADAM_LR = 0.001
ADAM_B1 = 0.9
ADAM_B2 = 0.999
ADAM_EPS = 1e-08
ADAM_WD = 0.01
ADAM_STEP = 10

V7X_VMEM_LIMIT = 56 * 1024 * 1024
TOKEN_TILE = 512
FFN_TILE = 256
WGRAD_TILE = 2048

GELU_K0 = math.sqrt(2.0 / math.pi)
GELU_K1 = 0.044715


def _params(sem=None):
    return pltpu.CompilerParams(dimension_semantics=sem, vmem_limit_bytes=V7X_VMEM_LIMIT)


def _dot(a, b):
    return jnp.dot(a, b, preferred_element_type=F32)


def _dot_nt(a, b):
    return lax.dot_general(a, b, (((1,), (1,)), ((), ())), preferred_element_type=F32)


def _dot_tn(a, b):
    return lax.dot_general(a, b, (((0,), (0,)), ((), ())), preferred_element_type=F32)


def _sigmoid(x):
    return 1.0 / (1.0 + jnp.exp(-x))


def _rms_r(xv):
    return lax.rsqrt(jnp.mean(xv * xv, axis=-1, keepdims=True) + EPS)


def _rms_bwd(xv, r, nw, dh):
    t = dh * nw
    dx = r * t - xv * (r * r * r * jnp.mean(t * xv, axis=-1, keepdims=True))
    dn = jnp.sum(dh * xv * r, axis=0, keepdims=True)
    return dx, dn


def _gelu(x):
    t = jnp.tanh(GELU_K0 * (x + GELU_K1 * x * x * x))
    return 0.5 * x * (1.0 + t)


def _gelu_grad(x):
    t = jnp.tanh(GELU_K0 * (x + GELU_K1 * x * x * x))
    return 0.5 * (1.0 + t) + 0.5 * x * (1.0 - t * t) * (GELU_K0 * (1.0 + 3.0 * GELU_K1 * x * x))


def _expm1_neg(x):
    p = 1.0 + x * (1.0 / 8.0)
    for k in (7.0, 6.0, 5.0, 4.0, 3.0, 2.0):
        p = 1.0 + x * (1.0 / k) * p
    return jnp.where(x > -0.25, x * p, jnp.exp(x) - 1.0)


def _log_sigmoid(x):
    return jnp.minimum(x, 0.0) - jnp.log(1.0 + jnp.exp(-jnp.abs(x)))


def _tile(s):
    return min(TOKEN_TILE, s)


def _ffn_fwd(x, nw, wg, wu, wd, tgt=None, gather=None):
    s, d = x.shape
    nb, fb, _ = wg.shape
    tm = min(FFN_TILE, s)
    ni = s // tm
    assert s % tm == 0
    with_loss = tgt is not None
    carried = list(gather[0]) + list(gather[1]) if gather else []
    n_split = len(gather[0]) if gather else 0
    nc = len(carried)

    def body(*refs):
        x_ref, nw_ref, wg_ref, wu_ref, wd_ref = refs[:5]
        at = 5
        if with_loss:
            tgt_ref = refs[at]
            at += 1
        at += nc
        out_ref, g_ref, u_ref, hb_ref, ab_ref = refs[at:at + 5]
        at += 5
        if with_loss:
            loss_ref = refs[at]
            at += 1
        stacks = refs[at:at + nc]
        sems = refs[at + nc:]
        i = pl.program_id(0)

        if nc:
            @pl.when(i == 0)
            def _():
                _start(_gather_ici(stacks, n_split, *sems)[0])

        xv = x_ref[...]
        hb = (xv * _rms_r(xv) * nw_ref[...]).astype(BF16)
        hb_ref[...] = hb
        y = jnp.zeros((tm, d), F32)
        for jb in range(nb):
            g = _dot_nt(hb, wg_ref[jb])
            u = _dot_nt(hb, wu_ref[jb])
            g_ref[jb] = g.astype(BF16)
            u_ref[jb] = u.astype(BF16)
            ab = (g * _sigmoid(g) * u).astype(BF16)
            ab_ref[jb] = ab
            y = y + _dot(ab, wd_ref[jb])
        y = xv + 0.5 * y
        if with_loss:
            diff = y - tgt_ref[...]
            out_ref[...] = diff * (1.0 / d)

            @pl.when(i == 0)
            def _():
                loss_ref[...] = jnp.zeros_like(loss_ref)

            loss_ref[...] += jnp.sum(diff * diff) * (0.5 / d)
        else:
            out_ref[...] = y

        if nc:
            @pl.when(i == ni - 1)
            def _():
                _finish(*_gather_ici(stacks, n_split, *sems))

    row = pl.BlockSpec((tm, d), lambda i: (i, 0))
    weight = pl.BlockSpec((nb, fb, d), lambda i: (0, 0, 0), pipeline_mode=pl.Buffered(1))
    in_specs = [row, pl.BlockSpec((1, d), lambda i: (0, 0)), weight, weight, weight]
    args = [x, nw, wg, wu, wd]
    if with_loss:
        in_specs.append(row)
        args.append(tgt)
    blk = pl.BlockSpec((nb, tm, fb), lambda i: (0, i, 0))
    out_shape = [jax.ShapeDtypeStruct((s, d), F32), jax.ShapeDtypeStruct((nb, s, fb), BF16),
                 jax.ShapeDtypeStruct((nb, s, fb), BF16), jax.ShapeDtypeStruct((s, d), BF16),
                 jax.ShapeDtypeStruct((nb, s, fb), BF16)]
    out_specs = [row, blk, blk, row, blk]
    if with_loss:
        out_shape.append(jax.ShapeDtypeStruct((1, 128), F32))
        out_specs.append(pl.BlockSpec((1, 128), lambda i: (0, 0)))
    aliases = {len(args) + k: len(out_shape) + k for k in range(nc)}
    out_shape += [jax.ShapeDtypeStruct(a.shape, a.dtype) for a in carried]
    return pl.pallas_call(
        body, name="ffn_fwd_loss" if with_loss else "ffn_fwd",
        grid=(ni,), in_specs=in_specs + [ANY] * nc, out_specs=out_specs + [ANY] * nc, out_shape=out_shape,
        input_output_aliases=aliases,
        scratch_shapes=[pltpu.SemaphoreType.DMA((3 * nc,))] * 2 if nc else [],
        compiler_params=_params(("arbitrary",)),
    )(*args, *carried)


def _ffn_bwd_act(x, nw, dy, g, u, wg, wu, wd, name):
    s, d = x.shape
    nb, fb, _ = wg.shape
    tm = min(FFN_TILE, s)
    assert s % tm == 0

    def body(x_ref, nw_ref, dy_ref, g_ref, u_ref, wg_ref, wu_ref, wd_ref,
             dx_ref, dg_ref, du_ref, dyb_ref, dnw_ref):
        dyv = dy_ref[...]
        dyb = dyv.astype(BF16)
        dyb_ref[...] = dyb
        dh = jnp.zeros((tm, d), F32)
        for jb in range(nb):
            da = 0.5 * _dot_nt(dyb, wd_ref[jb])
            gv = g_ref[jb].astype(F32)
            sg = _sigmoid(gv)
            dub = (da * (gv * sg)).astype(BF16)
            dgb = (da * u_ref[jb].astype(F32) * (sg * (1.0 + gv * (1.0 - sg)))).astype(BF16)
            dg_ref[jb] = dgb
            du_ref[jb] = dub
            dh = dh + _dot(dgb, wg_ref[jb]) + _dot(dub, wu_ref[jb])
        xv = x_ref[...]
        dx, dn = _rms_bwd(xv, _rms_r(xv), nw_ref[...], dh)
        dx_ref[...] = dyv + dx

        @pl.when(pl.program_id(0) == 0)
        def _():
            dnw_ref[...] = jnp.zeros_like(dnw_ref)

        dnw_ref[...] += dn

    row = pl.BlockSpec((tm, d), lambda i: (i, 0))
    vec = pl.BlockSpec((1, d), lambda i: (0, 0))
    blk = pl.BlockSpec((nb, tm, fb), lambda i: (0, i, 0))
    weight = pl.BlockSpec((nb, fb, d), lambda i: (0, 0, 0), pipeline_mode=pl.Buffered(1))
    return pl.pallas_call(
        body, name=name, grid=(s // tm,),
        in_specs=[row, vec, row, blk, blk, weight, weight, weight],
        out_specs=[row, blk, blk, row, vec],
        out_shape=[jax.ShapeDtypeStruct((s, d), F32), jax.ShapeDtypeStruct((nb, s, fb), BF16),
                   jax.ShapeDtypeStruct((nb, s, fb), BF16), jax.ShapeDtypeStruct((s, d), BF16),
                   jax.ShapeDtypeStruct((1, d), F32)],
        compiler_params=_params(("arbitrary",)),
    )(x, nw, dy, g, u, wg, wu, wd)


def _wgrad(a, b, a_spec, b_spec, out_rows, out_cols, scale, name, tk=None):
    s = a.shape[-2]
    tk = tk or _tile(s)
    nk = s // tk
    assert s % tk == 0

    def body(a_ref, b_ref, out_ref, acc):
        k = pl.program_id(1)

        @pl.when(k == 0)
        def _():
            acc[...] = jnp.zeros_like(acc)

        acc[...] += _dot_tn(a_ref[...], b_ref[...])

        @pl.when(k == nk - 1)
        def _():
            out_ref[...] = (acc[...] * scale).astype(BF16)

    return pl.pallas_call(
        body, name=name, grid=(N_CHIPS, nk),
        in_specs=[a_spec(tk), b_spec(tk)],
        out_specs=pl.BlockSpec((None, out_rows, out_cols), lambda j, k: (j, 0, 0)),
        out_shape=jax.ShapeDtypeStruct((N_CHIPS, out_rows, out_cols), BF16),
        scratch_shapes=[pltpu.VMEM((out_rows, out_cols), F32)],
        compiler_params=_params(("arbitrary", "arbitrary")),
    )(a, b)


def _ffn_wgrads(hb, ab, dg, du, dyb, tag):
    s, d = hb.shape
    fb = ab.shape[-1]
    tk = min(WGRAD_TILE, s)
    shared = lambda tk: pl.BlockSpec((tk, d), lambda j, k: (k, 0))
    stacked = lambda tk: pl.BlockSpec((None, tk, fb), lambda j, k: (j, k, 0))
    dwg = _wgrad(dg, hb, stacked, shared, fb, d, 1.0, "wgrad_gate_" + tag, tk)
    dwu = _wgrad(du, hb, stacked, shared, fb, d, 1.0, "wgrad_up_" + tag, tk)
    dwd = _wgrad(ab, dyb, stacked, shared, fb, d, 0.5, "wgrad_down_" + tag, tk)
    return dwg, dwu, dwd


def _mix_pre(x, nw, win):
    s, d = x.shape
    nb, _, cb = win.shape
    tm = _tile(s)

    def body(x_ref, nw_ref, w_ref, p_ref, hb_ref, hs):
        @pl.when(pl.program_id(1) == 0)
        def _():
            xv = x_ref[...]
            hb = (xv * _rms_r(xv) * nw_ref[...]).astype(BF16)
            hs[...] = hb
            hb_ref[...] = hb

        p_ref[...] = _dot(hs[...], w_ref[...])

    row = pl.BlockSpec((tm, d), lambda i, j: (i, 0))
    return pl.pallas_call(
        body, name="mix_pre", grid=(s // tm, nb),
        in_specs=[row, pl.BlockSpec((1, d), lambda i, j: (0, 0)),
                  pl.BlockSpec((None, d, cb), lambda i, j: (j, 0, 0))],
        out_specs=[pl.BlockSpec((tm, cb), lambda i, j: (i, j)), row],
        out_shape=[jax.ShapeDtypeStruct((s, nb * cb), F32), jax.ShapeDtypeStruct((s, d), BF16)],
        scratch_shapes=[pltpu.VMEM((tm, d), BF16)],
        compiler_params=_params(("arbitrary", "arbitrary")),
    )(x, nw, win)


def _mix_pre_bwd(x, nw, dres, dpb, win):
    s, d = x.shape
    nb, _, cb = win.shape
    tm = _tile(s)

    def body(x_ref, nw_ref, dres_ref, dp_ref, w_ref, dx_ref, dnw_ref, acc):
        i = pl.program_id(0)
        j = pl.program_id(1)

        @pl.when(j == 0)
        def _():
            acc[...] = jnp.zeros_like(acc)

        acc[...] += _dot_nt(dp_ref[...], w_ref[...])

        @pl.when(j == nb - 1)
        def _():
            xv = x_ref[...]
            dx, dn = _rms_bwd(xv, _rms_r(xv), nw_ref[...], acc[...])
            dx_ref[...] = dres_ref[...] + dx

            @pl.when(i == 0)
            def _():
                dnw_ref[...] = jnp.zeros_like(dnw_ref)

            dnw_ref[...] += dn

    row = pl.BlockSpec((tm, d), lambda i, j: (i, 0))
    vec = pl.BlockSpec((1, d), lambda i, j: (0, 0))
    return pl.pallas_call(
        body, name="mix_pre_bwd", grid=(s // tm, nb),
        in_specs=[row, vec, row, pl.BlockSpec((tm, cb), lambda i, j: (i, j)),
                  pl.BlockSpec((None, d, cb), lambda i, j: (j, 0, 0))],
        out_specs=[row, vec],
        out_shape=[jax.ShapeDtypeStruct((s, d), F32), jax.ShapeDtypeStruct((1, d), F32)],
        scratch_shapes=[pltpu.VMEM((tm, d), F32)],
        compiler_params=_params(("arbitrary", "arbitrary")),
    )(x, nw, dres, dpb, win)


def _mix_post(x, yr, ya, nr, na, wout):
    s, d = x.shape
    h = yr.shape[1]
    tm = _tile(s)

    def body(x_ref, yr_ref, ya_ref, nr_ref, na_ref, w_ref, out_ref):
        yrv = yr_ref[...]
        yav = ya_ref[...]
        onb = (yrv * _rms_r(yrv) * nr_ref[...]).astype(BF16)
        oab = (yav * _rms_r(yav) * na_ref[...]).astype(BF16)
        out_ref[...] = x_ref[...] + _dot(onb, w_ref[0:h, :]) + _dot(oab, w_ref[h:2 * h, :])

    row = pl.BlockSpec((tm, d), lambda i: (i, 0))
    half = pl.BlockSpec((tm, h), lambda i: (i, 0))
    vec = pl.BlockSpec((1, h), lambda i: (0, 0))
    return pl.pallas_call(
        body, name="mix_post", grid=(s // tm,),
        in_specs=[row, half, half, vec, vec, pl.BlockSpec((2 * h, d), lambda i: (0, 0))],
        out_specs=row, out_shape=jax.ShapeDtypeStruct((s, d), F32),
        compiler_params=_params(("arbitrary",)),
    )(x, yr, ya, nr, na, wout)


def _mix_post_bwd(dx, yr, ya, nr, na, wout):
    s, d = dx.shape
    h = yr.shape[1]
    tm = _tile(s)

    def body(dx_ref, yr_ref, ya_ref, nr_ref, na_ref, w_ref,
             dyr_ref, dya_ref, yc_ref, dxb_ref, dnr_ref, dna_ref):
        i = pl.program_id(0)
        dxb = dx_ref[...].astype(BF16)
        dxb_ref[...] = dxb
        dyc = _dot_nt(dxb, w_ref[...])
        yrv = yr_ref[...]
        yav = ya_ref[...]
        rr = _rms_r(yrv)
        ra = _rms_r(yav)
        yc_ref[:, 0:h] = (yrv * rr * nr_ref[...]).astype(BF16)
        yc_ref[:, h:2 * h] = (yav * ra * na_ref[...]).astype(BF16)
        dyr, dnr = _rms_bwd(yrv, rr, nr_ref[...], dyc[:, 0:h])
        dya, dna = _rms_bwd(yav, ra, na_ref[...], dyc[:, h:2 * h])
        dyr_ref[...] = dyr
        dya_ref[...] = dya

        @pl.when(i == 0)
        def _():
            dnr_ref[...] = jnp.zeros_like(dnr_ref)
            dna_ref[...] = jnp.zeros_like(dna_ref)

        dnr_ref[...] += dnr
        dna_ref[...] += dna

    row = pl.BlockSpec((tm, d), lambda i: (i, 0))
    half = pl.BlockSpec((tm, h), lambda i: (i, 0))
    vec = pl.BlockSpec((1, h), lambda i: (0, 0))
    return pl.pallas_call(
        body, name="mix_post_bwd", grid=(s // tm,),
        in_specs=[row, half, half, vec, vec, pl.BlockSpec((2 * h, d), lambda i: (0, 0))],
        out_specs=[half, half, pl.BlockSpec((tm, 2 * h), lambda i: (i, 0)), row, vec, vec],
        out_shape=[jax.ShapeDtypeStruct((s, h), F32), jax.ShapeDtypeStruct((s, h), F32),
                   jax.ShapeDtypeStruct((s, 2 * h), BF16), jax.ShapeDtypeStruct((s, d), BF16),
                   jax.ShapeDtypeStruct((1, h), F32), jax.ShapeDtypeStruct((1, h), F32)],
        compiler_params=_params(("arbitrary",)),
    )(dx, yr, ya, nr, na, wout)


def _shift_down(xv, s, prev8):
    rolled = pltpu.roll(xv, s, 0)
    row8 = lax.broadcasted_iota(jnp.int32, prev8.shape, 0)
    head = jnp.where(row8 < s, pltpu.roll(prev8, s, 0), rolled[0:8, :])
    return jnp.concatenate([head, rolled[8:, :]], axis=0)


def _shift_up(xv, s, next8):
    n = xv.shape[0]
    rolled = pltpu.roll(xv, n - s, 0)
    row8 = lax.broadcasted_iota(jnp.int32, next8.shape, 0)
    tail = jnp.where(row8 >= 8 - s, pltpu.roll(next8, 8 - s, 0), rolled[n - 8:, :])
    return jnp.concatenate([rolled[:n - 8, :], tail], axis=0)


def _scan_fwd(a, b):
    n = a.shape[0]
    row = lax.broadcasted_iota(jnp.int32, a.shape, 0)
    s = 1
    while s < n:
        ok = row >= s
        b = jnp.where(ok, a * pltpu.roll(b, s, 0) + b, b)
        a = jnp.where(ok, a * pltpu.roll(a, s, 0), a)
        s *= 2
    return b


def _scan_bwd(a, b):
    n = a.shape[0]
    row = lax.broadcasted_iota(jnp.int32, a.shape, 0)
    s = 1
    while s < n:
        ok = row < n - s
        b = jnp.where(ok, a * pltpu.roll(b, n - s, 0) + b, b)
        a = jnp.where(ok, a * pltpu.roll(a, n - s, 0), a)
        s *= 2
    return b


def _rglru_gates(xv, prev8, cw_ref, cb_ref, wa_ref, ba_ref, wx_ref, bx_ref, lam_ref):
    x1 = _shift_down(xv, 1, prev8)
    x2 = _shift_down(xv, 2, prev8)
    x3 = _shift_down(xv, 3, prev8)
    xc = cw_ref[3:4, :] * xv + cw_ref[2:3, :] * x1 + cw_ref[1:2, :] * x2 + cw_ref[0:1, :] * x3 + cb_ref[...]
    xcb = xc.astype(BF16)
    r = _sigmoid(_dot(xcb, wa_ref[...]) + ba_ref[...])
    ig = _sigmoid(_dot(xcb, wx_ref[...]) + bx_ref[...])
    c = RG_C * _log_sigmoid(lam_ref[...])
    la = r * c
    a = jnp.exp(la)
    m = jnp.sqrt(-_expm1_neg(2.0 * la))
    return (x1, x2, x3), xc, xcb, r, ig, c, a, m


def _rglru_fwd(proj, cw, cb, wa, ba, wx, bx, lam):
    s = proj.shape[0]
    w = D_RNN
    tm = _tile(s)

    def body(xr_ref, gate_ref, cw_ref, cb_ref, wa_ref, ba_ref, wx_ref, bx_ref, lam_ref,
             y_ref, h_ref, prev, hlast):
        @pl.when(pl.program_id(0) == 0)
        def _():
            prev[...] = jnp.zeros_like(prev)
            hlast[...] = jnp.zeros_like(hlast)

        xv = xr_ref[...]
        _, xc, _, _, ig, _, a, m = _rglru_gates(xv, prev[...], cw_ref, cb_ref, wa_ref, ba_ref,
                                                wx_ref, bx_ref, lam_ref)
        b = m * (ig * xc)
        row = lax.broadcasted_iota(jnp.int32, b.shape, 0)
        b = jnp.where(row == 0, b + a * hlast[...], b)
        h = _scan_fwd(a, b)
        h_ref[...] = h
        y_ref[...] = h * _gelu(gate_ref[...])
        prev[...] = xv[tm - 8:, :]
        hlast[...] = h[tm - 1:tm, :]

    vec = pl.BlockSpec((1, w), lambda i: (0, 0))
    sq = pl.BlockSpec((w, w), lambda i: (0, 0))
    out = pl.BlockSpec((tm, w), lambda i: (i, 0))
    return pl.pallas_call(
        body, name="rglru_fwd", grid=(s // tm,),
        in_specs=[pl.BlockSpec((tm, w), lambda i: (i, 0)), pl.BlockSpec((tm, w), lambda i: (i, 1)),
                  pl.BlockSpec((CONV_W, w), lambda i: (0, 0)), vec, sq, vec, sq, vec, vec],
        out_specs=[out, out],
        out_shape=[jax.ShapeDtypeStruct((s, w), F32), jax.ShapeDtypeStruct((s, w), F32)],
        scratch_shapes=[pltpu.VMEM((8, w), F32), pltpu.VMEM((1, w), F32)],
        compiler_params=_params(("arbitrary",)),
    )(proj, proj, cw, cb, wa, ba, wx, bx, lam)


def _rglru_bwd(proj, hseq, dyr, cw, cb, wa, ba, wx, bx, lam):
    s = proj.shape[0]
    w = D_RNN
    tm = _tile(s)
    nt = s // tm
    t8 = tm // 8

    def body(xr_ref, xp_ref, gate_ref, h_ref, hp_ref, dy_ref, cw_ref, cb_ref, wa_ref, ba_ref,
             wx_ref, bx_ref, lam_ref,
             dxr_ref, dgate_ref, dcw_ref, dcb_ref, dwa_ref, dba_ref, dwx_ref, dbx_ref, dlam_ref,
             carry, dxc_next):
        i = pl.program_id(0)
        first_tile = i == nt - 1

        @pl.when(i == 0)
        def _():
            carry[...] = jnp.zeros_like(carry)
            dxc_next[...] = jnp.zeros_like(dxc_next)
            for ref in (dcw_ref, dcb_ref, dwa_ref, dba_ref, dwx_ref, dbx_ref, dlam_ref):
                ref[...] = jnp.zeros_like(ref)

        xv = xr_ref[...]
        prev8 = jnp.where(first_tile, 0.0, xp_ref[...])
        hprev8 = jnp.where(first_tile, 0.0, hp_ref[...])
        (x1, x2, x3), xc, xcb, r, ig, c, a, m = _rglru_gates(
            xv, prev8, cw_ref, cb_ref, wa_ref, ba_ref, wx_ref, bx_ref, lam_ref)
        gv = gate_ref[...]
        hv = h_ref[...]
        dy = dy_ref[...]
        dgate_ref[...] = (dy * hv * _gelu_grad(gv)).astype(BF16)
        dh = dy * _gelu(gv)
        row = lax.broadcasted_iota(jnp.int32, dh.shape, 0)
        dh = jnp.where(row == tm - 1, dh + carry[...], dh)
        a_up = jnp.where(row == tm - 1, 0.0, pltpu.roll(a, tm - 1, 0))
        lam_t = _scan_bwd(a_up, dh)
        carry[...] = a[0:1, :] * lam_t[0:1, :]
        hm1 = _shift_down(hv, 1, hprev8)
        da = lam_t * hm1
        ixc = ig * xc
        dm = lam_t * ixc
        dig = lam_t * m * xc
        dxc = lam_t * m * ig
        dla = da * a - dm * (a * a) / m
        dr = dla * c
        dlam_ref[...] += jnp.sum(dla * r, axis=0, keepdims=True)
        dpa = dr * r * (1.0 - r)
        dpi = dig * ig * (1.0 - ig)
        dba_ref[...] += jnp.sum(dpa, axis=0, keepdims=True)
        dbx_ref[...] += jnp.sum(dpi, axis=0, keepdims=True)
        dpab = dpa.astype(BF16)
        dpib = dpi.astype(BF16)
        dwa_ref[...] += _dot_tn(xcb, dpab)
        dwx_ref[...] += _dot_tn(xcb, dpib)
        dxc = dxc + _dot_nt(dpab, wa_ref[...]) + _dot_nt(dpib, wx_ref[...])
        dcb_ref[...] += jnp.sum(dxc, axis=0, keepdims=True)
        dcw_ref[3:4, :] += jnp.sum(dxc * xv, axis=0, keepdims=True)
        dcw_ref[2:3, :] += jnp.sum(dxc * x1, axis=0, keepdims=True)
        dcw_ref[1:2, :] += jnp.sum(dxc * x2, axis=0, keepdims=True)
        dcw_ref[0:1, :] += jnp.sum(dxc * x3, axis=0, keepdims=True)
        nxt = dxc_next[...]
        dxr = (cw_ref[3:4, :] * dxc + cw_ref[2:3, :] * _shift_up(dxc, 1, nxt)
               + cw_ref[1:2, :] * _shift_up(dxc, 2, nxt) + cw_ref[0:1, :] * _shift_up(dxc, 3, nxt))
        dxr_ref[...] = dxr.astype(BF16)
        dxc_next[...] = dxc[0:8, :]

        @pl.when(first_tile)
        def _():
            lv = lam_ref[...]
            dlam_ref[...] = dlam_ref[...] * (RG_C * _sigmoid(-lv))

    rev = lambda i: nt - 1 - i
    vec = pl.BlockSpec((1, w), lambda i: (0, 0))
    sq = pl.BlockSpec((w, w), lambda i: (0, 0))
    cur = lambda col: pl.BlockSpec((tm, w), lambda i: (rev(i), col))
    before = lambda cols: pl.BlockSpec((8, w), lambda i: (jnp.maximum(rev(i) * t8 - 1, 0), 0))
    return pl.pallas_call(
        body, name="rglru_bwd", grid=(nt,),
        in_specs=[cur(0), before(None), cur(1), cur(0), before(None), cur(0),
                  pl.BlockSpec((CONV_W, w), lambda i: (0, 0)), vec, sq, vec, sq, vec, vec],
        out_specs=[cur(0), cur(0), pl.BlockSpec((CONV_W, w), lambda i: (0, 0)), vec, sq, vec, sq, vec, vec],
        out_shape=[jax.ShapeDtypeStruct((s, w), BF16), jax.ShapeDtypeStruct((s, w), BF16),
                   jax.ShapeDtypeStruct((CONV_W, w), F32), jax.ShapeDtypeStruct((1, w), F32),
                   jax.ShapeDtypeStruct((w, w), F32), jax.ShapeDtypeStruct((1, w), F32),
                   jax.ShapeDtypeStruct((w, w), F32), jax.ShapeDtypeStruct((1, w), F32),
                   jax.ShapeDtypeStruct((1, w), F32)],
        scratch_shapes=[pltpu.VMEM((1, w), F32), pltpu.VMEM((8, w), F32)],
        compiler_params=_params(("arbitrary",)),
    )(proj, proj, proj, hseq, hseq, dyr, cw, cb, wa, ba, wx, bx, lam)


def _sb_logs(z, valid):
    l1p = jnp.log(1.0 + jnp.exp(-jnp.abs(z)))
    lb = jnp.minimum(z, 0.0) - l1p
    lm = jnp.where(valid, -jnp.maximum(z, 0.0) - l1p, 0.0)
    return lb, lm


class _Window:
    def __init__(self):
        blk, win, cut = ATT_BLOCK, ATT_WINDOW, ATT_SPLIT
        self.row = lax.broadcasted_iota(jnp.int32, (blk, win), 0)
        self.col = lax.broadcasted_iota(jnp.int32, (blk, win), 1)

        def tri(n, later):
            j = lax.broadcasted_iota(jnp.int32, (n, n), 0)
            s = lax.broadcasted_iota(jnp.int32, (n, n), 1)
            return jnp.where((j > s) if later else (j < s), 1.0, 0.0).astype(BF16)

        self.later = (tri(cut, True), tri(win - cut, True))
        self.earlier = (tri(cut, False), tri(win - cut, False))

    def place(self, qi, g):
        end = (qi + 1) * ATT_BLOCK - g * ATT_WINDOW
        start = pl.multiple_of(jnp.maximum(end - ATT_WINDOW, 0), ATT_BLOCK)
        valid = start + self.col < jnp.minimum(qi * ATT_BLOCK + self.row, end)
        return start, valid

    @staticmethod
    def _parts(xv):
        hi = xv.astype(BF16)
        lo = (xv - hi.astype(F32)).astype(BF16)
        cut = ATT_SPLIT
        sums = (jnp.sum(xv[:, :cut], axis=1, keepdims=True), jnp.sum(xv[:, cut:], axis=1, keepdims=True))
        return (hi[:, :cut], lo[:, :cut]), (hi[:, cut:], lo[:, cut:]), sums

    def sums_after(self, xv, carry):
        (h0, l0), (h1, l1), (s0, s1) = self._parts(xv)
        first = _dot(h0, self.later[0]) + _dot(l0, self.later[0]) + (s1 + carry)
        last = _dot(h1, self.later[1]) + _dot(l1, self.later[1]) + carry
        return jnp.concatenate([first, last], axis=1), s0 + s1

    def sums_before(self, xv, carry):
        (h0, l0), (h1, l1), (s0, s1) = self._parts(xv)
        first = _dot(h0, self.earlier[0]) + _dot(l0, self.earlier[0]) + carry
        last = _dot(h1, self.earlier[1]) + _dot(l1, self.earlier[1]) + (s0 + carry)
        return jnp.concatenate([first, last], axis=1), s0 + s1


def _head_lanes(hh):
    return slice(hh * HEAD_DIM, (hh + 1) * HEAD_DIM)


def _attn_fwd(proj, qg, kg):
    s = proj.shape[0]
    blk, win, dh = ATT_BLOCK, ATT_WINDOW, HEAD_DIM
    nq = s // blk
    scale = 1.0 / math.sqrt(dh)
    heads = (0, 1)
    assert s >= win and s % blk == 0

    def body(q_ref, k_ref, v_ref, qg_ref, kg_ref, o_ref, qn, kn, vb, ob):
        wd = _Window()
        for h in heads:
            lanes = _head_lanes(h)
            qv = q_ref[:, lanes]
            qn[h] = (qv * _rms_r(qv) * qg_ref[...] * scale).astype(BF16)
            kv = k_ref[:, lanes]
            kn[h] = (kv * _rms_r(kv) * kg_ref[...]).astype(BF16)
            vb[h] = v_ref[:, lanes].astype(BF16)

        def q_step(qi, _):
            qoff = pl.multiple_of(qi * blk, blk)
            qts = [qn[h, pl.ds(qoff, blk), :] for h in heads]

            def more(carry):
                g, live = carry[:2]
                return jnp.logical_and((qi + 1) * blk - g * win > 0, live > 0)

            def window(carry):
                g, _, accs, runs = carry
                start, valid = wd.place(qi, g)
                zs = [_dot_nt(qts[h], kn[h, pl.ds(start, win), :]) for h in heads]
                logs = [_sb_logs(z, valid) for z in zs]
                sums = [wd.sums_after(logs[h][1], runs[h]) for h in heads]
                wgts = [jnp.where(valid, jnp.exp(logs[h][0] + sums[h][0]), 0.0).astype(BF16) for h in heads]
                accs = tuple(accs[h] + _dot(wgts[h], vb[h, pl.ds(start, win), :]) for h in heads)
                runs = tuple(runs[h] + sums[h][1] for h in heads)
                live = (jnp.maximum(jnp.max(runs[0]), jnp.max(runs[1])) > EXP_ZERO).astype(jnp.int32)
                return g + 1, live, accs, runs

            zero = lambda cols: tuple(jnp.zeros((blk, cols), F32) for _ in heads)
            _, _, accs, _ = lax.while_loop(more, window, (jnp.int32(0), jnp.int32(1), zero(dh), zero(1)))
            for h in heads:
                ob[h, pl.ds(qoff, blk), :] = accs[h]
            return 0

        lax.fori_loop(0, nq, q_step, 0)
        for h in heads:
            o_ref[:, _head_lanes(h)] = ob[h]

    pair = lambda group: pl.BlockSpec((s, 2 * dh), lambda p: (0, group * (D_ATT // (2 * dh)) + p))
    vec = pl.BlockSpec((1, dh), lambda p: (0, 0))
    return pl.pallas_call(
        body, name="attn_fwd", grid=(N_HEADS // 2,),
        in_specs=[pair(2), pair(3), pair(4), vec, vec], out_specs=pair(0),
        out_shape=jax.ShapeDtypeStruct((s, D_ATT), F32),
        scratch_shapes=[pltpu.VMEM((2, s, dh), BF16)] * 3 + [pltpu.VMEM((2, s, dh), F32)],
        compiler_params=_params(("arbitrary",)),
    )(proj, proj, proj, qg, kg)


def _attn_bwd(proj, dya, qg, kg, sums, slots):
    s = proj.shape[0]
    blk, win, dh = ATT_BLOCK, ATT_WINDOW, HEAD_DIM
    nq = s // blk
    max_windows = -(-s // win) + 1
    scale = 1.0 / math.sqrt(dh)
    steps = N_HEADS // 2
    heads = (0, 1)
    nx = len(sums)
    assert s >= win and s % blk == 0

    def body(*refs):
        q_ref, k_ref, v_ref, do_ref, qg_ref, kg_ref = refs[:6]
        sum_refs = refs[6:6 + nx]
        at = 6 + 2 * nx
        dq_ref, dk_ref, dv_ref, dqg_ref, dkg_ref = refs[at:at + 5]
        slot_refs = refs[at + 5:at + 5 + nx]
        qn, kn, vb, dob, runs_ref, dqn, dkn, dvn, send_sems, recv_sems = refs[at + 5 + nx:]
        wd = _Window()

        @pl.when(pl.program_id(0) == 0)
        def _():
            dqg_ref[...] = jnp.zeros_like(dqg_ref)
            dkg_ref[...] = jnp.zeros_like(dkg_ref)
            _start(_chip_copies(sum_refs, slot_refs, send_sems, recv_sems)[0])

        for h in heads:
            lanes = _head_lanes(h)
            qv = q_ref[:, lanes]
            qn[h] = (qv * _rms_r(qv) * qg_ref[...] * scale).astype(BF16)
            kv = k_ref[:, lanes]
            kn[h] = (kv * _rms_r(kv) * kg_ref[...]).astype(BF16)
            vb[h] = v_ref[:, lanes].astype(BF16)
            dob[h] = do_ref[:, lanes].astype(BF16)
        dkn[...] = jnp.zeros_like(dkn)
        dvn[...] = jnp.zeros_like(dvn)

        def q_step(qi, _):
            qoff = pl.multiple_of(qi * blk, blk)
            qts = [qn[h, pl.ds(qoff, blk), :] for h in heads]
            dots = [dob[h, pl.ds(qoff, blk), :] for h in heads]

            def more(carry):
                g, live = carry[:2]
                return jnp.logical_and((qi + 1) * blk - g * win > 0, live > 0)

            def run_window(carry):
                g, _, runs = carry
                start, valid = wd.place(qi, g)
                for h in heads:
                    runs_ref[h, g] = runs[h]
                zs = [_dot_nt(qts[h], kn[h, pl.ds(start, win), :]) for h in heads]
                runs = tuple(runs[h] + jnp.sum(_sb_logs(zs[h], valid)[1], axis=1, keepdims=True) for h in heads)
                live = (jnp.maximum(jnp.max(runs[0]), jnp.max(runs[1])) > EXP_ZERO).astype(jnp.int32)
                return g + 1, live, runs

            zero = lambda cols: tuple(jnp.zeros((blk, cols), F32) for _ in heads)
            windows, _, _ = lax.while_loop(more, run_window, (jnp.int32(0), jnp.int32(1), zero(1)))

            def k_window(gg, carry):
                dq_accs, esums = carry
                g = windows - 1 - gg
                start, valid = wd.place(qi, g)
                kts = [kn[h, pl.ds(start, win), :] for h in heads]
                vts = [vb[h, pl.ds(start, win), :] for h in heads]
                zs = [_dot_nt(qts[h], kts[h]) for h in heads]
                dws = [_dot_nt(dots[h], vts[h]) for h in heads]
                logs = [_sb_logs(z, valid) for z in zs]
                tails = [wd.sums_after(logs[h][1], runs_ref[h, g])[0] for h in heads]
                wgts = [jnp.where(valid, jnp.exp(logs[h][0] + tails[h]), 0.0) for h in heads]
                es = [dws[h] * wgts[h] for h in heads]
                befores = [wd.sums_before(es[h], esums[h]) for h in heads]
                dzbs = []
                for h in heads:
                    beta = jnp.exp(logs[h][0])
                    dz = jnp.where(valid, es[h] * (1.0 - beta) - befores[h][0] * beta, 0.0)
                    dzbs.append(dz.astype(BF16))
                dq_accs = tuple(dq_accs[h] + _dot(dzbs[h], kts[h]) for h in heads)
                for h in heads:
                    dkn[h, pl.ds(start, win), :] += _dot_tn(dzbs[h], qts[h])
                    dvn[h, pl.ds(start, win), :] += _dot_tn(wgts[h].astype(BF16), dots[h])
                return dq_accs, tuple(esums[h] + befores[h][1] for h in heads)

            dq_accs, _ = lax.fori_loop(0, windows, k_window, (zero(dh), zero(1)))
            for h in heads:
                dqn[h, pl.ds(qoff, blk), :] = dq_accs[h]
            return 0

        lax.fori_loop(0, nq, q_step, 0)

        for h in heads:
            lanes = _head_lanes(h)
            qv = q_ref[:, lanes]
            dq, dqg = _rms_bwd(qv, _rms_r(qv), qg_ref[...] * scale, dqn[h])
            dq_ref[:, lanes] = dq.astype(BF16)
            dqg_ref[...] += dqg * scale
            kv = k_ref[:, lanes]
            dk, dkg = _rms_bwd(kv, _rms_r(kv), kg_ref[...], dkn[h])
            dk_ref[:, lanes] = dk.astype(BF16)
            dkg_ref[...] += dkg
            dv_ref[:, lanes] = dvn[h].astype(BF16)

        @pl.when(pl.program_id(0) == steps - 1)
        def _():
            _finish(*_chip_copies(sum_refs, slot_refs, send_sems, recv_sems))

    pair = lambda group: pl.BlockSpec((s, 2 * dh), lambda p: (0, group * (D_ATT // (2 * dh)) + p))
    vec = pl.BlockSpec((1, dh), lambda p: (0, 0))
    outs = pl.pallas_call(
        body, name="attn_bwd", grid=(steps,),
        in_specs=[pair(2), pair(3), pair(4), pair(0), vec, vec] + [ANY] * (2 * nx),
        out_specs=[pair(0), pair(0), pair(0), vec, vec] + [ANY] * nx,
        out_shape=[jax.ShapeDtypeStruct((s, D_ATT), BF16)] * 3 + [jax.ShapeDtypeStruct((1, dh), F32)] * 2
        + [jax.ShapeDtypeStruct(a.shape, a.dtype) for a in slots],
        input_output_aliases={6 + nx + k: 5 + k for k in range(nx)},
        scratch_shapes=[pltpu.VMEM((2, s, dh), BF16)] * 4 + [pltpu.VMEM((2, max_windows, blk, 1), F32)]
        + [pltpu.VMEM((2, s, dh), F32)] * 3 + [pltpu.SemaphoreType.DMA((3 * nx,))] * 2,
        compiler_params=_params(("arbitrary",)),
    )(proj, proj, proj, dya, qg, kg, *sums, *slots)
    return outs[:5], outs[5:]


def _block_diag(w):
    n, c, d = w.shape
    return jnp.einsum("ncd,nm->ncmd", w, jnp.eye(n, dtype=w.dtype)).reshape(n * c, n * d)


def _diag_blocks(full, n):
    c = full.shape[0] // n
    return jnp.stack([full[i * c:(i + 1) * c, i * c:(i + 1) * c] for i in range(n)])


FIRST = ["ffn1_w_gate", "ffn1_w_up", "ffn1_w_down"]
LATER = ["w_in", "w_out", "ffn2_w_gate", "ffn2_w_up", "ffn2_w_down"]
EARLY_GRADS = ["ffn2_w_gate", "ffn2_w_up", "ffn2_w_down", "w_out"]
LATE_GRADS = ["w_in", "ffn1_w_gate", "ffn1_w_up", "ffn1_w_down"]


def _pair_sums(gb, names, where):
    theirs = _pair_exchange([gb[n] for n in names])
    return zip(*[_pair_sum(gb[n], t, where, "pair_sum_" + n) for n, t in zip(names, theirs)])


def _local_step(x, tgt, stacks, conv_stack, small, where):
    big = dict(zip(FIRST, _gather_weights([stacks[n] for n in FIRST], [])))
    wa = _block_diag(small["rg_w_a"]).astype(BF16)
    wx = _block_diag(small["rg_w_x"]).astype(BF16)

    x1, g1, u1, hb1, ab1, *landed = _ffn_fwd(x, small["ffn1_norm"], big["ffn1_w_gate"], big["ffn1_w_up"],
                                             big["ffn1_w_down"], gather=([stacks[n] for n in LATER], [conv_stack]))
    big.update(zip(LATER, _forward_weights(landed[:len(LATER)])))
    conv_w = jnp.transpose(landed[-1], (1, 0, 2)).reshape(CONV_W, D_RNN)
    wout = big["w_out"].reshape(D_MODEL, D_MODEL)
    rg = (conv_w, small["conv_b"], wa, small["rg_b_a"], wx, small["rg_b_x"], small["rg_lambda"])
    proj, hb2 = _mix_pre(x1, small["mix_norm"], big["w_in"])
    yr, hseq = _rglru_fwd(proj, *rg)
    ya = _attn_fwd(proj, small["q_norm"], small["k_norm"])
    x2 = _mix_post(x1, yr, ya, small["rnn_out_norm"], small["attn_out_norm"], wout)
    dx3, g2, u2, hb3, ab3, loss = _ffn_fwd(x2, small["ffn2_norm"], big["ffn2_w_gate"], big["ffn2_w_up"],
                                          big["ffn2_w_down"], tgt)

    gb, gs = {}, {}
    dx2, dg2, du2, dyb2, gs["ffn2_norm"] = _ffn_bwd_act(
        x2, small["ffn2_norm"], dx3, g2, u2, big["ffn2_w_gate"], big["ffn2_w_up"], big["ffn2_w_down"], "ffn2_bwd")
    gb["ffn2_w_gate"], gb["ffn2_w_up"], gb["ffn2_w_down"] = _ffn_wgrads(hb3, ab3, dg2, du2, dyb2, "ffn2")

    dyr, dya, ycat, dxb2, gs["rnn_out_norm"], gs["attn_out_norm"] = _mix_post_bwd(
        dx2, yr, ya, small["rnn_out_norm"], small["attn_out_norm"], wout)
    quarter = D_MODEL // N_CHIPS
    gb["w_out"] = _wgrad(ycat, dxb2, lambda tk: pl.BlockSpec((tk, quarter), lambda j, k: (k, j)),
                         lambda tk: pl.BlockSpec((tk, D_MODEL), lambda j, k: (k, 0)),
                         quarter, D_MODEL, 1.0, "wgrad_out")
    pair, own = _pair_sums(gb, EARLY_GRADS, where)
    (dq, dk, dv, gs["q_norm"], gs["k_norm"]), early = _attn_bwd(
        proj, dya, small["q_norm"], small["k_norm"], list(pair), list(own))
    dxr, dgate, gs["conv_w"], gs["conv_b"], dwa, gs["rg_b_a"], dwx, gs["rg_b_x"], gs["rg_lambda"] = _rglru_bwd(
        proj, hseq, dyr, *rg)
    gs["rg_w_a"] = _diag_blocks(dwa, RNN_BLOCKS)
    gs["rg_w_x"] = _diag_blocks(dwx, RNN_BLOCKS)
    dpb = jnp.concatenate([dxr, dgate, dq, dk, dv], axis=1)
    cb = N_IN // N_CHIPS
    gb["w_in"] = _wgrad(hb2, dpb, lambda tk: pl.BlockSpec((tk, D_MODEL), lambda j, k: (k, 0)),
                        lambda tk: pl.BlockSpec((tk, cb), lambda j, k: (k, j)),
                        D_MODEL, cb, 1.0, "wgrad_in")
    dx1, gs["mix_norm"] = _mix_pre_bwd(x1, small["mix_norm"], dx2, dpb, big["w_in"])

    dx0, dg1, du1, dyb1, gs["ffn1_norm"] = _ffn_bwd_act(
        x, small["ffn1_norm"], dx1, g1, u1, big["ffn1_w_gate"], big["ffn1_w_up"], big["ffn1_w_down"], "ffn1_bwd")
    gb["ffn1_w_gate"], gb["ffn1_w_up"], gb["ffn1_w_down"] = _ffn_wgrads(hb1, ab1, dg1, du1, dyb1, "ffn1")
    pair, own = _pair_sums(gb, LATE_GRADS, where)
    slots = dict(zip(EARLY_GRADS, early))
    slots.update(zip(LATE_GRADS, _chip_exchange(list(pair), list(own))))
    return loss[0, 0], dx0, slots, gs


ANY = pl.BlockSpec(memory_space=pl.ANY)


def _place():
    x, y, c = lax.axis_index("x"), lax.axis_index("y"), lax.axis_index("c")
    other_chips = [(1 - x, y), (x, 1 - y), (1 - x, 1 - y)]
    return x, y, c, 2 * x + y, other_chips


def _remote(src, dst, send_sem, recv_sem, to):
    return pltpu.make_async_remote_copy(src_ref=src, dst_ref=dst, send_sem=send_sem, recv_sem=recv_sem,
                                        device_id=to, device_id_type=MESH)


def _copy_plan(pairs):
    sends = [functools.partial(_remote, *a) for a, _ in pairs]
    arrivals = [functools.partial(_remote, *b) for _, b in pairs]
    return sends, arrivals


def _start(makers):
    for make in makers:
        make().start()


def _finish(sends, arrivals):
    for make in arrivals:
        make().wait_recv()
    for make in sends:
        make().wait_send()


def _half(rows, c):
    return pl.ds(pl.multiple_of(c * rows, 16), rows)


def _gather_weights(split, whole):
    arrs = list(split) + list(whole)
    n, ns = len(arrs), len(split)

    def body(*refs):
        outs = refs[n:2 * n]
        send_sems, recv_sems, fsend_sems, frecv_sems = refs[2 * n:]
        sends, arrivals = _gather_ici(outs, ns, send_sems, recv_sems)
        passes, passed = _gather_d2d(outs[:ns], fsend_sems, frecv_sems)
        _start(sends)
        for k, make in enumerate(arrivals):
            make().wait_recv()
            if k < 3 * ns:
                passes[k]().start()
        _finish(sends + passes, passed)

    return pl.pallas_call(
        body, name="gather_weights",
        in_specs=[ANY] * n, out_specs=[ANY] * n,
        out_shape=[jax.ShapeDtypeStruct(a.shape, a.dtype) for a in arrs],
        input_output_aliases={i: i for i in range(n)},
        scratch_shapes=[pltpu.SemaphoreType.DMA((3 * n,)), pltpu.SemaphoreType.DMA((3 * n,)),
                        pltpu.SemaphoreType.DMA((3 * ns,)), pltpu.SemaphoreType.DMA((3 * ns,))],
    )(*arrs)


def _gather_ici(stacks, n_split, send_sems, recv_sems):
    x, y, c, me, chips = _place()

    def region(i, chip):
        if i < n_split:
            return stacks[i].at[chip, _half(stacks[i].shape[1] // 2, c)]
        return stacks[i].at[chip]

    pairs = []
    for i in range(len(stacks)):
        for p, (cx, cy) in enumerate(chips):
            k = 3 * i + p
            mine, got = region(i, me), region(i, 2 * cx + cy)
            sems, to = (send_sems.at[k], recv_sems.at[k]), (cx, cy, c)
            pairs.append(((mine, mine, *sems, to), (got, got, *sems, to)))
    return _copy_plan(pairs)


def _gather_d2d(stacks, send_sems, recv_sems):
    x, y, c, _, chips = _place()
    sibling = (x, y, 1 - c)
    pairs = []
    for i, stack in enumerate(stacks):
        rows = stack.shape[1] // 2
        for p, (cx, cy) in enumerate(chips):
            k = 3 * i + p
            got, theirs = stack.at[2 * cx + cy, _half(rows, c)], stack.at[2 * cx + cy, _half(rows, 1 - c)]
            sems = (send_sems.at[k], recv_sems.at[k])
            pairs.append(((got, got, *sems, sibling), (theirs, theirs, *sems, sibling)))
    return _copy_plan(pairs)


def _forward_weights(split):
    n = len(split)

    def body(*refs):
        sends, arrivals = _gather_d2d(refs[n:2 * n], *refs[2 * n:])
        _start(sends)
        _finish(sends, arrivals)

    return pl.pallas_call(
        body, name="forward_weights",
        in_specs=[ANY] * n, out_specs=[ANY] * n,
        out_shape=[jax.ShapeDtypeStruct(a.shape, a.dtype) for a in split],
        input_output_aliases={i: i for i in range(n)},
        scratch_shapes=[pltpu.SemaphoreType.DMA((3 * n,))] * 2,
    )(*split)


def _pair_exchange(grads):
    n = len(grads)

    def body(*refs):
        ins, theirs = refs[:n], refs[n:2 * n]
        send_sems, recv_sems = refs[2 * n:]
        x, y, c, _, _ = _place()
        sibling = (x, y, 1 - c)
        sends = [_remote(ins[k].at[:, _half(grads[k].shape[1] // 2, 1 - c)], theirs[k],
                         send_sems.at[k], recv_sems.at[k], sibling) for k in range(n)]
        for cp in sends:
            cp.start()
        for k in range(n):
            _remote(theirs[k], theirs[k], send_sems.at[k], recv_sems.at[k], sibling).wait_recv()
        for cp in sends:
            cp.wait_send()

    return pl.pallas_call(
        body, name="grad_pair_exchange",
        in_specs=[ANY] * n, out_specs=[ANY] * n,
        out_shape=[jax.ShapeDtypeStruct((g.shape[0], g.shape[1] // 2, g.shape[2]), g.dtype) for g in grads],
        scratch_shapes=[pltpu.SemaphoreType.DMA((n,))] * 2,
    )(*grads)


def _chip_exchange(sums, slots):
    n = len(sums)

    def body(*refs):
        sends, arrivals = _chip_copies(refs[:n], refs[2 * n:3 * n], *refs[3 * n:])
        _start(sends)
        _finish(sends, arrivals)

    return pl.pallas_call(
        body, name="grad_chip_exchange",
        in_specs=[ANY] * (2 * n), out_specs=[ANY] * n,
        out_shape=[jax.ShapeDtypeStruct(a.shape, a.dtype) for a in slots],
        input_output_aliases={n + k: k for k in range(n)},
        scratch_shapes=[pltpu.SemaphoreType.DMA((3 * n,)), pltpu.SemaphoreType.DMA((3 * n,))],
    )(*sums, *slots)


def _chip_copies(sums, slots, send_sems, recv_sems):
    x, y, c, me, chips = _place()
    pairs = []
    for k in range(len(sums)):
        for p, (cx, cy) in enumerate(chips):
            j = 3 * k + p
            got = slots[k].at[2 * cx + cy]
            sems, to = (send_sems.at[j], recv_sems.at[j]), (cx, cy, c)
            pairs.append(((sums[k].at[2 * cx + cy], slots[k].at[me], *sems, to), (got, got, *sems, to)))
    return _copy_plan(pairs)


def _half_swap(halves):
    n = len(halves)

    def body(*refs):
        outs = refs[n:2 * n]
        send_sems, recv_sems = refs[2 * n:]
        x, y, c, _, _ = _place()
        sibling = (x, y, 1 - c)
        sends = [_remote(outs[k].at[c], outs[k].at[c], send_sems.at[k], recv_sems.at[k], sibling) for k in range(n)]
        for cp in sends:
            cp.start()
        for k in range(n):
            got = outs[k].at[1 - c]
            _remote(got, got, send_sems.at[k], recv_sems.at[k], sibling).wait_recv()
        for cp in sends:
            cp.wait_send()

    return pl.pallas_call(
        body, name="grad_half_swap",
        in_specs=[ANY] * n, out_specs=[ANY] * n,
        out_shape=[jax.ShapeDtypeStruct(a.shape, a.dtype) for a in halves],
        input_output_aliases={k: k for k in range(n)},
        scratch_shapes=[pltpu.SemaphoreType.DMA((n,))] * 2,
    )(*halves)


def _gather_small(packed):
    n_dev = 8

    def body(in_ref, out_ref, send_sems, recv_sems, loc_sem):
        x, y, c, _, _ = _place()
        me = 4 * x + 2 * y + c
        local = pltpu.make_async_copy(in_ref, out_ref.at[me], loc_sem)
        local.start()
        peers = []
        for k in range(1, n_dev):
            fx, fy, fc = (k >> 2) & 1, (k >> 1) & 1, k & 1
            peers.append((x ^ fx, y ^ fy, c ^ fc))
        sends = [_remote(in_ref, out_ref.at[me], send_sems.at[k], recv_sems.at[k], peer)
                 for k, peer in enumerate(peers)]
        for cp in sends:
            cp.start()
        for k, (px, py, pc) in enumerate(peers):
            got = out_ref.at[4 * px + 2 * py + pc]
            _remote(got, got, send_sems.at[k], recv_sems.at[k], (px, py, pc)).wait_recv()
        for cp in sends:
            cp.wait_send()
        local.wait()

    return pl.pallas_call(
        body, name="gather_small_grads",
        in_specs=[ANY], out_specs=ANY,
        out_shape=jax.ShapeDtypeStruct((n_dev,) + packed.shape, packed.dtype),
        scratch_shapes=[pltpu.SemaphoreType.DMA((n_dev - 1,)), pltpu.SemaphoreType.DMA((n_dev - 1,)),
                        pltpu.SemaphoreType.DMA],
    )(packed)


def _row_tile(r):
    return r // 4 if r >= 256 and (r // 4) % 16 == 0 else r


def _prefetch_call(body, name, grid, in_specs, out_specs, out_shape):
    spec = pltpu.PrefetchScalarGridSpec(num_scalar_prefetch=1, grid=grid, in_specs=in_specs, out_specs=out_specs)
    return pl.pallas_call(body, name=name, grid_spec=spec, out_shape=out_shape,
                          compiler_params=_params(("arbitrary",) * len(grid)))


def _place_shard(w2d, where, dtype, name):
    r, c = w2d.shape
    tr = _row_tile(r)

    def body(where_ref, w_ref, out_ref):
        out_ref[...] = w_ref[...].astype(dtype)

    return _prefetch_call(
        body, name, (r // tr,), [pl.BlockSpec((tr, c), lambda i, s: (i, 0))],
        pl.BlockSpec((None, tr, c), lambda i, s: (s[1], i, 0)),
        jax.ShapeDtypeStruct((N_CHIPS, r, c), dtype))(where, w2d)


def _pair_sum(full, theirs, where, name):
    nb, hs, c = theirs.shape

    def body(where_ref, a_ref, b_ref, out_ref, own_ref):
        total = (a_ref[...].astype(F32) + b_ref[...].astype(F32)).astype(BF16)
        out_ref[...] = total

        @pl.when(pl.program_id(0) == where_ref[1])
        def _():
            own_ref[...] = total

    blk = pl.BlockSpec((None, hs, c), lambda j, s: (j, 0, 0))
    shape = jax.ShapeDtypeStruct(theirs.shape, BF16)
    return _prefetch_call(
        body, name, (nb,), [pl.BlockSpec((None, hs, c), lambda j, s: (j, s[0], 0)), blk],
        [blk, pl.BlockSpec((None, hs, c), lambda j, s: (s[1], 0, 0))], [shape, shape])(where, full, theirs)


def _chip_sum(slots, where, name):
    nb, hs, c = slots.shape
    tr = _row_tile(hs)

    def body(where_ref, a_ref, out_ref):
        total = a_ref[0].astype(F32)
        for j in range(1, nb):
            total = total + a_ref[j].astype(F32)
        out_ref[...] = total

    return _prefetch_call(
        body, name, (hs // tr,), [pl.BlockSpec((nb, tr, c), lambda i, s: (0, i, 0))],
        pl.BlockSpec((None, tr, c), lambda i, s: (s[0], i, 0)),
        jax.ShapeDtypeStruct((2, hs, c), F32))(where, slots)


def _slot_sum(a, name):
    nb, r, c = a.shape
    tr = _row_tile(r)

    def body(a_ref, out_ref):
        total = a_ref[0].astype(F32)
        for j in range(1, nb):
            total = total + a_ref[j].astype(F32)
        out_ref[...] = total

    return pl.pallas_call(
        body, name=name, grid=(r // tr,),
        in_specs=[pl.BlockSpec((nb, tr, c), lambda i: (0, i, 0))],
        out_specs=pl.BlockSpec((tr, c), lambda i: (i, 0)),
        out_shape=jax.ShapeDtypeStruct((r, c), F32), compiler_params=_params(("arbitrary",)),
    )(a)


def _adamw(w, g, m, v, name):
    r, c = w.shape
    tr = _row_tile(r)
    c1 = 1.0 - ADAM_B1 ** ADAM_STEP
    c2 = 1.0 - ADAM_B2 ** ADAM_STEP

    def body(w_ref, g_ref, m_ref, v_ref, d_ref, m2_ref, v2_ref):
        gv = g_ref[...]
        m2 = ADAM_B1 * m_ref[...] + (1.0 - ADAM_B1) * gv
        v2 = ADAM_B2 * v_ref[...] + (1.0 - ADAM_B2) * (gv * gv)
        m2_ref[...] = m2
        v2_ref[...] = v2
        d_ref[...] = -ADAM_LR * ((m2 / c1) / (jnp.sqrt(v2 / c2) + ADAM_EPS) + ADAM_WD * w_ref[...])

    blk = pl.BlockSpec((tr, c), lambda i: (i, 0))
    return pl.pallas_call(
        body, name=name, grid=(r // tr,), in_specs=[blk] * 4, out_specs=[blk] * 3,
        out_shape=[jax.ShapeDtypeStruct((r, c), F32)] * 3, compiler_params=_params(("arbitrary",)),
    )(w, g, m, v)


WEIGHTS = ["ffn1_norm", "ffn1_w_gate", "ffn1_w_up", "ffn1_w_down", "mix_norm", "w_in", "conv_w", "conv_b",
           "rg_w_a", "rg_b_a", "rg_w_x", "rg_b_x", "rg_lambda", "q_norm", "k_norm", "rnn_out_norm",
           "attn_out_norm", "w_out", "ffn2_norm", "ffn2_w_gate", "ffn2_w_up", "ffn2_w_down"]
BIG = ["ffn1_w_gate", "ffn1_w_up", "ffn1_w_down", "w_in", "w_out", "ffn2_w_gate", "ffn2_w_up", "ffn2_w_down"]
SMALL = [n for n in WEIGHTS if n not in BIG]
PACK_LANES = 128
PACK_ROW_ALIGN = 8


def _hidden_major(name, a):
    return jnp.transpose(a) if name.endswith(("w_gate", "w_up")) else a


def _pack(parts):
    flat = jnp.concatenate([p.reshape(-1) for p in parts])
    unit = PACK_LANES * PACK_ROW_ALIGN
    padded = -(-flat.shape[0] // unit) * unit
    return jnp.pad(flat, (0, padded - flat.shape[0])).reshape(-1, PACK_LANES)


def _unpack(packed, shapes):
    flat = packed.reshape(-1)
    out, at = [], 0
    for shp in shapes:
        size = math.prod(shp)
        out.append(flat[at:at + size].reshape(shp))
        at += size
    return out


def kernel(x, ffn1_norm, ffn1_w_gate, ffn1_w_up, ffn1_w_down, mix_norm, w_in, conv_w, conv_b, rg_w_a, rg_b_a, rg_w_x, rg_b_x, rg_lambda, q_norm, k_norm, rnn_out_norm, attn_out_norm, w_out, ffn2_norm, ffn2_w_gate, ffn2_w_up, ffn2_w_down, loss_target, m_ffn1_norm, m_ffn1_w_gate, m_ffn1_w_up, m_ffn1_w_down, m_mix_norm, m_w_in, m_conv_w, m_conv_b, m_rg_w_a, m_rg_b_a, m_rg_w_x, m_rg_b_x, m_rg_lambda, m_q_norm, m_k_norm, m_rnn_out_norm, m_attn_out_norm, m_w_out, m_ffn2_norm, m_ffn2_w_gate, m_ffn2_w_up, m_ffn2_w_down, v_ffn1_norm, v_ffn1_w_gate, v_ffn1_w_up, v_ffn1_w_down, v_mix_norm, v_w_in, v_conv_w, v_conv_b, v_rg_w_a, v_rg_b_a, v_rg_w_x, v_rg_b_x, v_rg_lambda, v_q_norm, v_k_norm, v_rnn_out_norm, v_attn_out_norm, v_w_out, v_ffn2_norm, v_ffn2_w_gate, v_ffn2_w_up, v_ffn2_w_down):
    given = dict(locals())
    w = {n: given[n] for n in WEIGHTS}
    m = {n: given["m_" + n] for n in WEIGHTS}
    v = {n: given["v_" + n] for n in WEIGHTS}
    chip = 2 * lax.axis_index("x") + lax.axis_index("y")

    where = jnp.stack([lax.axis_index("c"), chip]).astype(jnp.int32)

    stacks = {n: _place_shard(_hidden_major(n, w[n][0]), where, BF16, "place_" + n) for n in BIG}
    conv_stack = _place_shard(w["conv_w"][0], where, F32, "place_conv_w")
    small = {n: (w[n][0] if w[n].ndim > 2 else w[n]) for n in SMALL if n != "conv_w"}

    loss, grad_x, slots, gs = _local_step(x[0], loss_target[0], stacks, conv_stack, small, where)
    loss = lax.psum(loss, ("x", "y", "c"))

    swapped = _half_swap([_chip_sum(slots[n], where, "chip_sum_" + n) for n in BIG])
    grads, deltas, new_m, new_v = {}, {}, {}, {}
    for n, t in zip(BIG, swapped):
        g2 = t.reshape(t.shape[0] * t.shape[1], t.shape[2])
        d2, m2, v2 = _adamw(_hidden_major(n, w[n][0]), g2, _hidden_major(n, m[n][0]), _hidden_major(n, v[n][0]),
                            "adamw_" + n)
        back = lambda a: _hidden_major(n, a).reshape(w[n].shape)
        grads[n], deltas[n], new_m[n], new_v[n] = back(g2), back(d2), back(m2), back(v2)

    full_shapes = [gs[n].shape for n in SMALL]
    everyone = _gather_small(_pack([gs[n] for n in SMALL]))
    g_small = _slot_sum(everyone, "small_grad_sum")
    g_parts = dict(zip(SMALL, _unpack(g_small, full_shapes)))
    quarter = D_RNN // N_CHIPS
    g_parts["conv_w"] = lax.dynamic_slice_in_dim(g_parts["conv_w"], chip * quarter, quarter, axis=1)
    local_shapes = [w[n].shape for n in SMALL]
    pk = lambda tree: _pack([tree[n] for n in SMALL])
    d_s, m_s, v_s = _adamw(pk(w), pk(g_parts), pk(m), pk(v), "adamw_small")
    for tree, packed in ((grads, pk(g_parts)), (deltas, d_s), (new_m, m_s), (new_v, v_s)):
        tree.update(zip(SMALL, _unpack(packed, local_shapes)))

    return (loss, grad_x.reshape(x.shape), *[grads[n] for n in WEIGHTS], *[deltas[n] for n in WEIGHTS],
            *[new_m[n] for n in WEIGHTS], *[new_v[n] for n in WEIGHTS])
```

```python
import functools
import math

import jax
import jax.numpy as jnp
from jax import lax
from jax.experimental import pallas as pl
from jax.experimental.pallas import tpu as pltpu

F32 = jnp.float32
BF16 = jnp.bfloat16
MESH = pl.DeviceIdType.MESH

D_MODEL = 1024
N_CHIPS = 4
D_RNN = 512
D_ATT = 512
N_HEADS = 8
HEAD_DIM = 64
RNN_BLOCKS = 8
CONV_W = 4
RG_C = 8.0
N_IN = 2 * D_RNN + 3 * D_ATT
EPS = 1e-6
ATT_BLOCK = 128
ATT_WINDOW = 384
ATT_SPLIT = 256
EXP_ZERO = -105.0

ADAM_LR = 0.001
ADAM_B1 = 0.9
ADAM_B2 = 0.999
ADAM_EPS = 1e-08
ADAM_WD = 0.01
ADAM_STEP = 10

V7X_VMEM_LIMIT = 56 * 1024 * 1024
TOKEN_TILE = 512
FFN_TILE = 256
WGRAD_TILE = 2048
WHOLE_TILE = 1024

GELU_K0 = math.sqrt(2.0 / math.pi)
GELU_K1 = 0.044715


def _params(sem=None):
    return pltpu.CompilerParams(dimension_semantics=sem, vmem_limit_bytes=V7X_VMEM_LIMIT)


def _dot(a, b):
    return jnp.dot(a, b, preferred_element_type=F32)


def _dot_nt(a, b):
    return lax.dot_general(a, b, (((1,), (1,)), ((), ())), preferred_element_type=F32)


def _dot_tn(a, b):
    return lax.dot_general(a, b, (((0,), (0,)), ((), ())), preferred_element_type=F32)


def _sigmoid(x):
    return 1.0 / (1.0 + jnp.exp(-x))


def _rms_r(xv):
    return lax.rsqrt(jnp.mean(xv * xv, axis=-1, keepdims=True) + EPS)


def _rms_bwd(xv, r, nw, dh):
    t = dh * nw
    dx = r * t - xv * (r * r * r * jnp.mean(t * xv, axis=-1, keepdims=True))
    dn = jnp.sum(dh * xv * r, axis=0, keepdims=True)
    return dx, dn


def _gelu(x):
    t = jnp.tanh(GELU_K0 * (x + GELU_K1 * x * x * x))
    return 0.5 * x * (1.0 + t)


def _gelu_grad(x):
    t = jnp.tanh(GELU_K0 * (x + GELU_K1 * x * x * x))
    return 0.5 * (1.0 + t) + 0.5 * x * (1.0 - t * t) * (GELU_K0 * (1.0 + 3.0 * GELU_K1 * x * x))


def _expm1_neg(x):
    p = 1.0 + x * (1.0 / 8.0)
    for k in (7.0, 6.0, 5.0, 4.0, 3.0, 2.0):
        p = 1.0 + x * (1.0 / k) * p
    return jnp.where(x > -0.25, x * p, jnp.exp(x) - 1.0)


def _log_sigmoid(x):
    return jnp.minimum(x, 0.0) - jnp.log(1.0 + jnp.exp(-jnp.abs(x)))


def _tile(s):
    return min(TOKEN_TILE, s)


def _ffn_fwd(x, nw, wg, wu, wd, tgt=None, rider=None):
    s, d = x.shape
    nb, fb, _ = wg.shape
    tm = min(FFN_TILE, s)
    ni = s // tm
    assert s % tm == 0
    with_loss = tgt is not None
    n_in, n_out = 5 + with_loss, 5 + with_loss

    def body(*refs):
        ins, outs, _, copies = _split_refs(refs, n_in, n_out, rider)
        x_ref, nw_ref, wg_ref, wu_ref, wd_ref = ins[:5]
        out_ref, g_ref, u_ref, hb_ref, ab_ref = outs[:5]
        i = pl.program_id(0)
        finish = _ride(copies, i == 0, i == ni - 1)

        xv = x_ref[...]
        hb = (xv * _rms_r(xv) * nw_ref[...]).astype(BF16)
        hb_ref[...] = hb
        y = jnp.zeros((tm, d), F32)
        for jb in range(nb):
            g = _dot_nt(hb, wg_ref[jb])
            u = _dot_nt(hb, wu_ref[jb])
            g_ref[jb] = g.astype(BF16)
            u_ref[jb] = u.astype(BF16)
            ab = (g * _sigmoid(g) * u).astype(BF16)
            ab_ref[jb] = ab
            y = y + _dot(ab, wd_ref[jb])
        y = xv + 0.5 * y
        if with_loss:
            tgt_ref, loss_ref = ins[5], outs[5]
            diff = y - tgt_ref[...]
            out_ref[...] = diff * (1.0 / d)

            @pl.when(i == 0)
            def _():
                loss_ref[...] = jnp.zeros_like(loss_ref)

            loss_ref[...] += jnp.sum(diff * diff) * (0.5 / d)
        else:
            out_ref[...] = y
        finish()

    row = pl.BlockSpec((tm, d), lambda i: (i, 0))
    weight = pl.BlockSpec((nb, fb, d), lambda i: (0, 0, 0), pipeline_mode=pl.Buffered(1))
    in_specs = [row, pl.BlockSpec((1, d), lambda i: (0, 0)), weight, weight, weight]
    args = [x, nw, wg, wu, wd]
    if with_loss:
        in_specs.append(row)
        args.append(tgt)
    blk = pl.BlockSpec((nb, tm, fb), lambda i: (0, i, 0))
    out_shape = [jax.ShapeDtypeStruct((s, d), F32), jax.ShapeDtypeStruct((nb, s, fb), BF16),
                 jax.ShapeDtypeStruct((nb, s, fb), BF16), jax.ShapeDtypeStruct((s, d), BF16),
                 jax.ShapeDtypeStruct((nb, s, fb), BF16)]
    out_specs = [row, blk, blk, row, blk]
    if with_loss:
        out_shape.append(jax.ShapeDtypeStruct((1, 128), F32))
        out_specs.append(pl.BlockSpec((1, 128), lambda i: (0, 0)))
    return _call(body, "ffn_fwd_loss" if with_loss else "ffn_fwd", (ni,), in_specs, out_specs, out_shape, args,
                 rider=rider)


def _call(body, name, grid, in_specs, out_specs, out_shape, args, scratch=(), rider=None):
    in_specs, out_specs, out_shape, scratch = list(in_specs), list(out_specs), list(out_shape), list(scratch)
    extra, aliases = [], {}
    if rider is not None:
        extra = rider.operands()
        aliases = rider.aliases(len(args), len(out_shape))
        in_specs += [ANY] * len(extra)
        out_specs += [ANY] * len(rider.inplace)
        out_shape += rider.out_shape()
        scratch += rider.scratch()
    return pl.pallas_call(
        body, name=name, grid=grid, in_specs=in_specs, out_specs=out_specs, out_shape=out_shape,
        input_output_aliases=aliases, scratch_shapes=scratch,
        compiler_params=_params(("arbitrary",) * len(grid)),
    )(*args, *extra)


def _ffn_bwd_act(x, nw, dy, g, u, wg, wu, wd, name, rider=None):
    s, d = x.shape
    nb, fb, _ = wg.shape
    tm = min(FFN_TILE, s)
    assert s % tm == 0

    def body(*refs):
        ins, outs, _, copies = _split_refs(refs, 8, 5, rider)
        x_ref, nw_ref, dy_ref, g_ref, u_ref, wg_ref, wu_ref, wd_ref = ins
        dx_ref, dg_ref, du_ref, dyb_ref, dnw_ref = outs
        finish = _ride(copies, pl.program_id(0) == 0, pl.program_id(0) == s // tm - 1)
        dyv = dy_ref[...]
        dyb = dyv.astype(BF16)
        dyb_ref[...] = dyb
        dh = jnp.zeros((tm, d), F32)
        for jb in range(nb):
            da = 0.5 * _dot_nt(dyb, wd_ref[jb])
            gv = g_ref[jb].astype(F32)
            sg = _sigmoid(gv)
            dub = (da * (gv * sg)).astype(BF16)
            dgb = (da * u_ref[jb].astype(F32) * (sg * (1.0 + gv * (1.0 - sg)))).astype(BF16)
            dg_ref[jb] = dgb
            du_ref[jb] = dub
            dh = dh + _dot(dgb, wg_ref[jb]) + _dot(dub, wu_ref[jb])
        xv = x_ref[...]
        dx, dn = _rms_bwd(xv, _rms_r(xv), nw_ref[...], dh)
        dx_ref[...] = dyv + dx

        @pl.when(pl.program_id(0) == 0)
        def _():
            dnw_ref[...] = jnp.zeros_like(dnw_ref)

        dnw_ref[...] += dn
        finish()

    row = pl.BlockSpec((tm, d), lambda i: (i, 0))
    vec = pl.BlockSpec((1, d), lambda i: (0, 0))
    blk = pl.BlockSpec((nb, tm, fb), lambda i: (0, i, 0))
    weight = pl.BlockSpec((nb, fb, d), lambda i: (0, 0, 0), pipeline_mode=pl.Buffered(1))
    return _call(
        body, name, (s // tm,), [row, vec, row, blk, blk, weight, weight, weight], [row, blk, blk, row, vec],
        [jax.ShapeDtypeStruct((s, d), F32), jax.ShapeDtypeStruct((nb, s, fb), BF16),
         jax.ShapeDtypeStruct((nb, s, fb), BF16), jax.ShapeDtypeStruct((s, d), BF16),
         jax.ShapeDtypeStruct((1, d), F32)],
        [x, nw, dy, g, u, wg, wu, wd], rider=rider)


def _wgrad(a, b, a_spec, b_spec, out_rows, out_cols, scale, name, tk, rider=None):
    s = a.shape[-2]
    nk = s // tk
    assert s % tk == 0

    def body(*refs):
        (a_ref, b_ref), (out_ref,), (acc,), copies = _split_refs(refs, 2, 1, rider)
        j, k = pl.program_id(0), pl.program_id(1)
        finish = _ride(copies, jnp.logical_and(j == 0, k == 0), jnp.logical_and(j == N_CHIPS - 1, k == nk - 1))

        @pl.when(k == 0)
        def _():
            acc[...] = jnp.zeros_like(acc)

        acc[...] += _dot_tn(a_ref[...], b_ref[...])

        @pl.when(k == nk - 1)
        def _():
            out_ref[...] = (acc[...] * scale).astype(BF16)

        finish()

    outs = _call(
        body, name, (N_CHIPS, nk), [a_spec(tk), b_spec(tk)],
        [pl.BlockSpec((None, out_rows, out_cols), lambda j, k: (j, 0, 0))],
        [jax.ShapeDtypeStruct((N_CHIPS, out_rows, out_cols), BF16)], [a, b],
        scratch=[pltpu.VMEM((out_rows, out_cols), F32)], rider=rider)
    return outs[0] if rider is None else outs


def _wgrad_whole(a, b, col_blocks, name):
    s, m = a.shape
    n = b.shape[1]
    tk = min(WHOLE_TILE, s)
    nk = s // tk
    assert s % tk == 0
    out_shape = (N_CHIPS, m, n // N_CHIPS) if col_blocks else (N_CHIPS, m // N_CHIPS, n)

    def body(a_ref, b_ref, out_ref, acc):
        k = pl.program_id(0)

        @pl.when(k == 0)
        def _():
            acc[...] = jnp.zeros_like(acc)

        acc[...] += _dot_tn(a_ref[...], b_ref[...])

        @pl.when(k == nk - 1)
        def _():
            for j in range(N_CHIPS):
                if col_blocks:
                    out_ref[j] = acc[:, j * out_shape[2]:(j + 1) * out_shape[2]].astype(BF16)
                else:
                    out_ref[j] = acc[j * out_shape[1]:(j + 1) * out_shape[1], :].astype(BF16)

    return pl.pallas_call(
        body, name=name, grid=(nk,),
        in_specs=[pl.BlockSpec((tk, m), lambda k: (k, 0)), pl.BlockSpec((tk, n), lambda k: (k, 0))],
        out_specs=pl.BlockSpec(out_shape, lambda k: (0, 0, 0)),
        out_shape=jax.ShapeDtypeStruct(out_shape, BF16),
        scratch_shapes=[pltpu.VMEM((m, n), F32)],
        compiler_params=_params(("arbitrary",)),
    )(a, b)


def _ffn_wgrad(stack, shared, scale, name, rider=None):
    s, d = shared.shape
    fb = stack.shape[-1]
    return _wgrad(stack, shared, lambda tk: pl.BlockSpec((None, tk, fb), lambda j, k: (j, k, 0)),
                  lambda tk: pl.BlockSpec((tk, d), lambda j, k: (k, 0)), fb, d, scale, name,
                  min(WGRAD_TILE, s), rider)


def _mix_pre(x, nw, win):
    s, d = x.shape
    nb, _, cb = win.shape
    tm = min(FFN_TILE, s)
    assert s % tm == 0

    def body(x_ref, nw_ref, w_ref, p_ref, hb_ref):
        xv = x_ref[...]
        hb = (xv * _rms_r(xv) * nw_ref[...]).astype(BF16)
        hb_ref[...] = hb
        for j in range(nb):
            p_ref[:, j * cb:(j + 1) * cb] = _dot(hb, w_ref[j])

    row = pl.BlockSpec((tm, d), lambda i: (i, 0))
    return pl.pallas_call(
        body, name="mix_pre", grid=(s // tm,),
        in_specs=[row, pl.BlockSpec((1, d), lambda i: (0, 0)),
                  pl.BlockSpec((nb, d, cb), lambda i: (0, 0, 0), pipeline_mode=pl.Buffered(1))],
        out_specs=[pl.BlockSpec((tm, nb * cb), lambda i: (i, 0)), row],
        out_shape=[jax.ShapeDtypeStruct((s, nb * cb), F32), jax.ShapeDtypeStruct((s, d), BF16)],
        compiler_params=_params(("arbitrary",)),
    )(x, nw, win)


def _mix_pre_bwd(x, nw, dres, dpb, win):
    s, d = x.shape
    nb, _, cb = win.shape
    tm = min(FFN_TILE, s)
    assert s % tm == 0

    def body(x_ref, nw_ref, dres_ref, dp_ref, w_ref, dx_ref, dnw_ref):
        dh = jnp.zeros((tm, d), F32)
        for j in range(nb):
            dh = dh + _dot_nt(dp_ref[:, j * cb:(j + 1) * cb], w_ref[j])
        xv = x_ref[...]
        dx, dn = _rms_bwd(xv, _rms_r(xv), nw_ref[...], dh)
        dx_ref[...] = dres_ref[...] + dx

        @pl.when(pl.program_id(0) == 0)
        def _():
            dnw_ref[...] = jnp.zeros_like(dnw_ref)

        dnw_ref[...] += dn

    row = pl.BlockSpec((tm, d), lambda i: (i, 0))
    vec = pl.BlockSpec((1, d), lambda i: (0, 0))
    return pl.pallas_call(
        body, name="mix_pre_bwd", grid=(s // tm,),
        in_specs=[row, vec, row, pl.BlockSpec((tm, nb * cb), lambda i: (i, 0)),
                  pl.BlockSpec((nb, d, cb), lambda i: (0, 0, 0), pipeline_mode=pl.Buffered(1))],
        out_specs=[row, vec],
        out_shape=[jax.ShapeDtypeStruct((s, d), F32), jax.ShapeDtypeStruct((1, d), F32)],
        compiler_params=_params(("arbitrary",)),
    )(x, nw, dres, dpb, win)


def _mix_post(x, yr, ya, nr, na, wout):
    s, d = x.shape
    h = yr.shape[1]
    tm = _tile(s)

    def body(x_ref, yr_ref, ya_ref, nr_ref, na_ref, w_ref, out_ref):
        yrv = yr_ref[...]
        yav = ya_ref[...]
        onb = (yrv * _rms_r(yrv) * nr_ref[...]).astype(BF16)
        oab = (yav * _rms_r(yav) * na_ref[...]).astype(BF16)
        out_ref[...] = x_ref[...] + _dot(onb, w_ref[0:h, :]) + _dot(oab, w_ref[h:2 * h, :])

    row = pl.BlockSpec((tm, d), lambda i: (i, 0))
    half = pl.BlockSpec((tm, h), lambda i: (i, 0))
    vec = pl.BlockSpec((1, h), lambda i: (0, 0))
    return pl.pallas_call(
        body, name="mix_post", grid=(s // tm,),
        in_specs=[row, half, half, vec, vec, pl.BlockSpec((2 * h, d), lambda i: (0, 0))],
        out_specs=row, out_shape=jax.ShapeDtypeStruct((s, d), F32),
        compiler_params=_params(("arbitrary",)),
    )(x, yr, ya, nr, na, wout)


def _mix_post_bwd(dx, yr, ya, nr, na, wout):
    s, d = dx.shape
    h = yr.shape[1]
    tm = _tile(s)

    def body(dx_ref, yr_ref, ya_ref, nr_ref, na_ref, w_ref,
             dyr_ref, dya_ref, yc_ref, dxb_ref, dnr_ref, dna_ref):
        i = pl.program_id(0)
        dxb = dx_ref[...].astype(BF16)
        dxb_ref[...] = dxb
        dyc = _dot_nt(dxb, w_ref[...])
        yrv = yr_ref[...]
        yav = ya_ref[...]
        rr = _rms_r(yrv)
        ra = _rms_r(yav)
        yc_ref[:, 0:h] = (yrv * rr * nr_ref[...]).astype(BF16)
        yc_ref[:, h:2 * h] = (yav * ra * na_ref[...]).astype(BF16)
        dyr, dnr = _rms_bwd(yrv, rr, nr_ref[...], dyc[:, 0:h])
        dya, dna = _rms_bwd(yav, ra, na_ref[...], dyc[:, h:2 * h])
        dyr_ref[...] = dyr
        dya_ref[...] = dya

        @pl.when(i == 0)
        def _():
            dnr_ref[...] = jnp.zeros_like(dnr_ref)
            dna_ref[...] = jnp.zeros_like(dna_ref)

        dnr_ref[...] += dnr
        dna_ref[...] += dna

    row = pl.BlockSpec((tm, d), lambda i: (i, 0))
    half = pl.BlockSpec((tm, h), lambda i: (i, 0))
    vec = pl.BlockSpec((1, h), lambda i: (0, 0))
    return pl.pallas_call(
        body, name="mix_post_bwd", grid=(s // tm,),
        in_specs=[row, half, half, vec, vec, pl.BlockSpec((2 * h, d), lambda i: (0, 0))],
        out_specs=[half, half, pl.BlockSpec((tm, 2 * h), lambda i: (i, 0)), row, vec, vec],
        out_shape=[jax.ShapeDtypeStruct((s, h), F32), jax.ShapeDtypeStruct((s, h), F32),
                   jax.ShapeDtypeStruct((s, 2 * h), BF16), jax.ShapeDtypeStruct((s, d), BF16),
                   jax.ShapeDtypeStruct((1, h), F32), jax.ShapeDtypeStruct((1, h), F32)],
        compiler_params=_params(("arbitrary",)),
    )(dx, yr, ya, nr, na, wout)


def _shift_down(xv, s, prev8):
    rolled = pltpu.roll(xv, s, 0)
    row8 = lax.broadcasted_iota(jnp.int32, prev8.shape, 0)
    head = jnp.where(row8 < s, pltpu.roll(prev8, s, 0), rolled[0:8, :])
    return jnp.concatenate([head, rolled[8:, :]], axis=0)


def _shift_up(xv, s, next8):
    n = xv.shape[0]
    rolled = pltpu.roll(xv, n - s, 0)
    row8 = lax.broadcasted_iota(jnp.int32, next8.shape, 0)
    tail = jnp.where(row8 >= 8 - s, pltpu.roll(next8, 8 - s, 0), rolled[n - 8:, :])
    return jnp.concatenate([rolled[:n - 8, :], tail], axis=0)


def _scan_fwd(a, b):
    n = a.shape[0]
    row = lax.broadcasted_iota(jnp.int32, a.shape, 0)
    s = 1
    while s < n:
        ok = row >= s
        b = jnp.where(ok, a * pltpu.roll(b, s, 0) + b, b)
        a = jnp.where(ok, a * pltpu.roll(a, s, 0), a)
        s *= 2
    return b


def _scan_bwd(a, b):
    n = a.shape[0]
    row = lax.broadcasted_iota(jnp.int32, a.shape, 0)
    s = 1
    while s < n:
        ok = row < n - s
        b = jnp.where(ok, a * pltpu.roll(b, n - s, 0) + b, b)
        a = jnp.where(ok, a * pltpu.roll(a, n - s, 0), a)
        s *= 2
    return b


def _rglru_gates(xv, prev8, cw_ref, cb_ref, wa_ref, ba_ref, wx_ref, bx_ref, lam_ref):
    x1 = _shift_down(xv, 1, prev8)
    x2 = _shift_down(xv, 2, prev8)
    x3 = _shift_down(xv, 3, prev8)
    xc = cw_ref[3:4, :] * xv + cw_ref[2:3, :] * x1 + cw_ref[1:2, :] * x2 + cw_ref[0:1, :] * x3 + cb_ref[...]
    xcb = xc.astype(BF16)
    r = _sigmoid(_dot(xcb, wa_ref[...]) + ba_ref[...])
    ig = _sigmoid(_dot(xcb, wx_ref[...]) + bx_ref[...])
    c = RG_C * _log_sigmoid(lam_ref[...])
    la = r * c
    a = jnp.exp(la)
    m = jnp.sqrt(-_expm1_neg(2.0 * la))
    return (x1, x2, x3), xc, xcb, r, ig, c, a, m


def _rglru_fwd(proj, cw, cb, wa, ba, wx, bx, lam):
    s = proj.shape[0]
    w = D_RNN
    tm = _tile(s)

    def body(xr_ref, gate_ref, cw_ref, cb_ref, wa_ref, ba_ref, wx_ref, bx_ref, lam_ref,
             y_ref, h_ref, prev, hlast):
        @pl.when(pl.program_id(0) == 0)
        def _():
            prev[...] = jnp.zeros_like(prev)
            hlast[...] = jnp.zeros_like(hlast)

        xv = xr_ref[...]
        _, xc, _, _, ig, _, a, m = _rglru_gates(xv, prev[...], cw_ref, cb_ref, wa_ref, ba_ref,
                                                wx_ref, bx_ref, lam_ref)
        b = m * (ig * xc)
        row = lax.broadcasted_iota(jnp.int32, b.shape, 0)
        b = jnp.where(row == 0, b + a * hlast[...], b)
        h = _scan_fwd(a, b)
        h_ref[...] = h
        y_ref[...] = h * _gelu(gate_ref[...])
        prev[...] = xv[tm - 8:, :]
        hlast[...] = h[tm - 1:tm, :]

    vec = pl.BlockSpec((1, w), lambda i: (0, 0))
    sq = pl.BlockSpec((w, w), lambda i: (0, 0))
    out = pl.BlockSpec((tm, w), lambda i: (i, 0))
    return pl.pallas_call(
        body, name="rglru_fwd", grid=(s // tm,),
        in_specs=[pl.BlockSpec((tm, w), lambda i: (i, 0)), pl.BlockSpec((tm, w), lambda i: (i, 1)),
                  pl.BlockSpec((CONV_W, w), lambda i: (0, 0)), vec, sq, vec, sq, vec, vec],
        out_specs=[out, out],
        out_shape=[jax.ShapeDtypeStruct((s, w), F32), jax.ShapeDtypeStruct((s, w), F32)],
        scratch_shapes=[pltpu.VMEM((8, w), F32), pltpu.VMEM((1, w), F32)],
        compiler_params=_params(("arbitrary",)),
    )(proj, proj, cw, cb, wa, ba, wx, bx, lam)


def _rglru_bwd(proj, hseq, dyr, cw, cb, wa, ba, wx, bx, lam):
    s = proj.shape[0]
    w = D_RNN
    tm = _tile(s)
    nt = s // tm
    t8 = tm // 8

    def body(xr_ref, xp_ref, gate_ref, h_ref, hp_ref, dy_ref, cw_ref, cb_ref, wa_ref, ba_ref,
             wx_ref, bx_ref, lam_ref,
             dxr_ref, dgate_ref, dcw_ref, dcb_ref, dwa_ref, dba_ref, dwx_ref, dbx_ref, dlam_ref,
             carry, dxc_next):
        i = pl.program_id(0)
        first_tile = i == nt - 1

        @pl.when(i == 0)
        def _():
            carry[...] = jnp.zeros_like(carry)
            dxc_next[...] = jnp.zeros_like(dxc_next)
            for ref in (dcw_ref, dcb_ref, dwa_ref, dba_ref, dwx_ref, dbx_ref, dlam_ref):
                ref[...] = jnp.zeros_like(ref)

        xv = xr_ref[...]
        prev8 = jnp.where(first_tile, 0.0, xp_ref[...])
        hprev8 = jnp.where(first_tile, 0.0, hp_ref[...])
        (x1, x2, x3), xc, xcb, r, ig, c, a, m = _rglru_gates(
            xv, prev8, cw_ref, cb_ref, wa_ref, ba_ref, wx_ref, bx_ref, lam_ref)
        gv = gate_ref[...]
        hv = h_ref[...]
        dy = dy_ref[...]
        dgate_ref[...] = (dy * hv * _gelu_grad(gv)).astype(BF16)
        dh = dy * _gelu(gv)
        row = lax.broadcasted_iota(jnp.int32, dh.shape, 0)
        dh = jnp.where(row == tm - 1, dh + carry[...], dh)
        a_up = jnp.where(row == tm - 1, 0.0, pltpu.roll(a, tm - 1, 0))
        lam_t = _scan_bwd(a_up, dh)
        carry[...] = a[0:1, :] * lam_t[0:1, :]
        hm1 = _shift_down(hv, 1, hprev8)
        da = lam_t * hm1
        ixc = ig * xc
        dm = lam_t * ixc
        dig = lam_t * m * xc
        dxc = lam_t * m * ig
        dla = da * a - dm * (a * a) / m
        dr = dla * c
        dlam_ref[...] += jnp.sum(dla * r, axis=0, keepdims=True)
        dpa = dr * r * (1.0 - r)
        dpi = dig * ig * (1.0 - ig)
        dba_ref[...] += jnp.sum(dpa, axis=0, keepdims=True)
        dbx_ref[...] += jnp.sum(dpi, axis=0, keepdims=True)
        dpab = dpa.astype(BF16)
        dpib = dpi.astype(BF16)
        dwa_ref[...] += _dot_tn(xcb, dpab)
        dwx_ref[...] += _dot_tn(xcb, dpib)
        dxc = dxc + _dot_nt(dpab, wa_ref[...]) + _dot_nt(dpib, wx_ref[...])
        dcb_ref[...] += jnp.sum(dxc, axis=0, keepdims=True)
        dcw_ref[3:4, :] += jnp.sum(dxc * xv, axis=0, keepdims=True)
        dcw_ref[2:3, :] += jnp.sum(dxc * x1, axis=0, keepdims=True)
        dcw_ref[1:2, :] += jnp.sum(dxc * x2, axis=0, keepdims=True)
        dcw_ref[0:1, :] += jnp.sum(dxc * x3, axis=0, keepdims=True)
        nxt = dxc_next[...]
        dxr = (cw_ref[3:4, :] * dxc + cw_ref[2:3, :] * _shift_up(dxc, 1, nxt)
               + cw_ref[1:2, :] * _shift_up(dxc, 2, nxt) + cw_ref[0:1, :] * _shift_up(dxc, 3, nxt))
        dxr_ref[...] = dxr.astype(BF16)
        dxc_next[...] = dxc[0:8, :]

        @pl.when(first_tile)
        def _():
            lv = lam_ref[...]
            dlam_ref[...] = dlam_ref[...] * (RG_C * _sigmoid(-lv))

    rev = lambda i: nt - 1 - i
    vec = pl.BlockSpec((1, w), lambda i: (0, 0))
    sq = pl.BlockSpec((w, w), lambda i: (0, 0))
    cur = lambda col: pl.BlockSpec((tm, w), lambda i: (rev(i), col))
    before = lambda cols: pl.BlockSpec((8, w), lambda i: (jnp.maximum(rev(i) * t8 - 1, 0), 0))
    return pl.pallas_call(
        body, name="rglru_bwd", grid=(nt,),
        in_specs=[cur(0), before(None), cur(1), cur(0), before(None), cur(0),
                  pl.BlockSpec((CONV_W, w), lambda i: (0, 0)), vec, sq, vec, sq, vec, vec],
        out_specs=[cur(0), cur(0), pl.BlockSpec((CONV_W, w), lambda i: (0, 0)), vec, sq, vec, sq, vec, vec],
        out_shape=[jax.ShapeDtypeStruct((s, w), BF16), jax.ShapeDtypeStruct((s, w), BF16),
                   jax.ShapeDtypeStruct((CONV_W, w), F32), jax.ShapeDtypeStruct((1, w), F32),
                   jax.ShapeDtypeStruct((w, w), F32), jax.ShapeDtypeStruct((1, w), F32),
                   jax.ShapeDtypeStruct((w, w), F32), jax.ShapeDtypeStruct((1, w), F32),
                   jax.ShapeDtypeStruct((1, w), F32)],
        scratch_shapes=[pltpu.VMEM((1, w), F32), pltpu.VMEM((8, w), F32)],
        compiler_params=_params(("arbitrary",)),
    )(proj, proj, proj, hseq, hseq, dyr, cw, cb, wa, ba, wx, bx, lam)


def _sb_logs(z, valid):
    l1p = jnp.log(1.0 + jnp.exp(-jnp.abs(z)))
    lb = jnp.minimum(z, 0.0) - l1p
    lm = jnp.where(valid, -jnp.maximum(z, 0.0) - l1p, 0.0)
    return lb, lm


class _Window:
    def __init__(self):
        blk, win, cut = ATT_BLOCK, ATT_WINDOW, ATT_SPLIT
        self.row = lax.broadcasted_iota(jnp.int32, (blk, win), 0)
        self.col = lax.broadcasted_iota(jnp.int32, (blk, win), 1)

        def tri(n, later):
            j = lax.broadcasted_iota(jnp.int32, (n, n), 0)
            s = lax.broadcasted_iota(jnp.int32, (n, n), 1)
            return jnp.where((j > s) if later else (j < s), 1.0, 0.0).astype(BF16)

        self.later = (tri(cut, True), tri(win - cut, True))
        self.earlier = (tri(cut, False), tri(win - cut, False))

    def place(self, qi, g):
        end = (qi + 1) * ATT_BLOCK - g * ATT_WINDOW
        start = pl.multiple_of(jnp.maximum(end - ATT_WINDOW, 0), ATT_BLOCK)
        valid = start + self.col < jnp.minimum(qi * ATT_BLOCK + self.row, end)
        return start, valid

    @staticmethod
    def _parts(xv):
        hi = xv.astype(BF16)
        lo = (xv - hi.astype(F32)).astype(BF16)
        cut = ATT_SPLIT
        sums = (jnp.sum(xv[:, :cut], axis=1, keepdims=True), jnp.sum(xv[:, cut:], axis=1, keepdims=True))
        return (hi[:, :cut], lo[:, :cut]), (hi[:, cut:], lo[:, cut:]), sums

    def sums_after(self, xv, carry):
        (h0, l0), (h1, l1), (s0, s1) = self._parts(xv)
        first = _dot(h0, self.later[0]) + _dot(l0, self.later[0]) + (s1 + carry)
        last = _dot(h1, self.later[1]) + _dot(l1, self.later[1]) + carry
        return jnp.concatenate([first, last], axis=1), s0 + s1

    def sums_before(self, xv, carry):
        (h0, l0), (h1, l1), (s0, s1) = self._parts(xv)
        first = _dot(h0, self.earlier[0]) + _dot(l0, self.earlier[0]) + carry
        last = _dot(h1, self.earlier[1]) + _dot(l1, self.earlier[1]) + (s0 + carry)
        return jnp.concatenate([first, last], axis=1), s0 + s1


def _head_lanes(hh):
    return slice(hh * HEAD_DIM, (hh + 1) * HEAD_DIM)


def _attn_fwd(proj, qg, kg, rider=None):
    s = proj.shape[0]
    blk, win, dh = ATT_BLOCK, ATT_WINDOW, HEAD_DIM
    nq = s // blk
    scale = 1.0 / math.sqrt(dh)
    heads = (0, 1)
    assert s >= win and s % blk == 0

    def body(*refs):
        (q_ref, k_ref, v_ref, qg_ref, kg_ref), (o_ref,), (qn, kn, vb, ob), copies = _split_refs(refs, 5, 1, rider)
        finish = _ride(copies, pl.program_id(0) == 0, pl.program_id(0) == N_HEADS // 2 - 1)
        wd = _Window()
        for h in heads:
            lanes = _head_lanes(h)
            qv = q_ref[:, lanes]
            qn[h] = (qv * _rms_r(qv) * qg_ref[...] * scale).astype(BF16)
            kv = k_ref[:, lanes]
            kn[h] = (kv * _rms_r(kv) * kg_ref[...]).astype(BF16)
            vb[h] = v_ref[:, lanes].astype(BF16)

        def q_step(qi, _):
            qoff = pl.multiple_of(qi * blk, blk)
            qts = [qn[h, pl.ds(qoff, blk), :] for h in heads]

            def more(carry):
                g, live = carry[:2]
                return jnp.logical_and((qi + 1) * blk - g * win > 0, live > 0)

            def window(carry):
                g, _, accs, runs = carry
                start, valid = wd.place(qi, g)
                zs = [_dot_nt(qts[h], kn[h, pl.ds(start, win), :]) for h in heads]
                logs = [_sb_logs(z, valid) for z in zs]
                sums = [wd.sums_after(logs[h][1], runs[h]) for h in heads]
                wgts = [jnp.where(valid, jnp.exp(logs[h][0] + sums[h][0]), 0.0).astype(BF16) for h in heads]
                accs = tuple(accs[h] + _dot(wgts[h], vb[h, pl.ds(start, win), :]) for h in heads)
                runs = tuple(runs[h] + sums[h][1] for h in heads)
                live = (jnp.maximum(jnp.max(runs[0]), jnp.max(runs[1])) > EXP_ZERO).astype(jnp.int32)
                return g + 1, live, accs, runs

            zero = lambda cols: tuple(jnp.zeros((blk, cols), F32) for _ in heads)
            _, _, accs, _ = lax.while_loop(more, window, (jnp.int32(0), jnp.int32(1), zero(dh), zero(1)))
            for h in heads:
                ob[h, pl.ds(qoff, blk), :] = accs[h]
            return 0

        lax.fori_loop(0, nq, q_step, 0)
        for h in heads:
            o_ref[:, _head_lanes(h)] = ob[h]
        finish()

    pair = lambda group: pl.BlockSpec((s, 2 * dh), lambda p: (0, group * (D_ATT // (2 * dh)) + p))
    vec = pl.BlockSpec((1, dh), lambda p: (0, 0))
    return _call(
        body, "attn_fwd", (N_HEADS // 2,), [pair(2), pair(3), pair(4), vec, vec], [pair(0)],
        [jax.ShapeDtypeStruct((s, D_ATT), F32)], [proj, proj, proj, qg, kg],
        scratch=[pltpu.VMEM((2, s, dh), BF16)] * 3 + [pltpu.VMEM((2, s, dh), F32)], rider=rider)


def _attn_bwd(proj, dya, qg, kg, rider=None):
    s = proj.shape[0]
    blk, win, dh = ATT_BLOCK, ATT_WINDOW, HEAD_DIM
    nq = s // blk
    max_windows = -(-s // win) + 1
    scale = 1.0 / math.sqrt(dh)
    steps = N_HEADS // 2
    heads = (0, 1)
    assert s >= win and s % blk == 0

    def body(*refs):
        ins, outs, scratch, copies = _split_refs(refs, 6, 5, rider)
        q_ref, k_ref, v_ref, do_ref, qg_ref, kg_ref = ins
        dq_ref, dk_ref, dv_ref, dqg_ref, dkg_ref = outs
        qn, kn, vb, dob, runs_ref, dqn, dkn, dvn = scratch
        finish = _ride(copies, pl.program_id(0) == 0, pl.program_id(0) == steps - 1)
        wd = _Window()

        @pl.when(pl.program_id(0) == 0)
        def _():
            dqg_ref[...] = jnp.zeros_like(dqg_ref)
            dkg_ref[...] = jnp.zeros_like(dkg_ref)

        for h in heads:
            lanes = _head_lanes(h)
            qv = q_ref[:, lanes]
            qn[h] = (qv * _rms_r(qv) * qg_ref[...] * scale).astype(BF16)
            kv = k_ref[:, lanes]
            kn[h] = (kv * _rms_r(kv) * kg_ref[...]).astype(BF16)
            vb[h] = v_ref[:, lanes].astype(BF16)
            dob[h] = do_ref[:, lanes].astype(BF16)
        dkn[...] = jnp.zeros_like(dkn)
        dvn[...] = jnp.zeros_like(dvn)

        def q_step(qi, _):
            qoff = pl.multiple_of(qi * blk, blk)
            qts = [qn[h, pl.ds(qoff, blk), :] for h in heads]
            dots = [dob[h, pl.ds(qoff, blk), :] for h in heads]

            def more(carry):
                g, live = carry[:2]
                return jnp.logical_and((qi + 1) * blk - g * win > 0, live > 0)

            def run_window(carry):
                g, _, runs = carry
                start, valid = wd.place(qi, g)
                for h in heads:
                    runs_ref[h, g] = runs[h]
                zs = [_dot_nt(qts[h], kn[h, pl.ds(start, win), :]) for h in heads]
                runs = tuple(runs[h] + jnp.sum(_sb_logs(zs[h], valid)[1], axis=1, keepdims=True) for h in heads)
                live = (jnp.maximum(jnp.max(runs[0]), jnp.max(runs[1])) > EXP_ZERO).astype(jnp.int32)
                return g + 1, live, runs

            zero = lambda cols: tuple(jnp.zeros((blk, cols), F32) for _ in heads)
            windows, _, _ = lax.while_loop(more, run_window, (jnp.int32(0), jnp.int32(1), zero(1)))

            def k_window(gg, carry):
                dq_accs, esums = carry
                g = windows - 1 - gg
                start, valid = wd.place(qi, g)
                kts = [kn[h, pl.ds(start, win), :] for h in heads]
                vts = [vb[h, pl.ds(start, win), :] for h in heads]
                zs = [_dot_nt(qts[h], kts[h]) for h in heads]
                dws = [_dot_nt(dots[h], vts[h]) for h in heads]
                logs = [_sb_logs(z, valid) for z in zs]
                tails = [wd.sums_after(logs[h][1], runs_ref[h, g])[0] for h in heads]
                wgts = [jnp.where(valid, jnp.exp(logs[h][0] + tails[h]), 0.0) for h in heads]
                es = [dws[h] * wgts[h] for h in heads]
                befores = [wd.sums_before(es[h], esums[h]) for h in heads]
                dzbs = []
                for h in heads:
                    beta = jnp.exp(logs[h][0])
                    dz = jnp.where(valid, es[h] * (1.0 - beta) - befores[h][0] * beta, 0.0)
                    dzbs.append(dz.astype(BF16))
                dq_accs = tuple(dq_accs[h] + _dot(dzbs[h], kts[h]) for h in heads)
                for h in heads:
                    dkn[h, pl.ds(start, win), :] += _dot_tn(dzbs[h], qts[h])
                    dvn[h, pl.ds(start, win), :] += _dot_tn(wgts[h].astype(BF16), dots[h])
                return dq_accs, tuple(esums[h] + befores[h][1] for h in heads)

            dq_accs, _ = lax.fori_loop(0, windows, k_window, (zero(dh), zero(1)))
            for h in heads:
                dqn[h, pl.ds(qoff, blk), :] = dq_accs[h]
            return 0

        lax.fori_loop(0, nq, q_step, 0)

        for h in heads:
            lanes = _head_lanes(h)
            qv = q_ref[:, lanes]
            dq, dqg = _rms_bwd(qv, _rms_r(qv), qg_ref[...] * scale, dqn[h])
            dq_ref[:, lanes] = dq.astype(BF16)
            dqg_ref[...] += dqg * scale
            kv = k_ref[:, lanes]
            dk, dkg = _rms_bwd(kv, _rms_r(kv), kg_ref[...], dkn[h])
            dk_ref[:, lanes] = dk.astype(BF16)
            dkg_ref[...] += dkg
            dv_ref[:, lanes] = dvn[h].astype(BF16)
        finish()

    pair = lambda group: pl.BlockSpec((s, 2 * dh), lambda p: (0, group * (D_ATT // (2 * dh)) + p))
    vec = pl.BlockSpec((1, dh), lambda p: (0, 0))
    return _call(
        body, "attn_bwd", (steps,), [pair(2), pair(3), pair(4), pair(0), vec, vec],
        [pair(0), pair(0), pair(0), vec, vec],
        [jax.ShapeDtypeStruct((s, D_ATT), BF16)] * 3 + [jax.ShapeDtypeStruct((1, dh), F32)] * 2,
        [proj, proj, proj, dya, qg, kg],
        scratch=[pltpu.VMEM((2, s, dh), BF16)] * 4 + [pltpu.VMEM((2, max_windows, blk, 1), F32)]
        + [pltpu.VMEM((2, s, dh), F32)] * 3, rider=rider)


def _block_diag(w):
    n, c, d = w.shape
    return jnp.einsum("ncd,nm->ncmd", w, jnp.eye(n, dtype=w.dtype)).reshape(n * c, n * d)


def _diag_blocks(full, n):
    c = full.shape[0] // n
    return jnp.stack([full[i * c:(i + 1) * c, i * c:(i + 1) * c] for i in range(n)])


FFN1 = ["ffn1_w_gate", "ffn1_w_up", "ffn1_w_down"]
FFN2 = ["ffn2_w_gate", "ffn2_w_up", "ffn2_w_down"]
MIXER = ["w_in", "w_out"]


def _pair_sums(gb, names, where):
    theirs = _pair_exchange([gb[n] for n in names], "pair_exchange_" + names[0])
    pair, own = zip(*[_pair_sum(gb[n], t, where, "pair_sum_" + n) for n, t in zip(names, theirs)])
    return _chip_rider(list(pair), list(own))


def _local_step(x, tgt, stacks, conv_stack, small, where):
    big = dict(zip(FFN1, _gather_weights([stacks[n] for n in FFN1], [])))
    wa = _block_diag(small["rg_w_a"]).astype(BF16)
    wx = _block_diag(small["rg_w_x"]).astype(BF16)

    x1, g1, u1, hb1, ab1, *landed = _ffn_fwd(x, small["ffn1_norm"], *[big[n] for n in FFN1],
                                             rider=_gather_rider([stacks[n] for n in MIXER], [conv_stack]))
    big.update(zip(MIXER, _forward_weights(landed[:len(MIXER)], "forward_mixer_weights")))
    conv_w = jnp.transpose(landed[-1], (1, 0, 2)).reshape(CONV_W, D_RNN)
    wout = big["w_out"].reshape(D_MODEL, D_MODEL)
    rg = (conv_w, small["conv_b"], wa, small["rg_b_a"], wx, small["rg_b_x"], small["rg_lambda"])
    proj, hb2 = _mix_pre(x1, small["mix_norm"], big["w_in"])
    yr, hseq = _rglru_fwd(proj, *rg)
    ya, *landed = _attn_fwd(proj, small["q_norm"], small["k_norm"], _gather_rider([stacks[n] for n in FFN2], []))
    big.update(zip(FFN2, _forward_weights(landed, "forward_ffn2_weights")))
    x2 = _mix_post(x1, yr, ya, small["rnn_out_norm"], small["attn_out_norm"], wout)
    dx3, g2, u2, hb3, ab3, loss = _ffn_fwd(x2, small["ffn2_norm"], *[big[n] for n in FFN2], tgt)

    gb, gs, slots = {}, {}, {}
    dx2, dg2, du2, dyb2, gs["ffn2_norm"] = _ffn_bwd_act(x2, small["ffn2_norm"], dx3, g2, u2, *[big[n] for n in FFN2],
                                                        "ffn2_bwd")
    gb["ffn2_w_gate"] = _ffn_wgrad(dg2, hb3, 1.0, "wgrad_gate_ffn2")
    gb["ffn2_w_up"] = _ffn_wgrad(du2, hb3, 1.0, "wgrad_up_ffn2")
    gb["ffn2_w_down"] = _ffn_wgrad(ab3, dyb2, 0.5, "wgrad_down_ffn2")
    dyr, dya, ycat, dxb2, gs["rnn_out_norm"], gs["attn_out_norm"] = _mix_post_bwd(
        dx2, yr, ya, small["rnn_out_norm"], small["attn_out_norm"], wout)
    gb["w_out"] = _wgrad_whole(ycat, dxb2, False, "wgrad_out")
    early = FFN2 + ["w_out"]
    dq, dk, dv, gs["q_norm"], gs["k_norm"], *done = _attn_bwd(
        proj, dya, small["q_norm"], small["k_norm"], _pair_sums(gb, early, where))
    slots.update(zip(early, done))
    dxr, dgate, gs["conv_w"], gs["conv_b"], dwa, gs["rg_b_a"], dwx, gs["rg_b_x"], gs["rg_lambda"] = _rglru_bwd(
        proj, hseq, dyr, *rg)
    gs["rg_w_a"] = _diag_blocks(dwa, RNN_BLOCKS)
    gs["rg_w_x"] = _diag_blocks(dwx, RNN_BLOCKS)
    dpb = jnp.concatenate([dxr, dgate, dq, dk, dv], axis=1)
    gb["w_in"] = _wgrad_whole(hb2, dpb, True, "wgrad_in")
    dx1, gs["mix_norm"] = _mix_pre_bwd(x1, small["mix_norm"], dx2, dpb, big["w_in"])

    dx0, dg1, du1, dyb1, gs["ffn1_norm"], slots["w_in"] = _ffn_bwd_act(
        x, small["ffn1_norm"], dx1, g1, u1, *[big[n] for n in FFN1], "ffn1_bwd", _pair_sums(gb, ["w_in"], where))
    gb["ffn1_w_gate"] = _ffn_wgrad(dg1, hb1, 1.0, "wgrad_gate_ffn1")
    gb["ffn1_w_up"] = _ffn_wgrad(du1, hb1, 1.0, "wgrad_up_ffn1")
    gb["ffn1_w_down"], slots["ffn1_w_gate"], slots["ffn1_w_up"] = _ffn_wgrad(
        ab1, dyb1, 0.5, "wgrad_down_ffn1", _pair_sums(gb, ["ffn1_w_gate", "ffn1_w_up"], where))
    last = _pair_sums(gb, ["ffn1_w_down"], where)
    slots["ffn1_w_down"], = _chip_exchange(last.plain, last.inplace)
    return loss[0, 0], dx0, slots, gs


ANY = pl.BlockSpec(memory_space=pl.ANY)


def _place():
    x, y, c = lax.axis_index("x"), lax.axis_index("y"), lax.axis_index("c")
    other_chips = [(1 - x, y), (x, 1 - y), (1 - x, 1 - y)]
    return x, y, c, 2 * x + y, other_chips


def _remote(src, dst, send_sem, recv_sem, to):
    return pltpu.make_async_remote_copy(src_ref=src, dst_ref=dst, send_sem=send_sem, recv_sem=recv_sem,
                                        device_id=to, device_id_type=MESH)


def _copy_plan(pairs):
    sends = [functools.partial(_remote, *a) for a, _ in pairs]
    arrivals = [functools.partial(_remote, *b) for _, b in pairs]
    return sends, arrivals


class _Rider:
    def __init__(self, plan, plain, inplace):
        self.plan, self.plain, self.inplace = plan, list(plain), list(inplace)

    def operands(self):
        return self.plain + self.inplace

    def out_shape(self):
        return [jax.ShapeDtypeStruct(a.shape, a.dtype) for a in self.inplace]

    def aliases(self, inputs_before, outputs_before):
        return {inputs_before + len(self.plain) + k: outputs_before + k for k in range(len(self.inplace))}

    def scratch(self):
        return [pltpu.SemaphoreType.DMA((3 * len(self.inplace),))] * 2


def _split_refs(refs, n_in, n_out, rider):
    if rider is None:
        return refs[:n_in], refs[n_in:n_in + n_out], refs[n_in + n_out:], None
    r_in, r_out = len(rider.operands()), len(rider.inplace)
    outs_at = n_in + r_in
    rest = refs[outs_at + n_out + r_out:]
    copies = functools.partial(rider.plan, refs[n_in:n_in + len(rider.plain)],
                               refs[outs_at + n_out:outs_at + n_out + r_out], *rest[-2:])
    return refs[:n_in], refs[outs_at:outs_at + n_out], rest[:-2], copies


def _ride(copies, first, last):
    if copies is None:
        return lambda: None

    @pl.when(first)
    def _():
        _start(copies()[0])

    def finish():
        @pl.when(last)
        def _():
            _finish(*copies())

    return finish


def _gather_rider(split, whole):
    n_split = len(split)
    return _Rider(lambda plain, stacks, ss, rs: _gather_ici(stacks, n_split, ss, rs), [], list(split) + list(whole))


def _chip_rider(sums, slots):
    return _Rider(_chip_copies, sums, slots)


def _start(makers):
    for make in makers:
        make().start()


def _finish(sends, arrivals):
    for make in arrivals:
        make().wait_recv()
    for make in sends:
        make().wait_send()


def _half(rows, c):
    return pl.ds(pl.multiple_of(c * rows, 16), rows)


def _gather_weights(split, whole):
    arrs = list(split) + list(whole)
    n, ns = len(arrs), len(split)

    def body(*refs):
        outs = refs[n:2 * n]
        send_sems, recv_sems, fsend_sems, frecv_sems = refs[2 * n:]
        sends, arrivals = _gather_ici(outs, ns, send_sems, recv_sems)
        passes, passed = _gather_d2d(outs[:ns], fsend_sems, frecv_sems)
        _start(sends)
        for k, make in enumerate(arrivals):
            make().wait_recv()
            if k < 3 * ns:
                passes[k]().start()
        _finish(sends + passes, passed)

    return pl.pallas_call(
        body, name="gather_weights",
        in_specs=[ANY] * n, out_specs=[ANY] * n,
        out_shape=[jax.ShapeDtypeStruct(a.shape, a.dtype) for a in arrs],
        input_output_aliases={i: i for i in range(n)},
        scratch_shapes=[pltpu.SemaphoreType.DMA((3 * n,)), pltpu.SemaphoreType.DMA((3 * n,)),
                        pltpu.SemaphoreType.DMA((3 * ns,)), pltpu.SemaphoreType.DMA((3 * ns,))],
    )(*arrs)


def _gather_ici(stacks, n_split, send_sems, recv_sems):
    x, y, c, me, chips = _place()

    def region(i, chip):
        if i < n_split:
            return stacks[i].at[chip, _half(stacks[i].shape[1] // 2, c)]
        return stacks[i].at[chip]

    pairs = []
    for i in range(len(stacks)):
        for p, (cx, cy) in enumerate(chips):
            k = 3 * i + p
            mine, got = region(i, me), region(i, 2 * cx + cy)
            sems, to = (send_sems.at[k], recv_sems.at[k]), (cx, cy, c)
            pairs.append(((mine, mine, *sems, to), (got, got, *sems, to)))
    return _copy_plan(pairs)


def _gather_d2d(stacks, send_sems, recv_sems):
    x, y, c, _, chips = _place()
    sibling = (x, y, 1 - c)
    pairs = []
    for i, stack in enumerate(stacks):
        rows = stack.shape[1] // 2
        for p, (cx, cy) in enumerate(chips):
            k = 3 * i + p
            got, theirs = stack.at[2 * cx + cy, _half(rows, c)], stack.at[2 * cx + cy, _half(rows, 1 - c)]
            sems = (send_sems.at[k], recv_sems.at[k])
            pairs.append(((got, got, *sems, sibling), (theirs, theirs, *sems, sibling)))
    return _copy_plan(pairs)


def _forward_weights(split, name):
    n = len(split)

    def body(*refs):
        sends, arrivals = _gather_d2d(refs[n:2 * n], *refs[2 * n:])
        _start(sends)
        _finish(sends, arrivals)

    return pl.pallas_call(
        body, name=name,
        in_specs=[ANY] * n, out_specs=[ANY] * n,
        out_shape=[jax.ShapeDtypeStruct(a.shape, a.dtype) for a in split],
        input_output_aliases={i: i for i in range(n)},
        scratch_shapes=[pltpu.SemaphoreType.DMA((3 * n,))] * 2,
    )(*split)


def _pair_exchange(grads, name):
    n = len(grads)

    def body(*refs):
        ins, theirs = refs[:n], refs[n:2 * n]
        send_sems, recv_sems = refs[2 * n:]
        x, y, c, _, _ = _place()
        sibling = (x, y, 1 - c)
        sends = [_remote(ins[k].at[:, _half(grads[k].shape[1] // 2, 1 - c)], theirs[k],
                         send_sems.at[k], recv_sems.at[k], sibling) for k in range(n)]
        for cp in sends:
            cp.start()
        for k in range(n):
            _remote(theirs[k], theirs[k], send_sems.at[k], recv_sems.at[k], sibling).wait_recv()
        for cp in sends:
            cp.wait_send()

    return pl.pallas_call(
        body, name=name,
        in_specs=[ANY] * n, out_specs=[ANY] * n,
        out_shape=[jax.ShapeDtypeStruct((g.shape[0], g.shape[1] // 2, g.shape[2]), g.dtype) for g in grads],
        scratch_shapes=[pltpu.SemaphoreType.DMA((n,))] * 2,
    )(*grads)


def _chip_exchange(sums, slots):
    n = len(sums)

    def body(*refs):
        sends, arrivals = _chip_copies(refs[:n], refs[2 * n:3 * n], *refs[3 * n:])
        _start(sends)
        _finish(sends, arrivals)

    return pl.pallas_call(
        body, name="grad_chip_exchange",
        in_specs=[ANY] * (2 * n), out_specs=[ANY] * n,
        out_shape=[jax.ShapeDtypeStruct(a.shape, a.dtype) for a in slots],
        input_output_aliases={n + k: k for k in range(n)},
        scratch_shapes=[pltpu.SemaphoreType.DMA((3 * n,)), pltpu.SemaphoreType.DMA((3 * n,))],
    )(*sums, *slots)


def _chip_copies(sums, slots, send_sems, recv_sems):
    x, y, c, me, chips = _place()
    pairs = []
    for k in range(len(sums)):
        for p, (cx, cy) in enumerate(chips):
            j = 3 * k + p
            got = slots[k].at[2 * cx + cy]
            sems, to = (send_sems.at[j], recv_sems.at[j]), (cx, cy, c)
            pairs.append(((sums[k].at[2 * cx + cy], slots[k].at[me], *sems, to), (got, got, *sems, to)))
    return _copy_plan(pairs)


def _half_swap(halves):
    n = len(halves)

    def body(*refs):
        outs = refs[n:2 * n]
        send_sems, recv_sems = refs[2 * n:]
        x, y, c, _, _ = _place()
        sibling = (x, y, 1 - c)
        sends = [_remote(outs[k].at[c], outs[k].at[c], send_sems.at[k], recv_sems.at[k], sibling) for k in range(n)]
        for cp in sends:
            cp.start()
        for k in range(n):
            got = outs[k].at[1 - c]
            _remote(got, got, send_sems.at[k], recv_sems.at[k], sibling).wait_recv()
        for cp in sends:
            cp.wait_send()

    return pl.pallas_call(
        body, name="grad_half_swap",
        in_specs=[ANY] * n, out_specs=[ANY] * n,
        out_shape=[jax.ShapeDtypeStruct(a.shape, a.dtype) for a in halves],
        input_output_aliases={k: k for k in range(n)},
        scratch_shapes=[pltpu.SemaphoreType.DMA((n,))] * 2,
    )(*halves)


def _gather_small(packed):
    n_dev = 8

    def body(in_ref, out_ref, send_sems, recv_sems, loc_sem):
        x, y, c, _, _ = _place()
        me = 4 * x + 2 * y + c
        local = pltpu.make_async_copy(in_ref, out_ref.at[me], loc_sem)
        local.start()
        peers = []
        for k in range(1, n_dev):
            fx, fy, fc = (k >> 2) & 1, (k >> 1) & 1, k & 1
            peers.append((x ^ fx, y ^ fy, c ^ fc))
        sends = [_remote(in_ref, out_ref.at[me], send_sems.at[k], recv_sems.at[k], peer)
                 for k, peer in enumerate(peers)]
        for cp in sends:
            cp.start()
        for k, (px, py, pc) in enumerate(peers):
            got = out_ref.at[4 * px + 2 * py + pc]
            _remote(got, got, send_sems.at[k], recv_sems.at[k], (px, py, pc)).wait_recv()
        for cp in sends:
            cp.wait_send()
        local.wait()

    return pl.pallas_call(
        body, name="gather_small_grads",
        in_specs=[ANY], out_specs=ANY,
        out_shape=jax.ShapeDtypeStruct((n_dev,) + packed.shape, packed.dtype),
        scratch_shapes=[pltpu.SemaphoreType.DMA((n_dev - 1,)), pltpu.SemaphoreType.DMA((n_dev - 1,)),
                        pltpu.SemaphoreType.DMA],
    )(packed)


def _row_tile(r):
    return r // 4 if r >= 256 and (r // 4) % 16 == 0 else r


def _prefetch_call(body, name, grid, in_specs, out_specs, out_shape):
    spec = pltpu.PrefetchScalarGridSpec(num_scalar_prefetch=1, grid=grid, in_specs=in_specs, out_specs=out_specs)
    return pl.pallas_call(body, name=name, grid_spec=spec, out_shape=out_shape,
                          compiler_params=_params(("arbitrary",) * len(grid)))


def _place_shard(w2d, where, dtype, name):
    r, c = w2d.shape
    tr = _row_tile(r)

    def body(where_ref, w_ref, out_ref):
        out_ref[...] = w_ref[...].astype(dtype)

    return _prefetch_call(
        body, name, (r // tr,), [pl.BlockSpec((tr, c), lambda i, s: (i, 0))],
        pl.BlockSpec((None, tr, c), lambda i, s: (s[1], i, 0)),
        jax.ShapeDtypeStruct((N_CHIPS, r, c), dtype))(where, w2d)


def _pair_sum(full, theirs, where, name):
    nb, hs, c = theirs.shape

    def body(where_ref, a_ref, b_ref, out_ref, own_ref):
        total = (a_ref[...].astype(F32) + b_ref[...].astype(F32)).astype(BF16)
        out_ref[...] = total

        @pl.when(pl.program_id(0) == where_ref[1])
        def _():
            own_ref[...] = total

    blk = pl.BlockSpec((None, hs, c), lambda j, s: (j, 0, 0))
    shape = jax.ShapeDtypeStruct(theirs.shape, BF16)
    return _prefetch_call(
        body, name, (nb,), [pl.BlockSpec((None, hs, c), lambda j, s: (j, s[0], 0)), blk],
        [blk, pl.BlockSpec((None, hs, c), lambda j, s: (s[1], 0, 0))], [shape, shape])(where, full, theirs)


def _chip_sum(slots, where, name):
    nb, hs, c = slots.shape
    tr = _row_tile(hs)

    def body(where_ref, a_ref, out_ref):
        total = a_ref[0].astype(F32)
        for j in range(1, nb):
            total = total + a_ref[j].astype(F32)
        out_ref[...] = total

    return _prefetch_call(
        body, name, (hs // tr,), [pl.BlockSpec((nb, tr, c), lambda i, s: (0, i, 0))],
        pl.BlockSpec((None, tr, c), lambda i, s: (s[0], i, 0)),
        jax.ShapeDtypeStruct((2, hs, c), F32))(where, slots)


def _slot_sum(a, name):
    nb, r, c = a.shape
    tr = _row_tile(r)

    def body(a_ref, out_ref):
        total = a_ref[0].astype(F32)
        for j in range(1, nb):
            total = total + a_ref[j].astype(F32)
        out_ref[...] = total

    return pl.pallas_call(
        body, name=name, grid=(r // tr,),
        in_specs=[pl.BlockSpec((nb, tr, c), lambda i: (0, i, 0))],
        out_specs=pl.BlockSpec((tr, c), lambda i: (i, 0)),
        out_shape=jax.ShapeDtypeStruct((r, c), F32), compiler_params=_params(("arbitrary",)),
    )(a)


def _adamw(w, g, m, v, name):
    r, c = w.shape
    tr = _row_tile(r)
    c1 = 1.0 - ADAM_B1 ** ADAM_STEP
    c2 = 1.0 - ADAM_B2 ** ADAM_STEP

    def body(w_ref, g_ref, m_ref, v_ref, d_ref, m2_ref, v2_ref):
        gv = g_ref[...]
        m2 = ADAM_B1 * m_ref[...] + (1.0 - ADAM_B1) * gv
        v2 = ADAM_B2 * v_ref[...] + (1.0 - ADAM_B2) * (gv * gv)
        m2_ref[...] = m2
        v2_ref[...] = v2
        d_ref[...] = -ADAM_LR * ((m2 / c1) / (jnp.sqrt(v2 / c2) + ADAM_EPS) + ADAM_WD * w_ref[...])

    blk = pl.BlockSpec((tr, c), lambda i: (i, 0))
    return pl.pallas_call(
        body, name=name, grid=(r // tr,), in_specs=[blk] * 4, out_specs=[blk] * 3,
        out_shape=[jax.ShapeDtypeStruct((r, c), F32)] * 3, compiler_params=_params(("arbitrary",)),
    )(w, g, m, v)


WEIGHTS = ["ffn1_norm", "ffn1_w_gate", "ffn1_w_up", "ffn1_w_down", "mix_norm", "w_in", "conv_w", "conv_b",
           "rg_w_a", "rg_b_a", "rg_w_x", "rg_b_x", "rg_lambda", "q_norm", "k_norm", "rnn_out_norm",
           "attn_out_norm", "w_out", "ffn2_norm", "ffn2_w_gate", "ffn2_w_up", "ffn2_w_down"]
BIG = ["ffn1_w_gate", "ffn1_w_up", "ffn1_w_down", "w_in", "w_out", "ffn2_w_gate", "ffn2_w_up", "ffn2_w_down"]
SMALL = [n for n in WEIGHTS if n not in BIG]
PACK_LANES = 128
PACK_ROW_ALIGN = 8


def _hidden_major(name, a):
    return jnp.transpose(a) if name.endswith(("w_gate", "w_up")) else a


def _pack(parts):
    flat = jnp.concatenate([p.reshape(-1) for p in parts])
    unit = PACK_LANES * PACK_ROW_ALIGN
    padded = -(-flat.shape[0] // unit) * unit
    return jnp.pad(flat, (0, padded - flat.shape[0])).reshape(-1, PACK_LANES)


def _unpack(packed, shapes):
    flat = packed.reshape(-1)
    out, at = [], 0
    for shp in shapes:
        size = math.prod(shp)
        out.append(flat[at:at + size].reshape(shp))
        at += size
    return out


def kernel(x, ffn1_norm, ffn1_w_gate, ffn1_w_up, ffn1_w_down, mix_norm, w_in, conv_w, conv_b, rg_w_a, rg_b_a, rg_w_x, rg_b_x, rg_lambda, q_norm, k_norm, rnn_out_norm, attn_out_norm, w_out, ffn2_norm, ffn2_w_gate, ffn2_w_up, ffn2_w_down, loss_target, m_ffn1_norm, m_ffn1_w_gate, m_ffn1_w_up, m_ffn1_w_down, m_mix_norm, m_w_in, m_conv_w, m_conv_b, m_rg_w_a, m_rg_b_a, m_rg_w_x, m_rg_b_x, m_rg_lambda, m_q_norm, m_k_norm, m_rnn_out_norm, m_attn_out_norm, m_w_out, m_ffn2_norm, m_ffn2_w_gate, m_ffn2_w_up, m_ffn2_w_down, v_ffn1_norm, v_ffn1_w_gate, v_ffn1_w_up, v_ffn1_w_down, v_mix_norm, v_w_in, v_conv_w, v_conv_b, v_rg_w_a, v_rg_b_a, v_rg_w_x, v_rg_b_x, v_rg_lambda, v_q_norm, v_k_norm, v_rnn_out_norm, v_attn_out_norm, v_w_out, v_ffn2_norm, v_ffn2_w_gate, v_ffn2_w_up, v_ffn2_w_down):
    given = dict(locals())
    w = {n: given[n] for n in WEIGHTS}
    m = {n: given["m_" + n] for n in WEIGHTS}
    v = {n: given["v_" + n] for n in WEIGHTS}
    chip = 2 * lax.axis_index("x") + lax.axis_index("y")

    where = jnp.stack([lax.axis_index("c"), chip]).astype(jnp.int32)

    stacks = {n: _place_shard(_hidden_major(n, w[n][0]), where, BF16, "place_" + n) for n in BIG}
    conv_stack = _place_shard(w["conv_w"][0], where, F32, "place_conv_w")
    small = {n: (w[n][0] if w[n].ndim > 2 else w[n]) for n in SMALL if n != "conv_w"}

    loss, grad_x, slots, gs = _local_step(x[0], loss_target[0], stacks, conv_stack, small, where)
    loss = lax.psum(loss, ("x", "y", "c"))

    swapped = _half_swap([_chip_sum(slots[n], where, "chip_sum_" + n) for n in BIG])
    grads, deltas, new_m, new_v = {}, {}, {}, {}
    for n, t in zip(BIG, swapped):
        g2 = t.reshape(t.shape[0] * t.shape[1], t.shape[2])
        d2, m2, v2 = _adamw(_hidden_major(n, w[n][0]), g2, _hidden_major(n, m[n][0]), _hidden_major(n, v[n][0]),
                            "adamw_" + n)
        back = lambda a: _hidden_major(n, a).reshape(w[n].shape)
        grads[n], deltas[n], new_m[n], new_v[n] = back(g2), back(d2), back(m2), back(v2)

    full_shapes = [gs[n].shape for n in SMALL]
    everyone = _gather_small(_pack([gs[n] for n in SMALL]))
    g_small = _slot_sum(everyone, "small_grad_sum")
    g_parts = dict(zip(SMALL, _unpack(g_small, full_shapes)))
    quarter = D_RNN // N_CHIPS
    g_parts["conv_w"] = lax.dynamic_slice_in_dim(g_parts["conv_w"], chip * quarter, quarter, axis=1)
    local_shapes = [w[n].shape for n in SMALL]
    pk = lambda tree: _pack([tree[n] for n in SMALL])
    d_s, m_s, v_s = _adamw(pk(w), pk(g_parts), pk(m), pk(v), "adamw_small")
    for tree, packed in ((grads, pk(g_parts)), (deltas, d_s), (new_m, m_s), (new_v, v_s)):
        tree.update(zip(SMALL, _unpack(packed, local_shapes)))

    return (loss, grad_x.reshape(x.shape), *[grads[n] for n in WEIGHTS], *[deltas[n] for n in WEIGHTS],
            *[new_m[n] for n in WEIGHTS], *[new_v[n] for n in WEIGHTS])
```

```python
import functools
import math

import jax
import jax.numpy as jnp
from jax import lax
from jax.experimental import pallas as pl
from jax.experimental.pallas import tpu as pltpu

F32 = jnp.float32
BF16 = jnp.bfloat16
MESH = pl.DeviceIdType.MESH

D_MODEL = 1024
N_CHIPS = 4
D_RNN = 512
D_ATT = 512
N_HEADS = 8
HEAD_DIM = 64
RNN_BLOCKS = 8
CONV_W = 4
RG_C = 8.0
N_IN = 2 * D_RNN + 3 * D_ATT
EPS = 1e-6
ATT_BLOCK = 128
ATT_WINDOW = 384
ATT_SPLIT = 256
EXP_ZERO = -105.0

ADAM_LR = 0.001
ADAM_B1 = 0.9
ADAM_B2 = 0.999
ADAM_EPS = 1e-08
ADAM_WD = 0.01
ADAM_STEP = 10

V7X_VMEM_LIMIT = 56 * 1024 * 1024
TOKEN_TILE = 512
FFN_TILE = 256
WGRAD_TILE = 2048
WHOLE_TILE = 1024

GELU_K0 = math.sqrt(2.0 / math.pi)
GELU_K1 = 0.044715


def _params(sem=None):
    return pltpu.CompilerParams(dimension_semantics=sem, vmem_limit_bytes=V7X_VMEM_LIMIT)


def _dot(a, b):
    return jnp.dot(a, b, preferred_element_type=F32)


def _dot_nt(a, b):
    return lax.dot_general(a, b, (((1,), (1,)), ((), ())), preferred_element_type=F32)


def _dot_tn(a, b):
    return lax.dot_general(a, b, (((0,), (0,)), ((), ())), preferred_element_type=F32)


def _sigmoid(x):
    return 1.0 / (1.0 + jnp.exp(-x))


def _rms_r(xv):
    return lax.rsqrt(jnp.mean(xv * xv, axis=-1, keepdims=True) + EPS)


def _rms_bwd(xv, r, nw, dh):
    t = dh * nw
    dx = r * t - xv * (r * r * r * jnp.mean(t * xv, axis=-1, keepdims=True))
    dn = jnp.sum(dh * xv * r, axis=0, keepdims=True)
    return dx, dn


def _gelu(x):
    t = jnp.tanh(GELU_K0 * (x + GELU_K1 * x * x * x))
    return 0.5 * x * (1.0 + t)


def _gelu_grad(x):
    t = jnp.tanh(GELU_K0 * (x + GELU_K1 * x * x * x))
    return 0.5 * (1.0 + t) + 0.5 * x * (1.0 - t * t) * (GELU_K0 * (1.0 + 3.0 * GELU_K1 * x * x))


def _expm1_neg(x):
    p = 1.0 + x * (1.0 / 8.0)
    for k in (7.0, 6.0, 5.0, 4.0, 3.0, 2.0):
        p = 1.0 + x * (1.0 / k) * p
    return jnp.where(x > -0.25, x * p, jnp.exp(x) - 1.0)


def _log_sigmoid(x):
    return jnp.minimum(x, 0.0) - jnp.log(1.0 + jnp.exp(-jnp.abs(x)))


def _tile(s):
    return min(TOKEN_TILE, s)


def _ffn_fwd(x, nw, wg, wu, wd, tgt=None, rider=None):
    s, d = x.shape
    nb, fb, _ = wg.shape
    tm = min(FFN_TILE, s)
    ni = s // tm
    assert s % tm == 0
    with_loss = tgt is not None
    n_in, n_out = 5 + with_loss, 5 + with_loss

    def body(*refs):
        ins, outs, _, copies = _split_refs(refs, n_in, n_out, rider)
        x_ref, nw_ref, wg_ref, wu_ref, wd_ref = ins[:5]
        out_ref, g_ref, u_ref, hb_ref, ab_ref = outs[:5]
        i = pl.program_id(0)
        finish = _ride(copies, i == 0, i == ni - 1)

        xv = x_ref[...]
        hb = (xv * _rms_r(xv) * nw_ref[...]).astype(BF16)
        hb_ref[...] = hb
        y = jnp.zeros((tm, d), F32)
        for jb in range(nb):
            g = _dot_nt(hb, wg_ref[jb])
            u = _dot_nt(hb, wu_ref[jb])
            g_ref[jb] = g.astype(BF16)
            u_ref[jb] = u.astype(BF16)
            ab = (g * _sigmoid(g) * u).astype(BF16)
            ab_ref[jb] = ab
            y = y + _dot(ab, wd_ref[jb])
        y = xv + 0.5 * y
        if with_loss:
            tgt_ref, loss_ref = ins[5], outs[5]
            diff = y - tgt_ref[...]
            out_ref[...] = diff * (1.0 / d)

            @pl.when(i == 0)
            def _():
                loss_ref[...] = jnp.zeros_like(loss_ref)

            loss_ref[...] += jnp.sum(diff * diff) * (0.5 / d)
        else:
            out_ref[...] = y
        finish()

    row = pl.BlockSpec((tm, d), lambda i: (i, 0))
    weight = pl.BlockSpec((nb, fb, d), lambda i: (0, 0, 0), pipeline_mode=pl.Buffered(1))
    in_specs = [row, pl.BlockSpec((1, d), lambda i: (0, 0)), weight, weight, weight]
    args = [x, nw, wg, wu, wd]
    if with_loss:
        in_specs.append(row)
        args.append(tgt)
    blk = pl.BlockSpec((nb, tm, fb), lambda i: (0, i, 0))
    out_shape = [jax.ShapeDtypeStruct((s, d), F32), jax.ShapeDtypeStruct((nb, s, fb), BF16),
                 jax.ShapeDtypeStruct((nb, s, fb), BF16), jax.ShapeDtypeStruct((s, d), BF16),
                 jax.ShapeDtypeStruct((nb, s, fb), BF16)]
    out_specs = [row, blk, blk, row, blk]
    if with_loss:
        out_shape.append(jax.ShapeDtypeStruct((1, 128), F32))
        out_specs.append(pl.BlockSpec((1, 128), lambda i: (0, 0)))
    return _call(body, "ffn_fwd_loss" if with_loss else "ffn_fwd", (ni,), in_specs, out_specs, out_shape, args,
                 rider=rider)


def _call(body, name, grid, in_specs, out_specs, out_shape, args, scratch=(), rider=None):
    in_specs, out_specs, out_shape, scratch = list(in_specs), list(out_specs), list(out_shape), list(scratch)
    extra, aliases = [], {}
    if rider is not None:
        extra = rider.operands()
        aliases = rider.aliases(len(args), len(out_shape))
        in_specs += [ANY] * len(extra)
        out_specs += [ANY] * len(rider.inplace)
        out_shape += rider.out_shape()
        scratch += rider.scratch()
    return pl.pallas_call(
        body, name=name, grid=grid, in_specs=in_specs, out_specs=out_specs, out_shape=out_shape,
        input_output_aliases=aliases, scratch_shapes=scratch,
        compiler_params=_params(("arbitrary",) * len(grid)),
    )(*args, *extra)


def _ffn_bwd_act(x, nw, dy, g, u, wg, wu, wd, name, rider=None):
    s, d = x.shape
    nb, fb, _ = wg.shape
    tm = min(FFN_TILE, s)
    assert s % tm == 0

    def body(*refs):
        ins, outs, _, copies = _split_refs(refs, 8, 5, rider)
        x_ref, nw_ref, dy_ref, g_ref, u_ref, wg_ref, wu_ref, wd_ref = ins
        dx_ref, dg_ref, du_ref, dyb_ref, dnw_ref = outs
        finish = _ride(copies, pl.program_id(0) == 0, pl.program_id(0) == s // tm - 1)
        dyv = dy_ref[...]
        dyb = dyv.astype(BF16)
        dyb_ref[...] = dyb
        dh = jnp.zeros((tm, d), F32)
        for jb in range(nb):
            da = 0.5 * _dot_nt(dyb, wd_ref[jb])
            gv = g_ref[jb].astype(F32)
            sg = _sigmoid(gv)
            dub = (da * (gv * sg)).astype(BF16)
            dgb = (da * u_ref[jb].astype(F32) * (sg * (1.0 + gv * (1.0 - sg)))).astype(BF16)
            dg_ref[jb] = dgb
            du_ref[jb] = dub
            dh = dh + _dot(dgb, wg_ref[jb]) + _dot(dub, wu_ref[jb])
        xv = x_ref[...]
        dx, dn = _rms_bwd(xv, _rms_r(xv), nw_ref[...], dh)
        dx_ref[...] = dyv + dx

        @pl.when(pl.program_id(0) == 0)
        def _():
            dnw_ref[...] = jnp.zeros_like(dnw_ref)

        dnw_ref[...] += dn
        finish()

    row = pl.BlockSpec((tm, d), lambda i: (i, 0))
    vec = pl.BlockSpec((1, d), lambda i: (0, 0))
    blk = pl.BlockSpec((nb, tm, fb), lambda i: (0, i, 0))
    weight = pl.BlockSpec((nb, fb, d), lambda i: (0, 0, 0), pipeline_mode=pl.Buffered(1))
    return _call(
        body, name, (s // tm,), [row, vec, row, blk, blk, weight, weight, weight], [row, blk, blk, row, vec],
        [jax.ShapeDtypeStruct((s, d), F32), jax.ShapeDtypeStruct((nb, s, fb), BF16),
         jax.ShapeDtypeStruct((nb, s, fb), BF16), jax.ShapeDtypeStruct((s, d), BF16),
         jax.ShapeDtypeStruct((1, d), F32)],
        [x, nw, dy, g, u, wg, wu, wd], rider=rider)


def _wgrad(a, b, a_spec, b_spec, out_rows, out_cols, scale, name, tk, rider=None):
    s = a.shape[-2]
    nk = s // tk
    assert s % tk == 0

    def body(*refs):
        (a_ref, b_ref), (out_ref,), (acc,), copies = _split_refs(refs, 2, 1, rider)
        j, k = pl.program_id(0), pl.program_id(1)
        finish = _ride(copies, jnp.logical_and(j == 0, k == 0), jnp.logical_and(j == N_CHIPS - 1, k == nk - 1))

        @pl.when(k == 0)
        def _():
            acc[...] = jnp.zeros_like(acc)

        acc[...] += _dot_tn(a_ref[...], b_ref[...])

        @pl.when(k == nk - 1)
        def _():
            out_ref[...] = (acc[...] * scale).astype(BF16)

        finish()

    outs = _call(
        body, name, (N_CHIPS, nk), [a_spec(tk), b_spec(tk)],
        [pl.BlockSpec((None, out_rows, out_cols), lambda j, k: (j, 0, 0))],
        [jax.ShapeDtypeStruct((N_CHIPS, out_rows, out_cols), BF16)], [a, b],
        scratch=[pltpu.VMEM((out_rows, out_cols), F32)], rider=rider)
    return outs[0] if rider is None else outs


def _wgrad_whole(a, b, col_blocks, name, rider=None):
    s, m = a.shape
    n = b.shape[1]
    tk = min(WHOLE_TILE, s)
    nk = s // tk
    assert s % tk == 0
    out_shape = (N_CHIPS, m, n // N_CHIPS) if col_blocks else (N_CHIPS, m // N_CHIPS, n)

    def body(*refs):
        (a_ref, b_ref), (out_ref,), (acc,), copies = _split_refs(refs, 2, 1, rider)
        k = pl.program_id(0)
        finish = _ride(copies, k == 0, k == nk - 1)

        @pl.when(k == 0)
        def _():
            acc[...] = jnp.zeros_like(acc)

        acc[...] += _dot_tn(a_ref[...], b_ref[...])

        @pl.when(k == nk - 1)
        def _():
            for j in range(N_CHIPS):
                if col_blocks:
                    out_ref[j] = acc[:, j * out_shape[2]:(j + 1) * out_shape[2]].astype(BF16)
                else:
                    out_ref[j] = acc[j * out_shape[1]:(j + 1) * out_shape[1], :].astype(BF16)

        finish()

    outs = _call(
        body, name, (nk,), [pl.BlockSpec((tk, m), lambda k: (k, 0)), pl.BlockSpec((tk, n), lambda k: (k, 0))],
        [pl.BlockSpec(out_shape, lambda k: (0, 0, 0))], [jax.ShapeDtypeStruct(out_shape, BF16)], [a, b],
        scratch=[pltpu.VMEM((m, n), F32)], rider=rider)
    return outs[0] if rider is None else outs


def _ffn_wgrad(stack, shared, scale, name, rider=None):
    s, d = shared.shape
    fb = stack.shape[-1]
    return _wgrad(stack, shared, lambda tk: pl.BlockSpec((None, tk, fb), lambda j, k: (j, k, 0)),
                  lambda tk: pl.BlockSpec((tk, d), lambda j, k: (k, 0)), fb, d, scale, name,
                  min(WGRAD_TILE, s), rider)


def _mix_pre(x, nw, win):
    s, d = x.shape
    nb, _, cb = win.shape
    tm = min(FFN_TILE, s)
    assert s % tm == 0

    def body(x_ref, nw_ref, w_ref, p_ref, hb_ref):
        xv = x_ref[...]
        hb = (xv * _rms_r(xv) * nw_ref[...]).astype(BF16)
        hb_ref[...] = hb
        for j in range(nb):
            p_ref[:, j * cb:(j + 1) * cb] = _dot(hb, w_ref[j])

    row = pl.BlockSpec((tm, d), lambda i: (i, 0))
    return pl.pallas_call(
        body, name="mix_pre", grid=(s // tm,),
        in_specs=[row, pl.BlockSpec((1, d), lambda i: (0, 0)),
                  pl.BlockSpec((nb, d, cb), lambda i: (0, 0, 0), pipeline_mode=pl.Buffered(1))],
        out_specs=[pl.BlockSpec((tm, nb * cb), lambda i: (i, 0)), row],
        out_shape=[jax.ShapeDtypeStruct((s, nb * cb), F32), jax.ShapeDtypeStruct((s, d), BF16)],
        compiler_params=_params(("arbitrary",)),
    )(x, nw, win)


def _mix_pre_bwd(x, nw, dres, dpb, win):
    s, d = x.shape
    nb, _, cb = win.shape
    tm = min(FFN_TILE, s)
    assert s % tm == 0

    def body(x_ref, nw_ref, dres_ref, dp_ref, w_ref, dx_ref, dnw_ref):
        dh = jnp.zeros((tm, d), F32)
        for j in range(nb):
            dh = dh + _dot_nt(dp_ref[:, j * cb:(j + 1) * cb], w_ref[j])
        xv = x_ref[...]
        dx, dn = _rms_bwd(xv, _rms_r(xv), nw_ref[...], dh)
        dx_ref[...] = dres_ref[...] + dx

        @pl.when(pl.program_id(0) == 0)
        def _():
            dnw_ref[...] = jnp.zeros_like(dnw_ref)

        dnw_ref[...] += dn

    row = pl.BlockSpec((tm, d), lambda i: (i, 0))
    vec = pl.BlockSpec((1, d), lambda i: (0, 0))
    return pl.pallas_call(
        body, name="mix_pre_bwd", grid=(s // tm,),
        in_specs=[row, vec, row, pl.BlockSpec((tm, nb * cb), lambda i: (i, 0)),
                  pl.BlockSpec((nb, d, cb), lambda i: (0, 0, 0), pipeline_mode=pl.Buffered(1))],
        out_specs=[row, vec],
        out_shape=[jax.ShapeDtypeStruct((s, d), F32), jax.ShapeDtypeStruct((1, d), F32)],
        compiler_params=_params(("arbitrary",)),
    )(x, nw, dres, dpb, win)


def _mix_post(x, yr, ya, nr, na, wout):
    s, d = x.shape
    h = yr.shape[1]
    tm = _tile(s)

    def body(x_ref, yr_ref, ya_ref, nr_ref, na_ref, w_ref, out_ref):
        yrv = yr_ref[...]
        yav = ya_ref[...]
        onb = (yrv * _rms_r(yrv) * nr_ref[...]).astype(BF16)
        oab = (yav * _rms_r(yav) * na_ref[...]).astype(BF16)
        out_ref[...] = x_ref[...] + _dot(onb, w_ref[0:h, :]) + _dot(oab, w_ref[h:2 * h, :])

    row = pl.BlockSpec((tm, d), lambda i: (i, 0))
    half = pl.BlockSpec((tm, h), lambda i: (i, 0))
    vec = pl.BlockSpec((1, h), lambda i: (0, 0))
    return pl.pallas_call(
        body, name="mix_post", grid=(s // tm,),
        in_specs=[row, half, half, vec, vec, pl.BlockSpec((2 * h, d), lambda i: (0, 0))],
        out_specs=row, out_shape=jax.ShapeDtypeStruct((s, d), F32),
        compiler_params=_params(("arbitrary",)),
    )(x, yr, ya, nr, na, wout)


def _mix_post_bwd(dx, yr, ya, nr, na, wout):
    s, d = dx.shape
    h = yr.shape[1]
    tm = _tile(s)

    def body(dx_ref, yr_ref, ya_ref, nr_ref, na_ref, w_ref,
             dyr_ref, dya_ref, yc_ref, dxb_ref, dnr_ref, dna_ref):
        i = pl.program_id(0)
        dxb = dx_ref[...].astype(BF16)
        dxb_ref[...] = dxb
        dyc = _dot_nt(dxb, w_ref[...])
        yrv = yr_ref[...]
        yav = ya_ref[...]
        rr = _rms_r(yrv)
        ra = _rms_r(yav)
        yc_ref[:, 0:h] = (yrv * rr * nr_ref[...]).astype(BF16)
        yc_ref[:, h:2 * h] = (yav * ra * na_ref[...]).astype(BF16)
        dyr, dnr = _rms_bwd(yrv, rr, nr_ref[...], dyc[:, 0:h])
        dya, dna = _rms_bwd(yav, ra, na_ref[...], dyc[:, h:2 * h])
        dyr_ref[...] = dyr
        dya_ref[...] = dya

        @pl.when(i == 0)
        def _():
            dnr_ref[...] = jnp.zeros_like(dnr_ref)
            dna_ref[...] = jnp.zeros_like(dna_ref)

        dnr_ref[...] += dnr
        dna_ref[...] += dna

    row = pl.BlockSpec((tm, d), lambda i: (i, 0))
    half = pl.BlockSpec((tm, h), lambda i: (i, 0))
    vec = pl.BlockSpec((1, h), lambda i: (0, 0))
    return pl.pallas_call(
        body, name="mix_post_bwd", grid=(s // tm,),
        in_specs=[row, half, half, vec, vec, pl.BlockSpec((2 * h, d), lambda i: (0, 0))],
        out_specs=[half, half, pl.BlockSpec((tm, 2 * h), lambda i: (i, 0)), row, vec, vec],
        out_shape=[jax.ShapeDtypeStruct((s, h), F32), jax.ShapeDtypeStruct((s, h), F32),
                   jax.ShapeDtypeStruct((s, 2 * h), BF16), jax.ShapeDtypeStruct((s, d), BF16),
                   jax.ShapeDtypeStruct((1, h), F32), jax.ShapeDtypeStruct((1, h), F32)],
        compiler_params=_params(("arbitrary",)),
    )(dx, yr, ya, nr, na, wout)


def _shift_down(xv, s, prev8):
    rolled = pltpu.roll(xv, s, 0)
    row8 = lax.broadcasted_iota(jnp.int32, prev8.shape, 0)
    head = jnp.where(row8 < s, pltpu.roll(prev8, s, 0), rolled[0:8, :])
    return jnp.concatenate([head, rolled[8:, :]], axis=0)


def _shift_up(xv, s, next8):
    n = xv.shape[0]
    rolled = pltpu.roll(xv, n - s, 0)
    row8 = lax.broadcasted_iota(jnp.int32, next8.shape, 0)
    tail = jnp.where(row8 >= 8 - s, pltpu.roll(next8, 8 - s, 0), rolled[n - 8:, :])
    return jnp.concatenate([rolled[:n - 8, :], tail], axis=0)


def _scan_fwd(a, b):
    n = a.shape[0]
    row = lax.broadcasted_iota(jnp.int32, a.shape, 0)
    s = 1
    while s < n:
        ok = row >= s
        b = jnp.where(ok, a * pltpu.roll(b, s, 0) + b, b)
        a = jnp.where(ok, a * pltpu.roll(a, s, 0), a)
        s *= 2
    return b


def _scan_bwd(a, b):
    n = a.shape[0]
    row = lax.broadcasted_iota(jnp.int32, a.shape, 0)
    s = 1
    while s < n:
        ok = row < n - s
        b = jnp.where(ok, a * pltpu.roll(b, n - s, 0) + b, b)
        a = jnp.where(ok, a * pltpu.roll(a, n - s, 0), a)
        s *= 2
    return b


def _rglru_gates(xv, prev8, cw_ref, cb_ref, wa_ref, ba_ref, wx_ref, bx_ref, lam_ref):
    x1 = _shift_down(xv, 1, prev8)
    x2 = _shift_down(xv, 2, prev8)
    x3 = _shift_down(xv, 3, prev8)
    xc = cw_ref[3:4, :] * xv + cw_ref[2:3, :] * x1 + cw_ref[1:2, :] * x2 + cw_ref[0:1, :] * x3 + cb_ref[...]
    xcb = xc.astype(BF16)
    r = _sigmoid(_dot(xcb, wa_ref[...]) + ba_ref[...])
    ig = _sigmoid(_dot(xcb, wx_ref[...]) + bx_ref[...])
    c = RG_C * _log_sigmoid(lam_ref[...])
    la = r * c
    a = jnp.exp(la)
    m = jnp.sqrt(-_expm1_neg(2.0 * la))
    return (x1, x2, x3), xc, xcb, r, ig, c, a, m


def _rglru_fwd(proj, cw, cb, wa, ba, wx, bx, lam):
    s = proj.shape[0]
    w = D_RNN
    tm = _tile(s)

    def body(xr_ref, gate_ref, cw_ref, cb_ref, wa_ref, ba_ref, wx_ref, bx_ref, lam_ref,
             y_ref, h_ref, prev, hlast):
        @pl.when(pl.program_id(0) == 0)
        def _():
            prev[...] = jnp.zeros_like(prev)
            hlast[...] = jnp.zeros_like(hlast)

        xv = xr_ref[...]
        _, xc, _, _, ig, _, a, m = _rglru_gates(xv, prev[...], cw_ref, cb_ref, wa_ref, ba_ref,
                                                wx_ref, bx_ref, lam_ref)
        b = m * (ig * xc)
        row = lax.broadcasted_iota(jnp.int32, b.shape, 0)
        b = jnp.where(row == 0, b + a * hlast[...], b)
        h = _scan_fwd(a, b)
        h_ref[...] = h
        y_ref[...] = h * _gelu(gate_ref[...])
        prev[...] = xv[tm - 8:, :]
        hlast[...] = h[tm - 1:tm, :]

    vec = pl.BlockSpec((1, w), lambda i: (0, 0))
    sq = pl.BlockSpec((w, w), lambda i: (0, 0))
    out = pl.BlockSpec((tm, w), lambda i: (i, 0))
    return pl.pallas_call(
        body, name="rglru_fwd", grid=(s // tm,),
        in_specs=[pl.BlockSpec((tm, w), lambda i: (i, 0)), pl.BlockSpec((tm, w), lambda i: (i, 1)),
                  pl.BlockSpec((CONV_W, w), lambda i: (0, 0)), vec, sq, vec, sq, vec, vec],
        out_specs=[out, out],
        out_shape=[jax.ShapeDtypeStruct((s, w), F32), jax.ShapeDtypeStruct((s, w), F32)],
        scratch_shapes=[pltpu.VMEM((8, w), F32), pltpu.VMEM((1, w), F32)],
        compiler_params=_params(("arbitrary",)),
    )(proj, proj, cw, cb, wa, ba, wx, bx, lam)


def _rglru_bwd(proj, hseq, dyr, cw, cb, wa, ba, wx, bx, lam):
    s = proj.shape[0]
    w = D_RNN
    tm = _tile(s)
    nt = s // tm
    t8 = tm // 8

    def body(xr_ref, xp_ref, gate_ref, h_ref, hp_ref, dy_ref, cw_ref, cb_ref, wa_ref, ba_ref,
             wx_ref, bx_ref, lam_ref,
             dxr_ref, dgate_ref, dcw_ref, dcb_ref, dwa_ref, dba_ref, dwx_ref, dbx_ref, dlam_ref,
             carry, dxc_next):
        i = pl.program_id(0)
        first_tile = i == nt - 1

        @pl.when(i == 0)
        def _():
            carry[...] = jnp.zeros_like(carry)
            dxc_next[...] = jnp.zeros_like(dxc_next)
            for ref in (dcw_ref, dcb_ref, dwa_ref, dba_ref, dwx_ref, dbx_ref, dlam_ref):
                ref[...] = jnp.zeros_like(ref)

        xv = xr_ref[...]
        prev8 = jnp.where(first_tile, 0.0, xp_ref[...])
        hprev8 = jnp.where(first_tile, 0.0, hp_ref[...])
        (x1, x2, x3), xc, xcb, r, ig, c, a, m = _rglru_gates(
            xv, prev8, cw_ref, cb_ref, wa_ref, ba_ref, wx_ref, bx_ref, lam_ref)
        gv = gate_ref[...]
        hv = h_ref[...]
        dy = dy_ref[...]
        dgate_ref[...] = (dy * hv * _gelu_grad(gv)).astype(BF16)
        dh = dy * _gelu(gv)
        row = lax.broadcasted_iota(jnp.int32, dh.shape, 0)
        dh = jnp.where(row == tm - 1, dh + carry[...], dh)
        a_up = jnp.where(row == tm - 1, 0.0, pltpu.roll(a, tm - 1, 0))
        lam_t = _scan_bwd(a_up, dh)
        carry[...] = a[0:1, :] * lam_t[0:1, :]
        hm1 = _shift_down(hv, 1, hprev8)
        da = lam_t * hm1
        ixc = ig * xc
        dm = lam_t * ixc
        dig = lam_t * m * xc
        dxc = lam_t * m * ig
        dla = da * a - dm * (a * a) / m
        dr = dla * c
        dlam_ref[...] += jnp.sum(dla * r, axis=0, keepdims=True)
        dpa = dr * r * (1.0 - r)
        dpi = dig * ig * (1.0 - ig)
        dba_ref[...] += jnp.sum(dpa, axis=0, keepdims=True)
        dbx_ref[...] += jnp.sum(dpi, axis=0, keepdims=True)
        dpab = dpa.astype(BF16)
        dpib = dpi.astype(BF16)
        dwa_ref[...] += _dot_tn(xcb, dpab)
        dwx_ref[...] += _dot_tn(xcb, dpib)
        dxc = dxc + _dot_nt(dpab, wa_ref[...]) + _dot_nt(dpib, wx_ref[...])
        dcb_ref[...] += jnp.sum(dxc, axis=0, keepdims=True)
        dcw_ref[3:4, :] += jnp.sum(dxc * xv, axis=0, keepdims=True)
        dcw_ref[2:3, :] += jnp.sum(dxc * x1, axis=0, keepdims=True)
        dcw_ref[1:2, :] += jnp.sum(dxc * x2, axis=0, keepdims=True)
        dcw_ref[0:1, :] += jnp.sum(dxc * x3, axis=0, keepdims=True)
        nxt = dxc_next[...]
        dxr = (cw_ref[3:4, :] * dxc + cw_ref[2:3, :] * _shift_up(dxc, 1, nxt)
               + cw_ref[1:2, :] * _shift_up(dxc, 2, nxt) + cw_ref[0:1, :] * _shift_up(dxc, 3, nxt))
        dxr_ref[...] = dxr.astype(BF16)
        dxc_next[...] = dxc[0:8, :]

        @pl.when(first_tile)
        def _():
            lv = lam_ref[...]
            dlam_ref[...] = dlam_ref[...] * (RG_C * _sigmoid(-lv))

    rev = lambda i: nt - 1 - i
    vec = pl.BlockSpec((1, w), lambda i: (0, 0))
    sq = pl.BlockSpec((w, w), lambda i: (0, 0))
    cur = lambda col: pl.BlockSpec((tm, w), lambda i: (rev(i), col))
    before = lambda cols: pl.BlockSpec((8, w), lambda i: (jnp.maximum(rev(i) * t8 - 1, 0), 0))
    return pl.pallas_call(
        body, name="rglru_bwd", grid=(nt,),
        in_specs=[cur(0), before(None), cur(1), cur(0), before(None), cur(0),
                  pl.BlockSpec((CONV_W, w), lambda i: (0, 0)), vec, sq, vec, sq, vec, vec],
        out_specs=[cur(0), cur(0), pl.BlockSpec((CONV_W, w), lambda i: (0, 0)), vec, sq, vec, sq, vec, vec],
        out_shape=[jax.ShapeDtypeStruct((s, w), BF16), jax.ShapeDtypeStruct((s, w), BF16),
                   jax.ShapeDtypeStruct((CONV_W, w), F32), jax.ShapeDtypeStruct((1, w), F32),
                   jax.ShapeDtypeStruct((w, w), F32), jax.ShapeDtypeStruct((1, w), F32),
                   jax.ShapeDtypeStruct((w, w), F32), jax.ShapeDtypeStruct((1, w), F32),
                   jax.ShapeDtypeStruct((1, w), F32)],
        scratch_shapes=[pltpu.VMEM((1, w), F32), pltpu.VMEM((8, w), F32)],
        compiler_params=_params(("arbitrary",)),
    )(proj, proj, proj, hseq, hseq, dyr, cw, cb, wa, ba, wx, bx, lam)


def _sb_logs(z, valid):
    l1p = jnp.log(1.0 + jnp.exp(-jnp.abs(z)))
    lb = jnp.minimum(z, 0.0) - l1p
    lm = jnp.where(valid, -jnp.maximum(z, 0.0) - l1p, 0.0)
    return lb, lm


class _Window:
    def __init__(self):
        blk, win, cut = ATT_BLOCK, ATT_WINDOW, ATT_SPLIT
        self.row = lax.broadcasted_iota(jnp.int32, (blk, win), 0)
        self.col = lax.broadcasted_iota(jnp.int32, (blk, win), 1)

        def tri(n, later):
            j = lax.broadcasted_iota(jnp.int32, (n, n), 0)
            s = lax.broadcasted_iota(jnp.int32, (n, n), 1)
            return jnp.where((j > s) if later else (j < s), 1.0, 0.0).astype(BF16)

        self.later = (tri(cut, True), tri(win - cut, True))
        self.earlier = (tri(cut, False), tri(win - cut, False))

    def place(self, qi, g):
        end = (qi + 1) * ATT_BLOCK - g * ATT_WINDOW
        start = pl.multiple_of(jnp.maximum(end - ATT_WINDOW, 0), ATT_BLOCK)
        valid = start + self.col < jnp.minimum(qi * ATT_BLOCK + self.row, end)
        return start, valid

    @staticmethod
    def _parts(xv):
        hi = xv.astype(BF16)
        lo = (xv - hi.astype(F32)).astype(BF16)
        cut = ATT_SPLIT
        sums = (jnp.sum(xv[:, :cut], axis=1, keepdims=True), jnp.sum(xv[:, cut:], axis=1, keepdims=True))
        return (hi[:, :cut], lo[:, :cut]), (hi[:, cut:], lo[:, cut:]), sums

    def sums_after(self, xv, carry):
        (h0, l0), (h1, l1), (s0, s1) = self._parts(xv)
        first = _dot(h0, self.later[0]) + _dot(l0, self.later[0]) + (s1 + carry)
        last = _dot(h1, self.later[1]) + _dot(l1, self.later[1]) + carry
        return jnp.concatenate([first, last], axis=1), s0 + s1

    def sums_before(self, xv, carry):
        (h0, l0), (h1, l1), (s0, s1) = self._parts(xv)
        first = _dot(h0, self.earlier[0]) + _dot(l0, self.earlier[0]) + carry
        last = _dot(h1, self.earlier[1]) + _dot(l1, self.earlier[1]) + (s0 + carry)
        return jnp.concatenate([first, last], axis=1), s0 + s1


def _head_lanes(hh):
    return slice(hh * HEAD_DIM, (hh + 1) * HEAD_DIM)


def _attn_fwd(proj, qg, kg, rider=None):
    s = proj.shape[0]
    blk, win, dh = ATT_BLOCK, ATT_WINDOW, HEAD_DIM
    nq = s // blk
    scale = 1.0 / math.sqrt(dh)
    heads = (0, 1)
    assert s >= win and s % blk == 0

    def body(*refs):
        (q_ref, k_ref, v_ref, qg_ref, kg_ref), (o_ref,), (qn, kn, vb, ob), copies = _split_refs(refs, 5, 1, rider)
        finish = _ride(copies, pl.program_id(0) == 0, pl.program_id(0) == N_HEADS // 2 - 1)
        wd = _Window()
        for h in heads:
            lanes = _head_lanes(h)
            qv = q_ref[:, lanes]
            qn[h] = (qv * _rms_r(qv) * qg_ref[...] * scale).astype(BF16)
            kv = k_ref[:, lanes]
            kn[h] = (kv * _rms_r(kv) * kg_ref[...]).astype(BF16)
            vb[h] = v_ref[:, lanes].astype(BF16)

        def q_step(qi, _):
            qoff = pl.multiple_of(qi * blk, blk)
            qts = [qn[h, pl.ds(qoff, blk), :] for h in heads]

            def more(carry):
                g, live = carry[:2]
                return jnp.logical_and((qi + 1) * blk - g * win > 0, live > 0)

            def window(carry):
                g, _, accs, runs = carry
                start, valid = wd.place(qi, g)
                zs = [_dot_nt(qts[h], kn[h, pl.ds(start, win), :]) for h in heads]
                logs = [_sb_logs(z, valid) for z in zs]
                sums = [wd.sums_after(logs[h][1], runs[h]) for h in heads]
                wgts = [jnp.where(valid, jnp.exp(logs[h][0] + sums[h][0]), 0.0).astype(BF16) for h in heads]
                accs = tuple(accs[h] + _dot(wgts[h], vb[h, pl.ds(start, win), :]) for h in heads)
                runs = tuple(runs[h] + sums[h][1] for h in heads)
                live = (jnp.maximum(jnp.max(runs[0]), jnp.max(runs[1])) > EXP_ZERO).astype(jnp.int32)
                return g + 1, live, accs, runs

            zero = lambda cols: tuple(jnp.zeros((blk, cols), F32) for _ in heads)
            _, _, accs, _ = lax.while_loop(more, window, (jnp.int32(0), jnp.int32(1), zero(dh), zero(1)))
            for h in heads:
                ob[h, pl.ds(qoff, blk), :] = accs[h]
            return 0

        lax.fori_loop(0, nq, q_step, 0)
        for h in heads:
            o_ref[:, _head_lanes(h)] = ob[h]
        finish()

    pair = lambda group: pl.BlockSpec((s, 2 * dh), lambda p: (0, group * (D_ATT // (2 * dh)) + p))
    vec = pl.BlockSpec((1, dh), lambda p: (0, 0))
    return _call(
        body, "attn_fwd", (N_HEADS // 2,), [pair(2), pair(3), pair(4), vec, vec], [pair(0)],
        [jax.ShapeDtypeStruct((s, D_ATT), F32)], [proj, proj, proj, qg, kg],
        scratch=[pltpu.VMEM((2, s, dh), BF16)] * 3 + [pltpu.VMEM((2, s, dh), F32)], rider=rider)


def _attn_bwd(proj, dya, qg, kg, rider=None):
    s = proj.shape[0]
    blk, win, dh = ATT_BLOCK, ATT_WINDOW, HEAD_DIM
    nq = s // blk
    max_windows = -(-s // win) + 1
    scale = 1.0 / math.sqrt(dh)
    steps = N_HEADS // 2
    heads = (0, 1)
    assert s >= win and s % blk == 0

    def body(*refs):
        ins, outs, scratch, copies = _split_refs(refs, 6, 5, rider)
        q_ref, k_ref, v_ref, do_ref, qg_ref, kg_ref = ins
        dq_ref, dk_ref, dv_ref, dqg_ref, dkg_ref = outs
        qn, kn, vb, dob, runs_ref, dqn, dkn, dvn = scratch
        finish = _ride(copies, pl.program_id(0) == 0, pl.program_id(0) == steps - 1)
        wd = _Window()

        @pl.when(pl.program_id(0) == 0)
        def _():
            dqg_ref[...] = jnp.zeros_like(dqg_ref)
            dkg_ref[...] = jnp.zeros_like(dkg_ref)

        for h in heads:
            lanes = _head_lanes(h)
            qv = q_ref[:, lanes]
            qn[h] = (qv * _rms_r(qv) * qg_ref[...] * scale).astype(BF16)
            kv = k_ref[:, lanes]
            kn[h] = (kv * _rms_r(kv) * kg_ref[...]).astype(BF16)
            vb[h] = v_ref[:, lanes].astype(BF16)
            dob[h] = do_ref[:, lanes].astype(BF16)
        dkn[...] = jnp.zeros_like(dkn)
        dvn[...] = jnp.zeros_like(dvn)

        def q_step(qi, _):
            qoff = pl.multiple_of(qi * blk, blk)
            qts = [qn[h, pl.ds(qoff, blk), :] for h in heads]
            dots = [dob[h, pl.ds(qoff, blk), :] for h in heads]

            def more(carry):
                g, live = carry[:2]
                return jnp.logical_and((qi + 1) * blk - g * win > 0, live > 0)

            def run_window(carry):
                g, _, runs = carry
                start, valid = wd.place(qi, g)
                for h in heads:
                    runs_ref[h, g] = runs[h]
                zs = [_dot_nt(qts[h], kn[h, pl.ds(start, win), :]) for h in heads]
                runs = tuple(runs[h] + jnp.sum(_sb_logs(zs[h], valid)[1], axis=1, keepdims=True) for h in heads)
                live = (jnp.maximum(jnp.max(runs[0]), jnp.max(runs[1])) > EXP_ZERO).astype(jnp.int32)
                return g + 1, live, runs

            zero = lambda cols: tuple(jnp.zeros((blk, cols), F32) for _ in heads)
            windows, _, _ = lax.while_loop(more, run_window, (jnp.int32(0), jnp.int32(1), zero(1)))

            def k_window(gg, carry):
                dq_accs, esums = carry
                g = windows - 1 - gg
                start, valid = wd.place(qi, g)
                kts = [kn[h, pl.ds(start, win), :] for h in heads]
                vts = [vb[h, pl.ds(start, win), :] for h in heads]
                zs = [_dot_nt(qts[h], kts[h]) for h in heads]
                dws = [_dot_nt(dots[h], vts[h]) for h in heads]
                logs = [_sb_logs(z, valid) for z in zs]
                tails = [wd.sums_after(logs[h][1], runs_ref[h, g])[0] for h in heads]
                wgts = [jnp.where(valid, jnp.exp(logs[h][0] + tails[h]), 0.0) for h in heads]
                es = [dws[h] * wgts[h] for h in heads]
                befores = [wd.sums_before(es[h], esums[h]) for h in heads]
                dzbs = []
                for h in heads:
                    beta = jnp.exp(logs[h][0])
                    dz = jnp.where(valid, es[h] * (1.0 - beta) - befores[h][0] * beta, 0.0)
                    dzbs.append(dz.astype(BF16))
                dq_accs = tuple(dq_accs[h] + _dot(dzbs[h], kts[h]) for h in heads)
                for h in heads:
                    dkn[h, pl.ds(start, win), :] += _dot_tn(dzbs[h], qts[h])
                    dvn[h, pl.ds(start, win), :] += _dot_tn(wgts[h].astype(BF16), dots[h])
                return dq_accs, tuple(esums[h] + befores[h][1] for h in heads)

            dq_accs, _ = lax.fori_loop(0, windows, k_window, (zero(dh), zero(1)))
            for h in heads:
                dqn[h, pl.ds(qoff, blk), :] = dq_accs[h]
            return 0

        lax.fori_loop(0, nq, q_step, 0)

        for h in heads:
            lanes = _head_lanes(h)
            qv = q_ref[:, lanes]
            dq, dqg = _rms_bwd(qv, _rms_r(qv), qg_ref[...] * scale, dqn[h])
            dq_ref[:, lanes] = dq.astype(BF16)
            dqg_ref[...] += dqg * scale
            kv = k_ref[:, lanes]
            dk, dkg = _rms_bwd(kv, _rms_r(kv), kg_ref[...], dkn[h])
            dk_ref[:, lanes] = dk.astype(BF16)
            dkg_ref[...] += dkg
            dv_ref[:, lanes] = dvn[h].astype(BF16)
        finish()

    pair = lambda group: pl.BlockSpec((s, 2 * dh), lambda p: (0, group * (D_ATT // (2 * dh)) + p))
    vec = pl.BlockSpec((1, dh), lambda p: (0, 0))
    return _call(
        body, "attn_bwd", (steps,), [pair(2), pair(3), pair(4), pair(0), vec, vec],
        [pair(0), pair(0), pair(0), vec, vec],
        [jax.ShapeDtypeStruct((s, D_ATT), BF16)] * 3 + [jax.ShapeDtypeStruct((1, dh), F32)] * 2,
        [proj, proj, proj, dya, qg, kg],
        scratch=[pltpu.VMEM((2, s, dh), BF16)] * 4 + [pltpu.VMEM((2, max_windows, blk, 1), F32)]
        + [pltpu.VMEM((2, s, dh), F32)] * 3, rider=rider)


def _block_diag(w):
    n, c, d = w.shape
    return jnp.einsum("ncd,nm->ncmd", w, jnp.eye(n, dtype=w.dtype)).reshape(n * c, n * d)


def _diag_blocks(full, n):
    c = full.shape[0] // n
    return jnp.stack([full[i * c:(i + 1) * c, i * c:(i + 1) * c] for i in range(n)])


FFN1 = ["ffn1_w_gate", "ffn1_w_up", "ffn1_w_down"]
FFN2 = ["ffn2_w_gate", "ffn2_w_up", "ffn2_w_down"]
MIXER = ["w_in", "w_out"]


def _pair_sums(gb, names, where):
    theirs = _pair_exchange([gb[n] for n in names], "pair_exchange_" + names[0])
    pair, own = zip(*[_pair_sum(gb[n], t, where, "pair_sum_" + n) for n, t in zip(names, theirs)])
    return _chip_rider(list(pair), list(own))


def _local_step(x, tgt, stacks, conv_stack, small, where):
    big = dict(zip(FFN1, _gather_weights([stacks[n] for n in FFN1], [])))
    wa = _block_diag(small["rg_w_a"]).astype(BF16)
    wx = _block_diag(small["rg_w_x"]).astype(BF16)

    x1, g1, u1, hb1, ab1, *landed = _ffn_fwd(x, small["ffn1_norm"], *[big[n] for n in FFN1],
                                             rider=_gather_rider([stacks[n] for n in MIXER], [conv_stack]))
    big.update(zip(MIXER, _forward_weights(landed[:len(MIXER)], "forward_mixer_weights")))
    conv_w = jnp.transpose(landed[-1], (1, 0, 2)).reshape(CONV_W, D_RNN)
    wout = big["w_out"].reshape(D_MODEL, D_MODEL)
    rg = (conv_w, small["conv_b"], wa, small["rg_b_a"], wx, small["rg_b_x"], small["rg_lambda"])
    proj, hb2 = _mix_pre(x1, small["mix_norm"], big["w_in"])
    yr, hseq = _rglru_fwd(proj, *rg)
    ya, *landed = _attn_fwd(proj, small["q_norm"], small["k_norm"], _gather_rider([stacks[n] for n in FFN2], []))
    big.update(zip(FFN2, _forward_weights(landed, "forward_ffn2_weights")))
    x2 = _mix_post(x1, yr, ya, small["rnn_out_norm"], small["attn_out_norm"], wout)
    dx3, g2, u2, hb3, ab3, loss = _ffn_fwd(x2, small["ffn2_norm"], *[big[n] for n in FFN2], tgt)

    gb, gs, slots = {}, {}, {}
    dx2, dg2, du2, dyb2, gs["ffn2_norm"] = _ffn_bwd_act(x2, small["ffn2_norm"], dx3, g2, u2, *[big[n] for n in FFN2],
                                                        "ffn2_bwd")
    gb["ffn2_w_gate"] = _ffn_wgrad(dg2, hb3, 1.0, "wgrad_gate_ffn2")
    gb["ffn2_w_up"] = _ffn_wgrad(du2, hb3, 1.0, "wgrad_up_ffn2")
    gb["ffn2_w_down"] = _ffn_wgrad(ab3, dyb2, 0.5, "wgrad_down_ffn2")
    dyr, dya, ycat, dxb2, gs["rnn_out_norm"], gs["attn_out_norm"] = _mix_post_bwd(
        dx2, yr, ya, small["rnn_out_norm"], small["attn_out_norm"], wout)
    gb["w_out"] = _wgrad_whole(ycat, dxb2, False, "wgrad_out")
    early = FFN2 + ["w_out"]
    dq, dk, dv, gs["q_norm"], gs["k_norm"], *done = _attn_bwd(
        proj, dya, small["q_norm"], small["k_norm"], _pair_sums(gb, early, where))
    slots.update(zip(early, done))
    dxr, dgate, gs["conv_w"], gs["conv_b"], dwa, gs["rg_b_a"], dwx, gs["rg_b_x"], gs["rg_lambda"] = _rglru_bwd(
        proj, hseq, dyr, *rg)
    gs["rg_w_a"] = _diag_blocks(dwa, RNN_BLOCKS)
    gs["rg_w_x"] = _diag_blocks(dwx, RNN_BLOCKS)
    dpb = jnp.concatenate([dxr, dgate, dq, dk, dv], axis=1)
    dx1, gs["mix_norm"] = _mix_pre_bwd(x1, small["mix_norm"], dx2, dpb, big["w_in"])
    dx0, dg1, du1, dyb1, gs["ffn1_norm"] = _ffn_bwd_act(x, small["ffn1_norm"], dx1, g1, u1, *[big[n] for n in FFN1],
                                                        "ffn1_bwd")

    mine = _place_shard(_pack([gs[n] for n in SMALL]), where, F32, "place_small_grads", by_device=True)
    gb["ffn1_w_gate"], everyone = _ffn_wgrad(dg1, hb1, 1.0, "wgrad_gate_ffn1", _small_rider(mine))
    gb["ffn1_w_up"], slots["ffn1_w_gate"] = _ffn_wgrad(
        du1, hb1, 1.0, "wgrad_up_ffn1", _pair_sums(gb, ["ffn1_w_gate"], where))
    gb["ffn1_w_down"], slots["ffn1_w_up"] = _ffn_wgrad(
        ab1, dyb1, 0.5, "wgrad_down_ffn1", _pair_sums(gb, ["ffn1_w_up"], where))
    gb["w_in"], slots["ffn1_w_down"] = _wgrad_whole(
        hb2, dpb, True, "wgrad_in", _pair_sums(gb, ["ffn1_w_down"], where))
    last = _pair_sums(gb, ["w_in"], where)
    slots["w_in"], = _chip_exchange(last.plain, last.inplace)
    return loss[0, 0], dx0, slots, gs, everyone


ANY = pl.BlockSpec(memory_space=pl.ANY)


def _place():
    x, y, c = lax.axis_index("x"), lax.axis_index("y"), lax.axis_index("c")
    other_chips = [(1 - x, y), (x, 1 - y), (1 - x, 1 - y)]
    return x, y, c, 2 * x + y, other_chips


def _remote(src, dst, send_sem, recv_sem, to):
    return pltpu.make_async_remote_copy(src_ref=src, dst_ref=dst, send_sem=send_sem, recv_sem=recv_sem,
                                        device_id=to, device_id_type=MESH)


def _copy_plan(pairs):
    sends = [functools.partial(_remote, *a) for a, _ in pairs]
    arrivals = [functools.partial(_remote, *b) for _, b in pairs]
    return sends, arrivals


class _Rider:
    def __init__(self, plan, plain, inplace, n_copies=None):
        self.plan, self.plain, self.inplace = plan, list(plain), list(inplace)
        self.n_copies = n_copies or 3 * len(self.inplace)

    def operands(self):
        return self.plain + self.inplace

    def out_shape(self):
        return [jax.ShapeDtypeStruct(a.shape, a.dtype) for a in self.inplace]

    def aliases(self, inputs_before, outputs_before):
        return {inputs_before + len(self.plain) + k: outputs_before + k for k in range(len(self.inplace))}

    def scratch(self):
        return [pltpu.SemaphoreType.DMA((self.n_copies,))] * 2


def _split_refs(refs, n_in, n_out, rider):
    if rider is None:
        return refs[:n_in], refs[n_in:n_in + n_out], refs[n_in + n_out:], None
    r_in, r_out = len(rider.operands()), len(rider.inplace)
    outs_at = n_in + r_in
    rest = refs[outs_at + n_out + r_out:]
    copies = functools.partial(rider.plan, refs[n_in:n_in + len(rider.plain)],
                               refs[outs_at + n_out:outs_at + n_out + r_out], *rest[-2:])
    return refs[:n_in], refs[outs_at:outs_at + n_out], rest[:-2], copies


def _ride(copies, first, last):
    if copies is None:
        return lambda: None

    @pl.when(first)
    def _():
        _start(copies()[0])

    def finish():
        @pl.when(last)
        def _():
            _finish(*copies())

    return finish


def _gather_rider(split, whole):
    n_split = len(split)
    return _Rider(lambda plain, stacks, ss, rs: _gather_ici(stacks, n_split, ss, rs), [], list(split) + list(whole))


def _chip_rider(sums, slots):
    return _Rider(_chip_copies, sums, slots)


def _start(makers):
    for make in makers:
        make().start()


def _finish(sends, arrivals):
    for make in arrivals:
        make().wait_recv()
    for make in sends:
        make().wait_send()


def _half(rows, c):
    return pl.ds(pl.multiple_of(c * rows, 16), rows)


def _gather_weights(split, whole):
    arrs = list(split) + list(whole)
    n, ns = len(arrs), len(split)

    def body(*refs):
        outs = refs[n:2 * n]
        send_sems, recv_sems, fsend_sems, frecv_sems = refs[2 * n:]
        sends, arrivals = _gather_ici(outs, ns, send_sems, recv_sems)
        passes, passed = _gather_d2d(outs[:ns], fsend_sems, frecv_sems)
        _start(sends)
        for k, make in enumerate(arrivals):
            make().wait_recv()
            if k < 3 * ns:
                passes[k]().start()
        _finish(sends + passes, passed)

    return pl.pallas_call(
        body, name="gather_weights",
        in_specs=[ANY] * n, out_specs=[ANY] * n,
        out_shape=[jax.ShapeDtypeStruct(a.shape, a.dtype) for a in arrs],
        input_output_aliases={i: i for i in range(n)},
        scratch_shapes=[pltpu.SemaphoreType.DMA((3 * n,)), pltpu.SemaphoreType.DMA((3 * n,)),
                        pltpu.SemaphoreType.DMA((3 * ns,)), pltpu.SemaphoreType.DMA((3 * ns,))],
    )(*arrs)


def _gather_ici(stacks, n_split, send_sems, recv_sems):
    x, y, c, me, chips = _place()

    def region(i, chip):
        if i < n_split:
            return stacks[i].at[chip, _half(stacks[i].shape[1] // 2, c)]
        return stacks[i].at[chip]

    pairs = []
    for i in range(len(stacks)):
        for p, (cx, cy) in enumerate(chips):
            k = 3 * i + p
            mine, got = region(i, me), region(i, 2 * cx + cy)
            sems, to = (send_sems.at[k], recv_sems.at[k]), (cx, cy, c)
            pairs.append(((mine, mine, *sems, to), (got, got, *sems, to)))
    return _copy_plan(pairs)


def _gather_d2d(stacks, send_sems, recv_sems):
    x, y, c, _, chips = _place()
    sibling = (x, y, 1 - c)
    pairs = []
    for i, stack in enumerate(stacks):
        rows = stack.shape[1] // 2
        for p, (cx, cy) in enumerate(chips):
            k = 3 * i + p
            got, theirs = stack.at[2 * cx + cy, _half(rows, c)], stack.at[2 * cx + cy, _half(rows, 1 - c)]
            sems = (send_sems.at[k], recv_sems.at[k])
            pairs.append(((got, got, *sems, sibling), (theirs, theirs, *sems, sibling)))
    return _copy_plan(pairs)


def _forward_weights(split, name):
    n = len(split)

    def body(*refs):
        sends, arrivals = _gather_d2d(refs[n:2 * n], *refs[2 * n:])
        _start(sends)
        _finish(sends, arrivals)

    return pl.pallas_call(
        body, name=name,
        in_specs=[ANY] * n, out_specs=[ANY] * n,
        out_shape=[jax.ShapeDtypeStruct(a.shape, a.dtype) for a in split],
        input_output_aliases={i: i for i in range(n)},
        scratch_shapes=[pltpu.SemaphoreType.DMA((3 * n,))] * 2,
    )(*split)


def _pair_exchange(grads, name):
    n = len(grads)

    def body(*refs):
        ins, theirs = refs[:n], refs[n:2 * n]
        send_sems, recv_sems = refs[2 * n:]
        x, y, c, _, _ = _place()
        sibling = (x, y, 1 - c)
        sends = [_remote(ins[k].at[:, _half(grads[k].shape[1] // 2, 1 - c)], theirs[k],
                         send_sems.at[k], recv_sems.at[k], sibling) for k in range(n)]
        for cp in sends:
            cp.start()
        for k in range(n):
            _remote(theirs[k], theirs[k], send_sems.at[k], recv_sems.at[k], sibling).wait_recv()
        for cp in sends:
            cp.wait_send()

    return pl.pallas_call(
        body, name=name,
        in_specs=[ANY] * n, out_specs=[ANY] * n,
        out_shape=[jax.ShapeDtypeStruct((g.shape[0], g.shape[1] // 2, g.shape[2]), g.dtype) for g in grads],
        scratch_shapes=[pltpu.SemaphoreType.DMA((n,))] * 2,
    )(*grads)


def _chip_exchange(sums, slots):
    n = len(sums)

    def body(*refs):
        sends, arrivals = _chip_copies(refs[:n], refs[2 * n:3 * n], *refs[3 * n:])
        _start(sends)
        _finish(sends, arrivals)

    return pl.pallas_call(
        body, name="grad_chip_exchange",
        in_specs=[ANY] * (2 * n), out_specs=[ANY] * n,
        out_shape=[jax.ShapeDtypeStruct(a.shape, a.dtype) for a in slots],
        input_output_aliases={n + k: k for k in range(n)},
        scratch_shapes=[pltpu.SemaphoreType.DMA((3 * n,)), pltpu.SemaphoreType.DMA((3 * n,))],
    )(*sums, *slots)


def _chip_copies(sums, slots, send_sems, recv_sems):
    x, y, c, me, chips = _place()
    pairs = []
    for k in range(len(sums)):
        for p, (cx, cy) in enumerate(chips):
            j = 3 * k + p
            got = slots[k].at[2 * cx + cy]
            sems, to = (send_sems.at[j], recv_sems.at[j]), (cx, cy, c)
            pairs.append(((sums[k].at[2 * cx + cy], slots[k].at[me], *sems, to), (got, got, *sems, to)))
    return _copy_plan(pairs)


def _half_swap(halves):
    n = len(halves)

    def body(*refs):
        outs = refs[n:2 * n]
        send_sems, recv_sems = refs[2 * n:]
        x, y, c, _, _ = _place()
        sibling = (x, y, 1 - c)
        sends = [_remote(outs[k].at[c], outs[k].at[c], send_sems.at[k], recv_sems.at[k], sibling) for k in range(n)]
        for cp in sends:
            cp.start()
        for k in range(n):
            got = outs[k].at[1 - c]
            _remote(got, got, send_sems.at[k], recv_sems.at[k], sibling).wait_recv()
        for cp in sends:
            cp.wait_send()

    return pl.pallas_call(
        body, name="grad_half_swap",
        in_specs=[ANY] * n, out_specs=[ANY] * n,
        out_shape=[jax.ShapeDtypeStruct(a.shape, a.dtype) for a in halves],
        input_output_aliases={k: k for k in range(n)},
        scratch_shapes=[pltpu.SemaphoreType.DMA((n,))] * 2,
    )(*halves)


def _small_rider(stack):
    n_dev = 2 * N_CHIPS

    def plan(_, stacks, send_sems, recv_sems):
        x, y, c, _, _ = _place()
        mine = stacks[0].at[4 * x + 2 * y + c]
        pairs = []
        for k in range(1, n_dev):
            px, py, pc = x ^ ((k >> 2) & 1), y ^ ((k >> 1) & 1), c ^ (k & 1)
            got = stacks[0].at[4 * px + 2 * py + pc]
            sems = (send_sems.at[k - 1], recv_sems.at[k - 1])
            pairs.append(((mine, mine, *sems, (px, py, pc)), (got, got, *sems, (px, py, pc))))
        return _copy_plan(pairs)

    return _Rider(plan, [], [stack], n_dev - 1)


def _row_tile(r):
    return r // 4 if r >= 256 and (r // 4) % 16 == 0 else r


def _prefetch_call(body, name, grid, in_specs, out_specs, out_shape):
    spec = pltpu.PrefetchScalarGridSpec(num_scalar_prefetch=1, grid=grid, in_specs=in_specs, out_specs=out_specs)
    return pl.pallas_call(body, name=name, grid_spec=spec, out_shape=out_shape,
                          compiler_params=_params(("arbitrary",) * len(grid)))


def _place_shard(w2d, where, dtype, name, by_device=False):
    r, c = w2d.shape
    tr = _row_tile(r)
    slots = 2 * N_CHIPS if by_device else N_CHIPS
    slot = (lambda s: 2 * s[1] + s[0]) if by_device else (lambda s: s[1])

    def body(where_ref, w_ref, out_ref):
        out_ref[...] = w_ref[...].astype(dtype)

    return _prefetch_call(
        body, name, (r // tr,), [pl.BlockSpec((tr, c), lambda i, s: (i, 0))],
        pl.BlockSpec((None, tr, c), lambda i, s: (slot(s), i, 0)),
        jax.ShapeDtypeStruct((slots, r, c), dtype))(where, w2d)


def _pair_sum(full, theirs, where, name):
    nb, hs, c = theirs.shape

    def body(where_ref, a_ref, b_ref, out_ref, own_ref):
        total = (a_ref[...].astype(F32) + b_ref[...].astype(F32)).astype(BF16)
        out_ref[...] = total

        @pl.when(pl.program_id(0) == where_ref[1])
        def _():
            own_ref[...] = total

    blk = pl.BlockSpec((None, hs, c), lambda j, s: (j, 0, 0))
    shape = jax.ShapeDtypeStruct(theirs.shape, BF16)
    return _prefetch_call(
        body, name, (nb,), [pl.BlockSpec((None, hs, c), lambda j, s: (j, s[0], 0)), blk],
        [blk, pl.BlockSpec((None, hs, c), lambda j, s: (s[1], 0, 0))], [shape, shape])(where, full, theirs)


def _chip_sum(slots, where, name):
    nb, hs, c = slots.shape
    tr = _row_tile(hs)

    def body(where_ref, a_ref, out_ref):
        total = a_ref[0].astype(F32)
        for j in range(1, nb):
            total = total + a_ref[j].astype(F32)
        out_ref[...] = total

    return _prefetch_call(
        body, name, (hs // tr,), [pl.BlockSpec((nb, tr, c), lambda i, s: (0, i, 0))],
        pl.BlockSpec((None, tr, c), lambda i, s: (s[0], i, 0)),
        jax.ShapeDtypeStruct((2, hs, c), F32))(where, slots)


def _slot_sum(a, name):
    nb, r, c = a.shape
    tr = _row_tile(r)

    def body(a_ref, out_ref):
        total = a_ref[0].astype(F32)
        for j in range(1, nb):
            total = total + a_ref[j].astype(F32)
        out_ref[...] = total

    return pl.pallas_call(
        body, name=name, grid=(r // tr,),
        in_specs=[pl.BlockSpec((nb, tr, c), lambda i: (0, i, 0))],
        out_specs=pl.BlockSpec((tr, c), lambda i: (i, 0)),
        out_shape=jax.ShapeDtypeStruct((r, c), F32), compiler_params=_params(("arbitrary",)),
    )(a)


def _adamw(w, g, m, v, name):
    r, c = w.shape
    tr = _row_tile(r)
    c1 = 1.0 - ADAM_B1 ** ADAM_STEP
    c2 = 1.0 - ADAM_B2 ** ADAM_STEP

    def body(w_ref, g_ref, m_ref, v_ref, d_ref, m2_ref, v2_ref):
        gv = g_ref[...]
        m2 = ADAM_B1 * m_ref[...] + (1.0 - ADAM_B1) * gv
        v2 = ADAM_B2 * v_ref[...] + (1.0 - ADAM_B2) * (gv * gv)
        m2_ref[...] = m2
        v2_ref[...] = v2
        d_ref[...] = -ADAM_LR * ((m2 / c1) / (jnp.sqrt(v2 / c2) + ADAM_EPS) + ADAM_WD * w_ref[...])

    blk = pl.BlockSpec((tr, c), lambda i: (i, 0))
    return pl.pallas_call(
        body, name=name, grid=(r // tr,), in_specs=[blk] * 4, out_specs=[blk] * 3,
        out_shape=[jax.ShapeDtypeStruct((r, c), F32)] * 3, compiler_params=_params(("arbitrary",)),
    )(w, g, m, v)


WEIGHTS = ["ffn1_norm", "ffn1_w_gate", "ffn1_w_up", "ffn1_w_down", "mix_norm", "w_in", "conv_w", "conv_b",
           "rg_w_a", "rg_b_a", "rg_w_x", "rg_b_x", "rg_lambda", "q_norm", "k_norm", "rnn_out_norm",
           "attn_out_norm", "w_out", "ffn2_norm", "ffn2_w_gate", "ffn2_w_up", "ffn2_w_down"]
BIG = ["ffn1_w_gate", "ffn1_w_up", "ffn1_w_down", "w_in", "w_out", "ffn2_w_gate", "ffn2_w_up", "ffn2_w_down"]
SMALL = [n for n in WEIGHTS if n not in BIG]
PACK_LANES = 128
PACK_ROW_ALIGN = 8


def _hidden_major(name, a):
    return jnp.transpose(a) if name.endswith(("w_gate", "w_up")) else a


def _pack(parts):
    flat = jnp.concatenate([p.reshape(-1) for p in parts])
    unit = PACK_LANES * PACK_ROW_ALIGN
    padded = -(-flat.shape[0] // unit) * unit
    return jnp.pad(flat, (0, padded - flat.shape[0])).reshape(-1, PACK_LANES)


def _unpack(packed, shapes):
    flat = packed.reshape(-1)
    out, at = [], 0
    for shp in shapes:
        size = math.prod(shp)
        out.append(flat[at:at + size].reshape(shp))
        at += size
    return out


def kernel(x, ffn1_norm, ffn1_w_gate, ffn1_w_up, ffn1_w_down, mix_norm, w_in, conv_w, conv_b, rg_w_a, rg_b_a, rg_w_x, rg_b_x, rg_lambda, q_norm, k_norm, rnn_out_norm, attn_out_norm, w_out, ffn2_norm, ffn2_w_gate, ffn2_w_up, ffn2_w_down, loss_target, m_ffn1_norm, m_ffn1_w_gate, m_ffn1_w_up, m_ffn1_w_down, m_mix_norm, m_w_in, m_conv_w, m_conv_b, m_rg_w_a, m_rg_b_a, m_rg_w_x, m_rg_b_x, m_rg_lambda, m_q_norm, m_k_norm, m_rnn_out_norm, m_attn_out_norm, m_w_out, m_ffn2_norm, m_ffn2_w_gate, m_ffn2_w_up, m_ffn2_w_down, v_ffn1_norm, v_ffn1_w_gate, v_ffn1_w_up, v_ffn1_w_down, v_mix_norm, v_w_in, v_conv_w, v_conv_b, v_rg_w_a, v_rg_b_a, v_rg_w_x, v_rg_b_x, v_rg_lambda, v_q_norm, v_k_norm, v_rnn_out_norm, v_attn_out_norm, v_w_out, v_ffn2_norm, v_ffn2_w_gate, v_ffn2_w_up, v_ffn2_w_down):
    given = dict(locals())
    w = {n: given[n] for n in WEIGHTS}
    m = {n: given["m_" + n] for n in WEIGHTS}
    v = {n: given["v_" + n] for n in WEIGHTS}
    chip = 2 * lax.axis_index("x") + lax.axis_index("y")

    where = jnp.stack([lax.axis_index("c"), chip]).astype(jnp.int32)

    stacks = {n: _place_shard(_hidden_major(n, w[n][0]), where, BF16, "place_" + n) for n in BIG}
    conv_stack = _place_shard(w["conv_w"][0], where, F32, "place_conv_w")
    small = {n: (w[n][0] if w[n].ndim > 2 else w[n]) for n in SMALL if n != "conv_w"}

    loss, grad_x, slots, gs, everyone = _local_step(x[0], loss_target[0], stacks, conv_stack, small, where)
    loss = lax.psum(loss, ("x", "y", "c"))

    swapped = _half_swap([_chip_sum(slots[n], where, "chip_sum_" + n) for n in BIG])
    grads, deltas, new_m, new_v = {}, {}, {}, {}
    for n, t in zip(BIG, swapped):
        g2 = t.reshape(t.shape[0] * t.shape[1], t.shape[2])
        d2, m2, v2 = _adamw(_hidden_major(n, w[n][0]), g2, _hidden_major(n, m[n][0]), _hidden_major(n, v[n][0]),
                            "adamw_" + n)
        back = lambda a: _hidden_major(n, a).reshape(w[n].shape)
        grads[n], deltas[n], new_m[n], new_v[n] = back(g2), back(d2), back(m2), back(v2)

    full_shapes = [gs[n].shape for n in SMALL]
    g_small = _slot_sum(everyone, "small_grad_sum")
    g_parts = dict(zip(SMALL, _unpack(g_small, full_shapes)))
    quarter = D_RNN // N_CHIPS
    g_parts["conv_w"] = lax.dynamic_slice_in_dim(g_parts["conv_w"], chip * quarter, quarter, axis=1)
    local_shapes = [w[n].shape for n in SMALL]
    pk = lambda tree: _pack([tree[n] for n in SMALL])
    d_s, m_s, v_s = _adamw(pk(w), pk(g_parts), pk(m), pk(v), "adamw_small")
    for tree, packed in ((grads, pk(g_parts)), (deltas, d_s), (new_m, m_s), (new_v, v_s)):
        tree.update(zip(SMALL, _unpack(packed, local_shapes)))

    return (loss, grad_x.reshape(x.shape), *[grads[n] for n in WEIGHTS], *[deltas[n] for n in WEIGHTS],
            *[new_m[n] for n in WEIGHTS], *[new_v[n] for n in WEIGHTS])
```

```python
import functools
import math

import jax
import jax.numpy as jnp
from jax import lax
from jax.experimental import pallas as pl
from jax.experimental.pallas import tpu as pltpu

F32 = jnp.float32
BF16 = jnp.bfloat16
MESH = pl.DeviceIdType.MESH

D_MODEL = 1024
N_CHIPS = 4
D_RNN = 512
D_ATT = 512
N_HEADS = 8
HEAD_DIM = 64
RNN_BLOCKS = 8
CONV_W = 4
RG_C = 8.0
N_IN = 2 * D_RNN + 3 * D_ATT
EPS = 1e-6
ATT_BLOCK = 128
ATT_WINDOW = 384
ATT_SPLIT = 256
EXP_ZERO = -105.0

ADAM_LR = 0.001
ADAM_B1 = 0.9
ADAM_B2 = 0.999
ADAM_EPS = 1e-08
ADAM_WD = 0.01
ADAM_STEP = 10

V7X_VMEM_LIMIT = 56 * 1024 * 1024
TOKEN_TILE = 512
FFN_TILE = 256
WGRAD_TILE = 2048
WHOLE_TILE = 1024

GELU_K0 = math.sqrt(2.0 / math.pi)
GELU_K1 = 0.044715


def _params(sem=None):
    return pltpu.CompilerParams(dimension_semantics=sem, vmem_limit_bytes=V7X_VMEM_LIMIT)


def _dot(a, b):
    return jnp.dot(a, b, preferred_element_type=F32)


def _dot_nt(a, b):
    return lax.dot_general(a, b, (((1,), (1,)), ((), ())), preferred_element_type=F32)


def _dot_tn(a, b):
    return lax.dot_general(a, b, (((0,), (0,)), ((), ())), preferred_element_type=F32)


def _sigmoid(x):
    return 1.0 / (1.0 + jnp.exp(-x))


def _rms_r(xv):
    return lax.rsqrt(jnp.mean(xv * xv, axis=-1, keepdims=True) + EPS)


def _rms_bwd(xv, r, nw, dh):
    t = dh * nw
    dx = r * t - xv * (r * r * r * jnp.mean(t * xv, axis=-1, keepdims=True))
    dn = jnp.sum(dh * xv * r, axis=0, keepdims=True)
    return dx, dn


def _gelu(x):
    t = jnp.tanh(GELU_K0 * (x + GELU_K1 * x * x * x))
    return 0.5 * x * (1.0 + t)


def _gelu_grad(x):
    t = jnp.tanh(GELU_K0 * (x + GELU_K1 * x * x * x))
    return 0.5 * (1.0 + t) + 0.5 * x * (1.0 - t * t) * (GELU_K0 * (1.0 + 3.0 * GELU_K1 * x * x))


def _expm1_neg(x):
    p = 1.0 + x * (1.0 / 8.0)
    for k in (7.0, 6.0, 5.0, 4.0, 3.0, 2.0):
        p = 1.0 + x * (1.0 / k) * p
    return jnp.where(x > -0.25, x * p, jnp.exp(x) - 1.0)


def _log_sigmoid(x):
    return jnp.minimum(x, 0.0) - jnp.log(1.0 + jnp.exp(-jnp.abs(x)))


def _tile(s):
    return min(TOKEN_TILE, s)


def _ffn_fwd(x, nw, wg, wu, wd, tgt=None, rider=None):
    s, d = x.shape
    nb, fb, _ = wg.shape
    tm = min(FFN_TILE, s)
    ni = s // tm
    assert s % tm == 0
    with_loss = tgt is not None
    n_in, n_out = 5 + with_loss, 5 + with_loss

    def body(*refs):
        ins, outs, _, copies = _split_refs(refs, n_in, n_out, rider)
        x_ref, nw_ref, wg_ref, wu_ref, wd_ref = ins[:5]
        out_ref, g_ref, u_ref, hb_ref, ab_ref = outs[:5]
        i = pl.program_id(0)
        finish = _ride(copies, i == 0, i == ni - 1)

        xv = x_ref[...]
        hb = (xv * _rms_r(xv) * nw_ref[...]).astype(BF16)
        hb_ref[...] = hb
        y = jnp.zeros((tm, d), F32)
        for jb in range(nb):
            g = _dot_nt(hb, wg_ref[jb])
            u = _dot_nt(hb, wu_ref[jb])
            g_ref[jb] = g.astype(BF16)
            u_ref[jb] = u.astype(BF16)
            ab = (g * _sigmoid(g) * u).astype(BF16)
            ab_ref[jb] = ab
            y = y + _dot(ab, wd_ref[jb])
        y = xv + 0.5 * y
        if with_loss:
            tgt_ref, loss_ref = ins[5], outs[5]
            diff = y - tgt_ref[...]
            out_ref[...] = diff * (1.0 / d)

            @pl.when(i == 0)
            def _():
                loss_ref[...] = jnp.zeros_like(loss_ref)

            loss_ref[...] += jnp.sum(diff * diff) * (0.5 / d)
        else:
            out_ref[...] = y
        finish()

    row = pl.BlockSpec((tm, d), lambda i: (i, 0))
    weight = pl.BlockSpec((nb, fb, d), lambda i: (0, 0, 0), pipeline_mode=pl.Buffered(1))
    in_specs = [row, pl.BlockSpec((1, d), lambda i: (0, 0)), weight, weight, weight]
    args = [x, nw, wg, wu, wd]
    if with_loss:
        in_specs.append(row)
        args.append(tgt)
    blk = pl.BlockSpec((nb, tm, fb), lambda i: (0, i, 0))
    out_shape = [jax.ShapeDtypeStruct((s, d), F32), jax.ShapeDtypeStruct((nb, s, fb), BF16),
                 jax.ShapeDtypeStruct((nb, s, fb), BF16), jax.ShapeDtypeStruct((s, d), BF16),
                 jax.ShapeDtypeStruct((nb, s, fb), BF16)]
    out_specs = [row, blk, blk, row, blk]
    if with_loss:
        out_shape.append(jax.ShapeDtypeStruct((1, 128), F32))
        out_specs.append(pl.BlockSpec((1, 128), lambda i: (0, 0)))
    return _call(body, "ffn_fwd_loss" if with_loss else "ffn_fwd", (ni,), in_specs, out_specs, out_shape, args,
                 rider=rider)


def _call(body, name, grid, in_specs, out_specs, out_shape, args, scratch=(), rider=None):
    in_specs, out_specs, out_shape, scratch = list(in_specs), list(out_specs), list(out_shape), list(scratch)
    extra, aliases = [], {}
    if rider is not None:
        extra = rider.operands()
        aliases = rider.aliases(len(args), len(out_shape))
        in_specs += [ANY] * len(extra)
        out_specs += [ANY] * len(rider.inplace)
        out_shape += rider.out_shape()
        scratch += rider.scratch()
    return pl.pallas_call(
        body, name=name, grid=grid, in_specs=in_specs, out_specs=out_specs, out_shape=out_shape,
        input_output_aliases=aliases, scratch_shapes=scratch,
        compiler_params=_params(("arbitrary",) * len(grid)),
    )(*args, *extra)


def _ffn_bwd_act(x, nw, dy, g, u, wg, wu, wd, name, rider=None):
    s, d = x.shape
    nb, fb, _ = wg.shape
    tm = min(FFN_TILE, s)
    assert s % tm == 0

    def body(*refs):
        ins, outs, _, copies = _split_refs(refs, 8, 5, rider)
        x_ref, nw_ref, dy_ref, g_ref, u_ref, wg_ref, wu_ref, wd_ref = ins
        dx_ref, dg_ref, du_ref, dyb_ref, dnw_ref = outs
        finish = _ride(copies, pl.program_id(0) == 0, pl.program_id(0) == s // tm - 1)
        dyv = dy_ref[...]
        dyb = dyv.astype(BF16)
        dyb_ref[...] = dyb
        dh = jnp.zeros((tm, d), F32)
        for jb in range(nb):
            da = 0.5 * _dot_nt(dyb, wd_ref[jb])
            gv = g_ref[jb].astype(F32)
            sg = _sigmoid(gv)
            dub = (da * (gv * sg)).astype(BF16)
            dgb = (da * u_ref[jb].astype(F32) * (sg * (1.0 + gv * (1.0 - sg)))).astype(BF16)
            dg_ref[jb] = dgb
            du_ref[jb] = dub
            dh = dh + _dot(dgb, wg_ref[jb]) + _dot(dub, wu_ref[jb])
        xv = x_ref[...]
        dx, dn = _rms_bwd(xv, _rms_r(xv), nw_ref[...], dh)
        dx_ref[...] = dyv + dx

        @pl.when(pl.program_id(0) == 0)
        def _():
            dnw_ref[...] = jnp.zeros_like(dnw_ref)

        dnw_ref[...] += dn
        finish()

    row = pl.BlockSpec((tm, d), lambda i: (i, 0))
    vec = pl.BlockSpec((1, d), lambda i: (0, 0))
    blk = pl.BlockSpec((nb, tm, fb), lambda i: (0, i, 0))
    weight = pl.BlockSpec((nb, fb, d), lambda i: (0, 0, 0), pipeline_mode=pl.Buffered(1))
    return _call(
        body, name, (s // tm,), [row, vec, row, blk, blk, weight, weight, weight], [row, blk, blk, row, vec],
        [jax.ShapeDtypeStruct((s, d), F32), jax.ShapeDtypeStruct((nb, s, fb), BF16),
         jax.ShapeDtypeStruct((nb, s, fb), BF16), jax.ShapeDtypeStruct((s, d), BF16),
         jax.ShapeDtypeStruct((1, d), F32)],
        [x, nw, dy, g, u, wg, wu, wd], rider=rider)


def _wgrad(a, b, a_spec, b_spec, out_rows, out_cols, scale, name, tk, rider=None):
    s = a.shape[-2]
    nk = s // tk
    assert s % tk == 0

    def body(*refs):
        (a_ref, b_ref), (out_ref,), (acc,), copies = _split_refs(refs, 2, 1, rider)
        j, k = pl.program_id(0), pl.program_id(1)
        finish = _ride(copies, jnp.logical_and(j == 0, k == 0), jnp.logical_and(j == N_CHIPS - 1, k == nk - 1))

        @pl.when(k == 0)
        def _():
            acc[...] = jnp.zeros_like(acc)

        acc[...] += _dot_tn(a_ref[...], b_ref[...])

        @pl.when(k == nk - 1)
        def _():
            out_ref[...] = (acc[...] * scale).astype(BF16)

        finish()

    outs = _call(
        body, name, (N_CHIPS, nk), [a_spec(tk), b_spec(tk)],
        [pl.BlockSpec((None, out_rows, out_cols), lambda j, k: (j, 0, 0))],
        [jax.ShapeDtypeStruct((N_CHIPS, out_rows, out_cols), BF16)], [a, b],
        scratch=[pltpu.VMEM((out_rows, out_cols), F32)], rider=rider)
    return outs[0] if rider is None else outs


def _wgrad_whole(a, b, col_blocks, name, rider=None):
    s, m = a.shape
    n = b.shape[1]
    tk = min(WHOLE_TILE, s)
    nk = s // tk
    assert s % tk == 0
    out_shape = (N_CHIPS, m, n // N_CHIPS) if col_blocks else (N_CHIPS, m // N_CHIPS, n)

    def body(*refs):
        (a_ref, b_ref), (out_ref,), (acc,), copies = _split_refs(refs, 2, 1, rider)
        k = pl.program_id(0)
        finish = _ride(copies, k == 0, k == nk - 1)

        @pl.when(k == 0)
        def _():
            acc[...] = jnp.zeros_like(acc)

        acc[...] += _dot_tn(a_ref[...], b_ref[...])

        @pl.when(k == nk - 1)
        def _():
            for j in range(N_CHIPS):
                if col_blocks:
                    out_ref[j] = acc[:, j * out_shape[2]:(j + 1) * out_shape[2]].astype(BF16)
                else:
                    out_ref[j] = acc[j * out_shape[1]:(j + 1) * out_shape[1], :].astype(BF16)

        finish()

    outs = _call(
        body, name, (nk,), [pl.BlockSpec((tk, m), lambda k: (k, 0)), pl.BlockSpec((tk, n), lambda k: (k, 0))],
        [pl.BlockSpec(out_shape, lambda k: (0, 0, 0))], [jax.ShapeDtypeStruct(out_shape, BF16)], [a, b],
        scratch=[pltpu.VMEM((m, n), F32)], rider=rider)
    return outs[0] if rider is None else outs


def _ffn_wgrad(stack, shared, scale, name, rider=None):
    s, d = shared.shape
    fb = stack.shape[-1]
    return _wgrad(stack, shared, lambda tk: pl.BlockSpec((None, tk, fb), lambda j, k: (j, k, 0)),
                  lambda tk: pl.BlockSpec((tk, d), lambda j, k: (k, 0)), fb, d, scale, name,
                  min(WGRAD_TILE, s), rider)


def _mix_pre(x, nw, win):
    s, d = x.shape
    nb, _, cb = win.shape
    tm = min(FFN_TILE, s)
    assert s % tm == 0

    def body(x_ref, nw_ref, w_ref, p_ref, hb_ref):
        xv = x_ref[...]
        hb = (xv * _rms_r(xv) * nw_ref[...]).astype(BF16)
        hb_ref[...] = hb
        for j in range(nb):
            p_ref[:, j * cb:(j + 1) * cb] = _dot(hb, w_ref[j])

    row = pl.BlockSpec((tm, d), lambda i: (i, 0))
    return pl.pallas_call(
        body, name="mix_pre", grid=(s // tm,),
        in_specs=[row, pl.BlockSpec((1, d), lambda i: (0, 0)),
                  pl.BlockSpec((nb, d, cb), lambda i: (0, 0, 0), pipeline_mode=pl.Buffered(1))],
        out_specs=[pl.BlockSpec((tm, nb * cb), lambda i: (i, 0)), row],
        out_shape=[jax.ShapeDtypeStruct((s, nb * cb), F32), jax.ShapeDtypeStruct((s, d), BF16)],
        compiler_params=_params(("arbitrary",)),
    )(x, nw, win)


def _mix_pre_bwd(x, nw, dres, dpb, win):
    s, d = x.shape
    nb, _, cb = win.shape
    tm = min(FFN_TILE, s)
    assert s % tm == 0

    def body(x_ref, nw_ref, dres_ref, dp_ref, w_ref, dx_ref, dnw_ref):
        dh = jnp.zeros((tm, d), F32)
        for j in range(nb):
            dh = dh + _dot_nt(dp_ref[:, j * cb:(j + 1) * cb], w_ref[j])
        xv = x_ref[...]
        dx, dn = _rms_bwd(xv, _rms_r(xv), nw_ref[...], dh)
        dx_ref[...] = dres_ref[...] + dx

        @pl.when(pl.program_id(0) == 0)
        def _():
            dnw_ref[...] = jnp.zeros_like(dnw_ref)

        dnw_ref[...] += dn

    row = pl.BlockSpec((tm, d), lambda i: (i, 0))
    vec = pl.BlockSpec((1, d), lambda i: (0, 0))
    return pl.pallas_call(
        body, name="mix_pre_bwd", grid=(s // tm,),
        in_specs=[row, vec, row, pl.BlockSpec((tm, nb * cb), lambda i: (i, 0)),
                  pl.BlockSpec((nb, d, cb), lambda i: (0, 0, 0), pipeline_mode=pl.Buffered(1))],
        out_specs=[row, vec],
        out_shape=[jax.ShapeDtypeStruct((s, d), F32), jax.ShapeDtypeStruct((1, d), F32)],
        compiler_params=_params(("arbitrary",)),
    )(x, nw, dres, dpb, win)


def _mix_post(x, yr, ya, nr, na, wout):
    s, d = x.shape
    h = yr.shape[1]
    tm = _tile(s)

    def body(x_ref, yr_ref, ya_ref, nr_ref, na_ref, w_ref, out_ref):
        yrv = yr_ref[...]
        yav = ya_ref[...]
        onb = (yrv * _rms_r(yrv) * nr_ref[...]).astype(BF16)
        oab = (yav * _rms_r(yav) * na_ref[...]).astype(BF16)
        out_ref[...] = x_ref[...] + _dot(onb, w_ref[0:h, :]) + _dot(oab, w_ref[h:2 * h, :])

    row = pl.BlockSpec((tm, d), lambda i: (i, 0))
    half = pl.BlockSpec((tm, h), lambda i: (i, 0))
    vec = pl.BlockSpec((1, h), lambda i: (0, 0))
    return pl.pallas_call(
        body, name="mix_post", grid=(s // tm,),
        in_specs=[row, half, half, vec, vec, pl.BlockSpec((2 * h, d), lambda i: (0, 0))],
        out_specs=row, out_shape=jax.ShapeDtypeStruct((s, d), F32),
        compiler_params=_params(("arbitrary",)),
    )(x, yr, ya, nr, na, wout)


def _mix_post_bwd(dx, yr, ya, nr, na, wout):
    s, d = dx.shape
    h = yr.shape[1]
    tm = _tile(s)

    def body(dx_ref, yr_ref, ya_ref, nr_ref, na_ref, w_ref,
             dyr_ref, dya_ref, yc_ref, dxb_ref, dnr_ref, dna_ref):
        i = pl.program_id(0)
        dxb = dx_ref[...].astype(BF16)
        dxb_ref[...] = dxb
        dyc = _dot_nt(dxb, w_ref[...])
        yrv = yr_ref[...]
        yav = ya_ref[...]
        rr = _rms_r(yrv)
        ra = _rms_r(yav)
        yc_ref[:, 0:h] = (yrv * rr * nr_ref[...]).astype(BF16)
        yc_ref[:, h:2 * h] = (yav * ra * na_ref[...]).astype(BF16)
        dyr, dnr = _rms_bwd(yrv, rr, nr_ref[...], dyc[:, 0:h])
        dya, dna = _rms_bwd(yav, ra, na_ref[...], dyc[:, h:2 * h])
        dyr_ref[...] = dyr
        dya_ref[...] = dya

        @pl.when(i == 0)
        def _():
            dnr_ref[...] = jnp.zeros_like(dnr_ref)
            dna_ref[...] = jnp.zeros_like(dna_ref)

        dnr_ref[...] += dnr
        dna_ref[...] += dna

    row = pl.BlockSpec((tm, d), lambda i: (i, 0))
    half = pl.BlockSpec((tm, h), lambda i: (i, 0))
    vec = pl.BlockSpec((1, h), lambda i: (0, 0))
    return pl.pallas_call(
        body, name="mix_post_bwd", grid=(s // tm,),
        in_specs=[row, half, half, vec, vec, pl.BlockSpec((2 * h, d), lambda i: (0, 0))],
        out_specs=[half, half, pl.BlockSpec((tm, 2 * h), lambda i: (i, 0)), row, vec, vec],
        out_shape=[jax.ShapeDtypeStruct((s, h), F32), jax.ShapeDtypeStruct((s, h), F32),
                   jax.ShapeDtypeStruct((s, 2 * h), BF16), jax.ShapeDtypeStruct((s, d), BF16),
                   jax.ShapeDtypeStruct((1, h), F32), jax.ShapeDtypeStruct((1, h), F32)],
        compiler_params=_params(("arbitrary",)),
    )(dx, yr, ya, nr, na, wout)


def _shift_down(xv, s, prev8):
    rolled = pltpu.roll(xv, s, 0)
    row8 = lax.broadcasted_iota(jnp.int32, prev8.shape, 0)
    head = jnp.where(row8 < s, pltpu.roll(prev8, s, 0), rolled[0:8, :])
    return jnp.concatenate([head, rolled[8:, :]], axis=0)


def _shift_up(xv, s, next8):
    n = xv.shape[0]
    rolled = pltpu.roll(xv, n - s, 0)
    row8 = lax.broadcasted_iota(jnp.int32, next8.shape, 0)
    tail = jnp.where(row8 >= 8 - s, pltpu.roll(next8, 8 - s, 0), rolled[n - 8:, :])
    return jnp.concatenate([rolled[:n - 8, :], tail], axis=0)


def _scan_fwd(a, b):
    n = a.shape[0]
    row = lax.broadcasted_iota(jnp.int32, a.shape, 0)
    s = 1
    while s < n:
        ok = row >= s
        b = jnp.where(ok, a * pltpu.roll(b, s, 0) + b, b)
        a = jnp.where(ok, a * pltpu.roll(a, s, 0), a)
        s *= 2
    return b


def _scan_bwd(a, b):
    n = a.shape[0]
    row = lax.broadcasted_iota(jnp.int32, a.shape, 0)
    s = 1
    while s < n:
        ok = row < n - s
        b = jnp.where(ok, a * pltpu.roll(b, n - s, 0) + b, b)
        a = jnp.where(ok, a * pltpu.roll(a, n - s, 0), a)
        s *= 2
    return b


def _rglru_gates(xv, prev8, cw_ref, cb_ref, wa_ref, ba_ref, wx_ref, bx_ref, lam_ref):
    x1 = _shift_down(xv, 1, prev8)
    x2 = _shift_down(xv, 2, prev8)
    x3 = _shift_down(xv, 3, prev8)
    xc = cw_ref[3:4, :] * xv + cw_ref[2:3, :] * x1 + cw_ref[1:2, :] * x2 + cw_ref[0:1, :] * x3 + cb_ref[...]
    xcb = xc.astype(BF16)
    r = _sigmoid(_dot(xcb, wa_ref[...]) + ba_ref[...])
    ig = _sigmoid(_dot(xcb, wx_ref[...]) + bx_ref[...])
    c = RG_C * _log_sigmoid(lam_ref[...])
    la = r * c
    a = jnp.exp(la)
    m = jnp.sqrt(-_expm1_neg(2.0 * la))
    return (x1, x2, x3), xc, xcb, r, ig, c, a, m


def _rglru_fwd(proj, cw, cb, wa, ba, wx, bx, lam):
    s = proj.shape[0]
    w = D_RNN
    tm = _tile(s)

    def body(xr_ref, gate_ref, cw_ref, cb_ref, wa_ref, ba_ref, wx_ref, bx_ref, lam_ref,
             y_ref, h_ref, prev, hlast):
        @pl.when(pl.program_id(0) == 0)
        def _():
            prev[...] = jnp.zeros_like(prev)
            hlast[...] = jnp.zeros_like(hlast)

        xv = xr_ref[...]
        _, xc, _, _, ig, _, a, m = _rglru_gates(xv, prev[...], cw_ref, cb_ref, wa_ref, ba_ref,
                                                wx_ref, bx_ref, lam_ref)
        b = m * (ig * xc)
        row = lax.broadcasted_iota(jnp.int32, b.shape, 0)
        b = jnp.where(row == 0, b + a * hlast[...], b)
        h = _scan_fwd(a, b)
        h_ref[...] = h
        y_ref[...] = h * _gelu(gate_ref[...])
        prev[...] = xv[tm - 8:, :]
        hlast[...] = h[tm - 1:tm, :]

    vec = pl.BlockSpec((1, w), lambda i: (0, 0))
    sq = pl.BlockSpec((w, w), lambda i: (0, 0))
    out = pl.BlockSpec((tm, w), lambda i: (i, 0))
    return pl.pallas_call(
        body, name="rglru_fwd", grid=(s // tm,),
        in_specs=[pl.BlockSpec((tm, w), lambda i: (i, 0)), pl.BlockSpec((tm, w), lambda i: (i, 1)),
                  pl.BlockSpec((CONV_W, w), lambda i: (0, 0)), vec, sq, vec, sq, vec, vec],
        out_specs=[out, out],
        out_shape=[jax.ShapeDtypeStruct((s, w), F32), jax.ShapeDtypeStruct((s, w), F32)],
        scratch_shapes=[pltpu.VMEM((8, w), F32), pltpu.VMEM((1, w), F32)],
        compiler_params=_params(("arbitrary",)),
    )(proj, proj, cw, cb, wa, ba, wx, bx, lam)


def _rglru_bwd(proj, hseq, dyr, cw, cb, wa, ba, wx, bx, lam):
    s = proj.shape[0]
    w = D_RNN
    tm = _tile(s)
    nt = s // tm
    t8 = tm // 8

    def body(xr_ref, xp_ref, gate_ref, h_ref, hp_ref, dy_ref, cw_ref, cb_ref, wa_ref, ba_ref,
             wx_ref, bx_ref, lam_ref,
             dxr_ref, dgate_ref, dcw_ref, dcb_ref, dwa_ref, dba_ref, dwx_ref, dbx_ref, dlam_ref,
             carry, dxc_next):
        i = pl.program_id(0)
        first_tile = i == nt - 1

        @pl.when(i == 0)
        def _():
            carry[...] = jnp.zeros_like(carry)
            dxc_next[...] = jnp.zeros_like(dxc_next)
            for ref in (dcw_ref, dcb_ref, dwa_ref, dba_ref, dwx_ref, dbx_ref, dlam_ref):
                ref[...] = jnp.zeros_like(ref)

        xv = xr_ref[...]
        prev8 = jnp.where(first_tile, 0.0, xp_ref[...])
        hprev8 = jnp.where(first_tile, 0.0, hp_ref[...])
        (x1, x2, x3), xc, xcb, r, ig, c, a, m = _rglru_gates(
            xv, prev8, cw_ref, cb_ref, wa_ref, ba_ref, wx_ref, bx_ref, lam_ref)
        gv = gate_ref[...]
        hv = h_ref[...]
        dy = dy_ref[...]
        dgate_ref[...] = (dy * hv * _gelu_grad(gv)).astype(BF16)
        dh = dy * _gelu(gv)
        row = lax.broadcasted_iota(jnp.int32, dh.shape, 0)
        dh = jnp.where(row == tm - 1, dh + carry[...], dh)
        a_up = jnp.where(row == tm - 1, 0.0, pltpu.roll(a, tm - 1, 0))
        lam_t = _scan_bwd(a_up, dh)
        carry[...] = a[0:1, :] * lam_t[0:1, :]
        hm1 = _shift_down(hv, 1, hprev8)
        da = lam_t * hm1
        ixc = ig * xc
        dm = lam_t * ixc
        dig = lam_t * m * xc
        dxc = lam_t * m * ig
        dla = da * a - dm * (a * a) / m
        dr = dla * c
        dlam_ref[...] += jnp.sum(dla * r, axis=0, keepdims=True)
        dpa = dr * r * (1.0 - r)
        dpi = dig * ig * (1.0 - ig)
        dba_ref[...] += jnp.sum(dpa, axis=0, keepdims=True)
        dbx_ref[...] += jnp.sum(dpi, axis=0, keepdims=True)
        dpab = dpa.astype(BF16)
        dpib = dpi.astype(BF16)
        dwa_ref[...] += _dot_tn(xcb, dpab)
        dwx_ref[...] += _dot_tn(xcb, dpib)
        dxc = dxc + _dot_nt(dpab, wa_ref[...]) + _dot_nt(dpib, wx_ref[...])
        dcb_ref[...] += jnp.sum(dxc, axis=0, keepdims=True)
        dcw_ref[3:4, :] += jnp.sum(dxc * xv, axis=0, keepdims=True)
        dcw_ref[2:3, :] += jnp.sum(dxc * x1, axis=0, keepdims=True)
        dcw_ref[1:2, :] += jnp.sum(dxc * x2, axis=0, keepdims=True)
        dcw_ref[0:1, :] += jnp.sum(dxc * x3, axis=0, keepdims=True)
        nxt = dxc_next[...]
        dxr = (cw_ref[3:4, :] * dxc + cw_ref[2:3, :] * _shift_up(dxc, 1, nxt)
               + cw_ref[1:2, :] * _shift_up(dxc, 2, nxt) + cw_ref[0:1, :] * _shift_up(dxc, 3, nxt))
        dxr_ref[...] = dxr.astype(BF16)
        dxc_next[...] = dxc[0:8, :]

        @pl.when(first_tile)
        def _():
            lv = lam_ref[...]
            dlam_ref[...] = dlam_ref[...] * (RG_C * _sigmoid(-lv))

    rev = lambda i: nt - 1 - i
    vec = pl.BlockSpec((1, w), lambda i: (0, 0))
    sq = pl.BlockSpec((w, w), lambda i: (0, 0))
    cur = lambda col: pl.BlockSpec((tm, w), lambda i: (rev(i), col))
    before = lambda cols: pl.BlockSpec((8, w), lambda i: (jnp.maximum(rev(i) * t8 - 1, 0), 0))
    return pl.pallas_call(
        body, name="rglru_bwd", grid=(nt,),
        in_specs=[cur(0), before(None), cur(1), cur(0), before(None), cur(0),
                  pl.BlockSpec((CONV_W, w), lambda i: (0, 0)), vec, sq, vec, sq, vec, vec],
        out_specs=[cur(0), cur(0), pl.BlockSpec((CONV_W, w), lambda i: (0, 0)), vec, sq, vec, sq, vec, vec],
        out_shape=[jax.ShapeDtypeStruct((s, w), BF16), jax.ShapeDtypeStruct((s, w), BF16),
                   jax.ShapeDtypeStruct((CONV_W, w), F32), jax.ShapeDtypeStruct((1, w), F32),
                   jax.ShapeDtypeStruct((w, w), F32), jax.ShapeDtypeStruct((1, w), F32),
                   jax.ShapeDtypeStruct((w, w), F32), jax.ShapeDtypeStruct((1, w), F32),
                   jax.ShapeDtypeStruct((1, w), F32)],
        scratch_shapes=[pltpu.VMEM((1, w), F32), pltpu.VMEM((8, w), F32)],
        compiler_params=_params(("arbitrary",)),
    )(proj, proj, proj, hseq, hseq, dyr, cw, cb, wa, ba, wx, bx, lam)


def _sb_logs(z, valid):
    l1p = jnp.log(1.0 + jnp.exp(-jnp.abs(z)))
    lb = jnp.minimum(z, 0.0) - l1p
    lm = jnp.where(valid, -jnp.maximum(z, 0.0) - l1p, 0.0)
    return lb, lm


class _Window:
    def __init__(self):
        blk, win, cut = ATT_BLOCK, ATT_WINDOW, ATT_SPLIT
        self.row = lax.broadcasted_iota(jnp.int32, (blk, win), 0)
        self.col = lax.broadcasted_iota(jnp.int32, (blk, win), 1)

        def tri(n, later):
            j = lax.broadcasted_iota(jnp.int32, (n, n), 0)
            s = lax.broadcasted_iota(jnp.int32, (n, n), 1)
            return jnp.where((j > s) if later else (j < s), 1.0, 0.0).astype(BF16)

        self.later = (tri(cut, True), tri(win - cut, True))
        self.earlier = (tri(cut, False), tri(win - cut, False))

    def place(self, qi, g):
        end = (qi + 1) * ATT_BLOCK - g * ATT_WINDOW
        start = pl.multiple_of(jnp.maximum(end - ATT_WINDOW, 0), ATT_BLOCK)
        valid = start + self.col < jnp.minimum(qi * ATT_BLOCK + self.row, end)
        return start, valid

    @staticmethod
    def _parts(xv):
        hi = xv.astype(BF16)
        lo = (xv - hi.astype(F32)).astype(BF16)
        cut = ATT_SPLIT
        sums = (jnp.sum(xv[:, :cut], axis=1, keepdims=True), jnp.sum(xv[:, cut:], axis=1, keepdims=True))
        return (hi[:, :cut], lo[:, :cut]), (hi[:, cut:], lo[:, cut:]), sums

    def sums_after(self, xv, carry):
        (h0, l0), (h1, l1), (s0, s1) = self._parts(xv)
        first = _dot(h0, self.later[0]) + _dot(l0, self.later[0]) + (s1 + carry)
        last = _dot(h1, self.later[1]) + _dot(l1, self.later[1]) + carry
        return jnp.concatenate([first, last], axis=1), s0 + s1

    def sums_before(self, xv, carry):
        (h0, l0), (h1, l1), (s0, s1) = self._parts(xv)
        first = _dot(h0, self.earlier[0]) + _dot(l0, self.earlier[0]) + carry
        last = _dot(h1, self.earlier[1]) + _dot(l1, self.earlier[1]) + (s0 + carry)
        return jnp.concatenate([first, last], axis=1), s0 + s1


class _HeadPair:
    def __init__(self):
        lanes = 2 * HEAD_DIM
        lane = lax.broadcasted_iota(jnp.int32, (1, lanes), 1)
        self.masks = [lane // HEAD_DIM == h for h in (0, 1)]
        i = lax.broadcasted_iota(jnp.int32, (lanes, lanes), 0) // HEAD_DIM
        j = lax.broadcasted_iota(jnp.int32, (lanes, lanes), 1) // HEAD_DIM
        self.same_head = jnp.where(i == j, 1.0, 0.0).astype(BF16)

    def only(self, h, xv):
        return jnp.where(self.masks[h], xv, jnp.zeros_like(xv))

    def merge(self, per_head):
        return jnp.where(self.masks[0], per_head[0], per_head[1])

    def mean(self, xv):
        hi = xv.astype(BF16)
        lo = (xv - hi.astype(F32)).astype(BF16)
        return (_dot(hi, self.same_head) + _dot(lo, self.same_head)) * (1.0 / HEAD_DIM)

    def rms_r(self, xv):
        return lax.rsqrt(self.mean(xv * xv) + EPS)

    def rms_bwd(self, xv, r, nw, dh):
        t = dh * nw
        dx = r * t - xv * (r * r * r * self.mean(t * xv))
        dn = jnp.sum(dh * xv * r, axis=0, keepdims=True)
        return dx, dn[:, :HEAD_DIM] + dn[:, HEAD_DIM:]


def _attn_fwd(proj, qg, kg, rider=None):
    s = proj.shape[0]
    blk, win, dh = ATT_BLOCK, ATT_WINDOW, HEAD_DIM
    nq = s // blk
    scale = 1.0 / math.sqrt(dh)
    heads = (0, 1)
    assert s >= win and s % blk == 0

    def body(*refs):
        (q_ref, k_ref, v_ref, qg_ref, kg_ref), (o_ref,), (qn, kn, vb), copies = _split_refs(refs, 5, 1, rider)
        finish = _ride(copies, pl.program_id(0) == 0, pl.program_id(0) == N_HEADS // 2 - 1)
        wd, hp = _Window(), _HeadPair()
        qv = q_ref[...]
        qn[...] = (qv * hp.rms_r(qv) * qg_ref[...] * scale).astype(BF16)
        kv = k_ref[...]
        kn[...] = (kv * hp.rms_r(kv) * kg_ref[...]).astype(BF16)
        vb[...] = v_ref[...].astype(BF16)

        def q_step(qi, _):
            qoff = pl.multiple_of(qi * blk, blk)
            qt = qn[pl.ds(qoff, blk), :]
            qts = [hp.only(h, qt) for h in heads]

            def more(carry):
                g, live = carry[:2]
                return jnp.logical_and((qi + 1) * blk - g * win > 0, live > 0)

            def window(carry):
                g, _, accs, runs = carry
                start, valid = wd.place(qi, g)
                kt = kn[pl.ds(start, win), :]
                zs = [_dot_nt(qts[h], kt) for h in heads]
                logs = [_sb_logs(z, valid) for z in zs]
                sums = [wd.sums_after(logs[h][1], runs[h]) for h in heads]
                wgts = [jnp.where(valid, jnp.exp(logs[h][0] + sums[h][0]), 0.0).astype(BF16) for h in heads]
                vt = vb[pl.ds(start, win), :]
                accs = tuple(accs[h] + _dot(wgts[h], vt) for h in heads)
                runs = tuple(runs[h] + sums[h][1] for h in heads)
                live = (jnp.maximum(jnp.max(runs[0]), jnp.max(runs[1])) > EXP_ZERO).astype(jnp.int32)
                return g + 1, live, accs, runs

            zero = lambda cols: tuple(jnp.zeros((blk, cols), F32) for _ in heads)
            _, _, accs, _ = lax.while_loop(more, window, (jnp.int32(0), jnp.int32(1), zero(2 * dh), zero(1)))
            o_ref[pl.ds(qoff, blk), :] = hp.merge(accs)
            return 0

        lax.fori_loop(0, nq, q_step, 0)
        finish()

    pair = lambda group: pl.BlockSpec((s, 2 * dh), lambda p: (0, group * (D_ATT // (2 * dh)) + p))
    vec = pl.BlockSpec((1, 2 * dh), lambda p: (0, 0))
    return _call(
        body, "attn_fwd", (N_HEADS // 2,), [pair(2), pair(3), pair(4), vec, vec], [pair(0)],
        [jax.ShapeDtypeStruct((s, D_ATT), F32)], [proj, proj, proj, jnp.tile(qg, (1, 2)), jnp.tile(kg, (1, 2))],
        scratch=[pltpu.VMEM((s, 2 * dh), BF16)] * 3, rider=rider)


def _attn_bwd(proj, dya, qg, kg, rider=None):
    s = proj.shape[0]
    blk, win, dh = ATT_BLOCK, ATT_WINDOW, HEAD_DIM
    nq = s // blk
    max_windows = -(-s // win) + 1
    scale = 1.0 / math.sqrt(dh)
    steps = N_HEADS // 2
    heads = (0, 1)
    assert s >= win and s % blk == 0

    def body(*refs):
        ins, outs, scratch, copies = _split_refs(refs, 6, 5, rider)
        q_ref, k_ref, v_ref, do_ref, qg_ref, kg_ref = ins
        dq_ref, dk_ref, dv_ref, dqg_ref, dkg_ref = outs
        qn, kn, vb, dob, runs_ref, dqn, dkn, dvn = scratch
        finish = _ride(copies, pl.program_id(0) == 0, pl.program_id(0) == steps - 1)
        wd, hp = _Window(), _HeadPair()

        @pl.when(pl.program_id(0) == 0)
        def _():
            dqg_ref[...] = jnp.zeros_like(dqg_ref)
            dkg_ref[...] = jnp.zeros_like(dkg_ref)

        qv = q_ref[...]
        qn[...] = (qv * hp.rms_r(qv) * qg_ref[...] * scale).astype(BF16)
        kv = k_ref[...]
        kn[...] = (kv * hp.rms_r(kv) * kg_ref[...]).astype(BF16)
        vb[...] = v_ref[...].astype(BF16)
        dob[...] = do_ref[...].astype(BF16)
        dkn[...] = jnp.zeros_like(dkn)
        dvn[...] = jnp.zeros_like(dvn)

        def q_step(qi, _):
            qoff = pl.multiple_of(qi * blk, blk)
            qt = qn[pl.ds(qoff, blk), :]
            dot = dob[pl.ds(qoff, blk), :]
            qts = [hp.only(h, qt) for h in heads]
            dots = [hp.only(h, dot) for h in heads]

            def more(carry):
                g, live = carry[:2]
                return jnp.logical_and((qi + 1) * blk - g * win > 0, live > 0)

            def run_window(carry):
                g, _, runs = carry
                start, valid = wd.place(qi, g)
                for h in heads:
                    runs_ref[h, g] = runs[h]
                kt = kn[pl.ds(start, win), :]
                zs = [_dot_nt(qts[h], kt) for h in heads]
                runs = tuple(runs[h] + jnp.sum(_sb_logs(zs[h], valid)[1], axis=1, keepdims=True) for h in heads)
                live = (jnp.maximum(jnp.max(runs[0]), jnp.max(runs[1])) > EXP_ZERO).astype(jnp.int32)
                return g + 1, live, runs

            zero = lambda cols: tuple(jnp.zeros((blk, cols), F32) for _ in heads)
            windows, _, _ = lax.while_loop(more, run_window, (jnp.int32(0), jnp.int32(1), zero(1)))

            def k_window(gg, carry):
                dq_accs, esums = carry
                g = windows - 1 - gg
                start, valid = wd.place(qi, g)
                kt = kn[pl.ds(start, win), :]
                vt = vb[pl.ds(start, win), :]
                zs = [_dot_nt(qts[h], kt) for h in heads]
                dws = [_dot_nt(dots[h], vt) for h in heads]
                logs = [_sb_logs(z, valid) for z in zs]
                tails = [wd.sums_after(logs[h][1], runs_ref[h, g])[0] for h in heads]
                wgts = [jnp.where(valid, jnp.exp(logs[h][0] + tails[h]), 0.0) for h in heads]
                es = [dws[h] * wgts[h] for h in heads]
                befores = [wd.sums_before(es[h], esums[h]) for h in heads]
                dzbs = []
                for h in heads:
                    beta = jnp.exp(logs[h][0])
                    dz = jnp.where(valid, es[h] * (1.0 - beta) - befores[h][0] * beta, 0.0)
                    dzbs.append(dz.astype(BF16))
                dq_accs = tuple(dq_accs[h] + _dot(dzbs[h], kt) for h in heads)
                dkn[pl.ds(start, win), :] += _dot_tn(dzbs[0], qts[0]) + _dot_tn(dzbs[1], qts[1])
                dvn[pl.ds(start, win), :] += (_dot_tn(wgts[0].astype(BF16), dots[0])
                                              + _dot_tn(wgts[1].astype(BF16), dots[1]))
                return dq_accs, tuple(esums[h] + befores[h][1] for h in heads)

            dq_accs, _ = lax.fori_loop(0, windows, k_window, (zero(2 * dh), zero(1)))
            dqn[pl.ds(qoff, blk), :] = hp.merge(dq_accs)
            return 0

        lax.fori_loop(0, nq, q_step, 0)

        dq, dqg = hp.rms_bwd(qv, hp.rms_r(qv), qg_ref[...] * scale, dqn[...])
        dq_ref[...] = dq.astype(BF16)
        dqg_ref[...] += dqg * scale
        dk, dkg = hp.rms_bwd(kv, hp.rms_r(kv), kg_ref[...], dkn[...])
        dk_ref[...] = dk.astype(BF16)
        dkg_ref[...] += dkg
        dv_ref[...] = dvn[...].astype(BF16)
        finish()

    pair = lambda group: pl.BlockSpec((s, 2 * dh), lambda p: (0, group * (D_ATT // (2 * dh)) + p))
    vec2 = pl.BlockSpec((1, 2 * dh), lambda p: (0, 0))
    vec = pl.BlockSpec((1, dh), lambda p: (0, 0))
    return _call(
        body, "attn_bwd", (steps,), [pair(2), pair(3), pair(4), pair(0), vec2, vec2],
        [pair(0), pair(0), pair(0), vec, vec],
        [jax.ShapeDtypeStruct((s, D_ATT), BF16)] * 3 + [jax.ShapeDtypeStruct((1, dh), F32)] * 2,
        [proj, proj, proj, dya, jnp.tile(qg, (1, 2)), jnp.tile(kg, (1, 2))],
        scratch=[pltpu.VMEM((s, 2 * dh), BF16)] * 4 + [pltpu.VMEM((2, max_windows, blk, 1), F32)]
        + [pltpu.VMEM((s, 2 * dh), F32)] * 3, rider=rider)


def _block_diag(w):
    n, c, d = w.shape
    return jnp.einsum("ncd,nm->ncmd", w, jnp.eye(n, dtype=w.dtype)).reshape(n * c, n * d)


def _diag_blocks(full, n):
    c = full.shape[0] // n
    return jnp.stack([full[i * c:(i + 1) * c, i * c:(i + 1) * c] for i in range(n)])


FFN1 = ["ffn1_w_gate", "ffn1_w_up", "ffn1_w_down"]
FFN2 = ["ffn2_w_gate", "ffn2_w_up", "ffn2_w_down"]
MIXER = ["w_in", "w_out"]


def _pair_sums(gb, names, where):
    theirs = _pair_exchange([gb[n] for n in names], "pair_exchange_" + names[0])
    pair, own = zip(*[_pair_sum(gb[n], t, where, "pair_sum_" + n) for n, t in zip(names, theirs)])
    return _chip_rider(list(pair), list(own))


def _local_step(x, tgt, stacks, conv_stack, small, where):
    big = dict(zip(FFN1, _gather_weights([stacks[n] for n in FFN1], [])))
    wa = _block_diag(small["rg_w_a"]).astype(BF16)
    wx = _block_diag(small["rg_w_x"]).astype(BF16)

    x1, g1, u1, hb1, ab1, *landed = _ffn_fwd(x, small["ffn1_norm"], *[big[n] for n in FFN1],
                                             rider=_gather_rider([stacks[n] for n in MIXER], [conv_stack]))
    big.update(zip(MIXER, _forward_weights(landed[:len(MIXER)], "forward_mixer_weights")))
    conv_w = jnp.transpose(landed[-1], (1, 0, 2)).reshape(CONV_W, D_RNN)
    wout = big["w_out"].reshape(D_MODEL, D_MODEL)
    rg = (conv_w, small["conv_b"], wa, small["rg_b_a"], wx, small["rg_b_x"], small["rg_lambda"])
    proj, hb2 = _mix_pre(x1, small["mix_norm"], big["w_in"])
    yr, hseq = _rglru_fwd(proj, *rg)
    ya, *landed = _attn_fwd(proj, small["q_norm"], small["k_norm"], _gather_rider([stacks[n] for n in FFN2], []))
    big.update(zip(FFN2, _forward_weights(landed, "forward_ffn2_weights")))
    x2 = _mix_post(x1, yr, ya, small["rnn_out_norm"], small["attn_out_norm"], wout)
    dx3, g2, u2, hb3, ab3, loss = _ffn_fwd(x2, small["ffn2_norm"], *[big[n] for n in FFN2], tgt)

    gb, gs, slots = {}, {}, {}
    dx2, dg2, du2, dyb2, gs["ffn2_norm"] = _ffn_bwd_act(x2, small["ffn2_norm"], dx3, g2, u2, *[big[n] for n in FFN2],
                                                        "ffn2_bwd")
    gb["ffn2_w_gate"] = _ffn_wgrad(dg2, hb3, 1.0, "wgrad_gate_ffn2")
    gb["ffn2_w_up"] = _ffn_wgrad(du2, hb3, 1.0, "wgrad_up_ffn2")
    gb["ffn2_w_down"] = _ffn_wgrad(ab3, dyb2, 0.5, "wgrad_down_ffn2")
    dyr, dya, ycat, dxb2, gs["rnn_out_norm"], gs["attn_out_norm"] = _mix_post_bwd(
        dx2, yr, ya, small["rnn_out_norm"], small["attn_out_norm"], wout)
    gb["w_out"] = _wgrad_whole(ycat, dxb2, False, "wgrad_out")
    early = FFN2 + ["w_out"]
    dq, dk, dv, gs["q_norm"], gs["k_norm"], *done = _attn_bwd(
        proj, dya, small["q_norm"], small["k_norm"], _pair_sums(gb, early, where))
    slots.update(zip(early, done))
    dxr, dgate, gs["conv_w"], gs["conv_b"], dwa, gs["rg_b_a"], dwx, gs["rg_b_x"], gs["rg_lambda"] = _rglru_bwd(
        proj, hseq, dyr, *rg)
    gs["rg_w_a"] = _diag_blocks(dwa, RNN_BLOCKS)
    gs["rg_w_x"] = _diag_blocks(dwx, RNN_BLOCKS)
    dpb = jnp.concatenate([dxr, dgate, dq, dk, dv], axis=1)
    dx1, gs["mix_norm"] = _mix_pre_bwd(x1, small["mix_norm"], dx2, dpb, big["w_in"])
    dx0, dg1, du1, dyb1, gs["ffn1_norm"] = _ffn_bwd_act(x, small["ffn1_norm"], dx1, g1, u1, *[big[n] for n in FFN1],
                                                        "ffn1_bwd")

    mine = _place_shard(_pack([gs[n] for n in SMALL]), where, F32, "place_small_grads", by_device=True)
    gb["ffn1_w_gate"], everyone = _ffn_wgrad(dg1, hb1, 1.0, "wgrad_gate_ffn1", _small_rider(mine))
    gb["ffn1_w_up"], slots["ffn1_w_gate"] = _ffn_wgrad(
        du1, hb1, 1.0, "wgrad_up_ffn1", _pair_sums(gb, ["ffn1_w_gate"], where))
    gb["ffn1_w_down"], slots["ffn1_w_up"] = _ffn_wgrad(
        ab1, dyb1, 0.5, "wgrad_down_ffn1", _pair_sums(gb, ["ffn1_w_up"], where))
    gb["w_in"], slots["ffn1_w_down"] = _wgrad_whole(
        hb2, dpb, True, "wgrad_in", _pair_sums(gb, ["ffn1_w_down"], where))
    last = _pair_sums(gb, ["w_in"], where)
    slots["w_in"], = _chip_exchange(last.plain, last.inplace)
    return loss[0, 0], dx0, slots, gs, everyone


ANY = pl.BlockSpec(memory_space=pl.ANY)


def _place():
    x, y, c = lax.axis_index("x"), lax.axis_index("y"), lax.axis_index("c")
    other_chips = [(1 - x, y), (x, 1 - y), (1 - x, 1 - y)]
    return x, y, c, 2 * x + y, other_chips


def _remote(src, dst, send_sem, recv_sem, to):
    return pltpu.make_async_remote_copy(src_ref=src, dst_ref=dst, send_sem=send_sem, recv_sem=recv_sem,
                                        device_id=to, device_id_type=MESH)


def _copy_plan(pairs):
    sends = [functools.partial(_remote, *a) for a, _ in pairs]
    arrivals = [functools.partial(_remote, *b) for _, b in pairs]
    return sends, arrivals


class _Rider:
    def __init__(self, plan, plain, inplace, n_copies=None):
        self.plan, self.plain, self.inplace = plan, list(plain), list(inplace)
        self.n_copies = n_copies or 3 * len(self.inplace)

    def operands(self):
        return self.plain + self.inplace

    def out_shape(self):
        return [jax.ShapeDtypeStruct(a.shape, a.dtype) for a in self.inplace]

    def aliases(self, inputs_before, outputs_before):
        return {inputs_before + len(self.plain) + k: outputs_before + k for k in range(len(self.inplace))}

    def scratch(self):
        return [pltpu.SemaphoreType.DMA((self.n_copies,))] * 2


def _split_refs(refs, n_in, n_out, rider):
    if rider is None:
        return refs[:n_in], refs[n_in:n_in + n_out], refs[n_in + n_out:], None
    r_in, r_out = len(rider.operands()), len(rider.inplace)
    outs_at = n_in + r_in
    rest = refs[outs_at + n_out + r_out:]
    copies = functools.partial(rider.plan, refs[n_in:n_in + len(rider.plain)],
                               refs[outs_at + n_out:outs_at + n_out + r_out], *rest[-2:])
    return refs[:n_in], refs[outs_at:outs_at + n_out], rest[:-2], copies


def _ride(copies, first, last):
    if copies is None:
        return lambda: None

    @pl.when(first)
    def _():
        _start(copies()[0])

    def finish():
        @pl.when(last)
        def _():
            _finish(*copies())

    return finish


def _gather_rider(split, whole):
    n_split = len(split)
    return _Rider(lambda plain, stacks, ss, rs: _gather_ici(stacks, n_split, ss, rs), [], list(split) + list(whole))


def _chip_rider(sums, slots):
    return _Rider(_chip_copies, sums, slots)


def _start(makers):
    for make in makers:
        make().start()


def _finish(sends, arrivals):
    for make in arrivals:
        make().wait_recv()
    for make in sends:
        make().wait_send()


def _half(rows, c):
    return pl.ds(pl.multiple_of(c * rows, 16), rows)


def _gather_weights(split, whole):
    arrs = list(split) + list(whole)
    n, ns = len(arrs), len(split)

    def body(*refs):
        outs = refs[n:2 * n]
        send_sems, recv_sems, fsend_sems, frecv_sems = refs[2 * n:]
        sends, arrivals = _gather_ici(outs, ns, send_sems, recv_sems)
        passes, passed = _gather_d2d(outs[:ns], fsend_sems, frecv_sems)
        _start(sends)
        for k, make in enumerate(arrivals):
            make().wait_recv()
            if k < 3 * ns:
                passes[k]().start()
        _finish(sends + passes, passed)

    return pl.pallas_call(
        body, name="gather_weights",
        in_specs=[ANY] * n, out_specs=[ANY] * n,
        out_shape=[jax.ShapeDtypeStruct(a.shape, a.dtype) for a in arrs],
        input_output_aliases={i: i for i in range(n)},
        scratch_shapes=[pltpu.SemaphoreType.DMA((3 * n,)), pltpu.SemaphoreType.DMA((3 * n,)),
                        pltpu.SemaphoreType.DMA((3 * ns,)), pltpu.SemaphoreType.DMA((3 * ns,))],
    )(*arrs)


def _gather_ici(stacks, n_split, send_sems, recv_sems):
    x, y, c, me, chips = _place()

    def region(i, chip):
        if i < n_split:
            return stacks[i].at[chip, _half(stacks[i].shape[1] // 2, c)]
        return stacks[i].at[chip]

    pairs = []
    for i in range(len(stacks)):
        for p, (cx, cy) in enumerate(chips):
            k = 3 * i + p
            mine, got = region(i, me), region(i, 2 * cx + cy)
            sems, to = (send_sems.at[k], recv_sems.at[k]), (cx, cy, c)
            pairs.append(((mine, mine, *sems, to), (got, got, *sems, to)))
    return _copy_plan(pairs)


def _gather_d2d(stacks, send_sems, recv_sems):
    x, y, c, _, chips = _place()
    sibling = (x, y, 1 - c)
    pairs = []
    for i, stack in enumerate(stacks):
        rows = stack.shape[1] // 2
        for p, (cx, cy) in enumerate(chips):
            k = 3 * i + p
            got, theirs = stack.at[2 * cx + cy, _half(rows, c)], stack.at[2 * cx + cy, _half(rows, 1 - c)]
            sems = (send_sems.at[k], recv_sems.at[k])
            pairs.append(((got, got, *sems, sibling), (theirs, theirs, *sems, sibling)))
    return _copy_plan(pairs)


def _forward_weights(split, name):
    n = len(split)

    def body(*refs):
        sends, arrivals = _gather_d2d(refs[n:2 * n], *refs[2 * n:])
        _start(sends)
        _finish(sends, arrivals)

    return pl.pallas_call(
        body, name=name,
        in_specs=[ANY] * n, out_specs=[ANY] * n,
        out_shape=[jax.ShapeDtypeStruct(a.shape, a.dtype) for a in split],
        input_output_aliases={i: i for i in range(n)},
        scratch_shapes=[pltpu.SemaphoreType.DMA((3 * n,))] * 2,
    )(*split)


def _pair_exchange(grads, name):
    n = len(grads)

    def body(*refs):
        ins, theirs = refs[:n], refs[n:2 * n]
        send_sems, recv_sems = refs[2 * n:]
        x, y, c, _, _ = _place()
        sibling = (x, y, 1 - c)
        sends = [_remote(ins[k].at[:, _half(grads[k].shape[1] // 2, 1 - c)], theirs[k],
                         send_sems.at[k], recv_sems.at[k], sibling) for k in range(n)]
        for cp in sends:
            cp.start()
        for k in range(n):
            _remote(theirs[k], theirs[k], send_sems.at[k], recv_sems.at[k], sibling).wait_recv()
        for cp in sends:
            cp.wait_send()

    return pl.pallas_call(
        body, name=name,
        in_specs=[ANY] * n, out_specs=[ANY] * n,
        out_shape=[jax.ShapeDtypeStruct((g.shape[0], g.shape[1] // 2, g.shape[2]), g.dtype) for g in grads],
        scratch_shapes=[pltpu.SemaphoreType.DMA((n,))] * 2,
    )(*grads)


def _chip_exchange(sums, slots):
    n = len(sums)

    def body(*refs):
        sends, arrivals = _chip_copies(refs[:n], refs[2 * n:3 * n], *refs[3 * n:])
        _start(sends)
        _finish(sends, arrivals)

    return pl.pallas_call(
        body, name="grad_chip_exchange",
        in_specs=[ANY] * (2 * n), out_specs=[ANY] * n,
        out_shape=[jax.ShapeDtypeStruct(a.shape, a.dtype) for a in slots],
        input_output_aliases={n + k: k for k in range(n)},
        scratch_shapes=[pltpu.SemaphoreType.DMA((3 * n,)), pltpu.SemaphoreType.DMA((3 * n,))],
    )(*sums, *slots)


def _chip_copies(sums, slots, send_sems, recv_sems):
    x, y, c, me, chips = _place()
    pairs = []
    for k in range(len(sums)):
        for p, (cx, cy) in enumerate(chips):
            j = 3 * k + p
            got = slots[k].at[2 * cx + cy]
            sems, to = (send_sems.at[j], recv_sems.at[j]), (cx, cy, c)
            pairs.append(((sums[k].at[2 * cx + cy], slots[k].at[me], *sems, to), (got, got, *sems, to)))
    return _copy_plan(pairs)


def _half_swap(halves):
    n = len(halves)

    def body(*refs):
        outs = refs[n:2 * n]
        send_sems, recv_sems = refs[2 * n:]
        x, y, c, _, _ = _place()
        sibling = (x, y, 1 - c)
        sends = [_remote(outs[k].at[c], outs[k].at[c], send_sems.at[k], recv_sems.at[k], sibling) for k in range(n)]
        for cp in sends:
            cp.start()
        for k in range(n):
            got = outs[k].at[1 - c]
            _remote(got, got, send_sems.at[k], recv_sems.at[k], sibling).wait_recv()
        for cp in sends:
            cp.wait_send()

    return pl.pallas_call(
        body, name="grad_half_swap",
        in_specs=[ANY] * n, out_specs=[ANY] * n,
        out_shape=[jax.ShapeDtypeStruct(a.shape, a.dtype) for a in halves],
        input_output_aliases={k: k for k in range(n)},
        scratch_shapes=[pltpu.SemaphoreType.DMA((n,))] * 2,
    )(*halves)


def _small_rider(stack):
    n_dev = 2 * N_CHIPS

    def plan(_, stacks, send_sems, recv_sems):
        x, y, c, _, _ = _place()
        mine = stacks[0].at[4 * x + 2 * y + c]
        pairs = []
        for k in range(1, n_dev):
            px, py, pc = x ^ ((k >> 2) & 1), y ^ ((k >> 1) & 1), c ^ (k & 1)
            got = stacks[0].at[4 * px + 2 * py + pc]
            sems = (send_sems.at[k - 1], recv_sems.at[k - 1])
            pairs.append(((mine, mine, *sems, (px, py, pc)), (got, got, *sems, (px, py, pc))))
        return _copy_plan(pairs)

    return _Rider(plan, [], [stack], n_dev - 1)


def _row_tile(r):
    return r // 4 if r >= 256 and (r // 4) % 16 == 0 else r


def _prefetch_call(body, name, grid, in_specs, out_specs, out_shape):
    spec = pltpu.PrefetchScalarGridSpec(num_scalar_prefetch=1, grid=grid, in_specs=in_specs, out_specs=out_specs)
    return pl.pallas_call(body, name=name, grid_spec=spec, out_shape=out_shape,
                          compiler_params=_params(("arbitrary",) * len(grid)))


def _place_shard(w2d, where, dtype, name, by_device=False):
    r, c = w2d.shape
    tr = _row_tile(r)
    slots = 2 * N_CHIPS if by_device else N_CHIPS
    slot = (lambda s: 2 * s[1] + s[0]) if by_device else (lambda s: s[1])

    def body(where_ref, w_ref, out_ref):
        out_ref[...] = w_ref[...].astype(dtype)

    return _prefetch_call(
        body, name, (r // tr,), [pl.BlockSpec((tr, c), lambda i, s: (i, 0))],
        pl.BlockSpec((None, tr, c), lambda i, s: (slot(s), i, 0)),
        jax.ShapeDtypeStruct((slots, r, c), dtype))(where, w2d)


def _pair_sum(full, theirs, where, name):
    nb, hs, c = theirs.shape

    def body(where_ref, a_ref, b_ref, out_ref, own_ref):
        total = (a_ref[...].astype(F32) + b_ref[...].astype(F32)).astype(BF16)
        out_ref[...] = total

        @pl.when(pl.program_id(0) == where_ref[1])
        def _():
            own_ref[...] = total

    blk = pl.BlockSpec((None, hs, c), lambda j, s: (j, 0, 0))
    shape = jax.ShapeDtypeStruct(theirs.shape, BF16)
    return _prefetch_call(
        body, name, (nb,), [pl.BlockSpec((None, hs, c), lambda j, s: (j, s[0], 0)), blk],
        [blk, pl.BlockSpec((None, hs, c), lambda j, s: (s[1], 0, 0))], [shape, shape])(where, full, theirs)


def _chip_sum(slots, where, name):
    nb, hs, c = slots.shape
    tr = _row_tile(hs)

    def body(where_ref, a_ref, out_ref):
        total = a_ref[0].astype(F32)
        for j in range(1, nb):
            total = total + a_ref[j].astype(F32)
        out_ref[...] = total

    return _prefetch_call(
        body, name, (hs // tr,), [pl.BlockSpec((nb, tr, c), lambda i, s: (0, i, 0))],
        pl.BlockSpec((None, tr, c), lambda i, s: (s[0], i, 0)),
        jax.ShapeDtypeStruct((2, hs, c), F32))(where, slots)


def _slot_sum(a, name):
    nb, r, c = a.shape
    tr = _row_tile(r)

    def body(a_ref, out_ref):
        total = a_ref[0].astype(F32)
        for j in range(1, nb):
            total = total + a_ref[j].astype(F32)
        out_ref[...] = total

    return pl.pallas_call(
        body, name=name, grid=(r // tr,),
        in_specs=[pl.BlockSpec((nb, tr, c), lambda i: (0, i, 0))],
        out_specs=pl.BlockSpec((tr, c), lambda i: (i, 0)),
        out_shape=jax.ShapeDtypeStruct((r, c), F32), compiler_params=_params(("arbitrary",)),
    )(a)


def _adamw(w, g, m, v, name):
    r, c = w.shape
    tr = _row_tile(r)
    c1 = 1.0 - ADAM_B1 ** ADAM_STEP
    c2 = 1.0 - ADAM_B2 ** ADAM_STEP

    def body(w_ref, g_ref, m_ref, v_ref, d_ref, m2_ref, v2_ref):
        gv = g_ref[...]
        m2 = ADAM_B1 * m_ref[...] + (1.0 - ADAM_B1) * gv
        v2 = ADAM_B2 * v_ref[...] + (1.0 - ADAM_B2) * (gv * gv)
        m2_ref[...] = m2
        v2_ref[...] = v2
        d_ref[...] = -ADAM_LR * ((m2 / c1) / (jnp.sqrt(v2 / c2) + ADAM_EPS) + ADAM_WD * w_ref[...])

    blk = pl.BlockSpec((tr, c), lambda i: (i, 0))
    return pl.pallas_call(
        body, name=name, grid=(r // tr,), in_specs=[blk] * 4, out_specs=[blk] * 3,
        out_shape=[jax.ShapeDtypeStruct((r, c), F32)] * 3, compiler_params=_params(("arbitrary",)),
    )(w, g, m, v)


WEIGHTS = ["ffn1_norm", "ffn1_w_gate", "ffn1_w_up", "ffn1_w_down", "mix_norm", "w_in", "conv_w", "conv_b",
           "rg_w_a", "rg_b_a", "rg_w_x", "rg_b_x", "rg_lambda", "q_norm", "k_norm", "rnn_out_norm",
           "attn_out_norm", "w_out", "ffn2_norm", "ffn2_w_gate", "ffn2_w_up", "ffn2_w_down"]
BIG = ["ffn1_w_gate", "ffn1_w_up", "ffn1_w_down", "w_in", "w_out", "ffn2_w_gate", "ffn2_w_up", "ffn2_w_down"]
SMALL = [n for n in WEIGHTS if n not in BIG]
PACK_LANES = 128
PACK_ROW_ALIGN = 8


def _hidden_major(name, a):
    return jnp.transpose(a) if name.endswith(("w_gate", "w_up")) else a


def _pack(parts):
    flat = jnp.concatenate([p.reshape(-1) for p in parts])
    unit = PACK_LANES * PACK_ROW_ALIGN
    padded = -(-flat.shape[0] // unit) * unit
    return jnp.pad(flat, (0, padded - flat.shape[0])).reshape(-1, PACK_LANES)


def _unpack(packed, shapes):
    flat = packed.reshape(-1)
    out, at = [], 0
    for shp in shapes:
        size = math.prod(shp)
        out.append(flat[at:at + size].reshape(shp))
        at += size
    return out


def kernel(x, ffn1_norm, ffn1_w_gate, ffn1_w_up, ffn1_w_down, mix_norm, w_in, conv_w, conv_b, rg_w_a, rg_b_a, rg_w_x, rg_b_x, rg_lambda, q_norm, k_norm, rnn_out_norm, attn_out_norm, w_out, ffn2_norm, ffn2_w_gate, ffn2_w_up, ffn2_w_down, loss_target, m_ffn1_norm, m_ffn1_w_gate, m_ffn1_w_up, m_ffn1_w_down, m_mix_norm, m_w_in, m_conv_w, m_conv_b, m_rg_w_a, m_rg_b_a, m_rg_w_x, m_rg_b_x, m_rg_lambda, m_q_norm, m_k_norm, m_rnn_out_norm, m_attn_out_norm, m_w_out, m_ffn2_norm, m_ffn2_w_gate, m_ffn2_w_up, m_ffn2_w_down, v_ffn1_norm, v_ffn1_w_gate, v_ffn1_w_up, v_ffn1_w_down, v_mix_norm, v_w_in, v_conv_w, v_conv_b, v_rg_w_a, v_rg_b_a, v_rg_w_x, v_rg_b_x, v_rg_lambda, v_q_norm, v_k_norm, v_rnn_out_norm, v_attn_out_norm, v_w_out, v_ffn2_norm, v_ffn2_w_gate, v_ffn2_w_up, v_ffn2_w_down):
    given = dict(locals())
    w = {n: given[n] for n in WEIGHTS}
    m = {n: given["m_" + n] for n in WEIGHTS}
    v = {n: given["v_" + n] for n in WEIGHTS}
    chip = 2 * lax.axis_index("x") + lax.axis_index("y")

    where = jnp.stack([lax.axis_index("c"), chip]).astype(jnp.int32)

    stacks = {n: _place_shard(_hidden_major(n, w[n][0]), where, BF16, "place_" + n) for n in BIG}
    conv_stack = _place_shard(w["conv_w"][0], where, F32, "place_conv_w")
    small = {n: (w[n][0] if w[n].ndim > 2 else w[n]) for n in SMALL if n != "conv_w"}

    loss, grad_x, slots, gs, everyone = _local_step(x[0], loss_target[0], stacks, conv_stack, small, where)
    loss = lax.psum(loss, ("x", "y", "c"))

    swapped = _half_swap([_chip_sum(slots[n], where, "chip_sum_" + n) for n in BIG])
    grads, deltas, new_m, new_v = {}, {}, {}, {}
    for n, t in zip(BIG, swapped):
        g2 = t.reshape(t.shape[0] * t.shape[1], t.shape[2])
        d2, m2, v2 = _adamw(_hidden_major(n, w[n][0]), g2, _hidden_major(n, m[n][0]), _hidden_major(n, v[n][0]),
                            "adamw_" + n)
        back = lambda a: _hidden_major(n, a).reshape(w[n].shape)
        grads[n], deltas[n], new_m[n], new_v[n] = back(g2), back(d2), back(m2), back(v2)

    full_shapes = [gs[n].shape for n in SMALL]
    g_small = _slot_sum(everyone, "small_grad_sum")
    g_parts = dict(zip(SMALL, _unpack(g_small, full_shapes)))
    quarter = D_RNN // N_CHIPS
    g_parts["conv_w"] = lax.dynamic_slice_in_dim(g_parts["conv_w"], chip * quarter, quarter, axis=1)
    local_shapes = [w[n].shape for n in SMALL]
    pk = lambda tree: _pack([tree[n] for n in SMALL])
    d_s, m_s, v_s = _adamw(pk(w), pk(g_parts), pk(m), pk(v), "adamw_small")
    for tree, packed in ((grads, pk(g_parts)), (deltas, d_s), (new_m, m_s), (new_v, v_s)):
        tree.update(zip(SMALL, _unpack(packed, local_shapes)))

    return (loss, grad_x.reshape(x.shape), *[grads[n] for n in WEIGHTS], *[deltas[n] for n in WEIGHTS],
            *[new_m[n] for n in WEIGHTS], *[new_v[n] for n in WEIGHTS])
```

```python
import functools
import math

import jax
import jax.numpy as jnp
from jax import lax
from jax.experimental import pallas as pl
from jax.experimental.pallas import tpu as pltpu

F32 = jnp.float32
BF16 = jnp.bfloat16
MESH = pl.DeviceIdType.MESH

D_MODEL = 1024
N_CHIPS = 4
D_RNN = 512
D_ATT = 512
N_HEADS = 8
HEAD_DIM = 64
RNN_BLOCKS = 8
CONV_W = 4
RG_C = 8.0
N_IN = 2 * D_RNN + 3 * D_ATT
EPS = 1e-6
ATT_BLOCK = 128
ATT_WINDOW = 384
ATT_SPLIT = 256
EXP_ZERO = -105.0

ADAM_LR = 0.001
ADAM_B1 = 0.9
ADAM_B2 = 0.999
ADAM_EPS = 1e-08
ADAM_WD = 0.01
ADAM_STEP = 10

V7X_VMEM_LIMIT = 56 * 1024 * 1024
TOKEN_TILE = 512
FFN_TILE = 256
WGRAD_TILE = 2048
WHOLE_TILE = 1024

GELU_K0 = math.sqrt(2.0 / math.pi)
GELU_K1 = 0.044715


def _params(sem=None):
    return pltpu.CompilerParams(dimension_semantics=sem, vmem_limit_bytes=V7X_VMEM_LIMIT)


def _dot(a, b):
    return jnp.dot(a, b, preferred_element_type=F32)


def _dot_nt(a, b):
    return lax.dot_general(a, b, (((1,), (1,)), ((), ())), preferred_element_type=F32)


def _dot_tn(a, b):
    return lax.dot_general(a, b, (((0,), (0,)), ((), ())), preferred_element_type=F32)


def _sigmoid(x):
    return 1.0 / (1.0 + jnp.exp(-x))


def _rms_r(xv):
    return lax.rsqrt(jnp.mean(xv * xv, axis=-1, keepdims=True) + EPS)


def _rms_bwd(xv, r, nw, dh):
    t = dh * nw
    dx = r * t - xv * (r * r * r * jnp.mean(t * xv, axis=-1, keepdims=True))
    dn = jnp.sum(dh * xv * r, axis=0, keepdims=True)
    return dx, dn


def _gelu(x):
    t = jnp.tanh(GELU_K0 * (x + GELU_K1 * x * x * x))
    return 0.5 * x * (1.0 + t)


def _gelu_grad(x):
    t = jnp.tanh(GELU_K0 * (x + GELU_K1 * x * x * x))
    return 0.5 * (1.0 + t) + 0.5 * x * (1.0 - t * t) * (GELU_K0 * (1.0 + 3.0 * GELU_K1 * x * x))


def _expm1_neg(x):
    p = 1.0 + x * (1.0 / 8.0)
    for k in (7.0, 6.0, 5.0, 4.0, 3.0, 2.0):
        p = 1.0 + x * (1.0 / k) * p
    return jnp.where(x > -0.25, x * p, jnp.exp(x) - 1.0)


def _log_sigmoid(x):
    return jnp.minimum(x, 0.0) - jnp.log(1.0 + jnp.exp(-jnp.abs(x)))


def _tile(s):
    return min(TOKEN_TILE, s)


def _ffn_fwd(x, nw, wg, wu, wd, tgt=None, rider=None):
    s, d = x.shape
    nb, fb, _ = wg.shape
    tm = min(FFN_TILE, s)
    ni = s // tm
    assert s % tm == 0
    with_loss = tgt is not None
    n_in, n_out = 5 + with_loss, 5 + with_loss

    def body(*refs):
        ins, outs, _, copies = _split_refs(refs, n_in, n_out, rider)
        x_ref, nw_ref, wg_ref, wu_ref, wd_ref = ins[:5]
        out_ref, g_ref, u_ref, hb_ref, ab_ref = outs[:5]
        i = pl.program_id(0)
        finish = _ride(copies, i == 0, i == ni - 1)

        xv = x_ref[...]
        hb = (xv * _rms_r(xv) * nw_ref[...]).astype(BF16)
        hb_ref[...] = hb
        y = jnp.zeros((tm, d), F32)
        for jb in range(nb):
            g = _dot_nt(hb, wg_ref[jb])
            u = _dot_nt(hb, wu_ref[jb])
            g_ref[jb] = g.astype(BF16)
            u_ref[jb] = u.astype(BF16)
            ab = (g * _sigmoid(g) * u).astype(BF16)
            ab_ref[jb] = ab
            y = y + _dot(ab, wd_ref[jb])
        y = xv + 0.5 * y
        if with_loss:
            tgt_ref, loss_ref = ins[5], outs[5]
            diff = y - tgt_ref[...]
            out_ref[...] = diff * (1.0 / d)

            @pl.when(i == 0)
            def _():
                loss_ref[...] = jnp.zeros_like(loss_ref)

            loss_ref[...] += jnp.sum(diff * diff) * (0.5 / d)
        else:
            out_ref[...] = y
        finish()

    row = pl.BlockSpec((tm, d), lambda i: (i, 0))
    weight = pl.BlockSpec((nb, fb, d), lambda i: (0, 0, 0), pipeline_mode=pl.Buffered(1))
    in_specs = [row, pl.BlockSpec((1, d), lambda i: (0, 0)), weight, weight, weight]
    args = [x, nw, wg, wu, wd]
    if with_loss:
        in_specs.append(row)
        args.append(tgt)
    blk = pl.BlockSpec((nb, tm, fb), lambda i: (0, i, 0))
    out_shape = [jax.ShapeDtypeStruct((s, d), F32), jax.ShapeDtypeStruct((nb, s, fb), BF16),
                 jax.ShapeDtypeStruct((nb, s, fb), BF16), jax.ShapeDtypeStruct((s, d), BF16),
                 jax.ShapeDtypeStruct((nb, s, fb), BF16)]
    out_specs = [row, blk, blk, row, blk]
    if with_loss:
        out_shape.append(jax.ShapeDtypeStruct((1, 128), F32))
        out_specs.append(pl.BlockSpec((1, 128), lambda i: (0, 0)))
    return _call(body, "ffn_fwd_loss" if with_loss else "ffn_fwd", (ni,), in_specs, out_specs, out_shape, args,
                 rider=rider)


def _call(body, name, grid, in_specs, out_specs, out_shape, args, scratch=(), rider=None):
    in_specs, out_specs, out_shape, scratch = list(in_specs), list(out_specs), list(out_shape), list(scratch)
    extra, aliases = [], {}
    if rider is not None:
        extra = rider.operands()
        aliases = rider.aliases(len(args), len(out_shape))
        in_specs += [ANY] * len(extra)
        out_specs += [ANY] * len(rider.inplace)
        out_shape += rider.out_shape()
        scratch += rider.scratch()
    return pl.pallas_call(
        body, name=name, grid=grid, in_specs=in_specs, out_specs=out_specs, out_shape=out_shape,
        input_output_aliases=aliases, scratch_shapes=scratch,
        compiler_params=_params(("arbitrary",) * len(grid)),
    )(*args, *extra)


def _ffn_bwd_act(x, nw, dy, g, u, wg, wu, wd, name, rider=None):
    s, d = x.shape
    nb, fb, _ = wg.shape
    tm = min(FFN_TILE, s)
    assert s % tm == 0

    def body(*refs):
        ins, outs, _, copies = _split_refs(refs, 8, 5, rider)
        x_ref, nw_ref, dy_ref, g_ref, u_ref, wg_ref, wu_ref, wd_ref = ins
        dx_ref, dg_ref, du_ref, dyb_ref, dnw_ref = outs
        finish = _ride(copies, pl.program_id(0) == 0, pl.program_id(0) == s // tm - 1)
        dyv = dy_ref[...]
        dyb = dyv.astype(BF16)
        dyb_ref[...] = dyb
        dh = jnp.zeros((tm, d), F32)
        for jb in range(nb):
            da = 0.5 * _dot_nt(dyb, wd_ref[jb])
            gv = g_ref[jb].astype(F32)
            sg = _sigmoid(gv)
            dub = (da * (gv * sg)).astype(BF16)
            dgb = (da * u_ref[jb].astype(F32) * (sg * (1.0 + gv * (1.0 - sg)))).astype(BF16)
            dg_ref[jb] = dgb
            du_ref[jb] = dub
            dh = dh + _dot(dgb, wg_ref[jb]) + _dot(dub, wu_ref[jb])
        xv = x_ref[...]
        dx, dn = _rms_bwd(xv, _rms_r(xv), nw_ref[...], dh)
        dx_ref[...] = dyv + dx

        @pl.when(pl.program_id(0) == 0)
        def _():
            dnw_ref[...] = jnp.zeros_like(dnw_ref)

        dnw_ref[...] += dn
        finish()

    row = pl.BlockSpec((tm, d), lambda i: (i, 0))
    vec = pl.BlockSpec((1, d), lambda i: (0, 0))
    blk = pl.BlockSpec((nb, tm, fb), lambda i: (0, i, 0))
    weight = pl.BlockSpec((nb, fb, d), lambda i: (0, 0, 0), pipeline_mode=pl.Buffered(1))
    return _call(
        body, name, (s // tm,), [row, vec, row, blk, blk, weight, weight, weight], [row, blk, blk, row, vec],
        [jax.ShapeDtypeStruct((s, d), F32), jax.ShapeDtypeStruct((nb, s, fb), BF16),
         jax.ShapeDtypeStruct((nb, s, fb), BF16), jax.ShapeDtypeStruct((s, d), BF16),
         jax.ShapeDtypeStruct((1, d), F32)],
        [x, nw, dy, g, u, wg, wu, wd], rider=rider)


def _wgrad(a, b, a_spec, b_spec, out_rows, out_cols, scale, name, tk, rider=None):
    s = a.shape[-2]
    nk = s // tk
    assert s % tk == 0

    def body(*refs):
        (a_ref, b_ref), (out_ref,), (acc,), copies = _split_refs(refs, 2, 1, rider)
        j, k = pl.program_id(0), pl.program_id(1)
        finish = _ride(copies, jnp.logical_and(j == 0, k == 0), jnp.logical_and(j == N_CHIPS - 1, k == nk - 1))

        @pl.when(k == 0)
        def _():
            acc[...] = jnp.zeros_like(acc)

        acc[...] += _dot_tn(a_ref[...], b_ref[...])

        @pl.when(k == nk - 1)
        def _():
            out_ref[...] = (acc[...] * scale).astype(BF16)

        finish()

    outs = _call(
        body, name, (N_CHIPS, nk), [a_spec(tk), b_spec(tk)],
        [pl.BlockSpec((None, out_rows, out_cols), lambda j, k: (j, 0, 0))],
        [jax.ShapeDtypeStruct((N_CHIPS, out_rows, out_cols), BF16)], [a, b],
        scratch=[pltpu.VMEM((out_rows, out_cols), F32)], rider=rider)
    return outs[0] if rider is None else outs


def _wgrad_whole(a, b, col_blocks, name, rider=None):
    s, m = a.shape
    n = b.shape[1]
    tk = min(WHOLE_TILE, s)
    nk = s // tk
    assert s % tk == 0
    out_shape = (N_CHIPS, m, n // N_CHIPS) if col_blocks else (N_CHIPS, m // N_CHIPS, n)

    def body(*refs):
        (a_ref, b_ref), (out_ref,), (acc,), copies = _split_refs(refs, 2, 1, rider)
        k = pl.program_id(0)
        finish = _ride(copies, k == 0, k == nk - 1)

        @pl.when(k == 0)
        def _():
            acc[...] = jnp.zeros_like(acc)

        acc[...] += _dot_tn(a_ref[...], b_ref[...])

        @pl.when(k == nk - 1)
        def _():
            for j in range(N_CHIPS):
                if col_blocks:
                    out_ref[j] = acc[:, j * out_shape[2]:(j + 1) * out_shape[2]].astype(BF16)
                else:
                    out_ref[j] = acc[j * out_shape[1]:(j + 1) * out_shape[1], :].astype(BF16)

        finish()

    outs = _call(
        body, name, (nk,), [pl.BlockSpec((tk, m), lambda k: (k, 0)), pl.BlockSpec((tk, n), lambda k: (k, 0))],
        [pl.BlockSpec(out_shape, lambda k: (0, 0, 0))], [jax.ShapeDtypeStruct(out_shape, BF16)], [a, b],
        scratch=[pltpu.VMEM((m, n), F32)], rider=rider)
    return outs[0] if rider is None else outs


def _ffn_wgrad(stack, shared, scale, name, rider=None):
    s, d = shared.shape
    fb = stack.shape[-1]
    return _wgrad(stack, shared, lambda tk: pl.BlockSpec((None, tk, fb), lambda j, k: (j, k, 0)),
                  lambda tk: pl.BlockSpec((tk, d), lambda j, k: (k, 0)), fb, d, scale, name,
                  min(WGRAD_TILE, s), rider)


def _mix_pre(x, nw, win):
    s, d = x.shape
    nb, _, cb = win.shape
    tm = min(FFN_TILE, s)
    assert s % tm == 0

    def body(x_ref, nw_ref, w_ref, p_ref, hb_ref):
        xv = x_ref[...]
        hb = (xv * _rms_r(xv) * nw_ref[...]).astype(BF16)
        hb_ref[...] = hb
        for j in range(nb):
            p_ref[:, j * cb:(j + 1) * cb] = _dot(hb, w_ref[j])

    row = pl.BlockSpec((tm, d), lambda i: (i, 0))
    return pl.pallas_call(
        body, name="mix_pre", grid=(s // tm,),
        in_specs=[row, pl.BlockSpec((1, d), lambda i: (0, 0)),
                  pl.BlockSpec((nb, d, cb), lambda i: (0, 0, 0), pipeline_mode=pl.Buffered(1))],
        out_specs=[pl.BlockSpec((tm, nb * cb), lambda i: (i, 0)), row],
        out_shape=[jax.ShapeDtypeStruct((s, nb * cb), F32), jax.ShapeDtypeStruct((s, d), BF16)],
        compiler_params=_params(("arbitrary",)),
    )(x, nw, win)


def _mix_pre_bwd(x, nw, dres, dpb, win):
    s, d = x.shape
    nb, _, cb = win.shape
    tm = min(FFN_TILE, s)
    assert s % tm == 0

    def body(x_ref, nw_ref, dres_ref, dp_ref, w_ref, dx_ref, dnw_ref):
        dh = jnp.zeros((tm, d), F32)
        for j in range(nb):
            dh = dh + _dot_nt(dp_ref[:, j * cb:(j + 1) * cb], w_ref[j])
        xv = x_ref[...]
        dx, dn = _rms_bwd(xv, _rms_r(xv), nw_ref[...], dh)
        dx_ref[...] = dres_ref[...] + dx

        @pl.when(pl.program_id(0) == 0)
        def _():
            dnw_ref[...] = jnp.zeros_like(dnw_ref)

        dnw_ref[...] += dn

    row = pl.BlockSpec((tm, d), lambda i: (i, 0))
    vec = pl.BlockSpec((1, d), lambda i: (0, 0))
    return pl.pallas_call(
        body, name="mix_pre_bwd", grid=(s // tm,),
        in_specs=[row, vec, row, pl.BlockSpec((tm, nb * cb), lambda i: (i, 0)),
                  pl.BlockSpec((nb, d, cb), lambda i: (0, 0, 0), pipeline_mode=pl.Buffered(1))],
        out_specs=[row, vec],
        out_shape=[jax.ShapeDtypeStruct((s, d), F32), jax.ShapeDtypeStruct((1, d), F32)],
        compiler_params=_params(("arbitrary",)),
    )(x, nw, dres, dpb, win)


def _mix_post(x, yr, ya, nr, na, wout):
    s, d = x.shape
    h = yr.shape[1]
    tm = _tile(s)

    def body(x_ref, yr_ref, ya_ref, nr_ref, na_ref, w_ref, out_ref):
        yrv = yr_ref[...]
        yav = ya_ref[...]
        onb = (yrv * _rms_r(yrv) * nr_ref[...]).astype(BF16)
        oab = (yav * _rms_r(yav) * na_ref[...]).astype(BF16)
        out_ref[...] = x_ref[...] + _dot(onb, w_ref[0:h, :]) + _dot(oab, w_ref[h:2 * h, :])

    row = pl.BlockSpec((tm, d), lambda i: (i, 0))
    half = pl.BlockSpec((tm, h), lambda i: (i, 0))
    vec = pl.BlockSpec((1, h), lambda i: (0, 0))
    return pl.pallas_call(
        body, name="mix_post", grid=(s // tm,),
        in_specs=[row, half, half, vec, vec, pl.BlockSpec((2 * h, d), lambda i: (0, 0))],
        out_specs=row, out_shape=jax.ShapeDtypeStruct((s, d), F32),
        compiler_params=_params(("arbitrary",)),
    )(x, yr, ya, nr, na, wout)


def _mix_post_bwd(dx, yr, ya, nr, na, wout):
    s, d = dx.shape
    h = yr.shape[1]
    tm = _tile(s)

    def body(dx_ref, yr_ref, ya_ref, nr_ref, na_ref, w_ref,
             dyr_ref, dya_ref, yc_ref, dxb_ref, dnr_ref, dna_ref):
        i = pl.program_id(0)
        dxb = dx_ref[...].astype(BF16)
        dxb_ref[...] = dxb
        dyc = _dot_nt(dxb, w_ref[...])
        yrv = yr_ref[...]
        yav = ya_ref[...]
        rr = _rms_r(yrv)
        ra = _rms_r(yav)
        yc_ref[:, 0:h] = (yrv * rr * nr_ref[...]).astype(BF16)
        yc_ref[:, h:2 * h] = (yav * ra * na_ref[...]).astype(BF16)
        dyr, dnr = _rms_bwd(yrv, rr, nr_ref[...], dyc[:, 0:h])
        dya, dna = _rms_bwd(yav, ra, na_ref[...], dyc[:, h:2 * h])
        dyr_ref[...] = dyr
        dya_ref[...] = dya

        @pl.when(i == 0)
        def _():
            dnr_ref[...] = jnp.zeros_like(dnr_ref)
            dna_ref[...] = jnp.zeros_like(dna_ref)

        dnr_ref[...] += dnr
        dna_ref[...] += dna

    row = pl.BlockSpec((tm, d), lambda i: (i, 0))
    half = pl.BlockSpec((tm, h), lambda i: (i, 0))
    vec = pl.BlockSpec((1, h), lambda i: (0, 0))
    return pl.pallas_call(
        body, name="mix_post_bwd", grid=(s // tm,),
        in_specs=[row, half, half, vec, vec, pl.BlockSpec((2 * h, d), lambda i: (0, 0))],
        out_specs=[half, half, pl.BlockSpec((tm, 2 * h), lambda i: (i, 0)), row, vec, vec],
        out_shape=[jax.ShapeDtypeStruct((s, h), F32), jax.ShapeDtypeStruct((s, h), F32),
                   jax.ShapeDtypeStruct((s, 2 * h), BF16), jax.ShapeDtypeStruct((s, d), BF16),
                   jax.ShapeDtypeStruct((1, h), F32), jax.ShapeDtypeStruct((1, h), F32)],
        compiler_params=_params(("arbitrary",)),
    )(dx, yr, ya, nr, na, wout)


def _shift_down(xv, s, prev8):
    rolled = pltpu.roll(xv, s, 0)
    row8 = lax.broadcasted_iota(jnp.int32, prev8.shape, 0)
    head = jnp.where(row8 < s, pltpu.roll(prev8, s, 0), rolled[0:8, :])
    return jnp.concatenate([head, rolled[8:, :]], axis=0)


def _shift_up(xv, s, next8):
    n = xv.shape[0]
    rolled = pltpu.roll(xv, n - s, 0)
    row8 = lax.broadcasted_iota(jnp.int32, next8.shape, 0)
    tail = jnp.where(row8 >= 8 - s, pltpu.roll(next8, 8 - s, 0), rolled[n - 8:, :])
    return jnp.concatenate([rolled[:n - 8, :], tail], axis=0)


def _scan_fwd(a, b):
    n = a.shape[0]
    row = lax.broadcasted_iota(jnp.int32, a.shape, 0)
    s = 1
    while s < n:
        ok = row >= s
        b = jnp.where(ok, a * pltpu.roll(b, s, 0) + b, b)
        a = jnp.where(ok, a * pltpu.roll(a, s, 0), a)
        s *= 2
    return b


def _scan_bwd(a, b):
    n = a.shape[0]
    row = lax.broadcasted_iota(jnp.int32, a.shape, 0)
    s = 1
    while s < n:
        ok = row < n - s
        b = jnp.where(ok, a * pltpu.roll(b, n - s, 0) + b, b)
        a = jnp.where(ok, a * pltpu.roll(a, n - s, 0), a)
        s *= 2
    return b


def _rglru_gates(xv, prev8, cw_ref, cb_ref, wa_ref, ba_ref, wx_ref, bx_ref, lam_ref):
    x1 = _shift_down(xv, 1, prev8)
    x2 = _shift_down(xv, 2, prev8)
    x3 = _shift_down(xv, 3, prev8)
    xc = cw_ref[3:4, :] * xv + cw_ref[2:3, :] * x1 + cw_ref[1:2, :] * x2 + cw_ref[0:1, :] * x3 + cb_ref[...]
    xcb = xc.astype(BF16)
    r = _sigmoid(_dot(xcb, wa_ref[...]) + ba_ref[...])
    ig = _sigmoid(_dot(xcb, wx_ref[...]) + bx_ref[...])
    c = RG_C * _log_sigmoid(lam_ref[...])
    la = r * c
    a = jnp.exp(la)
    m = jnp.sqrt(-_expm1_neg(2.0 * la))
    return (x1, x2, x3), xc, xcb, r, ig, c, a, m


def _rglru_fwd(proj, cw, cb, wa, ba, wx, bx, lam):
    s = proj.shape[0]
    w = D_RNN
    tm = _tile(s)

    def body(xr_ref, gate_ref, cw_ref, cb_ref, wa_ref, ba_ref, wx_ref, bx_ref, lam_ref,
             y_ref, h_ref, prev, hlast):
        @pl.when(pl.program_id(0) == 0)
        def _():
            prev[...] = jnp.zeros_like(prev)
            hlast[...] = jnp.zeros_like(hlast)

        xv = xr_ref[...]
        _, xc, _, _, ig, _, a, m = _rglru_gates(xv, prev[...], cw_ref, cb_ref, wa_ref, ba_ref,
                                                wx_ref, bx_ref, lam_ref)
        b = m * (ig * xc)
        row = lax.broadcasted_iota(jnp.int32, b.shape, 0)
        b = jnp.where(row == 0, b + a * hlast[...], b)
        h = _scan_fwd(a, b)
        h_ref[...] = h
        y_ref[...] = h * _gelu(gate_ref[...])
        prev[...] = xv[tm - 8:, :]
        hlast[...] = h[tm - 1:tm, :]

    vec = pl.BlockSpec((1, w), lambda i: (0, 0))
    sq = pl.BlockSpec((w, w), lambda i: (0, 0))
    out = pl.BlockSpec((tm, w), lambda i: (i, 0))
    return pl.pallas_call(
        body, name="rglru_fwd", grid=(s // tm,),
        in_specs=[pl.BlockSpec((tm, w), lambda i: (i, 0)), pl.BlockSpec((tm, w), lambda i: (i, 1)),
                  pl.BlockSpec((CONV_W, w), lambda i: (0, 0)), vec, sq, vec, sq, vec, vec],
        out_specs=[out, out],
        out_shape=[jax.ShapeDtypeStruct((s, w), F32), jax.ShapeDtypeStruct((s, w), F32)],
        scratch_shapes=[pltpu.VMEM((8, w), F32), pltpu.VMEM((1, w), F32)],
        compiler_params=_params(("arbitrary",)),
    )(proj, proj, cw, cb, wa, ba, wx, bx, lam)


def _rglru_bwd(proj, hseq, dyr, cw, cb, wa, ba, wx, bx, lam):
    s = proj.shape[0]
    w = D_RNN
    tm = _tile(s)
    nt = s // tm
    t8 = tm // 8

    def body(xr_ref, xp_ref, gate_ref, h_ref, hp_ref, dy_ref, cw_ref, cb_ref, wa_ref, ba_ref,
             wx_ref, bx_ref, lam_ref,
             dxr_ref, dgate_ref, dcw_ref, dcb_ref, dwa_ref, dba_ref, dwx_ref, dbx_ref, dlam_ref,
             carry, dxc_next):
        i = pl.program_id(0)
        first_tile = i == nt - 1

        @pl.when(i == 0)
        def _():
            carry[...] = jnp.zeros_like(carry)
            dxc_next[...] = jnp.zeros_like(dxc_next)
            for ref in (dcw_ref, dcb_ref, dwa_ref, dba_ref, dwx_ref, dbx_ref, dlam_ref):
                ref[...] = jnp.zeros_like(ref)

        xv = xr_ref[...]
        prev8 = jnp.where(first_tile, 0.0, xp_ref[...])
        hprev8 = jnp.where(first_tile, 0.0, hp_ref[...])
        (x1, x2, x3), xc, xcb, r, ig, c, a, m = _rglru_gates(
            xv, prev8, cw_ref, cb_ref, wa_ref, ba_ref, wx_ref, bx_ref, lam_ref)
        gv = gate_ref[...]
        hv = h_ref[...]
        dy = dy_ref[...]
        dgate_ref[...] = (dy * hv * _gelu_grad(gv)).astype(BF16)
        dh = dy * _gelu(gv)
        row = lax.broadcasted_iota(jnp.int32, dh.shape, 0)
        dh = jnp.where(row == tm - 1, dh + carry[...], dh)
        a_up = jnp.where(row == tm - 1, 0.0, pltpu.roll(a, tm - 1, 0))
        lam_t = _scan_bwd(a_up, dh)
        carry[...] = a[0:1, :] * lam_t[0:1, :]
        hm1 = _shift_down(hv, 1, hprev8)
        da = lam_t * hm1
        ixc = ig * xc
        dm = lam_t * ixc
        dig = lam_t * m * xc
        dxc = lam_t * m * ig
        dla = da * a - dm * (a * a) / m
        dr = dla * c
        dlam_ref[...] += jnp.sum(dla * r, axis=0, keepdims=True)
        dpa = dr * r * (1.0 - r)
        dpi = dig * ig * (1.0 - ig)
        dba_ref[...] += jnp.sum(dpa, axis=0, keepdims=True)
        dbx_ref[...] += jnp.sum(dpi, axis=0, keepdims=True)
        dpab = dpa.astype(BF16)
        dpib = dpi.astype(BF16)
        dwa_ref[...] += _dot_tn(xcb, dpab)
        dwx_ref[...] += _dot_tn(xcb, dpib)
        dxc = dxc + _dot_nt(dpab, wa_ref[...]) + _dot_nt(dpib, wx_ref[...])
        dcb_ref[...] += jnp.sum(dxc, axis=0, keepdims=True)
        dcw_ref[3:4, :] += jnp.sum(dxc * xv, axis=0, keepdims=True)
        dcw_ref[2:3, :] += jnp.sum(dxc * x1, axis=0, keepdims=True)
        dcw_ref[1:2, :] += jnp.sum(dxc * x2, axis=0, keepdims=True)
        dcw_ref[0:1, :] += jnp.sum(dxc * x3, axis=0, keepdims=True)
        nxt = dxc_next[...]
        dxr = (cw_ref[3:4, :] * dxc + cw_ref[2:3, :] * _shift_up(dxc, 1, nxt)
               + cw_ref[1:2, :] * _shift_up(dxc, 2, nxt) + cw_ref[0:1, :] * _shift_up(dxc, 3, nxt))
        dxr_ref[...] = dxr.astype(BF16)
        dxc_next[...] = dxc[0:8, :]

        @pl.when(first_tile)
        def _():
            lv = lam_ref[...]
            dlam_ref[...] = dlam_ref[...] * (RG_C * _sigmoid(-lv))

    rev = lambda i: nt - 1 - i
    vec = pl.BlockSpec((1, w), lambda i: (0, 0))
    sq = pl.BlockSpec((w, w), lambda i: (0, 0))
    cur = lambda col: pl.BlockSpec((tm, w), lambda i: (rev(i), col))
    before = lambda cols: pl.BlockSpec((8, w), lambda i: (jnp.maximum(rev(i) * t8 - 1, 0), 0))
    return pl.pallas_call(
        body, name="rglru_bwd", grid=(nt,),
        in_specs=[cur(0), before(None), cur(1), cur(0), before(None), cur(0),
                  pl.BlockSpec((CONV_W, w), lambda i: (0, 0)), vec, sq, vec, sq, vec, vec],
        out_specs=[cur(0), cur(0), pl.BlockSpec((CONV_W, w), lambda i: (0, 0)), vec, sq, vec, sq, vec, vec],
        out_shape=[jax.ShapeDtypeStruct((s, w), BF16), jax.ShapeDtypeStruct((s, w), BF16),
                   jax.ShapeDtypeStruct((CONV_W, w), F32), jax.ShapeDtypeStruct((1, w), F32),
                   jax.ShapeDtypeStruct((w, w), F32), jax.ShapeDtypeStruct((1, w), F32),
                   jax.ShapeDtypeStruct((w, w), F32), jax.ShapeDtypeStruct((1, w), F32),
                   jax.ShapeDtypeStruct((1, w), F32)],
        scratch_shapes=[pltpu.VMEM((1, w), F32), pltpu.VMEM((8, w), F32)],
        compiler_params=_params(("arbitrary",)),
    )(proj, proj, proj, hseq, hseq, dyr, cw, cb, wa, ba, wx, bx, lam)


def _sb_logs(z, valid):
    l1p = jnp.log(1.0 + jnp.exp(-jnp.abs(z)))
    lb = jnp.minimum(z, 0.0) - l1p
    lm = jnp.where(valid, -jnp.maximum(z, 0.0) - l1p, 0.0)
    return lb, lm


class _Window:
    def __init__(self):
        blk, win, cut = ATT_BLOCK, ATT_WINDOW, ATT_SPLIT
        self.row = lax.broadcasted_iota(jnp.int32, (blk, win), 0)
        self.col = lax.broadcasted_iota(jnp.int32, (blk, win), 1)

        def tri(n, later):
            j = lax.broadcasted_iota(jnp.int32, (n, n), 0)
            s = lax.broadcasted_iota(jnp.int32, (n, n), 1)
            return jnp.where((j > s) if later else (j < s), 1.0, 0.0).astype(BF16)

        self.later = (tri(cut, True), tri(win - cut, True))
        self.earlier = (tri(cut, False), tri(win - cut, False))

    def place(self, qi, g):
        end = (qi + 1) * ATT_BLOCK - g * ATT_WINDOW
        start = pl.multiple_of(jnp.maximum(end - ATT_WINDOW, 0), ATT_BLOCK)
        valid = start + self.col < jnp.minimum(qi * ATT_BLOCK + self.row, end)
        return start, valid

    @staticmethod
    def _parts(xv):
        hi = xv.astype(BF16)
        lo = (xv - hi.astype(F32)).astype(BF16)
        cut = ATT_SPLIT
        sums = (jnp.sum(xv[:, :cut], axis=1, keepdims=True), jnp.sum(xv[:, cut:], axis=1, keepdims=True))
        return (hi[:, :cut], lo[:, :cut]), (hi[:, cut:], lo[:, cut:]), sums

    def sums_after(self, xv, carry):
        (h0, l0), (h1, l1), (s0, s1) = self._parts(xv)
        first = _dot(h0, self.later[0]) + _dot(l0, self.later[0]) + (s1 + carry)
        last = _dot(h1, self.later[1]) + _dot(l1, self.later[1]) + carry
        return jnp.concatenate([first, last], axis=1), s0 + s1

    def sums_before(self, xv, carry):
        (h0, l0), (h1, l1), (s0, s1) = self._parts(xv)
        first = _dot(h0, self.earlier[0]) + _dot(l0, self.earlier[0]) + carry
        last = _dot(h1, self.earlier[1]) + _dot(l1, self.earlier[1]) + (s0 + carry)
        return jnp.concatenate([first, last], axis=1), s0 + s1


class _HeadPair:
    def __init__(self):
        lanes = 2 * HEAD_DIM
        lane = lax.broadcasted_iota(jnp.int32, (1, lanes), 1)
        self.masks = [lane // HEAD_DIM == h for h in (0, 1)]
        i = lax.broadcasted_iota(jnp.int32, (lanes, lanes), 0) // HEAD_DIM
        j = lax.broadcasted_iota(jnp.int32, (lanes, lanes), 1) // HEAD_DIM
        self.same_head = jnp.where(i == j, 1.0, 0.0).astype(BF16)

    def only(self, h, xv):
        return jnp.where(self.masks[h], xv, jnp.zeros_like(xv))

    def merge(self, per_head):
        return jnp.where(self.masks[0], per_head[0], per_head[1])

    def mean(self, xv):
        hi = xv.astype(BF16)
        lo = (xv - hi.astype(F32)).astype(BF16)
        return (_dot(hi, self.same_head) + _dot(lo, self.same_head)) * (1.0 / HEAD_DIM)

    def rms_r(self, xv):
        return lax.rsqrt(self.mean(xv * xv) + EPS)

    def rms_bwd(self, xv, r, nw, dh):
        t = dh * nw
        dx = r * t - xv * (r * r * r * self.mean(t * xv))
        dn = jnp.sum(dh * xv * r, axis=0, keepdims=True)
        return dx, dn[:, :HEAD_DIM] + dn[:, HEAD_DIM:]


def _attn_fwd(proj, qg, kg, rider=None):
    s = proj.shape[0]
    blk, win, dh = ATT_BLOCK, ATT_WINDOW, HEAD_DIM
    nq = s // blk
    scale = 1.0 / math.sqrt(dh)
    heads = (0, 1)
    assert s >= win and s % blk == 0

    def body(*refs):
        (q_ref, k_ref, v_ref, qg_ref, kg_ref), (o_ref,), (qn, kn, vb), copies = _split_refs(refs, 5, 1, rider)
        finish = _ride(copies, pl.program_id(0) == 0, pl.program_id(0) == N_HEADS // 2 - 1)
        wd, hp = _Window(), _HeadPair()
        qv = q_ref[...]
        qn[...] = (qv * hp.rms_r(qv) * qg_ref[...] * scale).astype(BF16)
        kv = k_ref[...]
        kn[...] = (kv * hp.rms_r(kv) * kg_ref[...]).astype(BF16)
        vb[...] = v_ref[...].astype(BF16)

        def q_step(qi, _):
            qoff = pl.multiple_of(qi * blk, blk)
            qt = qn[pl.ds(qoff, blk), :]
            qts = [hp.only(h, qt) for h in heads]

            def more(carry):
                g, live = carry[:2]
                return jnp.logical_and((qi + 1) * blk - g * win > 0, live > 0)

            def window(carry):
                g, _, accs, runs = carry
                start, valid = wd.place(qi, g)
                kt = kn[pl.ds(start, win), :]
                zs = [_dot_nt(qts[h], kt) for h in heads]
                logs = [_sb_logs(z, valid) for z in zs]
                sums = [wd.sums_after(logs[h][1], runs[h]) for h in heads]
                wgts = [jnp.where(valid, jnp.exp(logs[h][0] + sums[h][0]), 0.0).astype(BF16) for h in heads]
                vt = vb[pl.ds(start, win), :]
                accs = tuple(accs[h] + _dot(wgts[h], vt) for h in heads)
                runs = tuple(runs[h] + sums[h][1] for h in heads)
                live = (jnp.maximum(jnp.max(runs[0]), jnp.max(runs[1])) > EXP_ZERO).astype(jnp.int32)
                return g + 1, live, accs, runs

            zero = lambda cols: tuple(jnp.zeros((blk, cols), F32) for _ in heads)
            _, _, accs, _ = lax.while_loop(more, window, (jnp.int32(0), jnp.int32(1), zero(2 * dh), zero(1)))
            o_ref[pl.ds(qoff, blk), :] = hp.merge(accs)
            return 0

        lax.fori_loop(0, nq, q_step, 0)
        finish()

    pair = lambda group: pl.BlockSpec((s, 2 * dh), lambda p: (0, group * (D_ATT // (2 * dh)) + p))
    vec = pl.BlockSpec((1, 2 * dh), lambda p: (0, 0))
    return _call(
        body, "attn_fwd", (N_HEADS // 2,), [pair(2), pair(3), pair(4), vec, vec], [pair(0)],
        [jax.ShapeDtypeStruct((s, D_ATT), F32)], [proj, proj, proj, jnp.tile(qg, (1, 2)), jnp.tile(kg, (1, 2))],
        scratch=[pltpu.VMEM((s, 2 * dh), BF16)] * 3, rider=rider)


def _attn_bwd(proj, dya, qg, kg, rider=None):
    s = proj.shape[0]
    blk, win, dh = ATT_BLOCK, ATT_WINDOW, HEAD_DIM
    nq = s // blk
    max_windows = -(-s // win) + 1
    scale = 1.0 / math.sqrt(dh)
    steps = N_HEADS // 2
    heads = (0, 1)
    assert s >= win and s % blk == 0

    def body(*refs):
        ins, outs, scratch, copies = _split_refs(refs, 6, 5, rider)
        q_ref, k_ref, v_ref, do_ref, qg_ref, kg_ref = ins
        dq_ref, dk_ref, dv_ref, dqg_ref, dkg_ref = outs
        qn, kn, vb, dob, runs_ref, dqn, dkn, dvn = scratch
        finish = _ride(copies, pl.program_id(0) == 0, pl.program_id(0) == steps - 1)
        wd, hp = _Window(), _HeadPair()

        @pl.when(pl.program_id(0) == 0)
        def _():
            dqg_ref[...] = jnp.zeros_like(dqg_ref)
            dkg_ref[...] = jnp.zeros_like(dkg_ref)

        qv = q_ref[...]
        qn[...] = (qv * hp.rms_r(qv) * qg_ref[...] * scale).astype(BF16)
        kv = k_ref[...]
        kn[...] = (kv * hp.rms_r(kv) * kg_ref[...]).astype(BF16)
        vb[...] = v_ref[...].astype(BF16)
        dob[...] = do_ref[...].astype(BF16)
        dkn[...] = jnp.zeros_like(dkn)
        dvn[...] = jnp.zeros_like(dvn)

        def q_step(qi, _):
            qoff = pl.multiple_of(qi * blk, blk)
            qt = qn[pl.ds(qoff, blk), :]
            dot = dob[pl.ds(qoff, blk), :]
            qts = [hp.only(h, qt) for h in heads]
            dots = [hp.only(h, dot) for h in heads]

            zero = lambda cols: tuple(jnp.zeros((blk, cols), F32) for _ in heads)

            def logs_of(g):
                start, valid = wd.place(qi, g)
                kt = kn[pl.ds(start, win), :]
                return [_sb_logs(_dot_nt(qts[h], kt), valid) for h in heads]

            def row_sums(logs):
                return tuple(jnp.sum(logs[h][1], axis=1, keepdims=True) for h in heads)

            def still_live(runs):
                return jnp.maximum(jnp.max(runs[0]), jnp.max(runs[1])) > EXP_ZERO

            def window_grads(g, logs, runs, esums):
                start, valid = wd.place(qi, g)
                kt = kn[pl.ds(start, win), :]
                vt = vb[pl.ds(start, win), :]
                dws = [_dot_nt(dots[h], vt) for h in heads]
                tails = [wd.sums_after(logs[h][1], runs[h])[0] for h in heads]
                wgts = [jnp.where(valid, jnp.exp(logs[h][0] + tails[h]), 0.0) for h in heads]
                es = [dws[h] * wgts[h] for h in heads]
                befores = [wd.sums_before(es[h], esums[h]) for h in heads]
                dzbs = []
                for h in heads:
                    beta = jnp.exp(logs[h][0])
                    dz = jnp.where(valid, es[h] * (1.0 - beta) - befores[h][0] * beta, 0.0)
                    dzbs.append(dz.astype(BF16))
                dkn[pl.ds(start, win), :] += _dot_tn(dzbs[0], qts[0]) + _dot_tn(dzbs[1], qts[1])
                dvn[pl.ds(start, win), :] += (_dot_tn(wgts[0].astype(BF16), dots[0])
                                              + _dot_tn(wgts[1].astype(BF16), dots[1]))
                return tuple(_dot(dzbs[h], kt) for h in heads), tuple(befores[h][1] for h in heads)

            logs0 = logs_of(0)
            runs1 = row_sums(logs0)

            def one_window():
                return window_grads(0, logs0, zero(1), zero(1))[0]

            def all_windows():
                def more(carry):
                    g, live = carry[:2]
                    return jnp.logical_and((qi + 1) * blk - g * win > 0, live > 0)

                def run_window(carry):
                    g, _, runs = carry
                    for h in heads:
                        runs_ref[h, g] = runs[h]
                    sums = row_sums(logs_of(g))
                    runs = tuple(runs[h] + sums[h] for h in heads)
                    return g + 1, still_live(runs).astype(jnp.int32), runs

                for h in heads:
                    runs_ref[h, 0] = jnp.zeros((blk, 1), F32)
                windows, _, _ = lax.while_loop(more, run_window, (jnp.int32(1), jnp.int32(1), runs1))

                def k_window(gg, carry):
                    dq_accs, esums = carry
                    g = windows - 1 - gg
                    parts, totals = window_grads(g, logs_of(g), [runs_ref[h, g] for h in heads], esums)
                    return (tuple(dq_accs[h] + parts[h] for h in heads),
                            tuple(esums[h] + totals[h] for h in heads))

                return lax.fori_loop(0, windows, k_window, (zero(2 * dh), zero(1)))[0]

            earlier_keys = (qi + 1) * blk - win > 0
            dq_accs = lax.cond(jnp.logical_and(earlier_keys, still_live(runs1)), all_windows, one_window)
            dqn[pl.ds(qoff, blk), :] = hp.merge(dq_accs)
            return 0

        lax.fori_loop(0, nq, q_step, 0)

        dq, dqg = hp.rms_bwd(qv, hp.rms_r(qv), qg_ref[...] * scale, dqn[...])
        dq_ref[...] = dq.astype(BF16)
        dqg_ref[...] += dqg * scale
        dk, dkg = hp.rms_bwd(kv, hp.rms_r(kv), kg_ref[...], dkn[...])
        dk_ref[...] = dk.astype(BF16)
        dkg_ref[...] += dkg
        dv_ref[...] = dvn[...].astype(BF16)
        finish()

    pair = lambda group: pl.BlockSpec((s, 2 * dh), lambda p: (0, group * (D_ATT // (2 * dh)) + p))
    vec2 = pl.BlockSpec((1, 2 * dh), lambda p: (0, 0))
    vec = pl.BlockSpec((1, dh), lambda p: (0, 0))
    return _call(
        body, "attn_bwd", (steps,), [pair(2), pair(3), pair(4), pair(0), vec2, vec2],
        [pair(0), pair(0), pair(0), vec, vec],
        [jax.ShapeDtypeStruct((s, D_ATT), BF16)] * 3 + [jax.ShapeDtypeStruct((1, dh), F32)] * 2,
        [proj, proj, proj, dya, jnp.tile(qg, (1, 2)), jnp.tile(kg, (1, 2))],
        scratch=[pltpu.VMEM((s, 2 * dh), BF16)] * 4 + [pltpu.VMEM((2, max_windows, blk, 1), F32)]
        + [pltpu.VMEM((s, 2 * dh), F32)] * 3, rider=rider)


def _block_diag(w):
    n, c, d = w.shape
    return jnp.einsum("ncd,nm->ncmd", w, jnp.eye(n, dtype=w.dtype)).reshape(n * c, n * d)


def _diag_blocks(full, n):
    c = full.shape[0] // n
    return jnp.stack([full[i * c:(i + 1) * c, i * c:(i + 1) * c] for i in range(n)])


FFN1 = ["ffn1_w_gate", "ffn1_w_up", "ffn1_w_down"]
FFN2 = ["ffn2_w_gate", "ffn2_w_up", "ffn2_w_down"]
MIXER = ["w_in", "w_out"]


def _pair_sums(gb, names, where):
    theirs = _pair_exchange([gb[n] for n in names], "pair_exchange_" + names[0])
    pair, own = zip(*[_pair_sum(gb[n], t, where, "pair_sum_" + n) for n, t in zip(names, theirs)])
    return _chip_rider(list(pair), list(own))


def _local_step(x, tgt, stacks, conv_stack, small, where):
    big = dict(zip(FFN1, _gather_weights([stacks[n] for n in FFN1], [])))
    wa = _block_diag(small["rg_w_a"]).astype(BF16)
    wx = _block_diag(small["rg_w_x"]).astype(BF16)

    x1, g1, u1, hb1, ab1, *landed = _ffn_fwd(x, small["ffn1_norm"], *[big[n] for n in FFN1],
                                             rider=_gather_rider([stacks[n] for n in MIXER], [conv_stack]))
    big.update(zip(MIXER, _forward_weights(landed[:len(MIXER)], "forward_mixer_weights")))
    conv_w = jnp.transpose(landed[-1], (1, 0, 2)).reshape(CONV_W, D_RNN)
    wout = big["w_out"].reshape(D_MODEL, D_MODEL)
    rg = (conv_w, small["conv_b"], wa, small["rg_b_a"], wx, small["rg_b_x"], small["rg_lambda"])
    proj, hb2 = _mix_pre(x1, small["mix_norm"], big["w_in"])
    yr, hseq = _rglru_fwd(proj, *rg)
    ya, *landed = _attn_fwd(proj, small["q_norm"], small["k_norm"], _gather_rider([stacks[n] for n in FFN2], []))
    big.update(zip(FFN2, _forward_weights(landed, "forward_ffn2_weights")))
    x2 = _mix_post(x1, yr, ya, small["rnn_out_norm"], small["attn_out_norm"], wout)
    dx3, g2, u2, hb3, ab3, loss = _ffn_fwd(x2, small["ffn2_norm"], *[big[n] for n in FFN2], tgt)

    gb, gs, slots = {}, {}, {}
    dx2, dg2, du2, dyb2, gs["ffn2_norm"] = _ffn_bwd_act(x2, small["ffn2_norm"], dx3, g2, u2, *[big[n] for n in FFN2],
                                                        "ffn2_bwd")
    gb["ffn2_w_gate"] = _ffn_wgrad(dg2, hb3, 1.0, "wgrad_gate_ffn2")
    gb["ffn2_w_up"] = _ffn_wgrad(du2, hb3, 1.0, "wgrad_up_ffn2")
    gb["ffn2_w_down"] = _ffn_wgrad(ab3, dyb2, 0.5, "wgrad_down_ffn2")
    dyr, dya, ycat, dxb2, gs["rnn_out_norm"], gs["attn_out_norm"] = _mix_post_bwd(
        dx2, yr, ya, small["rnn_out_norm"], small["attn_out_norm"], wout)
    gb["w_out"] = _wgrad_whole(ycat, dxb2, False, "wgrad_out")
    early = FFN2 + ["w_out"]
    dq, dk, dv, gs["q_norm"], gs["k_norm"], *done = _attn_bwd(
        proj, dya, small["q_norm"], small["k_norm"], _pair_sums(gb, early, where))
    slots.update(zip(early, done))
    dxr, dgate, gs["conv_w"], gs["conv_b"], dwa, gs["rg_b_a"], dwx, gs["rg_b_x"], gs["rg_lambda"] = _rglru_bwd(
        proj, hseq, dyr, *rg)
    gs["rg_w_a"] = _diag_blocks(dwa, RNN_BLOCKS)
    gs["rg_w_x"] = _diag_blocks(dwx, RNN_BLOCKS)
    dpb = jnp.concatenate([dxr, dgate, dq, dk, dv], axis=1)
    dx1, gs["mix_norm"] = _mix_pre_bwd(x1, small["mix_norm"], dx2, dpb, big["w_in"])
    dx0, dg1, du1, dyb1, gs["ffn1_norm"] = _ffn_bwd_act(x, small["ffn1_norm"], dx1, g1, u1, *[big[n] for n in FFN1],
                                                        "ffn1_bwd")

    mine = _place_shard(_pack([gs[n] for n in SMALL]), where, F32, "place_small_grads", by_device=True)
    gb["ffn1_w_gate"], everyone = _ffn_wgrad(dg1, hb1, 1.0, "wgrad_gate_ffn1", _small_rider(mine))
    gb["ffn1_w_up"], slots["ffn1_w_gate"] = _ffn_wgrad(
        du1, hb1, 1.0, "wgrad_up_ffn1", _pair_sums(gb, ["ffn1_w_gate"], where))
    gb["ffn1_w_down"], slots["ffn1_w_up"] = _ffn_wgrad(
        ab1, dyb1, 0.5, "wgrad_down_ffn1", _pair_sums(gb, ["ffn1_w_up"], where))
    gb["w_in"], slots["ffn1_w_down"] = _wgrad_whole(
        hb2, dpb, True, "wgrad_in", _pair_sums(gb, ["ffn1_w_down"], where))
    last = _pair_sums(gb, ["w_in"], where)
    slots["w_in"], = _chip_exchange(last.plain, last.inplace)
    return loss[0, 0], dx0, slots, gs, everyone


ANY = pl.BlockSpec(memory_space=pl.ANY)


def _place():
    x, y, c = lax.axis_index("x"), lax.axis_index("y"), lax.axis_index("c")
    other_chips = [(1 - x, y), (x, 1 - y), (1 - x, 1 - y)]
    return x, y, c, 2 * x + y, other_chips


def _remote(src, dst, send_sem, recv_sem, to):
    return pltpu.make_async_remote_copy(src_ref=src, dst_ref=dst, send_sem=send_sem, recv_sem=recv_sem,
                                        device_id=to, device_id_type=MESH)


def _copy_plan(pairs):
    sends = [functools.partial(_remote, *a) for a, _ in pairs]
    arrivals = [functools.partial(_remote, *b) for _, b in pairs]
    return sends, arrivals


class _Rider:
    def __init__(self, plan, plain, inplace, n_copies=None):
        self.plan, self.plain, self.inplace = plan, list(plain), list(inplace)
        self.n_copies = n_copies or 3 * len(self.inplace)

    def operands(self):
        return self.plain + self.inplace

    def out_shape(self):
        return [jax.ShapeDtypeStruct(a.shape, a.dtype) for a in self.inplace]

    def aliases(self, inputs_before, outputs_before):
        return {inputs_before + len(self.plain) + k: outputs_before + k for k in range(len(self.inplace))}

    def scratch(self):
        return [pltpu.SemaphoreType.DMA((self.n_copies,))] * 2


def _split_refs(refs, n_in, n_out, rider):
    if rider is None:
        return refs[:n_in], refs[n_in:n_in + n_out], refs[n_in + n_out:], None
    r_in, r_out = len(rider.operands()), len(rider.inplace)
    outs_at = n_in + r_in
    rest = refs[outs_at + n_out + r_out:]
    copies = functools.partial(rider.plan, refs[n_in:n_in + len(rider.plain)],
                               refs[outs_at + n_out:outs_at + n_out + r_out], *rest[-2:])
    return refs[:n_in], refs[outs_at:outs_at + n_out], rest[:-2], copies


def _ride(copies, first, last):
    if copies is None:
        return lambda: None

    @pl.when(first)
    def _():
        _start(copies()[0])

    def finish():
        @pl.when(last)
        def _():
            _finish(*copies())

    return finish


def _gather_rider(split, whole):
    n_split = len(split)
    return _Rider(lambda plain, stacks, ss, rs: _gather_ici(stacks, n_split, ss, rs), [], list(split) + list(whole))


def _chip_rider(sums, slots):
    return _Rider(_chip_copies, sums, slots)


def _start(makers):
    for make in makers:
        make().start()


def _finish(sends, arrivals):
    for make in arrivals:
        make().wait_recv()
    for make in sends:
        make().wait_send()


def _half(rows, c):
    return pl.ds(pl.multiple_of(c * rows, 16), rows)


def _gather_weights(split, whole):
    arrs = list(split) + list(whole)
    n, ns = len(arrs), len(split)

    def body(*refs):
        outs = refs[n:2 * n]
        send_sems, recv_sems, fsend_sems, frecv_sems = refs[2 * n:]
        sends, arrivals = _gather_ici(outs, ns, send_sems, recv_sems)
        passes, passed = _gather_d2d(outs[:ns], fsend_sems, frecv_sems)
        _start(sends)
        for k, make in enumerate(arrivals):
            make().wait_recv()
            if k < 3 * ns:
                passes[k]().start()
        _finish(sends + passes, passed)

    return pl.pallas_call(
        body, name="gather_weights",
        in_specs=[ANY] * n, out_specs=[ANY] * n,
        out_shape=[jax.ShapeDtypeStruct(a.shape, a.dtype) for a in arrs],
        input_output_aliases={i: i for i in range(n)},
        scratch_shapes=[pltpu.SemaphoreType.DMA((3 * n,)), pltpu.SemaphoreType.DMA((3 * n,)),
                        pltpu.SemaphoreType.DMA((3 * ns,)), pltpu.SemaphoreType.DMA((3 * ns,))],
    )(*arrs)


def _gather_ici(stacks, n_split, send_sems, recv_sems):
    x, y, c, me, chips = _place()

    def region(i, chip):
        if i < n_split:
            return stacks[i].at[chip, _half(stacks[i].shape[1] // 2, c)]
        return stacks[i].at[chip]

    pairs = []
    for i in range(len(stacks)):
        for p, (cx, cy) in enumerate(chips):
            k = 3 * i + p
            mine, got = region(i, me), region(i, 2 * cx + cy)
            sems, to = (send_sems.at[k], recv_sems.at[k]), (cx, cy, c)
            pairs.append(((mine, mine, *sems, to), (got, got, *sems, to)))
    return _copy_plan(pairs)


def _gather_d2d(stacks, send_sems, recv_sems):
    x, y, c, _, chips = _place()
    sibling = (x, y, 1 - c)
    pairs = []
    for i, stack in enumerate(stacks):
        rows = stack.shape[1] // 2
        for p, (cx, cy) in enumerate(chips):
            k = 3 * i + p
            got, theirs = stack.at[2 * cx + cy, _half(rows, c)], stack.at[2 * cx + cy, _half(rows, 1 - c)]
            sems = (send_sems.at[k], recv_sems.at[k])
            pairs.append(((got, got, *sems, sibling), (theirs, theirs, *sems, sibling)))
    return _copy_plan(pairs)


def _forward_weights(split, name):
    n = len(split)

    def body(*refs):
        sends, arrivals = _gather_d2d(refs[n:2 * n], *refs[2 * n:])
        _start(sends)
        _finish(sends, arrivals)

    return pl.pallas_call(
        body, name=name,
        in_specs=[ANY] * n, out_specs=[ANY] * n,
        out_shape=[jax.ShapeDtypeStruct(a.shape, a.dtype) for a in split],
        input_output_aliases={i: i for i in range(n)},
        scratch_shapes=[pltpu.SemaphoreType.DMA((3 * n,))] * 2,
    )(*split)


def _pair_exchange(grads, name):
    n = len(grads)

    def body(*refs):
        ins, theirs = refs[:n], refs[n:2 * n]
        send_sems, recv_sems = refs[2 * n:]
        x, y, c, _, _ = _place()
        sibling = (x, y, 1 - c)
        sends = [_remote(ins[k].at[:, _half(grads[k].shape[1] // 2, 1 - c)], theirs[k],
                         send_sems.at[k], recv_sems.at[k], sibling) for k in range(n)]
        for cp in sends:
            cp.start()
        for k in range(n):
            _remote(theirs[k], theirs[k], send_sems.at[k], recv_sems.at[k], sibling).wait_recv()
        for cp in sends:
            cp.wait_send()

    return pl.pallas_call(
        body, name=name,
        in_specs=[ANY] * n, out_specs=[ANY] * n,
        out_shape=[jax.ShapeDtypeStruct((g.shape[0], g.shape[1] // 2, g.shape[2]), g.dtype) for g in grads],
        scratch_shapes=[pltpu.SemaphoreType.DMA((n,))] * 2,
    )(*grads)


def _chip_exchange(sums, slots):
    n = len(sums)

    def body(*refs):
        sends, arrivals = _chip_copies(refs[:n], refs[2 * n:3 * n], *refs[3 * n:])
        _start(sends)
        _finish(sends, arrivals)

    return pl.pallas_call(
        body, name="grad_chip_exchange",
        in_specs=[ANY] * (2 * n), out_specs=[ANY] * n,
        out_shape=[jax.ShapeDtypeStruct(a.shape, a.dtype) for a in slots],
        input_output_aliases={n + k: k for k in range(n)},
        scratch_shapes=[pltpu.SemaphoreType.DMA((3 * n,)), pltpu.SemaphoreType.DMA((3 * n,))],
    )(*sums, *slots)


def _chip_copies(sums, slots, send_sems, recv_sems):
    x, y, c, me, chips = _place()
    pairs = []
    for k in range(len(sums)):
        for p, (cx, cy) in enumerate(chips):
            j = 3 * k + p
            got = slots[k].at[2 * cx + cy]
            sems, to = (send_sems.at[j], recv_sems.at[j]), (cx, cy, c)
            pairs.append(((sums[k].at[2 * cx + cy], slots[k].at[me], *sems, to), (got, got, *sems, to)))
    return _copy_plan(pairs)


def _half_swap(halves):
    n = len(halves)

    def body(*refs):
        outs = refs[n:2 * n]
        send_sems, recv_sems = refs[2 * n:]
        x, y, c, _, _ = _place()
        sibling = (x, y, 1 - c)
        sends = [_remote(outs[k].at[c], outs[k].at[c], send_sems.at[k], recv_sems.at[k], sibling) for k in range(n)]
        for cp in sends:
            cp.start()
        for k in range(n):
            got = outs[k].at[1 - c]
            _remote(got, got, send_sems.at[k], recv_sems.at[k], sibling).wait_recv()
        for cp in sends:
            cp.wait_send()

    return pl.pallas_call(
        body, name="grad_half_swap",
        in_specs=[ANY] * n, out_specs=[ANY] * n,
        out_shape=[jax.ShapeDtypeStruct(a.shape, a.dtype) for a in halves],
        input_output_aliases={k: k for k in range(n)},
        scratch_shapes=[pltpu.SemaphoreType.DMA((n,))] * 2,
    )(*halves)


def _small_rider(stack):
    n_dev = 2 * N_CHIPS

    def plan(_, stacks, send_sems, recv_sems):
        x, y, c, _, _ = _place()
        mine = stacks[0].at[4 * x + 2 * y + c]
        pairs = []
        for k in range(1, n_dev):
            px, py, pc = x ^ ((k >> 2) & 1), y ^ ((k >> 1) & 1), c ^ (k & 1)
            got = stacks[0].at[4 * px + 2 * py + pc]
            sems = (send_sems.at[k - 1], recv_sems.at[k - 1])
            pairs.append(((mine, mine, *sems, (px, py, pc)), (got, got, *sems, (px, py, pc))))
        return _copy_plan(pairs)

    return _Rider(plan, [], [stack], n_dev - 1)


def _row_tile(r):
    return r // 4 if r >= 256 and (r // 4) % 16 == 0 else r


def _prefetch_call(body, name, grid, in_specs, out_specs, out_shape):
    spec = pltpu.PrefetchScalarGridSpec(num_scalar_prefetch=1, grid=grid, in_specs=in_specs, out_specs=out_specs)
    return pl.pallas_call(body, name=name, grid_spec=spec, out_shape=out_shape,
                          compiler_params=_params(("arbitrary",) * len(grid)))


def _place_shard(w2d, where, dtype, name, by_device=False):
    r, c = w2d.shape
    tr = _row_tile(r)
    slots = 2 * N_CHIPS if by_device else N_CHIPS
    slot = (lambda s: 2 * s[1] + s[0]) if by_device else (lambda s: s[1])

    def body(where_ref, w_ref, out_ref):
        out_ref[...] = w_ref[...].astype(dtype)

    return _prefetch_call(
        body, name, (r // tr,), [pl.BlockSpec((tr, c), lambda i, s: (i, 0))],
        pl.BlockSpec((None, tr, c), lambda i, s: (slot(s), i, 0)),
        jax.ShapeDtypeStruct((slots, r, c), dtype))(where, w2d)


def _pair_sum(full, theirs, where, name):
    nb, hs, c = theirs.shape

    def body(where_ref, a_ref, b_ref, out_ref, own_ref):
        total = (a_ref[...].astype(F32) + b_ref[...].astype(F32)).astype(BF16)
        out_ref[...] = total

        @pl.when(pl.program_id(0) == where_ref[1])
        def _():
            own_ref[...] = total

    blk = pl.BlockSpec((None, hs, c), lambda j, s: (j, 0, 0))
    shape = jax.ShapeDtypeStruct(theirs.shape, BF16)
    return _prefetch_call(
        body, name, (nb,), [pl.BlockSpec((None, hs, c), lambda j, s: (j, s[0], 0)), blk],
        [blk, pl.BlockSpec((None, hs, c), lambda j, s: (s[1], 0, 0))], [shape, shape])(where, full, theirs)


def _chip_sum(slots, where, name):
    nb, hs, c = slots.shape
    tr = _row_tile(hs)

    def body(where_ref, a_ref, out_ref):
        total = a_ref[0].astype(F32)
        for j in range(1, nb):
            total = total + a_ref[j].astype(F32)
        out_ref[...] = total

    return _prefetch_call(
        body, name, (hs // tr,), [pl.BlockSpec((nb, tr, c), lambda i, s: (0, i, 0))],
        pl.BlockSpec((None, tr, c), lambda i, s: (s[0], i, 0)),
        jax.ShapeDtypeStruct((2, hs, c), F32))(where, slots)


def _slot_sum(a, name):
    nb, r, c = a.shape
    tr = _row_tile(r)

    def body(a_ref, out_ref):
        total = a_ref[0].astype(F32)
        for j in range(1, nb):
            total = total + a_ref[j].astype(F32)
        out_ref[...] = total

    return pl.pallas_call(
        body, name=name, grid=(r // tr,),
        in_specs=[pl.BlockSpec((nb, tr, c), lambda i: (0, i, 0))],
        out_specs=pl.BlockSpec((tr, c), lambda i: (i, 0)),
        out_shape=jax.ShapeDtypeStruct((r, c), F32), compiler_params=_params(("arbitrary",)),
    )(a)


def _adamw(w, g, m, v, name):
    r, c = w.shape
    tr = _row_tile(r)
    c1 = 1.0 - ADAM_B1 ** ADAM_STEP
    c2 = 1.0 - ADAM_B2 ** ADAM_STEP

    def body(w_ref, g_ref, m_ref, v_ref, d_ref, m2_ref, v2_ref):
        gv = g_ref[...]
        m2 = ADAM_B1 * m_ref[...] + (1.0 - ADAM_B1) * gv
        v2 = ADAM_B2 * v_ref[...] + (1.0 - ADAM_B2) * (gv * gv)
        m2_ref[...] = m2
        v2_ref[...] = v2
        d_ref[...] = -ADAM_LR * ((m2 / c1) / (jnp.sqrt(v2 / c2) + ADAM_EPS) + ADAM_WD * w_ref[...])

    blk = pl.BlockSpec((tr, c), lambda i: (i, 0))
    return pl.pallas_call(
        body, name=name, grid=(r // tr,), in_specs=[blk] * 4, out_specs=[blk] * 3,
        out_shape=[jax.ShapeDtypeStruct((r, c), F32)] * 3, compiler_params=_params(("arbitrary",)),
    )(w, g, m, v)


WEIGHTS = ["ffn1_norm", "ffn1_w_gate", "ffn1_w_up", "ffn1_w_down", "mix_norm", "w_in", "conv_w", "conv_b",
           "rg_w_a", "rg_b_a", "rg_w_x", "rg_b_x", "rg_lambda", "q_norm", "k_norm", "rnn_out_norm",
           "attn_out_norm", "w_out", "ffn2_norm", "ffn2_w_gate", "ffn2_w_up", "ffn2_w_down"]
BIG = ["ffn1_w_gate", "ffn1_w_up", "ffn1_w_down", "w_in", "w_out", "ffn2_w_gate", "ffn2_w_up", "ffn2_w_down"]
SMALL = [n for n in WEIGHTS if n not in BIG]
PACK_LANES = 128
PACK_ROW_ALIGN = 8


def _hidden_major(name, a):
    return jnp.transpose(a) if name.endswith(("w_gate", "w_up")) else a


def _pack(parts):
    flat = jnp.concatenate([p.reshape(-1) for p in parts])
    unit = PACK_LANES * PACK_ROW_ALIGN
    padded = -(-flat.shape[0] // unit) * unit
    return jnp.pad(flat, (0, padded - flat.shape[0])).reshape(-1, PACK_LANES)


def _unpack(packed, shapes):
    flat = packed.reshape(-1)
    out, at = [], 0
    for shp in shapes:
        size = math.prod(shp)
        out.append(flat[at:at + size].reshape(shp))
        at += size
    return out


def kernel(x, ffn1_norm, ffn1_w_gate, ffn1_w_up, ffn1_w_down, mix_norm, w_in, conv_w, conv_b, rg_w_a, rg_b_a, rg_w_x, rg_b_x, rg_lambda, q_norm, k_norm, rnn_out_norm, attn_out_norm, w_out, ffn2_norm, ffn2_w_gate, ffn2_w_up, ffn2_w_down, loss_target, m_ffn1_norm, m_ffn1_w_gate, m_ffn1_w_up, m_ffn1_w_down, m_mix_norm, m_w_in, m_conv_w, m_conv_b, m_rg_w_a, m_rg_b_a, m_rg_w_x, m_rg_b_x, m_rg_lambda, m_q_norm, m_k_norm, m_rnn_out_norm, m_attn_out_norm, m_w_out, m_ffn2_norm, m_ffn2_w_gate, m_ffn2_w_up, m_ffn2_w_down, v_ffn1_norm, v_ffn1_w_gate, v_ffn1_w_up, v_ffn1_w_down, v_mix_norm, v_w_in, v_conv_w, v_conv_b, v_rg_w_a, v_rg_b_a, v_rg_w_x, v_rg_b_x, v_rg_lambda, v_q_norm, v_k_norm, v_rnn_out_norm, v_attn_out_norm, v_w_out, v_ffn2_norm, v_ffn2_w_gate, v_ffn2_w_up, v_ffn2_w_down):
    given = dict(locals())
    w = {n: given[n] for n in WEIGHTS}
    m = {n: given["m_" + n] for n in WEIGHTS}
    v = {n: given["v_" + n] for n in WEIGHTS}
    chip = 2 * lax.axis_index("x") + lax.axis_index("y")

    where = jnp.stack([lax.axis_index("c"), chip]).astype(jnp.int32)

    stacks = {n: _place_shard(_hidden_major(n, w[n][0]), where, BF16, "place_" + n) for n in BIG}
    conv_stack = _place_shard(w["conv_w"][0], where, F32, "place_conv_w")
    small = {n: (w[n][0] if w[n].ndim > 2 else w[n]) for n in SMALL if n != "conv_w"}

    loss, grad_x, slots, gs, everyone = _local_step(x[0], loss_target[0], stacks, conv_stack, small, where)
    loss = lax.psum(loss, ("x", "y", "c"))

    swapped = _half_swap([_chip_sum(slots[n], where, "chip_sum_" + n) for n in BIG])
    grads, deltas, new_m, new_v = {}, {}, {}, {}
    for n, t in zip(BIG, swapped):
        g2 = t.reshape(t.shape[0] * t.shape[1], t.shape[2])
        d2, m2, v2 = _adamw(_hidden_major(n, w[n][0]), g2, _hidden_major(n, m[n][0]), _hidden_major(n, v[n][0]),
                            "adamw_" + n)
        back = lambda a: _hidden_major(n, a).reshape(w[n].shape)
        grads[n], deltas[n], new_m[n], new_v[n] = back(g2), back(d2), back(m2), back(v2)

    full_shapes = [gs[n].shape for n in SMALL]
    g_small = _slot_sum(everyone, "small_grad_sum")
    g_parts = dict(zip(SMALL, _unpack(g_small, full_shapes)))
    quarter = D_RNN // N_CHIPS
    g_parts["conv_w"] = lax.dynamic_slice_in_dim(g_parts["conv_w"], chip * quarter, quarter, axis=1)
    local_shapes = [w[n].shape for n in SMALL]
    pk = lambda tree: _pack([tree[n] for n in SMALL])
    d_s, m_s, v_s = _adamw(pk(w), pk(g_parts), pk(m), pk(v), "adamw_small")
    for tree, packed in ((grads, pk(g_parts)), (deltas, d_s), (new_m, m_s), (new_v, v_s)):
        tree.update(zip(SMALL, _unpack(packed, local_shapes)))

    return (loss, grad_x.reshape(x.shape), *[grads[n] for n in WEIGHTS], *[deltas[n] for n in WEIGHTS],
            *[new_m[n] for n in WEIGHTS], *[new_v[n] for n in WEIGHTS])
```

```python
import functools
import math

import jax
import jax.numpy as jnp
from jax import lax
from jax.experimental import pallas as pl
from jax.experimental.pallas import tpu as pltpu

F32 = jnp.float32
BF16 = jnp.bfloat16
MESH = pl.DeviceIdType.MESH

D_MODEL = 1024
N_CHIPS = 4
D_RNN = 512
D_ATT = 512
N_HEADS = 8
HEAD_DIM = 64
RNN_BLOCKS = 8
CONV_W = 4
RG_C = 8.0
N_IN = 2 * D_RNN + 3 * D_ATT
EPS = 1e-6
ATT_BLOCK = 128
ATT_WINDOW = 384
ATT_SPLIT = 256
EXP_ZERO = -105.0

ADAM_LR = 0.001
ADAM_B1 = 0.9
ADAM_B2 = 0.999
ADAM_EPS = 1e-08
ADAM_WD = 0.01
ADAM_STEP = 10

V7X_VMEM_LIMIT = 56 * 1024 * 1024
TOKEN_TILE = 512
FFN_TILE = 256
WGRAD_TILE = 2048
WHOLE_TILE = 1024
ADAMW_STEPS = 8

GELU_K0 = math.sqrt(2.0 / math.pi)
GELU_K1 = 0.044715


def _params(sem=None):
    return pltpu.CompilerParams(dimension_semantics=sem, vmem_limit_bytes=V7X_VMEM_LIMIT)


def _dot(a, b):
    return jnp.dot(a, b, preferred_element_type=F32)


def _dot_nt(a, b):
    return lax.dot_general(a, b, (((1,), (1,)), ((), ())), preferred_element_type=F32)


def _dot_tn(a, b):
    return lax.dot_general(a, b, (((0,), (0,)), ((), ())), preferred_element_type=F32)


def _sigmoid(x):
    return 1.0 / (1.0 + jnp.exp(-x))


def _rms_r(xv):
    return lax.rsqrt(jnp.mean(xv * xv, axis=-1, keepdims=True) + EPS)


def _rms_bwd(xv, r, nw, dh):
    t = dh * nw
    dx = r * t - xv * (r * r * r * jnp.mean(t * xv, axis=-1, keepdims=True))
    dn = jnp.sum(dh * xv * r, axis=0, keepdims=True)
    return dx, dn


def _gelu(x):
    t = jnp.tanh(GELU_K0 * (x + GELU_K1 * x * x * x))
    return 0.5 * x * (1.0 + t)


def _gelu_grad(x):
    t = jnp.tanh(GELU_K0 * (x + GELU_K1 * x * x * x))
    return 0.5 * (1.0 + t) + 0.5 * x * (1.0 - t * t) * (GELU_K0 * (1.0 + 3.0 * GELU_K1 * x * x))


def _expm1_neg(x):
    p = 1.0 + x * (1.0 / 8.0)
    for k in (7.0, 6.0, 5.0, 4.0, 3.0, 2.0):
        p = 1.0 + x * (1.0 / k) * p
    return jnp.where(x > -0.25, x * p, jnp.exp(x) - 1.0)


def _log_sigmoid(x):
    return jnp.minimum(x, 0.0) - jnp.log(1.0 + jnp.exp(-jnp.abs(x)))


def _tile(s):
    return min(TOKEN_TILE, s)


def _ffn_fwd(x, nw, wg, wu, wd, tgt=None, rider=None):
    s, d = x.shape
    nb, fb, _ = wg.shape
    tm = min(FFN_TILE, s)
    ni = s // tm
    assert s % tm == 0
    with_loss = tgt is not None
    n_in, n_out = 5 + with_loss, 5 + with_loss

    def body(*refs):
        ins, outs, _, copies = _split_refs(refs, n_in, n_out, rider)
        x_ref, nw_ref, wg_ref, wu_ref, wd_ref = ins[:5]
        out_ref, g_ref, u_ref, hb_ref, ab_ref = outs[:5]
        i = pl.program_id(0)
        finish = _ride(copies, i == 0, i == ni - 1)

        xv = x_ref[...]
        hb = (xv * _rms_r(xv) * nw_ref[...]).astype(BF16)
        hb_ref[...] = hb
        y = jnp.zeros((tm, d), F32)
        for jb in range(nb):
            g = _dot_nt(hb, wg_ref[jb])
            u = _dot_nt(hb, wu_ref[jb])
            g_ref[jb] = g.astype(BF16)
            u_ref[jb] = u.astype(BF16)
            ab = (g * _sigmoid(g) * u).astype(BF16)
            ab_ref[jb] = ab
            y = y + _dot(ab, wd_ref[jb])
        y = xv + 0.5 * y
        if with_loss:
            tgt_ref, loss_ref = ins[5], outs[5]
            diff = y - tgt_ref[...]
            out_ref[...] = diff * (1.0 / d)

            @pl.when(i == 0)
            def _():
                loss_ref[...] = jnp.zeros_like(loss_ref)

            loss_ref[...] += jnp.sum(diff * diff) * (0.5 / d)
        else:
            out_ref[...] = y
        finish()

    row = pl.BlockSpec((tm, d), lambda i: (i, 0))
    weight = pl.BlockSpec((nb, fb, d), lambda i: (0, 0, 0), pipeline_mode=pl.Buffered(1))
    in_specs = [row, pl.BlockSpec((1, d), lambda i: (0, 0)), weight, weight, weight]
    args = [x, nw, wg, wu, wd]
    if with_loss:
        in_specs.append(row)
        args.append(tgt)
    blk = pl.BlockSpec((nb, tm, fb), lambda i: (0, i, 0))
    out_shape = [jax.ShapeDtypeStruct((s, d), F32), jax.ShapeDtypeStruct((nb, s, fb), BF16),
                 jax.ShapeDtypeStruct((nb, s, fb), BF16), jax.ShapeDtypeStruct((s, d), BF16),
                 jax.ShapeDtypeStruct((nb, s, fb), BF16)]
    out_specs = [row, blk, blk, row, blk]
    if with_loss:
        out_shape.append(jax.ShapeDtypeStruct((1, 128), F32))
        out_specs.append(pl.BlockSpec((1, 128), lambda i: (0, 0)))
    return _call(body, "ffn_fwd_loss" if with_loss else "ffn_fwd", (ni,), in_specs, out_specs, out_shape, args,
                 rider=rider)


def _call(body, name, grid, in_specs, out_specs, out_shape, args, scratch=(), rider=None):
    in_specs, out_specs, out_shape, scratch = list(in_specs), list(out_specs), list(out_shape), list(scratch)
    extra, aliases = [], {}
    if rider is not None:
        extra = rider.operands()
        aliases = rider.aliases(len(args), len(out_shape))
        in_specs += [ANY] * len(extra)
        out_specs += [ANY] * len(rider.inplace)
        out_shape += rider.out_shape()
        scratch += rider.scratch()
    return pl.pallas_call(
        body, name=name, grid=grid, in_specs=in_specs, out_specs=out_specs, out_shape=out_shape,
        input_output_aliases=aliases, scratch_shapes=scratch,
        compiler_params=_params(("arbitrary",) * len(grid)),
    )(*args, *extra)


def _ffn_bwd_act(x, nw, dy, g, u, wg, wu, wd, name, rider=None):
    s, d = x.shape
    nb, fb, _ = wg.shape
    tm = min(FFN_TILE, s)
    assert s % tm == 0

    def body(*refs):
        ins, outs, _, copies = _split_refs(refs, 8, 5, rider)
        x_ref, nw_ref, dy_ref, g_ref, u_ref, wg_ref, wu_ref, wd_ref = ins
        dx_ref, dg_ref, du_ref, dyb_ref, dnw_ref = outs
        finish = _ride(copies, pl.program_id(0) == 0, pl.program_id(0) == s // tm - 1)
        dyv = dy_ref[...]
        dyb = dyv.astype(BF16)
        dyb_ref[...] = dyb
        dh = jnp.zeros((tm, d), F32)
        for jb in range(nb):
            da = 0.5 * _dot_nt(dyb, wd_ref[jb])
            gv = g_ref[jb].astype(F32)
            sg = _sigmoid(gv)
            dub = (da * (gv * sg)).astype(BF16)
            dgb = (da * u_ref[jb].astype(F32) * (sg * (1.0 + gv * (1.0 - sg)))).astype(BF16)
            dg_ref[jb] = dgb
            du_ref[jb] = dub
            dh = dh + _dot(dgb, wg_ref[jb]) + _dot(dub, wu_ref[jb])
        xv = x_ref[...]
        dx, dn = _rms_bwd(xv, _rms_r(xv), nw_ref[...], dh)
        dx_ref[...] = dyv + dx

        @pl.when(pl.program_id(0) == 0)
        def _():
            dnw_ref[...] = jnp.zeros_like(dnw_ref)

        dnw_ref[...] += dn
        finish()

    row = pl.BlockSpec((tm, d), lambda i: (i, 0))
    vec = pl.BlockSpec((1, d), lambda i: (0, 0))
    blk = pl.BlockSpec((nb, tm, fb), lambda i: (0, i, 0))
    weight = pl.BlockSpec((nb, fb, d), lambda i: (0, 0, 0), pipeline_mode=pl.Buffered(1))
    return _call(
        body, name, (s // tm,), [row, vec, row, blk, blk, weight, weight, weight], [row, blk, blk, row, vec],
        [jax.ShapeDtypeStruct((s, d), F32), jax.ShapeDtypeStruct((nb, s, fb), BF16),
         jax.ShapeDtypeStruct((nb, s, fb), BF16), jax.ShapeDtypeStruct((s, d), BF16),
         jax.ShapeDtypeStruct((1, d), F32)],
        [x, nw, dy, g, u, wg, wu, wd], rider=rider)


def _wgrad(a, b, a_spec, b_spec, out_rows, out_cols, scale, name, tk, rider=None):
    s = a.shape[-2]
    nk = s // tk
    assert s % tk == 0

    def body(*refs):
        (a_ref, b_ref), (out_ref,), (acc,), copies = _split_refs(refs, 2, 1, rider)
        j, k = pl.program_id(0), pl.program_id(1)
        finish = _ride(copies, jnp.logical_and(j == 0, k == 0), jnp.logical_and(j == N_CHIPS - 1, k == nk - 1))

        @pl.when(k == 0)
        def _():
            acc[...] = jnp.zeros_like(acc)

        acc[...] += _dot_tn(a_ref[...], b_ref[...])

        @pl.when(k == nk - 1)
        def _():
            out_ref[...] = (acc[...] * scale).astype(BF16)

        finish()

    outs = _call(
        body, name, (N_CHIPS, nk), [a_spec(tk), b_spec(tk)],
        [pl.BlockSpec((None, out_rows, out_cols), lambda j, k: (j, 0, 0))],
        [jax.ShapeDtypeStruct((N_CHIPS, out_rows, out_cols), BF16)], [a, b],
        scratch=[pltpu.VMEM((out_rows, out_cols), F32)], rider=rider)
    return outs[0] if rider is None else outs


def _wgrad_whole(a, b, col_blocks, name, rider=None):
    s, m = a.shape
    n = b.shape[1]
    tk = min(WHOLE_TILE, s)
    nk = s // tk
    assert s % tk == 0
    out_shape = (N_CHIPS, m, n // N_CHIPS) if col_blocks else (N_CHIPS, m // N_CHIPS, n)

    def body(*refs):
        (a_ref, b_ref), (out_ref,), (acc,), copies = _split_refs(refs, 2, 1, rider)
        k = pl.program_id(0)
        finish = _ride(copies, k == 0, k == nk - 1)

        @pl.when(k == 0)
        def _():
            acc[...] = jnp.zeros_like(acc)

        acc[...] += _dot_tn(a_ref[...], b_ref[...])

        @pl.when(k == nk - 1)
        def _():
            for j in range(N_CHIPS):
                if col_blocks:
                    out_ref[j] = acc[:, j * out_shape[2]:(j + 1) * out_shape[2]].astype(BF16)
                else:
                    out_ref[j] = acc[j * out_shape[1]:(j + 1) * out_shape[1], :].astype(BF16)

        finish()

    outs = _call(
        body, name, (nk,), [pl.BlockSpec((tk, m), lambda k: (k, 0)), pl.BlockSpec((tk, n), lambda k: (k, 0))],
        [pl.BlockSpec(out_shape, lambda k: (0, 0, 0))], [jax.ShapeDtypeStruct(out_shape, BF16)], [a, b],
        scratch=[pltpu.VMEM((m, n), F32)], rider=rider)
    return outs[0] if rider is None else outs


def _ffn_wgrad(stack, shared, scale, name, rider=None):
    s, d = shared.shape
    fb = stack.shape[-1]
    return _wgrad(stack, shared, lambda tk: pl.BlockSpec((None, tk, fb), lambda j, k: (j, k, 0)),
                  lambda tk: pl.BlockSpec((tk, d), lambda j, k: (k, 0)), fb, d, scale, name,
                  min(WGRAD_TILE, s), rider)


def _mix_pre(x, nw, win):
    s, d = x.shape
    nb, _, cb = win.shape
    tm = min(FFN_TILE, s)
    assert s % tm == 0

    def body(x_ref, nw_ref, w_ref, p_ref, hb_ref):
        xv = x_ref[...]
        hb = (xv * _rms_r(xv) * nw_ref[...]).astype(BF16)
        hb_ref[...] = hb
        for j in range(nb):
            p_ref[:, j * cb:(j + 1) * cb] = _dot(hb, w_ref[j])

    row = pl.BlockSpec((tm, d), lambda i: (i, 0))
    return pl.pallas_call(
        body, name="mix_pre", grid=(s // tm,),
        in_specs=[row, pl.BlockSpec((1, d), lambda i: (0, 0)),
                  pl.BlockSpec((nb, d, cb), lambda i: (0, 0, 0), pipeline_mode=pl.Buffered(1))],
        out_specs=[pl.BlockSpec((tm, nb * cb), lambda i: (i, 0)), row],
        out_shape=[jax.ShapeDtypeStruct((s, nb * cb), F32), jax.ShapeDtypeStruct((s, d), BF16)],
        compiler_params=_params(("arbitrary",)),
    )(x, nw, win)


def _mix_pre_bwd(x, nw, dres, dpb, win):
    s, d = x.shape
    nb, _, cb = win.shape
    tm = min(FFN_TILE, s)
    assert s % tm == 0

    def body(x_ref, nw_ref, dres_ref, dp_ref, w_ref, dx_ref, dnw_ref):
        dh = jnp.zeros((tm, d), F32)
        for j in range(nb):
            dh = dh + _dot_nt(dp_ref[:, j * cb:(j + 1) * cb], w_ref[j])
        xv = x_ref[...]
        dx, dn = _rms_bwd(xv, _rms_r(xv), nw_ref[...], dh)
        dx_ref[...] = dres_ref[...] + dx

        @pl.when(pl.program_id(0) == 0)
        def _():
            dnw_ref[...] = jnp.zeros_like(dnw_ref)

        dnw_ref[...] += dn

    row = pl.BlockSpec((tm, d), lambda i: (i, 0))
    vec = pl.BlockSpec((1, d), lambda i: (0, 0))
    return pl.pallas_call(
        body, name="mix_pre_bwd", grid=(s // tm,),
        in_specs=[row, vec, row, pl.BlockSpec((tm, nb * cb), lambda i: (i, 0)),
                  pl.BlockSpec((nb, d, cb), lambda i: (0, 0, 0), pipeline_mode=pl.Buffered(1))],
        out_specs=[row, vec],
        out_shape=[jax.ShapeDtypeStruct((s, d), F32), jax.ShapeDtypeStruct((1, d), F32)],
        compiler_params=_params(("arbitrary",)),
    )(x, nw, dres, dpb, win)


def _mix_post(x, yr, ya, nr, na, wout):
    s, d = x.shape
    h = yr.shape[1]
    tm = _tile(s)

    def body(x_ref, yr_ref, ya_ref, nr_ref, na_ref, w_ref, out_ref):
        yrv = yr_ref[...]
        yav = ya_ref[...]
        onb = (yrv * _rms_r(yrv) * nr_ref[...]).astype(BF16)
        oab = (yav * _rms_r(yav) * na_ref[...]).astype(BF16)
        out_ref[...] = x_ref[...] + _dot(onb, w_ref[0:h, :]) + _dot(oab, w_ref[h:2 * h, :])

    row = pl.BlockSpec((tm, d), lambda i: (i, 0))
    half = pl.BlockSpec((tm, h), lambda i: (i, 0))
    vec = pl.BlockSpec((1, h), lambda i: (0, 0))
    return pl.pallas_call(
        body, name="mix_post", grid=(s // tm,),
        in_specs=[row, half, half, vec, vec, pl.BlockSpec((2 * h, d), lambda i: (0, 0))],
        out_specs=row, out_shape=jax.ShapeDtypeStruct((s, d), F32),
        compiler_params=_params(("arbitrary",)),
    )(x, yr, ya, nr, na, wout)


def _mix_post_bwd(dx, yr, ya, nr, na, wout):
    s, d = dx.shape
    h = yr.shape[1]
    tm = _tile(s)

    def body(dx_ref, yr_ref, ya_ref, nr_ref, na_ref, w_ref,
             dyr_ref, dya_ref, yc_ref, dxb_ref, dnr_ref, dna_ref):
        i = pl.program_id(0)
        dxb = dx_ref[...].astype(BF16)
        dxb_ref[...] = dxb
        dyc = _dot_nt(dxb, w_ref[...])
        yrv = yr_ref[...]
        yav = ya_ref[...]
        rr = _rms_r(yrv)
        ra = _rms_r(yav)
        yc_ref[:, 0:h] = (yrv * rr * nr_ref[...]).astype(BF16)
        yc_ref[:, h:2 * h] = (yav * ra * na_ref[...]).astype(BF16)
        dyr, dnr = _rms_bwd(yrv, rr, nr_ref[...], dyc[:, 0:h])
        dya, dna = _rms_bwd(yav, ra, na_ref[...], dyc[:, h:2 * h])
        dyr_ref[...] = dyr
        dya_ref[...] = dya

        @pl.when(i == 0)
        def _():
            dnr_ref[...] = jnp.zeros_like(dnr_ref)
            dna_ref[...] = jnp.zeros_like(dna_ref)

        dnr_ref[...] += dnr
        dna_ref[...] += dna

    row = pl.BlockSpec((tm, d), lambda i: (i, 0))
    half = pl.BlockSpec((tm, h), lambda i: (i, 0))
    vec = pl.BlockSpec((1, h), lambda i: (0, 0))
    return pl.pallas_call(
        body, name="mix_post_bwd", grid=(s // tm,),
        in_specs=[row, half, half, vec, vec, pl.BlockSpec((2 * h, d), lambda i: (0, 0))],
        out_specs=[half, half, pl.BlockSpec((tm, 2 * h), lambda i: (i, 0)), row, vec, vec],
        out_shape=[jax.ShapeDtypeStruct((s, h), F32), jax.ShapeDtypeStruct((s, h), F32),
                   jax.ShapeDtypeStruct((s, 2 * h), BF16), jax.ShapeDtypeStruct((s, d), BF16),
                   jax.ShapeDtypeStruct((1, h), F32), jax.ShapeDtypeStruct((1, h), F32)],
        compiler_params=_params(("arbitrary",)),
    )(dx, yr, ya, nr, na, wout)


def _shift_down(xv, s, prev8):
    rolled = pltpu.roll(xv, s, 0)
    row8 = lax.broadcasted_iota(jnp.int32, prev8.shape, 0)
    head = jnp.where(row8 < s, pltpu.roll(prev8, s, 0), rolled[0:8, :])
    return jnp.concatenate([head, rolled[8:, :]], axis=0)


def _shift_up(xv, s, next8):
    n = xv.shape[0]
    rolled = pltpu.roll(xv, n - s, 0)
    row8 = lax.broadcasted_iota(jnp.int32, next8.shape, 0)
    tail = jnp.where(row8 >= 8 - s, pltpu.roll(next8, 8 - s, 0), rolled[n - 8:, :])
    return jnp.concatenate([rolled[:n - 8, :], tail], axis=0)


def _scan_fwd(a, b):
    n = a.shape[0]
    row = lax.broadcasted_iota(jnp.int32, a.shape, 0)
    s = 1
    while s < n:
        ok = row >= s
        b = jnp.where(ok, a * pltpu.roll(b, s, 0) + b, b)
        a = jnp.where(ok, a * pltpu.roll(a, s, 0), a)
        s *= 2
    return b


def _scan_bwd(a, b):
    n = a.shape[0]
    row = lax.broadcasted_iota(jnp.int32, a.shape, 0)
    s = 1
    while s < n:
        ok = row < n - s
        b = jnp.where(ok, a * pltpu.roll(b, n - s, 0) + b, b)
        a = jnp.where(ok, a * pltpu.roll(a, n - s, 0), a)
        s *= 2
    return b


def _rglru_gates(xv, prev8, cw_ref, cb_ref, wa_ref, ba_ref, wx_ref, bx_ref, lam_ref):
    x1 = _shift_down(xv, 1, prev8)
    x2 = _shift_down(xv, 2, prev8)
    x3 = _shift_down(xv, 3, prev8)
    xc = cw_ref[3:4, :] * xv + cw_ref[2:3, :] * x1 + cw_ref[1:2, :] * x2 + cw_ref[0:1, :] * x3 + cb_ref[...]
    xcb = xc.astype(BF16)
    r = _sigmoid(_dot(xcb, wa_ref[...]) + ba_ref[...])
    ig = _sigmoid(_dot(xcb, wx_ref[...]) + bx_ref[...])
    c = RG_C * _log_sigmoid(lam_ref[...])
    la = r * c
    a = jnp.exp(la)
    m = jnp.sqrt(-_expm1_neg(2.0 * la))
    return (x1, x2, x3), xc, xcb, r, ig, c, a, m


def _rglru_fwd(proj, cw, cb, wa, ba, wx, bx, lam):
    s = proj.shape[0]
    w = D_RNN
    tm = _tile(s)

    def body(xr_ref, gate_ref, cw_ref, cb_ref, wa_ref, ba_ref, wx_ref, bx_ref, lam_ref,
             y_ref, h_ref, prev, hlast):
        @pl.when(pl.program_id(0) == 0)
        def _():
            prev[...] = jnp.zeros_like(prev)
            hlast[...] = jnp.zeros_like(hlast)

        xv = xr_ref[...]
        _, xc, _, _, ig, _, a, m = _rglru_gates(xv, prev[...], cw_ref, cb_ref, wa_ref, ba_ref,
                                                wx_ref, bx_ref, lam_ref)
        b = m * (ig * xc)
        row = lax.broadcasted_iota(jnp.int32, b.shape, 0)
        b = jnp.where(row == 0, b + a * hlast[...], b)
        h = _scan_fwd(a, b)
        h_ref[...] = h
        y_ref[...] = h * _gelu(gate_ref[...])
        prev[...] = xv[tm - 8:, :]
        hlast[...] = h[tm - 1:tm, :]

    vec = pl.BlockSpec((1, w), lambda i: (0, 0))
    sq = pl.BlockSpec((w, w), lambda i: (0, 0))
    out = pl.BlockSpec((tm, w), lambda i: (i, 0))
    return pl.pallas_call(
        body, name="rglru_fwd", grid=(s // tm,),
        in_specs=[pl.BlockSpec((tm, w), lambda i: (i, 0)), pl.BlockSpec((tm, w), lambda i: (i, 1)),
                  pl.BlockSpec((CONV_W, w), lambda i: (0, 0)), vec, sq, vec, sq, vec, vec],
        out_specs=[out, out],
        out_shape=[jax.ShapeDtypeStruct((s, w), F32), jax.ShapeDtypeStruct((s, w), F32)],
        scratch_shapes=[pltpu.VMEM((8, w), F32), pltpu.VMEM((1, w), F32)],
        compiler_params=_params(("arbitrary",)),
    )(proj, proj, cw, cb, wa, ba, wx, bx, lam)


def _rglru_bwd(proj, hseq, dyr, cw, cb, wa, ba, wx, bx, lam):
    s = proj.shape[0]
    w = D_RNN
    tm = _tile(s)
    nt = s // tm
    t8 = tm // 8

    def body(xr_ref, xp_ref, gate_ref, h_ref, hp_ref, dy_ref, cw_ref, cb_ref, wa_ref, ba_ref,
             wx_ref, bx_ref, lam_ref,
             dxr_ref, dgate_ref, dcw_ref, dcb_ref, dwa_ref, dba_ref, dwx_ref, dbx_ref, dlam_ref,
             carry, dxc_next):
        i = pl.program_id(0)
        first_tile = i == nt - 1

        @pl.when(i == 0)
        def _():
            carry[...] = jnp.zeros_like(carry)
            dxc_next[...] = jnp.zeros_like(dxc_next)
            for ref in (dcw_ref, dcb_ref, dwa_ref, dba_ref, dwx_ref, dbx_ref, dlam_ref):
                ref[...] = jnp.zeros_like(ref)

        xv = xr_ref[...]
        prev8 = jnp.where(first_tile, 0.0, xp_ref[...])
        hprev8 = jnp.where(first_tile, 0.0, hp_ref[...])
        (x1, x2, x3), xc, xcb, r, ig, c, a, m = _rglru_gates(
            xv, prev8, cw_ref, cb_ref, wa_ref, ba_ref, wx_ref, bx_ref, lam_ref)
        gv = gate_ref[...]
        hv = h_ref[...]
        dy = dy_ref[...]
        dgate_ref[...] = (dy * hv * _gelu_grad(gv)).astype(BF16)
        dh = dy * _gelu(gv)
        row = lax.broadcasted_iota(jnp.int32, dh.shape, 0)
        dh = jnp.where(row == tm - 1, dh + carry[...], dh)
        a_up = jnp.where(row == tm - 1, 0.0, pltpu.roll(a, tm - 1, 0))
        lam_t = _scan_bwd(a_up, dh)
        carry[...] = a[0:1, :] * lam_t[0:1, :]
        hm1 = _shift_down(hv, 1, hprev8)
        da = lam_t * hm1
        ixc = ig * xc
        dm = lam_t * ixc
        dig = lam_t * m * xc
        dxc = lam_t * m * ig
        dla = da * a - dm * (a * a) / m
        dr = dla * c
        dlam_ref[...] += jnp.sum(dla * r, axis=0, keepdims=True)
        dpa = dr * r * (1.0 - r)
        dpi = dig * ig * (1.0 - ig)
        dba_ref[...] += jnp.sum(dpa, axis=0, keepdims=True)
        dbx_ref[...] += jnp.sum(dpi, axis=0, keepdims=True)
        dpab = dpa.astype(BF16)
        dpib = dpi.astype(BF16)
        dwa_ref[...] += _dot_tn(xcb, dpab)
        dwx_ref[...] += _dot_tn(xcb, dpib)
        dxc = dxc + _dot_nt(dpab, wa_ref[...]) + _dot_nt(dpib, wx_ref[...])
        dcb_ref[...] += jnp.sum(dxc, axis=0, keepdims=True)
        dcw_ref[3:4, :] += jnp.sum(dxc * xv, axis=0, keepdims=True)
        dcw_ref[2:3, :] += jnp.sum(dxc * x1, axis=0, keepdims=True)
        dcw_ref[1:2, :] += jnp.sum(dxc * x2, axis=0, keepdims=True)
        dcw_ref[0:1, :] += jnp.sum(dxc * x3, axis=0, keepdims=True)
        nxt = dxc_next[...]
        dxr = (cw_ref[3:4, :] * dxc + cw_ref[2:3, :] * _shift_up(dxc, 1, nxt)
               + cw_ref[1:2, :] * _shift_up(dxc, 2, nxt) + cw_ref[0:1, :] * _shift_up(dxc, 3, nxt))
        dxr_ref[...] = dxr.astype(BF16)
        dxc_next[...] = dxc[0:8, :]

        @pl.when(first_tile)
        def _():
            lv = lam_ref[...]
            dlam_ref[...] = dlam_ref[...] * (RG_C * _sigmoid(-lv))

    rev = lambda i: nt - 1 - i
    vec = pl.BlockSpec((1, w), lambda i: (0, 0))
    sq = pl.BlockSpec((w, w), lambda i: (0, 0))
    cur = lambda col: pl.BlockSpec((tm, w), lambda i: (rev(i), col))
    before = lambda cols: pl.BlockSpec((8, w), lambda i: (jnp.maximum(rev(i) * t8 - 1, 0), 0))
    return pl.pallas_call(
        body, name="rglru_bwd", grid=(nt,),
        in_specs=[cur(0), before(None), cur(1), cur(0), before(None), cur(0),
                  pl.BlockSpec((CONV_W, w), lambda i: (0, 0)), vec, sq, vec, sq, vec, vec],
        out_specs=[cur(0), cur(0), pl.BlockSpec((CONV_W, w), lambda i: (0, 0)), vec, sq, vec, sq, vec, vec],
        out_shape=[jax.ShapeDtypeStruct((s, w), BF16), jax.ShapeDtypeStruct((s, w), BF16),
                   jax.ShapeDtypeStruct((CONV_W, w), F32), jax.ShapeDtypeStruct((1, w), F32),
                   jax.ShapeDtypeStruct((w, w), F32), jax.ShapeDtypeStruct((1, w), F32),
                   jax.ShapeDtypeStruct((w, w), F32), jax.ShapeDtypeStruct((1, w), F32),
                   jax.ShapeDtypeStruct((1, w), F32)],
        scratch_shapes=[pltpu.VMEM((1, w), F32), pltpu.VMEM((8, w), F32)],
        compiler_params=_params(("arbitrary",)),
    )(proj, proj, proj, hseq, hseq, dyr, cw, cb, wa, ba, wx, bx, lam)


def _sb_logs(z, valid):
    l1p = jnp.log(1.0 + jnp.exp(-jnp.abs(z)))
    lb = jnp.minimum(z, 0.0) - l1p
    lm = jnp.where(valid, -jnp.maximum(z, 0.0) - l1p, 0.0)
    return lb, lm


class _Window:
    def __init__(self):
        blk, win, cut = ATT_BLOCK, ATT_WINDOW, ATT_SPLIT
        self.row = lax.broadcasted_iota(jnp.int32, (blk, win), 0)
        self.col = lax.broadcasted_iota(jnp.int32, (blk, win), 1)

        def tri(n, later):
            j = lax.broadcasted_iota(jnp.int32, (n, n), 0)
            s = lax.broadcasted_iota(jnp.int32, (n, n), 1)
            return jnp.where((j > s) if later else (j < s), 1.0, 0.0).astype(BF16)

        self.later = (tri(cut, True), tri(win - cut, True))
        self.earlier = (tri(cut, False), tri(win - cut, False))

    def place(self, qi, g):
        end = (qi + 1) * ATT_BLOCK - g * ATT_WINDOW
        start = pl.multiple_of(jnp.maximum(end - ATT_WINDOW, 0), ATT_BLOCK)
        valid = start + self.col < jnp.minimum(qi * ATT_BLOCK + self.row, end)
        return start, valid

    @staticmethod
    def _parts(xv):
        hi = xv.astype(BF16)
        lo = (xv - hi.astype(F32)).astype(BF16)
        cut = ATT_SPLIT
        sums = (jnp.sum(xv[:, :cut], axis=1, keepdims=True), jnp.sum(xv[:, cut:], axis=1, keepdims=True))
        return (hi[:, :cut], lo[:, :cut]), (hi[:, cut:], lo[:, cut:]), sums

    def sums_after(self, xv, carry):
        (h0, l0), (h1, l1), (s0, s1) = self._parts(xv)
        first = _dot(h0, self.later[0]) + _dot(l0, self.later[0]) + (s1 + carry)
        last = _dot(h1, self.later[1]) + _dot(l1, self.later[1]) + carry
        return jnp.concatenate([first, last], axis=1), s0 + s1

    def sums_before(self, xv, carry):
        (h0, l0), (h1, l1), (s0, s1) = self._parts(xv)
        first = _dot(h0, self.earlier[0]) + _dot(l0, self.earlier[0]) + carry
        last = _dot(h1, self.earlier[1]) + _dot(l1, self.earlier[1]) + (s0 + carry)
        return jnp.concatenate([first, last], axis=1), s0 + s1


class _HeadPair:
    def __init__(self):
        lanes = 2 * HEAD_DIM
        lane = lax.broadcasted_iota(jnp.int32, (1, lanes), 1)
        self.masks = [lane // HEAD_DIM == h for h in (0, 1)]
        i = lax.broadcasted_iota(jnp.int32, (lanes, lanes), 0) // HEAD_DIM
        j = lax.broadcasted_iota(jnp.int32, (lanes, lanes), 1) // HEAD_DIM
        self.same_head = jnp.where(i == j, 1.0, 0.0).astype(BF16)

    def only(self, h, xv):
        return jnp.where(self.masks[h], xv, jnp.zeros_like(xv))

    def merge(self, per_head):
        return jnp.where(self.masks[0], per_head[0], per_head[1])

    def mean(self, xv):
        hi = xv.astype(BF16)
        lo = (xv - hi.astype(F32)).astype(BF16)
        return (_dot(hi, self.same_head) + _dot(lo, self.same_head)) * (1.0 / HEAD_DIM)

    def rms_r(self, xv):
        return lax.rsqrt(self.mean(xv * xv) + EPS)

    def rms_bwd(self, xv, r, nw, dh):
        t = dh * nw
        dx = r * t - xv * (r * r * r * self.mean(t * xv))
        dn = jnp.sum(dh * xv * r, axis=0, keepdims=True)
        return dx, dn[:, :HEAD_DIM] + dn[:, HEAD_DIM:]


def _attn_fwd(proj, qg, kg, rider=None):
    s = proj.shape[0]
    blk, win, dh = ATT_BLOCK, ATT_WINDOW, HEAD_DIM
    nq = s // blk
    scale = 1.0 / math.sqrt(dh)
    heads = (0, 1)
    assert s >= win and s % blk == 0

    def body(*refs):
        (q_ref, k_ref, v_ref, qg_ref, kg_ref), (o_ref,), (qn, kn, vb), copies = _split_refs(refs, 5, 1, rider)
        finish = _ride(copies, pl.program_id(0) == 0, pl.program_id(0) == N_HEADS // 2 - 1)
        wd, hp = _Window(), _HeadPair()
        qv = q_ref[...]
        qn[...] = (qv * hp.rms_r(qv) * qg_ref[...] * scale).astype(BF16)
        kv = k_ref[...]
        kn[...] = (kv * hp.rms_r(kv) * kg_ref[...]).astype(BF16)
        vb[...] = v_ref[...].astype(BF16)

        def q_step(qi, _):
            qoff = pl.multiple_of(qi * blk, blk)
            qt = qn[pl.ds(qoff, blk), :]
            qts = [hp.only(h, qt) for h in heads]

            def more(carry):
                g, live = carry[:2]
                return jnp.logical_and((qi + 1) * blk - g * win > 0, live > 0)

            def window(carry):
                g, _, accs, runs = carry
                start, valid = wd.place(qi, g)
                kt = kn[pl.ds(start, win), :]
                zs = [_dot_nt(qts[h], kt) for h in heads]
                logs = [_sb_logs(z, valid) for z in zs]
                sums = [wd.sums_after(logs[h][1], runs[h]) for h in heads]
                wgts = [jnp.where(valid, jnp.exp(logs[h][0] + sums[h][0]), 0.0).astype(BF16) for h in heads]
                vt = vb[pl.ds(start, win), :]
                accs = tuple(accs[h] + _dot(wgts[h], vt) for h in heads)
                runs = tuple(runs[h] + sums[h][1] for h in heads)
                live = (jnp.maximum(jnp.max(runs[0]), jnp.max(runs[1])) > EXP_ZERO).astype(jnp.int32)
                return g + 1, live, accs, runs

            zero = lambda cols: tuple(jnp.zeros((blk, cols), F32) for _ in heads)
            _, _, accs, _ = lax.while_loop(more, window, (jnp.int32(0), jnp.int32(1), zero(2 * dh), zero(1)))
            o_ref[pl.ds(qoff, blk), :] = hp.merge(accs)
            return 0

        lax.fori_loop(0, nq, q_step, 0)
        finish()

    pair = lambda group: pl.BlockSpec((s, 2 * dh), lambda p: (0, group * (D_ATT // (2 * dh)) + p))
    vec = pl.BlockSpec((1, 2 * dh), lambda p: (0, 0))
    return _call(
        body, "attn_fwd", (N_HEADS // 2,), [pair(2), pair(3), pair(4), vec, vec], [pair(0)],
        [jax.ShapeDtypeStruct((s, D_ATT), F32)], [proj, proj, proj, jnp.tile(qg, (1, 2)), jnp.tile(kg, (1, 2))],
        scratch=[pltpu.VMEM((s, 2 * dh), BF16)] * 3, rider=rider)


def _attn_bwd(proj, dya, qg, kg, rider=None):
    s = proj.shape[0]
    blk, win, dh = ATT_BLOCK, ATT_WINDOW, HEAD_DIM
    nq = s // blk
    max_windows = -(-s // win) + 1
    scale = 1.0 / math.sqrt(dh)
    steps = N_HEADS // 2
    heads = (0, 1)
    assert s >= win and s % blk == 0

    def body(*refs):
        ins, outs, scratch, copies = _split_refs(refs, 6, 5, rider)
        q_ref, k_ref, v_ref, do_ref, qg_ref, kg_ref = ins
        dq_ref, dk_ref, dv_ref, dqg_ref, dkg_ref = outs
        qn, kn, vb, dob, runs_ref, dqn, dkn, dvn = scratch
        finish = _ride(copies, pl.program_id(0) == 0, pl.program_id(0) == steps - 1)
        wd, hp = _Window(), _HeadPair()

        @pl.when(pl.program_id(0) == 0)
        def _():
            dqg_ref[...] = jnp.zeros_like(dqg_ref)
            dkg_ref[...] = jnp.zeros_like(dkg_ref)

        qv = q_ref[...]
        qn[...] = (qv * hp.rms_r(qv) * qg_ref[...] * scale).astype(BF16)
        kv = k_ref[...]
        kn[...] = (kv * hp.rms_r(kv) * kg_ref[...]).astype(BF16)
        vb[...] = v_ref[...].astype(BF16)
        dob[...] = do_ref[...].astype(BF16)
        dkn[...] = jnp.zeros_like(dkn)
        dvn[...] = jnp.zeros_like(dvn)

        def q_step(qi, _):
            qoff = pl.multiple_of(qi * blk, blk)
            qt = qn[pl.ds(qoff, blk), :]
            dot = dob[pl.ds(qoff, blk), :]
            qts = [hp.only(h, qt) for h in heads]
            dots = [hp.only(h, dot) for h in heads]

            zero = lambda cols: tuple(jnp.zeros((blk, cols), F32) for _ in heads)

            def logs_of(g):
                start, valid = wd.place(qi, g)
                kt = kn[pl.ds(start, win), :]
                return [_sb_logs(_dot_nt(qts[h], kt), valid) for h in heads]

            def row_sums(logs):
                return tuple(jnp.sum(logs[h][1], axis=1, keepdims=True) for h in heads)

            def still_live(runs):
                return jnp.maximum(jnp.max(runs[0]), jnp.max(runs[1])) > EXP_ZERO

            def window_grads(g, logs, runs, esums):
                start, valid = wd.place(qi, g)
                kt = kn[pl.ds(start, win), :]
                vt = vb[pl.ds(start, win), :]
                dws = [_dot_nt(dots[h], vt) for h in heads]
                tails = [wd.sums_after(logs[h][1], runs[h])[0] for h in heads]
                wgts = [jnp.where(valid, jnp.exp(logs[h][0] + tails[h]), 0.0) for h in heads]
                es = [dws[h] * wgts[h] for h in heads]
                befores = [wd.sums_before(es[h], esums[h]) for h in heads]
                dzbs = []
                for h in heads:
                    beta = jnp.exp(logs[h][0])
                    dz = jnp.where(valid, es[h] * (1.0 - beta) - befores[h][0] * beta, 0.0)
                    dzbs.append(dz.astype(BF16))
                dkn[pl.ds(start, win), :] += _dot_tn(dzbs[0], qts[0]) + _dot_tn(dzbs[1], qts[1])
                dvn[pl.ds(start, win), :] += (_dot_tn(wgts[0].astype(BF16), dots[0])
                                              + _dot_tn(wgts[1].astype(BF16), dots[1]))
                return tuple(_dot(dzbs[h], kt) for h in heads), tuple(befores[h][1] for h in heads)

            logs0 = logs_of(0)
            runs1 = row_sums(logs0)

            def one_window():
                return window_grads(0, logs0, zero(1), zero(1))[0]

            def all_windows():
                def more(carry):
                    g, live = carry[:2]
                    return jnp.logical_and((qi + 1) * blk - g * win > 0, live > 0)

                def run_window(carry):
                    g, _, runs = carry
                    for h in heads:
                        runs_ref[h, g] = runs[h]
                    sums = row_sums(logs_of(g))
                    runs = tuple(runs[h] + sums[h] for h in heads)
                    return g + 1, still_live(runs).astype(jnp.int32), runs

                for h in heads:
                    runs_ref[h, 0] = jnp.zeros((blk, 1), F32)
                windows, _, _ = lax.while_loop(more, run_window, (jnp.int32(1), jnp.int32(1), runs1))

                def k_window(gg, carry):
                    dq_accs, esums = carry
                    g = windows - 1 - gg
                    parts, totals = window_grads(g, logs_of(g), [runs_ref[h, g] for h in heads], esums)
                    return (tuple(dq_accs[h] + parts[h] for h in heads),
                            tuple(esums[h] + totals[h] for h in heads))

                return lax.fori_loop(0, windows, k_window, (zero(2 * dh), zero(1)))[0]

            earlier_keys = (qi + 1) * blk - win > 0
            dq_accs = lax.cond(jnp.logical_and(earlier_keys, still_live(runs1)), all_windows, one_window)
            dqn[pl.ds(qoff, blk), :] = hp.merge(dq_accs)
            return 0

        lax.fori_loop(0, nq, q_step, 0)

        dq, dqg = hp.rms_bwd(qv, hp.rms_r(qv), qg_ref[...] * scale, dqn[...])
        dq_ref[...] = dq.astype(BF16)
        dqg_ref[...] += dqg * scale
        dk, dkg = hp.rms_bwd(kv, hp.rms_r(kv), kg_ref[...], dkn[...])
        dk_ref[...] = dk.astype(BF16)
        dkg_ref[...] += dkg
        dv_ref[...] = dvn[...].astype(BF16)
        finish()

    pair = lambda group: pl.BlockSpec((s, 2 * dh), lambda p: (0, group * (D_ATT // (2 * dh)) + p))
    vec2 = pl.BlockSpec((1, 2 * dh), lambda p: (0, 0))
    vec = pl.BlockSpec((1, dh), lambda p: (0, 0))
    return _call(
        body, "attn_bwd", (steps,), [pair(2), pair(3), pair(4), pair(0), vec2, vec2],
        [pair(0), pair(0), pair(0), vec, vec],
        [jax.ShapeDtypeStruct((s, D_ATT), BF16)] * 3 + [jax.ShapeDtypeStruct((1, dh), F32)] * 2,
        [proj, proj, proj, dya, jnp.tile(qg, (1, 2)), jnp.tile(kg, (1, 2))],
        scratch=[pltpu.VMEM((s, 2 * dh), BF16)] * 4 + [pltpu.VMEM((2, max_windows, blk, 1), F32)]
        + [pltpu.VMEM((s, 2 * dh), F32)] * 3, rider=rider)


def _block_diag(w):
    n, c, d = w.shape
    return jnp.einsum("ncd,nm->ncmd", w, jnp.eye(n, dtype=w.dtype)).reshape(n * c, n * d)


def _diag_blocks(full, n):
    c = full.shape[0] // n
    return jnp.stack([full[i * c:(i + 1) * c, i * c:(i + 1) * c] for i in range(n)])


FFN1 = ["ffn1_w_gate", "ffn1_w_up", "ffn1_w_down"]
FFN2 = ["ffn2_w_gate", "ffn2_w_up", "ffn2_w_down"]
MIXER = ["w_in", "w_out"]


def _pair_sums(gb, names, where):
    theirs = _pair_exchange([gb[n] for n in names], "pair_exchange_" + names[0])
    pair, own = _pair_sum([gb[n] for n in names], theirs, where, "pair_sum_" + names[0])
    return _chip_rider(pair, own)


def _local_step(x, tgt, stacks, conv_stack, small, where):
    big = dict(zip(FFN1, _gather_weights([stacks[n] for n in FFN1], [])))
    wa = _block_diag(small["rg_w_a"]).astype(BF16)
    wx = _block_diag(small["rg_w_x"]).astype(BF16)

    x1, g1, u1, hb1, ab1, *landed = _ffn_fwd(x, small["ffn1_norm"], *[big[n] for n in FFN1],
                                             rider=_gather_rider([stacks[n] for n in MIXER], [conv_stack]))
    big.update(zip(MIXER, _forward_weights(landed[:len(MIXER)], "forward_mixer_weights")))
    conv_w = jnp.transpose(landed[-1], (1, 0, 2)).reshape(CONV_W, D_RNN)
    wout = big["w_out"].reshape(D_MODEL, D_MODEL)
    rg = (conv_w, small["conv_b"], wa, small["rg_b_a"], wx, small["rg_b_x"], small["rg_lambda"])
    proj, hb2 = _mix_pre(x1, small["mix_norm"], big["w_in"])
    yr, hseq = _rglru_fwd(proj, *rg)
    ya, *landed = _attn_fwd(proj, small["q_norm"], small["k_norm"], _gather_rider([stacks[n] for n in FFN2], []))
    big.update(zip(FFN2, _forward_weights(landed, "forward_ffn2_weights")))
    x2 = _mix_post(x1, yr, ya, small["rnn_out_norm"], small["attn_out_norm"], wout)
    dx3, g2, u2, hb3, ab3, loss = _ffn_fwd(x2, small["ffn2_norm"], *[big[n] for n in FFN2], tgt)

    gb, gs, slots = {}, {}, {}
    dx2, dg2, du2, dyb2, gs["ffn2_norm"] = _ffn_bwd_act(x2, small["ffn2_norm"], dx3, g2, u2, *[big[n] for n in FFN2],
                                                        "ffn2_bwd")
    gb["ffn2_w_gate"] = _ffn_wgrad(dg2, hb3, 1.0, "wgrad_gate_ffn2")
    gb["ffn2_w_up"] = _ffn_wgrad(du2, hb3, 1.0, "wgrad_up_ffn2")
    gb["ffn2_w_down"] = _ffn_wgrad(ab3, dyb2, 0.5, "wgrad_down_ffn2")
    dyr, dya, ycat, dxb2, gs["rnn_out_norm"], gs["attn_out_norm"] = _mix_post_bwd(
        dx2, yr, ya, small["rnn_out_norm"], small["attn_out_norm"], wout)
    gb["w_out"] = _wgrad_whole(ycat, dxb2, False, "wgrad_out")
    early = FFN2 + ["w_out"]
    dq, dk, dv, gs["q_norm"], gs["k_norm"], *done = _attn_bwd(
        proj, dya, small["q_norm"], small["k_norm"], _pair_sums(gb, early, where))
    slots.update(zip(early, done))
    dxr, dgate, gs["conv_w"], gs["conv_b"], dwa, gs["rg_b_a"], dwx, gs["rg_b_x"], gs["rg_lambda"] = _rglru_bwd(
        proj, hseq, dyr, *rg)
    gs["rg_w_a"] = _diag_blocks(dwa, RNN_BLOCKS)
    gs["rg_w_x"] = _diag_blocks(dwx, RNN_BLOCKS)
    dpb = jnp.concatenate([dxr, dgate, dq, dk, dv], axis=1)
    dx1, gs["mix_norm"] = _mix_pre_bwd(x1, small["mix_norm"], dx2, dpb, big["w_in"])
    dx0, dg1, du1, dyb1, gs["ffn1_norm"] = _ffn_bwd_act(x, small["ffn1_norm"], dx1, g1, u1, *[big[n] for n in FFN1],
                                                        "ffn1_bwd")

    mine = _place_shard(_pack([gs[n] for n in SMALL]), where, F32, "place_small_grads", by_device=True)
    gb["ffn1_w_gate"], everyone = _ffn_wgrad(dg1, hb1, 1.0, "wgrad_gate_ffn1", _small_rider(mine))
    gb["ffn1_w_up"], slots["ffn1_w_gate"] = _ffn_wgrad(
        du1, hb1, 1.0, "wgrad_up_ffn1", _pair_sums(gb, ["ffn1_w_gate"], where))
    gb["ffn1_w_down"], slots["ffn1_w_up"] = _ffn_wgrad(
        ab1, dyb1, 0.5, "wgrad_down_ffn1", _pair_sums(gb, ["ffn1_w_up"], where))
    gb["w_in"], slots["ffn1_w_down"] = _wgrad_whole(
        hb2, dpb, True, "wgrad_in", _pair_sums(gb, ["ffn1_w_down"], where))
    last = _pair_sums(gb, ["w_in"], where)
    slots["w_in"], = _chip_exchange(last.plain, last.inplace)
    return loss[0, 0], dx0, slots, gs, everyone


ANY = pl.BlockSpec(memory_space=pl.ANY)


def _place():
    x, y, c = lax.axis_index("x"), lax.axis_index("y"), lax.axis_index("c")
    other_chips = [(1 - x, y), (x, 1 - y), (1 - x, 1 - y)]
    return x, y, c, 2 * x + y, other_chips


def _remote(src, dst, send_sem, recv_sem, to):
    return pltpu.make_async_remote_copy(src_ref=src, dst_ref=dst, send_sem=send_sem, recv_sem=recv_sem,
                                        device_id=to, device_id_type=MESH)


def _copy_plan(pairs):
    sends = [functools.partial(_remote, *a) for a, _ in pairs]
    arrivals = [functools.partial(_remote, *b) for _, b in pairs]
    return sends, arrivals


class _Rider:
    def __init__(self, plan, plain, inplace, n_copies=None):
        self.plan, self.plain, self.inplace = plan, list(plain), list(inplace)
        self.n_copies = n_copies or 3 * len(self.inplace)

    def operands(self):
        return self.plain + self.inplace

    def out_shape(self):
        return [jax.ShapeDtypeStruct(a.shape, a.dtype) for a in self.inplace]

    def aliases(self, inputs_before, outputs_before):
        return {inputs_before + len(self.plain) + k: outputs_before + k for k in range(len(self.inplace))}

    def scratch(self):
        return [pltpu.SemaphoreType.DMA((self.n_copies,))] * 2


def _split_refs(refs, n_in, n_out, rider):
    if rider is None:
        return refs[:n_in], refs[n_in:n_in + n_out], refs[n_in + n_out:], None
    r_in, r_out = len(rider.operands()), len(rider.inplace)
    outs_at = n_in + r_in
    rest = refs[outs_at + n_out + r_out:]
    copies = functools.partial(rider.plan, refs[n_in:n_in + len(rider.plain)],
                               refs[outs_at + n_out:outs_at + n_out + r_out], *rest[-2:])
    return refs[:n_in], refs[outs_at:outs_at + n_out], rest[:-2], copies


def _ride(copies, first, last):
    if copies is None:
        return lambda: None

    @pl.when(first)
    def _():
        _start(copies()[0])

    def finish():
        @pl.when(last)
        def _():
            _finish(*copies())

    return finish


def _gather_rider(split, whole):
    n_split = len(split)
    return _Rider(lambda plain, stacks, ss, rs: _gather_ici(stacks, n_split, ss, rs), [], list(split) + list(whole))


def _chip_rider(sums, slots):
    return _Rider(_chip_copies, sums, slots)


def _start(makers):
    for make in makers:
        make().start()


def _finish(sends, arrivals):
    for make in arrivals:
        make().wait_recv()
    for make in sends:
        make().wait_send()


def _half(rows, c):
    return pl.ds(pl.multiple_of(c * rows, 16), rows)


def _gather_weights(split, whole):
    arrs = list(split) + list(whole)
    n, ns = len(arrs), len(split)

    def body(*refs):
        outs = refs[n:2 * n]
        send_sems, recv_sems, fsend_sems, frecv_sems = refs[2 * n:]
        sends, arrivals = _gather_ici(outs, ns, send_sems, recv_sems)
        passes, passed = _gather_d2d(outs[:ns], fsend_sems, frecv_sems)
        _start(sends)
        for k, make in enumerate(arrivals):
            make().wait_recv()
            if k < 3 * ns:
                passes[k]().start()
        _finish(sends + passes, passed)

    return pl.pallas_call(
        body, name="gather_weights",
        in_specs=[ANY] * n, out_specs=[ANY] * n,
        out_shape=[jax.ShapeDtypeStruct(a.shape, a.dtype) for a in arrs],
        input_output_aliases={i: i for i in range(n)},
        scratch_shapes=[pltpu.SemaphoreType.DMA((3 * n,)), pltpu.SemaphoreType.DMA((3 * n,)),
                        pltpu.SemaphoreType.DMA((3 * ns,)), pltpu.SemaphoreType.DMA((3 * ns,))],
    )(*arrs)


def _gather_ici(stacks, n_split, send_sems, recv_sems):
    x, y, c, me, chips = _place()

    def region(i, chip):
        if i < n_split:
            return stacks[i].at[chip, _half(stacks[i].shape[1] // 2, c)]
        return stacks[i].at[chip]

    pairs = []
    for i in range(len(stacks)):
        for p, (cx, cy) in enumerate(chips):
            k = 3 * i + p
            mine, got = region(i, me), region(i, 2 * cx + cy)
            sems, to = (send_sems.at[k], recv_sems.at[k]), (cx, cy, c)
            pairs.append(((mine, mine, *sems, to), (got, got, *sems, to)))
    return _copy_plan(pairs)


def _gather_d2d(stacks, send_sems, recv_sems):
    x, y, c, _, chips = _place()
    sibling = (x, y, 1 - c)
    pairs = []
    for i, stack in enumerate(stacks):
        rows = stack.shape[1] // 2
        for p, (cx, cy) in enumerate(chips):
            k = 3 * i + p
            got, theirs = stack.at[2 * cx + cy, _half(rows, c)], stack.at[2 * cx + cy, _half(rows, 1 - c)]
            sems = (send_sems.at[k], recv_sems.at[k])
            pairs.append(((got, got, *sems, sibling), (theirs, theirs, *sems, sibling)))
    return _copy_plan(pairs)


def _forward_weights(split, name):
    n = len(split)

    def body(*refs):
        sends, arrivals = _gather_d2d(refs[n:2 * n], *refs[2 * n:])
        _start(sends)
        _finish(sends, arrivals)

    return pl.pallas_call(
        body, name=name,
        in_specs=[ANY] * n, out_specs=[ANY] * n,
        out_shape=[jax.ShapeDtypeStruct(a.shape, a.dtype) for a in split],
        input_output_aliases={i: i for i in range(n)},
        scratch_shapes=[pltpu.SemaphoreType.DMA((3 * n,))] * 2,
    )(*split)


def _pair_exchange(grads, name):
    n = len(grads)

    def body(*refs):
        ins, theirs = refs[:n], refs[n:2 * n]
        send_sems, recv_sems = refs[2 * n:]
        x, y, c, _, _ = _place()
        sibling = (x, y, 1 - c)
        sends = [_remote(ins[k].at[:, _half(grads[k].shape[1] // 2, 1 - c)], theirs[k],
                         send_sems.at[k], recv_sems.at[k], sibling) for k in range(n)]
        for cp in sends:
            cp.start()
        for k in range(n):
            _remote(theirs[k], theirs[k], send_sems.at[k], recv_sems.at[k], sibling).wait_recv()
        for cp in sends:
            cp.wait_send()

    return pl.pallas_call(
        body, name=name,
        in_specs=[ANY] * n, out_specs=[ANY] * n,
        out_shape=[jax.ShapeDtypeStruct((g.shape[0], g.shape[1] // 2, g.shape[2]), g.dtype) for g in grads],
        scratch_shapes=[pltpu.SemaphoreType.DMA((n,))] * 2,
    )(*grads)


def _chip_exchange(sums, slots):
    n = len(sums)

    def body(*refs):
        sends, arrivals = _chip_copies(refs[:n], refs[2 * n:3 * n], *refs[3 * n:])
        _start(sends)
        _finish(sends, arrivals)

    return pl.pallas_call(
        body, name="grad_chip_exchange",
        in_specs=[ANY] * (2 * n), out_specs=[ANY] * n,
        out_shape=[jax.ShapeDtypeStruct(a.shape, a.dtype) for a in slots],
        input_output_aliases={n + k: k for k in range(n)},
        scratch_shapes=[pltpu.SemaphoreType.DMA((3 * n,)), pltpu.SemaphoreType.DMA((3 * n,))],
    )(*sums, *slots)


def _chip_copies(sums, slots, send_sems, recv_sems):
    x, y, c, me, chips = _place()
    pairs = []
    for k in range(len(sums)):
        for p, (cx, cy) in enumerate(chips):
            j = 3 * k + p
            got = slots[k].at[2 * cx + cy]
            sems, to = (send_sems.at[j], recv_sems.at[j]), (cx, cy, c)
            pairs.append(((sums[k].at[2 * cx + cy], slots[k].at[me], *sems, to), (got, got, *sems, to)))
    return _copy_plan(pairs)


def _half_swap(halves):
    n = len(halves)

    def body(*refs):
        outs = refs[n:2 * n]
        send_sems, recv_sems = refs[2 * n:]
        x, y, c, _, _ = _place()
        sibling = (x, y, 1 - c)
        sends = [_remote(outs[k].at[c], outs[k].at[c], send_sems.at[k], recv_sems.at[k], sibling) for k in range(n)]
        for cp in sends:
            cp.start()
        for k in range(n):
            got = outs[k].at[1 - c]
            _remote(got, got, send_sems.at[k], recv_sems.at[k], sibling).wait_recv()
        for cp in sends:
            cp.wait_send()

    return pl.pallas_call(
        body, name="grad_half_swap",
        in_specs=[ANY] * n, out_specs=[ANY] * n,
        out_shape=[jax.ShapeDtypeStruct(a.shape, a.dtype) for a in halves],
        input_output_aliases={k: k for k in range(n)},
        scratch_shapes=[pltpu.SemaphoreType.DMA((n,))] * 2,
    )(*halves)


def _small_rider(stack):
    n_dev = 2 * N_CHIPS

    def plan(_, stacks, send_sems, recv_sems):
        x, y, c, _, _ = _place()
        mine = stacks[0].at[4 * x + 2 * y + c]
        pairs = []
        for k in range(1, n_dev):
            px, py, pc = x ^ ((k >> 2) & 1), y ^ ((k >> 1) & 1), c ^ (k & 1)
            got = stacks[0].at[4 * px + 2 * py + pc]
            sems = (send_sems.at[k - 1], recv_sems.at[k - 1])
            pairs.append(((mine, mine, *sems, (px, py, pc)), (got, got, *sems, (px, py, pc))))
        return _copy_plan(pairs)

    return _Rider(plan, [], [stack], n_dev - 1)


def _row_tile(r):
    return r // 4 if r >= 256 and (r // 4) % 16 == 0 else r


def _prefetch_call(body, name, grid, in_specs, out_specs, out_shape):
    spec = pltpu.PrefetchScalarGridSpec(num_scalar_prefetch=1, grid=grid, in_specs=in_specs, out_specs=out_specs)
    return pl.pallas_call(body, name=name, grid_spec=spec, out_shape=out_shape,
                          compiler_params=_params(("arbitrary",) * len(grid)))


def _place_shard(w2d, where, dtype, name, by_device=False):
    r, c = w2d.shape
    tr = _row_tile(r)
    slots = 2 * N_CHIPS if by_device else N_CHIPS
    slot = (lambda s: 2 * s[1] + s[0]) if by_device else (lambda s: s[1])

    def body(where_ref, w_ref, out_ref):
        out_ref[...] = w_ref[...].astype(dtype)

    return _prefetch_call(
        body, name, (r // tr,), [pl.BlockSpec((tr, c), lambda i, s: (i, 0))],
        pl.BlockSpec((None, tr, c), lambda i, s: (slot(s), i, 0)),
        jax.ShapeDtypeStruct((slots, r, c), dtype))(where, w2d)


def _place_shards(w2ds, where, name):
    n = len(w2ds)
    steps = N_CHIPS
    assert all(w.shape[0] % (16 * steps) == 0 for w in w2ds)

    def body(where_ref, *refs):
        for k in range(n):
            refs[n + k][...] = refs[k][...].astype(BF16)

    tile = lambda w: (w.shape[0] // steps, w.shape[1])
    return _prefetch_call(
        body, name, (steps,), [pl.BlockSpec(tile(w), lambda i, s: (i, 0)) for w in w2ds],
        [pl.BlockSpec((None,) + tile(w), lambda i, s: (s[1], i, 0)) for w in w2ds],
        [jax.ShapeDtypeStruct((N_CHIPS,) + w.shape, BF16) for w in w2ds])(where, *w2ds)


def _pair_sum(fulls, theirs, where, name):
    n = len(fulls)

    def body(where_ref, *refs):
        for k in range(n):
            a_ref, b_ref, out_ref, own_ref = refs[k], refs[n + k], refs[2 * n + k], refs[3 * n + k]
            total = (a_ref[...].astype(F32) + b_ref[...].astype(F32)).astype(BF16)
            out_ref[...] = total

            @pl.when(pl.program_id(0) == where_ref[1])
            def _():
                own_ref[...] = total

    half = lambda t: pl.BlockSpec((None,) + t.shape[1:], lambda j, s: (j, s[0], 0))
    blk = lambda t: pl.BlockSpec((None,) + t.shape[1:], lambda j, s: (j, 0, 0))
    own = lambda t: pl.BlockSpec((None,) + t.shape[1:], lambda j, s: (s[1], 0, 0))
    shapes = [jax.ShapeDtypeStruct(t.shape, BF16) for t in theirs]
    outs = _prefetch_call(
        body, name, (N_CHIPS,), [half(t) for t in theirs] + [blk(t) for t in theirs],
        [blk(t) for t in theirs] + [own(t) for t in theirs], shapes + shapes)(where, *fulls, *theirs)
    return outs[:n], outs[n:]


def _chip_sum(slots, where, name):
    n = len(slots)
    steps = 2
    assert all(a.shape[1] % (16 * steps) == 0 for a in slots)

    def body(where_ref, *refs):
        for k in range(n):
            a_ref, out_ref = refs[k], refs[n + k]
            total = a_ref[0].astype(F32)
            for j in range(1, a_ref.shape[0]):
                total = total + a_ref[j].astype(F32)
            out_ref[...] = total

    tile = lambda a: (a.shape[1] // steps, a.shape[2])
    return _prefetch_call(
        body, name, (steps,), [pl.BlockSpec((a.shape[0],) + tile(a), lambda i, s: (0, i, 0)) for a in slots],
        [pl.BlockSpec((None,) + tile(a), lambda i, s: (s[0], i, 0)) for a in slots],
        [jax.ShapeDtypeStruct((2,) + a.shape[1:], F32) for a in slots])(where, *slots)


def _slot_sum(a, name):
    nb, r, c = a.shape
    tr = _row_tile(r)

    def body(a_ref, out_ref):
        total = a_ref[0].astype(F32)
        for j in range(1, nb):
            total = total + a_ref[j].astype(F32)
        out_ref[...] = total

    return pl.pallas_call(
        body, name=name, grid=(r // tr,),
        in_specs=[pl.BlockSpec((nb, tr, c), lambda i: (0, i, 0))],
        out_specs=pl.BlockSpec((tr, c), lambda i: (i, 0)),
        out_shape=jax.ShapeDtypeStruct((r, c), F32), compiler_params=_params(("arbitrary",)),
    )(a)


def _adamw(ws, gs, ms, vs, name, steps=1):
    n = len(ws)
    c1 = 1.0 - ADAM_B1 ** ADAM_STEP
    c2 = 1.0 - ADAM_B2 ** ADAM_STEP
    assert all(w.shape[0] % steps == 0 and (steps == 1 or w.shape[0] // steps % 8 == 0) for w in ws)

    def body(*refs):
        for k in range(n):
            w_ref, g_ref, m_ref, v_ref = (refs[j * n + k] for j in range(4))
            d_ref, m2_ref, v2_ref = (refs[(4 + j) * n + k] for j in range(3))
            gv = g_ref[...]
            m2 = ADAM_B1 * m_ref[...] + (1.0 - ADAM_B1) * gv
            v2 = ADAM_B2 * v_ref[...] + (1.0 - ADAM_B2) * (gv * gv)
            m2_ref[...] = m2
            v2_ref[...] = v2
            d_ref[...] = -ADAM_LR * ((m2 / c1) / (jnp.sqrt(v2 / c2) + ADAM_EPS) + ADAM_WD * w_ref[...])

    blks = [pl.BlockSpec((w.shape[0] // steps, w.shape[1]), lambda i: (i, 0)) for w in ws]
    shapes = [jax.ShapeDtypeStruct(w.shape, F32) for w in ws]
    outs = pl.pallas_call(
        body, name=name, grid=(steps,), in_specs=blks * 4, out_specs=blks * 3, out_shape=shapes * 3,
        compiler_params=_params(("arbitrary",)),
    )(*ws, *gs, *ms, *vs)
    return outs[:n], outs[n:2 * n], outs[2 * n:]


WEIGHTS = ["ffn1_norm", "ffn1_w_gate", "ffn1_w_up", "ffn1_w_down", "mix_norm", "w_in", "conv_w", "conv_b",
           "rg_w_a", "rg_b_a", "rg_w_x", "rg_b_x", "rg_lambda", "q_norm", "k_norm", "rnn_out_norm",
           "attn_out_norm", "w_out", "ffn2_norm", "ffn2_w_gate", "ffn2_w_up", "ffn2_w_down"]
BIG = ["ffn1_w_gate", "ffn1_w_up", "ffn1_w_down", "w_in", "w_out", "ffn2_w_gate", "ffn2_w_up", "ffn2_w_down"]
SMALL = [n for n in WEIGHTS if n not in BIG]
PACK_LANES = 128
PACK_ROW_ALIGN = 8


def _hidden_major(name, a):
    return jnp.transpose(a) if name.endswith(("w_gate", "w_up")) else a


def _pack(parts):
    flat = jnp.concatenate([p.reshape(-1) for p in parts])
    unit = PACK_LANES * PACK_ROW_ALIGN
    padded = -(-flat.shape[0] // unit) * unit
    return jnp.pad(flat, (0, padded - flat.shape[0])).reshape(-1, PACK_LANES)


def _unpack(packed, shapes):
    flat = packed.reshape(-1)
    out, at = [], 0
    for shp in shapes:
        size = math.prod(shp)
        out.append(flat[at:at + size].reshape(shp))
        at += size
    return out


def kernel(x, ffn1_norm, ffn1_w_gate, ffn1_w_up, ffn1_w_down, mix_norm, w_in, conv_w, conv_b, rg_w_a, rg_b_a, rg_w_x, rg_b_x, rg_lambda, q_norm, k_norm, rnn_out_norm, attn_out_norm, w_out, ffn2_norm, ffn2_w_gate, ffn2_w_up, ffn2_w_down, loss_target, m_ffn1_norm, m_ffn1_w_gate, m_ffn1_w_up, m_ffn1_w_down, m_mix_norm, m_w_in, m_conv_w, m_conv_b, m_rg_w_a, m_rg_b_a, m_rg_w_x, m_rg_b_x, m_rg_lambda, m_q_norm, m_k_norm, m_rnn_out_norm, m_attn_out_norm, m_w_out, m_ffn2_norm, m_ffn2_w_gate, m_ffn2_w_up, m_ffn2_w_down, v_ffn1_norm, v_ffn1_w_gate, v_ffn1_w_up, v_ffn1_w_down, v_mix_norm, v_w_in, v_conv_w, v_conv_b, v_rg_w_a, v_rg_b_a, v_rg_w_x, v_rg_b_x, v_rg_lambda, v_q_norm, v_k_norm, v_rnn_out_norm, v_attn_out_norm, v_w_out, v_ffn2_norm, v_ffn2_w_gate, v_ffn2_w_up, v_ffn2_w_down):
    given = dict(locals())
    w = {n: given[n] for n in WEIGHTS}
    m = {n: given["m_" + n] for n in WEIGHTS}
    v = {n: given["v_" + n] for n in WEIGHTS}
    chip = 2 * lax.axis_index("x") + lax.axis_index("y")

    where = jnp.stack([lax.axis_index("c"), chip]).astype(jnp.int32)

    stacks = dict(zip(BIG, _place_shards([_hidden_major(n, w[n][0]) for n in BIG], where, "place_weights")))
    conv_stack = _place_shard(w["conv_w"][0], where, F32, "place_conv_w")
    small = {n: (w[n][0] if w[n].ndim > 2 else w[n]) for n in SMALL if n != "conv_w"}

    loss, grad_x, slots, gs, everyone = _local_step(x[0], loss_target[0], stacks, conv_stack, small, where)
    loss = lax.psum(loss, ("x", "y", "c"))

    swapped = _half_swap(_chip_sum([slots[n] for n in BIG], where, "chip_sums"))
    g2s = [t.reshape(t.shape[0] * t.shape[1], t.shape[2]) for t in swapped]
    flat = lambda tree: [_hidden_major(n, tree[n][0]) for n in BIG]
    d2s, m2s, v2s = _adamw(flat(w), g2s, flat(m), flat(v), "adamw_weights", ADAMW_STEPS)
    grads, deltas, new_m, new_v = {}, {}, {}, {}
    for tree, parts in ((grads, g2s), (deltas, d2s), (new_m, m2s), (new_v, v2s)):
        tree.update({n: _hidden_major(n, a).reshape(w[n].shape) for n, a in zip(BIG, parts)})

    full_shapes = [gs[n].shape for n in SMALL]
    g_small = _slot_sum(everyone, "small_grad_sum")
    g_parts = dict(zip(SMALL, _unpack(g_small, full_shapes)))
    quarter = D_RNN // N_CHIPS
    g_parts["conv_w"] = lax.dynamic_slice_in_dim(g_parts["conv_w"], chip * quarter, quarter, axis=1)
    local_shapes = [w[n].shape for n in SMALL]
    pk = lambda tree: _pack([tree[n] for n in SMALL])
    (d_s,), (m_s,), (v_s,) = _adamw([pk(w)], [pk(g_parts)], [pk(m)], [pk(v)], "adamw_small")
    for tree, packed in ((grads, pk(g_parts)), (deltas, d_s), (new_m, m_s), (new_v, v_s)):
        tree.update(zip(SMALL, _unpack(packed, local_shapes)))

    return (loss, grad_x.reshape(x.shape), *[grads[n] for n in WEIGHTS], *[deltas[n] for n in WEIGHTS],
            *[new_m[n] for n in WEIGHTS], *[new_v[n] for n in WEIGHTS])
```

```python
import functools
import math

import jax
import jax.numpy as jnp
from jax import lax
from jax.experimental import pallas as pl
from jax.experimental.pallas import tpu as pltpu

F32 = jnp.float32
BF16 = jnp.bfloat16
MESH = pl.DeviceIdType.MESH

D_MODEL = 1024
N_CHIPS = 4
D_RNN = 512
D_ATT = 512
N_HEADS = 8
HEAD_DIM = 64
RNN_BLOCKS = 8
CONV_W = 4
RG_C = 8.0
N_IN = 2 * D_RNN + 3 * D_ATT
EPS = 1e-6
ATT_BLOCK = 128
ATT_WINDOW = 384
ATT_SPLIT = 256
EXP_ZERO = -105.0

ADAM_LR = 0.001
ADAM_B1 = 0.9
ADAM_B2 = 0.999
ADAM_EPS = 1e-08
ADAM_WD = 0.01
ADAM_STEP = 10

V7X_VMEM_LIMIT = 56 * 1024 * 1024
TOKEN_TILE = 512
SUBLANES = 8
FFN_TILE = 256
WGRAD_TILE = 2048
WHOLE_TILE = 1024
ADAMW_STEPS = 8

GELU_K0 = math.sqrt(2.0 / math.pi)
GELU_K1 = 0.044715


def _params(sem=None):
    return pltpu.CompilerParams(dimension_semantics=sem, vmem_limit_bytes=V7X_VMEM_LIMIT)


def _dot(a, b):
    return jnp.dot(a, b, preferred_element_type=F32)


def _dot_nt(a, b):
    return lax.dot_general(a, b, (((1,), (1,)), ((), ())), preferred_element_type=F32)


def _dot_tn(a, b):
    return lax.dot_general(a, b, (((0,), (0,)), ((), ())), preferred_element_type=F32)


def _sigmoid(x):
    return 1.0 / (1.0 + jnp.exp(-x))


def _rms_r(xv):
    return lax.rsqrt(jnp.mean(xv * xv, axis=-1, keepdims=True) + EPS)


def _rms_bwd(xv, r, nw, dh):
    t = dh * nw
    dx = r * t - xv * (r * r * r * jnp.mean(t * xv, axis=-1, keepdims=True))
    dn = jnp.sum(dh * xv * r, axis=0, keepdims=True)
    return dx, dn


def _gelu(x):
    t = jnp.tanh(GELU_K0 * (x + GELU_K1 * x * x * x))
    return 0.5 * x * (1.0 + t)


def _gelu_grad(x):
    t = jnp.tanh(GELU_K0 * (x + GELU_K1 * x * x * x))
    return 0.5 * (1.0 + t) + 0.5 * x * (1.0 - t * t) * (GELU_K0 * (1.0 + 3.0 * GELU_K1 * x * x))


def _expm1_neg(x):
    p = 1.0 + x * (1.0 / 6.0)
    for k in (5.0, 4.0, 3.0, 2.0):
        p = 1.0 + x * (1.0 / k) * p
    return jnp.where(x > -0.25, x * p, jnp.exp(x) - 1.0)


def _log_sigmoid(x):
    return jnp.minimum(x, 0.0) - jnp.log(1.0 + jnp.exp(-jnp.abs(x)))


def _tile(s):
    return min(TOKEN_TILE, s)


def _ffn_fwd(x, nw, wg, wu, wd, tgt=None, rider=None):
    s, d = x.shape
    nb, fb, _ = wg.shape
    tm = min(FFN_TILE, s)
    ni = s // tm
    assert s % tm == 0
    with_loss = tgt is not None
    n_in, n_out = 5 + with_loss, 5 + with_loss

    def body(*refs):
        ins, outs, _, copies = _split_refs(refs, n_in, n_out, rider)
        x_ref, nw_ref, wg_ref, wu_ref, wd_ref = ins[:5]
        out_ref, g_ref, u_ref, hb_ref, ab_ref = outs[:5]
        i = pl.program_id(0)
        finish = _ride(copies, i == 0, i == ni - 1)

        xv = x_ref[...]
        hb = (xv * _rms_r(xv) * nw_ref[...]).astype(BF16)
        hb_ref[...] = hb
        y = jnp.zeros((tm, d), F32)
        for jb in range(nb):
            g = _dot_nt(hb, wg_ref[jb])
            u = _dot_nt(hb, wu_ref[jb])
            g_ref[jb] = g.astype(BF16)
            u_ref[jb] = u.astype(BF16)
            ab = (g * _sigmoid(g) * u).astype(BF16)
            ab_ref[jb] = ab
            y = y + _dot(ab, wd_ref[jb])
        y = xv + 0.5 * y
        if with_loss:
            tgt_ref, loss_ref = ins[5], outs[5]
            diff = y - tgt_ref[...]
            out_ref[...] = diff * (1.0 / d)

            @pl.when(i == 0)
            def _():
                loss_ref[...] = jnp.zeros_like(loss_ref)

            loss_ref[...] += jnp.sum(diff * diff) * (0.5 / d)
        else:
            out_ref[...] = y
        finish()

    row = pl.BlockSpec((tm, d), lambda i: (i, 0))
    weight = pl.BlockSpec((nb, fb, d), lambda i: (0, 0, 0), pipeline_mode=pl.Buffered(1))
    in_specs = [row, pl.BlockSpec((1, d), lambda i: (0, 0)), weight, weight, weight]
    args = [x, nw, wg, wu, wd]
    if with_loss:
        in_specs.append(row)
        args.append(tgt)
    blk = pl.BlockSpec((nb, tm, fb), lambda i: (0, i, 0))
    out_shape = [jax.ShapeDtypeStruct((s, d), F32), jax.ShapeDtypeStruct((nb, s, fb), BF16),
                 jax.ShapeDtypeStruct((nb, s, fb), BF16), jax.ShapeDtypeStruct((s, d), BF16),
                 jax.ShapeDtypeStruct((nb, s, fb), BF16)]
    out_specs = [row, blk, blk, row, blk]
    if with_loss:
        out_shape.append(jax.ShapeDtypeStruct((1, 128), F32))
        out_specs.append(pl.BlockSpec((1, 128), lambda i: (0, 0)))
    return _call(body, "ffn_fwd_loss" if with_loss else "ffn_fwd", (ni,), in_specs, out_specs, out_shape, args,
                 rider=rider)


def _call(body, name, grid, in_specs, out_specs, out_shape, args, scratch=(), rider=None):
    in_specs, out_specs, out_shape, scratch = list(in_specs), list(out_specs), list(out_shape), list(scratch)
    extra, aliases = [], {}
    if rider is not None:
        extra = rider.operands()
        aliases = rider.aliases(len(args), len(out_shape))
        in_specs += [ANY] * len(extra)
        out_specs += [ANY] * len(rider.inplace)
        out_shape += rider.out_shape()
        scratch += rider.scratch()
    return pl.pallas_call(
        body, name=name, grid=grid, in_specs=in_specs, out_specs=out_specs, out_shape=out_shape,
        input_output_aliases=aliases, scratch_shapes=scratch,
        compiler_params=_params(("arbitrary",) * len(grid)),
    )(*args, *extra)


def _ffn_bwd_act(x, nw, dy, g, u, wg, wu, wd, name, rider=None):
    s, d = x.shape
    nb, fb, _ = wg.shape
    tm = min(FFN_TILE, s)
    assert s % tm == 0

    def body(*refs):
        ins, outs, _, copies = _split_refs(refs, 8, 5, rider)
        x_ref, nw_ref, dy_ref, g_ref, u_ref, wg_ref, wu_ref, wd_ref = ins
        dx_ref, dg_ref, du_ref, dyb_ref, dnw_ref = outs
        finish = _ride(copies, pl.program_id(0) == 0, pl.program_id(0) == s // tm - 1)
        dyv = dy_ref[...]
        dyb = dyv.astype(BF16)
        dyb_ref[...] = dyb
        dh = jnp.zeros((tm, d), F32)
        for jb in range(nb):
            da = 0.5 * _dot_nt(dyb, wd_ref[jb])
            gv = g_ref[jb].astype(F32)
            sg = _sigmoid(gv)
            dub = (da * (gv * sg)).astype(BF16)
            dgb = (da * u_ref[jb].astype(F32) * (sg * (1.0 + gv * (1.0 - sg)))).astype(BF16)
            dg_ref[jb] = dgb
            du_ref[jb] = dub
            dh = dh + _dot(dgb, wg_ref[jb]) + _dot(dub, wu_ref[jb])
        xv = x_ref[...]
        dx, dn = _rms_bwd(xv, _rms_r(xv), nw_ref[...], dh)
        dx_ref[...] = dyv + dx

        @pl.when(pl.program_id(0) == 0)
        def _():
            dnw_ref[...] = jnp.zeros_like(dnw_ref)

        dnw_ref[...] += dn
        finish()

    row = pl.BlockSpec((tm, d), lambda i: (i, 0))
    vec = pl.BlockSpec((1, d), lambda i: (0, 0))
    blk = pl.BlockSpec((nb, tm, fb), lambda i: (0, i, 0))
    weight = pl.BlockSpec((nb, fb, d), lambda i: (0, 0, 0), pipeline_mode=pl.Buffered(1))
    return _call(
        body, name, (s // tm,), [row, vec, row, blk, blk, weight, weight, weight], [row, blk, blk, row, vec],
        [jax.ShapeDtypeStruct((s, d), F32), jax.ShapeDtypeStruct((nb, s, fb), BF16),
         jax.ShapeDtypeStruct((nb, s, fb), BF16), jax.ShapeDtypeStruct((s, d), BF16),
         jax.ShapeDtypeStruct((1, d), F32)],
        [x, nw, dy, g, u, wg, wu, wd], rider=rider)


def _wgrad(a, b, a_spec, b_spec, out_rows, out_cols, scale, name, tk, rider=None):
    s = a.shape[-2]
    nk = s // tk
    assert s % tk == 0

    def body(*refs):
        (a_ref, b_ref), (out_ref,), (acc,), copies = _split_refs(refs, 2, 1, rider)
        j, k = pl.program_id(0), pl.program_id(1)
        finish = _ride(copies, jnp.logical_and(j == 0, k == 0), jnp.logical_and(j == N_CHIPS - 1, k == nk - 1))

        @pl.when(k == 0)
        def _():
            acc[...] = jnp.zeros_like(acc)

        acc[...] += _dot_tn(a_ref[...], b_ref[...])

        @pl.when(k == nk - 1)
        def _():
            out_ref[...] = (acc[...] * scale).astype(BF16)

        finish()

    outs = _call(
        body, name, (N_CHIPS, nk), [a_spec(tk), b_spec(tk)],
        [pl.BlockSpec((None, out_rows, out_cols), lambda j, k: (j, 0, 0))],
        [jax.ShapeDtypeStruct((N_CHIPS, out_rows, out_cols), BF16)], [a, b],
        scratch=[pltpu.VMEM((out_rows, out_cols), F32)], rider=rider)
    return outs[0] if rider is None else outs


def _wgrad_whole(a, b, col_blocks, name, rider=None):
    s, m = a.shape
    n = b.shape[1]
    tk = min(WHOLE_TILE, s)
    nk = s // tk
    assert s % tk == 0
    out_shape = (N_CHIPS, m, n // N_CHIPS) if col_blocks else (N_CHIPS, m // N_CHIPS, n)

    def body(*refs):
        (a_ref, b_ref), (out_ref,), (acc,), copies = _split_refs(refs, 2, 1, rider)
        k = pl.program_id(0)
        finish = _ride(copies, k == 0, k == nk - 1)

        @pl.when(k == 0)
        def _():
            acc[...] = jnp.zeros_like(acc)

        acc[...] += _dot_tn(a_ref[...], b_ref[...])

        @pl.when(k == nk - 1)
        def _():
            for j in range(N_CHIPS):
                if col_blocks:
                    out_ref[j] = acc[:, j * out_shape[2]:(j + 1) * out_shape[2]].astype(BF16)
                else:
                    out_ref[j] = acc[j * out_shape[1]:(j + 1) * out_shape[1], :].astype(BF16)

        finish()

    outs = _call(
        body, name, (nk,), [pl.BlockSpec((tk, m), lambda k: (k, 0)), pl.BlockSpec((tk, n), lambda k: (k, 0))],
        [pl.BlockSpec(out_shape, lambda k: (0, 0, 0))], [jax.ShapeDtypeStruct(out_shape, BF16)], [a, b],
        scratch=[pltpu.VMEM((m, n), F32)], rider=rider)
    return outs[0] if rider is None else outs


def _ffn_wgrad(stack, shared, scale, name, rider=None):
    s, d = shared.shape
    fb = stack.shape[-1]
    return _wgrad(stack, shared, lambda tk: pl.BlockSpec((None, tk, fb), lambda j, k: (j, k, 0)),
                  lambda tk: pl.BlockSpec((tk, d), lambda j, k: (k, 0)), fb, d, scale, name,
                  min(WGRAD_TILE, s), rider)


def _mix_pre(x, nw, win):
    s, d = x.shape
    nb, _, cb = win.shape
    tm = min(FFN_TILE, s)
    assert s % tm == 0

    def body(x_ref, nw_ref, w_ref, p_ref, hb_ref):
        xv = x_ref[...]
        hb = (xv * _rms_r(xv) * nw_ref[...]).astype(BF16)
        hb_ref[...] = hb
        for j in range(nb):
            p_ref[:, j * cb:(j + 1) * cb] = _dot(hb, w_ref[j])

    row = pl.BlockSpec((tm, d), lambda i: (i, 0))
    return pl.pallas_call(
        body, name="mix_pre", grid=(s // tm,),
        in_specs=[row, pl.BlockSpec((1, d), lambda i: (0, 0)),
                  pl.BlockSpec((nb, d, cb), lambda i: (0, 0, 0), pipeline_mode=pl.Buffered(1))],
        out_specs=[pl.BlockSpec((tm, nb * cb), lambda i: (i, 0)), row],
        out_shape=[jax.ShapeDtypeStruct((s, nb * cb), F32), jax.ShapeDtypeStruct((s, d), BF16)],
        compiler_params=_params(("arbitrary",)),
    )(x, nw, win)


def _mix_pre_bwd(x, nw, dres, dpb, win):
    s, d = x.shape
    nb, _, cb = win.shape
    tm = min(FFN_TILE, s)
    assert s % tm == 0

    def body(x_ref, nw_ref, dres_ref, dp_ref, w_ref, dx_ref, dnw_ref):
        dh = jnp.zeros((tm, d), F32)
        for j in range(nb):
            dh = dh + _dot_nt(dp_ref[:, j * cb:(j + 1) * cb], w_ref[j])
        xv = x_ref[...]
        dx, dn = _rms_bwd(xv, _rms_r(xv), nw_ref[...], dh)
        dx_ref[...] = dres_ref[...] + dx

        @pl.when(pl.program_id(0) == 0)
        def _():
            dnw_ref[...] = jnp.zeros_like(dnw_ref)

        dnw_ref[...] += dn

    row = pl.BlockSpec((tm, d), lambda i: (i, 0))
    vec = pl.BlockSpec((1, d), lambda i: (0, 0))
    return pl.pallas_call(
        body, name="mix_pre_bwd", grid=(s // tm,),
        in_specs=[row, vec, row, pl.BlockSpec((tm, nb * cb), lambda i: (i, 0)),
                  pl.BlockSpec((nb, d, cb), lambda i: (0, 0, 0), pipeline_mode=pl.Buffered(1))],
        out_specs=[row, vec],
        out_shape=[jax.ShapeDtypeStruct((s, d), F32), jax.ShapeDtypeStruct((1, d), F32)],
        compiler_params=_params(("arbitrary",)),
    )(x, nw, dres, dpb, win)


def _mix_post(x, yr, ya, nr, na, wout):
    s, d = x.shape
    h = yr.shape[1]
    tm = _tile(s)

    def body(x_ref, yr_ref, ya_ref, nr_ref, na_ref, w_ref, out_ref):
        yrv = yr_ref[...]
        yav = ya_ref[...]
        onb = (yrv * _rms_r(yrv) * nr_ref[...]).astype(BF16)
        oab = (yav * _rms_r(yav) * na_ref[...]).astype(BF16)
        out_ref[...] = x_ref[...] + _dot(onb, w_ref[0:h, :]) + _dot(oab, w_ref[h:2 * h, :])

    row = pl.BlockSpec((tm, d), lambda i: (i, 0))
    half = pl.BlockSpec((tm, h), lambda i: (i, 0))
    vec = pl.BlockSpec((1, h), lambda i: (0, 0))
    return pl.pallas_call(
        body, name="mix_post", grid=(s // tm,),
        in_specs=[row, half, half, vec, vec, pl.BlockSpec((2 * h, d), lambda i: (0, 0))],
        out_specs=row, out_shape=jax.ShapeDtypeStruct((s, d), F32),
        compiler_params=_params(("arbitrary",)),
    )(x, yr, ya, nr, na, wout)


def _mix_post_bwd(dx, yr, ya, nr, na, wout):
    s, d = dx.shape
    h = yr.shape[1]
    tm = _tile(s)

    def body(dx_ref, yr_ref, ya_ref, nr_ref, na_ref, w_ref,
             dyr_ref, dya_ref, yc_ref, dxb_ref, dnr_ref, dna_ref):
        i = pl.program_id(0)
        dxb = dx_ref[...].astype(BF16)
        dxb_ref[...] = dxb
        dyc = _dot_nt(dxb, w_ref[...])
        yrv = yr_ref[...]
        yav = ya_ref[...]
        rr = _rms_r(yrv)
        ra = _rms_r(yav)
        yc_ref[:, 0:h] = (yrv * rr * nr_ref[...]).astype(BF16)
        yc_ref[:, h:2 * h] = (yav * ra * na_ref[...]).astype(BF16)
        dyr, dnr = _rms_bwd(yrv, rr, nr_ref[...], dyc[:, 0:h])
        dya, dna = _rms_bwd(yav, ra, na_ref[...], dyc[:, h:2 * h])
        dyr_ref[...] = dyr
        dya_ref[...] = dya

        @pl.when(i == 0)
        def _():
            dnr_ref[...] = jnp.zeros_like(dnr_ref)
            dna_ref[...] = jnp.zeros_like(dna_ref)

        dnr_ref[...] += dnr
        dna_ref[...] += dna

    row = pl.BlockSpec((tm, d), lambda i: (i, 0))
    half = pl.BlockSpec((tm, h), lambda i: (i, 0))
    vec = pl.BlockSpec((1, h), lambda i: (0, 0))
    return pl.pallas_call(
        body, name="mix_post_bwd", grid=(s // tm,),
        in_specs=[row, half, half, vec, vec, pl.BlockSpec((2 * h, d), lambda i: (0, 0))],
        out_specs=[half, half, pl.BlockSpec((tm, 2 * h), lambda i: (i, 0)), row, vec, vec],
        out_shape=[jax.ShapeDtypeStruct((s, h), F32), jax.ShapeDtypeStruct((s, h), F32),
                   jax.ShapeDtypeStruct((s, 2 * h), BF16), jax.ShapeDtypeStruct((s, d), BF16),
                   jax.ShapeDtypeStruct((1, h), F32), jax.ShapeDtypeStruct((1, h), F32)],
        compiler_params=_params(("arbitrary",)),
    )(dx, yr, ya, nr, na, wout)


def _shift_down(xv, s, prev8):
    rolled = pltpu.roll(xv, s, 0)
    row8 = lax.broadcasted_iota(jnp.int32, prev8.shape, 0)
    head = jnp.where(row8 < s, pltpu.roll(prev8, s, 0), rolled[0:8, :])
    return jnp.concatenate([head, rolled[8:, :]], axis=0)


def _shift_up(xv, s, next8):
    n = xv.shape[0]
    rolled = pltpu.roll(xv, n - s, 0)
    row8 = lax.broadcasted_iota(jnp.int32, next8.shape, 0)
    tail = jnp.where(row8 >= 8 - s, pltpu.roll(next8, 8 - s, 0), rolled[n - 8:, :])
    return jnp.concatenate([rolled[:n - 8, :], tail], axis=0)


def _scan_fwd(a, b):
    n = a.shape[0]
    sub = lax.broadcasted_iota(jnp.int32, a.shape, 0) % SUBLANES
    s = 1
    while s < SUBLANES:
        ok = sub >= s
        b = jnp.where(ok, a * pltpu.roll(b, s, 0) + b, b)
        a = jnp.where(ok, a * pltpu.roll(a, s, 0), a)
        s *= 2
    groups = []
    before = jnp.zeros((1, a.shape[1]), F32)
    for g in range(n // SUBLANES):
        rows = slice(g * SUBLANES, (g + 1) * SUBLANES)
        groups.append(a[rows] * before + b[rows])
        before = groups[-1][SUBLANES - 1:]
    return jnp.concatenate(groups, axis=0)


def _scan_bwd(a, b):
    n = a.shape[0]
    sub = lax.broadcasted_iota(jnp.int32, a.shape, 0) % SUBLANES
    s = 1
    while s < SUBLANES:
        ok = sub < SUBLANES - s
        b = jnp.where(ok, a * pltpu.roll(b, n - s, 0) + b, b)
        a = jnp.where(ok, a * pltpu.roll(a, n - s, 0), a)
        s *= 2
    groups = []
    after = jnp.zeros((1, a.shape[1]), F32)
    for g in reversed(range(n // SUBLANES)):
        rows = slice(g * SUBLANES, (g + 1) * SUBLANES)
        groups.append(a[rows] * after + b[rows])
        after = groups[-1][:1]
    return jnp.concatenate(groups[::-1], axis=0)


def _rglru_gates(xv, prev8, cw_ref, cb_ref, wa_ref, ba_ref, wx_ref, bx_ref, lam_ref):
    x1 = _shift_down(xv, 1, prev8)
    x2 = _shift_down(xv, 2, prev8)
    x3 = _shift_down(xv, 3, prev8)
    xc = cw_ref[3:4, :] * xv + cw_ref[2:3, :] * x1 + cw_ref[1:2, :] * x2 + cw_ref[0:1, :] * x3 + cb_ref[...]
    xcb = xc.astype(BF16)
    r = _sigmoid(_dot(xcb, wa_ref[...]) + ba_ref[...])
    ig = _sigmoid(_dot(xcb, wx_ref[...]) + bx_ref[...])
    c = RG_C * _log_sigmoid(lam_ref[...])
    la = r * c
    a = jnp.exp(la)
    m = jnp.sqrt(-_expm1_neg(2.0 * la))
    return (x1, x2, x3), xc, xcb, r, ig, c, a, m


def _rglru_fwd(proj, cw, cb, wa, ba, wx, bx, lam):
    s = proj.shape[0]
    w = D_RNN
    tm = _tile(s)

    def body(xr_ref, gate_ref, cw_ref, cb_ref, wa_ref, ba_ref, wx_ref, bx_ref, lam_ref,
             y_ref, h_ref, prev, hlast):
        @pl.when(pl.program_id(0) == 0)
        def _():
            prev[...] = jnp.zeros_like(prev)
            hlast[...] = jnp.zeros_like(hlast)

        xv = xr_ref[...]
        _, xc, _, _, ig, _, a, m = _rglru_gates(xv, prev[...], cw_ref, cb_ref, wa_ref, ba_ref,
                                                wx_ref, bx_ref, lam_ref)
        b = m * (ig * xc)
        row = lax.broadcasted_iota(jnp.int32, b.shape, 0)
        b = jnp.where(row == 0, b + a * hlast[...], b)
        h = _scan_fwd(a, b)
        h_ref[...] = h
        y_ref[...] = h * _gelu(gate_ref[...])
        prev[...] = xv[tm - 8:, :]
        hlast[...] = h[tm - 1:tm, :]

    vec = pl.BlockSpec((1, w), lambda i: (0, 0))
    sq = pl.BlockSpec((w, w), lambda i: (0, 0))
    out = pl.BlockSpec((tm, w), lambda i: (i, 0))
    return pl.pallas_call(
        body, name="rglru_fwd", grid=(s // tm,),
        in_specs=[pl.BlockSpec((tm, w), lambda i: (i, 0)), pl.BlockSpec((tm, w), lambda i: (i, 1)),
                  pl.BlockSpec((CONV_W, w), lambda i: (0, 0)), vec, sq, vec, sq, vec, vec],
        out_specs=[out, out],
        out_shape=[jax.ShapeDtypeStruct((s, w), F32), jax.ShapeDtypeStruct((s, w), F32)],
        scratch_shapes=[pltpu.VMEM((8, w), F32), pltpu.VMEM((1, w), F32)],
        compiler_params=_params(("arbitrary",)),
    )(proj, proj, cw, cb, wa, ba, wx, bx, lam)


def _rglru_bwd(proj, hseq, dyr, cw, cb, wa, ba, wx, bx, lam):
    s = proj.shape[0]
    w = D_RNN
    tm = _tile(s)
    nt = s // tm
    t8 = tm // 8

    def body(xr_ref, xp_ref, gate_ref, h_ref, hp_ref, dy_ref, cw_ref, cb_ref, wa_ref, ba_ref,
             wx_ref, bx_ref, lam_ref,
             dxr_ref, dgate_ref, dcw_ref, dcb_ref, dwa_ref, dba_ref, dwx_ref, dbx_ref, dlam_ref,
             carry, dxc_next):
        i = pl.program_id(0)
        first_tile = i == nt - 1

        @pl.when(i == 0)
        def _():
            carry[...] = jnp.zeros_like(carry)
            dxc_next[...] = jnp.zeros_like(dxc_next)
            for ref in (dcw_ref, dcb_ref, dwa_ref, dba_ref, dwx_ref, dbx_ref, dlam_ref):
                ref[...] = jnp.zeros_like(ref)

        xv = xr_ref[...]
        prev8 = jnp.where(first_tile, 0.0, xp_ref[...])
        hprev8 = jnp.where(first_tile, 0.0, hp_ref[...])
        (x1, x2, x3), xc, xcb, r, ig, c, a, m = _rglru_gates(
            xv, prev8, cw_ref, cb_ref, wa_ref, ba_ref, wx_ref, bx_ref, lam_ref)
        gv = gate_ref[...]
        hv = h_ref[...]
        dy = dy_ref[...]
        dgate_ref[...] = (dy * hv * _gelu_grad(gv)).astype(BF16)
        dh = dy * _gelu(gv)
        row = lax.broadcasted_iota(jnp.int32, dh.shape, 0)
        dh = jnp.where(row == tm - 1, dh + carry[...], dh)
        a_up = jnp.where(row == tm - 1, 0.0, pltpu.roll(a, tm - 1, 0))
        lam_t = _scan_bwd(a_up, dh)
        carry[...] = a[0:1, :] * lam_t[0:1, :]
        hm1 = _shift_down(hv, 1, hprev8)
        da = lam_t * hm1
        ixc = ig * xc
        dm = lam_t * ixc
        dig = lam_t * m * xc
        dxc = lam_t * m * ig
        dla = da * a - dm * (a * a) / m
        dr = dla * c
        dlam_ref[...] += jnp.sum(dla * r, axis=0, keepdims=True)
        dpa = dr * r * (1.0 - r)
        dpi = dig * ig * (1.0 - ig)
        dba_ref[...] += jnp.sum(dpa, axis=0, keepdims=True)
        dbx_ref[...] += jnp.sum(dpi, axis=0, keepdims=True)
        dpab = dpa.astype(BF16)
        dpib = dpi.astype(BF16)
        dwa_ref[...] += _dot_tn(xcb, dpab)
        dwx_ref[...] += _dot_tn(xcb, dpib)
        dxc = dxc + _dot_nt(dpab, wa_ref[...]) + _dot_nt(dpib, wx_ref[...])
        dcb_ref[...] += jnp.sum(dxc, axis=0, keepdims=True)
        dcw_ref[3:4, :] += jnp.sum(dxc * xv, axis=0, keepdims=True)
        dcw_ref[2:3, :] += jnp.sum(dxc * x1, axis=0, keepdims=True)
        dcw_ref[1:2, :] += jnp.sum(dxc * x2, axis=0, keepdims=True)
        dcw_ref[0:1, :] += jnp.sum(dxc * x3, axis=0, keepdims=True)
        nxt = dxc_next[...]
        dxr = (cw_ref[3:4, :] * dxc + cw_ref[2:3, :] * _shift_up(dxc, 1, nxt)
               + cw_ref[1:2, :] * _shift_up(dxc, 2, nxt) + cw_ref[0:1, :] * _shift_up(dxc, 3, nxt))
        dxr_ref[...] = dxr.astype(BF16)
        dxc_next[...] = dxc[0:8, :]

        @pl.when(first_tile)
        def _():
            lv = lam_ref[...]
            dlam_ref[...] = dlam_ref[...] * (RG_C * _sigmoid(-lv))

    rev = lambda i: nt - 1 - i
    vec = pl.BlockSpec((1, w), lambda i: (0, 0))
    sq = pl.BlockSpec((w, w), lambda i: (0, 0))
    cur = lambda col: pl.BlockSpec((tm, w), lambda i: (rev(i), col))
    before = lambda cols: pl.BlockSpec((8, w), lambda i: (jnp.maximum(rev(i) * t8 - 1, 0), 0))
    return pl.pallas_call(
        body, name="rglru_bwd", grid=(nt,),
        in_specs=[cur(0), before(None), cur(1), cur(0), before(None), cur(0),
                  pl.BlockSpec((CONV_W, w), lambda i: (0, 0)), vec, sq, vec, sq, vec, vec],
        out_specs=[cur(0), cur(0), pl.BlockSpec((CONV_W, w), lambda i: (0, 0)), vec, sq, vec, sq, vec, vec],
        out_shape=[jax.ShapeDtypeStruct((s, w), BF16), jax.ShapeDtypeStruct((s, w), BF16),
                   jax.ShapeDtypeStruct((CONV_W, w), F32), jax.ShapeDtypeStruct((1, w), F32),
                   jax.ShapeDtypeStruct((w, w), F32), jax.ShapeDtypeStruct((1, w), F32),
                   jax.ShapeDtypeStruct((w, w), F32), jax.ShapeDtypeStruct((1, w), F32),
                   jax.ShapeDtypeStruct((1, w), F32)],
        scratch_shapes=[pltpu.VMEM((1, w), F32), pltpu.VMEM((8, w), F32)],
        compiler_params=_params(("arbitrary",)),
    )(proj, proj, proj, hseq, hseq, dyr, cw, cb, wa, ba, wx, bx, lam)


def _sb_logs(z, valid):
    l1p = jnp.log(1.0 + jnp.exp(-jnp.abs(z)))
    lb = jnp.minimum(z, 0.0) - l1p
    lm = jnp.where(valid, -jnp.maximum(z, 0.0) - l1p, 0.0)
    return lb, lm


class _Window:
    def __init__(self):
        blk, win, cut = ATT_BLOCK, ATT_WINDOW, ATT_SPLIT
        self.row = lax.broadcasted_iota(jnp.int32, (blk, win), 0)
        self.col = lax.broadcasted_iota(jnp.int32, (blk, win), 1)

        def tri(n, later):
            j = lax.broadcasted_iota(jnp.int32, (n, n), 0)
            s = lax.broadcasted_iota(jnp.int32, (n, n), 1)
            return jnp.where((j > s) if later else (j < s), 1.0, 0.0).astype(BF16)

        self.later = (tri(cut, True), tri(win - cut, True))
        self.earlier = (tri(cut, False), tri(win - cut, False))

    def place(self, qi, g):
        end = (qi + 1) * ATT_BLOCK - g * ATT_WINDOW
        start = pl.multiple_of(jnp.maximum(end - ATT_WINDOW, 0), ATT_BLOCK)
        valid = start + self.col < jnp.minimum(qi * ATT_BLOCK + self.row, end)
        return start, valid

    @staticmethod
    def _parts(xv):
        hi = xv.astype(BF16)
        lo = (xv - hi.astype(F32)).astype(BF16)
        cut = ATT_SPLIT
        sums = (jnp.sum(xv[:, :cut], axis=1, keepdims=True), jnp.sum(xv[:, cut:], axis=1, keepdims=True))
        return (hi[:, :cut], lo[:, :cut]), (hi[:, cut:], lo[:, cut:]), sums

    def sums_after(self, xv, carry):
        (h0, l0), (h1, l1), (s0, s1) = self._parts(xv)
        first = _dot(h0, self.later[0]) + _dot(l0, self.later[0]) + (s1 + carry)
        last = _dot(h1, self.later[1]) + _dot(l1, self.later[1]) + carry
        return jnp.concatenate([first, last], axis=1), s0 + s1

    def sums_before(self, xv, carry):
        (h0, l0), (h1, l1), (s0, s1) = self._parts(xv)
        first = _dot(h0, self.earlier[0]) + _dot(l0, self.earlier[0]) + carry
        last = _dot(h1, self.earlier[1]) + _dot(l1, self.earlier[1]) + (s0 + carry)
        return jnp.concatenate([first, last], axis=1), s0 + s1


class _HeadPair:
    def __init__(self):
        lanes = 2 * HEAD_DIM
        lane = lax.broadcasted_iota(jnp.int32, (1, lanes), 1)
        self.masks = [lane // HEAD_DIM == h for h in (0, 1)]
        i = lax.broadcasted_iota(jnp.int32, (lanes, lanes), 0) // HEAD_DIM
        j = lax.broadcasted_iota(jnp.int32, (lanes, lanes), 1) // HEAD_DIM
        self.same_head = jnp.where(i == j, 1.0, 0.0).astype(BF16)

    def only(self, h, xv):
        return jnp.where(self.masks[h], xv, jnp.zeros_like(xv))

    def merge(self, per_head):
        return jnp.where(self.masks[0], per_head[0], per_head[1])

    def mean(self, xv):
        hi = xv.astype(BF16)
        lo = (xv - hi.astype(F32)).astype(BF16)
        return (_dot(hi, self.same_head) + _dot(lo, self.same_head)) * (1.0 / HEAD_DIM)

    def rms_r(self, xv):
        return lax.rsqrt(self.mean(xv * xv) + EPS)

    def rms_bwd(self, xv, r, nw, dh):
        t = dh * nw
        dx = r * t - xv * (r * r * r * self.mean(t * xv))
        dn = jnp.sum(dh * xv * r, axis=0, keepdims=True)
        return dx, dn[:, :HEAD_DIM] + dn[:, HEAD_DIM:]


def _attn_fwd(proj, qg, kg, rider=None):
    s = proj.shape[0]
    blk, win, dh = ATT_BLOCK, ATT_WINDOW, HEAD_DIM
    nq = s // blk
    scale = 1.0 / math.sqrt(dh)
    heads = (0, 1)
    assert s >= win and s % blk == 0

    def body(*refs):
        (q_ref, k_ref, v_ref, qg_ref, kg_ref), (o_ref,), (qn, kn, vb), copies = _split_refs(refs, 5, 1, rider)
        finish = _ride(copies, pl.program_id(0) == 0, pl.program_id(0) == N_HEADS // 2 - 1)
        wd, hp = _Window(), _HeadPair()
        qv = q_ref[...]
        qn[...] = (qv * hp.rms_r(qv) * qg_ref[...] * scale).astype(BF16)
        kv = k_ref[...]
        kn[...] = (kv * hp.rms_r(kv) * kg_ref[...]).astype(BF16)
        vb[...] = v_ref[...].astype(BF16)

        def q_step(qi, _):
            qoff = pl.multiple_of(qi * blk, blk)
            qt = qn[pl.ds(qoff, blk), :]
            qts = [hp.only(h, qt) for h in heads]

            def more(carry):
                g, live = carry[:2]
                return jnp.logical_and((qi + 1) * blk - g * win > 0, live > 0)

            def window(carry):
                g, _, accs, runs = carry
                start, valid = wd.place(qi, g)
                kt = kn[pl.ds(start, win), :]
                zs = [_dot_nt(qts[h], kt) for h in heads]
                logs = [_sb_logs(z, valid) for z in zs]
                sums = [wd.sums_after(logs[h][1], runs[h]) for h in heads]
                wgts = [jnp.where(valid, jnp.exp(logs[h][0] + sums[h][0]), 0.0).astype(BF16) for h in heads]
                vt = vb[pl.ds(start, win), :]
                accs = tuple(accs[h] + _dot(wgts[h], vt) for h in heads)
                runs = tuple(runs[h] + sums[h][1] for h in heads)
                live = (jnp.maximum(jnp.max(runs[0]), jnp.max(runs[1])) > EXP_ZERO).astype(jnp.int32)
                return g + 1, live, accs, runs

            zero = lambda cols: tuple(jnp.zeros((blk, cols), F32) for _ in heads)
            _, _, accs, _ = lax.while_loop(more, window, (jnp.int32(0), jnp.int32(1), zero(2 * dh), zero(1)))
            o_ref[pl.ds(qoff, blk), :] = hp.merge(accs)
            return 0

        lax.fori_loop(0, nq, q_step, 0)
        finish()

    pair = lambda group: pl.BlockSpec((s, 2 * dh), lambda p: (0, group * (D_ATT // (2 * dh)) + p))
    vec = pl.BlockSpec((1, 2 * dh), lambda p: (0, 0))
    return _call(
        body, "attn_fwd", (N_HEADS // 2,), [pair(2), pair(3), pair(4), vec, vec], [pair(0)],
        [jax.ShapeDtypeStruct((s, D_ATT), F32)], [proj, proj, proj, jnp.tile(qg, (1, 2)), jnp.tile(kg, (1, 2))],
        scratch=[pltpu.VMEM((s, 2 * dh), BF16)] * 3, rider=rider)


def _attn_bwd(proj, dya, qg, kg, rider=None):
    s = proj.shape[0]
    blk, win, dh = ATT_BLOCK, ATT_WINDOW, HEAD_DIM
    nq = s // blk
    max_windows = -(-s // win) + 1
    scale = 1.0 / math.sqrt(dh)
    steps = N_HEADS // 2
    heads = (0, 1)
    assert s >= win and s % blk == 0

    def body(*refs):
        ins, outs, scratch, copies = _split_refs(refs, 6, 5, rider)
        q_ref, k_ref, v_ref, do_ref, qg_ref, kg_ref = ins
        dq_ref, dk_ref, dv_ref, dqg_ref, dkg_ref = outs
        qn, kn, vb, dob, runs_ref, dqn, dkn, dvn = scratch
        finish = _ride(copies, pl.program_id(0) == 0, pl.program_id(0) == steps - 1)
        wd, hp = _Window(), _HeadPair()

        @pl.when(pl.program_id(0) == 0)
        def _():
            dqg_ref[...] = jnp.zeros_like(dqg_ref)
            dkg_ref[...] = jnp.zeros_like(dkg_ref)

        qv = q_ref[...]
        qn[...] = (qv * hp.rms_r(qv) * qg_ref[...] * scale).astype(BF16)
        kv = k_ref[...]
        kn[...] = (kv * hp.rms_r(kv) * kg_ref[...]).astype(BF16)
        vb[...] = v_ref[...].astype(BF16)
        dob[...] = do_ref[...].astype(BF16)
        dkn[...] = jnp.zeros_like(dkn)
        dvn[...] = jnp.zeros_like(dvn)

        def q_step(qi, _):
            qoff = pl.multiple_of(qi * blk, blk)
            qt = qn[pl.ds(qoff, blk), :]
            dot = dob[pl.ds(qoff, blk), :]
            qts = [hp.only(h, qt) for h in heads]
            dots = [hp.only(h, dot) for h in heads]

            zero = lambda cols: tuple(jnp.zeros((blk, cols), F32) for _ in heads)

            def logs_of(g):
                start, valid = wd.place(qi, g)
                kt = kn[pl.ds(start, win), :]
                return [_sb_logs(_dot_nt(qts[h], kt), valid) for h in heads]

            def row_sums(logs):
                return tuple(jnp.sum(logs[h][1], axis=1, keepdims=True) for h in heads)

            def still_live(runs):
                return jnp.maximum(jnp.max(runs[0]), jnp.max(runs[1])) > EXP_ZERO

            def window_grads(g, logs, runs, esums):
                start, valid = wd.place(qi, g)
                kt = kn[pl.ds(start, win), :]
                vt = vb[pl.ds(start, win), :]
                dws = [_dot_nt(dots[h], vt) for h in heads]
                tails = [wd.sums_after(logs[h][1], runs[h])[0] for h in heads]
                wgts = [jnp.where(valid, jnp.exp(logs[h][0] + tails[h]), 0.0) for h in heads]
                es = [dws[h] * wgts[h] for h in heads]
                befores = [wd.sums_before(es[h], esums[h]) for h in heads]
                dzbs = []
                for h in heads:
                    beta = jnp.exp(logs[h][0])
                    dz = jnp.where(valid, es[h] * (1.0 - beta) - befores[h][0] * beta, 0.0)
                    dzbs.append(dz.astype(BF16))
                dkn[pl.ds(start, win), :] += _dot_tn(dzbs[0], qts[0]) + _dot_tn(dzbs[1], qts[1])
                dvn[pl.ds(start, win), :] += (_dot_tn(wgts[0].astype(BF16), dots[0])
                                              + _dot_tn(wgts[1].astype(BF16), dots[1]))
                return tuple(_dot(dzbs[h], kt) for h in heads), tuple(befores[h][1] for h in heads)

            logs0 = logs_of(0)
            runs1 = row_sums(logs0)

            def one_window():
                return window_grads(0, logs0, zero(1), zero(1))[0]

            def all_windows():
                def more(carry):
                    g, live = carry[:2]
                    return jnp.logical_and((qi + 1) * blk - g * win > 0, live > 0)

                def run_window(carry):
                    g, _, runs = carry
                    for h in heads:
                        runs_ref[h, g] = runs[h]
                    sums = row_sums(logs_of(g))
                    runs = tuple(runs[h] + sums[h] for h in heads)
                    return g + 1, still_live(runs).astype(jnp.int32), runs

                for h in heads:
                    runs_ref[h, 0] = jnp.zeros((blk, 1), F32)
                windows, _, _ = lax.while_loop(more, run_window, (jnp.int32(1), jnp.int32(1), runs1))

                def k_window(gg, carry):
                    dq_accs, esums = carry
                    g = windows - 1 - gg
                    parts, totals = window_grads(g, logs_of(g), [runs_ref[h, g] for h in heads], esums)
                    return (tuple(dq_accs[h] + parts[h] for h in heads),
                            tuple(esums[h] + totals[h] for h in heads))

                return lax.fori_loop(0, windows, k_window, (zero(2 * dh), zero(1)))[0]

            earlier_keys = (qi + 1) * blk - win > 0
            dq_accs = lax.cond(jnp.logical_and(earlier_keys, still_live(runs1)), all_windows, one_window)
            dqn[pl.ds(qoff, blk), :] = hp.merge(dq_accs)
            return 0

        lax.fori_loop(0, nq, q_step, 0)

        dq, dqg = hp.rms_bwd(qv, hp.rms_r(qv), qg_ref[...] * scale, dqn[...])
        dq_ref[...] = dq.astype(BF16)
        dqg_ref[...] += dqg * scale
        dk, dkg = hp.rms_bwd(kv, hp.rms_r(kv), kg_ref[...], dkn[...])
        dk_ref[...] = dk.astype(BF16)
        dkg_ref[...] += dkg
        dv_ref[...] = dvn[...].astype(BF16)
        finish()

    pair = lambda group: pl.BlockSpec((s, 2 * dh), lambda p: (0, group * (D_ATT // (2 * dh)) + p))
    vec2 = pl.BlockSpec((1, 2 * dh), lambda p: (0, 0))
    vec = pl.BlockSpec((1, dh), lambda p: (0, 0))
    return _call(
        body, "attn_bwd", (steps,), [pair(2), pair(3), pair(4), pair(0), vec2, vec2],
        [pair(0), pair(0), pair(0), vec, vec],
        [jax.ShapeDtypeStruct((s, D_ATT), BF16)] * 3 + [jax.ShapeDtypeStruct((1, dh), F32)] * 2,
        [proj, proj, proj, dya, jnp.tile(qg, (1, 2)), jnp.tile(kg, (1, 2))],
        scratch=[pltpu.VMEM((s, 2 * dh), BF16)] * 4 + [pltpu.VMEM((2, max_windows, blk, 1), F32)]
        + [pltpu.VMEM((s, 2 * dh), F32)] * 3, rider=rider)


def _block_diag(w):
    n, c, d = w.shape
    return jnp.einsum("ncd,nm->ncmd", w, jnp.eye(n, dtype=w.dtype)).reshape(n * c, n * d)


def _diag_blocks(full, n):
    c = full.shape[0] // n
    return jnp.stack([full[i * c:(i + 1) * c, i * c:(i + 1) * c] for i in range(n)])


FFN1 = ["ffn1_w_gate", "ffn1_w_up", "ffn1_w_down"]
FFN2 = ["ffn2_w_gate", "ffn2_w_up", "ffn2_w_down"]
MIXER = ["w_in", "w_out"]


def _pair_sums(gb, names, where):
    theirs = _pair_exchange([gb[n] for n in names], "pair_exchange_" + names[0])
    pair, own = _pair_sum([gb[n] for n in names], theirs, where, "pair_sum_" + names[0])
    return _chip_rider(pair, own)


def _local_step(x, tgt, stacks, conv_stack, small, where):
    big = dict(zip(FFN1, _gather_weights([stacks[n] for n in FFN1], [])))
    wa = _block_diag(small["rg_w_a"]).astype(BF16)
    wx = _block_diag(small["rg_w_x"]).astype(BF16)

    x1, g1, u1, hb1, ab1, *landed = _ffn_fwd(x, small["ffn1_norm"], *[big[n] for n in FFN1],
                                             rider=_gather_rider([stacks[n] for n in MIXER], [conv_stack]))
    big.update(zip(MIXER, _forward_weights(landed[:len(MIXER)], "forward_mixer_weights")))
    conv_w = jnp.transpose(landed[-1], (1, 0, 2)).reshape(CONV_W, D_RNN)
    wout = big["w_out"].reshape(D_MODEL, D_MODEL)
    rg = (conv_w, small["conv_b"], wa, small["rg_b_a"], wx, small["rg_b_x"], small["rg_lambda"])
    proj, hb2 = _mix_pre(x1, small["mix_norm"], big["w_in"])
    yr, hseq = _rglru_fwd(proj, *rg)
    ya, *landed = _attn_fwd(proj, small["q_norm"], small["k_norm"], _gather_rider([stacks[n] for n in FFN2], []))
    big.update(zip(FFN2, _forward_weights(landed, "forward_ffn2_weights")))
    x2 = _mix_post(x1, yr, ya, small["rnn_out_norm"], small["attn_out_norm"], wout)
    dx3, g2, u2, hb3, ab3, loss = _ffn_fwd(x2, small["ffn2_norm"], *[big[n] for n in FFN2], tgt)

    gb, gs, slots = {}, {}, {}
    dx2, dg2, du2, dyb2, gs["ffn2_norm"] = _ffn_bwd_act(x2, small["ffn2_norm"], dx3, g2, u2, *[big[n] for n in FFN2],
                                                        "ffn2_bwd")
    gb["ffn2_w_gate"] = _ffn_wgrad(dg2, hb3, 1.0, "wgrad_gate_ffn2")
    gb["ffn2_w_up"] = _ffn_wgrad(du2, hb3, 1.0, "wgrad_up_ffn2")
    gb["ffn2_w_down"] = _ffn_wgrad(ab3, dyb2, 0.5, "wgrad_down_ffn2")
    dyr, dya, ycat, dxb2, gs["rnn_out_norm"], gs["attn_out_norm"] = _mix_post_bwd(
        dx2, yr, ya, small["rnn_out_norm"], small["attn_out_norm"], wout)
    gb["w_out"] = _wgrad_whole(ycat, dxb2, False, "wgrad_out")
    early = FFN2 + ["w_out"]
    dq, dk, dv, gs["q_norm"], gs["k_norm"], *done = _attn_bwd(
        proj, dya, small["q_norm"], small["k_norm"], _pair_sums(gb, early, where))
    slots.update(zip(early, done))
    dxr, dgate, gs["conv_w"], gs["conv_b"], dwa, gs["rg_b_a"], dwx, gs["rg_b_x"], gs["rg_lambda"] = _rglru_bwd(
        proj, hseq, dyr, *rg)
    gs["rg_w_a"] = _diag_blocks(dwa, RNN_BLOCKS)
    gs["rg_w_x"] = _diag_blocks(dwx, RNN_BLOCKS)
    dpb = jnp.concatenate([dxr, dgate, dq, dk, dv], axis=1)
    dx1, gs["mix_norm"] = _mix_pre_bwd(x1, small["mix_norm"], dx2, dpb, big["w_in"])
    dx0, dg1, du1, dyb1, gs["ffn1_norm"] = _ffn_bwd_act(x, small["ffn1_norm"], dx1, g1, u1, *[big[n] for n in FFN1],
                                                        "ffn1_bwd")

    mine = _place_shard(_pack([gs[n] for n in SMALL] + [loss[:, :1]]), where, F32, "place_small_grads",
                        by_device=True)
    gb["ffn1_w_gate"], everyone = _ffn_wgrad(dg1, hb1, 1.0, "wgrad_gate_ffn1", _small_rider(mine))
    gb["ffn1_w_up"], slots["ffn1_w_gate"] = _ffn_wgrad(
        du1, hb1, 1.0, "wgrad_up_ffn1", _pair_sums(gb, ["ffn1_w_gate"], where))
    gb["ffn1_w_down"], slots["ffn1_w_up"] = _ffn_wgrad(
        ab1, dyb1, 0.5, "wgrad_down_ffn1", _pair_sums(gb, ["ffn1_w_up"], where))
    gb["w_in"], slots["ffn1_w_down"] = _wgrad_whole(
        hb2, dpb, True, "wgrad_in", _pair_sums(gb, ["ffn1_w_down"], where))
    last = _pair_sums(gb, ["w_in"], where)
    slots["w_in"], = _chip_exchange(last.plain, last.inplace)
    return dx0, slots, gs, everyone


ANY = pl.BlockSpec(memory_space=pl.ANY)


def _place():
    x, y, c = lax.axis_index("x"), lax.axis_index("y"), lax.axis_index("c")
    other_chips = [(1 - x, y), (x, 1 - y), (1 - x, 1 - y)]
    return x, y, c, 2 * x + y, other_chips


def _remote(src, dst, send_sem, recv_sem, to):
    return pltpu.make_async_remote_copy(src_ref=src, dst_ref=dst, send_sem=send_sem, recv_sem=recv_sem,
                                        device_id=to, device_id_type=MESH)


def _copy_plan(pairs):
    sends = [functools.partial(_remote, *a) for a, _ in pairs]
    arrivals = [functools.partial(_remote, *b) for _, b in pairs]
    return sends, arrivals


class _Rider:
    def __init__(self, plan, plain, inplace, n_copies=None):
        self.plan, self.plain, self.inplace = plan, list(plain), list(inplace)
        self.n_copies = n_copies or 3 * len(self.inplace)

    def operands(self):
        return self.plain + self.inplace

    def out_shape(self):
        return [jax.ShapeDtypeStruct(a.shape, a.dtype) for a in self.inplace]

    def aliases(self, inputs_before, outputs_before):
        return {inputs_before + len(self.plain) + k: outputs_before + k for k in range(len(self.inplace))}

    def scratch(self):
        return [pltpu.SemaphoreType.DMA((self.n_copies,))] * 2


def _split_refs(refs, n_in, n_out, rider):
    if rider is None:
        return refs[:n_in], refs[n_in:n_in + n_out], refs[n_in + n_out:], None
    r_in, r_out = len(rider.operands()), len(rider.inplace)
    outs_at = n_in + r_in
    rest = refs[outs_at + n_out + r_out:]
    copies = functools.partial(rider.plan, refs[n_in:n_in + len(rider.plain)],
                               refs[outs_at + n_out:outs_at + n_out + r_out], *rest[-2:])
    return refs[:n_in], refs[outs_at:outs_at + n_out], rest[:-2], copies


def _ride(copies, first, last):
    if copies is None:
        return lambda: None

    @pl.when(first)
    def _():
        _start(copies()[0])

    def finish():
        @pl.when(last)
        def _():
            _finish(*copies())

    return finish


def _gather_rider(split, whole):
    n_split = len(split)
    return _Rider(lambda plain, stacks, ss, rs: _gather_ici(stacks, n_split, ss, rs), [], list(split) + list(whole))


def _chip_rider(sums, slots):
    return _Rider(_chip_copies, sums, slots)


def _start(makers):
    for make in makers:
        make().start()


def _finish(sends, arrivals):
    for make in arrivals:
        make().wait_recv()
    for make in sends:
        make().wait_send()


def _half(rows, c):
    return pl.ds(pl.multiple_of(c * rows, 16), rows)


def _gather_weights(split, whole):
    arrs = list(split) + list(whole)
    n, ns = len(arrs), len(split)

    def body(*refs):
        outs = refs[n:2 * n]
        send_sems, recv_sems, fsend_sems, frecv_sems = refs[2 * n:]
        sends, arrivals = _gather_ici(outs, ns, send_sems, recv_sems)
        passes, passed = _gather_d2d(outs[:ns], fsend_sems, frecv_sems)
        _start(sends)
        for k, make in enumerate(arrivals):
            make().wait_recv()
            if k < 3 * ns:
                passes[k]().start()
        _finish(sends + passes, passed)

    return pl.pallas_call(
        body, name="gather_weights",
        in_specs=[ANY] * n, out_specs=[ANY] * n,
        out_shape=[jax.ShapeDtypeStruct(a.shape, a.dtype) for a in arrs],
        input_output_aliases={i: i for i in range(n)},
        scratch_shapes=[pltpu.SemaphoreType.DMA((3 * n,)), pltpu.SemaphoreType.DMA((3 * n,)),
                        pltpu.SemaphoreType.DMA((3 * ns,)), pltpu.SemaphoreType.DMA((3 * ns,))],
    )(*arrs)


def _gather_ici(stacks, n_split, send_sems, recv_sems):
    x, y, c, me, chips = _place()

    def region(i, chip):
        if i < n_split:
            return stacks[i].at[chip, _half(stacks[i].shape[1] // 2, c)]
        return stacks[i].at[chip]

    pairs = []
    for i in range(len(stacks)):
        for p, (cx, cy) in enumerate(chips):
            k = 3 * i + p
            mine, got = region(i, me), region(i, 2 * cx + cy)
            sems, to = (send_sems.at[k], recv_sems.at[k]), (cx, cy, c)
            pairs.append(((mine, mine, *sems, to), (got, got, *sems, to)))
    return _copy_plan(pairs)


def _gather_d2d(stacks, send_sems, recv_sems):
    x, y, c, _, chips = _place()
    sibling = (x, y, 1 - c)
    pairs = []
    for i, stack in enumerate(stacks):
        rows = stack.shape[1] // 2
        for p, (cx, cy) in enumerate(chips):
            k = 3 * i + p
            got, theirs = stack.at[2 * cx + cy, _half(rows, c)], stack.at[2 * cx + cy, _half(rows, 1 - c)]
            sems = (send_sems.at[k], recv_sems.at[k])
            pairs.append(((got, got, *sems, sibling), (theirs, theirs, *sems, sibling)))
    return _copy_plan(pairs)


def _forward_weights(split, name):
    n = len(split)

    def body(*refs):
        sends, arrivals = _gather_d2d(refs[n:2 * n], *refs[2 * n:])
        _start(sends)
        _finish(sends, arrivals)

    return pl.pallas_call(
        body, name=name,
        in_specs=[ANY] * n, out_specs=[ANY] * n,
        out_shape=[jax.ShapeDtypeStruct(a.shape, a.dtype) for a in split],
        input_output_aliases={i: i for i in range(n)},
        scratch_shapes=[pltpu.SemaphoreType.DMA((3 * n,))] * 2,
    )(*split)


def _pair_exchange(grads, name):
    n = len(grads)

    def body(*refs):
        ins, theirs = refs[:n], refs[n:2 * n]
        send_sems, recv_sems = refs[2 * n:]
        x, y, c, _, _ = _place()
        sibling = (x, y, 1 - c)
        sends = [_remote(ins[k].at[:, _half(grads[k].shape[1] // 2, 1 - c)], theirs[k],
                         send_sems.at[k], recv_sems.at[k], sibling) for k in range(n)]
        for cp in sends:
            cp.start()
        for k in range(n):
            _remote(theirs[k], theirs[k], send_sems.at[k], recv_sems.at[k], sibling).wait_recv()
        for cp in sends:
            cp.wait_send()

    return pl.pallas_call(
        body, name=name,
        in_specs=[ANY] * n, out_specs=[ANY] * n,
        out_shape=[jax.ShapeDtypeStruct((g.shape[0], g.shape[1] // 2, g.shape[2]), g.dtype) for g in grads],
        scratch_shapes=[pltpu.SemaphoreType.DMA((n,))] * 2,
    )(*grads)


def _chip_exchange(sums, slots):
    n = len(sums)

    def body(*refs):
        sends, arrivals = _chip_copies(refs[:n], refs[2 * n:3 * n], *refs[3 * n:])
        _start(sends)
        _finish(sends, arrivals)

    return pl.pallas_call(
        body, name="grad_chip_exchange",
        in_specs=[ANY] * (2 * n), out_specs=[ANY] * n,
        out_shape=[jax.ShapeDtypeStruct(a.shape, a.dtype) for a in slots],
        input_output_aliases={n + k: k for k in range(n)},
        scratch_shapes=[pltpu.SemaphoreType.DMA((3 * n,)), pltpu.SemaphoreType.DMA((3 * n,))],
    )(*sums, *slots)


def _chip_copies(sums, slots, send_sems, recv_sems):
    x, y, c, me, chips = _place()
    pairs = []
    for k in range(len(sums)):
        for p, (cx, cy) in enumerate(chips):
            j = 3 * k + p
            got = slots[k].at[2 * cx + cy]
            sems, to = (send_sems.at[j], recv_sems.at[j]), (cx, cy, c)
            pairs.append(((sums[k].at[2 * cx + cy], slots[k].at[me], *sems, to), (got, got, *sems, to)))
    return _copy_plan(pairs)


def _half_swap(halves):
    n = len(halves)

    def body(*refs):
        outs = refs[n:2 * n]
        send_sems, recv_sems = refs[2 * n:]
        x, y, c, _, _ = _place()
        sibling = (x, y, 1 - c)
        sends = [_remote(outs[k].at[c], outs[k].at[c], send_sems.at[k], recv_sems.at[k], sibling) for k in range(n)]
        for cp in sends:
            cp.start()
        for k in range(n):
            got = outs[k].at[1 - c]
            _remote(got, got, send_sems.at[k], recv_sems.at[k], sibling).wait_recv()
        for cp in sends:
            cp.wait_send()

    return pl.pallas_call(
        body, name="grad_half_swap",
        in_specs=[ANY] * n, out_specs=[ANY] * n,
        out_shape=[jax.ShapeDtypeStruct(a.shape, a.dtype) for a in halves],
        input_output_aliases={k: k for k in range(n)},
        scratch_shapes=[pltpu.SemaphoreType.DMA((n,))] * 2,
    )(*halves)


def _small_rider(stack):
    n_dev = 2 * N_CHIPS

    def plan(_, stacks, send_sems, recv_sems):
        x, y, c, _, _ = _place()
        mine = stacks[0].at[4 * x + 2 * y + c]
        pairs = []
        for k in range(1, n_dev):
            px, py, pc = x ^ ((k >> 2) & 1), y ^ ((k >> 1) & 1), c ^ (k & 1)
            got = stacks[0].at[4 * px + 2 * py + pc]
            sems = (send_sems.at[k - 1], recv_sems.at[k - 1])
            pairs.append(((mine, mine, *sems, (px, py, pc)), (got, got, *sems, (px, py, pc))))
        return _copy_plan(pairs)

    return _Rider(plan, [], [stack], n_dev - 1)


def _row_tile(r):
    return r // 4 if r >= 256 and (r // 4) % 16 == 0 else r


def _prefetch_call(body, name, grid, in_specs, out_specs, out_shape):
    spec = pltpu.PrefetchScalarGridSpec(num_scalar_prefetch=1, grid=grid, in_specs=in_specs, out_specs=out_specs)
    return pl.pallas_call(body, name=name, grid_spec=spec, out_shape=out_shape,
                          compiler_params=_params(("arbitrary",) * len(grid)))


def _place_shard(w2d, where, dtype, name, by_device=False):
    r, c = w2d.shape
    tr = _row_tile(r)
    slots = 2 * N_CHIPS if by_device else N_CHIPS
    slot = (lambda s: 2 * s[1] + s[0]) if by_device else (lambda s: s[1])

    def body(where_ref, w_ref, out_ref):
        out_ref[...] = w_ref[...].astype(dtype)

    return _prefetch_call(
        body, name, (r // tr,), [pl.BlockSpec((tr, c), lambda i, s: (i, 0))],
        pl.BlockSpec((None, tr, c), lambda i, s: (slot(s), i, 0)),
        jax.ShapeDtypeStruct((slots, r, c), dtype))(where, w2d)


def _place_shards(w2ds, where, name):
    n = len(w2ds)
    steps = N_CHIPS
    assert all(w.shape[0] % (16 * steps) == 0 for w in w2ds)

    def body(where_ref, *refs):
        for k in range(n):
            refs[n + k][...] = refs[k][...].astype(BF16)

    tile = lambda w: (w.shape[0] // steps, w.shape[1])
    return _prefetch_call(
        body, name, (steps,), [pl.BlockSpec(tile(w), lambda i, s: (i, 0)) for w in w2ds],
        [pl.BlockSpec((None,) + tile(w), lambda i, s: (s[1], i, 0)) for w in w2ds],
        [jax.ShapeDtypeStruct((N_CHIPS,) + w.shape, BF16) for w in w2ds])(where, *w2ds)


def _pair_sum(fulls, theirs, where, name):
    n = len(fulls)

    def body(where_ref, *refs):
        for k in range(n):
            a_ref, b_ref, out_ref, own_ref = refs[k], refs[n + k], refs[2 * n + k], refs[3 * n + k]
            total = (a_ref[...].astype(F32) + b_ref[...].astype(F32)).astype(BF16)
            out_ref[...] = total

            @pl.when(pl.program_id(0) == where_ref[1])
            def _():
                own_ref[...] = total

    half = lambda t: pl.BlockSpec((None,) + t.shape[1:], lambda j, s: (j, s[0], 0))
    blk = lambda t: pl.BlockSpec((None,) + t.shape[1:], lambda j, s: (j, 0, 0))
    own = lambda t: pl.BlockSpec((None,) + t.shape[1:], lambda j, s: (s[1], 0, 0))
    shapes = [jax.ShapeDtypeStruct(t.shape, BF16) for t in theirs]
    outs = _prefetch_call(
        body, name, (N_CHIPS,), [half(t) for t in theirs] + [blk(t) for t in theirs],
        [blk(t) for t in theirs] + [own(t) for t in theirs], shapes + shapes)(where, *fulls, *theirs)
    return outs[:n], outs[n:]


def _chip_sum(slots, where, name):
    n = len(slots)
    steps = 2
    assert all(a.shape[1] % (16 * steps) == 0 for a in slots)

    def body(where_ref, *refs):
        for k in range(n):
            a_ref, out_ref = refs[k], refs[n + k]
            total = a_ref[0].astype(F32)
            for j in range(1, a_ref.shape[0]):
                total = total + a_ref[j].astype(F32)
            out_ref[...] = total

    tile = lambda a: (a.shape[1] // steps, a.shape[2])
    return _prefetch_call(
        body, name, (steps,), [pl.BlockSpec((a.shape[0],) + tile(a), lambda i, s: (0, i, 0)) for a in slots],
        [pl.BlockSpec((None,) + tile(a), lambda i, s: (s[0], i, 0)) for a in slots],
        [jax.ShapeDtypeStruct((2,) + a.shape[1:], F32) for a in slots])(where, *slots)


def _slot_sum(a, name):
    nb, r, c = a.shape
    tr = _row_tile(r)

    def body(a_ref, out_ref):
        total = a_ref[0].astype(F32)
        for j in range(1, nb):
            total = total + a_ref[j].astype(F32)
        out_ref[...] = total

    return pl.pallas_call(
        body, name=name, grid=(r // tr,),
        in_specs=[pl.BlockSpec((nb, tr, c), lambda i: (0, i, 0))],
        out_specs=pl.BlockSpec((tr, c), lambda i: (i, 0)),
        out_shape=jax.ShapeDtypeStruct((r, c), F32), compiler_params=_params(("arbitrary",)),
    )(a)


def _adamw(ws, gs, ms, vs, name, steps=1):
    n = len(ws)
    c1 = 1.0 - ADAM_B1 ** ADAM_STEP
    c2 = 1.0 - ADAM_B2 ** ADAM_STEP
    assert all(w.shape[0] % steps == 0 and (steps == 1 or w.shape[0] // steps % 8 == 0) for w in ws)

    def body(*refs):
        for k in range(n):
            w_ref, g_ref, m_ref, v_ref = (refs[j * n + k] for j in range(4))
            d_ref, m2_ref, v2_ref = (refs[(4 + j) * n + k] for j in range(3))
            gv = g_ref[...]
            m2 = ADAM_B1 * m_ref[...] + (1.0 - ADAM_B1) * gv
            v2 = ADAM_B2 * v_ref[...] + (1.0 - ADAM_B2) * (gv * gv)
            m2_ref[...] = m2
            v2_ref[...] = v2
            d_ref[...] = -ADAM_LR * ((m2 / c1) / (jnp.sqrt(v2 / c2) + ADAM_EPS) + ADAM_WD * w_ref[...])

    blks = [pl.BlockSpec((w.shape[0] // steps, w.shape[1]), lambda i: (i, 0)) for w in ws]
    shapes = [jax.ShapeDtypeStruct(w.shape, F32) for w in ws]
    outs = pl.pallas_call(
        body, name=name, grid=(steps,), in_specs=blks * 4, out_specs=blks * 3, out_shape=shapes * 3,
        compiler_params=_params(("arbitrary",)),
    )(*ws, *gs, *ms, *vs)
    return outs[:n], outs[n:2 * n], outs[2 * n:]


WEIGHTS = ["ffn1_norm", "ffn1_w_gate", "ffn1_w_up", "ffn1_w_down", "mix_norm", "w_in", "conv_w", "conv_b",
           "rg_w_a", "rg_b_a", "rg_w_x", "rg_b_x", "rg_lambda", "q_norm", "k_norm", "rnn_out_norm",
           "attn_out_norm", "w_out", "ffn2_norm", "ffn2_w_gate", "ffn2_w_up", "ffn2_w_down"]
BIG = ["ffn1_w_gate", "ffn1_w_up", "ffn1_w_down", "w_in", "w_out", "ffn2_w_gate", "ffn2_w_up", "ffn2_w_down"]
SMALL = [n for n in WEIGHTS if n not in BIG]
PACK_LANES = 128
PACK_ROW_ALIGN = 8


def _hidden_major(name, a):
    return jnp.transpose(a) if name.endswith(("w_gate", "w_up")) else a


def _pack(parts):
    flat = jnp.concatenate([p.reshape(-1) for p in parts])
    unit = PACK_LANES * PACK_ROW_ALIGN
    padded = -(-flat.shape[0] // unit) * unit
    return jnp.pad(flat, (0, padded - flat.shape[0])).reshape(-1, PACK_LANES)


def _unpack(packed, shapes):
    flat = packed.reshape(-1)
    out, at = [], 0
    for shp in shapes:
        size = math.prod(shp)
        out.append(flat[at:at + size].reshape(shp))
        at += size
    return out


def kernel(x, ffn1_norm, ffn1_w_gate, ffn1_w_up, ffn1_w_down, mix_norm, w_in, conv_w, conv_b, rg_w_a, rg_b_a, rg_w_x, rg_b_x, rg_lambda, q_norm, k_norm, rnn_out_norm, attn_out_norm, w_out, ffn2_norm, ffn2_w_gate, ffn2_w_up, ffn2_w_down, loss_target, m_ffn1_norm, m_ffn1_w_gate, m_ffn1_w_up, m_ffn1_w_down, m_mix_norm, m_w_in, m_conv_w, m_conv_b, m_rg_w_a, m_rg_b_a, m_rg_w_x, m_rg_b_x, m_rg_lambda, m_q_norm, m_k_norm, m_rnn_out_norm, m_attn_out_norm, m_w_out, m_ffn2_norm, m_ffn2_w_gate, m_ffn2_w_up, m_ffn2_w_down, v_ffn1_norm, v_ffn1_w_gate, v_ffn1_w_up, v_ffn1_w_down, v_mix_norm, v_w_in, v_conv_w, v_conv_b, v_rg_w_a, v_rg_b_a, v_rg_w_x, v_rg_b_x, v_rg_lambda, v_q_norm, v_k_norm, v_rnn_out_norm, v_attn_out_norm, v_w_out, v_ffn2_norm, v_ffn2_w_gate, v_ffn2_w_up, v_ffn2_w_down):
    given = dict(locals())
    w = {n: given[n] for n in WEIGHTS}
    m = {n: given["m_" + n] for n in WEIGHTS}
    v = {n: given["v_" + n] for n in WEIGHTS}
    chip = 2 * lax.axis_index("x") + lax.axis_index("y")

    where = jnp.stack([lax.axis_index("c"), chip]).astype(jnp.int32)

    stacks = dict(zip(BIG, _place_shards([_hidden_major(n, w[n][0]) for n in BIG], where, "place_weights")))
    conv_stack = _place_shard(w["conv_w"][0], where, F32, "place_conv_w")
    small = {n: (w[n][0] if w[n].ndim > 2 else w[n]) for n in SMALL if n != "conv_w"}

    grad_x, slots, gs, everyone = _local_step(x[0], loss_target[0], stacks, conv_stack, small, where)

    swapped = _half_swap(_chip_sum([slots[n] for n in BIG], where, "chip_sums"))
    g2s = [t.reshape(t.shape[0] * t.shape[1], t.shape[2]) for t in swapped]
    flat = lambda tree: [_hidden_major(n, tree[n][0]) for n in BIG]
    d2s, m2s, v2s = _adamw(flat(w), g2s, flat(m), flat(v), "adamw_weights", ADAMW_STEPS)
    grads, deltas, new_m, new_v = {}, {}, {}, {}
    for tree, parts in ((grads, g2s), (deltas, d2s), (new_m, m2s), (new_v, v2s)):
        tree.update({n: _hidden_major(n, a).reshape(w[n].shape) for n, a in zip(BIG, parts)})

    full_shapes = [gs[n].shape for n in SMALL]
    *summed, loss = _unpack(_slot_sum(everyone, "small_grad_sum"), full_shapes + [(1, 1)])
    g_parts = dict(zip(SMALL, summed))
    quarter = D_RNN // N_CHIPS
    g_parts["conv_w"] = lax.dynamic_slice_in_dim(g_parts["conv_w"], chip * quarter, quarter, axis=1)
    local_shapes = [w[n].shape for n in SMALL]
    pk = lambda tree: _pack([tree[n] for n in SMALL])
    (d_s,), (m_s,), (v_s,) = _adamw([pk(w)], [pk(g_parts)], [pk(m)], [pk(v)], "adamw_small")
    for tree, packed in ((grads, pk(g_parts)), (deltas, d_s), (new_m, m_s), (new_v, v_s)):
        tree.update(zip(SMALL, _unpack(packed, local_shapes)))

    return (loss[0, 0], grad_x.reshape(x.shape), *[grads[n] for n in WEIGHTS], *[deltas[n] for n in WEIGHTS],
            *[new_m[n] for n in WEIGHTS], *[new_v[n] for n in WEIGHTS])
```

```python
import functools
import math

import jax
import jax.numpy as jnp
from jax import lax
from jax.experimental import pallas as pl
from jax.experimental.pallas import tpu as pltpu

F32 = jnp.float32
BF16 = jnp.bfloat16
MESH = pl.DeviceIdType.MESH

D_MODEL = 1024
N_CHIPS = 4
D_RNN = 512
D_ATT = 512
N_HEADS = 8
HEAD_DIM = 64
RNN_BLOCKS = 8
CONV_W = 4
RG_C = 8.0
N_IN = 2 * D_RNN + 3 * D_ATT
EPS = 1e-6
ATT_BLOCK = 128
ATT_WINDOW = 384
ATT_SPLIT = 256
EXP_ZERO = -105.0

ADAM_LR = 0.001
ADAM_B1 = 0.9
ADAM_B2 = 0.999
ADAM_EPS = 1e-08
ADAM_WD = 0.01
ADAM_STEP = 10

V7X_VMEM_LIMIT = 56 * 1024 * 1024
V7X_MXU_WIDTH = 256
TOKEN_TILE = 512
SUBLANES = 8
FFN_TILE = 256
WGRAD_TILE = 2048
WHOLE_TILE = 1024
ADAMW_STEPS = 8

GELU_K0 = math.sqrt(2.0 / math.pi)
GELU_K1 = 0.044715


def _params(sem=None):
    return pltpu.CompilerParams(dimension_semantics=sem, vmem_limit_bytes=V7X_VMEM_LIMIT)


def _dot(a, b):
    return jnp.dot(a, b, preferred_element_type=F32)


def _dot_nt(a, b):
    return lax.dot_general(a, b, (((1,), (1,)), ((), ())), preferred_element_type=F32)


def _dot_tn(a, b):
    return lax.dot_general(a, b, (((0,), (0,)), ((), ())), preferred_element_type=F32)


def _sigmoid(x):
    return 1.0 / (1.0 + jnp.exp(-x))


def _rms_r(xv):
    return lax.rsqrt(jnp.mean(xv * xv, axis=-1, keepdims=True) + EPS)


def _rms_bwd(xv, r, nw, dh):
    t = dh * nw
    dx = r * t - xv * (r * r * r * jnp.mean(t * xv, axis=-1, keepdims=True))
    dn = jnp.sum(dh * xv * r, axis=0, keepdims=True)
    return dx, dn


def _gelu(x):
    t = jnp.tanh(GELU_K0 * (x + GELU_K1 * x * x * x))
    return 0.5 * x * (1.0 + t)


def _gelu_grad(x):
    t = jnp.tanh(GELU_K0 * (x + GELU_K1 * x * x * x))
    return 0.5 * (1.0 + t) + 0.5 * x * (1.0 - t * t) * (GELU_K0 * (1.0 + 3.0 * GELU_K1 * x * x))


def _expm1_neg(x):
    p = 1.0 + x * (1.0 / 6.0)
    for k in (5.0, 4.0, 3.0, 2.0):
        p = 1.0 + x * (1.0 / k) * p
    return jnp.where(x > -0.25, x * p, jnp.exp(x) - 1.0)


def _log_sigmoid(x):
    return jnp.minimum(x, 0.0) - jnp.log(1.0 + jnp.exp(-jnp.abs(x)))


def _tile(s):
    return min(TOKEN_TILE, s)


def _ffn_chunks(f):
    cut = f // 2 // V7X_MXU_WIDTH * V7X_MXU_WIDTH
    return ((0, cut), (cut, f)) if 0 < cut < f else ((0, f),)


def _ffn_fwd(x, nw, wg, wu, wd, tgt=None, rider=None):
    s, d = x.shape
    f = wg.shape[0]
    tm = min(FFN_TILE, s)
    ni = s // tm
    assert s % tm == 0
    with_loss = tgt is not None
    n_in, n_out = 5 + with_loss, 5 + with_loss

    def body(*refs):
        ins, outs, _, copies = _split_refs(refs, n_in, n_out, rider)
        x_ref, nw_ref, wg_ref, wu_ref, wd_ref = ins[:5]
        out_ref, g_ref, u_ref, hb_ref, ab_ref = outs[:5]
        i = pl.program_id(0)
        finish = _ride(copies, i == 0, i == ni - 1)

        xv = x_ref[...]
        hb = (xv * _rms_r(xv) * nw_ref[...]).astype(BF16)
        hb_ref[...] = hb
        y = jnp.zeros((tm, d), F32)
        for lo, hi in _ffn_chunks(f):
            g = _dot_nt(hb, wg_ref[lo:hi, :])
            u = _dot_nt(hb, wu_ref[lo:hi, :])
            g_ref[:, lo:hi] = g.astype(BF16)
            u_ref[:, lo:hi] = u.astype(BF16)
            ab = (g * _sigmoid(g) * u).astype(BF16)
            ab_ref[:, lo:hi] = ab
            y = y + _dot(ab, wd_ref[lo:hi, :])
        y = xv + 0.5 * y
        if with_loss:
            tgt_ref, loss_ref = ins[5], outs[5]
            diff = y - tgt_ref[...]
            out_ref[...] = diff * (1.0 / d)

            @pl.when(i == 0)
            def _():
                loss_ref[...] = jnp.zeros_like(loss_ref)

            loss_ref[...] += jnp.sum(diff * diff) * (0.5 / d)
        else:
            out_ref[...] = y
        finish()

    row = pl.BlockSpec((tm, d), lambda i: (i, 0))
    weight = pl.BlockSpec((f, d), lambda i: (0, 0), pipeline_mode=pl.Buffered(1))
    in_specs = [row, pl.BlockSpec((1, d), lambda i: (0, 0)), weight, weight, weight]
    args = [x, nw, wg, wu, wd]
    if with_loss:
        in_specs.append(row)
        args.append(tgt)
    blk = pl.BlockSpec((tm, f), lambda i: (i, 0))
    out_shape = [jax.ShapeDtypeStruct((s, d), F32), jax.ShapeDtypeStruct((s, f), BF16),
                 jax.ShapeDtypeStruct((s, f), BF16), jax.ShapeDtypeStruct((s, d), BF16),
                 jax.ShapeDtypeStruct((s, f), BF16)]
    out_specs = [row, blk, blk, row, blk]
    if with_loss:
        out_shape.append(jax.ShapeDtypeStruct((1, 128), F32))
        out_specs.append(pl.BlockSpec((1, 128), lambda i: (0, 0)))
    return _call(body, "ffn_fwd_loss" if with_loss else "ffn_fwd", (ni,), in_specs, out_specs, out_shape, args,
                 rider=rider)


def _call(body, name, grid, in_specs, out_specs, out_shape, args, scratch=(), rider=None):
    in_specs, out_specs, out_shape, scratch = list(in_specs), list(out_specs), list(out_shape), list(scratch)
    extra, aliases = [], {}
    if rider is not None:
        extra = rider.operands()
        aliases = rider.aliases(len(args), len(out_shape))
        in_specs += [ANY] * len(extra)
        out_specs += [ANY] * len(rider.inplace)
        out_shape += rider.out_shape()
        scratch += rider.scratch()
    return pl.pallas_call(
        body, name=name, grid=grid, in_specs=in_specs, out_specs=out_specs, out_shape=out_shape,
        input_output_aliases=aliases, scratch_shapes=scratch,
        compiler_params=_params(("arbitrary",) * len(grid)),
    )(*args, *extra)


def _ffn_bwd_act(x, nw, dy, g, u, wg, wu, wd, name, rider=None):
    s, d = x.shape
    f = wg.shape[0]
    tm = min(FFN_TILE, s)
    assert s % tm == 0

    def body(*refs):
        ins, outs, _, copies = _split_refs(refs, 8, 5, rider)
        x_ref, nw_ref, dy_ref, g_ref, u_ref, wg_ref, wu_ref, wd_ref = ins
        dx_ref, dg_ref, du_ref, dyb_ref, dnw_ref = outs
        finish = _ride(copies, pl.program_id(0) == 0, pl.program_id(0) == s // tm - 1)
        dyv = dy_ref[...]
        dyb = dyv.astype(BF16)
        dyb_ref[...] = dyb
        dh = jnp.zeros((tm, d), F32)
        for lo, hi in _ffn_chunks(f):
            da = 0.5 * _dot_nt(dyb, wd_ref[lo:hi, :])
            gv = g_ref[:, lo:hi].astype(F32)
            sg = _sigmoid(gv)
            dub = (da * (gv * sg)).astype(BF16)
            dgb = (da * u_ref[:, lo:hi].astype(F32) * (sg * (1.0 + gv * (1.0 - sg)))).astype(BF16)
            dg_ref[:, lo:hi] = dgb
            du_ref[:, lo:hi] = dub
            dh = dh + _dot(dgb, wg_ref[lo:hi, :]) + _dot(dub, wu_ref[lo:hi, :])
        xv = x_ref[...]
        dx, dn = _rms_bwd(xv, _rms_r(xv), nw_ref[...], dh)
        dx_ref[...] = dyv + dx

        @pl.when(pl.program_id(0) == 0)
        def _():
            dnw_ref[...] = jnp.zeros_like(dnw_ref)

        dnw_ref[...] += dn
        finish()

    row = pl.BlockSpec((tm, d), lambda i: (i, 0))
    vec = pl.BlockSpec((1, d), lambda i: (0, 0))
    blk = pl.BlockSpec((tm, f), lambda i: (i, 0))
    weight = pl.BlockSpec((f, d), lambda i: (0, 0), pipeline_mode=pl.Buffered(1))
    return _call(
        body, name, (s // tm,), [row, vec, row, blk, blk, weight, weight, weight], [row, blk, blk, row, vec],
        [jax.ShapeDtypeStruct((s, d), F32), jax.ShapeDtypeStruct((s, f), BF16),
         jax.ShapeDtypeStruct((s, f), BF16), jax.ShapeDtypeStruct((s, d), BF16),
         jax.ShapeDtypeStruct((1, d), F32)],
        [x, nw, dy, g, u, wg, wu, wd], rider=rider)


def _wgrad(a, b, a_spec, b_spec, out_rows, out_cols, scale, name, tk, rider=None, per_step=1):
    s = a.shape[-2]
    nk = s // tk
    steps = N_CHIPS // per_step
    assert s % tk == 0

    def body(*refs):
        (a_ref, b_ref), (out_ref,), (acc,), copies = _split_refs(refs, 2, 1, rider)
        j, k = pl.program_id(0), pl.program_id(1)
        finish = _ride(copies, jnp.logical_and(j == 0, k == 0), jnp.logical_and(j == steps - 1, k == nk - 1))

        @pl.when(k == 0)
        def _():
            acc[...] = jnp.zeros_like(acc)

        acc[...] += _dot_tn(a_ref[...], b_ref[...])

        @pl.when(k == nk - 1)
        def _():
            for t in range(per_step):
                out_ref[t] = (acc[t * out_rows:(t + 1) * out_rows, :] * scale).astype(BF16)

        finish()

    outs = _call(
        body, name, (steps, nk), [a_spec(tk), b_spec(tk)],
        [pl.BlockSpec((per_step, out_rows, out_cols), lambda j, k: (j, 0, 0))],
        [jax.ShapeDtypeStruct((N_CHIPS, out_rows, out_cols), BF16)], [a, b],
        scratch=[pltpu.VMEM((per_step * out_rows, out_cols), F32)], rider=rider)
    return outs[0] if rider is None else outs


def _wgrad_whole(a, b, col_blocks, name, rider=None):
    s, m = a.shape
    n = b.shape[1]
    tk = min(WHOLE_TILE, s)
    nk = s // tk
    assert s % tk == 0
    out_shape = (N_CHIPS, m, n // N_CHIPS) if col_blocks else (N_CHIPS, m // N_CHIPS, n)

    def body(*refs):
        (a_ref, b_ref), (out_ref,), (acc,), copies = _split_refs(refs, 2, 1, rider)
        k = pl.program_id(0)
        finish = _ride(copies, k == 0, k == nk - 1)

        @pl.when(k == 0)
        def _():
            acc[...] = jnp.zeros_like(acc)

        acc[...] += _dot_tn(a_ref[...], b_ref[...])

        @pl.when(k == nk - 1)
        def _():
            for j in range(N_CHIPS):
                if col_blocks:
                    out_ref[j] = acc[:, j * out_shape[2]:(j + 1) * out_shape[2]].astype(BF16)
                else:
                    out_ref[j] = acc[j * out_shape[1]:(j + 1) * out_shape[1], :].astype(BF16)

        finish()

    outs = _call(
        body, name, (nk,), [pl.BlockSpec((tk, m), lambda k: (k, 0)), pl.BlockSpec((tk, n), lambda k: (k, 0))],
        [pl.BlockSpec(out_shape, lambda k: (0, 0, 0))], [jax.ShapeDtypeStruct(out_shape, BF16)], [a, b],
        scratch=[pltpu.VMEM((m, n), F32)], rider=rider)
    return outs[0] if rider is None else outs


def _ffn_wgrad(hidden, shared, scale, name, rider=None):
    s, d = shared.shape
    half = hidden.shape[1] // 2
    return _wgrad(hidden, shared, lambda tk: pl.BlockSpec((tk, half), lambda j, k: (k, j)),
                  lambda tk: pl.BlockSpec((tk, d), lambda j, k: (k, 0)), half // 2, d, scale, name,
                  min(WGRAD_TILE, s), rider, per_step=2)


def _mix_pre(x, nw, win):
    s, d = x.shape
    nb, _, cb = win.shape
    tm = min(FFN_TILE, s)
    assert s % tm == 0

    def body(x_ref, nw_ref, w_ref, p_ref, hb_ref):
        xv = x_ref[...]
        hb = (xv * _rms_r(xv) * nw_ref[...]).astype(BF16)
        hb_ref[...] = hb
        for j in range(nb):
            p_ref[:, j * cb:(j + 1) * cb] = _dot(hb, w_ref[j])

    row = pl.BlockSpec((tm, d), lambda i: (i, 0))
    return pl.pallas_call(
        body, name="mix_pre", grid=(s // tm,),
        in_specs=[row, pl.BlockSpec((1, d), lambda i: (0, 0)),
                  pl.BlockSpec((nb, d, cb), lambda i: (0, 0, 0), pipeline_mode=pl.Buffered(1))],
        out_specs=[pl.BlockSpec((tm, nb * cb), lambda i: (i, 0)), row],
        out_shape=[jax.ShapeDtypeStruct((s, nb * cb), F32), jax.ShapeDtypeStruct((s, d), BF16)],
        compiler_params=_params(("arbitrary",)),
    )(x, nw, win)


def _mix_pre_bwd(x, nw, dres, dpb, win):
    s, d = x.shape
    nb, _, cb = win.shape
    tm = min(FFN_TILE, s)
    assert s % tm == 0

    def body(x_ref, nw_ref, dres_ref, dp_ref, w_ref, dx_ref, dnw_ref):
        dh = jnp.zeros((tm, d), F32)
        for j in range(nb):
            dh = dh + _dot_nt(dp_ref[:, j * cb:(j + 1) * cb], w_ref[j])
        xv = x_ref[...]
        dx, dn = _rms_bwd(xv, _rms_r(xv), nw_ref[...], dh)
        dx_ref[...] = dres_ref[...] + dx

        @pl.when(pl.program_id(0) == 0)
        def _():
            dnw_ref[...] = jnp.zeros_like(dnw_ref)

        dnw_ref[...] += dn

    row = pl.BlockSpec((tm, d), lambda i: (i, 0))
    vec = pl.BlockSpec((1, d), lambda i: (0, 0))
    return pl.pallas_call(
        body, name="mix_pre_bwd", grid=(s // tm,),
        in_specs=[row, vec, row, pl.BlockSpec((tm, nb * cb), lambda i: (i, 0)),
                  pl.BlockSpec((nb, d, cb), lambda i: (0, 0, 0), pipeline_mode=pl.Buffered(1))],
        out_specs=[row, vec],
        out_shape=[jax.ShapeDtypeStruct((s, d), F32), jax.ShapeDtypeStruct((1, d), F32)],
        compiler_params=_params(("arbitrary",)),
    )(x, nw, dres, dpb, win)


def _mix_post(x, yr, ya, nr, na, wout):
    s, d = x.shape
    h = yr.shape[1]
    tm = _tile(s)

    def body(x_ref, yr_ref, ya_ref, nr_ref, na_ref, w_ref, out_ref):
        yrv = yr_ref[...]
        yav = ya_ref[...]
        onb = (yrv * _rms_r(yrv) * nr_ref[...]).astype(BF16)
        oab = (yav * _rms_r(yav) * na_ref[...]).astype(BF16)
        out_ref[...] = x_ref[...] + _dot(onb, w_ref[0:h, :]) + _dot(oab, w_ref[h:2 * h, :])

    row = pl.BlockSpec((tm, d), lambda i: (i, 0))
    half = pl.BlockSpec((tm, h), lambda i: (i, 0))
    vec = pl.BlockSpec((1, h), lambda i: (0, 0))
    return pl.pallas_call(
        body, name="mix_post", grid=(s // tm,),
        in_specs=[row, half, half, vec, vec, pl.BlockSpec((2 * h, d), lambda i: (0, 0))],
        out_specs=row, out_shape=jax.ShapeDtypeStruct((s, d), F32),
        compiler_params=_params(("arbitrary",)),
    )(x, yr, ya, nr, na, wout)


def _mix_post_bwd(dx, yr, ya, nr, na, wout):
    s, d = dx.shape
    h = yr.shape[1]
    tm = _tile(s)

    def body(dx_ref, yr_ref, ya_ref, nr_ref, na_ref, w_ref,
             dyr_ref, dya_ref, yc_ref, dxb_ref, dnr_ref, dna_ref):
        i = pl.program_id(0)
        dxb = dx_ref[...].astype(BF16)
        dxb_ref[...] = dxb
        dyc = _dot_nt(dxb, w_ref[...])
        yrv = yr_ref[...]
        yav = ya_ref[...]
        rr = _rms_r(yrv)
        ra = _rms_r(yav)
        yc_ref[:, 0:h] = (yrv * rr * nr_ref[...]).astype(BF16)
        yc_ref[:, h:2 * h] = (yav * ra * na_ref[...]).astype(BF16)
        dyr, dnr = _rms_bwd(yrv, rr, nr_ref[...], dyc[:, 0:h])
        dya, dna = _rms_bwd(yav, ra, na_ref[...], dyc[:, h:2 * h])
        dyr_ref[...] = dyr
        dya_ref[...] = dya

        @pl.when(i == 0)
        def _():
            dnr_ref[...] = jnp.zeros_like(dnr_ref)
            dna_ref[...] = jnp.zeros_like(dna_ref)

        dnr_ref[...] += dnr
        dna_ref[...] += dna

    row = pl.BlockSpec((tm, d), lambda i: (i, 0))
    half = pl.BlockSpec((tm, h), lambda i: (i, 0))
    vec = pl.BlockSpec((1, h), lambda i: (0, 0))
    return pl.pallas_call(
        body, name="mix_post_bwd", grid=(s // tm,),
        in_specs=[row, half, half, vec, vec, pl.BlockSpec((2 * h, d), lambda i: (0, 0))],
        out_specs=[half, half, pl.BlockSpec((tm, 2 * h), lambda i: (i, 0)), row, vec, vec],
        out_shape=[jax.ShapeDtypeStruct((s, h), F32), jax.ShapeDtypeStruct((s, h), F32),
                   jax.ShapeDtypeStruct((s, 2 * h), BF16), jax.ShapeDtypeStruct((s, d), BF16),
                   jax.ShapeDtypeStruct((1, h), F32), jax.ShapeDtypeStruct((1, h), F32)],
        compiler_params=_params(("arbitrary",)),
    )(dx, yr, ya, nr, na, wout)


def _shift_down(xv, s, prev8):
    rolled = pltpu.roll(xv, s, 0)
    row8 = lax.broadcasted_iota(jnp.int32, prev8.shape, 0)
    head = jnp.where(row8 < s, pltpu.roll(prev8, s, 0), rolled[0:8, :])
    return jnp.concatenate([head, rolled[8:, :]], axis=0)


def _shift_up(xv, s, next8):
    n = xv.shape[0]
    rolled = pltpu.roll(xv, n - s, 0)
    row8 = lax.broadcasted_iota(jnp.int32, next8.shape, 0)
    tail = jnp.where(row8 >= 8 - s, pltpu.roll(next8, 8 - s, 0), rolled[n - 8:, :])
    return jnp.concatenate([rolled[:n - 8, :], tail], axis=0)


def _scan_fwd(a, b):
    n = a.shape[0]
    sub = lax.broadcasted_iota(jnp.int32, a.shape, 0) % SUBLANES
    s = 1
    while s < SUBLANES:
        ok = sub >= s
        b = jnp.where(ok, a * pltpu.roll(b, s, 0) + b, b)
        a = jnp.where(ok, a * pltpu.roll(a, s, 0), a)
        s *= 2
    groups = []
    before = jnp.zeros((1, a.shape[1]), F32)
    for g in range(n // SUBLANES):
        rows = slice(g * SUBLANES, (g + 1) * SUBLANES)
        groups.append(a[rows] * before + b[rows])
        before = groups[-1][SUBLANES - 1:]
    return jnp.concatenate(groups, axis=0)


def _scan_bwd(a, b):
    n = a.shape[0]
    sub = lax.broadcasted_iota(jnp.int32, a.shape, 0) % SUBLANES
    s = 1
    while s < SUBLANES:
        ok = sub < SUBLANES - s
        b = jnp.where(ok, a * pltpu.roll(b, n - s, 0) + b, b)
        a = jnp.where(ok, a * pltpu.roll(a, n - s, 0), a)
        s *= 2
    groups = []
    after = jnp.zeros((1, a.shape[1]), F32)
    for g in reversed(range(n // SUBLANES)):
        rows = slice(g * SUBLANES, (g + 1) * SUBLANES)
        groups.append(a[rows] * after + b[rows])
        after = groups[-1][:1]
    return jnp.concatenate(groups[::-1], axis=0)


def _rglru_gates(xv, prev8, cw_ref, cb_ref, wa_ref, ba_ref, wx_ref, bx_ref, lam_ref):
    x1 = _shift_down(xv, 1, prev8)
    x2 = _shift_down(xv, 2, prev8)
    x3 = _shift_down(xv, 3, prev8)
    xc = cw_ref[3:4, :] * xv + cw_ref[2:3, :] * x1 + cw_ref[1:2, :] * x2 + cw_ref[0:1, :] * x3 + cb_ref[...]
    xcb = xc.astype(BF16)
    r = _sigmoid(_dot(xcb, wa_ref[...]) + ba_ref[...])
    ig = _sigmoid(_dot(xcb, wx_ref[...]) + bx_ref[...])
    c = RG_C * _log_sigmoid(lam_ref[...])
    la = r * c
    a = jnp.exp(la)
    m = jnp.sqrt(-_expm1_neg(2.0 * la))
    return (x1, x2, x3), xc, xcb, r, ig, c, a, m


def _rglru_fwd(proj, cw, cb, wa, ba, wx, bx, lam):
    s = proj.shape[0]
    w = D_RNN
    tm = _tile(s)

    def body(xr_ref, gate_ref, cw_ref, cb_ref, wa_ref, ba_ref, wx_ref, bx_ref, lam_ref,
             y_ref, h_ref, prev, hlast):
        @pl.when(pl.program_id(0) == 0)
        def _():
            prev[...] = jnp.zeros_like(prev)
            hlast[...] = jnp.zeros_like(hlast)

        xv = xr_ref[...]
        _, xc, _, _, ig, _, a, m = _rglru_gates(xv, prev[...], cw_ref, cb_ref, wa_ref, ba_ref,
                                                wx_ref, bx_ref, lam_ref)
        b = m * (ig * xc)
        row = lax.broadcasted_iota(jnp.int32, b.shape, 0)
        b = jnp.where(row == 0, b + a * hlast[...], b)
        h = _scan_fwd(a, b)
        h_ref[...] = h
        y_ref[...] = h * _gelu(gate_ref[...])
        prev[...] = xv[tm - 8:, :]
        hlast[...] = h[tm - 1:tm, :]

    vec = pl.BlockSpec((1, w), lambda i: (0, 0))
    sq = pl.BlockSpec((w, w), lambda i: (0, 0))
    out = pl.BlockSpec((tm, w), lambda i: (i, 0))
    return pl.pallas_call(
        body, name="rglru_fwd", grid=(s // tm,),
        in_specs=[pl.BlockSpec((tm, w), lambda i: (i, 0)), pl.BlockSpec((tm, w), lambda i: (i, 1)),
                  pl.BlockSpec((CONV_W, w), lambda i: (0, 0)), vec, sq, vec, sq, vec, vec],
        out_specs=[out, out],
        out_shape=[jax.ShapeDtypeStruct((s, w), F32), jax.ShapeDtypeStruct((s, w), F32)],
        scratch_shapes=[pltpu.VMEM((8, w), F32), pltpu.VMEM((1, w), F32)],
        compiler_params=_params(("arbitrary",)),
    )(proj, proj, cw, cb, wa, ba, wx, bx, lam)


def _rglru_bwd(proj, hseq, dyr, cw, cb, wa, ba, wx, bx, lam):
    s = proj.shape[0]
    w = D_RNN
    tm = _tile(s)
    nt = s // tm
    t8 = tm // 8

    def body(xr_ref, xp_ref, gate_ref, h_ref, hp_ref, dy_ref, cw_ref, cb_ref, wa_ref, ba_ref,
             wx_ref, bx_ref, lam_ref,
             dxr_ref, dgate_ref, dcw_ref, dcb_ref, dwa_ref, dba_ref, dwx_ref, dbx_ref, dlam_ref,
             carry, dxc_next):
        i = pl.program_id(0)
        first_tile = i == nt - 1

        @pl.when(i == 0)
        def _():
            carry[...] = jnp.zeros_like(carry)
            dxc_next[...] = jnp.zeros_like(dxc_next)
            for ref in (dcw_ref, dcb_ref, dwa_ref, dba_ref, dwx_ref, dbx_ref, dlam_ref):
                ref[...] = jnp.zeros_like(ref)

        xv = xr_ref[...]
        prev8 = jnp.where(first_tile, 0.0, xp_ref[...])
        hprev8 = jnp.where(first_tile, 0.0, hp_ref[...])
        (x1, x2, x3), xc, xcb, r, ig, c, a, m = _rglru_gates(
            xv, prev8, cw_ref, cb_ref, wa_ref, ba_ref, wx_ref, bx_ref, lam_ref)
        gv = gate_ref[...]
        hv = h_ref[...]
        dy = dy_ref[...]
        dgate_ref[...] = (dy * hv * _gelu_grad(gv)).astype(BF16)
        dh = dy * _gelu(gv)
        row = lax.broadcasted_iota(jnp.int32, dh.shape, 0)
        dh = jnp.where(row == tm - 1, dh + carry[...], dh)
        a_up = jnp.where(row == tm - 1, 0.0, pltpu.roll(a, tm - 1, 0))
        lam_t = _scan_bwd(a_up, dh)
        carry[...] = a[0:1, :] * lam_t[0:1, :]
        hm1 = _shift_down(hv, 1, hprev8)
        da = lam_t * hm1
        ixc = ig * xc
        dm = lam_t * ixc
        dig = lam_t * m * xc
        dxc = lam_t * m * ig
        dla = da * a - dm * (a * a) / m
        dr = dla * c
        dlam_ref[...] += jnp.sum(dla * r, axis=0, keepdims=True)
        dpa = dr * r * (1.0 - r)
        dpi = dig * ig * (1.0 - ig)
        dba_ref[...] += jnp.sum(dpa, axis=0, keepdims=True)
        dbx_ref[...] += jnp.sum(dpi, axis=0, keepdims=True)
        dpab = dpa.astype(BF16)
        dpib = dpi.astype(BF16)
        dwa_ref[...] += _dot_tn(xcb, dpab)
        dwx_ref[...] += _dot_tn(xcb, dpib)
        dxc = dxc + _dot_nt(dpab, wa_ref[...]) + _dot_nt(dpib, wx_ref[...])
        dcb_ref[...] += jnp.sum(dxc, axis=0, keepdims=True)
        dcw_ref[3:4, :] += jnp.sum(dxc * xv, axis=0, keepdims=True)
        dcw_ref[2:3, :] += jnp.sum(dxc * x1, axis=0, keepdims=True)
        dcw_ref[1:2, :] += jnp.sum(dxc * x2, axis=0, keepdims=True)
        dcw_ref[0:1, :] += jnp.sum(dxc * x3, axis=0, keepdims=True)
        nxt = dxc_next[...]
        dxr = (cw_ref[3:4, :] * dxc + cw_ref[2:3, :] * _shift_up(dxc, 1, nxt)
               + cw_ref[1:2, :] * _shift_up(dxc, 2, nxt) + cw_ref[0:1, :] * _shift_up(dxc, 3, nxt))
        dxr_ref[...] = dxr.astype(BF16)
        dxc_next[...] = dxc[0:8, :]

        @pl.when(first_tile)
        def _():
            lv = lam_ref[...]
            dlam_ref[...] = dlam_ref[...] * (RG_C * _sigmoid(-lv))

    rev = lambda i: nt - 1 - i
    vec = pl.BlockSpec((1, w), lambda i: (0, 0))
    sq = pl.BlockSpec((w, w), lambda i: (0, 0))
    cur = lambda col: pl.BlockSpec((tm, w), lambda i: (rev(i), col))
    before = lambda cols: pl.BlockSpec((8, w), lambda i: (jnp.maximum(rev(i) * t8 - 1, 0), 0))
    return pl.pallas_call(
        body, name="rglru_bwd", grid=(nt,),
        in_specs=[cur(0), before(None), cur(1), cur(0), before(None), cur(0),
                  pl.BlockSpec((CONV_W, w), lambda i: (0, 0)), vec, sq, vec, sq, vec, vec],
        out_specs=[cur(0), cur(0), pl.BlockSpec((CONV_W, w), lambda i: (0, 0)), vec, sq, vec, sq, vec, vec],
        out_shape=[jax.ShapeDtypeStruct((s, w), BF16), jax.ShapeDtypeStruct((s, w), BF16),
                   jax.ShapeDtypeStruct((CONV_W, w), F32), jax.ShapeDtypeStruct((1, w), F32),
                   jax.ShapeDtypeStruct((w, w), F32), jax.ShapeDtypeStruct((1, w), F32),
                   jax.ShapeDtypeStruct((w, w), F32), jax.ShapeDtypeStruct((1, w), F32),
                   jax.ShapeDtypeStruct((1, w), F32)],
        scratch_shapes=[pltpu.VMEM((1, w), F32), pltpu.VMEM((8, w), F32)],
        compiler_params=_params(("arbitrary",)),
    )(proj, proj, proj, hseq, hseq, dyr, cw, cb, wa, ba, wx, bx, lam)


def _sb_logs(z, valid):
    l1p = jnp.log(1.0 + jnp.exp(-jnp.abs(z)))
    lb = jnp.minimum(z, 0.0) - l1p
    lm = jnp.where(valid, -jnp.maximum(z, 0.0) - l1p, 0.0)
    return lb, lm


class _Window:
    def __init__(self):
        blk, win, cut = ATT_BLOCK, ATT_WINDOW, ATT_SPLIT
        self.row = lax.broadcasted_iota(jnp.int32, (blk, win), 0)
        self.col = lax.broadcasted_iota(jnp.int32, (blk, win), 1)

        def tri(n, later):
            j = lax.broadcasted_iota(jnp.int32, (n, n), 0)
            s = lax.broadcasted_iota(jnp.int32, (n, n), 1)
            return jnp.where((j > s) if later else (j < s), 1.0, 0.0).astype(BF16)

        self.later = (tri(cut, True), tri(win - cut, True))
        self.earlier = (tri(cut, False), tri(win - cut, False))

    def place(self, qi, g):
        end = (qi + 1) * ATT_BLOCK - g * ATT_WINDOW
        start = pl.multiple_of(jnp.maximum(end - ATT_WINDOW, 0), ATT_BLOCK)
        valid = start + self.col < jnp.minimum(qi * ATT_BLOCK + self.row, end)
        return start, valid

    @staticmethod
    def _parts(xv):
        hi = xv.astype(BF16)
        lo = (xv - hi.astype(F32)).astype(BF16)
        cut = ATT_SPLIT
        sums = (jnp.sum(xv[:, :cut], axis=1, keepdims=True), jnp.sum(xv[:, cut:], axis=1, keepdims=True))
        return (hi[:, :cut], lo[:, :cut]), (hi[:, cut:], lo[:, cut:]), sums

    def sums_after(self, xv, carry):
        (h0, l0), (h1, l1), (s0, s1) = self._parts(xv)
        first = _dot(h0, self.later[0]) + _dot(l0, self.later[0]) + (s1 + carry)
        last = _dot(h1, self.later[1]) + _dot(l1, self.later[1]) + carry
        return jnp.concatenate([first, last], axis=1), s0 + s1

    def sums_before(self, xv, carry):
        (h0, l0), (h1, l1), (s0, s1) = self._parts(xv)
        first = _dot(h0, self.earlier[0]) + _dot(l0, self.earlier[0]) + carry
        last = _dot(h1, self.earlier[1]) + _dot(l1, self.earlier[1]) + (s0 + carry)
        return jnp.concatenate([first, last], axis=1), s0 + s1


class _HeadPair:
    def __init__(self):
        lanes = 2 * HEAD_DIM
        lane = lax.broadcasted_iota(jnp.int32, (1, lanes), 1)
        self.masks = [lane // HEAD_DIM == h for h in (0, 1)]
        i = lax.broadcasted_iota(jnp.int32, (lanes, lanes), 0) // HEAD_DIM
        j = lax.broadcasted_iota(jnp.int32, (lanes, lanes), 1) // HEAD_DIM
        self.same_head = jnp.where(i == j, 1.0, 0.0).astype(BF16)

    def only(self, h, xv):
        return jnp.where(self.masks[h], xv, jnp.zeros_like(xv))

    def merge(self, per_head):
        return jnp.where(self.masks[0], per_head[0], per_head[1])

    def mean(self, xv):
        hi = xv.astype(BF16)
        lo = (xv - hi.astype(F32)).astype(BF16)
        return (_dot(hi, self.same_head) + _dot(lo, self.same_head)) * (1.0 / HEAD_DIM)

    def rms_r(self, xv):
        return lax.rsqrt(self.mean(xv * xv) + EPS)

    def rms_bwd(self, xv, r, nw, dh):
        t = dh * nw
        dx = r * t - xv * (r * r * r * self.mean(t * xv))
        dn = jnp.sum(dh * xv * r, axis=0, keepdims=True)
        return dx, dn[:, :HEAD_DIM] + dn[:, HEAD_DIM:]


def _attn_fwd(proj, qg, kg, rider=None):
    s = proj.shape[0]
    blk, win, dh = ATT_BLOCK, ATT_WINDOW, HEAD_DIM
    nq = s // blk
    scale = 1.0 / math.sqrt(dh)
    heads = (0, 1)
    assert s >= win and s % blk == 0

    def body(*refs):
        (q_ref, k_ref, v_ref, qg_ref, kg_ref), (o_ref,), (qn, kn, vb), copies = _split_refs(refs, 5, 1, rider)
        finish = _ride(copies, pl.program_id(0) == 0, pl.program_id(0) == N_HEADS // 2 - 1)
        wd, hp = _Window(), _HeadPair()
        qv = q_ref[...]
        qn[...] = (qv * hp.rms_r(qv) * qg_ref[...] * scale).astype(BF16)
        kv = k_ref[...]
        kn[...] = (kv * hp.rms_r(kv) * kg_ref[...]).astype(BF16)
        vb[...] = v_ref[...].astype(BF16)

        def q_step(qi, _):
            qoff = pl.multiple_of(qi * blk, blk)
            qt = qn[pl.ds(qoff, blk), :]
            qts = [hp.only(h, qt) for h in heads]

            def more(carry):
                g, live = carry[:2]
                return jnp.logical_and((qi + 1) * blk - g * win > 0, live > 0)

            def window(carry):
                g, _, accs, runs = carry
                start, valid = wd.place(qi, g)
                kt = kn[pl.ds(start, win), :]
                zs = [_dot_nt(qts[h], kt) for h in heads]
                logs = [_sb_logs(z, valid) for z in zs]
                sums = [wd.sums_after(logs[h][1], runs[h]) for h in heads]
                wgts = [jnp.where(valid, jnp.exp(logs[h][0] + sums[h][0]), 0.0).astype(BF16) for h in heads]
                vt = vb[pl.ds(start, win), :]
                accs = tuple(accs[h] + _dot(wgts[h], vt) for h in heads)
                runs = tuple(runs[h] + sums[h][1] for h in heads)
                live = (jnp.maximum(jnp.max(runs[0]), jnp.max(runs[1])) > EXP_ZERO).astype(jnp.int32)
                return g + 1, live, accs, runs

            zero = lambda cols: tuple(jnp.zeros((blk, cols), F32) for _ in heads)
            _, _, accs, _ = lax.while_loop(more, window, (jnp.int32(0), jnp.int32(1), zero(2 * dh), zero(1)))
            o_ref[pl.ds(qoff, blk), :] = hp.merge(accs)
            return 0

        lax.fori_loop(0, nq, q_step, 0)
        finish()

    pair = lambda group: pl.BlockSpec((s, 2 * dh), lambda p: (0, group * (D_ATT // (2 * dh)) + p))
    vec = pl.BlockSpec((1, 2 * dh), lambda p: (0, 0))
    return _call(
        body, "attn_fwd", (N_HEADS // 2,), [pair(2), pair(3), pair(4), vec, vec], [pair(0)],
        [jax.ShapeDtypeStruct((s, D_ATT), F32)], [proj, proj, proj, jnp.tile(qg, (1, 2)), jnp.tile(kg, (1, 2))],
        scratch=[pltpu.VMEM((s, 2 * dh), BF16)] * 3, rider=rider)


def _attn_bwd(proj, dya, qg, kg, rider=None):
    s = proj.shape[0]
    blk, win, dh = ATT_BLOCK, ATT_WINDOW, HEAD_DIM
    nq = s // blk
    max_windows = -(-s // win) + 1
    scale = 1.0 / math.sqrt(dh)
    steps = N_HEADS // 2
    heads = (0, 1)
    assert s >= win and s % blk == 0

    def body(*refs):
        ins, outs, scratch, copies = _split_refs(refs, 6, 5, rider)
        q_ref, k_ref, v_ref, do_ref, qg_ref, kg_ref = ins
        dq_ref, dk_ref, dv_ref, dqg_ref, dkg_ref = outs
        qn, kn, vb, dob, runs_ref, dqn, dkn, dvn = scratch
        finish = _ride(copies, pl.program_id(0) == 0, pl.program_id(0) == steps - 1)
        wd, hp = _Window(), _HeadPair()

        @pl.when(pl.program_id(0) == 0)
        def _():
            dqg_ref[...] = jnp.zeros_like(dqg_ref)
            dkg_ref[...] = jnp.zeros_like(dkg_ref)

        qv = q_ref[...]
        qn[...] = (qv * hp.rms_r(qv) * qg_ref[...] * scale).astype(BF16)
        kv = k_ref[...]
        kn[...] = (kv * hp.rms_r(kv) * kg_ref[...]).astype(BF16)
        vb[...] = v_ref[...].astype(BF16)
        dob[...] = do_ref[...].astype(BF16)
        dkn[...] = jnp.zeros_like(dkn)
        dvn[...] = jnp.zeros_like(dvn)

        def q_step(qi, _):
            qoff = pl.multiple_of(qi * blk, blk)
            qt = qn[pl.ds(qoff, blk), :]
            dot = dob[pl.ds(qoff, blk), :]
            qts = [hp.only(h, qt) for h in heads]
            dots = [hp.only(h, dot) for h in heads]

            zero = lambda cols: tuple(jnp.zeros((blk, cols), F32) for _ in heads)

            def logs_of(g):
                start, valid = wd.place(qi, g)
                kt = kn[pl.ds(start, win), :]
                return [_sb_logs(_dot_nt(qts[h], kt), valid) for h in heads]

            def row_sums(logs):
                return tuple(jnp.sum(logs[h][1], axis=1, keepdims=True) for h in heads)

            def still_live(runs):
                return jnp.maximum(jnp.max(runs[0]), jnp.max(runs[1])) > EXP_ZERO

            def window_grads(g, logs, runs, esums):
                start, valid = wd.place(qi, g)
                kt = kn[pl.ds(start, win), :]
                vt = vb[pl.ds(start, win), :]
                dws = [_dot_nt(dots[h], vt) for h in heads]
                tails = [wd.sums_after(logs[h][1], runs[h])[0] for h in heads]
                wgts = [jnp.where(valid, jnp.exp(logs[h][0] + tails[h]), 0.0) for h in heads]
                es = [dws[h] * wgts[h] for h in heads]
                befores = [wd.sums_before(es[h], esums[h]) for h in heads]
                dzbs = []
                for h in heads:
                    beta = jnp.exp(logs[h][0])
                    dz = jnp.where(valid, es[h] * (1.0 - beta) - befores[h][0] * beta, 0.0)
                    dzbs.append(dz.astype(BF16))
                dkn[pl.ds(start, win), :] += _dot_tn(dzbs[0], qts[0]) + _dot_tn(dzbs[1], qts[1])
                dvn[pl.ds(start, win), :] += (_dot_tn(wgts[0].astype(BF16), dots[0])
                                              + _dot_tn(wgts[1].astype(BF16), dots[1]))
                return tuple(_dot(dzbs[h], kt) for h in heads), tuple(befores[h][1] for h in heads)

            logs0 = logs_of(0)
            runs1 = row_sums(logs0)

            def one_window():
                return window_grads(0, logs0, zero(1), zero(1))[0]

            def all_windows():
                def more(carry):
                    g, live = carry[:2]
                    return jnp.logical_and((qi + 1) * blk - g * win > 0, live > 0)

                def run_window(carry):
                    g, _, runs = carry
                    for h in heads:
                        runs_ref[h, g] = runs[h]
                    sums = row_sums(logs_of(g))
                    runs = tuple(runs[h] + sums[h] for h in heads)
                    return g + 1, still_live(runs).astype(jnp.int32), runs

                for h in heads:
                    runs_ref[h, 0] = jnp.zeros((blk, 1), F32)
                windows, _, _ = lax.while_loop(more, run_window, (jnp.int32(1), jnp.int32(1), runs1))

                def k_window(gg, carry):
                    dq_accs, esums = carry
                    g = windows - 1 - gg
                    parts, totals = window_grads(g, logs_of(g), [runs_ref[h, g] for h in heads], esums)
                    return (tuple(dq_accs[h] + parts[h] for h in heads),
                            tuple(esums[h] + totals[h] for h in heads))

                return lax.fori_loop(0, windows, k_window, (zero(2 * dh), zero(1)))[0]

            earlier_keys = (qi + 1) * blk - win > 0
            dq_accs = lax.cond(jnp.logical_and(earlier_keys, still_live(runs1)), all_windows, one_window)
            dqn[pl.ds(qoff, blk), :] = hp.merge(dq_accs)
            return 0

        lax.fori_loop(0, nq, q_step, 0)

        dq, dqg = hp.rms_bwd(qv, hp.rms_r(qv), qg_ref[...] * scale, dqn[...])
        dq_ref[...] = dq.astype(BF16)
        dqg_ref[...] += dqg * scale
        dk, dkg = hp.rms_bwd(kv, hp.rms_r(kv), kg_ref[...], dkn[...])
        dk_ref[...] = dk.astype(BF16)
        dkg_ref[...] += dkg
        dv_ref[...] = dvn[...].astype(BF16)
        finish()

    pair = lambda group: pl.BlockSpec((s, 2 * dh), lambda p: (0, group * (D_ATT // (2 * dh)) + p))
    vec2 = pl.BlockSpec((1, 2 * dh), lambda p: (0, 0))
    vec = pl.BlockSpec((1, dh), lambda p: (0, 0))
    return _call(
        body, "attn_bwd", (steps,), [pair(2), pair(3), pair(4), pair(0), vec2, vec2],
        [pair(0), pair(0), pair(0), vec, vec],
        [jax.ShapeDtypeStruct((s, D_ATT), BF16)] * 3 + [jax.ShapeDtypeStruct((1, dh), F32)] * 2,
        [proj, proj, proj, dya, jnp.tile(qg, (1, 2)), jnp.tile(kg, (1, 2))],
        scratch=[pltpu.VMEM((s, 2 * dh), BF16)] * 4 + [pltpu.VMEM((2, max_windows, blk, 1), F32)]
        + [pltpu.VMEM((s, 2 * dh), F32)] * 3, rider=rider)


def _block_diag(w):
    n, c, d = w.shape
    return jnp.einsum("ncd,nm->ncmd", w, jnp.eye(n, dtype=w.dtype)).reshape(n * c, n * d)


def _diag_blocks(full, n):
    c = full.shape[0] // n
    return jnp.stack([full[i * c:(i + 1) * c, i * c:(i + 1) * c] for i in range(n)])


FFN1 = ["ffn1_w_gate", "ffn1_w_up", "ffn1_w_down"]
FFN2 = ["ffn2_w_gate", "ffn2_w_up", "ffn2_w_down"]
MIXER = ["w_in", "w_out"]


def _pair_sums(gb, names, where):
    theirs = _pair_exchange([gb[n] for n in names], "pair_exchange_" + names[0])
    pair, own = _pair_sum([gb[n] for n in names], theirs, where, "pair_sum_" + names[0])
    return _chip_rider(pair, own)


def _local_step(x, tgt, stacks, conv_stack, small, where):
    big = dict(zip(FFN1, _gather_weights([stacks[n] for n in FFN1], [])))
    wa = _block_diag(small["rg_w_a"]).astype(BF16)
    wx = _block_diag(small["rg_w_x"]).astype(BF16)

    whole = lambda names: [big[n].reshape(-1, D_MODEL) for n in names]
    x1, g1, u1, hb1, ab1, *landed = _ffn_fwd(x, small["ffn1_norm"], *whole(FFN1),
                                             rider=_gather_rider([stacks[n] for n in MIXER], [conv_stack]))
    big.update(zip(MIXER, _forward_weights(landed[:len(MIXER)], "forward_mixer_weights")))
    conv_w = jnp.transpose(landed[-1], (1, 0, 2)).reshape(CONV_W, D_RNN)
    wout = big["w_out"].reshape(D_MODEL, D_MODEL)
    rg = (conv_w, small["conv_b"], wa, small["rg_b_a"], wx, small["rg_b_x"], small["rg_lambda"])
    proj, hb2 = _mix_pre(x1, small["mix_norm"], big["w_in"])
    yr, hseq = _rglru_fwd(proj, *rg)
    ya, *landed = _attn_fwd(proj, small["q_norm"], small["k_norm"], _gather_rider([stacks[n] for n in FFN2], []))
    big.update(zip(FFN2, _forward_weights(landed, "forward_ffn2_weights")))
    x2 = _mix_post(x1, yr, ya, small["rnn_out_norm"], small["attn_out_norm"], wout)
    dx3, g2, u2, hb3, ab3, loss = _ffn_fwd(x2, small["ffn2_norm"], *whole(FFN2), tgt)

    gb, gs, slots = {}, {}, {}
    dx2, dg2, du2, dyb2, gs["ffn2_norm"] = _ffn_bwd_act(x2, small["ffn2_norm"], dx3, g2, u2, *whole(FFN2), "ffn2_bwd")
    gb["ffn2_w_gate"] = _ffn_wgrad(dg2, hb3, 1.0, "wgrad_gate_ffn2")
    gb["ffn2_w_up"] = _ffn_wgrad(du2, hb3, 1.0, "wgrad_up_ffn2")
    gb["ffn2_w_down"] = _ffn_wgrad(ab3, dyb2, 0.5, "wgrad_down_ffn2")
    dyr, dya, ycat, dxb2, gs["rnn_out_norm"], gs["attn_out_norm"] = _mix_post_bwd(
        dx2, yr, ya, small["rnn_out_norm"], small["attn_out_norm"], wout)
    gb["w_out"] = _wgrad_whole(ycat, dxb2, False, "wgrad_out")
    early = FFN2 + ["w_out"]
    dq, dk, dv, gs["q_norm"], gs["k_norm"], *done = _attn_bwd(
        proj, dya, small["q_norm"], small["k_norm"], _pair_sums(gb, early, where))
    slots.update(zip(early, done))
    dxr, dgate, gs["conv_w"], gs["conv_b"], dwa, gs["rg_b_a"], dwx, gs["rg_b_x"], gs["rg_lambda"] = _rglru_bwd(
        proj, hseq, dyr, *rg)
    gs["rg_w_a"] = _diag_blocks(dwa, RNN_BLOCKS)
    gs["rg_w_x"] = _diag_blocks(dwx, RNN_BLOCKS)
    dpb = jnp.concatenate([dxr, dgate, dq, dk, dv], axis=1)
    dx1, gs["mix_norm"] = _mix_pre_bwd(x1, small["mix_norm"], dx2, dpb, big["w_in"])
    dx0, dg1, du1, dyb1, gs["ffn1_norm"] = _ffn_bwd_act(x, small["ffn1_norm"], dx1, g1, u1, *whole(FFN1), "ffn1_bwd")

    mine = _place_shard(_pack([gs[n] for n in SMALL] + [loss[:, :1]]), where, F32, "place_small_grads",
                        by_device=True)
    gb["ffn1_w_gate"], everyone = _ffn_wgrad(dg1, hb1, 1.0, "wgrad_gate_ffn1", _small_rider(mine))
    gb["ffn1_w_up"], slots["ffn1_w_gate"] = _ffn_wgrad(
        du1, hb1, 1.0, "wgrad_up_ffn1", _pair_sums(gb, ["ffn1_w_gate"], where))
    gb["ffn1_w_down"], slots["ffn1_w_up"] = _ffn_wgrad(
        ab1, dyb1, 0.5, "wgrad_down_ffn1", _pair_sums(gb, ["ffn1_w_up"], where))
    gb["w_in"], slots["ffn1_w_down"] = _wgrad_whole(
        hb2, dpb, True, "wgrad_in", _pair_sums(gb, ["ffn1_w_down"], where))
    last = _pair_sums(gb, ["w_in"], where)
    slots["w_in"], = _chip_exchange(last.plain, last.inplace)
    return dx0, slots, gs, everyone


ANY = pl.BlockSpec(memory_space=pl.ANY)


def _place():
    x, y, c = lax.axis_index("x"), lax.axis_index("y"), lax.axis_index("c")
    other_chips = [(1 - x, y), (x, 1 - y), (1 - x, 1 - y)]
    return x, y, c, 2 * x + y, other_chips


def _remote(src, dst, send_sem, recv_sem, to):
    return pltpu.make_async_remote_copy(src_ref=src, dst_ref=dst, send_sem=send_sem, recv_sem=recv_sem,
                                        device_id=to, device_id_type=MESH)


def _copy_plan(pairs):
    sends = [functools.partial(_remote, *a) for a, _ in pairs]
    arrivals = [functools.partial(_remote, *b) for _, b in pairs]
    return sends, arrivals


class _Rider:
    def __init__(self, plan, plain, inplace, n_copies=None):
        self.plan, self.plain, self.inplace = plan, list(plain), list(inplace)
        self.n_copies = n_copies or 3 * len(self.inplace)

    def operands(self):
        return self.plain + self.inplace

    def out_shape(self):
        return [jax.ShapeDtypeStruct(a.shape, a.dtype) for a in self.inplace]

    def aliases(self, inputs_before, outputs_before):
        return {inputs_before + len(self.plain) + k: outputs_before + k for k in range(len(self.inplace))}

    def scratch(self):
        return [pltpu.SemaphoreType.DMA((self.n_copies,))] * 2


def _split_refs(refs, n_in, n_out, rider):
    if rider is None:
        return refs[:n_in], refs[n_in:n_in + n_out], refs[n_in + n_out:], None
    r_in, r_out = len(rider.operands()), len(rider.inplace)
    outs_at = n_in + r_in
    rest = refs[outs_at + n_out + r_out:]
    copies = functools.partial(rider.plan, refs[n_in:n_in + len(rider.plain)],
                               refs[outs_at + n_out:outs_at + n_out + r_out], *rest[-2:])
    return refs[:n_in], refs[outs_at:outs_at + n_out], rest[:-2], copies


def _ride(copies, first, last):
    if copies is None:
        return lambda: None

    @pl.when(first)
    def _():
        _start(copies()[0])

    def finish():
        @pl.when(last)
        def _():
            _finish(*copies())

    return finish


def _gather_rider(split, whole):
    n_split = len(split)
    return _Rider(lambda plain, stacks, ss, rs: _gather_ici(stacks, n_split, ss, rs), [], list(split) + list(whole))


def _chip_rider(sums, slots):
    return _Rider(_chip_copies, sums, slots)


def _start(makers):
    for make in makers:
        make().start()


def _finish(sends, arrivals):
    for make in arrivals:
        make().wait_recv()
    for make in sends:
        make().wait_send()


def _half(rows, c):
    return pl.ds(pl.multiple_of(c * rows, 16), rows)


def _gather_weights(split, whole):
    arrs = list(split) + list(whole)
    n, ns = len(arrs), len(split)

    def body(*refs):
        outs = refs[n:2 * n]
        send_sems, recv_sems, fsend_sems, frecv_sems = refs[2 * n:]
        sends, arrivals = _gather_ici(outs, ns, send_sems, recv_sems)
        passes, passed = _gather_d2d(outs[:ns], fsend_sems, frecv_sems)
        _start(sends)
        for k, make in enumerate(arrivals):
            make().wait_recv()
            if k < 3 * ns:
                passes[k]().start()
        _finish(sends + passes, passed)

    return pl.pallas_call(
        body, name="gather_weights",
        in_specs=[ANY] * n, out_specs=[ANY] * n,
        out_shape=[jax.ShapeDtypeStruct(a.shape, a.dtype) for a in arrs],
        input_output_aliases={i: i for i in range(n)},
        scratch_shapes=[pltpu.SemaphoreType.DMA((3 * n,)), pltpu.SemaphoreType.DMA((3 * n,)),
                        pltpu.SemaphoreType.DMA((3 * ns,)), pltpu.SemaphoreType.DMA((3 * ns,))],
    )(*arrs)


def _gather_ici(stacks, n_split, send_sems, recv_sems):
    x, y, c, me, chips = _place()

    def region(i, chip):
        if i < n_split:
            return stacks[i].at[chip, _half(stacks[i].shape[1] // 2, c)]
        return stacks[i].at[chip]

    pairs = []
    for i in range(len(stacks)):
        for p, (cx, cy) in enumerate(chips):
            k = 3 * i + p
            mine, got = region(i, me), region(i, 2 * cx + cy)
            sems, to = (send_sems.at[k], recv_sems.at[k]), (cx, cy, c)
            pairs.append(((mine, mine, *sems, to), (got, got, *sems, to)))
    return _copy_plan(pairs)


def _gather_d2d(stacks, send_sems, recv_sems):
    x, y, c, _, chips = _place()
    sibling = (x, y, 1 - c)
    pairs = []
    for i, stack in enumerate(stacks):
        rows = stack.shape[1] // 2
        for p, (cx, cy) in enumerate(chips):
            k = 3 * i + p
            got, theirs = stack.at[2 * cx + cy, _half(rows, c)], stack.at[2 * cx + cy, _half(rows, 1 - c)]
            sems = (send_sems.at[k], recv_sems.at[k])
            pairs.append(((got, got, *sems, sibling), (theirs, theirs, *sems, sibling)))
    return _copy_plan(pairs)


def _forward_weights(split, name):
    n = len(split)

    def body(*refs):
        sends, arrivals = _gather_d2d(refs[n:2 * n], *refs[2 * n:])
        _start(sends)
        _finish(sends, arrivals)

    return pl.pallas_call(
        body, name=name,
        in_specs=[ANY] * n, out_specs=[ANY] * n,
        out_shape=[jax.ShapeDtypeStruct(a.shape, a.dtype) for a in split],
        input_output_aliases={i: i for i in range(n)},
        scratch_shapes=[pltpu.SemaphoreType.DMA((3 * n,))] * 2,
    )(*split)


def _pair_exchange(grads, name):
    n = len(grads)

    def body(*refs):
        ins, theirs = refs[:n], refs[n:2 * n]
        send_sems, recv_sems = refs[2 * n:]
        x, y, c, _, _ = _place()
        sibling = (x, y, 1 - c)
        sends = [_remote(ins[k].at[:, _half(grads[k].shape[1] // 2, 1 - c)], theirs[k],
                         send_sems.at[k], recv_sems.at[k], sibling) for k in range(n)]
        for cp in sends:
            cp.start()
        for k in range(n):
            _remote(theirs[k], theirs[k], send_sems.at[k], recv_sems.at[k], sibling).wait_recv()
        for cp in sends:
            cp.wait_send()

    return pl.pallas_call(
        body, name=name,
        in_specs=[ANY] * n, out_specs=[ANY] * n,
        out_shape=[jax.ShapeDtypeStruct((g.shape[0], g.shape[1] // 2, g.shape[2]), g.dtype) for g in grads],
        scratch_shapes=[pltpu.SemaphoreType.DMA((n,))] * 2,
    )(*grads)


def _chip_exchange(sums, slots):
    n = len(sums)

    def body(*refs):
        sends, arrivals = _chip_copies(refs[:n], refs[2 * n:3 * n], *refs[3 * n:])
        _start(sends)
        _finish(sends, arrivals)

    return pl.pallas_call(
        body, name="grad_chip_exchange",
        in_specs=[ANY] * (2 * n), out_specs=[ANY] * n,
        out_shape=[jax.ShapeDtypeStruct(a.shape, a.dtype) for a in slots],
        input_output_aliases={n + k: k for k in range(n)},
        scratch_shapes=[pltpu.SemaphoreType.DMA((3 * n,)), pltpu.SemaphoreType.DMA((3 * n,))],
    )(*sums, *slots)


def _chip_copies(sums, slots, send_sems, recv_sems):
    x, y, c, me, chips = _place()
    pairs = []
    for k in range(len(sums)):
        for p, (cx, cy) in enumerate(chips):
            j = 3 * k + p
            got = slots[k].at[2 * cx + cy]
            sems, to = (send_sems.at[j], recv_sems.at[j]), (cx, cy, c)
            pairs.append(((sums[k].at[2 * cx + cy], slots[k].at[me], *sems, to), (got, got, *sems, to)))
    return _copy_plan(pairs)


def _half_swap(halves):
    n = len(halves)

    def body(*refs):
        outs = refs[n:2 * n]
        send_sems, recv_sems = refs[2 * n:]
        x, y, c, _, _ = _place()
        sibling = (x, y, 1 - c)
        sends = [_remote(outs[k].at[c], outs[k].at[c], send_sems.at[k], recv_sems.at[k], sibling) for k in range(n)]
        for cp in sends:
            cp.start()
        for k in range(n):
            got = outs[k].at[1 - c]
            _remote(got, got, send_sems.at[k], recv_sems.at[k], sibling).wait_recv()
        for cp in sends:
            cp.wait_send()

    return pl.pallas_call(
        body, name="grad_half_swap",
        in_specs=[ANY] * n, out_specs=[ANY] * n,
        out_shape=[jax.ShapeDtypeStruct(a.shape, a.dtype) for a in halves],
        input_output_aliases={k: k for k in range(n)},
        scratch_shapes=[pltpu.SemaphoreType.DMA((n,))] * 2,
    )(*halves)


def _small_rider(stack):
    n_dev = 2 * N_CHIPS

    def plan(_, stacks, send_sems, recv_sems):
        x, y, c, _, _ = _place()
        mine = stacks[0].at[4 * x + 2 * y + c]
        pairs = []
        for k in range(1, n_dev):
            px, py, pc = x ^ ((k >> 2) & 1), y ^ ((k >> 1) & 1), c ^ (k & 1)
            got = stacks[0].at[4 * px + 2 * py + pc]
            sems = (send_sems.at[k - 1], recv_sems.at[k - 1])
            pairs.append(((mine, mine, *sems, (px, py, pc)), (got, got, *sems, (px, py, pc))))
        return _copy_plan(pairs)

    return _Rider(plan, [], [stack], n_dev - 1)


def _row_tile(r):
    return r // 4 if r >= 256 and (r // 4) % 16 == 0 else r


def _prefetch_call(body, name, grid, in_specs, out_specs, out_shape):
    spec = pltpu.PrefetchScalarGridSpec(num_scalar_prefetch=1, grid=grid, in_specs=in_specs, out_specs=out_specs)
    return pl.pallas_call(body, name=name, grid_spec=spec, out_shape=out_shape,
                          compiler_params=_params(("arbitrary",) * len(grid)))


def _place_shard(w2d, where, dtype, name, by_device=False):
    r, c = w2d.shape
    tr = _row_tile(r)
    slots = 2 * N_CHIPS if by_device else N_CHIPS
    slot = (lambda s: 2 * s[1] + s[0]) if by_device else (lambda s: s[1])

    def body(where_ref, w_ref, out_ref):
        out_ref[...] = w_ref[...].astype(dtype)

    return _prefetch_call(
        body, name, (r // tr,), [pl.BlockSpec((tr, c), lambda i, s: (i, 0))],
        pl.BlockSpec((None, tr, c), lambda i, s: (slot(s), i, 0)),
        jax.ShapeDtypeStruct((slots, r, c), dtype))(where, w2d)


def _place_shards(w2ds, where, name):
    n = len(w2ds)
    steps = N_CHIPS
    assert all(w.shape[0] % (16 * steps) == 0 for w in w2ds)

    def body(where_ref, *refs):
        for k in range(n):
            refs[n + k][...] = refs[k][...].astype(BF16)

    tile = lambda w: (w.shape[0] // steps, w.shape[1])
    return _prefetch_call(
        body, name, (steps,), [pl.BlockSpec(tile(w), lambda i, s: (i, 0)) for w in w2ds],
        [pl.BlockSpec((None,) + tile(w), lambda i, s: (s[1], i, 0)) for w in w2ds],
        [jax.ShapeDtypeStruct((N_CHIPS,) + w.shape, BF16) for w in w2ds])(where, *w2ds)


def _pair_sum(fulls, theirs, where, name):
    n = len(fulls)

    def body(where_ref, *refs):
        for k in range(n):
            a_ref, b_ref, out_ref, own_ref = refs[k], refs[n + k], refs[2 * n + k], refs[3 * n + k]
            total = (a_ref[...].astype(F32) + b_ref[...].astype(F32)).astype(BF16)
            out_ref[...] = total

            @pl.when(pl.program_id(0) == where_ref[1])
            def _():
                own_ref[...] = total

    half = lambda t: pl.BlockSpec((None,) + t.shape[1:], lambda j, s: (j, s[0], 0))
    blk = lambda t: pl.BlockSpec((None,) + t.shape[1:], lambda j, s: (j, 0, 0))
    own = lambda t: pl.BlockSpec((None,) + t.shape[1:], lambda j, s: (s[1], 0, 0))
    shapes = [jax.ShapeDtypeStruct(t.shape, BF16) for t in theirs]
    outs = _prefetch_call(
        body, name, (N_CHIPS,), [half(t) for t in theirs] + [blk(t) for t in theirs],
        [blk(t) for t in theirs] + [own(t) for t in theirs], shapes + shapes)(where, *fulls, *theirs)
    return outs[:n], outs[n:]


def _chip_sum(slots, where, name):
    n = len(slots)
    steps = 2
    assert all(a.shape[1] % (16 * steps) == 0 for a in slots)

    def body(where_ref, *refs):
        for k in range(n):
            a_ref, out_ref = refs[k], refs[n + k]
            total = a_ref[0].astype(F32)
            for j in range(1, a_ref.shape[0]):
                total = total + a_ref[j].astype(F32)
            out_ref[...] = total

    tile = lambda a: (a.shape[1] // steps, a.shape[2])
    return _prefetch_call(
        body, name, (steps,), [pl.BlockSpec((a.shape[0],) + tile(a), lambda i, s: (0, i, 0)) for a in slots],
        [pl.BlockSpec((None,) + tile(a), lambda i, s: (s[0], i, 0)) for a in slots],
        [jax.ShapeDtypeStruct((2,) + a.shape[1:], F32) for a in slots])(where, *slots)


def _slot_sum(a, name):
    nb, r, c = a.shape
    tr = _row_tile(r)

    def body(a_ref, out_ref):
        total = a_ref[0].astype(F32)
        for j in range(1, nb):
            total = total + a_ref[j].astype(F32)
        out_ref[...] = total

    return pl.pallas_call(
        body, name=name, grid=(r // tr,),
        in_specs=[pl.BlockSpec((nb, tr, c), lambda i: (0, i, 0))],
        out_specs=pl.BlockSpec((tr, c), lambda i: (i, 0)),
        out_shape=jax.ShapeDtypeStruct((r, c), F32), compiler_params=_params(("arbitrary",)),
    )(a)


def _adamw(ws, gs, ms, vs, name, steps=1):
    n = len(ws)
    c1 = 1.0 - ADAM_B1 ** ADAM_STEP
    c2 = 1.0 - ADAM_B2 ** ADAM_STEP
    assert all(w.shape[0] % steps == 0 and (steps == 1 or w.shape[0] // steps % 8 == 0) for w in ws)

    def body(*refs):
        for k in range(n):
            w_ref, g_ref, m_ref, v_ref = (refs[j * n + k] for j in range(4))
            d_ref, m2_ref, v2_ref = (refs[(4 + j) * n + k] for j in range(3))
            gv = g_ref[...]
            m2 = ADAM_B1 * m_ref[...] + (1.0 - ADAM_B1) * gv
            v2 = ADAM_B2 * v_ref[...] + (1.0 - ADAM_B2) * (gv * gv)
            m2_ref[...] = m2
            v2_ref[...] = v2
            d_ref[...] = -ADAM_LR * ((m2 / c1) / (jnp.sqrt(v2 / c2) + ADAM_EPS) + ADAM_WD * w_ref[...])

    blks = [pl.BlockSpec((w.shape[0] // steps, w.shape[1]), lambda i: (i, 0)) for w in ws]
    shapes = [jax.ShapeDtypeStruct(w.shape, F32) for w in ws]
    outs = pl.pallas_call(
        body, name=name, grid=(steps,), in_specs=blks * 4, out_specs=blks * 3, out_shape=shapes * 3,
        compiler_params=_params(("arbitrary",)),
    )(*ws, *gs, *ms, *vs)
    return outs[:n], outs[n:2 * n], outs[2 * n:]


WEIGHTS = ["ffn1_norm", "ffn1_w_gate", "ffn1_w_up", "ffn1_w_down", "mix_norm", "w_in", "conv_w", "conv_b",
           "rg_w_a", "rg_b_a", "rg_w_x", "rg_b_x", "rg_lambda", "q_norm", "k_norm", "rnn_out_norm",
           "attn_out_norm", "w_out", "ffn2_norm", "ffn2_w_gate", "ffn2_w_up", "ffn2_w_down"]
BIG = ["ffn1_w_gate", "ffn1_w_up", "ffn1_w_down", "w_in", "w_out", "ffn2_w_gate", "ffn2_w_up", "ffn2_w_down"]
SMALL = [n for n in WEIGHTS if n not in BIG]
PACK_LANES = 128
PACK_ROW_ALIGN = 8


def _hidden_major(name, a):
    return jnp.transpose(a) if name.endswith(("w_gate", "w_up")) else a


def _pack(parts):
    flat = jnp.concatenate([p.reshape(-1) for p in parts])
    unit = PACK_LANES * PACK_ROW_ALIGN
    padded = -(-flat.shape[0] // unit) * unit
    return jnp.pad(flat, (0, padded - flat.shape[0])).reshape(-1, PACK_LANES)


def _unpack(packed, shapes):
    flat = packed.reshape(-1)
    out, at = [], 0
    for shp in shapes:
        size = math.prod(shp)
        out.append(flat[at:at + size].reshape(shp))
        at += size
    return out


def kernel(x, ffn1_norm, ffn1_w_gate, ffn1_w_up, ffn1_w_down, mix_norm, w_in, conv_w, conv_b, rg_w_a, rg_b_a, rg_w_x, rg_b_x, rg_lambda, q_norm, k_norm, rnn_out_norm, attn_out_norm, w_out, ffn2_norm, ffn2_w_gate, ffn2_w_up, ffn2_w_down, loss_target, m_ffn1_norm, m_ffn1_w_gate, m_ffn1_w_up, m_ffn1_w_down, m_mix_norm, m_w_in, m_conv_w, m_conv_b, m_rg_w_a, m_rg_b_a, m_rg_w_x, m_rg_b_x, m_rg_lambda, m_q_norm, m_k_norm, m_rnn_out_norm, m_attn_out_norm, m_w_out, m_ffn2_norm, m_ffn2_w_gate, m_ffn2_w_up, m_ffn2_w_down, v_ffn1_norm, v_ffn1_w_gate, v_ffn1_w_up, v_ffn1_w_down, v_mix_norm, v_w_in, v_conv_w, v_conv_b, v_rg_w_a, v_rg_b_a, v_rg_w_x, v_rg_b_x, v_rg_lambda, v_q_norm, v_k_norm, v_rnn_out_norm, v_attn_out_norm, v_w_out, v_ffn2_norm, v_ffn2_w_gate, v_ffn2_w_up, v_ffn2_w_down):
    given = dict(locals())
    w = {n: given[n] for n in WEIGHTS}
    m = {n: given["m_" + n] for n in WEIGHTS}
    v = {n: given["v_" + n] for n in WEIGHTS}
    chip = 2 * lax.axis_index("x") + lax.axis_index("y")

    where = jnp.stack([lax.axis_index("c"), chip]).astype(jnp.int32)

    stacks = dict(zip(BIG, _place_shards([_hidden_major(n, w[n][0]) for n in BIG], where, "place_weights")))
    conv_stack = _place_shard(w["conv_w"][0], where, F32, "place_conv_w")
    small = {n: (w[n][0] if w[n].ndim > 2 else w[n]) for n in SMALL if n != "conv_w"}

    grad_x, slots, gs, everyone = _local_step(x[0], loss_target[0], stacks, conv_stack, small, where)

    swapped = _half_swap(_chip_sum([slots[n] for n in BIG], where, "chip_sums"))
    g2s = [t.reshape(t.shape[0] * t.shape[1], t.shape[2]) for t in swapped]
    flat = lambda tree: [_hidden_major(n, tree[n][0]) for n in BIG]
    d2s, m2s, v2s = _adamw(flat(w), g2s, flat(m), flat(v), "adamw_weights", ADAMW_STEPS)
    grads, deltas, new_m, new_v = {}, {}, {}, {}
    for tree, parts in ((grads, g2s), (deltas, d2s), (new_m, m2s), (new_v, v2s)):
        tree.update({n: _hidden_major(n, a).reshape(w[n].shape) for n, a in zip(BIG, parts)})

    full_shapes = [gs[n].shape for n in SMALL]
    *summed, loss = _unpack(_slot_sum(everyone, "small_grad_sum"), full_shapes + [(1, 1)])
    g_parts = dict(zip(SMALL, summed))
    quarter = D_RNN // N_CHIPS
    g_parts["conv_w"] = lax.dynamic_slice_in_dim(g_parts["conv_w"], chip * quarter, quarter, axis=1)
    local_shapes = [w[n].shape for n in SMALL]
    pk = lambda tree: _pack([tree[n] for n in SMALL])
    (d_s,), (m_s,), (v_s,) = _adamw([pk(w)], [pk(g_parts)], [pk(m)], [pk(v)], "adamw_small")
    for tree, packed in ((grads, pk(g_parts)), (deltas, d_s), (new_m, m_s), (new_v, v_s)):
        tree.update(zip(SMALL, _unpack(packed, local_shapes)))

    return (loss[0, 0], grad_x.reshape(x.shape), *[grads[n] for n in WEIGHTS], *[deltas[n] for n in WEIGHTS],
            *[new_m[n] for n in WEIGHTS], *[new_v[n] for n in WEIGHTS])
```

```python
import functools
import math

import jax
import jax.numpy as jnp
from jax import lax
from jax.experimental import pallas as pl
from jax.experimental.pallas import tpu as pltpu

F32 = jnp.float32
BF16 = jnp.bfloat16
MESH = pl.DeviceIdType.MESH

D_MODEL = 1024
N_CHIPS = 4
D_RNN = 512
D_ATT = 512
N_HEADS = 8
HEAD_DIM = 64
RNN_BLOCKS = 8
CONV_W = 4
RG_C = 8.0
N_IN = 2 * D_RNN + 3 * D_ATT
EPS = 1e-6
ATT_BLOCK = 128
ATT_WINDOW = 384
ATT_SPLIT = 256
EXP_ZERO = -105.0

ADAM_LR = 0.001
ADAM_B1 = 0.9
ADAM_B2 = 0.999
ADAM_EPS = 1e-08
ADAM_WD = 0.01
ADAM_STEP = 10

V7X_VMEM_LIMIT = 56 * 1024 * 1024
V7X_MXU_WIDTH = 256
TOKEN_TILE = 512
SUBLANES = 8
FFN_TILE = 256
WGRAD_TILE = 2048
WHOLE_TILE = 1024
ADAMW_STEPS = 8

GELU_K0 = math.sqrt(2.0 / math.pi)
GELU_K1 = 0.044715


def _params(sem=None):
    return pltpu.CompilerParams(dimension_semantics=sem, vmem_limit_bytes=V7X_VMEM_LIMIT)


def _dot(a, b):
    return jnp.dot(a, b, preferred_element_type=F32)


def _dot_nt(a, b):
    return lax.dot_general(a, b, (((1,), (1,)), ((), ())), preferred_element_type=F32)


def _dot_tn(a, b):
    return lax.dot_general(a, b, (((0,), (0,)), ((), ())), preferred_element_type=F32)


def _sigmoid(x):
    return 1.0 / (1.0 + jnp.exp(-x))


def _rms_r(xv):
    return lax.rsqrt(jnp.mean(xv * xv, axis=-1, keepdims=True) + EPS)


def _rms_bwd(xv, r, nw, dh):
    t = dh * nw
    dx = r * t - xv * (r * r * r * jnp.mean(t * xv, axis=-1, keepdims=True))
    dn = jnp.sum(dh * xv * r, axis=0, keepdims=True)
    return dx, dn


def _gelu(x):
    t = jnp.tanh(GELU_K0 * (x + GELU_K1 * x * x * x))
    return 0.5 * x * (1.0 + t)


def _gelu_grad(x):
    t = jnp.tanh(GELU_K0 * (x + GELU_K1 * x * x * x))
    return 0.5 * (1.0 + t) + 0.5 * x * (1.0 - t * t) * (GELU_K0 * (1.0 + 3.0 * GELU_K1 * x * x))


def _expm1_neg(x):
    p = 1.0 + x * (1.0 / 6.0)
    for k in (5.0, 4.0, 3.0, 2.0):
        p = 1.0 + x * (1.0 / k) * p
    return jnp.where(x > -0.25, x * p, jnp.exp(x) - 1.0)


def _log_sigmoid(x):
    return jnp.minimum(x, 0.0) - jnp.log(1.0 + jnp.exp(-jnp.abs(x)))


def _tile(s):
    return min(TOKEN_TILE, s)


def _ffn_chunks(f):
    cut = f // 2 // V7X_MXU_WIDTH * V7X_MXU_WIDTH
    return ((0, cut), (cut, f)) if 0 < cut < f else ((0, f),)


def _ffn_fwd(x, nw, wg, wu, wd, tgt=None, rider=None):
    s, d = x.shape
    f = wg.shape[0]
    tm = min(FFN_TILE, s)
    ni = s // tm
    assert s % tm == 0
    with_loss = tgt is not None
    n_in, n_out = 5 + with_loss, 5 + with_loss

    def body(*refs):
        ins, outs, _, copies = _split_refs(refs, n_in, n_out, rider)
        x_ref, nw_ref, wg_ref, wu_ref, wd_ref = ins[:5]
        out_ref, g_ref, u_ref, hb_ref, ab_ref = outs[:5]
        i = pl.program_id(0)
        finish = _ride(copies, i == 0, i == ni - 1, i == 3 * ni // 4)

        xv = x_ref[...]
        hb = (xv * _rms_r(xv) * nw_ref[...]).astype(BF16)
        hb_ref[...] = hb
        y = jnp.zeros((tm, d), F32)
        for lo, hi in _ffn_chunks(f):
            g = _dot_nt(hb, wg_ref[lo:hi, :])
            u = _dot_nt(hb, wu_ref[lo:hi, :])
            g_ref[:, lo:hi] = g.astype(BF16)
            u_ref[:, lo:hi] = u.astype(BF16)
            ab = (g * _sigmoid(g) * u).astype(BF16)
            ab_ref[:, lo:hi] = ab
            y = y + _dot(ab, wd_ref[lo:hi, :])
        y = xv + 0.5 * y
        if with_loss:
            tgt_ref, loss_ref = ins[5], outs[5]
            diff = y - tgt_ref[...]
            out_ref[...] = diff * (1.0 / d)

            @pl.when(i == 0)
            def _():
                loss_ref[...] = jnp.zeros_like(loss_ref)

            loss_ref[...] += jnp.sum(diff * diff) * (0.5 / d)
        else:
            out_ref[...] = y
        finish()

    row = pl.BlockSpec((tm, d), lambda i: (i, 0))
    weight = pl.BlockSpec((f, d), lambda i: (0, 0), pipeline_mode=pl.Buffered(1))
    in_specs = [row, pl.BlockSpec((1, d), lambda i: (0, 0)), weight, weight, weight]
    args = [x, nw, wg, wu, wd]
    if with_loss:
        in_specs.append(row)
        args.append(tgt)
    blk = pl.BlockSpec((tm, f), lambda i: (i, 0))
    out_shape = [jax.ShapeDtypeStruct((s, d), F32), jax.ShapeDtypeStruct((s, f), BF16),
                 jax.ShapeDtypeStruct((s, f), BF16), jax.ShapeDtypeStruct((s, d), BF16),
                 jax.ShapeDtypeStruct((s, f), BF16)]
    out_specs = [row, blk, blk, row, blk]
    if with_loss:
        out_shape.append(jax.ShapeDtypeStruct((1, 128), F32))
        out_specs.append(pl.BlockSpec((1, 128), lambda i: (0, 0)))
    return _call(body, "ffn_fwd_loss" if with_loss else "ffn_fwd", (ni,), in_specs, out_specs, out_shape, args,
                 rider=rider)


def _call(body, name, grid, in_specs, out_specs, out_shape, args, scratch=(), rider=None):
    in_specs, out_specs, out_shape, scratch = list(in_specs), list(out_specs), list(out_shape), list(scratch)
    extra, aliases = [], {}
    if rider is not None:
        extra = rider.operands()
        aliases = rider.aliases(len(args), len(out_shape))
        in_specs += [ANY] * len(extra)
        out_specs += [ANY] * len(rider.inplace)
        out_shape += rider.out_shape()
        scratch += rider.scratch()
    return pl.pallas_call(
        body, name=name, grid=grid, in_specs=in_specs, out_specs=out_specs, out_shape=out_shape,
        input_output_aliases=aliases, scratch_shapes=scratch,
        compiler_params=_params(("arbitrary",) * len(grid)),
    )(*args, *extra)


def _ffn_bwd_act(x, nw, dy, g, u, wg, wu, wd, name, rider=None):
    s, d = x.shape
    f = wg.shape[0]
    tm = min(FFN_TILE, s)
    assert s % tm == 0

    def body(*refs):
        ins, outs, _, copies = _split_refs(refs, 8, 5, rider)
        x_ref, nw_ref, dy_ref, g_ref, u_ref, wg_ref, wu_ref, wd_ref = ins
        dx_ref, dg_ref, du_ref, dyb_ref, dnw_ref = outs
        finish = _ride(copies, pl.program_id(0) == 0, pl.program_id(0) == s // tm - 1)
        dyv = dy_ref[...]
        dyb = dyv.astype(BF16)
        dyb_ref[...] = dyb
        dh = jnp.zeros((tm, d), F32)
        for lo, hi in _ffn_chunks(f):
            da = 0.5 * _dot_nt(dyb, wd_ref[lo:hi, :])
            gv = g_ref[:, lo:hi].astype(F32)
            sg = _sigmoid(gv)
            dub = (da * (gv * sg)).astype(BF16)
            dgb = (da * u_ref[:, lo:hi].astype(F32) * (sg * (1.0 + gv * (1.0 - sg)))).astype(BF16)
            dg_ref[:, lo:hi] = dgb
            du_ref[:, lo:hi] = dub
            dh = dh + _dot(dgb, wg_ref[lo:hi, :]) + _dot(dub, wu_ref[lo:hi, :])
        xv = x_ref[...]
        dx, dn = _rms_bwd(xv, _rms_r(xv), nw_ref[...], dh)
        dx_ref[...] = dyv + dx

        @pl.when(pl.program_id(0) == 0)
        def _():
            dnw_ref[...] = jnp.zeros_like(dnw_ref)

        dnw_ref[...] += dn
        finish()

    row = pl.BlockSpec((tm, d), lambda i: (i, 0))
    vec = pl.BlockSpec((1, d), lambda i: (0, 0))
    blk = pl.BlockSpec((tm, f), lambda i: (i, 0))
    weight = pl.BlockSpec((f, d), lambda i: (0, 0), pipeline_mode=pl.Buffered(1))
    return _call(
        body, name, (s // tm,), [row, vec, row, blk, blk, weight, weight, weight], [row, blk, blk, row, vec],
        [jax.ShapeDtypeStruct((s, d), F32), jax.ShapeDtypeStruct((s, f), BF16),
         jax.ShapeDtypeStruct((s, f), BF16), jax.ShapeDtypeStruct((s, d), BF16),
         jax.ShapeDtypeStruct((1, d), F32)],
        [x, nw, dy, g, u, wg, wu, wd], rider=rider)


def _wgrad(a, b, a_spec, b_spec, out_rows, out_cols, scale, name, tk, rider=None, per_step=1):
    s = a.shape[-2]
    nk = s // tk
    steps = N_CHIPS // per_step
    assert s % tk == 0

    def body(*refs):
        (a_ref, b_ref), (out_ref,), (acc,), copies = _split_refs(refs, 2, 1, rider)
        j, k = pl.program_id(0), pl.program_id(1)
        finish = _ride(copies, jnp.logical_and(j == 0, k == 0), jnp.logical_and(j == steps - 1, k == nk - 1))

        @pl.when(k == 0)
        def _():
            acc[...] = jnp.zeros_like(acc)

        acc[...] += _dot_tn(a_ref[...], b_ref[...])

        @pl.when(k == nk - 1)
        def _():
            for t in range(per_step):
                out_ref[t] = (acc[t * out_rows:(t + 1) * out_rows, :] * scale).astype(BF16)

        finish()

    outs = _call(
        body, name, (steps, nk), [a_spec(tk), b_spec(tk)],
        [pl.BlockSpec((per_step, out_rows, out_cols), lambda j, k: (j, 0, 0))],
        [jax.ShapeDtypeStruct((N_CHIPS, out_rows, out_cols), BF16)], [a, b],
        scratch=[pltpu.VMEM((per_step * out_rows, out_cols), F32)], rider=rider)
    return outs[0] if rider is None else outs


def _wgrad_whole(a, b, col_blocks, name, rider=None):
    s, m = a.shape
    n = b.shape[1]
    tk = min(WHOLE_TILE, s)
    nk = s // tk
    assert s % tk == 0
    out_shape = (N_CHIPS, m, n // N_CHIPS) if col_blocks else (N_CHIPS, m // N_CHIPS, n)

    def body(*refs):
        (a_ref, b_ref), (out_ref,), (acc,), copies = _split_refs(refs, 2, 1, rider)
        k = pl.program_id(0)
        finish = _ride(copies, k == 0, k == nk - 1)

        @pl.when(k == 0)
        def _():
            acc[...] = jnp.zeros_like(acc)

        acc[...] += _dot_tn(a_ref[...], b_ref[...])

        @pl.when(k == nk - 1)
        def _():
            for j in range(N_CHIPS):
                if col_blocks:
                    out_ref[j] = acc[:, j * out_shape[2]:(j + 1) * out_shape[2]].astype(BF16)
                else:
                    out_ref[j] = acc[j * out_shape[1]:(j + 1) * out_shape[1], :].astype(BF16)

        finish()

    outs = _call(
        body, name, (nk,), [pl.BlockSpec((tk, m), lambda k: (k, 0)), pl.BlockSpec((tk, n), lambda k: (k, 0))],
        [pl.BlockSpec(out_shape, lambda k: (0, 0, 0))], [jax.ShapeDtypeStruct(out_shape, BF16)], [a, b],
        scratch=[pltpu.VMEM((m, n), F32)], rider=rider)
    return outs[0] if rider is None else outs


def _ffn_wgrad(hidden, shared, scale, name, rider=None):
    s, d = shared.shape
    half = hidden.shape[1] // 2
    return _wgrad(hidden, shared, lambda tk: pl.BlockSpec((tk, half), lambda j, k: (k, j)),
                  lambda tk: pl.BlockSpec((tk, d), lambda j, k: (k, 0)), half // 2, d, scale, name,
                  min(WGRAD_TILE, s), rider, per_step=2)


def _mix_pre(x, nw, win):
    s, d = x.shape
    nb, _, cb = win.shape
    tm = min(FFN_TILE, s)
    assert s % tm == 0

    def body(x_ref, nw_ref, w_ref, p_ref, hb_ref):
        xv = x_ref[...]
        hb = (xv * _rms_r(xv) * nw_ref[...]).astype(BF16)
        hb_ref[...] = hb
        for j in range(nb):
            p_ref[:, j * cb:(j + 1) * cb] = _dot(hb, w_ref[j])

    row = pl.BlockSpec((tm, d), lambda i: (i, 0))
    return pl.pallas_call(
        body, name="mix_pre", grid=(s // tm,),
        in_specs=[row, pl.BlockSpec((1, d), lambda i: (0, 0)),
                  pl.BlockSpec((nb, d, cb), lambda i: (0, 0, 0), pipeline_mode=pl.Buffered(1))],
        out_specs=[pl.BlockSpec((tm, nb * cb), lambda i: (i, 0)), row],
        out_shape=[jax.ShapeDtypeStruct((s, nb * cb), F32), jax.ShapeDtypeStruct((s, d), BF16)],
        compiler_params=_params(("arbitrary",)),
    )(x, nw, win)


def _mix_pre_bwd(x, nw, dres, dpb, win):
    s, d = x.shape
    nb, _, cb = win.shape
    tm = min(FFN_TILE, s)
    assert s % tm == 0

    def body(x_ref, nw_ref, dres_ref, dp_ref, w_ref, dx_ref, dnw_ref):
        dh = jnp.zeros((tm, d), F32)
        for j in range(nb):
            dh = dh + _dot_nt(dp_ref[:, j * cb:(j + 1) * cb], w_ref[j])
        xv = x_ref[...]
        dx, dn = _rms_bwd(xv, _rms_r(xv), nw_ref[...], dh)
        dx_ref[...] = dres_ref[...] + dx

        @pl.when(pl.program_id(0) == 0)
        def _():
            dnw_ref[...] = jnp.zeros_like(dnw_ref)

        dnw_ref[...] += dn

    row = pl.BlockSpec((tm, d), lambda i: (i, 0))
    vec = pl.BlockSpec((1, d), lambda i: (0, 0))
    return pl.pallas_call(
        body, name="mix_pre_bwd", grid=(s // tm,),
        in_specs=[row, vec, row, pl.BlockSpec((tm, nb * cb), lambda i: (i, 0)),
                  pl.BlockSpec((nb, d, cb), lambda i: (0, 0, 0), pipeline_mode=pl.Buffered(1))],
        out_specs=[row, vec],
        out_shape=[jax.ShapeDtypeStruct((s, d), F32), jax.ShapeDtypeStruct((1, d), F32)],
        compiler_params=_params(("arbitrary",)),
    )(x, nw, dres, dpb, win)


def _mix_post(x, yr, ya, nr, na, wout):
    s, d = x.shape
    h = yr.shape[1]
    tm = _tile(s)

    def body(x_ref, yr_ref, ya_ref, nr_ref, na_ref, w_ref, out_ref):
        yrv = yr_ref[...]
        yav = ya_ref[...]
        onb = (yrv * _rms_r(yrv) * nr_ref[...]).astype(BF16)
        oab = (yav * _rms_r(yav) * na_ref[...]).astype(BF16)
        out_ref[...] = x_ref[...] + _dot(onb, w_ref[0:h, :]) + _dot(oab, w_ref[h:2 * h, :])

    row = pl.BlockSpec((tm, d), lambda i: (i, 0))
    half = pl.BlockSpec((tm, h), lambda i: (i, 0))
    vec = pl.BlockSpec((1, h), lambda i: (0, 0))
    return pl.pallas_call(
        body, name="mix_post", grid=(s // tm,),
        in_specs=[row, half, half, vec, vec, pl.BlockSpec((2 * h, d), lambda i: (0, 0))],
        out_specs=row, out_shape=jax.ShapeDtypeStruct((s, d), F32),
        compiler_params=_params(("arbitrary",)),
    )(x, yr, ya, nr, na, wout)


def _mix_post_bwd(dx, yr, ya, nr, na, wout):
    s, d = dx.shape
    h = yr.shape[1]
    tm = _tile(s)

    def body(dx_ref, yr_ref, ya_ref, nr_ref, na_ref, w_ref,
             dyr_ref, dya_ref, yc_ref, dxb_ref, dnr_ref, dna_ref):
        i = pl.program_id(0)
        dxb = dx_ref[...].astype(BF16)
        dxb_ref[...] = dxb
        dyc = _dot_nt(dxb, w_ref[...])
        yrv = yr_ref[...]
        yav = ya_ref[...]
        rr = _rms_r(yrv)
        ra = _rms_r(yav)
        yc_ref[:, 0:h] = (yrv * rr * nr_ref[...]).astype(BF16)
        yc_ref[:, h:2 * h] = (yav * ra * na_ref[...]).astype(BF16)
        dyr, dnr = _rms_bwd(yrv, rr, nr_ref[...], dyc[:, 0:h])
        dya, dna = _rms_bwd(yav, ra, na_ref[...], dyc[:, h:2 * h])
        dyr_ref[...] = dyr
        dya_ref[...] = dya

        @pl.when(i == 0)
        def _():
            dnr_ref[...] = jnp.zeros_like(dnr_ref)
            dna_ref[...] = jnp.zeros_like(dna_ref)

        dnr_ref[...] += dnr
        dna_ref[...] += dna

    row = pl.BlockSpec((tm, d), lambda i: (i, 0))
    half = pl.BlockSpec((tm, h), lambda i: (i, 0))
    vec = pl.BlockSpec((1, h), lambda i: (0, 0))
    return pl.pallas_call(
        body, name="mix_post_bwd", grid=(s // tm,),
        in_specs=[row, half, half, vec, vec, pl.BlockSpec((2 * h, d), lambda i: (0, 0))],
        out_specs=[half, half, pl.BlockSpec((tm, 2 * h), lambda i: (i, 0)), row, vec, vec],
        out_shape=[jax.ShapeDtypeStruct((s, h), F32), jax.ShapeDtypeStruct((s, h), F32),
                   jax.ShapeDtypeStruct((s, 2 * h), BF16), jax.ShapeDtypeStruct((s, d), BF16),
                   jax.ShapeDtypeStruct((1, h), F32), jax.ShapeDtypeStruct((1, h), F32)],
        compiler_params=_params(("arbitrary",)),
    )(dx, yr, ya, nr, na, wout)


def _shift_down(xv, s, prev8):
    rolled = pltpu.roll(xv, s, 0)
    row8 = lax.broadcasted_iota(jnp.int32, prev8.shape, 0)
    head = jnp.where(row8 < s, pltpu.roll(prev8, s, 0), rolled[0:8, :])
    return jnp.concatenate([head, rolled[8:, :]], axis=0)


def _shift_up(xv, s, next8):
    n = xv.shape[0]
    rolled = pltpu.roll(xv, n - s, 0)
    row8 = lax.broadcasted_iota(jnp.int32, next8.shape, 0)
    tail = jnp.where(row8 >= 8 - s, pltpu.roll(next8, 8 - s, 0), rolled[n - 8:, :])
    return jnp.concatenate([rolled[:n - 8, :], tail], axis=0)


def _scan_fwd(a, b):
    n = a.shape[0]
    sub = lax.broadcasted_iota(jnp.int32, a.shape, 0) % SUBLANES
    s = 1
    while s < SUBLANES:
        ok = sub >= s
        b = jnp.where(ok, a * pltpu.roll(b, s, 0) + b, b)
        a = jnp.where(ok, a * pltpu.roll(a, s, 0), a)
        s *= 2
    groups = []
    before = jnp.zeros((1, a.shape[1]), F32)
    for g in range(n // SUBLANES):
        rows = slice(g * SUBLANES, (g + 1) * SUBLANES)
        groups.append(a[rows] * before + b[rows])
        before = groups[-1][SUBLANES - 1:]
    return jnp.concatenate(groups, axis=0)


def _scan_bwd(a, b):
    n = a.shape[0]
    sub = lax.broadcasted_iota(jnp.int32, a.shape, 0) % SUBLANES
    s = 1
    while s < SUBLANES:
        ok = sub < SUBLANES - s
        b = jnp.where(ok, a * pltpu.roll(b, n - s, 0) + b, b)
        a = jnp.where(ok, a * pltpu.roll(a, n - s, 0), a)
        s *= 2
    groups = []
    after = jnp.zeros((1, a.shape[1]), F32)
    for g in reversed(range(n // SUBLANES)):
        rows = slice(g * SUBLANES, (g + 1) * SUBLANES)
        groups.append(a[rows] * after + b[rows])
        after = groups[-1][:1]
    return jnp.concatenate(groups[::-1], axis=0)


def _rglru_gates(xv, prev8, cw_ref, cb_ref, wa_ref, ba_ref, wx_ref, bx_ref, lam_ref):
    x1 = _shift_down(xv, 1, prev8)
    x2 = _shift_down(xv, 2, prev8)
    x3 = _shift_down(xv, 3, prev8)
    xc = cw_ref[3:4, :] * xv + cw_ref[2:3, :] * x1 + cw_ref[1:2, :] * x2 + cw_ref[0:1, :] * x3 + cb_ref[...]
    xcb = xc.astype(BF16)
    r = _sigmoid(_dot(xcb, wa_ref[...]) + ba_ref[...])
    ig = _sigmoid(_dot(xcb, wx_ref[...]) + bx_ref[...])
    c = RG_C * _log_sigmoid(lam_ref[...])
    la = r * c
    a = jnp.exp(la)
    m = jnp.sqrt(-_expm1_neg(2.0 * la))
    return (x1, x2, x3), xc, xcb, r, ig, c, a, m


def _rglru_fwd(proj, cw, cb, wa, ba, wx, bx, lam):
    s = proj.shape[0]
    w = D_RNN
    tm = _tile(s)

    def body(xr_ref, gate_ref, cw_ref, cb_ref, wa_ref, ba_ref, wx_ref, bx_ref, lam_ref,
             y_ref, h_ref, prev, hlast):
        @pl.when(pl.program_id(0) == 0)
        def _():
            prev[...] = jnp.zeros_like(prev)
            hlast[...] = jnp.zeros_like(hlast)

        xv = xr_ref[...]
        _, xc, _, _, ig, _, a, m = _rglru_gates(xv, prev[...], cw_ref, cb_ref, wa_ref, ba_ref,
                                                wx_ref, bx_ref, lam_ref)
        b = m * (ig * xc)
        row = lax.broadcasted_iota(jnp.int32, b.shape, 0)
        b = jnp.where(row == 0, b + a * hlast[...], b)
        h = _scan_fwd(a, b)
        h_ref[...] = h
        y_ref[...] = h * _gelu(gate_ref[...])
        prev[...] = xv[tm - 8:, :]
        hlast[...] = h[tm - 1:tm, :]

    vec = pl.BlockSpec((1, w), lambda i: (0, 0))
    sq = pl.BlockSpec((w, w), lambda i: (0, 0))
    out = pl.BlockSpec((tm, w), lambda i: (i, 0))
    return pl.pallas_call(
        body, name="rglru_fwd", grid=(s // tm,),
        in_specs=[pl.BlockSpec((tm, w), lambda i: (i, 0)), pl.BlockSpec((tm, w), lambda i: (i, 1)),
                  pl.BlockSpec((CONV_W, w), lambda i: (0, 0)), vec, sq, vec, sq, vec, vec],
        out_specs=[out, out],
        out_shape=[jax.ShapeDtypeStruct((s, w), F32), jax.ShapeDtypeStruct((s, w), F32)],
        scratch_shapes=[pltpu.VMEM((8, w), F32), pltpu.VMEM((1, w), F32)],
        compiler_params=_params(("arbitrary",)),
    )(proj, proj, cw, cb, wa, ba, wx, bx, lam)


def _rglru_bwd(proj, hseq, dyr, cw, cb, wa, ba, wx, bx, lam):
    s = proj.shape[0]
    w = D_RNN
    tm = _tile(s)
    nt = s // tm
    t8 = tm // 8

    def body(xr_ref, xp_ref, gate_ref, h_ref, hp_ref, dy_ref, cw_ref, cb_ref, wa_ref, ba_ref,
             wx_ref, bx_ref, lam_ref,
             dxr_ref, dgate_ref, dcw_ref, dcb_ref, dwa_ref, dba_ref, dwx_ref, dbx_ref, dlam_ref,
             carry, dxc_next):
        i = pl.program_id(0)
        first_tile = i == nt - 1

        @pl.when(i == 0)
        def _():
            carry[...] = jnp.zeros_like(carry)
            dxc_next[...] = jnp.zeros_like(dxc_next)
            for ref in (dcw_ref, dcb_ref, dwa_ref, dba_ref, dwx_ref, dbx_ref, dlam_ref):
                ref[...] = jnp.zeros_like(ref)

        xv = xr_ref[...]
        prev8 = jnp.where(first_tile, 0.0, xp_ref[...])
        hprev8 = jnp.where(first_tile, 0.0, hp_ref[...])
        (x1, x2, x3), xc, xcb, r, ig, c, a, m = _rglru_gates(
            xv, prev8, cw_ref, cb_ref, wa_ref, ba_ref, wx_ref, bx_ref, lam_ref)
        gv = gate_ref[...]
        hv = h_ref[...]
        dy = dy_ref[...]
        dgate_ref[...] = (dy * hv * _gelu_grad(gv)).astype(BF16)
        dh = dy * _gelu(gv)
        row = lax.broadcasted_iota(jnp.int32, dh.shape, 0)
        dh = jnp.where(row == tm - 1, dh + carry[...], dh)
        a_up = jnp.where(row == tm - 1, 0.0, pltpu.roll(a, tm - 1, 0))
        lam_t = _scan_bwd(a_up, dh)
        carry[...] = a[0:1, :] * lam_t[0:1, :]
        hm1 = _shift_down(hv, 1, hprev8)
        da = lam_t * hm1
        ixc = ig * xc
        dm = lam_t * ixc
        dig = lam_t * m * xc
        dxc = lam_t * m * ig
        dla = da * a - dm * (a * a) / m
        dr = dla * c
        dlam_ref[...] += jnp.sum(dla * r, axis=0, keepdims=True)
        dpa = dr * r * (1.0 - r)
        dpi = dig * ig * (1.0 - ig)
        dba_ref[...] += jnp.sum(dpa, axis=0, keepdims=True)
        dbx_ref[...] += jnp.sum(dpi, axis=0, keepdims=True)
        dpab = dpa.astype(BF16)
        dpib = dpi.astype(BF16)
        dwa_ref[...] += _dot_tn(xcb, dpab)
        dwx_ref[...] += _dot_tn(xcb, dpib)
        dxc = dxc + _dot_nt(dpab, wa_ref[...]) + _dot_nt(dpib, wx_ref[...])
        dcb_ref[...] += jnp.sum(dxc, axis=0, keepdims=True)
        dcw_ref[3:4, :] += jnp.sum(dxc * xv, axis=0, keepdims=True)
        dcw_ref[2:3, :] += jnp.sum(dxc * x1, axis=0, keepdims=True)
        dcw_ref[1:2, :] += jnp.sum(dxc * x2, axis=0, keepdims=True)
        dcw_ref[0:1, :] += jnp.sum(dxc * x3, axis=0, keepdims=True)
        nxt = dxc_next[...]
        dxr = (cw_ref[3:4, :] * dxc + cw_ref[2:3, :] * _shift_up(dxc, 1, nxt)
               + cw_ref[1:2, :] * _shift_up(dxc, 2, nxt) + cw_ref[0:1, :] * _shift_up(dxc, 3, nxt))
        dxr_ref[...] = dxr.astype(BF16)
        dxc_next[...] = dxc[0:8, :]

        @pl.when(first_tile)
        def _():
            lv = lam_ref[...]
            dlam_ref[...] = dlam_ref[...] * (RG_C * _sigmoid(-lv))

    rev = lambda i: nt - 1 - i
    vec = pl.BlockSpec((1, w), lambda i: (0, 0))
    sq = pl.BlockSpec((w, w), lambda i: (0, 0))
    cur = lambda col: pl.BlockSpec((tm, w), lambda i: (rev(i), col))
    before = lambda cols: pl.BlockSpec((8, w), lambda i: (jnp.maximum(rev(i) * t8 - 1, 0), 0))
    return pl.pallas_call(
        body, name="rglru_bwd", grid=(nt,),
        in_specs=[cur(0), before(None), cur(1), cur(0), before(None), cur(0),
                  pl.BlockSpec((CONV_W, w), lambda i: (0, 0)), vec, sq, vec, sq, vec, vec],
        out_specs=[cur(0), cur(0), pl.BlockSpec((CONV_W, w), lambda i: (0, 0)), vec, sq, vec, sq, vec, vec],
        out_shape=[jax.ShapeDtypeStruct((s, w), BF16), jax.ShapeDtypeStruct((s, w), BF16),
                   jax.ShapeDtypeStruct((CONV_W, w), F32), jax.ShapeDtypeStruct((1, w), F32),
                   jax.ShapeDtypeStruct((w, w), F32), jax.ShapeDtypeStruct((1, w), F32),
                   jax.ShapeDtypeStruct((w, w), F32), jax.ShapeDtypeStruct((1, w), F32),
                   jax.ShapeDtypeStruct((1, w), F32)],
        scratch_shapes=[pltpu.VMEM((1, w), F32), pltpu.VMEM((8, w), F32)],
        compiler_params=_params(("arbitrary",)),
    )(proj, proj, proj, hseq, hseq, dyr, cw, cb, wa, ba, wx, bx, lam)


def _sb_logs(z, valid):
    l1p = jnp.log(1.0 + jnp.exp(-jnp.abs(z)))
    lb = jnp.minimum(z, 0.0) - l1p
    lm = jnp.where(valid, -jnp.maximum(z, 0.0) - l1p, 0.0)
    return lb, lm


class _Window:
    def __init__(self):
        blk, win, cut = ATT_BLOCK, ATT_WINDOW, ATT_SPLIT
        self.row = lax.broadcasted_iota(jnp.int32, (blk, win), 0)
        self.col = lax.broadcasted_iota(jnp.int32, (blk, win), 1)

        def tri(n, later):
            j = lax.broadcasted_iota(jnp.int32, (n, n), 0)
            s = lax.broadcasted_iota(jnp.int32, (n, n), 1)
            return jnp.where((j > s) if later else (j < s), 1.0, 0.0).astype(BF16)

        self.later = (tri(cut, True), tri(win - cut, True))
        self.earlier = (tri(cut, False), tri(win - cut, False))

    def place(self, qi, g):
        end = (qi + 1) * ATT_BLOCK - g * ATT_WINDOW
        start = pl.multiple_of(jnp.maximum(end - ATT_WINDOW, 0), ATT_BLOCK)
        valid = start + self.col < jnp.minimum(qi * ATT_BLOCK + self.row, end)
        return start, valid

    @staticmethod
    def _parts(xv):
        hi = xv.astype(BF16)
        lo = (xv - hi.astype(F32)).astype(BF16)
        cut = ATT_SPLIT
        sums = (jnp.sum(xv[:, :cut], axis=1, keepdims=True), jnp.sum(xv[:, cut:], axis=1, keepdims=True))
        return (hi[:, :cut], lo[:, :cut]), (hi[:, cut:], lo[:, cut:]), sums

    def sums_after(self, xv, carry):
        (h0, l0), (h1, l1), (s0, s1) = self._parts(xv)
        first = _dot(h0, self.later[0]) + _dot(l0, self.later[0]) + (s1 + carry)
        last = _dot(h1, self.later[1]) + _dot(l1, self.later[1]) + carry
        return jnp.concatenate([first, last], axis=1), s0 + s1

    def sums_before(self, xv, carry):
        (h0, l0), (h1, l1), (s0, s1) = self._parts(xv)
        first = _dot(h0, self.earlier[0]) + _dot(l0, self.earlier[0]) + carry
        last = _dot(h1, self.earlier[1]) + _dot(l1, self.earlier[1]) + (s0 + carry)
        return jnp.concatenate([first, last], axis=1), s0 + s1


class _HeadPair:
    def __init__(self):
        lanes = 2 * HEAD_DIM
        lane = lax.broadcasted_iota(jnp.int32, (1, lanes), 1)
        self.masks = [lane // HEAD_DIM == h for h in (0, 1)]
        i = lax.broadcasted_iota(jnp.int32, (lanes, lanes), 0) // HEAD_DIM
        j = lax.broadcasted_iota(jnp.int32, (lanes, lanes), 1) // HEAD_DIM
        self.same_head = jnp.where(i == j, 1.0, 0.0).astype(BF16)

    def only(self, h, xv):
        return jnp.where(self.masks[h], xv, jnp.zeros_like(xv))

    def merge(self, per_head):
        return jnp.where(self.masks[0], per_head[0], per_head[1])

    def mean(self, xv):
        hi = xv.astype(BF16)
        lo = (xv - hi.astype(F32)).astype(BF16)
        return (_dot(hi, self.same_head) + _dot(lo, self.same_head)) * (1.0 / HEAD_DIM)

    def rms_r(self, xv):
        return lax.rsqrt(self.mean(xv * xv) + EPS)

    def rms_bwd(self, xv, r, nw, dh):
        t = dh * nw
        dx = r * t - xv * (r * r * r * self.mean(t * xv))
        dn = jnp.sum(dh * xv * r, axis=0, keepdims=True)
        return dx, dn[:, :HEAD_DIM] + dn[:, HEAD_DIM:]


def _attn_fwd(proj, qg, kg, rider=None):
    s = proj.shape[0]
    blk, win, dh = ATT_BLOCK, ATT_WINDOW, HEAD_DIM
    nq = s // blk
    scale = 1.0 / math.sqrt(dh)
    heads = (0, 1)
    assert s >= win and s % blk == 0

    def body(*refs):
        (q_ref, k_ref, v_ref, qg_ref, kg_ref), (o_ref,), (qn, kn, vb), copies = _split_refs(refs, 5, 1, rider)
        last = pl.program_id(0) == N_HEADS // 2 - 1
        finish = _ride(copies, pl.program_id(0) == 0, last, last)
        wd, hp = _Window(), _HeadPair()
        qv = q_ref[...]
        qn[...] = (qv * hp.rms_r(qv) * qg_ref[...] * scale).astype(BF16)
        kv = k_ref[...]
        kn[...] = (kv * hp.rms_r(kv) * kg_ref[...]).astype(BF16)
        vb[...] = v_ref[...].astype(BF16)

        def q_step(qi, _):
            qoff = pl.multiple_of(qi * blk, blk)
            qt = qn[pl.ds(qoff, blk), :]
            qts = [hp.only(h, qt) for h in heads]

            def more(carry):
                g, live = carry[:2]
                return jnp.logical_and((qi + 1) * blk - g * win > 0, live > 0)

            def window(carry):
                g, _, accs, runs = carry
                start, valid = wd.place(qi, g)
                kt = kn[pl.ds(start, win), :]
                zs = [_dot_nt(qts[h], kt) for h in heads]
                logs = [_sb_logs(z, valid) for z in zs]
                sums = [wd.sums_after(logs[h][1], runs[h]) for h in heads]
                wgts = [jnp.where(valid, jnp.exp(logs[h][0] + sums[h][0]), 0.0).astype(BF16) for h in heads]
                vt = vb[pl.ds(start, win), :]
                accs = tuple(accs[h] + _dot(wgts[h], vt) for h in heads)
                runs = tuple(runs[h] + sums[h][1] for h in heads)
                live = (jnp.maximum(jnp.max(runs[0]), jnp.max(runs[1])) > EXP_ZERO).astype(jnp.int32)
                return g + 1, live, accs, runs

            zero = lambda cols: tuple(jnp.zeros((blk, cols), F32) for _ in heads)
            _, _, accs, _ = lax.while_loop(more, window, (jnp.int32(0), jnp.int32(1), zero(2 * dh), zero(1)))
            o_ref[pl.ds(qoff, blk), :] = hp.merge(accs)
            return 0

        lax.fori_loop(0, nq, q_step, 0)
        finish()

    pair = lambda group: pl.BlockSpec((s, 2 * dh), lambda p: (0, group * (D_ATT // (2 * dh)) + p))
    vec = pl.BlockSpec((1, 2 * dh), lambda p: (0, 0))
    return _call(
        body, "attn_fwd", (N_HEADS // 2,), [pair(2), pair(3), pair(4), vec, vec], [pair(0)],
        [jax.ShapeDtypeStruct((s, D_ATT), F32)], [proj, proj, proj, jnp.tile(qg, (1, 2)), jnp.tile(kg, (1, 2))],
        scratch=[pltpu.VMEM((s, 2 * dh), BF16)] * 3, rider=rider)


def _attn_bwd(proj, dya, qg, kg, rider=None):
    s = proj.shape[0]
    blk, win, dh = ATT_BLOCK, ATT_WINDOW, HEAD_DIM
    nq = s // blk
    max_windows = -(-s // win) + 1
    scale = 1.0 / math.sqrt(dh)
    steps = N_HEADS // 2
    heads = (0, 1)
    assert s >= win and s % blk == 0

    def body(*refs):
        ins, outs, scratch, copies = _split_refs(refs, 6, 5, rider)
        q_ref, k_ref, v_ref, do_ref, qg_ref, kg_ref = ins
        dq_ref, dk_ref, dv_ref, dqg_ref, dkg_ref = outs
        qn, kn, vb, dob, runs_ref, dqn, dkn, dvn = scratch
        finish = _ride(copies, pl.program_id(0) == 0, pl.program_id(0) == steps - 1)
        wd, hp = _Window(), _HeadPair()

        @pl.when(pl.program_id(0) == 0)
        def _():
            dqg_ref[...] = jnp.zeros_like(dqg_ref)
            dkg_ref[...] = jnp.zeros_like(dkg_ref)

        qv = q_ref[...]
        qn[...] = (qv * hp.rms_r(qv) * qg_ref[...] * scale).astype(BF16)
        kv = k_ref[...]
        kn[...] = (kv * hp.rms_r(kv) * kg_ref[...]).astype(BF16)
        vb[...] = v_ref[...].astype(BF16)
        dob[...] = do_ref[...].astype(BF16)
        dkn[...] = jnp.zeros_like(dkn)
        dvn[...] = jnp.zeros_like(dvn)

        def q_step(qi, _):
            qoff = pl.multiple_of(qi * blk, blk)
            qt = qn[pl.ds(qoff, blk), :]
            dot = dob[pl.ds(qoff, blk), :]
            qts = [hp.only(h, qt) for h in heads]
            dots = [hp.only(h, dot) for h in heads]

            zero = lambda cols: tuple(jnp.zeros((blk, cols), F32) for _ in heads)

            def logs_of(g):
                start, valid = wd.place(qi, g)
                kt = kn[pl.ds(start, win), :]
                return [_sb_logs(_dot_nt(qts[h], kt), valid) for h in heads]

            def row_sums(logs):
                return tuple(jnp.sum(logs[h][1], axis=1, keepdims=True) for h in heads)

            def still_live(runs):
                return jnp.maximum(jnp.max(runs[0]), jnp.max(runs[1])) > EXP_ZERO

            def window_grads(g, logs, runs, esums):
                start, valid = wd.place(qi, g)
                kt = kn[pl.ds(start, win), :]
                vt = vb[pl.ds(start, win), :]
                dws = [_dot_nt(dots[h], vt) for h in heads]
                tails = [wd.sums_after(logs[h][1], runs[h])[0] for h in heads]
                wgts = [jnp.where(valid, jnp.exp(logs[h][0] + tails[h]), 0.0) for h in heads]
                es = [dws[h] * wgts[h] for h in heads]
                befores = [wd.sums_before(es[h], esums[h]) for h in heads]
                dzbs = []
                for h in heads:
                    beta = jnp.exp(logs[h][0])
                    dz = jnp.where(valid, es[h] * (1.0 - beta) - befores[h][0] * beta, 0.0)
                    dzbs.append(dz.astype(BF16))
                dkn[pl.ds(start, win), :] += _dot_tn(dzbs[0], qts[0]) + _dot_tn(dzbs[1], qts[1])
                dvn[pl.ds(start, win), :] += (_dot_tn(wgts[0].astype(BF16), dots[0])
                                              + _dot_tn(wgts[1].astype(BF16), dots[1]))
                return tuple(_dot(dzbs[h], kt) for h in heads), tuple(befores[h][1] for h in heads)

            logs0 = logs_of(0)
            runs1 = row_sums(logs0)

            def one_window():
                return window_grads(0, logs0, zero(1), zero(1))[0]

            def all_windows():
                def more(carry):
                    g, live = carry[:2]
                    return jnp.logical_and((qi + 1) * blk - g * win > 0, live > 0)

                def run_window(carry):
                    g, _, runs = carry
                    for h in heads:
                        runs_ref[h, g] = runs[h]
                    sums = row_sums(logs_of(g))
                    runs = tuple(runs[h] + sums[h] for h in heads)
                    return g + 1, still_live(runs).astype(jnp.int32), runs

                for h in heads:
                    runs_ref[h, 0] = jnp.zeros((blk, 1), F32)
                windows, _, _ = lax.while_loop(more, run_window, (jnp.int32(1), jnp.int32(1), runs1))

                def k_window(gg, carry):
                    dq_accs, esums = carry
                    g = windows - 1 - gg
                    parts, totals = window_grads(g, logs_of(g), [runs_ref[h, g] for h in heads], esums)
                    return (tuple(dq_accs[h] + parts[h] for h in heads),
                            tuple(esums[h] + totals[h] for h in heads))

                return lax.fori_loop(0, windows, k_window, (zero(2 * dh), zero(1)))[0]

            earlier_keys = (qi + 1) * blk - win > 0
            dq_accs = lax.cond(jnp.logical_and(earlier_keys, still_live(runs1)), all_windows, one_window)
            dqn[pl.ds(qoff, blk), :] = hp.merge(dq_accs)
            return 0

        lax.fori_loop(0, nq, q_step, 0)

        dq, dqg = hp.rms_bwd(qv, hp.rms_r(qv), qg_ref[...] * scale, dqn[...])
        dq_ref[...] = dq.astype(BF16)
        dqg_ref[...] += dqg * scale
        dk, dkg = hp.rms_bwd(kv, hp.rms_r(kv), kg_ref[...], dkn[...])
        dk_ref[...] = dk.astype(BF16)
        dkg_ref[...] += dkg
        dv_ref[...] = dvn[...].astype(BF16)
        finish()

    pair = lambda group: pl.BlockSpec((s, 2 * dh), lambda p: (0, group * (D_ATT // (2 * dh)) + p))
    vec2 = pl.BlockSpec((1, 2 * dh), lambda p: (0, 0))
    vec = pl.BlockSpec((1, dh), lambda p: (0, 0))
    return _call(
        body, "attn_bwd", (steps,), [pair(2), pair(3), pair(4), pair(0), vec2, vec2],
        [pair(0), pair(0), pair(0), vec, vec],
        [jax.ShapeDtypeStruct((s, D_ATT), BF16)] * 3 + [jax.ShapeDtypeStruct((1, dh), F32)] * 2,
        [proj, proj, proj, dya, jnp.tile(qg, (1, 2)), jnp.tile(kg, (1, 2))],
        scratch=[pltpu.VMEM((s, 2 * dh), BF16)] * 4 + [pltpu.VMEM((2, max_windows, blk, 1), F32)]
        + [pltpu.VMEM((s, 2 * dh), F32)] * 3, rider=rider)


def _block_diag(w):
    n, c, d = w.shape
    return jnp.einsum("ncd,nm->ncmd", w, jnp.eye(n, dtype=w.dtype)).reshape(n * c, n * d)


def _diag_blocks(full, n):
    c = full.shape[0] // n
    return jnp.stack([full[i * c:(i + 1) * c, i * c:(i + 1) * c] for i in range(n)])


FFN1 = ["ffn1_w_gate", "ffn1_w_up", "ffn1_w_down"]
FFN2 = ["ffn2_w_gate", "ffn2_w_up", "ffn2_w_down"]
MIXER = ["w_in", "w_out"]


def _pair_sums(gb, names, where):
    theirs = _pair_exchange([gb[n] for n in names], "pair_exchange_" + names[0])
    pair, own = _pair_sum([gb[n] for n in names], theirs, where, "pair_sum_" + names[0])
    return _chip_rider(pair, own)


def _local_step(x, tgt, stacks, conv_stack, small, where):
    big = dict(zip(FFN1, _gather_weights([stacks[n] for n in FFN1], [])))
    wa = _block_diag(small["rg_w_a"]).astype(BF16)
    wx = _block_diag(small["rg_w_x"]).astype(BF16)

    whole = lambda names: [big[n].reshape(-1, D_MODEL) for n in names]
    x1, g1, u1, hb1, ab1, *landed = _ffn_fwd(x, small["ffn1_norm"], *whole(FFN1),
                                             rider=_gather_rider([stacks[n] for n in MIXER], [conv_stack]))
    big.update(zip(MIXER, landed))
    conv_w = jnp.transpose(landed[-1], (1, 0, 2)).reshape(CONV_W, D_RNN)
    wout = big["w_out"].reshape(D_MODEL, D_MODEL)
    rg = (conv_w, small["conv_b"], wa, small["rg_b_a"], wx, small["rg_b_x"], small["rg_lambda"])
    proj, hb2 = _mix_pre(x1, small["mix_norm"], big["w_in"])
    yr, hseq = _rglru_fwd(proj, *rg)
    ya, *landed = _attn_fwd(proj, small["q_norm"], small["k_norm"], _gather_rider([stacks[n] for n in FFN2], []))
    big.update(zip(FFN2, landed))
    x2 = _mix_post(x1, yr, ya, small["rnn_out_norm"], small["attn_out_norm"], wout)
    dx3, g2, u2, hb3, ab3, loss = _ffn_fwd(x2, small["ffn2_norm"], *whole(FFN2), tgt)

    gb, gs, slots = {}, {}, {}
    dx2, dg2, du2, dyb2, gs["ffn2_norm"] = _ffn_bwd_act(x2, small["ffn2_norm"], dx3, g2, u2, *whole(FFN2), "ffn2_bwd")
    gb["ffn2_w_gate"] = _ffn_wgrad(dg2, hb3, 1.0, "wgrad_gate_ffn2")
    gb["ffn2_w_up"] = _ffn_wgrad(du2, hb3, 1.0, "wgrad_up_ffn2")
    gb["ffn2_w_down"] = _ffn_wgrad(ab3, dyb2, 0.5, "wgrad_down_ffn2")
    dyr, dya, ycat, dxb2, gs["rnn_out_norm"], gs["attn_out_norm"] = _mix_post_bwd(
        dx2, yr, ya, small["rnn_out_norm"], small["attn_out_norm"], wout)
    gb["w_out"] = _wgrad_whole(ycat, dxb2, False, "wgrad_out")
    early = FFN2 + ["w_out"]
    dq, dk, dv, gs["q_norm"], gs["k_norm"], *done = _attn_bwd(
        proj, dya, small["q_norm"], small["k_norm"], _pair_sums(gb, early, where))
    slots.update(zip(early, done))
    dxr, dgate, gs["conv_w"], gs["conv_b"], dwa, gs["rg_b_a"], dwx, gs["rg_b_x"], gs["rg_lambda"] = _rglru_bwd(
        proj, hseq, dyr, *rg)
    gs["rg_w_a"] = _diag_blocks(dwa, RNN_BLOCKS)
    gs["rg_w_x"] = _diag_blocks(dwx, RNN_BLOCKS)
    dpb = jnp.concatenate([dxr, dgate, dq, dk, dv], axis=1)
    dx1, gs["mix_norm"] = _mix_pre_bwd(x1, small["mix_norm"], dx2, dpb, big["w_in"])
    dx0, dg1, du1, dyb1, gs["ffn1_norm"] = _ffn_bwd_act(x, small["ffn1_norm"], dx1, g1, u1, *whole(FFN1), "ffn1_bwd")

    mine = _place_shard(_pack([gs[n] for n in SMALL] + [loss[:, :1]]), where, F32, "place_small_grads",
                        by_device=True)
    gb["ffn1_w_gate"], everyone = _ffn_wgrad(dg1, hb1, 1.0, "wgrad_gate_ffn1", _small_rider(mine))
    gb["ffn1_w_up"], slots["ffn1_w_gate"] = _ffn_wgrad(
        du1, hb1, 1.0, "wgrad_up_ffn1", _pair_sums(gb, ["ffn1_w_gate"], where))
    gb["ffn1_w_down"], slots["ffn1_w_up"] = _ffn_wgrad(
        ab1, dyb1, 0.5, "wgrad_down_ffn1", _pair_sums(gb, ["ffn1_w_up"], where))
    gb["w_in"], slots["ffn1_w_down"] = _wgrad_whole(
        hb2, dpb, True, "wgrad_in", _pair_sums(gb, ["ffn1_w_down"], where))
    last = _pair_sums(gb, ["w_in"], where)
    slots["w_in"], = _chip_exchange(last.plain, last.inplace)
    return dx0, slots, gs, everyone


ANY = pl.BlockSpec(memory_space=pl.ANY)


def _place():
    x, y, c = lax.axis_index("x"), lax.axis_index("y"), lax.axis_index("c")
    other_chips = [(1 - x, y), (x, 1 - y), (1 - x, 1 - y)]
    return x, y, c, 2 * x + y, other_chips


def _remote(src, dst, send_sem, recv_sem, to):
    return pltpu.make_async_remote_copy(src_ref=src, dst_ref=dst, send_sem=send_sem, recv_sem=recv_sem,
                                        device_id=to, device_id_type=MESH)


def _copy_plan(pairs):
    sends = [functools.partial(_remote, *a) for a, _ in pairs]
    arrivals = [functools.partial(_remote, *b) for _, b in pairs]
    return sends, arrivals


class _Rider:
    def __init__(self, plan, plain, inplace, n_copies=None, relay=None, n_relay=0):
        self.plan, self.plain, self.inplace = plan, list(plain), list(inplace)
        self.n_copies = n_copies or 3 * len(self.inplace)
        self.relay, self.n_relay = relay, n_relay

    def operands(self):
        return self.plain + self.inplace

    def out_shape(self):
        return [jax.ShapeDtypeStruct(a.shape, a.dtype) for a in self.inplace]

    def aliases(self, inputs_before, outputs_before):
        return {inputs_before + len(self.plain) + k: outputs_before + k for k in range(len(self.inplace))}

    def scratch(self):
        relay = [pltpu.SemaphoreType.DMA((self.n_relay,))] * 2 if self.relay else []
        return [pltpu.SemaphoreType.DMA((self.n_copies,))] * 2 + relay


def _split_refs(refs, n_in, n_out, rider):
    if rider is None:
        return refs[:n_in], refs[n_in:n_in + n_out], refs[n_in + n_out:], None
    r_in, r_out = len(rider.operands()), len(rider.inplace)
    outs_at = n_in + r_in
    n_sems = len(rider.scratch())
    rest = refs[outs_at + n_out + r_out:]
    sems = rest[len(rest) - n_sems:]
    filled = refs[outs_at + n_out:outs_at + n_out + r_out]
    copies = functools.partial(rider.plan, refs[n_in:n_in + len(rider.plain)], filled, *sems[:2])
    relay = functools.partial(rider.relay, filled, *sems[2:]) if rider.relay else None
    return refs[:n_in], refs[outs_at:outs_at + n_out], rest[:len(rest) - n_sems], (copies, relay)


def _ride(copies, first, last, middle=None):
    if copies is None:
        return lambda: None
    copies, relay = copies

    @pl.when(first)
    def _():
        _start(copies()[0])

    if relay is not None:
        @pl.when(middle)
        def _():
            for make in copies()[1]:
                make().wait_recv()
            _start(relay()[0])

    def finish():
        @pl.when(last)
        def _():
            if relay is None:
                _finish(*copies())
            else:
                _finish(copies()[0] + relay()[0], relay()[1])

    return finish


def _gather_rider(split, whole):
    n_split = len(split)
    return _Rider(lambda plain, stacks, ss, rs: _gather_ici(stacks, n_split, ss, rs), [], list(split) + list(whole),
                  relay=lambda stacks, ss, rs: _gather_d2d(stacks[:n_split], ss, rs), n_relay=3 * n_split)


def _chip_rider(sums, slots):
    return _Rider(_chip_copies, sums, slots)


def _start(makers):
    for make in makers:
        make().start()


def _finish(sends, arrivals):
    for make in arrivals:
        make().wait_recv()
    for make in sends:
        make().wait_send()


def _half(rows, c):
    return pl.ds(pl.multiple_of(c * rows, 16), rows)


def _gather_weights(split, whole):
    arrs = list(split) + list(whole)
    n, ns = len(arrs), len(split)

    def body(*refs):
        outs = refs[n:2 * n]
        send_sems, recv_sems, fsend_sems, frecv_sems = refs[2 * n:]
        sends, arrivals = _gather_ici(outs, ns, send_sems, recv_sems)
        passes, passed = _gather_d2d(outs[:ns], fsend_sems, frecv_sems)
        _start(sends)
        for k, make in enumerate(arrivals):
            make().wait_recv()
            if k < 3 * ns:
                passes[k]().start()
        _finish(sends + passes, passed)

    return pl.pallas_call(
        body, name="gather_weights",
        in_specs=[ANY] * n, out_specs=[ANY] * n,
        out_shape=[jax.ShapeDtypeStruct(a.shape, a.dtype) for a in arrs],
        input_output_aliases={i: i for i in range(n)},
        scratch_shapes=[pltpu.SemaphoreType.DMA((3 * n,)), pltpu.SemaphoreType.DMA((3 * n,)),
                        pltpu.SemaphoreType.DMA((3 * ns,)), pltpu.SemaphoreType.DMA((3 * ns,))],
    )(*arrs)


def _gather_ici(stacks, n_split, send_sems, recv_sems):
    x, y, c, me, chips = _place()

    def region(i, chip):
        if i < n_split:
            return stacks[i].at[chip, _half(stacks[i].shape[1] // 2, c)]
        return stacks[i].at[chip]

    pairs = []
    for i in range(len(stacks)):
        for p, (cx, cy) in enumerate(chips):
            k = 3 * i + p
            mine, got = region(i, me), region(i, 2 * cx + cy)
            sems, to = (send_sems.at[k], recv_sems.at[k]), (cx, cy, c)
            pairs.append(((mine, mine, *sems, to), (got, got, *sems, to)))
    return _copy_plan(pairs)


def _gather_d2d(stacks, send_sems, recv_sems):
    x, y, c, _, chips = _place()
    sibling = (x, y, 1 - c)
    pairs = []
    for i, stack in enumerate(stacks):
        rows = stack.shape[1] // 2
        for p, (cx, cy) in enumerate(chips):
            k = 3 * i + p
            got, theirs = stack.at[2 * cx + cy, _half(rows, c)], stack.at[2 * cx + cy, _half(rows, 1 - c)]
            sems = (send_sems.at[k], recv_sems.at[k])
            pairs.append(((got, got, *sems, sibling), (theirs, theirs, *sems, sibling)))
    return _copy_plan(pairs)


def _pair_exchange(grads, name):
    n = len(grads)

    def body(*refs):
        ins, theirs = refs[:n], refs[n:2 * n]
        send_sems, recv_sems = refs[2 * n:]
        x, y, c, _, _ = _place()
        sibling = (x, y, 1 - c)
        sends = [_remote(ins[k].at[:, _half(grads[k].shape[1] // 2, 1 - c)], theirs[k],
                         send_sems.at[k], recv_sems.at[k], sibling) for k in range(n)]
        for cp in sends:
            cp.start()
        for k in range(n):
            _remote(theirs[k], theirs[k], send_sems.at[k], recv_sems.at[k], sibling).wait_recv()
        for cp in sends:
            cp.wait_send()

    return pl.pallas_call(
        body, name=name,
        in_specs=[ANY] * n, out_specs=[ANY] * n,
        out_shape=[jax.ShapeDtypeStruct((g.shape[0], g.shape[1] // 2, g.shape[2]), g.dtype) for g in grads],
        scratch_shapes=[pltpu.SemaphoreType.DMA((n,))] * 2,
    )(*grads)


def _chip_exchange(sums, slots):
    n = len(sums)

    def body(*refs):
        sends, arrivals = _chip_copies(refs[:n], refs[2 * n:3 * n], *refs[3 * n:])
        _start(sends)
        _finish(sends, arrivals)

    return pl.pallas_call(
        body, name="grad_chip_exchange",
        in_specs=[ANY] * (2 * n), out_specs=[ANY] * n,
        out_shape=[jax.ShapeDtypeStruct(a.shape, a.dtype) for a in slots],
        input_output_aliases={n + k: k for k in range(n)},
        scratch_shapes=[pltpu.SemaphoreType.DMA((3 * n,)), pltpu.SemaphoreType.DMA((3 * n,))],
    )(*sums, *slots)


def _chip_copies(sums, slots, send_sems, recv_sems):
    x, y, c, me, chips = _place()
    pairs = []
    for k in range(len(sums)):
        for p, (cx, cy) in enumerate(chips):
            j = 3 * k + p
            got = slots[k].at[2 * cx + cy]
            sems, to = (send_sems.at[j], recv_sems.at[j]), (cx, cy, c)
            pairs.append(((sums[k].at[2 * cx + cy], slots[k].at[me], *sems, to), (got, got, *sems, to)))
    return _copy_plan(pairs)


def _half_swap(halves):
    n = len(halves)

    def body(*refs):
        outs = refs[n:2 * n]
        send_sems, recv_sems = refs[2 * n:]
        x, y, c, _, _ = _place()
        sibling = (x, y, 1 - c)
        sends = [_remote(outs[k].at[c], outs[k].at[c], send_sems.at[k], recv_sems.at[k], sibling) for k in range(n)]
        for cp in sends:
            cp.start()
        for k in range(n):
            got = outs[k].at[1 - c]
            _remote(got, got, send_sems.at[k], recv_sems.at[k], sibling).wait_recv()
        for cp in sends:
            cp.wait_send()

    return pl.pallas_call(
        body, name="grad_half_swap",
        in_specs=[ANY] * n, out_specs=[ANY] * n,
        out_shape=[jax.ShapeDtypeStruct(a.shape, a.dtype) for a in halves],
        input_output_aliases={k: k for k in range(n)},
        scratch_shapes=[pltpu.SemaphoreType.DMA((n,))] * 2,
    )(*halves)


def _small_rider(stack):
    n_dev = 2 * N_CHIPS

    def plan(_, stacks, send_sems, recv_sems):
        x, y, c, _, _ = _place()
        mine = stacks[0].at[4 * x + 2 * y + c]
        pairs = []
        for k in range(1, n_dev):
            px, py, pc = x ^ ((k >> 2) & 1), y ^ ((k >> 1) & 1), c ^ (k & 1)
            got = stacks[0].at[4 * px + 2 * py + pc]
            sems = (send_sems.at[k - 1], recv_sems.at[k - 1])
            pairs.append(((mine, mine, *sems, (px, py, pc)), (got, got, *sems, (px, py, pc))))
        return _copy_plan(pairs)

    return _Rider(plan, [], [stack], n_dev - 1)


def _row_tile(r):
    return r // 4 if r >= 256 and (r // 4) % 16 == 0 else r


def _prefetch_call(body, name, grid, in_specs, out_specs, out_shape):
    spec = pltpu.PrefetchScalarGridSpec(num_scalar_prefetch=1, grid=grid, in_specs=in_specs, out_specs=out_specs)
    return pl.pallas_call(body, name=name, grid_spec=spec, out_shape=out_shape,
                          compiler_params=_params(("arbitrary",) * len(grid)))


def _place_shard(w2d, where, dtype, name, by_device=False):
    r, c = w2d.shape
    tr = _row_tile(r)
    slots = 2 * N_CHIPS if by_device else N_CHIPS
    slot = (lambda s: 2 * s[1] + s[0]) if by_device else (lambda s: s[1])

    def body(where_ref, w_ref, out_ref):
        out_ref[...] = w_ref[...].astype(dtype)

    return _prefetch_call(
        body, name, (r // tr,), [pl.BlockSpec((tr, c), lambda i, s: (i, 0))],
        pl.BlockSpec((None, tr, c), lambda i, s: (slot(s), i, 0)),
        jax.ShapeDtypeStruct((slots, r, c), dtype))(where, w2d)


def _place_shards(w2ds, where, name):
    n = len(w2ds)
    steps = N_CHIPS
    assert all(w.shape[0] % (16 * steps) == 0 for w in w2ds)

    def body(where_ref, *refs):
        for k in range(n):
            refs[n + k][...] = refs[k][...].astype(BF16)

    tile = lambda w: (w.shape[0] // steps, w.shape[1])
    return _prefetch_call(
        body, name, (steps,), [pl.BlockSpec(tile(w), lambda i, s: (i, 0)) for w in w2ds],
        [pl.BlockSpec((None,) + tile(w), lambda i, s: (s[1], i, 0)) for w in w2ds],
        [jax.ShapeDtypeStruct((N_CHIPS,) + w.shape, BF16) for w in w2ds])(where, *w2ds)


def _pair_sum(fulls, theirs, where, name):
    n = len(fulls)

    def body(where_ref, *refs):
        for k in range(n):
            a_ref, b_ref, out_ref, own_ref = refs[k], refs[n + k], refs[2 * n + k], refs[3 * n + k]
            total = (a_ref[...].astype(F32) + b_ref[...].astype(F32)).astype(BF16)
            out_ref[...] = total

            @pl.when(pl.program_id(0) == where_ref[1])
            def _():
                own_ref[...] = total

    half = lambda t: pl.BlockSpec((None,) + t.shape[1:], lambda j, s: (j, s[0], 0))
    blk = lambda t: pl.BlockSpec((None,) + t.shape[1:], lambda j, s: (j, 0, 0))
    own = lambda t: pl.BlockSpec((None,) + t.shape[1:], lambda j, s: (s[1], 0, 0))
    shapes = [jax.ShapeDtypeStruct(t.shape, BF16) for t in theirs]
    outs = _prefetch_call(
        body, name, (N_CHIPS,), [half(t) for t in theirs] + [blk(t) for t in theirs],
        [blk(t) for t in theirs] + [own(t) for t in theirs], shapes + shapes)(where, *fulls, *theirs)
    return outs[:n], outs[n:]


def _chip_sum(slots, where, name):
    n = len(slots)
    steps = 2
    assert all(a.shape[1] % (16 * steps) == 0 for a in slots)

    def body(where_ref, *refs):
        for k in range(n):
            a_ref, out_ref = refs[k], refs[n + k]
            total = a_ref[0].astype(F32)
            for j in range(1, a_ref.shape[0]):
                total = total + a_ref[j].astype(F32)
            out_ref[...] = total

    tile = lambda a: (a.shape[1] // steps, a.shape[2])
    return _prefetch_call(
        body, name, (steps,), [pl.BlockSpec((a.shape[0],) + tile(a), lambda i, s: (0, i, 0)) for a in slots],
        [pl.BlockSpec((None,) + tile(a), lambda i, s: (s[0], i, 0)) for a in slots],
        [jax.ShapeDtypeStruct((2,) + a.shape[1:], F32) for a in slots])(where, *slots)


def _slot_sum(a, name):
    nb, r, c = a.shape
    tr = _row_tile(r)

    def body(a_ref, out_ref):
        total = a_ref[0].astype(F32)
        for j in range(1, nb):
            total = total + a_ref[j].astype(F32)
        out_ref[...] = total

    return pl.pallas_call(
        body, name=name, grid=(r // tr,),
        in_specs=[pl.BlockSpec((nb, tr, c), lambda i: (0, i, 0))],
        out_specs=pl.BlockSpec((tr, c), lambda i: (i, 0)),
        out_shape=jax.ShapeDtypeStruct((r, c), F32), compiler_params=_params(("arbitrary",)),
    )(a)


def _adamw(ws, gs, ms, vs, name, steps=1):
    n = len(ws)
    c1 = 1.0 - ADAM_B1 ** ADAM_STEP
    c2 = 1.0 - ADAM_B2 ** ADAM_STEP
    assert all(w.shape[0] % steps == 0 and (steps == 1 or w.shape[0] // steps % 8 == 0) for w in ws)

    def body(*refs):
        for k in range(n):
            w_ref, g_ref, m_ref, v_ref = (refs[j * n + k] for j in range(4))
            d_ref, m2_ref, v2_ref = (refs[(4 + j) * n + k] for j in range(3))
            gv = g_ref[...]
            m2 = ADAM_B1 * m_ref[...] + (1.0 - ADAM_B1) * gv
            v2 = ADAM_B2 * v_ref[...] + (1.0 - ADAM_B2) * (gv * gv)
            m2_ref[...] = m2
            v2_ref[...] = v2
            d_ref[...] = -ADAM_LR * ((m2 / c1) / (jnp.sqrt(v2 / c2) + ADAM_EPS) + ADAM_WD * w_ref[...])

    blks = [pl.BlockSpec((w.shape[0] // steps, w.shape[1]), lambda i: (i, 0)) for w in ws]
    shapes = [jax.ShapeDtypeStruct(w.shape, F32) for w in ws]
    outs = pl.pallas_call(
        body, name=name, grid=(steps,), in_specs=blks * 4, out_specs=blks * 3, out_shape=shapes * 3,
        compiler_params=_params(("arbitrary",)),
    )(*ws, *gs, *ms, *vs)
    return outs[:n], outs[n:2 * n], outs[2 * n:]


WEIGHTS = ["ffn1_norm", "ffn1_w_gate", "ffn1_w_up", "ffn1_w_down", "mix_norm", "w_in", "conv_w", "conv_b",
           "rg_w_a", "rg_b_a", "rg_w_x", "rg_b_x", "rg_lambda", "q_norm", "k_norm", "rnn_out_norm",
           "attn_out_norm", "w_out", "ffn2_norm", "ffn2_w_gate", "ffn2_w_up", "ffn2_w_down"]
BIG = ["ffn1_w_gate", "ffn1_w_up", "ffn1_w_down", "w_in", "w_out", "ffn2_w_gate", "ffn2_w_up", "ffn2_w_down"]
SMALL = [n for n in WEIGHTS if n not in BIG]
PACK_LANES = 128
PACK_ROW_ALIGN = 8


def _hidden_major(name, a):
    return jnp.transpose(a) if name.endswith(("w_gate", "w_up")) else a


def _pack(parts):
    flat = jnp.concatenate([p.reshape(-1) for p in parts])
    unit = PACK_LANES * PACK_ROW_ALIGN
    padded = -(-flat.shape[0] // unit) * unit
    return jnp.pad(flat, (0, padded - flat.shape[0])).reshape(-1, PACK_LANES)


def _unpack(packed, shapes):
    flat = packed.reshape(-1)
    out, at = [], 0
    for shp in shapes:
        size = math.prod(shp)
        out.append(flat[at:at + size].reshape(shp))
        at += size
    return out


def kernel(x, ffn1_norm, ffn1_w_gate, ffn1_w_up, ffn1_w_down, mix_norm, w_in, conv_w, conv_b, rg_w_a, rg_b_a, rg_w_x, rg_b_x, rg_lambda, q_norm, k_norm, rnn_out_norm, attn_out_norm, w_out, ffn2_norm, ffn2_w_gate, ffn2_w_up, ffn2_w_down, loss_target, m_ffn1_norm, m_ffn1_w_gate, m_ffn1_w_up, m_ffn1_w_down, m_mix_norm, m_w_in, m_conv_w, m_conv_b, m_rg_w_a, m_rg_b_a, m_rg_w_x, m_rg_b_x, m_rg_lambda, m_q_norm, m_k_norm, m_rnn_out_norm, m_attn_out_norm, m_w_out, m_ffn2_norm, m_ffn2_w_gate, m_ffn2_w_up, m_ffn2_w_down, v_ffn1_norm, v_ffn1_w_gate, v_ffn1_w_up, v_ffn1_w_down, v_mix_norm, v_w_in, v_conv_w, v_conv_b, v_rg_w_a, v_rg_b_a, v_rg_w_x, v_rg_b_x, v_rg_lambda, v_q_norm, v_k_norm, v_rnn_out_norm, v_attn_out_norm, v_w_out, v_ffn2_norm, v_ffn2_w_gate, v_ffn2_w_up, v_ffn2_w_down):
    given = dict(locals())
    w = {n: given[n] for n in WEIGHTS}
    m = {n: given["m_" + n] for n in WEIGHTS}
    v = {n: given["v_" + n] for n in WEIGHTS}
    chip = 2 * lax.axis_index("x") + lax.axis_index("y")

    where = jnp.stack([lax.axis_index("c"), chip]).astype(jnp.int32)

    stacks = dict(zip(BIG, _place_shards([_hidden_major(n, w[n][0]) for n in BIG], where, "place_weights")))
    conv_stack = _place_shard(w["conv_w"][0], where, F32, "place_conv_w")
    small = {n: (w[n][0] if w[n].ndim > 2 else w[n]) for n in SMALL if n != "conv_w"}

    grad_x, slots, gs, everyone = _local_step(x[0], loss_target[0], stacks, conv_stack, small, where)

    swapped = _half_swap(_chip_sum([slots[n] for n in BIG], where, "chip_sums"))
    g2s = [t.reshape(t.shape[0] * t.shape[1], t.shape[2]) for t in swapped]
    flat = lambda tree: [_hidden_major(n, tree[n][0]) for n in BIG]
    d2s, m2s, v2s = _adamw(flat(w), g2s, flat(m), flat(v), "adamw_weights", ADAMW_STEPS)
    grads, deltas, new_m, new_v = {}, {}, {}, {}
    for tree, parts in ((grads, g2s), (deltas, d2s), (new_m, m2s), (new_v, v2s)):
        tree.update({n: _hidden_major(n, a).reshape(w[n].shape) for n, a in zip(BIG, parts)})

    full_shapes = [gs[n].shape for n in SMALL]
    *summed, loss = _unpack(_slot_sum(everyone, "small_grad_sum"), full_shapes + [(1, 1)])
    g_parts = dict(zip(SMALL, summed))
    quarter = D_RNN // N_CHIPS
    g_parts["conv_w"] = lax.dynamic_slice_in_dim(g_parts["conv_w"], chip * quarter, quarter, axis=1)
    local_shapes = [w[n].shape for n in SMALL]
    pk = lambda tree: _pack([tree[n] for n in SMALL])
    (d_s,), (m_s,), (v_s,) = _adamw([pk(w)], [pk(g_parts)], [pk(m)], [pk(v)], "adamw_small")
    for tree, packed in ((grads, pk(g_parts)), (deltas, d_s), (new_m, m_s), (new_v, v_s)):
        tree.update(zip(SMALL, _unpack(packed, local_shapes)))

    return (loss[0, 0], grad_x.reshape(x.shape), *[grads[n] for n in WEIGHTS], *[deltas[n] for n in WEIGHTS],
            *[new_m[n] for n in WEIGHTS], *[new_v[n] for n in WEIGHTS])
```

```python
import functools
import math

import jax
import jax.numpy as jnp
from jax import lax
from jax.experimental import pallas as pl
from jax.experimental.pallas import tpu as pltpu

F32 = jnp.float32
BF16 = jnp.bfloat16
MESH = pl.DeviceIdType.MESH

D_MODEL = 1024
N_CHIPS = 4
D_RNN = 512
D_ATT = 512
N_HEADS = 8
HEAD_DIM = 64
RNN_BLOCKS = 8
CONV_W = 4
RG_C = 8.0
N_IN = 2 * D_RNN + 3 * D_ATT
EPS = 1e-6
ATT_BLOCK = 128
ATT_WINDOW = 384
ATT_SPLIT = 256
EXP_ZERO = -105.0

ADAM_LR = 0.001
ADAM_B1 = 0.9
ADAM_B2 = 0.999
ADAM_EPS = 1e-08
ADAM_WD = 0.01
ADAM_STEP = 10

V7X_VMEM_LIMIT = 56 * 1024 * 1024
V7X_MXU_WIDTH = 256
TOKEN_TILE = 512
SUBLANES = 8
FFN_TILE = 256
WGRAD_TILE = 2048
WHOLE_TILE = 1024
ADAMW_STEPS = 8

GELU_K0 = math.sqrt(2.0 / math.pi)
GELU_K1 = 0.044715


def _params(sem=None):
    return pltpu.CompilerParams(dimension_semantics=sem, vmem_limit_bytes=V7X_VMEM_LIMIT)


def _dot(a, b):
    return jnp.dot(a, b, preferred_element_type=F32)


def _dot_nt(a, b):
    return lax.dot_general(a, b, (((1,), (1,)), ((), ())), preferred_element_type=F32)


def _dot_tn(a, b):
    return lax.dot_general(a, b, (((0,), (0,)), ((), ())), preferred_element_type=F32)


def _sigmoid(x):
    return 1.0 / (1.0 + jnp.exp(-x))


def _rms_r(xv):
    return lax.rsqrt(jnp.mean(xv * xv, axis=-1, keepdims=True) + EPS)


def _rms_bwd(xv, r, nw, dh):
    t = dh * nw
    dx = r * t - xv * (r * r * r * jnp.mean(t * xv, axis=-1, keepdims=True))
    dn = jnp.sum(dh * xv * r, axis=0, keepdims=True)
    return dx, dn


def _gelu(x):
    t = jnp.tanh(GELU_K0 * (x + GELU_K1 * x * x * x))
    return 0.5 * x * (1.0 + t)


def _gelu_grad(x):
    t = jnp.tanh(GELU_K0 * (x + GELU_K1 * x * x * x))
    return 0.5 * (1.0 + t) + 0.5 * x * (1.0 - t * t) * (GELU_K0 * (1.0 + 3.0 * GELU_K1 * x * x))


def _expm1_neg(x):
    p = 1.0 + x * (1.0 / 6.0)
    for k in (5.0, 4.0, 3.0, 2.0):
        p = 1.0 + x * (1.0 / k) * p
    return jnp.where(x > -0.25, x * p, jnp.exp(x) - 1.0)


def _log_sigmoid(x):
    return jnp.minimum(x, 0.0) - jnp.log(1.0 + jnp.exp(-jnp.abs(x)))


def _tile(s):
    return min(TOKEN_TILE, s)


def _ffn_chunks(f):
    cut = f // 2 // V7X_MXU_WIDTH * V7X_MXU_WIDTH
    return ((0, cut), (cut, f)) if 0 < cut < f else ((0, f),)


def _ffn_fwd(x, nw, wg, wu, wd, tgt=None, rider=None):
    s, d = x.shape
    f = wg.shape[0]
    tm = min(FFN_TILE, s)
    ni = s // tm
    assert s % tm == 0
    with_loss = tgt is not None
    n_in, n_out = 5 + with_loss, 5 + with_loss

    def body(*refs):
        ins, outs, _, copies = _split_refs(refs, n_in, n_out, rider)
        x_ref, nw_ref, wg_ref, wu_ref, wd_ref = ins[:5]
        out_ref, g_ref, u_ref, hb_ref, ab_ref = outs[:5]
        i = pl.program_id(0)
        finish = _ride(copies, i == 0, i == ni - 1, i == 3 * ni // 4)

        xv = x_ref[...]
        hb = (xv * _rms_r(xv) * nw_ref[...]).astype(BF16)
        hb_ref[...] = hb
        y = jnp.zeros((tm, d), F32)
        for lo, hi in _ffn_chunks(f):
            g = _dot_nt(hb, wg_ref[lo:hi, :])
            u = _dot_nt(hb, wu_ref[lo:hi, :])
            g_ref[:, lo:hi] = g.astype(BF16)
            u_ref[:, lo:hi] = u.astype(BF16)
            ab = (g * _sigmoid(g) * u).astype(BF16)
            ab_ref[:, lo:hi] = ab
            y = y + _dot(ab, wd_ref[lo:hi, :])
        y = xv + 0.5 * y
        if with_loss:
            tgt_ref, loss_ref = ins[5], outs[5]
            diff = y - tgt_ref[...]
            out_ref[...] = diff * (1.0 / d)

            @pl.when(i == 0)
            def _():
                loss_ref[...] = jnp.zeros_like(loss_ref)

            loss_ref[...] += jnp.sum(diff * diff) * (0.5 / d)
        else:
            out_ref[...] = y
        finish()

    row = pl.BlockSpec((tm, d), lambda i: (i, 0))
    weight = pl.BlockSpec((f, d), lambda i: (0, 0), pipeline_mode=pl.Buffered(1))
    in_specs = [row, pl.BlockSpec((1, d), lambda i: (0, 0)), weight, weight, weight]
    args = [x, nw, wg, wu, wd]
    if with_loss:
        in_specs.append(row)
        args.append(tgt)
    blk = pl.BlockSpec((tm, f), lambda i: (i, 0))
    out_shape = [jax.ShapeDtypeStruct((s, d), F32), jax.ShapeDtypeStruct((s, f), BF16),
                 jax.ShapeDtypeStruct((s, f), BF16), jax.ShapeDtypeStruct((s, d), BF16),
                 jax.ShapeDtypeStruct((s, f), BF16)]
    out_specs = [row, blk, blk, row, blk]
    if with_loss:
        out_shape.append(jax.ShapeDtypeStruct((1, 128), F32))
        out_specs.append(pl.BlockSpec((1, 128), lambda i: (0, 0)))
    return _call(body, "ffn_fwd_loss" if with_loss else "ffn_fwd", (ni,), in_specs, out_specs, out_shape, args,
                 rider=rider)


def _call(body, name, grid, in_specs, out_specs, out_shape, args, scratch=(), rider=None):
    in_specs, out_specs, out_shape, scratch = list(in_specs), list(out_specs), list(out_shape), list(scratch)
    extra, aliases = [], {}
    if rider is not None:
        extra = rider.operands()
        aliases = rider.aliases(len(args), len(out_shape))
        in_specs += [ANY] * len(extra)
        out_specs += [ANY] * len(rider.inplace)
        out_shape += rider.out_shape()
        scratch += rider.scratch()
    return pl.pallas_call(
        body, name=name, grid=grid, in_specs=in_specs, out_specs=out_specs, out_shape=out_shape,
        input_output_aliases=aliases, scratch_shapes=scratch,
        compiler_params=_params(("arbitrary",) * len(grid)),
    )(*args, *extra)


def _ffn_bwd_act(x, nw, dy, g, u, wg, wu, wd, name, rider=None):
    s, d = x.shape
    f = wg.shape[0]
    tm = min(FFN_TILE, s)
    assert s % tm == 0

    def body(*refs):
        ins, outs, _, copies = _split_refs(refs, 8, 5, rider)
        x_ref, nw_ref, dy_ref, g_ref, u_ref, wg_ref, wu_ref, wd_ref = ins
        dx_ref, dg_ref, du_ref, dyb_ref, dnw_ref = outs
        finish = _ride(copies, pl.program_id(0) == 0, pl.program_id(0) == s // tm - 1)
        dyv = dy_ref[...]
        dyb = dyv.astype(BF16)
        dyb_ref[...] = dyb
        dh = jnp.zeros((tm, d), F32)
        for lo, hi in _ffn_chunks(f):
            da = 0.5 * _dot_nt(dyb, wd_ref[lo:hi, :])
            gv = g_ref[:, lo:hi].astype(F32)
            sg = _sigmoid(gv)
            dub = (da * (gv * sg)).astype(BF16)
            dgb = (da * u_ref[:, lo:hi].astype(F32) * (sg * (1.0 + gv * (1.0 - sg)))).astype(BF16)
            dg_ref[:, lo:hi] = dgb
            du_ref[:, lo:hi] = dub
            dh = dh + _dot(dgb, wg_ref[lo:hi, :]) + _dot(dub, wu_ref[lo:hi, :])
        xv = x_ref[...]
        dx, dn = _rms_bwd(xv, _rms_r(xv), nw_ref[...], dh)
        dx_ref[...] = dyv + dx

        @pl.when(pl.program_id(0) == 0)
        def _():
            dnw_ref[...] = jnp.zeros_like(dnw_ref)

        dnw_ref[...] += dn
        finish()

    row = pl.BlockSpec((tm, d), lambda i: (i, 0))
    vec = pl.BlockSpec((1, d), lambda i: (0, 0))
    blk = pl.BlockSpec((tm, f), lambda i: (i, 0))
    weight = pl.BlockSpec((f, d), lambda i: (0, 0), pipeline_mode=pl.Buffered(1))
    return _call(
        body, name, (s // tm,), [row, vec, row, blk, blk, weight, weight, weight], [row, blk, blk, row, vec],
        [jax.ShapeDtypeStruct((s, d), F32), jax.ShapeDtypeStruct((s, f), BF16),
         jax.ShapeDtypeStruct((s, f), BF16), jax.ShapeDtypeStruct((s, d), BF16),
         jax.ShapeDtypeStruct((1, d), F32)],
        [x, nw, dy, g, u, wg, wu, wd], rider=rider)


def _wgrad(a, b, a_spec, b_spec, out_rows, out_cols, scale, name, tk, rider=None, per_step=1):
    s = a.shape[-2]
    nk = s // tk
    steps = N_CHIPS // per_step
    assert s % tk == 0

    def body(*refs):
        (a_ref, b_ref), (out_ref,), (acc,), copies = _split_refs(refs, 2, 1, rider)
        j, k = pl.program_id(0), pl.program_id(1)
        finish = _ride(copies, jnp.logical_and(j == 0, k == 0), jnp.logical_and(j == steps - 1, k == nk - 1))

        @pl.when(k == 0)
        def _():
            acc[...] = jnp.zeros_like(acc)

        acc[...] += _dot_tn(a_ref[...], b_ref[...])

        @pl.when(k == nk - 1)
        def _():
            for t in range(per_step):
                out_ref[t] = (acc[t * out_rows:(t + 1) * out_rows, :] * scale).astype(BF16)

        finish()

    outs = _call(
        body, name, (steps, nk), [a_spec(tk), b_spec(tk)],
        [pl.BlockSpec((per_step, out_rows, out_cols), lambda j, k: (j, 0, 0))],
        [jax.ShapeDtypeStruct((N_CHIPS, out_rows, out_cols), BF16)], [a, b],
        scratch=[pltpu.VMEM((per_step * out_rows, out_cols), F32)], rider=rider)
    return outs[0] if rider is None else outs


def _wgrad_whole(a, b, col_blocks, name, rider=None):
    s, m = a.shape
    n = b.shape[1]
    tk = min(WHOLE_TILE, s)
    nk = s // tk
    assert s % tk == 0
    out_shape = (N_CHIPS, m, n // N_CHIPS) if col_blocks else (N_CHIPS, m // N_CHIPS, n)

    def body(*refs):
        (a_ref, b_ref), (out_ref,), (acc,), copies = _split_refs(refs, 2, 1, rider)
        k = pl.program_id(0)
        finish = _ride(copies, k == 0, k == nk - 1)

        @pl.when(k == 0)
        def _():
            acc[...] = jnp.zeros_like(acc)

        acc[...] += _dot_tn(a_ref[...], b_ref[...])

        @pl.when(k == nk - 1)
        def _():
            for j in range(N_CHIPS):
                if col_blocks:
                    out_ref[j] = acc[:, j * out_shape[2]:(j + 1) * out_shape[2]].astype(BF16)
                else:
                    out_ref[j] = acc[j * out_shape[1]:(j + 1) * out_shape[1], :].astype(BF16)

        finish()

    outs = _call(
        body, name, (nk,), [pl.BlockSpec((tk, m), lambda k: (k, 0)), pl.BlockSpec((tk, n), lambda k: (k, 0))],
        [pl.BlockSpec(out_shape, lambda k: (0, 0, 0))], [jax.ShapeDtypeStruct(out_shape, BF16)], [a, b],
        scratch=[pltpu.VMEM((m, n), F32)], rider=rider)
    return outs[0] if rider is None else outs


def _ffn_wgrad(hidden, shared, scale, name, rider=None):
    s, d = shared.shape
    half = hidden.shape[1] // 2
    return _wgrad(hidden, shared, lambda tk: pl.BlockSpec((tk, half), lambda j, k: (k, j)),
                  lambda tk: pl.BlockSpec((tk, d), lambda j, k: (k, 0)), half // 2, d, scale, name,
                  min(WGRAD_TILE, s), rider, per_step=2)


def _mix_pre(x, nw, win):
    s, d = x.shape
    nb, _, cb = win.shape
    tm = _tile(s)
    assert s % tm == 0

    def body(x_ref, nw_ref, w_ref, p_ref, hb_ref):
        xv = x_ref[...]
        hb = (xv * _rms_r(xv) * nw_ref[...]).astype(BF16)
        hb_ref[...] = hb
        for j in range(nb):
            p_ref[:, j * cb:(j + 1) * cb] = _dot(hb, w_ref[j])

    row = pl.BlockSpec((tm, d), lambda i: (i, 0))
    return pl.pallas_call(
        body, name="mix_pre", grid=(s // tm,),
        in_specs=[row, pl.BlockSpec((1, d), lambda i: (0, 0)),
                  pl.BlockSpec((nb, d, cb), lambda i: (0, 0, 0), pipeline_mode=pl.Buffered(1))],
        out_specs=[pl.BlockSpec((tm, nb * cb), lambda i: (i, 0)), row],
        out_shape=[jax.ShapeDtypeStruct((s, nb * cb), F32), jax.ShapeDtypeStruct((s, d), BF16)],
        compiler_params=_params(("arbitrary",)),
    )(x, nw, win)


def _mix_pre_bwd(x, nw, dres, dpb, win):
    s, d = x.shape
    nb, _, cb = win.shape
    tm = _tile(s)
    assert s % tm == 0

    def body(x_ref, nw_ref, dres_ref, dp_ref, w_ref, dx_ref, dnw_ref):
        dh = jnp.zeros((tm, d), F32)
        for j in range(nb):
            dh = dh + _dot_nt(dp_ref[:, j * cb:(j + 1) * cb], w_ref[j])
        xv = x_ref[...]
        dx, dn = _rms_bwd(xv, _rms_r(xv), nw_ref[...], dh)
        dx_ref[...] = dres_ref[...] + dx

        @pl.when(pl.program_id(0) == 0)
        def _():
            dnw_ref[...] = jnp.zeros_like(dnw_ref)

        dnw_ref[...] += dn

    row = pl.BlockSpec((tm, d), lambda i: (i, 0))
    vec = pl.BlockSpec((1, d), lambda i: (0, 0))
    return pl.pallas_call(
        body, name="mix_pre_bwd", grid=(s // tm,),
        in_specs=[row, vec, row, pl.BlockSpec((tm, nb * cb), lambda i: (i, 0)),
                  pl.BlockSpec((nb, d, cb), lambda i: (0, 0, 0), pipeline_mode=pl.Buffered(1))],
        out_specs=[row, vec],
        out_shape=[jax.ShapeDtypeStruct((s, d), F32), jax.ShapeDtypeStruct((1, d), F32)],
        compiler_params=_params(("arbitrary",)),
    )(x, nw, dres, dpb, win)


def _mix_post(x, yr, ya, nr, na, wout):
    s, d = x.shape
    h = yr.shape[1]
    tm = _tile(s)

    def body(x_ref, yr_ref, ya_ref, nr_ref, na_ref, w_ref, out_ref):
        yrv = yr_ref[...]
        yav = ya_ref[...]
        onb = (yrv * _rms_r(yrv) * nr_ref[...]).astype(BF16)
        oab = (yav * _rms_r(yav) * na_ref[...]).astype(BF16)
        out_ref[...] = x_ref[...] + _dot(onb, w_ref[0:h, :]) + _dot(oab, w_ref[h:2 * h, :])

    row = pl.BlockSpec((tm, d), lambda i: (i, 0))
    half = pl.BlockSpec((tm, h), lambda i: (i, 0))
    vec = pl.BlockSpec((1, h), lambda i: (0, 0))
    return pl.pallas_call(
        body, name="mix_post", grid=(s // tm,),
        in_specs=[row, half, half, vec, vec, pl.BlockSpec((2 * h, d), lambda i: (0, 0))],
        out_specs=row, out_shape=jax.ShapeDtypeStruct((s, d), F32),
        compiler_params=_params(("arbitrary",)),
    )(x, yr, ya, nr, na, wout)


def _mix_post_bwd(dx, yr, ya, nr, na, wout):
    s, d = dx.shape
    h = yr.shape[1]
    tm = _tile(s)

    def body(dx_ref, yr_ref, ya_ref, nr_ref, na_ref, w_ref,
             dyr_ref, dya_ref, yc_ref, dxb_ref, dnr_ref, dna_ref):
        i = pl.program_id(0)
        dxb = dx_ref[...].astype(BF16)
        dxb_ref[...] = dxb
        dyc = _dot_nt(dxb, w_ref[...])
        yrv = yr_ref[...]
        yav = ya_ref[...]
        rr = _rms_r(yrv)
        ra = _rms_r(yav)
        yc_ref[:, 0:h] = (yrv * rr * nr_ref[...]).astype(BF16)
        yc_ref[:, h:2 * h] = (yav * ra * na_ref[...]).astype(BF16)
        dyr, dnr = _rms_bwd(yrv, rr, nr_ref[...], dyc[:, 0:h])
        dya, dna = _rms_bwd(yav, ra, na_ref[...], dyc[:, h:2 * h])
        dyr_ref[...] = dyr
        dya_ref[...] = dya

        @pl.when(i == 0)
        def _():
            dnr_ref[...] = jnp.zeros_like(dnr_ref)
            dna_ref[...] = jnp.zeros_like(dna_ref)

        dnr_ref[...] += dnr
        dna_ref[...] += dna

    row = pl.BlockSpec((tm, d), lambda i: (i, 0))
    half = pl.BlockSpec((tm, h), lambda i: (i, 0))
    vec = pl.BlockSpec((1, h), lambda i: (0, 0))
    return pl.pallas_call(
        body, name="mix_post_bwd", grid=(s // tm,),
        in_specs=[row, half, half, vec, vec, pl.BlockSpec((2 * h, d), lambda i: (0, 0))],
        out_specs=[half, half, pl.BlockSpec((tm, 2 * h), lambda i: (i, 0)), row, vec, vec],
        out_shape=[jax.ShapeDtypeStruct((s, h), F32), jax.ShapeDtypeStruct((s, h), F32),
                   jax.ShapeDtypeStruct((s, 2 * h), BF16), jax.ShapeDtypeStruct((s, d), BF16),
                   jax.ShapeDtypeStruct((1, h), F32), jax.ShapeDtypeStruct((1, h), F32)],
        compiler_params=_params(("arbitrary",)),
    )(dx, yr, ya, nr, na, wout)


def _shift_down(xv, s, prev8):
    rolled = pltpu.roll(xv, s, 0)
    row8 = lax.broadcasted_iota(jnp.int32, prev8.shape, 0)
    head = jnp.where(row8 < s, pltpu.roll(prev8, s, 0), rolled[0:8, :])
    return jnp.concatenate([head, rolled[8:, :]], axis=0)


def _shift_up(xv, s, next8):
    n = xv.shape[0]
    rolled = pltpu.roll(xv, n - s, 0)
    row8 = lax.broadcasted_iota(jnp.int32, next8.shape, 0)
    tail = jnp.where(row8 >= 8 - s, pltpu.roll(next8, 8 - s, 0), rolled[n - 8:, :])
    return jnp.concatenate([rolled[:n - 8, :], tail], axis=0)


def _scan_fwd(a, b):
    n = a.shape[0]
    sub = lax.broadcasted_iota(jnp.int32, a.shape, 0) % SUBLANES
    s = 1
    while s < SUBLANES:
        ok = sub >= s
        b = jnp.where(ok, a * pltpu.roll(b, s, 0) + b, b)
        a = jnp.where(ok, a * pltpu.roll(a, s, 0), a)
        s *= 2
    groups = []
    before = jnp.zeros((1, a.shape[1]), F32)
    for g in range(n // SUBLANES):
        rows = slice(g * SUBLANES, (g + 1) * SUBLANES)
        groups.append(a[rows] * before + b[rows])
        before = groups[-1][SUBLANES - 1:]
    return jnp.concatenate(groups, axis=0)


def _scan_bwd(a, b):
    n = a.shape[0]
    sub = lax.broadcasted_iota(jnp.int32, a.shape, 0) % SUBLANES
    s = 1
    while s < SUBLANES:
        ok = sub < SUBLANES - s
        b = jnp.where(ok, a * pltpu.roll(b, n - s, 0) + b, b)
        a = jnp.where(ok, a * pltpu.roll(a, n - s, 0), a)
        s *= 2
    groups = []
    after = jnp.zeros((1, a.shape[1]), F32)
    for g in reversed(range(n // SUBLANES)):
        rows = slice(g * SUBLANES, (g + 1) * SUBLANES)
        groups.append(a[rows] * after + b[rows])
        after = groups[-1][:1]
    return jnp.concatenate(groups[::-1], axis=0)


def _rglru_gates(xv, prev8, cw_ref, cb_ref, wa_ref, ba_ref, wx_ref, bx_ref, lam_ref):
    x1 = _shift_down(xv, 1, prev8)
    x2 = _shift_down(xv, 2, prev8)
    x3 = _shift_down(xv, 3, prev8)
    xc = cw_ref[3:4, :] * xv + cw_ref[2:3, :] * x1 + cw_ref[1:2, :] * x2 + cw_ref[0:1, :] * x3 + cb_ref[...]
    xcb = xc.astype(BF16)
    r = _sigmoid(_dot(xcb, wa_ref[...]) + ba_ref[...])
    ig = _sigmoid(_dot(xcb, wx_ref[...]) + bx_ref[...])
    c = RG_C * _log_sigmoid(lam_ref[...])
    la = r * c
    a = jnp.exp(la)
    m = jnp.sqrt(-_expm1_neg(2.0 * la))
    return (x1, x2, x3), xc, xcb, r, ig, c, a, m


def _rglru_fwd(proj, cw, cb, wa, ba, wx, bx, lam):
    s = proj.shape[0]
    w = D_RNN
    tm = _tile(s)

    def body(xr_ref, gate_ref, cw_ref, cb_ref, wa_ref, ba_ref, wx_ref, bx_ref, lam_ref,
             y_ref, h_ref, prev, hlast):
        @pl.when(pl.program_id(0) == 0)
        def _():
            prev[...] = jnp.zeros_like(prev)
            hlast[...] = jnp.zeros_like(hlast)

        xv = xr_ref[...]
        _, xc, _, _, ig, _, a, m = _rglru_gates(xv, prev[...], cw_ref, cb_ref, wa_ref, ba_ref,
                                                wx_ref, bx_ref, lam_ref)
        b = m * (ig * xc)
        row = lax.broadcasted_iota(jnp.int32, b.shape, 0)
        b = jnp.where(row == 0, b + a * hlast[...], b)
        h = _scan_fwd(a, b)
        h_ref[...] = h
        y_ref[...] = h * _gelu(gate_ref[...])
        prev[...] = xv[tm - 8:, :]
        hlast[...] = h[tm - 1:tm, :]

    vec = pl.BlockSpec((1, w), lambda i: (0, 0))
    sq = pl.BlockSpec((w, w), lambda i: (0, 0))
    out = pl.BlockSpec((tm, w), lambda i: (i, 0))
    return pl.pallas_call(
        body, name="rglru_fwd", grid=(s // tm,),
        in_specs=[pl.BlockSpec((tm, w), lambda i: (i, 0)), pl.BlockSpec((tm, w), lambda i: (i, 1)),
                  pl.BlockSpec((CONV_W, w), lambda i: (0, 0)), vec, sq, vec, sq, vec, vec],
        out_specs=[out, out],
        out_shape=[jax.ShapeDtypeStruct((s, w), F32), jax.ShapeDtypeStruct((s, w), F32)],
        scratch_shapes=[pltpu.VMEM((8, w), F32), pltpu.VMEM((1, w), F32)],
        compiler_params=_params(("arbitrary",)),
    )(proj, proj, cw, cb, wa, ba, wx, bx, lam)


def _rglru_bwd(proj, hseq, dyr, cw, cb, wa, ba, wx, bx, lam):
    s = proj.shape[0]
    w = D_RNN
    tm = _tile(s)
    nt = s // tm
    t8 = tm // 8

    def body(xr_ref, xp_ref, gate_ref, h_ref, hp_ref, dy_ref, cw_ref, cb_ref, wa_ref, ba_ref,
             wx_ref, bx_ref, lam_ref,
             dxr_ref, dgate_ref, dcw_ref, dcb_ref, dwa_ref, dba_ref, dwx_ref, dbx_ref, dlam_ref,
             carry, dxc_next):
        i = pl.program_id(0)
        first_tile = i == nt - 1

        @pl.when(i == 0)
        def _():
            carry[...] = jnp.zeros_like(carry)
            dxc_next[...] = jnp.zeros_like(dxc_next)
            for ref in (dcw_ref, dcb_ref, dwa_ref, dba_ref, dwx_ref, dbx_ref, dlam_ref):
                ref[...] = jnp.zeros_like(ref)

        xv = xr_ref[...]
        prev8 = jnp.where(first_tile, 0.0, xp_ref[...])
        hprev8 = jnp.where(first_tile, 0.0, hp_ref[...])
        (x1, x2, x3), xc, xcb, r, ig, c, a, m = _rglru_gates(
            xv, prev8, cw_ref, cb_ref, wa_ref, ba_ref, wx_ref, bx_ref, lam_ref)
        gv = gate_ref[...]
        hv = h_ref[...]
        dy = dy_ref[...]
        dgate_ref[...] = (dy * hv * _gelu_grad(gv)).astype(BF16)
        dh = dy * _gelu(gv)
        row = lax.broadcasted_iota(jnp.int32, dh.shape, 0)
        dh = jnp.where(row == tm - 1, dh + carry[...], dh)
        a_up = jnp.where(row == tm - 1, 0.0, pltpu.roll(a, tm - 1, 0))
        lam_t = _scan_bwd(a_up, dh)
        carry[...] = a[0:1, :] * lam_t[0:1, :]
        hm1 = _shift_down(hv, 1, hprev8)
        da = lam_t * hm1
        ixc = ig * xc
        dm = lam_t * ixc
        dig = lam_t * m * xc
        dxc = lam_t * m * ig
        dla = da * a - dm * (a * a) / m
        dr = dla * c
        dlam_ref[...] += jnp.sum(dla * r, axis=0, keepdims=True)
        dpa = dr * r * (1.0 - r)
        dpi = dig * ig * (1.0 - ig)
        dba_ref[...] += jnp.sum(dpa, axis=0, keepdims=True)
        dbx_ref[...] += jnp.sum(dpi, axis=0, keepdims=True)
        dpab = dpa.astype(BF16)
        dpib = dpi.astype(BF16)
        dwa_ref[...] += _dot_tn(xcb, dpab)
        dwx_ref[...] += _dot_tn(xcb, dpib)
        dxc = dxc + _dot_nt(dpab, wa_ref[...]) + _dot_nt(dpib, wx_ref[...])
        dcb_ref[...] += jnp.sum(dxc, axis=0, keepdims=True)
        dcw_ref[3:4, :] += jnp.sum(dxc * xv, axis=0, keepdims=True)
        dcw_ref[2:3, :] += jnp.sum(dxc * x1, axis=0, keepdims=True)
        dcw_ref[1:2, :] += jnp.sum(dxc * x2, axis=0, keepdims=True)
        dcw_ref[0:1, :] += jnp.sum(dxc * x3, axis=0, keepdims=True)
        nxt = dxc_next[...]
        dxr = (cw_ref[3:4, :] * dxc + cw_ref[2:3, :] * _shift_up(dxc, 1, nxt)
               + cw_ref[1:2, :] * _shift_up(dxc, 2, nxt) + cw_ref[0:1, :] * _shift_up(dxc, 3, nxt))
        dxr_ref[...] = dxr.astype(BF16)
        dxc_next[...] = dxc[0:8, :]

        @pl.when(first_tile)
        def _():
            lv = lam_ref[...]
            dlam_ref[...] = dlam_ref[...] * (RG_C * _sigmoid(-lv))

    rev = lambda i: nt - 1 - i
    vec = pl.BlockSpec((1, w), lambda i: (0, 0))
    sq = pl.BlockSpec((w, w), lambda i: (0, 0))
    cur = lambda col: pl.BlockSpec((tm, w), lambda i: (rev(i), col))
    before = lambda cols: pl.BlockSpec((8, w), lambda i: (jnp.maximum(rev(i) * t8 - 1, 0), 0))
    return pl.pallas_call(
        body, name="rglru_bwd", grid=(nt,),
        in_specs=[cur(0), before(None), cur(1), cur(0), before(None), cur(0),
                  pl.BlockSpec((CONV_W, w), lambda i: (0, 0)), vec, sq, vec, sq, vec, vec],
        out_specs=[cur(0), cur(0), pl.BlockSpec((CONV_W, w), lambda i: (0, 0)), vec, sq, vec, sq, vec, vec],
        out_shape=[jax.ShapeDtypeStruct((s, w), BF16), jax.ShapeDtypeStruct((s, w), BF16),
                   jax.ShapeDtypeStruct((CONV_W, w), F32), jax.ShapeDtypeStruct((1, w), F32),
                   jax.ShapeDtypeStruct((w, w), F32), jax.ShapeDtypeStruct((1, w), F32),
                   jax.ShapeDtypeStruct((w, w), F32), jax.ShapeDtypeStruct((1, w), F32),
                   jax.ShapeDtypeStruct((1, w), F32)],
        scratch_shapes=[pltpu.VMEM((1, w), F32), pltpu.VMEM((8, w), F32)],
        compiler_params=_params(("arbitrary",)),
    )(proj, proj, proj, hseq, hseq, dyr, cw, cb, wa, ba, wx, bx, lam)


def _sb_logs(z, valid):
    l1p = jnp.log(1.0 + jnp.exp(-jnp.abs(z)))
    lb = jnp.minimum(z, 0.0) - l1p
    lm = jnp.where(valid, -jnp.maximum(z, 0.0) - l1p, 0.0)
    return lb, lm


class _Window:
    def __init__(self):
        blk, win, cut = ATT_BLOCK, ATT_WINDOW, ATT_SPLIT
        self.row = lax.broadcasted_iota(jnp.int32, (blk, win), 0)
        self.col = lax.broadcasted_iota(jnp.int32, (blk, win), 1)

        def tri(n, later):
            j = lax.broadcasted_iota(jnp.int32, (n, n), 0)
            s = lax.broadcasted_iota(jnp.int32, (n, n), 1)
            return jnp.where((j > s) if later else (j < s), 1.0, 0.0).astype(BF16)

        self.later = (tri(cut, True), tri(win - cut, True))
        self.earlier = (tri(cut, False), tri(win - cut, False))

    def place(self, qi, g):
        end = (qi + 1) * ATT_BLOCK - g * ATT_WINDOW
        start = pl.multiple_of(jnp.maximum(end - ATT_WINDOW, 0), ATT_BLOCK)
        valid = start + self.col < jnp.minimum(qi * ATT_BLOCK + self.row, end)
        return start, valid

    @staticmethod
    def _parts(xv):
        hi = xv.astype(BF16)
        lo = (xv - hi.astype(F32)).astype(BF16)
        cut = ATT_SPLIT
        sums = (jnp.sum(xv[:, :cut], axis=1, keepdims=True), jnp.sum(xv[:, cut:], axis=1, keepdims=True))
        return (hi[:, :cut], lo[:, :cut]), (hi[:, cut:], lo[:, cut:]), sums

    def sums_after(self, xv, carry):
        (h0, l0), (h1, l1), (s0, s1) = self._parts(xv)
        first = _dot(h0, self.later[0]) + _dot(l0, self.later[0]) + (s1 + carry)
        last = _dot(h1, self.later[1]) + _dot(l1, self.later[1]) + carry
        return jnp.concatenate([first, last], axis=1), s0 + s1

    def sums_before(self, xv, carry):
        (h0, l0), (h1, l1), (s0, s1) = self._parts(xv)
        first = _dot(h0, self.earlier[0]) + _dot(l0, self.earlier[0]) + carry
        last = _dot(h1, self.earlier[1]) + _dot(l1, self.earlier[1]) + (s0 + carry)
        return jnp.concatenate([first, last], axis=1), s0 + s1


class _HeadPair:
    def __init__(self):
        lanes = 2 * HEAD_DIM
        lane = lax.broadcasted_iota(jnp.int32, (1, lanes), 1)
        self.masks = [lane // HEAD_DIM == h for h in (0, 1)]
        i = lax.broadcasted_iota(jnp.int32, (lanes, lanes), 0) // HEAD_DIM
        j = lax.broadcasted_iota(jnp.int32, (lanes, lanes), 1) // HEAD_DIM
        self.same_head = jnp.where(i == j, 1.0, 0.0).astype(BF16)

    def only(self, h, xv):
        return jnp.where(self.masks[h], xv, jnp.zeros_like(xv))

    def merge(self, per_head):
        return jnp.where(self.masks[0], per_head[0], per_head[1])

    def mean(self, xv):
        hi = xv.astype(BF16)
        lo = (xv - hi.astype(F32)).astype(BF16)
        return (_dot(hi, self.same_head) + _dot(lo, self.same_head)) * (1.0 / HEAD_DIM)

    def rms_r(self, xv):
        return lax.rsqrt(self.mean(xv * xv) + EPS)

    def rms_bwd(self, xv, r, nw, dh):
        t = dh * nw
        dx = r * t - xv * (r * r * r * self.mean(t * xv))
        dn = jnp.sum(dh * xv * r, axis=0, keepdims=True)
        return dx, dn[:, :HEAD_DIM] + dn[:, HEAD_DIM:]


def _attn_fwd(proj, qg, kg, rider=None):
    s = proj.shape[0]
    blk, win, dh = ATT_BLOCK, ATT_WINDOW, HEAD_DIM
    nq = s // blk
    scale = 1.0 / math.sqrt(dh)
    heads = (0, 1)
    assert s >= win and s % blk == 0

    def body(*refs):
        (q_ref, k_ref, v_ref, qg_ref, kg_ref), (o_ref,), (qn, kn, vb), copies = _split_refs(refs, 5, 1, rider)
        finish = _ride(copies, pl.program_id(0) == 0, pl.program_id(0) == N_HEADS // 2 - 1)
        wd, hp = _Window(), _HeadPair()
        qv = q_ref[...]
        qn[...] = (qv * hp.rms_r(qv) * qg_ref[...] * scale).astype(BF16)
        kv = k_ref[...]
        kn[...] = (kv * hp.rms_r(kv) * kg_ref[...]).astype(BF16)
        vb[...] = v_ref[...].astype(BF16)

        def q_step(qi, _):
            qoff = pl.multiple_of(qi * blk, blk)
            qt = qn[pl.ds(qoff, blk), :]
            qts = [hp.only(h, qt) for h in heads]

            def more(carry):
                g, live = carry[:2]
                return jnp.logical_and((qi + 1) * blk - g * win > 0, live > 0)

            def window(carry):
                g, _, accs, runs = carry
                start, valid = wd.place(qi, g)
                kt = kn[pl.ds(start, win), :]
                zs = [_dot_nt(qts[h], kt) for h in heads]
                logs = [_sb_logs(z, valid) for z in zs]
                sums = [wd.sums_after(logs[h][1], runs[h]) for h in heads]
                wgts = [jnp.where(valid, jnp.exp(logs[h][0] + sums[h][0]), 0.0).astype(BF16) for h in heads]
                vt = vb[pl.ds(start, win), :]
                accs = tuple(accs[h] + _dot(wgts[h], vt) for h in heads)
                runs = tuple(runs[h] + sums[h][1] for h in heads)
                live = (jnp.maximum(jnp.max(runs[0]), jnp.max(runs[1])) > EXP_ZERO).astype(jnp.int32)
                return g + 1, live, accs, runs

            zero = lambda cols: tuple(jnp.zeros((blk, cols), F32) for _ in heads)
            _, _, accs, _ = lax.while_loop(more, window, (jnp.int32(0), jnp.int32(1), zero(2 * dh), zero(1)))
            o_ref[pl.ds(qoff, blk), :] = hp.merge(accs)
            return 0

        lax.fori_loop(0, nq, q_step, 0)
        finish()

    pair = lambda group: pl.BlockSpec((s, 2 * dh), lambda p: (0, group * (D_ATT // (2 * dh)) + p))
    vec = pl.BlockSpec((1, 2 * dh), lambda p: (0, 0))
    return _call(
        body, "attn_fwd", (N_HEADS // 2,), [pair(2), pair(3), pair(4), vec, vec], [pair(0)],
        [jax.ShapeDtypeStruct((s, D_ATT), F32)], [proj, proj, proj, jnp.tile(qg, (1, 2)), jnp.tile(kg, (1, 2))],
        scratch=[pltpu.VMEM((s, 2 * dh), BF16)] * 3, rider=rider)


def _attn_bwd(proj, dya, qg, kg, rider=None):
    s = proj.shape[0]
    blk, win, dh = ATT_BLOCK, ATT_WINDOW, HEAD_DIM
    nq = s // blk
    max_windows = -(-s // win) + 1
    scale = 1.0 / math.sqrt(dh)
    steps = N_HEADS // 2
    heads = (0, 1)
    assert s >= win and s % blk == 0

    def body(*refs):
        ins, outs, scratch, copies = _split_refs(refs, 6, 5, rider)
        q_ref, k_ref, v_ref, do_ref, qg_ref, kg_ref = ins
        dq_ref, dk_ref, dv_ref, dqg_ref, dkg_ref = outs
        qn, kn, vb, dob, runs_ref, dqn, dkn, dvn = scratch
        finish = _ride(copies, pl.program_id(0) == 0, pl.program_id(0) == steps - 1)
        wd, hp = _Window(), _HeadPair()

        @pl.when(pl.program_id(0) == 0)
        def _():
            dqg_ref[...] = jnp.zeros_like(dqg_ref)
            dkg_ref[...] = jnp.zeros_like(dkg_ref)

        qv = q_ref[...]
        qn[...] = (qv * hp.rms_r(qv) * qg_ref[...] * scale).astype(BF16)
        kv = k_ref[...]
        kn[...] = (kv * hp.rms_r(kv) * kg_ref[...]).astype(BF16)
        vb[...] = v_ref[...].astype(BF16)
        dob[...] = do_ref[...].astype(BF16)
        dkn[...] = jnp.zeros_like(dkn)
        dvn[...] = jnp.zeros_like(dvn)

        def q_step(qi, _):
            qoff = pl.multiple_of(qi * blk, blk)
            qt = qn[pl.ds(qoff, blk), :]
            dot = dob[pl.ds(qoff, blk), :]
            qts = [hp.only(h, qt) for h in heads]
            dots = [hp.only(h, dot) for h in heads]

            zero = lambda cols: tuple(jnp.zeros((blk, cols), F32) for _ in heads)

            def logs_of(g):
                start, valid = wd.place(qi, g)
                kt = kn[pl.ds(start, win), :]
                return [_sb_logs(_dot_nt(qts[h], kt), valid) for h in heads]

            def row_sums(logs):
                return tuple(jnp.sum(logs[h][1], axis=1, keepdims=True) for h in heads)

            def still_live(runs):
                return jnp.maximum(jnp.max(runs[0]), jnp.max(runs[1])) > EXP_ZERO

            def window_grads(g, logs, runs, esums):
                start, valid = wd.place(qi, g)
                kt = kn[pl.ds(start, win), :]
                vt = vb[pl.ds(start, win), :]
                dws = [_dot_nt(dots[h], vt) for h in heads]
                tails = [wd.sums_after(logs[h][1], runs[h])[0] for h in heads]
                wgts = [jnp.where(valid, jnp.exp(logs[h][0] + tails[h]), 0.0) for h in heads]
                es = [dws[h] * wgts[h] for h in heads]
                befores = [wd.sums_before(es[h], esums[h]) for h in heads]
                dzbs = []
                for h in heads:
                    beta = jnp.exp(logs[h][0])
                    dz = jnp.where(valid, es[h] * (1.0 - beta) - befores[h][0] * beta, 0.0)
                    dzbs.append(dz.astype(BF16))
                dkn[pl.ds(start, win), :] += _dot_tn(dzbs[0], qts[0]) + _dot_tn(dzbs[1], qts[1])
                dvn[pl.ds(start, win), :] += (_dot_tn(wgts[0].astype(BF16), dots[0])
                                              + _dot_tn(wgts[1].astype(BF16), dots[1]))
                return tuple(_dot(dzbs[h], kt) for h in heads), tuple(befores[h][1] for h in heads)

            logs0 = logs_of(0)
            runs1 = row_sums(logs0)

            def one_window():
                return window_grads(0, logs0, zero(1), zero(1))[0]

            def all_windows():
                def more(carry):
                    g, live = carry[:2]
                    return jnp.logical_and((qi + 1) * blk - g * win > 0, live > 0)

                def run_window(carry):
                    g, _, runs = carry
                    for h in heads:
                        runs_ref[h, g] = runs[h]
                    sums = row_sums(logs_of(g))
                    runs = tuple(runs[h] + sums[h] for h in heads)
                    return g + 1, still_live(runs).astype(jnp.int32), runs

                for h in heads:
                    runs_ref[h, 0] = jnp.zeros((blk, 1), F32)
                windows, _, _ = lax.while_loop(more, run_window, (jnp.int32(1), jnp.int32(1), runs1))

                def k_window(gg, carry):
                    dq_accs, esums = carry
                    g = windows - 1 - gg
                    parts, totals = window_grads(g, logs_of(g), [runs_ref[h, g] for h in heads], esums)
                    return (tuple(dq_accs[h] + parts[h] for h in heads),
                            tuple(esums[h] + totals[h] for h in heads))

                return lax.fori_loop(0, windows, k_window, (zero(2 * dh), zero(1)))[0]

            earlier_keys = (qi + 1) * blk - win > 0
            dq_accs = lax.cond(jnp.logical_and(earlier_keys, still_live(runs1)), all_windows, one_window)
            dqn[pl.ds(qoff, blk), :] = hp.merge(dq_accs)
            return 0

        lax.fori_loop(0, nq, q_step, 0)

        dq, dqg = hp.rms_bwd(qv, hp.rms_r(qv), qg_ref[...] * scale, dqn[...])
        dq_ref[...] = dq.astype(BF16)
        dqg_ref[...] += dqg * scale
        dk, dkg = hp.rms_bwd(kv, hp.rms_r(kv), kg_ref[...], dkn[...])
        dk_ref[...] = dk.astype(BF16)
        dkg_ref[...] += dkg
        dv_ref[...] = dvn[...].astype(BF16)
        finish()

    pair = lambda group: pl.BlockSpec((s, 2 * dh), lambda p: (0, group * (D_ATT // (2 * dh)) + p))
    vec2 = pl.BlockSpec((1, 2 * dh), lambda p: (0, 0))
    vec = pl.BlockSpec((1, dh), lambda p: (0, 0))
    return _call(
        body, "attn_bwd", (steps,), [pair(2), pair(3), pair(4), pair(0), vec2, vec2],
        [pair(0), pair(0), pair(0), vec, vec],
        [jax.ShapeDtypeStruct((s, D_ATT), BF16)] * 3 + [jax.ShapeDtypeStruct((1, dh), F32)] * 2,
        [proj, proj, proj, dya, jnp.tile(qg, (1, 2)), jnp.tile(kg, (1, 2))],
        scratch=[pltpu.VMEM((s, 2 * dh), BF16)] * 4 + [pltpu.VMEM((2, max_windows, blk, 1), F32)]
        + [pltpu.VMEM((s, 2 * dh), F32)] * 3, rider=rider)


def _block_diag(w):
    n, c, d = w.shape
    return jnp.einsum("ncd,nm->ncmd", w, jnp.eye(n, dtype=w.dtype)).reshape(n * c, n * d)


def _diag_blocks(full, n):
    c = full.shape[0] // n
    return jnp.stack([full[i * c:(i + 1) * c, i * c:(i + 1) * c] for i in range(n)])


FFN1 = ["ffn1_w_gate", "ffn1_w_up", "ffn1_w_down"]
FFN2 = ["ffn2_w_gate", "ffn2_w_up", "ffn2_w_down"]
MIXER = ["w_in", "w_out"]


def _pair_sums(gb, names, where):
    theirs = _pair_exchange([gb[n] for n in names], "pair_exchange_" + names[0])
    pair, own = _pair_sum([gb[n] for n in names], theirs, where, "pair_sum_" + names[0])
    return _chip_rider(pair, own)


def _local_step(x, tgt, stacks, conv_stack, small, where):
    big = dict(zip(FFN1, _gather_weights([stacks[n] for n in FFN1], [])))
    wa = _block_diag(small["rg_w_a"]).astype(BF16)
    wx = _block_diag(small["rg_w_x"]).astype(BF16)

    whole = lambda names: [big[n].reshape(-1, D_MODEL) for n in names]
    x1, g1, u1, hb1, ab1, *landed = _ffn_fwd(x, small["ffn1_norm"], *whole(FFN1),
                                             rider=_gather_rider([stacks[n] for n in MIXER], [conv_stack]))
    big.update(zip(MIXER, landed))
    conv_w = jnp.transpose(landed[-1], (1, 0, 2)).reshape(CONV_W, D_RNN)
    wout = big["w_out"].reshape(D_MODEL, D_MODEL)
    rg = (conv_w, small["conv_b"], wa, small["rg_b_a"], wx, small["rg_b_x"], small["rg_lambda"])
    proj, hb2 = _mix_pre(x1, small["mix_norm"], big["w_in"])
    yr, hseq = _rglru_fwd(proj, *rg)
    ya, *landed = _attn_fwd(proj, small["q_norm"], small["k_norm"], _gather_rider([stacks[n] for n in FFN2], []))
    big.update(zip(FFN2, landed))
    x2 = _mix_post(x1, yr, ya, small["rnn_out_norm"], small["attn_out_norm"], wout)
    dx3, g2, u2, hb3, ab3, loss = _ffn_fwd(x2, small["ffn2_norm"], *whole(FFN2), tgt)

    gb, gs, slots = {}, {}, {}
    dx2, dg2, du2, dyb2, gs["ffn2_norm"] = _ffn_bwd_act(x2, small["ffn2_norm"], dx3, g2, u2, *whole(FFN2), "ffn2_bwd")
    gb["ffn2_w_gate"] = _ffn_wgrad(dg2, hb3, 1.0, "wgrad_gate_ffn2")
    gb["ffn2_w_up"] = _ffn_wgrad(du2, hb3, 1.0, "wgrad_up_ffn2")
    gb["ffn2_w_down"] = _ffn_wgrad(ab3, dyb2, 0.5, "wgrad_down_ffn2")
    dyr, dya, ycat, dxb2, gs["rnn_out_norm"], gs["attn_out_norm"] = _mix_post_bwd(
        dx2, yr, ya, small["rnn_out_norm"], small["attn_out_norm"], wout)
    gb["w_out"] = _wgrad_whole(ycat, dxb2, False, "wgrad_out")
    early = FFN2 + ["w_out"]
    dq, dk, dv, gs["q_norm"], gs["k_norm"], *done = _attn_bwd(
        proj, dya, small["q_norm"], small["k_norm"], _pair_sums(gb, early, where))
    slots.update(zip(early, done))
    dxr, dgate, gs["conv_w"], gs["conv_b"], dwa, gs["rg_b_a"], dwx, gs["rg_b_x"], gs["rg_lambda"] = _rglru_bwd(
        proj, hseq, dyr, *rg)
    gs["rg_w_a"] = _diag_blocks(dwa, RNN_BLOCKS)
    gs["rg_w_x"] = _diag_blocks(dwx, RNN_BLOCKS)
    dpb = jnp.concatenate([dxr, dgate, dq, dk, dv], axis=1)
    dx1, gs["mix_norm"] = _mix_pre_bwd(x1, small["mix_norm"], dx2, dpb, big["w_in"])
    dx0, dg1, du1, dyb1, gs["ffn1_norm"] = _ffn_bwd_act(x, small["ffn1_norm"], dx1, g1, u1, *whole(FFN1), "ffn1_bwd")

    mine = _place_shard(_pack([gs[n] for n in SMALL] + [loss[:, :1]]), where, F32, "place_small_grads",
                        by_device=True)
    gb["ffn1_w_gate"], everyone = _ffn_wgrad(dg1, hb1, 1.0, "wgrad_gate_ffn1", _small_rider(mine))
    gb["ffn1_w_up"], slots["ffn1_w_gate"] = _ffn_wgrad(
        du1, hb1, 1.0, "wgrad_up_ffn1", _pair_sums(gb, ["ffn1_w_gate"], where))
    gb["ffn1_w_down"], slots["ffn1_w_up"] = _ffn_wgrad(
        ab1, dyb1, 0.5, "wgrad_down_ffn1", _pair_sums(gb, ["ffn1_w_up"], where))
    gb["w_in"], slots["ffn1_w_down"] = _wgrad_whole(
        hb2, dpb, True, "wgrad_in", _pair_sums(gb, ["ffn1_w_down"], where))
    last = _pair_sums(gb, ["w_in"], where)
    slots["w_in"], = _chip_exchange(last.plain, last.inplace)
    return dx0, slots, gs, everyone


ANY = pl.BlockSpec(memory_space=pl.ANY)


def _place():
    x, y, c = lax.axis_index("x"), lax.axis_index("y"), lax.axis_index("c")
    other_chips = [(1 - x, y), (x, 1 - y), (1 - x, 1 - y)]
    return x, y, c, 2 * x + y, other_chips


def _remote(src, dst, send_sem, recv_sem, to):
    return pltpu.make_async_remote_copy(src_ref=src, dst_ref=dst, send_sem=send_sem, recv_sem=recv_sem,
                                        device_id=to, device_id_type=MESH)


def _copy_plan(pairs):
    sends = [functools.partial(_remote, *a) for a, _ in pairs]
    arrivals = [functools.partial(_remote, *b) for _, b in pairs]
    return sends, arrivals


class _Rider:
    def __init__(self, plan, plain, inplace, n_copies=None, relay=None, n_relay=0):
        self.plan, self.plain, self.inplace = plan, list(plain), list(inplace)
        self.n_copies = n_copies or 3 * len(self.inplace)
        self.relay, self.n_relay = relay, n_relay

    def operands(self):
        return self.plain + self.inplace

    def out_shape(self):
        return [jax.ShapeDtypeStruct(a.shape, a.dtype) for a in self.inplace]

    def aliases(self, inputs_before, outputs_before):
        return {inputs_before + len(self.plain) + k: outputs_before + k for k in range(len(self.inplace))}

    def scratch(self):
        relay = [pltpu.SemaphoreType.DMA((self.n_relay,))] * 2 if self.relay else []
        return [pltpu.SemaphoreType.DMA((self.n_copies,))] * 2 + relay


def _split_refs(refs, n_in, n_out, rider):
    if rider is None:
        return refs[:n_in], refs[n_in:n_in + n_out], refs[n_in + n_out:], None
    r_in, r_out = len(rider.operands()), len(rider.inplace)
    outs_at = n_in + r_in
    n_sems = len(rider.scratch())
    rest = refs[outs_at + n_out + r_out:]
    sems = rest[len(rest) - n_sems:]
    filled = refs[outs_at + n_out:outs_at + n_out + r_out]
    copies = functools.partial(rider.plan, refs[n_in:n_in + len(rider.plain)], filled, *sems[:2])
    relay = functools.partial(rider.relay, filled, *sems[2:]) if rider.relay else None
    return refs[:n_in], refs[outs_at:outs_at + n_out], rest[:len(rest) - n_sems], (copies, relay)


def _ride(copies, first, last, middle=None):
    if copies is None:
        return lambda: None
    copies, relay = copies

    @pl.when(first)
    def _():
        _start(copies()[0])

    def start_relay():
        for make in copies()[1]:
            make().wait_recv()
        _start(relay()[0])

    if relay is not None and middle is not None:
        pl.when(middle)(start_relay)

    def finish():
        @pl.when(last)
        def _():
            if relay is None:
                _finish(*copies())
            else:
                if middle is None:
                    start_relay()
                _finish(copies()[0] + relay()[0], relay()[1])

    return finish


def _gather_rider(split, whole):
    n_split = len(split)
    return _Rider(lambda plain, stacks, ss, rs: _gather_ici(stacks, n_split, ss, rs), [], list(split) + list(whole),
                  relay=lambda stacks, ss, rs: _gather_d2d(stacks[:n_split], ss, rs), n_relay=3 * n_split)


def _chip_rider(sums, slots):
    return _Rider(_chip_copies, sums, slots)


def _start(makers):
    for make in makers:
        make().start()


def _finish(sends, arrivals):
    for make in arrivals:
        make().wait_recv()
    for make in sends:
        make().wait_send()


def _half(rows, c):
    return pl.ds(pl.multiple_of(c * rows, 16), rows)


def _gather_weights(split, whole):
    arrs = list(split) + list(whole)
    n, ns = len(arrs), len(split)

    def body(*refs):
        outs = refs[n:2 * n]
        send_sems, recv_sems, fsend_sems, frecv_sems = refs[2 * n:]
        sends, arrivals = _gather_ici(outs, ns, send_sems, recv_sems)
        passes, passed = _gather_d2d(outs[:ns], fsend_sems, frecv_sems)
        _start(sends)
        for k, make in enumerate(arrivals):
            make().wait_recv()
            if k < 3 * ns:
                passes[k]().start()
        _finish(sends + passes, passed)

    return pl.pallas_call(
        body, name="gather_weights",
        in_specs=[ANY] * n, out_specs=[ANY] * n,
        out_shape=[jax.ShapeDtypeStruct(a.shape, a.dtype) for a in arrs],
        input_output_aliases={i: i for i in range(n)},
        scratch_shapes=[pltpu.SemaphoreType.DMA((3 * n,)), pltpu.SemaphoreType.DMA((3 * n,)),
                        pltpu.SemaphoreType.DMA((3 * ns,)), pltpu.SemaphoreType.DMA((3 * ns,))],
    )(*arrs)


def _gather_ici(stacks, n_split, send_sems, recv_sems):
    x, y, c, me, chips = _place()

    def region(i, chip):
        if i < n_split:
            return stacks[i].at[chip, _half(stacks[i].shape[1] // 2, c)]
        return stacks[i].at[chip]

    pairs = []
    for i in range(len(stacks)):
        for p, (cx, cy) in enumerate(chips):
            k = 3 * i + p
            mine, got = region(i, me), region(i, 2 * cx + cy)
            sems, to = (send_sems.at[k], recv_sems.at[k]), (cx, cy, c)
            pairs.append(((mine, mine, *sems, to), (got, got, *sems, to)))
    return _copy_plan(pairs)


def _gather_d2d(stacks, send_sems, recv_sems):
    x, y, c, _, chips = _place()
    sibling = (x, y, 1 - c)
    pairs = []
    for i, stack in enumerate(stacks):
        rows = stack.shape[1] // 2
        for p, (cx, cy) in enumerate(chips):
            k = 3 * i + p
            got, theirs = stack.at[2 * cx + cy, _half(rows, c)], stack.at[2 * cx + cy, _half(rows, 1 - c)]
            sems = (send_sems.at[k], recv_sems.at[k])
            pairs.append(((got, got, *sems, sibling), (theirs, theirs, *sems, sibling)))
    return _copy_plan(pairs)


def _pair_exchange(grads, name):
    n = len(grads)

    def body(*refs):
        ins, theirs = refs[:n], refs[n:2 * n]
        send_sems, recv_sems = refs[2 * n:]
        x, y, c, _, _ = _place()
        sibling = (x, y, 1 - c)
        sends = [_remote(ins[k].at[:, _half(grads[k].shape[1] // 2, 1 - c)], theirs[k],
                         send_sems.at[k], recv_sems.at[k], sibling) for k in range(n)]
        for cp in sends:
            cp.start()
        for k in range(n):
            _remote(theirs[k], theirs[k], send_sems.at[k], recv_sems.at[k], sibling).wait_recv()
        for cp in sends:
            cp.wait_send()

    return pl.pallas_call(
        body, name=name,
        in_specs=[ANY] * n, out_specs=[ANY] * n,
        out_shape=[jax.ShapeDtypeStruct((g.shape[0], g.shape[1] // 2, g.shape[2]), g.dtype) for g in grads],
        scratch_shapes=[pltpu.SemaphoreType.DMA((n,))] * 2,
    )(*grads)


def _chip_exchange(sums, slots):
    n = len(sums)

    def body(*refs):
        sends, arrivals = _chip_copies(refs[:n], refs[2 * n:3 * n], *refs[3 * n:])
        _start(sends)
        _finish(sends, arrivals)

    return pl.pallas_call(
        body, name="grad_chip_exchange",
        in_specs=[ANY] * (2 * n), out_specs=[ANY] * n,
        out_shape=[jax.ShapeDtypeStruct(a.shape, a.dtype) for a in slots],
        input_output_aliases={n + k: k for k in range(n)},
        scratch_shapes=[pltpu.SemaphoreType.DMA((3 * n,)), pltpu.SemaphoreType.DMA((3 * n,))],
    )(*sums, *slots)


def _chip_copies(sums, slots, send_sems, recv_sems):
    x, y, c, me, chips = _place()
    pairs = []
    for k in range(len(sums)):
        for p, (cx, cy) in enumerate(chips):
            j = 3 * k + p
            got = slots[k].at[2 * cx + cy]
            sems, to = (send_sems.at[j], recv_sems.at[j]), (cx, cy, c)
            pairs.append(((sums[k].at[2 * cx + cy], slots[k].at[me], *sems, to), (got, got, *sems, to)))
    return _copy_plan(pairs)


def _half_swap(halves):
    n = len(halves)

    def body(*refs):
        outs = refs[n:2 * n]
        send_sems, recv_sems = refs[2 * n:]
        x, y, c, _, _ = _place()
        sibling = (x, y, 1 - c)
        sends = [_remote(outs[k].at[c], outs[k].at[c], send_sems.at[k], recv_sems.at[k], sibling) for k in range(n)]
        for cp in sends:
            cp.start()
        for k in range(n):
            got = outs[k].at[1 - c]
            _remote(got, got, send_sems.at[k], recv_sems.at[k], sibling).wait_recv()
        for cp in sends:
            cp.wait_send()

    return pl.pallas_call(
        body, name="grad_half_swap",
        in_specs=[ANY] * n, out_specs=[ANY] * n,
        out_shape=[jax.ShapeDtypeStruct(a.shape, a.dtype) for a in halves],
        input_output_aliases={k: k for k in range(n)},
        scratch_shapes=[pltpu.SemaphoreType.DMA((n,))] * 2,
    )(*halves)


def _small_rider(stack):
    n_dev = 2 * N_CHIPS

    def plan(_, stacks, send_sems, recv_sems):
        x, y, c, _, _ = _place()
        mine = stacks[0].at[4 * x + 2 * y + c]
        pairs = []
        for k in range(1, n_dev):
            px, py, pc = x ^ ((k >> 2) & 1), y ^ ((k >> 1) & 1), c ^ (k & 1)
            got = stacks[0].at[4 * px + 2 * py + pc]
            sems = (send_sems.at[k - 1], recv_sems.at[k - 1])
            pairs.append(((mine, mine, *sems, (px, py, pc)), (got, got, *sems, (px, py, pc))))
        return _copy_plan(pairs)

    return _Rider(plan, [], [stack], n_dev - 1)


def _row_tile(r):
    return r // 4 if r >= 256 and (r // 4) % 16 == 0 else r


def _prefetch_call(body, name, grid, in_specs, out_specs, out_shape):
    spec = pltpu.PrefetchScalarGridSpec(num_scalar_prefetch=1, grid=grid, in_specs=in_specs, out_specs=out_specs)
    return pl.pallas_call(body, name=name, grid_spec=spec, out_shape=out_shape,
                          compiler_params=_params(("arbitrary",) * len(grid)))


def _place_shard(w2d, where, dtype, name, by_device=False):
    r, c = w2d.shape
    tr = _row_tile(r)
    slots = 2 * N_CHIPS if by_device else N_CHIPS
    slot = (lambda s: 2 * s[1] + s[0]) if by_device else (lambda s: s[1])

    def body(where_ref, w_ref, out_ref):
        out_ref[...] = w_ref[...].astype(dtype)

    return _prefetch_call(
        body, name, (r // tr,), [pl.BlockSpec((tr, c), lambda i, s: (i, 0))],
        pl.BlockSpec((None, tr, c), lambda i, s: (slot(s), i, 0)),
        jax.ShapeDtypeStruct((slots, r, c), dtype))(where, w2d)


def _place_shards(w2ds, where, name):
    n = len(w2ds)
    steps = N_CHIPS
    assert all(w.shape[0] % (16 * steps) == 0 for w in w2ds)

    def body(where_ref, *refs):
        for k in range(n):
            refs[n + k][...] = refs[k][...].astype(BF16)

    tile = lambda w: (w.shape[0] // steps, w.shape[1])
    return _prefetch_call(
        body, name, (steps,), [pl.BlockSpec(tile(w), lambda i, s: (i, 0)) for w in w2ds],
        [pl.BlockSpec((None,) + tile(w), lambda i, s: (s[1], i, 0)) for w in w2ds],
        [jax.ShapeDtypeStruct((N_CHIPS,) + w.shape, BF16) for w in w2ds])(where, *w2ds)


def _pair_sum(fulls, theirs, where, name):
    n = len(fulls)

    def body(where_ref, *refs):
        for k in range(n):
            a_ref, b_ref, out_ref, own_ref = refs[k], refs[n + k], refs[2 * n + k], refs[3 * n + k]
            total = (a_ref[...].astype(F32) + b_ref[...].astype(F32)).astype(BF16)
            out_ref[...] = total

            @pl.when(pl.program_id(0) == where_ref[1])
            def _():
                own_ref[...] = total

    half = lambda t: pl.BlockSpec((None,) + t.shape[1:], lambda j, s: (j, s[0], 0))
    blk = lambda t: pl.BlockSpec((None,) + t.shape[1:], lambda j, s: (j, 0, 0))
    own = lambda t: pl.BlockSpec((None,) + t.shape[1:], lambda j, s: (s[1], 0, 0))
    shapes = [jax.ShapeDtypeStruct(t.shape, BF16) for t in theirs]
    outs = _prefetch_call(
        body, name, (N_CHIPS,), [half(t) for t in theirs] + [blk(t) for t in theirs],
        [blk(t) for t in theirs] + [own(t) for t in theirs], shapes + shapes)(where, *fulls, *theirs)
    return outs[:n], outs[n:]


def _chip_sum(slots, where, name):
    n = len(slots)
    steps = 2
    assert all(a.shape[1] % (16 * steps) == 0 for a in slots)

    def body(where_ref, *refs):
        for k in range(n):
            a_ref, out_ref = refs[k], refs[n + k]
            total = a_ref[0].astype(F32)
            for j in range(1, a_ref.shape[0]):
                total = total + a_ref[j].astype(F32)
            out_ref[...] = total

    tile = lambda a: (a.shape[1] // steps, a.shape[2])
    return _prefetch_call(
        body, name, (steps,), [pl.BlockSpec((a.shape[0],) + tile(a), lambda i, s: (0, i, 0)) for a in slots],
        [pl.BlockSpec((None,) + tile(a), lambda i, s: (s[0], i, 0)) for a in slots],
        [jax.ShapeDtypeStruct((2,) + a.shape[1:], F32) for a in slots])(where, *slots)


def _slot_sum(a, name):
    nb, r, c = a.shape
    tr = _row_tile(r)

    def body(a_ref, out_ref):
        total = a_ref[0].astype(F32)
        for j in range(1, nb):
            total = total + a_ref[j].astype(F32)
        out_ref[...] = total

    return pl.pallas_call(
        body, name=name, grid=(r // tr,),
        in_specs=[pl.BlockSpec((nb, tr, c), lambda i: (0, i, 0))],
        out_specs=pl.BlockSpec((tr, c), lambda i: (i, 0)),
        out_shape=jax.ShapeDtypeStruct((r, c), F32), compiler_params=_params(("arbitrary",)),
    )(a)


def _adamw(ws, gs, ms, vs, name, steps=1):
    n = len(ws)
    c1 = 1.0 - ADAM_B1 ** ADAM_STEP
    c2 = 1.0 - ADAM_B2 ** ADAM_STEP
    assert all(w.shape[0] % steps == 0 and (steps == 1 or w.shape[0] // steps % 8 == 0) for w in ws)

    def body(*refs):
        for k in range(n):
            w_ref, g_ref, m_ref, v_ref = (refs[j * n + k] for j in range(4))
            d_ref, m2_ref, v2_ref = (refs[(4 + j) * n + k] for j in range(3))
            gv = g_ref[...]
            m2 = ADAM_B1 * m_ref[...] + (1.0 - ADAM_B1) * gv
            v2 = ADAM_B2 * v_ref[...] + (1.0 - ADAM_B2) * (gv * gv)
            m2_ref[...] = m2
            v2_ref[...] = v2
            d_ref[...] = -ADAM_LR * ((m2 / c1) / (jnp.sqrt(v2 / c2) + ADAM_EPS) + ADAM_WD * w_ref[...])

    blks = [pl.BlockSpec((w.shape[0] // steps, w.shape[1]), lambda i: (i, 0)) for w in ws]
    shapes = [jax.ShapeDtypeStruct(w.shape, F32) for w in ws]
    outs = pl.pallas_call(
        body, name=name, grid=(steps,), in_specs=blks * 4, out_specs=blks * 3, out_shape=shapes * 3,
        compiler_params=_params(("arbitrary",)),
    )(*ws, *gs, *ms, *vs)
    return outs[:n], outs[n:2 * n], outs[2 * n:]


WEIGHTS = ["ffn1_norm", "ffn1_w_gate", "ffn1_w_up", "ffn1_w_down", "mix_norm", "w_in", "conv_w", "conv_b",
           "rg_w_a", "rg_b_a", "rg_w_x", "rg_b_x", "rg_lambda", "q_norm", "k_norm", "rnn_out_norm",
           "attn_out_norm", "w_out", "ffn2_norm", "ffn2_w_gate", "ffn2_w_up", "ffn2_w_down"]
BIG = ["ffn1_w_gate", "ffn1_w_up", "ffn1_w_down", "w_in", "w_out", "ffn2_w_gate", "ffn2_w_up", "ffn2_w_down"]
SMALL = [n for n in WEIGHTS if n not in BIG]
PACK_LANES = 128
PACK_ROW_ALIGN = 8


def _hidden_major(name, a):
    return jnp.transpose(a) if name.endswith(("w_gate", "w_up")) else a


def _pack(parts):
    flat = jnp.concatenate([p.reshape(-1) for p in parts])
    unit = PACK_LANES * PACK_ROW_ALIGN
    padded = -(-flat.shape[0] // unit) * unit
    return jnp.pad(flat, (0, padded - flat.shape[0])).reshape(-1, PACK_LANES)


def _unpack(packed, shapes):
    flat = packed.reshape(-1)
    out, at = [], 0
    for shp in shapes:
        size = math.prod(shp)
        out.append(flat[at:at + size].reshape(shp))
        at += size
    return out


def kernel(x, ffn1_norm, ffn1_w_gate, ffn1_w_up, ffn1_w_down, mix_norm, w_in, conv_w, conv_b, rg_w_a, rg_b_a, rg_w_x, rg_b_x, rg_lambda, q_norm, k_norm, rnn_out_norm, attn_out_norm, w_out, ffn2_norm, ffn2_w_gate, ffn2_w_up, ffn2_w_down, loss_target, m_ffn1_norm, m_ffn1_w_gate, m_ffn1_w_up, m_ffn1_w_down, m_mix_norm, m_w_in, m_conv_w, m_conv_b, m_rg_w_a, m_rg_b_a, m_rg_w_x, m_rg_b_x, m_rg_lambda, m_q_norm, m_k_norm, m_rnn_out_norm, m_attn_out_norm, m_w_out, m_ffn2_norm, m_ffn2_w_gate, m_ffn2_w_up, m_ffn2_w_down, v_ffn1_norm, v_ffn1_w_gate, v_ffn1_w_up, v_ffn1_w_down, v_mix_norm, v_w_in, v_conv_w, v_conv_b, v_rg_w_a, v_rg_b_a, v_rg_w_x, v_rg_b_x, v_rg_lambda, v_q_norm, v_k_norm, v_rnn_out_norm, v_attn_out_norm, v_w_out, v_ffn2_norm, v_ffn2_w_gate, v_ffn2_w_up, v_ffn2_w_down):
    given = dict(locals())
    w = {n: given[n] for n in WEIGHTS}
    m = {n: given["m_" + n] for n in WEIGHTS}
    v = {n: given["v_" + n] for n in WEIGHTS}
    chip = 2 * lax.axis_index("x") + lax.axis_index("y")

    where = jnp.stack([lax.axis_index("c"), chip]).astype(jnp.int32)

    stacks = dict(zip(BIG, _place_shards([_hidden_major(n, w[n][0]) for n in BIG], where, "place_weights")))
    conv_stack = _place_shard(w["conv_w"][0], where, F32, "place_conv_w")
    small = {n: (w[n][0] if w[n].ndim > 2 else w[n]) for n in SMALL if n != "conv_w"}

    grad_x, slots, gs, everyone = _local_step(x[0], loss_target[0], stacks, conv_stack, small, where)

    swapped = _half_swap(_chip_sum([slots[n] for n in BIG], where, "chip_sums"))
    g2s = [t.reshape(t.shape[0] * t.shape[1], t.shape[2]) for t in swapped]
    flat = lambda tree: [_hidden_major(n, tree[n][0]) for n in BIG]
    d2s, m2s, v2s = _adamw(flat(w), g2s, flat(m), flat(v), "adamw_weights", ADAMW_STEPS)
    grads, deltas, new_m, new_v = {}, {}, {}, {}
    for tree, parts in ((grads, g2s), (deltas, d2s), (new_m, m2s), (new_v, v2s)):
        tree.update({n: _hidden_major(n, a).reshape(w[n].shape) for n, a in zip(BIG, parts)})

    full_shapes = [gs[n].shape for n in SMALL]
    *summed, loss = _unpack(_slot_sum(everyone, "small_grad_sum"), full_shapes + [(1, 1)])
    g_parts = dict(zip(SMALL, summed))
    quarter = D_RNN // N_CHIPS
    g_parts["conv_w"] = lax.dynamic_slice_in_dim(g_parts["conv_w"], chip * quarter, quarter, axis=1)
    local_shapes = [w[n].shape for n in SMALL]
    pk = lambda tree: _pack([tree[n] for n in SMALL])
    (d_s,), (m_s,), (v_s,) = _adamw([pk(w)], [pk(g_parts)], [pk(m)], [pk(v)], "adamw_small")
    for tree, packed in ((grads, pk(g_parts)), (deltas, d_s), (new_m, m_s), (new_v, v_s)):
        tree.update(zip(SMALL, _unpack(packed, local_shapes)))

    return (loss[0, 0], grad_x.reshape(x.shape), *[grads[n] for n in WEIGHTS], *[deltas[n] for n in WEIGHTS],
            *[new_m[n] for n in WEIGHTS], *[new_v[n] for n in WEIGHTS])
```

```python
import functools
import math

import jax
import jax.numpy as jnp
from jax import lax
from jax.experimental import pallas as pl
from jax.experimental.pallas import tpu as pltpu

F32 = jnp.float32
BF16 = jnp.bfloat16
MESH = pl.DeviceIdType.MESH

D_MODEL = 1024
N_CHIPS = 4
D_RNN = 512
D_ATT = 512
N_HEADS = 8
HEAD_DIM = 64
RNN_BLOCKS = 8
CONV_W = 4
RG_C = 8.0
N_IN = 2 * D_RNN + 3 * D_ATT
EPS = 1e-6
ATT_BLOCK = 128
ATT_WINDOW = 384
ATT_SPLIT = 256
EXP_ZERO = -105.0

ADAM_LR = 0.001
ADAM_B1 = 0.9
ADAM_B2 = 0.999
ADAM_EPS = 1e-08
ADAM_WD = 0.01
ADAM_STEP = 10

V7X_VMEM_LIMIT = 56 * 1024 * 1024
V7X_MXU_WIDTH = 256
TOKEN_TILE = 512
SUBLANES = 8
FFN_TILE = 256
WGRAD_TILE = 2048
WHOLE_TILE = 1024
ADAMW_STEPS = 8

GELU_K0 = math.sqrt(2.0 / math.pi)
GELU_K1 = 0.044715


def _params(sem=None):
    return pltpu.CompilerParams(dimension_semantics=sem, vmem_limit_bytes=V7X_VMEM_LIMIT)


def _dot(a, b):
    return jnp.dot(a, b, preferred_element_type=F32)


def _dot_nt(a, b):
    return lax.dot_general(a, b, (((1,), (1,)), ((), ())), preferred_element_type=F32)


def _dot_tn(a, b):
    return lax.dot_general(a, b, (((0,), (0,)), ((), ())), preferred_element_type=F32)


def _sigmoid(x):
    return 1.0 / (1.0 + jnp.exp(-x))


def _rms_r(xv):
    return lax.rsqrt(jnp.mean(xv * xv, axis=-1, keepdims=True) + EPS)


def _rms_bwd(xv, r, nw, dh):
    t = dh * nw
    dx = r * t - xv * (r * r * r * jnp.mean(t * xv, axis=-1, keepdims=True))
    dn = jnp.sum(dh * xv * r, axis=0, keepdims=True)
    return dx, dn


def _gelu(x):
    t = jnp.tanh(GELU_K0 * (x + GELU_K1 * x * x * x))
    return 0.5 * x * (1.0 + t)


def _gelu_grad(x):
    t = jnp.tanh(GELU_K0 * (x + GELU_K1 * x * x * x))
    return 0.5 * (1.0 + t) + 0.5 * x * (1.0 - t * t) * (GELU_K0 * (1.0 + 3.0 * GELU_K1 * x * x))


def _expm1_neg(x):
    p = 1.0 + x * (1.0 / 6.0)
    for k in (5.0, 4.0, 3.0, 2.0):
        p = 1.0 + x * (1.0 / k) * p
    return jnp.where(x > -0.25, x * p, jnp.exp(x) - 1.0)


def _log_sigmoid(x):
    return jnp.minimum(x, 0.0) - jnp.log(1.0 + jnp.exp(-jnp.abs(x)))


def _tile(s):
    return min(TOKEN_TILE, s)


def _ffn_chunks(f):
    cut = f // 2 // V7X_MXU_WIDTH * V7X_MXU_WIDTH
    return ((0, cut), (cut, f)) if 0 < cut < f else ((0, f),)


def _ffn_fwd(x, nw, wg, wu, wd, tgt=None, rider=None):
    s, d = x.shape
    f = wg.shape[0]
    tm = min(FFN_TILE, s)
    ni = s // tm
    assert s % tm == 0
    with_loss = tgt is not None
    n_in, n_out = 5 + with_loss, 5 + with_loss

    def body(*refs):
        ins, outs, _, copies = _split_refs(refs, n_in, n_out, rider)
        x_ref, nw_ref, wg_ref, wu_ref, wd_ref = ins[:5]
        out_ref, g_ref, u_ref, hb_ref, ab_ref = outs[:5]
        i = pl.program_id(0)
        finish = _ride(copies, i == 0, i == ni - 1, i == 3 * ni // 4)

        xv = x_ref[...]
        hb = (xv * _rms_r(xv) * nw_ref[...]).astype(BF16)
        hb_ref[...] = hb
        y = jnp.zeros((tm, d), F32)
        for lo, hi in _ffn_chunks(f):
            g = _dot_nt(hb, wg_ref[lo:hi, :])
            u = _dot_nt(hb, wu_ref[lo:hi, :])
            g_ref[:, lo:hi] = g.astype(BF16)
            u_ref[:, lo:hi] = u.astype(BF16)
            ab = (g * _sigmoid(g) * u).astype(BF16)
            ab_ref[:, lo:hi] = ab
            y = y + _dot(ab, wd_ref[lo:hi, :])
        y = xv + 0.5 * y
        if with_loss:
            tgt_ref, loss_ref = ins[5], outs[5]
            diff = y - tgt_ref[...]
            out_ref[...] = diff * (1.0 / d)

            @pl.when(i == 0)
            def _():
                loss_ref[...] = jnp.zeros_like(loss_ref)

            loss_ref[...] += jnp.sum(diff * diff) * (0.5 / d)
        else:
            out_ref[...] = y
        finish()

    row = pl.BlockSpec((tm, d), lambda i: (i, 0))
    weight = pl.BlockSpec((f, d), lambda i: (0, 0), pipeline_mode=pl.Buffered(1))
    in_specs = [row, pl.BlockSpec((1, d), lambda i: (0, 0)), weight, weight, weight]
    args = [x, nw, wg, wu, wd]
    if with_loss:
        in_specs.append(row)
        args.append(tgt)
    blk = pl.BlockSpec((tm, f), lambda i: (i, 0))
    out_shape = [jax.ShapeDtypeStruct((s, d), F32), jax.ShapeDtypeStruct((s, f), BF16),
                 jax.ShapeDtypeStruct((s, f), BF16), jax.ShapeDtypeStruct((s, d), BF16),
                 jax.ShapeDtypeStruct((s, f), BF16)]
    out_specs = [row, blk, blk, row, blk]
    if with_loss:
        out_shape.append(jax.ShapeDtypeStruct((1, 128), F32))
        out_specs.append(pl.BlockSpec((1, 128), lambda i: (0, 0)))
    return _call(body, "ffn_fwd_loss" if with_loss else "ffn_fwd", (ni,), in_specs, out_specs, out_shape, args,
                 rider=rider)


def _ffn_up(x, nw, wg, wu, rider=None):
    s, d = x.shape
    f = wg.shape[0]
    tm = min(FFN_TILE, s)
    ni = s // tm
    assert s % tm == 0

    def body(*refs):
        (x_ref, nw_ref, wg_ref, wu_ref), (g_ref, u_ref, hb_ref, ab_ref), _, copies = _split_refs(refs, 4, 4, rider)
        i = pl.program_id(0)
        finish = _ride(copies, i == 0, i == ni - 1, i == 3 * ni // 4)
        xv = x_ref[...]
        hb = (xv * _rms_r(xv) * nw_ref[...]).astype(BF16)
        hb_ref[...] = hb
        for lo, hi in _ffn_chunks(f):
            g = _dot_nt(hb, wg_ref[lo:hi, :])
            u = _dot_nt(hb, wu_ref[lo:hi, :])
            g_ref[:, lo:hi] = g.astype(BF16)
            u_ref[:, lo:hi] = u.astype(BF16)
            ab_ref[:, lo:hi] = (g * _sigmoid(g) * u).astype(BF16)
        finish()

    row = pl.BlockSpec((tm, d), lambda i: (i, 0))
    weight = pl.BlockSpec((f, d), lambda i: (0, 0), pipeline_mode=pl.Buffered(1))
    blk = pl.BlockSpec((tm, f), lambda i: (i, 0))
    wide = jax.ShapeDtypeStruct((s, f), BF16)
    return _call(body, "ffn_up", (ni,), [row, pl.BlockSpec((1, d), lambda i: (0, 0)), weight, weight],
                 [blk, blk, row, blk], [wide, wide, jax.ShapeDtypeStruct((s, d), BF16), wide], [x, nw, wg, wu],
                 rider=rider)


def _ffn_down(x, ab, wd, rider=None):
    s, d = x.shape
    f = wd.shape[0]
    tm = min(FFN_TILE, s)
    ni = s // tm
    assert s % tm == 0

    def body(*refs):
        (x_ref, ab_ref, wd_ref), (out_ref,), _, copies = _split_refs(refs, 3, 1, rider)
        i = pl.program_id(0)
        finish = _ride(copies, i == 0, i == ni - 1, i == 3 * ni // 4)
        out_ref[...] = x_ref[...] + 0.5 * _dot(ab_ref[...], wd_ref[...])
        finish()

    row = pl.BlockSpec((tm, d), lambda i: (i, 0))
    return _call(body, "ffn_down", (ni,),
                 [row, pl.BlockSpec((tm, f), lambda i: (i, 0)),
                  pl.BlockSpec((f, d), lambda i: (0, 0), pipeline_mode=pl.Buffered(1))],
                 [row], [jax.ShapeDtypeStruct((s, d), F32)], [x, ab, wd], rider=rider)


def _call(body, name, grid, in_specs, out_specs, out_shape, args, scratch=(), rider=None):
    in_specs, out_specs, out_shape, scratch = list(in_specs), list(out_specs), list(out_shape), list(scratch)
    extra, aliases = [], {}
    if rider is not None:
        extra = rider.operands()
        aliases = rider.aliases(len(args), len(out_shape))
        in_specs += [ANY] * len(extra)
        out_specs += [ANY] * len(rider.inplace)
        out_shape += rider.out_shape()
        scratch += rider.scratch()
    return pl.pallas_call(
        body, name=name, grid=grid, in_specs=in_specs, out_specs=out_specs, out_shape=out_shape,
        input_output_aliases=aliases, scratch_shapes=scratch,
        compiler_params=_params(("arbitrary",) * len(grid)),
    )(*args, *extra)


def _ffn_bwd_act(x, nw, dy, g, u, wg, wu, wd, name, rider=None):
    s, d = x.shape
    f = wg.shape[0]
    tm = min(FFN_TILE, s)
    assert s % tm == 0

    def body(*refs):
        ins, outs, _, copies = _split_refs(refs, 8, 5, rider)
        x_ref, nw_ref, dy_ref, g_ref, u_ref, wg_ref, wu_ref, wd_ref = ins
        dx_ref, dg_ref, du_ref, dyb_ref, dnw_ref = outs
        finish = _ride(copies, pl.program_id(0) == 0, pl.program_id(0) == s // tm - 1)
        dyv = dy_ref[...]
        dyb = dyv.astype(BF16)
        dyb_ref[...] = dyb
        dh = jnp.zeros((tm, d), F32)
        for lo, hi in _ffn_chunks(f):
            da = 0.5 * _dot_nt(dyb, wd_ref[lo:hi, :])
            gv = g_ref[:, lo:hi].astype(F32)
            sg = _sigmoid(gv)
            dub = (da * (gv * sg)).astype(BF16)
            dgb = (da * u_ref[:, lo:hi].astype(F32) * (sg * (1.0 + gv * (1.0 - sg)))).astype(BF16)
            dg_ref[:, lo:hi] = dgb
            du_ref[:, lo:hi] = dub
            dh = dh + _dot(dgb, wg_ref[lo:hi, :]) + _dot(dub, wu_ref[lo:hi, :])
        xv = x_ref[...]
        dx, dn = _rms_bwd(xv, _rms_r(xv), nw_ref[...], dh)
        dx_ref[...] = dyv + dx

        @pl.when(pl.program_id(0) == 0)
        def _():
            dnw_ref[...] = jnp.zeros_like(dnw_ref)

        dnw_ref[...] += dn
        finish()

    row = pl.BlockSpec((tm, d), lambda i: (i, 0))
    vec = pl.BlockSpec((1, d), lambda i: (0, 0))
    blk = pl.BlockSpec((tm, f), lambda i: (i, 0))
    weight = pl.BlockSpec((f, d), lambda i: (0, 0), pipeline_mode=pl.Buffered(1))
    return _call(
        body, name, (s // tm,), [row, vec, row, blk, blk, weight, weight, weight], [row, blk, blk, row, vec],
        [jax.ShapeDtypeStruct((s, d), F32), jax.ShapeDtypeStruct((s, f), BF16),
         jax.ShapeDtypeStruct((s, f), BF16), jax.ShapeDtypeStruct((s, d), BF16),
         jax.ShapeDtypeStruct((1, d), F32)],
        [x, nw, dy, g, u, wg, wu, wd], rider=rider)


def _wgrad(a, b, a_spec, b_spec, out_rows, out_cols, scale, name, tk, rider=None, per_step=1):
    s = a.shape[-2]
    nk = s // tk
    steps = N_CHIPS // per_step
    assert s % tk == 0

    def body(*refs):
        (a_ref, b_ref), (out_ref,), (acc,), copies = _split_refs(refs, 2, 1, rider)
        j, k = pl.program_id(0), pl.program_id(1)
        finish = _ride(copies, jnp.logical_and(j == 0, k == 0), jnp.logical_and(j == steps - 1, k == nk - 1))

        @pl.when(k == 0)
        def _():
            acc[...] = jnp.zeros_like(acc)

        acc[...] += _dot_tn(a_ref[...], b_ref[...])

        @pl.when(k == nk - 1)
        def _():
            for t in range(per_step):
                out_ref[t] = (acc[t * out_rows:(t + 1) * out_rows, :] * scale).astype(BF16)

        finish()

    outs = _call(
        body, name, (steps, nk), [a_spec(tk), b_spec(tk)],
        [pl.BlockSpec((per_step, out_rows, out_cols), lambda j, k: (j, 0, 0))],
        [jax.ShapeDtypeStruct((N_CHIPS, out_rows, out_cols), BF16)], [a, b],
        scratch=[pltpu.VMEM((per_step * out_rows, out_cols), F32)], rider=rider)
    return outs[0] if rider is None else outs


def _wgrad_whole(a, b, col_blocks, name, rider=None):
    s, m = a.shape
    n = b.shape[1]
    tk = min(WHOLE_TILE, s)
    nk = s // tk
    assert s % tk == 0
    out_shape = (N_CHIPS, m, n // N_CHIPS) if col_blocks else (N_CHIPS, m // N_CHIPS, n)

    def body(*refs):
        (a_ref, b_ref), (out_ref,), (acc,), copies = _split_refs(refs, 2, 1, rider)
        k = pl.program_id(0)
        finish = _ride(copies, k == 0, k == nk - 1)

        @pl.when(k == 0)
        def _():
            acc[...] = jnp.zeros_like(acc)

        acc[...] += _dot_tn(a_ref[...], b_ref[...])

        @pl.when(k == nk - 1)
        def _():
            for j in range(N_CHIPS):
                if col_blocks:
                    out_ref[j] = acc[:, j * out_shape[2]:(j + 1) * out_shape[2]].astype(BF16)
                else:
                    out_ref[j] = acc[j * out_shape[1]:(j + 1) * out_shape[1], :].astype(BF16)

        finish()

    outs = _call(
        body, name, (nk,), [pl.BlockSpec((tk, m), lambda k: (k, 0)), pl.BlockSpec((tk, n), lambda k: (k, 0))],
        [pl.BlockSpec(out_shape, lambda k: (0, 0, 0))], [jax.ShapeDtypeStruct(out_shape, BF16)], [a, b],
        scratch=[pltpu.VMEM((m, n), F32)], rider=rider)
    return outs[0] if rider is None else outs


def _ffn_wgrad(hidden, shared, scale, name, rider=None):
    s, d = shared.shape
    half = hidden.shape[1] // 2
    return _wgrad(hidden, shared, lambda tk: pl.BlockSpec((tk, half), lambda j, k: (k, j)),
                  lambda tk: pl.BlockSpec((tk, d), lambda j, k: (k, 0)), half // 2, d, scale, name,
                  min(WGRAD_TILE, s), rider, per_step=2)


def _mix_pre(x, nw, win):
    s, d = x.shape
    nb, _, cb = win.shape
    tm = _tile(s)
    assert s % tm == 0

    def body(x_ref, nw_ref, w_ref, p_ref, hb_ref):
        xv = x_ref[...]
        hb = (xv * _rms_r(xv) * nw_ref[...]).astype(BF16)
        hb_ref[...] = hb
        for j in range(nb):
            p_ref[:, j * cb:(j + 1) * cb] = _dot(hb, w_ref[j])

    row = pl.BlockSpec((tm, d), lambda i: (i, 0))
    return pl.pallas_call(
        body, name="mix_pre", grid=(s // tm,),
        in_specs=[row, pl.BlockSpec((1, d), lambda i: (0, 0)),
                  pl.BlockSpec((nb, d, cb), lambda i: (0, 0, 0), pipeline_mode=pl.Buffered(1))],
        out_specs=[pl.BlockSpec((tm, nb * cb), lambda i: (i, 0)), row],
        out_shape=[jax.ShapeDtypeStruct((s, nb * cb), F32), jax.ShapeDtypeStruct((s, d), BF16)],
        compiler_params=_params(("arbitrary",)),
    )(x, nw, win)


def _mix_pre_bwd(x, nw, dres, dpb, win):
    s, d = x.shape
    nb, _, cb = win.shape
    tm = _tile(s)
    assert s % tm == 0

    def body(x_ref, nw_ref, dres_ref, dp_ref, w_ref, dx_ref, dnw_ref):
        dh = jnp.zeros((tm, d), F32)
        for j in range(nb):
            dh = dh + _dot_nt(dp_ref[:, j * cb:(j + 1) * cb], w_ref[j])
        xv = x_ref[...]
        dx, dn = _rms_bwd(xv, _rms_r(xv), nw_ref[...], dh)
        dx_ref[...] = dres_ref[...] + dx

        @pl.when(pl.program_id(0) == 0)
        def _():
            dnw_ref[...] = jnp.zeros_like(dnw_ref)

        dnw_ref[...] += dn

    row = pl.BlockSpec((tm, d), lambda i: (i, 0))
    vec = pl.BlockSpec((1, d), lambda i: (0, 0))
    return pl.pallas_call(
        body, name="mix_pre_bwd", grid=(s // tm,),
        in_specs=[row, vec, row, pl.BlockSpec((tm, nb * cb), lambda i: (i, 0)),
                  pl.BlockSpec((nb, d, cb), lambda i: (0, 0, 0), pipeline_mode=pl.Buffered(1))],
        out_specs=[row, vec],
        out_shape=[jax.ShapeDtypeStruct((s, d), F32), jax.ShapeDtypeStruct((1, d), F32)],
        compiler_params=_params(("arbitrary",)),
    )(x, nw, dres, dpb, win)


def _mix_post(x, yr, ya, nr, na, wout):
    s, d = x.shape
    h = yr.shape[1]
    tm = _tile(s)

    def body(x_ref, yr_ref, ya_ref, nr_ref, na_ref, w_ref, out_ref):
        yrv = yr_ref[...]
        yav = ya_ref[...]
        onb = (yrv * _rms_r(yrv) * nr_ref[...]).astype(BF16)
        oab = (yav * _rms_r(yav) * na_ref[...]).astype(BF16)
        out_ref[...] = x_ref[...] + _dot(onb, w_ref[0:h, :]) + _dot(oab, w_ref[h:2 * h, :])

    row = pl.BlockSpec((tm, d), lambda i: (i, 0))
    half = pl.BlockSpec((tm, h), lambda i: (i, 0))
    vec = pl.BlockSpec((1, h), lambda i: (0, 0))
    return pl.pallas_call(
        body, name="mix_post", grid=(s // tm,),
        in_specs=[row, half, half, vec, vec, pl.BlockSpec((2 * h, d), lambda i: (0, 0))],
        out_specs=row, out_shape=jax.ShapeDtypeStruct((s, d), F32),
        compiler_params=_params(("arbitrary",)),
    )(x, yr, ya, nr, na, wout)


def _mix_post_bwd(dx, yr, ya, nr, na, wout):
    s, d = dx.shape
    h = yr.shape[1]
    tm = _tile(s)

    def body(dx_ref, yr_ref, ya_ref, nr_ref, na_ref, w_ref,
             dyr_ref, dya_ref, yc_ref, dxb_ref, dnr_ref, dna_ref):
        i = pl.program_id(0)
        dxb = dx_ref[...].astype(BF16)
        dxb_ref[...] = dxb
        dyc = _dot_nt(dxb, w_ref[...])
        yrv = yr_ref[...]
        yav = ya_ref[...]
        rr = _rms_r(yrv)
        ra = _rms_r(yav)
        yc_ref[:, 0:h] = (yrv * rr * nr_ref[...]).astype(BF16)
        yc_ref[:, h:2 * h] = (yav * ra * na_ref[...]).astype(BF16)
        dyr, dnr = _rms_bwd(yrv, rr, nr_ref[...], dyc[:, 0:h])
        dya, dna = _rms_bwd(yav, ra, na_ref[...], dyc[:, h:2 * h])
        dyr_ref[...] = dyr
        dya_ref[...] = dya

        @pl.when(i == 0)
        def _():
            dnr_ref[...] = jnp.zeros_like(dnr_ref)
            dna_ref[...] = jnp.zeros_like(dna_ref)

        dnr_ref[...] += dnr
        dna_ref[...] += dna

    row = pl.BlockSpec((tm, d), lambda i: (i, 0))
    half = pl.BlockSpec((tm, h), lambda i: (i, 0))
    vec = pl.BlockSpec((1, h), lambda i: (0, 0))
    return pl.pallas_call(
        body, name="mix_post_bwd", grid=(s // tm,),
        in_specs=[row, half, half, vec, vec, pl.BlockSpec((2 * h, d), lambda i: (0, 0))],
        out_specs=[half, half, pl.BlockSpec((tm, 2 * h), lambda i: (i, 0)), row, vec, vec],
        out_shape=[jax.ShapeDtypeStruct((s, h), F32), jax.ShapeDtypeStruct((s, h), F32),
                   jax.ShapeDtypeStruct((s, 2 * h), BF16), jax.ShapeDtypeStruct((s, d), BF16),
                   jax.ShapeDtypeStruct((1, h), F32), jax.ShapeDtypeStruct((1, h), F32)],
        compiler_params=_params(("arbitrary",)),
    )(dx, yr, ya, nr, na, wout)


def _shift_down(xv, s, prev8):
    rolled = pltpu.roll(xv, s, 0)
    row8 = lax.broadcasted_iota(jnp.int32, prev8.shape, 0)
    head = jnp.where(row8 < s, pltpu.roll(prev8, s, 0), rolled[0:8, :])
    return jnp.concatenate([head, rolled[8:, :]], axis=0)


def _shift_up(xv, s, next8):
    n = xv.shape[0]
    rolled = pltpu.roll(xv, n - s, 0)
    row8 = lax.broadcasted_iota(jnp.int32, next8.shape, 0)
    tail = jnp.where(row8 >= 8 - s, pltpu.roll(next8, 8 - s, 0), rolled[n - 8:, :])
    return jnp.concatenate([rolled[:n - 8, :], tail], axis=0)


def _scan_fwd(a, b):
    n = a.shape[0]
    sub = lax.broadcasted_iota(jnp.int32, a.shape, 0) % SUBLANES
    s = 1
    while s < SUBLANES:
        ok = sub >= s
        b = jnp.where(ok, a * pltpu.roll(b, s, 0) + b, b)
        a = jnp.where(ok, a * pltpu.roll(a, s, 0), a)
        s *= 2
    groups = []
    before = jnp.zeros((1, a.shape[1]), F32)
    for g in range(n // SUBLANES):
        rows = slice(g * SUBLANES, (g + 1) * SUBLANES)
        groups.append(a[rows] * before + b[rows])
        before = groups[-1][SUBLANES - 1:]
    return jnp.concatenate(groups, axis=0)


def _scan_bwd(a, b):
    n = a.shape[0]
    sub = lax.broadcasted_iota(jnp.int32, a.shape, 0) % SUBLANES
    s = 1
    while s < SUBLANES:
        ok = sub < SUBLANES - s
        b = jnp.where(ok, a * pltpu.roll(b, n - s, 0) + b, b)
        a = jnp.where(ok, a * pltpu.roll(a, n - s, 0), a)
        s *= 2
    groups = []
    after = jnp.zeros((1, a.shape[1]), F32)
    for g in reversed(range(n // SUBLANES)):
        rows = slice(g * SUBLANES, (g + 1) * SUBLANES)
        groups.append(a[rows] * after + b[rows])
        after = groups[-1][:1]
    return jnp.concatenate(groups[::-1], axis=0)


def _rglru_gates(xv, prev8, cw_ref, cb_ref, wa_ref, ba_ref, wx_ref, bx_ref, lam_ref):
    x1 = _shift_down(xv, 1, prev8)
    x2 = _shift_down(xv, 2, prev8)
    x3 = _shift_down(xv, 3, prev8)
    xc = cw_ref[3:4, :] * xv + cw_ref[2:3, :] * x1 + cw_ref[1:2, :] * x2 + cw_ref[0:1, :] * x3 + cb_ref[...]
    xcb = xc.astype(BF16)
    r = _sigmoid(_dot(xcb, wa_ref[...]) + ba_ref[...])
    ig = _sigmoid(_dot(xcb, wx_ref[...]) + bx_ref[...])
    c = RG_C * _log_sigmoid(lam_ref[...])
    la = r * c
    a = jnp.exp(la)
    m = jnp.sqrt(-_expm1_neg(2.0 * la))
    return (x1, x2, x3), xc, xcb, r, ig, c, a, m


def _rglru_fwd(proj, cw, cb, wa, ba, wx, bx, lam):
    s = proj.shape[0]
    w = D_RNN
    tm = _tile(s)

    def body(xr_ref, gate_ref, cw_ref, cb_ref, wa_ref, ba_ref, wx_ref, bx_ref, lam_ref,
             y_ref, h_ref, prev, hlast):
        @pl.when(pl.program_id(0) == 0)
        def _():
            prev[...] = jnp.zeros_like(prev)
            hlast[...] = jnp.zeros_like(hlast)

        xv = xr_ref[...]
        _, xc, _, _, ig, _, a, m = _rglru_gates(xv, prev[...], cw_ref, cb_ref, wa_ref, ba_ref,
                                                wx_ref, bx_ref, lam_ref)
        b = m * (ig * xc)
        row = lax.broadcasted_iota(jnp.int32, b.shape, 0)
        b = jnp.where(row == 0, b + a * hlast[...], b)
        h = _scan_fwd(a, b)
        h_ref[...] = h
        y_ref[...] = h * _gelu(gate_ref[...])
        prev[...] = xv[tm - 8:, :]
        hlast[...] = h[tm - 1:tm, :]

    vec = pl.BlockSpec((1, w), lambda i: (0, 0))
    sq = pl.BlockSpec((w, w), lambda i: (0, 0))
    out = pl.BlockSpec((tm, w), lambda i: (i, 0))
    return pl.pallas_call(
        body, name="rglru_fwd", grid=(s // tm,),
        in_specs=[pl.BlockSpec((tm, w), lambda i: (i, 0)), pl.BlockSpec((tm, w), lambda i: (i, 1)),
                  pl.BlockSpec((CONV_W, w), lambda i: (0, 0)), vec, sq, vec, sq, vec, vec],
        out_specs=[out, out],
        out_shape=[jax.ShapeDtypeStruct((s, w), F32), jax.ShapeDtypeStruct((s, w), F32)],
        scratch_shapes=[pltpu.VMEM((8, w), F32), pltpu.VMEM((1, w), F32)],
        compiler_params=_params(("arbitrary",)),
    )(proj, proj, cw, cb, wa, ba, wx, bx, lam)


def _rglru_bwd(proj, hseq, dyr, cw, cb, wa, ba, wx, bx, lam):
    s = proj.shape[0]
    w = D_RNN
    tm = _tile(s)
    nt = s // tm
    t8 = tm // 8

    def body(xr_ref, xp_ref, gate_ref, h_ref, hp_ref, dy_ref, cw_ref, cb_ref, wa_ref, ba_ref,
             wx_ref, bx_ref, lam_ref,
             dxr_ref, dgate_ref, dcw_ref, dcb_ref, dwa_ref, dba_ref, dwx_ref, dbx_ref, dlam_ref,
             carry, dxc_next):
        i = pl.program_id(0)
        first_tile = i == nt - 1

        @pl.when(i == 0)
        def _():
            carry[...] = jnp.zeros_like(carry)
            dxc_next[...] = jnp.zeros_like(dxc_next)
            for ref in (dcw_ref, dcb_ref, dwa_ref, dba_ref, dwx_ref, dbx_ref, dlam_ref):
                ref[...] = jnp.zeros_like(ref)

        xv = xr_ref[...]
        prev8 = jnp.where(first_tile, 0.0, xp_ref[...])
        hprev8 = jnp.where(first_tile, 0.0, hp_ref[...])
        (x1, x2, x3), xc, xcb, r, ig, c, a, m = _rglru_gates(
            xv, prev8, cw_ref, cb_ref, wa_ref, ba_ref, wx_ref, bx_ref, lam_ref)
        gv = gate_ref[...]
        hv = h_ref[...]
        dy = dy_ref[...]
        dgate_ref[...] = (dy * hv * _gelu_grad(gv)).astype(BF16)
        dh = dy * _gelu(gv)
        row = lax.broadcasted_iota(jnp.int32, dh.shape, 0)
        dh = jnp.where(row == tm - 1, dh + carry[...], dh)
        a_up = jnp.where(row == tm - 1, 0.0, pltpu.roll(a, tm - 1, 0))
        lam_t = _scan_bwd(a_up, dh)
        carry[...] = a[0:1, :] * lam_t[0:1, :]
        hm1 = _shift_down(hv, 1, hprev8)
        da = lam_t * hm1
        ixc = ig * xc
        dm = lam_t * ixc
        dig = lam_t * m * xc
        dxc = lam_t * m * ig
        dla = da * a - dm * (a * a) / m
        dr = dla * c
        dlam_ref[...] += jnp.sum(dla * r, axis=0, keepdims=True)
        dpa = dr * r * (1.0 - r)
        dpi = dig * ig * (1.0 - ig)
        dba_ref[...] += jnp.sum(dpa, axis=0, keepdims=True)
        dbx_ref[...] += jnp.sum(dpi, axis=0, keepdims=True)
        dpab = dpa.astype(BF16)
        dpib = dpi.astype(BF16)
        dwa_ref[...] += _dot_tn(xcb, dpab)
        dwx_ref[...] += _dot_tn(xcb, dpib)
        dxc = dxc + _dot_nt(dpab, wa_ref[...]) + _dot_nt(dpib, wx_ref[...])
        dcb_ref[...] += jnp.sum(dxc, axis=0, keepdims=True)
        dcw_ref[3:4, :] += jnp.sum(dxc * xv, axis=0, keepdims=True)
        dcw_ref[2:3, :] += jnp.sum(dxc * x1, axis=0, keepdims=True)
        dcw_ref[1:2, :] += jnp.sum(dxc * x2, axis=0, keepdims=True)
        dcw_ref[0:1, :] += jnp.sum(dxc * x3, axis=0, keepdims=True)
        nxt = dxc_next[...]
        dxr = (cw_ref[3:4, :] * dxc + cw_ref[2:3, :] * _shift_up(dxc, 1, nxt)
               + cw_ref[1:2, :] * _shift_up(dxc, 2, nxt) + cw_ref[0:1, :] * _shift_up(dxc, 3, nxt))
        dxr_ref[...] = dxr.astype(BF16)
        dxc_next[...] = dxc[0:8, :]

        @pl.when(first_tile)
        def _():
            lv = lam_ref[...]
            dlam_ref[...] = dlam_ref[...] * (RG_C * _sigmoid(-lv))

    rev = lambda i: nt - 1 - i
    vec = pl.BlockSpec((1, w), lambda i: (0, 0))
    sq = pl.BlockSpec((w, w), lambda i: (0, 0))
    cur = lambda col: pl.BlockSpec((tm, w), lambda i: (rev(i), col))
    before = lambda cols: pl.BlockSpec((8, w), lambda i: (jnp.maximum(rev(i) * t8 - 1, 0), 0))
    return pl.pallas_call(
        body, name="rglru_bwd", grid=(nt,),
        in_specs=[cur(0), before(None), cur(1), cur(0), before(None), cur(0),
                  pl.BlockSpec((CONV_W, w), lambda i: (0, 0)), vec, sq, vec, sq, vec, vec],
        out_specs=[cur(0), cur(0), pl.BlockSpec((CONV_W, w), lambda i: (0, 0)), vec, sq, vec, sq, vec, vec],
        out_shape=[jax.ShapeDtypeStruct((s, w), BF16), jax.ShapeDtypeStruct((s, w), BF16),
                   jax.ShapeDtypeStruct((CONV_W, w), F32), jax.ShapeDtypeStruct((1, w), F32),
                   jax.ShapeDtypeStruct((w, w), F32), jax.ShapeDtypeStruct((1, w), F32),
                   jax.ShapeDtypeStruct((w, w), F32), jax.ShapeDtypeStruct((1, w), F32),
                   jax.ShapeDtypeStruct((1, w), F32)],
        scratch_shapes=[pltpu.VMEM((1, w), F32), pltpu.VMEM((8, w), F32)],
        compiler_params=_params(("arbitrary",)),
    )(proj, proj, proj, hseq, hseq, dyr, cw, cb, wa, ba, wx, bx, lam)


def _sb_logs(z, valid):
    l1p = jnp.log(1.0 + jnp.exp(-jnp.abs(z)))
    lb = jnp.minimum(z, 0.0) - l1p
    lm = jnp.where(valid, -jnp.maximum(z, 0.0) - l1p, 0.0)
    return lb, lm


class _Window:
    def __init__(self):
        blk, win, cut = ATT_BLOCK, ATT_WINDOW, ATT_SPLIT
        self.row = lax.broadcasted_iota(jnp.int32, (blk, win), 0)
        self.col = lax.broadcasted_iota(jnp.int32, (blk, win), 1)

        def tri(n, later):
            j = lax.broadcasted_iota(jnp.int32, (n, n), 0)
            s = lax.broadcasted_iota(jnp.int32, (n, n), 1)
            return jnp.where((j > s) if later else (j < s), 1.0, 0.0).astype(BF16)

        self.later = (tri(cut, True), tri(win - cut, True))
        self.earlier = (tri(cut, False), tri(win - cut, False))

    def place(self, qi, g):
        end = (qi + 1) * ATT_BLOCK - g * ATT_WINDOW
        start = pl.multiple_of(jnp.maximum(end - ATT_WINDOW, 0), ATT_BLOCK)
        valid = start + self.col < jnp.minimum(qi * ATT_BLOCK + self.row, end)
        return start, valid

    @staticmethod
    def _parts(xv):
        hi = xv.astype(BF16)
        lo = (xv - hi.astype(F32)).astype(BF16)
        cut = ATT_SPLIT
        sums = (jnp.sum(xv[:, :cut], axis=1, keepdims=True), jnp.sum(xv[:, cut:], axis=1, keepdims=True))
        return (hi[:, :cut], lo[:, :cut]), (hi[:, cut:], lo[:, cut:]), sums

    def sums_after(self, xv, carry):
        (h0, l0), (h1, l1), (s0, s1) = self._parts(xv)
        first = _dot(h0, self.later[0]) + _dot(l0, self.later[0]) + (s1 + carry)
        last = _dot(h1, self.later[1]) + _dot(l1, self.later[1]) + carry
        return jnp.concatenate([first, last], axis=1), s0 + s1

    def sums_before(self, xv, carry):
        (h0, l0), (h1, l1), (s0, s1) = self._parts(xv)
        first = _dot(h0, self.earlier[0]) + _dot(l0, self.earlier[0]) + carry
        last = _dot(h1, self.earlier[1]) + _dot(l1, self.earlier[1]) + (s0 + carry)
        return jnp.concatenate([first, last], axis=1), s0 + s1


class _HeadPair:
    def __init__(self):
        lanes = 2 * HEAD_DIM
        lane = lax.broadcasted_iota(jnp.int32, (1, lanes), 1)
        self.masks = [lane // HEAD_DIM == h for h in (0, 1)]
        i = lax.broadcasted_iota(jnp.int32, (lanes, lanes), 0) // HEAD_DIM
        j = lax.broadcasted_iota(jnp.int32, (lanes, lanes), 1) // HEAD_DIM
        self.same_head = jnp.where(i == j, 1.0, 0.0).astype(BF16)

    def only(self, h, xv):
        return jnp.where(self.masks[h], xv, jnp.zeros_like(xv))

    def merge(self, per_head):
        return jnp.where(self.masks[0], per_head[0], per_head[1])

    def mean(self, xv):
        hi = xv.astype(BF16)
        lo = (xv - hi.astype(F32)).astype(BF16)
        return (_dot(hi, self.same_head) + _dot(lo, self.same_head)) * (1.0 / HEAD_DIM)

    def rms_r(self, xv):
        return lax.rsqrt(self.mean(xv * xv) + EPS)

    def rms_bwd(self, xv, r, nw, dh):
        t = dh * nw
        dx = r * t - xv * (r * r * r * self.mean(t * xv))
        dn = jnp.sum(dh * xv * r, axis=0, keepdims=True)
        return dx, dn[:, :HEAD_DIM] + dn[:, HEAD_DIM:]


def _attn_fwd(proj, qg, kg, rider=None):
    s = proj.shape[0]
    blk, win, dh = ATT_BLOCK, ATT_WINDOW, HEAD_DIM
    nq = s // blk
    scale = 1.0 / math.sqrt(dh)
    heads = (0, 1)
    assert s >= win and s % blk == 0

    def body(*refs):
        (q_ref, k_ref, v_ref, qg_ref, kg_ref), (o_ref,), (qn, kn, vb), copies = _split_refs(refs, 5, 1, rider)
        finish = _ride(copies, pl.program_id(0) == 0, pl.program_id(0) == N_HEADS // 2 - 1)
        wd, hp = _Window(), _HeadPair()
        qv = q_ref[...]
        qn[...] = (qv * hp.rms_r(qv) * qg_ref[...] * scale).astype(BF16)
        kv = k_ref[...]
        kn[...] = (kv * hp.rms_r(kv) * kg_ref[...]).astype(BF16)
        vb[...] = v_ref[...].astype(BF16)

        def q_step(qi, _):
            qoff = pl.multiple_of(qi * blk, blk)
            qt = qn[pl.ds(qoff, blk), :]
            qts = [hp.only(h, qt) for h in heads]

            def more(carry):
                g, live = carry[:2]
                return jnp.logical_and((qi + 1) * blk - g * win > 0, live > 0)

            def window(carry):
                g, _, accs, runs = carry
                start, valid = wd.place(qi, g)
                kt = kn[pl.ds(start, win), :]
                zs = [_dot_nt(qts[h], kt) for h in heads]
                logs = [_sb_logs(z, valid) for z in zs]
                sums = [wd.sums_after(logs[h][1], runs[h]) for h in heads]
                wgts = [jnp.where(valid, jnp.exp(logs[h][0] + sums[h][0]), 0.0).astype(BF16) for h in heads]
                vt = vb[pl.ds(start, win), :]
                accs = tuple(accs[h] + _dot(wgts[h], vt) for h in heads)
                runs = tuple(runs[h] + sums[h][1] for h in heads)
                live = (jnp.maximum(jnp.max(runs[0]), jnp.max(runs[1])) > EXP_ZERO).astype(jnp.int32)
                return g + 1, live, accs, runs

            zero = lambda cols: tuple(jnp.zeros((blk, cols), F32) for _ in heads)
            _, _, accs, _ = lax.while_loop(more, window, (jnp.int32(0), jnp.int32(1), zero(2 * dh), zero(1)))
            o_ref[pl.ds(qoff, blk), :] = hp.merge(accs)
            return 0

        lax.fori_loop(0, nq, q_step, 0)
        finish()

    pair = lambda group: pl.BlockSpec((s, 2 * dh), lambda p: (0, group * (D_ATT // (2 * dh)) + p))
    vec = pl.BlockSpec((1, 2 * dh), lambda p: (0, 0))
    return _call(
        body, "attn_fwd", (N_HEADS // 2,), [pair(2), pair(3), pair(4), vec, vec], [pair(0)],
        [jax.ShapeDtypeStruct((s, D_ATT), F32)], [proj, proj, proj, jnp.tile(qg, (1, 2)), jnp.tile(kg, (1, 2))],
        scratch=[pltpu.VMEM((s, 2 * dh), BF16)] * 3, rider=rider)


def _attn_bwd(proj, dya, qg, kg, rider=None):
    s = proj.shape[0]
    blk, win, dh = ATT_BLOCK, ATT_WINDOW, HEAD_DIM
    nq = s // blk
    max_windows = -(-s // win) + 1
    scale = 1.0 / math.sqrt(dh)
    steps = N_HEADS // 2
    heads = (0, 1)
    assert s >= win and s % blk == 0

    def body(*refs):
        ins, outs, scratch, copies = _split_refs(refs, 6, 5, rider)
        q_ref, k_ref, v_ref, do_ref, qg_ref, kg_ref = ins
        dq_ref, dk_ref, dv_ref, dqg_ref, dkg_ref = outs
        qn, kn, vb, dob, runs_ref, dqn, dkn, dvn = scratch
        finish = _ride(copies, pl.program_id(0) == 0, pl.program_id(0) == steps - 1)
        wd, hp = _Window(), _HeadPair()

        @pl.when(pl.program_id(0) == 0)
        def _():
            dqg_ref[...] = jnp.zeros_like(dqg_ref)
            dkg_ref[...] = jnp.zeros_like(dkg_ref)

        qv = q_ref[...]
        qn[...] = (qv * hp.rms_r(qv) * qg_ref[...] * scale).astype(BF16)
        kv = k_ref[...]
        kn[...] = (kv * hp.rms_r(kv) * kg_ref[...]).astype(BF16)
        vb[...] = v_ref[...].astype(BF16)
        dob[...] = do_ref[...].astype(BF16)
        dkn[...] = jnp.zeros_like(dkn)
        dvn[...] = jnp.zeros_like(dvn)

        def q_step(qi, _):
            qoff = pl.multiple_of(qi * blk, blk)
            qt = qn[pl.ds(qoff, blk), :]
            dot = dob[pl.ds(qoff, blk), :]
            qts = [hp.only(h, qt) for h in heads]
            dots = [hp.only(h, dot) for h in heads]

            zero = lambda cols: tuple(jnp.zeros((blk, cols), F32) for _ in heads)

            def logs_of(g):
                start, valid = wd.place(qi, g)
                kt = kn[pl.ds(start, win), :]
                return [_sb_logs(_dot_nt(qts[h], kt), valid) for h in heads]

            def row_sums(logs):
                return tuple(jnp.sum(logs[h][1], axis=1, keepdims=True) for h in heads)

            def still_live(runs):
                return jnp.maximum(jnp.max(runs[0]), jnp.max(runs[1])) > EXP_ZERO

            def window_grads(g, logs, runs, esums):
                start, valid = wd.place(qi, g)
                kt = kn[pl.ds(start, win), :]
                vt = vb[pl.ds(start, win), :]
                dws = [_dot_nt(dots[h], vt) for h in heads]
                tails = [wd.sums_after(logs[h][1], runs[h])[0] for h in heads]
                wgts = [jnp.where(valid, jnp.exp(logs[h][0] + tails[h]), 0.0) for h in heads]
                es = [dws[h] * wgts[h] for h in heads]
                befores = [wd.sums_before(es[h], esums[h]) for h in heads]
                dzbs = []
                for h in heads:
                    beta = jnp.exp(logs[h][0])
                    dz = jnp.where(valid, es[h] * (1.0 - beta) - befores[h][0] * beta, 0.0)
                    dzbs.append(dz.astype(BF16))
                dkn[pl.ds(start, win), :] += _dot_tn(dzbs[0], qts[0]) + _dot_tn(dzbs[1], qts[1])
                dvn[pl.ds(start, win), :] += (_dot_tn(wgts[0].astype(BF16), dots[0])
                                              + _dot_tn(wgts[1].astype(BF16), dots[1]))
                return tuple(_dot(dzbs[h], kt) for h in heads), tuple(befores[h][1] for h in heads)

            logs0 = logs_of(0)
            runs1 = row_sums(logs0)

            def one_window():
                return window_grads(0, logs0, zero(1), zero(1))[0]

            def all_windows():
                def more(carry):
                    g, live = carry[:2]
                    return jnp.logical_and((qi + 1) * blk - g * win > 0, live > 0)

                def run_window(carry):
                    g, _, runs = carry
                    for h in heads:
                        runs_ref[h, g] = runs[h]
                    sums = row_sums(logs_of(g))
                    runs = tuple(runs[h] + sums[h] for h in heads)
                    return g + 1, still_live(runs).astype(jnp.int32), runs

                for h in heads:
                    runs_ref[h, 0] = jnp.zeros((blk, 1), F32)
                windows, _, _ = lax.while_loop(more, run_window, (jnp.int32(1), jnp.int32(1), runs1))

                def k_window(gg, carry):
                    dq_accs, esums = carry
                    g = windows - 1 - gg
                    parts, totals = window_grads(g, logs_of(g), [runs_ref[h, g] for h in heads], esums)
                    return (tuple(dq_accs[h] + parts[h] for h in heads),
                            tuple(esums[h] + totals[h] for h in heads))

                return lax.fori_loop(0, windows, k_window, (zero(2 * dh), zero(1)))[0]

            earlier_keys = (qi + 1) * blk - win > 0
            dq_accs = lax.cond(jnp.logical_and(earlier_keys, still_live(runs1)), all_windows, one_window)
            dqn[pl.ds(qoff, blk), :] = hp.merge(dq_accs)
            return 0

        lax.fori_loop(0, nq, q_step, 0)

        dq, dqg = hp.rms_bwd(qv, hp.rms_r(qv), qg_ref[...] * scale, dqn[...])
        dq_ref[...] = dq.astype(BF16)
        dqg_ref[...] += dqg * scale
        dk, dkg = hp.rms_bwd(kv, hp.rms_r(kv), kg_ref[...], dkn[...])
        dk_ref[...] = dk.astype(BF16)
        dkg_ref[...] += dkg
        dv_ref[...] = dvn[...].astype(BF16)
        finish()

    pair = lambda group: pl.BlockSpec((s, 2 * dh), lambda p: (0, group * (D_ATT // (2 * dh)) + p))
    vec2 = pl.BlockSpec((1, 2 * dh), lambda p: (0, 0))
    vec = pl.BlockSpec((1, dh), lambda p: (0, 0))
    return _call(
        body, "attn_bwd", (steps,), [pair(2), pair(3), pair(4), pair(0), vec2, vec2],
        [pair(0), pair(0), pair(0), vec, vec],
        [jax.ShapeDtypeStruct((s, D_ATT), BF16)] * 3 + [jax.ShapeDtypeStruct((1, dh), F32)] * 2,
        [proj, proj, proj, dya, jnp.tile(qg, (1, 2)), jnp.tile(kg, (1, 2))],
        scratch=[pltpu.VMEM((s, 2 * dh), BF16)] * 4 + [pltpu.VMEM((2, max_windows, blk, 1), F32)]
        + [pltpu.VMEM((s, 2 * dh), F32)] * 3, rider=rider)


def _block_diag(w):
    n, c, d = w.shape
    return jnp.einsum("ncd,nm->ncmd", w, jnp.eye(n, dtype=w.dtype)).reshape(n * c, n * d)


def _diag_blocks(full, n):
    c = full.shape[0] // n
    return jnp.stack([full[i * c:(i + 1) * c, i * c:(i + 1) * c] for i in range(n)])


FFN1 = ["ffn1_w_gate", "ffn1_w_up", "ffn1_w_down"]
FFN2 = ["ffn2_w_gate", "ffn2_w_up", "ffn2_w_down"]
MIXER = ["w_in", "w_out"]


def _pair_sums(gb, names, where):
    theirs = _pair_exchange([gb[n] for n in names], "pair_exchange_" + names[0])
    pair, own = _pair_sum([gb[n] for n in names], theirs, where, "pair_sum_" + names[0])
    return _chip_rider(pair, own)


def _local_step(x, tgt, stacks, conv_stack, small, where):
    gate_up, down = FFN1[:2], FFN1[2:]
    big = dict(zip(gate_up, _gather_weights([stacks[n] for n in gate_up], [])))
    wa = _block_diag(small["rg_w_a"]).astype(BF16)
    wx = _block_diag(small["rg_w_x"]).astype(BF16)

    whole = lambda names: [big[n].reshape(-1, D_MODEL) for n in names]
    g1, u1, hb1, ab1, *landed = _ffn_up(x, small["ffn1_norm"], *whole(gate_up),
                                        rider=_gather_rider([stacks[n] for n in down], []))
    big.update(zip(down, landed))
    x1, *landed = _ffn_down(x, ab1, *whole(down), rider=_gather_rider([stacks[n] for n in MIXER], [conv_stack]))
    big.update(zip(MIXER, landed))
    conv_w = jnp.transpose(landed[-1], (1, 0, 2)).reshape(CONV_W, D_RNN)
    wout = big["w_out"].reshape(D_MODEL, D_MODEL)
    rg = (conv_w, small["conv_b"], wa, small["rg_b_a"], wx, small["rg_b_x"], small["rg_lambda"])
    proj, hb2 = _mix_pre(x1, small["mix_norm"], big["w_in"])
    yr, hseq = _rglru_fwd(proj, *rg)
    ya, *landed = _attn_fwd(proj, small["q_norm"], small["k_norm"], _gather_rider([stacks[n] for n in FFN2], []))
    big.update(zip(FFN2, landed))
    x2 = _mix_post(x1, yr, ya, small["rnn_out_norm"], small["attn_out_norm"], wout)
    dx3, g2, u2, hb3, ab3, loss = _ffn_fwd(x2, small["ffn2_norm"], *whole(FFN2), tgt)

    gb, gs, slots = {}, {}, {}
    dx2, dg2, du2, dyb2, gs["ffn2_norm"] = _ffn_bwd_act(x2, small["ffn2_norm"], dx3, g2, u2, *whole(FFN2), "ffn2_bwd")
    gb["ffn2_w_gate"] = _ffn_wgrad(dg2, hb3, 1.0, "wgrad_gate_ffn2")
    gb["ffn2_w_up"] = _ffn_wgrad(du2, hb3, 1.0, "wgrad_up_ffn2")
    gb["ffn2_w_down"] = _ffn_wgrad(ab3, dyb2, 0.5, "wgrad_down_ffn2")
    dyr, dya, ycat, dxb2, gs["rnn_out_norm"], gs["attn_out_norm"] = _mix_post_bwd(
        dx2, yr, ya, small["rnn_out_norm"], small["attn_out_norm"], wout)
    gb["w_out"] = _wgrad_whole(ycat, dxb2, False, "wgrad_out")
    early = FFN2 + ["w_out"]
    dq, dk, dv, gs["q_norm"], gs["k_norm"], *done = _attn_bwd(
        proj, dya, small["q_norm"], small["k_norm"], _pair_sums(gb, early, where))
    slots.update(zip(early, done))
    dxr, dgate, gs["conv_w"], gs["conv_b"], dwa, gs["rg_b_a"], dwx, gs["rg_b_x"], gs["rg_lambda"] = _rglru_bwd(
        proj, hseq, dyr, *rg)
    gs["rg_w_a"] = _diag_blocks(dwa, RNN_BLOCKS)
    gs["rg_w_x"] = _diag_blocks(dwx, RNN_BLOCKS)
    dpb = jnp.concatenate([dxr, dgate, dq, dk, dv], axis=1)
    dx1, gs["mix_norm"] = _mix_pre_bwd(x1, small["mix_norm"], dx2, dpb, big["w_in"])
    dx0, dg1, du1, dyb1, gs["ffn1_norm"] = _ffn_bwd_act(x, small["ffn1_norm"], dx1, g1, u1, *whole(FFN1), "ffn1_bwd")

    mine = _place_shard(_pack([gs[n] for n in SMALL] + [loss[:, :1]]), where, F32, "place_small_grads",
                        by_device=True)
    gb["ffn1_w_gate"], everyone = _ffn_wgrad(dg1, hb1, 1.0, "wgrad_gate_ffn1", _small_rider(mine))
    gb["ffn1_w_up"], slots["ffn1_w_gate"] = _ffn_wgrad(
        du1, hb1, 1.0, "wgrad_up_ffn1", _pair_sums(gb, ["ffn1_w_gate"], where))
    gb["ffn1_w_down"], slots["ffn1_w_up"] = _ffn_wgrad(
        ab1, dyb1, 0.5, "wgrad_down_ffn1", _pair_sums(gb, ["ffn1_w_up"], where))
    gb["w_in"], slots["ffn1_w_down"] = _wgrad_whole(
        hb2, dpb, True, "wgrad_in", _pair_sums(gb, ["ffn1_w_down"], where))
    last = _pair_sums(gb, ["w_in"], where)
    slots["w_in"], = _chip_exchange(last.plain, last.inplace)
    return dx0, slots, gs, everyone


ANY = pl.BlockSpec(memory_space=pl.ANY)


def _place():
    x, y, c = lax.axis_index("x"), lax.axis_index("y"), lax.axis_index("c")
    other_chips = [(1 - x, y), (x, 1 - y), (1 - x, 1 - y)]
    return x, y, c, 2 * x + y, other_chips


def _remote(src, dst, send_sem, recv_sem, to):
    return pltpu.make_async_remote_copy(src_ref=src, dst_ref=dst, send_sem=send_sem, recv_sem=recv_sem,
                                        device_id=to, device_id_type=MESH)


def _copy_plan(pairs):
    sends = [functools.partial(_remote, *a) for a, _ in pairs]
    arrivals = [functools.partial(_remote, *b) for _, b in pairs]
    return sends, arrivals


class _Rider:
    def __init__(self, plan, plain, inplace, n_copies=None, relay=None, n_relay=0):
        self.plan, self.plain, self.inplace = plan, list(plain), list(inplace)
        self.n_copies = n_copies or 3 * len(self.inplace)
        self.relay, self.n_relay = relay, n_relay

    def operands(self):
        return self.plain + self.inplace

    def out_shape(self):
        return [jax.ShapeDtypeStruct(a.shape, a.dtype) for a in self.inplace]

    def aliases(self, inputs_before, outputs_before):
        return {inputs_before + len(self.plain) + k: outputs_before + k for k in range(len(self.inplace))}

    def scratch(self):
        relay = [pltpu.SemaphoreType.DMA((self.n_relay,))] * 2 if self.relay else []
        return [pltpu.SemaphoreType.DMA((self.n_copies,))] * 2 + relay


def _split_refs(refs, n_in, n_out, rider):
    if rider is None:
        return refs[:n_in], refs[n_in:n_in + n_out], refs[n_in + n_out:], None
    r_in, r_out = len(rider.operands()), len(rider.inplace)
    outs_at = n_in + r_in
    n_sems = len(rider.scratch())
    rest = refs[outs_at + n_out + r_out:]
    sems = rest[len(rest) - n_sems:]
    filled = refs[outs_at + n_out:outs_at + n_out + r_out]
    copies = functools.partial(rider.plan, refs[n_in:n_in + len(rider.plain)], filled, *sems[:2])
    relay = functools.partial(rider.relay, filled, *sems[2:]) if rider.relay else None
    return refs[:n_in], refs[outs_at:outs_at + n_out], rest[:len(rest) - n_sems], (copies, relay)


def _ride(copies, first, last, middle=None):
    if copies is None:
        return lambda: None
    copies, relay = copies

    @pl.when(first)
    def _():
        _start(copies()[0])

    def start_relay():
        for make in copies()[1]:
            make().wait_recv()
        _start(relay()[0])

    if relay is not None and middle is not None:
        pl.when(middle)(start_relay)

    def finish():
        @pl.when(last)
        def _():
            if relay is None:
                _finish(*copies())
            else:
                if middle is None:
                    start_relay()
                _finish(copies()[0] + relay()[0], relay()[1])

    return finish


def _gather_rider(split, whole):
    n_split = len(split)
    return _Rider(lambda plain, stacks, ss, rs: _gather_ici(stacks, n_split, ss, rs), [], list(split) + list(whole),
                  relay=lambda stacks, ss, rs: _gather_d2d(stacks[:n_split], ss, rs), n_relay=3 * n_split)


def _chip_rider(sums, slots):
    return _Rider(_chip_copies, sums, slots)


def _start(makers):
    for make in makers:
        make().start()


def _finish(sends, arrivals):
    for make in arrivals:
        make().wait_recv()
    for make in sends:
        make().wait_send()


def _half(rows, c):
    return pl.ds(pl.multiple_of(c * rows, 16), rows)


def _gather_weights(split, whole):
    arrs = list(split) + list(whole)
    n, ns = len(arrs), len(split)

    def body(*refs):
        outs = refs[n:2 * n]
        send_sems, recv_sems, fsend_sems, frecv_sems = refs[2 * n:]
        sends, arrivals = _gather_ici(outs, ns, send_sems, recv_sems)
        passes, passed = _gather_d2d(outs[:ns], fsend_sems, frecv_sems)
        _start(sends)
        for k, make in enumerate(arrivals):
            make().wait_recv()
            if k < 3 * ns:
                passes[k]().start()
        _finish(sends + passes, passed)

    return pl.pallas_call(
        body, name="gather_weights",
        in_specs=[ANY] * n, out_specs=[ANY] * n,
        out_shape=[jax.ShapeDtypeStruct(a.shape, a.dtype) for a in arrs],
        input_output_aliases={i: i for i in range(n)},
        scratch_shapes=[pltpu.SemaphoreType.DMA((3 * n,)), pltpu.SemaphoreType.DMA((3 * n,)),
                        pltpu.SemaphoreType.DMA((3 * ns,)), pltpu.SemaphoreType.DMA((3 * ns,))],
    )(*arrs)


def _gather_ici(stacks, n_split, send_sems, recv_sems):
    x, y, c, me, chips = _place()

    def region(i, chip):
        if i < n_split:
            return stacks[i].at[chip, _half(stacks[i].shape[1] // 2, c)]
        return stacks[i].at[chip]

    pairs = []
    for i in range(len(stacks)):
        for p, (cx, cy) in enumerate(chips):
            k = 3 * i + p
            mine, got = region(i, me), region(i, 2 * cx + cy)
            sems, to = (send_sems.at[k], recv_sems.at[k]), (cx, cy, c)
            pairs.append(((mine, mine, *sems, to), (got, got, *sems, to)))
    return _copy_plan(pairs)


def _gather_d2d(stacks, send_sems, recv_sems):
    x, y, c, _, chips = _place()
    sibling = (x, y, 1 - c)
    pairs = []
    for i, stack in enumerate(stacks):
        rows = stack.shape[1] // 2
        for p, (cx, cy) in enumerate(chips):
            k = 3 * i + p
            got, theirs = stack.at[2 * cx + cy, _half(rows, c)], stack.at[2 * cx + cy, _half(rows, 1 - c)]
            sems = (send_sems.at[k], recv_sems.at[k])
            pairs.append(((got, got, *sems, sibling), (theirs, theirs, *sems, sibling)))
    return _copy_plan(pairs)


def _pair_exchange(grads, name):
    n = len(grads)

    def body(*refs):
        ins, theirs = refs[:n], refs[n:2 * n]
        send_sems, recv_sems = refs[2 * n:]
        x, y, c, _, _ = _place()
        sibling = (x, y, 1 - c)
        sends = [_remote(ins[k].at[:, _half(grads[k].shape[1] // 2, 1 - c)], theirs[k],
                         send_sems.at[k], recv_sems.at[k], sibling) for k in range(n)]
        for cp in sends:
            cp.start()
        for k in range(n):
            _remote(theirs[k], theirs[k], send_sems.at[k], recv_sems.at[k], sibling).wait_recv()
        for cp in sends:
            cp.wait_send()

    return pl.pallas_call(
        body, name=name,
        in_specs=[ANY] * n, out_specs=[ANY] * n,
        out_shape=[jax.ShapeDtypeStruct((g.shape[0], g.shape[1] // 2, g.shape[2]), g.dtype) for g in grads],
        scratch_shapes=[pltpu.SemaphoreType.DMA((n,))] * 2,
    )(*grads)


def _chip_exchange(sums, slots):
    n = len(sums)

    def body(*refs):
        sends, arrivals = _chip_copies(refs[:n], refs[2 * n:3 * n], *refs[3 * n:])
        _start(sends)
        _finish(sends, arrivals)

    return pl.pallas_call(
        body, name="grad_chip_exchange",
        in_specs=[ANY] * (2 * n), out_specs=[ANY] * n,
        out_shape=[jax.ShapeDtypeStruct(a.shape, a.dtype) for a in slots],
        input_output_aliases={n + k: k for k in range(n)},
        scratch_shapes=[pltpu.SemaphoreType.DMA((3 * n,)), pltpu.SemaphoreType.DMA((3 * n,))],
    )(*sums, *slots)


def _chip_copies(sums, slots, send_sems, recv_sems):
    x, y, c, me, chips = _place()
    pairs = []
    for k in range(len(sums)):
        for p, (cx, cy) in enumerate(chips):
            j = 3 * k + p
            got = slots[k].at[2 * cx + cy]
            sems, to = (send_sems.at[j], recv_sems.at[j]), (cx, cy, c)
            pairs.append(((sums[k].at[2 * cx + cy], slots[k].at[me], *sems, to), (got, got, *sems, to)))
    return _copy_plan(pairs)


def _half_swap(halves):
    n = len(halves)

    def body(*refs):
        outs = refs[n:2 * n]
        send_sems, recv_sems = refs[2 * n:]
        x, y, c, _, _ = _place()
        sibling = (x, y, 1 - c)
        sends = [_remote(outs[k].at[c], outs[k].at[c], send_sems.at[k], recv_sems.at[k], sibling) for k in range(n)]
        for cp in sends:
            cp.start()
        for k in range(n):
            got = outs[k].at[1 - c]
            _remote(got, got, send_sems.at[k], recv_sems.at[k], sibling).wait_recv()
        for cp in sends:
            cp.wait_send()

    return pl.pallas_call(
        body, name="grad_half_swap",
        in_specs=[ANY] * n, out_specs=[ANY] * n,
        out_shape=[jax.ShapeDtypeStruct(a.shape, a.dtype) for a in halves],
        input_output_aliases={k: k for k in range(n)},
        scratch_shapes=[pltpu.SemaphoreType.DMA((n,))] * 2,
    )(*halves)


def _small_rider(stack):
    n_dev = 2 * N_CHIPS

    def plan(_, stacks, send_sems, recv_sems):
        x, y, c, _, _ = _place()
        mine = stacks[0].at[4 * x + 2 * y + c]
        pairs = []
        for k in range(1, n_dev):
            px, py, pc = x ^ ((k >> 2) & 1), y ^ ((k >> 1) & 1), c ^ (k & 1)
            got = stacks[0].at[4 * px + 2 * py + pc]
            sems = (send_sems.at[k - 1], recv_sems.at[k - 1])
            pairs.append(((mine, mine, *sems, (px, py, pc)), (got, got, *sems, (px, py, pc))))
        return _copy_plan(pairs)

    return _Rider(plan, [], [stack], n_dev - 1)


def _row_tile(r):
    return r // 4 if r >= 256 and (r // 4) % 16 == 0 else r


def _prefetch_call(body, name, grid, in_specs, out_specs, out_shape):
    spec = pltpu.PrefetchScalarGridSpec(num_scalar_prefetch=1, grid=grid, in_specs=in_specs, out_specs=out_specs)
    return pl.pallas_call(body, name=name, grid_spec=spec, out_shape=out_shape,
                          compiler_params=_params(("arbitrary",) * len(grid)))


def _place_shard(w2d, where, dtype, name, by_device=False):
    r, c = w2d.shape
    tr = _row_tile(r)
    slots = 2 * N_CHIPS if by_device else N_CHIPS
    slot = (lambda s: 2 * s[1] + s[0]) if by_device else (lambda s: s[1])

    def body(where_ref, w_ref, out_ref):
        out_ref[...] = w_ref[...].astype(dtype)

    return _prefetch_call(
        body, name, (r // tr,), [pl.BlockSpec((tr, c), lambda i, s: (i, 0))],
        pl.BlockSpec((None, tr, c), lambda i, s: (slot(s), i, 0)),
        jax.ShapeDtypeStruct((slots, r, c), dtype))(where, w2d)


def _place_shards(w2ds, where, name):
    n = len(w2ds)
    steps = N_CHIPS
    assert all(w.shape[0] % (16 * steps) == 0 for w in w2ds)

    def body(where_ref, *refs):
        for k in range(n):
            refs[n + k][...] = refs[k][...].astype(BF16)

    tile = lambda w: (w.shape[0] // steps, w.shape[1])
    return _prefetch_call(
        body, name, (steps,), [pl.BlockSpec(tile(w), lambda i, s: (i, 0)) for w in w2ds],
        [pl.BlockSpec((None,) + tile(w), lambda i, s: (s[1], i, 0)) for w in w2ds],
        [jax.ShapeDtypeStruct((N_CHIPS,) + w.shape, BF16) for w in w2ds])(where, *w2ds)


def _pair_sum(fulls, theirs, where, name):
    n = len(fulls)

    def body(where_ref, *refs):
        for k in range(n):
            a_ref, b_ref, out_ref, own_ref = refs[k], refs[n + k], refs[2 * n + k], refs[3 * n + k]
            total = (a_ref[...].astype(F32) + b_ref[...].astype(F32)).astype(BF16)
            out_ref[...] = total

            @pl.when(pl.program_id(0) == where_ref[1])
            def _():
                own_ref[...] = total

    half = lambda t: pl.BlockSpec((None,) + t.shape[1:], lambda j, s: (j, s[0], 0))
    blk = lambda t: pl.BlockSpec((None,) + t.shape[1:], lambda j, s: (j, 0, 0))
    own = lambda t: pl.BlockSpec((None,) + t.shape[1:], lambda j, s: (s[1], 0, 0))
    shapes = [jax.ShapeDtypeStruct(t.shape, BF16) for t in theirs]
    outs = _prefetch_call(
        body, name, (N_CHIPS,), [half(t) for t in theirs] + [blk(t) for t in theirs],
        [blk(t) for t in theirs] + [own(t) for t in theirs], shapes + shapes)(where, *fulls, *theirs)
    return outs[:n], outs[n:]


def _chip_sum(slots, where, name):
    n = len(slots)
    steps = 2
    assert all(a.shape[1] % (16 * steps) == 0 for a in slots)

    def body(where_ref, *refs):
        for k in range(n):
            a_ref, out_ref = refs[k], refs[n + k]
            total = a_ref[0].astype(F32)
            for j in range(1, a_ref.shape[0]):
                total = total + a_ref[j].astype(F32)
            out_ref[...] = total

    tile = lambda a: (a.shape[1] // steps, a.shape[2])
    return _prefetch_call(
        body, name, (steps,), [pl.BlockSpec((a.shape[0],) + tile(a), lambda i, s: (0, i, 0)) for a in slots],
        [pl.BlockSpec((None,) + tile(a), lambda i, s: (s[0], i, 0)) for a in slots],
        [jax.ShapeDtypeStruct((2,) + a.shape[1:], F32) for a in slots])(where, *slots)


def _slot_sum(a, name):
    nb, r, c = a.shape
    tr = _row_tile(r)

    def body(a_ref, out_ref):
        total = a_ref[0].astype(F32)
        for j in range(1, nb):
            total = total + a_ref[j].astype(F32)
        out_ref[...] = total

    return pl.pallas_call(
        body, name=name, grid=(r // tr,),
        in_specs=[pl.BlockSpec((nb, tr, c), lambda i: (0, i, 0))],
        out_specs=pl.BlockSpec((tr, c), lambda i: (i, 0)),
        out_shape=jax.ShapeDtypeStruct((r, c), F32), compiler_params=_params(("arbitrary",)),
    )(a)


def _adamw(ws, gs, ms, vs, name, steps=1):
    n = len(ws)
    c1 = 1.0 - ADAM_B1 ** ADAM_STEP
    c2 = 1.0 - ADAM_B2 ** ADAM_STEP
    assert all(w.shape[0] % steps == 0 and (steps == 1 or w.shape[0] // steps % 8 == 0) for w in ws)

    def body(*refs):
        for k in range(n):
            w_ref, g_ref, m_ref, v_ref = (refs[j * n + k] for j in range(4))
            d_ref, m2_ref, v2_ref = (refs[(4 + j) * n + k] for j in range(3))
            gv = g_ref[...]
            m2 = ADAM_B1 * m_ref[...] + (1.0 - ADAM_B1) * gv
            v2 = ADAM_B2 * v_ref[...] + (1.0 - ADAM_B2) * (gv * gv)
            m2_ref[...] = m2
            v2_ref[...] = v2
            d_ref[...] = -ADAM_LR * ((m2 / c1) / (jnp.sqrt(v2 / c2) + ADAM_EPS) + ADAM_WD * w_ref[...])

    blks = [pl.BlockSpec((w.shape[0] // steps, w.shape[1]), lambda i: (i, 0)) for w in ws]
    shapes = [jax.ShapeDtypeStruct(w.shape, F32) for w in ws]
    outs = pl.pallas_call(
        body, name=name, grid=(steps,), in_specs=blks * 4, out_specs=blks * 3, out_shape=shapes * 3,
        compiler_params=_params(("arbitrary",)),
    )(*ws, *gs, *ms, *vs)
    return outs[:n], outs[n:2 * n], outs[2 * n:]


WEIGHTS = ["ffn1_norm", "ffn1_w_gate", "ffn1_w_up", "ffn1_w_down", "mix_norm", "w_in", "conv_w", "conv_b",
           "rg_w_a", "rg_b_a", "rg_w_x", "rg_b_x", "rg_lambda", "q_norm", "k_norm", "rnn_out_norm",
           "attn_out_norm", "w_out", "ffn2_norm", "ffn2_w_gate", "ffn2_w_up", "ffn2_w_down"]
BIG = ["ffn1_w_gate", "ffn1_w_up", "ffn1_w_down", "w_in", "w_out", "ffn2_w_gate", "ffn2_w_up", "ffn2_w_down"]
SMALL = [n for n in WEIGHTS if n not in BIG]
PACK_LANES = 128
PACK_ROW_ALIGN = 8


def _hidden_major(name, a):
    return jnp.transpose(a) if name.endswith(("w_gate", "w_up")) else a


def _pack(parts):
    flat = jnp.concatenate([p.reshape(-1) for p in parts])
    unit = PACK_LANES * PACK_ROW_ALIGN
    padded = -(-flat.shape[0] // unit) * unit
    return jnp.pad(flat, (0, padded - flat.shape[0])).reshape(-1, PACK_LANES)


def _unpack(packed, shapes):
    flat = packed.reshape(-1)
    out, at = [], 0
    for shp in shapes:
        size = math.prod(shp)
        out.append(flat[at:at + size].reshape(shp))
        at += size
    return out


def kernel(x, ffn1_norm, ffn1_w_gate, ffn1_w_up, ffn1_w_down, mix_norm, w_in, conv_w, conv_b, rg_w_a, rg_b_a, rg_w_x, rg_b_x, rg_lambda, q_norm, k_norm, rnn_out_norm, attn_out_norm, w_out, ffn2_norm, ffn2_w_gate, ffn2_w_up, ffn2_w_down, loss_target, m_ffn1_norm, m_ffn1_w_gate, m_ffn1_w_up, m_ffn1_w_down, m_mix_norm, m_w_in, m_conv_w, m_conv_b, m_rg_w_a, m_rg_b_a, m_rg_w_x, m_rg_b_x, m_rg_lambda, m_q_norm, m_k_norm, m_rnn_out_norm, m_attn_out_norm, m_w_out, m_ffn2_norm, m_ffn2_w_gate, m_ffn2_w_up, m_ffn2_w_down, v_ffn1_norm, v_ffn1_w_gate, v_ffn1_w_up, v_ffn1_w_down, v_mix_norm, v_w_in, v_conv_w, v_conv_b, v_rg_w_a, v_rg_b_a, v_rg_w_x, v_rg_b_x, v_rg_lambda, v_q_norm, v_k_norm, v_rnn_out_norm, v_attn_out_norm, v_w_out, v_ffn2_norm, v_ffn2_w_gate, v_ffn2_w_up, v_ffn2_w_down):
    given = dict(locals())
    w = {n: given[n] for n in WEIGHTS}
    m = {n: given["m_" + n] for n in WEIGHTS}
    v = {n: given["v_" + n] for n in WEIGHTS}
    chip = 2 * lax.axis_index("x") + lax.axis_index("y")

    where = jnp.stack([lax.axis_index("c"), chip]).astype(jnp.int32)

    stacks = dict(zip(BIG, _place_shards([_hidden_major(n, w[n][0]) for n in BIG], where, "place_weights")))
    conv_stack = _place_shard(w["conv_w"][0], where, F32, "place_conv_w")
    small = {n: (w[n][0] if w[n].ndim > 2 else w[n]) for n in SMALL if n != "conv_w"}

    grad_x, slots, gs, everyone = _local_step(x[0], loss_target[0], stacks, conv_stack, small, where)

    swapped = _half_swap(_chip_sum([slots[n] for n in BIG], where, "chip_sums"))
    g2s = [t.reshape(t.shape[0] * t.shape[1], t.shape[2]) for t in swapped]
    flat = lambda tree: [_hidden_major(n, tree[n][0]) for n in BIG]
    d2s, m2s, v2s = _adamw(flat(w), g2s, flat(m), flat(v), "adamw_weights", ADAMW_STEPS)
    grads, deltas, new_m, new_v = {}, {}, {}, {}
    for tree, parts in ((grads, g2s), (deltas, d2s), (new_m, m2s), (new_v, v2s)):
        tree.update({n: _hidden_major(n, a).reshape(w[n].shape) for n, a in zip(BIG, parts)})

    full_shapes = [gs[n].shape for n in SMALL]
    *summed, loss = _unpack(_slot_sum(everyone, "small_grad_sum"), full_shapes + [(1, 1)])
    g_parts = dict(zip(SMALL, summed))
    quarter = D_RNN // N_CHIPS
    g_parts["conv_w"] = lax.dynamic_slice_in_dim(g_parts["conv_w"], chip * quarter, quarter, axis=1)
    local_shapes = [w[n].shape for n in SMALL]
    pk = lambda tree: _pack([tree[n] for n in SMALL])
    (d_s,), (m_s,), (v_s,) = _adamw([pk(w)], [pk(g_parts)], [pk(m)], [pk(v)], "adamw_small")
    for tree, packed in ((grads, pk(g_parts)), (deltas, d_s), (new_m, m_s), (new_v, v_s)):
        tree.update(zip(SMALL, _unpack(packed, local_shapes)))

    return (loss[0, 0], grad_x.reshape(x.shape), *[grads[n] for n in WEIGHTS], *[deltas[n] for n in WEIGHTS],
            *[new_m[n] for n in WEIGHTS], *[new_v[n] for n in WEIGHTS])
```

```python
import functools
import math

import jax
import jax.numpy as jnp
from jax import lax
from jax.experimental import pallas as pl
from jax.experimental.pallas import tpu as pltpu

F32 = jnp.float32
BF16 = jnp.bfloat16
MESH = pl.DeviceIdType.MESH

D_MODEL = 1024
N_CHIPS = 4
D_RNN = 512
D_ATT = 512
N_HEADS = 8
HEAD_DIM = 64
RNN_BLOCKS = 8
CONV_W = 4
RG_C = 8.0
N_IN = 2 * D_RNN + 3 * D_ATT
EPS = 1e-6
ATT_BLOCK = 128
ATT_WINDOW = 384
ATT_SPLIT = 256
EXP_ZERO = -105.0

ADAM_LR = 0.001
ADAM_B1 = 0.9
ADAM_B2 = 0.999
ADAM_EPS = 1e-08
ADAM_WD = 0.01
ADAM_STEP = 10

V7X_VMEM_LIMIT = 56 * 1024 * 1024
V7X_MXU_WIDTH = 256
TOKEN_TILE = 512
SUBLANES = 8
FFN_TILE = 256
WGRAD_TILE = 2048
WHOLE_TILE = 1024
ADAMW_STEPS = 8

GELU_K0 = math.sqrt(2.0 / math.pi)
GELU_K1 = 0.044715


def _params(sem=None):
    return pltpu.CompilerParams(dimension_semantics=sem, vmem_limit_bytes=V7X_VMEM_LIMIT)


def _dot(a, b):
    return jnp.dot(a, b, preferred_element_type=F32)


def _dot_nt(a, b):
    return lax.dot_general(a, b, (((1,), (1,)), ((), ())), preferred_element_type=F32)


def _dot_tn(a, b):
    return lax.dot_general(a, b, (((0,), (0,)), ((), ())), preferred_element_type=F32)


def _sigmoid(x):
    return 1.0 / (1.0 + jnp.exp(-x))


def _rms_r(xv):
    return lax.rsqrt(jnp.mean(xv * xv, axis=-1, keepdims=True) + EPS)


def _rms_bwd(xv, r, nw, dh):
    t = dh * nw
    dx = r * t - xv * (r * r * r * jnp.mean(t * xv, axis=-1, keepdims=True))
    dn = jnp.sum(dh * xv * r, axis=0, keepdims=True)
    return dx, dn


def _gelu(x):
    t = jnp.tanh(GELU_K0 * (x + GELU_K1 * x * x * x))
    return 0.5 * x * (1.0 + t)


def _gelu_grad(x):
    t = jnp.tanh(GELU_K0 * (x + GELU_K1 * x * x * x))
    return 0.5 * (1.0 + t) + 0.5 * x * (1.0 - t * t) * (GELU_K0 * (1.0 + 3.0 * GELU_K1 * x * x))


def _expm1_neg(x):
    p = 1.0 + x * (1.0 / 6.0)
    for k in (5.0, 4.0, 3.0, 2.0):
        p = 1.0 + x * (1.0 / k) * p
    return jnp.where(x > -0.25, x * p, jnp.exp(x) - 1.0)


def _log_sigmoid(x):
    return jnp.minimum(x, 0.0) - jnp.log(1.0 + jnp.exp(-jnp.abs(x)))


def _tile(s):
    return min(TOKEN_TILE, s)


def _ffn_chunks(f):
    cut = f // 2 // V7X_MXU_WIDTH * V7X_MXU_WIDTH
    return ((0, cut), (cut, f)) if 0 < cut < f else ((0, f),)


def _ffn_fwd(x, nw, wg, wu, wd, tgt=None, rider=None):
    s, d = x.shape
    f = wg.shape[0]
    tm = min(FFN_TILE, s)
    ni = s // tm
    assert s % tm == 0
    with_loss = tgt is not None
    n_in, n_out = 5 + with_loss, 5 + with_loss

    def body(*refs):
        ins, outs, _, copies = _split_refs(refs, n_in, n_out, rider)
        x_ref, nw_ref, wg_ref, wu_ref, wd_ref = ins[:5]
        out_ref, g_ref, u_ref, hb_ref, ab_ref = outs[:5]
        i = pl.program_id(0)
        finish = _ride(copies, i == 0, i == ni - 1, i == 3 * ni // 4)

        xv = x_ref[...]
        hb = (xv * _rms_r(xv) * nw_ref[...]).astype(BF16)
        hb_ref[...] = hb
        y = jnp.zeros((tm, d), F32)
        for lo, hi in _ffn_chunks(f):
            g = _dot_nt(hb, wg_ref[lo:hi, :])
            u = _dot_nt(hb, wu_ref[lo:hi, :])
            g_ref[:, lo:hi] = g.astype(BF16)
            u_ref[:, lo:hi] = u.astype(BF16)
            ab = (g * _sigmoid(g) * u).astype(BF16)
            ab_ref[:, lo:hi] = ab
            y = y + _dot(ab, wd_ref[lo:hi, :])
        y = xv + 0.5 * y
        if with_loss:
            tgt_ref, loss_ref = ins[5], outs[5]
            diff = y - tgt_ref[...]
            out_ref[...] = diff * (1.0 / d)

            @pl.when(i == 0)
            def _():
                loss_ref[...] = jnp.zeros_like(loss_ref)

            loss_ref[...] += jnp.sum(diff * diff) * (0.5 / d)
        else:
            out_ref[...] = y
        finish()

    row = pl.BlockSpec((tm, d), lambda i: (i, 0))
    weight = pl.BlockSpec((f, d), lambda i: (0, 0), pipeline_mode=pl.Buffered(1))
    in_specs = [row, pl.BlockSpec((1, d), lambda i: (0, 0)), weight, weight, weight]
    args = [x, nw, wg, wu, wd]
    if with_loss:
        in_specs.append(row)
        args.append(tgt)
    blk = pl.BlockSpec((tm, f), lambda i: (i, 0))
    out_shape = [jax.ShapeDtypeStruct((s, d), F32), jax.ShapeDtypeStruct((s, f), BF16),
                 jax.ShapeDtypeStruct((s, f), BF16), jax.ShapeDtypeStruct((s, d), BF16),
                 jax.ShapeDtypeStruct((s, f), BF16)]
    out_specs = [row, blk, blk, row, blk]
    if with_loss:
        out_shape.append(jax.ShapeDtypeStruct((1, 128), F32))
        out_specs.append(pl.BlockSpec((1, 128), lambda i: (0, 0)))
    return _call(body, "ffn_fwd_loss" if with_loss else "ffn_fwd", (ni,), in_specs, out_specs, out_shape, args,
                 rider=rider)


def _ffn_up(x, nw, wg, wu, rider=None):
    s, d = x.shape
    f = wg.shape[0]
    tm = min(FFN_TILE, s)
    ni = s // tm
    assert s % tm == 0

    def body(*refs):
        (x_ref, nw_ref, wg_ref, wu_ref), (g_ref, u_ref, hb_ref, ab_ref), _, copies = _split_refs(refs, 4, 4, rider)
        i = pl.program_id(0)
        finish = _ride(copies, i == 0, i == ni - 1)
        xv = x_ref[...]
        hb = (xv * _rms_r(xv) * nw_ref[...]).astype(BF16)
        hb_ref[...] = hb
        for lo, hi in _ffn_chunks(f):
            g = _dot_nt(hb, wg_ref[lo:hi, :])
            u = _dot_nt(hb, wu_ref[lo:hi, :])
            g_ref[:, lo:hi] = g.astype(BF16)
            u_ref[:, lo:hi] = u.astype(BF16)
            ab_ref[:, lo:hi] = (g * _sigmoid(g) * u).astype(BF16)
        finish()

    row = pl.BlockSpec((tm, d), lambda i: (i, 0))
    weight = pl.BlockSpec((f, d), lambda i: (0, 0), pipeline_mode=pl.Buffered(1))
    blk = pl.BlockSpec((tm, f), lambda i: (i, 0))
    wide = jax.ShapeDtypeStruct((s, f), BF16)
    return _call(body, "ffn_up", (ni,), [row, pl.BlockSpec((1, d), lambda i: (0, 0)), weight, weight],
                 [blk, blk, row, blk], [wide, wide, jax.ShapeDtypeStruct((s, d), BF16), wide], [x, nw, wg, wu],
                 rider=rider)


def _ffn_down(x, ab, wd, rider=None):
    s, d = x.shape
    f = wd.shape[0]
    tm = min(FFN_TILE, s)
    ni = s // tm
    assert s % tm == 0

    def body(*refs):
        (x_ref, ab_ref, wd_ref), (out_ref,), _, copies = _split_refs(refs, 3, 1, rider)
        i = pl.program_id(0)
        finish = _ride(copies, i == 0, i == ni - 1)
        out_ref[...] = x_ref[...] + 0.5 * _dot(ab_ref[...], wd_ref[...])
        finish()

    row = pl.BlockSpec((tm, d), lambda i: (i, 0))
    return _call(body, "ffn_down", (ni,),
                 [row, pl.BlockSpec((tm, f), lambda i: (i, 0)),
                  pl.BlockSpec((f, d), lambda i: (0, 0), pipeline_mode=pl.Buffered(1))],
                 [row], [jax.ShapeDtypeStruct((s, d), F32)], [x, ab, wd], rider=rider)


def _call(body, name, grid, in_specs, out_specs, out_shape, args, scratch=(), rider=None):
    in_specs, out_specs, out_shape, scratch = list(in_specs), list(out_specs), list(out_shape), list(scratch)
    extra, aliases = [], {}
    if rider is not None:
        extra = rider.operands()
        aliases = rider.aliases(len(args), len(out_shape))
        in_specs += [ANY] * len(extra)
        out_specs += [ANY] * len(rider.inplace)
        out_shape += rider.out_shape()
        scratch += rider.scratch()
    return pl.pallas_call(
        body, name=name, grid=grid, in_specs=in_specs, out_specs=out_specs, out_shape=out_shape,
        input_output_aliases=aliases, scratch_shapes=scratch,
        compiler_params=_params(("arbitrary",) * len(grid)),
    )(*args, *extra)


def _ffn_bwd_act(x, nw, dy, g, u, wg, wu, wd, name, rider=None):
    s, d = x.shape
    f = wg.shape[0]
    tm = min(FFN_TILE, s)
    assert s % tm == 0

    def body(*refs):
        ins, outs, _, copies = _split_refs(refs, 8, 5, rider)
        x_ref, nw_ref, dy_ref, g_ref, u_ref, wg_ref, wu_ref, wd_ref = ins
        dx_ref, dg_ref, du_ref, dyb_ref, dnw_ref = outs
        finish = _ride(copies, pl.program_id(0) == 0, pl.program_id(0) == s // tm - 1)
        dyv = dy_ref[...]
        dyb = dyv.astype(BF16)
        dyb_ref[...] = dyb
        dh = jnp.zeros((tm, d), F32)
        for lo, hi in _ffn_chunks(f):
            da = 0.5 * _dot_nt(dyb, wd_ref[lo:hi, :])
            gv = g_ref[:, lo:hi].astype(F32)
            sg = _sigmoid(gv)
            dub = (da * (gv * sg)).astype(BF16)
            dgb = (da * u_ref[:, lo:hi].astype(F32) * (sg * (1.0 + gv * (1.0 - sg)))).astype(BF16)
            dg_ref[:, lo:hi] = dgb
            du_ref[:, lo:hi] = dub
            dh = dh + _dot(dgb, wg_ref[lo:hi, :]) + _dot(dub, wu_ref[lo:hi, :])
        xv = x_ref[...]
        dx, dn = _rms_bwd(xv, _rms_r(xv), nw_ref[...], dh)
        dx_ref[...] = dyv + dx

        @pl.when(pl.program_id(0) == 0)
        def _():
            dnw_ref[...] = jnp.zeros_like(dnw_ref)

        dnw_ref[...] += dn
        finish()

    row = pl.BlockSpec((tm, d), lambda i: (i, 0))
    vec = pl.BlockSpec((1, d), lambda i: (0, 0))
    blk = pl.BlockSpec((tm, f), lambda i: (i, 0))
    weight = pl.BlockSpec((f, d), lambda i: (0, 0), pipeline_mode=pl.Buffered(1))
    return _call(
        body, name, (s // tm,), [row, vec, row, blk, blk, weight, weight, weight], [row, blk, blk, row, vec],
        [jax.ShapeDtypeStruct((s, d), F32), jax.ShapeDtypeStruct((s, f), BF16),
         jax.ShapeDtypeStruct((s, f), BF16), jax.ShapeDtypeStruct((s, d), BF16),
         jax.ShapeDtypeStruct((1, d), F32)],
        [x, nw, dy, g, u, wg, wu, wd], rider=rider)


def _wgrad(a, b, a_spec, b_spec, out_rows, out_cols, scale, name, tk, rider=None, per_step=1):
    s = a.shape[-2]
    nk = s // tk
    steps = N_CHIPS // per_step
    assert s % tk == 0

    def body(*refs):
        (a_ref, b_ref), (out_ref,), (acc,), copies = _split_refs(refs, 2, 1, rider)
        j, k = pl.program_id(0), pl.program_id(1)
        finish = _ride(copies, jnp.logical_and(j == 0, k == 0), jnp.logical_and(j == steps - 1, k == nk - 1))

        @pl.when(k == 0)
        def _():
            acc[...] = jnp.zeros_like(acc)

        acc[...] += _dot_tn(a_ref[...], b_ref[...])

        @pl.when(k == nk - 1)
        def _():
            for t in range(per_step):
                out_ref[t] = (acc[t * out_rows:(t + 1) * out_rows, :] * scale).astype(BF16)

        finish()

    outs = _call(
        body, name, (steps, nk), [a_spec(tk), b_spec(tk)],
        [pl.BlockSpec((per_step, out_rows, out_cols), lambda j, k: (j, 0, 0))],
        [jax.ShapeDtypeStruct((N_CHIPS, out_rows, out_cols), BF16)], [a, b],
        scratch=[pltpu.VMEM((per_step * out_rows, out_cols), F32)], rider=rider)
    return outs[0] if rider is None else outs


def _wgrad_whole(a, b, col_blocks, name, rider=None):
    s, m = a.shape
    n = b.shape[1]
    tk = min(WHOLE_TILE, s)
    nk = s // tk
    assert s % tk == 0
    out_shape = (N_CHIPS, m, n // N_CHIPS) if col_blocks else (N_CHIPS, m // N_CHIPS, n)

    def body(*refs):
        (a_ref, b_ref), (out_ref,), (acc,), copies = _split_refs(refs, 2, 1, rider)
        k = pl.program_id(0)
        finish = _ride(copies, k == 0, k == nk - 1)

        @pl.when(k == 0)
        def _():
            acc[...] = jnp.zeros_like(acc)

        acc[...] += _dot_tn(a_ref[...], b_ref[...])

        @pl.when(k == nk - 1)
        def _():
            for j in range(N_CHIPS):
                if col_blocks:
                    out_ref[j] = acc[:, j * out_shape[2]:(j + 1) * out_shape[2]].astype(BF16)
                else:
                    out_ref[j] = acc[j * out_shape[1]:(j + 1) * out_shape[1], :].astype(BF16)

        finish()

    outs = _call(
        body, name, (nk,), [pl.BlockSpec((tk, m), lambda k: (k, 0)), pl.BlockSpec((tk, n), lambda k: (k, 0))],
        [pl.BlockSpec(out_shape, lambda k: (0, 0, 0))], [jax.ShapeDtypeStruct(out_shape, BF16)], [a, b],
        scratch=[pltpu.VMEM((m, n), F32)], rider=rider)
    return outs[0] if rider is None else outs


def _ffn_wgrad(hidden, shared, scale, name, rider=None):
    s, d = shared.shape
    half = hidden.shape[1] // 2
    return _wgrad(hidden, shared, lambda tk: pl.BlockSpec((tk, half), lambda j, k: (k, j)),
                  lambda tk: pl.BlockSpec((tk, d), lambda j, k: (k, 0)), half // 2, d, scale, name,
                  min(WGRAD_TILE, s), rider, per_step=2)


def _mix_pre(x, nw, win):
    s, d = x.shape
    nb, _, cb = win.shape
    tm = _tile(s)
    assert s % tm == 0

    def body(x_ref, nw_ref, w_ref, p_ref, hb_ref):
        xv = x_ref[...]
        hb = (xv * _rms_r(xv) * nw_ref[...]).astype(BF16)
        hb_ref[...] = hb
        for j in range(nb):
            p_ref[:, j * cb:(j + 1) * cb] = _dot(hb, w_ref[j])

    row = pl.BlockSpec((tm, d), lambda i: (i, 0))
    return pl.pallas_call(
        body, name="mix_pre", grid=(s // tm,),
        in_specs=[row, pl.BlockSpec((1, d), lambda i: (0, 0)),
                  pl.BlockSpec((nb, d, cb), lambda i: (0, 0, 0), pipeline_mode=pl.Buffered(1))],
        out_specs=[pl.BlockSpec((tm, nb * cb), lambda i: (i, 0)), row],
        out_shape=[jax.ShapeDtypeStruct((s, nb * cb), F32), jax.ShapeDtypeStruct((s, d), BF16)],
        compiler_params=_params(("arbitrary",)),
    )(x, nw, win)


def _mix_pre_bwd(x, nw, dres, dpb, win):
    s, d = x.shape
    nb, _, cb = win.shape
    tm = _tile(s)
    assert s % tm == 0

    def body(x_ref, nw_ref, dres_ref, dp_ref, w_ref, dx_ref, dnw_ref):
        dh = jnp.zeros((tm, d), F32)
        for j in range(nb):
            dh = dh + _dot_nt(dp_ref[:, j * cb:(j + 1) * cb], w_ref[j])
        xv = x_ref[...]
        dx, dn = _rms_bwd(xv, _rms_r(xv), nw_ref[...], dh)
        dx_ref[...] = dres_ref[...] + dx

        @pl.when(pl.program_id(0) == 0)
        def _():
            dnw_ref[...] = jnp.zeros_like(dnw_ref)

        dnw_ref[...] += dn

    row = pl.BlockSpec((tm, d), lambda i: (i, 0))
    vec = pl.BlockSpec((1, d), lambda i: (0, 0))
    return pl.pallas_call(
        body, name="mix_pre_bwd", grid=(s // tm,),
        in_specs=[row, vec, row, pl.BlockSpec((tm, nb * cb), lambda i: (i, 0)),
                  pl.BlockSpec((nb, d, cb), lambda i: (0, 0, 0), pipeline_mode=pl.Buffered(1))],
        out_specs=[row, vec],
        out_shape=[jax.ShapeDtypeStruct((s, d), F32), jax.ShapeDtypeStruct((1, d), F32)],
        compiler_params=_params(("arbitrary",)),
    )(x, nw, dres, dpb, win)


def _mix_post(x, yr, ya, nr, na, wout):
    s, d = x.shape
    h = yr.shape[1]
    tm = _tile(s)

    def body(x_ref, yr_ref, ya_ref, nr_ref, na_ref, w_ref, out_ref):
        yrv = yr_ref[...]
        yav = ya_ref[...]
        onb = (yrv * _rms_r(yrv) * nr_ref[...]).astype(BF16)
        oab = (yav * _rms_r(yav) * na_ref[...]).astype(BF16)
        out_ref[...] = x_ref[...] + _dot(onb, w_ref[0:h, :]) + _dot(oab, w_ref[h:2 * h, :])

    row = pl.BlockSpec((tm, d), lambda i: (i, 0))
    half = pl.BlockSpec((tm, h), lambda i: (i, 0))
    vec = pl.BlockSpec((1, h), lambda i: (0, 0))
    return pl.pallas_call(
        body, name="mix_post", grid=(s // tm,),
        in_specs=[row, half, half, vec, vec, pl.BlockSpec((2 * h, d), lambda i: (0, 0))],
        out_specs=row, out_shape=jax.ShapeDtypeStruct((s, d), F32),
        compiler_params=_params(("arbitrary",)),
    )(x, yr, ya, nr, na, wout)


def _mix_post_bwd(dx, yr, ya, nr, na, wout):
    s, d = dx.shape
    h = yr.shape[1]
    tm = _tile(s)

    def body(dx_ref, yr_ref, ya_ref, nr_ref, na_ref, w_ref,
             dyr_ref, dya_ref, yc_ref, dxb_ref, dnr_ref, dna_ref):
        i = pl.program_id(0)
        dxb = dx_ref[...].astype(BF16)
        dxb_ref[...] = dxb
        dyc = _dot_nt(dxb, w_ref[...])
        yrv = yr_ref[...]
        yav = ya_ref[...]
        rr = _rms_r(yrv)
        ra = _rms_r(yav)
        yc_ref[:, 0:h] = (yrv * rr * nr_ref[...]).astype(BF16)
        yc_ref[:, h:2 * h] = (yav * ra * na_ref[...]).astype(BF16)
        dyr, dnr = _rms_bwd(yrv, rr, nr_ref[...], dyc[:, 0:h])
        dya, dna = _rms_bwd(yav, ra, na_ref[...], dyc[:, h:2 * h])
        dyr_ref[...] = dyr
        dya_ref[...] = dya

        @pl.when(i == 0)
        def _():
            dnr_ref[...] = jnp.zeros_like(dnr_ref)
            dna_ref[...] = jnp.zeros_like(dna_ref)

        dnr_ref[...] += dnr
        dna_ref[...] += dna

    row = pl.BlockSpec((tm, d), lambda i: (i, 0))
    half = pl.BlockSpec((tm, h), lambda i: (i, 0))
    vec = pl.BlockSpec((1, h), lambda i: (0, 0))
    return pl.pallas_call(
        body, name="mix_post_bwd", grid=(s // tm,),
        in_specs=[row, half, half, vec, vec, pl.BlockSpec((2 * h, d), lambda i: (0, 0))],
        out_specs=[half, half, pl.BlockSpec((tm, 2 * h), lambda i: (i, 0)), row, vec, vec],
        out_shape=[jax.ShapeDtypeStruct((s, h), F32), jax.ShapeDtypeStruct((s, h), F32),
                   jax.ShapeDtypeStruct((s, 2 * h), BF16), jax.ShapeDtypeStruct((s, d), BF16),
                   jax.ShapeDtypeStruct((1, h), F32), jax.ShapeDtypeStruct((1, h), F32)],
        compiler_params=_params(("arbitrary",)),
    )(dx, yr, ya, nr, na, wout)


def _shift_down(xv, s, prev8):
    rolled = pltpu.roll(xv, s, 0)
    row8 = lax.broadcasted_iota(jnp.int32, prev8.shape, 0)
    head = jnp.where(row8 < s, pltpu.roll(prev8, s, 0), rolled[0:8, :])
    return jnp.concatenate([head, rolled[8:, :]], axis=0)


def _shift_up(xv, s, next8):
    n = xv.shape[0]
    rolled = pltpu.roll(xv, n - s, 0)
    row8 = lax.broadcasted_iota(jnp.int32, next8.shape, 0)
    tail = jnp.where(row8 >= 8 - s, pltpu.roll(next8, 8 - s, 0), rolled[n - 8:, :])
    return jnp.concatenate([rolled[:n - 8, :], tail], axis=0)


def _scan_fwd(a, b):
    n = a.shape[0]
    sub = lax.broadcasted_iota(jnp.int32, a.shape, 0) % SUBLANES
    s = 1
    while s < SUBLANES:
        ok = sub >= s
        b = jnp.where(ok, a * pltpu.roll(b, s, 0) + b, b)
        a = jnp.where(ok, a * pltpu.roll(a, s, 0), a)
        s *= 2
    groups = []
    before = jnp.zeros((1, a.shape[1]), F32)
    for g in range(n // SUBLANES):
        rows = slice(g * SUBLANES, (g + 1) * SUBLANES)
        groups.append(a[rows] * before + b[rows])
        before = groups[-1][SUBLANES - 1:]
    return jnp.concatenate(groups, axis=0)


def _scan_bwd(a, b):
    n = a.shape[0]
    sub = lax.broadcasted_iota(jnp.int32, a.shape, 0) % SUBLANES
    s = 1
    while s < SUBLANES:
        ok = sub < SUBLANES - s
        b = jnp.where(ok, a * pltpu.roll(b, n - s, 0) + b, b)
        a = jnp.where(ok, a * pltpu.roll(a, n - s, 0), a)
        s *= 2
    groups = []
    after = jnp.zeros((1, a.shape[1]), F32)
    for g in reversed(range(n // SUBLANES)):
        rows = slice(g * SUBLANES, (g + 1) * SUBLANES)
        groups.append(a[rows] * after + b[rows])
        after = groups[-1][:1]
    return jnp.concatenate(groups[::-1], axis=0)


def _rglru_gates(xv, prev8, cw_ref, cb_ref, wa_ref, ba_ref, wx_ref, bx_ref, lam_ref):
    x1 = _shift_down(xv, 1, prev8)
    x2 = _shift_down(xv, 2, prev8)
    x3 = _shift_down(xv, 3, prev8)
    xc = cw_ref[3:4, :] * xv + cw_ref[2:3, :] * x1 + cw_ref[1:2, :] * x2 + cw_ref[0:1, :] * x3 + cb_ref[...]
    xcb = xc.astype(BF16)
    r = _sigmoid(_dot(xcb, wa_ref[...]) + ba_ref[...])
    ig = _sigmoid(_dot(xcb, wx_ref[...]) + bx_ref[...])
    c = RG_C * _log_sigmoid(lam_ref[...])
    la = r * c
    a = jnp.exp(la)
    m = jnp.sqrt(-_expm1_neg(2.0 * la))
    return (x1, x2, x3), xc, xcb, r, ig, c, a, m


def _rglru_fwd(proj, cw, cb, wa, ba, wx, bx, lam):
    s = proj.shape[0]
    w = D_RNN
    tm = _tile(s)

    def body(xr_ref, gate_ref, cw_ref, cb_ref, wa_ref, ba_ref, wx_ref, bx_ref, lam_ref,
             y_ref, h_ref, prev, hlast):
        @pl.when(pl.program_id(0) == 0)
        def _():
            prev[...] = jnp.zeros_like(prev)
            hlast[...] = jnp.zeros_like(hlast)

        xv = xr_ref[...]
        _, xc, _, _, ig, _, a, m = _rglru_gates(xv, prev[...], cw_ref, cb_ref, wa_ref, ba_ref,
                                                wx_ref, bx_ref, lam_ref)
        b = m * (ig * xc)
        row = lax.broadcasted_iota(jnp.int32, b.shape, 0)
        b = jnp.where(row == 0, b + a * hlast[...], b)
        h = _scan_fwd(a, b)
        h_ref[...] = h
        y_ref[...] = h * _gelu(gate_ref[...])
        prev[...] = xv[tm - 8:, :]
        hlast[...] = h[tm - 1:tm, :]

    vec = pl.BlockSpec((1, w), lambda i: (0, 0))
    sq = pl.BlockSpec((w, w), lambda i: (0, 0))
    out = pl.BlockSpec((tm, w), lambda i: (i, 0))
    return pl.pallas_call(
        body, name="rglru_fwd", grid=(s // tm,),
        in_specs=[pl.BlockSpec((tm, w), lambda i: (i, 0)), pl.BlockSpec((tm, w), lambda i: (i, 1)),
                  pl.BlockSpec((CONV_W, w), lambda i: (0, 0)), vec, sq, vec, sq, vec, vec],
        out_specs=[out, out],
        out_shape=[jax.ShapeDtypeStruct((s, w), F32), jax.ShapeDtypeStruct((s, w), F32)],
        scratch_shapes=[pltpu.VMEM((8, w), F32), pltpu.VMEM((1, w), F32)],
        compiler_params=_params(("arbitrary",)),
    )(proj, proj, cw, cb, wa, ba, wx, bx, lam)


def _rglru_bwd(proj, hseq, dyr, cw, cb, wa, ba, wx, bx, lam):
    s = proj.shape[0]
    w = D_RNN
    tm = _tile(s)
    nt = s // tm
    t8 = tm // 8

    def body(xr_ref, xp_ref, gate_ref, h_ref, hp_ref, dy_ref, cw_ref, cb_ref, wa_ref, ba_ref,
             wx_ref, bx_ref, lam_ref,
             dxr_ref, dgate_ref, dcw_ref, dcb_ref, dwa_ref, dba_ref, dwx_ref, dbx_ref, dlam_ref,
             carry, dxc_next):
        i = pl.program_id(0)
        first_tile = i == nt - 1

        @pl.when(i == 0)
        def _():
            carry[...] = jnp.zeros_like(carry)
            dxc_next[...] = jnp.zeros_like(dxc_next)
            for ref in (dcw_ref, dcb_ref, dwa_ref, dba_ref, dwx_ref, dbx_ref, dlam_ref):
                ref[...] = jnp.zeros_like(ref)

        xv = xr_ref[...]
        prev8 = jnp.where(first_tile, 0.0, xp_ref[...])
        hprev8 = jnp.where(first_tile, 0.0, hp_ref[...])
        (x1, x2, x3), xc, xcb, r, ig, c, a, m = _rglru_gates(
            xv, prev8, cw_ref, cb_ref, wa_ref, ba_ref, wx_ref, bx_ref, lam_ref)
        gv = gate_ref[...]
        hv = h_ref[...]
        dy = dy_ref[...]
        dgate_ref[...] = (dy * hv * _gelu_grad(gv)).astype(BF16)
        dh = dy * _gelu(gv)
        row = lax.broadcasted_iota(jnp.int32, dh.shape, 0)
        dh = jnp.where(row == tm - 1, dh + carry[...], dh)
        a_up = jnp.where(row == tm - 1, 0.0, pltpu.roll(a, tm - 1, 0))
        lam_t = _scan_bwd(a_up, dh)
        carry[...] = a[0:1, :] * lam_t[0:1, :]
        hm1 = _shift_down(hv, 1, hprev8)
        da = lam_t * hm1
        ixc = ig * xc
        dm = lam_t * ixc
        dig = lam_t * m * xc
        dxc = lam_t * m * ig
        dla = da * a - dm * (a * a) / m
        dr = dla * c
        dlam_ref[...] += jnp.sum(dla * r, axis=0, keepdims=True)
        dpa = dr * r * (1.0 - r)
        dpi = dig * ig * (1.0 - ig)
        dba_ref[...] += jnp.sum(dpa, axis=0, keepdims=True)
        dbx_ref[...] += jnp.sum(dpi, axis=0, keepdims=True)
        dpab = dpa.astype(BF16)
        dpib = dpi.astype(BF16)
        dwa_ref[...] += _dot_tn(xcb, dpab)
        dwx_ref[...] += _dot_tn(xcb, dpib)
        dxc = dxc + _dot_nt(dpab, wa_ref[...]) + _dot_nt(dpib, wx_ref[...])
        dcb_ref[...] += jnp.sum(dxc, axis=0, keepdims=True)
        dcw_ref[3:4, :] += jnp.sum(dxc * xv, axis=0, keepdims=True)
        dcw_ref[2:3, :] += jnp.sum(dxc * x1, axis=0, keepdims=True)
        dcw_ref[1:2, :] += jnp.sum(dxc * x2, axis=0, keepdims=True)
        dcw_ref[0:1, :] += jnp.sum(dxc * x3, axis=0, keepdims=True)
        nxt = dxc_next[...]
        dxr = (cw_ref[3:4, :] * dxc + cw_ref[2:3, :] * _shift_up(dxc, 1, nxt)
               + cw_ref[1:2, :] * _shift_up(dxc, 2, nxt) + cw_ref[0:1, :] * _shift_up(dxc, 3, nxt))
        dxr_ref[...] = dxr.astype(BF16)
        dxc_next[...] = dxc[0:8, :]

        @pl.when(first_tile)
        def _():
            lv = lam_ref[...]
            dlam_ref[...] = dlam_ref[...] * (RG_C * _sigmoid(-lv))

    rev = lambda i: nt - 1 - i
    vec = pl.BlockSpec((1, w), lambda i: (0, 0))
    sq = pl.BlockSpec((w, w), lambda i: (0, 0))
    cur = lambda col: pl.BlockSpec((tm, w), lambda i: (rev(i), col))
    before = lambda cols: pl.BlockSpec((8, w), lambda i: (jnp.maximum(rev(i) * t8 - 1, 0), 0))
    return pl.pallas_call(
        body, name="rglru_bwd", grid=(nt,),
        in_specs=[cur(0), before(None), cur(1), cur(0), before(None), cur(0),
                  pl.BlockSpec((CONV_W, w), lambda i: (0, 0)), vec, sq, vec, sq, vec, vec],
        out_specs=[cur(0), cur(0), pl.BlockSpec((CONV_W, w), lambda i: (0, 0)), vec, sq, vec, sq, vec, vec],
        out_shape=[jax.ShapeDtypeStruct((s, w), BF16), jax.ShapeDtypeStruct((s, w), BF16),
                   jax.ShapeDtypeStruct((CONV_W, w), F32), jax.ShapeDtypeStruct((1, w), F32),
                   jax.ShapeDtypeStruct((w, w), F32), jax.ShapeDtypeStruct((1, w), F32),
                   jax.ShapeDtypeStruct((w, w), F32), jax.ShapeDtypeStruct((1, w), F32),
                   jax.ShapeDtypeStruct((1, w), F32)],
        scratch_shapes=[pltpu.VMEM((1, w), F32), pltpu.VMEM((8, w), F32)],
        compiler_params=_params(("arbitrary",)),
    )(proj, proj, proj, hseq, hseq, dyr, cw, cb, wa, ba, wx, bx, lam)


def _sb_logs(z, valid):
    l1p = jnp.log(1.0 + jnp.exp(-jnp.abs(z)))
    lb = jnp.minimum(z, 0.0) - l1p
    lm = jnp.where(valid, -jnp.maximum(z, 0.0) - l1p, 0.0)
    return lb, lm


class _Window:
    def __init__(self):
        blk, win, cut = ATT_BLOCK, ATT_WINDOW, ATT_SPLIT
        self.row = lax.broadcasted_iota(jnp.int32, (blk, win), 0)
        self.col = lax.broadcasted_iota(jnp.int32, (blk, win), 1)

        def tri(n, later):
            j = lax.broadcasted_iota(jnp.int32, (n, n), 0)
            s = lax.broadcasted_iota(jnp.int32, (n, n), 1)
            return jnp.where((j > s) if later else (j < s), 1.0, 0.0).astype(BF16)

        self.later = (tri(cut, True), tri(win - cut, True))
        self.earlier = (tri(cut, False), tri(win - cut, False))

    def place(self, qi, g):
        end = (qi + 1) * ATT_BLOCK - g * ATT_WINDOW
        start = pl.multiple_of(jnp.maximum(end - ATT_WINDOW, 0), ATT_BLOCK)
        valid = start + self.col < jnp.minimum(qi * ATT_BLOCK + self.row, end)
        return start, valid

    @staticmethod
    def _parts(xv):
        hi = xv.astype(BF16)
        lo = (xv - hi.astype(F32)).astype(BF16)
        cut = ATT_SPLIT
        sums = (jnp.sum(xv[:, :cut], axis=1, keepdims=True), jnp.sum(xv[:, cut:], axis=1, keepdims=True))
        return (hi[:, :cut], lo[:, :cut]), (hi[:, cut:], lo[:, cut:]), sums

    def sums_after(self, xv, carry):
        (h0, l0), (h1, l1), (s0, s1) = self._parts(xv)
        first = _dot(h0, self.later[0]) + _dot(l0, self.later[0]) + (s1 + carry)
        last = _dot(h1, self.later[1]) + _dot(l1, self.later[1]) + carry
        return jnp.concatenate([first, last], axis=1), s0 + s1

    def sums_before(self, xv, carry):
        (h0, l0), (h1, l1), (s0, s1) = self._parts(xv)
        first = _dot(h0, self.earlier[0]) + _dot(l0, self.earlier[0]) + carry
        last = _dot(h1, self.earlier[1]) + _dot(l1, self.earlier[1]) + (s0 + carry)
        return jnp.concatenate([first, last], axis=1), s0 + s1


class _HeadPair:
    def __init__(self):
        lanes = 2 * HEAD_DIM
        lane = lax.broadcasted_iota(jnp.int32, (1, lanes), 1)
        self.masks = [lane // HEAD_DIM == h for h in (0, 1)]
        i = lax.broadcasted_iota(jnp.int32, (lanes, lanes), 0) // HEAD_DIM
        j = lax.broadcasted_iota(jnp.int32, (lanes, lanes), 1) // HEAD_DIM
        self.same_head = jnp.where(i == j, 1.0, 0.0).astype(BF16)

    def only(self, h, xv):
        return jnp.where(self.masks[h], xv, jnp.zeros_like(xv))

    def merge(self, per_head):
        return jnp.where(self.masks[0], per_head[0], per_head[1])

    def mean(self, xv):
        hi = xv.astype(BF16)
        lo = (xv - hi.astype(F32)).astype(BF16)
        return (_dot(hi, self.same_head) + _dot(lo, self.same_head)) * (1.0 / HEAD_DIM)

    def rms_r(self, xv):
        return lax.rsqrt(self.mean(xv * xv) + EPS)

    def rms_bwd(self, xv, r, nw, dh):
        t = dh * nw
        dx = r * t - xv * (r * r * r * self.mean(t * xv))
        dn = jnp.sum(dh * xv * r, axis=0, keepdims=True)
        return dx, dn[:, :HEAD_DIM] + dn[:, HEAD_DIM:]


def _attn_fwd(proj, qg, kg, rider=None):
    s = proj.shape[0]
    blk, win, dh = ATT_BLOCK, ATT_WINDOW, HEAD_DIM
    nq = s // blk
    scale = 1.0 / math.sqrt(dh)
    heads = (0, 1)
    assert s >= win and s % blk == 0

    def body(*refs):
        (q_ref, k_ref, v_ref, qg_ref, kg_ref), (o_ref,), (qn, kn, vb), copies = _split_refs(refs, 5, 1, rider)
        finish = _ride(copies, pl.program_id(0) == 0, pl.program_id(0) == N_HEADS // 2 - 1)
        wd, hp = _Window(), _HeadPair()
        qv = q_ref[...]
        qn[...] = (qv * hp.rms_r(qv) * qg_ref[...] * scale).astype(BF16)
        kv = k_ref[...]
        kn[...] = (kv * hp.rms_r(kv) * kg_ref[...]).astype(BF16)
        vb[...] = v_ref[...].astype(BF16)

        def q_step(qi, _):
            qoff = pl.multiple_of(qi * blk, blk)
            qt = qn[pl.ds(qoff, blk), :]
            qts = [hp.only(h, qt) for h in heads]

            def more(carry):
                g, live = carry[:2]
                return jnp.logical_and((qi + 1) * blk - g * win > 0, live > 0)

            def window(carry):
                g, _, accs, runs = carry
                start, valid = wd.place(qi, g)
                kt = kn[pl.ds(start, win), :]
                zs = [_dot_nt(qts[h], kt) for h in heads]
                logs = [_sb_logs(z, valid) for z in zs]
                sums = [wd.sums_after(logs[h][1], runs[h]) for h in heads]
                wgts = [jnp.where(valid, jnp.exp(logs[h][0] + sums[h][0]), 0.0).astype(BF16) for h in heads]
                vt = vb[pl.ds(start, win), :]
                accs = tuple(accs[h] + _dot(wgts[h], vt) for h in heads)
                runs = tuple(runs[h] + sums[h][1] for h in heads)
                live = (jnp.maximum(jnp.max(runs[0]), jnp.max(runs[1])) > EXP_ZERO).astype(jnp.int32)
                return g + 1, live, accs, runs

            zero = lambda cols: tuple(jnp.zeros((blk, cols), F32) for _ in heads)
            _, _, accs, _ = lax.while_loop(more, window, (jnp.int32(0), jnp.int32(1), zero(2 * dh), zero(1)))
            o_ref[pl.ds(qoff, blk), :] = hp.merge(accs)
            return 0

        lax.fori_loop(0, nq, q_step, 0)
        finish()

    pair = lambda group: pl.BlockSpec((s, 2 * dh), lambda p: (0, group * (D_ATT // (2 * dh)) + p))
    vec = pl.BlockSpec((1, 2 * dh), lambda p: (0, 0))
    return _call(
        body, "attn_fwd", (N_HEADS // 2,), [pair(2), pair(3), pair(4), vec, vec], [pair(0)],
        [jax.ShapeDtypeStruct((s, D_ATT), F32)], [proj, proj, proj, jnp.tile(qg, (1, 2)), jnp.tile(kg, (1, 2))],
        scratch=[pltpu.VMEM((s, 2 * dh), BF16)] * 3, rider=rider)


def _attn_bwd(proj, dya, qg, kg, rider=None):
    s = proj.shape[0]
    blk, win, dh = ATT_BLOCK, ATT_WINDOW, HEAD_DIM
    nq = s // blk
    max_windows = -(-s // win) + 1
    scale = 1.0 / math.sqrt(dh)
    steps = N_HEADS // 2
    heads = (0, 1)
    assert s >= win and s % blk == 0

    def body(*refs):
        ins, outs, scratch, copies = _split_refs(refs, 6, 5, rider)
        q_ref, k_ref, v_ref, do_ref, qg_ref, kg_ref = ins
        dq_ref, dk_ref, dv_ref, dqg_ref, dkg_ref = outs
        qn, kn, vb, dob, runs_ref, dqn, dkn, dvn = scratch
        finish = _ride(copies, pl.program_id(0) == 0, pl.program_id(0) == steps - 1)
        wd, hp = _Window(), _HeadPair()

        @pl.when(pl.program_id(0) == 0)
        def _():
            dqg_ref[...] = jnp.zeros_like(dqg_ref)
            dkg_ref[...] = jnp.zeros_like(dkg_ref)

        qv = q_ref[...]
        qn[...] = (qv * hp.rms_r(qv) * qg_ref[...] * scale).astype(BF16)
        kv = k_ref[...]
        kn[...] = (kv * hp.rms_r(kv) * kg_ref[...]).astype(BF16)
        vb[...] = v_ref[...].astype(BF16)
        dob[...] = do_ref[...].astype(BF16)
        dkn[...] = jnp.zeros_like(dkn)
        dvn[...] = jnp.zeros_like(dvn)

        def q_step(qi, _):
            qoff = pl.multiple_of(qi * blk, blk)
            qt = qn[pl.ds(qoff, blk), :]
            dot = dob[pl.ds(qoff, blk), :]
            qts = [hp.only(h, qt) for h in heads]
            dots = [hp.only(h, dot) for h in heads]

            zero = lambda cols: tuple(jnp.zeros((blk, cols), F32) for _ in heads)

            def logs_of(g):
                start, valid = wd.place(qi, g)
                kt = kn[pl.ds(start, win), :]
                return [_sb_logs(_dot_nt(qts[h], kt), valid) for h in heads]

            def row_sums(logs):
                return tuple(jnp.sum(logs[h][1], axis=1, keepdims=True) for h in heads)

            def still_live(runs):
                return jnp.maximum(jnp.max(runs[0]), jnp.max(runs[1])) > EXP_ZERO

            def window_grads(g, logs, runs, esums):
                start, valid = wd.place(qi, g)
                kt = kn[pl.ds(start, win), :]
                vt = vb[pl.ds(start, win), :]
                dws = [_dot_nt(dots[h], vt) for h in heads]
                tails = [wd.sums_after(logs[h][1], runs[h])[0] for h in heads]
                wgts = [jnp.where(valid, jnp.exp(logs[h][0] + tails[h]), 0.0) for h in heads]
                es = [dws[h] * wgts[h] for h in heads]
                befores = [wd.sums_before(es[h], esums[h]) for h in heads]
                dzbs = []
                for h in heads:
                    beta = jnp.exp(logs[h][0])
                    dz = jnp.where(valid, es[h] * (1.0 - beta) - befores[h][0] * beta, 0.0)
                    dzbs.append(dz.astype(BF16))
                dkn[pl.ds(start, win), :] += _dot_tn(dzbs[0], qts[0]) + _dot_tn(dzbs[1], qts[1])
                dvn[pl.ds(start, win), :] += (_dot_tn(wgts[0].astype(BF16), dots[0])
                                              + _dot_tn(wgts[1].astype(BF16), dots[1]))
                return tuple(_dot(dzbs[h], kt) for h in heads), tuple(befores[h][1] for h in heads)

            logs0 = logs_of(0)
            runs1 = row_sums(logs0)

            def one_window():
                return window_grads(0, logs0, zero(1), zero(1))[0]

            def all_windows():
                def more(carry):
                    g, live = carry[:2]
                    return jnp.logical_and((qi + 1) * blk - g * win > 0, live > 0)

                def run_window(carry):
                    g, _, runs = carry
                    for h in heads:
                        runs_ref[h, g] = runs[h]
                    sums = row_sums(logs_of(g))
                    runs = tuple(runs[h] + sums[h] for h in heads)
                    return g + 1, still_live(runs).astype(jnp.int32), runs

                for h in heads:
                    runs_ref[h, 0] = jnp.zeros((blk, 1), F32)
                windows, _, _ = lax.while_loop(more, run_window, (jnp.int32(1), jnp.int32(1), runs1))

                def k_window(gg, carry):
                    dq_accs, esums = carry
                    g = windows - 1 - gg
                    parts, totals = window_grads(g, logs_of(g), [runs_ref[h, g] for h in heads], esums)
                    return (tuple(dq_accs[h] + parts[h] for h in heads),
                            tuple(esums[h] + totals[h] for h in heads))

                return lax.fori_loop(0, windows, k_window, (zero(2 * dh), zero(1)))[0]

            earlier_keys = (qi + 1) * blk - win > 0
            dq_accs = lax.cond(jnp.logical_and(earlier_keys, still_live(runs1)), all_windows, one_window)
            dqn[pl.ds(qoff, blk), :] = hp.merge(dq_accs)
            return 0

        lax.fori_loop(0, nq, q_step, 0)

        dq, dqg = hp.rms_bwd(qv, hp.rms_r(qv), qg_ref[...] * scale, dqn[...])
        dq_ref[...] = dq.astype(BF16)
        dqg_ref[...] += dqg * scale
        dk, dkg = hp.rms_bwd(kv, hp.rms_r(kv), kg_ref[...], dkn[...])
        dk_ref[...] = dk.astype(BF16)
        dkg_ref[...] += dkg
        dv_ref[...] = dvn[...].astype(BF16)
        finish()

    pair = lambda group: pl.BlockSpec((s, 2 * dh), lambda p: (0, group * (D_ATT // (2 * dh)) + p))
    vec2 = pl.BlockSpec((1, 2 * dh), lambda p: (0, 0))
    vec = pl.BlockSpec((1, dh), lambda p: (0, 0))
    return _call(
        body, "attn_bwd", (steps,), [pair(2), pair(3), pair(4), pair(0), vec2, vec2],
        [pair(0), pair(0), pair(0), vec, vec],
        [jax.ShapeDtypeStruct((s, D_ATT), BF16)] * 3 + [jax.ShapeDtypeStruct((1, dh), F32)] * 2,
        [proj, proj, proj, dya, jnp.tile(qg, (1, 2)), jnp.tile(kg, (1, 2))],
        scratch=[pltpu.VMEM((s, 2 * dh), BF16)] * 4 + [pltpu.VMEM((2, max_windows, blk, 1), F32)]
        + [pltpu.VMEM((s, 2 * dh), F32)] * 3, rider=rider)


def _block_diag(w):
    n, c, d = w.shape
    return jnp.einsum("ncd,nm->ncmd", w, jnp.eye(n, dtype=w.dtype)).reshape(n * c, n * d)


def _diag_blocks(full, n):
    c = full.shape[0] // n
    return jnp.stack([full[i * c:(i + 1) * c, i * c:(i + 1) * c] for i in range(n)])


FFN1 = ["ffn1_w_gate", "ffn1_w_up", "ffn1_w_down"]
FFN2 = ["ffn2_w_gate", "ffn2_w_up", "ffn2_w_down"]
MIXER = ["w_in", "w_out"]


def _pair_sums(gb, names, where):
    theirs = _pair_exchange([gb[n] for n in names], "pair_exchange_" + names[0])
    pair, own = _pair_sum([gb[n] for n in names], theirs, where, "pair_sum_" + names[0])
    return _chip_rider(pair, own)


def _local_step(x, tgt, stacks, conv_stack, small, where):
    gate_up, down = FFN1[:2], FFN1[2:]
    big = dict(zip(gate_up, _gather_weights([stacks[n] for n in gate_up], [])))
    wa = _block_diag(small["rg_w_a"]).astype(BF16)
    wx = _block_diag(small["rg_w_x"]).astype(BF16)

    whole = lambda names: [big[n].reshape(-1, D_MODEL) for n in names]
    g1, u1, hb1, ab1, *landed = _ffn_up(x, small["ffn1_norm"], *whole(gate_up),
                                        rider=_gather_rider([stacks[n] for n in down + MIXER], [conv_stack]))
    big.update(zip(down + MIXER, landed))
    x1, = _ffn_down(x, ab1, *whole(down))
    conv_w = jnp.transpose(landed[-1], (1, 0, 2)).reshape(CONV_W, D_RNN)
    wout = big["w_out"].reshape(D_MODEL, D_MODEL)
    rg = (conv_w, small["conv_b"], wa, small["rg_b_a"], wx, small["rg_b_x"], small["rg_lambda"])
    proj, hb2 = _mix_pre(x1, small["mix_norm"], big["w_in"])
    yr, hseq = _rglru_fwd(proj, *rg)
    ya, *landed = _attn_fwd(proj, small["q_norm"], small["k_norm"], _gather_rider([stacks[n] for n in FFN2], []))
    big.update(zip(FFN2, landed))
    x2 = _mix_post(x1, yr, ya, small["rnn_out_norm"], small["attn_out_norm"], wout)
    dx3, g2, u2, hb3, ab3, loss = _ffn_fwd(x2, small["ffn2_norm"], *whole(FFN2), tgt)

    gb, gs, slots = {}, {}, {}
    dx2, dg2, du2, dyb2, gs["ffn2_norm"] = _ffn_bwd_act(x2, small["ffn2_norm"], dx3, g2, u2, *whole(FFN2), "ffn2_bwd")
    gb["ffn2_w_gate"] = _ffn_wgrad(dg2, hb3, 1.0, "wgrad_gate_ffn2")
    gb["ffn2_w_up"] = _ffn_wgrad(du2, hb3, 1.0, "wgrad_up_ffn2")
    gb["ffn2_w_down"] = _ffn_wgrad(ab3, dyb2, 0.5, "wgrad_down_ffn2")
    dyr, dya, ycat, dxb2, gs["rnn_out_norm"], gs["attn_out_norm"] = _mix_post_bwd(
        dx2, yr, ya, small["rnn_out_norm"], small["attn_out_norm"], wout)
    gb["w_out"] = _wgrad_whole(ycat, dxb2, False, "wgrad_out")
    early = FFN2 + ["w_out"]
    dq, dk, dv, gs["q_norm"], gs["k_norm"], *done = _attn_bwd(
        proj, dya, small["q_norm"], small["k_norm"], _pair_sums(gb, early, where))
    slots.update(zip(early, done))
    dxr, dgate, gs["conv_w"], gs["conv_b"], dwa, gs["rg_b_a"], dwx, gs["rg_b_x"], gs["rg_lambda"] = _rglru_bwd(
        proj, hseq, dyr, *rg)
    gs["rg_w_a"] = _diag_blocks(dwa, RNN_BLOCKS)
    gs["rg_w_x"] = _diag_blocks(dwx, RNN_BLOCKS)
    dpb = jnp.concatenate([dxr, dgate, dq, dk, dv], axis=1)
    dx1, gs["mix_norm"] = _mix_pre_bwd(x1, small["mix_norm"], dx2, dpb, big["w_in"])
    dx0, dg1, du1, dyb1, gs["ffn1_norm"] = _ffn_bwd_act(x, small["ffn1_norm"], dx1, g1, u1, *whole(FFN1), "ffn1_bwd")

    mine = _place_shard(_pack([gs[n] for n in SMALL] + [loss[:, :1]]), where, F32, "place_small_grads",
                        by_device=True)
    gb["ffn1_w_gate"], everyone = _ffn_wgrad(dg1, hb1, 1.0, "wgrad_gate_ffn1", _small_rider(mine))
    gb["ffn1_w_up"], slots["ffn1_w_gate"] = _ffn_wgrad(
        du1, hb1, 1.0, "wgrad_up_ffn1", _pair_sums(gb, ["ffn1_w_gate"], where))
    gb["ffn1_w_down"], slots["ffn1_w_up"] = _ffn_wgrad(
        ab1, dyb1, 0.5, "wgrad_down_ffn1", _pair_sums(gb, ["ffn1_w_up"], where))
    gb["w_in"], slots["ffn1_w_down"] = _wgrad_whole(
        hb2, dpb, True, "wgrad_in", _pair_sums(gb, ["ffn1_w_down"], where))
    last = _pair_sums(gb, ["w_in"], where)
    slots["w_in"], = _chip_exchange(last.plain, last.inplace)
    return dx0, slots, gs, everyone


ANY = pl.BlockSpec(memory_space=pl.ANY)


def _place():
    x, y, c = lax.axis_index("x"), lax.axis_index("y"), lax.axis_index("c")
    other_chips = [(1 - x, y), (x, 1 - y), (1 - x, 1 - y)]
    return x, y, c, 2 * x + y, other_chips


def _remote(src, dst, send_sem, recv_sem, to):
    return pltpu.make_async_remote_copy(src_ref=src, dst_ref=dst, send_sem=send_sem, recv_sem=recv_sem,
                                        device_id=to, device_id_type=MESH)


def _copy_plan(pairs):
    sends = [functools.partial(_remote, *a) for a, _ in pairs]
    arrivals = [functools.partial(_remote, *b) for _, b in pairs]
    return sends, arrivals


class _Rider:
    def __init__(self, plan, plain, inplace, n_copies=None, relay=None, n_relay=0):
        self.plan, self.plain, self.inplace = plan, list(plain), list(inplace)
        self.n_copies = n_copies or 3 * len(self.inplace)
        self.relay, self.n_relay = relay, n_relay

    def operands(self):
        return self.plain + self.inplace

    def out_shape(self):
        return [jax.ShapeDtypeStruct(a.shape, a.dtype) for a in self.inplace]

    def aliases(self, inputs_before, outputs_before):
        return {inputs_before + len(self.plain) + k: outputs_before + k for k in range(len(self.inplace))}

    def scratch(self):
        relay = [pltpu.SemaphoreType.DMA((self.n_relay,))] * 2 if self.relay else []
        return [pltpu.SemaphoreType.DMA((self.n_copies,))] * 2 + relay


def _split_refs(refs, n_in, n_out, rider):
    if rider is None:
        return refs[:n_in], refs[n_in:n_in + n_out], refs[n_in + n_out:], None
    r_in, r_out = len(rider.operands()), len(rider.inplace)
    outs_at = n_in + r_in
    n_sems = len(rider.scratch())
    rest = refs[outs_at + n_out + r_out:]
    sems = rest[len(rest) - n_sems:]
    filled = refs[outs_at + n_out:outs_at + n_out + r_out]
    copies = functools.partial(rider.plan, refs[n_in:n_in + len(rider.plain)], filled, *sems[:2])
    relay = functools.partial(rider.relay, filled, *sems[2:]) if rider.relay else None
    return refs[:n_in], refs[outs_at:outs_at + n_out], rest[:len(rest) - n_sems], (copies, relay)


def _ride(copies, first, last, middle=None):
    if copies is None:
        return lambda: None
    copies, relay = copies

    @pl.when(first)
    def _():
        _start(copies()[0])

    def start_relay():
        for make in copies()[1]:
            make().wait_recv()
        _start(relay()[0])

    if relay is not None and middle is not None:
        pl.when(middle)(start_relay)

    def finish():
        @pl.when(last)
        def _():
            if relay is None:
                _finish(*copies())
            else:
                if middle is None:
                    start_relay()
                _finish(copies()[0] + relay()[0], relay()[1])

    return finish


def _gather_rider(split, whole):
    n_split = len(split)
    return _Rider(lambda plain, stacks, ss, rs: _gather_ici(stacks, n_split, ss, rs), [], list(split) + list(whole),
                  relay=lambda stacks, ss, rs: _gather_d2d(stacks[:n_split], ss, rs), n_relay=3 * n_split)


def _chip_rider(sums, slots):
    return _Rider(_chip_copies, sums, slots)


def _start(makers):
    for make in makers:
        make().start()


def _finish(sends, arrivals):
    for make in arrivals:
        make().wait_recv()
    for make in sends:
        make().wait_send()


def _half(rows, c):
    return pl.ds(pl.multiple_of(c * rows, 16), rows)


def _gather_weights(split, whole):
    arrs = list(split) + list(whole)
    n, ns = len(arrs), len(split)

    def body(*refs):
        outs = refs[n:2 * n]
        send_sems, recv_sems, fsend_sems, frecv_sems = refs[2 * n:]
        sends, arrivals = _gather_ici(outs, ns, send_sems, recv_sems)
        passes, passed = _gather_d2d(outs[:ns], fsend_sems, frecv_sems)
        _start(sends)
        for k, make in enumerate(arrivals):
            make().wait_recv()
            if k < 3 * ns:
                passes[k]().start()
        _finish(sends + passes, passed)

    return pl.pallas_call(
        body, name="gather_weights",
        in_specs=[ANY] * n, out_specs=[ANY] * n,
        out_shape=[jax.ShapeDtypeStruct(a.shape, a.dtype) for a in arrs],
        input_output_aliases={i: i for i in range(n)},
        scratch_shapes=[pltpu.SemaphoreType.DMA((3 * n,)), pltpu.SemaphoreType.DMA((3 * n,)),
                        pltpu.SemaphoreType.DMA((3 * ns,)), pltpu.SemaphoreType.DMA((3 * ns,))],
    )(*arrs)


def _gather_ici(stacks, n_split, send_sems, recv_sems):
    x, y, c, me, chips = _place()

    def region(i, chip):
        if i < n_split:
            return stacks[i].at[chip, _half(stacks[i].shape[1] // 2, c)]
        return stacks[i].at[chip]

    pairs = []
    for i in range(len(stacks)):
        for p, (cx, cy) in enumerate(chips):
            k = 3 * i + p
            mine, got = region(i, me), region(i, 2 * cx + cy)
            sems, to = (send_sems.at[k], recv_sems.at[k]), (cx, cy, c)
            pairs.append(((mine, mine, *sems, to), (got, got, *sems, to)))
    return _copy_plan(pairs)


def _gather_d2d(stacks, send_sems, recv_sems):
    x, y, c, _, chips = _place()
    sibling = (x, y, 1 - c)
    pairs = []
    for i, stack in enumerate(stacks):
        rows = stack.shape[1] // 2
        for p, (cx, cy) in enumerate(chips):
            k = 3 * i + p
            got, theirs = stack.at[2 * cx + cy, _half(rows, c)], stack.at[2 * cx + cy, _half(rows, 1 - c)]
            sems = (send_sems.at[k], recv_sems.at[k])
            pairs.append(((got, got, *sems, sibling), (theirs, theirs, *sems, sibling)))
    return _copy_plan(pairs)


def _pair_exchange(grads, name):
    n = len(grads)

    def body(*refs):
        ins, theirs = refs[:n], refs[n:2 * n]
        send_sems, recv_sems = refs[2 * n:]
        x, y, c, _, _ = _place()
        sibling = (x, y, 1 - c)
        sends = [_remote(ins[k].at[:, _half(grads[k].shape[1] // 2, 1 - c)], theirs[k],
                         send_sems.at[k], recv_sems.at[k], sibling) for k in range(n)]
        for cp in sends:
            cp.start()
        for k in range(n):
            _remote(theirs[k], theirs[k], send_sems.at[k], recv_sems.at[k], sibling).wait_recv()
        for cp in sends:
            cp.wait_send()

    return pl.pallas_call(
        body, name=name,
        in_specs=[ANY] * n, out_specs=[ANY] * n,
        out_shape=[jax.ShapeDtypeStruct((g.shape[0], g.shape[1] // 2, g.shape[2]), g.dtype) for g in grads],
        scratch_shapes=[pltpu.SemaphoreType.DMA((n,))] * 2,
    )(*grads)


def _chip_exchange(sums, slots):
    n = len(sums)

    def body(*refs):
        sends, arrivals = _chip_copies(refs[:n], refs[2 * n:3 * n], *refs[3 * n:])
        _start(sends)
        _finish(sends, arrivals)

    return pl.pallas_call(
        body, name="grad_chip_exchange",
        in_specs=[ANY] * (2 * n), out_specs=[ANY] * n,
        out_shape=[jax.ShapeDtypeStruct(a.shape, a.dtype) for a in slots],
        input_output_aliases={n + k: k for k in range(n)},
        scratch_shapes=[pltpu.SemaphoreType.DMA((3 * n,)), pltpu.SemaphoreType.DMA((3 * n,))],
    )(*sums, *slots)


def _chip_copies(sums, slots, send_sems, recv_sems):
    x, y, c, me, chips = _place()
    pairs = []
    for k in range(len(sums)):
        for p, (cx, cy) in enumerate(chips):
            j = 3 * k + p
            got = slots[k].at[2 * cx + cy]
            sems, to = (send_sems.at[j], recv_sems.at[j]), (cx, cy, c)
            pairs.append(((sums[k].at[2 * cx + cy], slots[k].at[me], *sems, to), (got, got, *sems, to)))
    return _copy_plan(pairs)


def _half_swap(halves):
    n = len(halves)

    def body(*refs):
        outs = refs[n:2 * n]
        send_sems, recv_sems = refs[2 * n:]
        x, y, c, _, _ = _place()
        sibling = (x, y, 1 - c)
        sends = [_remote(outs[k].at[c], outs[k].at[c], send_sems.at[k], recv_sems.at[k], sibling) for k in range(n)]
        for cp in sends:
            cp.start()
        for k in range(n):
            got = outs[k].at[1 - c]
            _remote(got, got, send_sems.at[k], recv_sems.at[k], sibling).wait_recv()
        for cp in sends:
            cp.wait_send()

    return pl.pallas_call(
        body, name="grad_half_swap",
        in_specs=[ANY] * n, out_specs=[ANY] * n,
        out_shape=[jax.ShapeDtypeStruct(a.shape, a.dtype) for a in halves],
        input_output_aliases={k: k for k in range(n)},
        scratch_shapes=[pltpu.SemaphoreType.DMA((n,))] * 2,
    )(*halves)


def _small_rider(stack):
    n_dev = 2 * N_CHIPS

    def plan(_, stacks, send_sems, recv_sems):
        x, y, c, _, _ = _place()
        mine = stacks[0].at[4 * x + 2 * y + c]
        pairs = []
        for k in range(1, n_dev):
            px, py, pc = x ^ ((k >> 2) & 1), y ^ ((k >> 1) & 1), c ^ (k & 1)
            got = stacks[0].at[4 * px + 2 * py + pc]
            sems = (send_sems.at[k - 1], recv_sems.at[k - 1])
            pairs.append(((mine, mine, *sems, (px, py, pc)), (got, got, *sems, (px, py, pc))))
        return _copy_plan(pairs)

    return _Rider(plan, [], [stack], n_dev - 1)


def _row_tile(r):
    return r // 4 if r >= 256 and (r // 4) % 16 == 0 else r


def _prefetch_call(body, name, grid, in_specs, out_specs, out_shape):
    spec = pltpu.PrefetchScalarGridSpec(num_scalar_prefetch=1, grid=grid, in_specs=in_specs, out_specs=out_specs)
    return pl.pallas_call(body, name=name, grid_spec=spec, out_shape=out_shape,
                          compiler_params=_params(("arbitrary",) * len(grid)))


def _place_shard(w2d, where, dtype, name, by_device=False):
    r, c = w2d.shape
    tr = _row_tile(r)
    slots = 2 * N_CHIPS if by_device else N_CHIPS
    slot = (lambda s: 2 * s[1] + s[0]) if by_device else (lambda s: s[1])

    def body(where_ref, w_ref, out_ref):
        out_ref[...] = w_ref[...].astype(dtype)

    return _prefetch_call(
        body, name, (r // tr,), [pl.BlockSpec((tr, c), lambda i, s: (i, 0))],
        pl.BlockSpec((None, tr, c), lambda i, s: (slot(s), i, 0)),
        jax.ShapeDtypeStruct((slots, r, c), dtype))(where, w2d)


def _place_shards(w2ds, where, name):
    n = len(w2ds)
    steps = N_CHIPS
    assert all(w.shape[0] % (16 * steps) == 0 for w in w2ds)

    def body(where_ref, *refs):
        for k in range(n):
            refs[n + k][...] = refs[k][...].astype(BF16)

    tile = lambda w: (w.shape[0] // steps, w.shape[1])
    return _prefetch_call(
        body, name, (steps,), [pl.BlockSpec(tile(w), lambda i, s: (i, 0)) for w in w2ds],
        [pl.BlockSpec((None,) + tile(w), lambda i, s: (s[1], i, 0)) for w in w2ds],
        [jax.ShapeDtypeStruct((N_CHIPS,) + w.shape, BF16) for w in w2ds])(where, *w2ds)


def _pair_sum(fulls, theirs, where, name):
    n = len(fulls)

    def body(where_ref, *refs):
        for k in range(n):
            a_ref, b_ref, out_ref, own_ref = refs[k], refs[n + k], refs[2 * n + k], refs[3 * n + k]
            total = (a_ref[...].astype(F32) + b_ref[...].astype(F32)).astype(BF16)
            out_ref[...] = total

            @pl.when(pl.program_id(0) == where_ref[1])
            def _():
                own_ref[...] = total

    half = lambda t: pl.BlockSpec((None,) + t.shape[1:], lambda j, s: (j, s[0], 0))
    blk = lambda t: pl.BlockSpec((None,) + t.shape[1:], lambda j, s: (j, 0, 0))
    own = lambda t: pl.BlockSpec((None,) + t.shape[1:], lambda j, s: (s[1], 0, 0))
    shapes = [jax.ShapeDtypeStruct(t.shape, BF16) for t in theirs]
    outs = _prefetch_call(
        body, name, (N_CHIPS,), [half(t) for t in theirs] + [blk(t) for t in theirs],
        [blk(t) for t in theirs] + [own(t) for t in theirs], shapes + shapes)(where, *fulls, *theirs)
    return outs[:n], outs[n:]


def _chip_sum(slots, where, name):
    n = len(slots)
    steps = 2
    assert all(a.shape[1] % (16 * steps) == 0 for a in slots)

    def body(where_ref, *refs):
        for k in range(n):
            a_ref, out_ref = refs[k], refs[n + k]
            total = a_ref[0].astype(F32)
            for j in range(1, a_ref.shape[0]):
                total = total + a_ref[j].astype(F32)
            out_ref[...] = total

    tile = lambda a: (a.shape[1] // steps, a.shape[2])
    return _prefetch_call(
        body, name, (steps,), [pl.BlockSpec((a.shape[0],) + tile(a), lambda i, s: (0, i, 0)) for a in slots],
        [pl.BlockSpec((None,) + tile(a), lambda i, s: (s[0], i, 0)) for a in slots],
        [jax.ShapeDtypeStruct((2,) + a.shape[1:], F32) for a in slots])(where, *slots)


def _slot_sum(a, name):
    nb, r, c = a.shape
    tr = _row_tile(r)

    def body(a_ref, out_ref):
        total = a_ref[0].astype(F32)
        for j in range(1, nb):
            total = total + a_ref[j].astype(F32)
        out_ref[...] = total

    return pl.pallas_call(
        body, name=name, grid=(r // tr,),
        in_specs=[pl.BlockSpec((nb, tr, c), lambda i: (0, i, 0))],
        out_specs=pl.BlockSpec((tr, c), lambda i: (i, 0)),
        out_shape=jax.ShapeDtypeStruct((r, c), F32), compiler_params=_params(("arbitrary",)),
    )(a)


def _adamw(ws, gs, ms, vs, name, steps=1):
    n = len(ws)
    c1 = 1.0 - ADAM_B1 ** ADAM_STEP
    c2 = 1.0 - ADAM_B2 ** ADAM_STEP
    assert all(w.shape[0] % steps == 0 and (steps == 1 or w.shape[0] // steps % 8 == 0) for w in ws)

    def body(*refs):
        for k in range(n):
            w_ref, g_ref, m_ref, v_ref = (refs[j * n + k] for j in range(4))
            d_ref, m2_ref, v2_ref = (refs[(4 + j) * n + k] for j in range(3))
            gv = g_ref[...]
            m2 = ADAM_B1 * m_ref[...] + (1.0 - ADAM_B1) * gv
            v2 = ADAM_B2 * v_ref[...] + (1.0 - ADAM_B2) * (gv * gv)
            m2_ref[...] = m2
            v2_ref[...] = v2
            d_ref[...] = -ADAM_LR * ((m2 / c1) / (jnp.sqrt(v2 / c2) + ADAM_EPS) + ADAM_WD * w_ref[...])

    blks = [pl.BlockSpec((w.shape[0] // steps, w.shape[1]), lambda i: (i, 0)) for w in ws]
    shapes = [jax.ShapeDtypeStruct(w.shape, F32) for w in ws]
    outs = pl.pallas_call(
        body, name=name, grid=(steps,), in_specs=blks * 4, out_specs=blks * 3, out_shape=shapes * 3,
        compiler_params=_params(("arbitrary",)),
    )(*ws, *gs, *ms, *vs)
    return outs[:n], outs[n:2 * n], outs[2 * n:]


WEIGHTS = ["ffn1_norm", "ffn1_w_gate", "ffn1_w_up", "ffn1_w_down", "mix_norm", "w_in", "conv_w", "conv_b",
           "rg_w_a", "rg_b_a", "rg_w_x", "rg_b_x", "rg_lambda", "q_norm", "k_norm", "rnn_out_norm",
           "attn_out_norm", "w_out", "ffn2_norm", "ffn2_w_gate", "ffn2_w_up", "ffn2_w_down"]
BIG = ["ffn1_w_gate", "ffn1_w_up", "ffn1_w_down", "w_in", "w_out", "ffn2_w_gate", "ffn2_w_up", "ffn2_w_down"]
SMALL = [n for n in WEIGHTS if n not in BIG]
PACK_LANES = 128
PACK_ROW_ALIGN = 8


def _hidden_major(name, a):
    return jnp.transpose(a) if name.endswith(("w_gate", "w_up")) else a


def _pack(parts):
    flat = jnp.concatenate([p.reshape(-1) for p in parts])
    unit = PACK_LANES * PACK_ROW_ALIGN
    padded = -(-flat.shape[0] // unit) * unit
    return jnp.pad(flat, (0, padded - flat.shape[0])).reshape(-1, PACK_LANES)


def _unpack(packed, shapes):
    flat = packed.reshape(-1)
    out, at = [], 0
    for shp in shapes:
        size = math.prod(shp)
        out.append(flat[at:at + size].reshape(shp))
        at += size
    return out


def kernel(x, ffn1_norm, ffn1_w_gate, ffn1_w_up, ffn1_w_down, mix_norm, w_in, conv_w, conv_b, rg_w_a, rg_b_a, rg_w_x, rg_b_x, rg_lambda, q_norm, k_norm, rnn_out_norm, attn_out_norm, w_out, ffn2_norm, ffn2_w_gate, ffn2_w_up, ffn2_w_down, loss_target, m_ffn1_norm, m_ffn1_w_gate, m_ffn1_w_up, m_ffn1_w_down, m_mix_norm, m_w_in, m_conv_w, m_conv_b, m_rg_w_a, m_rg_b_a, m_rg_w_x, m_rg_b_x, m_rg_lambda, m_q_norm, m_k_norm, m_rnn_out_norm, m_attn_out_norm, m_w_out, m_ffn2_norm, m_ffn2_w_gate, m_ffn2_w_up, m_ffn2_w_down, v_ffn1_norm, v_ffn1_w_gate, v_ffn1_w_up, v_ffn1_w_down, v_mix_norm, v_w_in, v_conv_w, v_conv_b, v_rg_w_a, v_rg_b_a, v_rg_w_x, v_rg_b_x, v_rg_lambda, v_q_norm, v_k_norm, v_rnn_out_norm, v_attn_out_norm, v_w_out, v_ffn2_norm, v_ffn2_w_gate, v_ffn2_w_up, v_ffn2_w_down):
    given = dict(locals())
    w = {n: given[n] for n in WEIGHTS}
    m = {n: given["m_" + n] for n in WEIGHTS}
    v = {n: given["v_" + n] for n in WEIGHTS}
    chip = 2 * lax.axis_index("x") + lax.axis_index("y")

    where = jnp.stack([lax.axis_index("c"), chip]).astype(jnp.int32)

    stacks = dict(zip(BIG, _place_shards([_hidden_major(n, w[n][0]) for n in BIG], where, "place_weights")))
    conv_stack = _place_shard(w["conv_w"][0], where, F32, "place_conv_w")
    small = {n: (w[n][0] if w[n].ndim > 2 else w[n]) for n in SMALL if n != "conv_w"}

    grad_x, slots, gs, everyone = _local_step(x[0], loss_target[0], stacks, conv_stack, small, where)

    swapped = _half_swap(_chip_sum([slots[n] for n in BIG], where, "chip_sums"))
    g2s = [t.reshape(t.shape[0] * t.shape[1], t.shape[2]) for t in swapped]
    flat = lambda tree: [_hidden_major(n, tree[n][0]) for n in BIG]
    d2s, m2s, v2s = _adamw(flat(w), g2s, flat(m), flat(v), "adamw_weights", ADAMW_STEPS)
    grads, deltas, new_m, new_v = {}, {}, {}, {}
    for tree, parts in ((grads, g2s), (deltas, d2s), (new_m, m2s), (new_v, v2s)):
        tree.update({n: _hidden_major(n, a).reshape(w[n].shape) for n, a in zip(BIG, parts)})

    full_shapes = [gs[n].shape for n in SMALL]
    *summed, loss = _unpack(_slot_sum(everyone, "small_grad_sum"), full_shapes + [(1, 1)])
    g_parts = dict(zip(SMALL, summed))
    quarter = D_RNN // N_CHIPS
    g_parts["conv_w"] = lax.dynamic_slice_in_dim(g_parts["conv_w"], chip * quarter, quarter, axis=1)
    local_shapes = [w[n].shape for n in SMALL]
    pk = lambda tree: _pack([tree[n] for n in SMALL])
    (d_s,), (m_s,), (v_s,) = _adamw([pk(w)], [pk(g_parts)], [pk(m)], [pk(v)], "adamw_small")
    for tree, packed in ((grads, pk(g_parts)), (deltas, d_s), (new_m, m_s), (new_v, v_s)):
        tree.update(zip(SMALL, _unpack(packed, local_shapes)))

    return (loss[0, 0], grad_x.reshape(x.shape), *[grads[n] for n in WEIGHTS], *[deltas[n] for n in WEIGHTS],
            *[new_m[n] for n in WEIGHTS], *[new_v[n] for n in WEIGHTS])
```

```python
import functools
import math

import jax
import jax.numpy as jnp
from jax import lax
from jax.experimental import pallas as pl
from jax.experimental.pallas import tpu as pltpu

F32 = jnp.float32
BF16 = jnp.bfloat16
MESH = pl.DeviceIdType.MESH

D_MODEL = 1024
N_CHIPS = 4
D_RNN = 512
D_ATT = 512
N_HEADS = 8
HEAD_DIM = 64
RNN_BLOCKS = 8
CONV_W = 4
RG_C = 8.0
N_IN = 2 * D_RNN + 3 * D_ATT
EPS = 1e-6
ATT_BLOCK = 128
ATT_WINDOW = 384
ATT_SPLIT = 256
EXP_ZERO = -105.0

ADAM_LR = 0.001
ADAM_B1 = 0.9
ADAM_B2 = 0.999
ADAM_EPS = 1e-08
ADAM_WD = 0.01
ADAM_STEP = 10

V7X_VMEM_LIMIT = 56 * 1024 * 1024
V7X_MXU_WIDTH = 256
TOKEN_TILE = 512
SUBLANES = 8
FFN_TILE = 256
WGRAD_TILE = 2048
WHOLE_TILE = 1024
ADAMW_STEPS = 8

GELU_K0 = math.sqrt(2.0 / math.pi)
GELU_K1 = 0.044715


def _params(sem=None):
    return pltpu.CompilerParams(dimension_semantics=sem, vmem_limit_bytes=V7X_VMEM_LIMIT)


def _dot(a, b):
    return jnp.dot(a, b, preferred_element_type=F32)


def _dot_nt(a, b):
    return lax.dot_general(a, b, (((1,), (1,)), ((), ())), preferred_element_type=F32)


def _dot_tn(a, b):
    return lax.dot_general(a, b, (((0,), (0,)), ((), ())), preferred_element_type=F32)


def _sigmoid(x):
    return 1.0 / (1.0 + jnp.exp(-x))


def _rms_r(xv):
    return lax.rsqrt(jnp.mean(xv * xv, axis=-1, keepdims=True) + EPS)


def _rms_bwd(xv, r, nw, dh):
    t = dh * nw
    dx = r * t - xv * (r * r * r * jnp.mean(t * xv, axis=-1, keepdims=True))
    dn = jnp.sum(dh * xv * r, axis=0, keepdims=True)
    return dx, dn


def _gelu(x):
    t = jnp.tanh(GELU_K0 * (x + GELU_K1 * x * x * x))
    return 0.5 * x * (1.0 + t)


def _gelu_grad(x):
    t = jnp.tanh(GELU_K0 * (x + GELU_K1 * x * x * x))
    return 0.5 * (1.0 + t) + 0.5 * x * (1.0 - t * t) * (GELU_K0 * (1.0 + 3.0 * GELU_K1 * x * x))


def _expm1_neg(x):
    p = 1.0 + x * (1.0 / 6.0)
    for k in (5.0, 4.0, 3.0, 2.0):
        p = 1.0 + x * (1.0 / k) * p
    return jnp.where(x > -0.25, x * p, jnp.exp(x) - 1.0)


def _log_sigmoid(x):
    return jnp.minimum(x, 0.0) - jnp.log(1.0 + jnp.exp(-jnp.abs(x)))


def _tile(s):
    return min(TOKEN_TILE, s)


def _ffn_chunks(f):
    cut = f // 2 // V7X_MXU_WIDTH * V7X_MXU_WIDTH
    return ((0, cut), (cut, f)) if 0 < cut < f else ((0, f),)


def _ffn_fwd_loss(x, nw, wg, wu, wd, tgt):
    s, d = x.shape
    f = wg.shape[0]
    tm = min(FFN_TILE, s)
    ni = s // tm
    assert s % tm == 0

    def body(x_ref, nw_ref, wg_ref, wu_ref, wd_ref, tgt_ref, out_ref, g_ref, u_ref, hb_ref, ab_ref, loss_ref):
        i = pl.program_id(0)
        xv = x_ref[...]
        hb = (xv * _rms_r(xv) * nw_ref[...]).astype(BF16)
        hb_ref[...] = hb
        y = jnp.zeros((tm, d), F32)
        for lo, hi in _ffn_chunks(f):
            g = _dot_nt(hb, wg_ref[lo:hi, :])
            u = _dot_nt(hb, wu_ref[lo:hi, :])
            g_ref[:, lo:hi] = g.astype(BF16)
            u_ref[:, lo:hi] = u.astype(BF16)
            ab = (g * _sigmoid(g) * u).astype(BF16)
            ab_ref[:, lo:hi] = ab
            y = y + _dot(ab, wd_ref[lo:hi, :])
        diff = xv + 0.5 * y - tgt_ref[...]
        out_ref[...] = diff * (1.0 / d)

        @pl.when(i == 0)
        def _():
            loss_ref[...] = jnp.zeros_like(loss_ref)

        loss_ref[...] += jnp.sum(diff * diff) * (0.5 / d)

    row = pl.BlockSpec((tm, d), lambda i: (i, 0))
    weight = pl.BlockSpec((f, d), lambda i: (0, 0), pipeline_mode=pl.Buffered(1))
    blk = pl.BlockSpec((tm, f), lambda i: (i, 0))
    wide = jax.ShapeDtypeStruct((s, f), BF16)
    return _call(body, "ffn_fwd_loss", (ni,),
                 [row, pl.BlockSpec((1, d), lambda i: (0, 0)), weight, weight, weight, row],
                 [row, blk, blk, row, blk, pl.BlockSpec((1, 128), lambda i: (0, 0))],
                 [jax.ShapeDtypeStruct((s, d), F32), wide, wide, jax.ShapeDtypeStruct((s, d), BF16), wide,
                  jax.ShapeDtypeStruct((1, 128), F32)], [x, nw, wg, wu, wd, tgt])


def _ffn_up(x, nw, wg, wu, rider=None):
    s, d = x.shape
    f = wg.shape[0]
    tm = min(FFN_TILE, s)
    ni = s // tm
    assert s % tm == 0

    def body(*refs):
        (x_ref, nw_ref, wg_ref, wu_ref), (g_ref, u_ref, hb_ref, ab_ref), _, copies = _split_refs(refs, 4, 4, rider)
        i = pl.program_id(0)
        finish = _ride(copies, i == 0, i == ni - 1)
        xv = x_ref[...]
        hb = (xv * _rms_r(xv) * nw_ref[...]).astype(BF16)
        hb_ref[...] = hb
        for lo, hi in _ffn_chunks(f):
            g = _dot_nt(hb, wg_ref[lo:hi, :])
            u = _dot_nt(hb, wu_ref[lo:hi, :])
            g_ref[:, lo:hi] = g.astype(BF16)
            u_ref[:, lo:hi] = u.astype(BF16)
            ab_ref[:, lo:hi] = (g * _sigmoid(g) * u).astype(BF16)
        finish()

    row = pl.BlockSpec((tm, d), lambda i: (i, 0))
    weight = pl.BlockSpec((f, d), lambda i: (0, 0), pipeline_mode=pl.Buffered(1))
    blk = pl.BlockSpec((tm, f), lambda i: (i, 0))
    wide = jax.ShapeDtypeStruct((s, f), BF16)
    return _call(body, "ffn_up", (ni,), [row, pl.BlockSpec((1, d), lambda i: (0, 0)), weight, weight],
                 [blk, blk, row, blk], [wide, wide, jax.ShapeDtypeStruct((s, d), BF16), wide], [x, nw, wg, wu],
                 rider=rider)


def _ffn_down(x, ab, wd):
    s, d = x.shape
    f = wd.shape[0]
    tm = min(FFN_TILE, s)
    assert s % tm == 0

    def body(x_ref, ab_ref, wd_ref, out_ref):
        out_ref[...] = x_ref[...] + 0.5 * _dot(ab_ref[...], wd_ref[...])

    row = pl.BlockSpec((tm, d), lambda i: (i, 0))
    return _call(body, "ffn_down", (s // tm,),
                 [row, pl.BlockSpec((tm, f), lambda i: (i, 0)),
                  pl.BlockSpec((f, d), lambda i: (0, 0), pipeline_mode=pl.Buffered(1))],
                 [row], [jax.ShapeDtypeStruct((s, d), F32)], [x, ab, wd])[0]


def _call(body, name, grid, in_specs, out_specs, out_shape, args, scratch=(), rider=None):
    in_specs, out_specs, out_shape, scratch = list(in_specs), list(out_specs), list(out_shape), list(scratch)
    extra, aliases = [], {}
    if rider is not None:
        extra = rider.operands()
        aliases = rider.aliases(len(args), len(out_shape))
        in_specs += [ANY] * len(extra)
        out_specs += [ANY] * len(rider.inplace)
        out_shape += rider.out_shape()
        scratch += rider.scratch()
    return pl.pallas_call(
        body, name=name, grid=grid, in_specs=in_specs, out_specs=out_specs, out_shape=out_shape,
        input_output_aliases=aliases, scratch_shapes=scratch,
        compiler_params=_params(("arbitrary",) * len(grid)),
    )(*args, *extra)


def _ffn_bwd_act(x, nw, dy, g, u, wg, wu, wd, name):
    s, d = x.shape
    f = wg.shape[0]
    tm = min(FFN_TILE, s)
    assert s % tm == 0

    def body(x_ref, nw_ref, dy_ref, g_ref, u_ref, wg_ref, wu_ref, wd_ref,
             dx_ref, dg_ref, du_ref, dyb_ref, dnw_ref):
        dyv = dy_ref[...]
        dyb = dyv.astype(BF16)
        dyb_ref[...] = dyb
        dh = jnp.zeros((tm, d), F32)
        for lo, hi in _ffn_chunks(f):
            da = 0.5 * _dot_nt(dyb, wd_ref[lo:hi, :])
            gv = g_ref[:, lo:hi].astype(F32)
            sg = _sigmoid(gv)
            dub = (da * (gv * sg)).astype(BF16)
            dgb = (da * u_ref[:, lo:hi].astype(F32) * (sg * (1.0 + gv * (1.0 - sg)))).astype(BF16)
            dg_ref[:, lo:hi] = dgb
            du_ref[:, lo:hi] = dub
            dh = dh + _dot(dgb, wg_ref[lo:hi, :]) + _dot(dub, wu_ref[lo:hi, :])
        xv = x_ref[...]
        dx, dn = _rms_bwd(xv, _rms_r(xv), nw_ref[...], dh)
        dx_ref[...] = dyv + dx

        @pl.when(pl.program_id(0) == 0)
        def _():
            dnw_ref[...] = jnp.zeros_like(dnw_ref)

        dnw_ref[...] += dn

    row = pl.BlockSpec((tm, d), lambda i: (i, 0))
    vec = pl.BlockSpec((1, d), lambda i: (0, 0))
    blk = pl.BlockSpec((tm, f), lambda i: (i, 0))
    weight = pl.BlockSpec((f, d), lambda i: (0, 0), pipeline_mode=pl.Buffered(1))
    return _call(
        body, name, (s // tm,), [row, vec, row, blk, blk, weight, weight, weight], [row, blk, blk, row, vec],
        [jax.ShapeDtypeStruct((s, d), F32), jax.ShapeDtypeStruct((s, f), BF16),
         jax.ShapeDtypeStruct((s, f), BF16), jax.ShapeDtypeStruct((s, d), BF16),
         jax.ShapeDtypeStruct((1, d), F32)],
        [x, nw, dy, g, u, wg, wu, wd])


def _wgrad(a, b, a_spec, b_spec, out_rows, out_cols, scale, name, tk, rider=None, per_step=1):
    s = a.shape[-2]
    nk = s // tk
    steps = N_CHIPS // per_step
    assert s % tk == 0

    def body(*refs):
        (a_ref, b_ref), (out_ref,), (acc,), copies = _split_refs(refs, 2, 1, rider)
        j, k = pl.program_id(0), pl.program_id(1)
        finish = _ride(copies, jnp.logical_and(j == 0, k == 0), jnp.logical_and(j == steps - 1, k == nk - 1))

        @pl.when(k == 0)
        def _():
            acc[...] = jnp.zeros_like(acc)

        acc[...] += _dot_tn(a_ref[...], b_ref[...])

        @pl.when(k == nk - 1)
        def _():
            for t in range(per_step):
                out_ref[t] = (acc[t * out_rows:(t + 1) * out_rows, :] * scale).astype(BF16)

        finish()

    outs = _call(
        body, name, (steps, nk), [a_spec(tk), b_spec(tk)],
        [pl.BlockSpec((per_step, out_rows, out_cols), lambda j, k: (j, 0, 0))],
        [jax.ShapeDtypeStruct((N_CHIPS, out_rows, out_cols), BF16)], [a, b],
        scratch=[pltpu.VMEM((per_step * out_rows, out_cols), F32)], rider=rider)
    return outs[0] if rider is None else outs


def _wgrad_whole(a, b, col_blocks, name, rider=None):
    s, m = a.shape
    n = b.shape[1]
    tk = min(WHOLE_TILE, s)
    nk = s // tk
    assert s % tk == 0
    out_shape = (N_CHIPS, m, n // N_CHIPS) if col_blocks else (N_CHIPS, m // N_CHIPS, n)

    def body(*refs):
        (a_ref, b_ref), (out_ref,), (acc,), copies = _split_refs(refs, 2, 1, rider)
        k = pl.program_id(0)
        finish = _ride(copies, k == 0, k == nk - 1)

        @pl.when(k == 0)
        def _():
            acc[...] = jnp.zeros_like(acc)

        acc[...] += _dot_tn(a_ref[...], b_ref[...])

        @pl.when(k == nk - 1)
        def _():
            for j in range(N_CHIPS):
                if col_blocks:
                    out_ref[j] = acc[:, j * out_shape[2]:(j + 1) * out_shape[2]].astype(BF16)
                else:
                    out_ref[j] = acc[j * out_shape[1]:(j + 1) * out_shape[1], :].astype(BF16)

        finish()

    outs = _call(
        body, name, (nk,), [pl.BlockSpec((tk, m), lambda k: (k, 0)), pl.BlockSpec((tk, n), lambda k: (k, 0))],
        [pl.BlockSpec(out_shape, lambda k: (0, 0, 0))], [jax.ShapeDtypeStruct(out_shape, BF16)], [a, b],
        scratch=[pltpu.VMEM((m, n), F32)], rider=rider)
    return outs[0] if rider is None else outs


def _ffn_wgrad(hidden, shared, scale, name, rider=None):
    s, d = shared.shape
    half = hidden.shape[1] // 2
    return _wgrad(hidden, shared, lambda tk: pl.BlockSpec((tk, half), lambda j, k: (k, j)),
                  lambda tk: pl.BlockSpec((tk, d), lambda j, k: (k, 0)), half // 2, d, scale, name,
                  min(WGRAD_TILE, s), rider, per_step=2)


def _mix_pre(x, nw, win):
    s, d = x.shape
    nb, _, cb = win.shape
    tm = _tile(s)
    assert s % tm == 0

    def body(x_ref, nw_ref, w_ref, p_ref, hb_ref):
        xv = x_ref[...]
        hb = (xv * _rms_r(xv) * nw_ref[...]).astype(BF16)
        hb_ref[...] = hb
        for j in range(nb):
            p_ref[:, j * cb:(j + 1) * cb] = _dot(hb, w_ref[j])

    row = pl.BlockSpec((tm, d), lambda i: (i, 0))
    return pl.pallas_call(
        body, name="mix_pre", grid=(s // tm,),
        in_specs=[row, pl.BlockSpec((1, d), lambda i: (0, 0)),
                  pl.BlockSpec((nb, d, cb), lambda i: (0, 0, 0), pipeline_mode=pl.Buffered(1))],
        out_specs=[pl.BlockSpec((tm, nb * cb), lambda i: (i, 0)), row],
        out_shape=[jax.ShapeDtypeStruct((s, nb * cb), F32), jax.ShapeDtypeStruct((s, d), BF16)],
        compiler_params=_params(("arbitrary",)),
    )(x, nw, win)


def _mix_pre_bwd(x, nw, dres, dpb, win):
    s, d = x.shape
    nb, _, cb = win.shape
    tm = _tile(s)
    assert s % tm == 0

    def body(x_ref, nw_ref, dres_ref, dp_ref, w_ref, dx_ref, dnw_ref):
        dh = jnp.zeros((tm, d), F32)
        for j in range(nb):
            dh = dh + _dot_nt(dp_ref[:, j * cb:(j + 1) * cb], w_ref[j])
        xv = x_ref[...]
        dx, dn = _rms_bwd(xv, _rms_r(xv), nw_ref[...], dh)
        dx_ref[...] = dres_ref[...] + dx

        @pl.when(pl.program_id(0) == 0)
        def _():
            dnw_ref[...] = jnp.zeros_like(dnw_ref)

        dnw_ref[...] += dn

    row = pl.BlockSpec((tm, d), lambda i: (i, 0))
    vec = pl.BlockSpec((1, d), lambda i: (0, 0))
    return pl.pallas_call(
        body, name="mix_pre_bwd", grid=(s // tm,),
        in_specs=[row, vec, row, pl.BlockSpec((tm, nb * cb), lambda i: (i, 0)),
                  pl.BlockSpec((nb, d, cb), lambda i: (0, 0, 0), pipeline_mode=pl.Buffered(1))],
        out_specs=[row, vec],
        out_shape=[jax.ShapeDtypeStruct((s, d), F32), jax.ShapeDtypeStruct((1, d), F32)],
        compiler_params=_params(("arbitrary",)),
    )(x, nw, dres, dpb, win)


def _mix_post(x, yr, ya, nr, na, wout):
    s, d = x.shape
    h = yr.shape[1]
    tm = _tile(s)

    def body(x_ref, yr_ref, ya_ref, nr_ref, na_ref, w_ref, out_ref):
        yrv = yr_ref[...]
        yav = ya_ref[...]
        onb = (yrv * _rms_r(yrv) * nr_ref[...]).astype(BF16)
        oab = (yav * _rms_r(yav) * na_ref[...]).astype(BF16)
        out_ref[...] = x_ref[...] + _dot(onb, w_ref[0:h, :]) + _dot(oab, w_ref[h:2 * h, :])

    row = pl.BlockSpec((tm, d), lambda i: (i, 0))
    half = pl.BlockSpec((tm, h), lambda i: (i, 0))
    vec = pl.BlockSpec((1, h), lambda i: (0, 0))
    return pl.pallas_call(
        body, name="mix_post", grid=(s // tm,),
        in_specs=[row, half, half, vec, vec, pl.BlockSpec((2 * h, d), lambda i: (0, 0))],
        out_specs=row, out_shape=jax.ShapeDtypeStruct((s, d), F32),
        compiler_params=_params(("arbitrary",)),
    )(x, yr, ya, nr, na, wout)


def _mix_post_bwd(dx, yr, ya, nr, na, wout):
    s, d = dx.shape
    h = yr.shape[1]
    tm = _tile(s)

    def body(dx_ref, yr_ref, ya_ref, nr_ref, na_ref, w_ref,
             dyr_ref, dya_ref, yc_ref, dxb_ref, dnr_ref, dna_ref):
        i = pl.program_id(0)
        dxb = dx_ref[...].astype(BF16)
        dxb_ref[...] = dxb
        dyc = _dot_nt(dxb, w_ref[...])
        yrv = yr_ref[...]
        yav = ya_ref[...]
        rr = _rms_r(yrv)
        ra = _rms_r(yav)
        yc_ref[:, 0:h] = (yrv * rr * nr_ref[...]).astype(BF16)
        yc_ref[:, h:2 * h] = (yav * ra * na_ref[...]).astype(BF16)
        dyr, dnr = _rms_bwd(yrv, rr, nr_ref[...], dyc[:, 0:h])
        dya, dna = _rms_bwd(yav, ra, na_ref[...], dyc[:, h:2 * h])
        dyr_ref[...] = dyr
        dya_ref[...] = dya

        @pl.when(i == 0)
        def _():
            dnr_ref[...] = jnp.zeros_like(dnr_ref)
            dna_ref[...] = jnp.zeros_like(dna_ref)

        dnr_ref[...] += dnr
        dna_ref[...] += dna

    row = pl.BlockSpec((tm, d), lambda i: (i, 0))
    half = pl.BlockSpec((tm, h), lambda i: (i, 0))
    vec = pl.BlockSpec((1, h), lambda i: (0, 0))
    return pl.pallas_call(
        body, name="mix_post_bwd", grid=(s // tm,),
        in_specs=[row, half, half, vec, vec, pl.BlockSpec((2 * h, d), lambda i: (0, 0))],
        out_specs=[half, half, pl.BlockSpec((tm, 2 * h), lambda i: (i, 0)), row, vec, vec],
        out_shape=[jax.ShapeDtypeStruct((s, h), F32), jax.ShapeDtypeStruct((s, h), F32),
                   jax.ShapeDtypeStruct((s, 2 * h), BF16), jax.ShapeDtypeStruct((s, d), BF16),
                   jax.ShapeDtypeStruct((1, h), F32), jax.ShapeDtypeStruct((1, h), F32)],
        compiler_params=_params(("arbitrary",)),
    )(dx, yr, ya, nr, na, wout)


def _shift_down(xv, s, prev8):
    rolled = pltpu.roll(xv, s, 0)
    row8 = lax.broadcasted_iota(jnp.int32, prev8.shape, 0)
    head = jnp.where(row8 < s, pltpu.roll(prev8, s, 0), rolled[0:8, :])
    return jnp.concatenate([head, rolled[8:, :]], axis=0)


def _shift_up(xv, s, next8):
    n = xv.shape[0]
    rolled = pltpu.roll(xv, n - s, 0)
    row8 = lax.broadcasted_iota(jnp.int32, next8.shape, 0)
    tail = jnp.where(row8 >= 8 - s, pltpu.roll(next8, 8 - s, 0), rolled[n - 8:, :])
    return jnp.concatenate([rolled[:n - 8, :], tail], axis=0)


def _scan_fwd(a, b):
    n = a.shape[0]
    sub = lax.broadcasted_iota(jnp.int32, a.shape, 0) % SUBLANES
    s = 1
    while s < SUBLANES:
        ok = sub >= s
        b = jnp.where(ok, a * pltpu.roll(b, s, 0) + b, b)
        a = jnp.where(ok, a * pltpu.roll(a, s, 0), a)
        s *= 2
    groups = []
    before = jnp.zeros((1, a.shape[1]), F32)
    for g in range(n // SUBLANES):
        rows = slice(g * SUBLANES, (g + 1) * SUBLANES)
        groups.append(a[rows] * before + b[rows])
        before = groups[-1][SUBLANES - 1:]
    return jnp.concatenate(groups, axis=0)


def _scan_bwd(a, b):
    n = a.shape[0]
    sub = lax.broadcasted_iota(jnp.int32, a.shape, 0) % SUBLANES
    s = 1
    while s < SUBLANES:
        ok = sub < SUBLANES - s
        b = jnp.where(ok, a * pltpu.roll(b, n - s, 0) + b, b)
        a = jnp.where(ok, a * pltpu.roll(a, n - s, 0), a)
        s *= 2
    groups = []
    after = jnp.zeros((1, a.shape[1]), F32)
    for g in reversed(range(n // SUBLANES)):
        rows = slice(g * SUBLANES, (g + 1) * SUBLANES)
        groups.append(a[rows] * after + b[rows])
        after = groups[-1][:1]
    return jnp.concatenate(groups[::-1], axis=0)


def _rglru_gates(xv, prev8, cw_ref, cb_ref, wa_ref, ba_ref, wx_ref, bx_ref, lam_ref):
    x1 = _shift_down(xv, 1, prev8)
    x2 = _shift_down(xv, 2, prev8)
    x3 = _shift_down(xv, 3, prev8)
    xc = cw_ref[3:4, :] * xv + cw_ref[2:3, :] * x1 + cw_ref[1:2, :] * x2 + cw_ref[0:1, :] * x3 + cb_ref[...]
    xcb = xc.astype(BF16)
    r = _sigmoid(_dot(xcb, wa_ref[...]) + ba_ref[...])
    ig = _sigmoid(_dot(xcb, wx_ref[...]) + bx_ref[...])
    c = RG_C * _log_sigmoid(lam_ref[...])
    la = r * c
    a = jnp.exp(la)
    m = jnp.sqrt(-_expm1_neg(2.0 * la))
    return (x1, x2, x3), xc, xcb, r, ig, c, a, m


def _rglru_fwd(proj, cw, cb, wa, ba, wx, bx, lam):
    s = proj.shape[0]
    w = D_RNN
    tm = _tile(s)

    def body(xr_ref, gate_ref, cw_ref, cb_ref, wa_ref, ba_ref, wx_ref, bx_ref, lam_ref,
             y_ref, h_ref, prev, hlast):
        @pl.when(pl.program_id(0) == 0)
        def _():
            prev[...] = jnp.zeros_like(prev)
            hlast[...] = jnp.zeros_like(hlast)

        xv = xr_ref[...]
        _, xc, _, _, ig, _, a, m = _rglru_gates(xv, prev[...], cw_ref, cb_ref, wa_ref, ba_ref,
                                                wx_ref, bx_ref, lam_ref)
        b = m * (ig * xc)
        row = lax.broadcasted_iota(jnp.int32, b.shape, 0)
        b = jnp.where(row == 0, b + a * hlast[...], b)
        h = _scan_fwd(a, b)
        h_ref[...] = h
        y_ref[...] = h * _gelu(gate_ref[...])
        prev[...] = xv[tm - 8:, :]
        hlast[...] = h[tm - 1:tm, :]

    vec = pl.BlockSpec((1, w), lambda i: (0, 0))
    sq = pl.BlockSpec((w, w), lambda i: (0, 0))
    out = pl.BlockSpec((tm, w), lambda i: (i, 0))
    return pl.pallas_call(
        body, name="rglru_fwd", grid=(s // tm,),
        in_specs=[pl.BlockSpec((tm, w), lambda i: (i, 0)), pl.BlockSpec((tm, w), lambda i: (i, 1)),
                  pl.BlockSpec((CONV_W, w), lambda i: (0, 0)), vec, sq, vec, sq, vec, vec],
        out_specs=[out, out],
        out_shape=[jax.ShapeDtypeStruct((s, w), F32), jax.ShapeDtypeStruct((s, w), F32)],
        scratch_shapes=[pltpu.VMEM((8, w), F32), pltpu.VMEM((1, w), F32)],
        compiler_params=_params(("arbitrary",)),
    )(proj, proj, cw, cb, wa, ba, wx, bx, lam)


def _rglru_bwd(proj, hseq, dyr, cw, cb, wa, ba, wx, bx, lam):
    s = proj.shape[0]
    w = D_RNN
    tm = _tile(s)
    nt = s // tm
    t8 = tm // 8

    def body(xr_ref, xp_ref, gate_ref, h_ref, hp_ref, dy_ref, cw_ref, cb_ref, wa_ref, ba_ref,
             wx_ref, bx_ref, lam_ref,
             dxr_ref, dgate_ref, dcw_ref, dcb_ref, dwa_ref, dba_ref, dwx_ref, dbx_ref, dlam_ref,
             carry, dxc_next):
        i = pl.program_id(0)
        first_tile = i == nt - 1

        @pl.when(i == 0)
        def _():
            carry[...] = jnp.zeros_like(carry)
            dxc_next[...] = jnp.zeros_like(dxc_next)
            for ref in (dcw_ref, dcb_ref, dwa_ref, dba_ref, dwx_ref, dbx_ref, dlam_ref):
                ref[...] = jnp.zeros_like(ref)

        xv = xr_ref[...]
        prev8 = jnp.where(first_tile, 0.0, xp_ref[...])
        hprev8 = jnp.where(first_tile, 0.0, hp_ref[...])
        (x1, x2, x3), xc, xcb, r, ig, c, a, m = _rglru_gates(
            xv, prev8, cw_ref, cb_ref, wa_ref, ba_ref, wx_ref, bx_ref, lam_ref)
        gv = gate_ref[...]
        hv = h_ref[...]
        dy = dy_ref[...]
        dgate_ref[...] = (dy * hv * _gelu_grad(gv)).astype(BF16)
        dh = dy * _gelu(gv)
        row = lax.broadcasted_iota(jnp.int32, dh.shape, 0)
        dh = jnp.where(row == tm - 1, dh + carry[...], dh)
        a_up = jnp.where(row == tm - 1, 0.0, pltpu.roll(a, tm - 1, 0))
        lam_t = _scan_bwd(a_up, dh)
        carry[...] = a[0:1, :] * lam_t[0:1, :]
        hm1 = _shift_down(hv, 1, hprev8)
        da = lam_t * hm1
        ixc = ig * xc
        dm = lam_t * ixc
        dig = lam_t * m * xc
        dxc = lam_t * m * ig
        dla = da * a - dm * (a * a) / m
        dr = dla * c
        dlam_ref[...] += jnp.sum(dla * r, axis=0, keepdims=True)
        dpa = dr * r * (1.0 - r)
        dpi = dig * ig * (1.0 - ig)
        dba_ref[...] += jnp.sum(dpa, axis=0, keepdims=True)
        dbx_ref[...] += jnp.sum(dpi, axis=0, keepdims=True)
        dpab = dpa.astype(BF16)
        dpib = dpi.astype(BF16)
        dwa_ref[...] += _dot_tn(xcb, dpab)
        dwx_ref[...] += _dot_tn(xcb, dpib)
        dxc = dxc + _dot_nt(dpab, wa_ref[...]) + _dot_nt(dpib, wx_ref[...])
        dcb_ref[...] += jnp.sum(dxc, axis=0, keepdims=True)
        dcw_ref[3:4, :] += jnp.sum(dxc * xv, axis=0, keepdims=True)
        dcw_ref[2:3, :] += jnp.sum(dxc * x1, axis=0, keepdims=True)
        dcw_ref[1:2, :] += jnp.sum(dxc * x2, axis=0, keepdims=True)
        dcw_ref[0:1, :] += jnp.sum(dxc * x3, axis=0, keepdims=True)
        nxt = dxc_next[...]
        dxr = (cw_ref[3:4, :] * dxc + cw_ref[2:3, :] * _shift_up(dxc, 1, nxt)
               + cw_ref[1:2, :] * _shift_up(dxc, 2, nxt) + cw_ref[0:1, :] * _shift_up(dxc, 3, nxt))
        dxr_ref[...] = dxr.astype(BF16)
        dxc_next[...] = dxc[0:8, :]

        @pl.when(first_tile)
        def _():
            lv = lam_ref[...]
            dlam_ref[...] = dlam_ref[...] * (RG_C * _sigmoid(-lv))

    rev = lambda i: nt - 1 - i
    vec = pl.BlockSpec((1, w), lambda i: (0, 0))
    sq = pl.BlockSpec((w, w), lambda i: (0, 0))
    cur = lambda col: pl.BlockSpec((tm, w), lambda i: (rev(i), col))
    before = lambda cols: pl.BlockSpec((8, w), lambda i: (jnp.maximum(rev(i) * t8 - 1, 0), 0))
    return pl.pallas_call(
        body, name="rglru_bwd", grid=(nt,),
        in_specs=[cur(0), before(None), cur(1), cur(0), before(None), cur(0),
                  pl.BlockSpec((CONV_W, w), lambda i: (0, 0)), vec, sq, vec, sq, vec, vec],
        out_specs=[cur(0), cur(0), pl.BlockSpec((CONV_W, w), lambda i: (0, 0)), vec, sq, vec, sq, vec, vec],
        out_shape=[jax.ShapeDtypeStruct((s, w), BF16), jax.ShapeDtypeStruct((s, w), BF16),
                   jax.ShapeDtypeStruct((CONV_W, w), F32), jax.ShapeDtypeStruct((1, w), F32),
                   jax.ShapeDtypeStruct((w, w), F32), jax.ShapeDtypeStruct((1, w), F32),
                   jax.ShapeDtypeStruct((w, w), F32), jax.ShapeDtypeStruct((1, w), F32),
                   jax.ShapeDtypeStruct((1, w), F32)],
        scratch_shapes=[pltpu.VMEM((1, w), F32), pltpu.VMEM((8, w), F32)],
        compiler_params=_params(("arbitrary",)),
    )(proj, proj, proj, hseq, hseq, dyr, cw, cb, wa, ba, wx, bx, lam)


def _sb_logs(z, valid):
    l1p = jnp.log(1.0 + jnp.exp(-jnp.abs(z)))
    lb = jnp.minimum(z, 0.0) - l1p
    lm = jnp.where(valid, -jnp.maximum(z, 0.0) - l1p, 0.0)
    return lb, lm


class _Window:
    def __init__(self):
        blk, win, cut = ATT_BLOCK, ATT_WINDOW, ATT_SPLIT
        self.row = lax.broadcasted_iota(jnp.int32, (blk, win), 0)
        self.col = lax.broadcasted_iota(jnp.int32, (blk, win), 1)

        def tri(n, later):
            j = lax.broadcasted_iota(jnp.int32, (n, n), 0)
            s = lax.broadcasted_iota(jnp.int32, (n, n), 1)
            return jnp.where((j > s) if later else (j < s), 1.0, 0.0).astype(BF16)

        self.later = (tri(cut, True), tri(win - cut, True))
        self.earlier = (tri(cut, False), tri(win - cut, False))

    def place(self, qi, g):
        end = (qi + 1) * ATT_BLOCK - g * ATT_WINDOW
        start = pl.multiple_of(jnp.maximum(end - ATT_WINDOW, 0), ATT_BLOCK)
        valid = start + self.col < jnp.minimum(qi * ATT_BLOCK + self.row, end)
        return start, valid

    @staticmethod
    def _parts(xv):
        hi = xv.astype(BF16)
        lo = (xv - hi.astype(F32)).astype(BF16)
        cut = ATT_SPLIT
        sums = (jnp.sum(xv[:, :cut], axis=1, keepdims=True), jnp.sum(xv[:, cut:], axis=1, keepdims=True))
        return (hi[:, :cut], lo[:, :cut]), (hi[:, cut:], lo[:, cut:]), sums

    def sums_after(self, xv, carry):
        (h0, l0), (h1, l1), (s0, s1) = self._parts(xv)
        first = _dot(h0, self.later[0]) + _dot(l0, self.later[0]) + (s1 + carry)
        last = _dot(h1, self.later[1]) + _dot(l1, self.later[1]) + carry
        return jnp.concatenate([first, last], axis=1), s0 + s1

    def sums_before(self, xv, carry):
        (h0, l0), (h1, l1), (s0, s1) = self._parts(xv)
        first = _dot(h0, self.earlier[0]) + _dot(l0, self.earlier[0]) + carry
        last = _dot(h1, self.earlier[1]) + _dot(l1, self.earlier[1]) + (s0 + carry)
        return jnp.concatenate([first, last], axis=1), s0 + s1


class _HeadPair:
    def __init__(self):
        lanes = 2 * HEAD_DIM
        lane = lax.broadcasted_iota(jnp.int32, (1, lanes), 1)
        self.masks = [lane // HEAD_DIM == h for h in (0, 1)]
        i = lax.broadcasted_iota(jnp.int32, (lanes, lanes), 0) // HEAD_DIM
        j = lax.broadcasted_iota(jnp.int32, (lanes, lanes), 1) // HEAD_DIM
        self.same_head = jnp.where(i == j, 1.0, 0.0).astype(BF16)

    def only(self, h, xv):
        return jnp.where(self.masks[h], xv, jnp.zeros_like(xv))

    def merge(self, per_head):
        return jnp.where(self.masks[0], per_head[0], per_head[1])

    def mean(self, xv):
        hi = xv.astype(BF16)
        lo = (xv - hi.astype(F32)).astype(BF16)
        return (_dot(hi, self.same_head) + _dot(lo, self.same_head)) * (1.0 / HEAD_DIM)

    def rms_r(self, xv):
        return lax.rsqrt(self.mean(xv * xv) + EPS)

    def rms_bwd(self, xv, r, nw, dh):
        t = dh * nw
        dx = r * t - xv * (r * r * r * self.mean(t * xv))
        dn = jnp.sum(dh * xv * r, axis=0, keepdims=True)
        return dx, dn[:, :HEAD_DIM] + dn[:, HEAD_DIM:]


def _attn_fwd(proj, qg, kg, rider=None):
    s = proj.shape[0]
    blk, win, dh = ATT_BLOCK, ATT_WINDOW, HEAD_DIM
    nq = s // blk
    scale = 1.0 / math.sqrt(dh)
    heads = (0, 1)
    assert s >= win and s % blk == 0

    def body(*refs):
        (q_ref, k_ref, v_ref, qg_ref, kg_ref), (o_ref,), (qn, kn, vb), copies = _split_refs(refs, 5, 1, rider)
        finish = _ride(copies, pl.program_id(0) == 0, pl.program_id(0) == N_HEADS // 2 - 1)
        wd, hp = _Window(), _HeadPair()
        qv = q_ref[...]
        qn[...] = (qv * hp.rms_r(qv) * qg_ref[...] * scale).astype(BF16)
        kv = k_ref[...]
        kn[...] = (kv * hp.rms_r(kv) * kg_ref[...]).astype(BF16)
        vb[...] = v_ref[...].astype(BF16)

        def q_step(qi, _):
            qoff = pl.multiple_of(qi * blk, blk)
            qt = qn[pl.ds(qoff, blk), :]
            qts = [hp.only(h, qt) for h in heads]

            def more(carry):
                g, live = carry[:2]
                return jnp.logical_and((qi + 1) * blk - g * win > 0, live > 0)

            def window(carry):
                g, _, accs, runs = carry
                start, valid = wd.place(qi, g)
                kt = kn[pl.ds(start, win), :]
                zs = [_dot_nt(qts[h], kt) for h in heads]
                logs = [_sb_logs(z, valid) for z in zs]
                sums = [wd.sums_after(logs[h][1], runs[h]) for h in heads]
                wgts = [jnp.where(valid, jnp.exp(logs[h][0] + sums[h][0]), 0.0).astype(BF16) for h in heads]
                vt = vb[pl.ds(start, win), :]
                accs = tuple(accs[h] + _dot(wgts[h], vt) for h in heads)
                runs = tuple(runs[h] + sums[h][1] for h in heads)
                live = (jnp.maximum(jnp.max(runs[0]), jnp.max(runs[1])) > EXP_ZERO).astype(jnp.int32)
                return g + 1, live, accs, runs

            zero = lambda cols: tuple(jnp.zeros((blk, cols), F32) for _ in heads)
            _, _, accs, _ = lax.while_loop(more, window, (jnp.int32(0), jnp.int32(1), zero(2 * dh), zero(1)))
            o_ref[pl.ds(qoff, blk), :] = hp.merge(accs)
            return 0

        lax.fori_loop(0, nq, q_step, 0)
        finish()

    pair = lambda group: pl.BlockSpec((s, 2 * dh), lambda p: (0, group * (D_ATT // (2 * dh)) + p))
    vec = pl.BlockSpec((1, 2 * dh), lambda p: (0, 0))
    return _call(
        body, "attn_fwd", (N_HEADS // 2,), [pair(2), pair(3), pair(4), vec, vec], [pair(0)],
        [jax.ShapeDtypeStruct((s, D_ATT), F32)], [proj, proj, proj, jnp.tile(qg, (1, 2)), jnp.tile(kg, (1, 2))],
        scratch=[pltpu.VMEM((s, 2 * dh), BF16)] * 3, rider=rider)


def _attn_bwd(proj, dya, qg, kg, rider=None):
    s = proj.shape[0]
    blk, win, dh = ATT_BLOCK, ATT_WINDOW, HEAD_DIM
    nq = s // blk
    max_windows = -(-s // win) + 1
    scale = 1.0 / math.sqrt(dh)
    steps = N_HEADS // 2
    heads = (0, 1)
    assert s >= win and s % blk == 0

    def body(*refs):
        ins, outs, scratch, copies = _split_refs(refs, 6, 5, rider)
        q_ref, k_ref, v_ref, do_ref, qg_ref, kg_ref = ins
        dq_ref, dk_ref, dv_ref, dqg_ref, dkg_ref = outs
        qn, kn, vb, dob, runs_ref, dqn, dkn, dvn = scratch
        finish = _ride(copies, pl.program_id(0) == 0, pl.program_id(0) == steps - 1)
        wd, hp = _Window(), _HeadPair()

        @pl.when(pl.program_id(0) == 0)
        def _():
            dqg_ref[...] = jnp.zeros_like(dqg_ref)
            dkg_ref[...] = jnp.zeros_like(dkg_ref)

        qv = q_ref[...]
        qn[...] = (qv * hp.rms_r(qv) * qg_ref[...] * scale).astype(BF16)
        kv = k_ref[...]
        kn[...] = (kv * hp.rms_r(kv) * kg_ref[...]).astype(BF16)
        vb[...] = v_ref[...].astype(BF16)
        dob[...] = do_ref[...].astype(BF16)
        dkn[...] = jnp.zeros_like(dkn)
        dvn[...] = jnp.zeros_like(dvn)

        def q_step(qi, _):
            qoff = pl.multiple_of(qi * blk, blk)
            qt = qn[pl.ds(qoff, blk), :]
            dot = dob[pl.ds(qoff, blk), :]
            qts = [hp.only(h, qt) for h in heads]
            dots = [hp.only(h, dot) for h in heads]

            zero = lambda cols: tuple(jnp.zeros((blk, cols), F32) for _ in heads)

            def logs_of(g):
                start, valid = wd.place(qi, g)
                kt = kn[pl.ds(start, win), :]
                return [_sb_logs(_dot_nt(qts[h], kt), valid) for h in heads]

            def row_sums(logs):
                return tuple(jnp.sum(logs[h][1], axis=1, keepdims=True) for h in heads)

            def still_live(runs):
                return jnp.maximum(jnp.max(runs[0]), jnp.max(runs[1])) > EXP_ZERO

            def window_grads(g, logs, runs, esums):
                start, valid = wd.place(qi, g)
                kt = kn[pl.ds(start, win), :]
                vt = vb[pl.ds(start, win), :]
                dws = [_dot_nt(dots[h], vt) for h in heads]
                tails = [wd.sums_after(logs[h][1], runs[h])[0] for h in heads]
                wgts = [jnp.where(valid, jnp.exp(logs[h][0] + tails[h]), 0.0) for h in heads]
                es = [dws[h] * wgts[h] for h in heads]
                befores = [wd.sums_before(es[h], esums[h]) for h in heads]
                dzbs = []
                for h in heads:
                    beta = jnp.exp(logs[h][0])
                    dz = jnp.where(valid, es[h] * (1.0 - beta) - befores[h][0] * beta, 0.0)
                    dzbs.append(dz.astype(BF16))
                dkn[pl.ds(start, win), :] += _dot_tn(dzbs[0], qts[0]) + _dot_tn(dzbs[1], qts[1])
                dvn[pl.ds(start, win), :] += (_dot_tn(wgts[0].astype(BF16), dots[0])
                                              + _dot_tn(wgts[1].astype(BF16), dots[1]))
                return tuple(_dot(dzbs[h], kt) for h in heads), tuple(befores[h][1] for h in heads)

            logs0 = logs_of(0)
            runs1 = row_sums(logs0)

            def one_window():
                return window_grads(0, logs0, zero(1), zero(1))[0]

            def all_windows():
                def more(carry):
                    g, live = carry[:2]
                    return jnp.logical_and((qi + 1) * blk - g * win > 0, live > 0)

                def run_window(carry):
                    g, _, runs = carry
                    for h in heads:
                        runs_ref[h, g] = runs[h]
                    sums = row_sums(logs_of(g))
                    runs = tuple(runs[h] + sums[h] for h in heads)
                    return g + 1, still_live(runs).astype(jnp.int32), runs

                for h in heads:
                    runs_ref[h, 0] = jnp.zeros((blk, 1), F32)
                windows, _, _ = lax.while_loop(more, run_window, (jnp.int32(1), jnp.int32(1), runs1))

                def k_window(gg, carry):
                    dq_accs, esums = carry
                    g = windows - 1 - gg
                    parts, totals = window_grads(g, logs_of(g), [runs_ref[h, g] for h in heads], esums)
                    return (tuple(dq_accs[h] + parts[h] for h in heads),
                            tuple(esums[h] + totals[h] for h in heads))

                return lax.fori_loop(0, windows, k_window, (zero(2 * dh), zero(1)))[0]

            earlier_keys = (qi + 1) * blk - win > 0
            dq_accs = lax.cond(jnp.logical_and(earlier_keys, still_live(runs1)), all_windows, one_window)
            dqn[pl.ds(qoff, blk), :] = hp.merge(dq_accs)
            return 0

        lax.fori_loop(0, nq, q_step, 0)

        dq, dqg = hp.rms_bwd(qv, hp.rms_r(qv), qg_ref[...] * scale, dqn[...])
        dq_ref[...] = dq.astype(BF16)
        dqg_ref[...] += dqg * scale
        dk, dkg = hp.rms_bwd(kv, hp.rms_r(kv), kg_ref[...], dkn[...])
        dk_ref[...] = dk.astype(BF16)
        dkg_ref[...] += dkg
        dv_ref[...] = dvn[...].astype(BF16)
        finish()

    pair = lambda group: pl.BlockSpec((s, 2 * dh), lambda p: (0, group * (D_ATT // (2 * dh)) + p))
    vec2 = pl.BlockSpec((1, 2 * dh), lambda p: (0, 0))
    vec = pl.BlockSpec((1, dh), lambda p: (0, 0))
    return _call(
        body, "attn_bwd", (steps,), [pair(2), pair(3), pair(4), pair(0), vec2, vec2],
        [pair(0), pair(0), pair(0), vec, vec],
        [jax.ShapeDtypeStruct((s, D_ATT), BF16)] * 3 + [jax.ShapeDtypeStruct((1, dh), F32)] * 2,
        [proj, proj, proj, dya, jnp.tile(qg, (1, 2)), jnp.tile(kg, (1, 2))],
        scratch=[pltpu.VMEM((s, 2 * dh), BF16)] * 4 + [pltpu.VMEM((2, max_windows, blk, 1), F32)]
        + [pltpu.VMEM((s, 2 * dh), F32)] * 3, rider=rider)


def _block_diag(w):
    n, c, d = w.shape
    return jnp.einsum("ncd,nm->ncmd", w, jnp.eye(n, dtype=w.dtype)).reshape(n * c, n * d)


def _diag_blocks(full, n):
    c = full.shape[0] // n
    return jnp.stack([full[i * c:(i + 1) * c, i * c:(i + 1) * c] for i in range(n)])


FFN1 = ["ffn1_w_gate", "ffn1_w_up", "ffn1_w_down"]
FFN2 = ["ffn2_w_gate", "ffn2_w_up", "ffn2_w_down"]


def _pair_sums(gb, names, where):
    theirs = _pair_exchange([gb[n] for n in names], "pair_exchange_" + names[0])
    pair, own = _pair_sum([gb[n] for n in names], theirs, where, "pair_sum_" + names[0])
    return _chip_rider(pair, own)


def _local_step(x, tgt, stacks, conv_stack, small, where):
    gate_up, down = FFN1[:2], FFN1[2:]
    big = dict(zip(gate_up, _gather_weights([stacks[n] for n in gate_up], [])))
    wa = _block_diag(small["rg_w_a"]).astype(BF16)
    wx = _block_diag(small["rg_w_x"]).astype(BF16)

    whole = lambda names: [big[n].reshape(-1, D_MODEL) for n in names]
    soon, later = down + ["w_in"], FFN2 + ["w_out"]
    g1, u1, hb1, ab1, *landed = _ffn_up(x, small["ffn1_norm"], *whole(gate_up),
                                        rider=_gather_rider([stacks[n] for n in soon], [conv_stack]))
    big.update(zip(soon, landed))
    x1 = _ffn_down(x, ab1, *whole(down))
    conv_w = jnp.transpose(landed[-1], (1, 0, 2)).reshape(CONV_W, D_RNN)
    rg = (conv_w, small["conv_b"], wa, small["rg_b_a"], wx, small["rg_b_x"], small["rg_lambda"])
    proj, hb2 = _mix_pre(x1, small["mix_norm"], big["w_in"])
    yr, hseq = _rglru_fwd(proj, *rg)
    ya, *landed = _attn_fwd(proj, small["q_norm"], small["k_norm"], _gather_rider([stacks[n] for n in later], []))
    big.update(zip(later, landed))
    wout = big["w_out"].reshape(D_MODEL, D_MODEL)
    x2 = _mix_post(x1, yr, ya, small["rnn_out_norm"], small["attn_out_norm"], wout)
    dx3, g2, u2, hb3, ab3, loss = _ffn_fwd_loss(x2, small["ffn2_norm"], *whole(FFN2), tgt)

    gb, gs, slots = {}, {}, {}
    dx2, dg2, du2, dyb2, gs["ffn2_norm"] = _ffn_bwd_act(x2, small["ffn2_norm"], dx3, g2, u2, *whole(FFN2), "ffn2_bwd")
    gb["ffn2_w_gate"] = _ffn_wgrad(dg2, hb3, 1.0, "wgrad_gate_ffn2")
    gb["ffn2_w_up"] = _ffn_wgrad(du2, hb3, 1.0, "wgrad_up_ffn2")
    gb["ffn2_w_down"] = _ffn_wgrad(ab3, dyb2, 0.5, "wgrad_down_ffn2")
    dyr, dya, ycat, dxb2, gs["rnn_out_norm"], gs["attn_out_norm"] = _mix_post_bwd(
        dx2, yr, ya, small["rnn_out_norm"], small["attn_out_norm"], wout)
    gb["w_out"] = _wgrad_whole(ycat, dxb2, False, "wgrad_out")
    early = FFN2 + ["w_out"]
    dq, dk, dv, gs["q_norm"], gs["k_norm"], *done = _attn_bwd(
        proj, dya, small["q_norm"], small["k_norm"], _pair_sums(gb, early, where))
    slots.update(zip(early, done))
    dxr, dgate, gs["conv_w"], gs["conv_b"], dwa, gs["rg_b_a"], dwx, gs["rg_b_x"], gs["rg_lambda"] = _rglru_bwd(
        proj, hseq, dyr, *rg)
    gs["rg_w_a"] = _diag_blocks(dwa, RNN_BLOCKS)
    gs["rg_w_x"] = _diag_blocks(dwx, RNN_BLOCKS)
    dpb = jnp.concatenate([dxr, dgate, dq, dk, dv], axis=1)
    dx1, gs["mix_norm"] = _mix_pre_bwd(x1, small["mix_norm"], dx2, dpb, big["w_in"])
    dx0, dg1, du1, dyb1, gs["ffn1_norm"] = _ffn_bwd_act(x, small["ffn1_norm"], dx1, g1, u1, *whole(FFN1), "ffn1_bwd")

    mine = _place_shard(_pack([gs[n] for n in SMALL] + [loss[:, :1]]), where, F32, "place_small_grads",
                        by_device=True)
    gb["ffn1_w_gate"], everyone = _ffn_wgrad(dg1, hb1, 1.0, "wgrad_gate_ffn1", _small_rider(mine))
    gb["ffn1_w_up"], slots["ffn1_w_gate"] = _ffn_wgrad(
        du1, hb1, 1.0, "wgrad_up_ffn1", _pair_sums(gb, ["ffn1_w_gate"], where))
    gb["ffn1_w_down"], slots["ffn1_w_up"] = _ffn_wgrad(
        ab1, dyb1, 0.5, "wgrad_down_ffn1", _pair_sums(gb, ["ffn1_w_up"], where))
    gb["w_in"], slots["ffn1_w_down"] = _wgrad_whole(
        hb2, dpb, True, "wgrad_in", _pair_sums(gb, ["ffn1_w_down"], where))
    last = _pair_sums(gb, ["w_in"], where)
    slots["w_in"], = _chip_exchange(last.plain, last.inplace)
    return dx0, slots, gs, everyone


ANY = pl.BlockSpec(memory_space=pl.ANY)


def _place():
    x, y, c = lax.axis_index("x"), lax.axis_index("y"), lax.axis_index("c")
    other_chips = [(1 - x, y), (x, 1 - y), (1 - x, 1 - y)]
    return x, y, c, 2 * x + y, other_chips


def _remote(src, dst, send_sem, recv_sem, to):
    return pltpu.make_async_remote_copy(src_ref=src, dst_ref=dst, send_sem=send_sem, recv_sem=recv_sem,
                                        device_id=to, device_id_type=MESH)


def _copy_plan(pairs):
    sends = [functools.partial(_remote, *a) for a, _ in pairs]
    arrivals = [functools.partial(_remote, *b) for _, b in pairs]
    return sends, arrivals


class _Rider:
    def __init__(self, plan, plain, inplace, n_copies=None, relay=None, n_relay=0):
        self.plan, self.plain, self.inplace = plan, list(plain), list(inplace)
        self.n_copies = n_copies or 3 * len(self.inplace)
        self.relay, self.n_relay = relay, n_relay

    def operands(self):
        return self.plain + self.inplace

    def out_shape(self):
        return [jax.ShapeDtypeStruct(a.shape, a.dtype) for a in self.inplace]

    def aliases(self, inputs_before, outputs_before):
        return {inputs_before + len(self.plain) + k: outputs_before + k for k in range(len(self.inplace))}

    def scratch(self):
        relay = [pltpu.SemaphoreType.DMA((self.n_relay,))] * 2 if self.relay else []
        return [pltpu.SemaphoreType.DMA((self.n_copies,))] * 2 + relay


def _split_refs(refs, n_in, n_out, rider):
    if rider is None:
        return refs[:n_in], refs[n_in:n_in + n_out], refs[n_in + n_out:], None
    r_in, r_out = len(rider.operands()), len(rider.inplace)
    outs_at = n_in + r_in
    n_sems = len(rider.scratch())
    rest = refs[outs_at + n_out + r_out:]
    sems = rest[len(rest) - n_sems:]
    filled = refs[outs_at + n_out:outs_at + n_out + r_out]
    copies = functools.partial(rider.plan, refs[n_in:n_in + len(rider.plain)], filled, *sems[:2])
    relay = functools.partial(rider.relay, filled, *sems[2:]) if rider.relay else None
    return refs[:n_in], refs[outs_at:outs_at + n_out], rest[:len(rest) - n_sems], (copies, relay)


def _ride(copies, first, last, middle=None):
    if copies is None:
        return lambda: None
    copies, relay = copies

    @pl.when(first)
    def _():
        _start(copies()[0])

    def start_relay():
        for make in copies()[1]:
            make().wait_recv()
        _start(relay()[0])

    if relay is not None and middle is not None:
        pl.when(middle)(start_relay)

    def finish():
        @pl.when(last)
        def _():
            if relay is None:
                _finish(*copies())
            else:
                if middle is None:
                    start_relay()
                _finish(copies()[0] + relay()[0], relay()[1])

    return finish


def _gather_rider(split, whole):
    n_split = len(split)
    return _Rider(lambda plain, stacks, ss, rs: _gather_ici(stacks, n_split, ss, rs), [], list(split) + list(whole),
                  relay=lambda stacks, ss, rs: _gather_d2d(stacks[:n_split], ss, rs), n_relay=3 * n_split)


def _chip_rider(sums, slots):
    return _Rider(_chip_copies, sums, slots)


def _start(makers):
    for make in makers:
        make().start()


def _finish(sends, arrivals):
    for make in arrivals:
        make().wait_recv()
    for make in sends:
        make().wait_send()


def _half(rows, c):
    return pl.ds(pl.multiple_of(c * rows, 16), rows)


def _gather_weights(split, whole):
    arrs = list(split) + list(whole)
    n, ns = len(arrs), len(split)

    def body(*refs):
        outs = refs[n:2 * n]
        send_sems, recv_sems, fsend_sems, frecv_sems = refs[2 * n:]
        sends, arrivals = _gather_ici(outs, ns, send_sems, recv_sems)
        passes, passed = _gather_d2d(outs[:ns], fsend_sems, frecv_sems)
        _start(sends)
        for k, make in enumerate(arrivals):
            make().wait_recv()
            if k < 3 * ns:
                passes[k]().start()
        _finish(sends + passes, passed)

    return pl.pallas_call(
        body, name="gather_weights",
        in_specs=[ANY] * n, out_specs=[ANY] * n,
        out_shape=[jax.ShapeDtypeStruct(a.shape, a.dtype) for a in arrs],
        input_output_aliases={i: i for i in range(n)},
        scratch_shapes=[pltpu.SemaphoreType.DMA((3 * n,)), pltpu.SemaphoreType.DMA((3 * n,)),
                        pltpu.SemaphoreType.DMA((3 * ns,)), pltpu.SemaphoreType.DMA((3 * ns,))],
    )(*arrs)


def _gather_ici(stacks, n_split, send_sems, recv_sems):
    x, y, c, me, chips = _place()

    def region(i, chip):
        if i < n_split:
            return stacks[i].at[chip, _half(stacks[i].shape[1] // 2, c)]
        return stacks[i].at[chip]

    pairs = []
    for i in range(len(stacks)):
        for p, (cx, cy) in enumerate(chips):
            k = 3 * i + p
            mine, got = region(i, me), region(i, 2 * cx + cy)
            sems, to = (send_sems.at[k], recv_sems.at[k]), (cx, cy, c)
            pairs.append(((mine, mine, *sems, to), (got, got, *sems, to)))
    return _copy_plan(pairs)


def _gather_d2d(stacks, send_sems, recv_sems):
    x, y, c, _, chips = _place()
    sibling = (x, y, 1 - c)
    pairs = []
    for i, stack in enumerate(stacks):
        rows = stack.shape[1] // 2
        for p, (cx, cy) in enumerate(chips):
            k = 3 * i + p
            got, theirs = stack.at[2 * cx + cy, _half(rows, c)], stack.at[2 * cx + cy, _half(rows, 1 - c)]
            sems = (send_sems.at[k], recv_sems.at[k])
            pairs.append(((got, got, *sems, sibling), (theirs, theirs, *sems, sibling)))
    return _copy_plan(pairs)


def _pair_exchange(grads, name):
    n = len(grads)

    def body(*refs):
        ins, theirs = refs[:n], refs[n:2 * n]
        send_sems, recv_sems = refs[2 * n:]
        x, y, c, _, _ = _place()
        sibling = (x, y, 1 - c)
        sends = [_remote(ins[k].at[:, _half(grads[k].shape[1] // 2, 1 - c)], theirs[k],
                         send_sems.at[k], recv_sems.at[k], sibling) for k in range(n)]
        for cp in sends:
            cp.start()
        for k in range(n):
            _remote(theirs[k], theirs[k], send_sems.at[k], recv_sems.at[k], sibling).wait_recv()
        for cp in sends:
            cp.wait_send()

    return pl.pallas_call(
        body, name=name,
        in_specs=[ANY] * n, out_specs=[ANY] * n,
        out_shape=[jax.ShapeDtypeStruct((g.shape[0], g.shape[1] // 2, g.shape[2]), g.dtype) for g in grads],
        scratch_shapes=[pltpu.SemaphoreType.DMA((n,))] * 2,
    )(*grads)


def _chip_exchange(sums, slots):
    n = len(sums)

    def body(*refs):
        sends, arrivals = _chip_copies(refs[:n], refs[2 * n:3 * n], *refs[3 * n:])
        _start(sends)
        _finish(sends, arrivals)

    return pl.pallas_call(
        body, name="grad_chip_exchange",
        in_specs=[ANY] * (2 * n), out_specs=[ANY] * n,
        out_shape=[jax.ShapeDtypeStruct(a.shape, a.dtype) for a in slots],
        input_output_aliases={n + k: k for k in range(n)},
        scratch_shapes=[pltpu.SemaphoreType.DMA((3 * n,)), pltpu.SemaphoreType.DMA((3 * n,))],
    )(*sums, *slots)


def _chip_copies(sums, slots, send_sems, recv_sems):
    x, y, c, me, chips = _place()
    pairs = []
    for k in range(len(sums)):
        for p, (cx, cy) in enumerate(chips):
            j = 3 * k + p
            got = slots[k].at[2 * cx + cy]
            sems, to = (send_sems.at[j], recv_sems.at[j]), (cx, cy, c)
            pairs.append(((sums[k].at[2 * cx + cy], slots[k].at[me], *sems, to), (got, got, *sems, to)))
    return _copy_plan(pairs)


def _half_swap(halves):
    n = len(halves)

    def body(*refs):
        outs = refs[n:2 * n]
        send_sems, recv_sems = refs[2 * n:]
        x, y, c, _, _ = _place()
        sibling = (x, y, 1 - c)
        sends = [_remote(outs[k].at[c], outs[k].at[c], send_sems.at[k], recv_sems.at[k], sibling) for k in range(n)]
        for cp in sends:
            cp.start()
        for k in range(n):
            got = outs[k].at[1 - c]
            _remote(got, got, send_sems.at[k], recv_sems.at[k], sibling).wait_recv()
        for cp in sends:
            cp.wait_send()

    return pl.pallas_call(
        body, name="grad_half_swap",
        in_specs=[ANY] * n, out_specs=[ANY] * n,
        out_shape=[jax.ShapeDtypeStruct(a.shape, a.dtype) for a in halves],
        input_output_aliases={k: k for k in range(n)},
        scratch_shapes=[pltpu.SemaphoreType.DMA((n,))] * 2,
    )(*halves)


def _small_rider(stack):
    n_dev = 2 * N_CHIPS

    def plan(_, stacks, send_sems, recv_sems):
        x, y, c, _, _ = _place()
        mine = stacks[0].at[4 * x + 2 * y + c]
        pairs = []
        for k in range(1, n_dev):
            px, py, pc = x ^ ((k >> 2) & 1), y ^ ((k >> 1) & 1), c ^ (k & 1)
            got = stacks[0].at[4 * px + 2 * py + pc]
            sems = (send_sems.at[k - 1], recv_sems.at[k - 1])
            pairs.append(((mine, mine, *sems, (px, py, pc)), (got, got, *sems, (px, py, pc))))
        return _copy_plan(pairs)

    return _Rider(plan, [], [stack], n_dev - 1)


def _row_tile(r):
    return r // 4 if r >= 256 and (r // 4) % 16 == 0 else r


def _prefetch_call(body, name, grid, in_specs, out_specs, out_shape):
    spec = pltpu.PrefetchScalarGridSpec(num_scalar_prefetch=1, grid=grid, in_specs=in_specs, out_specs=out_specs)
    return pl.pallas_call(body, name=name, grid_spec=spec, out_shape=out_shape,
                          compiler_params=_params(("arbitrary",) * len(grid)))


def _place_shard(w2d, where, dtype, name, by_device=False):
    r, c = w2d.shape
    tr = _row_tile(r)
    slots = 2 * N_CHIPS if by_device else N_CHIPS
    slot = (lambda s: 2 * s[1] + s[0]) if by_device else (lambda s: s[1])

    def body(where_ref, w_ref, out_ref):
        out_ref[...] = w_ref[...].astype(dtype)

    return _prefetch_call(
        body, name, (r // tr,), [pl.BlockSpec((tr, c), lambda i, s: (i, 0))],
        pl.BlockSpec((None, tr, c), lambda i, s: (slot(s), i, 0)),
        jax.ShapeDtypeStruct((slots, r, c), dtype))(where, w2d)


def _place_shards(w2ds, where, name):
    n = len(w2ds)
    steps = N_CHIPS
    assert all(w.shape[0] % (16 * steps) == 0 for w in w2ds)

    def body(where_ref, *refs):
        for k in range(n):
            refs[n + k][...] = refs[k][...].astype(BF16)

    tile = lambda w: (w.shape[0] // steps, w.shape[1])
    return _prefetch_call(
        body, name, (steps,), [pl.BlockSpec(tile(w), lambda i, s: (i, 0)) for w in w2ds],
        [pl.BlockSpec((None,) + tile(w), lambda i, s: (s[1], i, 0)) for w in w2ds],
        [jax.ShapeDtypeStruct((N_CHIPS,) + w.shape, BF16) for w in w2ds])(where, *w2ds)


def _pair_sum(fulls, theirs, where, name):
    n = len(fulls)

    def body(where_ref, *refs):
        for k in range(n):
            a_ref, b_ref, out_ref, own_ref = refs[k], refs[n + k], refs[2 * n + k], refs[3 * n + k]
            total = (a_ref[...].astype(F32) + b_ref[...].astype(F32)).astype(BF16)
            out_ref[...] = total

            @pl.when(pl.program_id(0) == where_ref[1])
            def _():
                own_ref[...] = total

    half = lambda t: pl.BlockSpec((None,) + t.shape[1:], lambda j, s: (j, s[0], 0))
    blk = lambda t: pl.BlockSpec((None,) + t.shape[1:], lambda j, s: (j, 0, 0))
    own = lambda t: pl.BlockSpec((None,) + t.shape[1:], lambda j, s: (s[1], 0, 0))
    shapes = [jax.ShapeDtypeStruct(t.shape, BF16) for t in theirs]
    outs = _prefetch_call(
        body, name, (N_CHIPS,), [half(t) for t in theirs] + [blk(t) for t in theirs],
        [blk(t) for t in theirs] + [own(t) for t in theirs], shapes + shapes)(where, *fulls, *theirs)
    return outs[:n], outs[n:]


def _chip_sum(slots, where, name):
    n = len(slots)
    steps = 2
    assert all(a.shape[1] % (16 * steps) == 0 for a in slots)

    def body(where_ref, *refs):
        for k in range(n):
            a_ref, out_ref = refs[k], refs[n + k]
            total = a_ref[0].astype(F32)
            for j in range(1, a_ref.shape[0]):
                total = total + a_ref[j].astype(F32)
            out_ref[...] = total

    tile = lambda a: (a.shape[1] // steps, a.shape[2])
    return _prefetch_call(
        body, name, (steps,), [pl.BlockSpec((a.shape[0],) + tile(a), lambda i, s: (0, i, 0)) for a in slots],
        [pl.BlockSpec((None,) + tile(a), lambda i, s: (s[0], i, 0)) for a in slots],
        [jax.ShapeDtypeStruct((2,) + a.shape[1:], F32) for a in slots])(where, *slots)


def _slot_sum(a, name):
    nb, r, c = a.shape
    tr = _row_tile(r)

    def body(a_ref, out_ref):
        total = a_ref[0].astype(F32)
        for j in range(1, nb):
            total = total + a_ref[j].astype(F32)
        out_ref[...] = total

    return pl.pallas_call(
        body, name=name, grid=(r // tr,),
        in_specs=[pl.BlockSpec((nb, tr, c), lambda i: (0, i, 0))],
        out_specs=pl.BlockSpec((tr, c), lambda i: (i, 0)),
        out_shape=jax.ShapeDtypeStruct((r, c), F32), compiler_params=_params(("arbitrary",)),
    )(a)


def _adamw(ws, gs, ms, vs, name, steps=1):
    n = len(ws)
    c1 = 1.0 - ADAM_B1 ** ADAM_STEP
    c2 = 1.0 - ADAM_B2 ** ADAM_STEP
    assert all(w.shape[0] % steps == 0 and (steps == 1 or w.shape[0] // steps % 8 == 0) for w in ws)

    def body(*refs):
        for k in range(n):
            w_ref, g_ref, m_ref, v_ref = (refs[j * n + k] for j in range(4))
            d_ref, m2_ref, v2_ref = (refs[(4 + j) * n + k] for j in range(3))
            gv = g_ref[...]
            m2 = ADAM_B1 * m_ref[...] + (1.0 - ADAM_B1) * gv
            v2 = ADAM_B2 * v_ref[...] + (1.0 - ADAM_B2) * (gv * gv)
            m2_ref[...] = m2
            v2_ref[...] = v2
            d_ref[...] = -ADAM_LR * ((m2 / c1) / (jnp.sqrt(v2 / c2) + ADAM_EPS) + ADAM_WD * w_ref[...])

    blks = [pl.BlockSpec((w.shape[0] // steps, w.shape[1]), lambda i: (i, 0)) for w in ws]
    shapes = [jax.ShapeDtypeStruct(w.shape, F32) for w in ws]
    outs = pl.pallas_call(
        body, name=name, grid=(steps,), in_specs=blks * 4, out_specs=blks * 3, out_shape=shapes * 3,
        compiler_params=_params(("arbitrary",)),
    )(*ws, *gs, *ms, *vs)
    return outs[:n], outs[n:2 * n], outs[2 * n:]


WEIGHTS = ["ffn1_norm", "ffn1_w_gate", "ffn1_w_up", "ffn1_w_down", "mix_norm", "w_in", "conv_w", "conv_b",
           "rg_w_a", "rg_b_a", "rg_w_x", "rg_b_x", "rg_lambda", "q_norm", "k_norm", "rnn_out_norm",
           "attn_out_norm", "w_out", "ffn2_norm", "ffn2_w_gate", "ffn2_w_up", "ffn2_w_down"]
BIG = ["ffn1_w_gate", "ffn1_w_up", "ffn1_w_down", "w_in", "w_out", "ffn2_w_gate", "ffn2_w_up", "ffn2_w_down"]
SMALL = [n for n in WEIGHTS if n not in BIG]
PACK_LANES = 128
PACK_ROW_ALIGN = 8


def _hidden_major(name, a):
    return jnp.transpose(a) if name.endswith(("w_gate", "w_up")) else a


def _pack(parts):
    flat = jnp.concatenate([p.reshape(-1) for p in parts])
    unit = PACK_LANES * PACK_ROW_ALIGN
    padded = -(-flat.shape[0] // unit) * unit
    return jnp.pad(flat, (0, padded - flat.shape[0])).reshape(-1, PACK_LANES)


def _unpack(packed, shapes):
    flat = packed.reshape(-1)
    out, at = [], 0
    for shp in shapes:
        size = math.prod(shp)
        out.append(flat[at:at + size].reshape(shp))
        at += size
    return out


def kernel(x, ffn1_norm, ffn1_w_gate, ffn1_w_up, ffn1_w_down, mix_norm, w_in, conv_w, conv_b, rg_w_a, rg_b_a, rg_w_x, rg_b_x, rg_lambda, q_norm, k_norm, rnn_out_norm, attn_out_norm, w_out, ffn2_norm, ffn2_w_gate, ffn2_w_up, ffn2_w_down, loss_target, m_ffn1_norm, m_ffn1_w_gate, m_ffn1_w_up, m_ffn1_w_down, m_mix_norm, m_w_in, m_conv_w, m_conv_b, m_rg_w_a, m_rg_b_a, m_rg_w_x, m_rg_b_x, m_rg_lambda, m_q_norm, m_k_norm, m_rnn_out_norm, m_attn_out_norm, m_w_out, m_ffn2_norm, m_ffn2_w_gate, m_ffn2_w_up, m_ffn2_w_down, v_ffn1_norm, v_ffn1_w_gate, v_ffn1_w_up, v_ffn1_w_down, v_mix_norm, v_w_in, v_conv_w, v_conv_b, v_rg_w_a, v_rg_b_a, v_rg_w_x, v_rg_b_x, v_rg_lambda, v_q_norm, v_k_norm, v_rnn_out_norm, v_attn_out_norm, v_w_out, v_ffn2_norm, v_ffn2_w_gate, v_ffn2_w_up, v_ffn2_w_down):
    given = dict(locals())
    w = {n: given[n] for n in WEIGHTS}
    m = {n: given["m_" + n] for n in WEIGHTS}
    v = {n: given["v_" + n] for n in WEIGHTS}
    chip = 2 * lax.axis_index("x") + lax.axis_index("y")

    where = jnp.stack([lax.axis_index("c"), chip]).astype(jnp.int32)

    stacks = dict(zip(BIG, _place_shards([_hidden_major(n, w[n][0]) for n in BIG], where, "place_weights")))
    conv_stack = _place_shard(w["conv_w"][0], where, F32, "place_conv_w")
    small = {n: (w[n][0] if w[n].ndim > 2 else w[n]) for n in SMALL if n != "conv_w"}

    grad_x, slots, gs, everyone = _local_step(x[0], loss_target[0], stacks, conv_stack, small, where)

    swapped = _half_swap(_chip_sum([slots[n] for n in BIG], where, "chip_sums"))
    g2s = [t.reshape(t.shape[0] * t.shape[1], t.shape[2]) for t in swapped]
    flat = lambda tree: [_hidden_major(n, tree[n][0]) for n in BIG]
    d2s, m2s, v2s = _adamw(flat(w), g2s, flat(m), flat(v), "adamw_weights", ADAMW_STEPS)
    grads, deltas, new_m, new_v = {}, {}, {}, {}
    for tree, parts in ((grads, g2s), (deltas, d2s), (new_m, m2s), (new_v, v2s)):
        tree.update({n: _hidden_major(n, a).reshape(w[n].shape) for n, a in zip(BIG, parts)})

    full_shapes = [gs[n].shape for n in SMALL]
    *summed, loss = _unpack(_slot_sum(everyone, "small_grad_sum"), full_shapes + [(1, 1)])
    g_parts = dict(zip(SMALL, summed))
    quarter = D_RNN // N_CHIPS
    g_parts["conv_w"] = lax.dynamic_slice_in_dim(g_parts["conv_w"], chip * quarter, quarter, axis=1)
    local_shapes = [w[n].shape for n in SMALL]
    pk = lambda tree: _pack([tree[n] for n in SMALL])
    (d_s,), (m_s,), (v_s,) = _adamw([pk(w)], [pk(g_parts)], [pk(m)], [pk(v)], "adamw_small")
    for tree, packed in ((grads, pk(g_parts)), (deltas, d_s), (new_m, m_s), (new_v, v_s)):
        tree.update(zip(SMALL, _unpack(packed, local_shapes)))

    return (loss[0, 0], grad_x.reshape(x.shape), *[grads[n] for n in WEIGHTS], *[deltas[n] for n in WEIGHTS],
            *[new_m[n] for n in WEIGHTS], *[new_v[n] for n in WEIGHTS])
```

```python
import functools
import math

import jax
import jax.numpy as jnp
from jax import lax
from jax.experimental import pallas as pl
from jax.experimental.pallas import tpu as pltpu

F32 = jnp.float32
BF16 = jnp.bfloat16
MESH = pl.DeviceIdType.MESH

D_MODEL = 1024
N_CHIPS = 4
D_RNN = 512
D_ATT = 512
N_HEADS = 8
HEAD_DIM = 64
RNN_BLOCKS = 8
CONV_W = 4
RG_C = 8.0
N_IN = 2 * D_RNN + 3 * D_ATT
EPS = 1e-6
ATT_BLOCK = 128
ATT_WINDOW = 384
ATT_SPLIT = 256
EXP_ZERO = -105.0

ADAM_LR = 0.001
ADAM_B1 = 0.9
ADAM_B2 = 0.999
ADAM_EPS = 1e-08
ADAM_WD = 0.01
ADAM_STEP = 10

V7X_VMEM_LIMIT = 56 * 1024 * 1024
V7X_MXU_WIDTH = 256
TOKEN_TILE = 512
SUBLANES = 8
FFN_TILE = 256
WGRAD_TILE = 2048
WHOLE_TILE = 1024
ADAMW_STEPS = 8

GELU_K0 = math.sqrt(2.0 / math.pi)
GELU_K1 = 0.044715


def _params(sem=None):
    return pltpu.CompilerParams(dimension_semantics=sem, vmem_limit_bytes=V7X_VMEM_LIMIT)


def _dot(a, b):
    return jnp.dot(a, b, preferred_element_type=F32)


def _dot_nt(a, b):
    return lax.dot_general(a, b, (((1,), (1,)), ((), ())), preferred_element_type=F32)


def _dot_tn(a, b):
    return lax.dot_general(a, b, (((0,), (0,)), ((), ())), preferred_element_type=F32)


def _sigmoid(x):
    return 1.0 / (1.0 + jnp.exp(-x))


def _rms_r(xv):
    return lax.rsqrt(jnp.mean(xv * xv, axis=-1, keepdims=True) + EPS)


def _rms_bwd(xv, r, nw, dh):
    t = dh * nw
    dx = r * t - xv * (r * r * r * jnp.mean(t * xv, axis=-1, keepdims=True))
    dn = jnp.sum(dh * xv * r, axis=0, keepdims=True)
    return dx, dn


def _gelu(x):
    t = jnp.tanh(GELU_K0 * (x + GELU_K1 * x * x * x))
    return 0.5 * x * (1.0 + t)


def _gelu_grad(x):
    t = jnp.tanh(GELU_K0 * (x + GELU_K1 * x * x * x))
    return 0.5 * (1.0 + t) + 0.5 * x * (1.0 - t * t) * (GELU_K0 * (1.0 + 3.0 * GELU_K1 * x * x))


def _expm1_neg(x):
    p = 1.0 + x * (1.0 / 6.0)
    for k in (5.0, 4.0, 3.0, 2.0):
        p = 1.0 + x * (1.0 / k) * p
    return jnp.where(x > -0.25, x * p, jnp.exp(x) - 1.0)


def _log_sigmoid(x):
    return jnp.minimum(x, 0.0) - jnp.log(1.0 + jnp.exp(-jnp.abs(x)))


def _tile(s):
    return min(TOKEN_TILE, s)


def _ffn_chunks(f):
    cut = f // 2 // V7X_MXU_WIDTH * V7X_MXU_WIDTH
    return ((0, cut), (cut, f)) if 0 < cut < f else ((0, f),)


def _ffn_fwd_loss(x, nw, wg, wu, wd, tgt):
    s, d = x.shape
    f = wg.shape[0]
    tm = min(FFN_TILE, s)
    ni = s // tm
    assert s % tm == 0

    def body(x_ref, nw_ref, wg_ref, wu_ref, wd_ref, tgt_ref, out_ref, g_ref, u_ref, hb_ref, ab_ref, loss_ref):
        i = pl.program_id(0)
        xv = x_ref[...]
        hb = (xv * _rms_r(xv) * nw_ref[...]).astype(BF16)
        hb_ref[...] = hb
        y = jnp.zeros((tm, d), F32)
        for lo, hi in _ffn_chunks(f):
            g = _dot_nt(hb, wg_ref[lo:hi, :])
            u = _dot_nt(hb, wu_ref[lo:hi, :])
            g_ref[:, lo:hi] = g.astype(BF16)
            u_ref[:, lo:hi] = u.astype(BF16)
            ab = (g * _sigmoid(g) * u).astype(BF16)
            ab_ref[:, lo:hi] = ab
            y = y + _dot(ab, wd_ref[lo:hi, :])
        diff = xv + 0.5 * y - tgt_ref[...]
        out_ref[...] = diff * (1.0 / d)

        @pl.when(i == 0)
        def _():
            loss_ref[...] = jnp.zeros_like(loss_ref)

        loss_ref[...] += jnp.sum(diff * diff) * (0.5 / d)

    row = pl.BlockSpec((tm, d), lambda i: (i, 0))
    weight = pl.BlockSpec((f, d), lambda i: (0, 0), pipeline_mode=pl.Buffered(1))
    blk = pl.BlockSpec((tm, f), lambda i: (i, 0))
    wide = jax.ShapeDtypeStruct((s, f), BF16)
    return _call(body, "ffn_fwd_loss", (ni,),
                 [row, pl.BlockSpec((1, d), lambda i: (0, 0)), weight, weight, weight, row],
                 [row, blk, blk, row, blk, pl.BlockSpec((1, 128), lambda i: (0, 0))],
                 [jax.ShapeDtypeStruct((s, d), F32), wide, wide, jax.ShapeDtypeStruct((s, d), BF16), wide,
                  jax.ShapeDtypeStruct((1, 128), F32)], [x, nw, wg, wu, wd, tgt])


def _ffn_up(x, nw, wg, wu, rider=None):
    s, d = x.shape
    f = wg.shape[0]
    tm = min(FFN_TILE, s)
    ni = s // tm
    assert s % tm == 0

    def body(*refs):
        (x_ref, nw_ref, wg_ref, wu_ref), (g_ref, u_ref, hb_ref, ab_ref), _, copies = _split_refs(refs, 4, 4, rider)
        i = pl.program_id(0)
        finish = _ride(copies, i == 0, i == ni - 1)
        xv = x_ref[...]
        hb = (xv * _rms_r(xv) * nw_ref[...]).astype(BF16)
        hb_ref[...] = hb
        for lo, hi in _ffn_chunks(f):
            g = _dot_nt(hb, wg_ref[lo:hi, :])
            u = _dot_nt(hb, wu_ref[lo:hi, :])
            g_ref[:, lo:hi] = g.astype(BF16)
            u_ref[:, lo:hi] = u.astype(BF16)
            ab_ref[:, lo:hi] = (g * _sigmoid(g) * u).astype(BF16)
        finish()

    row = pl.BlockSpec((tm, d), lambda i: (i, 0))
    weight = pl.BlockSpec((f, d), lambda i: (0, 0), pipeline_mode=pl.Buffered(1))
    blk = pl.BlockSpec((tm, f), lambda i: (i, 0))
    wide = jax.ShapeDtypeStruct((s, f), BF16)
    return _call(body, "ffn_up", (ni,), [row, pl.BlockSpec((1, d), lambda i: (0, 0)), weight, weight],
                 [blk, blk, row, blk], [wide, wide, jax.ShapeDtypeStruct((s, d), BF16), wide], [x, nw, wg, wu],
                 rider=rider)


def _ffn_down(x, ab, wd):
    s, d = x.shape
    f = wd.shape[0]
    tm = _tile(s)
    assert s % tm == 0

    def body(x_ref, ab_ref, wd_ref, out_ref):
        out_ref[...] = x_ref[...] + 0.5 * _dot(ab_ref[...], wd_ref[...])

    row = pl.BlockSpec((tm, d), lambda i: (i, 0))
    return _call(body, "ffn_down", (s // tm,),
                 [row, pl.BlockSpec((tm, f), lambda i: (i, 0)),
                  pl.BlockSpec((f, d), lambda i: (0, 0), pipeline_mode=pl.Buffered(1))],
                 [row], [jax.ShapeDtypeStruct((s, d), F32)], [x, ab, wd])[0]


def _call(body, name, grid, in_specs, out_specs, out_shape, args, scratch=(), rider=None):
    in_specs, out_specs, out_shape, scratch = list(in_specs), list(out_specs), list(out_shape), list(scratch)
    extra, aliases = [], {}
    if rider is not None:
        extra = rider.operands()
        aliases = rider.aliases(len(args), len(out_shape))
        in_specs += [ANY] * len(extra)
        out_specs += [ANY] * len(rider.inplace)
        out_shape += rider.out_shape()
        scratch += rider.scratch()
    return pl.pallas_call(
        body, name=name, grid=grid, in_specs=in_specs, out_specs=out_specs, out_shape=out_shape,
        input_output_aliases=aliases, scratch_shapes=scratch,
        compiler_params=_params(("arbitrary",) * len(grid)),
    )(*args, *extra)


def _ffn_bwd_act(x, nw, dy, g, u, wg, wu, wd, name):
    s, d = x.shape
    f = wg.shape[0]
    tm = min(FFN_TILE, s)
    assert s % tm == 0

    def body(x_ref, nw_ref, dy_ref, g_ref, u_ref, wg_ref, wu_ref, wd_ref,
             dx_ref, dg_ref, du_ref, dyb_ref, dnw_ref):
        dyv = dy_ref[...]
        dyb = dyv.astype(BF16)
        dyb_ref[...] = dyb
        dh = jnp.zeros((tm, d), F32)
        for lo, hi in _ffn_chunks(f):
            da = 0.5 * _dot_nt(dyb, wd_ref[lo:hi, :])
            gv = g_ref[:, lo:hi].astype(F32)
            sg = _sigmoid(gv)
            dub = (da * (gv * sg)).astype(BF16)
            dgb = (da * u_ref[:, lo:hi].astype(F32) * (sg * (1.0 + gv * (1.0 - sg)))).astype(BF16)
            dg_ref[:, lo:hi] = dgb
            du_ref[:, lo:hi] = dub
            dh = dh + _dot(dgb, wg_ref[lo:hi, :]) + _dot(dub, wu_ref[lo:hi, :])
        xv = x_ref[...]
        dx, dn = _rms_bwd(xv, _rms_r(xv), nw_ref[...], dh)
        dx_ref[...] = dyv + dx

        @pl.when(pl.program_id(0) == 0)
        def _():
            dnw_ref[...] = jnp.zeros_like(dnw_ref)

        dnw_ref[...] += dn

    row = pl.BlockSpec((tm, d), lambda i: (i, 0))
    vec = pl.BlockSpec((1, d), lambda i: (0, 0))
    blk = pl.BlockSpec((tm, f), lambda i: (i, 0))
    weight = pl.BlockSpec((f, d), lambda i: (0, 0), pipeline_mode=pl.Buffered(1))
    return _call(
        body, name, (s // tm,), [row, vec, row, blk, blk, weight, weight, weight], [row, blk, blk, row, vec],
        [jax.ShapeDtypeStruct((s, d), F32), jax.ShapeDtypeStruct((s, f), BF16),
         jax.ShapeDtypeStruct((s, f), BF16), jax.ShapeDtypeStruct((s, d), BF16),
         jax.ShapeDtypeStruct((1, d), F32)],
        [x, nw, dy, g, u, wg, wu, wd])


def _wgrad(a, b, a_spec, b_spec, out_rows, out_cols, scale, name, tk, rider=None, per_step=1):
    s = a.shape[-2]
    nk = s // tk
    steps = N_CHIPS // per_step
    assert s % tk == 0

    def body(*refs):
        (a_ref, b_ref), (out_ref,), (acc,), copies = _split_refs(refs, 2, 1, rider)
        j, k = pl.program_id(0), pl.program_id(1)
        finish = _ride(copies, jnp.logical_and(j == 0, k == 0), jnp.logical_and(j == steps - 1, k == nk - 1))

        @pl.when(k == 0)
        def _():
            acc[...] = jnp.zeros_like(acc)

        acc[...] += _dot_tn(a_ref[...], b_ref[...])

        @pl.when(k == nk - 1)
        def _():
            for t in range(per_step):
                out_ref[t] = (acc[t * out_rows:(t + 1) * out_rows, :] * scale).astype(BF16)

        finish()

    outs = _call(
        body, name, (steps, nk), [a_spec(tk), b_spec(tk)],
        [pl.BlockSpec((per_step, out_rows, out_cols), lambda j, k: (j, 0, 0))],
        [jax.ShapeDtypeStruct((N_CHIPS, out_rows, out_cols), BF16)], [a, b],
        scratch=[pltpu.VMEM((per_step * out_rows, out_cols), F32)], rider=rider)
    return outs[0] if rider is None else outs


def _wgrad_whole(a, b, col_blocks, name, rider=None):
    s, m = a.shape
    n = b.shape[1]
    tk = min(WHOLE_TILE, s)
    nk = s // tk
    assert s % tk == 0
    out_shape = (N_CHIPS, m, n // N_CHIPS) if col_blocks else (N_CHIPS, m // N_CHIPS, n)

    def body(*refs):
        (a_ref, b_ref), (out_ref,), (acc,), copies = _split_refs(refs, 2, 1, rider)
        k = pl.program_id(0)
        finish = _ride(copies, k == 0, k == nk - 1)

        @pl.when(k == 0)
        def _():
            acc[...] = jnp.zeros_like(acc)

        acc[...] += _dot_tn(a_ref[...], b_ref[...])

        @pl.when(k == nk - 1)
        def _():
            for j in range(N_CHIPS):
                if col_blocks:
                    out_ref[j] = acc[:, j * out_shape[2]:(j + 1) * out_shape[2]].astype(BF16)
                else:
                    out_ref[j] = acc[j * out_shape[1]:(j + 1) * out_shape[1], :].astype(BF16)

        finish()

    outs = _call(
        body, name, (nk,), [pl.BlockSpec((tk, m), lambda k: (k, 0)), pl.BlockSpec((tk, n), lambda k: (k, 0))],
        [pl.BlockSpec(out_shape, lambda k: (0, 0, 0))], [jax.ShapeDtypeStruct(out_shape, BF16)], [a, b],
        scratch=[pltpu.VMEM((m, n), F32)], rider=rider)
    return outs[0] if rider is None else outs


def _ffn_wgrad(hidden, shared, scale, name, rider=None):
    s, d = shared.shape
    half = hidden.shape[1] // 2
    return _wgrad(hidden, shared, lambda tk: pl.BlockSpec((tk, half), lambda j, k: (k, j)),
                  lambda tk: pl.BlockSpec((tk, d), lambda j, k: (k, 0)), half // 2, d, scale, name,
                  min(WGRAD_TILE, s), rider, per_step=2)


def _mix_pre(x, nw, win):
    s, d = x.shape
    nb, _, cb = win.shape
    tm = _tile(s)
    assert s % tm == 0

    def body(x_ref, nw_ref, w_ref, p_ref, hb_ref):
        xv = x_ref[...]
        hb = (xv * _rms_r(xv) * nw_ref[...]).astype(BF16)
        hb_ref[...] = hb
        for j in range(nb):
            p_ref[:, j * cb:(j + 1) * cb] = _dot(hb, w_ref[j])

    row = pl.BlockSpec((tm, d), lambda i: (i, 0))
    return pl.pallas_call(
        body, name="mix_pre", grid=(s // tm,),
        in_specs=[row, pl.BlockSpec((1, d), lambda i: (0, 0)),
                  pl.BlockSpec((nb, d, cb), lambda i: (0, 0, 0), pipeline_mode=pl.Buffered(1))],
        out_specs=[pl.BlockSpec((tm, nb * cb), lambda i: (i, 0)), row],
        out_shape=[jax.ShapeDtypeStruct((s, nb * cb), F32), jax.ShapeDtypeStruct((s, d), BF16)],
        compiler_params=_params(("arbitrary",)),
    )(x, nw, win)


def _mix_pre_bwd(x, nw, dres, dpb, win):
    s, d = x.shape
    nb, _, cb = win.shape
    tm = _tile(s)
    assert s % tm == 0

    def body(x_ref, nw_ref, dres_ref, dp_ref, w_ref, dx_ref, dnw_ref):
        dh = jnp.zeros((tm, d), F32)
        for j in range(nb):
            dh = dh + _dot_nt(dp_ref[:, j * cb:(j + 1) * cb], w_ref[j])
        xv = x_ref[...]
        dx, dn = _rms_bwd(xv, _rms_r(xv), nw_ref[...], dh)
        dx_ref[...] = dres_ref[...] + dx

        @pl.when(pl.program_id(0) == 0)
        def _():
            dnw_ref[...] = jnp.zeros_like(dnw_ref)

        dnw_ref[...] += dn

    row = pl.BlockSpec((tm, d), lambda i: (i, 0))
    vec = pl.BlockSpec((1, d), lambda i: (0, 0))
    return pl.pallas_call(
        body, name="mix_pre_bwd", grid=(s // tm,),
        in_specs=[row, vec, row, pl.BlockSpec((tm, nb * cb), lambda i: (i, 0)),
                  pl.BlockSpec((nb, d, cb), lambda i: (0, 0, 0), pipeline_mode=pl.Buffered(1))],
        out_specs=[row, vec],
        out_shape=[jax.ShapeDtypeStruct((s, d), F32), jax.ShapeDtypeStruct((1, d), F32)],
        compiler_params=_params(("arbitrary",)),
    )(x, nw, dres, dpb, win)


def _mix_post(x, yr, ya, nr, na, wout):
    s, d = x.shape
    h = yr.shape[1]
    tm = _tile(s)

    def body(x_ref, yr_ref, ya_ref, nr_ref, na_ref, w_ref, out_ref):
        yrv = yr_ref[...]
        yav = ya_ref[...]
        onb = (yrv * _rms_r(yrv) * nr_ref[...]).astype(BF16)
        oab = (yav * _rms_r(yav) * na_ref[...]).astype(BF16)
        out_ref[...] = x_ref[...] + _dot(onb, w_ref[0:h, :]) + _dot(oab, w_ref[h:2 * h, :])

    row = pl.BlockSpec((tm, d), lambda i: (i, 0))
    half = pl.BlockSpec((tm, h), lambda i: (i, 0))
    vec = pl.BlockSpec((1, h), lambda i: (0, 0))
    return pl.pallas_call(
        body, name="mix_post", grid=(s // tm,),
        in_specs=[row, half, half, vec, vec, pl.BlockSpec((2 * h, d), lambda i: (0, 0))],
        out_specs=row, out_shape=jax.ShapeDtypeStruct((s, d), F32),
        compiler_params=_params(("arbitrary",)),
    )(x, yr, ya, nr, na, wout)


def _mix_post_bwd(dx, yr, ya, nr, na, wout):
    s, d = dx.shape
    h = yr.shape[1]
    tm = _tile(s)

    def body(dx_ref, yr_ref, ya_ref, nr_ref, na_ref, w_ref,
             dyr_ref, dya_ref, yc_ref, dxb_ref, dnr_ref, dna_ref):
        i = pl.program_id(0)
        dxb = dx_ref[...].astype(BF16)
        dxb_ref[...] = dxb
        dyc = _dot_nt(dxb, w_ref[...])
        yrv = yr_ref[...]
        yav = ya_ref[...]
        rr = _rms_r(yrv)
        ra = _rms_r(yav)
        yc_ref[:, 0:h] = (yrv * rr * nr_ref[...]).astype(BF16)
        yc_ref[:, h:2 * h] = (yav * ra * na_ref[...]).astype(BF16)
        dyr, dnr = _rms_bwd(yrv, rr, nr_ref[...], dyc[:, 0:h])
        dya, dna = _rms_bwd(yav, ra, na_ref[...], dyc[:, h:2 * h])
        dyr_ref[...] = dyr
        dya_ref[...] = dya

        @pl.when(i == 0)
        def _():
            dnr_ref[...] = jnp.zeros_like(dnr_ref)
            dna_ref[...] = jnp.zeros_like(dna_ref)

        dnr_ref[...] += dnr
        dna_ref[...] += dna

    row = pl.BlockSpec((tm, d), lambda i: (i, 0))
    half = pl.BlockSpec((tm, h), lambda i: (i, 0))
    vec = pl.BlockSpec((1, h), lambda i: (0, 0))
    return pl.pallas_call(
        body, name="mix_post_bwd", grid=(s // tm,),
        in_specs=[row, half, half, vec, vec, pl.BlockSpec((2 * h, d), lambda i: (0, 0))],
        out_specs=[half, half, pl.BlockSpec((tm, 2 * h), lambda i: (i, 0)), row, vec, vec],
        out_shape=[jax.ShapeDtypeStruct((s, h), F32), jax.ShapeDtypeStruct((s, h), F32),
                   jax.ShapeDtypeStruct((s, 2 * h), BF16), jax.ShapeDtypeStruct((s, d), BF16),
                   jax.ShapeDtypeStruct((1, h), F32), jax.ShapeDtypeStruct((1, h), F32)],
        compiler_params=_params(("arbitrary",)),
    )(dx, yr, ya, nr, na, wout)


def _shift_down(xv, s, prev8):
    rolled = pltpu.roll(xv, s, 0)
    row8 = lax.broadcasted_iota(jnp.int32, prev8.shape, 0)
    head = jnp.where(row8 < s, pltpu.roll(prev8, s, 0), rolled[0:8, :])
    return jnp.concatenate([head, rolled[8:, :]], axis=0)


def _shift_up(xv, s, next8):
    n = xv.shape[0]
    rolled = pltpu.roll(xv, n - s, 0)
    row8 = lax.broadcasted_iota(jnp.int32, next8.shape, 0)
    tail = jnp.where(row8 >= 8 - s, pltpu.roll(next8, 8 - s, 0), rolled[n - 8:, :])
    return jnp.concatenate([rolled[:n - 8, :], tail], axis=0)


def _scan_fwd(a, b):
    n = a.shape[0]
    sub = lax.broadcasted_iota(jnp.int32, a.shape, 0) % SUBLANES
    s = 1
    while s < SUBLANES:
        ok = sub >= s
        b = jnp.where(ok, a * pltpu.roll(b, s, 0) + b, b)
        a = jnp.where(ok, a * pltpu.roll(a, s, 0), a)
        s *= 2
    groups = []
    before = jnp.zeros((1, a.shape[1]), F32)
    for g in range(n // SUBLANES):
        rows = slice(g * SUBLANES, (g + 1) * SUBLANES)
        groups.append(a[rows] * before + b[rows])
        before = groups[-1][SUBLANES - 1:]
    return jnp.concatenate(groups, axis=0)


def _scan_bwd(a, b):
    n = a.shape[0]
    sub = lax.broadcasted_iota(jnp.int32, a.shape, 0) % SUBLANES
    s = 1
    while s < SUBLANES:
        ok = sub < SUBLANES - s
        b = jnp.where(ok, a * pltpu.roll(b, n - s, 0) + b, b)
        a = jnp.where(ok, a * pltpu.roll(a, n - s, 0), a)
        s *= 2
    groups = []
    after = jnp.zeros((1, a.shape[1]), F32)
    for g in reversed(range(n // SUBLANES)):
        rows = slice(g * SUBLANES, (g + 1) * SUBLANES)
        groups.append(a[rows] * after + b[rows])
        after = groups[-1][:1]
    return jnp.concatenate(groups[::-1], axis=0)


def _rglru_gates(xv, prev8, cw_ref, cb_ref, wa_ref, ba_ref, wx_ref, bx_ref, lam_ref):
    x1 = _shift_down(xv, 1, prev8)
    x2 = _shift_down(xv, 2, prev8)
    x3 = _shift_down(xv, 3, prev8)
    xc = cw_ref[3:4, :] * xv + cw_ref[2:3, :] * x1 + cw_ref[1:2, :] * x2 + cw_ref[0:1, :] * x3 + cb_ref[...]
    xcb = xc.astype(BF16)
    r = _sigmoid(_dot(xcb, wa_ref[...]) + ba_ref[...])
    ig = _sigmoid(_dot(xcb, wx_ref[...]) + bx_ref[...])
    c = RG_C * _log_sigmoid(lam_ref[...])
    la = r * c
    a = jnp.exp(la)
    m = jnp.sqrt(-_expm1_neg(2.0 * la))
    return (x1, x2, x3), xc, xcb, r, ig, c, a, m


def _rglru_fwd(proj, cw, cb, wa, ba, wx, bx, lam):
    s = proj.shape[0]
    w = D_RNN
    tm = _tile(s)

    def body(xr_ref, gate_ref, cw_ref, cb_ref, wa_ref, ba_ref, wx_ref, bx_ref, lam_ref,
             y_ref, h_ref, prev, hlast):
        @pl.when(pl.program_id(0) == 0)
        def _():
            prev[...] = jnp.zeros_like(prev)
            hlast[...] = jnp.zeros_like(hlast)

        xv = xr_ref[...]
        _, xc, _, _, ig, _, a, m = _rglru_gates(xv, prev[...], cw_ref, cb_ref, wa_ref, ba_ref,
                                                wx_ref, bx_ref, lam_ref)
        b = m * (ig * xc)
        row = lax.broadcasted_iota(jnp.int32, b.shape, 0)
        b = jnp.where(row == 0, b + a * hlast[...], b)
        h = _scan_fwd(a, b)
        h_ref[...] = h
        y_ref[...] = h * _gelu(gate_ref[...])
        prev[...] = xv[tm - 8:, :]
        hlast[...] = h[tm - 1:tm, :]

    vec = pl.BlockSpec((1, w), lambda i: (0, 0))
    sq = pl.BlockSpec((w, w), lambda i: (0, 0))
    out = pl.BlockSpec((tm, w), lambda i: (i, 0))
    return pl.pallas_call(
        body, name="rglru_fwd", grid=(s // tm,),
        in_specs=[pl.BlockSpec((tm, w), lambda i: (i, 0)), pl.BlockSpec((tm, w), lambda i: (i, 1)),
                  pl.BlockSpec((CONV_W, w), lambda i: (0, 0)), vec, sq, vec, sq, vec, vec],
        out_specs=[out, out],
        out_shape=[jax.ShapeDtypeStruct((s, w), F32), jax.ShapeDtypeStruct((s, w), F32)],
        scratch_shapes=[pltpu.VMEM((8, w), F32), pltpu.VMEM((1, w), F32)],
        compiler_params=_params(("arbitrary",)),
    )(proj, proj, cw, cb, wa, ba, wx, bx, lam)


def _rglru_bwd(proj, hseq, dyr, cw, cb, wa, ba, wx, bx, lam):
    s = proj.shape[0]
    w = D_RNN
    tm = _tile(s)
    nt = s // tm
    t8 = tm // 8

    def body(xr_ref, xp_ref, gate_ref, h_ref, hp_ref, dy_ref, cw_ref, cb_ref, wa_ref, ba_ref,
             wx_ref, bx_ref, lam_ref,
             dxr_ref, dgate_ref, dcw_ref, dcb_ref, dwa_ref, dba_ref, dwx_ref, dbx_ref, dlam_ref,
             carry, dxc_next):
        i = pl.program_id(0)
        first_tile = i == nt - 1

        @pl.when(i == 0)
        def _():
            carry[...] = jnp.zeros_like(carry)
            dxc_next[...] = jnp.zeros_like(dxc_next)
            for ref in (dcw_ref, dcb_ref, dwa_ref, dba_ref, dwx_ref, dbx_ref, dlam_ref):
                ref[...] = jnp.zeros_like(ref)

        xv = xr_ref[...]
        prev8 = jnp.where(first_tile, 0.0, xp_ref[...])
        hprev8 = jnp.where(first_tile, 0.0, hp_ref[...])
        (x1, x2, x3), xc, xcb, r, ig, c, a, m = _rglru_gates(
            xv, prev8, cw_ref, cb_ref, wa_ref, ba_ref, wx_ref, bx_ref, lam_ref)
        gv = gate_ref[...]
        hv = h_ref[...]
        dy = dy_ref[...]
        dgate_ref[...] = (dy * hv * _gelu_grad(gv)).astype(BF16)
        dh = dy * _gelu(gv)
        row = lax.broadcasted_iota(jnp.int32, dh.shape, 0)
        dh = jnp.where(row == tm - 1, dh + carry[...], dh)
        a_up = jnp.where(row == tm - 1, 0.0, pltpu.roll(a, tm - 1, 0))
        lam_t = _scan_bwd(a_up, dh)
        carry[...] = a[0:1, :] * lam_t[0:1, :]
        hm1 = _shift_down(hv, 1, hprev8)
        da = lam_t * hm1
        ixc = ig * xc
        dm = lam_t * ixc
        dig = lam_t * m * xc
        dxc = lam_t * m * ig
        dla = da * a - dm * (a * a) / m
        dr = dla * c
        dlam_ref[...] += jnp.sum(dla * r, axis=0, keepdims=True)
        dpa = dr * r * (1.0 - r)
        dpi = dig * ig * (1.0 - ig)
        dba_ref[...] += jnp.sum(dpa, axis=0, keepdims=True)
        dbx_ref[...] += jnp.sum(dpi, axis=0, keepdims=True)
        dpab = dpa.astype(BF16)
        dpib = dpi.astype(BF16)
        dwa_ref[...] += _dot_tn(xcb, dpab)
        dwx_ref[...] += _dot_tn(xcb, dpib)
        dxc = dxc + _dot_nt(dpab, wa_ref[...]) + _dot_nt(dpib, wx_ref[...])
        dcb_ref[...] += jnp.sum(dxc, axis=0, keepdims=True)
        dcw_ref[3:4, :] += jnp.sum(dxc * xv, axis=0, keepdims=True)
        dcw_ref[2:3, :] += jnp.sum(dxc * x1, axis=0, keepdims=True)
        dcw_ref[1:2, :] += jnp.sum(dxc * x2, axis=0, keepdims=True)
        dcw_ref[0:1, :] += jnp.sum(dxc * x3, axis=0, keepdims=True)
        nxt = dxc_next[...]
        dxr = (cw_ref[3:4, :] * dxc + cw_ref[2:3, :] * _shift_up(dxc, 1, nxt)
               + cw_ref[1:2, :] * _shift_up(dxc, 2, nxt) + cw_ref[0:1, :] * _shift_up(dxc, 3, nxt))
        dxr_ref[...] = dxr.astype(BF16)
        dxc_next[...] = dxc[0:8, :]

        @pl.when(first_tile)
        def _():
            lv = lam_ref[...]
            dlam_ref[...] = dlam_ref[...] * (RG_C * _sigmoid(-lv))

    rev = lambda i: nt - 1 - i
    vec = pl.BlockSpec((1, w), lambda i: (0, 0))
    sq = pl.BlockSpec((w, w), lambda i: (0, 0))
    cur = lambda col: pl.BlockSpec((tm, w), lambda i: (rev(i), col))
    before = lambda cols: pl.BlockSpec((8, w), lambda i: (jnp.maximum(rev(i) * t8 - 1, 0), 0))
    return pl.pallas_call(
        body, name="rglru_bwd", grid=(nt,),
        in_specs=[cur(0), before(None), cur(1), cur(0), before(None), cur(0),
                  pl.BlockSpec((CONV_W, w), lambda i: (0, 0)), vec, sq, vec, sq, vec, vec],
        out_specs=[cur(0), cur(0), pl.BlockSpec((CONV_W, w), lambda i: (0, 0)), vec, sq, vec, sq, vec, vec],
        out_shape=[jax.ShapeDtypeStruct((s, w), BF16), jax.ShapeDtypeStruct((s, w), BF16),
                   jax.ShapeDtypeStruct((CONV_W, w), F32), jax.ShapeDtypeStruct((1, w), F32),
                   jax.ShapeDtypeStruct((w, w), F32), jax.ShapeDtypeStruct((1, w), F32),
                   jax.ShapeDtypeStruct((w, w), F32), jax.ShapeDtypeStruct((1, w), F32),
                   jax.ShapeDtypeStruct((1, w), F32)],
        scratch_shapes=[pltpu.VMEM((1, w), F32), pltpu.VMEM((8, w), F32)],
        compiler_params=_params(("arbitrary",)),
    )(proj, proj, proj, hseq, hseq, dyr, cw, cb, wa, ba, wx, bx, lam)


def _sb_logs(z, valid):
    lb = jnp.minimum(z, 0.0) - jnp.log(1.0 + jnp.exp(-jnp.abs(z)))
    return lb, jnp.where(valid, lb - z, 0.0)


class _Window:
    def __init__(self):
        blk, win, cut = ATT_BLOCK, ATT_WINDOW, ATT_SPLIT
        self.row = lax.broadcasted_iota(jnp.int32, (blk, win), 0)
        self.col = lax.broadcasted_iota(jnp.int32, (blk, win), 1)

        def tri(n, later):
            j = lax.broadcasted_iota(jnp.int32, (n, n), 0)
            s = lax.broadcasted_iota(jnp.int32, (n, n), 1)
            return jnp.where((j > s) if later else (j < s), 1.0, 0.0).astype(BF16)

        self.later = (tri(cut, True), tri(win - cut, True))
        self.earlier = (tri(cut, False), tri(win - cut, False))

    def place(self, qi, g):
        end = (qi + 1) * ATT_BLOCK - g * ATT_WINDOW
        start = pl.multiple_of(jnp.maximum(end - ATT_WINDOW, 0), ATT_BLOCK)
        valid = self.col < jnp.minimum(self.row + (qi * ATT_BLOCK - start), end - start)
        return start, valid

    @staticmethod
    def _parts(xv):
        hi = xv.astype(BF16)
        lo = (xv - hi.astype(F32)).astype(BF16)
        cut = ATT_SPLIT
        sums = (jnp.sum(xv[:, :cut], axis=1, keepdims=True), jnp.sum(xv[:, cut:], axis=1, keepdims=True))
        return (hi[:, :cut], lo[:, :cut]), (hi[:, cut:], lo[:, cut:]), sums

    def sums_after(self, xv, carry):
        (h0, l0), (h1, l1), (s0, s1) = self._parts(xv)
        first = _dot(h0, self.later[0]) + _dot(l0, self.later[0]) + (s1 + carry)
        last = _dot(h1, self.later[1]) + _dot(l1, self.later[1]) + carry
        return jnp.concatenate([first, last], axis=1), s0 + s1

    def sums_before(self, xv, carry):
        (h0, l0), (h1, l1), (s0, s1) = self._parts(xv)
        first = _dot(h0, self.earlier[0]) + _dot(l0, self.earlier[0]) + carry
        last = _dot(h1, self.earlier[1]) + _dot(l1, self.earlier[1]) + (s0 + carry)
        return jnp.concatenate([first, last], axis=1), s0 + s1


class _HeadPair:
    def __init__(self):
        lanes = 2 * HEAD_DIM
        lane = lax.broadcasted_iota(jnp.int32, (1, lanes), 1)
        self.masks = [lane // HEAD_DIM == h for h in (0, 1)]
        i = lax.broadcasted_iota(jnp.int32, (lanes, lanes), 0) // HEAD_DIM
        j = lax.broadcasted_iota(jnp.int32, (lanes, lanes), 1) // HEAD_DIM
        self.same_head = jnp.where(i == j, 1.0, 0.0).astype(BF16)

    def only(self, h, xv):
        return jnp.where(self.masks[h], xv, jnp.zeros_like(xv))

    def merge(self, per_head):
        return jnp.where(self.masks[0], per_head[0], per_head[1])

    def mean(self, xv):
        hi = xv.astype(BF16)
        lo = (xv - hi.astype(F32)).astype(BF16)
        return (_dot(hi, self.same_head) + _dot(lo, self.same_head)) * (1.0 / HEAD_DIM)

    def rms_r(self, xv):
        return lax.rsqrt(self.mean(xv * xv) + EPS)

    def rms_bwd(self, xv, r, nw, dh):
        t = dh * nw
        dx = r * t - xv * (r * r * r * self.mean(t * xv))
        dn = jnp.sum(dh * xv * r, axis=0, keepdims=True)
        return dx, dn[:, :HEAD_DIM] + dn[:, HEAD_DIM:]


def _attn_fwd(proj, qg, kg, rider=None):
    s = proj.shape[0]
    blk, win, dh = ATT_BLOCK, ATT_WINDOW, HEAD_DIM
    nq = s // blk
    scale = 1.0 / math.sqrt(dh)
    heads = (0, 1)
    assert s >= win and s % blk == 0

    def body(*refs):
        (q_ref, k_ref, v_ref, qg_ref, kg_ref), (o_ref,), (qn, kn, vb), copies = _split_refs(refs, 5, 1, rider)
        finish = _ride(copies, pl.program_id(0) == 0, pl.program_id(0) == N_HEADS // 2 - 1)
        wd, hp = _Window(), _HeadPair()
        qv = q_ref[...]
        qn[...] = (qv * hp.rms_r(qv) * qg_ref[...] * scale).astype(BF16)
        kv = k_ref[...]
        kn[...] = (kv * hp.rms_r(kv) * kg_ref[...]).astype(BF16)
        vb[...] = v_ref[...].astype(BF16)

        def q_step(qi, _):
            qoff = pl.multiple_of(qi * blk, blk)
            qt = qn[pl.ds(qoff, blk), :]
            qts = [hp.only(h, qt) for h in heads]

            def more(carry):
                g, live = carry[:2]
                return jnp.logical_and((qi + 1) * blk - g * win > 0, live > 0)

            def window(carry):
                g, _, accs, runs = carry
                start, valid = wd.place(qi, g)
                kt = kn[pl.ds(start, win), :]
                zs = [_dot_nt(qts[h], kt) for h in heads]
                logs = [_sb_logs(z, valid) for z in zs]
                sums = [wd.sums_after(logs[h][1], runs[h]) for h in heads]
                wgts = [jnp.where(valid, jnp.exp(logs[h][0] + sums[h][0]), 0.0).astype(BF16) for h in heads]
                vt = vb[pl.ds(start, win), :]
                accs = tuple(accs[h] + _dot(wgts[h], vt) for h in heads)
                runs = tuple(runs[h] + sums[h][1] for h in heads)
                live = (jnp.maximum(jnp.max(runs[0]), jnp.max(runs[1])) > EXP_ZERO).astype(jnp.int32)
                return g + 1, live, accs, runs

            zero = lambda cols: tuple(jnp.zeros((blk, cols), F32) for _ in heads)
            _, _, accs, _ = lax.while_loop(more, window, (jnp.int32(0), jnp.int32(1), zero(2 * dh), zero(1)))
            o_ref[pl.ds(qoff, blk), :] = hp.merge(accs)
            return 0

        lax.fori_loop(0, nq, q_step, 0)
        finish()

    pair = lambda group: pl.BlockSpec((s, 2 * dh), lambda p: (0, group * (D_ATT // (2 * dh)) + p))
    vec = pl.BlockSpec((1, 2 * dh), lambda p: (0, 0))
    return _call(
        body, "attn_fwd", (N_HEADS // 2,), [pair(2), pair(3), pair(4), vec, vec], [pair(0)],
        [jax.ShapeDtypeStruct((s, D_ATT), F32)], [proj, proj, proj, jnp.tile(qg, (1, 2)), jnp.tile(kg, (1, 2))],
        scratch=[pltpu.VMEM((s, 2 * dh), BF16)] * 3, rider=rider)


def _attn_bwd(proj, dya, qg, kg, rider=None):
    s = proj.shape[0]
    blk, win, dh = ATT_BLOCK, ATT_WINDOW, HEAD_DIM
    nq = s // blk
    max_windows = -(-s // win) + 1
    scale = 1.0 / math.sqrt(dh)
    steps = N_HEADS // 2
    heads = (0, 1)
    assert s >= win and s % blk == 0

    def body(*refs):
        ins, outs, scratch, copies = _split_refs(refs, 6, 5, rider)
        q_ref, k_ref, v_ref, do_ref, qg_ref, kg_ref = ins
        dq_ref, dk_ref, dv_ref, dqg_ref, dkg_ref = outs
        qn, kn, vb, dob, runs_ref, dqn, dkn, dvn = scratch
        finish = _ride(copies, pl.program_id(0) == 0, pl.program_id(0) == steps - 1)
        wd, hp = _Window(), _HeadPair()

        @pl.when(pl.program_id(0) == 0)
        def _():
            dqg_ref[...] = jnp.zeros_like(dqg_ref)
            dkg_ref[...] = jnp.zeros_like(dkg_ref)

        qv = q_ref[...]
        qn[...] = (qv * hp.rms_r(qv) * qg_ref[...] * scale).astype(BF16)
        kv = k_ref[...]
        kn[...] = (kv * hp.rms_r(kv) * kg_ref[...]).astype(BF16)
        vb[...] = v_ref[...].astype(BF16)
        dob[...] = do_ref[...].astype(BF16)
        dkn[...] = jnp.zeros_like(dkn)
        dvn[...] = jnp.zeros_like(dvn)

        def q_step(qi, _):
            qoff = pl.multiple_of(qi * blk, blk)
            qt = qn[pl.ds(qoff, blk), :]
            dot = dob[pl.ds(qoff, blk), :]
            qts = [hp.only(h, qt) for h in heads]
            dots = [hp.only(h, dot) for h in heads]

            zero = lambda cols: tuple(jnp.zeros((blk, cols), F32) for _ in heads)

            def logs_of(g):
                start, valid = wd.place(qi, g)
                kt = kn[pl.ds(start, win), :]
                return [_sb_logs(_dot_nt(qts[h], kt), valid) for h in heads]

            def row_sums(logs):
                return tuple(jnp.sum(logs[h][1], axis=1, keepdims=True) for h in heads)

            def still_live(runs):
                return jnp.maximum(jnp.max(runs[0]), jnp.max(runs[1])) > EXP_ZERO

            def window_grads(g, logs, runs, esums):
                start, valid = wd.place(qi, g)
                kt = kn[pl.ds(start, win), :]
                vt = vb[pl.ds(start, win), :]
                dws = [_dot_nt(dots[h], vt) for h in heads]
                tails = [wd.sums_after(logs[h][1], runs[h])[0] for h in heads]
                wgts = [jnp.where(valid, jnp.exp(logs[h][0] + tails[h]), 0.0) for h in heads]
                es = [dws[h] * wgts[h] for h in heads]
                befores = [wd.sums_before(es[h], esums[h]) for h in heads]
                dzbs = []
                for h in heads:
                    beta = jnp.exp(logs[h][0])
                    dz = jnp.where(valid, es[h] * (1.0 - beta) - befores[h][0] * beta, 0.0)
                    dzbs.append(dz.astype(BF16))
                dkn[pl.ds(start, win), :] += _dot_tn(dzbs[0], qts[0]) + _dot_tn(dzbs[1], qts[1])
                dvn[pl.ds(start, win), :] += (_dot_tn(wgts[0].astype(BF16), dots[0])
                                              + _dot_tn(wgts[1].astype(BF16), dots[1]))
                return tuple(_dot(dzbs[h], kt) for h in heads), tuple(befores[h][1] for h in heads)

            logs0 = logs_of(0)
            runs1 = row_sums(logs0)

            def one_window():
                return window_grads(0, logs0, zero(1), zero(1))[0]

            def all_windows():
                def more(carry):
                    g, live = carry[:2]
                    return jnp.logical_and((qi + 1) * blk - g * win > 0, live > 0)

                def run_window(carry):
                    g, _, runs = carry
                    for h in heads:
                        runs_ref[h, g] = runs[h]
                    sums = row_sums(logs_of(g))
                    runs = tuple(runs[h] + sums[h] for h in heads)
                    return g + 1, still_live(runs).astype(jnp.int32), runs

                for h in heads:
                    runs_ref[h, 0] = jnp.zeros((blk, 1), F32)
                windows, _, _ = lax.while_loop(more, run_window, (jnp.int32(1), jnp.int32(1), runs1))

                def k_window(gg, carry):
                    dq_accs, esums = carry
                    g = windows - 1 - gg
                    parts, totals = window_grads(g, logs_of(g), [runs_ref[h, g] for h in heads], esums)
                    return (tuple(dq_accs[h] + parts[h] for h in heads),
                            tuple(esums[h] + totals[h] for h in heads))

                return lax.fori_loop(0, windows, k_window, (zero(2 * dh), zero(1)))[0]

            earlier_keys = (qi + 1) * blk - win > 0
            dq_accs = lax.cond(jnp.logical_and(earlier_keys, still_live(runs1)), all_windows, one_window)
            dqn[pl.ds(qoff, blk), :] = hp.merge(dq_accs)
            return 0

        lax.fori_loop(0, nq, q_step, 0)

        dq, dqg = hp.rms_bwd(qv, hp.rms_r(qv), qg_ref[...] * scale, dqn[...])
        dq_ref[...] = dq.astype(BF16)
        dqg_ref[...] += dqg * scale
        dk, dkg = hp.rms_bwd(kv, hp.rms_r(kv), kg_ref[...], dkn[...])
        dk_ref[...] = dk.astype(BF16)
        dkg_ref[...] += dkg
        dv_ref[...] = dvn[...].astype(BF16)
        finish()

    pair = lambda group: pl.BlockSpec((s, 2 * dh), lambda p: (0, group * (D_ATT // (2 * dh)) + p))
    vec2 = pl.BlockSpec((1, 2 * dh), lambda p: (0, 0))
    vec = pl.BlockSpec((1, dh), lambda p: (0, 0))
    return _call(
        body, "attn_bwd", (steps,), [pair(2), pair(3), pair(4), pair(0), vec2, vec2],
        [pair(0), pair(0), pair(0), vec, vec],
        [jax.ShapeDtypeStruct((s, D_ATT), BF16)] * 3 + [jax.ShapeDtypeStruct((1, dh), F32)] * 2,
        [proj, proj, proj, dya, jnp.tile(qg, (1, 2)), jnp.tile(kg, (1, 2))],
        scratch=[pltpu.VMEM((s, 2 * dh), BF16)] * 4 + [pltpu.VMEM((2, max_windows, blk, 1), F32)]
        + [pltpu.VMEM((s, 2 * dh), F32)] * 3, rider=rider)


def _block_diag(w):
    n, c, d = w.shape
    return jnp.einsum("ncd,nm->ncmd", w, jnp.eye(n, dtype=w.dtype)).reshape(n * c, n * d)


def _diag_blocks(full, n):
    c = full.shape[0] // n
    return jnp.stack([full[i * c:(i + 1) * c, i * c:(i + 1) * c] for i in range(n)])


FFN1 = ["ffn1_w_gate", "ffn1_w_up", "ffn1_w_down"]
FFN2 = ["ffn2_w_gate", "ffn2_w_up", "ffn2_w_down"]


def _pair_sums(gb, names, where):
    theirs = _pair_exchange([gb[n] for n in names], "pair_exchange_" + names[0])
    pair, own = _pair_sum([gb[n] for n in names], theirs, where, "pair_sum_" + names[0])
    return _chip_rider(pair, own)


def _local_step(x, tgt, stacks, conv_stack, small, where):
    gate_up, down = FFN1[:2], FFN1[2:]
    big = dict(zip(gate_up, _gather_weights([stacks[n] for n in gate_up], [])))
    wa = _block_diag(small["rg_w_a"]).astype(BF16)
    wx = _block_diag(small["rg_w_x"]).astype(BF16)

    whole = lambda names: [big[n].reshape(-1, D_MODEL) for n in names]
    soon, later = down + ["w_in"], FFN2 + ["w_out"]
    g1, u1, hb1, ab1, *landed = _ffn_up(x, small["ffn1_norm"], *whole(gate_up),
                                        rider=_gather_rider([stacks[n] for n in soon], [conv_stack]))
    big.update(zip(soon, landed))
    x1 = _ffn_down(x, ab1, *whole(down))
    conv_w = jnp.transpose(landed[-1], (1, 0, 2)).reshape(CONV_W, D_RNN)
    rg = (conv_w, small["conv_b"], wa, small["rg_b_a"], wx, small["rg_b_x"], small["rg_lambda"])
    proj, hb2 = _mix_pre(x1, small["mix_norm"], big["w_in"])
    yr, hseq = _rglru_fwd(proj, *rg)
    ya, *landed = _attn_fwd(proj, small["q_norm"], small["k_norm"], _gather_rider([stacks[n] for n in later], []))
    big.update(zip(later, landed))
    wout = big["w_out"].reshape(D_MODEL, D_MODEL)
    x2 = _mix_post(x1, yr, ya, small["rnn_out_norm"], small["attn_out_norm"], wout)
    dx3, g2, u2, hb3, ab3, loss = _ffn_fwd_loss(x2, small["ffn2_norm"], *whole(FFN2), tgt)

    gb, gs, slots = {}, {}, {}
    dx2, dg2, du2, dyb2, gs["ffn2_norm"] = _ffn_bwd_act(x2, small["ffn2_norm"], dx3, g2, u2, *whole(FFN2), "ffn2_bwd")
    gb["ffn2_w_gate"] = _ffn_wgrad(dg2, hb3, 1.0, "wgrad_gate_ffn2")
    gb["ffn2_w_up"] = _ffn_wgrad(du2, hb3, 1.0, "wgrad_up_ffn2")
    gb["ffn2_w_down"] = _ffn_wgrad(ab3, dyb2, 0.5, "wgrad_down_ffn2")
    dyr, dya, ycat, dxb2, gs["rnn_out_norm"], gs["attn_out_norm"] = _mix_post_bwd(
        dx2, yr, ya, small["rnn_out_norm"], small["attn_out_norm"], wout)
    gb["w_out"] = _wgrad_whole(ycat, dxb2, False, "wgrad_out")
    early = FFN2 + ["w_out"]
    dq, dk, dv, gs["q_norm"], gs["k_norm"], *done = _attn_bwd(
        proj, dya, small["q_norm"], small["k_norm"], _pair_sums(gb, early, where))
    slots.update(zip(early, done))
    dxr, dgate, gs["conv_w"], gs["conv_b"], dwa, gs["rg_b_a"], dwx, gs["rg_b_x"], gs["rg_lambda"] = _rglru_bwd(
        proj, hseq, dyr, *rg)
    gs["rg_w_a"] = _diag_blocks(dwa, RNN_BLOCKS)
    gs["rg_w_x"] = _diag_blocks(dwx, RNN_BLOCKS)
    dpb = jnp.concatenate([dxr, dgate, dq, dk, dv], axis=1)
    dx1, gs["mix_norm"] = _mix_pre_bwd(x1, small["mix_norm"], dx2, dpb, big["w_in"])
    dx0, dg1, du1, dyb1, gs["ffn1_norm"] = _ffn_bwd_act(x, small["ffn1_norm"], dx1, g1, u1, *whole(FFN1), "ffn1_bwd")

    mine = _place_shard(_pack([gs[n] for n in SMALL] + [loss[:, :1]]), where, F32, "place_small_grads",
                        by_device=True)
    gb["ffn1_w_gate"], everyone = _ffn_wgrad(dg1, hb1, 1.0, "wgrad_gate_ffn1", _small_rider(mine))
    gb["ffn1_w_up"], slots["ffn1_w_gate"] = _ffn_wgrad(
        du1, hb1, 1.0, "wgrad_up_ffn1", _pair_sums(gb, ["ffn1_w_gate"], where))
    gb["ffn1_w_down"], slots["ffn1_w_up"] = _ffn_wgrad(
        ab1, dyb1, 0.5, "wgrad_down_ffn1", _pair_sums(gb, ["ffn1_w_up"], where))
    gb["w_in"], slots["ffn1_w_down"] = _wgrad_whole(
        hb2, dpb, True, "wgrad_in", _pair_sums(gb, ["ffn1_w_down"], where))
    last = _pair_sums(gb, ["w_in"], where)
    slots["w_in"], = _chip_exchange(last.plain, last.inplace)
    return dx0, slots, gs, everyone


ANY = pl.BlockSpec(memory_space=pl.ANY)


def _place():
    x, y, c = lax.axis_index("x"), lax.axis_index("y"), lax.axis_index("c")
    other_chips = [(1 - x, y), (x, 1 - y), (1 - x, 1 - y)]
    return x, y, c, 2 * x + y, other_chips


def _remote(src, dst, send_sem, recv_sem, to):
    return pltpu.make_async_remote_copy(src_ref=src, dst_ref=dst, send_sem=send_sem, recv_sem=recv_sem,
                                        device_id=to, device_id_type=MESH)


def _copy_plan(pairs):
    sends = [functools.partial(_remote, *a) for a, _ in pairs]
    arrivals = [functools.partial(_remote, *b) for _, b in pairs]
    return sends, arrivals


class _Rider:
    def __init__(self, plan, plain, inplace, n_copies=None, relay=None, n_relay=0):
        self.plan, self.plain, self.inplace = plan, list(plain), list(inplace)
        self.n_copies = n_copies or 3 * len(self.inplace)
        self.relay, self.n_relay = relay, n_relay

    def operands(self):
        return self.plain + self.inplace

    def out_shape(self):
        return [jax.ShapeDtypeStruct(a.shape, a.dtype) for a in self.inplace]

    def aliases(self, inputs_before, outputs_before):
        return {inputs_before + len(self.plain) + k: outputs_before + k for k in range(len(self.inplace))}

    def scratch(self):
        relay = [pltpu.SemaphoreType.DMA((self.n_relay,))] * 2 if self.relay else []
        return [pltpu.SemaphoreType.DMA((self.n_copies,))] * 2 + relay


def _split_refs(refs, n_in, n_out, rider):
    if rider is None:
        return refs[:n_in], refs[n_in:n_in + n_out], refs[n_in + n_out:], None
    r_in, r_out = len(rider.operands()), len(rider.inplace)
    outs_at = n_in + r_in
    n_sems = len(rider.scratch())
    rest = refs[outs_at + n_out + r_out:]
    sems = rest[len(rest) - n_sems:]
    filled = refs[outs_at + n_out:outs_at + n_out + r_out]
    copies = functools.partial(rider.plan, refs[n_in:n_in + len(rider.plain)], filled, *sems[:2])
    relay = functools.partial(rider.relay, filled, *sems[2:]) if rider.relay else None
    return refs[:n_in], refs[outs_at:outs_at + n_out], rest[:len(rest) - n_sems], (copies, relay)


def _ride(copies, first, last, middle=None):
    if copies is None:
        return lambda: None
    copies, relay = copies

    @pl.when(first)
    def _():
        _start(copies()[0])

    def start_relay():
        for make in copies()[1]:
            make().wait_recv()
        _start(relay()[0])

    if relay is not None and middle is not None:
        pl.when(middle)(start_relay)

    def finish():
        @pl.when(last)
        def _():
            if relay is None:
                _finish(*copies())
            else:
                if middle is None:
                    start_relay()
                _finish(copies()[0] + relay()[0], relay()[1])

    return finish


def _gather_rider(split, whole):
    n_split = len(split)
    return _Rider(lambda plain, stacks, ss, rs: _gather_ici(stacks, n_split, ss, rs), [], list(split) + list(whole),
                  relay=lambda stacks, ss, rs: _gather_d2d(stacks[:n_split], ss, rs), n_relay=3 * n_split)


def _chip_rider(sums, slots):
    return _Rider(_chip_copies, sums, slots)


def _start(makers):
    for make in makers:
        make().start()


def _finish(sends, arrivals):
    for make in arrivals:
        make().wait_recv()
    for make in sends:
        make().wait_send()


def _half(rows, c):
    return pl.ds(pl.multiple_of(c * rows, 16), rows)


def _gather_weights(split, whole):
    arrs = list(split) + list(whole)
    n, ns = len(arrs), len(split)

    def body(*refs):
        outs = refs[n:2 * n]
        send_sems, recv_sems, fsend_sems, frecv_sems = refs[2 * n:]
        sends, arrivals = _gather_ici(outs, ns, send_sems, recv_sems)
        passes, passed = _gather_d2d(outs[:ns], fsend_sems, frecv_sems)
        _start(sends)
        for k, make in enumerate(arrivals):
            make().wait_recv()
            if k < 3 * ns:
                passes[k]().start()
        _finish(sends + passes, passed)

    return pl.pallas_call(
        body, name="gather_weights",
        in_specs=[ANY] * n, out_specs=[ANY] * n,
        out_shape=[jax.ShapeDtypeStruct(a.shape, a.dtype) for a in arrs],
        input_output_aliases={i: i for i in range(n)},
        scratch_shapes=[pltpu.SemaphoreType.DMA((3 * n,)), pltpu.SemaphoreType.DMA((3 * n,)),
                        pltpu.SemaphoreType.DMA((3 * ns,)), pltpu.SemaphoreType.DMA((3 * ns,))],
    )(*arrs)


def _gather_ici(stacks, n_split, send_sems, recv_sems):
    x, y, c, me, chips = _place()

    def region(i, chip):
        if i < n_split:
            return stacks[i].at[chip, _half(stacks[i].shape[1] // 2, c)]
        return stacks[i].at[chip]

    pairs = []
    for i in range(len(stacks)):
        for p, (cx, cy) in enumerate(chips):
            k = 3 * i + p
            mine, got = region(i, me), region(i, 2 * cx + cy)
            sems, to = (send_sems.at[k], recv_sems.at[k]), (cx, cy, c)
            pairs.append(((mine, mine, *sems, to), (got, got, *sems, to)))
    return _copy_plan(pairs)


def _gather_d2d(stacks, send_sems, recv_sems):
    x, y, c, _, chips = _place()
    sibling = (x, y, 1 - c)
    pairs = []
    for i, stack in enumerate(stacks):
        rows = stack.shape[1] // 2
        for p, (cx, cy) in enumerate(chips):
            k = 3 * i + p
            got, theirs = stack.at[2 * cx + cy, _half(rows, c)], stack.at[2 * cx + cy, _half(rows, 1 - c)]
            sems = (send_sems.at[k], recv_sems.at[k])
            pairs.append(((got, got, *sems, sibling), (theirs, theirs, *sems, sibling)))
    return _copy_plan(pairs)


def _pair_exchange(grads, name):
    n = len(grads)

    def body(*refs):
        ins, theirs = refs[:n], refs[n:2 * n]
        send_sems, recv_sems = refs[2 * n:]
        x, y, c, _, _ = _place()
        sibling = (x, y, 1 - c)
        sends = [_remote(ins[k].at[:, _half(grads[k].shape[1] // 2, 1 - c)], theirs[k],
                         send_sems.at[k], recv_sems.at[k], sibling) for k in range(n)]
        for cp in sends:
            cp.start()
        for k in range(n):
            _remote(theirs[k], theirs[k], send_sems.at[k], recv_sems.at[k], sibling).wait_recv()
        for cp in sends:
            cp.wait_send()

    return pl.pallas_call(
        body, name=name,
        in_specs=[ANY] * n, out_specs=[ANY] * n,
        out_shape=[jax.ShapeDtypeStruct((g.shape[0], g.shape[1] // 2, g.shape[2]), g.dtype) for g in grads],
        scratch_shapes=[pltpu.SemaphoreType.DMA((n,))] * 2,
    )(*grads)


def _chip_exchange(sums, slots):
    n = len(sums)

    def body(*refs):
        sends, arrivals = _chip_copies(refs[:n], refs[2 * n:3 * n], *refs[3 * n:])
        _start(sends)
        _finish(sends, arrivals)

    return pl.pallas_call(
        body, name="grad_chip_exchange",
        in_specs=[ANY] * (2 * n), out_specs=[ANY] * n,
        out_shape=[jax.ShapeDtypeStruct(a.shape, a.dtype) for a in slots],
        input_output_aliases={n + k: k for k in range(n)},
        scratch_shapes=[pltpu.SemaphoreType.DMA((3 * n,)), pltpu.SemaphoreType.DMA((3 * n,))],
    )(*sums, *slots)


def _chip_copies(sums, slots, send_sems, recv_sems):
    x, y, c, me, chips = _place()
    pairs = []
    for k in range(len(sums)):
        for p, (cx, cy) in enumerate(chips):
            j = 3 * k + p
            got = slots[k].at[2 * cx + cy]
            sems, to = (send_sems.at[j], recv_sems.at[j]), (cx, cy, c)
            pairs.append(((sums[k].at[2 * cx + cy], slots[k].at[me], *sems, to), (got, got, *sems, to)))
    return _copy_plan(pairs)


def _half_swap(halves):
    n = len(halves)

    def body(*refs):
        outs = refs[n:2 * n]
        send_sems, recv_sems = refs[2 * n:]
        x, y, c, _, _ = _place()
        sibling = (x, y, 1 - c)
        sends = [_remote(outs[k].at[c], outs[k].at[c], send_sems.at[k], recv_sems.at[k], sibling) for k in range(n)]
        for cp in sends:
            cp.start()
        for k in range(n):
            got = outs[k].at[1 - c]
            _remote(got, got, send_sems.at[k], recv_sems.at[k], sibling).wait_recv()
        for cp in sends:
            cp.wait_send()

    return pl.pallas_call(
        body, name="grad_half_swap",
        in_specs=[ANY] * n, out_specs=[ANY] * n,
        out_shape=[jax.ShapeDtypeStruct(a.shape, a.dtype) for a in halves],
        input_output_aliases={k: k for k in range(n)},
        scratch_shapes=[pltpu.SemaphoreType.DMA((n,))] * 2,
    )(*halves)


def _small_rider(stack):
    n_dev = 2 * N_CHIPS

    def plan(_, stacks, send_sems, recv_sems):
        x, y, c, _, _ = _place()
        mine = stacks[0].at[4 * x + 2 * y + c]
        pairs = []
        for k in range(1, n_dev):
            px, py, pc = x ^ ((k >> 2) & 1), y ^ ((k >> 1) & 1), c ^ (k & 1)
            got = stacks[0].at[4 * px + 2 * py + pc]
            sems = (send_sems.at[k - 1], recv_sems.at[k - 1])
            pairs.append(((mine, mine, *sems, (px, py, pc)), (got, got, *sems, (px, py, pc))))
        return _copy_plan(pairs)

    return _Rider(plan, [], [stack], n_dev - 1)


def _row_tile(r):
    return r // 4 if r >= 256 and (r // 4) % 16 == 0 else r


def _prefetch_call(body, name, grid, in_specs, out_specs, out_shape):
    spec = pltpu.PrefetchScalarGridSpec(num_scalar_prefetch=1, grid=grid, in_specs=in_specs, out_specs=out_specs)
    return pl.pallas_call(body, name=name, grid_spec=spec, out_shape=out_shape,
                          compiler_params=_params(("arbitrary",) * len(grid)))


def _place_shard(w2d, where, dtype, name, by_device=False):
    r, c = w2d.shape
    tr = _row_tile(r)
    slots = 2 * N_CHIPS if by_device else N_CHIPS
    slot = (lambda s: 2 * s[1] + s[0]) if by_device else (lambda s: s[1])

    def body(where_ref, w_ref, out_ref):
        out_ref[...] = w_ref[...].astype(dtype)

    return _prefetch_call(
        body, name, (r // tr,), [pl.BlockSpec((tr, c), lambda i, s: (i, 0))],
        pl.BlockSpec((None, tr, c), lambda i, s: (slot(s), i, 0)),
        jax.ShapeDtypeStruct((slots, r, c), dtype))(where, w2d)


def _place_shards(w2ds, where, name):
    n = len(w2ds)
    steps = N_CHIPS
    assert all(w.shape[0] % (16 * steps) == 0 for w in w2ds)

    def body(where_ref, *refs):
        for k in range(n):
            refs[n + k][...] = refs[k][...].astype(BF16)

    tile = lambda w: (w.shape[0] // steps, w.shape[1])
    return _prefetch_call(
        body, name, (steps,), [pl.BlockSpec(tile(w), lambda i, s: (i, 0)) for w in w2ds],
        [pl.BlockSpec((None,) + tile(w), lambda i, s: (s[1], i, 0)) for w in w2ds],
        [jax.ShapeDtypeStruct((N_CHIPS,) + w.shape, BF16) for w in w2ds])(where, *w2ds)


def _pair_sum(fulls, theirs, where, name):
    n = len(fulls)

    def body(where_ref, *refs):
        for k in range(n):
            a_ref, b_ref, out_ref, own_ref = refs[k], refs[n + k], refs[2 * n + k], refs[3 * n + k]
            total = (a_ref[...].astype(F32) + b_ref[...].astype(F32)).astype(BF16)
            out_ref[...] = total

            @pl.when(pl.program_id(0) == where_ref[1])
            def _():
                own_ref[...] = total

    half = lambda t: pl.BlockSpec((None,) + t.shape[1:], lambda j, s: (j, s[0], 0))
    blk = lambda t: pl.BlockSpec((None,) + t.shape[1:], lambda j, s: (j, 0, 0))
    own = lambda t: pl.BlockSpec((None,) + t.shape[1:], lambda j, s: (s[1], 0, 0))
    shapes = [jax.ShapeDtypeStruct(t.shape, BF16) for t in theirs]
    outs = _prefetch_call(
        body, name, (N_CHIPS,), [half(t) for t in theirs] + [blk(t) for t in theirs],
        [blk(t) for t in theirs] + [own(t) for t in theirs], shapes + shapes)(where, *fulls, *theirs)
    return outs[:n], outs[n:]


def _chip_sum(slots, where, name):
    n = len(slots)
    steps = 2
    assert all(a.shape[1] % (16 * steps) == 0 for a in slots)

    def body(where_ref, *refs):
        for k in range(n):
            a_ref, out_ref = refs[k], refs[n + k]
            total = a_ref[0].astype(F32)
            for j in range(1, a_ref.shape[0]):
                total = total + a_ref[j].astype(F32)
            out_ref[...] = total

    tile = lambda a: (a.shape[1] // steps, a.shape[2])
    return _prefetch_call(
        body, name, (steps,), [pl.BlockSpec((a.shape[0],) + tile(a), lambda i, s: (0, i, 0)) for a in slots],
        [pl.BlockSpec((None,) + tile(a), lambda i, s: (s[0], i, 0)) for a in slots],
        [jax.ShapeDtypeStruct((2,) + a.shape[1:], F32) for a in slots])(where, *slots)


def _slot_sum(a, name):
    nb, r, c = a.shape
    tr = _row_tile(r)

    def body(a_ref, out_ref):
        total = a_ref[0].astype(F32)
        for j in range(1, nb):
            total = total + a_ref[j].astype(F32)
        out_ref[...] = total

    return pl.pallas_call(
        body, name=name, grid=(r // tr,),
        in_specs=[pl.BlockSpec((nb, tr, c), lambda i: (0, i, 0))],
        out_specs=pl.BlockSpec((tr, c), lambda i: (i, 0)),
        out_shape=jax.ShapeDtypeStruct((r, c), F32), compiler_params=_params(("arbitrary",)),
    )(a)


def _adamw(ws, gs, ms, vs, name, steps=1):
    n = len(ws)
    c1 = 1.0 - ADAM_B1 ** ADAM_STEP
    c2 = 1.0 - ADAM_B2 ** ADAM_STEP
    assert all(w.shape[0] % steps == 0 and (steps == 1 or w.shape[0] // steps % 8 == 0) for w in ws)

    def body(*refs):
        for k in range(n):
            w_ref, g_ref, m_ref, v_ref = (refs[j * n + k] for j in range(4))
            d_ref, m2_ref, v2_ref = (refs[(4 + j) * n + k] for j in range(3))
            gv = g_ref[...]
            m2 = ADAM_B1 * m_ref[...] + (1.0 - ADAM_B1) * gv
            v2 = ADAM_B2 * v_ref[...] + (1.0 - ADAM_B2) * (gv * gv)
            m2_ref[...] = m2
            v2_ref[...] = v2
            d_ref[...] = -ADAM_LR * ((m2 / c1) / (jnp.sqrt(v2 / c2) + ADAM_EPS) + ADAM_WD * w_ref[...])

    blks = [pl.BlockSpec((w.shape[0] // steps, w.shape[1]), lambda i: (i, 0)) for w in ws]
    shapes = [jax.ShapeDtypeStruct(w.shape, F32) for w in ws]
    outs = pl.pallas_call(
        body, name=name, grid=(steps,), in_specs=blks * 4, out_specs=blks * 3, out_shape=shapes * 3,
        compiler_params=_params(("arbitrary",)),
    )(*ws, *gs, *ms, *vs)
    return outs[:n], outs[n:2 * n], outs[2 * n:]


WEIGHTS = ["ffn1_norm", "ffn1_w_gate", "ffn1_w_up", "ffn1_w_down", "mix_norm", "w_in", "conv_w", "conv_b",
           "rg_w_a", "rg_b_a", "rg_w_x", "rg_b_x", "rg_lambda", "q_norm", "k_norm", "rnn_out_norm",
           "attn_out_norm", "w_out", "ffn2_norm", "ffn2_w_gate", "ffn2_w_up", "ffn2_w_down"]
BIG = ["ffn1_w_gate", "ffn1_w_up", "ffn1_w_down", "w_in", "w_out", "ffn2_w_gate", "ffn2_w_up", "ffn2_w_down"]
SMALL = [n for n in WEIGHTS if n not in BIG]
PACK_LANES = 128
PACK_ROW_ALIGN = 8


def _hidden_major(name, a):
    return jnp.transpose(a) if name.endswith(("w_gate", "w_up")) else a


def _pack(parts):
    flat = jnp.concatenate([p.reshape(-1) for p in parts])
    unit = PACK_LANES * PACK_ROW_ALIGN
    padded = -(-flat.shape[0] // unit) * unit
    return jnp.pad(flat, (0, padded - flat.shape[0])).reshape(-1, PACK_LANES)


def _unpack(packed, shapes):
    flat = packed.reshape(-1)
    out, at = [], 0
    for shp in shapes:
        size = math.prod(shp)
        out.append(flat[at:at + size].reshape(shp))
        at += size
    return out


def kernel(x, ffn1_norm, ffn1_w_gate, ffn1_w_up, ffn1_w_down, mix_norm, w_in, conv_w, conv_b, rg_w_a, rg_b_a, rg_w_x, rg_b_x, rg_lambda, q_norm, k_norm, rnn_out_norm, attn_out_norm, w_out, ffn2_norm, ffn2_w_gate, ffn2_w_up, ffn2_w_down, loss_target, m_ffn1_norm, m_ffn1_w_gate, m_ffn1_w_up, m_ffn1_w_down, m_mix_norm, m_w_in, m_conv_w, m_conv_b, m_rg_w_a, m_rg_b_a, m_rg_w_x, m_rg_b_x, m_rg_lambda, m_q_norm, m_k_norm, m_rnn_out_norm, m_attn_out_norm, m_w_out, m_ffn2_norm, m_ffn2_w_gate, m_ffn2_w_up, m_ffn2_w_down, v_ffn1_norm, v_ffn1_w_gate, v_ffn1_w_up, v_ffn1_w_down, v_mix_norm, v_w_in, v_conv_w, v_conv_b, v_rg_w_a, v_rg_b_a, v_rg_w_x, v_rg_b_x, v_rg_lambda, v_q_norm, v_k_norm, v_rnn_out_norm, v_attn_out_norm, v_w_out, v_ffn2_norm, v_ffn2_w_gate, v_ffn2_w_up, v_ffn2_w_down):
    given = dict(locals())
    w = {n: given[n] for n in WEIGHTS}
    m = {n: given["m_" + n] for n in WEIGHTS}
    v = {n: given["v_" + n] for n in WEIGHTS}
    chip = 2 * lax.axis_index("x") + lax.axis_index("y")

    where = jnp.stack([lax.axis_index("c"), chip]).astype(jnp.int32)

    stacks = dict(zip(BIG, _place_shards([_hidden_major(n, w[n][0]) for n in BIG], where, "place_weights")))
    conv_stack = _place_shard(w["conv_w"][0], where, F32, "place_conv_w")
    small = {n: (w[n][0] if w[n].ndim > 2 else w[n]) for n in SMALL if n != "conv_w"}

    grad_x, slots, gs, everyone = _local_step(x[0], loss_target[0], stacks, conv_stack, small, where)

    swapped = _half_swap(_chip_sum([slots[n] for n in BIG], where, "chip_sums"))
    g2s = [t.reshape(t.shape[0] * t.shape[1], t.shape[2]) for t in swapped]
    flat = lambda tree: [_hidden_major(n, tree[n][0]) for n in BIG]
    d2s, m2s, v2s = _adamw(flat(w), g2s, flat(m), flat(v), "adamw_weights", ADAMW_STEPS)
    grads, deltas, new_m, new_v = {}, {}, {}, {}
    for tree, parts in ((grads, g2s), (deltas, d2s), (new_m, m2s), (new_v, v2s)):
        tree.update({n: _hidden_major(n, a).reshape(w[n].shape) for n, a in zip(BIG, parts)})

    full_shapes = [gs[n].shape for n in SMALL]
    *summed, loss = _unpack(_slot_sum(everyone, "small_grad_sum"), full_shapes + [(1, 1)])
    g_parts = dict(zip(SMALL, summed))
    quarter = D_RNN // N_CHIPS
    g_parts["conv_w"] = lax.dynamic_slice_in_dim(g_parts["conv_w"], chip * quarter, quarter, axis=1)
    local_shapes = [w[n].shape for n in SMALL]
    pk = lambda tree: _pack([tree[n] for n in SMALL])
    (d_s,), (m_s,), (v_s,) = _adamw([pk(w)], [pk(g_parts)], [pk(m)], [pk(v)], "adamw_small")
    for tree, packed in ((grads, pk(g_parts)), (deltas, d_s), (new_m, m_s), (new_v, v_s)):
        tree.update(zip(SMALL, _unpack(packed, local_shapes)))

    return (loss[0, 0], grad_x.reshape(x.shape), *[grads[n] for n in WEIGHTS], *[deltas[n] for n in WEIGHTS],
            *[new_m[n] for n in WEIGHTS], *[new_v[n] for n in WEIGHTS])
```

```python
import functools
import math

import jax
import jax.numpy as jnp
from jax import lax
from jax.experimental import pallas as pl
from jax.experimental.pallas import tpu as pltpu

F32 = jnp.float32
BF16 = jnp.bfloat16
MESH = pl.DeviceIdType.MESH

D_MODEL = 1024
N_CHIPS = 4
D_RNN = 512
D_ATT = 512
N_HEADS = 8
HEAD_DIM = 64
RNN_BLOCKS = 8
CONV_W = 4
RG_C = 8.0
N_IN = 2 * D_RNN + 3 * D_ATT
EPS = 1e-6
ATT_BLOCK = 128
ATT_WINDOW = 384
ATT_SPLIT = 256
EXP_ZERO = -105.0

ADAM_LR = 0.001
ADAM_B1 = 0.9
ADAM_B2 = 0.999
ADAM_EPS = 1e-08
ADAM_WD = 0.01
ADAM_STEP = 10

V7X_VMEM_LIMIT = 56 * 1024 * 1024
V7X_MXU_WIDTH = 256
TOKEN_TILE = 512
SUBLANES = 8
BF16_ROWS = 16
FFN_TILE = 256
WGRAD_TILE = 2048
WHOLE_TILE = 1024
ADAMW_STEPS = 8

GELU_K0 = math.sqrt(2.0 / math.pi)
GELU_K1 = 0.044715


def _params(sem=None):
    return pltpu.CompilerParams(dimension_semantics=sem, vmem_limit_bytes=V7X_VMEM_LIMIT)


def _dot(a, b):
    return jnp.dot(a, b, preferred_element_type=F32)


def _dot_nt(a, b):
    return lax.dot_general(a, b, (((1,), (1,)), ((), ())), preferred_element_type=F32)


def _dot_tn(a, b):
    return lax.dot_general(a, b, (((0,), (0,)), ((), ())), preferred_element_type=F32)


def _sigmoid(x):
    return 1.0 / (1.0 + jnp.exp(-x))


def _rms_r(xv):
    return lax.rsqrt(jnp.mean(xv * xv, axis=-1, keepdims=True) + EPS)


def _rms_bwd(xv, r, nw, dh):
    t = dh * nw
    dx = r * t - xv * (r * r * r * jnp.mean(t * xv, axis=-1, keepdims=True))
    dn = jnp.sum(dh * xv * r, axis=0, keepdims=True)
    return dx, dn


def _gelu(x):
    t = jnp.tanh(GELU_K0 * (x + GELU_K1 * x * x * x))
    return 0.5 * x * (1.0 + t)


def _gelu_grad(x):
    t = jnp.tanh(GELU_K0 * (x + GELU_K1 * x * x * x))
    return 0.5 * (1.0 + t) + 0.5 * x * (1.0 - t * t) * (GELU_K0 * (1.0 + 3.0 * GELU_K1 * x * x))


def _expm1_neg(x):
    p = 1.0 + x * (1.0 / 6.0)
    for k in (5.0, 4.0, 3.0, 2.0):
        p = 1.0 + x * (1.0 / k) * p
    return jnp.where(x > -0.25, x * p, jnp.exp(x) - 1.0)


def _log_sigmoid(x):
    return jnp.minimum(x, 0.0) - jnp.log(1.0 + jnp.exp(-jnp.abs(x)))


def _tile(s):
    return min(TOKEN_TILE, s)


def _ffn_chunks(f):
    cut = f // 2 // V7X_MXU_WIDTH * V7X_MXU_WIDTH
    return ((0, cut), (cut, f)) if 0 < cut < f else ((0, f),)


def _ffn_fwd_loss(x, nw, wg, wu, wd, tgt):
    s, d = x.shape
    f = wg.shape[0]
    tm = min(FFN_TILE, s)
    ni = s // tm
    assert s % tm == 0

    def body(x_ref, nw_ref, wg_ref, wu_ref, wd_ref, tgt_ref, out_ref, g_ref, u_ref, hb_ref, ab_ref, loss_ref):
        i = pl.program_id(0)
        xv = x_ref[...]
        hb = (xv * _rms_r(xv) * nw_ref[...]).astype(BF16)
        hb_ref[...] = hb
        y = jnp.zeros((tm, d), F32)
        for lo, hi in _ffn_chunks(f):
            g = _dot_nt(hb, wg_ref[lo:hi, :])
            u = _dot_nt(hb, wu_ref[lo:hi, :])
            g_ref[:, lo:hi] = g.astype(BF16)
            u_ref[:, lo:hi] = u.astype(BF16)
            ab = (g * _sigmoid(g) * u).astype(BF16)
            ab_ref[:, lo:hi] = ab
            y = y + _dot(ab, wd_ref[lo:hi, :])
        diff = xv + 0.5 * y - tgt_ref[...]
        out_ref[...] = diff * (1.0 / d)

        @pl.when(i == 0)
        def _():
            loss_ref[...] = jnp.zeros_like(loss_ref)

        loss_ref[...] += jnp.sum(diff * diff) * (0.5 / d)

    row = pl.BlockSpec((tm, d), lambda i: (i, 0))
    weight = pl.BlockSpec((f, d), lambda i: (0, 0), pipeline_mode=pl.Buffered(1))
    blk = pl.BlockSpec((tm, f), lambda i: (i, 0))
    wide = jax.ShapeDtypeStruct((s, f), BF16)
    return _call(body, "ffn_fwd_loss", (ni,),
                 [row, pl.BlockSpec((1, d), lambda i: (0, 0)), weight, weight, weight, row],
                 [row, blk, blk, row, blk, pl.BlockSpec((1, 128), lambda i: (0, 0))],
                 [jax.ShapeDtypeStruct((s, d), F32), wide, wide, jax.ShapeDtypeStruct((s, d), BF16), wide,
                  jax.ShapeDtypeStruct((1, 128), F32)], [x, nw, wg, wu, wd, tgt])


def _ffn_up(x, nw, wg, wu, rider=None):
    s, d = x.shape
    f = wg.shape[0]
    tm = min(FFN_TILE, s)
    ni = s // tm
    assert s % tm == 0

    def body(*refs):
        (x_ref, nw_ref, wg_ref, wu_ref), (g_ref, u_ref, hb_ref, ab_ref), _, copies = _split_refs(refs, 4, 4, rider)
        i = pl.program_id(0)
        finish = _ride(copies, i == 0, i == ni - 1)
        xv = x_ref[...]
        hb = (xv * _rms_r(xv) * nw_ref[...]).astype(BF16)
        hb_ref[...] = hb
        for lo, hi in _ffn_chunks(f):
            g = _dot_nt(hb, wg_ref[lo:hi, :])
            u = _dot_nt(hb, wu_ref[lo:hi, :])
            g_ref[:, lo:hi] = g.astype(BF16)
            u_ref[:, lo:hi] = u.astype(BF16)
            ab_ref[:, lo:hi] = (g * _sigmoid(g) * u).astype(BF16)
        finish()

    row = pl.BlockSpec((tm, d), lambda i: (i, 0))
    weight = pl.BlockSpec((f, d), lambda i: (0, 0), pipeline_mode=pl.Buffered(1))
    blk = pl.BlockSpec((tm, f), lambda i: (i, 0))
    wide = jax.ShapeDtypeStruct((s, f), BF16)
    return _call(body, "ffn_up", (ni,), [row, pl.BlockSpec((1, d), lambda i: (0, 0)), weight, weight],
                 [blk, blk, row, blk], [wide, wide, jax.ShapeDtypeStruct((s, d), BF16), wide], [x, nw, wg, wu],
                 rider=rider)


def _ffn_down(x, ab, wd):
    s, d = x.shape
    f = wd.shape[0]
    tm = _tile(s)
    assert s % tm == 0

    def body(x_ref, ab_ref, wd_ref, out_ref):
        out_ref[...] = x_ref[...] + 0.5 * _dot(ab_ref[...], wd_ref[...])

    row = pl.BlockSpec((tm, d), lambda i: (i, 0))
    return _call(body, "ffn_down", (s // tm,),
                 [row, pl.BlockSpec((tm, f), lambda i: (i, 0)),
                  pl.BlockSpec((f, d), lambda i: (0, 0), pipeline_mode=pl.Buffered(1))],
                 [row], [jax.ShapeDtypeStruct((s, d), F32)], [x, ab, wd])[0]


def _call(body, name, grid, in_specs, out_specs, out_shape, args, scratch=(), rider=None):
    in_specs, out_specs, out_shape, scratch = list(in_specs), list(out_specs), list(out_shape), list(scratch)
    extra, aliases = [], {}
    if rider is not None:
        extra = rider.operands()
        aliases = rider.aliases(len(args), len(out_shape))
        in_specs += [ANY] * len(extra)
        out_specs += [ANY] * len(rider.inplace)
        out_shape += rider.out_shape()
        scratch += rider.scratch()
    return pl.pallas_call(
        body, name=name, grid=grid, in_specs=in_specs, out_specs=out_specs, out_shape=out_shape,
        input_output_aliases=aliases, scratch_shapes=scratch,
        compiler_params=_params(("arbitrary",) * len(grid)),
    )(*args, *extra)


def _ffn_bwd_act(x, nw, dy, g, u, wg, wu, wd, name):
    s, d = x.shape
    f = wg.shape[0]
    tm = min(FFN_TILE, s)
    assert s % tm == 0

    def body(x_ref, nw_ref, dy_ref, g_ref, u_ref, wg_ref, wu_ref, wd_ref,
             dx_ref, dg_ref, du_ref, dyb_ref, dnw_ref):
        dyv = dy_ref[...]
        dyb = dyv.astype(BF16)
        dyb_ref[...] = dyb
        dh = jnp.zeros((tm, d), F32)
        for lo, hi in _ffn_chunks(f):
            da = 0.5 * _dot_nt(dyb, wd_ref[lo:hi, :])
            gv = g_ref[:, lo:hi].astype(F32)
            sg = _sigmoid(gv)
            dub = (da * (gv * sg)).astype(BF16)
            dgb = (da * u_ref[:, lo:hi].astype(F32) * (sg * (1.0 + gv * (1.0 - sg)))).astype(BF16)
            dg_ref[:, lo:hi] = dgb
            du_ref[:, lo:hi] = dub
            dh = dh + _dot(dgb, wg_ref[lo:hi, :]) + _dot(dub, wu_ref[lo:hi, :])
        xv = x_ref[...]
        dx, dn = _rms_bwd(xv, _rms_r(xv), nw_ref[...], dh)
        dx_ref[...] = dyv + dx

        @pl.when(pl.program_id(0) == 0)
        def _():
            dnw_ref[...] = jnp.zeros_like(dnw_ref)

        dnw_ref[...] += dn

    row = pl.BlockSpec((tm, d), lambda i: (i, 0))
    vec = pl.BlockSpec((1, d), lambda i: (0, 0))
    blk = pl.BlockSpec((tm, f), lambda i: (i, 0))
    weight = pl.BlockSpec((f, d), lambda i: (0, 0), pipeline_mode=pl.Buffered(1))
    return _call(
        body, name, (s // tm,), [row, vec, row, blk, blk, weight, weight, weight], [row, blk, blk, row, vec],
        [jax.ShapeDtypeStruct((s, d), F32), jax.ShapeDtypeStruct((s, f), BF16),
         jax.ShapeDtypeStruct((s, f), BF16), jax.ShapeDtypeStruct((s, d), BF16),
         jax.ShapeDtypeStruct((1, d), F32)],
        [x, nw, dy, g, u, wg, wu, wd])


def _wgrad(a, b, a_spec, b_spec, out_rows, out_cols, scale, name, tk, rider=None, per_step=1):
    s = a.shape[-2]
    nk = s // tk
    steps = N_CHIPS // per_step
    assert s % tk == 0

    def body(*refs):
        (a_ref, b_ref), (out_ref,), (acc,), copies = _split_refs(refs, 2, 1, rider)
        j, k = pl.program_id(0), pl.program_id(1)
        finish = _ride(copies, jnp.logical_and(j == 0, k == 0), jnp.logical_and(j == steps - 1, k == nk - 1))

        @pl.when(k == 0)
        def _():
            acc[...] = jnp.zeros_like(acc)

        acc[...] += _dot_tn(a_ref[...], b_ref[...])

        @pl.when(k == nk - 1)
        def _():
            for t in range(per_step):
                out_ref[t] = (acc[t * out_rows:(t + 1) * out_rows, :] * scale).astype(BF16)

        finish()

    outs = _call(
        body, name, (steps, nk), [a_spec(tk), b_spec(tk)],
        [pl.BlockSpec((per_step, out_rows, out_cols), lambda j, k: (j, 0, 0))],
        [jax.ShapeDtypeStruct((N_CHIPS, out_rows, out_cols), BF16)], [a, b],
        scratch=[pltpu.VMEM((per_step * out_rows, out_cols), F32)], rider=rider)
    return outs[0] if rider is None else outs


def _wgrad_whole(a, b, col_blocks, name, rider=None):
    s, m = a.shape
    n = b.shape[1]
    tk = min(WHOLE_TILE, s)
    nk = s // tk
    assert s % tk == 0
    out_shape = (N_CHIPS, m, n // N_CHIPS) if col_blocks else (N_CHIPS, m // N_CHIPS, n)

    def body(*refs):
        (a_ref, b_ref), (out_ref,), (acc,), copies = _split_refs(refs, 2, 1, rider)
        k = pl.program_id(0)
        finish = _ride(copies, k == 0, k == nk - 1)

        @pl.when(k == 0)
        def _():
            acc[...] = jnp.zeros_like(acc)

        acc[...] += _dot_tn(a_ref[...], b_ref[...])

        @pl.when(k == nk - 1)
        def _():
            for j in range(N_CHIPS):
                if col_blocks:
                    out_ref[j] = acc[:, j * out_shape[2]:(j + 1) * out_shape[2]].astype(BF16)
                else:
                    out_ref[j] = acc[j * out_shape[1]:(j + 1) * out_shape[1], :].astype(BF16)

        finish()

    outs = _call(
        body, name, (nk,), [pl.BlockSpec((tk, m), lambda k: (k, 0)), pl.BlockSpec((tk, n), lambda k: (k, 0))],
        [pl.BlockSpec(out_shape, lambda k: (0, 0, 0))], [jax.ShapeDtypeStruct(out_shape, BF16)], [a, b],
        scratch=[pltpu.VMEM((m, n), F32)], rider=rider)
    return outs[0] if rider is None else outs


def _ffn_wgrad(hidden, shared, scale, name, rider=None):
    s, d = shared.shape
    half = hidden.shape[1] // 2
    return _wgrad(hidden, shared, lambda tk: pl.BlockSpec((tk, half), lambda j, k: (k, j)),
                  lambda tk: pl.BlockSpec((tk, d), lambda j, k: (k, 0)), half // 2, d, scale, name,
                  min(WGRAD_TILE, s), rider, per_step=2)


def _mix_pre(x, nw, win):
    s, d = x.shape
    nb, _, cb = win.shape
    tm = _tile(s)
    assert s % tm == 0

    def body(x_ref, nw_ref, w_ref, p_ref, hb_ref):
        xv = x_ref[...]
        hb = (xv * _rms_r(xv) * nw_ref[...]).astype(BF16)
        hb_ref[...] = hb
        for j in range(nb):
            p_ref[:, j * cb:(j + 1) * cb] = _dot(hb, w_ref[j])

    row = pl.BlockSpec((tm, d), lambda i: (i, 0))
    return pl.pallas_call(
        body, name="mix_pre", grid=(s // tm,),
        in_specs=[row, pl.BlockSpec((1, d), lambda i: (0, 0)),
                  pl.BlockSpec((nb, d, cb), lambda i: (0, 0, 0), pipeline_mode=pl.Buffered(1))],
        out_specs=[pl.BlockSpec((tm, nb * cb), lambda i: (i, 0)), row],
        out_shape=[jax.ShapeDtypeStruct((s, nb * cb), F32), jax.ShapeDtypeStruct((s, d), BF16)],
        compiler_params=_params(("arbitrary",)),
    )(x, nw, win)


def _mix_pre_bwd(x, nw, dres, dpb, win):
    s, d = x.shape
    nb, _, cb = win.shape
    tm = _tile(s)
    assert s % tm == 0

    def body(x_ref, nw_ref, dres_ref, dp_ref, w_ref, dx_ref, dnw_ref):
        dh = jnp.zeros((tm, d), F32)
        for j in range(nb):
            dh = dh + _dot_nt(dp_ref[:, j * cb:(j + 1) * cb], w_ref[j])
        xv = x_ref[...]
        dx, dn = _rms_bwd(xv, _rms_r(xv), nw_ref[...], dh)
        dx_ref[...] = dres_ref[...] + dx

        @pl.when(pl.program_id(0) == 0)
        def _():
            dnw_ref[...] = jnp.zeros_like(dnw_ref)

        dnw_ref[...] += dn

    row = pl.BlockSpec((tm, d), lambda i: (i, 0))
    vec = pl.BlockSpec((1, d), lambda i: (0, 0))
    return pl.pallas_call(
        body, name="mix_pre_bwd", grid=(s // tm,),
        in_specs=[row, vec, row, pl.BlockSpec((tm, nb * cb), lambda i: (i, 0)),
                  pl.BlockSpec((nb, d, cb), lambda i: (0, 0, 0), pipeline_mode=pl.Buffered(1))],
        out_specs=[row, vec],
        out_shape=[jax.ShapeDtypeStruct((s, d), F32), jax.ShapeDtypeStruct((1, d), F32)],
        compiler_params=_params(("arbitrary",)),
    )(x, nw, dres, dpb, win)


def _mix_post(x, yr, ya, nr, na, wout):
    s, d = x.shape
    h = yr.shape[1]
    tm = _tile(s)

    def body(x_ref, yr_ref, ya_ref, nr_ref, na_ref, w_ref, out_ref):
        yrv = yr_ref[...]
        yav = ya_ref[...]
        onb = (yrv * _rms_r(yrv) * nr_ref[...]).astype(BF16)
        oab = (yav * _rms_r(yav) * na_ref[...]).astype(BF16)
        out_ref[...] = x_ref[...] + _dot(onb, w_ref[0:h, :]) + _dot(oab, w_ref[h:2 * h, :])

    row = pl.BlockSpec((tm, d), lambda i: (i, 0))
    half = pl.BlockSpec((tm, h), lambda i: (i, 0))
    vec = pl.BlockSpec((1, h), lambda i: (0, 0))
    return pl.pallas_call(
        body, name="mix_post", grid=(s // tm,),
        in_specs=[row, half, half, vec, vec, pl.BlockSpec((2 * h, d), lambda i: (0, 0))],
        out_specs=row, out_shape=jax.ShapeDtypeStruct((s, d), F32),
        compiler_params=_params(("arbitrary",)),
    )(x, yr, ya, nr, na, wout)


def _mix_post_bwd(dx, yr, ya, nr, na, wout):
    s, d = dx.shape
    h = yr.shape[1]
    tm = _tile(s)

    def body(dx_ref, yr_ref, ya_ref, nr_ref, na_ref, w_ref,
             dyr_ref, dya_ref, yc_ref, dxb_ref, dnr_ref, dna_ref):
        i = pl.program_id(0)
        dxb = dx_ref[...].astype(BF16)
        dxb_ref[...] = dxb
        dyc = _dot_nt(dxb, w_ref[...])
        yrv = yr_ref[...]
        yav = ya_ref[...]
        rr = _rms_r(yrv)
        ra = _rms_r(yav)
        yc_ref[:, 0:h] = (yrv * rr * nr_ref[...]).astype(BF16)
        yc_ref[:, h:2 * h] = (yav * ra * na_ref[...]).astype(BF16)
        dyr, dnr = _rms_bwd(yrv, rr, nr_ref[...], dyc[:, 0:h])
        dya, dna = _rms_bwd(yav, ra, na_ref[...], dyc[:, h:2 * h])
        dyr_ref[...] = dyr
        dya_ref[...] = dya

        @pl.when(i == 0)
        def _():
            dnr_ref[...] = jnp.zeros_like(dnr_ref)
            dna_ref[...] = jnp.zeros_like(dna_ref)

        dnr_ref[...] += dnr
        dna_ref[...] += dna

    row = pl.BlockSpec((tm, d), lambda i: (i, 0))
    half = pl.BlockSpec((tm, h), lambda i: (i, 0))
    vec = pl.BlockSpec((1, h), lambda i: (0, 0))
    return pl.pallas_call(
        body, name="mix_post_bwd", grid=(s // tm,),
        in_specs=[row, half, half, vec, vec, pl.BlockSpec((2 * h, d), lambda i: (0, 0))],
        out_specs=[half, half, pl.BlockSpec((tm, 2 * h), lambda i: (i, 0)), row, vec, vec],
        out_shape=[jax.ShapeDtypeStruct((s, h), F32), jax.ShapeDtypeStruct((s, h), F32),
                   jax.ShapeDtypeStruct((s, 2 * h), BF16), jax.ShapeDtypeStruct((s, d), BF16),
                   jax.ShapeDtypeStruct((1, h), F32), jax.ShapeDtypeStruct((1, h), F32)],
        compiler_params=_params(("arbitrary",)),
    )(dx, yr, ya, nr, na, wout)


def _shift_down(xv, s, prev8):
    rolled = pltpu.roll(xv, s, 0)
    row8 = lax.broadcasted_iota(jnp.int32, prev8.shape, 0)
    head = jnp.where(row8 < s, pltpu.roll(prev8, s, 0), rolled[0:8, :])
    return jnp.concatenate([head, rolled[8:, :]], axis=0)


def _shift_up(xv, s, next8):
    n = xv.shape[0]
    rolled = pltpu.roll(xv, n - s, 0)
    row8 = lax.broadcasted_iota(jnp.int32, next8.shape, 0)
    tail = jnp.where(row8 >= 8 - s, pltpu.roll(next8, 8 - s, 0), rolled[n - 8:, :])
    return jnp.concatenate([rolled[:n - 8, :], tail], axis=0)


def _scan_fwd(a, b):
    n = a.shape[0]
    sub = lax.broadcasted_iota(jnp.int32, a.shape, 0) % SUBLANES
    s = 1
    while s < SUBLANES:
        ok = sub >= s
        b = jnp.where(ok, a * pltpu.roll(b, s, 0) + b, b)
        a = jnp.where(ok, a * pltpu.roll(a, s, 0), a)
        s *= 2
    groups = []
    before = jnp.zeros((1, a.shape[1]), F32)
    for g in range(n // SUBLANES):
        rows = slice(g * SUBLANES, (g + 1) * SUBLANES)
        groups.append(a[rows] * before + b[rows])
        before = groups[-1][SUBLANES - 1:]
    return jnp.concatenate(groups, axis=0)


def _scan_bwd(a, b):
    n = a.shape[0]
    sub = lax.broadcasted_iota(jnp.int32, a.shape, 0) % SUBLANES
    s = 1
    while s < SUBLANES:
        ok = sub < SUBLANES - s
        b = jnp.where(ok, a * pltpu.roll(b, n - s, 0) + b, b)
        a = jnp.where(ok, a * pltpu.roll(a, n - s, 0), a)
        s *= 2
    groups = []
    after = jnp.zeros((1, a.shape[1]), F32)
    for g in reversed(range(n // SUBLANES)):
        rows = slice(g * SUBLANES, (g + 1) * SUBLANES)
        groups.append(a[rows] * after + b[rows])
        after = groups[-1][:1]
    return jnp.concatenate(groups[::-1], axis=0)


def _rglru_gates(xv, prev8, cw_ref, cb_ref, wa_ref, ba_ref, wx_ref, bx_ref, lam_ref):
    x1 = _shift_down(xv, 1, prev8)
    x2 = _shift_down(xv, 2, prev8)
    x3 = _shift_down(xv, 3, prev8)
    xc = cw_ref[3:4, :] * xv + cw_ref[2:3, :] * x1 + cw_ref[1:2, :] * x2 + cw_ref[0:1, :] * x3 + cb_ref[...]
    xcb = xc.astype(BF16)
    r = _sigmoid(_dot(xcb, wa_ref[...]) + ba_ref[...])
    ig = _sigmoid(_dot(xcb, wx_ref[...]) + bx_ref[...])
    c = RG_C * _log_sigmoid(lam_ref[...])
    la = r * c
    a = jnp.exp(la)
    m = jnp.sqrt(-_expm1_neg(2.0 * la))
    return (x1, x2, x3), xc, xcb, r, ig, c, a, m


def _rglru_fwd(proj, cw, cb, wa, ba, wx, bx, lam):
    s = proj.shape[0]
    w = D_RNN
    tm = _tile(s)

    def body(xr_ref, gate_ref, cw_ref, cb_ref, wa_ref, ba_ref, wx_ref, bx_ref, lam_ref,
             y_ref, h_ref, prev, hlast):
        @pl.when(pl.program_id(0) == 0)
        def _():
            prev[...] = jnp.zeros_like(prev)
            hlast[...] = jnp.zeros_like(hlast)

        xv = xr_ref[...]
        _, xc, _, _, ig, _, a, m = _rglru_gates(xv, prev[...], cw_ref, cb_ref, wa_ref, ba_ref,
                                                wx_ref, bx_ref, lam_ref)
        b = m * (ig * xc)
        row = lax.broadcasted_iota(jnp.int32, b.shape, 0)
        b = jnp.where(row == 0, b + a * hlast[...], b)
        h = _scan_fwd(a, b)
        h_ref[...] = h
        y_ref[...] = h * _gelu(gate_ref[...])
        prev[...] = xv[tm - 8:, :]
        hlast[...] = h[tm - 1:tm, :]

    vec = pl.BlockSpec((1, w), lambda i: (0, 0))
    sq = pl.BlockSpec((w, w), lambda i: (0, 0))
    out = pl.BlockSpec((tm, w), lambda i: (i, 0))
    return pl.pallas_call(
        body, name="rglru_fwd", grid=(s // tm,),
        in_specs=[pl.BlockSpec((tm, w), lambda i: (i, 0)), pl.BlockSpec((tm, w), lambda i: (i, 1)),
                  pl.BlockSpec((CONV_W, w), lambda i: (0, 0)), vec, sq, vec, sq, vec, vec],
        out_specs=[out, out],
        out_shape=[jax.ShapeDtypeStruct((s, w), F32), jax.ShapeDtypeStruct((s, w), F32)],
        scratch_shapes=[pltpu.VMEM((8, w), F32), pltpu.VMEM((1, w), F32)],
        compiler_params=_params(("arbitrary",)),
    )(proj, proj, cw, cb, wa, ba, wx, bx, lam)


def _rglru_bwd(proj, hseq, dyr, cw, cb, wa, ba, wx, bx, lam):
    s = proj.shape[0]
    w = D_RNN
    tm = _tile(s)
    nt = s // tm
    t8 = tm // 8

    def body(xr_ref, xp_ref, gate_ref, h_ref, hp_ref, dy_ref, cw_ref, cb_ref, wa_ref, ba_ref,
             wx_ref, bx_ref, lam_ref,
             dxr_ref, dgate_ref, dcw_ref, dcb_ref, dwa_ref, dba_ref, dwx_ref, dbx_ref, dlam_ref,
             carry, dxc_next):
        i = pl.program_id(0)
        first_tile = i == nt - 1

        @pl.when(i == 0)
        def _():
            carry[...] = jnp.zeros_like(carry)
            dxc_next[...] = jnp.zeros_like(dxc_next)
            for ref in (dcw_ref, dcb_ref, dwa_ref, dba_ref, dwx_ref, dbx_ref, dlam_ref):
                ref[...] = jnp.zeros_like(ref)

        xv = xr_ref[...]
        prev8 = jnp.where(first_tile, 0.0, xp_ref[...])
        hprev8 = jnp.where(first_tile, 0.0, hp_ref[...])
        (x1, x2, x3), xc, xcb, r, ig, c, a, m = _rglru_gates(
            xv, prev8, cw_ref, cb_ref, wa_ref, ba_ref, wx_ref, bx_ref, lam_ref)
        gv = gate_ref[...]
        hv = h_ref[...]
        dy = dy_ref[...]
        dgate_ref[...] = (dy * hv * _gelu_grad(gv)).astype(BF16)
        dh = dy * _gelu(gv)
        row = lax.broadcasted_iota(jnp.int32, dh.shape, 0)
        dh = jnp.where(row == tm - 1, dh + carry[...], dh)
        a_up = jnp.where(row == tm - 1, 0.0, pltpu.roll(a, tm - 1, 0))
        lam_t = _scan_bwd(a_up, dh)
        carry[...] = a[0:1, :] * lam_t[0:1, :]
        hm1 = _shift_down(hv, 1, hprev8)
        da = lam_t * hm1
        ixc = ig * xc
        dm = lam_t * ixc
        dig = lam_t * m * xc
        dxc = lam_t * m * ig
        dla = da * a - dm * (a * a) / m
        dr = dla * c
        dlam_ref[...] += jnp.sum(dla * r, axis=0, keepdims=True)
        dpa = dr * r * (1.0 - r)
        dpi = dig * ig * (1.0 - ig)
        dba_ref[...] += jnp.sum(dpa, axis=0, keepdims=True)
        dbx_ref[...] += jnp.sum(dpi, axis=0, keepdims=True)
        dpab = dpa.astype(BF16)
        dpib = dpi.astype(BF16)
        dwa_ref[...] += _dot_tn(xcb, dpab)
        dwx_ref[...] += _dot_tn(xcb, dpib)
        dxc = dxc + _dot_nt(dpab, wa_ref[...]) + _dot_nt(dpib, wx_ref[...])
        dcb_ref[...] += jnp.sum(dxc, axis=0, keepdims=True)
        dcw_ref[3:4, :] += jnp.sum(dxc * xv, axis=0, keepdims=True)
        dcw_ref[2:3, :] += jnp.sum(dxc * x1, axis=0, keepdims=True)
        dcw_ref[1:2, :] += jnp.sum(dxc * x2, axis=0, keepdims=True)
        dcw_ref[0:1, :] += jnp.sum(dxc * x3, axis=0, keepdims=True)
        nxt = dxc_next[...]
        dxr = (cw_ref[3:4, :] * dxc + cw_ref[2:3, :] * _shift_up(dxc, 1, nxt)
               + cw_ref[1:2, :] * _shift_up(dxc, 2, nxt) + cw_ref[0:1, :] * _shift_up(dxc, 3, nxt))
        dxr_ref[...] = dxr.astype(BF16)
        dxc_next[...] = dxc[0:8, :]

        @pl.when(first_tile)
        def _():
            lv = lam_ref[...]
            dlam_ref[...] = dlam_ref[...] * (RG_C * _sigmoid(-lv))

    rev = lambda i: nt - 1 - i
    vec = pl.BlockSpec((1, w), lambda i: (0, 0))
    sq = pl.BlockSpec((w, w), lambda i: (0, 0))
    cur = lambda col: pl.BlockSpec((tm, w), lambda i: (rev(i), col))
    before = lambda cols: pl.BlockSpec((8, w), lambda i: (jnp.maximum(rev(i) * t8 - 1, 0), 0))
    return pl.pallas_call(
        body, name="rglru_bwd", grid=(nt,),
        in_specs=[cur(0), before(None), cur(1), cur(0), before(None), cur(0),
                  pl.BlockSpec((CONV_W, w), lambda i: (0, 0)), vec, sq, vec, sq, vec, vec],
        out_specs=[cur(0), cur(0), pl.BlockSpec((CONV_W, w), lambda i: (0, 0)), vec, sq, vec, sq, vec, vec],
        out_shape=[jax.ShapeDtypeStruct((s, w), BF16), jax.ShapeDtypeStruct((s, w), BF16),
                   jax.ShapeDtypeStruct((CONV_W, w), F32), jax.ShapeDtypeStruct((1, w), F32),
                   jax.ShapeDtypeStruct((w, w), F32), jax.ShapeDtypeStruct((1, w), F32),
                   jax.ShapeDtypeStruct((w, w), F32), jax.ShapeDtypeStruct((1, w), F32),
                   jax.ShapeDtypeStruct((1, w), F32)],
        scratch_shapes=[pltpu.VMEM((1, w), F32), pltpu.VMEM((8, w), F32)],
        compiler_params=_params(("arbitrary",)),
    )(proj, proj, proj, hseq, hseq, dyr, cw, cb, wa, ba, wx, bx, lam)


def _sb_logs(z, valid):
    lb = jnp.minimum(z, 0.0) - jnp.log(1.0 + jnp.exp(-jnp.abs(z)))
    return lb, jnp.where(valid, lb - z, 0.0)


class _Window:
    def __init__(self):
        blk, win, cut = ATT_BLOCK, ATT_WINDOW, ATT_SPLIT
        self.row = lax.broadcasted_iota(jnp.int32, (blk, win), 0)
        self.col = lax.broadcasted_iota(jnp.int32, (blk, win), 1)

        def tri(n, later):
            j = lax.broadcasted_iota(jnp.int32, (n, n), 0)
            s = lax.broadcasted_iota(jnp.int32, (n, n), 1)
            return jnp.where((j > s) if later else (j < s), 1.0, 0.0).astype(BF16)

        self.later = (tri(cut, True), tri(win - cut, True))
        self.earlier = (tri(cut, False), tri(win - cut, False))

    def place(self, qi, g):
        end = (qi + 1) * ATT_BLOCK - g * ATT_WINDOW
        start = pl.multiple_of(jnp.maximum(end - ATT_WINDOW, 0), ATT_BLOCK)
        valid = self.col < jnp.minimum(self.row + (qi * ATT_BLOCK - start), end - start)
        return start, valid

    @staticmethod
    def _parts(xv):
        hi = xv.astype(BF16)
        lo = (xv - hi.astype(F32)).astype(BF16)
        cut = ATT_SPLIT
        sums = (jnp.sum(xv[:, :cut], axis=1, keepdims=True), jnp.sum(xv[:, cut:], axis=1, keepdims=True))
        return (hi[:, :cut], lo[:, :cut]), (hi[:, cut:], lo[:, cut:]), sums

    def sums_after(self, xv, carry):
        (h0, l0), (h1, l1), (s0, s1) = self._parts(xv)
        first = _dot(h0, self.later[0]) + _dot(l0, self.later[0]) + (s1 + carry)
        last = _dot(h1, self.later[1]) + _dot(l1, self.later[1]) + carry
        return jnp.concatenate([first, last], axis=1), s0 + s1

    def sums_before(self, xv, carry):
        (h0, l0), (h1, l1), (s0, s1) = self._parts(xv)
        first = _dot(h0, self.earlier[0]) + _dot(l0, self.earlier[0]) + carry
        last = _dot(h1, self.earlier[1]) + _dot(l1, self.earlier[1]) + (s0 + carry)
        return jnp.concatenate([first, last], axis=1), s0 + s1


class _HeadPair:
    def __init__(self):
        lanes = 2 * HEAD_DIM
        lane = lax.broadcasted_iota(jnp.int32, (1, lanes), 1)
        self.masks = [lane // HEAD_DIM == h for h in (0, 1)]
        i = lax.broadcasted_iota(jnp.int32, (lanes, lanes), 0) // HEAD_DIM
        j = lax.broadcasted_iota(jnp.int32, (lanes, lanes), 1) // HEAD_DIM
        self.same_head = jnp.where(i == j, 1.0, 0.0).astype(BF16)

    def only(self, h, xv):
        return jnp.where(self.masks[h], xv, jnp.zeros_like(xv))

    def merge(self, per_head):
        return jnp.where(self.masks[0], per_head[0], per_head[1])

    def mean(self, xv):
        hi = xv.astype(BF16)
        lo = (xv - hi.astype(F32)).astype(BF16)
        return (_dot(hi, self.same_head) + _dot(lo, self.same_head)) * (1.0 / HEAD_DIM)

    def rms_r(self, xv):
        return lax.rsqrt(self.mean(xv * xv) + EPS)

    def rms_bwd(self, xv, r, nw, dh):
        t = dh * nw
        dx = r * t - xv * (r * r * r * self.mean(t * xv))
        dn = jnp.sum(dh * xv * r, axis=0, keepdims=True)
        return dx, dn[:, :HEAD_DIM] + dn[:, HEAD_DIM:]


def _attn_fwd(proj, qg, kg, rider=None):
    s = proj.shape[0]
    blk, win, dh = ATT_BLOCK, ATT_WINDOW, HEAD_DIM
    nq = s // blk
    scale = 1.0 / math.sqrt(dh)
    heads = (0, 1)
    assert s >= win and s % blk == 0

    def body(*refs):
        (q_ref, k_ref, v_ref, qg_ref, kg_ref), (o_ref,), (qn, kn, vb), copies = _split_refs(refs, 5, 1, rider)
        finish = _ride(copies, pl.program_id(0) == 0, pl.program_id(0) == N_HEADS // 2 - 1)
        wd, hp = _Window(), _HeadPair()
        qv = q_ref[...]
        qn[...] = (qv * hp.rms_r(qv) * qg_ref[...] * scale).astype(BF16)
        kv = k_ref[...]
        kn[...] = (kv * hp.rms_r(kv) * kg_ref[...]).astype(BF16)
        vb[...] = v_ref[...].astype(BF16)

        def q_step(qi, _):
            qoff = pl.multiple_of(qi * blk, blk)
            qt = qn[pl.ds(qoff, blk), :]
            qts = [hp.only(h, qt) for h in heads]

            def more(carry):
                g, live = carry[:2]
                return jnp.logical_and((qi + 1) * blk - g * win > 0, live > 0)

            def window(carry):
                g, _, accs, runs = carry
                start, valid = wd.place(qi, g)
                kt = kn[pl.ds(start, win), :]
                zs = [_dot_nt(qts[h], kt) for h in heads]
                logs = [_sb_logs(z, valid) for z in zs]
                sums = [wd.sums_after(logs[h][1], runs[h]) for h in heads]
                wgts = [jnp.where(valid, jnp.exp(logs[h][0] + sums[h][0]), 0.0).astype(BF16) for h in heads]
                vt = vb[pl.ds(start, win), :]
                accs = tuple(accs[h] + _dot(wgts[h], vt) for h in heads)
                runs = tuple(runs[h] + sums[h][1] for h in heads)
                live = (jnp.maximum(jnp.max(runs[0]), jnp.max(runs[1])) > EXP_ZERO).astype(jnp.int32)
                return g + 1, live, accs, runs

            zero = lambda cols: tuple(jnp.zeros((blk, cols), F32) for _ in heads)
            _, _, accs, _ = lax.while_loop(more, window, (jnp.int32(0), jnp.int32(1), zero(2 * dh), zero(1)))
            o_ref[pl.ds(qoff, blk), :] = hp.merge(accs)
            return 0

        lax.fori_loop(0, nq, q_step, 0)
        finish()

    pair = lambda group: pl.BlockSpec((s, 2 * dh), lambda p: (0, group * (D_ATT // (2 * dh)) + p))
    vec = pl.BlockSpec((1, 2 * dh), lambda p: (0, 0))
    return _call(
        body, "attn_fwd", (N_HEADS // 2,), [pair(2), pair(3), pair(4), vec, vec], [pair(0)],
        [jax.ShapeDtypeStruct((s, D_ATT), F32)], [proj, proj, proj, jnp.tile(qg, (1, 2)), jnp.tile(kg, (1, 2))],
        scratch=[pltpu.VMEM((s, 2 * dh), BF16)] * 3, rider=rider)


def _attn_bwd(proj, dya, qg, kg, rider=None):
    s = proj.shape[0]
    blk, win, dh = ATT_BLOCK, ATT_WINDOW, HEAD_DIM
    nq = s // blk
    max_windows = -(-s // win) + 1
    scale = 1.0 / math.sqrt(dh)
    steps = N_HEADS // 2
    heads = (0, 1)
    assert s >= win and s % blk == 0

    def body(*refs):
        ins, outs, scratch, copies = _split_refs(refs, 6, 5, rider)
        q_ref, k_ref, v_ref, do_ref, qg_ref, kg_ref = ins
        dq_ref, dk_ref, dv_ref, dqg_ref, dkg_ref = outs
        qn, kn, vb, dob, runs_ref, dqn, dkn, dvn = scratch
        finish = _ride(copies, pl.program_id(0) == 0, pl.program_id(0) == steps - 1)
        wd, hp = _Window(), _HeadPair()

        @pl.when(pl.program_id(0) == 0)
        def _():
            dqg_ref[...] = jnp.zeros_like(dqg_ref)
            dkg_ref[...] = jnp.zeros_like(dkg_ref)

        qv = q_ref[...]
        qn[...] = (qv * hp.rms_r(qv) * qg_ref[...] * scale).astype(BF16)
        kv = k_ref[...]
        kn[...] = (kv * hp.rms_r(kv) * kg_ref[...]).astype(BF16)
        vb[...] = v_ref[...].astype(BF16)
        dob[...] = do_ref[...].astype(BF16)
        dkn[...] = jnp.zeros_like(dkn)
        dvn[...] = jnp.zeros_like(dvn)

        def q_step(qi, _):
            qoff = pl.multiple_of(qi * blk, blk)
            qt = qn[pl.ds(qoff, blk), :]
            dot = dob[pl.ds(qoff, blk), :]
            qts = [hp.only(h, qt) for h in heads]
            dots = [hp.only(h, dot) for h in heads]

            zero = lambda cols: tuple(jnp.zeros((blk, cols), F32) for _ in heads)

            def logs_of(g):
                start, valid = wd.place(qi, g)
                kt = kn[pl.ds(start, win), :]
                return [_sb_logs(_dot_nt(qts[h], kt), valid) for h in heads]

            def row_sums(logs):
                return tuple(jnp.sum(logs[h][1], axis=1, keepdims=True) for h in heads)

            def still_live(runs):
                return jnp.maximum(jnp.max(runs[0]), jnp.max(runs[1])) > EXP_ZERO

            def window_grads(g, logs, runs, esums):
                start, valid = wd.place(qi, g)
                kt = kn[pl.ds(start, win), :]
                vt = vb[pl.ds(start, win), :]
                dws = [_dot_nt(dots[h], vt) for h in heads]
                tails = [wd.sums_after(logs[h][1], runs[h])[0] for h in heads]
                wgts = [jnp.where(valid, jnp.exp(logs[h][0] + tails[h]), 0.0) for h in heads]
                es = [dws[h] * wgts[h] for h in heads]
                befores = [wd.sums_before(es[h], esums[h]) for h in heads]
                dzbs = []
                for h in heads:
                    beta = jnp.exp(logs[h][0])
                    dz = jnp.where(valid, es[h] * (1.0 - beta) - befores[h][0] * beta, 0.0)
                    dzbs.append(dz.astype(BF16))
                dkn[pl.ds(start, win), :] += _dot_tn(dzbs[0], qts[0]) + _dot_tn(dzbs[1], qts[1])
                dvn[pl.ds(start, win), :] += (_dot_tn(wgts[0].astype(BF16), dots[0])
                                              + _dot_tn(wgts[1].astype(BF16), dots[1]))
                return tuple(_dot(dzbs[h], kt) for h in heads), tuple(befores[h][1] for h in heads)

            logs0 = logs_of(0)
            runs1 = row_sums(logs0)

            def one_window():
                return window_grads(0, logs0, zero(1), zero(1))[0]

            def all_windows():
                def more(carry):
                    g, live = carry[:2]
                    return jnp.logical_and((qi + 1) * blk - g * win > 0, live > 0)

                def run_window(carry):
                    g, _, runs = carry
                    for h in heads:
                        runs_ref[h, g] = runs[h]
                    sums = row_sums(logs_of(g))
                    runs = tuple(runs[h] + sums[h] for h in heads)
                    return g + 1, still_live(runs).astype(jnp.int32), runs

                for h in heads:
                    runs_ref[h, 0] = jnp.zeros((blk, 1), F32)
                windows, _, _ = lax.while_loop(more, run_window, (jnp.int32(1), jnp.int32(1), runs1))

                def k_window(gg, carry):
                    dq_accs, esums = carry
                    g = windows - 1 - gg
                    parts, totals = window_grads(g, logs_of(g), [runs_ref[h, g] for h in heads], esums)
                    return (tuple(dq_accs[h] + parts[h] for h in heads),
                            tuple(esums[h] + totals[h] for h in heads))

                return lax.fori_loop(0, windows, k_window, (zero(2 * dh), zero(1)))[0]

            earlier_keys = (qi + 1) * blk - win > 0
            dq_accs = lax.cond(jnp.logical_and(earlier_keys, still_live(runs1)), all_windows, one_window)
            dqn[pl.ds(qoff, blk), :] = hp.merge(dq_accs)
            return 0

        lax.fori_loop(0, nq, q_step, 0)

        dq, dqg = hp.rms_bwd(qv, hp.rms_r(qv), qg_ref[...] * scale, dqn[...])
        dq_ref[...] = dq.astype(BF16)
        dqg_ref[...] += dqg * scale
        dk, dkg = hp.rms_bwd(kv, hp.rms_r(kv), kg_ref[...], dkn[...])
        dk_ref[...] = dk.astype(BF16)
        dkg_ref[...] += dkg
        dv_ref[...] = dvn[...].astype(BF16)
        finish()

    pair = lambda group: pl.BlockSpec((s, 2 * dh), lambda p: (0, group * (D_ATT // (2 * dh)) + p))
    vec2 = pl.BlockSpec((1, 2 * dh), lambda p: (0, 0))
    vec = pl.BlockSpec((1, dh), lambda p: (0, 0))
    return _call(
        body, "attn_bwd", (steps,), [pair(2), pair(3), pair(4), pair(0), vec2, vec2],
        [pair(0), pair(0), pair(0), vec, vec],
        [jax.ShapeDtypeStruct((s, D_ATT), BF16)] * 3 + [jax.ShapeDtypeStruct((1, dh), F32)] * 2,
        [proj, proj, proj, dya, jnp.tile(qg, (1, 2)), jnp.tile(kg, (1, 2))],
        scratch=[pltpu.VMEM((s, 2 * dh), BF16)] * 4 + [pltpu.VMEM((2, max_windows, blk, 1), F32)]
        + [pltpu.VMEM((s, 2 * dh), F32)] * 3, rider=rider)


def _block_diag(w):
    n, c, d = w.shape
    return jnp.einsum("ncd,nm->ncmd", w, jnp.eye(n, dtype=w.dtype)).reshape(n * c, n * d)


def _diag_blocks(full, n):
    c = full.shape[0] // n
    return jnp.stack([full[i * c:(i + 1) * c, i * c:(i + 1) * c] for i in range(n)])


FFN1 = ["ffn1_w_gate", "ffn1_w_up", "ffn1_w_down"]
FFN2 = ["ffn2_w_gate", "ffn2_w_up", "ffn2_w_down"]


def _pair_sums(gb, names, where):
    theirs = _pair_exchange([gb[n] for n in names], "pair_exchange_" + names[0])
    pair, own = _pair_sum([gb[n] for n in names], theirs, where, "pair_sum_" + names[0])
    return _chip_rider(pair, own)


def _local_step(x, tgt, stacks, conv_stack, small, where):
    gate_up, down = FFN1[:2], FFN1[2:]
    big = dict(zip(gate_up, _gather_weights([stacks[n] for n in gate_up], [])))
    wa = _block_diag(small["rg_w_a"]).astype(BF16)
    wx = _block_diag(small["rg_w_x"]).astype(BF16)

    whole = lambda names: [big[n].reshape(-1, D_MODEL) for n in names]
    soon, later = down + ["w_in"], FFN2 + ["w_out"]
    g1, u1, hb1, ab1, *landed = _ffn_up(x, small["ffn1_norm"], *whole(gate_up),
                                        rider=_gather_rider([stacks[n] for n in soon], [conv_stack]))
    big.update(zip(soon, landed))
    x1 = _ffn_down(x, ab1, *whole(down))
    conv_w = jnp.transpose(landed[-1], (1, 0, 2)).reshape(CONV_W, D_RNN)
    rg = (conv_w, small["conv_b"], wa, small["rg_b_a"], wx, small["rg_b_x"], small["rg_lambda"])
    proj, hb2 = _mix_pre(x1, small["mix_norm"], big["w_in"])
    yr, hseq = _rglru_fwd(proj, *rg)
    ya, *landed = _attn_fwd(proj, small["q_norm"], small["k_norm"], _gather_rider([stacks[n] for n in later], []))
    big.update(zip(later, landed))
    wout = big["w_out"].reshape(D_MODEL, D_MODEL)
    x2 = _mix_post(x1, yr, ya, small["rnn_out_norm"], small["attn_out_norm"], wout)
    dx3, g2, u2, hb3, ab3, loss = _ffn_fwd_loss(x2, small["ffn2_norm"], *whole(FFN2), tgt)

    gb, gs, slots = {}, {}, {}
    dx2, dg2, du2, dyb2, gs["ffn2_norm"] = _ffn_bwd_act(x2, small["ffn2_norm"], dx3, g2, u2, *whole(FFN2), "ffn2_bwd")
    gb["ffn2_w_gate"] = _ffn_wgrad(dg2, hb3, 1.0, "wgrad_gate_ffn2")
    gb["ffn2_w_up"] = _ffn_wgrad(du2, hb3, 1.0, "wgrad_up_ffn2")
    gb["ffn2_w_down"] = _ffn_wgrad(ab3, dyb2, 0.5, "wgrad_down_ffn2")
    dyr, dya, ycat, dxb2, gs["rnn_out_norm"], gs["attn_out_norm"] = _mix_post_bwd(
        dx2, yr, ya, small["rnn_out_norm"], small["attn_out_norm"], wout)
    gb["w_out"] = _wgrad_whole(ycat, dxb2, False, "wgrad_out")
    early = FFN2 + ["w_out"]
    dq, dk, dv, gs["q_norm"], gs["k_norm"], *done = _attn_bwd(
        proj, dya, small["q_norm"], small["k_norm"], _pair_sums(gb, early, where))
    slots.update(zip(early, done))
    dxr, dgate, gs["conv_w"], gs["conv_b"], dwa, gs["rg_b_a"], dwx, gs["rg_b_x"], gs["rg_lambda"] = _rglru_bwd(
        proj, hseq, dyr, *rg)
    gs["rg_w_a"] = _diag_blocks(dwa, RNN_BLOCKS)
    gs["rg_w_x"] = _diag_blocks(dwx, RNN_BLOCKS)
    dpb = jnp.concatenate([dxr, dgate, dq, dk, dv], axis=1)
    dx1, gs["mix_norm"] = _mix_pre_bwd(x1, small["mix_norm"], dx2, dpb, big["w_in"])
    dx0, dg1, du1, dyb1, gs["ffn1_norm"] = _ffn_bwd_act(x, small["ffn1_norm"], dx1, g1, u1, *whole(FFN1), "ffn1_bwd")

    mine = _place_shard(_pack([gs[n] for n in SMALL] + [loss[:, :1]]), where, F32, "place_small_grads",
                        by_device=True)
    gb["ffn1_w_gate"], everyone = _ffn_wgrad(dg1, hb1, 1.0, "wgrad_gate_ffn1", _small_rider(mine))
    gb["ffn1_w_up"], slots["ffn1_w_gate"] = _ffn_wgrad(
        du1, hb1, 1.0, "wgrad_up_ffn1", _pair_sums(gb, ["ffn1_w_gate"], where))
    gb["ffn1_w_down"], slots["ffn1_w_up"] = _ffn_wgrad(
        ab1, dyb1, 0.5, "wgrad_down_ffn1", _pair_sums(gb, ["ffn1_w_up"], where))
    gb["w_in"], slots["ffn1_w_down"] = _wgrad_whole(
        hb2, dpb, True, "wgrad_in", _pair_sums(gb, ["ffn1_w_down"], where))
    last = _pair_sums(gb, ["w_in"], where)
    slots["w_in"], = _chip_exchange(last.plain, last.inplace)
    return dx0, slots, gs, everyone


ANY = pl.BlockSpec(memory_space=pl.ANY)


def _place():
    x, y, c = lax.axis_index("x"), lax.axis_index("y"), lax.axis_index("c")
    other_chips = [(1 - x, y), (x, 1 - y), (1 - x, 1 - y)]
    return x, y, c, 2 * x + y, other_chips


def _remote(src, dst, send_sem, recv_sem, to):
    return pltpu.make_async_remote_copy(src_ref=src, dst_ref=dst, send_sem=send_sem, recv_sem=recv_sem,
                                        device_id=to, device_id_type=MESH)


def _copy_plan(pairs):
    sends = [functools.partial(_remote, *a) for a, _ in pairs]
    arrivals = [functools.partial(_remote, *b) for _, b in pairs]
    return sends, arrivals


class _Rider:
    def __init__(self, plan, plain, inplace, n_copies=None, relay=None, n_relay=0):
        self.plan, self.plain, self.inplace = plan, list(plain), list(inplace)
        self.n_copies = n_copies or 3 * len(self.inplace)
        self.relay, self.n_relay = relay, n_relay

    def operands(self):
        return self.plain + self.inplace

    def out_shape(self):
        return [jax.ShapeDtypeStruct(a.shape, a.dtype) for a in self.inplace]

    def aliases(self, inputs_before, outputs_before):
        return {inputs_before + len(self.plain) + k: outputs_before + k for k in range(len(self.inplace))}

    def scratch(self):
        relay = [pltpu.SemaphoreType.DMA((self.n_relay,))] * 2 if self.relay else []
        return [pltpu.SemaphoreType.DMA((self.n_copies,))] * 2 + relay


def _split_refs(refs, n_in, n_out, rider):
    if rider is None:
        return refs[:n_in], refs[n_in:n_in + n_out], refs[n_in + n_out:], None
    r_in, r_out = len(rider.operands()), len(rider.inplace)
    outs_at = n_in + r_in
    n_sems = len(rider.scratch())
    rest = refs[outs_at + n_out + r_out:]
    sems = rest[len(rest) - n_sems:]
    filled = refs[outs_at + n_out:outs_at + n_out + r_out]
    copies = functools.partial(rider.plan, refs[n_in:n_in + len(rider.plain)], filled, *sems[:2])
    relay = functools.partial(rider.relay, filled, *sems[2:]) if rider.relay else None
    return refs[:n_in], refs[outs_at:outs_at + n_out], rest[:len(rest) - n_sems], (copies, relay)


def _ride(copies, first, last, middle=None):
    if copies is None:
        return lambda: None
    copies, relay = copies

    @pl.when(first)
    def _():
        _start(copies()[0])

    def start_relay():
        for make in copies()[1]:
            make().wait_recv()
        _start(relay()[0])

    if relay is not None and middle is not None:
        pl.when(middle)(start_relay)

    def finish():
        @pl.when(last)
        def _():
            if relay is None:
                _finish(*copies())
            else:
                if middle is None:
                    start_relay()
                _finish(copies()[0] + relay()[0], relay()[1])

    return finish


def _gather_rider(split, whole):
    n_split = len(split)
    return _Rider(lambda plain, stacks, ss, rs: _gather_ici(stacks, n_split, ss, rs), [], list(split) + list(whole),
                  relay=lambda stacks, ss, rs: _gather_d2d(stacks[:n_split], ss, rs), n_relay=3 * n_split)


def _chip_rider(sums, slots):
    return _Rider(_chip_copies, sums, slots)


def _start(makers):
    for make in makers:
        make().start()


def _finish(sends, arrivals):
    for make in arrivals:
        make().wait_recv()
    for make in sends:
        make().wait_send()


def _half(rows, c):
    return pl.ds(pl.multiple_of(c * rows, BF16_ROWS), rows)


def _gather_weights(split, whole):
    arrs = list(split) + list(whole)
    n, ns = len(arrs), len(split)

    def body(*refs):
        outs = refs[n:2 * n]
        send_sems, recv_sems, fsend_sems, frecv_sems = refs[2 * n:]
        sends, arrivals = _gather_ici(outs, ns, send_sems, recv_sems)
        passes, passed = _gather_d2d(outs[:ns], fsend_sems, frecv_sems)
        _start(sends)
        for k, make in enumerate(arrivals):
            make().wait_recv()
            if k < 3 * ns:
                passes[k]().start()
        _finish(sends + passes, passed)

    return pl.pallas_call(
        body, name="gather_weights",
        in_specs=[ANY] * n, out_specs=[ANY] * n,
        out_shape=[jax.ShapeDtypeStruct(a.shape, a.dtype) for a in arrs],
        input_output_aliases={i: i for i in range(n)},
        scratch_shapes=[pltpu.SemaphoreType.DMA((3 * n,)), pltpu.SemaphoreType.DMA((3 * n,)),
                        pltpu.SemaphoreType.DMA((3 * ns,)), pltpu.SemaphoreType.DMA((3 * ns,))],
    )(*arrs)


def _gather_ici(stacks, n_split, send_sems, recv_sems):
    x, y, c, me, chips = _place()

    def region(i, chip):
        if i < n_split:
            return stacks[i].at[chip, _half(stacks[i].shape[1] // 2, c)]
        return stacks[i].at[chip]

    pairs = []
    for i in range(len(stacks)):
        for p, (cx, cy) in enumerate(chips):
            k = 3 * i + p
            mine, got = region(i, me), region(i, 2 * cx + cy)
            sems, to = (send_sems.at[k], recv_sems.at[k]), (cx, cy, c)
            pairs.append(((mine, mine, *sems, to), (got, got, *sems, to)))
    return _copy_plan(pairs)


def _gather_d2d(stacks, send_sems, recv_sems):
    x, y, c, _, chips = _place()
    sibling = (x, y, 1 - c)
    pairs = []
    for i, stack in enumerate(stacks):
        rows = stack.shape[1] // 2
        for p, (cx, cy) in enumerate(chips):
            k = 3 * i + p
            got, theirs = stack.at[2 * cx + cy, _half(rows, c)], stack.at[2 * cx + cy, _half(rows, 1 - c)]
            sems = (send_sems.at[k], recv_sems.at[k])
            pairs.append(((got, got, *sems, sibling), (theirs, theirs, *sems, sibling)))
    return _copy_plan(pairs)


def _pair_exchange(grads, name):
    n = len(grads)

    def body(*refs):
        ins, theirs = refs[:n], refs[n:2 * n]
        send_sems, recv_sems = refs[2 * n:]
        x, y, c, _, _ = _place()
        sibling = (x, y, 1 - c)
        sends = [_remote(ins[k].at[:, _half(grads[k].shape[1] // 2, 1 - c)], theirs[k],
                         send_sems.at[k], recv_sems.at[k], sibling) for k in range(n)]
        for cp in sends:
            cp.start()
        for k in range(n):
            _remote(theirs[k], theirs[k], send_sems.at[k], recv_sems.at[k], sibling).wait_recv()
        for cp in sends:
            cp.wait_send()

    return pl.pallas_call(
        body, name=name,
        in_specs=[ANY] * n, out_specs=[ANY] * n,
        out_shape=[jax.ShapeDtypeStruct((g.shape[0], g.shape[1] // 2, g.shape[2]), g.dtype) for g in grads],
        scratch_shapes=[pltpu.SemaphoreType.DMA((n,))] * 2,
    )(*grads)


def _chip_exchange(sums, slots):
    n = len(sums)

    def body(*refs):
        sends, arrivals = _chip_copies(refs[:n], refs[2 * n:3 * n], *refs[3 * n:])
        _start(sends)
        _finish(sends, arrivals)

    return pl.pallas_call(
        body, name="grad_chip_exchange",
        in_specs=[ANY] * (2 * n), out_specs=[ANY] * n,
        out_shape=[jax.ShapeDtypeStruct(a.shape, a.dtype) for a in slots],
        input_output_aliases={n + k: k for k in range(n)},
        scratch_shapes=[pltpu.SemaphoreType.DMA((3 * n,)), pltpu.SemaphoreType.DMA((3 * n,))],
    )(*sums, *slots)


def _chip_copies(sums, slots, send_sems, recv_sems):
    x, y, c, me, chips = _place()
    pairs = []
    for k in range(len(sums)):
        for p, (cx, cy) in enumerate(chips):
            j = 3 * k + p
            got = slots[k].at[2 * cx + cy]
            sems, to = (send_sems.at[j], recv_sems.at[j]), (cx, cy, c)
            pairs.append(((sums[k].at[2 * cx + cy], slots[k].at[me], *sems, to), (got, got, *sems, to)))
    return _copy_plan(pairs)


def _half_swap(halves):
    n = len(halves)

    def body(*refs):
        outs = refs[n:2 * n]
        send_sems, recv_sems = refs[2 * n:]
        x, y, c, _, _ = _place()
        sibling = (x, y, 1 - c)
        sends = [_remote(outs[k].at[c], outs[k].at[c], send_sems.at[k], recv_sems.at[k], sibling) for k in range(n)]
        for cp in sends:
            cp.start()
        for k in range(n):
            got = outs[k].at[1 - c]
            _remote(got, got, send_sems.at[k], recv_sems.at[k], sibling).wait_recv()
        for cp in sends:
            cp.wait_send()

    return pl.pallas_call(
        body, name="grad_half_swap",
        in_specs=[ANY] * n, out_specs=[ANY] * n,
        out_shape=[jax.ShapeDtypeStruct(a.shape, a.dtype) for a in halves],
        input_output_aliases={k: k for k in range(n)},
        scratch_shapes=[pltpu.SemaphoreType.DMA((n,))] * 2,
    )(*halves)


def _small_rider(stack):
    n_dev = 2 * N_CHIPS

    def plan(_, stacks, send_sems, recv_sems):
        x, y, c, _, _ = _place()
        mine = stacks[0].at[4 * x + 2 * y + c]
        pairs = []
        for k in range(1, n_dev):
            px, py, pc = x ^ ((k >> 2) & 1), y ^ ((k >> 1) & 1), c ^ (k & 1)
            got = stacks[0].at[4 * px + 2 * py + pc]
            sems = (send_sems.at[k - 1], recv_sems.at[k - 1])
            pairs.append(((mine, mine, *sems, (px, py, pc)), (got, got, *sems, (px, py, pc))))
        return _copy_plan(pairs)

    return _Rider(plan, [], [stack], n_dev - 1)


def _row_tile(r):
    return r // 4 if r >= 256 and (r // 4) % BF16_ROWS == 0 else r


def _prefetch_call(body, name, grid, in_specs, out_specs, out_shape):
    spec = pltpu.PrefetchScalarGridSpec(num_scalar_prefetch=1, grid=grid, in_specs=in_specs, out_specs=out_specs)
    return pl.pallas_call(body, name=name, grid_spec=spec, out_shape=out_shape,
                          compiler_params=_params(("arbitrary",) * len(grid)))


def _place_shard(w2d, where, dtype, name, by_device=False):
    r, c = w2d.shape
    tr = _row_tile(r)
    slots = 2 * N_CHIPS if by_device else N_CHIPS
    slot = (lambda s: 2 * s[1] + s[0]) if by_device else (lambda s: s[1])

    def body(where_ref, w_ref, out_ref):
        out_ref[...] = w_ref[...].astype(dtype)

    return _prefetch_call(
        body, name, (r // tr,), [pl.BlockSpec((tr, c), lambda i, s: (i, 0))],
        pl.BlockSpec((None, tr, c), lambda i, s: (slot(s), i, 0)),
        jax.ShapeDtypeStruct((slots, r, c), dtype))(where, w2d)


def _place_shards(w2ds, where, name):
    n = len(w2ds)
    steps = N_CHIPS
    assert all(w.shape[0] % (BF16_ROWS * steps) == 0 for w in w2ds)

    def body(where_ref, *refs):
        for k in range(n):
            refs[n + k][...] = refs[k][...].astype(BF16)

    tile = lambda w: (w.shape[0] // steps, w.shape[1])
    return _prefetch_call(
        body, name, (steps,), [pl.BlockSpec(tile(w), lambda i, s: (i, 0)) for w in w2ds],
        [pl.BlockSpec((None,) + tile(w), lambda i, s: (s[1], i, 0)) for w in w2ds],
        [jax.ShapeDtypeStruct((N_CHIPS,) + w.shape, BF16) for w in w2ds])(where, *w2ds)


def _pair_sum(fulls, theirs, where, name):
    n = len(fulls)

    def body(where_ref, *refs):
        for k in range(n):
            a_ref, b_ref, out_ref, own_ref = refs[k], refs[n + k], refs[2 * n + k], refs[3 * n + k]
            total = (a_ref[...].astype(F32) + b_ref[...].astype(F32)).astype(BF16)
            out_ref[...] = total

            @pl.when(pl.program_id(0) == where_ref[1])
            def _():
                own_ref[...] = total

    half = lambda t: pl.BlockSpec((None,) + t.shape[1:], lambda j, s: (j, s[0], 0))
    blk = lambda t: pl.BlockSpec((None,) + t.shape[1:], lambda j, s: (j, 0, 0))
    own = lambda t: pl.BlockSpec((None,) + t.shape[1:], lambda j, s: (s[1], 0, 0))
    shapes = [jax.ShapeDtypeStruct(t.shape, BF16) for t in theirs]
    outs = _prefetch_call(
        body, name, (N_CHIPS,), [half(t) for t in theirs] + [blk(t) for t in theirs],
        [blk(t) for t in theirs] + [own(t) for t in theirs], shapes + shapes)(where, *fulls, *theirs)
    return outs[:n], outs[n:]


def _chip_sum(slots, where, name):
    n = len(slots)
    steps = 2
    assert all(a.shape[1] % (BF16_ROWS * steps) == 0 for a in slots)

    def body(where_ref, *refs):
        for k in range(n):
            a_ref, out_ref = refs[k], refs[n + k]
            total = a_ref[0].astype(F32)
            for j in range(1, a_ref.shape[0]):
                total = total + a_ref[j].astype(F32)
            out_ref[...] = total

    tile = lambda a: (a.shape[1] // steps, a.shape[2])
    return _prefetch_call(
        body, name, (steps,), [pl.BlockSpec((a.shape[0],) + tile(a), lambda i, s: (0, i, 0)) for a in slots],
        [pl.BlockSpec((None,) + tile(a), lambda i, s: (s[0], i, 0)) for a in slots],
        [jax.ShapeDtypeStruct((2,) + a.shape[1:], F32) for a in slots])(where, *slots)


def _slot_sum(a, name):
    nb, r, c = a.shape
    tr = _row_tile(r)

    def body(a_ref, out_ref):
        total = a_ref[0].astype(F32)
        for j in range(1, nb):
            total = total + a_ref[j].astype(F32)
        out_ref[...] = total

    return pl.pallas_call(
        body, name=name, grid=(r // tr,),
        in_specs=[pl.BlockSpec((nb, tr, c), lambda i: (0, i, 0))],
        out_specs=pl.BlockSpec((tr, c), lambda i: (i, 0)),
        out_shape=jax.ShapeDtypeStruct((r, c), F32), compiler_params=_params(("arbitrary",)),
    )(a)


def _adamw(ws, gs, ms, vs, name, steps=1):
    n = len(ws)
    c1 = 1.0 - ADAM_B1 ** ADAM_STEP
    c2 = 1.0 - ADAM_B2 ** ADAM_STEP
    assert all(w.shape[0] % steps == 0 and (steps == 1 or w.shape[0] // steps % 8 == 0) for w in ws)

    def body(*refs):
        for k in range(n):
            w_ref, g_ref, m_ref, v_ref = (refs[j * n + k] for j in range(4))
            g_out, d_ref, m2_ref, v2_ref = (refs[(4 + j) * n + k] for j in range(4))
            gv = g_ref[...]
            g_out[...] = gv
            m2 = ADAM_B1 * m_ref[...] + (1.0 - ADAM_B1) * gv
            v2 = ADAM_B2 * v_ref[...] + (1.0 - ADAM_B2) * (gv * gv)
            m2_ref[...] = m2
            v2_ref[...] = v2
            d_ref[...] = -ADAM_LR * ((m2 / c1) / (jnp.sqrt(v2 / c2) + ADAM_EPS) + ADAM_WD * w_ref[...])

    blks = [pl.BlockSpec((w.shape[0] // steps, w.shape[1]), lambda i: (i, 0)) for w in ws]
    shapes = [jax.ShapeDtypeStruct(w.shape, F32) for w in ws]
    outs = pl.pallas_call(
        body, name=name, grid=(steps,), in_specs=blks * 4, out_specs=blks * 4, out_shape=shapes * 4,
        compiler_params=_params(("arbitrary",)),
    )(*ws, *gs, *ms, *vs)
    return [outs[j * n:(j + 1) * n] for j in range(4)]


WEIGHTS = ["ffn1_norm", "ffn1_w_gate", "ffn1_w_up", "ffn1_w_down", "mix_norm", "w_in", "conv_w", "conv_b",
           "rg_w_a", "rg_b_a", "rg_w_x", "rg_b_x", "rg_lambda", "q_norm", "k_norm", "rnn_out_norm",
           "attn_out_norm", "w_out", "ffn2_norm", "ffn2_w_gate", "ffn2_w_up", "ffn2_w_down"]
BIG = ["ffn1_w_gate", "ffn1_w_up", "ffn1_w_down", "w_in", "w_out", "ffn2_w_gate", "ffn2_w_up", "ffn2_w_down"]
SMALL = [n for n in WEIGHTS if n not in BIG]
PACK_LANES = 128
PACK_ROW_ALIGN = 8


def _hidden_major(name, a):
    return jnp.transpose(a) if name.endswith(("w_gate", "w_up")) else a


def _pack(parts):
    flat = jnp.concatenate([p.reshape(-1) for p in parts])
    unit = PACK_LANES * PACK_ROW_ALIGN
    padded = -(-flat.shape[0] // unit) * unit
    return jnp.pad(flat, (0, padded - flat.shape[0])).reshape(-1, PACK_LANES)


def _unpack(packed, shapes):
    flat = packed.reshape(-1)
    out, at = [], 0
    for shp in shapes:
        size = math.prod(shp)
        out.append(flat[at:at + size].reshape(shp))
        at += size
    return out


def kernel(x, ffn1_norm, ffn1_w_gate, ffn1_w_up, ffn1_w_down, mix_norm, w_in, conv_w, conv_b, rg_w_a, rg_b_a, rg_w_x, rg_b_x, rg_lambda, q_norm, k_norm, rnn_out_norm, attn_out_norm, w_out, ffn2_norm, ffn2_w_gate, ffn2_w_up, ffn2_w_down, loss_target, m_ffn1_norm, m_ffn1_w_gate, m_ffn1_w_up, m_ffn1_w_down, m_mix_norm, m_w_in, m_conv_w, m_conv_b, m_rg_w_a, m_rg_b_a, m_rg_w_x, m_rg_b_x, m_rg_lambda, m_q_norm, m_k_norm, m_rnn_out_norm, m_attn_out_norm, m_w_out, m_ffn2_norm, m_ffn2_w_gate, m_ffn2_w_up, m_ffn2_w_down, v_ffn1_norm, v_ffn1_w_gate, v_ffn1_w_up, v_ffn1_w_down, v_mix_norm, v_w_in, v_conv_w, v_conv_b, v_rg_w_a, v_rg_b_a, v_rg_w_x, v_rg_b_x, v_rg_lambda, v_q_norm, v_k_norm, v_rnn_out_norm, v_attn_out_norm, v_w_out, v_ffn2_norm, v_ffn2_w_gate, v_ffn2_w_up, v_ffn2_w_down):
    given = dict(locals())
    w = {n: given[n] for n in WEIGHTS}
    m = {n: given["m_" + n] for n in WEIGHTS}
    v = {n: given["v_" + n] for n in WEIGHTS}
    chip = 2 * lax.axis_index("x") + lax.axis_index("y")

    where = jnp.stack([lax.axis_index("c"), chip]).astype(jnp.int32)

    stacks = dict(zip(BIG, _place_shards([_hidden_major(n, w[n][0]) for n in BIG], where, "place_weights")))
    conv_stack = _place_shard(w["conv_w"][0], where, F32, "place_conv_w")
    small = {n: (w[n][0] if w[n].ndim > 2 else w[n]) for n in SMALL if n != "conv_w"}

    grad_x, slots, gs, everyone = _local_step(x[0], loss_target[0], stacks, conv_stack, small, where)

    swapped = _half_swap(_chip_sum([slots[n] for n in BIG], where, "chip_sums"))
    g2s = [t.reshape(t.shape[0] * t.shape[1], t.shape[2]) for t in swapped]
    flat = lambda tree: [_hidden_major(n, tree[n][0]) for n in BIG]
    g2s, d2s, m2s, v2s = _adamw(flat(w), g2s, flat(m), flat(v), "adamw_weights", ADAMW_STEPS)
    grads, deltas, new_m, new_v = {}, {}, {}, {}
    for tree, parts in ((grads, g2s), (deltas, d2s), (new_m, m2s), (new_v, v2s)):
        tree.update({n: _hidden_major(n, a).reshape(w[n].shape) for n, a in zip(BIG, parts)})

    full_shapes = [gs[n].shape for n in SMALL]
    *summed, loss = _unpack(_slot_sum(everyone, "small_grad_sum"), full_shapes + [(1, 1)])
    g_parts = dict(zip(SMALL, summed))
    quarter = D_RNN // N_CHIPS
    g_parts["conv_w"] = lax.dynamic_slice_in_dim(g_parts["conv_w"], chip * quarter, quarter, axis=1)
    local_shapes = [w[n].shape for n in SMALL]
    pk = lambda tree: _pack([tree[n] for n in SMALL])
    (g_s,), (d_s,), (m_s,), (v_s,) = _adamw([pk(w)], [pk(g_parts)], [pk(m)], [pk(v)], "adamw_small")
    for tree, packed in ((grads, g_s), (deltas, d_s), (new_m, m_s), (new_v, v_s)):
        tree.update(zip(SMALL, _unpack(packed, local_shapes)))

    return (loss[0, 0], grad_x.reshape(x.shape), *[grads[n] for n in WEIGHTS], *[deltas[n] for n in WEIGHTS],
            *[new_m[n] for n in WEIGHTS], *[new_v[n] for n in WEIGHTS])
```

```python
import functools
import math

import jax
import jax.numpy as jnp
from jax import lax
from jax.experimental import pallas as pl
from jax.experimental.pallas import tpu as pltpu

F32 = jnp.float32
BF16 = jnp.bfloat16
MESH = pl.DeviceIdType.MESH

D_MODEL = 1024
N_CHIPS = 4
D_RNN = 512
D_ATT = 512
N_HEADS = 8
HEAD_DIM = 64
RNN_BLOCKS = 8
CONV_W = 4
RG_C = 8.0
N_IN = 2 * D_RNN + 3 * D_ATT
EPS = 1e-6
ATT_BLOCK = 128
ATT_WINDOW = 384
ATT_SPLIT = 256
EXP_ZERO = -105.0

ADAM_LR = 0.001
ADAM_B1 = 0.9
ADAM_B2 = 0.999
ADAM_EPS = 1e-08
ADAM_WD = 0.01
ADAM_STEP = 10

V7X_VMEM_LIMIT = 56 * 1024 * 1024
V7X_MXU_WIDTH = 256
TOKEN_TILE = 512
SUBLANES = 8
BF16_ROWS = 16
FFN_TILE = 256
WGRAD_TILE = 2048
WHOLE_TILE = 1024
ADAMW_STEPS = 8

GELU_K0 = math.sqrt(2.0 / math.pi)
GELU_K1 = 0.044715


def _params(sem=None):
    return pltpu.CompilerParams(dimension_semantics=sem, vmem_limit_bytes=V7X_VMEM_LIMIT)


def _dot(a, b):
    return jnp.dot(a, b, preferred_element_type=F32)


def _dot_nt(a, b):
    return lax.dot_general(a, b, (((1,), (1,)), ((), ())), preferred_element_type=F32)


def _dot_tn(a, b):
    return lax.dot_general(a, b, (((0,), (0,)), ((), ())), preferred_element_type=F32)


def _sigmoid(x):
    return 1.0 / (1.0 + jnp.exp(-x))


def _rms_r(xv):
    return lax.rsqrt(jnp.mean(xv * xv, axis=-1, keepdims=True) + EPS)


def _rms_bwd(xv, r, nw, dh):
    t = dh * nw
    dx = r * t - xv * (r * r * r * jnp.mean(t * xv, axis=-1, keepdims=True))
    dn = jnp.sum(dh * xv * r, axis=0, keepdims=True)
    return dx, dn


def _gelu(x):
    t = jnp.tanh(GELU_K0 * (x + GELU_K1 * x * x * x))
    return 0.5 * x * (1.0 + t)


def _gelu_grad(x):
    t = jnp.tanh(GELU_K0 * (x + GELU_K1 * x * x * x))
    return 0.5 * (1.0 + t) + 0.5 * x * (1.0 - t * t) * (GELU_K0 * (1.0 + 3.0 * GELU_K1 * x * x))


def _expm1_neg(x):
    p = 1.0 + x * (1.0 / 6.0)
    for k in (5.0, 4.0, 3.0, 2.0):
        p = 1.0 + x * (1.0 / k) * p
    return jnp.where(x > -0.25, x * p, jnp.exp(x) - 1.0)


def _log_sigmoid(x):
    return jnp.minimum(x, 0.0) - jnp.log(1.0 + jnp.exp(-jnp.abs(x)))


def _tile(s):
    return min(TOKEN_TILE, s)


def _ffn_chunks(f):
    cut = f // 2 // V7X_MXU_WIDTH * V7X_MXU_WIDTH
    return ((0, cut), (cut, f)) if 0 < cut < f else ((0, f),)


def _ffn_fwd_loss(x, nw, wg, wu, wd, tgt):
    s, d = x.shape
    f = wg.shape[0]
    tm = min(FFN_TILE, s)
    ni = s // tm
    assert s % tm == 0

    def body(x_ref, nw_ref, wg_ref, wu_ref, wd_ref, tgt_ref, out_ref, g_ref, u_ref, hb_ref, ab_ref, loss_ref):
        i = pl.program_id(0)
        xv = x_ref[...]
        hb = (xv * _rms_r(xv) * nw_ref[...]).astype(BF16)
        hb_ref[...] = hb
        y = jnp.zeros((tm, d), F32)
        for lo, hi in _ffn_chunks(f):
            g = _dot_nt(hb, wg_ref[lo:hi, :])
            u = _dot_nt(hb, wu_ref[lo:hi, :])
            g_ref[:, lo:hi] = g.astype(BF16)
            u_ref[:, lo:hi] = u.astype(BF16)
            ab = (g * _sigmoid(g) * u).astype(BF16)
            ab_ref[:, lo:hi] = ab
            y = y + _dot(ab, wd_ref[lo:hi, :])
        diff = xv + 0.5 * y - tgt_ref[...]
        out_ref[...] = diff * (1.0 / d)

        @pl.when(i == 0)
        def _():
            loss_ref[...] = jnp.zeros_like(loss_ref)

        loss_ref[...] += jnp.sum(diff * diff) * (0.5 / d)

    row = pl.BlockSpec((tm, d), lambda i: (i, 0))
    weight = pl.BlockSpec((f, d), lambda i: (0, 0), pipeline_mode=pl.Buffered(1))
    blk = pl.BlockSpec((tm, f), lambda i: (i, 0))
    wide = jax.ShapeDtypeStruct((s, f), BF16)
    return _call(body, "ffn_fwd_loss", (ni,),
                 [row, pl.BlockSpec((1, d), lambda i: (0, 0)), weight, weight, weight, row],
                 [row, blk, blk, row, blk, pl.BlockSpec((1, 128), lambda i: (0, 0))],
                 [jax.ShapeDtypeStruct((s, d), F32), wide, wide, jax.ShapeDtypeStruct((s, d), BF16), wide,
                  jax.ShapeDtypeStruct((1, 128), F32)], [x, nw, wg, wu, wd, tgt])


def _ffn_up(x, nw, wg, wu, rider=None):
    s, d = x.shape
    f = wg.shape[0]
    tm = min(FFN_TILE, s)
    ni = s // tm
    assert s % tm == 0

    def body(*refs):
        (x_ref, nw_ref, wg_ref, wu_ref), (g_ref, u_ref, hb_ref, ab_ref), _, copies = _split_refs(refs, 4, 4, rider)
        i = pl.program_id(0)
        finish = _ride(copies, i == 0, i == ni - 1)
        xv = x_ref[...]
        hb = (xv * _rms_r(xv) * nw_ref[...]).astype(BF16)
        hb_ref[...] = hb
        for lo, hi in _ffn_chunks(f):
            g = _dot_nt(hb, wg_ref[lo:hi, :])
            u = _dot_nt(hb, wu_ref[lo:hi, :])
            g_ref[:, lo:hi] = g.astype(BF16)
            u_ref[:, lo:hi] = u.astype(BF16)
            ab_ref[:, lo:hi] = (g * _sigmoid(g) * u).astype(BF16)
        finish()

    row = pl.BlockSpec((tm, d), lambda i: (i, 0))
    weight = pl.BlockSpec((f, d), lambda i: (0, 0), pipeline_mode=pl.Buffered(1))
    blk = pl.BlockSpec((tm, f), lambda i: (i, 0))
    wide = jax.ShapeDtypeStruct((s, f), BF16)
    return _call(body, "ffn_up", (ni,), [row, pl.BlockSpec((1, d), lambda i: (0, 0)), weight, weight],
                 [blk, blk, row, blk], [wide, wide, jax.ShapeDtypeStruct((s, d), BF16), wide], [x, nw, wg, wu],
                 rider=rider)


def _ffn_down(x, ab, wd):
    s, d = x.shape
    f = wd.shape[0]
    tm = _tile(s)
    assert s % tm == 0

    def body(x_ref, ab_ref, wd_ref, out_ref):
        out_ref[...] = x_ref[...] + 0.5 * _dot(ab_ref[...], wd_ref[...])

    row = pl.BlockSpec((tm, d), lambda i: (i, 0))
    return _call(body, "ffn_down", (s // tm,),
                 [row, pl.BlockSpec((tm, f), lambda i: (i, 0)),
                  pl.BlockSpec((f, d), lambda i: (0, 0), pipeline_mode=pl.Buffered(1))],
                 [row], [jax.ShapeDtypeStruct((s, d), F32)], [x, ab, wd])[0]


def _call(body, name, grid, in_specs, out_specs, out_shape, args, scratch=(), rider=None):
    in_specs, out_specs, out_shape, scratch = list(in_specs), list(out_specs), list(out_shape), list(scratch)
    extra, aliases = [], {}
    if rider is not None:
        extra = rider.operands()
        aliases = rider.aliases(len(args), len(out_shape))
        in_specs += [ANY] * len(extra)
        out_specs += [ANY] * len(rider.inplace)
        out_shape += rider.out_shape()
        scratch += rider.scratch()
    return pl.pallas_call(
        body, name=name, grid=grid, in_specs=in_specs, out_specs=out_specs, out_shape=out_shape,
        input_output_aliases=aliases, scratch_shapes=scratch,
        compiler_params=_params(("arbitrary",) * len(grid)),
    )(*args, *extra)


def _ffn_bwd_act(x, nw, dy, g, u, wg, wu, wd, name):
    s, d = x.shape
    f = wg.shape[0]
    tm = min(FFN_TILE, s)
    assert s % tm == 0

    def body(x_ref, nw_ref, dy_ref, g_ref, u_ref, wg_ref, wu_ref, wd_ref,
             dx_ref, dg_ref, du_ref, dyb_ref, dnw_ref):
        dyv = dy_ref[...]
        dyb = dyv.astype(BF16)
        dyb_ref[...] = dyb
        dh = jnp.zeros((tm, d), F32)
        for lo, hi in _ffn_chunks(f):
            da = 0.5 * _dot_nt(dyb, wd_ref[lo:hi, :])
            gv = g_ref[:, lo:hi].astype(F32)
            sg = _sigmoid(gv)
            dub = (da * (gv * sg)).astype(BF16)
            dgb = (da * u_ref[:, lo:hi].astype(F32) * (sg * (1.0 + gv * (1.0 - sg)))).astype(BF16)
            dg_ref[:, lo:hi] = dgb
            du_ref[:, lo:hi] = dub
            dh = dh + _dot(dgb, wg_ref[lo:hi, :]) + _dot(dub, wu_ref[lo:hi, :])
        xv = x_ref[...]
        dx, dn = _rms_bwd(xv, _rms_r(xv), nw_ref[...], dh)
        dx_ref[...] = dyv + dx

        @pl.when(pl.program_id(0) == 0)
        def _():
            dnw_ref[...] = jnp.zeros_like(dnw_ref)

        dnw_ref[...] += dn

    row = pl.BlockSpec((tm, d), lambda i: (i, 0))
    vec = pl.BlockSpec((1, d), lambda i: (0, 0))
    blk = pl.BlockSpec((tm, f), lambda i: (i, 0))
    weight = pl.BlockSpec((f, d), lambda i: (0, 0), pipeline_mode=pl.Buffered(1))
    return _call(
        body, name, (s // tm,), [row, vec, row, blk, blk, weight, weight, weight], [row, blk, blk, row, vec],
        [jax.ShapeDtypeStruct((s, d), F32), jax.ShapeDtypeStruct((s, f), BF16),
         jax.ShapeDtypeStruct((s, f), BF16), jax.ShapeDtypeStruct((s, d), BF16),
         jax.ShapeDtypeStruct((1, d), F32)],
        [x, nw, dy, g, u, wg, wu, wd])


def _wgrad(a, b, a_spec, b_spec, out_rows, out_cols, scale, name, tk, rider=None, per_step=1):
    s = a.shape[-2]
    nk = s // tk
    steps = N_CHIPS // per_step
    assert s % tk == 0

    def body(*refs):
        (a_ref, b_ref), (out_ref,), (acc,), copies = _split_refs(refs, 2, 1, rider)
        j, k = pl.program_id(0), pl.program_id(1)
        finish = _ride(copies, jnp.logical_and(j == 0, k == 0), jnp.logical_and(j == steps - 1, k == nk - 1))

        @pl.when(k == 0)
        def _():
            acc[...] = jnp.zeros_like(acc)

        acc[...] += _dot_tn(a_ref[...], b_ref[...])

        @pl.when(k == nk - 1)
        def _():
            for t in range(per_step):
                out_ref[t] = (acc[t * out_rows:(t + 1) * out_rows, :] * scale).astype(BF16)

        finish()

    outs = _call(
        body, name, (steps, nk), [a_spec(tk), b_spec(tk)],
        [pl.BlockSpec((per_step, out_rows, out_cols), lambda j, k: (j, 0, 0))],
        [jax.ShapeDtypeStruct((N_CHIPS, out_rows, out_cols), BF16)], [a, b],
        scratch=[pltpu.VMEM((per_step * out_rows, out_cols), F32)], rider=rider)
    return outs[0] if rider is None else outs


def _wgrad_whole(a, b, col_blocks, name, rider=None):
    s, m = a.shape
    n = b.shape[1]
    tk = min(WHOLE_TILE, s)
    nk = s // tk
    assert s % tk == 0
    out_shape = (N_CHIPS, m, n // N_CHIPS) if col_blocks else (N_CHIPS, m // N_CHIPS, n)

    def body(*refs):
        (a_ref, b_ref), (out_ref,), (acc,), copies = _split_refs(refs, 2, 1, rider)
        k = pl.program_id(0)
        finish = _ride(copies, k == 0, k == nk - 1)

        @pl.when(k == 0)
        def _():
            acc[...] = jnp.zeros_like(acc)

        acc[...] += _dot_tn(a_ref[...], b_ref[...])

        @pl.when(k == nk - 1)
        def _():
            for j in range(N_CHIPS):
                if col_blocks:
                    out_ref[j] = acc[:, j * out_shape[2]:(j + 1) * out_shape[2]].astype(BF16)
                else:
                    out_ref[j] = acc[j * out_shape[1]:(j + 1) * out_shape[1], :].astype(BF16)

        finish()

    outs = _call(
        body, name, (nk,), [pl.BlockSpec((tk, m), lambda k: (k, 0)), pl.BlockSpec((tk, n), lambda k: (k, 0))],
        [pl.BlockSpec(out_shape, lambda k: (0, 0, 0))], [jax.ShapeDtypeStruct(out_shape, BF16)], [a, b],
        scratch=[pltpu.VMEM((m, n), F32)], rider=rider)
    return outs[0] if rider is None else outs


def _ffn_wgrad(hidden, shared, scale, name, rider=None):
    s, d = shared.shape
    half = hidden.shape[1] // 2
    return _wgrad(hidden, shared, lambda tk: pl.BlockSpec((tk, half), lambda j, k: (k, j)),
                  lambda tk: pl.BlockSpec((tk, d), lambda j, k: (k, 0)), half // 2, d, scale, name,
                  min(WGRAD_TILE, s), rider, per_step=2)


def _mix_pre(x, nw, win, rider=None):
    s, d = x.shape
    nb, _, cb = win.shape
    tm = _tile(s)
    ni = s // tm
    assert s % tm == 0

    def body(*refs):
        (x_ref, nw_ref, w_ref), (p_ref, hb_ref), _, copies = _split_refs(refs, 3, 2, rider)
        finish = _ride(copies, pl.program_id(0) == 0, pl.program_id(0) == ni - 1)
        xv = x_ref[...]
        hb = (xv * _rms_r(xv) * nw_ref[...]).astype(BF16)
        hb_ref[...] = hb
        for j in range(nb):
            p_ref[:, j * cb:(j + 1) * cb] = _dot(hb, w_ref[j])
        finish()

    row = pl.BlockSpec((tm, d), lambda i: (i, 0))
    return _call(
        body, "mix_pre", (ni,),
        [row, pl.BlockSpec((1, d), lambda i: (0, 0)),
         pl.BlockSpec((nb, d, cb), lambda i: (0, 0, 0), pipeline_mode=pl.Buffered(1))],
        [pl.BlockSpec((tm, nb * cb), lambda i: (i, 0)), row],
        [jax.ShapeDtypeStruct((s, nb * cb), F32), jax.ShapeDtypeStruct((s, d), BF16)], [x, nw, win], rider=rider)


def _mix_pre_bwd(x, nw, dres, dpb, win):
    s, d = x.shape
    nb, _, cb = win.shape
    tm = _tile(s)
    assert s % tm == 0

    def body(x_ref, nw_ref, dres_ref, dp_ref, w_ref, dx_ref, dnw_ref):
        dh = jnp.zeros((tm, d), F32)
        for j in range(nb):
            dh = dh + _dot_nt(dp_ref[:, j * cb:(j + 1) * cb], w_ref[j])
        xv = x_ref[...]
        dx, dn = _rms_bwd(xv, _rms_r(xv), nw_ref[...], dh)
        dx_ref[...] = dres_ref[...] + dx

        @pl.when(pl.program_id(0) == 0)
        def _():
            dnw_ref[...] = jnp.zeros_like(dnw_ref)

        dnw_ref[...] += dn

    row = pl.BlockSpec((tm, d), lambda i: (i, 0))
    vec = pl.BlockSpec((1, d), lambda i: (0, 0))
    return pl.pallas_call(
        body, name="mix_pre_bwd", grid=(s // tm,),
        in_specs=[row, vec, row, pl.BlockSpec((tm, nb * cb), lambda i: (i, 0)),
                  pl.BlockSpec((nb, d, cb), lambda i: (0, 0, 0), pipeline_mode=pl.Buffered(1))],
        out_specs=[row, vec],
        out_shape=[jax.ShapeDtypeStruct((s, d), F32), jax.ShapeDtypeStruct((1, d), F32)],
        compiler_params=_params(("arbitrary",)),
    )(x, nw, dres, dpb, win)


def _mix_post(x, yr, ya, nr, na, wout):
    s, d = x.shape
    h = yr.shape[1]
    tm = _tile(s)

    def body(x_ref, yr_ref, ya_ref, nr_ref, na_ref, w_ref, out_ref):
        yrv = yr_ref[...]
        yav = ya_ref[...]
        onb = (yrv * _rms_r(yrv) * nr_ref[...]).astype(BF16)
        oab = (yav * _rms_r(yav) * na_ref[...]).astype(BF16)
        out_ref[...] = x_ref[...] + _dot(onb, w_ref[0:h, :]) + _dot(oab, w_ref[h:2 * h, :])

    row = pl.BlockSpec((tm, d), lambda i: (i, 0))
    half = pl.BlockSpec((tm, h), lambda i: (i, 0))
    vec = pl.BlockSpec((1, h), lambda i: (0, 0))
    return pl.pallas_call(
        body, name="mix_post", grid=(s // tm,),
        in_specs=[row, half, half, vec, vec, pl.BlockSpec((2 * h, d), lambda i: (0, 0))],
        out_specs=row, out_shape=jax.ShapeDtypeStruct((s, d), F32),
        compiler_params=_params(("arbitrary",)),
    )(x, yr, ya, nr, na, wout)


def _mix_post_bwd(dx, yr, ya, nr, na, wout):
    s, d = dx.shape
    h = yr.shape[1]
    tm = _tile(s)

    def body(dx_ref, yr_ref, ya_ref, nr_ref, na_ref, w_ref,
             dyr_ref, dya_ref, yc_ref, dxb_ref, dnr_ref, dna_ref):
        i = pl.program_id(0)
        dxb = dx_ref[...].astype(BF16)
        dxb_ref[...] = dxb
        dyc = _dot_nt(dxb, w_ref[...])
        yrv = yr_ref[...]
        yav = ya_ref[...]
        rr = _rms_r(yrv)
        ra = _rms_r(yav)
        yc_ref[:, 0:h] = (yrv * rr * nr_ref[...]).astype(BF16)
        yc_ref[:, h:2 * h] = (yav * ra * na_ref[...]).astype(BF16)
        dyr, dnr = _rms_bwd(yrv, rr, nr_ref[...], dyc[:, 0:h])
        dya, dna = _rms_bwd(yav, ra, na_ref[...], dyc[:, h:2 * h])
        dyr_ref[...] = dyr
        dya_ref[...] = dya

        @pl.when(i == 0)
        def _():
            dnr_ref[...] = jnp.zeros_like(dnr_ref)
            dna_ref[...] = jnp.zeros_like(dna_ref)

        dnr_ref[...] += dnr
        dna_ref[...] += dna

    row = pl.BlockSpec((tm, d), lambda i: (i, 0))
    half = pl.BlockSpec((tm, h), lambda i: (i, 0))
    vec = pl.BlockSpec((1, h), lambda i: (0, 0))
    return pl.pallas_call(
        body, name="mix_post_bwd", grid=(s // tm,),
        in_specs=[row, half, half, vec, vec, pl.BlockSpec((2 * h, d), lambda i: (0, 0))],
        out_specs=[half, half, pl.BlockSpec((tm, 2 * h), lambda i: (i, 0)), row, vec, vec],
        out_shape=[jax.ShapeDtypeStruct((s, h), F32), jax.ShapeDtypeStruct((s, h), F32),
                   jax.ShapeDtypeStruct((s, 2 * h), BF16), jax.ShapeDtypeStruct((s, d), BF16),
                   jax.ShapeDtypeStruct((1, h), F32), jax.ShapeDtypeStruct((1, h), F32)],
        compiler_params=_params(("arbitrary",)),
    )(dx, yr, ya, nr, na, wout)


def _shift_down(xv, s, prev8):
    rolled = pltpu.roll(xv, s, 0)
    row8 = lax.broadcasted_iota(jnp.int32, prev8.shape, 0)
    head = jnp.where(row8 < s, pltpu.roll(prev8, s, 0), rolled[0:8, :])
    return jnp.concatenate([head, rolled[8:, :]], axis=0)


def _shift_up(xv, s, next8):
    n = xv.shape[0]
    rolled = pltpu.roll(xv, n - s, 0)
    row8 = lax.broadcasted_iota(jnp.int32, next8.shape, 0)
    tail = jnp.where(row8 >= 8 - s, pltpu.roll(next8, 8 - s, 0), rolled[n - 8:, :])
    return jnp.concatenate([rolled[:n - 8, :], tail], axis=0)


def _scan_fwd(a, b):
    n = a.shape[0]
    sub = lax.broadcasted_iota(jnp.int32, a.shape, 0) % SUBLANES
    s = 1
    while s < SUBLANES:
        ok = sub >= s
        b = jnp.where(ok, a * pltpu.roll(b, s, 0) + b, b)
        a = jnp.where(ok, a * pltpu.roll(a, s, 0), a)
        s *= 2
    groups = []
    before = jnp.zeros((1, a.shape[1]), F32)
    for g in range(n // SUBLANES):
        rows = slice(g * SUBLANES, (g + 1) * SUBLANES)
        groups.append(a[rows] * before + b[rows])
        before = groups[-1][SUBLANES - 1:]
    return jnp.concatenate(groups, axis=0)


def _scan_bwd(a, b):
    n = a.shape[0]
    sub = lax.broadcasted_iota(jnp.int32, a.shape, 0) % SUBLANES
    s = 1
    while s < SUBLANES:
        ok = sub < SUBLANES - s
        b = jnp.where(ok, a * pltpu.roll(b, n - s, 0) + b, b)
        a = jnp.where(ok, a * pltpu.roll(a, n - s, 0), a)
        s *= 2
    groups = []
    after = jnp.zeros((1, a.shape[1]), F32)
    for g in reversed(range(n // SUBLANES)):
        rows = slice(g * SUBLANES, (g + 1) * SUBLANES)
        groups.append(a[rows] * after + b[rows])
        after = groups[-1][:1]
    return jnp.concatenate(groups[::-1], axis=0)


def _rglru_gates(xv, prev8, cw_ref, cb_ref, wa_ref, ba_ref, wx_ref, bx_ref, lam_ref):
    x1 = _shift_down(xv, 1, prev8)
    x2 = _shift_down(xv, 2, prev8)
    x3 = _shift_down(xv, 3, prev8)
    xc = cw_ref[3:4, :] * xv + cw_ref[2:3, :] * x1 + cw_ref[1:2, :] * x2 + cw_ref[0:1, :] * x3 + cb_ref[...]
    xcb = xc.astype(BF16)
    r = _sigmoid(_dot(xcb, wa_ref[...]) + ba_ref[...])
    ig = _sigmoid(_dot(xcb, wx_ref[...]) + bx_ref[...])
    c = RG_C * _log_sigmoid(lam_ref[...])
    la = r * c
    a = jnp.exp(la)
    m = jnp.sqrt(-_expm1_neg(2.0 * la))
    return (x1, x2, x3), xc, xcb, r, ig, c, a, m


def _rglru_fwd(proj, cw, cb, wa, ba, wx, bx, lam, rider=None):
    s = proj.shape[0]
    w = D_RNN
    tm = _tile(s)
    ni = s // tm

    def body(*refs):
        ins, (y_ref, h_ref), (prev, hlast), copies = _split_refs(refs, 9, 2, rider)
        xr_ref, gate_ref, cw_ref, cb_ref, wa_ref, ba_ref, wx_ref, bx_ref, lam_ref = ins
        finish = _ride(copies, pl.program_id(0) == 0, pl.program_id(0) == ni - 1)

        @pl.when(pl.program_id(0) == 0)
        def _():
            prev[...] = jnp.zeros_like(prev)
            hlast[...] = jnp.zeros_like(hlast)

        xv = xr_ref[...]
        _, xc, _, _, ig, _, a, m = _rglru_gates(xv, prev[...], cw_ref, cb_ref, wa_ref, ba_ref,
                                                wx_ref, bx_ref, lam_ref)
        b = m * (ig * xc)
        row = lax.broadcasted_iota(jnp.int32, b.shape, 0)
        b = jnp.where(row == 0, b + a * hlast[...], b)
        h = _scan_fwd(a, b)
        h_ref[...] = h
        y_ref[...] = h * _gelu(gate_ref[...])
        prev[...] = xv[tm - 8:, :]
        hlast[...] = h[tm - 1:tm, :]
        finish()

    vec = pl.BlockSpec((1, w), lambda i: (0, 0))
    sq = pl.BlockSpec((w, w), lambda i: (0, 0))
    out = pl.BlockSpec((tm, w), lambda i: (i, 0))
    return _call(
        body, "rglru_fwd", (ni,),
        [pl.BlockSpec((tm, w), lambda i: (i, 0)), pl.BlockSpec((tm, w), lambda i: (i, 1)),
         pl.BlockSpec((CONV_W, w), lambda i: (0, 0)), vec, sq, vec, sq, vec, vec], [out, out],
        [jax.ShapeDtypeStruct((s, w), F32), jax.ShapeDtypeStruct((s, w), F32)],
        [proj, proj, cw, cb, wa, ba, wx, bx, lam],
        scratch=[pltpu.VMEM((8, w), F32), pltpu.VMEM((1, w), F32)], rider=rider)


def _rglru_bwd(proj, hseq, dyr, cw, cb, wa, ba, wx, bx, lam):
    s = proj.shape[0]
    w = D_RNN
    tm = _tile(s)
    nt = s // tm
    t8 = tm // 8

    def body(xr_ref, xp_ref, gate_ref, h_ref, hp_ref, dy_ref, cw_ref, cb_ref, wa_ref, ba_ref,
             wx_ref, bx_ref, lam_ref,
             dxr_ref, dgate_ref, dcw_ref, dcb_ref, dwa_ref, dba_ref, dwx_ref, dbx_ref, dlam_ref,
             carry, dxc_next):
        i = pl.program_id(0)
        first_tile = i == nt - 1

        @pl.when(i == 0)
        def _():
            carry[...] = jnp.zeros_like(carry)
            dxc_next[...] = jnp.zeros_like(dxc_next)
            for ref in (dcw_ref, dcb_ref, dwa_ref, dba_ref, dwx_ref, dbx_ref, dlam_ref):
                ref[...] = jnp.zeros_like(ref)

        xv = xr_ref[...]
        prev8 = jnp.where(first_tile, 0.0, xp_ref[...])
        hprev8 = jnp.where(first_tile, 0.0, hp_ref[...])
        (x1, x2, x3), xc, xcb, r, ig, c, a, m = _rglru_gates(
            xv, prev8, cw_ref, cb_ref, wa_ref, ba_ref, wx_ref, bx_ref, lam_ref)
        gv = gate_ref[...]
        hv = h_ref[...]
        dy = dy_ref[...]
        dgate_ref[...] = (dy * hv * _gelu_grad(gv)).astype(BF16)
        dh = dy * _gelu(gv)
        row = lax.broadcasted_iota(jnp.int32, dh.shape, 0)
        dh = jnp.where(row == tm - 1, dh + carry[...], dh)
        a_up = jnp.where(row == tm - 1, 0.0, pltpu.roll(a, tm - 1, 0))
        lam_t = _scan_bwd(a_up, dh)
        carry[...] = a[0:1, :] * lam_t[0:1, :]
        hm1 = _shift_down(hv, 1, hprev8)
        da = lam_t * hm1
        ixc = ig * xc
        dm = lam_t * ixc
        dig = lam_t * m * xc
        dxc = lam_t * m * ig
        dla = da * a - dm * (a * a) / m
        dr = dla * c
        dlam_ref[...] += jnp.sum(dla * r, axis=0, keepdims=True)
        dpa = dr * r * (1.0 - r)
        dpi = dig * ig * (1.0 - ig)
        dba_ref[...] += jnp.sum(dpa, axis=0, keepdims=True)
        dbx_ref[...] += jnp.sum(dpi, axis=0, keepdims=True)
        dpab = dpa.astype(BF16)
        dpib = dpi.astype(BF16)
        dwa_ref[...] += _dot_tn(xcb, dpab)
        dwx_ref[...] += _dot_tn(xcb, dpib)
        dxc = dxc + _dot_nt(dpab, wa_ref[...]) + _dot_nt(dpib, wx_ref[...])
        dcb_ref[...] += jnp.sum(dxc, axis=0, keepdims=True)
        dcw_ref[3:4, :] += jnp.sum(dxc * xv, axis=0, keepdims=True)
        dcw_ref[2:3, :] += jnp.sum(dxc * x1, axis=0, keepdims=True)
        dcw_ref[1:2, :] += jnp.sum(dxc * x2, axis=0, keepdims=True)
        dcw_ref[0:1, :] += jnp.sum(dxc * x3, axis=0, keepdims=True)
        nxt = dxc_next[...]
        dxr = (cw_ref[3:4, :] * dxc + cw_ref[2:3, :] * _shift_up(dxc, 1, nxt)
               + cw_ref[1:2, :] * _shift_up(dxc, 2, nxt) + cw_ref[0:1, :] * _shift_up(dxc, 3, nxt))
        dxr_ref[...] = dxr.astype(BF16)
        dxc_next[...] = dxc[0:8, :]

        @pl.when(first_tile)
        def _():
            lv = lam_ref[...]
            dlam_ref[...] = dlam_ref[...] * (RG_C * _sigmoid(-lv))

    rev = lambda i: nt - 1 - i
    vec = pl.BlockSpec((1, w), lambda i: (0, 0))
    sq = pl.BlockSpec((w, w), lambda i: (0, 0))
    cur = lambda col: pl.BlockSpec((tm, w), lambda i: (rev(i), col))
    before = lambda cols: pl.BlockSpec((8, w), lambda i: (jnp.maximum(rev(i) * t8 - 1, 0), 0))
    return pl.pallas_call(
        body, name="rglru_bwd", grid=(nt,),
        in_specs=[cur(0), before(None), cur(1), cur(0), before(None), cur(0),
                  pl.BlockSpec((CONV_W, w), lambda i: (0, 0)), vec, sq, vec, sq, vec, vec],
        out_specs=[cur(0), cur(0), pl.BlockSpec((CONV_W, w), lambda i: (0, 0)), vec, sq, vec, sq, vec, vec],
        out_shape=[jax.ShapeDtypeStruct((s, w), BF16), jax.ShapeDtypeStruct((s, w), BF16),
                   jax.ShapeDtypeStruct((CONV_W, w), F32), jax.ShapeDtypeStruct((1, w), F32),
                   jax.ShapeDtypeStruct((w, w), F32), jax.ShapeDtypeStruct((1, w), F32),
                   jax.ShapeDtypeStruct((w, w), F32), jax.ShapeDtypeStruct((1, w), F32),
                   jax.ShapeDtypeStruct((1, w), F32)],
        scratch_shapes=[pltpu.VMEM((1, w), F32), pltpu.VMEM((8, w), F32)],
        compiler_params=_params(("arbitrary",)),
    )(proj, proj, proj, hseq, hseq, dyr, cw, cb, wa, ba, wx, bx, lam)


def _sb_logs(z, valid):
    lb = jnp.minimum(z, 0.0) - jnp.log(1.0 + jnp.exp(-jnp.abs(z)))
    return lb, jnp.where(valid, lb - z, 0.0)


class _Window:
    def __init__(self):
        blk, win, cut = ATT_BLOCK, ATT_WINDOW, ATT_SPLIT
        self.row = lax.broadcasted_iota(jnp.int32, (blk, win), 0)
        self.col = lax.broadcasted_iota(jnp.int32, (blk, win), 1)

        def tri(n, later):
            j = lax.broadcasted_iota(jnp.int32, (n, n), 0)
            s = lax.broadcasted_iota(jnp.int32, (n, n), 1)
            return jnp.where((j > s) if later else (j < s), 1.0, 0.0).astype(BF16)

        self.later = (tri(cut, True), tri(win - cut, True))
        self.earlier = (tri(cut, False), tri(win - cut, False))

    def place(self, qi, g):
        end = (qi + 1) * ATT_BLOCK - g * ATT_WINDOW
        start = pl.multiple_of(jnp.maximum(end - ATT_WINDOW, 0), ATT_BLOCK)
        valid = self.col < jnp.minimum(self.row + (qi * ATT_BLOCK - start), end - start)
        return start, valid

    @staticmethod
    def _parts(xv):
        hi = xv.astype(BF16)
        lo = (xv - hi.astype(F32)).astype(BF16)
        cut = ATT_SPLIT
        sums = (jnp.sum(xv[:, :cut], axis=1, keepdims=True), jnp.sum(xv[:, cut:], axis=1, keepdims=True))
        return (hi[:, :cut], lo[:, :cut]), (hi[:, cut:], lo[:, cut:]), sums

    def sums_after(self, xv, carry):
        (h0, l0), (h1, l1), (s0, s1) = self._parts(xv)
        first = _dot(h0, self.later[0]) + _dot(l0, self.later[0]) + (s1 + carry)
        last = _dot(h1, self.later[1]) + _dot(l1, self.later[1]) + carry
        return jnp.concatenate([first, last], axis=1), s0 + s1

    def sums_before(self, xv, carry):
        (h0, l0), (h1, l1), (s0, s1) = self._parts(xv)
        first = _dot(h0, self.earlier[0]) + _dot(l0, self.earlier[0]) + carry
        last = _dot(h1, self.earlier[1]) + _dot(l1, self.earlier[1]) + (s0 + carry)
        return jnp.concatenate([first, last], axis=1), s0 + s1


class _HeadPair:
    def __init__(self):
        lanes = 2 * HEAD_DIM
        lane = lax.broadcasted_iota(jnp.int32, (1, lanes), 1)
        self.masks = [lane // HEAD_DIM == h for h in (0, 1)]
        i = lax.broadcasted_iota(jnp.int32, (lanes, lanes), 0) // HEAD_DIM
        j = lax.broadcasted_iota(jnp.int32, (lanes, lanes), 1) // HEAD_DIM
        self.same_head = jnp.where(i == j, 1.0, 0.0).astype(BF16)

    def only(self, h, xv):
        return jnp.where(self.masks[h], xv, jnp.zeros_like(xv))

    def merge(self, per_head):
        return jnp.where(self.masks[0], per_head[0], per_head[1])

    def mean(self, xv):
        hi = xv.astype(BF16)
        lo = (xv - hi.astype(F32)).astype(BF16)
        return (_dot(hi, self.same_head) + _dot(lo, self.same_head)) * (1.0 / HEAD_DIM)

    def rms_r(self, xv):
        return lax.rsqrt(self.mean(xv * xv) + EPS)

    def rms_bwd(self, xv, r, nw, dh):
        t = dh * nw
        dx = r * t - xv * (r * r * r * self.mean(t * xv))
        dn = jnp.sum(dh * xv * r, axis=0, keepdims=True)
        return dx, dn[:, :HEAD_DIM] + dn[:, HEAD_DIM:]


def _attn_fwd(proj, qg, kg, rider=None):
    s = proj.shape[0]
    blk, win, dh = ATT_BLOCK, ATT_WINDOW, HEAD_DIM
    nq = s // blk
    scale = 1.0 / math.sqrt(dh)
    heads = (0, 1)
    blocks = (0, 1)
    assert s >= win and s % (blk * len(blocks)) == 0

    def body(*refs):
        (q_ref, k_ref, v_ref, qg_ref, kg_ref), (o_ref,), (qn, kn, vb), copies = _split_refs(refs, 5, 1, rider)
        finish = _ride(copies, pl.program_id(0) == 0, pl.program_id(0) == N_HEADS // 2 - 1)
        wd, hp = _Window(), _HeadPair()
        qv = q_ref[...]
        qn[...] = (qv * hp.rms_r(qv) * qg_ref[...] * scale).astype(BF16)
        kv = k_ref[...]
        kn[...] = (kv * hp.rms_r(kv) * kg_ref[...]).astype(BF16)
        vb[...] = v_ref[...].astype(BF16)

        def q_step(pair_i, _):
            qis = [2 * pair_i + b for b in blocks]
            chains = [(b, h) for b in blocks for h in heads]
            qoffs = [pl.multiple_of(qi * blk, blk) for qi in qis]
            qtiles = [qn[pl.ds(qoff, blk), :] for qoff in qoffs]
            qts = [hp.only(h, qtiles[b]) for b, h in chains]

            def more(carry):
                g, live = carry[:2]
                return jnp.logical_and((qis[-1] + 1) * blk - g * win > 0, live > 0)

            def window(carry):
                g, _, accs, runs = carry
                places = [wd.place(qi, g) for qi in qis]
                kts = [kn[pl.ds(start, win), :] for start, _ in places]
                zs = [_dot_nt(qts[c], kts[b]) for c, (b, h) in enumerate(chains)]
                logs = [_sb_logs(zs[c], places[b][1]) for c, (b, h) in enumerate(chains)]
                sums = [wd.sums_after(logs[c][1], runs[c]) for c in range(len(chains))]
                wgts = [jnp.where(places[b][1], jnp.exp(logs[c][0] + sums[c][0]), 0.0).astype(BF16)
                        for c, (b, h) in enumerate(chains)]
                vts = [vb[pl.ds(start, win), :] for start, _ in places]
                accs = tuple(accs[c] + _dot(wgts[c], vts[b]) for c, (b, h) in enumerate(chains))
                runs = tuple(runs[c] + sums[c][1] for c in range(len(chains)))
                top = functools.reduce(jnp.maximum, [jnp.max(r) for r in runs])
                return g + 1, (top > EXP_ZERO).astype(jnp.int32), accs, runs

            zero = lambda cols: tuple(jnp.zeros((blk, cols), F32) for _ in chains)
            _, _, accs, _ = lax.while_loop(more, window, (jnp.int32(0), jnp.int32(1), zero(2 * dh), zero(1)))
            for b in blocks:
                o_ref[pl.ds(qoffs[b], blk), :] = hp.merge([accs[2 * b + h] for h in heads])
            return 0

        lax.fori_loop(0, nq // len(blocks), q_step, 0)
        finish()

    pair = lambda group: pl.BlockSpec((s, 2 * dh), lambda p: (0, group * (D_ATT // (2 * dh)) + p))
    vec = pl.BlockSpec((1, 2 * dh), lambda p: (0, 0))
    return _call(
        body, "attn_fwd", (N_HEADS // 2,), [pair(2), pair(3), pair(4), vec, vec], [pair(0)],
        [jax.ShapeDtypeStruct((s, D_ATT), F32)], [proj, proj, proj, jnp.tile(qg, (1, 2)), jnp.tile(kg, (1, 2))],
        scratch=[pltpu.VMEM((s, 2 * dh), BF16)] * 3, rider=rider)


def _attn_bwd(proj, dya, qg, kg, rider=None):
    s = proj.shape[0]
    blk, win, dh = ATT_BLOCK, ATT_WINDOW, HEAD_DIM
    nq = s // blk
    max_windows = -(-s // win) + 1
    scale = 1.0 / math.sqrt(dh)
    steps = N_HEADS // 2
    heads = (0, 1)
    assert s >= win and s % blk == 0

    def body(*refs):
        ins, outs, scratch, copies = _split_refs(refs, 6, 5, rider)
        q_ref, k_ref, v_ref, do_ref, qg_ref, kg_ref = ins
        dq_ref, dk_ref, dv_ref, dqg_ref, dkg_ref = outs
        qn, kn, vb, dob, runs_ref, dqn, dkn, dvn = scratch
        finish = _ride(copies, pl.program_id(0) == 0, pl.program_id(0) == steps - 1)
        wd, hp = _Window(), _HeadPair()

        @pl.when(pl.program_id(0) == 0)
        def _():
            dqg_ref[...] = jnp.zeros_like(dqg_ref)
            dkg_ref[...] = jnp.zeros_like(dkg_ref)

        qv = q_ref[...]
        qn[...] = (qv * hp.rms_r(qv) * qg_ref[...] * scale).astype(BF16)
        kv = k_ref[...]
        kn[...] = (kv * hp.rms_r(kv) * kg_ref[...]).astype(BF16)
        vb[...] = v_ref[...].astype(BF16)
        dob[...] = do_ref[...].astype(BF16)
        dkn[...] = jnp.zeros_like(dkn)
        dvn[...] = jnp.zeros_like(dvn)

        def q_step(qi, _):
            qoff = pl.multiple_of(qi * blk, blk)
            qt = qn[pl.ds(qoff, blk), :]
            dot = dob[pl.ds(qoff, blk), :]
            qts = [hp.only(h, qt) for h in heads]
            dots = [hp.only(h, dot) for h in heads]

            zero = lambda cols: tuple(jnp.zeros((blk, cols), F32) for _ in heads)

            def logs_of(g):
                start, valid = wd.place(qi, g)
                kt = kn[pl.ds(start, win), :]
                return [_sb_logs(_dot_nt(qts[h], kt), valid) for h in heads]

            def row_sums(logs):
                return tuple(jnp.sum(logs[h][1], axis=1, keepdims=True) for h in heads)

            def still_live(runs):
                return jnp.maximum(jnp.max(runs[0]), jnp.max(runs[1])) > EXP_ZERO

            def window_grads(g, logs, runs, esums):
                start, valid = wd.place(qi, g)
                kt = kn[pl.ds(start, win), :]
                vt = vb[pl.ds(start, win), :]
                dws = [_dot_nt(dots[h], vt) for h in heads]
                tails = [wd.sums_after(logs[h][1], runs[h])[0] for h in heads]
                wgts = [jnp.where(valid, jnp.exp(logs[h][0] + tails[h]), 0.0) for h in heads]
                es = [dws[h] * wgts[h] for h in heads]
                befores = [wd.sums_before(es[h], esums[h]) for h in heads]
                dzbs = []
                for h in heads:
                    beta = jnp.exp(logs[h][0])
                    dz = jnp.where(valid, es[h] * (1.0 - beta) - befores[h][0] * beta, 0.0)
                    dzbs.append(dz.astype(BF16))
                dkn[pl.ds(start, win), :] += _dot_tn(dzbs[0], qts[0]) + _dot_tn(dzbs[1], qts[1])
                dvn[pl.ds(start, win), :] += (_dot_tn(wgts[0].astype(BF16), dots[0])
                                              + _dot_tn(wgts[1].astype(BF16), dots[1]))
                return tuple(_dot(dzbs[h], kt) for h in heads), tuple(befores[h][1] for h in heads)

            logs0 = logs_of(0)
            runs1 = row_sums(logs0)

            def one_window():
                return window_grads(0, logs0, zero(1), zero(1))[0]

            def all_windows():
                def more(carry):
                    g, live = carry[:2]
                    return jnp.logical_and((qi + 1) * blk - g * win > 0, live > 0)

                def run_window(carry):
                    g, _, runs = carry
                    for h in heads:
                        runs_ref[h, g] = runs[h]
                    sums = row_sums(logs_of(g))
                    runs = tuple(runs[h] + sums[h] for h in heads)
                    return g + 1, still_live(runs).astype(jnp.int32), runs

                for h in heads:
                    runs_ref[h, 0] = jnp.zeros((blk, 1), F32)
                windows, _, _ = lax.while_loop(more, run_window, (jnp.int32(1), jnp.int32(1), runs1))

                def k_window(gg, carry):
                    dq_accs, esums = carry
                    g = windows - 1 - gg
                    parts, totals = window_grads(g, logs_of(g), [runs_ref[h, g] for h in heads], esums)
                    return (tuple(dq_accs[h] + parts[h] for h in heads),
                            tuple(esums[h] + totals[h] for h in heads))

                return lax.fori_loop(0, windows, k_window, (zero(2 * dh), zero(1)))[0]

            earlier_keys = (qi + 1) * blk - win > 0
            dq_accs = lax.cond(jnp.logical_and(earlier_keys, still_live(runs1)), all_windows, one_window)
            dqn[pl.ds(qoff, blk), :] = hp.merge(dq_accs)
            return 0

        lax.fori_loop(0, nq, q_step, 0)

        dq, dqg = hp.rms_bwd(qv, hp.rms_r(qv), qg_ref[...] * scale, dqn[...])
        dq_ref[...] = dq.astype(BF16)
        dqg_ref[...] += dqg * scale
        dk, dkg = hp.rms_bwd(kv, hp.rms_r(kv), kg_ref[...], dkn[...])
        dk_ref[...] = dk.astype(BF16)
        dkg_ref[...] += dkg
        dv_ref[...] = dvn[...].astype(BF16)
        finish()

    pair = lambda group: pl.BlockSpec((s, 2 * dh), lambda p: (0, group * (D_ATT // (2 * dh)) + p))
    vec2 = pl.BlockSpec((1, 2 * dh), lambda p: (0, 0))
    vec = pl.BlockSpec((1, dh), lambda p: (0, 0))
    return _call(
        body, "attn_bwd", (steps,), [pair(2), pair(3), pair(4), pair(0), vec2, vec2],
        [pair(0), pair(0), pair(0), vec, vec],
        [jax.ShapeDtypeStruct((s, D_ATT), BF16)] * 3 + [jax.ShapeDtypeStruct((1, dh), F32)] * 2,
        [proj, proj, proj, dya, jnp.tile(qg, (1, 2)), jnp.tile(kg, (1, 2))],
        scratch=[pltpu.VMEM((s, 2 * dh), BF16)] * 4 + [pltpu.VMEM((2, max_windows, blk, 1), F32)]
        + [pltpu.VMEM((s, 2 * dh), F32)] * 3, rider=rider)


def _block_diag(w):
    n, c, d = w.shape
    return jnp.einsum("ncd,nm->ncmd", w, jnp.eye(n, dtype=w.dtype)).reshape(n * c, n * d)


def _diag_blocks(full, n):
    c = full.shape[0] // n
    return jnp.stack([full[i * c:(i + 1) * c, i * c:(i + 1) * c] for i in range(n)])


FFN1 = ["ffn1_w_gate", "ffn1_w_up", "ffn1_w_down"]
FFN2 = ["ffn2_w_gate", "ffn2_w_up", "ffn2_w_down"]


def _pair_sums(gb, names, where):
    theirs = _pair_exchange([gb[n] for n in names], "pair_exchange_" + names[0])
    pair, own = _pair_sum([gb[n] for n in names], theirs, where, "pair_sum_" + names[0])
    return _chip_rider(pair, own)


def _local_step(x, tgt, stacks, conv_stack, small, where):
    gate_up, down = FFN1[:2], FFN1[2:]
    big = dict(zip(gate_up, _gather_weights([stacks[n] for n in gate_up], [])))
    wa = _block_diag(small["rg_w_a"]).astype(BF16)
    wx = _block_diag(small["rg_w_x"]).astype(BF16)

    whole = lambda names: [big[n].reshape(-1, D_MODEL) for n in names]
    soon = down + ["w_in"]
    g1, u1, hb1, ab1, *landed = _ffn_up(x, small["ffn1_norm"], *whole(gate_up),
                                        rider=_gather_rider([stacks[n] for n in soon], [conv_stack]))
    big.update(zip(soon, landed))
    x1 = _ffn_down(x, ab1, *whole(down))
    conv_w = jnp.transpose(landed[-1], (1, 0, 2)).reshape(CONV_W, D_RNN)
    rg = (conv_w, small["conv_b"], wa, small["rg_b_a"], wx, small["rg_b_x"], small["rg_lambda"])
    riding = lambda names: _gather_rider([stacks[n] for n in names], [])
    proj, hb2, big["ffn2_w_gate"] = _mix_pre(x1, small["mix_norm"], big["w_in"], riding(["ffn2_w_gate"]))
    yr, hseq, big["ffn2_w_up"] = _rglru_fwd(proj, *rg, riding(["ffn2_w_up"]))
    ya, big["ffn2_w_down"], big["w_out"] = _attn_fwd(proj, small["q_norm"], small["k_norm"],
                                                     riding(["ffn2_w_down", "w_out"]))
    wout = big["w_out"].reshape(D_MODEL, D_MODEL)
    x2 = _mix_post(x1, yr, ya, small["rnn_out_norm"], small["attn_out_norm"], wout)
    dx3, g2, u2, hb3, ab3, loss = _ffn_fwd_loss(x2, small["ffn2_norm"], *whole(FFN2), tgt)

    gb, gs, slots = {}, {}, {}
    dx2, dg2, du2, dyb2, gs["ffn2_norm"] = _ffn_bwd_act(x2, small["ffn2_norm"], dx3, g2, u2, *whole(FFN2), "ffn2_bwd")
    gb["ffn2_w_gate"] = _ffn_wgrad(dg2, hb3, 1.0, "wgrad_gate_ffn2")
    gb["ffn2_w_up"] = _ffn_wgrad(du2, hb3, 1.0, "wgrad_up_ffn2")
    gb["ffn2_w_down"] = _ffn_wgrad(ab3, dyb2, 0.5, "wgrad_down_ffn2")
    dyr, dya, ycat, dxb2, gs["rnn_out_norm"], gs["attn_out_norm"] = _mix_post_bwd(
        dx2, yr, ya, small["rnn_out_norm"], small["attn_out_norm"], wout)
    gb["w_out"] = _wgrad_whole(ycat, dxb2, False, "wgrad_out")
    early = FFN2 + ["w_out"]
    dq, dk, dv, gs["q_norm"], gs["k_norm"], *done = _attn_bwd(
        proj, dya, small["q_norm"], small["k_norm"], _pair_sums(gb, early, where))
    slots.update(zip(early, done))
    dxr, dgate, gs["conv_w"], gs["conv_b"], dwa, gs["rg_b_a"], dwx, gs["rg_b_x"], gs["rg_lambda"] = _rglru_bwd(
        proj, hseq, dyr, *rg)
    gs["rg_w_a"] = _diag_blocks(dwa, RNN_BLOCKS)
    gs["rg_w_x"] = _diag_blocks(dwx, RNN_BLOCKS)
    dpb = jnp.concatenate([dxr, dgate, dq, dk, dv], axis=1)
    dx1, gs["mix_norm"] = _mix_pre_bwd(x1, small["mix_norm"], dx2, dpb, big["w_in"])
    dx0, dg1, du1, dyb1, gs["ffn1_norm"] = _ffn_bwd_act(x, small["ffn1_norm"], dx1, g1, u1, *whole(FFN1), "ffn1_bwd")

    mine = _place_shard(_pack([gs[n] for n in SMALL] + [loss[:, :1]]), where, F32, "place_small_grads",
                        by_device=True)
    gb["ffn1_w_gate"], everyone = _ffn_wgrad(dg1, hb1, 1.0, "wgrad_gate_ffn1", _small_rider(mine))
    gb["ffn1_w_up"], slots["ffn1_w_gate"] = _ffn_wgrad(
        du1, hb1, 1.0, "wgrad_up_ffn1", _pair_sums(gb, ["ffn1_w_gate"], where))
    gb["ffn1_w_down"], slots["ffn1_w_up"] = _ffn_wgrad(
        ab1, dyb1, 0.5, "wgrad_down_ffn1", _pair_sums(gb, ["ffn1_w_up"], where))
    gb["w_in"], slots["ffn1_w_down"] = _wgrad_whole(
        hb2, dpb, True, "wgrad_in", _pair_sums(gb, ["ffn1_w_down"], where))
    last = _pair_sums(gb, ["w_in"], where)
    slots["w_in"], = _chip_exchange(last.plain, last.inplace)
    return dx0, slots, gs, everyone


ANY = pl.BlockSpec(memory_space=pl.ANY)


def _place():
    x, y, c = lax.axis_index("x"), lax.axis_index("y"), lax.axis_index("c")
    other_chips = [(1 - x, y), (x, 1 - y), (1 - x, 1 - y)]
    return x, y, c, 2 * x + y, other_chips


def _remote(src, dst, send_sem, recv_sem, to):
    return pltpu.make_async_remote_copy(src_ref=src, dst_ref=dst, send_sem=send_sem, recv_sem=recv_sem,
                                        device_id=to, device_id_type=MESH)


def _copy_plan(pairs):
    sends = [functools.partial(_remote, *a) for a, _ in pairs]
    arrivals = [functools.partial(_remote, *b) for _, b in pairs]
    return sends, arrivals


class _Rider:
    def __init__(self, plan, plain, inplace, n_copies=None, relay=None, n_relay=0):
        self.plan, self.plain, self.inplace = plan, list(plain), list(inplace)
        self.n_copies = n_copies or 3 * len(self.inplace)
        self.relay, self.n_relay = relay, n_relay

    def operands(self):
        return self.plain + self.inplace

    def out_shape(self):
        return [jax.ShapeDtypeStruct(a.shape, a.dtype) for a in self.inplace]

    def aliases(self, inputs_before, outputs_before):
        return {inputs_before + len(self.plain) + k: outputs_before + k for k in range(len(self.inplace))}

    def scratch(self):
        relay = [pltpu.SemaphoreType.DMA((self.n_relay,))] * 2 if self.relay else []
        return [pltpu.SemaphoreType.DMA((self.n_copies,))] * 2 + relay


def _split_refs(refs, n_in, n_out, rider):
    if rider is None:
        return refs[:n_in], refs[n_in:n_in + n_out], refs[n_in + n_out:], None
    r_in, r_out = len(rider.operands()), len(rider.inplace)
    outs_at = n_in + r_in
    n_sems = len(rider.scratch())
    rest = refs[outs_at + n_out + r_out:]
    sems = rest[len(rest) - n_sems:]
    filled = refs[outs_at + n_out:outs_at + n_out + r_out]
    copies = functools.partial(rider.plan, refs[n_in:n_in + len(rider.plain)], filled, *sems[:2])
    relay = functools.partial(rider.relay, filled, *sems[2:]) if rider.relay else None
    return refs[:n_in], refs[outs_at:outs_at + n_out], rest[:len(rest) - n_sems], (copies, relay)


def _ride(copies, first, last, middle=None):
    if copies is None:
        return lambda: None
    copies, relay = copies

    @pl.when(first)
    def _():
        _start(copies()[0])

    def start_relay():
        for make in copies()[1]:
            make().wait_recv()
        _start(relay()[0])

    if relay is not None and middle is not None:
        pl.when(middle)(start_relay)

    def finish():
        @pl.when(last)
        def _():
            if relay is None:
                _finish(*copies())
            else:
                if middle is None:
                    start_relay()
                _finish(copies()[0] + relay()[0], relay()[1])

    return finish


def _gather_rider(split, whole):
    n_split = len(split)
    return _Rider(lambda plain, stacks, ss, rs: _gather_ici(stacks, n_split, ss, rs), [], list(split) + list(whole),
                  relay=lambda stacks, ss, rs: _gather_d2d(stacks[:n_split], ss, rs), n_relay=3 * n_split)


def _chip_rider(sums, slots):
    return _Rider(_chip_copies, sums, slots)


def _start(makers):
    for make in makers:
        make().start()


def _finish(sends, arrivals):
    for make in arrivals:
        make().wait_recv()
    for make in sends:
        make().wait_send()


def _half(rows, c):
    return pl.ds(pl.multiple_of(c * rows, BF16_ROWS), rows)


def _gather_weights(split, whole):
    arrs = list(split) + list(whole)
    n, ns = len(arrs), len(split)

    def body(*refs):
        outs = refs[n:2 * n]
        send_sems, recv_sems, fsend_sems, frecv_sems = refs[2 * n:]
        sends, arrivals = _gather_ici(outs, ns, send_sems, recv_sems)
        passes, passed = _gather_d2d(outs[:ns], fsend_sems, frecv_sems)
        _start(sends)
        for k, make in enumerate(arrivals):
            make().wait_recv()
            if k < 3 * ns:
                passes[k]().start()
        _finish(sends + passes, passed)

    return pl.pallas_call(
        body, name="gather_weights",
        in_specs=[ANY] * n, out_specs=[ANY] * n,
        out_shape=[jax.ShapeDtypeStruct(a.shape, a.dtype) for a in arrs],
        input_output_aliases={i: i for i in range(n)},
        scratch_shapes=[pltpu.SemaphoreType.DMA((3 * n,)), pltpu.SemaphoreType.DMA((3 * n,)),
                        pltpu.SemaphoreType.DMA((3 * ns,)), pltpu.SemaphoreType.DMA((3 * ns,))],
    )(*arrs)


def _gather_ici(stacks, n_split, send_sems, recv_sems):
    x, y, c, me, chips = _place()

    def region(i, chip):
        if i < n_split:
            return stacks[i].at[chip, _half(stacks[i].shape[1] // 2, c)]
        return stacks[i].at[chip]

    pairs = []
    for i in range(len(stacks)):
        for p, (cx, cy) in enumerate(chips):
            k = 3 * i + p
            mine, got = region(i, me), region(i, 2 * cx + cy)
            sems, to = (send_sems.at[k], recv_sems.at[k]), (cx, cy, c)
            pairs.append(((mine, mine, *sems, to), (got, got, *sems, to)))
    return _copy_plan(pairs)


def _gather_d2d(stacks, send_sems, recv_sems):
    x, y, c, _, chips = _place()
    sibling = (x, y, 1 - c)
    pairs = []
    for i, stack in enumerate(stacks):
        rows = stack.shape[1] // 2
        for p, (cx, cy) in enumerate(chips):
            k = 3 * i + p
            got, theirs = stack.at[2 * cx + cy, _half(rows, c)], stack.at[2 * cx + cy, _half(rows, 1 - c)]
            sems = (send_sems.at[k], recv_sems.at[k])
            pairs.append(((got, got, *sems, sibling), (theirs, theirs, *sems, sibling)))
    return _copy_plan(pairs)


def _pair_exchange(grads, name):
    n = len(grads)

    def body(*refs):
        ins, theirs = refs[:n], refs[n:2 * n]
        send_sems, recv_sems = refs[2 * n:]
        x, y, c, _, _ = _place()
        sibling = (x, y, 1 - c)
        sends = [_remote(ins[k].at[:, _half(grads[k].shape[1] // 2, 1 - c)], theirs[k],
                         send_sems.at[k], recv_sems.at[k], sibling) for k in range(n)]
        for cp in sends:
            cp.start()
        for k in range(n):
            _remote(theirs[k], theirs[k], send_sems.at[k], recv_sems.at[k], sibling).wait_recv()
        for cp in sends:
            cp.wait_send()

    return pl.pallas_call(
        body, name=name,
        in_specs=[ANY] * n, out_specs=[ANY] * n,
        out_shape=[jax.ShapeDtypeStruct((g.shape[0], g.shape[1] // 2, g.shape[2]), g.dtype) for g in grads],
        scratch_shapes=[pltpu.SemaphoreType.DMA((n,))] * 2,
    )(*grads)


def _chip_exchange(sums, slots):
    n = len(sums)

    def body(*refs):
        sends, arrivals = _chip_copies(refs[:n], refs[2 * n:3 * n], *refs[3 * n:])
        _start(sends)
        _finish(sends, arrivals)

    return pl.pallas_call(
        body, name="grad_chip_exchange",
        in_specs=[ANY] * (2 * n), out_specs=[ANY] * n,
        out_shape=[jax.ShapeDtypeStruct(a.shape, a.dtype) for a in slots],
        input_output_aliases={n + k: k for k in range(n)},
        scratch_shapes=[pltpu.SemaphoreType.DMA((3 * n,)), pltpu.SemaphoreType.DMA((3 * n,))],
    )(*sums, *slots)


def _chip_copies(sums, slots, send_sems, recv_sems):
    x, y, c, me, chips = _place()
    pairs = []
    for k in range(len(sums)):
        for p, (cx, cy) in enumerate(chips):
            j = 3 * k + p
            got = slots[k].at[2 * cx + cy]
            sems, to = (send_sems.at[j], recv_sems.at[j]), (cx, cy, c)
            pairs.append(((sums[k].at[2 * cx + cy], slots[k].at[me], *sems, to), (got, got, *sems, to)))
    return _copy_plan(pairs)


def _half_swap(halves):
    n = len(halves)

    def body(*refs):
        outs = refs[n:2 * n]
        send_sems, recv_sems = refs[2 * n:]
        x, y, c, _, _ = _place()
        sibling = (x, y, 1 - c)
        sends = [_remote(outs[k].at[c], outs[k].at[c], send_sems.at[k], recv_sems.at[k], sibling) for k in range(n)]
        for cp in sends:
            cp.start()
        for k in range(n):
            got = outs[k].at[1 - c]
            _remote(got, got, send_sems.at[k], recv_sems.at[k], sibling).wait_recv()
        for cp in sends:
            cp.wait_send()

    return pl.pallas_call(
        body, name="grad_half_swap",
        in_specs=[ANY] * n, out_specs=[ANY] * n,
        out_shape=[jax.ShapeDtypeStruct(a.shape, a.dtype) for a in halves],
        input_output_aliases={k: k for k in range(n)},
        scratch_shapes=[pltpu.SemaphoreType.DMA((n,))] * 2,
    )(*halves)


def _small_rider(stack):
    n_dev = 2 * N_CHIPS

    def plan(_, stacks, send_sems, recv_sems):
        x, y, c, _, _ = _place()
        mine = stacks[0].at[4 * x + 2 * y + c]
        pairs = []
        for k in range(1, n_dev):
            px, py, pc = x ^ ((k >> 2) & 1), y ^ ((k >> 1) & 1), c ^ (k & 1)
            got = stacks[0].at[4 * px + 2 * py + pc]
            sems = (send_sems.at[k - 1], recv_sems.at[k - 1])
            pairs.append(((mine, mine, *sems, (px, py, pc)), (got, got, *sems, (px, py, pc))))
        return _copy_plan(pairs)

    return _Rider(plan, [], [stack], n_dev - 1)


def _row_tile(r):
    return r // 4 if r >= 256 and (r // 4) % BF16_ROWS == 0 else r


def _prefetch_call(body, name, grid, in_specs, out_specs, out_shape):
    spec = pltpu.PrefetchScalarGridSpec(num_scalar_prefetch=1, grid=grid, in_specs=in_specs, out_specs=out_specs)
    return pl.pallas_call(body, name=name, grid_spec=spec, out_shape=out_shape,
                          compiler_params=_params(("arbitrary",) * len(grid)))


def _place_shard(w2d, where, dtype, name, by_device=False):
    r, c = w2d.shape
    tr = _row_tile(r)
    slots = 2 * N_CHIPS if by_device else N_CHIPS
    slot = (lambda s: 2 * s[1] + s[0]) if by_device else (lambda s: s[1])

    def body(where_ref, w_ref, out_ref):
        out_ref[...] = w_ref[...].astype(dtype)

    return _prefetch_call(
        body, name, (r // tr,), [pl.BlockSpec((tr, c), lambda i, s: (i, 0))],
        pl.BlockSpec((None, tr, c), lambda i, s: (slot(s), i, 0)),
        jax.ShapeDtypeStruct((slots, r, c), dtype))(where, w2d)


def _place_shards(w2ds, where, name):
    n = len(w2ds)
    steps = N_CHIPS
    assert all(w.shape[0] % (BF16_ROWS * steps) == 0 for w in w2ds)

    def body(where_ref, *refs):
        for k in range(n):
            refs[n + k][...] = refs[k][...].astype(BF16)

    tile = lambda w: (w.shape[0] // steps, w.shape[1])
    return _prefetch_call(
        body, name, (steps,), [pl.BlockSpec(tile(w), lambda i, s: (i, 0)) for w in w2ds],
        [pl.BlockSpec((None,) + tile(w), lambda i, s: (s[1], i, 0)) for w in w2ds],
        [jax.ShapeDtypeStruct((N_CHIPS,) + w.shape, BF16) for w in w2ds])(where, *w2ds)


def _pair_sum(fulls, theirs, where, name):
    n = len(fulls)

    def body(where_ref, *refs):
        for k in range(n):
            a_ref, b_ref, out_ref, own_ref = refs[k], refs[n + k], refs[2 * n + k], refs[3 * n + k]
            total = (a_ref[...].astype(F32) + b_ref[...].astype(F32)).astype(BF16)
            out_ref[...] = total

            @pl.when(pl.program_id(0) == where_ref[1])
            def _():
                own_ref[...] = total

    half = lambda t: pl.BlockSpec((None,) + t.shape[1:], lambda j, s: (j, s[0], 0))
    blk = lambda t: pl.BlockSpec((None,) + t.shape[1:], lambda j, s: (j, 0, 0))
    own = lambda t: pl.BlockSpec((None,) + t.shape[1:], lambda j, s: (s[1], 0, 0))
    shapes = [jax.ShapeDtypeStruct(t.shape, BF16) for t in theirs]
    outs = _prefetch_call(
        body, name, (N_CHIPS,), [half(t) for t in theirs] + [blk(t) for t in theirs],
        [blk(t) for t in theirs] + [own(t) for t in theirs], shapes + shapes)(where, *fulls, *theirs)
    return outs[:n], outs[n:]


def _chip_sum(slots, where, name):
    n = len(slots)
    steps = 2
    assert all(a.shape[1] % (BF16_ROWS * steps) == 0 for a in slots)

    def body(where_ref, *refs):
        for k in range(n):
            a_ref, out_ref = refs[k], refs[n + k]
            total = a_ref[0].astype(F32)
            for j in range(1, a_ref.shape[0]):
                total = total + a_ref[j].astype(F32)
            out_ref[...] = total

    tile = lambda a: (a.shape[1] // steps, a.shape[2])
    return _prefetch_call(
        body, name, (steps,), [pl.BlockSpec((a.shape[0],) + tile(a), lambda i, s: (0, i, 0)) for a in slots],
        [pl.BlockSpec((None,) + tile(a), lambda i, s: (s[0], i, 0)) for a in slots],
        [jax.ShapeDtypeStruct((2,) + a.shape[1:], F32) for a in slots])(where, *slots)


def _slot_sum(a, name):
    nb, r, c = a.shape
    tr = _row_tile(r)

    def body(a_ref, out_ref):
        total = a_ref[0].astype(F32)
        for j in range(1, nb):
            total = total + a_ref[j].astype(F32)
        out_ref[...] = total

    return pl.pallas_call(
        body, name=name, grid=(r // tr,),
        in_specs=[pl.BlockSpec((nb, tr, c), lambda i: (0, i, 0))],
        out_specs=pl.BlockSpec((tr, c), lambda i: (i, 0)),
        out_shape=jax.ShapeDtypeStruct((r, c), F32), compiler_params=_params(("arbitrary",)),
    )(a)


def _adamw(ws, gs, ms, vs, name, steps=1):
    n = len(ws)
    c1 = 1.0 - ADAM_B1 ** ADAM_STEP
    c2 = 1.0 - ADAM_B2 ** ADAM_STEP
    assert all(w.shape[0] % steps == 0 and (steps == 1 or w.shape[0] // steps % 8 == 0) for w in ws)

    def body(*refs):
        for k in range(n):
            w_ref, g_ref, m_ref, v_ref = (refs[j * n + k] for j in range(4))
            g_out, d_ref, m2_ref, v2_ref = (refs[(4 + j) * n + k] for j in range(4))
            gv = g_ref[...]
            g_out[...] = gv
            m2 = ADAM_B1 * m_ref[...] + (1.0 - ADAM_B1) * gv
            v2 = ADAM_B2 * v_ref[...] + (1.0 - ADAM_B2) * (gv * gv)
            m2_ref[...] = m2
            v2_ref[...] = v2
            d_ref[...] = -ADAM_LR * ((m2 / c1) / (jnp.sqrt(v2 / c2) + ADAM_EPS) + ADAM_WD * w_ref[...])

    blks = [pl.BlockSpec((w.shape[0] // steps, w.shape[1]), lambda i: (i, 0)) for w in ws]
    shapes = [jax.ShapeDtypeStruct(w.shape, F32) for w in ws]
    outs = pl.pallas_call(
        body, name=name, grid=(steps,), in_specs=blks * 4, out_specs=blks * 4, out_shape=shapes * 4,
        compiler_params=_params(("arbitrary",)),
    )(*ws, *gs, *ms, *vs)
    return [outs[j * n:(j + 1) * n] for j in range(4)]


WEIGHTS = ["ffn1_norm", "ffn1_w_gate", "ffn1_w_up", "ffn1_w_down", "mix_norm", "w_in", "conv_w", "conv_b",
           "rg_w_a", "rg_b_a", "rg_w_x", "rg_b_x", "rg_lambda", "q_norm", "k_norm", "rnn_out_norm",
           "attn_out_norm", "w_out", "ffn2_norm", "ffn2_w_gate", "ffn2_w_up", "ffn2_w_down"]
BIG = ["ffn1_w_gate", "ffn1_w_up", "ffn1_w_down", "w_in", "w_out", "ffn2_w_gate", "ffn2_w_up", "ffn2_w_down"]
SMALL = [n for n in WEIGHTS if n not in BIG]
PACK_LANES = 128
PACK_ROW_ALIGN = 8


def _hidden_major(name, a):
    return jnp.transpose(a) if name.endswith(("w_gate", "w_up")) else a


def _pack(parts):
    flat = jnp.concatenate([p.reshape(-1) for p in parts])
    unit = PACK_LANES * PACK_ROW_ALIGN
    padded = -(-flat.shape[0] // unit) * unit
    return jnp.pad(flat, (0, padded - flat.shape[0])).reshape(-1, PACK_LANES)


def _unpack(packed, shapes):
    flat = packed.reshape(-1)
    out, at = [], 0
    for shp in shapes:
        size = math.prod(shp)
        out.append(flat[at:at + size].reshape(shp))
        at += size
    return out


def kernel(x, ffn1_norm, ffn1_w_gate, ffn1_w_up, ffn1_w_down, mix_norm, w_in, conv_w, conv_b, rg_w_a, rg_b_a, rg_w_x, rg_b_x, rg_lambda, q_norm, k_norm, rnn_out_norm, attn_out_norm, w_out, ffn2_norm, ffn2_w_gate, ffn2_w_up, ffn2_w_down, loss_target, m_ffn1_norm, m_ffn1_w_gate, m_ffn1_w_up, m_ffn1_w_down, m_mix_norm, m_w_in, m_conv_w, m_conv_b, m_rg_w_a, m_rg_b_a, m_rg_w_x, m_rg_b_x, m_rg_lambda, m_q_norm, m_k_norm, m_rnn_out_norm, m_attn_out_norm, m_w_out, m_ffn2_norm, m_ffn2_w_gate, m_ffn2_w_up, m_ffn2_w_down, v_ffn1_norm, v_ffn1_w_gate, v_ffn1_w_up, v_ffn1_w_down, v_mix_norm, v_w_in, v_conv_w, v_conv_b, v_rg_w_a, v_rg_b_a, v_rg_w_x, v_rg_b_x, v_rg_lambda, v_q_norm, v_k_norm, v_rnn_out_norm, v_attn_out_norm, v_w_out, v_ffn2_norm, v_ffn2_w_gate, v_ffn2_w_up, v_ffn2_w_down):
    given = dict(locals())
    w = {n: given[n] for n in WEIGHTS}
    m = {n: given["m_" + n] for n in WEIGHTS}
    v = {n: given["v_" + n] for n in WEIGHTS}
    chip = 2 * lax.axis_index("x") + lax.axis_index("y")

    where = jnp.stack([lax.axis_index("c"), chip]).astype(jnp.int32)

    stacks = dict(zip(BIG, _place_shards([_hidden_major(n, w[n][0]) for n in BIG], where, "place_weights")))
    conv_stack = _place_shard(w["conv_w"][0], where, F32, "place_conv_w")
    small = {n: (w[n][0] if w[n].ndim > 2 else w[n]) for n in SMALL if n != "conv_w"}

    grad_x, slots, gs, everyone = _local_step(x[0], loss_target[0], stacks, conv_stack, small, where)

    swapped = _half_swap(_chip_sum([slots[n] for n in BIG], where, "chip_sums"))
    g2s = [t.reshape(t.shape[0] * t.shape[1], t.shape[2]) for t in swapped]
    flat = lambda tree: [_hidden_major(n, tree[n][0]) for n in BIG]
    g2s, d2s, m2s, v2s = _adamw(flat(w), g2s, flat(m), flat(v), "adamw_weights", ADAMW_STEPS)
    grads, deltas, new_m, new_v = {}, {}, {}, {}
    for tree, parts in ((grads, g2s), (deltas, d2s), (new_m, m2s), (new_v, v2s)):
        tree.update({n: _hidden_major(n, a).reshape(w[n].shape) for n, a in zip(BIG, parts)})

    full_shapes = [gs[n].shape for n in SMALL]
    *summed, loss = _unpack(_slot_sum(everyone, "small_grad_sum"), full_shapes + [(1, 1)])
    g_parts = dict(zip(SMALL, summed))
    quarter = D_RNN // N_CHIPS
    g_parts["conv_w"] = lax.dynamic_slice_in_dim(g_parts["conv_w"], chip * quarter, quarter, axis=1)
    local_shapes = [w[n].shape for n in SMALL]
    pk = lambda tree: _pack([tree[n] for n in SMALL])
    (g_s,), (d_s,), (m_s,), (v_s,) = _adamw([pk(w)], [pk(g_parts)], [pk(m)], [pk(v)], "adamw_small")
    for tree, packed in ((grads, g_s), (deltas, d_s), (new_m, m_s), (new_v, v_s)):
        tree.update(zip(SMALL, _unpack(packed, local_shapes)))

    return (loss[0, 0], grad_x.reshape(x.shape), *[grads[n] for n in WEIGHTS], *[deltas[n] for n in WEIGHTS],
            *[new_m[n] for n in WEIGHTS], *[new_v[n] for n in WEIGHTS])
```

```python
import functools
import math

import jax
import jax.numpy as jnp
from jax import lax
from jax.experimental import pallas as pl
from jax.experimental.pallas import tpu as pltpu

F32 = jnp.float32
BF16 = jnp.bfloat16
MESH = pl.DeviceIdType.MESH

D_MODEL = 1024
N_CHIPS = 4
D_RNN = 512
D_ATT = 512
N_HEADS = 8
HEAD_DIM = 64
RNN_BLOCKS = 8
CONV_W = 4
RG_C = 8.0
N_IN = 2 * D_RNN + 3 * D_ATT
EPS = 1e-6
ATT_BLOCK = 128
ATT_WINDOW = 384
ATT_SPLIT = 256
EXP_ZERO = -105.0

ADAM_LR = 0.001
ADAM_B1 = 0.9
ADAM_B2 = 0.999
ADAM_EPS = 1e-08
ADAM_WD = 0.01
ADAM_STEP = 10

V7X_VMEM_LIMIT = 60 * 1024 * 1024
V7X_MXU_WIDTH = 256
TOKEN_TILE = 512
SUBLANES = 8
BF16_ROWS = 16
FFN_TILE = 256
WGRAD_TILE = 2048
WHOLE_TILE = 1024
ADAMW_STEPS = 8

GELU_K0 = math.sqrt(2.0 / math.pi)
GELU_K1 = 0.044715


def _params(sem=None):
    return pltpu.CompilerParams(dimension_semantics=sem, vmem_limit_bytes=V7X_VMEM_LIMIT)


def _dot(a, b):
    return jnp.dot(a, b, preferred_element_type=F32)


def _dot_nt(a, b):
    return lax.dot_general(a, b, (((1,), (1,)), ((), ())), preferred_element_type=F32)


def _dot_tn(a, b):
    return lax.dot_general(a, b, (((0,), (0,)), ((), ())), preferred_element_type=F32)


def _sigmoid(x):
    return 1.0 / (1.0 + jnp.exp(-x))


def _rms_r(xv):
    return lax.rsqrt(jnp.mean(xv * xv, axis=-1, keepdims=True) + EPS)


def _rms_bwd(xv, r, nw, dh):
    t = dh * nw
    dx = r * t - xv * (r * r * r * jnp.mean(t * xv, axis=-1, keepdims=True))
    dn = jnp.sum(dh * xv * r, axis=0, keepdims=True)
    return dx, dn


def _gelu(x):
    t = jnp.tanh(GELU_K0 * (x + GELU_K1 * x * x * x))
    return 0.5 * x * (1.0 + t)


def _gelu_grad(x):
    t = jnp.tanh(GELU_K0 * (x + GELU_K1 * x * x * x))
    return 0.5 * (1.0 + t) + 0.5 * x * (1.0 - t * t) * (GELU_K0 * (1.0 + 3.0 * GELU_K1 * x * x))


def _expm1_neg(x):
    p = 1.0 + x * (1.0 / 6.0)
    for k in (5.0, 4.0, 3.0, 2.0):
        p = 1.0 + x * (1.0 / k) * p
    return jnp.where(x > -0.25, x * p, jnp.exp(x) - 1.0)


def _log_sigmoid(x):
    return jnp.minimum(x, 0.0) - jnp.log(1.0 + jnp.exp(-jnp.abs(x)))


def _tile(s):
    return min(TOKEN_TILE, s)


def _ffn_chunks(f):
    cut = f // 2 // V7X_MXU_WIDTH * V7X_MXU_WIDTH
    return ((0, cut), (cut, f)) if 0 < cut < f else ((0, f),)


def _ffn_fwd_loss(x, nw, wg, wu, wd, tgt):
    s, d = x.shape
    f = wg.shape[0]
    tm = min(FFN_TILE, s)
    ni = s // tm
    assert s % tm == 0

    def body(x_ref, nw_ref, wg_ref, wu_ref, wd_ref, tgt_ref, out_ref, g_ref, u_ref, hb_ref, ab_ref, loss_ref):
        i = pl.program_id(0)
        xv = x_ref[...]
        hb = (xv * _rms_r(xv) * nw_ref[...]).astype(BF16)
        hb_ref[...] = hb
        y = jnp.zeros((tm, d), F32)
        for lo, hi in _ffn_chunks(f):
            g = _dot_nt(hb, wg_ref[lo:hi, :])
            u = _dot_nt(hb, wu_ref[lo:hi, :])
            g_ref[:, lo:hi] = g.astype(BF16)
            u_ref[:, lo:hi] = u.astype(BF16)
            ab = (g * _sigmoid(g) * u).astype(BF16)
            ab_ref[:, lo:hi] = ab
            y = y + _dot(ab, wd_ref[lo:hi, :])
        diff = xv + 0.5 * y - tgt_ref[...]
        out_ref[...] = diff * (1.0 / d)

        @pl.when(i == 0)
        def _():
            loss_ref[...] = jnp.zeros_like(loss_ref)

        loss_ref[...] += jnp.sum(diff * diff) * (0.5 / d)

    row = pl.BlockSpec((tm, d), lambda i: (i, 0))
    weight = pl.BlockSpec((f, d), lambda i: (0, 0), pipeline_mode=pl.Buffered(1))
    blk = pl.BlockSpec((tm, f), lambda i: (i, 0))
    wide = jax.ShapeDtypeStruct((s, f), BF16)
    return _call(body, "ffn_fwd_loss", (ni,),
                 [row, pl.BlockSpec((1, d), lambda i: (0, 0)), weight, weight, weight, row],
                 [row, blk, blk, row, blk, pl.BlockSpec((1, 128), lambda i: (0, 0))],
                 [jax.ShapeDtypeStruct((s, d), F32), wide, wide, jax.ShapeDtypeStruct((s, d), BF16), wide,
                  jax.ShapeDtypeStruct((1, 128), F32)], [x, nw, wg, wu, wd, tgt])


def _ffn_up(x, nw, wg, wu, rider=None):
    s, d = x.shape
    f = wg.shape[0]
    tm = min(FFN_TILE, s)
    ni = s // tm
    assert s % tm == 0

    def body(*refs):
        (x_ref, nw_ref, wg_ref, wu_ref), (g_ref, u_ref, hb_ref, ab_ref), _, copies = _split_refs(refs, 4, 4, rider)
        i = pl.program_id(0)
        finish = _ride(copies, i == 0, i == ni - 1)
        xv = x_ref[...]
        hb = (xv * _rms_r(xv) * nw_ref[...]).astype(BF16)
        hb_ref[...] = hb
        for lo, hi in _ffn_chunks(f):
            g = _dot_nt(hb, wg_ref[lo:hi, :])
            u = _dot_nt(hb, wu_ref[lo:hi, :])
            g_ref[:, lo:hi] = g.astype(BF16)
            u_ref[:, lo:hi] = u.astype(BF16)
            ab_ref[:, lo:hi] = (g * _sigmoid(g) * u).astype(BF16)
        finish()

    row = pl.BlockSpec((tm, d), lambda i: (i, 0))
    weight = pl.BlockSpec((f, d), lambda i: (0, 0), pipeline_mode=pl.Buffered(1))
    blk = pl.BlockSpec((tm, f), lambda i: (i, 0))
    wide = jax.ShapeDtypeStruct((s, f), BF16)
    return _call(body, "ffn_up", (ni,), [row, pl.BlockSpec((1, d), lambda i: (0, 0)), weight, weight],
                 [blk, blk, row, blk], [wide, wide, jax.ShapeDtypeStruct((s, d), BF16), wide], [x, nw, wg, wu],
                 rider=rider)


def _ffn_down(x, ab, wd):
    s, d = x.shape
    f = wd.shape[0]
    tm = _tile(s)
    assert s % tm == 0

    def body(x_ref, ab_ref, wd_ref, out_ref):
        out_ref[...] = x_ref[...] + 0.5 * _dot(ab_ref[...], wd_ref[...])

    row = pl.BlockSpec((tm, d), lambda i: (i, 0))
    return _call(body, "ffn_down", (s // tm,),
                 [row, pl.BlockSpec((tm, f), lambda i: (i, 0)),
                  pl.BlockSpec((f, d), lambda i: (0, 0), pipeline_mode=pl.Buffered(1))],
                 [row], [jax.ShapeDtypeStruct((s, d), F32)], [x, ab, wd])[0]


def _call(body, name, grid, in_specs, out_specs, out_shape, args, scratch=(), rider=None):
    in_specs, out_specs, out_shape, scratch = list(in_specs), list(out_specs), list(out_shape), list(scratch)
    extra, aliases = [], {}
    if rider is not None:
        extra = rider.operands()
        aliases = rider.aliases(len(args), len(out_shape))
        in_specs += [ANY] * len(extra)
        out_specs += [ANY] * len(rider.inplace)
        out_shape += rider.out_shape()
        scratch += rider.scratch()
    return pl.pallas_call(
        body, name=name, grid=grid, in_specs=in_specs, out_specs=out_specs, out_shape=out_shape,
        input_output_aliases=aliases, scratch_shapes=scratch,
        compiler_params=_params(("arbitrary",) * len(grid)),
    )(*args, *extra)


def _ffn_bwd_act(x, nw, dy, g, u, wg, wu, wd, name):
    s, d = x.shape
    f = wg.shape[0]
    tm = min(FFN_TILE, s)
    assert s % tm == 0

    def body(x_ref, nw_ref, dy_ref, g_ref, u_ref, wg_ref, wu_ref, wd_ref,
             dx_ref, dg_ref, du_ref, dyb_ref, dnw_ref):
        dyv = dy_ref[...]
        dyb = dyv.astype(BF16)
        dyb_ref[...] = dyb
        dh = jnp.zeros((tm, d), F32)
        for lo, hi in _ffn_chunks(f):
            da = 0.5 * _dot_nt(dyb, wd_ref[lo:hi, :])
            gv = g_ref[:, lo:hi].astype(F32)
            sg = _sigmoid(gv)
            dub = (da * (gv * sg)).astype(BF16)
            dgb = (da * u_ref[:, lo:hi].astype(F32) * (sg * (1.0 + gv * (1.0 - sg)))).astype(BF16)
            dg_ref[:, lo:hi] = dgb
            du_ref[:, lo:hi] = dub
            dh = dh + _dot(dgb, wg_ref[lo:hi, :]) + _dot(dub, wu_ref[lo:hi, :])
        xv = x_ref[...]
        dx, dn = _rms_bwd(xv, _rms_r(xv), nw_ref[...], dh)
        dx_ref[...] = dyv + dx

        @pl.when(pl.program_id(0) == 0)
        def _():
            dnw_ref[...] = jnp.zeros_like(dnw_ref)

        dnw_ref[...] += dn

    row = pl.BlockSpec((tm, d), lambda i: (i, 0))
    vec = pl.BlockSpec((1, d), lambda i: (0, 0))
    blk = pl.BlockSpec((tm, f), lambda i: (i, 0))
    weight = pl.BlockSpec((f, d), lambda i: (0, 0), pipeline_mode=pl.Buffered(1))
    return _call(
        body, name, (s // tm,), [row, vec, row, blk, blk, weight, weight, weight], [row, blk, blk, row, vec],
        [jax.ShapeDtypeStruct((s, d), F32), jax.ShapeDtypeStruct((s, f), BF16),
         jax.ShapeDtypeStruct((s, f), BF16), jax.ShapeDtypeStruct((s, d), BF16),
         jax.ShapeDtypeStruct((1, d), F32)],
        [x, nw, dy, g, u, wg, wu, wd])


def _wgrad(a, b, a_spec, b_spec, out_rows, out_cols, scale, name, tk, rider=None, per_step=1):
    s = a.shape[-2]
    nk = s // tk
    steps = N_CHIPS // per_step
    assert s % tk == 0

    def body(*refs):
        (a_ref, b_ref), (out_ref,), (acc,), copies = _split_refs(refs, 2, 1, rider)
        j, k = pl.program_id(0), pl.program_id(1)
        finish = _ride(copies, jnp.logical_and(j == 0, k == 0), jnp.logical_and(j == steps - 1, k == nk - 1))

        @pl.when(k == 0)
        def _():
            acc[...] = jnp.zeros_like(acc)

        acc[...] += _dot_tn(a_ref[...], b_ref[...])

        @pl.when(k == nk - 1)
        def _():
            for t in range(per_step):
                out_ref[t] = (acc[t * out_rows:(t + 1) * out_rows, :] * scale).astype(BF16)

        finish()

    outs = _call(
        body, name, (steps, nk), [a_spec(tk), b_spec(tk)],
        [pl.BlockSpec((per_step, out_rows, out_cols), lambda j, k: (j, 0, 0))],
        [jax.ShapeDtypeStruct((N_CHIPS, out_rows, out_cols), BF16)], [a, b],
        scratch=[pltpu.VMEM((per_step * out_rows, out_cols), F32)], rider=rider)
    return outs[0] if rider is None else outs


def _wgrad_whole(a, b, col_blocks, name, rider=None):
    s, m = a.shape
    n = b.shape[1]
    tk = min(WHOLE_TILE, s)
    nk = s // tk
    assert s % tk == 0
    out_shape = (N_CHIPS, m, n // N_CHIPS) if col_blocks else (N_CHIPS, m // N_CHIPS, n)

    def body(*refs):
        (a_ref, b_ref), (out_ref,), (acc,), copies = _split_refs(refs, 2, 1, rider)
        k = pl.program_id(0)
        finish = _ride(copies, k == 0, k == nk - 1)

        @pl.when(k == 0)
        def _():
            acc[...] = jnp.zeros_like(acc)

        acc[...] += _dot_tn(a_ref[...], b_ref[...])

        @pl.when(k == nk - 1)
        def _():
            for j in range(N_CHIPS):
                if col_blocks:
                    out_ref[j] = acc[:, j * out_shape[2]:(j + 1) * out_shape[2]].astype(BF16)
                else:
                    out_ref[j] = acc[j * out_shape[1]:(j + 1) * out_shape[1], :].astype(BF16)

        finish()

    outs = _call(
        body, name, (nk,), [pl.BlockSpec((tk, m), lambda k: (k, 0)), pl.BlockSpec((tk, n), lambda k: (k, 0))],
        [pl.BlockSpec(out_shape, lambda k: (0, 0, 0))], [jax.ShapeDtypeStruct(out_shape, BF16)], [a, b],
        scratch=[pltpu.VMEM((m, n), F32)], rider=rider)
    return outs[0] if rider is None else outs


def _ffn_wgrad(hidden, shared, scale, name, rider=None):
    s, d = shared.shape
    half = hidden.shape[1] // 2
    return _wgrad(hidden, shared, lambda tk: pl.BlockSpec((tk, half), lambda j, k: (k, j)),
                  lambda tk: pl.BlockSpec((tk, d), lambda j, k: (k, 0)), half // 2, d, scale, name,
                  min(WGRAD_TILE, s), rider, per_step=2)


def _mix_pre(x, nw, win, rider=None):
    s, d = x.shape
    nb, _, cb = win.shape
    tm = _tile(s)
    ni = s // tm
    assert s % tm == 0

    def body(*refs):
        (x_ref, nw_ref, w_ref), (p_ref, hb_ref), _, copies = _split_refs(refs, 3, 2, rider)
        finish = _ride(copies, pl.program_id(0) == 0, pl.program_id(0) == ni - 1)
        xv = x_ref[...]
        hb = (xv * _rms_r(xv) * nw_ref[...]).astype(BF16)
        hb_ref[...] = hb
        for j in range(nb):
            p_ref[:, j * cb:(j + 1) * cb] = _dot(hb, w_ref[j])
        finish()

    row = pl.BlockSpec((tm, d), lambda i: (i, 0))
    return _call(
        body, "mix_pre", (ni,),
        [row, pl.BlockSpec((1, d), lambda i: (0, 0)),
         pl.BlockSpec((nb, d, cb), lambda i: (0, 0, 0), pipeline_mode=pl.Buffered(1))],
        [pl.BlockSpec((tm, nb * cb), lambda i: (i, 0)), row],
        [jax.ShapeDtypeStruct((s, nb * cb), F32), jax.ShapeDtypeStruct((s, d), BF16)], [x, nw, win], rider=rider)


def _mix_pre_bwd(x, nw, dres, dpb, win):
    s, d = x.shape
    nb, _, cb = win.shape
    tm = _tile(s)
    assert s % tm == 0

    def body(x_ref, nw_ref, dres_ref, dp_ref, w_ref, dx_ref, dnw_ref):
        dh = jnp.zeros((tm, d), F32)
        for j in range(nb):
            dh = dh + _dot_nt(dp_ref[:, j * cb:(j + 1) * cb], w_ref[j])
        xv = x_ref[...]
        dx, dn = _rms_bwd(xv, _rms_r(xv), nw_ref[...], dh)
        dx_ref[...] = dres_ref[...] + dx

        @pl.when(pl.program_id(0) == 0)
        def _():
            dnw_ref[...] = jnp.zeros_like(dnw_ref)

        dnw_ref[...] += dn

    row = pl.BlockSpec((tm, d), lambda i: (i, 0))
    vec = pl.BlockSpec((1, d), lambda i: (0, 0))
    return pl.pallas_call(
        body, name="mix_pre_bwd", grid=(s // tm,),
        in_specs=[row, vec, row, pl.BlockSpec((tm, nb * cb), lambda i: (i, 0)),
                  pl.BlockSpec((nb, d, cb), lambda i: (0, 0, 0), pipeline_mode=pl.Buffered(1))],
        out_specs=[row, vec],
        out_shape=[jax.ShapeDtypeStruct((s, d), F32), jax.ShapeDtypeStruct((1, d), F32)],
        compiler_params=_params(("arbitrary",)),
    )(x, nw, dres, dpb, win)


def _mix_post(x, yr, ya, nr, na, wout):
    s, d = x.shape
    h = yr.shape[1]
    tm = _tile(s)

    def body(x_ref, yr_ref, ya_ref, nr_ref, na_ref, w_ref, out_ref):
        yrv = yr_ref[...]
        yav = ya_ref[...]
        onb = (yrv * _rms_r(yrv) * nr_ref[...]).astype(BF16)
        oab = (yav * _rms_r(yav) * na_ref[...]).astype(BF16)
        out_ref[...] = x_ref[...] + _dot(onb, w_ref[0:h, :]) + _dot(oab, w_ref[h:2 * h, :])

    row = pl.BlockSpec((tm, d), lambda i: (i, 0))
    half = pl.BlockSpec((tm, h), lambda i: (i, 0))
    vec = pl.BlockSpec((1, h), lambda i: (0, 0))
    return pl.pallas_call(
        body, name="mix_post", grid=(s // tm,),
        in_specs=[row, half, half, vec, vec, pl.BlockSpec((2 * h, d), lambda i: (0, 0))],
        out_specs=row, out_shape=jax.ShapeDtypeStruct((s, d), F32),
        compiler_params=_params(("arbitrary",)),
    )(x, yr, ya, nr, na, wout)


def _mix_post_bwd(dx, yr, ya, nr, na, wout):
    s, d = dx.shape
    h = yr.shape[1]
    tm = _tile(s)

    def body(dx_ref, yr_ref, ya_ref, nr_ref, na_ref, w_ref,
             dyr_ref, dya_ref, yc_ref, dxb_ref, dnr_ref, dna_ref):
        i = pl.program_id(0)
        dxb = dx_ref[...].astype(BF16)
        dxb_ref[...] = dxb
        dyc = _dot_nt(dxb, w_ref[...])
        yrv = yr_ref[...]
        yav = ya_ref[...]
        rr = _rms_r(yrv)
        ra = _rms_r(yav)
        yc_ref[:, 0:h] = (yrv * rr * nr_ref[...]).astype(BF16)
        yc_ref[:, h:2 * h] = (yav * ra * na_ref[...]).astype(BF16)
        dyr, dnr = _rms_bwd(yrv, rr, nr_ref[...], dyc[:, 0:h])
        dya, dna = _rms_bwd(yav, ra, na_ref[...], dyc[:, h:2 * h])
        dyr_ref[...] = dyr
        dya_ref[...] = dya

        @pl.when(i == 0)
        def _():
            dnr_ref[...] = jnp.zeros_like(dnr_ref)
            dna_ref[...] = jnp.zeros_like(dna_ref)

        dnr_ref[...] += dnr
        dna_ref[...] += dna

    row = pl.BlockSpec((tm, d), lambda i: (i, 0))
    half = pl.BlockSpec((tm, h), lambda i: (i, 0))
    vec = pl.BlockSpec((1, h), lambda i: (0, 0))
    return pl.pallas_call(
        body, name="mix_post_bwd", grid=(s // tm,),
        in_specs=[row, half, half, vec, vec, pl.BlockSpec((2 * h, d), lambda i: (0, 0))],
        out_specs=[half, half, pl.BlockSpec((tm, 2 * h), lambda i: (i, 0)), row, vec, vec],
        out_shape=[jax.ShapeDtypeStruct((s, h), F32), jax.ShapeDtypeStruct((s, h), F32),
                   jax.ShapeDtypeStruct((s, 2 * h), BF16), jax.ShapeDtypeStruct((s, d), BF16),
                   jax.ShapeDtypeStruct((1, h), F32), jax.ShapeDtypeStruct((1, h), F32)],
        compiler_params=_params(("arbitrary",)),
    )(dx, yr, ya, nr, na, wout)


def _shift_down(xv, s, prev8):
    rolled = pltpu.roll(xv, s, 0)
    row8 = lax.broadcasted_iota(jnp.int32, prev8.shape, 0)
    head = jnp.where(row8 < s, pltpu.roll(prev8, s, 0), rolled[0:8, :])
    return jnp.concatenate([head, rolled[8:, :]], axis=0)


def _shift_up(xv, s, next8):
    n = xv.shape[0]
    rolled = pltpu.roll(xv, n - s, 0)
    row8 = lax.broadcasted_iota(jnp.int32, next8.shape, 0)
    tail = jnp.where(row8 >= 8 - s, pltpu.roll(next8, 8 - s, 0), rolled[n - 8:, :])
    return jnp.concatenate([rolled[:n - 8, :], tail], axis=0)


def _scan_fwd(a, b):
    n = a.shape[0]
    sub = lax.broadcasted_iota(jnp.int32, a.shape, 0) % SUBLANES
    s = 1
    while s < SUBLANES:
        ok = sub >= s
        b = jnp.where(ok, a * pltpu.roll(b, s, 0) + b, b)
        a = jnp.where(ok, a * pltpu.roll(a, s, 0), a)
        s *= 2
    groups = []
    before = jnp.zeros((1, a.shape[1]), F32)
    for g in range(n // SUBLANES):
        rows = slice(g * SUBLANES, (g + 1) * SUBLANES)
        groups.append(a[rows] * before + b[rows])
        before = groups[-1][SUBLANES - 1:]
    return jnp.concatenate(groups, axis=0)


def _scan_bwd(a, b):
    n = a.shape[0]
    sub = lax.broadcasted_iota(jnp.int32, a.shape, 0) % SUBLANES
    s = 1
    while s < SUBLANES:
        ok = sub < SUBLANES - s
        b = jnp.where(ok, a * pltpu.roll(b, n - s, 0) + b, b)
        a = jnp.where(ok, a * pltpu.roll(a, n - s, 0), a)
        s *= 2
    groups = []
    after = jnp.zeros((1, a.shape[1]), F32)
    for g in reversed(range(n // SUBLANES)):
        rows = slice(g * SUBLANES, (g + 1) * SUBLANES)
        groups.append(a[rows] * after + b[rows])
        after = groups[-1][:1]
    return jnp.concatenate(groups[::-1], axis=0)


def _rglru_gates(xv, prev8, cw_ref, cb_ref, wa_ref, ba_ref, wx_ref, bx_ref, lam_ref):
    x1 = _shift_down(xv, 1, prev8)
    x2 = _shift_down(xv, 2, prev8)
    x3 = _shift_down(xv, 3, prev8)
    xc = cw_ref[3:4, :] * xv + cw_ref[2:3, :] * x1 + cw_ref[1:2, :] * x2 + cw_ref[0:1, :] * x3 + cb_ref[...]
    xcb = xc.astype(BF16)
    r = _sigmoid(_dot(xcb, wa_ref[...]) + ba_ref[...])
    ig = _sigmoid(_dot(xcb, wx_ref[...]) + bx_ref[...])
    c = RG_C * _log_sigmoid(lam_ref[...])
    la = r * c
    a = jnp.exp(la)
    m = jnp.sqrt(-_expm1_neg(2.0 * la))
    return (x1, x2, x3), xc, xcb, r, ig, c, a, m


def _rglru_fwd(proj, cw, cb, wa, ba, wx, bx, lam, rider=None):
    s = proj.shape[0]
    w = D_RNN
    tm = _tile(s)
    ni = s // tm

    def body(*refs):
        ins, (y_ref, h_ref), (prev, hlast), copies = _split_refs(refs, 9, 2, rider)
        xr_ref, gate_ref, cw_ref, cb_ref, wa_ref, ba_ref, wx_ref, bx_ref, lam_ref = ins
        finish = _ride(copies, pl.program_id(0) == 0, pl.program_id(0) == ni - 1)

        @pl.when(pl.program_id(0) == 0)
        def _():
            prev[...] = jnp.zeros_like(prev)
            hlast[...] = jnp.zeros_like(hlast)

        xv = xr_ref[...]
        _, xc, _, _, ig, _, a, m = _rglru_gates(xv, prev[...], cw_ref, cb_ref, wa_ref, ba_ref,
                                                wx_ref, bx_ref, lam_ref)
        b = m * (ig * xc)
        row = lax.broadcasted_iota(jnp.int32, b.shape, 0)
        b = jnp.where(row == 0, b + a * hlast[...], b)
        h = _scan_fwd(a, b)
        h_ref[...] = h
        y_ref[...] = h * _gelu(gate_ref[...])
        prev[...] = xv[tm - 8:, :]
        hlast[...] = h[tm - 1:tm, :]
        finish()

    vec = pl.BlockSpec((1, w), lambda i: (0, 0))
    sq = pl.BlockSpec((w, w), lambda i: (0, 0))
    out = pl.BlockSpec((tm, w), lambda i: (i, 0))
    return _call(
        body, "rglru_fwd", (ni,),
        [pl.BlockSpec((tm, w), lambda i: (i, 0)), pl.BlockSpec((tm, w), lambda i: (i, 1)),
         pl.BlockSpec((CONV_W, w), lambda i: (0, 0)), vec, sq, vec, sq, vec, vec], [out, out],
        [jax.ShapeDtypeStruct((s, w), F32), jax.ShapeDtypeStruct((s, w), F32)],
        [proj, proj, cw, cb, wa, ba, wx, bx, lam],
        scratch=[pltpu.VMEM((8, w), F32), pltpu.VMEM((1, w), F32)], rider=rider)


def _rglru_bwd(proj, hseq, dyr, cw, cb, wa, ba, wx, bx, lam):
    s = proj.shape[0]
    w = D_RNN
    tm = _tile(s)
    nt = s // tm
    t8 = tm // 8

    def body(xr_ref, xp_ref, gate_ref, h_ref, hp_ref, dy_ref, cw_ref, cb_ref, wa_ref, ba_ref,
             wx_ref, bx_ref, lam_ref,
             dxr_ref, dgate_ref, dcw_ref, dcb_ref, dwa_ref, dba_ref, dwx_ref, dbx_ref, dlam_ref,
             carry, dxc_next):
        i = pl.program_id(0)
        first_tile = i == nt - 1

        @pl.when(i == 0)
        def _():
            carry[...] = jnp.zeros_like(carry)
            dxc_next[...] = jnp.zeros_like(dxc_next)
            for ref in (dcw_ref, dcb_ref, dwa_ref, dba_ref, dwx_ref, dbx_ref, dlam_ref):
                ref[...] = jnp.zeros_like(ref)

        xv = xr_ref[...]
        prev8 = jnp.where(first_tile, 0.0, xp_ref[...])
        hprev8 = jnp.where(first_tile, 0.0, hp_ref[...])
        (x1, x2, x3), xc, xcb, r, ig, c, a, m = _rglru_gates(
            xv, prev8, cw_ref, cb_ref, wa_ref, ba_ref, wx_ref, bx_ref, lam_ref)
        gv = gate_ref[...]
        hv = h_ref[...]
        dy = dy_ref[...]
        dgate_ref[...] = (dy * hv * _gelu_grad(gv)).astype(BF16)
        dh = dy * _gelu(gv)
        row = lax.broadcasted_iota(jnp.int32, dh.shape, 0)
        dh = jnp.where(row == tm - 1, dh + carry[...], dh)
        a_up = jnp.where(row == tm - 1, 0.0, pltpu.roll(a, tm - 1, 0))
        lam_t = _scan_bwd(a_up, dh)
        carry[...] = a[0:1, :] * lam_t[0:1, :]
        hm1 = _shift_down(hv, 1, hprev8)
        da = lam_t * hm1
        ixc = ig * xc
        dm = lam_t * ixc
        dig = lam_t * m * xc
        dxc = lam_t * m * ig
        dla = da * a - dm * (a * a) / m
        dr = dla * c
        dlam_ref[...] += jnp.sum(dla * r, axis=0, keepdims=True)
        dpa = dr * r * (1.0 - r)
        dpi = dig * ig * (1.0 - ig)
        dba_ref[...] += jnp.sum(dpa, axis=0, keepdims=True)
        dbx_ref[...] += jnp.sum(dpi, axis=0, keepdims=True)
        dpab = dpa.astype(BF16)
        dpib = dpi.astype(BF16)
        dwa_ref[...] += _dot_tn(xcb, dpab)
        dwx_ref[...] += _dot_tn(xcb, dpib)
        dxc = dxc + _dot_nt(dpab, wa_ref[...]) + _dot_nt(dpib, wx_ref[...])
        dcb_ref[...] += jnp.sum(dxc, axis=0, keepdims=True)
        dcw_ref[3:4, :] += jnp.sum(dxc * xv, axis=0, keepdims=True)
        dcw_ref[2:3, :] += jnp.sum(dxc * x1, axis=0, keepdims=True)
        dcw_ref[1:2, :] += jnp.sum(dxc * x2, axis=0, keepdims=True)
        dcw_ref[0:1, :] += jnp.sum(dxc * x3, axis=0, keepdims=True)
        nxt = dxc_next[...]
        dxr = (cw_ref[3:4, :] * dxc + cw_ref[2:3, :] * _shift_up(dxc, 1, nxt)
               + cw_ref[1:2, :] * _shift_up(dxc, 2, nxt) + cw_ref[0:1, :] * _shift_up(dxc, 3, nxt))
        dxr_ref[...] = dxr.astype(BF16)
        dxc_next[...] = dxc[0:8, :]

        @pl.when(first_tile)
        def _():
            lv = lam_ref[...]
            dlam_ref[...] = dlam_ref[...] * (RG_C * _sigmoid(-lv))

    rev = lambda i: nt - 1 - i
    vec = pl.BlockSpec((1, w), lambda i: (0, 0))
    sq = pl.BlockSpec((w, w), lambda i: (0, 0))
    cur = lambda col: pl.BlockSpec((tm, w), lambda i: (rev(i), col))
    before = lambda cols: pl.BlockSpec((8, w), lambda i: (jnp.maximum(rev(i) * t8 - 1, 0), 0))
    return pl.pallas_call(
        body, name="rglru_bwd", grid=(nt,),
        in_specs=[cur(0), before(None), cur(1), cur(0), before(None), cur(0),
                  pl.BlockSpec((CONV_W, w), lambda i: (0, 0)), vec, sq, vec, sq, vec, vec],
        out_specs=[cur(0), cur(0), pl.BlockSpec((CONV_W, w), lambda i: (0, 0)), vec, sq, vec, sq, vec, vec],
        out_shape=[jax.ShapeDtypeStruct((s, w), BF16), jax.ShapeDtypeStruct((s, w), BF16),
                   jax.ShapeDtypeStruct((CONV_W, w), F32), jax.ShapeDtypeStruct((1, w), F32),
                   jax.ShapeDtypeStruct((w, w), F32), jax.ShapeDtypeStruct((1, w), F32),
                   jax.ShapeDtypeStruct((w, w), F32), jax.ShapeDtypeStruct((1, w), F32),
                   jax.ShapeDtypeStruct((1, w), F32)],
        scratch_shapes=[pltpu.VMEM((1, w), F32), pltpu.VMEM((8, w), F32)],
        compiler_params=_params(("arbitrary",)),
    )(proj, proj, proj, hseq, hseq, dyr, cw, cb, wa, ba, wx, bx, lam)


def _sb_logs(z, valid):
    lb = jnp.minimum(z, 0.0) - jnp.log(1.0 + jnp.exp(-jnp.abs(z)))
    return lb, jnp.where(valid, lb - z, 0.0)


class _Window:
    def __init__(self):
        blk, win, cut = ATT_BLOCK, ATT_WINDOW, ATT_SPLIT
        self.row = lax.broadcasted_iota(jnp.int32, (blk, win), 0)
        self.col = lax.broadcasted_iota(jnp.int32, (blk, win), 1)

        def tri(n, later):
            j = lax.broadcasted_iota(jnp.int32, (n, n), 0)
            s = lax.broadcasted_iota(jnp.int32, (n, n), 1)
            return jnp.where((j > s) if later else (j < s), 1.0, 0.0).astype(BF16)

        self.later = (tri(cut, True), tri(win - cut, True))
        self.earlier = (tri(cut, False), tri(win - cut, False))

    def place(self, qi, g):
        end = (qi + 1) * ATT_BLOCK - g * ATT_WINDOW
        start = pl.multiple_of(jnp.maximum(end - ATT_WINDOW, 0), ATT_BLOCK)
        valid = self.col < jnp.minimum(self.row + (qi * ATT_BLOCK - start), end - start)
        return start, valid

    @staticmethod
    def _parts(xv):
        hi = xv.astype(BF16)
        lo = (xv - hi.astype(F32)).astype(BF16)
        cut = ATT_SPLIT
        sums = (jnp.sum(xv[:, :cut], axis=1, keepdims=True), jnp.sum(xv[:, cut:], axis=1, keepdims=True))
        return (hi[:, :cut], lo[:, :cut]), (hi[:, cut:], lo[:, cut:]), sums

    def sums_after(self, xv, carry):
        (h0, l0), (h1, l1), (s0, s1) = self._parts(xv)
        first = _dot(h0, self.later[0]) + _dot(l0, self.later[0]) + (s1 + carry)
        last = _dot(h1, self.later[1]) + _dot(l1, self.later[1]) + carry
        return jnp.concatenate([first, last], axis=1), s0 + s1

    def sums_before(self, xv, carry):
        (h0, l0), (h1, l1), (s0, s1) = self._parts(xv)
        first = _dot(h0, self.earlier[0]) + _dot(l0, self.earlier[0]) + carry
        last = _dot(h1, self.earlier[1]) + _dot(l1, self.earlier[1]) + (s0 + carry)
        return jnp.concatenate([first, last], axis=1), s0 + s1


class _HeadPair:
    def __init__(self):
        lanes = 2 * HEAD_DIM
        lane = lax.broadcasted_iota(jnp.int32, (1, lanes), 1)
        self.masks = [lane // HEAD_DIM == h for h in (0, 1)]
        i = lax.broadcasted_iota(jnp.int32, (lanes, lanes), 0) // HEAD_DIM
        j = lax.broadcasted_iota(jnp.int32, (lanes, lanes), 1) // HEAD_DIM
        self.same_head = jnp.where(i == j, 1.0, 0.0).astype(BF16)

    def only(self, h, xv):
        return jnp.where(self.masks[h], xv, jnp.zeros_like(xv))

    def merge(self, per_head):
        return jnp.where(self.masks[0], per_head[0], per_head[1])

    def mean(self, xv):
        hi = xv.astype(BF16)
        lo = (xv - hi.astype(F32)).astype(BF16)
        return (_dot(hi, self.same_head) + _dot(lo, self.same_head)) * (1.0 / HEAD_DIM)

    def rms_r(self, xv):
        return lax.rsqrt(self.mean(xv * xv) + EPS)

    def rms_bwd(self, xv, r, nw, dh):
        t = dh * nw
        dx = r * t - xv * (r * r * r * self.mean(t * xv))
        dn = jnp.sum(dh * xv * r, axis=0, keepdims=True)
        return dx, dn[:, :HEAD_DIM] + dn[:, HEAD_DIM:]


def _attn_fwd(proj, qg, kg, rider=None):
    s = proj.shape[0]
    blk, win, dh = ATT_BLOCK, ATT_WINDOW, HEAD_DIM
    nq = s // blk
    scale = 1.0 / math.sqrt(dh)
    heads = (0, 1)
    blocks = (0, 1)
    assert s >= win and s % (blk * len(blocks)) == 0

    def body(*refs):
        (q_ref, k_ref, v_ref, qg_ref, kg_ref), (o_ref,), (qn, kn, vb), copies = _split_refs(refs, 5, 1, rider)
        finish = _ride(copies, pl.program_id(0) == 0, pl.program_id(0) == N_HEADS // 2 - 1)
        wd, hp = _Window(), _HeadPair()
        qv = q_ref[...]
        qn[...] = (qv * hp.rms_r(qv) * qg_ref[...] * scale).astype(BF16)
        kv = k_ref[...]
        kn[...] = (kv * hp.rms_r(kv) * kg_ref[...]).astype(BF16)
        vb[...] = v_ref[...].astype(BF16)

        def q_step(pair_i, _):
            qis = [2 * pair_i + b for b in blocks]
            chains = [(b, h) for b in blocks for h in heads]
            qoffs = [pl.multiple_of(qi * blk, blk) for qi in qis]
            qtiles = [qn[pl.ds(qoff, blk), :] for qoff in qoffs]
            qts = [hp.only(h, qtiles[b]) for b, h in chains]

            def more(carry):
                g, live = carry[:2]
                return jnp.logical_and((qis[-1] + 1) * blk - g * win > 0, live > 0)

            def window(carry):
                g, _, accs, runs = carry
                places = [wd.place(qi, g) for qi in qis]
                kts = [kn[pl.ds(start, win), :] for start, _ in places]
                zs = [_dot_nt(qts[c], kts[b]) for c, (b, h) in enumerate(chains)]
                logs = [_sb_logs(zs[c], places[b][1]) for c, (b, h) in enumerate(chains)]
                sums = [wd.sums_after(logs[c][1], runs[c]) for c in range(len(chains))]
                wgts = [jnp.where(places[b][1], jnp.exp(logs[c][0] + sums[c][0]), 0.0).astype(BF16)
                        for c, (b, h) in enumerate(chains)]
                vts = [vb[pl.ds(start, win), :] for start, _ in places]
                accs = tuple(accs[c] + _dot(wgts[c], vts[b]) for c, (b, h) in enumerate(chains))
                runs = tuple(runs[c] + sums[c][1] for c in range(len(chains)))
                top = functools.reduce(jnp.maximum, [jnp.max(r) for r in runs])
                return g + 1, (top > EXP_ZERO).astype(jnp.int32), accs, runs

            zero = lambda cols: tuple(jnp.zeros((blk, cols), F32) for _ in chains)
            _, _, accs, _ = lax.while_loop(more, window, (jnp.int32(0), jnp.int32(1), zero(2 * dh), zero(1)))
            for b in blocks:
                o_ref[pl.ds(qoffs[b], blk), :] = hp.merge([accs[2 * b + h] for h in heads])
            return 0

        lax.fori_loop(0, nq // len(blocks), q_step, 0)
        finish()

    pair = lambda group: pl.BlockSpec((s, 2 * dh), lambda p: (0, group * (D_ATT // (2 * dh)) + p))
    vec = pl.BlockSpec((1, 2 * dh), lambda p: (0, 0))
    return _call(
        body, "attn_fwd", (N_HEADS // 2,), [pair(2), pair(3), pair(4), vec, vec], [pair(0)],
        [jax.ShapeDtypeStruct((s, D_ATT), F32)], [proj, proj, proj, jnp.tile(qg, (1, 2)), jnp.tile(kg, (1, 2))],
        scratch=[pltpu.VMEM((s, 2 * dh), BF16)] * 3, rider=rider)


def _attn_bwd(proj, dya, qg, kg, rider=None):
    s = proj.shape[0]
    blk, win, dh = ATT_BLOCK, ATT_WINDOW, HEAD_DIM
    nq = s // blk
    max_windows = -(-s // win) + 1
    scale = 1.0 / math.sqrt(dh)
    steps = N_HEADS // 2
    heads = (0, 1)
    blocks = (0, 1)
    assert s >= win and s % (blk * len(blocks)) == 0

    def body(*refs):
        ins, outs, scratch, copies = _split_refs(refs, 6, 5, rider)
        q_ref, k_ref, v_ref, do_ref, qg_ref, kg_ref = ins
        dq_ref, dk_ref, dv_ref, dqg_ref, dkg_ref = outs
        qn, kn, vb, dob, runs_ref, dqn, dkn, dvn = scratch
        finish = _ride(copies, pl.program_id(0) == 0, pl.program_id(0) == steps - 1)
        wd, hp = _Window(), _HeadPair()

        @pl.when(pl.program_id(0) == 0)
        def _():
            dqg_ref[...] = jnp.zeros_like(dqg_ref)
            dkg_ref[...] = jnp.zeros_like(dkg_ref)

        qv = q_ref[...]
        qn[...] = (qv * hp.rms_r(qv) * qg_ref[...] * scale).astype(BF16)
        kv = k_ref[...]
        kn[...] = (kv * hp.rms_r(kv) * kg_ref[...]).astype(BF16)
        vb[...] = v_ref[...].astype(BF16)
        dob[...] = do_ref[...].astype(BF16)
        dkn[...] = jnp.zeros_like(dkn)
        dvn[...] = jnp.zeros_like(dvn)

        def q_step(pair_i, _):
            qis = [2 * pair_i + b for b in blocks]
            chains = [(b, h) for b in blocks for h in heads]
            ids = range(len(chains))
            qoffs = [pl.multiple_of(qi * blk, blk) for qi in qis]
            qts = [hp.only(h, qn[pl.ds(qoffs[b], blk), :]) for b, h in chains]
            dots = [hp.only(h, dob[pl.ds(qoffs[b], blk), :]) for b, h in chains]

            zero = lambda cols: tuple(jnp.zeros((blk, cols), F32) for _ in chains)

            def logs_of(g):
                places = [wd.place(qi, g) for qi in qis]
                kts = [kn[pl.ds(start, win), :] for start, _ in places]
                return [_sb_logs(_dot_nt(qts[c], kts[b]), places[b][1]) for c, (b, h) in enumerate(chains)]

            def row_sums(logs):
                return tuple(jnp.sum(logs[c][1], axis=1, keepdims=True) for c in ids)

            def still_live(runs):
                return functools.reduce(jnp.maximum, [jnp.max(r) for r in runs]) > EXP_ZERO

            def window_grads(g, logs, runs, esums):
                places = [wd.place(qi, g) for qi in qis]
                kts = [kn[pl.ds(start, win), :] for start, _ in places]
                vts = [vb[pl.ds(start, win), :] for start, _ in places]
                dws = [_dot_nt(dots[c], vts[b]) for c, (b, h) in enumerate(chains)]
                tails = [wd.sums_after(logs[c][1], runs[c])[0] for c in ids]
                wgts = [jnp.where(places[b][1], jnp.exp(logs[c][0] + tails[c]), 0.0) for c, (b, h) in enumerate(chains)]
                es = [dws[c] * wgts[c] for c in ids]
                befores = [wd.sums_before(es[c], esums[c]) for c in ids]
                dzbs = []
                for c, (b, h) in enumerate(chains):
                    beta = jnp.exp(logs[c][0])
                    dz = jnp.where(places[b][1], es[c] * (1.0 - beta) - befores[c][0] * beta, 0.0)
                    dzbs.append(dz.astype(BF16))
                for b in blocks:
                    rows = pl.ds(places[b][0], win)
                    dkn[rows, :] += _dot_tn(dzbs[2 * b], qts[2 * b]) + _dot_tn(dzbs[2 * b + 1], qts[2 * b + 1])
                    dvn[rows, :] += (_dot_tn(wgts[2 * b].astype(BF16), dots[2 * b])
                                     + _dot_tn(wgts[2 * b + 1].astype(BF16), dots[2 * b + 1]))
                return (tuple(_dot(dzbs[c], kts[b]) for c, (b, h) in enumerate(chains)),
                        tuple(befores[c][1] for c in ids))

            logs0 = logs_of(0)
            runs1 = row_sums(logs0)

            def one_window():
                return window_grads(0, logs0, zero(1), zero(1))[0]

            def all_windows():
                def more(carry):
                    g, live = carry[:2]
                    return jnp.logical_and((qis[-1] + 1) * blk - g * win > 0, live > 0)

                def run_window(carry):
                    g, _, runs = carry
                    for c in ids:
                        runs_ref[c, g] = runs[c]
                    sums = row_sums(logs_of(g))
                    runs = tuple(runs[c] + sums[c] for c in ids)
                    return g + 1, still_live(runs).astype(jnp.int32), runs

                for c in ids:
                    runs_ref[c, 0] = jnp.zeros((blk, 1), F32)
                windows, _, _ = lax.while_loop(more, run_window, (jnp.int32(1), jnp.int32(1), runs1))

                def k_window(gg, carry):
                    dq_accs, esums = carry
                    g = windows - 1 - gg
                    parts, totals = window_grads(g, logs_of(g), [runs_ref[c, g] for c in ids], esums)
                    return (tuple(dq_accs[c] + parts[c] for c in ids), tuple(esums[c] + totals[c] for c in ids))

                return lax.fori_loop(0, windows, k_window, (zero(2 * dh), zero(1)))[0]

            earlier_keys = (qis[-1] + 1) * blk - win > 0
            dq_accs = lax.cond(jnp.logical_and(earlier_keys, still_live(runs1)), all_windows, one_window)
            for b in blocks:
                dqn[pl.ds(qoffs[b], blk), :] = hp.merge([dq_accs[2 * b + h] for h in heads])
            return 0

        lax.fori_loop(0, nq // len(blocks), q_step, 0)

        dq, dqg = hp.rms_bwd(qv, hp.rms_r(qv), qg_ref[...] * scale, dqn[...])
        dq_ref[...] = dq.astype(BF16)
        dqg_ref[...] += dqg * scale
        dk, dkg = hp.rms_bwd(kv, hp.rms_r(kv), kg_ref[...], dkn[...])
        dk_ref[...] = dk.astype(BF16)
        dkg_ref[...] += dkg
        dv_ref[...] = dvn[...].astype(BF16)
        finish()

    pair = lambda group: pl.BlockSpec((s, 2 * dh), lambda p: (0, group * (D_ATT // (2 * dh)) + p))
    vec2 = pl.BlockSpec((1, 2 * dh), lambda p: (0, 0))
    vec = pl.BlockSpec((1, dh), lambda p: (0, 0))
    return _call(
        body, "attn_bwd", (steps,), [pair(2), pair(3), pair(4), pair(0), vec2, vec2],
        [pair(0), pair(0), pair(0), vec, vec],
        [jax.ShapeDtypeStruct((s, D_ATT), BF16)] * 3 + [jax.ShapeDtypeStruct((1, dh), F32)] * 2,
        [proj, proj, proj, dya, jnp.tile(qg, (1, 2)), jnp.tile(kg, (1, 2))],
        scratch=[pltpu.VMEM((s, 2 * dh), BF16)] * 4 + [pltpu.VMEM((4, max_windows, blk, 1), F32)]
        + [pltpu.VMEM((s, 2 * dh), F32)] * 3, rider=rider)


def _block_diag(w):
    n, c, d = w.shape
    return jnp.einsum("ncd,nm->ncmd", w, jnp.eye(n, dtype=w.dtype)).reshape(n * c, n * d)


def _diag_blocks(full, n):
    c = full.shape[0] // n
    return jnp.stack([full[i * c:(i + 1) * c, i * c:(i + 1) * c] for i in range(n)])


FFN1 = ["ffn1_w_gate", "ffn1_w_up", "ffn1_w_down"]
FFN2 = ["ffn2_w_gate", "ffn2_w_up", "ffn2_w_down"]


def _pair_sums(gb, names, where):
    theirs = _pair_exchange([gb[n] for n in names], "pair_exchange_" + names[0])
    pair, own = _pair_sum([gb[n] for n in names], theirs, where, "pair_sum_" + names[0])
    return _chip_rider(pair, own)


def _local_step(x, tgt, stacks, conv_stack, small, where):
    gate_up, down = FFN1[:2], FFN1[2:]
    big = dict(zip(gate_up, _gather_weights([stacks[n] for n in gate_up], [])))
    wa = _block_diag(small["rg_w_a"]).astype(BF16)
    wx = _block_diag(small["rg_w_x"]).astype(BF16)

    whole = lambda names: [big[n].reshape(-1, D_MODEL) for n in names]
    soon = down + ["w_in"]
    g1, u1, hb1, ab1, *landed = _ffn_up(x, small["ffn1_norm"], *whole(gate_up),
                                        rider=_gather_rider([stacks[n] for n in soon], [conv_stack]))
    big.update(zip(soon, landed))
    x1 = _ffn_down(x, ab1, *whole(down))
    conv_w = jnp.transpose(landed[-1], (1, 0, 2)).reshape(CONV_W, D_RNN)
    rg = (conv_w, small["conv_b"], wa, small["rg_b_a"], wx, small["rg_b_x"], small["rg_lambda"])
    riding = lambda names: _gather_rider([stacks[n] for n in names], [])
    proj, hb2, big["ffn2_w_gate"] = _mix_pre(x1, small["mix_norm"], big["w_in"], riding(["ffn2_w_gate"]))
    yr, hseq, big["ffn2_w_up"] = _rglru_fwd(proj, *rg, riding(["ffn2_w_up"]))
    ya, big["ffn2_w_down"], big["w_out"] = _attn_fwd(proj, small["q_norm"], small["k_norm"],
                                                     riding(["ffn2_w_down", "w_out"]))
    wout = big["w_out"].reshape(D_MODEL, D_MODEL)
    x2 = _mix_post(x1, yr, ya, small["rnn_out_norm"], small["attn_out_norm"], wout)
    dx3, g2, u2, hb3, ab3, loss = _ffn_fwd_loss(x2, small["ffn2_norm"], *whole(FFN2), tgt)

    gb, gs, slots = {}, {}, {}
    dx2, dg2, du2, dyb2, gs["ffn2_norm"] = _ffn_bwd_act(x2, small["ffn2_norm"], dx3, g2, u2, *whole(FFN2), "ffn2_bwd")
    gb["ffn2_w_gate"] = _ffn_wgrad(dg2, hb3, 1.0, "wgrad_gate_ffn2")
    gb["ffn2_w_up"] = _ffn_wgrad(du2, hb3, 1.0, "wgrad_up_ffn2")
    gb["ffn2_w_down"] = _ffn_wgrad(ab3, dyb2, 0.5, "wgrad_down_ffn2")
    dyr, dya, ycat, dxb2, gs["rnn_out_norm"], gs["attn_out_norm"] = _mix_post_bwd(
        dx2, yr, ya, small["rnn_out_norm"], small["attn_out_norm"], wout)
    gb["w_out"] = _wgrad_whole(ycat, dxb2, False, "wgrad_out")
    early = FFN2 + ["w_out"]
    dq, dk, dv, gs["q_norm"], gs["k_norm"], *done = _attn_bwd(
        proj, dya, small["q_norm"], small["k_norm"], _pair_sums(gb, early, where))
    slots.update(zip(early, done))
    dxr, dgate, gs["conv_w"], gs["conv_b"], dwa, gs["rg_b_a"], dwx, gs["rg_b_x"], gs["rg_lambda"] = _rglru_bwd(
        proj, hseq, dyr, *rg)
    gs["rg_w_a"] = _diag_blocks(dwa, RNN_BLOCKS)
    gs["rg_w_x"] = _diag_blocks(dwx, RNN_BLOCKS)
    dpb = jnp.concatenate([dxr, dgate, dq, dk, dv], axis=1)
    dx1, gs["mix_norm"] = _mix_pre_bwd(x1, small["mix_norm"], dx2, dpb, big["w_in"])
    dx0, dg1, du1, dyb1, gs["ffn1_norm"] = _ffn_bwd_act(x, small["ffn1_norm"], dx1, g1, u1, *whole(FFN1), "ffn1_bwd")

    mine = _place_shard(_pack([gs[n] for n in SMALL] + [loss[:, :1]]), where, F32, "place_small_grads",
                        by_device=True)
    gb["ffn1_w_gate"], everyone = _ffn_wgrad(dg1, hb1, 1.0, "wgrad_gate_ffn1", _small_rider(mine))
    gb["ffn1_w_up"], slots["ffn1_w_gate"] = _ffn_wgrad(
        du1, hb1, 1.0, "wgrad_up_ffn1", _pair_sums(gb, ["ffn1_w_gate"], where))
    gb["ffn1_w_down"], slots["ffn1_w_up"] = _ffn_wgrad(
        ab1, dyb1, 0.5, "wgrad_down_ffn1", _pair_sums(gb, ["ffn1_w_up"], where))
    gb["w_in"], slots["ffn1_w_down"] = _wgrad_whole(
        hb2, dpb, True, "wgrad_in", _pair_sums(gb, ["ffn1_w_down"], where))
    last = _pair_sums(gb, ["w_in"], where)
    slots["w_in"], = _chip_exchange(last.plain, last.inplace)
    return dx0, slots, gs, everyone


ANY = pl.BlockSpec(memory_space=pl.ANY)


def _place():
    x, y, c = lax.axis_index("x"), lax.axis_index("y"), lax.axis_index("c")
    other_chips = [(1 - x, y), (x, 1 - y), (1 - x, 1 - y)]
    return x, y, c, 2 * x + y, other_chips


def _remote(src, dst, send_sem, recv_sem, to):
    return pltpu.make_async_remote_copy(src_ref=src, dst_ref=dst, send_sem=send_sem, recv_sem=recv_sem,
                                        device_id=to, device_id_type=MESH)


def _copy_plan(pairs):
    sends = [functools.partial(_remote, *a) for a, _ in pairs]
    arrivals = [functools.partial(_remote, *b) for _, b in pairs]
    return sends, arrivals


class _Rider:
    def __init__(self, plan, plain, inplace, n_copies=None, relay=None, n_relay=0):
        self.plan, self.plain, self.inplace = plan, list(plain), list(inplace)
        self.n_copies = n_copies or 3 * len(self.inplace)
        self.relay, self.n_relay = relay, n_relay

    def operands(self):
        return self.plain + self.inplace

    def out_shape(self):
        return [jax.ShapeDtypeStruct(a.shape, a.dtype) for a in self.inplace]

    def aliases(self, inputs_before, outputs_before):
        return {inputs_before + len(self.plain) + k: outputs_before + k for k in range(len(self.inplace))}

    def scratch(self):
        relay = [pltpu.SemaphoreType.DMA((self.n_relay,))] * 2 if self.relay else []
        return [pltpu.SemaphoreType.DMA((self.n_copies,))] * 2 + relay


def _split_refs(refs, n_in, n_out, rider):
    if rider is None:
        return refs[:n_in], refs[n_in:n_in + n_out], refs[n_in + n_out:], None
    r_in, r_out = len(rider.operands()), len(rider.inplace)
    outs_at = n_in + r_in
    n_sems = len(rider.scratch())
    rest = refs[outs_at + n_out + r_out:]
    sems = rest[len(rest) - n_sems:]
    filled = refs[outs_at + n_out:outs_at + n_out + r_out]
    copies = functools.partial(rider.plan, refs[n_in:n_in + len(rider.plain)], filled, *sems[:2])
    relay = functools.partial(rider.relay, filled, *sems[2:]) if rider.relay else None
    return refs[:n_in], refs[outs_at:outs_at + n_out], rest[:len(rest) - n_sems], (copies, relay)


def _ride(copies, first, last, middle=None):
    if copies is None:
        return lambda: None
    copies, relay = copies

    @pl.when(first)
    def _():
        _start(copies()[0])

    def start_relay():
        for make in copies()[1]:
            make().wait_recv()
        _start(relay()[0])

    if relay is not None and middle is not None:
        pl.when(middle)(start_relay)

    def finish():
        @pl.when(last)
        def _():
            if relay is None:
                _finish(*copies())
            else:
                if middle is None:
                    start_relay()
                _finish(copies()[0] + relay()[0], relay()[1])

    return finish


def _gather_rider(split, whole):
    n_split = len(split)
    return _Rider(lambda plain, stacks, ss, rs: _gather_ici(stacks, n_split, ss, rs), [], list(split) + list(whole),
                  relay=lambda stacks, ss, rs: _gather_d2d(stacks[:n_split], ss, rs), n_relay=3 * n_split)


def _chip_rider(sums, slots):
    return _Rider(_chip_copies, sums, slots)


def _start(makers):
    for make in makers:
        make().start()


def _finish(sends, arrivals):
    for make in arrivals:
        make().wait_recv()
    for make in sends:
        make().wait_send()


def _half(rows, c):
    return pl.ds(pl.multiple_of(c * rows, BF16_ROWS), rows)


def _gather_weights(split, whole):
    arrs = list(split) + list(whole)
    n, ns = len(arrs), len(split)

    def body(*refs):
        outs = refs[n:2 * n]
        send_sems, recv_sems, fsend_sems, frecv_sems = refs[2 * n:]
        sends, arrivals = _gather_ici(outs, ns, send_sems, recv_sems)
        passes, passed = _gather_d2d(outs[:ns], fsend_sems, frecv_sems)
        _start(sends)
        for k, make in enumerate(arrivals):
            make().wait_recv()
            if k < 3 * ns:
                passes[k]().start()
        _finish(sends + passes, passed)

    return pl.pallas_call(
        body, name="gather_weights",
        in_specs=[ANY] * n, out_specs=[ANY] * n,
        out_shape=[jax.ShapeDtypeStruct(a.shape, a.dtype) for a in arrs],
        input_output_aliases={i: i for i in range(n)},
        scratch_shapes=[pltpu.SemaphoreType.DMA((3 * n,)), pltpu.SemaphoreType.DMA((3 * n,)),
                        pltpu.SemaphoreType.DMA((3 * ns,)), pltpu.SemaphoreType.DMA((3 * ns,))],
    )(*arrs)


def _gather_ici(stacks, n_split, send_sems, recv_sems):
    x, y, c, me, chips = _place()

    def region(i, chip):
        if i < n_split:
            return stacks[i].at[chip, _half(stacks[i].shape[1] // 2, c)]
        return stacks[i].at[chip]

    pairs = []
    for i in range(len(stacks)):
        for p, (cx, cy) in enumerate(chips):
            k = 3 * i + p
            mine, got = region(i, me), region(i, 2 * cx + cy)
            sems, to = (send_sems.at[k], recv_sems.at[k]), (cx, cy, c)
            pairs.append(((mine, mine, *sems, to), (got, got, *sems, to)))
    return _copy_plan(pairs)


def _gather_d2d(stacks, send_sems, recv_sems):
    x, y, c, _, chips = _place()
    sibling = (x, y, 1 - c)
    pairs = []
    for i, stack in enumerate(stacks):
        rows = stack.shape[1] // 2
        for p, (cx, cy) in enumerate(chips):
            k = 3 * i + p
            got, theirs = stack.at[2 * cx + cy, _half(rows, c)], stack.at[2 * cx + cy, _half(rows, 1 - c)]
            sems = (send_sems.at[k], recv_sems.at[k])
            pairs.append(((got, got, *sems, sibling), (theirs, theirs, *sems, sibling)))
    return _copy_plan(pairs)


def _pair_exchange(grads, name):
    n = len(grads)

    def body(*refs):
        ins, theirs = refs[:n], refs[n:2 * n]
        send_sems, recv_sems = refs[2 * n:]
        x, y, c, _, _ = _place()
        sibling = (x, y, 1 - c)
        sends = [_remote(ins[k].at[:, _half(grads[k].shape[1] // 2, 1 - c)], theirs[k],
                         send_sems.at[k], recv_sems.at[k], sibling) for k in range(n)]
        for cp in sends:
            cp.start()
        for k in range(n):
            _remote(theirs[k], theirs[k], send_sems.at[k], recv_sems.at[k], sibling).wait_recv()
        for cp in sends:
            cp.wait_send()

    return pl.pallas_call(
        body, name=name,
        in_specs=[ANY] * n, out_specs=[ANY] * n,
        out_shape=[jax.ShapeDtypeStruct((g.shape[0], g.shape[1] // 2, g.shape[2]), g.dtype) for g in grads],
        scratch_shapes=[pltpu.SemaphoreType.DMA((n,))] * 2,
    )(*grads)


def _chip_exchange(sums, slots):
    n = len(sums)

    def body(*refs):
        sends, arrivals = _chip_copies(refs[:n], refs[2 * n:3 * n], *refs[3 * n:])
        _start(sends)
        _finish(sends, arrivals)

    return pl.pallas_call(
        body, name="grad_chip_exchange",
        in_specs=[ANY] * (2 * n), out_specs=[ANY] * n,
        out_shape=[jax.ShapeDtypeStruct(a.shape, a.dtype) for a in slots],
        input_output_aliases={n + k: k for k in range(n)},
        scratch_shapes=[pltpu.SemaphoreType.DMA((3 * n,)), pltpu.SemaphoreType.DMA((3 * n,))],
    )(*sums, *slots)


def _chip_copies(sums, slots, send_sems, recv_sems):
    x, y, c, me, chips = _place()
    pairs = []
    for k in range(len(sums)):
        for p, (cx, cy) in enumerate(chips):
            j = 3 * k + p
            got = slots[k].at[2 * cx + cy]
            sems, to = (send_sems.at[j], recv_sems.at[j]), (cx, cy, c)
            pairs.append(((sums[k].at[2 * cx + cy], slots[k].at[me], *sems, to), (got, got, *sems, to)))
    return _copy_plan(pairs)


def _half_swap(halves):
    n = len(halves)

    def body(*refs):
        outs = refs[n:2 * n]
        send_sems, recv_sems = refs[2 * n:]
        x, y, c, _, _ = _place()
        sibling = (x, y, 1 - c)
        sends = [_remote(outs[k].at[c], outs[k].at[c], send_sems.at[k], recv_sems.at[k], sibling) for k in range(n)]
        for cp in sends:
            cp.start()
        for k in range(n):
            got = outs[k].at[1 - c]
            _remote(got, got, send_sems.at[k], recv_sems.at[k], sibling).wait_recv()
        for cp in sends:
            cp.wait_send()

    return pl.pallas_call(
        body, name="grad_half_swap",
        in_specs=[ANY] * n, out_specs=[ANY] * n,
        out_shape=[jax.ShapeDtypeStruct(a.shape, a.dtype) for a in halves],
        input_output_aliases={k: k for k in range(n)},
        scratch_shapes=[pltpu.SemaphoreType.DMA((n,))] * 2,
    )(*halves)


def _small_rider(stack):
    n_dev = 2 * N_CHIPS

    def plan(_, stacks, send_sems, recv_sems):
        x, y, c, _, _ = _place()
        mine = stacks[0].at[4 * x + 2 * y + c]
        pairs = []
        for k in range(1, n_dev):
            px, py, pc = x ^ ((k >> 2) & 1), y ^ ((k >> 1) & 1), c ^ (k & 1)
            got = stacks[0].at[4 * px + 2 * py + pc]
            sems = (send_sems.at[k - 1], recv_sems.at[k - 1])
            pairs.append(((mine, mine, *sems, (px, py, pc)), (got, got, *sems, (px, py, pc))))
        return _copy_plan(pairs)

    return _Rider(plan, [], [stack], n_dev - 1)


def _row_tile(r):
    return r // 4 if r >= 256 and (r // 4) % BF16_ROWS == 0 else r


def _prefetch_call(body, name, grid, in_specs, out_specs, out_shape):
    spec = pltpu.PrefetchScalarGridSpec(num_scalar_prefetch=1, grid=grid, in_specs=in_specs, out_specs=out_specs)
    return pl.pallas_call(body, name=name, grid_spec=spec, out_shape=out_shape,
                          compiler_params=_params(("arbitrary",) * len(grid)))


def _place_shard(w2d, where, dtype, name, by_device=False):
    r, c = w2d.shape
    tr = _row_tile(r)
    slots = 2 * N_CHIPS if by_device else N_CHIPS
    slot = (lambda s: 2 * s[1] + s[0]) if by_device else (lambda s: s[1])

    def body(where_ref, w_ref, out_ref):
        out_ref[...] = w_ref[...].astype(dtype)

    return _prefetch_call(
        body, name, (r // tr,), [pl.BlockSpec((tr, c), lambda i, s: (i, 0))],
        pl.BlockSpec((None, tr, c), lambda i, s: (slot(s), i, 0)),
        jax.ShapeDtypeStruct((slots, r, c), dtype))(where, w2d)


def _place_shards(w2ds, where, name):
    n = len(w2ds)
    steps = N_CHIPS
    assert all(w.shape[0] % (BF16_ROWS * steps) == 0 for w in w2ds)

    def body(where_ref, *refs):
        for k in range(n):
            refs[n + k][...] = refs[k][...].astype(BF16)

    tile = lambda w: (w.shape[0] // steps, w.shape[1])
    return _prefetch_call(
        body, name, (steps,), [pl.BlockSpec(tile(w), lambda i, s: (i, 0)) for w in w2ds],
        [pl.BlockSpec((None,) + tile(w), lambda i, s: (s[1], i, 0)) for w in w2ds],
        [jax.ShapeDtypeStruct((N_CHIPS,) + w.shape, BF16) for w in w2ds])(where, *w2ds)


def _pair_sum(fulls, theirs, where, name):
    n = len(fulls)

    def body(where_ref, *refs):
        for k in range(n):
            a_ref, b_ref, out_ref, own_ref = refs[k], refs[n + k], refs[2 * n + k], refs[3 * n + k]
            total = (a_ref[...].astype(F32) + b_ref[...].astype(F32)).astype(BF16)
            out_ref[...] = total

            @pl.when(pl.program_id(0) == where_ref[1])
            def _():
                own_ref[...] = total

    half = lambda t: pl.BlockSpec((None,) + t.shape[1:], lambda j, s: (j, s[0], 0))
    blk = lambda t: pl.BlockSpec((None,) + t.shape[1:], lambda j, s: (j, 0, 0))
    own = lambda t: pl.BlockSpec((None,) + t.shape[1:], lambda j, s: (s[1], 0, 0))
    shapes = [jax.ShapeDtypeStruct(t.shape, BF16) for t in theirs]
    outs = _prefetch_call(
        body, name, (N_CHIPS,), [half(t) for t in theirs] + [blk(t) for t in theirs],
        [blk(t) for t in theirs] + [own(t) for t in theirs], shapes + shapes)(where, *fulls, *theirs)
    return outs[:n], outs[n:]


def _chip_sum(slots, where, name):
    n = len(slots)
    steps = 2
    assert all(a.shape[1] % (BF16_ROWS * steps) == 0 for a in slots)

    def body(where_ref, *refs):
        for k in range(n):
            a_ref, out_ref = refs[k], refs[n + k]
            total = a_ref[0].astype(F32)
            for j in range(1, a_ref.shape[0]):
                total = total + a_ref[j].astype(F32)
            out_ref[...] = total

    tile = lambda a: (a.shape[1] // steps, a.shape[2])
    return _prefetch_call(
        body, name, (steps,), [pl.BlockSpec((a.shape[0],) + tile(a), lambda i, s: (0, i, 0)) for a in slots],
        [pl.BlockSpec((None,) + tile(a), lambda i, s: (s[0], i, 0)) for a in slots],
        [jax.ShapeDtypeStruct((2,) + a.shape[1:], F32) for a in slots])(where, *slots)


def _slot_sum(a, name):
    nb, r, c = a.shape
    tr = _row_tile(r)

    def body(a_ref, out_ref):
        total = a_ref[0].astype(F32)
        for j in range(1, nb):
            total = total + a_ref[j].astype(F32)
        out_ref[...] = total

    return pl.pallas_call(
        body, name=name, grid=(r // tr,),
        in_specs=[pl.BlockSpec((nb, tr, c), lambda i: (0, i, 0))],
        out_specs=pl.BlockSpec((tr, c), lambda i: (i, 0)),
        out_shape=jax.ShapeDtypeStruct((r, c), F32), compiler_params=_params(("arbitrary",)),
    )(a)


def _adamw(ws, gs, ms, vs, name, steps=1):
    n = len(ws)
    c1 = 1.0 - ADAM_B1 ** ADAM_STEP
    c2 = 1.0 - ADAM_B2 ** ADAM_STEP
    assert all(w.shape[0] % steps == 0 and (steps == 1 or w.shape[0] // steps % 8 == 0) for w in ws)

    def body(*refs):
        for k in range(n):
            w_ref, g_ref, m_ref, v_ref = (refs[j * n + k] for j in range(4))
            g_out, d_ref, m2_ref, v2_ref = (refs[(4 + j) * n + k] for j in range(4))
            gv = g_ref[...]
            g_out[...] = gv
            m2 = ADAM_B1 * m_ref[...] + (1.0 - ADAM_B1) * gv
            v2 = ADAM_B2 * v_ref[...] + (1.0 - ADAM_B2) * (gv * gv)
            m2_ref[...] = m2
            v2_ref[...] = v2
            d_ref[...] = -ADAM_LR * ((m2 / c1) / (jnp.sqrt(v2 / c2) + ADAM_EPS) + ADAM_WD * w_ref[...])

    blks = [pl.BlockSpec((w.shape[0] // steps, w.shape[1]), lambda i: (i, 0)) for w in ws]
    shapes = [jax.ShapeDtypeStruct(w.shape, F32) for w in ws]
    outs = pl.pallas_call(
        body, name=name, grid=(steps,), in_specs=blks * 4, out_specs=blks * 4, out_shape=shapes * 4,
        compiler_params=_params(("arbitrary",)),
    )(*ws, *gs, *ms, *vs)
    return [outs[j * n:(j + 1) * n] for j in range(4)]


WEIGHTS = ["ffn1_norm", "ffn1_w_gate", "ffn1_w_up", "ffn1_w_down", "mix_norm", "w_in", "conv_w", "conv_b",
           "rg_w_a", "rg_b_a", "rg_w_x", "rg_b_x", "rg_lambda", "q_norm", "k_norm", "rnn_out_norm",
           "attn_out_norm", "w_out", "ffn2_norm", "ffn2_w_gate", "ffn2_w_up", "ffn2_w_down"]
BIG = ["ffn1_w_gate", "ffn1_w_up", "ffn1_w_down", "w_in", "w_out", "ffn2_w_gate", "ffn2_w_up", "ffn2_w_down"]
SMALL = [n for n in WEIGHTS if n not in BIG]
PACK_LANES = 128
PACK_ROW_ALIGN = 8


def _hidden_major(name, a):
    return jnp.transpose(a) if name.endswith(("w_gate", "w_up")) else a


def _pack(parts):
    flat = jnp.concatenate([p.reshape(-1) for p in parts])
    unit = PACK_LANES * PACK_ROW_ALIGN
    padded = -(-flat.shape[0] // unit) * unit
    return jnp.pad(flat, (0, padded - flat.shape[0])).reshape(-1, PACK_LANES)


def _unpack(packed, shapes):
    flat = packed.reshape(-1)
    out, at = [], 0
    for shp in shapes:
        size = math.prod(shp)
        out.append(flat[at:at + size].reshape(shp))
        at += size
    return out


def kernel(x, ffn1_norm, ffn1_w_gate, ffn1_w_up, ffn1_w_down, mix_norm, w_in, conv_w, conv_b, rg_w_a, rg_b_a, rg_w_x, rg_b_x, rg_lambda, q_norm, k_norm, rnn_out_norm, attn_out_norm, w_out, ffn2_norm, ffn2_w_gate, ffn2_w_up, ffn2_w_down, loss_target, m_ffn1_norm, m_ffn1_w_gate, m_ffn1_w_up, m_ffn1_w_down, m_mix_norm, m_w_in, m_conv_w, m_conv_b, m_rg_w_a, m_rg_b_a, m_rg_w_x, m_rg_b_x, m_rg_lambda, m_q_norm, m_k_norm, m_rnn_out_norm, m_attn_out_norm, m_w_out, m_ffn2_norm, m_ffn2_w_gate, m_ffn2_w_up, m_ffn2_w_down, v_ffn1_norm, v_ffn1_w_gate, v_ffn1_w_up, v_ffn1_w_down, v_mix_norm, v_w_in, v_conv_w, v_conv_b, v_rg_w_a, v_rg_b_a, v_rg_w_x, v_rg_b_x, v_rg_lambda, v_q_norm, v_k_norm, v_rnn_out_norm, v_attn_out_norm, v_w_out, v_ffn2_norm, v_ffn2_w_gate, v_ffn2_w_up, v_ffn2_w_down):
    given = dict(locals())
    w = {n: given[n] for n in WEIGHTS}
    m = {n: given["m_" + n] for n in WEIGHTS}
    v = {n: given["v_" + n] for n in WEIGHTS}
    chip = 2 * lax.axis_index("x") + lax.axis_index("y")

    where = jnp.stack([lax.axis_index("c"), chip]).astype(jnp.int32)

    stacks = dict(zip(BIG, _place_shards([_hidden_major(n, w[n][0]) for n in BIG], where, "place_weights")))
    conv_stack = _place_shard(w["conv_w"][0], where, F32, "place_conv_w")
    small = {n: (w[n][0] if w[n].ndim > 2 else w[n]) for n in SMALL if n != "conv_w"}

    grad_x, slots, gs, everyone = _local_step(x[0], loss_target[0], stacks, conv_stack, small, where)

    swapped = _half_swap(_chip_sum([slots[n] for n in BIG], where, "chip_sums"))
    g2s = [t.reshape(t.shape[0] * t.shape[1], t.shape[2]) for t in swapped]
    flat = lambda tree: [_hidden_major(n, tree[n][0]) for n in BIG]
    g2s, d2s, m2s, v2s = _adamw(flat(w), g2s, flat(m), flat(v), "adamw_weights", ADAMW_STEPS)
    grads, deltas, new_m, new_v = {}, {}, {}, {}
    for tree, parts in ((grads, g2s), (deltas, d2s), (new_m, m2s), (new_v, v2s)):
        tree.update({n: _hidden_major(n, a).reshape(w[n].shape) for n, a in zip(BIG, parts)})

    full_shapes = [gs[n].shape for n in SMALL]
    *summed, loss = _unpack(_slot_sum(everyone, "small_grad_sum"), full_shapes + [(1, 1)])
    g_parts = dict(zip(SMALL, summed))
    quarter = D_RNN // N_CHIPS
    g_parts["conv_w"] = lax.dynamic_slice_in_dim(g_parts["conv_w"], chip * quarter, quarter, axis=1)
    local_shapes = [w[n].shape for n in SMALL]
    pk = lambda tree: _pack([tree[n] for n in SMALL])
    (g_s,), (d_s,), (m_s,), (v_s,) = _adamw([pk(w)], [pk(g_parts)], [pk(m)], [pk(v)], "adamw_small")
    for tree, packed in ((grads, g_s), (deltas, d_s), (new_m, m_s), (new_v, v_s)):
        tree.update(zip(SMALL, _unpack(packed, local_shapes)))

    return (loss[0, 0], grad_x.reshape(x.shape), *[grads[n] for n in WEIGHTS], *[deltas[n] for n in WEIGHTS],
            *[new_m[n] for n in WEIGHTS], *[new_v[n] for n in WEIGHTS])
```

```python
import functools
import math

import jax
import jax.numpy as jnp
from jax import lax
from jax.experimental import pallas as pl
from jax.experimental.pallas import tpu as pltpu

F32 = jnp.float32
BF16 = jnp.bfloat16
MESH = pl.DeviceIdType.MESH

D_MODEL = 1024
N_CHIPS = 4
D_RNN = 512
D_ATT = 512
N_HEADS = 8
HEAD_DIM = 64
RNN_BLOCKS = 8
CONV_W = 4
RG_C = 8.0
N_IN = 2 * D_RNN + 3 * D_ATT
EPS = 1e-6
ATT_BLOCK = 128
ATT_WINDOW = 384
ATT_SPLIT = 256
EXP_ZERO = -105.0

ADAM_LR = 0.001
ADAM_B1 = 0.9
ADAM_B2 = 0.999
ADAM_EPS = 1e-08
ADAM_WD = 0.01
ADAM_STEP = 10

V7X_VMEM_LIMIT = 60 * 1024 * 1024
V7X_MXU_WIDTH = 256
TOKEN_TILE = 512
SUBLANES = 8
BF16_ROWS = 16
FFN_TILE = 256
WGRAD_TILE = 2048
WHOLE_TILE = 1024
ADAMW_STEPS = 8

GELU_K0 = math.sqrt(2.0 / math.pi)
GELU_K1 = 0.044715


def _params(sem=None):
    return pltpu.CompilerParams(dimension_semantics=sem, vmem_limit_bytes=V7X_VMEM_LIMIT)


def _dot(a, b):
    return jnp.dot(a, b, preferred_element_type=F32)


def _dot_nt(a, b):
    return lax.dot_general(a, b, (((1,), (1,)), ((), ())), preferred_element_type=F32)


def _dot_tn(a, b):
    return lax.dot_general(a, b, (((0,), (0,)), ((), ())), preferred_element_type=F32)


def _sigmoid(x):
    return 1.0 / (1.0 + jnp.exp(-x))


def _rms_r(xv):
    return lax.rsqrt(jnp.mean(xv * xv, axis=-1, keepdims=True) + EPS)


def _rms_bwd(xv, r, nw, dh):
    t = dh * nw
    dx = r * t - xv * (r * r * r * jnp.mean(t * xv, axis=-1, keepdims=True))
    dn = jnp.sum(dh * xv * r, axis=0, keepdims=True)
    return dx, dn


def _gelu(x):
    t = jnp.tanh(GELU_K0 * (x + GELU_K1 * x * x * x))
    return 0.5 * x * (1.0 + t)


def _gelu_grad(x):
    t = jnp.tanh(GELU_K0 * (x + GELU_K1 * x * x * x))
    return 0.5 * (1.0 + t) + 0.5 * x * (1.0 - t * t) * (GELU_K0 * (1.0 + 3.0 * GELU_K1 * x * x))


def _expm1_neg(x):
    p = 1.0 + x * (1.0 / 6.0)
    for k in (5.0, 4.0, 3.0, 2.0):
        p = 1.0 + x * (1.0 / k) * p
    return jnp.where(x > -0.25, x * p, jnp.exp(x) - 1.0)


def _log_sigmoid(x):
    return jnp.minimum(x, 0.0) - jnp.log(1.0 + jnp.exp(-jnp.abs(x)))


def _tile(s):
    return min(TOKEN_TILE, s)


def _ffn_chunks(f):
    cut = f // 2 // V7X_MXU_WIDTH * V7X_MXU_WIDTH
    return ((0, cut), (cut, f)) if 0 < cut < f else ((0, f),)


def _ffn_fwd_loss(x, nw, wg, wu, wd, tgt):
    s, d = x.shape
    f = wg.shape[0]
    tm = min(FFN_TILE, s)
    ni = s // tm
    assert s % tm == 0

    def body(x_ref, nw_ref, wg_ref, wu_ref, wd_ref, tgt_ref, out_ref, g_ref, u_ref, hb_ref, ab_ref, loss_ref):
        i = pl.program_id(0)
        xv = x_ref[...]
        hb = (xv * _rms_r(xv) * nw_ref[...]).astype(BF16)
        hb_ref[...] = hb
        y = jnp.zeros((tm, d), F32)
        for lo, hi in _ffn_chunks(f):
            g = _dot_nt(hb, wg_ref[lo:hi, :])
            u = _dot_nt(hb, wu_ref[lo:hi, :])
            g_ref[:, lo:hi] = g.astype(BF16)
            u_ref[:, lo:hi] = u.astype(BF16)
            ab = (g * _sigmoid(g) * u).astype(BF16)
            ab_ref[:, lo:hi] = ab
            y = y + _dot(ab, wd_ref[lo:hi, :])
        diff = xv + 0.5 * y - tgt_ref[...]
        out_ref[...] = diff * (1.0 / d)

        @pl.when(i == 0)
        def _():
            loss_ref[...] = jnp.zeros_like(loss_ref)

        loss_ref[...] += jnp.sum(diff * diff) * (0.5 / d)

    row = pl.BlockSpec((tm, d), lambda i: (i, 0))
    weight = pl.BlockSpec((f, d), lambda i: (0, 0), pipeline_mode=pl.Buffered(1))
    blk = pl.BlockSpec((tm, f), lambda i: (i, 0))
    wide = jax.ShapeDtypeStruct((s, f), BF16)
    return _call(body, "ffn_fwd_loss", (ni,),
                 [row, pl.BlockSpec((1, d), lambda i: (0, 0)), weight, weight, weight, row],
                 [row, blk, blk, row, blk, pl.BlockSpec((1, 128), lambda i: (0, 0))],
                 [jax.ShapeDtypeStruct((s, d), F32), wide, wide, jax.ShapeDtypeStruct((s, d), BF16), wide,
                  jax.ShapeDtypeStruct((1, 128), F32)], [x, nw, wg, wu, wd, tgt])


def _ffn_up(x, nw, wg, wu, rider=None):
    s, d = x.shape
    f = wg.shape[0]
    tm = min(FFN_TILE, s)
    ni = s // tm
    assert s % tm == 0

    def body(*refs):
        (x_ref, nw_ref, wg_ref, wu_ref), (g_ref, u_ref, hb_ref, ab_ref), _, copies = _split_refs(refs, 4, 4, rider)
        i = pl.program_id(0)
        finish = _ride(copies, i == 0, i == ni - 1)
        xv = x_ref[...]
        hb = (xv * _rms_r(xv) * nw_ref[...]).astype(BF16)
        hb_ref[...] = hb
        for lo, hi in _ffn_chunks(f):
            g = _dot_nt(hb, wg_ref[lo:hi, :])
            u = _dot_nt(hb, wu_ref[lo:hi, :])
            g_ref[:, lo:hi] = g.astype(BF16)
            u_ref[:, lo:hi] = u.astype(BF16)
            ab_ref[:, lo:hi] = (g * _sigmoid(g) * u).astype(BF16)
        finish()

    row = pl.BlockSpec((tm, d), lambda i: (i, 0))
    weight = pl.BlockSpec((f, d), lambda i: (0, 0), pipeline_mode=pl.Buffered(1))
    blk = pl.BlockSpec((tm, f), lambda i: (i, 0))
    wide = jax.ShapeDtypeStruct((s, f), BF16)
    return _call(body, "ffn_up", (ni,), [row, pl.BlockSpec((1, d), lambda i: (0, 0)), weight, weight],
                 [blk, blk, row, blk], [wide, wide, jax.ShapeDtypeStruct((s, d), BF16), wide], [x, nw, wg, wu],
                 rider=rider)


def _ffn_down(x, ab, wd):
    s, d = x.shape
    f = wd.shape[0]
    tm = _tile(s)
    assert s % tm == 0

    def body(x_ref, ab_ref, wd_ref, out_ref):
        out_ref[...] = x_ref[...] + 0.5 * _dot(ab_ref[...], wd_ref[...])

    row = pl.BlockSpec((tm, d), lambda i: (i, 0))
    return _call(body, "ffn_down", (s // tm,),
                 [row, pl.BlockSpec((tm, f), lambda i: (i, 0)),
                  pl.BlockSpec((f, d), lambda i: (0, 0), pipeline_mode=pl.Buffered(1))],
                 [row], [jax.ShapeDtypeStruct((s, d), F32)], [x, ab, wd])[0]


def _call(body, name, grid, in_specs, out_specs, out_shape, args, scratch=(), rider=None):
    in_specs, out_specs, out_shape, scratch = list(in_specs), list(out_specs), list(out_shape), list(scratch)
    extra, aliases = [], {}
    if rider is not None:
        extra = rider.operands()
        aliases = rider.aliases(len(args), len(out_shape))
        in_specs += [ANY] * len(extra)
        out_specs += [ANY] * len(rider.inplace)
        out_shape += rider.out_shape()
        scratch += rider.scratch()
    return pl.pallas_call(
        body, name=name, grid=grid, in_specs=in_specs, out_specs=out_specs, out_shape=out_shape,
        input_output_aliases=aliases, scratch_shapes=scratch,
        compiler_params=_params(("arbitrary",) * len(grid)),
    )(*args, *extra)


def _ffn_bwd_act(x, nw, dy, g, u, wg, wu, wd, name):
    s, d = x.shape
    f = wg.shape[0]
    tm = min(FFN_TILE, s)
    assert s % tm == 0

    def body(x_ref, nw_ref, dy_ref, g_ref, u_ref, wg_ref, wu_ref, wd_ref,
             dx_ref, dg_ref, du_ref, dyb_ref, dnw_ref):
        dyv = dy_ref[...]
        dyb = dyv.astype(BF16)
        dyb_ref[...] = dyb
        dh = jnp.zeros((tm, d), F32)
        for lo, hi in _ffn_chunks(f):
            da = 0.5 * _dot_nt(dyb, wd_ref[lo:hi, :])
            gv = g_ref[:, lo:hi].astype(F32)
            sg = _sigmoid(gv)
            dub = (da * (gv * sg)).astype(BF16)
            dgb = (da * u_ref[:, lo:hi].astype(F32) * (sg * (1.0 + gv * (1.0 - sg)))).astype(BF16)
            dg_ref[:, lo:hi] = dgb
            du_ref[:, lo:hi] = dub
            dh = dh + _dot(dgb, wg_ref[lo:hi, :]) + _dot(dub, wu_ref[lo:hi, :])
        xv = x_ref[...]
        dx, dn = _rms_bwd(xv, _rms_r(xv), nw_ref[...], dh)
        dx_ref[...] = dyv + dx

        @pl.when(pl.program_id(0) == 0)
        def _():
            dnw_ref[...] = jnp.zeros_like(dnw_ref)

        dnw_ref[...] += dn

    row = pl.BlockSpec((tm, d), lambda i: (i, 0))
    vec = pl.BlockSpec((1, d), lambda i: (0, 0))
    blk = pl.BlockSpec((tm, f), lambda i: (i, 0))
    weight = pl.BlockSpec((f, d), lambda i: (0, 0), pipeline_mode=pl.Buffered(1))
    return _call(
        body, name, (s // tm,), [row, vec, row, blk, blk, weight, weight, weight], [row, blk, blk, row, vec],
        [jax.ShapeDtypeStruct((s, d), F32), jax.ShapeDtypeStruct((s, f), BF16),
         jax.ShapeDtypeStruct((s, f), BF16), jax.ShapeDtypeStruct((s, d), BF16),
         jax.ShapeDtypeStruct((1, d), F32)],
        [x, nw, dy, g, u, wg, wu, wd])


def _wgrad(a, b, a_spec, b_spec, out_rows, out_cols, scale, name, tk, rider=None, per_step=1):
    s = a.shape[-2]
    nk = s // tk
    steps = N_CHIPS // per_step
    assert s % tk == 0

    def body(*refs):
        (a_ref, b_ref), (out_ref,), (acc,), copies = _split_refs(refs, 2, 1, rider)
        j, k = pl.program_id(0), pl.program_id(1)
        finish = _ride(copies, jnp.logical_and(j == 0, k == 0), jnp.logical_and(j == steps - 1, k == nk - 1))

        @pl.when(k == 0)
        def _():
            acc[...] = jnp.zeros_like(acc)

        acc[...] += _dot_tn(a_ref[...], b_ref[...])

        @pl.when(k == nk - 1)
        def _():
            for t in range(per_step):
                out_ref[t] = (acc[t * out_rows:(t + 1) * out_rows, :] * scale).astype(BF16)

        finish()

    outs = _call(
        body, name, (steps, nk), [a_spec(tk), b_spec(tk)],
        [pl.BlockSpec((per_step, out_rows, out_cols), lambda j, k: (j, 0, 0))],
        [jax.ShapeDtypeStruct((N_CHIPS, out_rows, out_cols), BF16)], [a, b],
        scratch=[pltpu.VMEM((per_step * out_rows, out_cols), F32)], rider=rider)
    return outs[0] if rider is None else outs


def _wgrad_whole(a, b, col_blocks, name, rider=None):
    s, m = a.shape
    n = b.shape[1]
    tk = min(WHOLE_TILE, s)
    nk = s // tk
    assert s % tk == 0
    out_shape = (N_CHIPS, m, n // N_CHIPS) if col_blocks else (N_CHIPS, m // N_CHIPS, n)

    def body(*refs):
        (a_ref, b_ref), (out_ref,), (acc,), copies = _split_refs(refs, 2, 1, rider)
        k = pl.program_id(0)
        finish = _ride(copies, k == 0, k == nk - 1)

        @pl.when(k == 0)
        def _():
            acc[...] = jnp.zeros_like(acc)

        acc[...] += _dot_tn(a_ref[...], b_ref[...])

        @pl.when(k == nk - 1)
        def _():
            for j in range(N_CHIPS):
                if col_blocks:
                    out_ref[j] = acc[:, j * out_shape[2]:(j + 1) * out_shape[2]].astype(BF16)
                else:
                    out_ref[j] = acc[j * out_shape[1]:(j + 1) * out_shape[1], :].astype(BF16)

        finish()

    outs = _call(
        body, name, (nk,), [pl.BlockSpec((tk, m), lambda k: (k, 0)), pl.BlockSpec((tk, n), lambda k: (k, 0))],
        [pl.BlockSpec(out_shape, lambda k: (0, 0, 0))], [jax.ShapeDtypeStruct(out_shape, BF16)], [a, b],
        scratch=[pltpu.VMEM((m, n), F32)], rider=rider)
    return outs[0] if rider is None else outs


def _ffn_wgrad(hidden, shared, scale, name, rider=None):
    s, d = shared.shape
    half = hidden.shape[1] // 2
    return _wgrad(hidden, shared, lambda tk: pl.BlockSpec((tk, half), lambda j, k: (k, j)),
                  lambda tk: pl.BlockSpec((tk, d), lambda j, k: (k, 0)), half // 2, d, scale, name,
                  min(WGRAD_TILE, s), rider, per_step=2)


def _mix_pre(x, nw, win, rider=None):
    s, d = x.shape
    nb, _, cb = win.shape
    tm = _tile(s)
    ni = s // tm
    assert s % tm == 0

    def body(*refs):
        (x_ref, nw_ref, w_ref), (p_ref, hb_ref), _, copies = _split_refs(refs, 3, 2, rider)
        finish = _ride(copies, pl.program_id(0) == 0, pl.program_id(0) == ni - 1)
        xv = x_ref[...]
        hb = (xv * _rms_r(xv) * nw_ref[...]).astype(BF16)
        hb_ref[...] = hb
        for j in range(nb):
            p_ref[:, j * cb:(j + 1) * cb] = _dot(hb, w_ref[j])
        finish()

    row = pl.BlockSpec((tm, d), lambda i: (i, 0))
    return _call(
        body, "mix_pre", (ni,),
        [row, pl.BlockSpec((1, d), lambda i: (0, 0)),
         pl.BlockSpec((nb, d, cb), lambda i: (0, 0, 0), pipeline_mode=pl.Buffered(1))],
        [pl.BlockSpec((tm, nb * cb), lambda i: (i, 0)), row],
        [jax.ShapeDtypeStruct((s, nb * cb), F32), jax.ShapeDtypeStruct((s, d), BF16)], [x, nw, win], rider=rider)


def _mix_pre_bwd(x, nw, dres, dpb, win):
    s, d = x.shape
    nb, _, cb = win.shape
    tm = _tile(s)
    assert s % tm == 0

    def body(x_ref, nw_ref, dres_ref, dp_ref, w_ref, dx_ref, dnw_ref):
        dh = jnp.zeros((tm, d), F32)
        for j in range(nb):
            dh = dh + _dot_nt(dp_ref[:, j * cb:(j + 1) * cb], w_ref[j])
        xv = x_ref[...]
        dx, dn = _rms_bwd(xv, _rms_r(xv), nw_ref[...], dh)
        dx_ref[...] = dres_ref[...] + dx

        @pl.when(pl.program_id(0) == 0)
        def _():
            dnw_ref[...] = jnp.zeros_like(dnw_ref)

        dnw_ref[...] += dn

    row = pl.BlockSpec((tm, d), lambda i: (i, 0))
    vec = pl.BlockSpec((1, d), lambda i: (0, 0))
    return pl.pallas_call(
        body, name="mix_pre_bwd", grid=(s // tm,),
        in_specs=[row, vec, row, pl.BlockSpec((tm, nb * cb), lambda i: (i, 0)),
                  pl.BlockSpec((nb, d, cb), lambda i: (0, 0, 0), pipeline_mode=pl.Buffered(1))],
        out_specs=[row, vec],
        out_shape=[jax.ShapeDtypeStruct((s, d), F32), jax.ShapeDtypeStruct((1, d), F32)],
        compiler_params=_params(("arbitrary",)),
    )(x, nw, dres, dpb, win)


def _mix_post(x, yr, ya, nr, na, wout):
    s, d = x.shape
    h = yr.shape[1]
    tm = _tile(s)

    def body(x_ref, yr_ref, ya_ref, nr_ref, na_ref, w_ref, out_ref):
        yrv = yr_ref[...]
        yav = ya_ref[...]
        onb = (yrv * _rms_r(yrv) * nr_ref[...]).astype(BF16)
        oab = (yav * _rms_r(yav) * na_ref[...]).astype(BF16)
        out_ref[...] = x_ref[...] + _dot(onb, w_ref[0:h, :]) + _dot(oab, w_ref[h:2 * h, :])

    row = pl.BlockSpec((tm, d), lambda i: (i, 0))
    half = pl.BlockSpec((tm, h), lambda i: (i, 0))
    vec = pl.BlockSpec((1, h), lambda i: (0, 0))
    return pl.pallas_call(
        body, name="mix_post", grid=(s // tm,),
        in_specs=[row, half, half, vec, vec, pl.BlockSpec((2 * h, d), lambda i: (0, 0))],
        out_specs=row, out_shape=jax.ShapeDtypeStruct((s, d), F32),
        compiler_params=_params(("arbitrary",)),
    )(x, yr, ya, nr, na, wout)


def _mix_post_bwd(dx, yr, ya, nr, na, wout):
    s, d = dx.shape
    h = yr.shape[1]
    tm = _tile(s)

    def body(dx_ref, yr_ref, ya_ref, nr_ref, na_ref, w_ref,
             dyr_ref, dya_ref, yc_ref, dxb_ref, dnr_ref, dna_ref):
        i = pl.program_id(0)
        dxb = dx_ref[...].astype(BF16)
        dxb_ref[...] = dxb
        dyc = _dot_nt(dxb, w_ref[...])
        yrv = yr_ref[...]
        yav = ya_ref[...]
        rr = _rms_r(yrv)
        ra = _rms_r(yav)
        yc_ref[:, 0:h] = (yrv * rr * nr_ref[...]).astype(BF16)
        yc_ref[:, h:2 * h] = (yav * ra * na_ref[...]).astype(BF16)
        dyr, dnr = _rms_bwd(yrv, rr, nr_ref[...], dyc[:, 0:h])
        dya, dna = _rms_bwd(yav, ra, na_ref[...], dyc[:, h:2 * h])
        dyr_ref[...] = dyr
        dya_ref[...] = dya

        @pl.when(i == 0)
        def _():
            dnr_ref[...] = jnp.zeros_like(dnr_ref)
            dna_ref[...] = jnp.zeros_like(dna_ref)

        dnr_ref[...] += dnr
        dna_ref[...] += dna

    row = pl.BlockSpec((tm, d), lambda i: (i, 0))
    half = pl.BlockSpec((tm, h), lambda i: (i, 0))
    vec = pl.BlockSpec((1, h), lambda i: (0, 0))
    return pl.pallas_call(
        body, name="mix_post_bwd", grid=(s // tm,),
        in_specs=[row, half, half, vec, vec, pl.BlockSpec((2 * h, d), lambda i: (0, 0))],
        out_specs=[half, half, pl.BlockSpec((tm, 2 * h), lambda i: (i, 0)), row, vec, vec],
        out_shape=[jax.ShapeDtypeStruct((s, h), F32), jax.ShapeDtypeStruct((s, h), F32),
                   jax.ShapeDtypeStruct((s, 2 * h), BF16), jax.ShapeDtypeStruct((s, d), BF16),
                   jax.ShapeDtypeStruct((1, h), F32), jax.ShapeDtypeStruct((1, h), F32)],
        compiler_params=_params(("arbitrary",)),
    )(dx, yr, ya, nr, na, wout)


def _shift_down(xv, s, prev8):
    rolled = pltpu.roll(xv, s, 0)
    row8 = lax.broadcasted_iota(jnp.int32, prev8.shape, 0)
    head = jnp.where(row8 < s, pltpu.roll(prev8, s, 0), rolled[0:8, :])
    return jnp.concatenate([head, rolled[8:, :]], axis=0)


def _shift_up(xv, s, next8):
    n = xv.shape[0]
    rolled = pltpu.roll(xv, n - s, 0)
    row8 = lax.broadcasted_iota(jnp.int32, next8.shape, 0)
    tail = jnp.where(row8 >= 8 - s, pltpu.roll(next8, 8 - s, 0), rolled[n - 8:, :])
    return jnp.concatenate([rolled[:n - 8, :], tail], axis=0)


def _scan_fwd(a, b):
    n = a.shape[0]
    sub = lax.broadcasted_iota(jnp.int32, a.shape, 0) % SUBLANES
    s = 1
    while s < SUBLANES:
        ok = sub >= s
        b = jnp.where(ok, a * pltpu.roll(b, s, 0) + b, b)
        a = jnp.where(ok, a * pltpu.roll(a, s, 0), a)
        s *= 2
    groups = []
    before = jnp.zeros((1, a.shape[1]), F32)
    for g in range(n // SUBLANES):
        rows = slice(g * SUBLANES, (g + 1) * SUBLANES)
        groups.append(a[rows] * before + b[rows])
        before = groups[-1][SUBLANES - 1:]
    return jnp.concatenate(groups, axis=0)


def _scan_bwd(a, b):
    n = a.shape[0]
    sub = lax.broadcasted_iota(jnp.int32, a.shape, 0) % SUBLANES
    s = 1
    while s < SUBLANES:
        ok = sub < SUBLANES - s
        b = jnp.where(ok, a * pltpu.roll(b, n - s, 0) + b, b)
        a = jnp.where(ok, a * pltpu.roll(a, n - s, 0), a)
        s *= 2
    groups = []
    after = jnp.zeros((1, a.shape[1]), F32)
    for g in reversed(range(n // SUBLANES)):
        rows = slice(g * SUBLANES, (g + 1) * SUBLANES)
        groups.append(a[rows] * after + b[rows])
        after = groups[-1][:1]
    return jnp.concatenate(groups[::-1], axis=0)


def _rglru_gates(xv, prev8, cw_ref, cb_ref, wa_ref, ba_ref, wx_ref, bx_ref, lam_ref):
    x1 = _shift_down(xv, 1, prev8)
    x2 = _shift_down(xv, 2, prev8)
    x3 = _shift_down(xv, 3, prev8)
    xc = cw_ref[3:4, :] * xv + cw_ref[2:3, :] * x1 + cw_ref[1:2, :] * x2 + cw_ref[0:1, :] * x3 + cb_ref[...]
    xcb = xc.astype(BF16)
    r = _sigmoid(_dot(xcb, wa_ref[...]) + ba_ref[...])
    ig = _sigmoid(_dot(xcb, wx_ref[...]) + bx_ref[...])
    c = RG_C * _log_sigmoid(lam_ref[...])
    la = r * c
    a = jnp.exp(la)
    m = jnp.sqrt(-_expm1_neg(2.0 * la))
    return (x1, x2, x3), xc, xcb, r, ig, c, a, m


def _rglru_fwd(proj, cw, cb, wa, ba, wx, bx, lam, rider=None):
    s = proj.shape[0]
    w = D_RNN
    tm = _tile(s)
    ni = s // tm

    def body(*refs):
        ins, (y_ref, h_ref), (prev, hlast), copies = _split_refs(refs, 9, 2, rider)
        xr_ref, gate_ref, cw_ref, cb_ref, wa_ref, ba_ref, wx_ref, bx_ref, lam_ref = ins
        finish = _ride(copies, pl.program_id(0) == 0, pl.program_id(0) == ni - 1)

        @pl.when(pl.program_id(0) == 0)
        def _():
            prev[...] = jnp.zeros_like(prev)
            hlast[...] = jnp.zeros_like(hlast)

        xv = xr_ref[...]
        _, xc, _, _, ig, _, a, m = _rglru_gates(xv, prev[...], cw_ref, cb_ref, wa_ref, ba_ref,
                                                wx_ref, bx_ref, lam_ref)
        b = m * (ig * xc)
        row = lax.broadcasted_iota(jnp.int32, b.shape, 0)
        b = jnp.where(row == 0, b + a * hlast[...], b)
        h = _scan_fwd(a, b)
        h_ref[...] = h
        y_ref[...] = h * _gelu(gate_ref[...])
        prev[...] = xv[tm - 8:, :]
        hlast[...] = h[tm - 1:tm, :]
        finish()

    vec = pl.BlockSpec((1, w), lambda i: (0, 0))
    sq = pl.BlockSpec((w, w), lambda i: (0, 0))
    out = pl.BlockSpec((tm, w), lambda i: (i, 0))
    return _call(
        body, "rglru_fwd", (ni,),
        [pl.BlockSpec((tm, w), lambda i: (i, 0)), pl.BlockSpec((tm, w), lambda i: (i, 1)),
         pl.BlockSpec((CONV_W, w), lambda i: (0, 0)), vec, sq, vec, sq, vec, vec], [out, out],
        [jax.ShapeDtypeStruct((s, w), F32), jax.ShapeDtypeStruct((s, w), F32)],
        [proj, proj, cw, cb, wa, ba, wx, bx, lam],
        scratch=[pltpu.VMEM((8, w), F32), pltpu.VMEM((1, w), F32)], rider=rider)


def _rglru_bwd(proj, hseq, dyr, cw, cb, wa, ba, wx, bx, lam):
    s = proj.shape[0]
    w = D_RNN
    tm = _tile(s)
    nt = s // tm
    t8 = tm // 8

    def body(xr_ref, xp_ref, gate_ref, h_ref, hp_ref, dy_ref, cw_ref, cb_ref, wa_ref, ba_ref,
             wx_ref, bx_ref, lam_ref,
             dxr_ref, dgate_ref, dcw_ref, dcb_ref, dwa_ref, dba_ref, dwx_ref, dbx_ref, dlam_ref,
             carry, dxc_next):
        i = pl.program_id(0)
        first_tile = i == nt - 1

        @pl.when(i == 0)
        def _():
            carry[...] = jnp.zeros_like(carry)
            dxc_next[...] = jnp.zeros_like(dxc_next)
            for ref in (dcw_ref, dcb_ref, dwa_ref, dba_ref, dwx_ref, dbx_ref, dlam_ref):
                ref[...] = jnp.zeros_like(ref)

        xv = xr_ref[...]
        prev8 = jnp.where(first_tile, 0.0, xp_ref[...])
        hprev8 = jnp.where(first_tile, 0.0, hp_ref[...])
        (x1, x2, x3), xc, xcb, r, ig, c, a, m = _rglru_gates(
            xv, prev8, cw_ref, cb_ref, wa_ref, ba_ref, wx_ref, bx_ref, lam_ref)
        gv = gate_ref[...]
        hv = h_ref[...]
        dy = dy_ref[...]
        dgate_ref[...] = (dy * hv * _gelu_grad(gv)).astype(BF16)
        dh = dy * _gelu(gv)
        row = lax.broadcasted_iota(jnp.int32, dh.shape, 0)
        dh = jnp.where(row == tm - 1, dh + carry[...], dh)
        a_up = jnp.where(row == tm - 1, 0.0, pltpu.roll(a, tm - 1, 0))
        lam_t = _scan_bwd(a_up, dh)
        carry[...] = a[0:1, :] * lam_t[0:1, :]
        hm1 = _shift_down(hv, 1, hprev8)
        da = lam_t * hm1
        ixc = ig * xc
        dm = lam_t * ixc
        dig = lam_t * m * xc
        dxc = lam_t * m * ig
        dla = da * a - dm * (a * a) / m
        dr = dla * c
        dlam_ref[...] += jnp.sum(dla * r, axis=0, keepdims=True)
        dpa = dr * r * (1.0 - r)
        dpi = dig * ig * (1.0 - ig)
        dba_ref[...] += jnp.sum(dpa, axis=0, keepdims=True)
        dbx_ref[...] += jnp.sum(dpi, axis=0, keepdims=True)
        dpab = dpa.astype(BF16)
        dpib = dpi.astype(BF16)
        dwa_ref[...] += _dot_tn(xcb, dpab)
        dwx_ref[...] += _dot_tn(xcb, dpib)
        dxc = dxc + _dot_nt(dpab, wa_ref[...]) + _dot_nt(dpib, wx_ref[...])
        dcb_ref[...] += jnp.sum(dxc, axis=0, keepdims=True)
        dcw_ref[3:4, :] += jnp.sum(dxc * xv, axis=0, keepdims=True)
        dcw_ref[2:3, :] += jnp.sum(dxc * x1, axis=0, keepdims=True)
        dcw_ref[1:2, :] += jnp.sum(dxc * x2, axis=0, keepdims=True)
        dcw_ref[0:1, :] += jnp.sum(dxc * x3, axis=0, keepdims=True)
        nxt = dxc_next[...]
        dxr = (cw_ref[3:4, :] * dxc + cw_ref[2:3, :] * _shift_up(dxc, 1, nxt)
               + cw_ref[1:2, :] * _shift_up(dxc, 2, nxt) + cw_ref[0:1, :] * _shift_up(dxc, 3, nxt))
        dxr_ref[...] = dxr.astype(BF16)
        dxc_next[...] = dxc[0:8, :]

        @pl.when(first_tile)
        def _():
            lv = lam_ref[...]
            dlam_ref[...] = dlam_ref[...] * (RG_C * _sigmoid(-lv))

    rev = lambda i: nt - 1 - i
    vec = pl.BlockSpec((1, w), lambda i: (0, 0))
    sq = pl.BlockSpec((w, w), lambda i: (0, 0))
    cur = lambda col: pl.BlockSpec((tm, w), lambda i: (rev(i), col))
    before = lambda cols: pl.BlockSpec((8, w), lambda i: (jnp.maximum(rev(i) * t8 - 1, 0), 0))
    return pl.pallas_call(
        body, name="rglru_bwd", grid=(nt,),
        in_specs=[cur(0), before(None), cur(1), cur(0), before(None), cur(0),
                  pl.BlockSpec((CONV_W, w), lambda i: (0, 0)), vec, sq, vec, sq, vec, vec],
        out_specs=[cur(0), cur(0), pl.BlockSpec((CONV_W, w), lambda i: (0, 0)), vec, sq, vec, sq, vec, vec],
        out_shape=[jax.ShapeDtypeStruct((s, w), BF16), jax.ShapeDtypeStruct((s, w), BF16),
                   jax.ShapeDtypeStruct((CONV_W, w), F32), jax.ShapeDtypeStruct((1, w), F32),
                   jax.ShapeDtypeStruct((w, w), F32), jax.ShapeDtypeStruct((1, w), F32),
                   jax.ShapeDtypeStruct((w, w), F32), jax.ShapeDtypeStruct((1, w), F32),
                   jax.ShapeDtypeStruct((1, w), F32)],
        scratch_shapes=[pltpu.VMEM((1, w), F32), pltpu.VMEM((8, w), F32)],
        compiler_params=_params(("arbitrary",)),
    )(proj, proj, proj, hseq, hseq, dyr, cw, cb, wa, ba, wx, bx, lam)


def _sb_logs(z, valid):
    lb = jnp.minimum(z, 0.0) - jnp.log(1.0 + jnp.exp(-jnp.abs(z)))
    return lb, jnp.where(valid, lb - z, 0.0)


class _Window:
    def __init__(self):
        blk, win, cut = ATT_BLOCK, ATT_WINDOW, ATT_SPLIT
        self.row = lax.broadcasted_iota(jnp.int32, (blk, win), 0)
        self.col = lax.broadcasted_iota(jnp.int32, (blk, win), 1)

        def tri(n, later):
            j = lax.broadcasted_iota(jnp.int32, (n, n), 0)
            s = lax.broadcasted_iota(jnp.int32, (n, n), 1)
            return jnp.where((j > s) if later else (j < s), 1.0, 0.0).astype(BF16)

        self.later = (tri(cut, True), tri(win - cut, True))
        self.earlier = (tri(cut, False), tri(win - cut, False))

    def place(self, qi, g):
        end = (qi + 1) * ATT_BLOCK - g * ATT_WINDOW
        start = pl.multiple_of(jnp.maximum(end - ATT_WINDOW, 0), ATT_BLOCK)
        valid = self.col < jnp.minimum(self.row + (qi * ATT_BLOCK - start), end - start)
        return start, valid

    @staticmethod
    def _parts(xv):
        hi = xv.astype(BF16)
        lo = (xv - hi.astype(F32)).astype(BF16)
        cut = ATT_SPLIT
        sums = (jnp.sum(xv[:, :cut], axis=1, keepdims=True), jnp.sum(xv[:, cut:], axis=1, keepdims=True))
        return (hi[:, :cut], lo[:, :cut]), (hi[:, cut:], lo[:, cut:]), sums

    def sums_after(self, xv, carry):
        (h0, l0), (h1, l1), (s0, s1) = self._parts(xv)
        first = _dot(h0, self.later[0]) + _dot(l0, self.later[0]) + (s1 + carry)
        last = _dot(h1, self.later[1]) + _dot(l1, self.later[1]) + carry
        return jnp.concatenate([first, last], axis=1), s0 + s1

    def sums_before(self, xv, carry):
        (h0, l0), (h1, l1), (s0, s1) = self._parts(xv)
        first = _dot(h0, self.earlier[0]) + _dot(l0, self.earlier[0]) + carry
        last = _dot(h1, self.earlier[1]) + _dot(l1, self.earlier[1]) + (s0 + carry)
        return jnp.concatenate([first, last], axis=1), s0 + s1


class _HeadPair:
    def __init__(self):
        lanes = 2 * HEAD_DIM
        lane = lax.broadcasted_iota(jnp.int32, (1, lanes), 1)
        self.masks = [lane // HEAD_DIM == h for h in (0, 1)]
        i = lax.broadcasted_iota(jnp.int32, (lanes, lanes), 0) // HEAD_DIM
        j = lax.broadcasted_iota(jnp.int32, (lanes, lanes), 1) // HEAD_DIM
        self.same_head = jnp.where(i == j, 1.0, 0.0).astype(BF16)

    def only(self, h, xv):
        return jnp.where(self.masks[h], xv, jnp.zeros_like(xv))

    def merge(self, per_head):
        return jnp.where(self.masks[0], per_head[0], per_head[1])

    def mean(self, xv):
        hi = xv.astype(BF16)
        lo = (xv - hi.astype(F32)).astype(BF16)
        return (_dot(hi, self.same_head) + _dot(lo, self.same_head)) * (1.0 / HEAD_DIM)

    def rms_r(self, xv):
        return lax.rsqrt(self.mean(xv * xv) + EPS)

    def rms_bwd(self, xv, r, nw, dh):
        t = dh * nw
        dx = r * t - xv * (r * r * r * self.mean(t * xv))
        dn = jnp.sum(dh * xv * r, axis=0, keepdims=True)
        return dx, dn[:, :HEAD_DIM] + dn[:, HEAD_DIM:]


def _attn_fwd(proj, qg, kg, rider=None):
    s = proj.shape[0]
    blk, win, dh = ATT_BLOCK, ATT_WINDOW, HEAD_DIM
    nq = s // blk
    scale = 1.0 / math.sqrt(dh)
    heads = (0, 1)
    blocks = (0, 1)
    assert s >= win and s % (blk * len(blocks)) == 0

    def body(*refs):
        (q_ref, k_ref, v_ref, qg_ref, kg_ref), (o_ref,), (qn, kn, vb), copies = _split_refs(refs, 5, 1, rider)
        finish = _ride(copies, pl.program_id(0) == 0, pl.program_id(0) == N_HEADS // 2 - 1)
        wd, hp = _Window(), _HeadPair()
        qv = q_ref[...]
        qn[...] = (qv * hp.rms_r(qv) * qg_ref[...] * scale).astype(BF16)
        kv = k_ref[...]
        kn[...] = (kv * hp.rms_r(kv) * kg_ref[...]).astype(BF16)
        vb[...] = v_ref[...].astype(BF16)

        def q_step(pair_i, _):
            qis = [2 * pair_i + b for b in blocks]
            chains = [(b, h) for b in blocks for h in heads]
            qoffs = [pl.multiple_of(qi * blk, blk) for qi in qis]
            qtiles = [qn[pl.ds(qoff, blk), :] for qoff in qoffs]
            qts = [hp.only(h, qtiles[b]) for b, h in chains]

            def more(carry):
                g, live = carry[:2]
                return jnp.logical_and((qis[-1] + 1) * blk - g * win > 0, live > 0)

            def window(carry):
                g, _, accs, runs = carry
                places = [wd.place(qi, g) for qi in qis]
                kts = [kn[pl.ds(start, win), :] for start, _ in places]
                zs = [_dot_nt(qts[c], kts[b]) for c, (b, h) in enumerate(chains)]
                logs = [_sb_logs(zs[c], places[b][1]) for c, (b, h) in enumerate(chains)]
                sums = [wd.sums_after(logs[c][1], runs[c]) for c in range(len(chains))]
                wgts = [jnp.where(places[b][1], jnp.exp(logs[c][0] + sums[c][0]), 0.0).astype(BF16)
                        for c, (b, h) in enumerate(chains)]
                vts = [vb[pl.ds(start, win), :] for start, _ in places]
                accs = tuple(accs[c] + _dot(wgts[c], vts[b]) for c, (b, h) in enumerate(chains))
                runs = tuple(runs[c] + sums[c][1] for c in range(len(chains)))
                top = functools.reduce(jnp.maximum, [jnp.max(r) for r in runs])
                return g + 1, (top > EXP_ZERO).astype(jnp.int32), accs, runs

            zero = lambda cols: tuple(jnp.zeros((blk, cols), F32) for _ in chains)
            _, _, accs, _ = lax.while_loop(more, window, (jnp.int32(0), jnp.int32(1), zero(2 * dh), zero(1)))
            for b in blocks:
                o_ref[pl.ds(qoffs[b], blk), :] = hp.merge([accs[2 * b + h] for h in heads])
            return 0

        lax.fori_loop(0, nq // len(blocks), q_step, 0)
        finish()

    pair = lambda group: pl.BlockSpec((s, 2 * dh), lambda p: (0, group * (D_ATT // (2 * dh)) + p))
    vec = pl.BlockSpec((1, 2 * dh), lambda p: (0, 0))
    return _call(
        body, "attn_fwd", (N_HEADS // 2,), [pair(2), pair(3), pair(4), vec, vec], [pair(0)],
        [jax.ShapeDtypeStruct((s, D_ATT), F32)], [proj, proj, proj, jnp.tile(qg, (1, 2)), jnp.tile(kg, (1, 2))],
        scratch=[pltpu.VMEM((s, 2 * dh), BF16)] * 3, rider=rider)


def _attn_bwd(proj, dya, qg, kg, rider=None):
    s = proj.shape[0]
    blk, win, dh = ATT_BLOCK, ATT_WINDOW, HEAD_DIM
    nq = s // blk
    max_windows = -(-s // win) + 1
    scale = 1.0 / math.sqrt(dh)
    steps = N_HEADS // 2
    heads = (0, 1)
    blocks = (0, 1)
    assert s >= win and s % (blk * len(blocks)) == 0

    def body(*refs):
        ins, outs, scratch, copies = _split_refs(refs, 6, 5, rider)
        q_ref, k_ref, v_ref, do_ref, qg_ref, kg_ref = ins
        dq_ref, dk_ref, dv_ref, dqg_ref, dkg_ref = outs
        qn, kn, vb, dob, runs_ref, dqn, dkn, dvn = scratch
        finish = _ride(copies, pl.program_id(0) == 0, pl.program_id(0) == steps - 1)
        wd, hp = _Window(), _HeadPair()

        @pl.when(pl.program_id(0) == 0)
        def _():
            dqg_ref[...] = jnp.zeros_like(dqg_ref)
            dkg_ref[...] = jnp.zeros_like(dkg_ref)

        qv = q_ref[...]
        qn[...] = (qv * hp.rms_r(qv) * qg_ref[...] * scale).astype(BF16)
        kv = k_ref[...]
        kn[...] = (kv * hp.rms_r(kv) * kg_ref[...]).astype(BF16)
        vb[...] = v_ref[...].astype(BF16)
        dob[...] = do_ref[...].astype(BF16)
        dkn[...] = jnp.zeros_like(dkn)
        dvn[...] = jnp.zeros_like(dvn)

        def q_step(pair_i, _):
            qis = [2 * pair_i + b for b in blocks]
            chains = [(b, h) for b in blocks for h in heads]
            ids = range(len(chains))
            qoffs = [pl.multiple_of(qi * blk, blk) for qi in qis]
            qts = [hp.only(h, qn[pl.ds(qoffs[b], blk), :]) for b, h in chains]
            dots = [hp.only(h, dob[pl.ds(qoffs[b], blk), :]) for b, h in chains]

            zero = lambda cols: tuple(jnp.zeros((blk, cols), F32) for _ in chains)

            def logs_of(g):
                places = [wd.place(qi, g) for qi in qis]
                kts = [kn[pl.ds(start, win), :] for start, _ in places]
                return [_sb_logs(_dot_nt(qts[c], kts[b]), places[b][1]) for c, (b, h) in enumerate(chains)]

            def row_sums(logs):
                return tuple(jnp.sum(logs[c][1], axis=1, keepdims=True) for c in ids)

            def still_live(runs):
                return functools.reduce(jnp.maximum, [jnp.max(r) for r in runs]) > EXP_ZERO

            def window_grads(g, logs, runs, esums):
                places = [wd.place(qi, g) for qi in qis]
                kts = [kn[pl.ds(start, win), :] for start, _ in places]
                vts = [vb[pl.ds(start, win), :] for start, _ in places]
                dws = [_dot_nt(dots[c], vts[b]) for c, (b, h) in enumerate(chains)]
                tails = [wd.sums_after(logs[c][1], runs[c])[0] for c in ids]
                wgts = [jnp.where(places[b][1], jnp.exp(logs[c][0] + tails[c]), 0.0) for c, (b, h) in enumerate(chains)]
                es = [dws[c] * wgts[c] for c in ids]
                befores = [wd.sums_before(es[c], esums[c]) for c in ids]
                dzbs = []
                for c, (b, h) in enumerate(chains):
                    beta = jnp.exp(logs[c][0])
                    dz = jnp.where(places[b][1], es[c] * (1.0 - beta) - befores[c][0] * beta, 0.0)
                    dzbs.append(dz.astype(BF16))
                for b in blocks:
                    rows = pl.ds(places[b][0], win)
                    dkn[rows, :] += _dot_tn(dzbs[2 * b], qts[2 * b]) + _dot_tn(dzbs[2 * b + 1], qts[2 * b + 1])
                    dvn[rows, :] += (_dot_tn(wgts[2 * b].astype(BF16), dots[2 * b])
                                     + _dot_tn(wgts[2 * b + 1].astype(BF16), dots[2 * b + 1]))
                return (tuple(_dot(dzbs[c], kts[b]) for c, (b, h) in enumerate(chains)),
                        tuple(befores[c][1] for c in ids))

            logs0 = logs_of(0)
            runs1 = row_sums(logs0)

            def one_window():
                return window_grads(0, logs0, zero(1), zero(1))[0]

            def all_windows():
                def more(carry):
                    g, live = carry[:2]
                    return jnp.logical_and((qis[-1] + 1) * blk - g * win > 0, live > 0)

                def run_window(carry):
                    g, _, runs = carry
                    for c in ids:
                        runs_ref[c, g] = runs[c]
                    sums = row_sums(logs_of(g))
                    runs = tuple(runs[c] + sums[c] for c in ids)
                    return g + 1, still_live(runs).astype(jnp.int32), runs

                for c in ids:
                    runs_ref[c, 0] = jnp.zeros((blk, 1), F32)
                windows, _, _ = lax.while_loop(more, run_window, (jnp.int32(1), jnp.int32(1), runs1))

                def k_window(gg, carry):
                    dq_accs, esums = carry
                    g = windows - 1 - gg
                    parts, totals = window_grads(g, logs_of(g), [runs_ref[c, g] for c in ids], esums)
                    return (tuple(dq_accs[c] + parts[c] for c in ids), tuple(esums[c] + totals[c] for c in ids))

                return lax.fori_loop(0, windows, k_window, (zero(2 * dh), zero(1)))[0]

            earlier_keys = (qis[-1] + 1) * blk - win > 0
            dq_accs = lax.cond(jnp.logical_and(earlier_keys, still_live(runs1)), all_windows, one_window)
            for b in blocks:
                dqn[pl.ds(qoffs[b], blk), :] = hp.merge([dq_accs[2 * b + h] for h in heads])
            return 0

        lax.fori_loop(0, nq // len(blocks), q_step, 0)

        dq, dqg = hp.rms_bwd(qv, hp.rms_r(qv), qg_ref[...] * scale, dqn[...])
        dq_ref[...] = dq.astype(BF16)
        dqg_ref[...] += dqg * scale
        dk, dkg = hp.rms_bwd(kv, hp.rms_r(kv), kg_ref[...], dkn[...])
        dk_ref[...] = dk.astype(BF16)
        dkg_ref[...] += dkg
        dv_ref[...] = dvn[...].astype(BF16)
        finish()

    pair = lambda group: pl.BlockSpec((s, 2 * dh), lambda p: (0, group * (D_ATT // (2 * dh)) + p))
    vec2 = pl.BlockSpec((1, 2 * dh), lambda p: (0, 0))
    vec = pl.BlockSpec((1, dh), lambda p: (0, 0))
    return _call(
        body, "attn_bwd", (steps,), [pair(2), pair(3), pair(4), pair(0), vec2, vec2],
        [pair(0), pair(0), pair(0), vec, vec],
        [jax.ShapeDtypeStruct((s, D_ATT), BF16)] * 3 + [jax.ShapeDtypeStruct((1, dh), F32)] * 2,
        [proj, proj, proj, dya, jnp.tile(qg, (1, 2)), jnp.tile(kg, (1, 2))],
        scratch=[pltpu.VMEM((s, 2 * dh), BF16)] * 4 + [pltpu.VMEM((4, max_windows, blk, 1), F32)]
        + [pltpu.VMEM((s, 2 * dh), F32)] * 3, rider=rider)


def _block_diag(w):
    n, c, d = w.shape
    return jnp.einsum("ncd,nm->ncmd", w, jnp.eye(n, dtype=w.dtype)).reshape(n * c, n * d)


def _diag_blocks(full, n):
    c = full.shape[0] // n
    return jnp.stack([full[i * c:(i + 1) * c, i * c:(i + 1) * c] for i in range(n)])


FFN1 = ["ffn1_w_gate", "ffn1_w_up", "ffn1_w_down"]
FFN2 = ["ffn2_w_gate", "ffn2_w_up", "ffn2_w_down"]


def _pair_sums(gb, names, where):
    theirs = _pair_exchange([gb[n] for n in names], "pair_exchange_" + names[0])
    pair, own = _pair_sum([gb[n] for n in names], theirs, where, "pair_sum_" + names[0])
    return _chip_rider(pair, own)


def _local_step(x, tgt, stacks, conv_stack, small, where):
    gate_up, down = FFN1[:2], FFN1[2:]
    big = dict(zip(gate_up, _gather_weights([stacks[n] for n in gate_up], [])))
    wa = _block_diag(small["rg_w_a"]).astype(BF16)
    wx = _block_diag(small["rg_w_x"]).astype(BF16)

    whole = lambda names: [big[n].reshape(-1, D_MODEL) for n in names]
    soon = down + ["w_in"]
    g1, u1, hb1, ab1, *landed = _ffn_up(x, small["ffn1_norm"], *whole(gate_up),
                                        rider=_gather_rider([stacks[n] for n in soon], [conv_stack]))
    big.update(zip(soon, landed))
    x1 = _ffn_down(x, ab1, *whole(down))
    conv_w = jnp.transpose(landed[-1], (1, 0, 2)).reshape(CONV_W, D_RNN)
    rg = (conv_w, small["conv_b"], wa, small["rg_b_a"], wx, small["rg_b_x"], small["rg_lambda"])
    riding = lambda names: _gather_rider([stacks[n] for n in names], [])
    proj, hb2, big["ffn2_w_gate"] = _mix_pre(x1, small["mix_norm"], big["w_in"], riding(["ffn2_w_gate"]))
    yr, hseq, big["ffn2_w_up"] = _rglru_fwd(proj, *rg, riding(["ffn2_w_up"]))
    ya, big["ffn2_w_down"], big["w_out"] = _attn_fwd(proj, small["q_norm"], small["k_norm"],
                                                     riding(["ffn2_w_down", "w_out"]))
    wout = big["w_out"].reshape(D_MODEL, D_MODEL)
    x2 = _mix_post(x1, yr, ya, small["rnn_out_norm"], small["attn_out_norm"], wout)
    dx3, g2, u2, hb3, ab3, loss = _ffn_fwd_loss(x2, small["ffn2_norm"], *whole(FFN2), tgt)

    gb, gs, slots = {}, {}, {}
    dx2, dg2, du2, dyb2, gs["ffn2_norm"] = _ffn_bwd_act(x2, small["ffn2_norm"], dx3, g2, u2, *whole(FFN2), "ffn2_bwd")
    gb["ffn2_w_gate"] = _ffn_wgrad(dg2, hb3, 1.0, "wgrad_gate_ffn2")
    gb["ffn2_w_up"] = _ffn_wgrad(du2, hb3, 1.0, "wgrad_up_ffn2")
    gb["ffn2_w_down"] = _ffn_wgrad(ab3, dyb2, 0.5, "wgrad_down_ffn2")
    dyr, dya, ycat, dxb2, gs["rnn_out_norm"], gs["attn_out_norm"] = _mix_post_bwd(
        dx2, yr, ya, small["rnn_out_norm"], small["attn_out_norm"], wout)
    gb["w_out"] = _wgrad_whole(ycat, dxb2, False, "wgrad_out")
    early = FFN2 + ["w_out"]
    dq, dk, dv, gs["q_norm"], gs["k_norm"], *done = _attn_bwd(
        proj, dya, small["q_norm"], small["k_norm"], _pair_sums(gb, early, where))
    slots.update(zip(early, done))
    dxr, dgate, gs["conv_w"], gs["conv_b"], dwa, gs["rg_b_a"], dwx, gs["rg_b_x"], gs["rg_lambda"] = _rglru_bwd(
        proj, hseq, dyr, *rg)
    gs["rg_w_a"] = _diag_blocks(dwa, RNN_BLOCKS)
    gs["rg_w_x"] = _diag_blocks(dwx, RNN_BLOCKS)
    dpb = jnp.concatenate([dxr, dgate, dq, dk, dv], axis=1)
    dx1, gs["mix_norm"] = _mix_pre_bwd(x1, small["mix_norm"], dx2, dpb, big["w_in"])
    dx0, dg1, du1, dyb1, gs["ffn1_norm"] = _ffn_bwd_act(x, small["ffn1_norm"], dx1, g1, u1, *whole(FFN1), "ffn1_bwd")

    mine = _place_shard(_pack([gs[n] for n in SMALL] + [loss[:, :1]]), where, F32, "place_small_grads",
                        by_device=True)
    gb["ffn1_w_gate"], everyone = _ffn_wgrad(dg1, hb1, 1.0, "wgrad_gate_ffn1", _small_rider(mine))
    gb["ffn1_w_up"], slots["ffn1_w_gate"] = _ffn_wgrad(
        du1, hb1, 1.0, "wgrad_up_ffn1", _pair_sums(gb, ["ffn1_w_gate"], where))
    gb["ffn1_w_down"], slots["ffn1_w_up"] = _ffn_wgrad(
        ab1, dyb1, 0.5, "wgrad_down_ffn1", _pair_sums(gb, ["ffn1_w_up"], where))
    gb["w_in"], slots["ffn1_w_down"] = _wgrad_whole(
        hb2, dpb, True, "wgrad_in", _pair_sums(gb, ["ffn1_w_down"], where))
    return dx0, slots, _pair_sums(gb, ["w_in"], where), gs, everyone


ANY = pl.BlockSpec(memory_space=pl.ANY)


def _place():
    x, y, c = lax.axis_index("x"), lax.axis_index("y"), lax.axis_index("c")
    other_chips = [(1 - x, y), (x, 1 - y), (1 - x, 1 - y)]
    return x, y, c, 2 * x + y, other_chips


def _remote(src, dst, send_sem, recv_sem, to):
    return pltpu.make_async_remote_copy(src_ref=src, dst_ref=dst, send_sem=send_sem, recv_sem=recv_sem,
                                        device_id=to, device_id_type=MESH)


def _copy_plan(pairs):
    sends = [functools.partial(_remote, *a) for a, _ in pairs]
    arrivals = [functools.partial(_remote, *b) for _, b in pairs]
    return sends, arrivals


class _Rider:
    def __init__(self, plan, plain, inplace, n_copies=None, relay=None, n_relay=0):
        self.plan, self.plain, self.inplace = plan, list(plain), list(inplace)
        self.n_copies = n_copies or 3 * len(self.inplace)
        self.relay, self.n_relay = relay, n_relay

    def operands(self):
        return self.plain + self.inplace

    def out_shape(self):
        return [jax.ShapeDtypeStruct(a.shape, a.dtype) for a in self.inplace]

    def aliases(self, inputs_before, outputs_before):
        return {inputs_before + len(self.plain) + k: outputs_before + k for k in range(len(self.inplace))}

    def scratch(self):
        relay = [pltpu.SemaphoreType.DMA((self.n_relay,))] * 2 if self.relay else []
        return [pltpu.SemaphoreType.DMA((self.n_copies,))] * 2 + relay


def _split_refs(refs, n_in, n_out, rider):
    if rider is None:
        return refs[:n_in], refs[n_in:n_in + n_out], refs[n_in + n_out:], None
    r_in, r_out = len(rider.operands()), len(rider.inplace)
    outs_at = n_in + r_in
    n_sems = len(rider.scratch())
    rest = refs[outs_at + n_out + r_out:]
    sems = rest[len(rest) - n_sems:]
    filled = refs[outs_at + n_out:outs_at + n_out + r_out]
    copies = functools.partial(rider.plan, refs[n_in:n_in + len(rider.plain)], filled, *sems[:2])
    relay = functools.partial(rider.relay, filled, *sems[2:]) if rider.relay else None
    return refs[:n_in], refs[outs_at:outs_at + n_out], rest[:len(rest) - n_sems], (copies, relay)


def _ride(copies, first, last, middle=None):
    if copies is None:
        return lambda: None
    copies, relay = copies

    @pl.when(first)
    def _():
        _start(copies()[0])

    def start_relay():
        for make in copies()[1]:
            make().wait_recv()
        _start(relay()[0])

    if relay is not None and middle is not None:
        pl.when(middle)(start_relay)

    def finish():
        @pl.when(last)
        def _():
            if relay is None:
                _finish(*copies())
            else:
                if middle is None:
                    start_relay()
                _finish(copies()[0] + relay()[0], relay()[1])

    return finish


def _gather_rider(split, whole):
    n_split = len(split)
    return _Rider(lambda plain, stacks, ss, rs: _gather_ici(stacks, n_split, ss, rs), [], list(split) + list(whole),
                  relay=lambda stacks, ss, rs: _gather_d2d(stacks[:n_split], ss, rs), n_relay=3 * n_split)


def _chip_rider(sums, slots):
    return _Rider(_chip_copies, sums, slots)


def _start(makers):
    for make in makers:
        make().start()


def _finish(sends, arrivals):
    for make in arrivals:
        make().wait_recv()
    for make in sends:
        make().wait_send()


def _half(rows, c):
    return pl.ds(pl.multiple_of(c * rows, BF16_ROWS), rows)


def _gather_weights(split, whole):
    arrs = list(split) + list(whole)
    n, ns = len(arrs), len(split)

    def body(*refs):
        outs = refs[n:2 * n]
        send_sems, recv_sems, fsend_sems, frecv_sems = refs[2 * n:]
        sends, arrivals = _gather_ici(outs, ns, send_sems, recv_sems)
        passes, passed = _gather_d2d(outs[:ns], fsend_sems, frecv_sems)
        _start(sends)
        for k, make in enumerate(arrivals):
            make().wait_recv()
            if k < 3 * ns:
                passes[k]().start()
        _finish(sends + passes, passed)

    return pl.pallas_call(
        body, name="gather_weights",
        in_specs=[ANY] * n, out_specs=[ANY] * n,
        out_shape=[jax.ShapeDtypeStruct(a.shape, a.dtype) for a in arrs],
        input_output_aliases={i: i for i in range(n)},
        scratch_shapes=[pltpu.SemaphoreType.DMA((3 * n,)), pltpu.SemaphoreType.DMA((3 * n,)),
                        pltpu.SemaphoreType.DMA((3 * ns,)), pltpu.SemaphoreType.DMA((3 * ns,))],
    )(*arrs)


def _gather_ici(stacks, n_split, send_sems, recv_sems):
    x, y, c, me, chips = _place()

    def region(i, chip):
        if i < n_split:
            return stacks[i].at[chip, _half(stacks[i].shape[1] // 2, c)]
        return stacks[i].at[chip]

    pairs = []
    for i in range(len(stacks)):
        for p, (cx, cy) in enumerate(chips):
            k = 3 * i + p
            mine, got = region(i, me), region(i, 2 * cx + cy)
            sems, to = (send_sems.at[k], recv_sems.at[k]), (cx, cy, c)
            pairs.append(((mine, mine, *sems, to), (got, got, *sems, to)))
    return _copy_plan(pairs)


def _gather_d2d(stacks, send_sems, recv_sems):
    x, y, c, _, chips = _place()
    sibling = (x, y, 1 - c)
    pairs = []
    for i, stack in enumerate(stacks):
        rows = stack.shape[1] // 2
        for p, (cx, cy) in enumerate(chips):
            k = 3 * i + p
            got, theirs = stack.at[2 * cx + cy, _half(rows, c)], stack.at[2 * cx + cy, _half(rows, 1 - c)]
            sems = (send_sems.at[k], recv_sems.at[k])
            pairs.append(((got, got, *sems, sibling), (theirs, theirs, *sems, sibling)))
    return _copy_plan(pairs)


def _pair_exchange(grads, name):
    n = len(grads)

    def body(*refs):
        ins, theirs = refs[:n], refs[n:2 * n]
        send_sems, recv_sems = refs[2 * n:]
        x, y, c, _, _ = _place()
        sibling = (x, y, 1 - c)
        sends = [_remote(ins[k].at[:, _half(grads[k].shape[1] // 2, 1 - c)], theirs[k],
                         send_sems.at[k], recv_sems.at[k], sibling) for k in range(n)]
        for cp in sends:
            cp.start()
        for k in range(n):
            _remote(theirs[k], theirs[k], send_sems.at[k], recv_sems.at[k], sibling).wait_recv()
        for cp in sends:
            cp.wait_send()

    return pl.pallas_call(
        body, name=name,
        in_specs=[ANY] * n, out_specs=[ANY] * n,
        out_shape=[jax.ShapeDtypeStruct((g.shape[0], g.shape[1] // 2, g.shape[2]), g.dtype) for g in grads],
        scratch_shapes=[pltpu.SemaphoreType.DMA((n,))] * 2,
    )(*grads)


def _chip_copies(sums, slots, send_sems, recv_sems):
    x, y, c, me, chips = _place()
    pairs = []
    for k in range(len(sums)):
        for p, (cx, cy) in enumerate(chips):
            j = 3 * k + p
            got = slots[k].at[2 * cx + cy]
            sems, to = (send_sems.at[j], recv_sems.at[j]), (cx, cy, c)
            pairs.append(((sums[k].at[2 * cx + cy], slots[k].at[me], *sems, to), (got, got, *sems, to)))
    return _copy_plan(pairs)


def _half_swap(halves, name):
    n = len(halves)

    def body(*refs):
        outs = refs[n:2 * n]
        send_sems, recv_sems = refs[2 * n:]
        x, y, c, _, _ = _place()
        sibling = (x, y, 1 - c)
        sends = [_remote(outs[k].at[c], outs[k].at[c], send_sems.at[k], recv_sems.at[k], sibling) for k in range(n)]
        for cp in sends:
            cp.start()
        for k in range(n):
            got = outs[k].at[1 - c]
            _remote(got, got, send_sems.at[k], recv_sems.at[k], sibling).wait_recv()
        for cp in sends:
            cp.wait_send()

    return pl.pallas_call(
        body, name=name,
        in_specs=[ANY] * n, out_specs=[ANY] * n,
        out_shape=[jax.ShapeDtypeStruct(a.shape, a.dtype) for a in halves],
        input_output_aliases={k: k for k in range(n)},
        scratch_shapes=[pltpu.SemaphoreType.DMA((n,))] * 2,
    )(*halves)


def _small_rider(stack):
    n_dev = 2 * N_CHIPS

    def plan(_, stacks, send_sems, recv_sems):
        x, y, c, _, _ = _place()
        mine = stacks[0].at[4 * x + 2 * y + c]
        pairs = []
        for k in range(1, n_dev):
            px, py, pc = x ^ ((k >> 2) & 1), y ^ ((k >> 1) & 1), c ^ (k & 1)
            got = stacks[0].at[4 * px + 2 * py + pc]
            sems = (send_sems.at[k - 1], recv_sems.at[k - 1])
            pairs.append(((mine, mine, *sems, (px, py, pc)), (got, got, *sems, (px, py, pc))))
        return _copy_plan(pairs)

    return _Rider(plan, [], [stack], n_dev - 1)


def _row_tile(r):
    return r // 4 if r >= 256 and (r // 4) % BF16_ROWS == 0 else r


def _prefetch_call(body, name, grid, in_specs, out_specs, out_shape):
    spec = pltpu.PrefetchScalarGridSpec(num_scalar_prefetch=1, grid=grid, in_specs=in_specs, out_specs=out_specs)
    return pl.pallas_call(body, name=name, grid_spec=spec, out_shape=out_shape,
                          compiler_params=_params(("arbitrary",) * len(grid)))


def _place_shard(w2d, where, dtype, name, by_device=False):
    r, c = w2d.shape
    tr = _row_tile(r)
    slots = 2 * N_CHIPS if by_device else N_CHIPS
    slot = (lambda s: 2 * s[1] + s[0]) if by_device else (lambda s: s[1])

    def body(where_ref, w_ref, out_ref):
        out_ref[...] = w_ref[...].astype(dtype)

    return _prefetch_call(
        body, name, (r // tr,), [pl.BlockSpec((tr, c), lambda i, s: (i, 0))],
        pl.BlockSpec((None, tr, c), lambda i, s: (slot(s), i, 0)),
        jax.ShapeDtypeStruct((slots, r, c), dtype))(where, w2d)


def _place_shards(w2ds, where, name):
    n = len(w2ds)
    steps = N_CHIPS
    assert all(w.shape[0] % (BF16_ROWS * steps) == 0 for w in w2ds)

    def body(where_ref, *refs):
        for k in range(n):
            refs[n + k][...] = refs[k][...].astype(BF16)

    tile = lambda w: (w.shape[0] // steps, w.shape[1])
    return _prefetch_call(
        body, name, (steps,), [pl.BlockSpec(tile(w), lambda i, s: (i, 0)) for w in w2ds],
        [pl.BlockSpec((None,) + tile(w), lambda i, s: (s[1], i, 0)) for w in w2ds],
        [jax.ShapeDtypeStruct((N_CHIPS,) + w.shape, BF16) for w in w2ds])(where, *w2ds)


def _pair_sum(fulls, theirs, where, name):
    n = len(fulls)

    def body(where_ref, *refs):
        for k in range(n):
            a_ref, b_ref, out_ref, own_ref = refs[k], refs[n + k], refs[2 * n + k], refs[3 * n + k]
            total = (a_ref[...].astype(F32) + b_ref[...].astype(F32)).astype(BF16)
            out_ref[...] = total

            @pl.when(pl.program_id(0) == where_ref[1])
            def _():
                own_ref[...] = total

    half = lambda t: pl.BlockSpec((None,) + t.shape[1:], lambda j, s: (j, s[0], 0))
    blk = lambda t: pl.BlockSpec((None,) + t.shape[1:], lambda j, s: (j, 0, 0))
    own = lambda t: pl.BlockSpec((None,) + t.shape[1:], lambda j, s: (s[1], 0, 0))
    shapes = [jax.ShapeDtypeStruct(t.shape, BF16) for t in theirs]
    outs = _prefetch_call(
        body, name, (N_CHIPS,), [half(t) for t in theirs] + [blk(t) for t in theirs],
        [blk(t) for t in theirs] + [own(t) for t in theirs], shapes + shapes)(where, *fulls, *theirs)
    return outs[:n], outs[n:]


def _chip_sum(slots, where, name):
    n = len(slots)
    steps = 2
    assert all(a.shape[1] % (BF16_ROWS * steps) == 0 for a in slots)

    def body(where_ref, *refs):
        for k in range(n):
            a_ref, out_ref = refs[k], refs[n + k]
            total = a_ref[0].astype(F32)
            for j in range(1, a_ref.shape[0]):
                total = total + a_ref[j].astype(F32)
            out_ref[...] = total

    tile = lambda a: (a.shape[1] // steps, a.shape[2])
    return _prefetch_call(
        body, name, (steps,), [pl.BlockSpec((a.shape[0],) + tile(a), lambda i, s: (0, i, 0)) for a in slots],
        [pl.BlockSpec((None,) + tile(a), lambda i, s: (s[0], i, 0)) for a in slots],
        [jax.ShapeDtypeStruct((2,) + a.shape[1:], F32) for a in slots])(where, *slots)


def _slot_sum(a, name):
    nb, r, c = a.shape
    tr = _row_tile(r)

    def body(a_ref, out_ref):
        total = a_ref[0].astype(F32)
        for j in range(1, nb):
            total = total + a_ref[j].astype(F32)
        out_ref[...] = total

    return pl.pallas_call(
        body, name=name, grid=(r // tr,),
        in_specs=[pl.BlockSpec((nb, tr, c), lambda i: (0, i, 0))],
        out_specs=pl.BlockSpec((tr, c), lambda i: (i, 0)),
        out_shape=jax.ShapeDtypeStruct((r, c), F32), compiler_params=_params(("arbitrary",)),
    )(a)


def _adamw(ws, gs, ms, vs, name, steps=1, rider=None):
    n = len(ws)
    c1 = 1.0 - ADAM_B1 ** ADAM_STEP
    c2 = 1.0 - ADAM_B2 ** ADAM_STEP
    assert all(w.shape[0] % steps == 0 and (steps == 1 or w.shape[0] // steps % 8 == 0) for w in ws)

    def body(*refs):
        ins, outs, _, copies = _split_refs(refs, 4 * n, 4 * n, rider)
        finish = _ride(copies, pl.program_id(0) == 0, pl.program_id(0) == steps - 1)
        for k in range(n):
            w_ref, g_ref, m_ref, v_ref = (ins[j * n + k] for j in range(4))
            g_out, d_ref, m2_ref, v2_ref = (outs[j * n + k] for j in range(4))
            gv = g_ref[...]
            g_out[...] = gv
            m2 = ADAM_B1 * m_ref[...] + (1.0 - ADAM_B1) * gv
            v2 = ADAM_B2 * v_ref[...] + (1.0 - ADAM_B2) * (gv * gv)
            m2_ref[...] = m2
            v2_ref[...] = v2
            d_ref[...] = -ADAM_LR * ((m2 / c1) / (jnp.sqrt(v2 / c2) + ADAM_EPS) + ADAM_WD * w_ref[...])
        finish()

    blks = [pl.BlockSpec((w.shape[0] // steps, w.shape[1]), lambda i: (i, 0)) for w in ws]
    shapes = [jax.ShapeDtypeStruct(w.shape, F32) for w in ws]
    outs = _call(body, name, (steps,), blks * 4, blks * 4, shapes * 4, [*ws, *gs, *ms, *vs], rider=rider)
    return [outs[j * n:(j + 1) * n] for j in range(4)] + list(outs[4 * n:])


WEIGHTS = ["ffn1_norm", "ffn1_w_gate", "ffn1_w_up", "ffn1_w_down", "mix_norm", "w_in", "conv_w", "conv_b",
           "rg_w_a", "rg_b_a", "rg_w_x", "rg_b_x", "rg_lambda", "q_norm", "k_norm", "rnn_out_norm",
           "attn_out_norm", "w_out", "ffn2_norm", "ffn2_w_gate", "ffn2_w_up", "ffn2_w_down"]
BIG = ["ffn1_w_gate", "ffn1_w_up", "ffn1_w_down", "w_in", "w_out", "ffn2_w_gate", "ffn2_w_up", "ffn2_w_down"]
SMALL = [n for n in WEIGHTS if n not in BIG]
PACK_LANES = 128
PACK_ROW_ALIGN = 8


def _hidden_major(name, a):
    return jnp.transpose(a) if name.endswith(("w_gate", "w_up")) else a


def _pack(parts):
    flat = jnp.concatenate([p.reshape(-1) for p in parts])
    unit = PACK_LANES * PACK_ROW_ALIGN
    padded = -(-flat.shape[0] // unit) * unit
    return jnp.pad(flat, (0, padded - flat.shape[0])).reshape(-1, PACK_LANES)


def _unpack(packed, shapes):
    flat = packed.reshape(-1)
    out, at = [], 0
    for shp in shapes:
        size = math.prod(shp)
        out.append(flat[at:at + size].reshape(shp))
        at += size
    return out


def kernel(x, ffn1_norm, ffn1_w_gate, ffn1_w_up, ffn1_w_down, mix_norm, w_in, conv_w, conv_b, rg_w_a, rg_b_a, rg_w_x, rg_b_x, rg_lambda, q_norm, k_norm, rnn_out_norm, attn_out_norm, w_out, ffn2_norm, ffn2_w_gate, ffn2_w_up, ffn2_w_down, loss_target, m_ffn1_norm, m_ffn1_w_gate, m_ffn1_w_up, m_ffn1_w_down, m_mix_norm, m_w_in, m_conv_w, m_conv_b, m_rg_w_a, m_rg_b_a, m_rg_w_x, m_rg_b_x, m_rg_lambda, m_q_norm, m_k_norm, m_rnn_out_norm, m_attn_out_norm, m_w_out, m_ffn2_norm, m_ffn2_w_gate, m_ffn2_w_up, m_ffn2_w_down, v_ffn1_norm, v_ffn1_w_gate, v_ffn1_w_up, v_ffn1_w_down, v_mix_norm, v_w_in, v_conv_w, v_conv_b, v_rg_w_a, v_rg_b_a, v_rg_w_x, v_rg_b_x, v_rg_lambda, v_q_norm, v_k_norm, v_rnn_out_norm, v_attn_out_norm, v_w_out, v_ffn2_norm, v_ffn2_w_gate, v_ffn2_w_up, v_ffn2_w_down):
    given = dict(locals())
    w = {n: given[n] for n in WEIGHTS}
    m = {n: given["m_" + n] for n in WEIGHTS}
    v = {n: given["v_" + n] for n in WEIGHTS}
    chip = 2 * lax.axis_index("x") + lax.axis_index("y")

    where = jnp.stack([lax.axis_index("c"), chip]).astype(jnp.int32)

    stacks = dict(zip(BIG, _place_shards([_hidden_major(n, w[n][0]) for n in BIG], where, "place_weights")))
    conv_stack = _place_shard(w["conv_w"][0], where, F32, "place_conv_w")
    small = {n: (w[n][0] if w[n].ndim > 2 else w[n]) for n in SMALL if n != "conv_w"}

    grad_x, slots, w_in_exchange, gs, everyone = _local_step(x[0], loss_target[0], stacks, conv_stack, small, where)

    grads, deltas, new_m, new_v = {}, {}, {}, {}

    def update(names, slot_list, tag, rider=None):
        swapped = _half_swap(_chip_sum(slot_list, where, "chip_sums_" + tag), "half_swap_" + tag)
        g2s = [t.reshape(t.shape[0] * t.shape[1], t.shape[2]) for t in swapped]
        flat = lambda tree: [_hidden_major(n, tree[n][0]) for n in names]
        *parts, = _adamw(flat(w), g2s, flat(m), flat(v), "adamw_" + tag, ADAMW_STEPS, rider)
        for tree, arrays in zip((grads, deltas, new_m, new_v), parts):
            tree.update({n: _hidden_major(n, a).reshape(w[n].shape) for n, a in zip(names, arrays)})
        return parts[4:]

    others = [n for n in BIG if n != "w_in"]
    w_in_slots = update(others, [slots[n] for n in others], "weights", w_in_exchange)
    update(["w_in"], w_in_slots, "w_in")

    full_shapes = [gs[n].shape for n in SMALL]
    *summed, loss = _unpack(_slot_sum(everyone, "small_grad_sum"), full_shapes + [(1, 1)])
    g_parts = dict(zip(SMALL, summed))
    quarter = D_RNN // N_CHIPS
    g_parts["conv_w"] = lax.dynamic_slice_in_dim(g_parts["conv_w"], chip * quarter, quarter, axis=1)
    local_shapes = [w[n].shape for n in SMALL]
    pk = lambda tree: _pack([tree[n] for n in SMALL])
    (g_s,), (d_s,), (m_s,), (v_s,) = _adamw([pk(w)], [pk(g_parts)], [pk(m)], [pk(v)], "adamw_small")
    for tree, packed in ((grads, g_s), (deltas, d_s), (new_m, m_s), (new_v, v_s)):
        tree.update(zip(SMALL, _unpack(packed, local_shapes)))

    return (loss[0, 0], grad_x.reshape(x.shape), *[grads[n] for n in WEIGHTS], *[deltas[n] for n in WEIGHTS],
            *[new_m[n] for n in WEIGHTS], *[new_v[n] for n in WEIGHTS])
```

```python
import functools
import math

import jax
import jax.numpy as jnp
from jax import lax
from jax.experimental import pallas as pl
from jax.experimental.pallas import tpu as pltpu

F32 = jnp.float32
BF16 = jnp.bfloat16
MESH = pl.DeviceIdType.MESH

D_MODEL = 1024
N_CHIPS = 4
D_RNN = 512
D_ATT = 512
N_HEADS = 8
HEAD_DIM = 64
RNN_BLOCKS = 8
CONV_W = 4
RG_C = 8.0
N_IN = 2 * D_RNN + 3 * D_ATT
EPS = 1e-6
ATT_BLOCK = 128
ATT_WINDOW = 384
ATT_SPLIT = 256
EXP_ZERO = -105.0

ADAM_LR = 0.001
ADAM_B1 = 0.9
ADAM_B2 = 0.999
ADAM_EPS = 1e-08
ADAM_WD = 0.01
ADAM_STEP = 10

V7X_VMEM_LIMIT = 60 * 1024 * 1024
V7X_MXU_WIDTH = 256
TOKEN_TILE = 512
SUBLANES = 8
BF16_ROWS = 16
FFN_TILE = 256
WGRAD_TILE = 2048
WHOLE_TILE = 1024
ADAMW_STEPS = 8

GELU_K0 = math.sqrt(2.0 / math.pi)
GELU_K1 = 0.044715


def _params(sem=None):
    return pltpu.CompilerParams(dimension_semantics=sem, vmem_limit_bytes=V7X_VMEM_LIMIT)


def _dot(a, b):
    return jnp.dot(a, b, preferred_element_type=F32)


def _dot_nt(a, b):
    return lax.dot_general(a, b, (((1,), (1,)), ((), ())), preferred_element_type=F32)


def _dot_tn(a, b):
    return lax.dot_general(a, b, (((0,), (0,)), ((), ())), preferred_element_type=F32)


def _sigmoid(x):
    return 1.0 / (1.0 + jnp.exp(-x))


def _rms_r(xv):
    return lax.rsqrt(jnp.mean(xv * xv, axis=-1, keepdims=True) + EPS)


def _rms_bwd(xv, r, nw, dh):
    t = dh * nw
    dx = r * t - xv * (r * r * r * jnp.mean(t * xv, axis=-1, keepdims=True))
    dn = jnp.sum(dh * xv * r, axis=0, keepdims=True)
    return dx, dn


def _gelu(x):
    t = jnp.tanh(GELU_K0 * (x + GELU_K1 * x * x * x))
    return 0.5 * x * (1.0 + t)


def _gelu_grad(x):
    t = jnp.tanh(GELU_K0 * (x + GELU_K1 * x * x * x))
    return 0.5 * (1.0 + t) + 0.5 * x * (1.0 - t * t) * (GELU_K0 * (1.0 + 3.0 * GELU_K1 * x * x))


def _expm1_neg(x):
    p = 1.0 + x * (1.0 / 6.0)
    for k in (5.0, 4.0, 3.0, 2.0):
        p = 1.0 + x * (1.0 / k) * p
    return jnp.where(x > -0.25, x * p, jnp.exp(x) - 1.0)


def _log_sigmoid(x):
    return jnp.minimum(x, 0.0) - jnp.log(1.0 + jnp.exp(-jnp.abs(x)))


def _tile(s):
    return min(TOKEN_TILE, s)


def _ffn_chunks(f):
    cut = f // 2 // V7X_MXU_WIDTH * V7X_MXU_WIDTH
    return ((0, cut), (cut, f)) if 0 < cut < f else ((0, f),)


def _ffn_fwd_loss(x, nw, wg, wu, wd, tgt):
    s, d = x.shape
    f = wg.shape[0]
    tm = min(FFN_TILE, s)
    ni = s // tm
    assert s % tm == 0

    def body(x_ref, nw_ref, wg_ref, wu_ref, wd_ref, tgt_ref, out_ref, g_ref, u_ref, hb_ref, ab_ref, loss_ref):
        i = pl.program_id(0)
        xv = x_ref[...]
        hb = (xv * _rms_r(xv) * nw_ref[...]).astype(BF16)
        hb_ref[...] = hb
        y = jnp.zeros((tm, d), F32)
        for lo, hi in _ffn_chunks(f):
            g = _dot_nt(hb, wg_ref[lo:hi, :])
            u = _dot_nt(hb, wu_ref[lo:hi, :])
            g_ref[:, lo:hi] = g.astype(BF16)
            u_ref[:, lo:hi] = u.astype(BF16)
            ab = (g * _sigmoid(g) * u).astype(BF16)
            ab_ref[:, lo:hi] = ab
            y = y + _dot(ab, wd_ref[lo:hi, :])
        diff = xv + 0.5 * y - tgt_ref[...]
        out_ref[...] = diff * (1.0 / d)

        @pl.when(i == 0)
        def _():
            loss_ref[...] = jnp.zeros_like(loss_ref)

        loss_ref[...] += jnp.sum(diff * diff) * (0.5 / d)

    row = pl.BlockSpec((tm, d), lambda i: (i, 0))
    weight = pl.BlockSpec((f, d), lambda i: (0, 0), pipeline_mode=pl.Buffered(1))
    blk = pl.BlockSpec((tm, f), lambda i: (i, 0))
    wide = jax.ShapeDtypeStruct((s, f), BF16)
    return _call(body, "ffn_fwd_loss", (ni,),
                 [row, pl.BlockSpec((1, d), lambda i: (0, 0)), weight, weight, weight, row],
                 [row, blk, blk, row, blk, pl.BlockSpec((1, 128), lambda i: (0, 0))],
                 [jax.ShapeDtypeStruct((s, d), F32), wide, wide, jax.ShapeDtypeStruct((s, d), BF16), wide,
                  jax.ShapeDtypeStruct((1, 128), F32)], [x, nw, wg, wu, wd, tgt])


def _ffn_up(x, nw, wg, wu, rider=None):
    s, d = x.shape
    f = wg.shape[0]
    tm = _tile(s)
    ni = s // tm
    assert s % tm == 0

    def body(*refs):
        (x_ref, nw_ref, wg_ref, wu_ref), (g_ref, u_ref, hb_ref, ab_ref), _, copies = _split_refs(refs, 4, 4, rider)
        i = pl.program_id(0)
        finish = _ride(copies, i == 0, i == ni - 1)
        xv = x_ref[...]
        hb = (xv * _rms_r(xv) * nw_ref[...]).astype(BF16)
        hb_ref[...] = hb
        for lo, hi in _ffn_chunks(f):
            g = _dot_nt(hb, wg_ref[lo:hi, :])
            u = _dot_nt(hb, wu_ref[lo:hi, :])
            g_ref[:, lo:hi] = g.astype(BF16)
            u_ref[:, lo:hi] = u.astype(BF16)
            ab_ref[:, lo:hi] = (g * _sigmoid(g) * u).astype(BF16)
        finish()

    row = pl.BlockSpec((tm, d), lambda i: (i, 0))
    weight = pl.BlockSpec((f, d), lambda i: (0, 0), pipeline_mode=pl.Buffered(1))
    blk = pl.BlockSpec((tm, f), lambda i: (i, 0))
    wide = jax.ShapeDtypeStruct((s, f), BF16)
    return _call(body, "ffn_up", (ni,), [row, pl.BlockSpec((1, d), lambda i: (0, 0)), weight, weight],
                 [blk, blk, row, blk], [wide, wide, jax.ShapeDtypeStruct((s, d), BF16), wide], [x, nw, wg, wu],
                 rider=rider)


def _ffn_down(x, ab, wd):
    s, d = x.shape
    f = wd.shape[0]
    tm = _tile(s)
    assert s % tm == 0

    def body(x_ref, ab_ref, wd_ref, out_ref):
        out_ref[...] = x_ref[...] + 0.5 * _dot(ab_ref[...], wd_ref[...])

    row = pl.BlockSpec((tm, d), lambda i: (i, 0))
    return _call(body, "ffn_down", (s // tm,),
                 [row, pl.BlockSpec((tm, f), lambda i: (i, 0)),
                  pl.BlockSpec((f, d), lambda i: (0, 0), pipeline_mode=pl.Buffered(1))],
                 [row], [jax.ShapeDtypeStruct((s, d), F32)], [x, ab, wd])[0]


def _call(body, name, grid, in_specs, out_specs, out_shape, args, scratch=(), rider=None):
    in_specs, out_specs, out_shape, scratch = list(in_specs), list(out_specs), list(out_shape), list(scratch)
    extra, aliases = [], {}
    if rider is not None:
        extra = rider.operands()
        aliases = rider.aliases(len(args), len(out_shape))
        in_specs += [ANY] * len(extra)
        out_specs += [ANY] * len(rider.inplace)
        out_shape += rider.out_shape()
        scratch += rider.scratch()
    return pl.pallas_call(
        body, name=name, grid=grid, in_specs=in_specs, out_specs=out_specs, out_shape=out_shape,
        input_output_aliases=aliases, scratch_shapes=scratch,
        compiler_params=_params(("arbitrary",) * len(grid)),
    )(*args, *extra)


def _ffn_bwd_act(x, nw, dy, g, u, wg, wu, wd, name):
    s, d = x.shape
    f = wg.shape[0]
    tm = min(FFN_TILE, s)
    assert s % tm == 0

    def body(x_ref, nw_ref, dy_ref, g_ref, u_ref, wg_ref, wu_ref, wd_ref,
             dx_ref, dg_ref, du_ref, dyb_ref, dnw_ref):
        dyv = dy_ref[...]
        dyb = dyv.astype(BF16)
        dyb_ref[...] = dyb
        dh = jnp.zeros((tm, d), F32)
        for lo, hi in _ffn_chunks(f):
            da = 0.5 * _dot_nt(dyb, wd_ref[lo:hi, :])
            gv = g_ref[:, lo:hi].astype(F32)
            sg = _sigmoid(gv)
            dub = (da * (gv * sg)).astype(BF16)
            dgb = (da * u_ref[:, lo:hi].astype(F32) * (sg * (1.0 + gv * (1.0 - sg)))).astype(BF16)
            dg_ref[:, lo:hi] = dgb
            du_ref[:, lo:hi] = dub
            dh = dh + _dot(dgb, wg_ref[lo:hi, :]) + _dot(dub, wu_ref[lo:hi, :])
        xv = x_ref[...]
        dx, dn = _rms_bwd(xv, _rms_r(xv), nw_ref[...], dh)
        dx_ref[...] = dyv + dx

        @pl.when(pl.program_id(0) == 0)
        def _():
            dnw_ref[...] = jnp.zeros_like(dnw_ref)

        dnw_ref[...] += dn

    row = pl.BlockSpec((tm, d), lambda i: (i, 0))
    vec = pl.BlockSpec((1, d), lambda i: (0, 0))
    blk = pl.BlockSpec((tm, f), lambda i: (i, 0))
    weight = pl.BlockSpec((f, d), lambda i: (0, 0), pipeline_mode=pl.Buffered(1))
    return _call(
        body, name, (s // tm,), [row, vec, row, blk, blk, weight, weight, weight], [row, blk, blk, row, vec],
        [jax.ShapeDtypeStruct((s, d), F32), jax.ShapeDtypeStruct((s, f), BF16),
         jax.ShapeDtypeStruct((s, f), BF16), jax.ShapeDtypeStruct((s, d), BF16),
         jax.ShapeDtypeStruct((1, d), F32)],
        [x, nw, dy, g, u, wg, wu, wd])


def _wgrad(a, b, a_spec, b_spec, out_rows, out_cols, scale, name, tk, rider=None, per_step=1):
    s = a.shape[-2]
    nk = s // tk
    steps = N_CHIPS // per_step
    assert s % tk == 0

    def body(*refs):
        (a_ref, b_ref), (out_ref,), (acc,), copies = _split_refs(refs, 2, 1, rider)
        j, k = pl.program_id(0), pl.program_id(1)
        finish = _ride(copies, jnp.logical_and(j == 0, k == 0), jnp.logical_and(j == steps - 1, k == nk - 1))

        @pl.when(k == 0)
        def _():
            acc[...] = jnp.zeros_like(acc)

        acc[...] += _dot_tn(a_ref[...], b_ref[...])

        @pl.when(k == nk - 1)
        def _():
            for t in range(per_step):
                out_ref[t] = (acc[t * out_rows:(t + 1) * out_rows, :] * scale).astype(BF16)

        finish()

    outs = _call(
        body, name, (steps, nk), [a_spec(tk), b_spec(tk)],
        [pl.BlockSpec((per_step, out_rows, out_cols), lambda j, k: (j, 0, 0))],
        [jax.ShapeDtypeStruct((N_CHIPS, out_rows, out_cols), BF16)], [a, b],
        scratch=[pltpu.VMEM((per_step * out_rows, out_cols), F32)], rider=rider)
    return outs[0] if rider is None else outs


def _wgrad_whole(a, b, col_blocks, name, rider=None):
    s, m = a.shape
    n = b.shape[1]
    tk = min(WHOLE_TILE, s)
    nk = s // tk
    assert s % tk == 0
    out_shape = (N_CHIPS, m, n // N_CHIPS) if col_blocks else (N_CHIPS, m // N_CHIPS, n)

    def body(*refs):
        (a_ref, b_ref), (out_ref,), (acc,), copies = _split_refs(refs, 2, 1, rider)
        k = pl.program_id(0)
        finish = _ride(copies, k == 0, k == nk - 1)

        @pl.when(k == 0)
        def _():
            acc[...] = jnp.zeros_like(acc)

        acc[...] += _dot_tn(a_ref[...], b_ref[...])

        @pl.when(k == nk - 1)
        def _():
            for j in range(N_CHIPS):
                if col_blocks:
                    out_ref[j] = acc[:, j * out_shape[2]:(j + 1) * out_shape[2]].astype(BF16)
                else:
                    out_ref[j] = acc[j * out_shape[1]:(j + 1) * out_shape[1], :].astype(BF16)

        finish()

    outs = _call(
        body, name, (nk,), [pl.BlockSpec((tk, m), lambda k: (k, 0)), pl.BlockSpec((tk, n), lambda k: (k, 0))],
        [pl.BlockSpec(out_shape, lambda k: (0, 0, 0))], [jax.ShapeDtypeStruct(out_shape, BF16)], [a, b],
        scratch=[pltpu.VMEM((m, n), F32)], rider=rider)
    return outs[0] if rider is None else outs


def _ffn_wgrad(hidden, shared, scale, name, rider=None):
    s, d = shared.shape
    half = hidden.shape[1] // 2
    return _wgrad(hidden, shared, lambda tk: pl.BlockSpec((tk, half), lambda j, k: (k, j)),
                  lambda tk: pl.BlockSpec((tk, d), lambda j, k: (k, 0)), half // 2, d, scale, name,
                  min(WGRAD_TILE, s), rider, per_step=2)


def _mix_pre(x, nw, win, rider=None):
    s, d = x.shape
    nb, _, cb = win.shape
    tm = _tile(s)
    ni = s // tm
    assert s % tm == 0

    def body(*refs):
        (x_ref, nw_ref, w_ref), (p_ref, hb_ref), _, copies = _split_refs(refs, 3, 2, rider)
        finish = _ride(copies, pl.program_id(0) == 0, pl.program_id(0) == ni - 1)
        xv = x_ref[...]
        hb = (xv * _rms_r(xv) * nw_ref[...]).astype(BF16)
        hb_ref[...] = hb
        for j in range(nb):
            p_ref[:, j * cb:(j + 1) * cb] = _dot(hb, w_ref[j])
        finish()

    row = pl.BlockSpec((tm, d), lambda i: (i, 0))
    return _call(
        body, "mix_pre", (ni,),
        [row, pl.BlockSpec((1, d), lambda i: (0, 0)),
         pl.BlockSpec((nb, d, cb), lambda i: (0, 0, 0), pipeline_mode=pl.Buffered(1))],
        [pl.BlockSpec((tm, nb * cb), lambda i: (i, 0)), row],
        [jax.ShapeDtypeStruct((s, nb * cb), F32), jax.ShapeDtypeStruct((s, d), BF16)], [x, nw, win], rider=rider)


def _mix_pre_bwd(x, nw, dres, dpb, win):
    s, d = x.shape
    nb, _, cb = win.shape
    tm = _tile(s)
    assert s % tm == 0

    def body(x_ref, nw_ref, dres_ref, dp_ref, w_ref, dx_ref, dnw_ref):
        dh = jnp.zeros((tm, d), F32)
        for j in range(nb):
            dh = dh + _dot_nt(dp_ref[:, j * cb:(j + 1) * cb], w_ref[j])
        xv = x_ref[...]
        dx, dn = _rms_bwd(xv, _rms_r(xv), nw_ref[...], dh)
        dx_ref[...] = dres_ref[...] + dx

        @pl.when(pl.program_id(0) == 0)
        def _():
            dnw_ref[...] = jnp.zeros_like(dnw_ref)

        dnw_ref[...] += dn

    row = pl.BlockSpec((tm, d), lambda i: (i, 0))
    vec = pl.BlockSpec((1, d), lambda i: (0, 0))
    return pl.pallas_call(
        body, name="mix_pre_bwd", grid=(s // tm,),
        in_specs=[row, vec, row, pl.BlockSpec((tm, nb * cb), lambda i: (i, 0)),
                  pl.BlockSpec((nb, d, cb), lambda i: (0, 0, 0), pipeline_mode=pl.Buffered(1))],
        out_specs=[row, vec],
        out_shape=[jax.ShapeDtypeStruct((s, d), F32), jax.ShapeDtypeStruct((1, d), F32)],
        compiler_params=_params(("arbitrary",)),
    )(x, nw, dres, dpb, win)


def _mix_post(x, yr, ya, nr, na, wout):
    s, d = x.shape
    h = yr.shape[1]
    tm = _tile(s)

    def body(x_ref, yr_ref, ya_ref, nr_ref, na_ref, w_ref, out_ref):
        yrv = yr_ref[...]
        yav = ya_ref[...]
        onb = (yrv * _rms_r(yrv) * nr_ref[...]).astype(BF16)
        oab = (yav * _rms_r(yav) * na_ref[...]).astype(BF16)
        out_ref[...] = x_ref[...] + _dot(onb, w_ref[0:h, :]) + _dot(oab, w_ref[h:2 * h, :])

    row = pl.BlockSpec((tm, d), lambda i: (i, 0))
    half = pl.BlockSpec((tm, h), lambda i: (i, 0))
    vec = pl.BlockSpec((1, h), lambda i: (0, 0))
    return pl.pallas_call(
        body, name="mix_post", grid=(s // tm,),
        in_specs=[row, half, half, vec, vec, pl.BlockSpec((2 * h, d), lambda i: (0, 0))],
        out_specs=row, out_shape=jax.ShapeDtypeStruct((s, d), F32),
        compiler_params=_params(("arbitrary",)),
    )(x, yr, ya, nr, na, wout)


def _mix_post_bwd(dx, yr, ya, nr, na, wout):
    s, d = dx.shape
    h = yr.shape[1]
    tm = _tile(s)

    def body(dx_ref, yr_ref, ya_ref, nr_ref, na_ref, w_ref,
             dyr_ref, dya_ref, yc_ref, dxb_ref, dnr_ref, dna_ref):
        i = pl.program_id(0)
        dxb = dx_ref[...].astype(BF16)
        dxb_ref[...] = dxb
        dyc = _dot_nt(dxb, w_ref[...])
        yrv = yr_ref[...]
        yav = ya_ref[...]
        rr = _rms_r(yrv)
        ra = _rms_r(yav)
        yc_ref[:, 0:h] = (yrv * rr * nr_ref[...]).astype(BF16)
        yc_ref[:, h:2 * h] = (yav * ra * na_ref[...]).astype(BF16)
        dyr, dnr = _rms_bwd(yrv, rr, nr_ref[...], dyc[:, 0:h])
        dya, dna = _rms_bwd(yav, ra, na_ref[...], dyc[:, h:2 * h])
        dyr_ref[...] = dyr
        dya_ref[...] = dya

        @pl.when(i == 0)
        def _():
            dnr_ref[...] = jnp.zeros_like(dnr_ref)
            dna_ref[...] = jnp.zeros_like(dna_ref)

        dnr_ref[...] += dnr
        dna_ref[...] += dna

    row = pl.BlockSpec((tm, d), lambda i: (i, 0))
    half = pl.BlockSpec((tm, h), lambda i: (i, 0))
    vec = pl.BlockSpec((1, h), lambda i: (0, 0))
    return pl.pallas_call(
        body, name="mix_post_bwd", grid=(s // tm,),
        in_specs=[row, half, half, vec, vec, pl.BlockSpec((2 * h, d), lambda i: (0, 0))],
        out_specs=[half, half, pl.BlockSpec((tm, 2 * h), lambda i: (i, 0)), row, vec, vec],
        out_shape=[jax.ShapeDtypeStruct((s, h), F32), jax.ShapeDtypeStruct((s, h), F32),
                   jax.ShapeDtypeStruct((s, 2 * h), BF16), jax.ShapeDtypeStruct((s, d), BF16),
                   jax.ShapeDtypeStruct((1, h), F32), jax.ShapeDtypeStruct((1, h), F32)],
        compiler_params=_params(("arbitrary",)),
    )(dx, yr, ya, nr, na, wout)


def _shift_down(xv, s, prev8):
    rolled = pltpu.roll(xv, s, 0)
    row8 = lax.broadcasted_iota(jnp.int32, prev8.shape, 0)
    head = jnp.where(row8 < s, pltpu.roll(prev8, s, 0), rolled[0:8, :])
    return jnp.concatenate([head, rolled[8:, :]], axis=0)


def _shift_up(xv, s, next8):
    n = xv.shape[0]
    rolled = pltpu.roll(xv, n - s, 0)
    row8 = lax.broadcasted_iota(jnp.int32, next8.shape, 0)
    tail = jnp.where(row8 >= 8 - s, pltpu.roll(next8, 8 - s, 0), rolled[n - 8:, :])
    return jnp.concatenate([rolled[:n - 8, :], tail], axis=0)


def _scan_fwd(a, b):
    n = a.shape[0]
    sub = lax.broadcasted_iota(jnp.int32, a.shape, 0) % SUBLANES
    s = 1
    while s < SUBLANES:
        ok = sub >= s
        b = jnp.where(ok, a * pltpu.roll(b, s, 0) + b, b)
        a = jnp.where(ok, a * pltpu.roll(a, s, 0), a)
        s *= 2
    groups = []
    before = jnp.zeros((1, a.shape[1]), F32)
    for g in range(n // SUBLANES):
        rows = slice(g * SUBLANES, (g + 1) * SUBLANES)
        groups.append(a[rows] * before + b[rows])
        before = groups[-1][SUBLANES - 1:]
    return jnp.concatenate(groups, axis=0)


def _scan_bwd(a, b):
    n = a.shape[0]
    sub = lax.broadcasted_iota(jnp.int32, a.shape, 0) % SUBLANES
    s = 1
    while s < SUBLANES:
        ok = sub < SUBLANES - s
        b = jnp.where(ok, a * pltpu.roll(b, n - s, 0) + b, b)
        a = jnp.where(ok, a * pltpu.roll(a, n - s, 0), a)
        s *= 2
    groups = []
    after = jnp.zeros((1, a.shape[1]), F32)
    for g in reversed(range(n // SUBLANES)):
        rows = slice(g * SUBLANES, (g + 1) * SUBLANES)
        groups.append(a[rows] * after + b[rows])
        after = groups[-1][:1]
    return jnp.concatenate(groups[::-1], axis=0)


def _rglru_gates(xv, prev8, cw_ref, cb_ref, wa_ref, ba_ref, wx_ref, bx_ref, lam_ref):
    x1 = _shift_down(xv, 1, prev8)
    x2 = _shift_down(xv, 2, prev8)
    x3 = _shift_down(xv, 3, prev8)
    xc = cw_ref[3:4, :] * xv + cw_ref[2:3, :] * x1 + cw_ref[1:2, :] * x2 + cw_ref[0:1, :] * x3 + cb_ref[...]
    xcb = xc.astype(BF16)
    r = _sigmoid(_dot(xcb, wa_ref[...]) + ba_ref[...])
    ig = _sigmoid(_dot(xcb, wx_ref[...]) + bx_ref[...])
    c = RG_C * _log_sigmoid(lam_ref[...])
    la = r * c
    a = jnp.exp(la)
    m = jnp.sqrt(-_expm1_neg(2.0 * la))
    return (x1, x2, x3), xc, xcb, r, ig, c, a, m


def _rglru_fwd(proj, cw, cb, wa, ba, wx, bx, lam, rider=None):
    s = proj.shape[0]
    w = D_RNN
    tm = _tile(s)
    ni = s // tm

    def body(*refs):
        ins, (y_ref, h_ref), (prev, hlast), copies = _split_refs(refs, 9, 2, rider)
        xr_ref, gate_ref, cw_ref, cb_ref, wa_ref, ba_ref, wx_ref, bx_ref, lam_ref = ins
        finish = _ride(copies, pl.program_id(0) == 0, pl.program_id(0) == ni - 1)

        @pl.when(pl.program_id(0) == 0)
        def _():
            prev[...] = jnp.zeros_like(prev)
            hlast[...] = jnp.zeros_like(hlast)

        xv = xr_ref[...]
        _, xc, _, _, ig, _, a, m = _rglru_gates(xv, prev[...], cw_ref, cb_ref, wa_ref, ba_ref,
                                                wx_ref, bx_ref, lam_ref)
        b = m * (ig * xc)
        row = lax.broadcasted_iota(jnp.int32, b.shape, 0)
        b = jnp.where(row == 0, b + a * hlast[...], b)
        h = _scan_fwd(a, b)
        h_ref[...] = h
        y_ref[...] = h * _gelu(gate_ref[...])
        prev[...] = xv[tm - 8:, :]
        hlast[...] = h[tm - 1:tm, :]
        finish()

    vec = pl.BlockSpec((1, w), lambda i: (0, 0))
    sq = pl.BlockSpec((w, w), lambda i: (0, 0))
    out = pl.BlockSpec((tm, w), lambda i: (i, 0))
    return _call(
        body, "rglru_fwd", (ni,),
        [pl.BlockSpec((tm, w), lambda i: (i, 0)), pl.BlockSpec((tm, w), lambda i: (i, 1)),
         pl.BlockSpec((CONV_W, w), lambda i: (0, 0)), vec, sq, vec, sq, vec, vec], [out, out],
        [jax.ShapeDtypeStruct((s, w), F32), jax.ShapeDtypeStruct((s, w), F32)],
        [proj, proj, cw, cb, wa, ba, wx, bx, lam],
        scratch=[pltpu.VMEM((8, w), F32), pltpu.VMEM((1, w), F32)], rider=rider)


def _rglru_bwd(proj, hseq, dyr, cw, cb, wa, ba, wx, bx, lam):
    s = proj.shape[0]
    w = D_RNN
    tm = _tile(s)
    nt = s // tm
    t8 = tm // 8

    def body(xr_ref, xp_ref, gate_ref, h_ref, hp_ref, dy_ref, cw_ref, cb_ref, wa_ref, ba_ref,
             wx_ref, bx_ref, lam_ref,
             dxr_ref, dgate_ref, dcw_ref, dcb_ref, dwa_ref, dba_ref, dwx_ref, dbx_ref, dlam_ref,
             carry, dxc_next):
        i = pl.program_id(0)
        first_tile = i == nt - 1

        @pl.when(i == 0)
        def _():
            carry[...] = jnp.zeros_like(carry)
            dxc_next[...] = jnp.zeros_like(dxc_next)
            for ref in (dcw_ref, dcb_ref, dwa_ref, dba_ref, dwx_ref, dbx_ref, dlam_ref):
                ref[...] = jnp.zeros_like(ref)

        xv = xr_ref[...]
        prev8 = jnp.where(first_tile, 0.0, xp_ref[...])
        hprev8 = jnp.where(first_tile, 0.0, hp_ref[...])
        (x1, x2, x3), xc, xcb, r, ig, c, a, m = _rglru_gates(
            xv, prev8, cw_ref, cb_ref, wa_ref, ba_ref, wx_ref, bx_ref, lam_ref)
        gv = gate_ref[...]
        hv = h_ref[...]
        dy = dy_ref[...]
        dgate_ref[...] = (dy * hv * _gelu_grad(gv)).astype(BF16)
        dh = dy * _gelu(gv)
        row = lax.broadcasted_iota(jnp.int32, dh.shape, 0)
        dh = jnp.where(row == tm - 1, dh + carry[...], dh)
        a_up = jnp.where(row == tm - 1, 0.0, pltpu.roll(a, tm - 1, 0))
        lam_t = _scan_bwd(a_up, dh)
        carry[...] = a[0:1, :] * lam_t[0:1, :]
        hm1 = _shift_down(hv, 1, hprev8)
        da = lam_t * hm1
        ixc = ig * xc
        dm = lam_t * ixc
        dig = lam_t * m * xc
        dxc = lam_t * m * ig
        dla = da * a - dm * (a * a) / m
        dr = dla * c
        dlam_ref[...] += jnp.sum(dla * r, axis=0, keepdims=True)
        dpa = dr * r * (1.0 - r)
        dpi = dig * ig * (1.0 - ig)
        dba_ref[...] += jnp.sum(dpa, axis=0, keepdims=True)
        dbx_ref[...] += jnp.sum(dpi, axis=0, keepdims=True)
        dpab = dpa.astype(BF16)
        dpib = dpi.astype(BF16)
        dwa_ref[...] += _dot_tn(xcb, dpab)
        dwx_ref[...] += _dot_tn(xcb, dpib)
        dxc = dxc + _dot_nt(dpab, wa_ref[...]) + _dot_nt(dpib, wx_ref[...])
        dcb_ref[...] += jnp.sum(dxc, axis=0, keepdims=True)
        dcw_ref[3:4, :] += jnp.sum(dxc * xv, axis=0, keepdims=True)
        dcw_ref[2:3, :] += jnp.sum(dxc * x1, axis=0, keepdims=True)
        dcw_ref[1:2, :] += jnp.sum(dxc * x2, axis=0, keepdims=True)
        dcw_ref[0:1, :] += jnp.sum(dxc * x3, axis=0, keepdims=True)
        nxt = dxc_next[...]
        dxr = (cw_ref[3:4, :] * dxc + cw_ref[2:3, :] * _shift_up(dxc, 1, nxt)
               + cw_ref[1:2, :] * _shift_up(dxc, 2, nxt) + cw_ref[0:1, :] * _shift_up(dxc, 3, nxt))
        dxr_ref[...] = dxr.astype(BF16)
        dxc_next[...] = dxc[0:8, :]

        @pl.when(first_tile)
        def _():
            lv = lam_ref[...]
            dlam_ref[...] = dlam_ref[...] * (RG_C * _sigmoid(-lv))

    rev = lambda i: nt - 1 - i
    vec = pl.BlockSpec((1, w), lambda i: (0, 0))
    sq = pl.BlockSpec((w, w), lambda i: (0, 0))
    cur = lambda col: pl.BlockSpec((tm, w), lambda i: (rev(i), col))
    before = lambda cols: pl.BlockSpec((8, w), lambda i: (jnp.maximum(rev(i) * t8 - 1, 0), 0))
    return pl.pallas_call(
        body, name="rglru_bwd", grid=(nt,),
        in_specs=[cur(0), before(None), cur(1), cur(0), before(None), cur(0),
                  pl.BlockSpec((CONV_W, w), lambda i: (0, 0)), vec, sq, vec, sq, vec, vec],
        out_specs=[cur(0), cur(0), pl.BlockSpec((CONV_W, w), lambda i: (0, 0)), vec, sq, vec, sq, vec, vec],
        out_shape=[jax.ShapeDtypeStruct((s, w), BF16), jax.ShapeDtypeStruct((s, w), BF16),
                   jax.ShapeDtypeStruct((CONV_W, w), F32), jax.ShapeDtypeStruct((1, w), F32),
                   jax.ShapeDtypeStruct((w, w), F32), jax.ShapeDtypeStruct((1, w), F32),
                   jax.ShapeDtypeStruct((w, w), F32), jax.ShapeDtypeStruct((1, w), F32),
                   jax.ShapeDtypeStruct((1, w), F32)],
        scratch_shapes=[pltpu.VMEM((1, w), F32), pltpu.VMEM((8, w), F32)],
        compiler_params=_params(("arbitrary",)),
    )(proj, proj, proj, hseq, hseq, dyr, cw, cb, wa, ba, wx, bx, lam)


def _sb_logs(z, valid):
    lb = jnp.minimum(z, 0.0) - jnp.log(1.0 + jnp.exp(-jnp.abs(z)))
    return lb, jnp.where(valid, lb - z, 0.0)


class _Window:
    def __init__(self):
        blk, win, cut = ATT_BLOCK, ATT_WINDOW, ATT_SPLIT
        self.row = lax.broadcasted_iota(jnp.int32, (blk, win), 0)
        self.col = lax.broadcasted_iota(jnp.int32, (blk, win), 1)

        def tri(n, later):
            j = lax.broadcasted_iota(jnp.int32, (n, n), 0)
            s = lax.broadcasted_iota(jnp.int32, (n, n), 1)
            return jnp.where((j > s) if later else (j < s), 1.0, 0.0).astype(BF16)

        self.later = (tri(cut, True), tri(win - cut, True))
        self.earlier = (tri(cut, False), tri(win - cut, False))

    def place(self, qi, g):
        end = (qi + 1) * ATT_BLOCK - g * ATT_WINDOW
        start = pl.multiple_of(jnp.maximum(end - ATT_WINDOW, 0), ATT_BLOCK)
        valid = self.col < jnp.minimum(self.row + (qi * ATT_BLOCK - start), end - start)
        return start, valid

    @staticmethod
    def _parts(xv):
        hi = xv.astype(BF16)
        lo = (xv - hi.astype(F32)).astype(BF16)
        cut = ATT_SPLIT
        sums = (jnp.sum(xv[:, :cut], axis=1, keepdims=True), jnp.sum(xv[:, cut:], axis=1, keepdims=True))
        return (hi[:, :cut], lo[:, :cut]), (hi[:, cut:], lo[:, cut:]), sums

    def sums_after(self, xv, carry):
        (h0, l0), (h1, l1), (s0, s1) = self._parts(xv)
        first = _dot(h0, self.later[0]) + _dot(l0, self.later[0]) + (s1 + carry)
        last = _dot(h1, self.later[1]) + _dot(l1, self.later[1]) + carry
        return jnp.concatenate([first, last], axis=1), s0 + s1

    def sums_before(self, xv, carry):
        (h0, l0), (h1, l1), (s0, s1) = self._parts(xv)
        first = _dot(h0, self.earlier[0]) + _dot(l0, self.earlier[0]) + carry
        last = _dot(h1, self.earlier[1]) + _dot(l1, self.earlier[1]) + (s0 + carry)
        return jnp.concatenate([first, last], axis=1), s0 + s1


class _HeadPair:
    def __init__(self):
        lanes = 2 * HEAD_DIM
        lane = lax.broadcasted_iota(jnp.int32, (1, lanes), 1)
        self.masks = [lane // HEAD_DIM == h for h in (0, 1)]
        i = lax.broadcasted_iota(jnp.int32, (lanes, lanes), 0) // HEAD_DIM
        j = lax.broadcasted_iota(jnp.int32, (lanes, lanes), 1) // HEAD_DIM
        self.same_head = jnp.where(i == j, 1.0, 0.0).astype(BF16)

    def only(self, h, xv):
        return jnp.where(self.masks[h], xv, jnp.zeros_like(xv))

    def merge(self, per_head):
        return jnp.where(self.masks[0], per_head[0], per_head[1])

    def mean(self, xv):
        hi = xv.astype(BF16)
        lo = (xv - hi.astype(F32)).astype(BF16)
        return (_dot(hi, self.same_head) + _dot(lo, self.same_head)) * (1.0 / HEAD_DIM)

    def rms_r(self, xv):
        return lax.rsqrt(self.mean(xv * xv) + EPS)

    def rms_bwd(self, xv, r, nw, dh):
        t = dh * nw
        dx = r * t - xv * (r * r * r * self.mean(t * xv))
        dn = jnp.sum(dh * xv * r, axis=0, keepdims=True)
        return dx, dn[:, :HEAD_DIM] + dn[:, HEAD_DIM:]


def _attn_fwd(proj, qg, kg, rider=None):
    s = proj.shape[0]
    blk, win, dh = ATT_BLOCK, ATT_WINDOW, HEAD_DIM
    nq = s // blk
    scale = 1.0 / math.sqrt(dh)
    heads = (0, 1)
    blocks = (0, 1)
    assert s >= win and s % (blk * len(blocks)) == 0

    def body(*refs):
        (q_ref, k_ref, v_ref, qg_ref, kg_ref), (o_ref,), (qn, kn, vb), copies = _split_refs(refs, 5, 1, rider)
        finish = _ride(copies, pl.program_id(0) == 0, pl.program_id(0) == N_HEADS // 2 - 1)
        wd, hp = _Window(), _HeadPair()
        qv = q_ref[...]
        qn[...] = (qv * hp.rms_r(qv) * qg_ref[...] * scale).astype(BF16)
        kv = k_ref[...]
        kn[...] = (kv * hp.rms_r(kv) * kg_ref[...]).astype(BF16)
        vb[...] = v_ref[...].astype(BF16)

        def q_step(pair_i, _):
            qis = [2 * pair_i + b for b in blocks]
            chains = [(b, h) for b in blocks for h in heads]
            qoffs = [pl.multiple_of(qi * blk, blk) for qi in qis]
            qtiles = [qn[pl.ds(qoff, blk), :] for qoff in qoffs]
            qts = [hp.only(h, qtiles[b]) for b, h in chains]

            def more(carry):
                g, live = carry[:2]
                return jnp.logical_and((qis[-1] + 1) * blk - g * win > 0, live > 0)

            def window(carry):
                g, _, accs, runs = carry
                places = [wd.place(qi, g) for qi in qis]
                kts = [kn[pl.ds(start, win), :] for start, _ in places]
                zs = [_dot_nt(qts[c], kts[b]) for c, (b, h) in enumerate(chains)]
                logs = [_sb_logs(zs[c], places[b][1]) for c, (b, h) in enumerate(chains)]
                sums = [wd.sums_after(logs[c][1], runs[c]) for c in range(len(chains))]
                wgts = [jnp.where(places[b][1], jnp.exp(logs[c][0] + sums[c][0]), 0.0).astype(BF16)
                        for c, (b, h) in enumerate(chains)]
                vts = [vb[pl.ds(start, win), :] for start, _ in places]
                accs = tuple(accs[c] + _dot(wgts[c], vts[b]) for c, (b, h) in enumerate(chains))
                runs = tuple(runs[c] + sums[c][1] for c in range(len(chains)))
                top = functools.reduce(jnp.maximum, [jnp.max(r) for r in runs])
                return g + 1, (top > EXP_ZERO).astype(jnp.int32), accs, runs

            zero = lambda cols: tuple(jnp.zeros((blk, cols), F32) for _ in chains)
            _, _, accs, _ = lax.while_loop(more, window, (jnp.int32(0), jnp.int32(1), zero(2 * dh), zero(1)))
            for b in blocks:
                o_ref[pl.ds(qoffs[b], blk), :] = hp.merge([accs[2 * b + h] for h in heads])
            return 0

        lax.fori_loop(0, nq // len(blocks), q_step, 0)
        finish()

    pair = lambda group: pl.BlockSpec((s, 2 * dh), lambda p: (0, group * (D_ATT // (2 * dh)) + p))
    vec = pl.BlockSpec((1, 2 * dh), lambda p: (0, 0))
    return _call(
        body, "attn_fwd", (N_HEADS // 2,), [pair(2), pair(3), pair(4), vec, vec], [pair(0)],
        [jax.ShapeDtypeStruct((s, D_ATT), F32)], [proj, proj, proj, jnp.tile(qg, (1, 2)), jnp.tile(kg, (1, 2))],
        scratch=[pltpu.VMEM((s, 2 * dh), BF16)] * 3, rider=rider)


def _attn_bwd(proj, dya, qg, kg, rider=None):
    s = proj.shape[0]
    blk, win, dh = ATT_BLOCK, ATT_WINDOW, HEAD_DIM
    nq = s // blk
    max_windows = -(-s // win) + 1
    scale = 1.0 / math.sqrt(dh)
    steps = N_HEADS // 2
    heads = (0, 1)
    blocks = (0, 1)
    assert s >= win and s % (blk * len(blocks)) == 0

    def body(*refs):
        ins, outs, scratch, copies = _split_refs(refs, 6, 5, rider)
        q_ref, k_ref, v_ref, do_ref, qg_ref, kg_ref = ins
        dq_ref, dk_ref, dv_ref, dqg_ref, dkg_ref = outs
        qn, kn, vb, dob, runs_ref, dqn, dkn, dvn = scratch
        finish = _ride(copies, pl.program_id(0) == 0, pl.program_id(0) == steps - 1)
        wd, hp = _Window(), _HeadPair()

        @pl.when(pl.program_id(0) == 0)
        def _():
            dqg_ref[...] = jnp.zeros_like(dqg_ref)
            dkg_ref[...] = jnp.zeros_like(dkg_ref)

        qv = q_ref[...]
        qn[...] = (qv * hp.rms_r(qv) * qg_ref[...] * scale).astype(BF16)
        kv = k_ref[...]
        kn[...] = (kv * hp.rms_r(kv) * kg_ref[...]).astype(BF16)
        vb[...] = v_ref[...].astype(BF16)
        dob[...] = do_ref[...].astype(BF16)
        dkn[...] = jnp.zeros_like(dkn)
        dvn[...] = jnp.zeros_like(dvn)

        def q_step(pair_i, _):
            qis = [2 * pair_i + b for b in blocks]
            chains = [(b, h) for b in blocks for h in heads]
            ids = range(len(chains))
            qoffs = [pl.multiple_of(qi * blk, blk) for qi in qis]
            qts = [hp.only(h, qn[pl.ds(qoffs[b], blk), :]) for b, h in chains]
            dots = [hp.only(h, dob[pl.ds(qoffs[b], blk), :]) for b, h in chains]

            zero = lambda cols: tuple(jnp.zeros((blk, cols), F32) for _ in chains)

            def logs_of(g):
                places = [wd.place(qi, g) for qi in qis]
                kts = [kn[pl.ds(start, win), :] for start, _ in places]
                return [_sb_logs(_dot_nt(qts[c], kts[b]), places[b][1]) for c, (b, h) in enumerate(chains)]

            def row_sums(logs):
                return tuple(jnp.sum(logs[c][1], axis=1, keepdims=True) for c in ids)

            def still_live(runs):
                return functools.reduce(jnp.maximum, [jnp.max(r) for r in runs]) > EXP_ZERO

            def window_grads(g, logs, runs, esums):
                places = [wd.place(qi, g) for qi in qis]
                kts = [kn[pl.ds(start, win), :] for start, _ in places]
                vts = [vb[pl.ds(start, win), :] for start, _ in places]
                dws = [_dot_nt(dots[c], vts[b]) for c, (b, h) in enumerate(chains)]
                tails = [wd.sums_after(logs[c][1], runs[c])[0] for c in ids]
                wgts = [jnp.where(places[b][1], jnp.exp(logs[c][0] + tails[c]), 0.0) for c, (b, h) in enumerate(chains)]
                es = [dws[c] * wgts[c] for c in ids]
                befores = [wd.sums_before(es[c], esums[c]) for c in ids]
                dzbs = []
                for c, (b, h) in enumerate(chains):
                    beta = jnp.exp(logs[c][0])
                    dz = jnp.where(places[b][1], es[c] * (1.0 - beta) - befores[c][0] * beta, 0.0)
                    dzbs.append(dz.astype(BF16))
                for b in blocks:
                    rows = pl.ds(places[b][0], win)
                    dkn[rows, :] += _dot_tn(dzbs[2 * b], qts[2 * b]) + _dot_tn(dzbs[2 * b + 1], qts[2 * b + 1])
                    dvn[rows, :] += (_dot_tn(wgts[2 * b].astype(BF16), dots[2 * b])
                                     + _dot_tn(wgts[2 * b + 1].astype(BF16), dots[2 * b + 1]))
                return (tuple(_dot(dzbs[c], kts[b]) for c, (b, h) in enumerate(chains)),
                        tuple(befores[c][1] for c in ids))

            logs0 = logs_of(0)
            runs1 = row_sums(logs0)

            def one_window():
                return window_grads(0, logs0, zero(1), zero(1))[0]

            def all_windows():
                def more(carry):
                    g, live = carry[:2]
                    return jnp.logical_and((qis[-1] + 1) * blk - g * win > 0, live > 0)

                def run_window(carry):
                    g, _, runs = carry
                    for c in ids:
                        runs_ref[c, g] = runs[c]
                    sums = row_sums(logs_of(g))
                    runs = tuple(runs[c] + sums[c] for c in ids)
                    return g + 1, still_live(runs).astype(jnp.int32), runs

                for c in ids:
                    runs_ref[c, 0] = jnp.zeros((blk, 1), F32)
                windows, _, _ = lax.while_loop(more, run_window, (jnp.int32(1), jnp.int32(1), runs1))

                def k_window(gg, carry):
                    dq_accs, esums = carry
                    g = windows - 1 - gg
                    parts, totals = window_grads(g, logs_of(g), [runs_ref[c, g] for c in ids], esums)
                    return (tuple(dq_accs[c] + parts[c] for c in ids), tuple(esums[c] + totals[c] for c in ids))

                return lax.fori_loop(0, windows, k_window, (zero(2 * dh), zero(1)))[0]

            earlier_keys = (qis[-1] + 1) * blk - win > 0
            dq_accs = lax.cond(jnp.logical_and(earlier_keys, still_live(runs1)), all_windows, one_window)
            for b in blocks:
                dqn[pl.ds(qoffs[b], blk), :] = hp.merge([dq_accs[2 * b + h] for h in heads])
            return 0

        lax.fori_loop(0, nq // len(blocks), q_step, 0)

        dq, dqg = hp.rms_bwd(qv, hp.rms_r(qv), qg_ref[...] * scale, dqn[...])
        dq_ref[...] = dq.astype(BF16)
        dqg_ref[...] += dqg * scale
        dk, dkg = hp.rms_bwd(kv, hp.rms_r(kv), kg_ref[...], dkn[...])
        dk_ref[...] = dk.astype(BF16)
        dkg_ref[...] += dkg
        dv_ref[...] = dvn[...].astype(BF16)
        finish()

    pair = lambda group: pl.BlockSpec((s, 2 * dh), lambda p: (0, group * (D_ATT // (2 * dh)) + p))
    vec2 = pl.BlockSpec((1, 2 * dh), lambda p: (0, 0))
    vec = pl.BlockSpec((1, dh), lambda p: (0, 0))
    return _call(
        body, "attn_bwd", (steps,), [pair(2), pair(3), pair(4), pair(0), vec2, vec2],
        [pair(0), pair(0), pair(0), vec, vec],
        [jax.ShapeDtypeStruct((s, D_ATT), BF16)] * 3 + [jax.ShapeDtypeStruct((1, dh), F32)] * 2,
        [proj, proj, proj, dya, jnp.tile(qg, (1, 2)), jnp.tile(kg, (1, 2))],
        scratch=[pltpu.VMEM((s, 2 * dh), BF16)] * 4 + [pltpu.VMEM((4, max_windows, blk, 1), F32)]
        + [pltpu.VMEM((s, 2 * dh), F32)] * 3, rider=rider)


def _block_diag(w):
    n, c, d = w.shape
    return jnp.einsum("ncd,nm->ncmd", w, jnp.eye(n, dtype=w.dtype)).reshape(n * c, n * d)


def _diag_blocks(full, n):
    c = full.shape[0] // n
    return jnp.stack([full[i * c:(i + 1) * c, i * c:(i + 1) * c] for i in range(n)])


FFN1 = ["ffn1_w_gate", "ffn1_w_up", "ffn1_w_down"]
FFN2 = ["ffn2_w_gate", "ffn2_w_up", "ffn2_w_down"]


def _pair_sums(gb, names, where):
    theirs = _pair_exchange([gb[n] for n in names], "pair_exchange_" + names[0])
    pair, own = _pair_sum([gb[n] for n in names], theirs, where, "pair_sum_" + names[0])
    return _chip_rider(pair, own)


def _local_step(x, tgt, stacks, conv_stack, small, where):
    gate_up, down = FFN1[:2], FFN1[2:]
    big = dict(zip(gate_up, _gather_weights([stacks[n] for n in gate_up], [])))
    wa = _block_diag(small["rg_w_a"]).astype(BF16)
    wx = _block_diag(small["rg_w_x"]).astype(BF16)

    whole = lambda names: [big[n].reshape(-1, D_MODEL) for n in names]
    soon = down + ["w_in"]
    g1, u1, hb1, ab1, *landed = _ffn_up(x, small["ffn1_norm"], *whole(gate_up),
                                        rider=_gather_rider([stacks[n] for n in soon], [conv_stack]))
    big.update(zip(soon, landed))
    x1 = _ffn_down(x, ab1, *whole(down))
    conv_w = jnp.transpose(landed[-1], (1, 0, 2)).reshape(CONV_W, D_RNN)
    rg = (conv_w, small["conv_b"], wa, small["rg_b_a"], wx, small["rg_b_x"], small["rg_lambda"])
    riding = lambda names: _gather_rider([stacks[n] for n in names], [])
    proj, hb2, big["ffn2_w_gate"] = _mix_pre(x1, small["mix_norm"], big["w_in"], riding(["ffn2_w_gate"]))
    yr, hseq, big["ffn2_w_up"] = _rglru_fwd(proj, *rg, riding(["ffn2_w_up"]))
    ya, big["ffn2_w_down"], big["w_out"] = _attn_fwd(proj, small["q_norm"], small["k_norm"],
                                                     riding(["ffn2_w_down", "w_out"]))
    wout = big["w_out"].reshape(D_MODEL, D_MODEL)
    x2 = _mix_post(x1, yr, ya, small["rnn_out_norm"], small["attn_out_norm"], wout)
    dx3, g2, u2, hb3, ab3, loss = _ffn_fwd_loss(x2, small["ffn2_norm"], *whole(FFN2), tgt)

    gb, gs, slots = {}, {}, {}
    dx2, dg2, du2, dyb2, gs["ffn2_norm"] = _ffn_bwd_act(x2, small["ffn2_norm"], dx3, g2, u2, *whole(FFN2), "ffn2_bwd")
    gb["ffn2_w_gate"] = _ffn_wgrad(dg2, hb3, 1.0, "wgrad_gate_ffn2")
    gb["ffn2_w_up"] = _ffn_wgrad(du2, hb3, 1.0, "wgrad_up_ffn2")
    gb["ffn2_w_down"] = _ffn_wgrad(ab3, dyb2, 0.5, "wgrad_down_ffn2")
    dyr, dya, ycat, dxb2, gs["rnn_out_norm"], gs["attn_out_norm"] = _mix_post_bwd(
        dx2, yr, ya, small["rnn_out_norm"], small["attn_out_norm"], wout)
    gb["w_out"] = _wgrad_whole(ycat, dxb2, False, "wgrad_out")
    early = FFN2 + ["w_out"]
    dq, dk, dv, gs["q_norm"], gs["k_norm"], *done = _attn_bwd(
        proj, dya, small["q_norm"], small["k_norm"], _pair_sums(gb, early, where))
    slots.update(zip(early, done))
    dxr, dgate, gs["conv_w"], gs["conv_b"], dwa, gs["rg_b_a"], dwx, gs["rg_b_x"], gs["rg_lambda"] = _rglru_bwd(
        proj, hseq, dyr, *rg)
    gs["rg_w_a"] = _diag_blocks(dwa, RNN_BLOCKS)
    gs["rg_w_x"] = _diag_blocks(dwx, RNN_BLOCKS)
    dpb = jnp.concatenate([dxr, dgate, dq, dk, dv], axis=1)
    dx1, gs["mix_norm"] = _mix_pre_bwd(x1, small["mix_norm"], dx2, dpb, big["w_in"])
    dx0, dg1, du1, dyb1, gs["ffn1_norm"] = _ffn_bwd_act(x, small["ffn1_norm"], dx1, g1, u1, *whole(FFN1), "ffn1_bwd")

    mine = _place_shard(_pack([gs[n] for n in SMALL] + [loss[:, :1]]), where, F32, "place_small_grads",
                        by_device=True)
    gb["ffn1_w_gate"], everyone = _ffn_wgrad(dg1, hb1, 1.0, "wgrad_gate_ffn1", _small_rider(mine))
    gb["ffn1_w_up"], slots["ffn1_w_gate"] = _ffn_wgrad(
        du1, hb1, 1.0, "wgrad_up_ffn1", _pair_sums(gb, ["ffn1_w_gate"], where))
    gb["ffn1_w_down"], slots["ffn1_w_up"] = _ffn_wgrad(
        ab1, dyb1, 0.5, "wgrad_down_ffn1", _pair_sums(gb, ["ffn1_w_up"], where))
    gb["w_in"], slots["ffn1_w_down"] = _wgrad_whole(
        hb2, dpb, True, "wgrad_in", _pair_sums(gb, ["ffn1_w_down"], where))
    last = _pair_sums(gb, ["w_in"], where)
    slots["w_in"], = _chip_exchange(last.plain, last.inplace)
    return dx0, slots, gs, everyone


ANY = pl.BlockSpec(memory_space=pl.ANY)


def _place():
    x, y, c = lax.axis_index("x"), lax.axis_index("y"), lax.axis_index("c")
    other_chips = [(1 - x, y), (x, 1 - y), (1 - x, 1 - y)]
    return x, y, c, 2 * x + y, other_chips


def _remote(src, dst, send_sem, recv_sem, to):
    return pltpu.make_async_remote_copy(src_ref=src, dst_ref=dst, send_sem=send_sem, recv_sem=recv_sem,
                                        device_id=to, device_id_type=MESH)


def _copy_plan(pairs):
    sends = [functools.partial(_remote, *a) for a, _ in pairs]
    arrivals = [functools.partial(_remote, *b) for _, b in pairs]
    return sends, arrivals


class _Rider:
    def __init__(self, plan, plain, inplace, n_copies=None, relay=None, n_relay=0):
        self.plan, self.plain, self.inplace = plan, list(plain), list(inplace)
        self.n_copies = n_copies or 3 * len(self.inplace)
        self.relay, self.n_relay = relay, n_relay

    def operands(self):
        return self.plain + self.inplace

    def out_shape(self):
        return [jax.ShapeDtypeStruct(a.shape, a.dtype) for a in self.inplace]

    def aliases(self, inputs_before, outputs_before):
        return {inputs_before + len(self.plain) + k: outputs_before + k for k in range(len(self.inplace))}

    def scratch(self):
        relay = [pltpu.SemaphoreType.DMA((self.n_relay,))] * 2 if self.relay else []
        return [pltpu.SemaphoreType.DMA((self.n_copies,))] * 2 + relay


def _split_refs(refs, n_in, n_out, rider):
    if rider is None:
        return refs[:n_in], refs[n_in:n_in + n_out], refs[n_in + n_out:], None
    r_in, r_out = len(rider.operands()), len(rider.inplace)
    outs_at = n_in + r_in
    n_sems = len(rider.scratch())
    rest = refs[outs_at + n_out + r_out:]
    sems = rest[len(rest) - n_sems:]
    filled = refs[outs_at + n_out:outs_at + n_out + r_out]
    copies = functools.partial(rider.plan, refs[n_in:n_in + len(rider.plain)], filled, *sems[:2])
    relay = functools.partial(rider.relay, filled, *sems[2:]) if rider.relay else None
    return refs[:n_in], refs[outs_at:outs_at + n_out], rest[:len(rest) - n_sems], (copies, relay)


def _ride(copies, first, last, middle=None):
    if copies is None:
        return lambda: None
    copies, relay = copies

    @pl.when(first)
    def _():
        _start(copies()[0])

    def start_relay():
        for make in copies()[1]:
            make().wait_recv()
        _start(relay()[0])

    if relay is not None and middle is not None:
        pl.when(middle)(start_relay)

    def finish():
        @pl.when(last)
        def _():
            if relay is None:
                _finish(*copies())
            else:
                if middle is None:
                    start_relay()
                _finish(copies()[0] + relay()[0], relay()[1])

    return finish


def _gather_rider(split, whole):
    n_split = len(split)
    return _Rider(lambda plain, stacks, ss, rs: _gather_ici(stacks, n_split, ss, rs), [], list(split) + list(whole),
                  relay=lambda stacks, ss, rs: _gather_d2d(stacks[:n_split], ss, rs), n_relay=3 * n_split)


def _chip_rider(sums, slots):
    return _Rider(_chip_copies, sums, slots)


def _start(makers):
    for make in makers:
        make().start()


def _finish(sends, arrivals):
    for make in arrivals:
        make().wait_recv()
    for make in sends:
        make().wait_send()


def _half(rows, c):
    return pl.ds(pl.multiple_of(c * rows, BF16_ROWS), rows)


def _gather_weights(split, whole):
    arrs = list(split) + list(whole)
    n, ns = len(arrs), len(split)

    def body(*refs):
        outs = refs[n:2 * n]
        send_sems, recv_sems, fsend_sems, frecv_sems = refs[2 * n:]
        sends, arrivals = _gather_ici(outs, ns, send_sems, recv_sems)
        passes, passed = _gather_d2d(outs[:ns], fsend_sems, frecv_sems)
        _start(sends)
        for k, make in enumerate(arrivals):
            make().wait_recv()
            if k < 3 * ns:
                passes[k]().start()
        _finish(sends + passes, passed)

    return pl.pallas_call(
        body, name="gather_weights",
        in_specs=[ANY] * n, out_specs=[ANY] * n,
        out_shape=[jax.ShapeDtypeStruct(a.shape, a.dtype) for a in arrs],
        input_output_aliases={i: i for i in range(n)},
        scratch_shapes=[pltpu.SemaphoreType.DMA((3 * n,)), pltpu.SemaphoreType.DMA((3 * n,)),
                        pltpu.SemaphoreType.DMA((3 * ns,)), pltpu.SemaphoreType.DMA((3 * ns,))],
    )(*arrs)


def _gather_ici(stacks, n_split, send_sems, recv_sems):
    x, y, c, me, chips = _place()

    def region(i, chip):
        if i < n_split:
            return stacks[i].at[chip, _half(stacks[i].shape[1] // 2, c)]
        return stacks[i].at[chip]

    pairs = []
    for i in range(len(stacks)):
        for p, (cx, cy) in enumerate(chips):
            k = 3 * i + p
            mine, got = region(i, me), region(i, 2 * cx + cy)
            sems, to = (send_sems.at[k], recv_sems.at[k]), (cx, cy, c)
            pairs.append(((mine, mine, *sems, to), (got, got, *sems, to)))
    return _copy_plan(pairs)


def _gather_d2d(stacks, send_sems, recv_sems):
    x, y, c, _, chips = _place()
    sibling = (x, y, 1 - c)
    pairs = []
    for i, stack in enumerate(stacks):
        rows = stack.shape[1] // 2
        for p, (cx, cy) in enumerate(chips):
            k = 3 * i + p
            got, theirs = stack.at[2 * cx + cy, _half(rows, c)], stack.at[2 * cx + cy, _half(rows, 1 - c)]
            sems = (send_sems.at[k], recv_sems.at[k])
            pairs.append(((got, got, *sems, sibling), (theirs, theirs, *sems, sibling)))
    return _copy_plan(pairs)


def _pair_exchange(grads, name):
    n = len(grads)

    def body(*refs):
        ins, theirs = refs[:n], refs[n:2 * n]
        send_sems, recv_sems = refs[2 * n:]
        x, y, c, _, _ = _place()
        sibling = (x, y, 1 - c)
        sends = [_remote(ins[k].at[:, _half(grads[k].shape[1] // 2, 1 - c)], theirs[k],
                         send_sems.at[k], recv_sems.at[k], sibling) for k in range(n)]
        for cp in sends:
            cp.start()
        for k in range(n):
            _remote(theirs[k], theirs[k], send_sems.at[k], recv_sems.at[k], sibling).wait_recv()
        for cp in sends:
            cp.wait_send()

    return pl.pallas_call(
        body, name=name,
        in_specs=[ANY] * n, out_specs=[ANY] * n,
        out_shape=[jax.ShapeDtypeStruct((g.shape[0], g.shape[1] // 2, g.shape[2]), g.dtype) for g in grads],
        scratch_shapes=[pltpu.SemaphoreType.DMA((n,))] * 2,
    )(*grads)


def _chip_exchange(sums, slots):
    n = len(sums)

    def body(*refs):
        sends, arrivals = _chip_copies(refs[:n], refs[2 * n:3 * n], *refs[3 * n:])
        _start(sends)
        _finish(sends, arrivals)

    return pl.pallas_call(
        body, name="grad_chip_exchange",
        in_specs=[ANY] * (2 * n), out_specs=[ANY] * n,
        out_shape=[jax.ShapeDtypeStruct(a.shape, a.dtype) for a in slots],
        input_output_aliases={n + k: k for k in range(n)},
        scratch_shapes=[pltpu.SemaphoreType.DMA((3 * n,)), pltpu.SemaphoreType.DMA((3 * n,))],
    )(*sums, *slots)


def _chip_copies(sums, slots, send_sems, recv_sems):
    x, y, c, me, chips = _place()
    pairs = []
    for k in range(len(sums)):
        for p, (cx, cy) in enumerate(chips):
            j = 3 * k + p
            got = slots[k].at[2 * cx + cy]
            sems, to = (send_sems.at[j], recv_sems.at[j]), (cx, cy, c)
            pairs.append(((sums[k].at[2 * cx + cy], slots[k].at[me], *sems, to), (got, got, *sems, to)))
    return _copy_plan(pairs)


def _half_swap(halves):
    n = len(halves)

    def body(*refs):
        outs = refs[n:2 * n]
        send_sems, recv_sems = refs[2 * n:]
        x, y, c, _, _ = _place()
        sibling = (x, y, 1 - c)
        sends = [_remote(outs[k].at[c], outs[k].at[c], send_sems.at[k], recv_sems.at[k], sibling) for k in range(n)]
        for cp in sends:
            cp.start()
        for k in range(n):
            got = outs[k].at[1 - c]
            _remote(got, got, send_sems.at[k], recv_sems.at[k], sibling).wait_recv()
        for cp in sends:
            cp.wait_send()

    return pl.pallas_call(
        body, name="grad_half_swap",
        in_specs=[ANY] * n, out_specs=[ANY] * n,
        out_shape=[jax.ShapeDtypeStruct(a.shape, a.dtype) for a in halves],
        input_output_aliases={k: k for k in range(n)},
        scratch_shapes=[pltpu.SemaphoreType.DMA((n,))] * 2,
    )(*halves)


def _small_rider(stack):
    n_dev = 2 * N_CHIPS

    def plan(_, stacks, send_sems, recv_sems):
        x, y, c, _, _ = _place()
        mine = stacks[0].at[4 * x + 2 * y + c]
        pairs = []
        for k in range(1, n_dev):
            px, py, pc = x ^ ((k >> 2) & 1), y ^ ((k >> 1) & 1), c ^ (k & 1)
            got = stacks[0].at[4 * px + 2 * py + pc]
            sems = (send_sems.at[k - 1], recv_sems.at[k - 1])
            pairs.append(((mine, mine, *sems, (px, py, pc)), (got, got, *sems, (px, py, pc))))
        return _copy_plan(pairs)

    return _Rider(plan, [], [stack], n_dev - 1)


def _row_tile(r):
    return r // 4 if r >= 256 and (r // 4) % BF16_ROWS == 0 else r


def _prefetch_call(body, name, grid, in_specs, out_specs, out_shape):
    spec = pltpu.PrefetchScalarGridSpec(num_scalar_prefetch=1, grid=grid, in_specs=in_specs, out_specs=out_specs)
    return pl.pallas_call(body, name=name, grid_spec=spec, out_shape=out_shape,
                          compiler_params=_params(("arbitrary",) * len(grid)))


def _place_shard(w2d, where, dtype, name, by_device=False):
    r, c = w2d.shape
    tr = _row_tile(r)
    slots = 2 * N_CHIPS if by_device else N_CHIPS
    slot = (lambda s: 2 * s[1] + s[0]) if by_device else (lambda s: s[1])

    def body(where_ref, w_ref, out_ref):
        out_ref[...] = w_ref[...].astype(dtype)

    return _prefetch_call(
        body, name, (r // tr,), [pl.BlockSpec((tr, c), lambda i, s: (i, 0))],
        pl.BlockSpec((None, tr, c), lambda i, s: (slot(s), i, 0)),
        jax.ShapeDtypeStruct((slots, r, c), dtype))(where, w2d)


def _place_shards(w2ds, where, name):
    n = len(w2ds)
    steps = N_CHIPS
    assert all(w.shape[0] % (BF16_ROWS * steps) == 0 for w in w2ds)

    def body(where_ref, *refs):
        for k in range(n):
            refs[n + k][...] = refs[k][...].astype(BF16)

    tile = lambda w: (w.shape[0] // steps, w.shape[1])
    return _prefetch_call(
        body, name, (steps,), [pl.BlockSpec(tile(w), lambda i, s: (i, 0)) for w in w2ds],
        [pl.BlockSpec((None,) + tile(w), lambda i, s: (s[1], i, 0)) for w in w2ds],
        [jax.ShapeDtypeStruct((N_CHIPS,) + w.shape, BF16) for w in w2ds])(where, *w2ds)


def _pair_sum(fulls, theirs, where, name):
    n = len(fulls)

    def body(where_ref, *refs):
        for k in range(n):
            a_ref, b_ref, out_ref, own_ref = refs[k], refs[n + k], refs[2 * n + k], refs[3 * n + k]
            total = (a_ref[...].astype(F32) + b_ref[...].astype(F32)).astype(BF16)
            out_ref[...] = total

            @pl.when(pl.program_id(0) == where_ref[1])
            def _():
                own_ref[...] = total

    half = lambda t: pl.BlockSpec((None,) + t.shape[1:], lambda j, s: (j, s[0], 0))
    blk = lambda t: pl.BlockSpec((None,) + t.shape[1:], lambda j, s: (j, 0, 0))
    own = lambda t: pl.BlockSpec((None,) + t.shape[1:], lambda j, s: (s[1], 0, 0))
    shapes = [jax.ShapeDtypeStruct(t.shape, BF16) for t in theirs]
    outs = _prefetch_call(
        body, name, (N_CHIPS,), [half(t) for t in theirs] + [blk(t) for t in theirs],
        [blk(t) for t in theirs] + [own(t) for t in theirs], shapes + shapes)(where, *fulls, *theirs)
    return outs[:n], outs[n:]


def _chip_sum(slots, where, name):
    n = len(slots)
    steps = 2
    assert all(a.shape[1] % (BF16_ROWS * steps) == 0 for a in slots)

    def body(where_ref, *refs):
        for k in range(n):
            a_ref, out_ref = refs[k], refs[n + k]
            total = a_ref[0].astype(F32)
            for j in range(1, a_ref.shape[0]):
                total = total + a_ref[j].astype(F32)
            out_ref[...] = total

    tile = lambda a: (a.shape[1] // steps, a.shape[2])
    return _prefetch_call(
        body, name, (steps,), [pl.BlockSpec((a.shape[0],) + tile(a), lambda i, s: (0, i, 0)) for a in slots],
        [pl.BlockSpec((None,) + tile(a), lambda i, s: (s[0], i, 0)) for a in slots],
        [jax.ShapeDtypeStruct((2,) + a.shape[1:], F32) for a in slots])(where, *slots)


def _slot_sum(a, name):
    nb, r, c = a.shape
    tr = _row_tile(r)

    def body(a_ref, out_ref):
        total = a_ref[0].astype(F32)
        for j in range(1, nb):
            total = total + a_ref[j].astype(F32)
        out_ref[...] = total

    return pl.pallas_call(
        body, name=name, grid=(r // tr,),
        in_specs=[pl.BlockSpec((nb, tr, c), lambda i: (0, i, 0))],
        out_specs=pl.BlockSpec((tr, c), lambda i: (i, 0)),
        out_shape=jax.ShapeDtypeStruct((r, c), F32), compiler_params=_params(("arbitrary",)),
    )(a)


def _adamw(ws, gs, ms, vs, name, steps=1):
    n = len(ws)
    c1 = 1.0 - ADAM_B1 ** ADAM_STEP
    c2 = 1.0 - ADAM_B2 ** ADAM_STEP
    assert all(w.shape[0] % steps == 0 and (steps == 1 or w.shape[0] // steps % 8 == 0) for w in ws)

    def body(*refs):
        for k in range(n):
            w_ref, g_ref, m_ref, v_ref = (refs[j * n + k] for j in range(4))
            g_out, d_ref, m2_ref, v2_ref = (refs[(4 + j) * n + k] for j in range(4))
            gv = g_ref[...]
            g_out[...] = gv
            m2 = ADAM_B1 * m_ref[...] + (1.0 - ADAM_B1) * gv
            v2 = ADAM_B2 * v_ref[...] + (1.0 - ADAM_B2) * (gv * gv)
            m2_ref[...] = m2
            v2_ref[...] = v2
            d_ref[...] = -ADAM_LR * ((m2 / c1) / (jnp.sqrt(v2 / c2) + ADAM_EPS) + ADAM_WD * w_ref[...])

    blks = [pl.BlockSpec((w.shape[0] // steps, w.shape[1]), lambda i: (i, 0)) for w in ws]
    shapes = [jax.ShapeDtypeStruct(w.shape, F32) for w in ws]
    outs = pl.pallas_call(
        body, name=name, grid=(steps,), in_specs=blks * 4, out_specs=blks * 4, out_shape=shapes * 4,
        compiler_params=_params(("arbitrary",)),
    )(*ws, *gs, *ms, *vs)
    return [outs[j * n:(j + 1) * n] for j in range(4)]


WEIGHTS = ["ffn1_norm", "ffn1_w_gate", "ffn1_w_up", "ffn1_w_down", "mix_norm", "w_in", "conv_w", "conv_b",
           "rg_w_a", "rg_b_a", "rg_w_x", "rg_b_x", "rg_lambda", "q_norm", "k_norm", "rnn_out_norm",
           "attn_out_norm", "w_out", "ffn2_norm", "ffn2_w_gate", "ffn2_w_up", "ffn2_w_down"]
BIG = ["ffn1_w_gate", "ffn1_w_up", "ffn1_w_down", "w_in", "w_out", "ffn2_w_gate", "ffn2_w_up", "ffn2_w_down"]
SMALL = [n for n in WEIGHTS if n not in BIG]
PACK_LANES = 128
PACK_ROW_ALIGN = 8


def _hidden_major(name, a):
    return jnp.transpose(a) if name.endswith(("w_gate", "w_up")) else a


def _pack(parts):
    flat = jnp.concatenate([p.reshape(-1) for p in parts])
    unit = PACK_LANES * PACK_ROW_ALIGN
    padded = -(-flat.shape[0] // unit) * unit
    return jnp.pad(flat, (0, padded - flat.shape[0])).reshape(-1, PACK_LANES)


def _unpack(packed, shapes):
    flat = packed.reshape(-1)
    out, at = [], 0
    for shp in shapes:
        size = math.prod(shp)
        out.append(flat[at:at + size].reshape(shp))
        at += size
    return out


def kernel(x, ffn1_norm, ffn1_w_gate, ffn1_w_up, ffn1_w_down, mix_norm, w_in, conv_w, conv_b, rg_w_a, rg_b_a, rg_w_x, rg_b_x, rg_lambda, q_norm, k_norm, rnn_out_norm, attn_out_norm, w_out, ffn2_norm, ffn2_w_gate, ffn2_w_up, ffn2_w_down, loss_target, m_ffn1_norm, m_ffn1_w_gate, m_ffn1_w_up, m_ffn1_w_down, m_mix_norm, m_w_in, m_conv_w, m_conv_b, m_rg_w_a, m_rg_b_a, m_rg_w_x, m_rg_b_x, m_rg_lambda, m_q_norm, m_k_norm, m_rnn_out_norm, m_attn_out_norm, m_w_out, m_ffn2_norm, m_ffn2_w_gate, m_ffn2_w_up, m_ffn2_w_down, v_ffn1_norm, v_ffn1_w_gate, v_ffn1_w_up, v_ffn1_w_down, v_mix_norm, v_w_in, v_conv_w, v_conv_b, v_rg_w_a, v_rg_b_a, v_rg_w_x, v_rg_b_x, v_rg_lambda, v_q_norm, v_k_norm, v_rnn_out_norm, v_attn_out_norm, v_w_out, v_ffn2_norm, v_ffn2_w_gate, v_ffn2_w_up, v_ffn2_w_down):
    given = dict(locals())
    w = {n: given[n] for n in WEIGHTS}
    m = {n: given["m_" + n] for n in WEIGHTS}
    v = {n: given["v_" + n] for n in WEIGHTS}
    chip = 2 * lax.axis_index("x") + lax.axis_index("y")

    where = jnp.stack([lax.axis_index("c"), chip]).astype(jnp.int32)

    stacks = dict(zip(BIG, _place_shards([_hidden_major(n, w[n][0]) for n in BIG], where, "place_weights")))
    conv_stack = _place_shard(w["conv_w"][0], where, F32, "place_conv_w")
    small = {n: (w[n][0] if w[n].ndim > 2 else w[n]) for n in SMALL if n != "conv_w"}

    grad_x, slots, gs, everyone = _local_step(x[0], loss_target[0], stacks, conv_stack, small, where)

    swapped = _half_swap(_chip_sum([slots[n] for n in BIG], where, "chip_sums"))
    g2s = [t.reshape(t.shape[0] * t.shape[1], t.shape[2]) for t in swapped]
    flat = lambda tree: [_hidden_major(n, tree[n][0]) for n in BIG]
    g2s, d2s, m2s, v2s = _adamw(flat(w), g2s, flat(m), flat(v), "adamw_weights", ADAMW_STEPS)
    grads, deltas, new_m, new_v = {}, {}, {}, {}
    for tree, parts in ((grads, g2s), (deltas, d2s), (new_m, m2s), (new_v, v2s)):
        tree.update({n: _hidden_major(n, a).reshape(w[n].shape) for n, a in zip(BIG, parts)})

    full_shapes = [gs[n].shape for n in SMALL]
    *summed, loss = _unpack(_slot_sum(everyone, "small_grad_sum"), full_shapes + [(1, 1)])
    g_parts = dict(zip(SMALL, summed))
    quarter = D_RNN // N_CHIPS
    g_parts["conv_w"] = lax.dynamic_slice_in_dim(g_parts["conv_w"], chip * quarter, quarter, axis=1)
    local_shapes = [w[n].shape for n in SMALL]
    pk = lambda tree: _pack([tree[n] for n in SMALL])
    (g_s,), (d_s,), (m_s,), (v_s,) = _adamw([pk(w)], [pk(g_parts)], [pk(m)], [pk(v)], "adamw_small")
    for tree, packed in ((grads, g_s), (deltas, d_s), (new_m, m_s), (new_v, v_s)):
        tree.update(zip(SMALL, _unpack(packed, local_shapes)))

    return (loss[0, 0], grad_x.reshape(x.shape), *[grads[n] for n in WEIGHTS], *[deltas[n] for n in WEIGHTS],
            *[new_m[n] for n in WEIGHTS], *[new_v[n] for n in WEIGHTS])
```

```python
import functools
import math

import jax
import jax.numpy as jnp
from jax import lax
from jax.experimental import pallas as pl
from jax.experimental.pallas import tpu as pltpu

F32 = jnp.float32
BF16 = jnp.bfloat16
MESH = pl.DeviceIdType.MESH

D_MODEL = 1024
N_CHIPS = 4
D_RNN = 512
D_ATT = 512
N_HEADS = 8
HEAD_DIM = 64
RNN_BLOCKS = 8
CONV_W = 4
RG_C = 8.0
N_IN = 2 * D_RNN + 3 * D_ATT
EPS = 1e-6
ATT_BLOCK = 128
ATT_WINDOW = 384
ATT_SPLIT = 256
EXP_ZERO = -105.0

ADAM_LR = 0.001
ADAM_B1 = 0.9
ADAM_B2 = 0.999
ADAM_EPS = 1e-08
ADAM_WD = 0.01
ADAM_STEP = 10

V7X_VMEM_LIMIT = 60 * 1024 * 1024
V7X_MXU_WIDTH = 256
TOKEN_TILE = 512
SUBLANES = 8
BF16_ROWS = 16
FFN_TILE = 256
WGRAD_TILE = 2048
WHOLE_TILE = 1024
ADAMW_STEPS = 8

GELU_K0 = math.sqrt(2.0 / math.pi)
GELU_K1 = 0.044715


def _params(sem=None):
    return pltpu.CompilerParams(dimension_semantics=sem, vmem_limit_bytes=V7X_VMEM_LIMIT)


def _dot(a, b):
    return jnp.dot(a, b, preferred_element_type=F32)


def _dot_nt(a, b):
    return lax.dot_general(a, b, (((1,), (1,)), ((), ())), preferred_element_type=F32)


def _dot_tn(a, b):
    return lax.dot_general(a, b, (((0,), (0,)), ((), ())), preferred_element_type=F32)


def _sigmoid(x):
    return 1.0 / (1.0 + jnp.exp(-x))


def _rms_r(xv):
    return lax.rsqrt(jnp.mean(xv * xv, axis=-1, keepdims=True) + EPS)


def _rms_bwd(xv, r, nw, dh):
    t = dh * nw
    dx = r * t - xv * (r * r * r * jnp.mean(t * xv, axis=-1, keepdims=True))
    dn = jnp.sum(dh * xv * r, axis=0, keepdims=True)
    return dx, dn


def _gelu(x):
    t = jnp.tanh(GELU_K0 * (x + GELU_K1 * x * x * x))
    return 0.5 * x * (1.0 + t)


def _gelu_grad(x):
    t = jnp.tanh(GELU_K0 * (x + GELU_K1 * x * x * x))
    return 0.5 * (1.0 + t) + 0.5 * x * (1.0 - t * t) * (GELU_K0 * (1.0 + 3.0 * GELU_K1 * x * x))


def _expm1_neg(x):
    p = 1.0 + x * (1.0 / 6.0)
    for k in (5.0, 4.0, 3.0, 2.0):
        p = 1.0 + x * (1.0 / k) * p
    return jnp.where(x > -0.25, x * p, jnp.exp(x) - 1.0)


def _log_sigmoid(x):
    return jnp.minimum(x, 0.0) - jnp.log(1.0 + jnp.exp(-jnp.abs(x)))


def _tile(s):
    return min(TOKEN_TILE, s)


def _ffn_chunks(f):
    cut = f // 2 // V7X_MXU_WIDTH * V7X_MXU_WIDTH
    return ((0, cut), (cut, f)) if 0 < cut < f else ((0, f),)


def _ffn_fwd_loss(x, nw, wg, wu, wd, tgt):
    s, d = x.shape
    f = wg.shape[0]
    tm = min(FFN_TILE, s)
    ni = s // tm
    assert s % tm == 0

    def body(x_ref, nw_ref, wg_ref, wu_ref, wd_ref, tgt_ref, out_ref, g_ref, u_ref, hb_ref, ab_ref, loss_ref):
        i = pl.program_id(0)
        xv = x_ref[...]
        hb = (xv * _rms_r(xv) * nw_ref[...]).astype(BF16)
        hb_ref[...] = hb
        y = jnp.zeros((tm, d), F32)
        for lo, hi in _ffn_chunks(f):
            g = _dot_nt(hb, wg_ref[lo:hi, :])
            u = _dot_nt(hb, wu_ref[lo:hi, :])
            g_ref[:, lo:hi] = g.astype(BF16)
            u_ref[:, lo:hi] = u.astype(BF16)
            ab = (g * _sigmoid(g) * u).astype(BF16)
            ab_ref[:, lo:hi] = ab
            y = y + _dot(ab, wd_ref[lo:hi, :])
        diff = xv + 0.5 * y - tgt_ref[...]
        out_ref[...] = diff * (1.0 / d)

        @pl.when(i == 0)
        def _():
            loss_ref[...] = jnp.zeros_like(loss_ref)

        loss_ref[...] += jnp.sum(diff * diff) * (0.5 / d)

    row = pl.BlockSpec((tm, d), lambda i: (i, 0))
    weight = pl.BlockSpec((f, d), lambda i: (0, 0), pipeline_mode=pl.Buffered(1))
    blk = pl.BlockSpec((tm, f), lambda i: (i, 0))
    wide = jax.ShapeDtypeStruct((s, f), BF16)
    return _call(body, "ffn_fwd_loss", (ni,),
                 [row, pl.BlockSpec((1, d), lambda i: (0, 0)), weight, weight, weight, row],
                 [row, blk, blk, row, blk, pl.BlockSpec((1, 128), lambda i: (0, 0))],
                 [jax.ShapeDtypeStruct((s, d), F32), wide, wide, jax.ShapeDtypeStruct((s, d), BF16), wide,
                  jax.ShapeDtypeStruct((1, 128), F32)], [x, nw, wg, wu, wd, tgt])


def _ffn_up(x, nw, wg, wu, rider=None):
    s, d = x.shape
    f = wg.shape[0]
    tm = min(FFN_TILE, s)
    ni = s // tm
    assert s % tm == 0

    def body(*refs):
        (x_ref, nw_ref, wg_ref, wu_ref), (g_ref, u_ref, hb_ref, ab_ref), _, copies = _split_refs(refs, 4, 4, rider)
        i = pl.program_id(0)
        finish = _ride(copies, i == 0, i == ni - 1)
        xv = x_ref[...]
        hb = (xv * _rms_r(xv) * nw_ref[...]).astype(BF16)
        hb_ref[...] = hb
        for lo, hi in _ffn_chunks(f):
            g = _dot_nt(hb, wg_ref[lo:hi, :])
            u = _dot_nt(hb, wu_ref[lo:hi, :])
            g_ref[:, lo:hi] = g.astype(BF16)
            u_ref[:, lo:hi] = u.astype(BF16)
            ab_ref[:, lo:hi] = (g * _sigmoid(g) * u).astype(BF16)
        finish()

    row = pl.BlockSpec((tm, d), lambda i: (i, 0))
    weight = pl.BlockSpec((f, d), lambda i: (0, 0), pipeline_mode=pl.Buffered(1))
    blk = pl.BlockSpec((tm, f), lambda i: (i, 0))
    wide = jax.ShapeDtypeStruct((s, f), BF16)
    return _call(body, "ffn_up", (ni,), [row, pl.BlockSpec((1, d), lambda i: (0, 0)), weight, weight],
                 [blk, blk, row, blk], [wide, wide, jax.ShapeDtypeStruct((s, d), BF16), wide], [x, nw, wg, wu],
                 rider=rider)


def _ffn_down(x, ab, wd):
    s, d = x.shape
    f = wd.shape[0]
    tm = _tile(s)
    assert s % tm == 0

    def body(x_ref, ab_ref, wd_ref, out_ref):
        out_ref[...] = x_ref[...] + 0.5 * _dot(ab_ref[...], wd_ref[...])

    row = pl.BlockSpec((tm, d), lambda i: (i, 0))
    return _call(body, "ffn_down", (s // tm,),
                 [row, pl.BlockSpec((tm, f), lambda i: (i, 0)),
                  pl.BlockSpec((f, d), lambda i: (0, 0), pipeline_mode=pl.Buffered(1))],
                 [row], [jax.ShapeDtypeStruct((s, d), F32)], [x, ab, wd])[0]


def _call(body, name, grid, in_specs, out_specs, out_shape, args, scratch=(), rider=None):
    in_specs, out_specs, out_shape, scratch = list(in_specs), list(out_specs), list(out_shape), list(scratch)
    extra, aliases = [], {}
    if rider is not None:
        extra = rider.operands()
        aliases = rider.aliases(len(args), len(out_shape))
        in_specs += [ANY] * len(extra)
        out_specs += [ANY] * len(rider.inplace)
        out_shape += rider.out_shape()
        scratch += rider.scratch()
    return pl.pallas_call(
        body, name=name, grid=grid, in_specs=in_specs, out_specs=out_specs, out_shape=out_shape,
        input_output_aliases=aliases, scratch_shapes=scratch,
        compiler_params=_params(("arbitrary",) * len(grid)),
    )(*args, *extra)


def _ffn_bwd_act(x, nw, dy, g, u, wg, wu, wd, name):
    s, d = x.shape
    f = wg.shape[0]
    tm = min(FFN_TILE, s)
    assert s % tm == 0

    def body(x_ref, nw_ref, dy_ref, g_ref, u_ref, wg_ref, wu_ref, wd_ref,
             dx_ref, dg_ref, du_ref, dyb_ref, dnw_ref):
        dyv = dy_ref[...]
        dyb = dyv.astype(BF16)
        dyb_ref[...] = dyb
        dh = jnp.zeros((tm, d), F32)
        for lo, hi in _ffn_chunks(f):
            da = 0.5 * _dot_nt(dyb, wd_ref[lo:hi, :])
            gv = g_ref[:, lo:hi].astype(F32)
            sg = _sigmoid(gv)
            dub = (da * (gv * sg)).astype(BF16)
            dgb = (da * u_ref[:, lo:hi].astype(F32) * (sg * (1.0 + gv * (1.0 - sg)))).astype(BF16)
            dg_ref[:, lo:hi] = dgb
            du_ref[:, lo:hi] = dub
            dh = dh + _dot(dgb, wg_ref[lo:hi, :]) + _dot(dub, wu_ref[lo:hi, :])
        xv = x_ref[...]
        dx, dn = _rms_bwd(xv, _rms_r(xv), nw_ref[...], dh)
        dx_ref[...] = dyv + dx

        @pl.when(pl.program_id(0) == 0)
        def _():
            dnw_ref[...] = jnp.zeros_like(dnw_ref)

        dnw_ref[...] += dn

    row = pl.BlockSpec((tm, d), lambda i: (i, 0))
    vec = pl.BlockSpec((1, d), lambda i: (0, 0))
    blk = pl.BlockSpec((tm, f), lambda i: (i, 0))
    weight = pl.BlockSpec((f, d), lambda i: (0, 0), pipeline_mode=pl.Buffered(1))
    return _call(
        body, name, (s // tm,), [row, vec, row, blk, blk, weight, weight, weight], [row, blk, blk, row, vec],
        [jax.ShapeDtypeStruct((s, d), F32), jax.ShapeDtypeStruct((s, f), BF16),
         jax.ShapeDtypeStruct((s, f), BF16), jax.ShapeDtypeStruct((s, d), BF16),
         jax.ShapeDtypeStruct((1, d), F32)],
        [x, nw, dy, g, u, wg, wu, wd])


def _wgrad(a, b, a_spec, b_spec, out_rows, out_cols, scale, name, tk, rider=None, per_step=1):
    s = a.shape[-2]
    nk = s // tk
    steps = N_CHIPS // per_step
    assert s % tk == 0

    def body(*refs):
        (a_ref, b_ref), (out_ref,), (acc,), copies = _split_refs(refs, 2, 1, rider)
        j, k = pl.program_id(0), pl.program_id(1)
        finish = _ride(copies, jnp.logical_and(j == 0, k == 0), jnp.logical_and(j == steps - 1, k == nk - 1))

        @pl.when(k == 0)
        def _():
            acc[...] = jnp.zeros_like(acc)

        acc[...] += _dot_tn(a_ref[...], b_ref[...])

        @pl.when(k == nk - 1)
        def _():
            for t in range(per_step):
                out_ref[t] = (acc[t * out_rows:(t + 1) * out_rows, :] * scale).astype(BF16)

        finish()

    outs = _call(
        body, name, (steps, nk), [a_spec(tk), b_spec(tk)],
        [pl.BlockSpec((per_step, out_rows, out_cols), lambda j, k: (j, 0, 0))],
        [jax.ShapeDtypeStruct((N_CHIPS, out_rows, out_cols), BF16)], [a, b],
        scratch=[pltpu.VMEM((per_step * out_rows, out_cols), F32)], rider=rider)
    return outs[0] if rider is None else outs


def _beside(refs):
    return jnp.concatenate([r[...] for r in refs], axis=1) if len(refs) > 1 else refs[0][...]


def _wgrad_whole(a, bs, col_blocks, name, rider=None):
    s, m = a.shape
    n = sum(b.shape[1] for b in bs)
    tk = min(WHOLE_TILE, s)
    nk = s // tk
    assert s % tk == 0
    out_shape = (N_CHIPS, m, n // N_CHIPS) if col_blocks else (N_CHIPS, m // N_CHIPS, n)

    def body(*refs):
        (a_ref, *b_refs), (out_ref,), (acc,), copies = _split_refs(refs, 1 + len(bs), 1, rider)
        k = pl.program_id(0)
        finish = _ride(copies, k == 0, k == nk - 1)

        @pl.when(k == 0)
        def _():
            acc[...] = jnp.zeros_like(acc)

        acc[...] += _dot_tn(a_ref[...], _beside(b_refs))

        @pl.when(k == nk - 1)
        def _():
            for j in range(N_CHIPS):
                if col_blocks:
                    out_ref[j] = acc[:, j * out_shape[2]:(j + 1) * out_shape[2]].astype(BF16)
                else:
                    out_ref[j] = acc[j * out_shape[1]:(j + 1) * out_shape[1], :].astype(BF16)

        finish()

    outs = _call(
        body, name, (nk,),
        [pl.BlockSpec((tk, m), lambda k: (k, 0))] + [pl.BlockSpec((tk, b.shape[1]), lambda k: (k, 0)) for b in bs],
        [pl.BlockSpec(out_shape, lambda k: (0, 0, 0))], [jax.ShapeDtypeStruct(out_shape, BF16)], [a, *bs],
        scratch=[pltpu.VMEM((m, n), F32)], rider=rider)
    return outs[0] if rider is None else outs


def _ffn_wgrad(hidden, shared, scale, name, rider=None):
    s, d = shared.shape
    half = hidden.shape[1] // 2
    return _wgrad(hidden, shared, lambda tk: pl.BlockSpec((tk, half), lambda j, k: (k, j)),
                  lambda tk: pl.BlockSpec((tk, d), lambda j, k: (k, 0)), half // 2, d, scale, name,
                  min(WGRAD_TILE, s), rider, per_step=2)


def _mix_pre(x, nw, win, rider=None):
    s, d = x.shape
    nb, _, cb = win.shape
    tm = _tile(s)
    ni = s // tm
    assert s % tm == 0

    def body(*refs):
        (x_ref, nw_ref, w_ref), (p_ref, hb_ref), _, copies = _split_refs(refs, 3, 2, rider)
        finish = _ride(copies, pl.program_id(0) == 0, pl.program_id(0) == ni - 1)
        xv = x_ref[...]
        hb = (xv * _rms_r(xv) * nw_ref[...]).astype(BF16)
        hb_ref[...] = hb
        for j in range(nb):
            p_ref[:, j * cb:(j + 1) * cb] = _dot(hb, w_ref[j])
        finish()

    row = pl.BlockSpec((tm, d), lambda i: (i, 0))
    return _call(
        body, "mix_pre", (ni,),
        [row, pl.BlockSpec((1, d), lambda i: (0, 0)),
         pl.BlockSpec((nb, d, cb), lambda i: (0, 0, 0), pipeline_mode=pl.Buffered(1))],
        [pl.BlockSpec((tm, nb * cb), lambda i: (i, 0)), row],
        [jax.ShapeDtypeStruct((s, nb * cb), F32), jax.ShapeDtypeStruct((s, d), BF16)], [x, nw, win], rider=rider)


def _mix_pre_bwd(x, nw, dres, dps, win):
    s, d = x.shape
    nb, _, cb = win.shape
    tm = _tile(s)
    assert s % tm == 0 and sum(p.shape[1] for p in dps) == nb * cb

    def body(x_ref, nw_ref, dres_ref, *rest):
        *dp_refs, w_ref, dx_ref, dnw_ref = rest
        dp = _beside(dp_refs)
        dh = jnp.zeros((tm, d), F32)
        for j in range(nb):
            dh = dh + _dot_nt(dp[:, j * cb:(j + 1) * cb], w_ref[j])
        xv = x_ref[...]
        dx, dn = _rms_bwd(xv, _rms_r(xv), nw_ref[...], dh)
        dx_ref[...] = dres_ref[...] + dx

        @pl.when(pl.program_id(0) == 0)
        def _():
            dnw_ref[...] = jnp.zeros_like(dnw_ref)

        dnw_ref[...] += dn

    row = pl.BlockSpec((tm, d), lambda i: (i, 0))
    vec = pl.BlockSpec((1, d), lambda i: (0, 0))
    return pl.pallas_call(
        body, name="mix_pre_bwd", grid=(s // tm,),
        in_specs=[row, vec, row] + [pl.BlockSpec((tm, p.shape[1]), lambda i: (i, 0)) for p in dps]
        + [pl.BlockSpec((nb, d, cb), lambda i: (0, 0, 0), pipeline_mode=pl.Buffered(1))],
        out_specs=[row, vec],
        out_shape=[jax.ShapeDtypeStruct((s, d), F32), jax.ShapeDtypeStruct((1, d), F32)],
        compiler_params=_params(("arbitrary",)),
    )(x, nw, dres, *dps, win)


def _mix_post(x, yr, ya, nr, na, wout):
    s, d = x.shape
    h = yr.shape[1]
    tm = _tile(s)

    def body(x_ref, yr_ref, ya_ref, nr_ref, na_ref, w_ref, out_ref):
        yrv = yr_ref[...]
        yav = ya_ref[...]
        onb = (yrv * _rms_r(yrv) * nr_ref[...]).astype(BF16)
        oab = (yav * _rms_r(yav) * na_ref[...]).astype(BF16)
        out_ref[...] = x_ref[...] + _dot(onb, w_ref[0:h, :]) + _dot(oab, w_ref[h:2 * h, :])

    row = pl.BlockSpec((tm, d), lambda i: (i, 0))
    half = pl.BlockSpec((tm, h), lambda i: (i, 0))
    vec = pl.BlockSpec((1, h), lambda i: (0, 0))
    return pl.pallas_call(
        body, name="mix_post", grid=(s // tm,),
        in_specs=[row, half, half, vec, vec, pl.BlockSpec((2 * h, d), lambda i: (0, 0))],
        out_specs=row, out_shape=jax.ShapeDtypeStruct((s, d), F32),
        compiler_params=_params(("arbitrary",)),
    )(x, yr, ya, nr, na, wout)


def _mix_post_bwd(dx, yr, ya, nr, na, wout):
    s, d = dx.shape
    h = yr.shape[1]
    tm = _tile(s)

    def body(dx_ref, yr_ref, ya_ref, nr_ref, na_ref, w_ref,
             dyr_ref, dya_ref, yc_ref, dxb_ref, dnr_ref, dna_ref):
        i = pl.program_id(0)
        dxb = dx_ref[...].astype(BF16)
        dxb_ref[...] = dxb
        dyc = _dot_nt(dxb, w_ref[...])
        yrv = yr_ref[...]
        yav = ya_ref[...]
        rr = _rms_r(yrv)
        ra = _rms_r(yav)
        yc_ref[:, 0:h] = (yrv * rr * nr_ref[...]).astype(BF16)
        yc_ref[:, h:2 * h] = (yav * ra * na_ref[...]).astype(BF16)
        dyr, dnr = _rms_bwd(yrv, rr, nr_ref[...], dyc[:, 0:h])
        dya, dna = _rms_bwd(yav, ra, na_ref[...], dyc[:, h:2 * h])
        dyr_ref[...] = dyr
        dya_ref[...] = dya

        @pl.when(i == 0)
        def _():
            dnr_ref[...] = jnp.zeros_like(dnr_ref)
            dna_ref[...] = jnp.zeros_like(dna_ref)

        dnr_ref[...] += dnr
        dna_ref[...] += dna

    row = pl.BlockSpec((tm, d), lambda i: (i, 0))
    half = pl.BlockSpec((tm, h), lambda i: (i, 0))
    vec = pl.BlockSpec((1, h), lambda i: (0, 0))
    return pl.pallas_call(
        body, name="mix_post_bwd", grid=(s // tm,),
        in_specs=[row, half, half, vec, vec, pl.BlockSpec((2 * h, d), lambda i: (0, 0))],
        out_specs=[half, half, pl.BlockSpec((tm, 2 * h), lambda i: (i, 0)), row, vec, vec],
        out_shape=[jax.ShapeDtypeStruct((s, h), F32), jax.ShapeDtypeStruct((s, h), F32),
                   jax.ShapeDtypeStruct((s, 2 * h), BF16), jax.ShapeDtypeStruct((s, d), BF16),
                   jax.ShapeDtypeStruct((1, h), F32), jax.ShapeDtypeStruct((1, h), F32)],
        compiler_params=_params(("arbitrary",)),
    )(dx, yr, ya, nr, na, wout)


def _shift_down(xv, s, prev8):
    rolled = pltpu.roll(xv, s, 0)
    row8 = lax.broadcasted_iota(jnp.int32, prev8.shape, 0)
    head = jnp.where(row8 < s, pltpu.roll(prev8, s, 0), rolled[0:8, :])
    return jnp.concatenate([head, rolled[8:, :]], axis=0)


def _shift_up(xv, s, next8):
    n = xv.shape[0]
    rolled = pltpu.roll(xv, n - s, 0)
    row8 = lax.broadcasted_iota(jnp.int32, next8.shape, 0)
    tail = jnp.where(row8 >= 8 - s, pltpu.roll(next8, 8 - s, 0), rolled[n - 8:, :])
    return jnp.concatenate([rolled[:n - 8, :], tail], axis=0)


def _scan_fwd(a, b):
    n = a.shape[0]
    sub = lax.broadcasted_iota(jnp.int32, a.shape, 0) % SUBLANES
    s = 1
    while s < SUBLANES:
        ok = sub >= s
        b = jnp.where(ok, a * pltpu.roll(b, s, 0) + b, b)
        a = jnp.where(ok, a * pltpu.roll(a, s, 0), a)
        s *= 2
    groups = []
    before = jnp.zeros((1, a.shape[1]), F32)
    for g in range(n // SUBLANES):
        rows = slice(g * SUBLANES, (g + 1) * SUBLANES)
        groups.append(a[rows] * before + b[rows])
        before = groups[-1][SUBLANES - 1:]
    return jnp.concatenate(groups, axis=0)


def _scan_bwd(a, b):
    n = a.shape[0]
    sub = lax.broadcasted_iota(jnp.int32, a.shape, 0) % SUBLANES
    s = 1
    while s < SUBLANES:
        ok = sub < SUBLANES - s
        b = jnp.where(ok, a * pltpu.roll(b, n - s, 0) + b, b)
        a = jnp.where(ok, a * pltpu.roll(a, n - s, 0), a)
        s *= 2
    groups = []
    after = jnp.zeros((1, a.shape[1]), F32)
    for g in reversed(range(n // SUBLANES)):
        rows = slice(g * SUBLANES, (g + 1) * SUBLANES)
        groups.append(a[rows] * after + b[rows])
        after = groups[-1][:1]
    return jnp.concatenate(groups[::-1], axis=0)


def _rglru_gates(xv, prev8, cw_ref, cb_ref, wa_ref, ba_ref, wx_ref, bx_ref, lam_ref):
    x1 = _shift_down(xv, 1, prev8)
    x2 = _shift_down(xv, 2, prev8)
    x3 = _shift_down(xv, 3, prev8)
    xc = cw_ref[3:4, :] * xv + cw_ref[2:3, :] * x1 + cw_ref[1:2, :] * x2 + cw_ref[0:1, :] * x3 + cb_ref[...]
    xcb = xc.astype(BF16)
    r = _sigmoid(_dot(xcb, wa_ref[...]) + ba_ref[...])
    ig = _sigmoid(_dot(xcb, wx_ref[...]) + bx_ref[...])
    c = RG_C * _log_sigmoid(lam_ref[...])
    la = r * c
    a = jnp.exp(la)
    m = jnp.sqrt(-_expm1_neg(2.0 * la))
    return (x1, x2, x3), xc, xcb, r, ig, c, a, m


def _rglru_fwd(proj, cw, cb, wa, ba, wx, bx, lam, rider=None):
    s = proj.shape[0]
    w = D_RNN
    tm = _tile(s)
    ni = s // tm

    def body(*refs):
        ins, (y_ref, h_ref), (prev, hlast), copies = _split_refs(refs, 9, 2, rider)
        xr_ref, gate_ref, cw_ref, cb_ref, wa_ref, ba_ref, wx_ref, bx_ref, lam_ref = ins
        finish = _ride(copies, pl.program_id(0) == 0, pl.program_id(0) == ni - 1)

        @pl.when(pl.program_id(0) == 0)
        def _():
            prev[...] = jnp.zeros_like(prev)
            hlast[...] = jnp.zeros_like(hlast)

        xv = xr_ref[...]
        _, xc, _, _, ig, _, a, m = _rglru_gates(xv, prev[...], cw_ref, cb_ref, wa_ref, ba_ref,
                                                wx_ref, bx_ref, lam_ref)
        b = m * (ig * xc)
        row = lax.broadcasted_iota(jnp.int32, b.shape, 0)
        b = jnp.where(row == 0, b + a * hlast[...], b)
        h = _scan_fwd(a, b)
        h_ref[...] = h
        y_ref[...] = h * _gelu(gate_ref[...])
        prev[...] = xv[tm - 8:, :]
        hlast[...] = h[tm - 1:tm, :]
        finish()

    vec = pl.BlockSpec((1, w), lambda i: (0, 0))
    sq = pl.BlockSpec((w, w), lambda i: (0, 0))
    out = pl.BlockSpec((tm, w), lambda i: (i, 0))
    return _call(
        body, "rglru_fwd", (ni,),
        [pl.BlockSpec((tm, w), lambda i: (i, 0)), pl.BlockSpec((tm, w), lambda i: (i, 1)),
         pl.BlockSpec((CONV_W, w), lambda i: (0, 0)), vec, sq, vec, sq, vec, vec], [out, out],
        [jax.ShapeDtypeStruct((s, w), F32), jax.ShapeDtypeStruct((s, w), F32)],
        [proj, proj, cw, cb, wa, ba, wx, bx, lam],
        scratch=[pltpu.VMEM((8, w), F32), pltpu.VMEM((1, w), F32)], rider=rider)


def _rglru_bwd(proj, hseq, dyr, cw, cb, wa, ba, wx, bx, lam):
    s = proj.shape[0]
    w = D_RNN
    tm = _tile(s)
    nt = s // tm
    t8 = tm // 8

    def body(xr_ref, xp_ref, gate_ref, h_ref, hp_ref, dy_ref, cw_ref, cb_ref, wa_ref, ba_ref,
             wx_ref, bx_ref, lam_ref,
             dxr_ref, dgate_ref, dcw_ref, dcb_ref, dwa_ref, dba_ref, dwx_ref, dbx_ref, dlam_ref,
             carry, dxc_next):
        i = pl.program_id(0)
        first_tile = i == nt - 1

        @pl.when(i == 0)
        def _():
            carry[...] = jnp.zeros_like(carry)
            dxc_next[...] = jnp.zeros_like(dxc_next)
            for ref in (dcw_ref, dcb_ref, dwa_ref, dba_ref, dwx_ref, dbx_ref, dlam_ref):
                ref[...] = jnp.zeros_like(ref)

        xv = xr_ref[...]
        prev8 = jnp.where(first_tile, 0.0, xp_ref[...])
        hprev8 = jnp.where(first_tile, 0.0, hp_ref[...])
        (x1, x2, x3), xc, xcb, r, ig, c, a, m = _rglru_gates(
            xv, prev8, cw_ref, cb_ref, wa_ref, ba_ref, wx_ref, bx_ref, lam_ref)
        gv = gate_ref[...]
        hv = h_ref[...]
        dy = dy_ref[...]
        dgate_ref[...] = (dy * hv * _gelu_grad(gv)).astype(BF16)
        dh = dy * _gelu(gv)
        row = lax.broadcasted_iota(jnp.int32, dh.shape, 0)
        dh = jnp.where(row == tm - 1, dh + carry[...], dh)
        a_up = jnp.where(row == tm - 1, 0.0, pltpu.roll(a, tm - 1, 0))
        lam_t = _scan_bwd(a_up, dh)
        carry[...] = a[0:1, :] * lam_t[0:1, :]
        hm1 = _shift_down(hv, 1, hprev8)
        da = lam_t * hm1
        ixc = ig * xc
        dm = lam_t * ixc
        dig = lam_t * m * xc
        dxc = lam_t * m * ig
        dla = da * a - dm * (a * a) / m
        dr = dla * c
        dlam_ref[...] += jnp.sum(dla * r, axis=0, keepdims=True)
        dpa = dr * r * (1.0 - r)
        dpi = dig * ig * (1.0 - ig)
        dba_ref[...] += jnp.sum(dpa, axis=0, keepdims=True)
        dbx_ref[...] += jnp.sum(dpi, axis=0, keepdims=True)
        dpab = dpa.astype(BF16)
        dpib = dpi.astype(BF16)
        dwa_ref[...] += _dot_tn(xcb, dpab)
        dwx_ref[...] += _dot_tn(xcb, dpib)
        dxc = dxc + _dot_nt(dpab, wa_ref[...]) + _dot_nt(dpib, wx_ref[...])
        dcb_ref[...] += jnp.sum(dxc, axis=0, keepdims=True)
        dcw_ref[3:4, :] += jnp.sum(dxc * xv, axis=0, keepdims=True)
        dcw_ref[2:3, :] += jnp.sum(dxc * x1, axis=0, keepdims=True)
        dcw_ref[1:2, :] += jnp.sum(dxc * x2, axis=0, keepdims=True)
        dcw_ref[0:1, :] += jnp.sum(dxc * x3, axis=0, keepdims=True)
        nxt = dxc_next[...]
        dxr = (cw_ref[3:4, :] * dxc + cw_ref[2:3, :] * _shift_up(dxc, 1, nxt)
               + cw_ref[1:2, :] * _shift_up(dxc, 2, nxt) + cw_ref[0:1, :] * _shift_up(dxc, 3, nxt))
        dxr_ref[...] = dxr.astype(BF16)
        dxc_next[...] = dxc[0:8, :]

        @pl.when(first_tile)
        def _():
            lv = lam_ref[...]
            dlam_ref[...] = dlam_ref[...] * (RG_C * _sigmoid(-lv))

    rev = lambda i: nt - 1 - i
    vec = pl.BlockSpec((1, w), lambda i: (0, 0))
    sq = pl.BlockSpec((w, w), lambda i: (0, 0))
    cur = lambda col: pl.BlockSpec((tm, w), lambda i: (rev(i), col))
    before = lambda cols: pl.BlockSpec((8, w), lambda i: (jnp.maximum(rev(i) * t8 - 1, 0), 0))
    return pl.pallas_call(
        body, name="rglru_bwd", grid=(nt,),
        in_specs=[cur(0), before(None), cur(1), cur(0), before(None), cur(0),
                  pl.BlockSpec((CONV_W, w), lambda i: (0, 0)), vec, sq, vec, sq, vec, vec],
        out_specs=[cur(0), cur(0), pl.BlockSpec((CONV_W, w), lambda i: (0, 0)), vec, sq, vec, sq, vec, vec],
        out_shape=[jax.ShapeDtypeStruct((s, w), BF16), jax.ShapeDtypeStruct((s, w), BF16),
                   jax.ShapeDtypeStruct((CONV_W, w), F32), jax.ShapeDtypeStruct((1, w), F32),
                   jax.ShapeDtypeStruct((w, w), F32), jax.ShapeDtypeStruct((1, w), F32),
                   jax.ShapeDtypeStruct((w, w), F32), jax.ShapeDtypeStruct((1, w), F32),
                   jax.ShapeDtypeStruct((1, w), F32)],
        scratch_shapes=[pltpu.VMEM((1, w), F32), pltpu.VMEM((8, w), F32)],
        compiler_params=_params(("arbitrary",)),
    )(proj, proj, proj, hseq, hseq, dyr, cw, cb, wa, ba, wx, bx, lam)


def _sb_logs(z, valid):
    lb = jnp.minimum(z, 0.0) - jnp.log(1.0 + jnp.exp(-jnp.abs(z)))
    return lb, jnp.where(valid, lb - z, 0.0)


class _Window:
    def __init__(self):
        blk, win, cut = ATT_BLOCK, ATT_WINDOW, ATT_SPLIT
        self.row = lax.broadcasted_iota(jnp.int32, (blk, win), 0)
        self.col = lax.broadcasted_iota(jnp.int32, (blk, win), 1)

        def tri(n, later):
            j = lax.broadcasted_iota(jnp.int32, (n, n), 0)
            s = lax.broadcasted_iota(jnp.int32, (n, n), 1)
            return jnp.where((j > s) if later else (j < s), 1.0, 0.0).astype(BF16)

        self.later = (tri(cut, True), tri(win - cut, True))
        self.earlier = (tri(cut, False), tri(win - cut, False))

    def place(self, qi, g):
        end = (qi + 1) * ATT_BLOCK - g * ATT_WINDOW
        start = pl.multiple_of(jnp.maximum(end - ATT_WINDOW, 0), ATT_BLOCK)
        valid = self.col < jnp.minimum(self.row + (qi * ATT_BLOCK - start), end - start)
        return start, valid

    @staticmethod
    def _parts(xv):
        hi = xv.astype(BF16)
        lo = (xv - hi.astype(F32)).astype(BF16)
        cut = ATT_SPLIT
        sums = (jnp.sum(xv[:, :cut], axis=1, keepdims=True), jnp.sum(xv[:, cut:], axis=1, keepdims=True))
        return (hi[:, :cut], lo[:, :cut]), (hi[:, cut:], lo[:, cut:]), sums

    def sums_after(self, xv, carry):
        (h0, l0), (h1, l1), (s0, s1) = self._parts(xv)
        first = _dot(h0, self.later[0]) + _dot(l0, self.later[0]) + (s1 + carry)
        last = _dot(h1, self.later[1]) + _dot(l1, self.later[1]) + carry
        return jnp.concatenate([first, last], axis=1), s0 + s1

    def sums_before(self, xv, carry):
        (h0, l0), (h1, l1), (s0, s1) = self._parts(xv)
        first = _dot(h0, self.earlier[0]) + _dot(l0, self.earlier[0]) + carry
        last = _dot(h1, self.earlier[1]) + _dot(l1, self.earlier[1]) + (s0 + carry)
        return jnp.concatenate([first, last], axis=1), s0 + s1


class _HeadPair:
    def __init__(self):
        lanes = 2 * HEAD_DIM
        lane = lax.broadcasted_iota(jnp.int32, (1, lanes), 1)
        self.masks = [lane // HEAD_DIM == h for h in (0, 1)]
        i = lax.broadcasted_iota(jnp.int32, (lanes, lanes), 0) // HEAD_DIM
        j = lax.broadcasted_iota(jnp.int32, (lanes, lanes), 1) // HEAD_DIM
        self.same_head = jnp.where(i == j, 1.0, 0.0).astype(BF16)

    def only(self, h, xv):
        return jnp.where(self.masks[h], xv, jnp.zeros_like(xv))

    def merge(self, per_head):
        return jnp.where(self.masks[0], per_head[0], per_head[1])

    def mean(self, xv):
        hi = xv.astype(BF16)
        lo = (xv - hi.astype(F32)).astype(BF16)
        return (_dot(hi, self.same_head) + _dot(lo, self.same_head)) * (1.0 / HEAD_DIM)

    def rms_r(self, xv):
        return lax.rsqrt(self.mean(xv * xv) + EPS)

    def rms_bwd(self, xv, r, nw, dh):
        t = dh * nw
        dx = r * t - xv * (r * r * r * self.mean(t * xv))
        dn = jnp.sum(dh * xv * r, axis=0, keepdims=True)
        return dx, dn[:, :HEAD_DIM] + dn[:, HEAD_DIM:]


def _attn_fwd(proj, qg, kg, rider=None):
    s = proj.shape[0]
    blk, win, dh = ATT_BLOCK, ATT_WINDOW, HEAD_DIM
    nq = s // blk
    scale = 1.0 / math.sqrt(dh)
    heads = (0, 1)
    blocks = (0, 1)
    assert s >= win and s % (blk * len(blocks)) == 0

    def body(*refs):
        (q_ref, k_ref, v_ref, qg_ref, kg_ref), (o_ref,), (qn, kn, vb), copies = _split_refs(refs, 5, 1, rider)
        finish = _ride(copies, pl.program_id(0) == 0, pl.program_id(0) == N_HEADS // 2 - 1)
        wd, hp = _Window(), _HeadPair()
        qv = q_ref[...]
        qn[...] = (qv * hp.rms_r(qv) * qg_ref[...] * scale).astype(BF16)
        kv = k_ref[...]
        kn[...] = (kv * hp.rms_r(kv) * kg_ref[...]).astype(BF16)
        vb[...] = v_ref[...].astype(BF16)

        def q_step(pair_i, _):
            qis = [2 * pair_i + b for b in blocks]
            chains = [(b, h) for b in blocks for h in heads]
            qoffs = [pl.multiple_of(qi * blk, blk) for qi in qis]
            qtiles = [qn[pl.ds(qoff, blk), :] for qoff in qoffs]
            qts = [hp.only(h, qtiles[b]) for b, h in chains]

            def more(carry):
                g, live = carry[:2]
                return jnp.logical_and((qis[-1] + 1) * blk - g * win > 0, live > 0)

            def window(carry):
                g, _, accs, runs = carry
                places = [wd.place(qi, g) for qi in qis]
                kts = [kn[pl.ds(start, win), :] for start, _ in places]
                zs = [_dot_nt(qts[c], kts[b]) for c, (b, h) in enumerate(chains)]
                logs = [_sb_logs(zs[c], places[b][1]) for c, (b, h) in enumerate(chains)]
                sums = [wd.sums_after(logs[c][1], runs[c]) for c in range(len(chains))]
                wgts = [jnp.where(places[b][1], jnp.exp(logs[c][0] + sums[c][0]), 0.0).astype(BF16)
                        for c, (b, h) in enumerate(chains)]
                vts = [vb[pl.ds(start, win), :] for start, _ in places]
                accs = tuple(accs[c] + _dot(wgts[c], vts[b]) for c, (b, h) in enumerate(chains))
                runs = tuple(runs[c] + sums[c][1] for c in range(len(chains)))
                top = functools.reduce(jnp.maximum, [jnp.max(r) for r in runs])
                return g + 1, (top > EXP_ZERO).astype(jnp.int32), accs, runs

            zero = lambda cols: tuple(jnp.zeros((blk, cols), F32) for _ in chains)
            _, _, accs, _ = lax.while_loop(more, window, (jnp.int32(0), jnp.int32(1), zero(2 * dh), zero(1)))
            for b in blocks:
                o_ref[pl.ds(qoffs[b], blk), :] = hp.merge([accs[2 * b + h] for h in heads])
            return 0

        lax.fori_loop(0, nq // len(blocks), q_step, 0)
        finish()

    pair = lambda group: pl.BlockSpec((s, 2 * dh), lambda p: (0, group * (D_ATT // (2 * dh)) + p))
    vec = pl.BlockSpec((1, 2 * dh), lambda p: (0, 0))
    return _call(
        body, "attn_fwd", (N_HEADS // 2,), [pair(2), pair(3), pair(4), vec, vec], [pair(0)],
        [jax.ShapeDtypeStruct((s, D_ATT), F32)], [proj, proj, proj, jnp.tile(qg, (1, 2)), jnp.tile(kg, (1, 2))],
        scratch=[pltpu.VMEM((s, 2 * dh), BF16)] * 3, rider=rider)


def _attn_bwd(proj, dya, qg, kg, rider=None):
    s = proj.shape[0]
    blk, win, dh = ATT_BLOCK, ATT_WINDOW, HEAD_DIM
    nq = s // blk
    max_windows = -(-s // win) + 1
    scale = 1.0 / math.sqrt(dh)
    steps = N_HEADS // 2
    heads = (0, 1)
    blocks = (0, 1)
    assert s >= win and s % (blk * len(blocks)) == 0

    def body(*refs):
        ins, outs, scratch, copies = _split_refs(refs, 6, 5, rider)
        q_ref, k_ref, v_ref, do_ref, qg_ref, kg_ref = ins
        dq_ref, dk_ref, dv_ref, dqg_ref, dkg_ref = outs
        qn, kn, vb, dob, runs_ref, dqn, dkn, dvn = scratch
        finish = _ride(copies, pl.program_id(0) == 0, pl.program_id(0) == steps - 1)
        wd, hp = _Window(), _HeadPair()

        @pl.when(pl.program_id(0) == 0)
        def _():
            dqg_ref[...] = jnp.zeros_like(dqg_ref)
            dkg_ref[...] = jnp.zeros_like(dkg_ref)

        qv = q_ref[...]
        qn[...] = (qv * hp.rms_r(qv) * qg_ref[...] * scale).astype(BF16)
        kv = k_ref[...]
        kn[...] = (kv * hp.rms_r(kv) * kg_ref[...]).astype(BF16)
        vb[...] = v_ref[...].astype(BF16)
        dob[...] = do_ref[...].astype(BF16)
        dkn[...] = jnp.zeros_like(dkn)
        dvn[...] = jnp.zeros_like(dvn)

        def q_step(pair_i, _):
            qis = [2 * pair_i + b for b in blocks]
            chains = [(b, h) for b in blocks for h in heads]
            ids = range(len(chains))
            qoffs = [pl.multiple_of(qi * blk, blk) for qi in qis]
            qts = [hp.only(h, qn[pl.ds(qoffs[b], blk), :]) for b, h in chains]
            dots = [hp.only(h, dob[pl.ds(qoffs[b], blk), :]) for b, h in chains]

            zero = lambda cols: tuple(jnp.zeros((blk, cols), F32) for _ in chains)

            def logs_of(g):
                places = [wd.place(qi, g) for qi in qis]
                kts = [kn[pl.ds(start, win), :] for start, _ in places]
                return [_sb_logs(_dot_nt(qts[c], kts[b]), places[b][1]) for c, (b, h) in enumerate(chains)]

            def row_sums(logs):
                return tuple(jnp.sum(logs[c][1], axis=1, keepdims=True) for c in ids)

            def still_live(runs):
                return functools.reduce(jnp.maximum, [jnp.max(r) for r in runs]) > EXP_ZERO

            def window_grads(g, logs, runs, esums):
                places = [wd.place(qi, g) for qi in qis]
                kts = [kn[pl.ds(start, win), :] for start, _ in places]
                vts = [vb[pl.ds(start, win), :] for start, _ in places]
                dws = [_dot_nt(dots[c], vts[b]) for c, (b, h) in enumerate(chains)]
                tails = [wd.sums_after(logs[c][1], runs[c])[0] for c in ids]
                wgts = [jnp.where(places[b][1], jnp.exp(logs[c][0] + tails[c]), 0.0) for c, (b, h) in enumerate(chains)]
                es = [dws[c] * wgts[c] for c in ids]
                befores = [wd.sums_before(es[c], esums[c]) for c in ids]
                dzbs = []
                for c, (b, h) in enumerate(chains):
                    beta = jnp.exp(logs[c][0])
                    dz = jnp.where(places[b][1], es[c] * (1.0 - beta) - befores[c][0] * beta, 0.0)
                    dzbs.append(dz.astype(BF16))
                for b in blocks:
                    rows = pl.ds(places[b][0], win)
                    dkn[rows, :] += _dot_tn(dzbs[2 * b], qts[2 * b]) + _dot_tn(dzbs[2 * b + 1], qts[2 * b + 1])
                    dvn[rows, :] += (_dot_tn(wgts[2 * b].astype(BF16), dots[2 * b])
                                     + _dot_tn(wgts[2 * b + 1].astype(BF16), dots[2 * b + 1]))
                return (tuple(_dot(dzbs[c], kts[b]) for c, (b, h) in enumerate(chains)),
                        tuple(befores[c][1] for c in ids))

            logs0 = logs_of(0)
            runs1 = row_sums(logs0)

            def one_window():
                return window_grads(0, logs0, zero(1), zero(1))[0]

            def all_windows():
                def more(carry):
                    g, live = carry[:2]
                    return jnp.logical_and((qis[-1] + 1) * blk - g * win > 0, live > 0)

                def run_window(carry):
                    g, _, runs = carry
                    for c in ids:
                        runs_ref[c, g] = runs[c]
                    sums = row_sums(logs_of(g))
                    runs = tuple(runs[c] + sums[c] for c in ids)
                    return g + 1, still_live(runs).astype(jnp.int32), runs

                for c in ids:
                    runs_ref[c, 0] = jnp.zeros((blk, 1), F32)
                windows, _, _ = lax.while_loop(more, run_window, (jnp.int32(1), jnp.int32(1), runs1))

                def k_window(gg, carry):
                    dq_accs, esums = carry
                    g = windows - 1 - gg
                    parts, totals = window_grads(g, logs_of(g), [runs_ref[c, g] for c in ids], esums)
                    return (tuple(dq_accs[c] + parts[c] for c in ids), tuple(esums[c] + totals[c] for c in ids))

                return lax.fori_loop(0, windows, k_window, (zero(2 * dh), zero(1)))[0]

            earlier_keys = (qis[-1] + 1) * blk - win > 0
            dq_accs = lax.cond(jnp.logical_and(earlier_keys, still_live(runs1)), all_windows, one_window)
            for b in blocks:
                dqn[pl.ds(qoffs[b], blk), :] = hp.merge([dq_accs[2 * b + h] for h in heads])
            return 0

        lax.fori_loop(0, nq // len(blocks), q_step, 0)

        dq, dqg = hp.rms_bwd(qv, hp.rms_r(qv), qg_ref[...] * scale, dqn[...])
        dq_ref[...] = dq.astype(BF16)
        dqg_ref[...] += dqg * scale
        dk, dkg = hp.rms_bwd(kv, hp.rms_r(kv), kg_ref[...], dkn[...])
        dk_ref[...] = dk.astype(BF16)
        dkg_ref[...] += dkg
        dv_ref[...] = dvn[...].astype(BF16)
        finish()

    pair = lambda group: pl.BlockSpec((s, 2 * dh), lambda p: (0, group * (D_ATT // (2 * dh)) + p))
    vec2 = pl.BlockSpec((1, 2 * dh), lambda p: (0, 0))
    vec = pl.BlockSpec((1, dh), lambda p: (0, 0))
    return _call(
        body, "attn_bwd", (steps,), [pair(2), pair(3), pair(4), pair(0), vec2, vec2],
        [pair(0), pair(0), pair(0), vec, vec],
        [jax.ShapeDtypeStruct((s, D_ATT), BF16)] * 3 + [jax.ShapeDtypeStruct((1, dh), F32)] * 2,
        [proj, proj, proj, dya, jnp.tile(qg, (1, 2)), jnp.tile(kg, (1, 2))],
        scratch=[pltpu.VMEM((s, 2 * dh), BF16)] * 4 + [pltpu.VMEM((4, max_windows, blk, 1), F32)]
        + [pltpu.VMEM((s, 2 * dh), F32)] * 3, rider=rider)


def _block_diag(w):
    n, c, d = w.shape
    return jnp.einsum("ncd,nm->ncmd", w, jnp.eye(n, dtype=w.dtype)).reshape(n * c, n * d)


def _diag_blocks(full, n):
    c = full.shape[0] // n
    return jnp.stack([full[i * c:(i + 1) * c, i * c:(i + 1) * c] for i in range(n)])


FFN1 = ["ffn1_w_gate", "ffn1_w_up", "ffn1_w_down"]
FFN2 = ["ffn2_w_gate", "ffn2_w_up", "ffn2_w_down"]


def _pair_sums(gb, names, where):
    theirs = _pair_exchange([gb[n] for n in names], "pair_exchange_" + names[0])
    pair, own = _pair_sum([gb[n] for n in names], theirs, where, "pair_sum_" + names[0])
    return _chip_rider(pair, own)


def _local_step(x, tgt, stacks, conv_stack, small, where):
    gate_up, down = FFN1[:2], FFN1[2:]
    big = dict(zip(gate_up, _gather_weights([stacks[n] for n in gate_up], [])))
    wa = _block_diag(small["rg_w_a"]).astype(BF16)
    wx = _block_diag(small["rg_w_x"]).astype(BF16)

    whole = lambda names: [big[n].reshape(-1, D_MODEL) for n in names]
    soon = down + ["w_in"]
    g1, u1, hb1, ab1, *landed = _ffn_up(x, small["ffn1_norm"], *whole(gate_up),
                                        rider=_gather_rider([stacks[n] for n in soon], [conv_stack]))
    big.update(zip(soon, landed))
    x1 = _ffn_down(x, ab1, *whole(down))
    conv_w = jnp.transpose(landed[-1], (1, 0, 2)).reshape(CONV_W, D_RNN)
    rg = (conv_w, small["conv_b"], wa, small["rg_b_a"], wx, small["rg_b_x"], small["rg_lambda"])
    riding = lambda names: _gather_rider([stacks[n] for n in names], [])
    proj, hb2, big["ffn2_w_gate"] = _mix_pre(x1, small["mix_norm"], big["w_in"], riding(["ffn2_w_gate"]))
    yr, hseq, big["ffn2_w_up"] = _rglru_fwd(proj, *rg, riding(["ffn2_w_up"]))
    ya, big["ffn2_w_down"], big["w_out"] = _attn_fwd(proj, small["q_norm"], small["k_norm"],
                                                     riding(["ffn2_w_down", "w_out"]))
    wout = big["w_out"].reshape(D_MODEL, D_MODEL)
    x2 = _mix_post(x1, yr, ya, small["rnn_out_norm"], small["attn_out_norm"], wout)
    dx3, g2, u2, hb3, ab3, loss = _ffn_fwd_loss(x2, small["ffn2_norm"], *whole(FFN2), tgt)

    gb, gs, slots = {}, {}, {}
    dx2, dg2, du2, dyb2, gs["ffn2_norm"] = _ffn_bwd_act(x2, small["ffn2_norm"], dx3, g2, u2, *whole(FFN2), "ffn2_bwd")
    gb["ffn2_w_gate"] = _ffn_wgrad(dg2, hb3, 1.0, "wgrad_gate_ffn2")
    gb["ffn2_w_up"] = _ffn_wgrad(du2, hb3, 1.0, "wgrad_up_ffn2")
    gb["ffn2_w_down"] = _ffn_wgrad(ab3, dyb2, 0.5, "wgrad_down_ffn2")
    dyr, dya, ycat, dxb2, gs["rnn_out_norm"], gs["attn_out_norm"] = _mix_post_bwd(
        dx2, yr, ya, small["rnn_out_norm"], small["attn_out_norm"], wout)
    gb["w_out"] = _wgrad_whole(ycat, [dxb2], False, "wgrad_out")
    early = FFN2 + ["w_out"]
    dq, dk, dv, gs["q_norm"], gs["k_norm"], *done = _attn_bwd(
        proj, dya, small["q_norm"], small["k_norm"], _pair_sums(gb, early, where))
    slots.update(zip(early, done))
    dxr, dgate, gs["conv_w"], gs["conv_b"], dwa, gs["rg_b_a"], dwx, gs["rg_b_x"], gs["rg_lambda"] = _rglru_bwd(
        proj, hseq, dyr, *rg)
    gs["rg_w_a"] = _diag_blocks(dwa, RNN_BLOCKS)
    gs["rg_w_x"] = _diag_blocks(dwx, RNN_BLOCKS)
    dps = [dxr, dgate, dq, dk, dv]
    dx1, gs["mix_norm"] = _mix_pre_bwd(x1, small["mix_norm"], dx2, dps, big["w_in"])
    dx0, dg1, du1, dyb1, gs["ffn1_norm"] = _ffn_bwd_act(x, small["ffn1_norm"], dx1, g1, u1, *whole(FFN1), "ffn1_bwd")

    mine = _place_shard(_pack([gs[n] for n in SMALL] + [loss[:, :1]]), where, F32, "place_small_grads",
                        by_device=True)
    gb["ffn1_w_gate"], everyone = _ffn_wgrad(dg1, hb1, 1.0, "wgrad_gate_ffn1", _small_rider(mine))
    gb["ffn1_w_up"], slots["ffn1_w_gate"] = _ffn_wgrad(
        du1, hb1, 1.0, "wgrad_up_ffn1", _pair_sums(gb, ["ffn1_w_gate"], where))
    gb["ffn1_w_down"], slots["ffn1_w_up"] = _ffn_wgrad(
        ab1, dyb1, 0.5, "wgrad_down_ffn1", _pair_sums(gb, ["ffn1_w_up"], where))
    gb["w_in"], slots["ffn1_w_down"] = _wgrad_whole(
        hb2, dps, True, "wgrad_in", _pair_sums(gb, ["ffn1_w_down"], where))
    last = _pair_sums(gb, ["w_in"], where)
    slots["w_in"], = _chip_exchange(last.plain, last.inplace)
    return dx0, slots, gs, everyone


ANY = pl.BlockSpec(memory_space=pl.ANY)


def _place():
    x, y, c = lax.axis_index("x"), lax.axis_index("y"), lax.axis_index("c")
    other_chips = [(1 - x, y), (x, 1 - y), (1 - x, 1 - y)]
    return x, y, c, 2 * x + y, other_chips


def _remote(src, dst, send_sem, recv_sem, to):
    return pltpu.make_async_remote_copy(src_ref=src, dst_ref=dst, send_sem=send_sem, recv_sem=recv_sem,
                                        device_id=to, device_id_type=MESH)


def _copy_plan(pairs):
    sends = [functools.partial(_remote, *a) for a, _ in pairs]
    arrivals = [functools.partial(_remote, *b) for _, b in pairs]
    return sends, arrivals


class _Rider:
    def __init__(self, plan, plain, inplace, n_copies=None, relay=None, n_relay=0):
        self.plan, self.plain, self.inplace = plan, list(plain), list(inplace)
        self.n_copies = n_copies or 3 * len(self.inplace)
        self.relay, self.n_relay = relay, n_relay

    def operands(self):
        return self.plain + self.inplace

    def out_shape(self):
        return [jax.ShapeDtypeStruct(a.shape, a.dtype) for a in self.inplace]

    def aliases(self, inputs_before, outputs_before):
        return {inputs_before + len(self.plain) + k: outputs_before + k for k in range(len(self.inplace))}

    def scratch(self):
        relay = [pltpu.SemaphoreType.DMA((self.n_relay,))] * 2 if self.relay else []
        return [pltpu.SemaphoreType.DMA((self.n_copies,))] * 2 + relay


def _split_refs(refs, n_in, n_out, rider):
    if rider is None:
        return refs[:n_in], refs[n_in:n_in + n_out], refs[n_in + n_out:], None
    r_in, r_out = len(rider.operands()), len(rider.inplace)
    outs_at = n_in + r_in
    n_sems = len(rider.scratch())
    rest = refs[outs_at + n_out + r_out:]
    sems = rest[len(rest) - n_sems:]
    filled = refs[outs_at + n_out:outs_at + n_out + r_out]
    copies = functools.partial(rider.plan, refs[n_in:n_in + len(rider.plain)], filled, *sems[:2])
    relay = functools.partial(rider.relay, filled, *sems[2:]) if rider.relay else None
    return refs[:n_in], refs[outs_at:outs_at + n_out], rest[:len(rest) - n_sems], (copies, relay)


def _ride(copies, first, last, middle=None):
    if copies is None:
        return lambda: None
    copies, relay = copies

    @pl.when(first)
    def _():
        _start(copies()[0])

    def start_relay():
        for make in copies()[1]:
            make().wait_recv()
        _start(relay()[0])

    if relay is not None and middle is not None:
        pl.when(middle)(start_relay)

    def finish():
        @pl.when(last)
        def _():
            if relay is None:
                _finish(*copies())
            else:
                if middle is None:
                    start_relay()
                _finish(copies()[0] + relay()[0], relay()[1])

    return finish


def _gather_rider(split, whole):
    n_split = len(split)
    return _Rider(lambda plain, stacks, ss, rs: _gather_ici(stacks, n_split, ss, rs), [], list(split) + list(whole),
                  relay=lambda stacks, ss, rs: _gather_d2d(stacks[:n_split], ss, rs), n_relay=3 * n_split)


def _chip_rider(sums, slots):
    return _Rider(_chip_copies, sums, slots)


def _start(makers):
    for make in makers:
        make().start()


def _finish(sends, arrivals):
    for make in arrivals:
        make().wait_recv()
    for make in sends:
        make().wait_send()


def _half(rows, c):
    return pl.ds(pl.multiple_of(c * rows, BF16_ROWS), rows)


def _gather_weights(split, whole):
    arrs = list(split) + list(whole)
    n, ns = len(arrs), len(split)

    def body(*refs):
        outs = refs[n:2 * n]
        send_sems, recv_sems, fsend_sems, frecv_sems = refs[2 * n:]
        sends, arrivals = _gather_ici(outs, ns, send_sems, recv_sems)
        passes, passed = _gather_d2d(outs[:ns], fsend_sems, frecv_sems)
        _start(sends)
        for k, make in enumerate(arrivals):
            make().wait_recv()
            if k < 3 * ns:
                passes[k]().start()
        _finish(sends + passes, passed)

    return pl.pallas_call(
        body, name="gather_weights",
        in_specs=[ANY] * n, out_specs=[ANY] * n,
        out_shape=[jax.ShapeDtypeStruct(a.shape, a.dtype) for a in arrs],
        input_output_aliases={i: i for i in range(n)},
        scratch_shapes=[pltpu.SemaphoreType.DMA((3 * n,)), pltpu.SemaphoreType.DMA((3 * n,)),
                        pltpu.SemaphoreType.DMA((3 * ns,)), pltpu.SemaphoreType.DMA((3 * ns,))],
    )(*arrs)


def _gather_ici(stacks, n_split, send_sems, recv_sems):
    x, y, c, me, chips = _place()

    def region(i, chip):
        if i < n_split:
            return stacks[i].at[chip, _half(stacks[i].shape[1] // 2, c)]
        return stacks[i].at[chip]

    pairs = []
    for i in range(len(stacks)):
        for p, (cx, cy) in enumerate(chips):
            k = 3 * i + p
            mine, got = region(i, me), region(i, 2 * cx + cy)
            sems, to = (send_sems.at[k], recv_sems.at[k]), (cx, cy, c)
            pairs.append(((mine, mine, *sems, to), (got, got, *sems, to)))
    return _copy_plan(pairs)


def _gather_d2d(stacks, send_sems, recv_sems):
    x, y, c, _, chips = _place()
    sibling = (x, y, 1 - c)
    pairs = []
    for i, stack in enumerate(stacks):
        rows = stack.shape[1] // 2
        for p, (cx, cy) in enumerate(chips):
            k = 3 * i + p
            got, theirs = stack.at[2 * cx + cy, _half(rows, c)], stack.at[2 * cx + cy, _half(rows, 1 - c)]
            sems = (send_sems.at[k], recv_sems.at[k])
            pairs.append(((got, got, *sems, sibling), (theirs, theirs, *sems, sibling)))
    return _copy_plan(pairs)


def _pair_exchange(grads, name):
    n = len(grads)

    def body(*refs):
        ins, theirs = refs[:n], refs[n:2 * n]
        send_sems, recv_sems = refs[2 * n:]
        x, y, c, _, _ = _place()
        sibling = (x, y, 1 - c)
        sends = [_remote(ins[k].at[:, _half(grads[k].shape[1] // 2, 1 - c)], theirs[k],
                         send_sems.at[k], recv_sems.at[k], sibling) for k in range(n)]
        for cp in sends:
            cp.start()
        for k in range(n):
            _remote(theirs[k], theirs[k], send_sems.at[k], recv_sems.at[k], sibling).wait_recv()
        for cp in sends:
            cp.wait_send()

    return pl.pallas_call(
        body, name=name,
        in_specs=[ANY] * n, out_specs=[ANY] * n,
        out_shape=[jax.ShapeDtypeStruct((g.shape[0], g.shape[1] // 2, g.shape[2]), g.dtype) for g in grads],
        scratch_shapes=[pltpu.SemaphoreType.DMA((n,))] * 2,
    )(*grads)


def _chip_exchange(sums, slots):
    n = len(sums)

    def body(*refs):
        sends, arrivals = _chip_copies(refs[:n], refs[2 * n:3 * n], *refs[3 * n:])
        _start(sends)
        _finish(sends, arrivals)

    return pl.pallas_call(
        body, name="grad_chip_exchange",
        in_specs=[ANY] * (2 * n), out_specs=[ANY] * n,
        out_shape=[jax.ShapeDtypeStruct(a.shape, a.dtype) for a in slots],
        input_output_aliases={n + k: k for k in range(n)},
        scratch_shapes=[pltpu.SemaphoreType.DMA((3 * n,)), pltpu.SemaphoreType.DMA((3 * n,))],
    )(*sums, *slots)


def _chip_copies(sums, slots, send_sems, recv_sems):
    x, y, c, me, chips = _place()
    pairs = []
    for k in range(len(sums)):
        for p, (cx, cy) in enumerate(chips):
            j = 3 * k + p
            got = slots[k].at[2 * cx + cy]
            sems, to = (send_sems.at[j], recv_sems.at[j]), (cx, cy, c)
            pairs.append(((sums[k].at[2 * cx + cy], slots[k].at[me], *sems, to), (got, got, *sems, to)))
    return _copy_plan(pairs)


def _half_swap(halves):
    n = len(halves)

    def body(*refs):
        outs = refs[n:2 * n]
        send_sems, recv_sems = refs[2 * n:]
        x, y, c, _, _ = _place()
        sibling = (x, y, 1 - c)
        sends = [_remote(outs[k].at[c], outs[k].at[c], send_sems.at[k], recv_sems.at[k], sibling) for k in range(n)]
        for cp in sends:
            cp.start()
        for k in range(n):
            got = outs[k].at[1 - c]
            _remote(got, got, send_sems.at[k], recv_sems.at[k], sibling).wait_recv()
        for cp in sends:
            cp.wait_send()

    return pl.pallas_call(
        body, name="grad_half_swap",
        in_specs=[ANY] * n, out_specs=[ANY] * n,
        out_shape=[jax.ShapeDtypeStruct(a.shape, a.dtype) for a in halves],
        input_output_aliases={k: k for k in range(n)},
        scratch_shapes=[pltpu.SemaphoreType.DMA((n,))] * 2,
    )(*halves)


def _small_rider(stack):
    n_dev = 2 * N_CHIPS

    def plan(_, stacks, send_sems, recv_sems):
        x, y, c, _, _ = _place()
        mine = stacks[0].at[4 * x + 2 * y + c]
        pairs = []
        for k in range(1, n_dev):
            px, py, pc = x ^ ((k >> 2) & 1), y ^ ((k >> 1) & 1), c ^ (k & 1)
            got = stacks[0].at[4 * px + 2 * py + pc]
            sems = (send_sems.at[k - 1], recv_sems.at[k - 1])
            pairs.append(((mine, mine, *sems, (px, py, pc)), (got, got, *sems, (px, py, pc))))
        return _copy_plan(pairs)

    return _Rider(plan, [], [stack], n_dev - 1)


def _row_tile(r):
    return r // 4 if r >= 256 and (r // 4) % BF16_ROWS == 0 else r


def _prefetch_call(body, name, grid, in_specs, out_specs, out_shape):
    spec = pltpu.PrefetchScalarGridSpec(num_scalar_prefetch=1, grid=grid, in_specs=in_specs, out_specs=out_specs)
    return pl.pallas_call(body, name=name, grid_spec=spec, out_shape=out_shape,
                          compiler_params=_params(("arbitrary",) * len(grid)))


def _place_shard(w2d, where, dtype, name, by_device=False):
    r, c = w2d.shape
    tr = _row_tile(r)
    slots = 2 * N_CHIPS if by_device else N_CHIPS
    slot = (lambda s: 2 * s[1] + s[0]) if by_device else (lambda s: s[1])

    def body(where_ref, w_ref, out_ref):
        out_ref[...] = w_ref[...].astype(dtype)

    return _prefetch_call(
        body, name, (r // tr,), [pl.BlockSpec((tr, c), lambda i, s: (i, 0))],
        pl.BlockSpec((None, tr, c), lambda i, s: (slot(s), i, 0)),
        jax.ShapeDtypeStruct((slots, r, c), dtype))(where, w2d)


def _place_shards(w2ds, where, name):
    n = len(w2ds)
    steps = N_CHIPS
    assert all(w.shape[0] % (BF16_ROWS * steps) == 0 for w in w2ds)

    def body(where_ref, *refs):
        for k in range(n):
            refs[n + k][...] = refs[k][...].astype(BF16)

    tile = lambda w: (w.shape[0] // steps, w.shape[1])
    return _prefetch_call(
        body, name, (steps,), [pl.BlockSpec(tile(w), lambda i, s: (i, 0)) for w in w2ds],
        [pl.BlockSpec((None,) + tile(w), lambda i, s: (s[1], i, 0)) for w in w2ds],
        [jax.ShapeDtypeStruct((N_CHIPS,) + w.shape, BF16) for w in w2ds])(where, *w2ds)


def _pair_sum(fulls, theirs, where, name):
    n = len(fulls)

    def body(where_ref, *refs):
        for k in range(n):
            a_ref, b_ref, out_ref, own_ref = refs[k], refs[n + k], refs[2 * n + k], refs[3 * n + k]
            total = (a_ref[...].astype(F32) + b_ref[...].astype(F32)).astype(BF16)
            out_ref[...] = total

            @pl.when(pl.program_id(0) == where_ref[1])
            def _():
                own_ref[...] = total

    half = lambda t: pl.BlockSpec((None,) + t.shape[1:], lambda j, s: (j, s[0], 0))
    blk = lambda t: pl.BlockSpec((None,) + t.shape[1:], lambda j, s: (j, 0, 0))
    own = lambda t: pl.BlockSpec((None,) + t.shape[1:], lambda j, s: (s[1], 0, 0))
    shapes = [jax.ShapeDtypeStruct(t.shape, BF16) for t in theirs]
    outs = _prefetch_call(
        body, name, (N_CHIPS,), [half(t) for t in theirs] + [blk(t) for t in theirs],
        [blk(t) for t in theirs] + [own(t) for t in theirs], shapes + shapes)(where, *fulls, *theirs)
    return outs[:n], outs[n:]


def _chip_sum(slots, where, name):
    n = len(slots)
    steps = 2
    assert all(a.shape[1] % (BF16_ROWS * steps) == 0 for a in slots)

    def body(where_ref, *refs):
        for k in range(n):
            a_ref, out_ref = refs[k], refs[n + k]
            total = a_ref[0].astype(F32)
            for j in range(1, a_ref.shape[0]):
                total = total + a_ref[j].astype(F32)
            out_ref[...] = total

    tile = lambda a: (a.shape[1] // steps, a.shape[2])
    return _prefetch_call(
        body, name, (steps,), [pl.BlockSpec((a.shape[0],) + tile(a), lambda i, s: (0, i, 0)) for a in slots],
        [pl.BlockSpec((None,) + tile(a), lambda i, s: (s[0], i, 0)) for a in slots],
        [jax.ShapeDtypeStruct((2,) + a.shape[1:], F32) for a in slots])(where, *slots)


def _slot_sum(a, name):
    nb, r, c = a.shape
    tr = _row_tile(r)

    def body(a_ref, out_ref):
        total = a_ref[0].astype(F32)
        for j in range(1, nb):
            total = total + a_ref[j].astype(F32)
        out_ref[...] = total

    return pl.pallas_call(
        body, name=name, grid=(r // tr,),
        in_specs=[pl.BlockSpec((nb, tr, c), lambda i: (0, i, 0))],
        out_specs=pl.BlockSpec((tr, c), lambda i: (i, 0)),
        out_shape=jax.ShapeDtypeStruct((r, c), F32), compiler_params=_params(("arbitrary",)),
    )(a)


def _adamw(ws, gs, ms, vs, name, steps=1):
    n = len(ws)
    c1 = 1.0 - ADAM_B1 ** ADAM_STEP
    c2 = 1.0 - ADAM_B2 ** ADAM_STEP
    assert all(w.shape[0] % steps == 0 and (steps == 1 or w.shape[0] // steps % 8 == 0) for w in ws)

    def body(*refs):
        for k in range(n):
            w_ref, g_ref, m_ref, v_ref = (refs[j * n + k] for j in range(4))
            g_out, d_ref, m2_ref, v2_ref = (refs[(4 + j) * n + k] for j in range(4))
            gv = g_ref[...]
            g_out[...] = gv
            m2 = ADAM_B1 * m_ref[...] + (1.0 - ADAM_B1) * gv
            v2 = ADAM_B2 * v_ref[...] + (1.0 - ADAM_B2) * (gv * gv)
            m2_ref[...] = m2
            v2_ref[...] = v2
            d_ref[...] = -ADAM_LR * ((m2 / c1) / (jnp.sqrt(v2 / c2) + ADAM_EPS) + ADAM_WD * w_ref[...])

    blks = [pl.BlockSpec((w.shape[0] // steps, w.shape[1]), lambda i: (i, 0)) for w in ws]
    shapes = [jax.ShapeDtypeStruct(w.shape, F32) for w in ws]
    outs = pl.pallas_call(
        body, name=name, grid=(steps,), in_specs=blks * 4, out_specs=blks * 4, out_shape=shapes * 4,
        compiler_params=_params(("arbitrary",)),
    )(*ws, *gs, *ms, *vs)
    return [outs[j * n:(j + 1) * n] for j in range(4)]


WEIGHTS = ["ffn1_norm", "ffn1_w_gate", "ffn1_w_up", "ffn1_w_down", "mix_norm", "w_in", "conv_w", "conv_b",
           "rg_w_a", "rg_b_a", "rg_w_x", "rg_b_x", "rg_lambda", "q_norm", "k_norm", "rnn_out_norm",
           "attn_out_norm", "w_out", "ffn2_norm", "ffn2_w_gate", "ffn2_w_up", "ffn2_w_down"]
BIG = ["ffn1_w_gate", "ffn1_w_up", "ffn1_w_down", "w_in", "w_out", "ffn2_w_gate", "ffn2_w_up", "ffn2_w_down"]
SMALL = [n for n in WEIGHTS if n not in BIG]
PACK_LANES = 128
PACK_ROW_ALIGN = 8


def _hidden_major(name, a):
    return jnp.transpose(a) if name.endswith(("w_gate", "w_up")) else a


def _pack(parts):
    flat = jnp.concatenate([p.reshape(-1) for p in parts])
    unit = PACK_LANES * PACK_ROW_ALIGN
    padded = -(-flat.shape[0] // unit) * unit
    return jnp.pad(flat, (0, padded - flat.shape[0])).reshape(-1, PACK_LANES)


def _unpack(packed, shapes):
    flat = packed.reshape(-1)
    out, at = [], 0
    for shp in shapes:
        size = math.prod(shp)
        out.append(flat[at:at + size].reshape(shp))
        at += size
    return out


def kernel(x, ffn1_norm, ffn1_w_gate, ffn1_w_up, ffn1_w_down, mix_norm, w_in, conv_w, conv_b, rg_w_a, rg_b_a, rg_w_x, rg_b_x, rg_lambda, q_norm, k_norm, rnn_out_norm, attn_out_norm, w_out, ffn2_norm, ffn2_w_gate, ffn2_w_up, ffn2_w_down, loss_target, m_ffn1_norm, m_ffn1_w_gate, m_ffn1_w_up, m_ffn1_w_down, m_mix_norm, m_w_in, m_conv_w, m_conv_b, m_rg_w_a, m_rg_b_a, m_rg_w_x, m_rg_b_x, m_rg_lambda, m_q_norm, m_k_norm, m_rnn_out_norm, m_attn_out_norm, m_w_out, m_ffn2_norm, m_ffn2_w_gate, m_ffn2_w_up, m_ffn2_w_down, v_ffn1_norm, v_ffn1_w_gate, v_ffn1_w_up, v_ffn1_w_down, v_mix_norm, v_w_in, v_conv_w, v_conv_b, v_rg_w_a, v_rg_b_a, v_rg_w_x, v_rg_b_x, v_rg_lambda, v_q_norm, v_k_norm, v_rnn_out_norm, v_attn_out_norm, v_w_out, v_ffn2_norm, v_ffn2_w_gate, v_ffn2_w_up, v_ffn2_w_down):
    given = dict(locals())
    w = {n: given[n] for n in WEIGHTS}
    m = {n: given["m_" + n] for n in WEIGHTS}
    v = {n: given["v_" + n] for n in WEIGHTS}
    chip = 2 * lax.axis_index("x") + lax.axis_index("y")

    where = jnp.stack([lax.axis_index("c"), chip]).astype(jnp.int32)

    stacks = dict(zip(BIG, _place_shards([_hidden_major(n, w[n][0]) for n in BIG], where, "place_weights")))
    conv_stack = _place_shard(w["conv_w"][0], where, F32, "place_conv_w")
    small = {n: (w[n][0] if w[n].ndim > 2 else w[n]) for n in SMALL if n != "conv_w"}

    grad_x, slots, gs, everyone = _local_step(x[0], loss_target[0], stacks, conv_stack, small, where)

    swapped = _half_swap(_chip_sum([slots[n] for n in BIG], where, "chip_sums"))
    g2s = [t.reshape(t.shape[0] * t.shape[1], t.shape[2]) for t in swapped]
    flat = lambda tree: [_hidden_major(n, tree[n][0]) for n in BIG]
    g2s, d2s, m2s, v2s = _adamw(flat(w), g2s, flat(m), flat(v), "adamw_weights", ADAMW_STEPS)
    grads, deltas, new_m, new_v = {}, {}, {}, {}
    for tree, parts in ((grads, g2s), (deltas, d2s), (new_m, m2s), (new_v, v2s)):
        tree.update({n: _hidden_major(n, a).reshape(w[n].shape) for n, a in zip(BIG, parts)})

    full_shapes = [gs[n].shape for n in SMALL]
    *summed, loss = _unpack(_slot_sum(everyone, "small_grad_sum"), full_shapes + [(1, 1)])
    g_parts = dict(zip(SMALL, summed))
    quarter = D_RNN // N_CHIPS
    g_parts["conv_w"] = lax.dynamic_slice_in_dim(g_parts["conv_w"], chip * quarter, quarter, axis=1)
    local_shapes = [w[n].shape for n in SMALL]
    pk = lambda tree: _pack([tree[n] for n in SMALL])
    (g_s,), (d_s,), (m_s,), (v_s,) = _adamw([pk(w)], [pk(g_parts)], [pk(m)], [pk(v)], "adamw_small")
    for tree, packed in ((grads, g_s), (deltas, d_s), (new_m, m_s), (new_v, v_s)):
        tree.update(zip(SMALL, _unpack(packed, local_shapes)))

    return (loss[0, 0], grad_x.reshape(x.shape), *[grads[n] for n in WEIGHTS], *[deltas[n] for n in WEIGHTS],
            *[new_m[n] for n in WEIGHTS], *[new_v[n] for n in WEIGHTS])
```

```python
import functools
import math

import jax
import jax.numpy as jnp
from jax import lax
from jax.experimental import pallas as pl
from jax.experimental.pallas import tpu as pltpu

F32 = jnp.float32
BF16 = jnp.bfloat16
MESH = pl.DeviceIdType.MESH

D_MODEL = 1024
N_CHIPS = 4
D_RNN = 512
D_ATT = 512
N_HEADS = 8
HEAD_DIM = 64
RNN_BLOCKS = 8
CONV_W = 4
RG_C = 8.0
N_IN = 2 * D_RNN + 3 * D_ATT
EPS = 1e-6
ATT_BLOCK = 128
ATT_WINDOW = 384
ATT_SPLIT = 256
EXP_ZERO = -105.0

ADAM_LR = 0.001
ADAM_B1 = 0.9
ADAM_B2 = 0.999
ADAM_EPS = 1e-08
ADAM_WD = 0.01
ADAM_STEP = 10

V7X_VMEM_LIMIT = 60 * 1024 * 1024
V7X_MXU_WIDTH = 256
TOKEN_TILE = 512
SUBLANES = 8
BF16_ROWS = 16
FFN_TILE = 256
WGRAD_TILE = 2048
WHOLE_TILE = 1024
ADAMW_STEPS = 8

GELU_K0 = math.sqrt(2.0 / math.pi)
GELU_K1 = 0.044715


def _params(sem=None):
    return pltpu.CompilerParams(dimension_semantics=sem, vmem_limit_bytes=V7X_VMEM_LIMIT)


def _dot(a, b):
    return jnp.dot(a, b, preferred_element_type=F32)


def _dot_nt(a, b):
    return lax.dot_general(a, b, (((1,), (1,)), ((), ())), preferred_element_type=F32)


def _dot_tn(a, b):
    return lax.dot_general(a, b, (((0,), (0,)), ((), ())), preferred_element_type=F32)


def _sigmoid(x):
    return 1.0 / (1.0 + jnp.exp(-x))


def _rms_r(xv):
    return lax.rsqrt(jnp.mean(xv * xv, axis=-1, keepdims=True) + EPS)


def _rms_bwd(xv, r, nw, dh):
    t = dh * nw
    dx = r * t - xv * (r * r * r * jnp.mean(t * xv, axis=-1, keepdims=True))
    dn = jnp.sum(dh * xv * r, axis=0, keepdims=True)
    return dx, dn


def _gelu(x):
    t = jnp.tanh(GELU_K0 * (x + GELU_K1 * x * x * x))
    return 0.5 * x * (1.0 + t)


def _gelu_grad(x):
    t = jnp.tanh(GELU_K0 * (x + GELU_K1 * x * x * x))
    return 0.5 * (1.0 + t) + 0.5 * x * (1.0 - t * t) * (GELU_K0 * (1.0 + 3.0 * GELU_K1 * x * x))


def _expm1_neg(x):
    p = 1.0 + x * (1.0 / 6.0)
    for k in (5.0, 4.0, 3.0, 2.0):
        p = 1.0 + x * (1.0 / k) * p
    return jnp.where(x > -0.25, x * p, jnp.exp(x) - 1.0)


def _log_sigmoid(x):
    return jnp.minimum(x, 0.0) - jnp.log(1.0 + jnp.exp(-jnp.abs(x)))


def _tile(s):
    return min(TOKEN_TILE, s)


def _ffn_chunks(f):
    cut = f // 2 // V7X_MXU_WIDTH * V7X_MXU_WIDTH
    return ((0, cut), (cut, f)) if 0 < cut < f else ((0, f),)


def _ffn_fwd_loss(x, nw, wg, wu, wd, tgt):
    s, d = x.shape
    f = wg.shape[0]
    tm = min(FFN_TILE, s)
    ni = s // tm
    assert s % tm == 0

    def body(x_ref, nw_ref, wg_ref, wu_ref, wd_ref, tgt_ref, out_ref, g_ref, u_ref, hb_ref, ab_ref, loss_ref):
        i = pl.program_id(0)
        xv = x_ref[...]
        hb = (xv * _rms_r(xv) * nw_ref[...]).astype(BF16)
        hb_ref[...] = hb
        y = jnp.zeros((tm, d), F32)
        for lo, hi in _ffn_chunks(f):
            g = _dot_nt(hb, wg_ref[lo:hi, :])
            u = _dot_nt(hb, wu_ref[lo:hi, :])
            g_ref[:, lo:hi] = g.astype(BF16)
            u_ref[:, lo:hi] = u.astype(BF16)
            ab = (g * _sigmoid(g) * u).astype(BF16)
            ab_ref[:, lo:hi] = ab
            y = y + _dot(ab, wd_ref[lo:hi, :])
        diff = xv + 0.5 * y - tgt_ref[...]
        out_ref[...] = diff * (1.0 / d)

        @pl.when(i == 0)
        def _():
            loss_ref[...] = jnp.zeros_like(loss_ref)

        loss_ref[...] += jnp.sum(diff * diff) * (0.5 / d)

    row = pl.BlockSpec((tm, d), lambda i: (i, 0))
    weight = pl.BlockSpec((f, d), lambda i: (0, 0), pipeline_mode=pl.Buffered(1))
    blk = pl.BlockSpec((tm, f), lambda i: (i, 0))
    wide = jax.ShapeDtypeStruct((s, f), BF16)
    return _call(body, "ffn_fwd_loss", (ni,),
                 [row, pl.BlockSpec((1, d), lambda i: (0, 0)), weight, weight, weight, row],
                 [row, blk, blk, row, blk, pl.BlockSpec((1, 128), lambda i: (0, 0))],
                 [jax.ShapeDtypeStruct((s, d), F32), wide, wide, jax.ShapeDtypeStruct((s, d), BF16), wide,
                  jax.ShapeDtypeStruct((1, 128), F32)], [x, nw, wg, wu, wd, tgt])


def _ffn_up(x, nw, wg, wu, rider=None):
    s, d = x.shape
    f = wg.shape[0]
    tm = min(FFN_TILE, s)
    ni = s // tm
    assert s % tm == 0

    def body(*refs):
        (x_ref, nw_ref, wg_ref, wu_ref), (g_ref, u_ref, hb_ref, ab_ref), _, copies = _split_refs(refs, 4, 4, rider)
        i = pl.program_id(0)
        finish = _ride(copies, i == 0, i == ni - 1)
        xv = x_ref[...]
        hb = (xv * _rms_r(xv) * nw_ref[...]).astype(BF16)
        hb_ref[...] = hb
        for lo, hi in _ffn_chunks(f):
            g = _dot_nt(hb, wg_ref[lo:hi, :])
            u = _dot_nt(hb, wu_ref[lo:hi, :])
            g_ref[:, lo:hi] = g.astype(BF16)
            u_ref[:, lo:hi] = u.astype(BF16)
            ab_ref[:, lo:hi] = (g * _sigmoid(g) * u).astype(BF16)
        finish()

    row = pl.BlockSpec((tm, d), lambda i: (i, 0))
    weight = pl.BlockSpec((f, d), lambda i: (0, 0), pipeline_mode=pl.Buffered(1))
    blk = pl.BlockSpec((tm, f), lambda i: (i, 0))
    wide = jax.ShapeDtypeStruct((s, f), BF16)
    return _call(body, "ffn_up", (ni,), [row, pl.BlockSpec((1, d), lambda i: (0, 0)), weight, weight],
                 [blk, blk, row, blk], [wide, wide, jax.ShapeDtypeStruct((s, d), BF16), wide], [x, nw, wg, wu],
                 rider=rider)


def _ffn_down(x, ab, wd):
    s, d = x.shape
    f = wd.shape[0]
    tm = _tile(s)
    assert s % tm == 0

    def body(x_ref, ab_ref, wd_ref, out_ref):
        out_ref[...] = x_ref[...] + 0.5 * _dot(ab_ref[...], wd_ref[...])

    row = pl.BlockSpec((tm, d), lambda i: (i, 0))
    return _call(body, "ffn_down", (s // tm,),
                 [row, pl.BlockSpec((tm, f), lambda i: (i, 0)),
                  pl.BlockSpec((f, d), lambda i: (0, 0), pipeline_mode=pl.Buffered(1))],
                 [row], [jax.ShapeDtypeStruct((s, d), F32)], [x, ab, wd])[0]


def _call(body, name, grid, in_specs, out_specs, out_shape, args, scratch=(), rider=None):
    in_specs, out_specs, out_shape, scratch = list(in_specs), list(out_specs), list(out_shape), list(scratch)
    extra, aliases = [], {}
    if rider is not None:
        extra = rider.operands()
        aliases = rider.aliases(len(args), len(out_shape))
        in_specs += [ANY] * len(extra)
        out_specs += [ANY] * len(rider.inplace)
        out_shape += rider.out_shape()
        scratch += rider.scratch()
    return pl.pallas_call(
        body, name=name, grid=grid, in_specs=in_specs, out_specs=out_specs, out_shape=out_shape,
        input_output_aliases=aliases, scratch_shapes=scratch,
        compiler_params=_params(("arbitrary",) * len(grid)),
    )(*args, *extra)


def _ffn_bwd_act(x, nw, dy, g, u, wg, wu, wd, name):
    s, d = x.shape
    f = wg.shape[0]
    tm = min(FFN_TILE, s)
    assert s % tm == 0

    def body(x_ref, nw_ref, dy_ref, g_ref, u_ref, wg_ref, wu_ref, wd_ref,
             dx_ref, dg_ref, du_ref, dyb_ref, dnw_ref):
        dyv = dy_ref[...]
        dyb = dyv.astype(BF16)
        dyb_ref[...] = dyb
        dh = jnp.zeros((tm, d), F32)
        for lo, hi in _ffn_chunks(f):
            da = 0.5 * _dot_nt(dyb, wd_ref[lo:hi, :])
            gv = g_ref[:, lo:hi].astype(F32)
            sg = _sigmoid(gv)
            dub = (da * (gv * sg)).astype(BF16)
            dgb = (da * u_ref[:, lo:hi].astype(F32) * (sg * (1.0 + gv * (1.0 - sg)))).astype(BF16)
            dg_ref[:, lo:hi] = dgb
            du_ref[:, lo:hi] = dub
            dh = dh + _dot(dgb, wg_ref[lo:hi, :]) + _dot(dub, wu_ref[lo:hi, :])
        xv = x_ref[...]
        dx, dn = _rms_bwd(xv, _rms_r(xv), nw_ref[...], dh)
        dx_ref[...] = dyv + dx

        @pl.when(pl.program_id(0) == 0)
        def _():
            dnw_ref[...] = jnp.zeros_like(dnw_ref)

        dnw_ref[...] += dn

    row = pl.BlockSpec((tm, d), lambda i: (i, 0))
    vec = pl.BlockSpec((1, d), lambda i: (0, 0))
    blk = pl.BlockSpec((tm, f), lambda i: (i, 0))
    weight = pl.BlockSpec((f, d), lambda i: (0, 0), pipeline_mode=pl.Buffered(1))
    return _call(
        body, name, (s // tm,), [row, vec, row, blk, blk, weight, weight, weight], [row, blk, blk, row, vec],
        [jax.ShapeDtypeStruct((s, d), F32), jax.ShapeDtypeStruct((s, f), BF16),
         jax.ShapeDtypeStruct((s, f), BF16), jax.ShapeDtypeStruct((s, d), BF16),
         jax.ShapeDtypeStruct((1, d), F32)],
        [x, nw, dy, g, u, wg, wu, wd])


def _wgrad(a, b, a_spec, b_spec, out_rows, out_cols, scale, name, tk, rider=None, per_step=1):
    s = a.shape[-2]
    nk = s // tk
    steps = N_CHIPS // per_step
    assert s % tk == 0

    def body(*refs):
        (a_ref, b_ref), (out_ref,), (acc,), copies = _split_refs(refs, 2, 1, rider)
        j, k = pl.program_id(0), pl.program_id(1)
        finish = _ride(copies, jnp.logical_and(j == 0, k == 0), jnp.logical_and(j == steps - 1, k == nk - 1))

        @pl.when(k == 0)
        def _():
            acc[...] = jnp.zeros_like(acc)

        acc[...] += _dot_tn(a_ref[...], b_ref[...])

        @pl.when(k == nk - 1)
        def _():
            for t in range(per_step):
                out_ref[t] = (acc[t * out_rows:(t + 1) * out_rows, :] * scale).astype(BF16)

        finish()

    outs = _call(
        body, name, (steps, nk), [a_spec(tk), b_spec(tk)],
        [pl.BlockSpec((per_step, out_rows, out_cols), lambda j, k: (j, 0, 0))],
        [jax.ShapeDtypeStruct((N_CHIPS, out_rows, out_cols), BF16)], [a, b],
        scratch=[pltpu.VMEM((per_step * out_rows, out_cols), F32)], rider=rider)
    return outs[0] if rider is None else outs


def _beside(refs):
    return jnp.concatenate([r[...] for r in refs], axis=1) if len(refs) > 1 else refs[0][...]


def _wgrad_whole(a, bs, col_blocks, name, rider=None):
    s, m = a.shape
    n = sum(b.shape[1] for b in bs)
    tk = min(WHOLE_TILE, s)
    nk = s // tk
    assert s % tk == 0
    out_shape = (N_CHIPS, m, n // N_CHIPS) if col_blocks else (N_CHIPS, m // N_CHIPS, n)

    def body(*refs):
        (a_ref, *b_refs), (out_ref,), (acc,), copies = _split_refs(refs, 1 + len(bs), 1, rider)
        k = pl.program_id(0)
        finish = _ride(copies, k == 0, k == nk - 1)

        @pl.when(k == 0)
        def _():
            acc[...] = jnp.zeros_like(acc)

        acc[...] += _dot_tn(a_ref[...], _beside(b_refs))

        @pl.when(k == nk - 1)
        def _():
            for j in range(N_CHIPS):
                if col_blocks:
                    out_ref[j] = acc[:, j * out_shape[2]:(j + 1) * out_shape[2]].astype(BF16)
                else:
                    out_ref[j] = acc[j * out_shape[1]:(j + 1) * out_shape[1], :].astype(BF16)

        finish()

    outs = _call(
        body, name, (nk,),
        [pl.BlockSpec((tk, m), lambda k: (k, 0))] + [pl.BlockSpec((tk, b.shape[1]), lambda k: (k, 0)) for b in bs],
        [pl.BlockSpec(out_shape, lambda k: (0, 0, 0))], [jax.ShapeDtypeStruct(out_shape, BF16)], [a, *bs],
        scratch=[pltpu.VMEM((m, n), F32)], rider=rider)
    return outs[0] if rider is None else outs


def _ffn_wgrad(hidden, shared, scale, name, rider=None):
    s, d = shared.shape
    half = hidden.shape[1] // 2
    return _wgrad(hidden, shared, lambda tk: pl.BlockSpec((tk, half), lambda j, k: (k, j)),
                  lambda tk: pl.BlockSpec((tk, d), lambda j, k: (k, 0)), half // 2, d, scale, name,
                  min(WGRAD_TILE, s), rider, per_step=2)


def _mix_pre(x, nw, win, rider=None):
    s, d = x.shape
    nb, _, cb = win.shape
    tm = _tile(s)
    ni = s // tm
    assert s % tm == 0

    def body(*refs):
        (x_ref, nw_ref, w_ref), (p_ref, hb_ref), _, copies = _split_refs(refs, 3, 2, rider)
        finish = _ride(copies, pl.program_id(0) == 0, pl.program_id(0) == ni - 1)
        xv = x_ref[...]
        hb = (xv * _rms_r(xv) * nw_ref[...]).astype(BF16)
        hb_ref[...] = hb
        for j in range(nb):
            p_ref[:, j * cb:(j + 1) * cb] = _dot(hb, w_ref[j])
        finish()

    row = pl.BlockSpec((tm, d), lambda i: (i, 0))
    return _call(
        body, "mix_pre", (ni,),
        [row, pl.BlockSpec((1, d), lambda i: (0, 0)),
         pl.BlockSpec((nb, d, cb), lambda i: (0, 0, 0), pipeline_mode=pl.Buffered(1))],
        [pl.BlockSpec((tm, nb * cb), lambda i: (i, 0)), row],
        [jax.ShapeDtypeStruct((s, nb * cb), F32), jax.ShapeDtypeStruct((s, d), BF16)], [x, nw, win], rider=rider)


def _mix_pre_bwd(x, nw, dres, dps, win):
    s, d = x.shape
    nb, _, cb = win.shape
    tm = _tile(s)
    assert s % tm == 0 and sum(p.shape[1] for p in dps) == nb * cb

    def body(x_ref, nw_ref, dres_ref, *rest):
        *dp_refs, w_ref, dx_ref, dnw_ref = rest
        dp = _beside(dp_refs)
        dh = jnp.zeros((tm, d), F32)
        for j in range(nb):
            dh = dh + _dot_nt(dp[:, j * cb:(j + 1) * cb], w_ref[j])
        xv = x_ref[...]
        dx, dn = _rms_bwd(xv, _rms_r(xv), nw_ref[...], dh)
        dx_ref[...] = dres_ref[...] + dx

        @pl.when(pl.program_id(0) == 0)
        def _():
            dnw_ref[...] = jnp.zeros_like(dnw_ref)

        dnw_ref[...] += dn

    row = pl.BlockSpec((tm, d), lambda i: (i, 0))
    vec = pl.BlockSpec((1, d), lambda i: (0, 0))
    return pl.pallas_call(
        body, name="mix_pre_bwd", grid=(s // tm,),
        in_specs=[row, vec, row] + [pl.BlockSpec((tm, p.shape[1]), lambda i: (i, 0)) for p in dps]
        + [pl.BlockSpec((nb, d, cb), lambda i: (0, 0, 0), pipeline_mode=pl.Buffered(1))],
        out_specs=[row, vec],
        out_shape=[jax.ShapeDtypeStruct((s, d), F32), jax.ShapeDtypeStruct((1, d), F32)],
        compiler_params=_params(("arbitrary",)),
    )(x, nw, dres, *dps, win)


def _mix_post(x, yr, ya, nr, na, wout):
    s, d = x.shape
    h = yr.shape[1]
    tm = _tile(s)

    def body(x_ref, yr_ref, ya_ref, nr_ref, na_ref, w_ref, out_ref):
        yrv = yr_ref[...]
        yav = ya_ref[...]
        onb = (yrv * _rms_r(yrv) * nr_ref[...]).astype(BF16)
        oab = (yav * _rms_r(yav) * na_ref[...]).astype(BF16)
        out_ref[...] = x_ref[...] + _dot(onb, w_ref[0:h, :]) + _dot(oab, w_ref[h:2 * h, :])

    row = pl.BlockSpec((tm, d), lambda i: (i, 0))
    half = pl.BlockSpec((tm, h), lambda i: (i, 0))
    vec = pl.BlockSpec((1, h), lambda i: (0, 0))
    return pl.pallas_call(
        body, name="mix_post", grid=(s // tm,),
        in_specs=[row, half, half, vec, vec, pl.BlockSpec((2 * h, d), lambda i: (0, 0))],
        out_specs=row, out_shape=jax.ShapeDtypeStruct((s, d), F32),
        compiler_params=_params(("arbitrary",)),
    )(x, yr, ya, nr, na, wout)


def _mix_post_bwd(dx, yr, ya, nr, na, wout):
    s, d = dx.shape
    h = yr.shape[1]
    tm = _tile(s)

    def body(dx_ref, yr_ref, ya_ref, nr_ref, na_ref, w_ref,
             dyr_ref, dya_ref, yc_ref, dxb_ref, dnr_ref, dna_ref):
        i = pl.program_id(0)
        dxb = dx_ref[...].astype(BF16)
        dxb_ref[...] = dxb
        dyc = _dot_nt(dxb, w_ref[...])
        yrv = yr_ref[...]
        yav = ya_ref[...]
        rr = _rms_r(yrv)
        ra = _rms_r(yav)
        yc_ref[:, 0:h] = (yrv * rr * nr_ref[...]).astype(BF16)
        yc_ref[:, h:2 * h] = (yav * ra * na_ref[...]).astype(BF16)
        dyr, dnr = _rms_bwd(yrv, rr, nr_ref[...], dyc[:, 0:h])
        dya, dna = _rms_bwd(yav, ra, na_ref[...], dyc[:, h:2 * h])
        dyr_ref[...] = dyr
        dya_ref[...] = dya

        @pl.when(i == 0)
        def _():
            dnr_ref[...] = jnp.zeros_like(dnr_ref)
            dna_ref[...] = jnp.zeros_like(dna_ref)

        dnr_ref[...] += dnr
        dna_ref[...] += dna

    row = pl.BlockSpec((tm, d), lambda i: (i, 0))
    half = pl.BlockSpec((tm, h), lambda i: (i, 0))
    vec = pl.BlockSpec((1, h), lambda i: (0, 0))
    return pl.pallas_call(
        body, name="mix_post_bwd", grid=(s // tm,),
        in_specs=[row, half, half, vec, vec, pl.BlockSpec((2 * h, d), lambda i: (0, 0))],
        out_specs=[half, half, pl.BlockSpec((tm, 2 * h), lambda i: (i, 0)), row, vec, vec],
        out_shape=[jax.ShapeDtypeStruct((s, h), F32), jax.ShapeDtypeStruct((s, h), F32),
                   jax.ShapeDtypeStruct((s, 2 * h), BF16), jax.ShapeDtypeStruct((s, d), BF16),
                   jax.ShapeDtypeStruct((1, h), F32), jax.ShapeDtypeStruct((1, h), F32)],
        compiler_params=_params(("arbitrary",)),
    )(dx, yr, ya, nr, na, wout)


def _shift_down(xv, s, prev8):
    rolled = pltpu.roll(xv, s, 0)
    row8 = lax.broadcasted_iota(jnp.int32, prev8.shape, 0)
    head = jnp.where(row8 < s, pltpu.roll(prev8, s, 0), rolled[0:8, :])
    return jnp.concatenate([head, rolled[8:, :]], axis=0)


def _shift_up(xv, s, next8):
    n = xv.shape[0]
    rolled = pltpu.roll(xv, n - s, 0)
    row8 = lax.broadcasted_iota(jnp.int32, next8.shape, 0)
    tail = jnp.where(row8 >= 8 - s, pltpu.roll(next8, 8 - s, 0), rolled[n - 8:, :])
    return jnp.concatenate([rolled[:n - 8, :], tail], axis=0)


def _scan_fwd(a, b):
    n = a.shape[0]
    sub = lax.broadcasted_iota(jnp.int32, a.shape, 0) % SUBLANES
    s = 1
    while s < SUBLANES:
        ok = sub >= s
        b = jnp.where(ok, a * pltpu.roll(b, s, 0) + b, b)
        a = jnp.where(ok, a * pltpu.roll(a, s, 0), a)
        s *= 2
    groups = []
    before = jnp.zeros((1, a.shape[1]), F32)
    for g in range(n // SUBLANES):
        rows = slice(g * SUBLANES, (g + 1) * SUBLANES)
        groups.append(a[rows] * before + b[rows])
        before = groups[-1][SUBLANES - 1:]
    return jnp.concatenate(groups, axis=0)


def _scan_bwd(a, b):
    n = a.shape[0]
    sub = lax.broadcasted_iota(jnp.int32, a.shape, 0) % SUBLANES
    s = 1
    while s < SUBLANES:
        ok = sub < SUBLANES - s
        b = jnp.where(ok, a * pltpu.roll(b, n - s, 0) + b, b)
        a = jnp.where(ok, a * pltpu.roll(a, n - s, 0), a)
        s *= 2
    groups = []
    after = jnp.zeros((1, a.shape[1]), F32)
    for g in reversed(range(n // SUBLANES)):
        rows = slice(g * SUBLANES, (g + 1) * SUBLANES)
        groups.append(a[rows] * after + b[rows])
        after = groups[-1][:1]
    return jnp.concatenate(groups[::-1], axis=0)


def _rglru_gates(xv, prev8, cw_ref, cb_ref, wa_ref, ba_ref, wx_ref, bx_ref, lam_ref):
    x1 = _shift_down(xv, 1, prev8)
    x2 = _shift_down(xv, 2, prev8)
    x3 = _shift_down(xv, 3, prev8)
    xc = cw_ref[3:4, :] * xv + cw_ref[2:3, :] * x1 + cw_ref[1:2, :] * x2 + cw_ref[0:1, :] * x3 + cb_ref[...]
    xcb = xc.astype(BF16)
    r = _sigmoid(_dot(xcb, wa_ref[...]) + ba_ref[...])
    ig = _sigmoid(_dot(xcb, wx_ref[...]) + bx_ref[...])
    c = RG_C * _log_sigmoid(lam_ref[...])
    la = r * c
    a = jnp.exp(la)
    m = jnp.sqrt(-_expm1_neg(2.0 * la))
    return (x1, x2, x3), xc, xcb, r, ig, c, a, m


def _rglru_fwd(proj, cw, cb, wa, ba, wx, bx, lam, rider=None):
    s = proj.shape[0]
    w = D_RNN
    tm = _tile(s)
    ni = s // tm

    def body(*refs):
        ins, (y_ref, h_ref), (prev, hlast), copies = _split_refs(refs, 9, 2, rider)
        xr_ref, gate_ref, cw_ref, cb_ref, wa_ref, ba_ref, wx_ref, bx_ref, lam_ref = ins
        finish = _ride(copies, pl.program_id(0) == 0, pl.program_id(0) == ni - 1)

        @pl.when(pl.program_id(0) == 0)
        def _():
            prev[...] = jnp.zeros_like(prev)
            hlast[...] = jnp.zeros_like(hlast)

        xv = xr_ref[...]
        _, xc, _, _, ig, _, a, m = _rglru_gates(xv, prev[...], cw_ref, cb_ref, wa_ref, ba_ref,
                                                wx_ref, bx_ref, lam_ref)
        b = m * (ig * xc)
        row = lax.broadcasted_iota(jnp.int32, b.shape, 0)
        b = jnp.where(row == 0, b + a * hlast[...], b)
        h = _scan_fwd(a, b)
        h_ref[...] = h
        y_ref[...] = h * _gelu(gate_ref[...])
        prev[...] = xv[tm - 8:, :]
        hlast[...] = h[tm - 1:tm, :]
        finish()

    vec = pl.BlockSpec((1, w), lambda i: (0, 0))
    sq = pl.BlockSpec((w, w), lambda i: (0, 0))
    out = pl.BlockSpec((tm, w), lambda i: (i, 0))
    return _call(
        body, "rglru_fwd", (ni,),
        [pl.BlockSpec((tm, w), lambda i: (i, 0)), pl.BlockSpec((tm, w), lambda i: (i, 1)),
         pl.BlockSpec((CONV_W, w), lambda i: (0, 0)), vec, sq, vec, sq, vec, vec], [out, out],
        [jax.ShapeDtypeStruct((s, w), F32), jax.ShapeDtypeStruct((s, w), F32)],
        [proj, proj, cw, cb, wa, ba, wx, bx, lam],
        scratch=[pltpu.VMEM((8, w), F32), pltpu.VMEM((1, w), F32)], rider=rider)


def _rglru_bwd(proj, hseq, dyr, cw, cb, wa, ba, wx, bx, lam):
    s = proj.shape[0]
    w = D_RNN
    tm = _tile(s)
    nt = s // tm
    t8 = tm // 8

    def body(xr_ref, xp_ref, gate_ref, h_ref, hp_ref, dy_ref, cw_ref, cb_ref, wa_ref, ba_ref,
             wx_ref, bx_ref, lam_ref,
             dxr_ref, dgate_ref, dcw_ref, dcb_ref, dwa_ref, dba_ref, dwx_ref, dbx_ref, dlam_ref,
             carry, dxc_next):
        i = pl.program_id(0)
        first_tile = i == nt - 1

        @pl.when(i == 0)
        def _():
            carry[...] = jnp.zeros_like(carry)
            dxc_next[...] = jnp.zeros_like(dxc_next)
            for ref in (dcw_ref, dcb_ref, dwa_ref, dba_ref, dwx_ref, dbx_ref, dlam_ref):
                ref[...] = jnp.zeros_like(ref)

        xv = xr_ref[...]
        prev8 = jnp.where(first_tile, 0.0, xp_ref[...])
        hprev8 = jnp.where(first_tile, 0.0, hp_ref[...])
        (x1, x2, x3), xc, xcb, r, ig, c, a, m = _rglru_gates(
            xv, prev8, cw_ref, cb_ref, wa_ref, ba_ref, wx_ref, bx_ref, lam_ref)
        gv = gate_ref[...]
        hv = h_ref[...]
        dy = dy_ref[...]
        dgate_ref[...] = (dy * hv * _gelu_grad(gv)).astype(BF16)
        dh = dy * _gelu(gv)
        row = lax.broadcasted_iota(jnp.int32, dh.shape, 0)
        dh = jnp.where(row == tm - 1, dh + carry[...], dh)
        a_up = jnp.where(row == tm - 1, 0.0, pltpu.roll(a, tm - 1, 0))
        lam_t = _scan_bwd(a_up, dh)
        carry[...] = a[0:1, :] * lam_t[0:1, :]
        hm1 = _shift_down(hv, 1, hprev8)
        da = lam_t * hm1
        ixc = ig * xc
        dm = lam_t * ixc
        dig = lam_t * m * xc
        dxc = lam_t * m * ig
        dla = da * a - dm * (a * a) / m
        dr = dla * c
        dlam_ref[...] += jnp.sum(dla * r, axis=0, keepdims=True)
        dpa = dr * r * (1.0 - r)
        dpi = dig * ig * (1.0 - ig)
        dba_ref[...] += jnp.sum(dpa, axis=0, keepdims=True)
        dbx_ref[...] += jnp.sum(dpi, axis=0, keepdims=True)
        dpab = dpa.astype(BF16)
        dpib = dpi.astype(BF16)
        dwa_ref[...] += _dot_tn(xcb, dpab)
        dwx_ref[...] += _dot_tn(xcb, dpib)
        dxc = dxc + _dot_nt(dpab, wa_ref[...]) + _dot_nt(dpib, wx_ref[...])
        dcb_ref[...] += jnp.sum(dxc, axis=0, keepdims=True)
        dcw_ref[3:4, :] += jnp.sum(dxc * xv, axis=0, keepdims=True)
        dcw_ref[2:3, :] += jnp.sum(dxc * x1, axis=0, keepdims=True)
        dcw_ref[1:2, :] += jnp.sum(dxc * x2, axis=0, keepdims=True)
        dcw_ref[0:1, :] += jnp.sum(dxc * x3, axis=0, keepdims=True)
        nxt = dxc_next[...]
        dxr = (cw_ref[3:4, :] * dxc + cw_ref[2:3, :] * _shift_up(dxc, 1, nxt)
               + cw_ref[1:2, :] * _shift_up(dxc, 2, nxt) + cw_ref[0:1, :] * _shift_up(dxc, 3, nxt))
        dxr_ref[...] = dxr.astype(BF16)
        dxc_next[...] = dxc[0:8, :]

        @pl.when(first_tile)
        def _():
            lv = lam_ref[...]
            dlam_ref[...] = dlam_ref[...] * (RG_C * _sigmoid(-lv))

    rev = lambda i: nt - 1 - i
    vec = pl.BlockSpec((1, w), lambda i: (0, 0))
    sq = pl.BlockSpec((w, w), lambda i: (0, 0))
    cur = lambda col: pl.BlockSpec((tm, w), lambda i: (rev(i), col))
    before = lambda cols: pl.BlockSpec((8, w), lambda i: (jnp.maximum(rev(i) * t8 - 1, 0), 0))
    return pl.pallas_call(
        body, name="rglru_bwd", grid=(nt,),
        in_specs=[cur(0), before(None), cur(1), cur(0), before(None), cur(0),
                  pl.BlockSpec((CONV_W, w), lambda i: (0, 0)), vec, sq, vec, sq, vec, vec],
        out_specs=[cur(0), cur(0), pl.BlockSpec((CONV_W, w), lambda i: (0, 0)), vec, sq, vec, sq, vec, vec],
        out_shape=[jax.ShapeDtypeStruct((s, w), BF16), jax.ShapeDtypeStruct((s, w), BF16),
                   jax.ShapeDtypeStruct((CONV_W, w), F32), jax.ShapeDtypeStruct((1, w), F32),
                   jax.ShapeDtypeStruct((w, w), F32), jax.ShapeDtypeStruct((1, w), F32),
                   jax.ShapeDtypeStruct((w, w), F32), jax.ShapeDtypeStruct((1, w), F32),
                   jax.ShapeDtypeStruct((1, w), F32)],
        scratch_shapes=[pltpu.VMEM((1, w), F32), pltpu.VMEM((8, w), F32)],
        compiler_params=_params(("arbitrary",)),
    )(proj, proj, proj, hseq, hseq, dyr, cw, cb, wa, ba, wx, bx, lam)


def _sb_logs(z, valid):
    lb = jnp.minimum(z, 0.0) - jnp.log(1.0 + jnp.exp(-jnp.abs(z)))
    return lb, jnp.where(valid, lb - z, 0.0)


class _Window:
    def __init__(self):
        blk, win, cut = ATT_BLOCK, ATT_WINDOW, ATT_SPLIT
        self.row = lax.broadcasted_iota(jnp.int32, (blk, win), 0)
        self.col = lax.broadcasted_iota(jnp.int32, (blk, win), 1)

        def tri(n, later):
            j = lax.broadcasted_iota(jnp.int32, (n, n), 0)
            s = lax.broadcasted_iota(jnp.int32, (n, n), 1)
            return jnp.where((j > s) if later else (j < s), 1.0, 0.0).astype(BF16)

        self.later = (tri(cut, True), tri(win - cut, True))
        self.earlier = (tri(cut, False), tri(win - cut, False))

    def place(self, qi, g):
        end = (qi + 1) * ATT_BLOCK - g * ATT_WINDOW
        start = pl.multiple_of(jnp.maximum(end - ATT_WINDOW, 0), ATT_BLOCK)
        valid = self.col < jnp.minimum(self.row + (qi * ATT_BLOCK - start), end - start)
        return start, valid

    @staticmethod
    def _parts(xv):
        hi = xv.astype(BF16)
        lo = (xv - hi.astype(F32)).astype(BF16)
        cut = ATT_SPLIT
        sums = (jnp.sum(xv[:, :cut], axis=1, keepdims=True), jnp.sum(xv[:, cut:], axis=1, keepdims=True))
        return (hi[:, :cut], lo[:, :cut]), (hi[:, cut:], lo[:, cut:]), sums

    def sums_after(self, xv, carry):
        (h0, l0), (h1, l1), (s0, s1) = self._parts(xv)
        first = _dot(h0, self.later[0]) + _dot(l0, self.later[0]) + (s1 + carry)
        last = _dot(h1, self.later[1]) + _dot(l1, self.later[1]) + carry
        return jnp.concatenate([first, last], axis=1), s0 + s1

    def sums_before(self, xv, carry):
        (h0, l0), (h1, l1), (s0, s1) = self._parts(xv)
        first = _dot(h0, self.earlier[0]) + _dot(l0, self.earlier[0]) + carry
        last = _dot(h1, self.earlier[1]) + _dot(l1, self.earlier[1]) + (s0 + carry)
        return jnp.concatenate([first, last], axis=1), s0 + s1


class _HeadPair:
    def __init__(self):
        lanes = 2 * HEAD_DIM
        lane = lax.broadcasted_iota(jnp.int32, (1, lanes), 1)
        self.masks = [lane // HEAD_DIM == h for h in (0, 1)]
        i = lax.broadcasted_iota(jnp.int32, (lanes, lanes), 0) // HEAD_DIM
        j = lax.broadcasted_iota(jnp.int32, (lanes, lanes), 1) // HEAD_DIM
        self.same_head = jnp.where(i == j, 1.0, 0.0).astype(BF16)

    def only(self, h, xv):
        return jnp.where(self.masks[h], xv, jnp.zeros_like(xv))

    def merge(self, per_head):
        return jnp.where(self.masks[0], per_head[0], per_head[1])

    def mean(self, xv):
        hi = xv.astype(BF16)
        lo = (xv - hi.astype(F32)).astype(BF16)
        return (_dot(hi, self.same_head) + _dot(lo, self.same_head)) * (1.0 / HEAD_DIM)

    def rms_r(self, xv):
        return lax.rsqrt(self.mean(xv * xv) + EPS)

    def rms_bwd(self, xv, r, nw, dh):
        t = dh * nw
        dx = r * t - xv * (r * r * r * self.mean(t * xv))
        dn = jnp.sum(dh * xv * r, axis=0, keepdims=True)
        return dx, dn[:, :HEAD_DIM] + dn[:, HEAD_DIM:]


def _attn_fwd(proj, qg, kg, rider=None):
    s = proj.shape[0]
    blk, win, dh = ATT_BLOCK, ATT_WINDOW, HEAD_DIM
    nq = s // blk
    scale = 1.0 / math.sqrt(dh)
    heads = (0, 1)
    blocks = (0, 1)
    assert s >= win and s % (blk * len(blocks)) == 0

    def body(*refs):
        (q_ref, k_ref, v_ref, qg_ref, kg_ref), (o_ref,), (qn, kn, vb), copies = _split_refs(refs, 5, 1, rider)
        finish = _ride(copies, pl.program_id(0) == 0, pl.program_id(0) == N_HEADS // 2 - 1)
        wd, hp = _Window(), _HeadPair()
        qv = q_ref[...]
        qn[...] = (qv * hp.rms_r(qv) * qg_ref[...] * scale).astype(BF16)
        kv = k_ref[...]
        kn[...] = (kv * hp.rms_r(kv) * kg_ref[...]).astype(BF16)
        vb[...] = v_ref[...].astype(BF16)

        def q_step(pair_i, _):
            qis = [2 * pair_i + b for b in blocks]
            chains = [(b, h) for b in blocks for h in heads]
            qoffs = [pl.multiple_of(qi * blk, blk) for qi in qis]
            qtiles = [qn[pl.ds(qoff, blk), :] for qoff in qoffs]
            qts = [hp.only(h, qtiles[b]) for b, h in chains]

            def more(carry):
                g, live = carry[:2]
                return jnp.logical_and((qis[-1] + 1) * blk - g * win > 0, live > 0)

            def window(carry):
                g, _, accs, runs = carry
                places = [wd.place(qi, g) for qi in qis]
                kts = [kn[pl.ds(start, win), :] for start, _ in places]
                zs = [_dot_nt(qts[c], kts[b]) for c, (b, h) in enumerate(chains)]
                logs = [_sb_logs(zs[c], places[b][1]) for c, (b, h) in enumerate(chains)]
                sums = [wd.sums_after(logs[c][1], runs[c]) for c in range(len(chains))]
                wgts = [jnp.where(places[b][1], jnp.exp(logs[c][0] + sums[c][0]), 0.0).astype(BF16)
                        for c, (b, h) in enumerate(chains)]
                vts = [vb[pl.ds(start, win), :] for start, _ in places]
                accs = tuple(accs[c] + _dot(wgts[c], vts[b]) for c, (b, h) in enumerate(chains))
                runs = tuple(runs[c] + sums[c][1] for c in range(len(chains)))
                top = functools.reduce(jnp.maximum, [jnp.max(r) for r in runs])
                return g + 1, (top > EXP_ZERO).astype(jnp.int32), accs, runs

            zero = lambda cols: tuple(jnp.zeros((blk, cols), F32) for _ in chains)
            _, _, accs, _ = lax.while_loop(more, window, (jnp.int32(0), jnp.int32(1), zero(2 * dh), zero(1)))
            for b in blocks:
                o_ref[pl.ds(qoffs[b], blk), :] = hp.merge([accs[2 * b + h] for h in heads])
            return 0

        lax.fori_loop(0, nq // len(blocks), q_step, 0)
        finish()

    pair = lambda group: pl.BlockSpec((s, 2 * dh), lambda p: (0, group * (D_ATT // (2 * dh)) + p))
    vec = pl.BlockSpec((1, 2 * dh), lambda p: (0, 0))
    return _call(
        body, "attn_fwd", (N_HEADS // 2,), [pair(2), pair(3), pair(4), vec, vec], [pair(0)],
        [jax.ShapeDtypeStruct((s, D_ATT), F32)], [proj, proj, proj, jnp.tile(qg, (1, 2)), jnp.tile(kg, (1, 2))],
        scratch=[pltpu.VMEM((s, 2 * dh), BF16)] * 3, rider=rider)


def _attn_bwd(proj, dya, qg, kg, rider=None):
    s = proj.shape[0]
    blk, win, dh = ATT_BLOCK, ATT_WINDOW, HEAD_DIM
    nq = s // blk
    max_windows = -(-s // win) + 1
    scale = 1.0 / math.sqrt(dh)
    steps = N_HEADS // 2
    heads = (0, 1)
    blocks = (0, 1)
    assert s >= win and s % (blk * len(blocks)) == 0

    def body(*refs):
        ins, outs, scratch, copies = _split_refs(refs, 6, 5, rider)
        q_ref, k_ref, v_ref, do_ref, qg_ref, kg_ref = ins
        dq_ref, dk_ref, dv_ref, dqg_ref, dkg_ref = outs
        qn, kn, vb, dob, runs_ref, dqn, dkn, dvn = scratch
        finish = _ride(copies, pl.program_id(0) == 0, pl.program_id(0) == steps - 1)
        wd, hp = _Window(), _HeadPair()

        @pl.when(pl.program_id(0) == 0)
        def _():
            dqg_ref[...] = jnp.zeros_like(dqg_ref)
            dkg_ref[...] = jnp.zeros_like(dkg_ref)

        qv = q_ref[...]
        qn[...] = (qv * hp.rms_r(qv) * qg_ref[...] * scale).astype(BF16)
        kv = k_ref[...]
        kn[...] = (kv * hp.rms_r(kv) * kg_ref[...]).astype(BF16)
        vb[...] = v_ref[...].astype(BF16)
        dob[...] = do_ref[...].astype(BF16)
        dkn[...] = jnp.zeros_like(dkn)
        dvn[...] = jnp.zeros_like(dvn)

        def q_step(pair_i, _):
            qis = [2 * pair_i + b for b in blocks]
            chains = [(b, h) for b in blocks for h in heads]
            ids = range(len(chains))
            qoffs = [pl.multiple_of(qi * blk, blk) for qi in qis]
            qts = [hp.only(h, qn[pl.ds(qoffs[b], blk), :]) for b, h in chains]
            dots = [hp.only(h, dob[pl.ds(qoffs[b], blk), :]) for b, h in chains]

            zero = lambda cols: tuple(jnp.zeros((blk, cols), F32) for _ in chains)

            def logs_of(g):
                places = [wd.place(qi, g) for qi in qis]
                kts = [kn[pl.ds(start, win), :] for start, _ in places]
                return [_sb_logs(_dot_nt(qts[c], kts[b]), places[b][1]) for c, (b, h) in enumerate(chains)]

            def row_sums(logs):
                return tuple(jnp.sum(logs[c][1], axis=1, keepdims=True) for c in ids)

            def still_live(runs):
                return functools.reduce(jnp.maximum, [jnp.max(r) for r in runs]) > EXP_ZERO

            def window_grads(g, logs, runs, esums):
                places = [wd.place(qi, g) for qi in qis]
                kts = [kn[pl.ds(start, win), :] for start, _ in places]
                vts = [vb[pl.ds(start, win), :] for start, _ in places]
                dws = [_dot_nt(dots[c], vts[b]) for c, (b, h) in enumerate(chains)]
                tails = [wd.sums_after(logs[c][1], runs[c])[0] for c in ids]
                wgts = [jnp.where(places[b][1], jnp.exp(logs[c][0] + tails[c]), 0.0) for c, (b, h) in enumerate(chains)]
                es = [dws[c] * wgts[c] for c in ids]
                befores = [wd.sums_before(es[c], esums[c]) for c in ids]
                dzbs = []
                for c, (b, h) in enumerate(chains):
                    beta = jnp.exp(logs[c][0])
                    dz = jnp.where(places[b][1], es[c] * (1.0 - beta) - befores[c][0] * beta, 0.0)
                    dzbs.append(dz.astype(BF16))
                for b in blocks:
                    rows = pl.ds(places[b][0], win)
                    dkn[rows, :] += _dot_tn(dzbs[2 * b], qts[2 * b]) + _dot_tn(dzbs[2 * b + 1], qts[2 * b + 1])
                    dvn[rows, :] += (_dot_tn(wgts[2 * b].astype(BF16), dots[2 * b])
                                     + _dot_tn(wgts[2 * b + 1].astype(BF16), dots[2 * b + 1]))
                return (tuple(_dot(dzbs[c], kts[b]) for c, (b, h) in enumerate(chains)),
                        tuple(befores[c][1] for c in ids))

            logs0 = logs_of(0)
            runs1 = row_sums(logs0)

            def one_window():
                return window_grads(0, logs0, zero(1), zero(1))[0]

            def all_windows():
                def more(carry):
                    g, live = carry[:2]
                    return jnp.logical_and((qis[-1] + 1) * blk - g * win > 0, live > 0)

                def run_window(carry):
                    g, _, runs = carry
                    for c in ids:
                        runs_ref[c, g] = runs[c]
                    sums = row_sums(logs_of(g))
                    runs = tuple(runs[c] + sums[c] for c in ids)
                    return g + 1, still_live(runs).astype(jnp.int32), runs

                for c in ids:
                    runs_ref[c, 0] = jnp.zeros((blk, 1), F32)
                windows, _, _ = lax.while_loop(more, run_window, (jnp.int32(1), jnp.int32(1), runs1))

                def k_window(gg, carry):
                    dq_accs, esums = carry
                    g = windows - 1 - gg
                    parts, totals = window_grads(g, logs_of(g), [runs_ref[c, g] for c in ids], esums)
                    return (tuple(dq_accs[c] + parts[c] for c in ids), tuple(esums[c] + totals[c] for c in ids))

                return lax.fori_loop(0, windows, k_window, (zero(2 * dh), zero(1)))[0]

            earlier_keys = (qis[-1] + 1) * blk - win > 0
            dq_accs = lax.cond(jnp.logical_and(earlier_keys, still_live(runs1)), all_windows, one_window)
            for b in blocks:
                dqn[pl.ds(qoffs[b], blk), :] = hp.merge([dq_accs[2 * b + h] for h in heads])
            return 0

        lax.fori_loop(0, nq // len(blocks), q_step, 0)

        dq, dqg = hp.rms_bwd(qv, hp.rms_r(qv), qg_ref[...] * scale, dqn[...])
        dq_ref[...] = dq.astype(BF16)
        dqg_ref[...] += dqg * scale
        dk, dkg = hp.rms_bwd(kv, hp.rms_r(kv), kg_ref[...], dkn[...])
        dk_ref[...] = dk.astype(BF16)
        dkg_ref[...] += dkg
        dv_ref[...] = dvn[...].astype(BF16)
        finish()

    pair = lambda group: pl.BlockSpec((s, 2 * dh), lambda p: (0, group * (D_ATT // (2 * dh)) + p))
    vec2 = pl.BlockSpec((1, 2 * dh), lambda p: (0, 0))
    vec = pl.BlockSpec((1, dh), lambda p: (0, 0))
    return _call(
        body, "attn_bwd", (steps,), [pair(2), pair(3), pair(4), pair(0), vec2, vec2],
        [pair(0), pair(0), pair(0), vec, vec],
        [jax.ShapeDtypeStruct((s, D_ATT), BF16)] * 3 + [jax.ShapeDtypeStruct((1, dh), F32)] * 2,
        [proj, proj, proj, dya, jnp.tile(qg, (1, 2)), jnp.tile(kg, (1, 2))],
        scratch=[pltpu.VMEM((s, 2 * dh), BF16)] * 4 + [pltpu.VMEM((4, max_windows, blk, 1), F32)]
        + [pltpu.VMEM((s, 2 * dh), F32)] * 3, rider=rider)


def _block_diag(w):
    n, c, d = w.shape
    return jnp.einsum("ncd,nm->ncmd", w, jnp.eye(n, dtype=w.dtype)).reshape(n * c, n * d)


def _diag_blocks(full, n):
    c = full.shape[0] // n
    return jnp.stack([full[i * c:(i + 1) * c, i * c:(i + 1) * c] for i in range(n)])


FFN1 = ["ffn1_w_gate", "ffn1_w_up", "ffn1_w_down"]
FFN2 = ["ffn2_w_gate", "ffn2_w_up", "ffn2_w_down"]


def _pair_sums(gb, names, where):
    theirs = _pair_exchange([gb[n] for n in names], "pair_exchange_" + names[0])
    pair, own = _pair_sum([gb[n] for n in names], theirs, where, "pair_sum_" + names[0])
    return _chip_rider(pair, own)


def _local_step(x, tgt, stacks, conv_stack, small, where):
    gate_up, down = FFN1[:2], FFN1[2:]
    big = dict(zip(gate_up, _gather_weights([stacks[n] for n in gate_up], [])))
    wa = _block_diag(small["rg_w_a"]).astype(BF16)
    wx = _block_diag(small["rg_w_x"]).astype(BF16)

    whole = lambda names: [big[n].reshape(-1, D_MODEL) for n in names]
    soon = down + ["w_in"]
    g1, u1, hb1, ab1, *landed = _ffn_up(x, small["ffn1_norm"], *whole(gate_up),
                                        rider=_gather_rider([stacks[n] for n in soon], [conv_stack]))
    big.update(zip(soon, landed))
    x1 = _ffn_down(x, ab1, *whole(down))
    conv_w = jnp.transpose(landed[-1], (1, 0, 2)).reshape(CONV_W, D_RNN)
    rg = (conv_w, small["conv_b"], wa, small["rg_b_a"], wx, small["rg_b_x"], small["rg_lambda"])
    riding = lambda names: _gather_rider([stacks[n] for n in names], [])
    proj, hb2, big["ffn2_w_gate"] = _mix_pre(x1, small["mix_norm"], big["w_in"], riding(["ffn2_w_gate"]))
    yr, hseq, big["ffn2_w_up"] = _rglru_fwd(proj, *rg, riding(["ffn2_w_up"]))
    ya, big["ffn2_w_down"], big["w_out"] = _attn_fwd(proj, small["q_norm"], small["k_norm"],
                                                     riding(["ffn2_w_down", "w_out"]))
    wout = big["w_out"].reshape(D_MODEL, D_MODEL)
    x2 = _mix_post(x1, yr, ya, small["rnn_out_norm"], small["attn_out_norm"], wout)
    dx3, g2, u2, hb3, ab3, loss = _ffn_fwd_loss(x2, small["ffn2_norm"], *whole(FFN2), tgt)

    gb, gs, slots = {}, {}, {}
    dx2, dg2, du2, dyb2, gs["ffn2_norm"] = _ffn_bwd_act(x2, small["ffn2_norm"], dx3, g2, u2, *whole(FFN2), "ffn2_bwd")
    gb["ffn2_w_gate"] = _ffn_wgrad(dg2, hb3, 1.0, "wgrad_gate_ffn2")
    gb["ffn2_w_up"] = _ffn_wgrad(du2, hb3, 1.0, "wgrad_up_ffn2")
    gb["ffn2_w_down"] = _ffn_wgrad(ab3, dyb2, 0.5, "wgrad_down_ffn2")
    dyr, dya, ycat, dxb2, gs["rnn_out_norm"], gs["attn_out_norm"] = _mix_post_bwd(
        dx2, yr, ya, small["rnn_out_norm"], small["attn_out_norm"], wout)
    gb["w_out"] = _wgrad_whole(ycat, [dxb2], False, "wgrad_out")
    early = FFN2 + ["w_out"]
    dq, dk, dv, gs["q_norm"], gs["k_norm"], *done = _attn_bwd(
        proj, dya, small["q_norm"], small["k_norm"], _pair_sums(gb, early, where))
    slots.update(zip(early, done))
    dxr, dgate, gs["conv_w"], gs["conv_b"], dwa, gs["rg_b_a"], dwx, gs["rg_b_x"], gs["rg_lambda"] = _rglru_bwd(
        proj, hseq, dyr, *rg)
    gs["rg_w_a"] = _diag_blocks(dwa, RNN_BLOCKS)
    gs["rg_w_x"] = _diag_blocks(dwx, RNN_BLOCKS)
    dps = [dxr, dgate, dq, dk, dv]
    dx1, gs["mix_norm"] = _mix_pre_bwd(x1, small["mix_norm"], dx2, dps, big["w_in"])
    dx0, dg1, du1, dyb1, gs["ffn1_norm"] = _ffn_bwd_act(x, small["ffn1_norm"], dx1, g1, u1, *whole(FFN1), "ffn1_bwd")

    mine = _place_shard(_pack([gs[n] for n in SMALL] + [loss[:, :1]]), where, F32, "place_small_grads",
                        by_device=True)
    gb["ffn1_w_gate"], everyone = _ffn_wgrad(dg1, hb1, 1.0, "wgrad_gate_ffn1", _small_rider(mine))
    gb["ffn1_w_up"], slots["ffn1_w_gate"] = _ffn_wgrad(
        du1, hb1, 1.0, "wgrad_up_ffn1", _pair_sums(gb, ["ffn1_w_gate"], where))
    gb["ffn1_w_down"], slots["ffn1_w_up"] = _ffn_wgrad(
        ab1, dyb1, 0.5, "wgrad_down_ffn1", _pair_sums(gb, ["ffn1_w_up"], where))
    gb["w_in"], slots["ffn1_w_down"] = _wgrad_whole(
        hb2, dps, True, "wgrad_in", _pair_sums(gb, ["ffn1_w_down"], where))
    last = _pair_sums(gb, ["w_in"], where)
    slots["w_in"], = _chip_exchange(last.plain, last.inplace)
    return dx0, slots, gs, everyone


ANY = pl.BlockSpec(memory_space=pl.ANY)


def _place():
    x, y, c = lax.axis_index("x"), lax.axis_index("y"), lax.axis_index("c")
    other_chips = [(1 - x, y), (x, 1 - y), (1 - x, 1 - y)]
    return x, y, c, 2 * x + y, other_chips


def _remote(src, dst, send_sem, recv_sem, to):
    return pltpu.make_async_remote_copy(src_ref=src, dst_ref=dst, send_sem=send_sem, recv_sem=recv_sem,
                                        device_id=to, device_id_type=MESH)


def _copy_plan(pairs):
    sends = [functools.partial(_remote, *a) for a, _ in pairs]
    arrivals = [functools.partial(_remote, *b) for _, b in pairs]
    return sends, arrivals


class _Rider:
    def __init__(self, plan, plain, inplace, n_copies=None, relay=None, n_relay=0):
        self.plan, self.plain, self.inplace = plan, list(plain), list(inplace)
        self.n_copies = n_copies or 3 * len(self.inplace)
        self.relay, self.n_relay = relay, n_relay

    def operands(self):
        return self.plain + self.inplace

    def out_shape(self):
        return [jax.ShapeDtypeStruct(a.shape, a.dtype) for a in self.inplace]

    def aliases(self, inputs_before, outputs_before):
        return {inputs_before + len(self.plain) + k: outputs_before + k for k in range(len(self.inplace))}

    def scratch(self):
        relay = [pltpu.SemaphoreType.DMA((self.n_relay,))] * 2 if self.relay else []
        return [pltpu.SemaphoreType.DMA((self.n_copies,))] * 2 + relay


def _split_refs(refs, n_in, n_out, rider):
    if rider is None:
        return refs[:n_in], refs[n_in:n_in + n_out], refs[n_in + n_out:], None
    r_in, r_out = len(rider.operands()), len(rider.inplace)
    outs_at = n_in + r_in
    n_sems = len(rider.scratch())
    rest = refs[outs_at + n_out + r_out:]
    sems = rest[len(rest) - n_sems:]
    filled = refs[outs_at + n_out:outs_at + n_out + r_out]
    copies = functools.partial(rider.plan, refs[n_in:n_in + len(rider.plain)], filled, *sems[:2])
    relay = functools.partial(rider.relay, filled, *sems[2:]) if rider.relay else None
    return refs[:n_in], refs[outs_at:outs_at + n_out], rest[:len(rest) - n_sems], (copies, relay)


def _ride(copies, first, last, middle=None):
    if copies is None:
        return lambda: None
    copies, relay = copies

    @pl.when(first)
    def _():
        _start(copies()[0])

    def start_relay():
        for make in copies()[1]:
            make().wait_recv()
        _start(relay()[0])

    if relay is not None and middle is not None:
        pl.when(middle)(start_relay)

    def finish():
        @pl.when(last)
        def _():
            if relay is None:
                _finish(*copies())
            else:
                if middle is None:
                    start_relay()
                _finish(copies()[0] + relay()[0], relay()[1])

    return finish


def _gather_rider(split, whole):
    n_split = len(split)
    return _Rider(lambda plain, stacks, ss, rs: _gather_ici(stacks, n_split, ss, rs), [], list(split) + list(whole),
                  relay=lambda stacks, ss, rs: _gather_d2d(stacks[:n_split], ss, rs), n_relay=3 * n_split)


def _chip_rider(sums, slots):
    return _Rider(_chip_copies, sums, slots)


def _start(makers):
    for make in makers:
        make().start()


def _finish(sends, arrivals):
    for make in arrivals:
        make().wait_recv()
    for make in sends:
        make().wait_send()


def _half(rows, c):
    return pl.ds(pl.multiple_of(c * rows, BF16_ROWS), rows)


def _gather_weights(split, whole):
    arrs = list(split) + list(whole)
    n, ns = len(arrs), len(split)

    def body(*refs):
        outs = refs[n:2 * n]
        send_sems, recv_sems, fsend_sems, frecv_sems = refs[2 * n:]
        sends, arrivals = _gather_ici(outs, ns, send_sems, recv_sems)
        passes, passed = _gather_d2d(outs[:ns], fsend_sems, frecv_sems)
        _start(sends)
        for k, make in enumerate(arrivals):
            make().wait_recv()
            if k < 3 * ns:
                passes[k]().start()
        _finish(sends + passes, passed)

    return pl.pallas_call(
        body, name="gather_weights",
        in_specs=[ANY] * n, out_specs=[ANY] * n,
        out_shape=[jax.ShapeDtypeStruct(a.shape, a.dtype) for a in arrs],
        input_output_aliases={i: i for i in range(n)},
        scratch_shapes=[pltpu.SemaphoreType.DMA((3 * n,)), pltpu.SemaphoreType.DMA((3 * n,)),
                        pltpu.SemaphoreType.DMA((3 * ns,)), pltpu.SemaphoreType.DMA((3 * ns,))],
    )(*arrs)


def _gather_ici(stacks, n_split, send_sems, recv_sems):
    x, y, c, me, chips = _place()

    def region(i, chip):
        if i < n_split:
            return stacks[i].at[chip, _half(stacks[i].shape[1] // 2, c)]
        return stacks[i].at[chip]

    pairs = []
    for i in range(len(stacks)):
        for p, (cx, cy) in enumerate(chips):
            k = 3 * i + p
            mine, got = region(i, me), region(i, 2 * cx + cy)
            sems, to = (send_sems.at[k], recv_sems.at[k]), (cx, cy, c)
            pairs.append(((mine, mine, *sems, to), (got, got, *sems, to)))
    return _copy_plan(pairs)


def _gather_d2d(stacks, send_sems, recv_sems):
    x, y, c, _, chips = _place()
    sibling = (x, y, 1 - c)
    pairs = []
    for i, stack in enumerate(stacks):
        rows = stack.shape[1] // 2
        for p, (cx, cy) in enumerate(chips):
            k = 3 * i + p
            got, theirs = stack.at[2 * cx + cy, _half(rows, c)], stack.at[2 * cx + cy, _half(rows, 1 - c)]
            sems = (send_sems.at[k], recv_sems.at[k])
            pairs.append(((got, got, *sems, sibling), (theirs, theirs, *sems, sibling)))
    return _copy_plan(pairs)


def _pair_exchange(grads, name):
    n = len(grads)

    def body(*refs):
        ins, theirs = refs[:n], refs[n:2 * n]
        send_sems, recv_sems = refs[2 * n:]
        x, y, c, _, _ = _place()
        sibling = (x, y, 1 - c)
        sends = [_remote(ins[k].at[:, _half(grads[k].shape[1] // 2, 1 - c)], theirs[k],
                         send_sems.at[k], recv_sems.at[k], sibling) for k in range(n)]
        for cp in sends:
            cp.start()
        for k in range(n):
            _remote(theirs[k], theirs[k], send_sems.at[k], recv_sems.at[k], sibling).wait_recv()
        for cp in sends:
            cp.wait_send()

    return pl.pallas_call(
        body, name=name,
        in_specs=[ANY] * n, out_specs=[ANY] * n,
        out_shape=[jax.ShapeDtypeStruct((g.shape[0], g.shape[1] // 2, g.shape[2]), g.dtype) for g in grads],
        scratch_shapes=[pltpu.SemaphoreType.DMA((n,))] * 2,
    )(*grads)


def _chip_exchange(sums, slots):
    n = len(sums)

    def body(*refs):
        sends, arrivals = _chip_copies(refs[:n], refs[2 * n:3 * n], *refs[3 * n:])
        _start(sends)
        _finish(sends, arrivals)

    return pl.pallas_call(
        body, name="grad_chip_exchange",
        in_specs=[ANY] * (2 * n), out_specs=[ANY] * n,
        out_shape=[jax.ShapeDtypeStruct(a.shape, a.dtype) for a in slots],
        input_output_aliases={n + k: k for k in range(n)},
        scratch_shapes=[pltpu.SemaphoreType.DMA((3 * n,)), pltpu.SemaphoreType.DMA((3 * n,))],
    )(*sums, *slots)


def _chip_copies(sums, slots, send_sems, recv_sems):
    x, y, c, me, chips = _place()
    pairs = []
    for k in range(len(sums)):
        for p, (cx, cy) in enumerate(chips):
            j = 3 * k + p
            got = slots[k].at[2 * cx + cy]
            sems, to = (send_sems.at[j], recv_sems.at[j]), (cx, cy, c)
            pairs.append(((sums[k].at[2 * cx + cy], slots[k].at[me], *sems, to), (got, got, *sems, to)))
    return _copy_plan(pairs)


def _half_swap(halves):
    n = len(halves)

    def body(*refs):
        outs = refs[n:2 * n]
        send_sems, recv_sems = refs[2 * n:]
        x, y, c, _, _ = _place()
        sibling = (x, y, 1 - c)
        sends = [_remote(outs[k].at[c], outs[k].at[c], send_sems.at[k], recv_sems.at[k], sibling) for k in range(n)]
        for cp in sends:
            cp.start()
        for k in range(n):
            got = outs[k].at[1 - c]
            _remote(got, got, send_sems.at[k], recv_sems.at[k], sibling).wait_recv()
        for cp in sends:
            cp.wait_send()

    return pl.pallas_call(
        body, name="grad_half_swap",
        in_specs=[ANY] * n, out_specs=[ANY] * n,
        out_shape=[jax.ShapeDtypeStruct(a.shape, a.dtype) for a in halves],
        input_output_aliases={k: k for k in range(n)},
        scratch_shapes=[pltpu.SemaphoreType.DMA((n,))] * 2,
    )(*halves)


def _small_rider(stack):
    n_dev = 2 * N_CHIPS

    def plan(_, stacks, send_sems, recv_sems):
        x, y, c, _, _ = _place()
        mine = stacks[0].at[4 * x + 2 * y + c]
        pairs = []
        for k in range(1, n_dev):
            px, py, pc = x ^ ((k >> 2) & 1), y ^ ((k >> 1) & 1), c ^ (k & 1)
            got = stacks[0].at[4 * px + 2 * py + pc]
            sems = (send_sems.at[k - 1], recv_sems.at[k - 1])
            pairs.append(((mine, mine, *sems, (px, py, pc)), (got, got, *sems, (px, py, pc))))
        return _copy_plan(pairs)

    return _Rider(plan, [], [stack], n_dev - 1)


def _row_tile(r):
    return r // 4 if r >= 256 and (r // 4) % BF16_ROWS == 0 else r


def _prefetch_call(body, name, grid, in_specs, out_specs, out_shape):
    spec = pltpu.PrefetchScalarGridSpec(num_scalar_prefetch=1, grid=grid, in_specs=in_specs, out_specs=out_specs)
    return pl.pallas_call(body, name=name, grid_spec=spec, out_shape=out_shape,
                          compiler_params=_params(("arbitrary",) * len(grid)))


def _place_shard(w2d, where, dtype, name, by_device=False):
    r, c = w2d.shape
    tr = _row_tile(r)
    slots = 2 * N_CHIPS if by_device else N_CHIPS
    slot = (lambda s: 2 * s[1] + s[0]) if by_device else (lambda s: s[1])

    def body(where_ref, w_ref, out_ref):
        out_ref[...] = w_ref[...].astype(dtype)

    return _prefetch_call(
        body, name, (r // tr,), [pl.BlockSpec((tr, c), lambda i, s: (i, 0))],
        pl.BlockSpec((None, tr, c), lambda i, s: (slot(s), i, 0)),
        jax.ShapeDtypeStruct((slots, r, c), dtype))(where, w2d)


def _place_shards(w2ds, where, name):
    n = len(w2ds)
    steps = N_CHIPS
    assert all(w.shape[0] % (BF16_ROWS * steps) == 0 for w in w2ds)

    def body(where_ref, *refs):
        for k in range(n):
            refs[n + k][...] = refs[k][...].astype(BF16)

    tile = lambda w: (w.shape[0] // steps, w.shape[1])
    return _prefetch_call(
        body, name, (steps,), [pl.BlockSpec(tile(w), lambda i, s: (i, 0)) for w in w2ds],
        [pl.BlockSpec((None,) + tile(w), lambda i, s: (s[1], i, 0)) for w in w2ds],
        [jax.ShapeDtypeStruct((N_CHIPS,) + w.shape, BF16) for w in w2ds])(where, *w2ds)


def _pair_sum(fulls, theirs, where, name):
    n = len(fulls)

    def body(where_ref, *refs):
        for k in range(n):
            a_ref, b_ref, out_ref, own_ref = refs[k], refs[n + k], refs[2 * n + k], refs[3 * n + k]
            total = (a_ref[...].astype(F32) + b_ref[...].astype(F32)).astype(BF16)
            out_ref[...] = total

            @pl.when(pl.program_id(0) == where_ref[1])
            def _():
                own_ref[...] = total

    half = lambda t: pl.BlockSpec((None,) + t.shape[1:], lambda j, s: (j, s[0], 0))
    blk = lambda t: pl.BlockSpec((None,) + t.shape[1:], lambda j, s: (j, 0, 0))
    own = lambda t: pl.BlockSpec((None,) + t.shape[1:], lambda j, s: (s[1], 0, 0))
    shapes = [jax.ShapeDtypeStruct(t.shape, BF16) for t in theirs]
    outs = _prefetch_call(
        body, name, (N_CHIPS,), [half(t) for t in theirs] + [blk(t) for t in theirs],
        [blk(t) for t in theirs] + [own(t) for t in theirs], shapes + shapes)(where, *fulls, *theirs)
    return outs[:n], outs[n:]


def _chip_sum(slots, where, name):
    n = len(slots)
    steps = 2
    assert all(a.shape[1] % (BF16_ROWS * steps) == 0 for a in slots)

    def body(where_ref, *refs):
        for k in range(n):
            a_ref, out_ref = refs[k], refs[n + k]
            total = a_ref[0].astype(F32)
            for j in range(1, a_ref.shape[0]):
                total = total + a_ref[j].astype(F32)
            out_ref[...] = total

    tile = lambda a: (a.shape[1] // steps, a.shape[2])
    return _prefetch_call(
        body, name, (steps,), [pl.BlockSpec((a.shape[0],) + tile(a), lambda i, s: (0, i, 0)) for a in slots],
        [pl.BlockSpec((None,) + tile(a), lambda i, s: (s[0], i, 0)) for a in slots],
        [jax.ShapeDtypeStruct((2,) + a.shape[1:], F32) for a in slots])(where, *slots)


def _slot_sum(a, name):
    nb, r, c = a.shape
    tr = _row_tile(r)

    def body(a_ref, out_ref):
        total = a_ref[0].astype(F32)
        for j in range(1, nb):
            total = total + a_ref[j].astype(F32)
        out_ref[...] = total

    return pl.pallas_call(
        body, name=name, grid=(r // tr,),
        in_specs=[pl.BlockSpec((nb, tr, c), lambda i: (0, i, 0))],
        out_specs=pl.BlockSpec((tr, c), lambda i: (i, 0)),
        out_shape=jax.ShapeDtypeStruct((r, c), F32), compiler_params=_params(("arbitrary",)),
    )(a)


def _adamw(ws, gs, ms, vs, name, steps=1):
    n = len(ws)
    c1 = 1.0 - ADAM_B1 ** ADAM_STEP
    c2 = 1.0 - ADAM_B2 ** ADAM_STEP
    assert all(w.shape[0] % steps == 0 and (steps == 1 or w.shape[0] // steps % 8 == 0) for w in ws)

    def body(*refs):
        for k in range(n):
            w_ref, g_ref, m_ref, v_ref = (refs[j * n + k] for j in range(4))
            g_out, d_ref, m2_ref, v2_ref = (refs[(4 + j) * n + k] for j in range(4))
            gv = g_ref[...]
            g_out[...] = gv
            m2 = ADAM_B1 * m_ref[...] + (1.0 - ADAM_B1) * gv
            v2 = ADAM_B2 * v_ref[...] + (1.0 - ADAM_B2) * (gv * gv)
            m2_ref[...] = m2
            v2_ref[...] = v2
            d_ref[...] = -ADAM_LR * ((m2 / c1) / (jnp.sqrt(v2 / c2) + ADAM_EPS) + ADAM_WD * w_ref[...])

    blks = [pl.BlockSpec((w.shape[0] // steps, w.shape[1]), lambda i: (i, 0)) for w in ws]
    shapes = [jax.ShapeDtypeStruct(w.shape, F32) for w in ws]
    outs = pl.pallas_call(
        body, name=name, grid=(steps,), in_specs=blks * 4, out_specs=blks * 4, out_shape=shapes * 4,
        compiler_params=_params(("arbitrary",)),
    )(*ws, *gs, *ms, *vs)
    return [outs[j * n:(j + 1) * n] for j in range(4)]


WEIGHTS = ["ffn1_norm", "ffn1_w_gate", "ffn1_w_up", "ffn1_w_down", "mix_norm", "w_in", "conv_w", "conv_b",
           "rg_w_a", "rg_b_a", "rg_w_x", "rg_b_x", "rg_lambda", "q_norm", "k_norm", "rnn_out_norm",
           "attn_out_norm", "w_out", "ffn2_norm", "ffn2_w_gate", "ffn2_w_up", "ffn2_w_down"]
BIG = ["ffn1_w_gate", "ffn1_w_up", "ffn1_w_down", "w_in", "w_out", "ffn2_w_gate", "ffn2_w_up", "ffn2_w_down"]
SMALL = [n for n in WEIGHTS if n not in BIG]
PACK_LANES = 128
PACK_ROW_ALIGN = 8


def _hidden_major(name, a):
    return jnp.transpose(a) if name.endswith(("w_gate", "w_up")) else a


def _pack(parts):
    sizes = [math.prod(p.shape) for p in parts]
    unit = PACK_LANES * PACK_ROW_ALIGN
    padded = -(-sum(sizes) // unit) * unit
    flat, at = 0.0, 0
    for p, size in zip(parts, sizes):
        flat = flat + jnp.pad(p.reshape(-1), (at, padded - at - size))
        at += size
    return flat.reshape(-1, PACK_LANES)


def _unpack(packed, shapes):
    flat = packed.reshape(-1)
    out, at = [], 0
    for shp in shapes:
        size = math.prod(shp)
        out.append(flat[at:at + size].reshape(shp))
        at += size
    return out


def kernel(x, ffn1_norm, ffn1_w_gate, ffn1_w_up, ffn1_w_down, mix_norm, w_in, conv_w, conv_b, rg_w_a, rg_b_a, rg_w_x, rg_b_x, rg_lambda, q_norm, k_norm, rnn_out_norm, attn_out_norm, w_out, ffn2_norm, ffn2_w_gate, ffn2_w_up, ffn2_w_down, loss_target, m_ffn1_norm, m_ffn1_w_gate, m_ffn1_w_up, m_ffn1_w_down, m_mix_norm, m_w_in, m_conv_w, m_conv_b, m_rg_w_a, m_rg_b_a, m_rg_w_x, m_rg_b_x, m_rg_lambda, m_q_norm, m_k_norm, m_rnn_out_norm, m_attn_out_norm, m_w_out, m_ffn2_norm, m_ffn2_w_gate, m_ffn2_w_up, m_ffn2_w_down, v_ffn1_norm, v_ffn1_w_gate, v_ffn1_w_up, v_ffn1_w_down, v_mix_norm, v_w_in, v_conv_w, v_conv_b, v_rg_w_a, v_rg_b_a, v_rg_w_x, v_rg_b_x, v_rg_lambda, v_q_norm, v_k_norm, v_rnn_out_norm, v_attn_out_norm, v_w_out, v_ffn2_norm, v_ffn2_w_gate, v_ffn2_w_up, v_ffn2_w_down):
    given = dict(locals())
    w = {n: given[n] for n in WEIGHTS}
    m = {n: given["m_" + n] for n in WEIGHTS}
    v = {n: given["v_" + n] for n in WEIGHTS}
    chip = 2 * lax.axis_index("x") + lax.axis_index("y")

    where = jnp.stack([lax.axis_index("c"), chip]).astype(jnp.int32)

    stacks = dict(zip(BIG, _place_shards([_hidden_major(n, w[n][0]) for n in BIG], where, "place_weights")))
    conv_stack = _place_shard(w["conv_w"][0], where, F32, "place_conv_w")
    small = {n: (w[n][0] if w[n].ndim > 2 else w[n]) for n in SMALL if n != "conv_w"}

    grad_x, slots, gs, everyone = _local_step(x[0], loss_target[0], stacks, conv_stack, small, where)

    swapped = _half_swap(_chip_sum([slots[n] for n in BIG], where, "chip_sums"))
    g2s = [t.reshape(t.shape[0] * t.shape[1], t.shape[2]) for t in swapped]
    flat = lambda tree: [_hidden_major(n, tree[n][0]) for n in BIG]
    g2s, d2s, m2s, v2s = _adamw(flat(w), g2s, flat(m), flat(v), "adamw_weights", ADAMW_STEPS)
    grads, deltas, new_m, new_v = {}, {}, {}, {}
    for tree, parts in ((grads, g2s), (deltas, d2s), (new_m, m2s), (new_v, v2s)):
        tree.update({n: _hidden_major(n, a).reshape(w[n].shape) for n, a in zip(BIG, parts)})

    full_shapes = [gs[n].shape for n in SMALL]
    *summed, loss = _unpack(_slot_sum(everyone, "small_grad_sum"), full_shapes + [(1, 1)])
    g_parts = dict(zip(SMALL, summed))
    quarter = D_RNN // N_CHIPS
    g_parts["conv_w"] = lax.dynamic_slice_in_dim(g_parts["conv_w"], chip * quarter, quarter, axis=1)
    local_shapes = [w[n].shape for n in SMALL]
    pk = lambda tree: _pack([tree[n] for n in SMALL])
    (g_s,), (d_s,), (m_s,), (v_s,) = _adamw([pk(w)], [pk(g_parts)], [pk(m)], [pk(v)], "adamw_small")
    for tree, packed in ((grads, g_s), (deltas, d_s), (new_m, m_s), (new_v, v_s)):
        tree.update(zip(SMALL, _unpack(packed, local_shapes)))

    return (loss[0, 0], grad_x.reshape(x.shape), *[grads[n] for n in WEIGHTS], *[deltas[n] for n in WEIGHTS],
            *[new_m[n] for n in WEIGHTS], *[new_v[n] for n in WEIGHTS])
```

```python
import functools
import math

import jax
import jax.numpy as jnp
from jax import lax
from jax.experimental import pallas as pl
from jax.experimental.pallas import tpu as pltpu

F32 = jnp.float32
BF16 = jnp.bfloat16
MESH = pl.DeviceIdType.MESH

D_MODEL = 1024
N_CHIPS = 4
D_RNN = 512
D_ATT = 512
N_HEADS = 8
HEAD_DIM = 64
RNN_BLOCKS = 8
CONV_W = 4
RG_C = 8.0
N_IN = 2 * D_RNN + 3 * D_ATT
EPS = 1e-6
ATT_BLOCK = 128
ATT_WINDOW = 384
ATT_SPLIT = 256
EXP_ZERO = -105.0

ADAM_LR = 0.001
ADAM_B1 = 0.9
ADAM_B2 = 0.999
ADAM_EPS = 1e-08
ADAM_WD = 0.01
ADAM_STEP = 10

V7X_VMEM_LIMIT = 60 * 1024 * 1024
V7X_MXU_WIDTH = 256
TOKEN_TILE = 512
SUBLANES = 8
BF16_ROWS = 16
FFN_TILE = 256
WGRAD_TILE = 2048
WHOLE_TILE = 1024
ADAMW_STEPS = 8

GELU_K0 = math.sqrt(2.0 / math.pi)
GELU_K1 = 0.044715


def _params(sem=None):
    return pltpu.CompilerParams(dimension_semantics=sem, vmem_limit_bytes=V7X_VMEM_LIMIT)


def _dot(a, b):
    return jnp.dot(a, b, preferred_element_type=F32)


def _dot_nt(a, b):
    return lax.dot_general(a, b, (((1,), (1,)), ((), ())), preferred_element_type=F32)


def _dot_tn(a, b):
    return lax.dot_general(a, b, (((0,), (0,)), ((), ())), preferred_element_type=F32)


def _sigmoid(x):
    return 1.0 / (1.0 + jnp.exp(-x))


def _rms_r(xv):
    return lax.rsqrt(jnp.mean(xv * xv, axis=-1, keepdims=True) + EPS)


def _rms_bwd(xv, r, nw, dh):
    t = dh * nw
    dx = r * t - xv * (r * r * r * jnp.mean(t * xv, axis=-1, keepdims=True))
    dn = jnp.sum(dh * xv * r, axis=0, keepdims=True)
    return dx, dn


def _gelu(x):
    t = jnp.tanh(GELU_K0 * (x + GELU_K1 * x * x * x))
    return 0.5 * x * (1.0 + t)


def _gelu_grad(x):
    t = jnp.tanh(GELU_K0 * (x + GELU_K1 * x * x * x))
    return 0.5 * (1.0 + t) + 0.5 * x * (1.0 - t * t) * (GELU_K0 * (1.0 + 3.0 * GELU_K1 * x * x))


def _expm1_neg(x):
    p = 1.0 + x * (1.0 / 6.0)
    for k in (5.0, 4.0, 3.0, 2.0):
        p = 1.0 + x * (1.0 / k) * p
    return jnp.where(x > -0.25, x * p, jnp.exp(x) - 1.0)


def _log_sigmoid(x):
    return jnp.minimum(x, 0.0) - jnp.log(1.0 + jnp.exp(-jnp.abs(x)))


def _tile(s):
    return min(TOKEN_TILE, s)


def _ffn_chunks(f):
    cut = f // 2 // V7X_MXU_WIDTH * V7X_MXU_WIDTH
    return ((0, cut), (cut, f)) if 0 < cut < f else ((0, f),)


def _ffn_fwd_loss(x, nw, wg, wu, wd, tgt):
    s, d = x.shape
    f = wg.shape[0]
    tm = min(FFN_TILE, s)
    ni = s // tm
    assert s % tm == 0

    def body(x_ref, nw_ref, wg_ref, wu_ref, wd_ref, tgt_ref, out_ref, g_ref, u_ref, hb_ref, ab_ref, loss_ref):
        i = pl.program_id(0)
        xv = x_ref[...]
        hb = (xv * _rms_r(xv) * nw_ref[...]).astype(BF16)
        hb_ref[...] = hb
        y = jnp.zeros((tm, d), F32)
        for lo, hi in _ffn_chunks(f):
            g = _dot_nt(hb, wg_ref[lo:hi, :])
            u = _dot_nt(hb, wu_ref[lo:hi, :])
            g_ref[:, lo:hi] = g.astype(BF16)
            u_ref[:, lo:hi] = u.astype(BF16)
            ab = (g * _sigmoid(g) * u).astype(BF16)
            ab_ref[:, lo:hi] = ab
            y = y + _dot(ab, wd_ref[lo:hi, :])
        diff = xv + 0.5 * y - tgt_ref[...]
        out_ref[...] = diff * (1.0 / d)

        @pl.when(i == 0)
        def _():
            loss_ref[...] = jnp.zeros_like(loss_ref)

        loss_ref[...] += jnp.sum(diff * diff) * (0.5 / d)

    row = pl.BlockSpec((tm, d), lambda i: (i, 0))
    weight = pl.BlockSpec((f, d), lambda i: (0, 0), pipeline_mode=pl.Buffered(1))
    blk = pl.BlockSpec((tm, f), lambda i: (i, 0))
    wide = jax.ShapeDtypeStruct((s, f), BF16)
    return _call(body, "ffn_fwd_loss", (ni,),
                 [row, pl.BlockSpec((1, d), lambda i: (0, 0)), weight, weight, weight, row],
                 [row, blk, blk, row, blk, pl.BlockSpec((1, 128), lambda i: (0, 0))],
                 [jax.ShapeDtypeStruct((s, d), F32), wide, wide, jax.ShapeDtypeStruct((s, d), BF16), wide,
                  jax.ShapeDtypeStruct((1, 128), F32)], [x, nw, wg, wu, wd, tgt])


def _ffn_up(x, nw, wg, wu, rider=None):
    s, d = x.shape
    f = wg.shape[0]
    tm = min(FFN_TILE, s)
    ni = s // tm
    assert s % tm == 0

    def body(*refs):
        (x_ref, nw_ref, wg_ref, wu_ref), (g_ref, u_ref, hb_ref, ab_ref), _, copies = _split_refs(refs, 4, 4, rider)
        i = pl.program_id(0)
        finish = _ride(copies, i == 0, i == ni - 1)
        xv = x_ref[...]
        hb = (xv * _rms_r(xv) * nw_ref[...]).astype(BF16)
        hb_ref[...] = hb
        for lo, hi in _ffn_chunks(f):
            g = _dot_nt(hb, wg_ref[lo:hi, :])
            u = _dot_nt(hb, wu_ref[lo:hi, :])
            g_ref[:, lo:hi] = g.astype(BF16)
            u_ref[:, lo:hi] = u.astype(BF16)
            ab_ref[:, lo:hi] = (g * _sigmoid(g) * u).astype(BF16)
        finish()

    row = pl.BlockSpec((tm, d), lambda i: (i, 0))
    weight = pl.BlockSpec((f, d), lambda i: (0, 0), pipeline_mode=pl.Buffered(1))
    blk = pl.BlockSpec((tm, f), lambda i: (i, 0))
    wide = jax.ShapeDtypeStruct((s, f), BF16)
    return _call(body, "ffn_up", (ni,), [row, pl.BlockSpec((1, d), lambda i: (0, 0)), weight, weight],
                 [blk, blk, row, blk], [wide, wide, jax.ShapeDtypeStruct((s, d), BF16), wide], [x, nw, wg, wu],
                 rider=rider)


def _ffn_down(x, ab, wd):
    s, d = x.shape
    f = wd.shape[0]
    tm = _tile(s)
    assert s % tm == 0

    def body(x_ref, ab_ref, wd_ref, out_ref):
        out_ref[...] = x_ref[...] + 0.5 * _dot(ab_ref[...], wd_ref[...])

    row = pl.BlockSpec((tm, d), lambda i: (i, 0))
    return _call(body, "ffn_down", (s // tm,),
                 [row, pl.BlockSpec((tm, f), lambda i: (i, 0)),
                  pl.BlockSpec((f, d), lambda i: (0, 0), pipeline_mode=pl.Buffered(1))],
                 [row], [jax.ShapeDtypeStruct((s, d), F32)], [x, ab, wd])[0]


def _call(body, name, grid, in_specs, out_specs, out_shape, args, scratch=(), rider=None):
    in_specs, out_specs, out_shape, scratch = list(in_specs), list(out_specs), list(out_shape), list(scratch)
    extra, aliases = [], {}
    if rider is not None:
        extra = rider.operands()
        aliases = rider.aliases(len(args), len(out_shape))
        in_specs += [ANY] * len(extra)
        out_specs += [ANY] * len(rider.inplace)
        out_shape += rider.out_shape()
        scratch += rider.scratch()
    return pl.pallas_call(
        body, name=name, grid=grid, in_specs=in_specs, out_specs=out_specs, out_shape=out_shape,
        input_output_aliases=aliases, scratch_shapes=scratch,
        compiler_params=_params(("arbitrary",) * len(grid)),
    )(*args, *extra)


def _ffn_bwd_act(x, nw, dy, g, u, wg, wu, wd, name):
    s, d = x.shape
    f = wg.shape[0]
    tm = min(FFN_TILE, s)
    assert s % tm == 0

    def body(x_ref, nw_ref, dy_ref, g_ref, u_ref, wg_ref, wu_ref, wd_ref,
             dx_ref, dg_ref, du_ref, dyb_ref, dnw_ref):
        dyv = dy_ref[...]
        dyb = dyv.astype(BF16)
        dyb_ref[...] = dyb
        dh = jnp.zeros((tm, d), F32)
        for lo, hi in _ffn_chunks(f):
            da = 0.5 * _dot_nt(dyb, wd_ref[lo:hi, :])
            gv = g_ref[:, lo:hi].astype(F32)
            sg = _sigmoid(gv)
            dub = (da * (gv * sg)).astype(BF16)
            dgb = (da * u_ref[:, lo:hi].astype(F32) * (sg * (1.0 + gv * (1.0 - sg)))).astype(BF16)
            dg_ref[:, lo:hi] = dgb
            du_ref[:, lo:hi] = dub
            dh = dh + _dot(dgb, wg_ref[lo:hi, :]) + _dot(dub, wu_ref[lo:hi, :])
        xv = x_ref[...]
        dx, dn = _rms_bwd(xv, _rms_r(xv), nw_ref[...], dh)
        dx_ref[...] = dyv + dx

        @pl.when(pl.program_id(0) == 0)
        def _():
            dnw_ref[...] = jnp.zeros_like(dnw_ref)

        dnw_ref[...] += dn

    row = pl.BlockSpec((tm, d), lambda i: (i, 0))
    vec = pl.BlockSpec((1, d), lambda i: (0, 0))
    blk = pl.BlockSpec((tm, f), lambda i: (i, 0))
    weight = pl.BlockSpec((f, d), lambda i: (0, 0), pipeline_mode=pl.Buffered(1))
    return _call(
        body, name, (s // tm,), [row, vec, row, blk, blk, weight, weight, weight], [row, blk, blk, row, vec],
        [jax.ShapeDtypeStruct((s, d), F32), jax.ShapeDtypeStruct((s, f), BF16),
         jax.ShapeDtypeStruct((s, f), BF16), jax.ShapeDtypeStruct((s, d), BF16),
         jax.ShapeDtypeStruct((1, d), F32)],
        [x, nw, dy, g, u, wg, wu, wd])


def _wgrad(a, b, a_spec, b_spec, out_rows, out_cols, scale, name, tk, rider=None, per_step=1):
    s = a.shape[-2]
    nk = s // tk
    steps = N_CHIPS // per_step
    assert s % tk == 0

    def body(*refs):
        (a_ref, b_ref), (out_ref,), (acc,), copies = _split_refs(refs, 2, 1, rider)
        j, k = pl.program_id(0), pl.program_id(1)
        finish = _ride(copies, jnp.logical_and(j == 0, k == 0), jnp.logical_and(j == steps - 1, k == nk - 1))

        @pl.when(k == 0)
        def _():
            acc[...] = jnp.zeros_like(acc)

        acc[...] += _dot_tn(a_ref[...], b_ref[...])

        @pl.when(k == nk - 1)
        def _():
            for t in range(per_step):
                out_ref[t] = (acc[t * out_rows:(t + 1) * out_rows, :] * scale).astype(BF16)

        finish()

    outs = _call(
        body, name, (steps, nk), [a_spec(tk), b_spec(tk)],
        [pl.BlockSpec((per_step, out_rows, out_cols), lambda j, k: (j, 0, 0))],
        [jax.ShapeDtypeStruct((N_CHIPS, out_rows, out_cols), BF16)], [a, b],
        scratch=[pltpu.VMEM((per_step * out_rows, out_cols), F32)], rider=rider)
    return outs[0] if rider is None else outs


def _beside(refs):
    return jnp.concatenate([r[...] for r in refs], axis=1) if len(refs) > 1 else refs[0][...]


def _wgrad_whole(a, bs, col_blocks, name, rider=None):
    s, m = a.shape
    n = sum(b.shape[1] for b in bs)
    tk = min(WHOLE_TILE, s)
    nk = s // tk
    assert s % tk == 0
    out_shape = (N_CHIPS, m, n // N_CHIPS) if col_blocks else (N_CHIPS, m // N_CHIPS, n)

    def body(*refs):
        (a_ref, *b_refs), (out_ref,), (acc,), copies = _split_refs(refs, 1 + len(bs), 1, rider)
        k = pl.program_id(0)
        finish = _ride(copies, k == 0, k == nk - 1)

        @pl.when(k == 0)
        def _():
            acc[...] = jnp.zeros_like(acc)

        acc[...] += _dot_tn(a_ref[...], _beside(b_refs))

        @pl.when(k == nk - 1)
        def _():
            for j in range(N_CHIPS):
                if col_blocks:
                    out_ref[j] = acc[:, j * out_shape[2]:(j + 1) * out_shape[2]].astype(BF16)
                else:
                    out_ref[j] = acc[j * out_shape[1]:(j + 1) * out_shape[1], :].astype(BF16)

        finish()

    outs = _call(
        body, name, (nk,),
        [pl.BlockSpec((tk, m), lambda k: (k, 0))] + [pl.BlockSpec((tk, b.shape[1]), lambda k: (k, 0)) for b in bs],
        [pl.BlockSpec(out_shape, lambda k: (0, 0, 0))], [jax.ShapeDtypeStruct(out_shape, BF16)], [a, *bs],
        scratch=[pltpu.VMEM((m, n), F32)], rider=rider)
    return outs[0] if rider is None else outs


def _ffn_wgrad(hidden, shared, scale, name, rider=None):
    s, d = shared.shape
    half = hidden.shape[1] // 2
    return _wgrad(hidden, shared, lambda tk: pl.BlockSpec((tk, half), lambda j, k: (k, j)),
                  lambda tk: pl.BlockSpec((tk, d), lambda j, k: (k, 0)), half // 2, d, scale, name,
                  min(WGRAD_TILE, s), rider, per_step=2)


def _mix_pre(x, nw, win, rider=None):
    s, d = x.shape
    nb, _, cb = win.shape
    tm = _tile(s)
    ni = s // tm
    assert s % tm == 0

    def body(*refs):
        (x_ref, nw_ref, w_ref), (p_ref, hb_ref), _, copies = _split_refs(refs, 3, 2, rider)
        finish = _ride(copies, pl.program_id(0) == 0, pl.program_id(0) == ni - 1)
        xv = x_ref[...]
        hb = (xv * _rms_r(xv) * nw_ref[...]).astype(BF16)
        hb_ref[...] = hb
        for j in range(nb):
            p_ref[:, j * cb:(j + 1) * cb] = _dot(hb, w_ref[j])
        finish()

    row = pl.BlockSpec((tm, d), lambda i: (i, 0))
    return _call(
        body, "mix_pre", (ni,),
        [row, pl.BlockSpec((1, d), lambda i: (0, 0)),
         pl.BlockSpec((nb, d, cb), lambda i: (0, 0, 0), pipeline_mode=pl.Buffered(1))],
        [pl.BlockSpec((tm, nb * cb), lambda i: (i, 0)), row],
        [jax.ShapeDtypeStruct((s, nb * cb), F32), jax.ShapeDtypeStruct((s, d), BF16)], [x, nw, win], rider=rider)


def _mix_pre_bwd(x, nw, dres, dps, win):
    s, d = x.shape
    nb, _, cb = win.shape
    tm = _tile(s)
    assert s % tm == 0 and sum(p.shape[1] for p in dps) == nb * cb

    def body(x_ref, nw_ref, dres_ref, *rest):
        *dp_refs, w_ref, dx_ref, dnw_ref = rest
        dp = _beside(dp_refs)
        dh = jnp.zeros((tm, d), F32)
        for j in range(nb):
            dh = dh + _dot_nt(dp[:, j * cb:(j + 1) * cb], w_ref[j])
        xv = x_ref[...]
        dx, dn = _rms_bwd(xv, _rms_r(xv), nw_ref[...], dh)
        dx_ref[...] = dres_ref[...] + dx

        @pl.when(pl.program_id(0) == 0)
        def _():
            dnw_ref[...] = jnp.zeros_like(dnw_ref)

        dnw_ref[...] += dn

    row = pl.BlockSpec((tm, d), lambda i: (i, 0))
    vec = pl.BlockSpec((1, d), lambda i: (0, 0))
    return pl.pallas_call(
        body, name="mix_pre_bwd", grid=(s // tm,),
        in_specs=[row, vec, row] + [pl.BlockSpec((tm, p.shape[1]), lambda i: (i, 0)) for p in dps]
        + [pl.BlockSpec((nb, d, cb), lambda i: (0, 0, 0), pipeline_mode=pl.Buffered(1))],
        out_specs=[row, vec],
        out_shape=[jax.ShapeDtypeStruct((s, d), F32), jax.ShapeDtypeStruct((1, d), F32)],
        compiler_params=_params(("arbitrary",)),
    )(x, nw, dres, *dps, win)


def _mix_post(x, yr, ya, nr, na, wout):
    s, d = x.shape
    h = yr.shape[1]
    tm = _tile(s)

    def body(x_ref, yr_ref, ya_ref, nr_ref, na_ref, w_ref, out_ref):
        yrv = yr_ref[...]
        yav = ya_ref[...]
        onb = (yrv * _rms_r(yrv) * nr_ref[...]).astype(BF16)
        oab = (yav * _rms_r(yav) * na_ref[...]).astype(BF16)
        out_ref[...] = x_ref[...] + _dot(onb, w_ref[0:h, :]) + _dot(oab, w_ref[h:2 * h, :])

    row = pl.BlockSpec((tm, d), lambda i: (i, 0))
    half = pl.BlockSpec((tm, h), lambda i: (i, 0))
    vec = pl.BlockSpec((1, h), lambda i: (0, 0))
    return pl.pallas_call(
        body, name="mix_post", grid=(s // tm,),
        in_specs=[row, half, half, vec, vec, pl.BlockSpec((2 * h, d), lambda i: (0, 0))],
        out_specs=row, out_shape=jax.ShapeDtypeStruct((s, d), F32),
        compiler_params=_params(("arbitrary",)),
    )(x, yr, ya, nr, na, wout)


def _mix_post_bwd(dx, yr, ya, nr, na, wout):
    s, d = dx.shape
    h = yr.shape[1]
    tm = _tile(s)

    def body(dx_ref, yr_ref, ya_ref, nr_ref, na_ref, w_ref,
             dyr_ref, dya_ref, yc_ref, dxb_ref, dnr_ref, dna_ref):
        i = pl.program_id(0)
        dxb = dx_ref[...].astype(BF16)
        dxb_ref[...] = dxb
        dyc = _dot_nt(dxb, w_ref[...])
        yrv = yr_ref[...]
        yav = ya_ref[...]
        rr = _rms_r(yrv)
        ra = _rms_r(yav)
        yc_ref[:, 0:h] = (yrv * rr * nr_ref[...]).astype(BF16)
        yc_ref[:, h:2 * h] = (yav * ra * na_ref[...]).astype(BF16)
        dyr, dnr = _rms_bwd(yrv, rr, nr_ref[...], dyc[:, 0:h])
        dya, dna = _rms_bwd(yav, ra, na_ref[...], dyc[:, h:2 * h])
        dyr_ref[...] = dyr
        dya_ref[...] = dya

        @pl.when(i == 0)
        def _():
            dnr_ref[...] = jnp.zeros_like(dnr_ref)
            dna_ref[...] = jnp.zeros_like(dna_ref)

        dnr_ref[...] += dnr
        dna_ref[...] += dna

    row = pl.BlockSpec((tm, d), lambda i: (i, 0))
    half = pl.BlockSpec((tm, h), lambda i: (i, 0))
    vec = pl.BlockSpec((1, h), lambda i: (0, 0))
    return pl.pallas_call(
        body, name="mix_post_bwd", grid=(s // tm,),
        in_specs=[row, half, half, vec, vec, pl.BlockSpec((2 * h, d), lambda i: (0, 0))],
        out_specs=[half, half, pl.BlockSpec((tm, 2 * h), lambda i: (i, 0)), row, vec, vec],
        out_shape=[jax.ShapeDtypeStruct((s, h), F32), jax.ShapeDtypeStruct((s, h), F32),
                   jax.ShapeDtypeStruct((s, 2 * h), BF16), jax.ShapeDtypeStruct((s, d), BF16),
                   jax.ShapeDtypeStruct((1, h), F32), jax.ShapeDtypeStruct((1, h), F32)],
        compiler_params=_params(("arbitrary",)),
    )(dx, yr, ya, nr, na, wout)


def _shift_down(xv, s, prev8):
    rolled = pltpu.roll(xv, s, 0)
    row8 = lax.broadcasted_iota(jnp.int32, prev8.shape, 0)
    head = jnp.where(row8 < s, pltpu.roll(prev8, s, 0), rolled[0:8, :])
    return jnp.concatenate([head, rolled[8:, :]], axis=0)


def _shift_up(xv, s, next8):
    n = xv.shape[0]
    rolled = pltpu.roll(xv, n - s, 0)
    row8 = lax.broadcasted_iota(jnp.int32, next8.shape, 0)
    tail = jnp.where(row8 >= 8 - s, pltpu.roll(next8, 8 - s, 0), rolled[n - 8:, :])
    return jnp.concatenate([rolled[:n - 8, :], tail], axis=0)


def _scan_fwd(a, b):
    n = a.shape[0]
    sub = lax.broadcasted_iota(jnp.int32, a.shape, 0) % SUBLANES
    s = 1
    while s < SUBLANES:
        ok = sub >= s
        b = jnp.where(ok, a * pltpu.roll(b, s, 0) + b, b)
        a = jnp.where(ok, a * pltpu.roll(a, s, 0), a)
        s *= 2
    groups = []
    before = jnp.zeros((1, a.shape[1]), F32)
    for g in range(n // SUBLANES):
        rows = slice(g * SUBLANES, (g + 1) * SUBLANES)
        groups.append(a[rows] * before + b[rows])
        before = groups[-1][SUBLANES - 1:]
    return jnp.concatenate(groups, axis=0)


def _scan_bwd(a, b):
    n = a.shape[0]
    sub = lax.broadcasted_iota(jnp.int32, a.shape, 0) % SUBLANES
    s = 1
    while s < SUBLANES:
        ok = sub < SUBLANES - s
        b = jnp.where(ok, a * pltpu.roll(b, n - s, 0) + b, b)
        a = jnp.where(ok, a * pltpu.roll(a, n - s, 0), a)
        s *= 2
    groups = []
    after = jnp.zeros((1, a.shape[1]), F32)
    for g in reversed(range(n // SUBLANES)):
        rows = slice(g * SUBLANES, (g + 1) * SUBLANES)
        groups.append(a[rows] * after + b[rows])
        after = groups[-1][:1]
    return jnp.concatenate(groups[::-1], axis=0)


def _rglru_gates(xv, prev8, cw_ref, cb_ref, wa_ref, ba_ref, wx_ref, bx_ref, lam_ref):
    x1 = _shift_down(xv, 1, prev8)
    x2 = _shift_down(xv, 2, prev8)
    x3 = _shift_down(xv, 3, prev8)
    xc = cw_ref[3:4, :] * xv + cw_ref[2:3, :] * x1 + cw_ref[1:2, :] * x2 + cw_ref[0:1, :] * x3 + cb_ref[...]
    xcb = xc.astype(BF16)
    r = _sigmoid(_dot(xcb, wa_ref[...]) + ba_ref[...])
    ig = _sigmoid(_dot(xcb, wx_ref[...]) + bx_ref[...])
    c = RG_C * _log_sigmoid(lam_ref[...])
    la = r * c
    a = jnp.exp(la)
    m = jnp.sqrt(-_expm1_neg(2.0 * la))
    return (x1, x2, x3), xc, xcb, r, ig, c, a, m


def _rglru_fwd(proj, cw, cb, wa, ba, wx, bx, lam, rider=None):
    s = proj.shape[0]
    w = D_RNN
    tm = _tile(s)
    ni = s // tm

    def body(*refs):
        ins, (y_ref, h_ref), (prev, hlast), copies = _split_refs(refs, 9, 2, rider)
        xr_ref, gate_ref, cw_ref, cb_ref, wa_ref, ba_ref, wx_ref, bx_ref, lam_ref = ins
        finish = _ride(copies, pl.program_id(0) == 0, pl.program_id(0) == ni - 1)

        @pl.when(pl.program_id(0) == 0)
        def _():
            prev[...] = jnp.zeros_like(prev)
            hlast[...] = jnp.zeros_like(hlast)

        xv = xr_ref[...]
        _, xc, _, _, ig, _, a, m = _rglru_gates(xv, prev[...], cw_ref, cb_ref, wa_ref, ba_ref,
                                                wx_ref, bx_ref, lam_ref)
        b = m * (ig * xc)
        row = lax.broadcasted_iota(jnp.int32, b.shape, 0)
        b = jnp.where(row == 0, b + a * hlast[...], b)
        h = _scan_fwd(a, b)
        h_ref[...] = h
        y_ref[...] = h * _gelu(gate_ref[...])
        prev[...] = xv[tm - 8:, :]
        hlast[...] = h[tm - 1:tm, :]
        finish()

    vec = pl.BlockSpec((1, w), lambda i: (0, 0))
    sq = pl.BlockSpec((w, w), lambda i: (0, 0))
    out = pl.BlockSpec((tm, w), lambda i: (i, 0))
    return _call(
        body, "rglru_fwd", (ni,),
        [pl.BlockSpec((tm, w), lambda i: (i, 0)), pl.BlockSpec((tm, w), lambda i: (i, 1)),
         pl.BlockSpec((CONV_W, w), lambda i: (0, 0)), vec, sq, vec, sq, vec, vec], [out, out],
        [jax.ShapeDtypeStruct((s, w), F32), jax.ShapeDtypeStruct((s, w), F32)],
        [proj, proj, cw, cb, wa, ba, wx, bx, lam],
        scratch=[pltpu.VMEM((8, w), F32), pltpu.VMEM((1, w), F32)], rider=rider)


def _rglru_bwd(proj, hseq, dyr, cw, cb, wa, ba, wx, bx, lam):
    s = proj.shape[0]
    w = D_RNN
    tm = _tile(s)
    nt = s // tm
    t8 = tm // 8

    def body(xr_ref, xp_ref, gate_ref, h_ref, hp_ref, dy_ref, cw_ref, cb_ref, wa_ref, ba_ref,
             wx_ref, bx_ref, lam_ref,
             dxr_ref, dgate_ref, dcw_ref, dcb_ref, dwa_ref, dba_ref, dwx_ref, dbx_ref, dlam_ref,
             carry, dxc_next):
        i = pl.program_id(0)
        first_tile = i == nt - 1

        @pl.when(i == 0)
        def _():
            carry[...] = jnp.zeros_like(carry)
            dxc_next[...] = jnp.zeros_like(dxc_next)
            for ref in (dcw_ref, dcb_ref, dwa_ref, dba_ref, dwx_ref, dbx_ref, dlam_ref):
                ref[...] = jnp.zeros_like(ref)

        xv = xr_ref[...]
        prev8 = jnp.where(first_tile, 0.0, xp_ref[...])
        hprev8 = jnp.where(first_tile, 0.0, hp_ref[...])
        (x1, x2, x3), xc, xcb, r, ig, c, a, m = _rglru_gates(
            xv, prev8, cw_ref, cb_ref, wa_ref, ba_ref, wx_ref, bx_ref, lam_ref)
        gv = gate_ref[...]
        hv = h_ref[...]
        dy = dy_ref[...]
        dgate_ref[...] = (dy * hv * _gelu_grad(gv)).astype(BF16)
        dh = dy * _gelu(gv)
        row = lax.broadcasted_iota(jnp.int32, dh.shape, 0)
        dh = jnp.where(row == tm - 1, dh + carry[...], dh)
        a_up = jnp.where(row == tm - 1, 0.0, pltpu.roll(a, tm - 1, 0))
        lam_t = _scan_bwd(a_up, dh)
        carry[...] = a[0:1, :] * lam_t[0:1, :]
        hm1 = _shift_down(hv, 1, hprev8)
        da = lam_t * hm1
        ixc = ig * xc
        dm = lam_t * ixc
        dig = lam_t * m * xc
        dxc = lam_t * m * ig
        dla = da * a - dm * (a * a) / m
        dr = dla * c
        dlam_ref[...] += jnp.sum(dla * r, axis=0, keepdims=True)
        dpa = dr * r * (1.0 - r)
        dpi = dig * ig * (1.0 - ig)
        dba_ref[...] += jnp.sum(dpa, axis=0, keepdims=True)
        dbx_ref[...] += jnp.sum(dpi, axis=0, keepdims=True)
        dpab = dpa.astype(BF16)
        dpib = dpi.astype(BF16)
        dwa_ref[...] += _dot_tn(xcb, dpab)
        dwx_ref[...] += _dot_tn(xcb, dpib)
        dxc = dxc + _dot_nt(dpab, wa_ref[...]) + _dot_nt(dpib, wx_ref[...])
        dcb_ref[...] += jnp.sum(dxc, axis=0, keepdims=True)
        dcw_ref[3:4, :] += jnp.sum(dxc * xv, axis=0, keepdims=True)
        dcw_ref[2:3, :] += jnp.sum(dxc * x1, axis=0, keepdims=True)
        dcw_ref[1:2, :] += jnp.sum(dxc * x2, axis=0, keepdims=True)
        dcw_ref[0:1, :] += jnp.sum(dxc * x3, axis=0, keepdims=True)
        nxt = dxc_next[...]
        dxr = (cw_ref[3:4, :] * dxc + cw_ref[2:3, :] * _shift_up(dxc, 1, nxt)
               + cw_ref[1:2, :] * _shift_up(dxc, 2, nxt) + cw_ref[0:1, :] * _shift_up(dxc, 3, nxt))
        dxr_ref[...] = dxr.astype(BF16)
        dxc_next[...] = dxc[0:8, :]

        @pl.when(first_tile)
        def _():
            lv = lam_ref[...]
            dlam_ref[...] = dlam_ref[...] * (RG_C * _sigmoid(-lv))

    rev = lambda i: nt - 1 - i
    vec = pl.BlockSpec((1, w), lambda i: (0, 0))
    sq = pl.BlockSpec((w, w), lambda i: (0, 0))
    cur = lambda col: pl.BlockSpec((tm, w), lambda i: (rev(i), col))
    before = lambda cols: pl.BlockSpec((8, w), lambda i: (jnp.maximum(rev(i) * t8 - 1, 0), 0))
    return pl.pallas_call(
        body, name="rglru_bwd", grid=(nt,),
        in_specs=[cur(0), before(None), cur(1), cur(0), before(None), cur(0),
                  pl.BlockSpec((CONV_W, w), lambda i: (0, 0)), vec, sq, vec, sq, vec, vec],
        out_specs=[cur(0), cur(0), pl.BlockSpec((CONV_W, w), lambda i: (0, 0)), vec, sq, vec, sq, vec, vec],
        out_shape=[jax.ShapeDtypeStruct((s, w), BF16), jax.ShapeDtypeStruct((s, w), BF16),
                   jax.ShapeDtypeStruct((CONV_W, w), F32), jax.ShapeDtypeStruct((1, w), F32),
                   jax.ShapeDtypeStruct((w, w), F32), jax.ShapeDtypeStruct((1, w), F32),
                   jax.ShapeDtypeStruct((w, w), F32), jax.ShapeDtypeStruct((1, w), F32),
                   jax.ShapeDtypeStruct((1, w), F32)],
        scratch_shapes=[pltpu.VMEM((1, w), F32), pltpu.VMEM((8, w), F32)],
        compiler_params=_params(("arbitrary",)),
    )(proj, proj, proj, hseq, hseq, dyr, cw, cb, wa, ba, wx, bx, lam)


def _sb_logs(z, valid):
    lb = jnp.minimum(z, 0.0) - jnp.log(1.0 + jnp.exp(-jnp.abs(z)))
    return lb, jnp.where(valid, lb - z, 0.0)


class _Window:
    def __init__(self):
        blk, win, cut = ATT_BLOCK, ATT_WINDOW, ATT_SPLIT
        self.row = lax.broadcasted_iota(jnp.int32, (blk, win), 0)
        self.col = lax.broadcasted_iota(jnp.int32, (blk, win), 1)

        def tri(n, later):
            j = lax.broadcasted_iota(jnp.int32, (n, n), 0)
            s = lax.broadcasted_iota(jnp.int32, (n, n), 1)
            return jnp.where((j > s) if later else (j < s), 1.0, 0.0).astype(BF16)

        self.later = (tri(cut, True), tri(win - cut, True))
        self.earlier = (tri(cut, False), tri(win - cut, False))

    def place(self, qi, g):
        end = (qi + 1) * ATT_BLOCK - g * ATT_WINDOW
        start = pl.multiple_of(jnp.maximum(end - ATT_WINDOW, 0), ATT_BLOCK)
        valid = self.col < jnp.minimum(self.row + (qi * ATT_BLOCK - start), end - start)
        return start, valid

    @staticmethod
    def _parts(xv):
        hi = xv.astype(BF16)
        lo = (xv - hi.astype(F32)).astype(BF16)
        cut = ATT_SPLIT
        sums = (jnp.sum(xv[:, :cut], axis=1, keepdims=True), jnp.sum(xv[:, cut:], axis=1, keepdims=True))
        return (hi[:, :cut], lo[:, :cut]), (hi[:, cut:], lo[:, cut:]), sums

    def sums_after(self, xv, carry):
        (h0, l0), (h1, l1), (s0, s1) = self._parts(xv)
        first = _dot(h0, self.later[0]) + _dot(l0, self.later[0]) + (s1 + carry)
        last = _dot(h1, self.later[1]) + _dot(l1, self.later[1]) + carry
        return jnp.concatenate([first, last], axis=1), s0 + s1

    def sums_before(self, xv, carry):
        (h0, l0), (h1, l1), (s0, s1) = self._parts(xv)
        first = _dot(h0, self.earlier[0]) + _dot(l0, self.earlier[0]) + carry
        last = _dot(h1, self.earlier[1]) + _dot(l1, self.earlier[1]) + (s0 + carry)
        return jnp.concatenate([first, last], axis=1), s0 + s1


class _HeadPair:
    def __init__(self):
        lanes = 2 * HEAD_DIM
        lane = lax.broadcasted_iota(jnp.int32, (1, lanes), 1)
        self.masks = [lane // HEAD_DIM == h for h in (0, 1)]
        i = lax.broadcasted_iota(jnp.int32, (lanes, lanes), 0) // HEAD_DIM
        j = lax.broadcasted_iota(jnp.int32, (lanes, lanes), 1) // HEAD_DIM
        self.same_head = jnp.where(i == j, 1.0, 0.0).astype(BF16)

    def only(self, h, xv):
        return jnp.where(self.masks[h], xv, jnp.zeros_like(xv))

    def merge(self, per_head):
        return jnp.where(self.masks[0], per_head[0], per_head[1])

    def mean(self, xv):
        hi = xv.astype(BF16)
        lo = (xv - hi.astype(F32)).astype(BF16)
        return (_dot(hi, self.same_head) + _dot(lo, self.same_head)) * (1.0 / HEAD_DIM)

    def rms_r(self, xv):
        return lax.rsqrt(self.mean(xv * xv) + EPS)

    def rms_bwd(self, xv, r, nw, dh):
        t = dh * nw
        dx = r * t - xv * (r * r * r * self.mean(t * xv))
        dn = jnp.sum(dh * xv * r, axis=0, keepdims=True)
        return dx, dn[:, :HEAD_DIM] + dn[:, HEAD_DIM:]


def _attn_fwd(proj, qg, kg, rider=None):
    s = proj.shape[0]
    blk, win, dh = ATT_BLOCK, ATT_WINDOW, HEAD_DIM
    nq = s // blk
    scale = 1.0 / math.sqrt(dh)
    heads = (0, 1)
    blocks = (0, 1)
    assert s >= win and s % (blk * len(blocks)) == 0

    def body(*refs):
        (q_ref, k_ref, v_ref, qg_ref, kg_ref), (o_ref,), (qn, kn, vb), copies = _split_refs(refs, 5, 1, rider)
        finish = _ride(copies, pl.program_id(0) == 0, pl.program_id(0) == N_HEADS // 2 - 1)
        wd, hp = _Window(), _HeadPair()
        qv = q_ref[...]
        qn[...] = (qv * hp.rms_r(qv) * qg_ref[...] * scale).astype(BF16)
        kv = k_ref[...]
        kn[...] = (kv * hp.rms_r(kv) * kg_ref[...]).astype(BF16)
        vb[...] = v_ref[...].astype(BF16)

        def q_step(pair_i, _):
            qis = [2 * pair_i + b for b in blocks]
            chains = [(b, h) for b in blocks for h in heads]
            qoffs = [pl.multiple_of(qi * blk, blk) for qi in qis]
            qtiles = [qn[pl.ds(qoff, blk), :] for qoff in qoffs]
            qts = [hp.only(h, qtiles[b]) for b, h in chains]

            def more(carry):
                g, live = carry[:2]
                return jnp.logical_and((qis[-1] + 1) * blk - g * win > 0, live > 0)

            def window(carry):
                g, _, accs, runs = carry
                places = [wd.place(qi, g) for qi in qis]
                kts = [kn[pl.ds(start, win), :] for start, _ in places]
                zs = [_dot_nt(qts[c], kts[b]) for c, (b, h) in enumerate(chains)]
                logs = [_sb_logs(zs[c], places[b][1]) for c, (b, h) in enumerate(chains)]
                sums = [wd.sums_after(logs[c][1], runs[c]) for c in range(len(chains))]
                wgts = [jnp.where(places[b][1], jnp.exp(logs[c][0] + sums[c][0]), 0.0).astype(BF16)
                        for c, (b, h) in enumerate(chains)]
                vts = [vb[pl.ds(start, win), :] for start, _ in places]
                accs = tuple(accs[c] + _dot(wgts[c], vts[b]) for c, (b, h) in enumerate(chains))
                runs = tuple(runs[c] + sums[c][1] for c in range(len(chains)))
                top = functools.reduce(jnp.maximum, [jnp.max(r) for r in runs])
                return g + 1, (top > EXP_ZERO).astype(jnp.int32), accs, runs

            zero = lambda cols: tuple(jnp.zeros((blk, cols), F32) for _ in chains)
            _, _, accs, _ = lax.while_loop(more, window, (jnp.int32(0), jnp.int32(1), zero(2 * dh), zero(1)))
            for b in blocks:
                o_ref[pl.ds(qoffs[b], blk), :] = hp.merge([accs[2 * b + h] for h in heads])
            return 0

        lax.fori_loop(0, nq // len(blocks), q_step, 0)
        finish()

    pair = lambda group: pl.BlockSpec((s, 2 * dh), lambda p: (0, group * (D_ATT // (2 * dh)) + p))
    vec = pl.BlockSpec((1, 2 * dh), lambda p: (0, 0))
    return _call(
        body, "attn_fwd", (N_HEADS // 2,), [pair(2), pair(3), pair(4), vec, vec], [pair(0)],
        [jax.ShapeDtypeStruct((s, D_ATT), F32)], [proj, proj, proj, jnp.tile(qg, (1, 2)), jnp.tile(kg, (1, 2))],
        scratch=[pltpu.VMEM((s, 2 * dh), BF16)] * 3, rider=rider)


def _attn_bwd(proj, dya, qg, kg, rider=None):
    s = proj.shape[0]
    blk, win, dh = ATT_BLOCK, ATT_WINDOW, HEAD_DIM
    nq = s // blk
    max_windows = -(-s // win) + 1
    scale = 1.0 / math.sqrt(dh)
    steps = N_HEADS // 2
    heads = (0, 1)
    blocks = (0, 1)
    assert s >= win and s % (blk * len(blocks)) == 0

    def body(*refs):
        ins, outs, scratch, copies = _split_refs(refs, 6, 5, rider)
        q_ref, k_ref, v_ref, do_ref, qg_ref, kg_ref = ins
        dq_ref, dk_ref, dv_ref, dqg_ref, dkg_ref = outs
        qn, kn, vb, dob, runs_ref, dqn, dkn, dvn = scratch
        finish = _ride(copies, pl.program_id(0) == 0, pl.program_id(0) == steps - 1)
        wd, hp = _Window(), _HeadPair()

        @pl.when(pl.program_id(0) == 0)
        def _():
            dqg_ref[...] = jnp.zeros_like(dqg_ref)
            dkg_ref[...] = jnp.zeros_like(dkg_ref)

        qv = q_ref[...]
        qn[...] = (qv * hp.rms_r(qv) * qg_ref[...] * scale).astype(BF16)
        kv = k_ref[...]
        kn[...] = (kv * hp.rms_r(kv) * kg_ref[...]).astype(BF16)
        vb[...] = v_ref[...].astype(BF16)
        dob[...] = do_ref[...].astype(BF16)
        dkn[...] = jnp.zeros_like(dkn)
        dvn[...] = jnp.zeros_like(dvn)

        def q_step(pair_i, _):
            qis = [2 * pair_i + b for b in blocks]
            chains = [(b, h) for b in blocks for h in heads]
            ids = range(len(chains))
            qoffs = [pl.multiple_of(qi * blk, blk) for qi in qis]
            qts = [hp.only(h, qn[pl.ds(qoffs[b], blk), :]) for b, h in chains]
            dots = [hp.only(h, dob[pl.ds(qoffs[b], blk), :]) for b, h in chains]

            zero = lambda cols: tuple(jnp.zeros((blk, cols), F32) for _ in chains)

            def logs_of(g):
                places = [wd.place(qi, g) for qi in qis]
                kts = [kn[pl.ds(start, win), :] for start, _ in places]
                return [_sb_logs(_dot_nt(qts[c], kts[b]), places[b][1]) for c, (b, h) in enumerate(chains)]

            def row_sums(logs):
                return tuple(jnp.sum(logs[c][1], axis=1, keepdims=True) for c in ids)

            def still_live(runs):
                return functools.reduce(jnp.maximum, [jnp.max(r) for r in runs]) > EXP_ZERO

            def window_grads(g, logs, runs, esums):
                places = [wd.place(qi, g) for qi in qis]
                kts = [kn[pl.ds(start, win), :] for start, _ in places]
                vts = [vb[pl.ds(start, win), :] for start, _ in places]
                dws = [_dot_nt(dots[c], vts[b]) for c, (b, h) in enumerate(chains)]
                tails = [wd.sums_after(logs[c][1], runs[c])[0] for c in ids]
                wgts = [jnp.where(places[b][1], jnp.exp(logs[c][0] + tails[c]), 0.0) for c, (b, h) in enumerate(chains)]
                es = [dws[c] * wgts[c] for c in ids]
                befores = [wd.sums_before(es[c], esums[c]) for c in ids]
                dzbs = []
                for c, (b, h) in enumerate(chains):
                    beta = jnp.exp(logs[c][0])
                    dz = jnp.where(places[b][1], es[c] * (1.0 - beta) - befores[c][0] * beta, 0.0)
                    dzbs.append(dz.astype(BF16))
                for b in blocks:
                    rows = pl.ds(places[b][0], win)
                    dkn[rows, :] += _dot_tn(dzbs[2 * b], qts[2 * b]) + _dot_tn(dzbs[2 * b + 1], qts[2 * b + 1])
                    dvn[rows, :] += (_dot_tn(wgts[2 * b].astype(BF16), dots[2 * b])
                                     + _dot_tn(wgts[2 * b + 1].astype(BF16), dots[2 * b + 1]))
                return (tuple(_dot(dzbs[c], kts[b]) for c, (b, h) in enumerate(chains)),
                        tuple(befores[c][1] for c in ids))

            logs0 = logs_of(0)
            runs1 = row_sums(logs0)

            def one_window():
                return window_grads(0, logs0, zero(1), zero(1))[0]

            def all_windows():
                def more(carry):
                    g, live = carry[:2]
                    return jnp.logical_and((qis[-1] + 1) * blk - g * win > 0, live > 0)

                def run_window(carry):
                    g, _, runs = carry
                    for c in ids:
                        runs_ref[c, g] = runs[c]
                    sums = row_sums(logs_of(g))
                    runs = tuple(runs[c] + sums[c] for c in ids)
                    return g + 1, still_live(runs).astype(jnp.int32), runs

                for c in ids:
                    runs_ref[c, 0] = jnp.zeros((blk, 1), F32)
                windows, _, _ = lax.while_loop(more, run_window, (jnp.int32(1), jnp.int32(1), runs1))

                def k_window(gg, carry):
                    dq_accs, esums = carry
                    g = windows - 1 - gg
                    parts, totals = window_grads(g, logs_of(g), [runs_ref[c, g] for c in ids], esums)
                    return (tuple(dq_accs[c] + parts[c] for c in ids), tuple(esums[c] + totals[c] for c in ids))

                return lax.fori_loop(0, windows, k_window, (zero(2 * dh), zero(1)))[0]

            earlier_keys = (qis[-1] + 1) * blk - win > 0
            dq_accs = lax.cond(jnp.logical_and(earlier_keys, still_live(runs1)), all_windows, one_window)
            for b in blocks:
                dqn[pl.ds(qoffs[b], blk), :] = hp.merge([dq_accs[2 * b + h] for h in heads])
            return 0

        lax.fori_loop(0, nq // len(blocks), q_step, 0)

        dq, dqg = hp.rms_bwd(qv, hp.rms_r(qv), qg_ref[...] * scale, dqn[...])
        dq_ref[...] = dq.astype(BF16)
        dqg_ref[...] += dqg * scale
        dk, dkg = hp.rms_bwd(kv, hp.rms_r(kv), kg_ref[...], dkn[...])
        dk_ref[...] = dk.astype(BF16)
        dkg_ref[...] += dkg
        dv_ref[...] = dvn[...].astype(BF16)
        finish()

    pair = lambda group: pl.BlockSpec((s, 2 * dh), lambda p: (0, group * (D_ATT // (2 * dh)) + p))
    vec2 = pl.BlockSpec((1, 2 * dh), lambda p: (0, 0))
    vec = pl.BlockSpec((1, dh), lambda p: (0, 0))
    return _call(
        body, "attn_bwd", (steps,), [pair(2), pair(3), pair(4), pair(0), vec2, vec2],
        [pair(0), pair(0), pair(0), vec, vec],
        [jax.ShapeDtypeStruct((s, D_ATT), BF16)] * 3 + [jax.ShapeDtypeStruct((1, dh), F32)] * 2,
        [proj, proj, proj, dya, jnp.tile(qg, (1, 2)), jnp.tile(kg, (1, 2))],
        scratch=[pltpu.VMEM((s, 2 * dh), BF16)] * 4 + [pltpu.VMEM((4, max_windows, blk, 1), F32)]
        + [pltpu.VMEM((s, 2 * dh), F32)] * 3, rider=rider)


def _block_diag(w):
    n, c, d = w.shape
    return jnp.einsum("ncd,nm->ncmd", w, jnp.eye(n, dtype=w.dtype)).reshape(n * c, n * d)


def _diag_blocks(full, n):
    c = full.shape[0] // n
    on_diagonal = jnp.eye(n, dtype=bool)[:, None, :, None]
    return jnp.sum(jnp.where(on_diagonal, full.reshape(n, c, n, c), 0.0), axis=2)


FFN1 = ["ffn1_w_gate", "ffn1_w_up", "ffn1_w_down"]
FFN2 = ["ffn2_w_gate", "ffn2_w_up", "ffn2_w_down"]


def _pair_sums(gb, names, where):
    theirs = _pair_exchange([gb[n] for n in names], "pair_exchange_" + names[0])
    pair, own = _pair_sum([gb[n] for n in names], theirs, where, "pair_sum_" + names[0])
    return _chip_rider(pair, own)


def _local_step(x, tgt, stacks, conv_stack, small, where):
    gate_up, down = FFN1[:2], FFN1[2:]
    big = dict(zip(gate_up, _gather_weights([stacks[n] for n in gate_up], [])))
    wa = _block_diag(small["rg_w_a"]).astype(BF16)
    wx = _block_diag(small["rg_w_x"]).astype(BF16)

    whole = lambda names: [big[n].reshape(-1, D_MODEL) for n in names]
    soon = down + ["w_in"]
    g1, u1, hb1, ab1, *landed = _ffn_up(x, small["ffn1_norm"], *whole(gate_up),
                                        rider=_gather_rider([stacks[n] for n in soon], [conv_stack]))
    big.update(zip(soon, landed))
    x1 = _ffn_down(x, ab1, *whole(down))
    conv_w = jnp.transpose(landed[-1], (1, 0, 2)).reshape(CONV_W, D_RNN)
    rg = (conv_w, small["conv_b"], wa, small["rg_b_a"], wx, small["rg_b_x"], small["rg_lambda"])
    riding = lambda names: _gather_rider([stacks[n] for n in names], [])
    proj, hb2, big["ffn2_w_gate"] = _mix_pre(x1, small["mix_norm"], big["w_in"], riding(["ffn2_w_gate"]))
    yr, hseq, big["ffn2_w_up"] = _rglru_fwd(proj, *rg, riding(["ffn2_w_up"]))
    ya, big["ffn2_w_down"], big["w_out"] = _attn_fwd(proj, small["q_norm"], small["k_norm"],
                                                     riding(["ffn2_w_down", "w_out"]))
    wout = big["w_out"].reshape(D_MODEL, D_MODEL)
    x2 = _mix_post(x1, yr, ya, small["rnn_out_norm"], small["attn_out_norm"], wout)
    dx3, g2, u2, hb3, ab3, loss = _ffn_fwd_loss(x2, small["ffn2_norm"], *whole(FFN2), tgt)

    gb, gs, slots = {}, {}, {}
    dx2, dg2, du2, dyb2, gs["ffn2_norm"] = _ffn_bwd_act(x2, small["ffn2_norm"], dx3, g2, u2, *whole(FFN2), "ffn2_bwd")
    gb["ffn2_w_gate"] = _ffn_wgrad(dg2, hb3, 1.0, "wgrad_gate_ffn2")
    gb["ffn2_w_up"] = _ffn_wgrad(du2, hb3, 1.0, "wgrad_up_ffn2")
    gb["ffn2_w_down"] = _ffn_wgrad(ab3, dyb2, 0.5, "wgrad_down_ffn2")
    dyr, dya, ycat, dxb2, gs["rnn_out_norm"], gs["attn_out_norm"] = _mix_post_bwd(
        dx2, yr, ya, small["rnn_out_norm"], small["attn_out_norm"], wout)
    gb["w_out"] = _wgrad_whole(ycat, [dxb2], False, "wgrad_out")
    early = FFN2 + ["w_out"]
    dq, dk, dv, gs["q_norm"], gs["k_norm"], *done = _attn_bwd(
        proj, dya, small["q_norm"], small["k_norm"], _pair_sums(gb, early, where))
    slots.update(zip(early, done))
    dxr, dgate, gs["conv_w"], gs["conv_b"], dwa, gs["rg_b_a"], dwx, gs["rg_b_x"], gs["rg_lambda"] = _rglru_bwd(
        proj, hseq, dyr, *rg)
    gs["rg_w_a"] = _diag_blocks(dwa, RNN_BLOCKS)
    gs["rg_w_x"] = _diag_blocks(dwx, RNN_BLOCKS)
    dps = [dxr, dgate, dq, dk, dv]
    dx1, gs["mix_norm"] = _mix_pre_bwd(x1, small["mix_norm"], dx2, dps, big["w_in"])
    dx0, dg1, du1, dyb1, gs["ffn1_norm"] = _ffn_bwd_act(x, small["ffn1_norm"], dx1, g1, u1, *whole(FFN1), "ffn1_bwd")

    mine = _place_shard(_pack([gs[n] for n in SMALL] + [loss[:, :1]]), where, F32, "place_small_grads",
                        by_device=True)
    gb["ffn1_w_gate"], everyone = _ffn_wgrad(dg1, hb1, 1.0, "wgrad_gate_ffn1", _small_rider(mine))
    gb["ffn1_w_up"], slots["ffn1_w_gate"] = _ffn_wgrad(
        du1, hb1, 1.0, "wgrad_up_ffn1", _pair_sums(gb, ["ffn1_w_gate"], where))
    gb["ffn1_w_down"], slots["ffn1_w_up"] = _ffn_wgrad(
        ab1, dyb1, 0.5, "wgrad_down_ffn1", _pair_sums(gb, ["ffn1_w_up"], where))
    gb["w_in"], slots["ffn1_w_down"] = _wgrad_whole(
        hb2, dps, True, "wgrad_in", _pair_sums(gb, ["ffn1_w_down"], where))
    last = _pair_sums(gb, ["w_in"], where)
    slots["w_in"], = _chip_exchange(last.plain, last.inplace)
    return dx0, slots, gs, everyone


ANY = pl.BlockSpec(memory_space=pl.ANY)


def _place():
    x, y, c = lax.axis_index("x"), lax.axis_index("y"), lax.axis_index("c")
    other_chips = [(1 - x, y), (x, 1 - y), (1 - x, 1 - y)]
    return x, y, c, 2 * x + y, other_chips


def _remote(src, dst, send_sem, recv_sem, to):
    return pltpu.make_async_remote_copy(src_ref=src, dst_ref=dst, send_sem=send_sem, recv_sem=recv_sem,
                                        device_id=to, device_id_type=MESH)


def _copy_plan(pairs):
    sends = [functools.partial(_remote, *a) for a, _ in pairs]
    arrivals = [functools.partial(_remote, *b) for _, b in pairs]
    return sends, arrivals


class _Rider:
    def __init__(self, plan, plain, inplace, n_copies=None, relay=None, n_relay=0):
        self.plan, self.plain, self.inplace = plan, list(plain), list(inplace)
        self.n_copies = n_copies or 3 * len(self.inplace)
        self.relay, self.n_relay = relay, n_relay

    def operands(self):
        return self.plain + self.inplace

    def out_shape(self):
        return [jax.ShapeDtypeStruct(a.shape, a.dtype) for a in self.inplace]

    def aliases(self, inputs_before, outputs_before):
        return {inputs_before + len(self.plain) + k: outputs_before + k for k in range(len(self.inplace))}

    def scratch(self):
        relay = [pltpu.SemaphoreType.DMA((self.n_relay,))] * 2 if self.relay else []
        return [pltpu.SemaphoreType.DMA((self.n_copies,))] * 2 + relay


def _split_refs(refs, n_in, n_out, rider):
    if rider is None:
        return refs[:n_in], refs[n_in:n_in + n_out], refs[n_in + n_out:], None
    r_in, r_out = len(rider.operands()), len(rider.inplace)
    outs_at = n_in + r_in
    n_sems = len(rider.scratch())
    rest = refs[outs_at + n_out + r_out:]
    sems = rest[len(rest) - n_sems:]
    filled = refs[outs_at + n_out:outs_at + n_out + r_out]
    copies = functools.partial(rider.plan, refs[n_in:n_in + len(rider.plain)], filled, *sems[:2])
    relay = functools.partial(rider.relay, filled, *sems[2:]) if rider.relay else None
    return refs[:n_in], refs[outs_at:outs_at + n_out], rest[:len(rest) - n_sems], (copies, relay)


def _ride(copies, first, last, middle=None):
    if copies is None:
        return lambda: None
    copies, relay = copies

    @pl.when(first)
    def _():
        _start(copies()[0])

    def start_relay():
        for make in copies()[1]:
            make().wait_recv()
        _start(relay()[0])

    if relay is not None and middle is not None:
        pl.when(middle)(start_relay)

    def finish():
        @pl.when(last)
        def _():
            if relay is None:
                _finish(*copies())
            else:
                if middle is None:
                    start_relay()
                _finish(copies()[0] + relay()[0], relay()[1])

    return finish


def _gather_rider(split, whole):
    n_split = len(split)
    return _Rider(lambda plain, stacks, ss, rs: _gather_ici(stacks, n_split, ss, rs), [], list(split) + list(whole),
                  relay=lambda stacks, ss, rs: _gather_d2d(stacks[:n_split], ss, rs), n_relay=3 * n_split)


def _chip_rider(sums, slots):
    return _Rider(_chip_copies, sums, slots)


def _start(makers):
    for make in makers:
        make().start()


def _finish(sends, arrivals):
    for make in arrivals:
        make().wait_recv()
    for make in sends:
        make().wait_send()


def _half(rows, c):
    return pl.ds(pl.multiple_of(c * rows, BF16_ROWS), rows)


def _gather_weights(split, whole):
    arrs = list(split) + list(whole)
    n, ns = len(arrs), len(split)

    def body(*refs):
        outs = refs[n:2 * n]
        send_sems, recv_sems, fsend_sems, frecv_sems = refs[2 * n:]
        sends, arrivals = _gather_ici(outs, ns, send_sems, recv_sems)
        passes, passed = _gather_d2d(outs[:ns], fsend_sems, frecv_sems)
        _start(sends)
        for k, make in enumerate(arrivals):
            make().wait_recv()
            if k < 3 * ns:
                passes[k]().start()
        _finish(sends + passes, passed)

    return pl.pallas_call(
        body, name="gather_weights",
        in_specs=[ANY] * n, out_specs=[ANY] * n,
        out_shape=[jax.ShapeDtypeStruct(a.shape, a.dtype) for a in arrs],
        input_output_aliases={i: i for i in range(n)},
        scratch_shapes=[pltpu.SemaphoreType.DMA((3 * n,)), pltpu.SemaphoreType.DMA((3 * n,)),
                        pltpu.SemaphoreType.DMA((3 * ns,)), pltpu.SemaphoreType.DMA((3 * ns,))],
    )(*arrs)


def _gather_ici(stacks, n_split, send_sems, recv_sems):
    x, y, c, me, chips = _place()

    def region(i, chip):
        if i < n_split:
            return stacks[i].at[chip, _half(stacks[i].shape[1] // 2, c)]
        return stacks[i].at[chip]

    pairs = []
    for i in range(len(stacks)):
        for p, (cx, cy) in enumerate(chips):
            k = 3 * i + p
            mine, got = region(i, me), region(i, 2 * cx + cy)
            sems, to = (send_sems.at[k], recv_sems.at[k]), (cx, cy, c)
            pairs.append(((mine, mine, *sems, to), (got, got, *sems, to)))
    return _copy_plan(pairs)


def _gather_d2d(stacks, send_sems, recv_sems):
    x, y, c, _, chips = _place()
    sibling = (x, y, 1 - c)
    pairs = []
    for i, stack in enumerate(stacks):
        rows = stack.shape[1] // 2
        for p, (cx, cy) in enumerate(chips):
            k = 3 * i + p
            got, theirs = stack.at[2 * cx + cy, _half(rows, c)], stack.at[2 * cx + cy, _half(rows, 1 - c)]
            sems = (send_sems.at[k], recv_sems.at[k])
            pairs.append(((got, got, *sems, sibling), (theirs, theirs, *sems, sibling)))
    return _copy_plan(pairs)


def _pair_exchange(grads, name):
    n = len(grads)

    def body(*refs):
        ins, theirs = refs[:n], refs[n:2 * n]
        send_sems, recv_sems = refs[2 * n:]
        x, y, c, _, _ = _place()
        sibling = (x, y, 1 - c)
        sends = [_remote(ins[k].at[:, _half(grads[k].shape[1] // 2, 1 - c)], theirs[k],
                         send_sems.at[k], recv_sems.at[k], sibling) for k in range(n)]
        for cp in sends:
            cp.start()
        for k in range(n):
            _remote(theirs[k], theirs[k], send_sems.at[k], recv_sems.at[k], sibling).wait_recv()
        for cp in sends:
            cp.wait_send()

    return pl.pallas_call(
        body, name=name,
        in_specs=[ANY] * n, out_specs=[ANY] * n,
        out_shape=[jax.ShapeDtypeStruct((g.shape[0], g.shape[1] // 2, g.shape[2]), g.dtype) for g in grads],
        scratch_shapes=[pltpu.SemaphoreType.DMA((n,))] * 2,
    )(*grads)


def _chip_exchange(sums, slots):
    n = len(sums)

    def body(*refs):
        sends, arrivals = _chip_copies(refs[:n], refs[2 * n:3 * n], *refs[3 * n:])
        _start(sends)
        _finish(sends, arrivals)

    return pl.pallas_call(
        body, name="grad_chip_exchange",
        in_specs=[ANY] * (2 * n), out_specs=[ANY] * n,
        out_shape=[jax.ShapeDtypeStruct(a.shape, a.dtype) for a in slots],
        input_output_aliases={n + k: k for k in range(n)},
        scratch_shapes=[pltpu.SemaphoreType.DMA((3 * n,)), pltpu.SemaphoreType.DMA((3 * n,))],
    )(*sums, *slots)


def _chip_copies(sums, slots, send_sems, recv_sems):
    x, y, c, me, chips = _place()
    pairs = []
    for k in range(len(sums)):
        for p, (cx, cy) in enumerate(chips):
            j = 3 * k + p
            got = slots[k].at[2 * cx + cy]
            sems, to = (send_sems.at[j], recv_sems.at[j]), (cx, cy, c)
            pairs.append(((sums[k].at[2 * cx + cy], slots[k].at[me], *sems, to), (got, got, *sems, to)))
    return _copy_plan(pairs)


def _half_swap(halves):
    n = len(halves)

    def body(*refs):
        outs = refs[n:2 * n]
        send_sems, recv_sems = refs[2 * n:]
        x, y, c, _, _ = _place()
        sibling = (x, y, 1 - c)
        sends = [_remote(outs[k].at[c], outs[k].at[c], send_sems.at[k], recv_sems.at[k], sibling) for k in range(n)]
        for cp in sends:
            cp.start()
        for k in range(n):
            got = outs[k].at[1 - c]
            _remote(got, got, send_sems.at[k], recv_sems.at[k], sibling).wait_recv()
        for cp in sends:
            cp.wait_send()

    return pl.pallas_call(
        body, name="grad_half_swap",
        in_specs=[ANY] * n, out_specs=[ANY] * n,
        out_shape=[jax.ShapeDtypeStruct(a.shape, a.dtype) for a in halves],
        input_output_aliases={k: k for k in range(n)},
        scratch_shapes=[pltpu.SemaphoreType.DMA((n,))] * 2,
    )(*halves)


def _small_rider(stack):
    n_dev = 2 * N_CHIPS

    def plan(_, stacks, send_sems, recv_sems):
        x, y, c, _, _ = _place()
        mine = stacks[0].at[4 * x + 2 * y + c]
        pairs = []
        for k in range(1, n_dev):
            px, py, pc = x ^ ((k >> 2) & 1), y ^ ((k >> 1) & 1), c ^ (k & 1)
            got = stacks[0].at[4 * px + 2 * py + pc]
            sems = (send_sems.at[k - 1], recv_sems.at[k - 1])
            pairs.append(((mine, mine, *sems, (px, py, pc)), (got, got, *sems, (px, py, pc))))
        return _copy_plan(pairs)

    return _Rider(plan, [], [stack], n_dev - 1)


def _row_tile(r):
    return r // 4 if r >= 256 and (r // 4) % BF16_ROWS == 0 else r


def _prefetch_call(body, name, grid, in_specs, out_specs, out_shape):
    spec = pltpu.PrefetchScalarGridSpec(num_scalar_prefetch=1, grid=grid, in_specs=in_specs, out_specs=out_specs)
    return pl.pallas_call(body, name=name, grid_spec=spec, out_shape=out_shape,
                          compiler_params=_params(("arbitrary",) * len(grid)))


def _place_shard(w2d, where, dtype, name, by_device=False):
    r, c = w2d.shape
    tr = _row_tile(r)
    slots = 2 * N_CHIPS if by_device else N_CHIPS
    slot = (lambda s: 2 * s[1] + s[0]) if by_device else (lambda s: s[1])

    def body(where_ref, w_ref, out_ref):
        out_ref[...] = w_ref[...].astype(dtype)

    return _prefetch_call(
        body, name, (r // tr,), [pl.BlockSpec((tr, c), lambda i, s: (i, 0))],
        pl.BlockSpec((None, tr, c), lambda i, s: (slot(s), i, 0)),
        jax.ShapeDtypeStruct((slots, r, c), dtype))(where, w2d)


def _place_shards(w2ds, where, name):
    n = len(w2ds)
    steps = N_CHIPS
    assert all(w.shape[0] % (BF16_ROWS * steps) == 0 for w in w2ds)

    def body(where_ref, *refs):
        for k in range(n):
            refs[n + k][...] = refs[k][...].astype(BF16)

    tile = lambda w: (w.shape[0] // steps, w.shape[1])
    return _prefetch_call(
        body, name, (steps,), [pl.BlockSpec(tile(w), lambda i, s: (i, 0)) for w in w2ds],
        [pl.BlockSpec((None,) + tile(w), lambda i, s: (s[1], i, 0)) for w in w2ds],
        [jax.ShapeDtypeStruct((N_CHIPS,) + w.shape, BF16) for w in w2ds])(where, *w2ds)


def _pair_sum(fulls, theirs, where, name):
    n = len(fulls)

    def body(where_ref, *refs):
        for k in range(n):
            a_ref, b_ref, out_ref, own_ref = refs[k], refs[n + k], refs[2 * n + k], refs[3 * n + k]
            total = (a_ref[...].astype(F32) + b_ref[...].astype(F32)).astype(BF16)
            out_ref[...] = total

            @pl.when(pl.program_id(0) == where_ref[1])
            def _():
                own_ref[...] = total

    half = lambda t: pl.BlockSpec((None,) + t.shape[1:], lambda j, s: (j, s[0], 0))
    blk = lambda t: pl.BlockSpec((None,) + t.shape[1:], lambda j, s: (j, 0, 0))
    own = lambda t: pl.BlockSpec((None,) + t.shape[1:], lambda j, s: (s[1], 0, 0))
    shapes = [jax.ShapeDtypeStruct(t.shape, BF16) for t in theirs]
    outs = _prefetch_call(
        body, name, (N_CHIPS,), [half(t) for t in theirs] + [blk(t) for t in theirs],
        [blk(t) for t in theirs] + [own(t) for t in theirs], shapes + shapes)(where, *fulls, *theirs)
    return outs[:n], outs[n:]


def _chip_sum(slots, where, name):
    n = len(slots)
    steps = 2
    assert all(a.shape[1] % (BF16_ROWS * steps) == 0 for a in slots)

    def body(where_ref, *refs):
        for k in range(n):
            a_ref, out_ref = refs[k], refs[n + k]
            total = a_ref[0].astype(F32)
            for j in range(1, a_ref.shape[0]):
                total = total + a_ref[j].astype(F32)
            out_ref[...] = total

    tile = lambda a: (a.shape[1] // steps, a.shape[2])
    return _prefetch_call(
        body, name, (steps,), [pl.BlockSpec((a.shape[0],) + tile(a), lambda i, s: (0, i, 0)) for a in slots],
        [pl.BlockSpec((None,) + tile(a), lambda i, s: (s[0], i, 0)) for a in slots],
        [jax.ShapeDtypeStruct((2,) + a.shape[1:], F32) for a in slots])(where, *slots)


def _slot_sum(a, name):
    nb, r, c = a.shape
    tr = _row_tile(r)

    def body(a_ref, out_ref):
        total = a_ref[0].astype(F32)
        for j in range(1, nb):
            total = total + a_ref[j].astype(F32)
        out_ref[...] = total

    return pl.pallas_call(
        body, name=name, grid=(r // tr,),
        in_specs=[pl.BlockSpec((nb, tr, c), lambda i: (0, i, 0))],
        out_specs=pl.BlockSpec((tr, c), lambda i: (i, 0)),
        out_shape=jax.ShapeDtypeStruct((r, c), F32), compiler_params=_params(("arbitrary",)),
    )(a)


def _adamw(ws, gs, ms, vs, name, steps=1):
    n = len(ws)
    c1 = 1.0 - ADAM_B1 ** ADAM_STEP
    c2 = 1.0 - ADAM_B2 ** ADAM_STEP
    assert all(w.shape[0] % steps == 0 and (steps == 1 or w.shape[0] // steps % 8 == 0) for w in ws)

    def body(*refs):
        for k in range(n):
            w_ref, g_ref, m_ref, v_ref = (refs[j * n + k] for j in range(4))
            g_out, d_ref, m2_ref, v2_ref = (refs[(4 + j) * n + k] for j in range(4))
            gv = g_ref[...]
            g_out[...] = gv
            m2 = ADAM_B1 * m_ref[...] + (1.0 - ADAM_B1) * gv
            v2 = ADAM_B2 * v_ref[...] + (1.0 - ADAM_B2) * (gv * gv)
            m2_ref[...] = m2
            v2_ref[...] = v2
            d_ref[...] = -ADAM_LR * ((m2 / c1) / (jnp.sqrt(v2 / c2) + ADAM_EPS) + ADAM_WD * w_ref[...])

    blks = [pl.BlockSpec((w.shape[0] // steps, w.shape[1]), lambda i: (i, 0)) for w in ws]
    shapes = [jax.ShapeDtypeStruct(w.shape, F32) for w in ws]
    outs = pl.pallas_call(
        body, name=name, grid=(steps,), in_specs=blks * 4, out_specs=blks * 4, out_shape=shapes * 4,
        compiler_params=_params(("arbitrary",)),
    )(*ws, *gs, *ms, *vs)
    return [outs[j * n:(j + 1) * n] for j in range(4)]


WEIGHTS = ["ffn1_norm", "ffn1_w_gate", "ffn1_w_up", "ffn1_w_down", "mix_norm", "w_in", "conv_w", "conv_b",
           "rg_w_a", "rg_b_a", "rg_w_x", "rg_b_x", "rg_lambda", "q_norm", "k_norm", "rnn_out_norm",
           "attn_out_norm", "w_out", "ffn2_norm", "ffn2_w_gate", "ffn2_w_up", "ffn2_w_down"]
BIG = ["ffn1_w_gate", "ffn1_w_up", "ffn1_w_down", "w_in", "w_out", "ffn2_w_gate", "ffn2_w_up", "ffn2_w_down"]
SMALL = [n for n in WEIGHTS if n not in BIG]
PACK_LANES = 128
PACK_ROW_ALIGN = 8


def _hidden_major(name, a):
    return jnp.transpose(a) if name.endswith(("w_gate", "w_up")) else a


def _pack(parts):
    sizes = [math.prod(p.shape) for p in parts]
    unit = PACK_LANES * PACK_ROW_ALIGN
    padded = -(-sum(sizes) // unit) * unit
    flat, at = 0.0, 0
    for p, size in zip(parts, sizes):
        flat = flat + jnp.pad(p.reshape(-1), (at, padded - at - size))
        at += size
    return flat.reshape(-1, PACK_LANES)


def _unpack(packed, shapes):
    flat = packed.reshape(-1)
    out, at = [], 0
    for shp in shapes:
        size = math.prod(shp)
        out.append(flat[at:at + size].reshape(shp))
        at += size
    return out


def kernel(x, ffn1_norm, ffn1_w_gate, ffn1_w_up, ffn1_w_down, mix_norm, w_in, conv_w, conv_b, rg_w_a, rg_b_a, rg_w_x, rg_b_x, rg_lambda, q_norm, k_norm, rnn_out_norm, attn_out_norm, w_out, ffn2_norm, ffn2_w_gate, ffn2_w_up, ffn2_w_down, loss_target, m_ffn1_norm, m_ffn1_w_gate, m_ffn1_w_up, m_ffn1_w_down, m_mix_norm, m_w_in, m_conv_w, m_conv_b, m_rg_w_a, m_rg_b_a, m_rg_w_x, m_rg_b_x, m_rg_lambda, m_q_norm, m_k_norm, m_rnn_out_norm, m_attn_out_norm, m_w_out, m_ffn2_norm, m_ffn2_w_gate, m_ffn2_w_up, m_ffn2_w_down, v_ffn1_norm, v_ffn1_w_gate, v_ffn1_w_up, v_ffn1_w_down, v_mix_norm, v_w_in, v_conv_w, v_conv_b, v_rg_w_a, v_rg_b_a, v_rg_w_x, v_rg_b_x, v_rg_lambda, v_q_norm, v_k_norm, v_rnn_out_norm, v_attn_out_norm, v_w_out, v_ffn2_norm, v_ffn2_w_gate, v_ffn2_w_up, v_ffn2_w_down):
    given = dict(locals())
    w = {n: given[n] for n in WEIGHTS}
    m = {n: given["m_" + n] for n in WEIGHTS}
    v = {n: given["v_" + n] for n in WEIGHTS}
    chip = 2 * lax.axis_index("x") + lax.axis_index("y")

    where = jnp.stack([lax.axis_index("c"), chip]).astype(jnp.int32)

    stacks = dict(zip(BIG, _place_shards([_hidden_major(n, w[n][0]) for n in BIG], where, "place_weights")))
    conv_stack = _place_shard(w["conv_w"][0], where, F32, "place_conv_w")
    small = {n: (w[n][0] if w[n].ndim > 2 else w[n]) for n in SMALL if n != "conv_w"}

    grad_x, slots, gs, everyone = _local_step(x[0], loss_target[0], stacks, conv_stack, small, where)

    swapped = _half_swap(_chip_sum([slots[n] for n in BIG], where, "chip_sums"))
    g2s = [t.reshape(t.shape[0] * t.shape[1], t.shape[2]) for t in swapped]
    flat = lambda tree: [_hidden_major(n, tree[n][0]) for n in BIG]
    g2s, d2s, m2s, v2s = _adamw(flat(w), g2s, flat(m), flat(v), "adamw_weights", ADAMW_STEPS)
    grads, deltas, new_m, new_v = {}, {}, {}, {}
    for tree, parts in ((grads, g2s), (deltas, d2s), (new_m, m2s), (new_v, v2s)):
        tree.update({n: _hidden_major(n, a).reshape(w[n].shape) for n, a in zip(BIG, parts)})

    full_shapes = [gs[n].shape for n in SMALL]
    *summed, loss = _unpack(_slot_sum(everyone, "small_grad_sum"), full_shapes + [(1, 1)])
    g_parts = dict(zip(SMALL, summed))
    quarter = D_RNN // N_CHIPS
    g_parts["conv_w"] = lax.dynamic_slice_in_dim(g_parts["conv_w"], chip * quarter, quarter, axis=1)
    local_shapes = [w[n].shape for n in SMALL]
    pk = lambda tree: _pack([tree[n] for n in SMALL])
    (g_s,), (d_s,), (m_s,), (v_s,) = _adamw([pk(w)], [pk(g_parts)], [pk(m)], [pk(v)], "adamw_small")
    for tree, packed in ((grads, g_s), (deltas, d_s), (new_m, m_s), (new_v, v_s)):
        tree.update(zip(SMALL, _unpack(packed, local_shapes)))

    return (loss[0, 0], grad_x.reshape(x.shape), *[grads[n] for n in WEIGHTS], *[deltas[n] for n in WEIGHTS],
            *[new_m[n] for n in WEIGHTS], *[new_v[n] for n in WEIGHTS])
```

```python
import functools
import math

import jax
import jax.numpy as jnp
from jax import lax
from jax.experimental import pallas as pl
from jax.experimental.pallas import tpu as pltpu

F32 = jnp.float32
BF16 = jnp.bfloat16
MESH = pl.DeviceIdType.MESH

D_MODEL = 1024
N_CHIPS = 4
D_RNN = 512
D_ATT = 512
N_HEADS = 8
HEAD_DIM = 64
RNN_BLOCKS = 8
CONV_W = 4
RG_C = 8.0
N_IN = 2 * D_RNN + 3 * D_ATT
EPS = 1e-6
ATT_BLOCK = 128
ATT_WINDOW = 384
ATT_SPLIT = 256
EXP_ZERO = -105.0

ADAM_LR = 0.001
ADAM_B1 = 0.9
ADAM_B2 = 0.999
ADAM_EPS = 1e-08
ADAM_WD = 0.01
ADAM_STEP = 10

V7X_VMEM_LIMIT = 60 * 1024 * 1024
V7X_MXU_WIDTH = 256
TOKEN_TILE = 512
SUBLANES = 8
BF16_ROWS = 16
FFN_TILE = 256
WGRAD_TILE = 2048
WHOLE_TILE = 1024
ADAMW_STEPS = 8

GELU_K0 = math.sqrt(2.0 / math.pi)
GELU_K1 = 0.044715


def _params(sem=None):
    return pltpu.CompilerParams(dimension_semantics=sem, vmem_limit_bytes=V7X_VMEM_LIMIT)


def _dot(a, b):
    return jnp.dot(a, b, preferred_element_type=F32)


def _dot_nt(a, b):
    return lax.dot_general(a, b, (((1,), (1,)), ((), ())), preferred_element_type=F32)


def _dot_tn(a, b):
    return lax.dot_general(a, b, (((0,), (0,)), ((), ())), preferred_element_type=F32)


def _sigmoid(x):
    return 1.0 / (1.0 + jnp.exp(-x))


def _rms_r(xv):
    return lax.rsqrt(jnp.mean(xv * xv, axis=-1, keepdims=True) + EPS)


def _rms_bwd(xv, r, nw, dh):
    t = dh * nw
    dx = r * t - xv * (r * r * r * jnp.mean(t * xv, axis=-1, keepdims=True))
    dn = jnp.sum(dh * xv * r, axis=0, keepdims=True)
    return dx, dn


def _gelu(x):
    t = jnp.tanh(GELU_K0 * (x + GELU_K1 * x * x * x))
    return 0.5 * x * (1.0 + t)


def _gelu_grad(x):
    t = jnp.tanh(GELU_K0 * (x + GELU_K1 * x * x * x))
    return 0.5 * (1.0 + t) + 0.5 * x * (1.0 - t * t) * (GELU_K0 * (1.0 + 3.0 * GELU_K1 * x * x))


def _expm1_neg(x):
    p = 1.0 + x * (1.0 / 6.0)
    for k in (5.0, 4.0, 3.0, 2.0):
        p = 1.0 + x * (1.0 / k) * p
    return jnp.where(x > -0.25, x * p, jnp.exp(x) - 1.0)


def _log_sigmoid(x):
    return jnp.minimum(x, 0.0) - jnp.log(1.0 + jnp.exp(-jnp.abs(x)))


def _tile(s):
    return min(TOKEN_TILE, s)


def _ffn_chunks(f):
    cut = f // 2 // V7X_MXU_WIDTH * V7X_MXU_WIDTH
    return ((0, cut), (cut, f)) if 0 < cut < f else ((0, f),)


def _ffn_fwd_loss(x, nw, wg, wu, wd, tgt):
    s, d = x.shape
    f = wg.shape[0]
    tm = min(FFN_TILE, s)
    ni = s // tm
    assert s % tm == 0

    def body(x_ref, nw_ref, wg_ref, wu_ref, wd_ref, tgt_ref, out_ref, g_ref, u_ref, hb_ref, ab_ref, loss_ref):
        i = pl.program_id(0)
        xv = x_ref[...]
        hb = (xv * _rms_r(xv) * nw_ref[...]).astype(BF16)
        hb_ref[...] = hb
        y = jnp.zeros((tm, d), F32)
        for lo, hi in _ffn_chunks(f):
            g = _dot_nt(hb, wg_ref[lo:hi, :])
            u = _dot_nt(hb, wu_ref[lo:hi, :])
            g_ref[:, lo:hi] = g.astype(BF16)
            u_ref[:, lo:hi] = u.astype(BF16)
            ab = (g * _sigmoid(g) * u).astype(BF16)
            ab_ref[:, lo:hi] = ab
            y = y + _dot(ab, wd_ref[lo:hi, :])
        diff = xv + 0.5 * y - tgt_ref[...]
        out_ref[...] = diff * (1.0 / d)

        @pl.when(i == 0)
        def _():
            loss_ref[...] = jnp.zeros_like(loss_ref)

        loss_ref[...] += jnp.sum(diff * diff) * (0.5 / d)

    row = pl.BlockSpec((tm, d), lambda i: (i, 0))
    weight = pl.BlockSpec((f, d), lambda i: (0, 0), pipeline_mode=pl.Buffered(1))
    blk = pl.BlockSpec((tm, f), lambda i: (i, 0))
    wide = jax.ShapeDtypeStruct((s, f), BF16)
    return _call(body, "ffn_fwd_loss", (ni,),
                 [row, pl.BlockSpec((1, d), lambda i: (0, 0)), weight, weight, weight, row],
                 [row, blk, blk, row, blk, pl.BlockSpec((1, 128), lambda i: (0, 0))],
                 [jax.ShapeDtypeStruct((s, d), F32), wide, wide, jax.ShapeDtypeStruct((s, d), BF16), wide,
                  jax.ShapeDtypeStruct((1, 128), F32)], [x, nw, wg, wu, wd, tgt])


def _ffn_up(x, nw, wg, wu, rider=None):
    s, d = x.shape
    f = wg.shape[0]
    tm = min(FFN_TILE, s)
    ni = s // tm
    assert s % tm == 0

    def body(*refs):
        (x_ref, nw_ref, wg_ref, wu_ref), (g_ref, u_ref, hb_ref, ab_ref), _, copies = _split_refs(refs, 4, 4, rider)
        i = pl.program_id(0)
        finish = _ride(copies, i == 0, i == ni - 1)
        xv = x_ref[...]
        hb = (xv * _rms_r(xv) * nw_ref[...]).astype(BF16)
        hb_ref[...] = hb
        for lo, hi in _ffn_chunks(f):
            g = _dot_nt(hb, wg_ref[lo:hi, :])
            u = _dot_nt(hb, wu_ref[lo:hi, :])
            g_ref[:, lo:hi] = g.astype(BF16)
            u_ref[:, lo:hi] = u.astype(BF16)
            ab_ref[:, lo:hi] = (g * _sigmoid(g) * u).astype(BF16)
        finish()

    row = pl.BlockSpec((tm, d), lambda i: (i, 0))
    weight = pl.BlockSpec((f, d), lambda i: (0, 0), pipeline_mode=pl.Buffered(1))
    blk = pl.BlockSpec((tm, f), lambda i: (i, 0))
    wide = jax.ShapeDtypeStruct((s, f), BF16)
    return _call(body, "ffn_up", (ni,), [row, pl.BlockSpec((1, d), lambda i: (0, 0)), weight, weight],
                 [blk, blk, row, blk], [wide, wide, jax.ShapeDtypeStruct((s, d), BF16), wide], [x, nw, wg, wu],
                 rider=rider)


def _ffn_down(x, ab, wd):
    s, d = x.shape
    f = wd.shape[0]
    tm = _tile(s)
    assert s % tm == 0

    def body(x_ref, ab_ref, wd_ref, out_ref):
        out_ref[...] = x_ref[...] + 0.5 * _dot(ab_ref[...], wd_ref[...])

    row = pl.BlockSpec((tm, d), lambda i: (i, 0))
    return _call(body, "ffn_down", (s // tm,),
                 [row, pl.BlockSpec((tm, f), lambda i: (i, 0)),
                  pl.BlockSpec((f, d), lambda i: (0, 0), pipeline_mode=pl.Buffered(1))],
                 [row], [jax.ShapeDtypeStruct((s, d), F32)], [x, ab, wd])[0]


def _call(body, name, grid, in_specs, out_specs, out_shape, args, scratch=(), rider=None):
    in_specs, out_specs, out_shape, scratch = list(in_specs), list(out_specs), list(out_shape), list(scratch)
    extra, aliases = [], {}
    if rider is not None:
        extra = rider.operands()
        aliases = rider.aliases(len(args), len(out_shape))
        in_specs += [ANY] * len(extra)
        out_specs += [ANY] * len(rider.inplace)
        out_shape += rider.out_shape()
        scratch += rider.scratch()
    return pl.pallas_call(
        body, name=name, grid=grid, in_specs=in_specs, out_specs=out_specs, out_shape=out_shape,
        input_output_aliases=aliases, scratch_shapes=scratch,
        compiler_params=_params(("arbitrary",) * len(grid)),
    )(*args, *extra)


def _ffn_bwd_act(x, nw, dy, g, u, wg, wu, wd, name):
    s, d = x.shape
    f = wg.shape[0]
    tm = min(FFN_TILE, s)
    assert s % tm == 0

    def body(x_ref, nw_ref, dy_ref, g_ref, u_ref, wg_ref, wu_ref, wd_ref,
             dx_ref, dg_ref, du_ref, dyb_ref, dnw_ref):
        dyv = dy_ref[...]
        dyb = dyv.astype(BF16)
        dyb_ref[...] = dyb
        dh = jnp.zeros((tm, d), F32)
        for lo, hi in _ffn_chunks(f):
            da = 0.5 * _dot_nt(dyb, wd_ref[lo:hi, :])
            gv = g_ref[:, lo:hi].astype(F32)
            sg = _sigmoid(gv)
            dub = (da * (gv * sg)).astype(BF16)
            dgb = (da * u_ref[:, lo:hi].astype(F32) * (sg * (1.0 + gv * (1.0 - sg)))).astype(BF16)
            dg_ref[:, lo:hi] = dgb
            du_ref[:, lo:hi] = dub
            dh = dh + _dot(dgb, wg_ref[lo:hi, :]) + _dot(dub, wu_ref[lo:hi, :])
        xv = x_ref[...]
        dx, dn = _rms_bwd(xv, _rms_r(xv), nw_ref[...], dh)
        dx_ref[...] = dyv + dx

        @pl.when(pl.program_id(0) == 0)
        def _():
            dnw_ref[...] = jnp.zeros_like(dnw_ref)

        dnw_ref[...] += dn

    row = pl.BlockSpec((tm, d), lambda i: (i, 0))
    vec = pl.BlockSpec((1, d), lambda i: (0, 0))
    blk = pl.BlockSpec((tm, f), lambda i: (i, 0))
    weight = pl.BlockSpec((f, d), lambda i: (0, 0), pipeline_mode=pl.Buffered(1))
    return _call(
        body, name, (s // tm,), [row, vec, row, blk, blk, weight, weight, weight], [row, blk, blk, row, vec],
        [jax.ShapeDtypeStruct((s, d), F32), jax.ShapeDtypeStruct((s, f), BF16),
         jax.ShapeDtypeStruct((s, f), BF16), jax.ShapeDtypeStruct((s, d), BF16),
         jax.ShapeDtypeStruct((1, d), F32)],
        [x, nw, dy, g, u, wg, wu, wd])


def _wgrad(a, b, a_spec, b_spec, out_rows, out_cols, scale, name, tk, rider=None, per_step=1):
    s = a.shape[-2]
    nk = s // tk
    steps = N_CHIPS // per_step
    assert s % tk == 0

    def body(*refs):
        (a_ref, b_ref), (out_ref,), (acc,), copies = _split_refs(refs, 2, 1, rider)
        j, k = pl.program_id(0), pl.program_id(1)
        finish = _ride(copies, jnp.logical_and(j == 0, k == 0), jnp.logical_and(j == steps - 1, k == nk - 1))

        @pl.when(k == 0)
        def _():
            acc[...] = jnp.zeros_like(acc)

        acc[...] += _dot_tn(a_ref[...], b_ref[...])

        @pl.when(k == nk - 1)
        def _():
            for t in range(per_step):
                out_ref[t] = (acc[t * out_rows:(t + 1) * out_rows, :] * scale).astype(BF16)

        finish()

    outs = _call(
        body, name, (steps, nk), [a_spec(tk), b_spec(tk)],
        [pl.BlockSpec((per_step, out_rows, out_cols), lambda j, k: (j, 0, 0))],
        [jax.ShapeDtypeStruct((N_CHIPS, out_rows, out_cols), BF16)], [a, b],
        scratch=[pltpu.VMEM((per_step * out_rows, out_cols), F32)], rider=rider)
    return outs[0] if rider is None else outs


def _beside(refs):
    return jnp.concatenate([r[...] for r in refs], axis=1) if len(refs) > 1 else refs[0][...]


def _wgrad_whole(a, bs, col_blocks, name, rider=None):
    s, m = a.shape
    n = sum(b.shape[1] for b in bs)
    tk = min(WHOLE_TILE, s)
    nk = s // tk
    assert s % tk == 0
    out_shape = (N_CHIPS, m, n // N_CHIPS) if col_blocks else (N_CHIPS, m // N_CHIPS, n)

    def body(*refs):
        (a_ref, *b_refs), (out_ref,), (acc,), copies = _split_refs(refs, 1 + len(bs), 1, rider)
        k = pl.program_id(0)
        finish = _ride(copies, k == 0, k == nk - 1)

        @pl.when(k == 0)
        def _():
            acc[...] = jnp.zeros_like(acc)

        acc[...] += _dot_tn(a_ref[...], _beside(b_refs))

        @pl.when(k == nk - 1)
        def _():
            for j in range(N_CHIPS):
                if col_blocks:
                    out_ref[j] = acc[:, j * out_shape[2]:(j + 1) * out_shape[2]].astype(BF16)
                else:
                    out_ref[j] = acc[j * out_shape[1]:(j + 1) * out_shape[1], :].astype(BF16)

        finish()

    outs = _call(
        body, name, (nk,),
        [pl.BlockSpec((tk, m), lambda k: (k, 0))] + [pl.BlockSpec((tk, b.shape[1]), lambda k: (k, 0)) for b in bs],
        [pl.BlockSpec(out_shape, lambda k: (0, 0, 0))], [jax.ShapeDtypeStruct(out_shape, BF16)], [a, *bs],
        scratch=[pltpu.VMEM((m, n), F32)], rider=rider)
    return outs[0] if rider is None else outs


def _ffn_wgrad(hidden, shared, scale, name, rider=None):
    s, d = shared.shape
    half = hidden.shape[1] // 2
    return _wgrad(hidden, shared, lambda tk: pl.BlockSpec((tk, half), lambda j, k: (k, j)),
                  lambda tk: pl.BlockSpec((tk, d), lambda j, k: (k, 0)), half // 2, d, scale, name,
                  min(WGRAD_TILE, s), rider, per_step=2)


def _mix_pre(x, nw, win, rider=None):
    s, d = x.shape
    nb, _, cb = win.shape
    tm = _tile(s)
    ni = s // tm
    assert s % tm == 0

    def body(*refs):
        (x_ref, nw_ref, w_ref), (p_ref, hb_ref), _, copies = _split_refs(refs, 3, 2, rider)
        finish = _ride(copies, pl.program_id(0) == 0, pl.program_id(0) == ni - 1)
        xv = x_ref[...]
        hb = (xv * _rms_r(xv) * nw_ref[...]).astype(BF16)
        hb_ref[...] = hb
        for j in range(nb):
            p_ref[:, j * cb:(j + 1) * cb] = _dot(hb, w_ref[j])
        finish()

    row = pl.BlockSpec((tm, d), lambda i: (i, 0))
    return _call(
        body, "mix_pre", (ni,),
        [row, pl.BlockSpec((1, d), lambda i: (0, 0)),
         pl.BlockSpec((nb, d, cb), lambda i: (0, 0, 0), pipeline_mode=pl.Buffered(1))],
        [pl.BlockSpec((tm, nb * cb), lambda i: (i, 0)), row],
        [jax.ShapeDtypeStruct((s, nb * cb), F32), jax.ShapeDtypeStruct((s, d), BF16)], [x, nw, win], rider=rider)


def _mix_pre_bwd(x, nw, dres, dps, win):
    s, d = x.shape
    nb, _, cb = win.shape
    tm = _tile(s)
    assert s % tm == 0 and sum(p.shape[1] for p in dps) == nb * cb

    def body(x_ref, nw_ref, dres_ref, *rest):
        *dp_refs, w_ref, dx_ref, dnw_ref, w_all = rest

        @pl.when(pl.program_id(0) == 0)
        def _():
            dnw_ref[...] = jnp.zeros_like(dnw_ref)
            for j in range(nb):
                w_all[:, j * cb:(j + 1) * cb] = w_ref[j]

        dh = _dot_nt(_beside(dp_refs), w_all[...])
        xv = x_ref[...]
        dx, dn = _rms_bwd(xv, _rms_r(xv), nw_ref[...], dh)
        dx_ref[...] = dres_ref[...] + dx
        dnw_ref[...] += dn

    row = pl.BlockSpec((tm, d), lambda i: (i, 0))
    vec = pl.BlockSpec((1, d), lambda i: (0, 0))
    return pl.pallas_call(
        body, name="mix_pre_bwd", grid=(s // tm,),
        in_specs=[row, vec, row] + [pl.BlockSpec((tm, p.shape[1]), lambda i: (i, 0)) for p in dps]
        + [pl.BlockSpec((nb, d, cb), lambda i: (0, 0, 0), pipeline_mode=pl.Buffered(1))],
        out_specs=[row, vec],
        out_shape=[jax.ShapeDtypeStruct((s, d), F32), jax.ShapeDtypeStruct((1, d), F32)],
        scratch_shapes=[pltpu.VMEM((d, nb * cb), BF16)],
        compiler_params=_params(("arbitrary",)),
    )(x, nw, dres, *dps, win)


def _mix_post(x, yr, ya, nr, na, wout):
    s, d = x.shape
    h = yr.shape[1]
    tm = _tile(s)

    def body(x_ref, yr_ref, ya_ref, nr_ref, na_ref, w_ref, out_ref):
        yrv = yr_ref[...]
        yav = ya_ref[...]
        onb = (yrv * _rms_r(yrv) * nr_ref[...]).astype(BF16)
        oab = (yav * _rms_r(yav) * na_ref[...]).astype(BF16)
        out_ref[...] = x_ref[...] + _dot(onb, w_ref[0:h, :]) + _dot(oab, w_ref[h:2 * h, :])

    row = pl.BlockSpec((tm, d), lambda i: (i, 0))
    half = pl.BlockSpec((tm, h), lambda i: (i, 0))
    vec = pl.BlockSpec((1, h), lambda i: (0, 0))
    return pl.pallas_call(
        body, name="mix_post", grid=(s // tm,),
        in_specs=[row, half, half, vec, vec, pl.BlockSpec((2 * h, d), lambda i: (0, 0))],
        out_specs=row, out_shape=jax.ShapeDtypeStruct((s, d), F32),
        compiler_params=_params(("arbitrary",)),
    )(x, yr, ya, nr, na, wout)


def _mix_post_bwd(dx, yr, ya, nr, na, wout):
    s, d = dx.shape
    h = yr.shape[1]
    tm = _tile(s)

    def body(dx_ref, yr_ref, ya_ref, nr_ref, na_ref, w_ref,
             dyr_ref, dya_ref, yc_ref, dxb_ref, dnr_ref, dna_ref):
        i = pl.program_id(0)
        dxb = dx_ref[...].astype(BF16)
        dxb_ref[...] = dxb
        dyc = _dot_nt(dxb, w_ref[...])
        yrv = yr_ref[...]
        yav = ya_ref[...]
        rr = _rms_r(yrv)
        ra = _rms_r(yav)
        yc_ref[:, 0:h] = (yrv * rr * nr_ref[...]).astype(BF16)
        yc_ref[:, h:2 * h] = (yav * ra * na_ref[...]).astype(BF16)
        dyr, dnr = _rms_bwd(yrv, rr, nr_ref[...], dyc[:, 0:h])
        dya, dna = _rms_bwd(yav, ra, na_ref[...], dyc[:, h:2 * h])
        dyr_ref[...] = dyr
        dya_ref[...] = dya

        @pl.when(i == 0)
        def _():
            dnr_ref[...] = jnp.zeros_like(dnr_ref)
            dna_ref[...] = jnp.zeros_like(dna_ref)

        dnr_ref[...] += dnr
        dna_ref[...] += dna

    row = pl.BlockSpec((tm, d), lambda i: (i, 0))
    half = pl.BlockSpec((tm, h), lambda i: (i, 0))
    vec = pl.BlockSpec((1, h), lambda i: (0, 0))
    return pl.pallas_call(
        body, name="mix_post_bwd", grid=(s // tm,),
        in_specs=[row, half, half, vec, vec, pl.BlockSpec((2 * h, d), lambda i: (0, 0))],
        out_specs=[half, half, pl.BlockSpec((tm, 2 * h), lambda i: (i, 0)), row, vec, vec],
        out_shape=[jax.ShapeDtypeStruct((s, h), F32), jax.ShapeDtypeStruct((s, h), F32),
                   jax.ShapeDtypeStruct((s, 2 * h), BF16), jax.ShapeDtypeStruct((s, d), BF16),
                   jax.ShapeDtypeStruct((1, h), F32), jax.ShapeDtypeStruct((1, h), F32)],
        compiler_params=_params(("arbitrary",)),
    )(dx, yr, ya, nr, na, wout)


def _shift_down(xv, s, prev8):
    rolled = pltpu.roll(xv, s, 0)
    row8 = lax.broadcasted_iota(jnp.int32, prev8.shape, 0)
    head = jnp.where(row8 < s, pltpu.roll(prev8, s, 0), rolled[0:8, :])
    return jnp.concatenate([head, rolled[8:, :]], axis=0)


def _shift_up(xv, s, next8):
    n = xv.shape[0]
    rolled = pltpu.roll(xv, n - s, 0)
    row8 = lax.broadcasted_iota(jnp.int32, next8.shape, 0)
    tail = jnp.where(row8 >= 8 - s, pltpu.roll(next8, 8 - s, 0), rolled[n - 8:, :])
    return jnp.concatenate([rolled[:n - 8, :], tail], axis=0)


def _scan_fwd(a, b):
    n = a.shape[0]
    sub = lax.broadcasted_iota(jnp.int32, a.shape, 0) % SUBLANES
    s = 1
    while s < SUBLANES:
        ok = sub >= s
        b = jnp.where(ok, a * pltpu.roll(b, s, 0) + b, b)
        a = jnp.where(ok, a * pltpu.roll(a, s, 0), a)
        s *= 2
    groups = []
    before = jnp.zeros((1, a.shape[1]), F32)
    for g in range(n // SUBLANES):
        rows = slice(g * SUBLANES, (g + 1) * SUBLANES)
        groups.append(a[rows] * before + b[rows])
        before = groups[-1][SUBLANES - 1:]
    return jnp.concatenate(groups, axis=0)


def _scan_bwd(a, b):
    n = a.shape[0]
    sub = lax.broadcasted_iota(jnp.int32, a.shape, 0) % SUBLANES
    s = 1
    while s < SUBLANES:
        ok = sub < SUBLANES - s
        b = jnp.where(ok, a * pltpu.roll(b, n - s, 0) + b, b)
        a = jnp.where(ok, a * pltpu.roll(a, n - s, 0), a)
        s *= 2
    groups = []
    after = jnp.zeros((1, a.shape[1]), F32)
    for g in reversed(range(n // SUBLANES)):
        rows = slice(g * SUBLANES, (g + 1) * SUBLANES)
        groups.append(a[rows] * after + b[rows])
        after = groups[-1][:1]
    return jnp.concatenate(groups[::-1], axis=0)


def _rglru_gates(xv, prev8, cw_ref, cb_ref, wa_ref, ba_ref, wx_ref, bx_ref, lam_ref):
    x1 = _shift_down(xv, 1, prev8)
    x2 = _shift_down(xv, 2, prev8)
    x3 = _shift_down(xv, 3, prev8)
    xc = cw_ref[3:4, :] * xv + cw_ref[2:3, :] * x1 + cw_ref[1:2, :] * x2 + cw_ref[0:1, :] * x3 + cb_ref[...]
    xcb = xc.astype(BF16)
    r = _sigmoid(_dot(xcb, wa_ref[...]) + ba_ref[...])
    ig = _sigmoid(_dot(xcb, wx_ref[...]) + bx_ref[...])
    c = RG_C * _log_sigmoid(lam_ref[...])
    la = r * c
    a = jnp.exp(la)
    m = jnp.sqrt(-_expm1_neg(2.0 * la))
    return (x1, x2, x3), xc, xcb, r, ig, c, a, m


def _rglru_fwd(proj, cw, cb, wa, ba, wx, bx, lam, rider=None):
    s = proj.shape[0]
    w = D_RNN
    tm = _tile(s)
    ni = s // tm

    def body(*refs):
        ins, (y_ref, h_ref), (prev, hlast), copies = _split_refs(refs, 9, 2, rider)
        xr_ref, gate_ref, cw_ref, cb_ref, wa_ref, ba_ref, wx_ref, bx_ref, lam_ref = ins
        finish = _ride(copies, pl.program_id(0) == 0, pl.program_id(0) == ni - 1)

        @pl.when(pl.program_id(0) == 0)
        def _():
            prev[...] = jnp.zeros_like(prev)
            hlast[...] = jnp.zeros_like(hlast)

        xv = xr_ref[...]
        _, xc, _, _, ig, _, a, m = _rglru_gates(xv, prev[...], cw_ref, cb_ref, wa_ref, ba_ref,
                                                wx_ref, bx_ref, lam_ref)
        b = m * (ig * xc)
        row = lax.broadcasted_iota(jnp.int32, b.shape, 0)
        b = jnp.where(row == 0, b + a * hlast[...], b)
        h = _scan_fwd(a, b)
        h_ref[...] = h
        y_ref[...] = h * _gelu(gate_ref[...])
        prev[...] = xv[tm - 8:, :]
        hlast[...] = h[tm - 1:tm, :]
        finish()

    vec = pl.BlockSpec((1, w), lambda i: (0, 0))
    sq = pl.BlockSpec((w, w), lambda i: (0, 0))
    out = pl.BlockSpec((tm, w), lambda i: (i, 0))
    return _call(
        body, "rglru_fwd", (ni,),
        [pl.BlockSpec((tm, w), lambda i: (i, 0)), pl.BlockSpec((tm, w), lambda i: (i, 1)),
         pl.BlockSpec((CONV_W, w), lambda i: (0, 0)), vec, sq, vec, sq, vec, vec], [out, out],
        [jax.ShapeDtypeStruct((s, w), F32), jax.ShapeDtypeStruct((s, w), F32)],
        [proj, proj, cw, cb, wa, ba, wx, bx, lam],
        scratch=[pltpu.VMEM((8, w), F32), pltpu.VMEM((1, w), F32)], rider=rider)


def _rglru_bwd(proj, hseq, dyr, cw, cb, wa, ba, wx, bx, lam):
    s = proj.shape[0]
    w = D_RNN
    tm = _tile(s)
    nt = s // tm
    t8 = tm // 8

    def body(xr_ref, xp_ref, gate_ref, h_ref, hp_ref, dy_ref, cw_ref, cb_ref, wa_ref, ba_ref,
             wx_ref, bx_ref, lam_ref,
             dxr_ref, dgate_ref, dcw_ref, dcb_ref, dwa_ref, dba_ref, dwx_ref, dbx_ref, dlam_ref,
             carry, dxc_next):
        i = pl.program_id(0)
        first_tile = i == nt - 1

        @pl.when(i == 0)
        def _():
            carry[...] = jnp.zeros_like(carry)
            dxc_next[...] = jnp.zeros_like(dxc_next)
            for ref in (dcw_ref, dcb_ref, dwa_ref, dba_ref, dwx_ref, dbx_ref, dlam_ref):
                ref[...] = jnp.zeros_like(ref)

        xv = xr_ref[...]
        prev8 = jnp.where(first_tile, 0.0, xp_ref[...])
        hprev8 = jnp.where(first_tile, 0.0, hp_ref[...])
        (x1, x2, x3), xc, xcb, r, ig, c, a, m = _rglru_gates(
            xv, prev8, cw_ref, cb_ref, wa_ref, ba_ref, wx_ref, bx_ref, lam_ref)
        gv = gate_ref[...]
        hv = h_ref[...]
        dy = dy_ref[...]
        dgate_ref[...] = (dy * hv * _gelu_grad(gv)).astype(BF16)
        dh = dy * _gelu(gv)
        row = lax.broadcasted_iota(jnp.int32, dh.shape, 0)
        dh = jnp.where(row == tm - 1, dh + carry[...], dh)
        a_up = jnp.where(row == tm - 1, 0.0, pltpu.roll(a, tm - 1, 0))
        lam_t = _scan_bwd(a_up, dh)
        carry[...] = a[0:1, :] * lam_t[0:1, :]
        hm1 = _shift_down(hv, 1, hprev8)
        da = lam_t * hm1
        ixc = ig * xc
        dm = lam_t * ixc
        dig = lam_t * m * xc
        dxc = lam_t * m * ig
        dla = da * a - dm * (a * a) / m
        dr = dla * c
        dlam_ref[...] += jnp.sum(dla * r, axis=0, keepdims=True)
        dpa = dr * r * (1.0 - r)
        dpi = dig * ig * (1.0 - ig)
        dba_ref[...] += jnp.sum(dpa, axis=0, keepdims=True)
        dbx_ref[...] += jnp.sum(dpi, axis=0, keepdims=True)
        dpab = dpa.astype(BF16)
        dpib = dpi.astype(BF16)
        dwa_ref[...] += _dot_tn(xcb, dpab)
        dwx_ref[...] += _dot_tn(xcb, dpib)
        dxc = dxc + _dot_nt(dpab, wa_ref[...]) + _dot_nt(dpib, wx_ref[...])
        dcb_ref[...] += jnp.sum(dxc, axis=0, keepdims=True)
        dcw_ref[3:4, :] += jnp.sum(dxc * xv, axis=0, keepdims=True)
        dcw_ref[2:3, :] += jnp.sum(dxc * x1, axis=0, keepdims=True)
        dcw_ref[1:2, :] += jnp.sum(dxc * x2, axis=0, keepdims=True)
        dcw_ref[0:1, :] += jnp.sum(dxc * x3, axis=0, keepdims=True)
        nxt = dxc_next[...]
        dxr = (cw_ref[3:4, :] * dxc + cw_ref[2:3, :] * _shift_up(dxc, 1, nxt)
               + cw_ref[1:2, :] * _shift_up(dxc, 2, nxt) + cw_ref[0:1, :] * _shift_up(dxc, 3, nxt))
        dxr_ref[...] = dxr.astype(BF16)
        dxc_next[...] = dxc[0:8, :]

        @pl.when(first_tile)
        def _():
            lv = lam_ref[...]
            dlam_ref[...] = dlam_ref[...] * (RG_C * _sigmoid(-lv))

    rev = lambda i: nt - 1 - i
    vec = pl.BlockSpec((1, w), lambda i: (0, 0))
    sq = pl.BlockSpec((w, w), lambda i: (0, 0))
    cur = lambda col: pl.BlockSpec((tm, w), lambda i: (rev(i), col))
    before = lambda cols: pl.BlockSpec((8, w), lambda i: (jnp.maximum(rev(i) * t8 - 1, 0), 0))
    return pl.pallas_call(
        body, name="rglru_bwd", grid=(nt,),
        in_specs=[cur(0), before(None), cur(1), cur(0), before(None), cur(0),
                  pl.BlockSpec((CONV_W, w), lambda i: (0, 0)), vec, sq, vec, sq, vec, vec],
        out_specs=[cur(0), cur(0), pl.BlockSpec((CONV_W, w), lambda i: (0, 0)), vec, sq, vec, sq, vec, vec],
        out_shape=[jax.ShapeDtypeStruct((s, w), BF16), jax.ShapeDtypeStruct((s, w), BF16),
                   jax.ShapeDtypeStruct((CONV_W, w), F32), jax.ShapeDtypeStruct((1, w), F32),
                   jax.ShapeDtypeStruct((w, w), F32), jax.ShapeDtypeStruct((1, w), F32),
                   jax.ShapeDtypeStruct((w, w), F32), jax.ShapeDtypeStruct((1, w), F32),
                   jax.ShapeDtypeStruct((1, w), F32)],
        scratch_shapes=[pltpu.VMEM((1, w), F32), pltpu.VMEM((8, w), F32)],
        compiler_params=_params(("arbitrary",)),
    )(proj, proj, proj, hseq, hseq, dyr, cw, cb, wa, ba, wx, bx, lam)


def _sb_logs(z, valid):
    lb = jnp.minimum(z, 0.0) - jnp.log(1.0 + jnp.exp(-jnp.abs(z)))
    return lb, jnp.where(valid, lb - z, 0.0)


class _Window:
    def __init__(self):
        blk, win, cut = ATT_BLOCK, ATT_WINDOW, ATT_SPLIT
        self.row = lax.broadcasted_iota(jnp.int32, (blk, win), 0)
        self.col = lax.broadcasted_iota(jnp.int32, (blk, win), 1)

        def tri(n, later):
            j = lax.broadcasted_iota(jnp.int32, (n, n), 0)
            s = lax.broadcasted_iota(jnp.int32, (n, n), 1)
            return jnp.where((j > s) if later else (j < s), 1.0, 0.0).astype(BF16)

        self.later = (tri(cut, True), tri(win - cut, True))
        self.earlier = (tri(cut, False), tri(win - cut, False))

    def place(self, qi, g):
        end = (qi + 1) * ATT_BLOCK - g * ATT_WINDOW
        start = pl.multiple_of(jnp.maximum(end - ATT_WINDOW, 0), ATT_BLOCK)
        valid = self.col < jnp.minimum(self.row + (qi * ATT_BLOCK - start), end - start)
        return start, valid

    @staticmethod
    def _parts(xv):
        hi = xv.astype(BF16)
        lo = (xv - hi.astype(F32)).astype(BF16)
        cut = ATT_SPLIT
        sums = (jnp.sum(xv[:, :cut], axis=1, keepdims=True), jnp.sum(xv[:, cut:], axis=1, keepdims=True))
        return (hi[:, :cut], lo[:, :cut]), (hi[:, cut:], lo[:, cut:]), sums

    def sums_after(self, xv, carry):
        (h0, l0), (h1, l1), (s0, s1) = self._parts(xv)
        first = _dot(h0, self.later[0]) + _dot(l0, self.later[0]) + (s1 + carry)
        last = _dot(h1, self.later[1]) + _dot(l1, self.later[1]) + carry
        return jnp.concatenate([first, last], axis=1), s0 + s1

    def sums_before(self, xv, carry):
        (h0, l0), (h1, l1), (s0, s1) = self._parts(xv)
        first = _dot(h0, self.earlier[0]) + _dot(l0, self.earlier[0]) + carry
        last = _dot(h1, self.earlier[1]) + _dot(l1, self.earlier[1]) + (s0 + carry)
        return jnp.concatenate([first, last], axis=1), s0 + s1


class _HeadPair:
    def __init__(self):
        lanes = 2 * HEAD_DIM
        lane = lax.broadcasted_iota(jnp.int32, (1, lanes), 1)
        self.masks = [lane // HEAD_DIM == h for h in (0, 1)]
        i = lax.broadcasted_iota(jnp.int32, (lanes, lanes), 0) // HEAD_DIM
        j = lax.broadcasted_iota(jnp.int32, (lanes, lanes), 1) // HEAD_DIM
        self.same_head = jnp.where(i == j, 1.0, 0.0).astype(BF16)

    def only(self, h, xv):
        return jnp.where(self.masks[h], xv, jnp.zeros_like(xv))

    def merge(self, per_head):
        return jnp.where(self.masks[0], per_head[0], per_head[1])

    def mean(self, xv):
        hi = xv.astype(BF16)
        lo = (xv - hi.astype(F32)).astype(BF16)
        return (_dot(hi, self.same_head) + _dot(lo, self.same_head)) * (1.0 / HEAD_DIM)

    def rms_r(self, xv):
        return lax.rsqrt(self.mean(xv * xv) + EPS)

    def rms_bwd(self, xv, r, nw, dh):
        t = dh * nw
        dx = r * t - xv * (r * r * r * self.mean(t * xv))
        dn = jnp.sum(dh * xv * r, axis=0, keepdims=True)
        return dx, dn[:, :HEAD_DIM] + dn[:, HEAD_DIM:]


def _attn_fwd(proj, qg, kg, rider=None):
    s = proj.shape[0]
    blk, win, dh = ATT_BLOCK, ATT_WINDOW, HEAD_DIM
    nq = s // blk
    scale = 1.0 / math.sqrt(dh)
    heads = (0, 1)
    blocks = (0, 1)
    assert s >= win and s % (blk * len(blocks)) == 0

    def body(*refs):
        (q_ref, k_ref, v_ref, qg_ref, kg_ref), (o_ref,), (qn, kn, vb), copies = _split_refs(refs, 5, 1, rider)
        finish = _ride(copies, pl.program_id(0) == 0, pl.program_id(0) == N_HEADS // 2 - 1)
        wd, hp = _Window(), _HeadPair()
        qv = q_ref[...]
        qn[...] = (qv * hp.rms_r(qv) * qg_ref[...] * scale).astype(BF16)
        kv = k_ref[...]
        kn[...] = (kv * hp.rms_r(kv) * kg_ref[...]).astype(BF16)
        vb[...] = v_ref[...].astype(BF16)

        def q_step(pair_i, _):
            qis = [2 * pair_i + b for b in blocks]
            chains = [(b, h) for b in blocks for h in heads]
            qoffs = [pl.multiple_of(qi * blk, blk) for qi in qis]
            qtiles = [qn[pl.ds(qoff, blk), :] for qoff in qoffs]
            qts = [hp.only(h, qtiles[b]) for b, h in chains]

            def more(carry):
                g, live = carry[:2]
                return jnp.logical_and((qis[-1] + 1) * blk - g * win > 0, live > 0)

            def window(carry):
                g, _, accs, runs = carry
                places = [wd.place(qi, g) for qi in qis]
                kts = [kn[pl.ds(start, win), :] for start, _ in places]
                zs = [_dot_nt(qts[c], kts[b]) for c, (b, h) in enumerate(chains)]
                logs = [_sb_logs(zs[c], places[b][1]) for c, (b, h) in enumerate(chains)]
                sums = [wd.sums_after(logs[c][1], runs[c]) for c in range(len(chains))]
                wgts = [jnp.where(places[b][1], jnp.exp(logs[c][0] + sums[c][0]), 0.0).astype(BF16)
                        for c, (b, h) in enumerate(chains)]
                vts = [vb[pl.ds(start, win), :] for start, _ in places]
                accs = tuple(accs[c] + _dot(wgts[c], vts[b]) for c, (b, h) in enumerate(chains))
                runs = tuple(runs[c] + sums[c][1] for c in range(len(chains)))
                top = functools.reduce(jnp.maximum, [jnp.max(r) for r in runs])
                return g + 1, (top > EXP_ZERO).astype(jnp.int32), accs, runs

            zero = lambda cols: tuple(jnp.zeros((blk, cols), F32) for _ in chains)
            _, _, accs, _ = lax.while_loop(more, window, (jnp.int32(0), jnp.int32(1), zero(2 * dh), zero(1)))
            for b in blocks:
                o_ref[pl.ds(qoffs[b], blk), :] = hp.merge([accs[2 * b + h] for h in heads])
            return 0

        lax.fori_loop(0, nq // len(blocks), q_step, 0)
        finish()

    pair = lambda group: pl.BlockSpec((s, 2 * dh), lambda p: (0, group * (D_ATT // (2 * dh)) + p))
    vec = pl.BlockSpec((1, 2 * dh), lambda p: (0, 0))
    return _call(
        body, "attn_fwd", (N_HEADS // 2,), [pair(2), pair(3), pair(4), vec, vec], [pair(0)],
        [jax.ShapeDtypeStruct((s, D_ATT), F32)], [proj, proj, proj, jnp.tile(qg, (1, 2)), jnp.tile(kg, (1, 2))],
        scratch=[pltpu.VMEM((s, 2 * dh), BF16)] * 3, rider=rider)


def _attn_bwd(proj, dya, qg, kg, rider=None):
    s = proj.shape[0]
    blk, win, dh = ATT_BLOCK, ATT_WINDOW, HEAD_DIM
    nq = s // blk
    max_windows = -(-s // win) + 1
    scale = 1.0 / math.sqrt(dh)
    steps = N_HEADS // 2
    heads = (0, 1)
    blocks = (0, 1)
    assert s >= win and s % (blk * len(blocks)) == 0

    def body(*refs):
        ins, outs, scratch, copies = _split_refs(refs, 6, 5, rider)
        q_ref, k_ref, v_ref, do_ref, qg_ref, kg_ref = ins
        dq_ref, dk_ref, dv_ref, dqg_ref, dkg_ref = outs
        qn, kn, vb, dob, runs_ref, dqn, dkn, dvn = scratch
        finish = _ride(copies, pl.program_id(0) == 0, pl.program_id(0) == steps - 1)
        wd, hp = _Window(), _HeadPair()

        @pl.when(pl.program_id(0) == 0)
        def _():
            dqg_ref[...] = jnp.zeros_like(dqg_ref)
            dkg_ref[...] = jnp.zeros_like(dkg_ref)

        qv = q_ref[...]
        qn[...] = (qv * hp.rms_r(qv) * qg_ref[...] * scale).astype(BF16)
        kv = k_ref[...]
        kn[...] = (kv * hp.rms_r(kv) * kg_ref[...]).astype(BF16)
        vb[...] = v_ref[...].astype(BF16)
        dob[...] = do_ref[...].astype(BF16)
        dkn[...] = jnp.zeros_like(dkn)
        dvn[...] = jnp.zeros_like(dvn)

        def q_step(pair_i, _):
            qis = [2 * pair_i + b for b in blocks]
            chains = [(b, h) for b in blocks for h in heads]
            ids = range(len(chains))
            qoffs = [pl.multiple_of(qi * blk, blk) for qi in qis]
            qts = [hp.only(h, qn[pl.ds(qoffs[b], blk), :]) for b, h in chains]
            dots = [hp.only(h, dob[pl.ds(qoffs[b], blk), :]) for b, h in chains]

            zero = lambda cols: tuple(jnp.zeros((blk, cols), F32) for _ in chains)

            def logs_of(g):
                places = [wd.place(qi, g) for qi in qis]
                kts = [kn[pl.ds(start, win), :] for start, _ in places]
                return [_sb_logs(_dot_nt(qts[c], kts[b]), places[b][1]) for c, (b, h) in enumerate(chains)]

            def row_sums(logs):
                return tuple(jnp.sum(logs[c][1], axis=1, keepdims=True) for c in ids)

            def still_live(runs):
                return functools.reduce(jnp.maximum, [jnp.max(r) for r in runs]) > EXP_ZERO

            def window_grads(g, logs, runs, esums):
                places = [wd.place(qi, g) for qi in qis]
                kts = [kn[pl.ds(start, win), :] for start, _ in places]
                vts = [vb[pl.ds(start, win), :] for start, _ in places]
                dws = [_dot_nt(dots[c], vts[b]) for c, (b, h) in enumerate(chains)]
                tails = [wd.sums_after(logs[c][1], runs[c])[0] for c in ids]
                wgts = [jnp.where(places[b][1], jnp.exp(logs[c][0] + tails[c]), 0.0) for c, (b, h) in enumerate(chains)]
                es = [dws[c] * wgts[c] for c in ids]
                befores = [wd.sums_before(es[c], esums[c]) for c in ids]
                dzbs = []
                for c, (b, h) in enumerate(chains):
                    beta = jnp.exp(logs[c][0])
                    dz = jnp.where(places[b][1], es[c] * (1.0 - beta) - befores[c][0] * beta, 0.0)
                    dzbs.append(dz.astype(BF16))
                for b in blocks:
                    rows = pl.ds(places[b][0], win)
                    dkn[rows, :] += _dot_tn(dzbs[2 * b], qts[2 * b]) + _dot_tn(dzbs[2 * b + 1], qts[2 * b + 1])
                    dvn[rows, :] += (_dot_tn(wgts[2 * b].astype(BF16), dots[2 * b])
                                     + _dot_tn(wgts[2 * b + 1].astype(BF16), dots[2 * b + 1]))
                return (tuple(_dot(dzbs[c], kts[b]) for c, (b, h) in enumerate(chains)),
                        tuple(befores[c][1] for c in ids))

            logs0 = logs_of(0)
            runs1 = row_sums(logs0)

            def one_window():
                return window_grads(0, logs0, zero(1), zero(1))[0]

            def all_windows():
                def more(carry):
                    g, live = carry[:2]
                    return jnp.logical_and((qis[-1] + 1) * blk - g * win > 0, live > 0)

                def run_window(carry):
                    g, _, runs = carry
                    for c in ids:
                        runs_ref[c, g] = runs[c]
                    sums = row_sums(logs_of(g))
                    runs = tuple(runs[c] + sums[c] for c in ids)
                    return g + 1, still_live(runs).astype(jnp.int32), runs

                for c in ids:
                    runs_ref[c, 0] = jnp.zeros((blk, 1), F32)
                windows, _, _ = lax.while_loop(more, run_window, (jnp.int32(1), jnp.int32(1), runs1))

                def k_window(gg, carry):
                    dq_accs, esums = carry
                    g = windows - 1 - gg
                    parts, totals = window_grads(g, logs_of(g), [runs_ref[c, g] for c in ids], esums)
                    return (tuple(dq_accs[c] + parts[c] for c in ids), tuple(esums[c] + totals[c] for c in ids))

                return lax.fori_loop(0, windows, k_window, (zero(2 * dh), zero(1)))[0]

            earlier_keys = (qis[-1] + 1) * blk - win > 0
            dq_accs = lax.cond(jnp.logical_and(earlier_keys, still_live(runs1)), all_windows, one_window)
            for b in blocks:
                dqn[pl.ds(qoffs[b], blk), :] = hp.merge([dq_accs[2 * b + h] for h in heads])
            return 0

        lax.fori_loop(0, nq // len(blocks), q_step, 0)

        dq, dqg = hp.rms_bwd(qv, hp.rms_r(qv), qg_ref[...] * scale, dqn[...])
        dq_ref[...] = dq.astype(BF16)
        dqg_ref[...] += dqg * scale
        dk, dkg = hp.rms_bwd(kv, hp.rms_r(kv), kg_ref[...], dkn[...])
        dk_ref[...] = dk.astype(BF16)
        dkg_ref[...] += dkg
        dv_ref[...] = dvn[...].astype(BF16)
        finish()

    pair = lambda group: pl.BlockSpec((s, 2 * dh), lambda p: (0, group * (D_ATT // (2 * dh)) + p))
    vec2 = pl.BlockSpec((1, 2 * dh), lambda p: (0, 0))
    vec = pl.BlockSpec((1, dh), lambda p: (0, 0))
    return _call(
        body, "attn_bwd", (steps,), [pair(2), pair(3), pair(4), pair(0), vec2, vec2],
        [pair(0), pair(0), pair(0), vec, vec],
        [jax.ShapeDtypeStruct((s, D_ATT), BF16)] * 3 + [jax.ShapeDtypeStruct((1, dh), F32)] * 2,
        [proj, proj, proj, dya, jnp.tile(qg, (1, 2)), jnp.tile(kg, (1, 2))],
        scratch=[pltpu.VMEM((s, 2 * dh), BF16)] * 4 + [pltpu.VMEM((4, max_windows, blk, 1), F32)]
        + [pltpu.VMEM((s, 2 * dh), F32)] * 3, rider=rider)


def _block_diag(w):
    n, c, d = w.shape
    return jnp.einsum("ncd,nm->ncmd", w, jnp.eye(n, dtype=w.dtype)).reshape(n * c, n * d)


def _diag_blocks(full, n):
    c = full.shape[0] // n
    on_diagonal = jnp.eye(n, dtype=bool)[:, None, :, None]
    return jnp.sum(jnp.where(on_diagonal, full.reshape(n, c, n, c), 0.0), axis=2)


FFN1 = ["ffn1_w_gate", "ffn1_w_up", "ffn1_w_down"]
FFN2 = ["ffn2_w_gate", "ffn2_w_up", "ffn2_w_down"]


def _pair_sums(gb, names, where):
    theirs = _pair_exchange([gb[n] for n in names], "pair_exchange_" + names[0])
    pair, own = _pair_sum([gb[n] for n in names], theirs, where, "pair_sum_" + names[0])
    return _chip_rider(pair, own)


def _local_step(x, tgt, stacks, conv_stack, small, where):
    gate_up, down = FFN1[:2], FFN1[2:]
    big = dict(zip(gate_up, _gather_weights([stacks[n] for n in gate_up], [])))
    wa = _block_diag(small["rg_w_a"]).astype(BF16)
    wx = _block_diag(small["rg_w_x"]).astype(BF16)

    whole = lambda names: [big[n].reshape(-1, D_MODEL) for n in names]
    soon = down + ["w_in"]
    g1, u1, hb1, ab1, *landed = _ffn_up(x, small["ffn1_norm"], *whole(gate_up),
                                        rider=_gather_rider([stacks[n] for n in soon], [conv_stack]))
    big.update(zip(soon, landed))
    x1 = _ffn_down(x, ab1, *whole(down))
    conv_w = jnp.transpose(landed[-1], (1, 0, 2)).reshape(CONV_W, D_RNN)
    rg = (conv_w, small["conv_b"], wa, small["rg_b_a"], wx, small["rg_b_x"], small["rg_lambda"])
    riding = lambda names: _gather_rider([stacks[n] for n in names], [])
    proj, hb2, big["ffn2_w_gate"] = _mix_pre(x1, small["mix_norm"], big["w_in"], riding(["ffn2_w_gate"]))
    yr, hseq, big["ffn2_w_up"] = _rglru_fwd(proj, *rg, riding(["ffn2_w_up"]))
    ya, big["ffn2_w_down"], big["w_out"] = _attn_fwd(proj, small["q_norm"], small["k_norm"],
                                                     riding(["ffn2_w_down", "w_out"]))
    wout = big["w_out"].reshape(D_MODEL, D_MODEL)
    x2 = _mix_post(x1, yr, ya, small["rnn_out_norm"], small["attn_out_norm"], wout)
    dx3, g2, u2, hb3, ab3, loss = _ffn_fwd_loss(x2, small["ffn2_norm"], *whole(FFN2), tgt)

    gb, gs, slots = {}, {}, {}
    dx2, dg2, du2, dyb2, gs["ffn2_norm"] = _ffn_bwd_act(x2, small["ffn2_norm"], dx3, g2, u2, *whole(FFN2), "ffn2_bwd")
    gb["ffn2_w_gate"] = _ffn_wgrad(dg2, hb3, 1.0, "wgrad_gate_ffn2")
    gb["ffn2_w_up"] = _ffn_wgrad(du2, hb3, 1.0, "wgrad_up_ffn2")
    gb["ffn2_w_down"] = _ffn_wgrad(ab3, dyb2, 0.5, "wgrad_down_ffn2")
    dyr, dya, ycat, dxb2, gs["rnn_out_norm"], gs["attn_out_norm"] = _mix_post_bwd(
        dx2, yr, ya, small["rnn_out_norm"], small["attn_out_norm"], wout)
    gb["w_out"] = _wgrad_whole(ycat, [dxb2], False, "wgrad_out")
    early = FFN2 + ["w_out"]
    dq, dk, dv, gs["q_norm"], gs["k_norm"], *done = _attn_bwd(
        proj, dya, small["q_norm"], small["k_norm"], _pair_sums(gb, early, where))
    slots.update(zip(early, done))
    dxr, dgate, gs["conv_w"], gs["conv_b"], dwa, gs["rg_b_a"], dwx, gs["rg_b_x"], gs["rg_lambda"] = _rglru_bwd(
        proj, hseq, dyr, *rg)
    gs["rg_w_a"] = _diag_blocks(dwa, RNN_BLOCKS)
    gs["rg_w_x"] = _diag_blocks(dwx, RNN_BLOCKS)
    dps = [dxr, dgate, dq, dk, dv]
    dx1, gs["mix_norm"] = _mix_pre_bwd(x1, small["mix_norm"], dx2, dps, big["w_in"])
    dx0, dg1, du1, dyb1, gs["ffn1_norm"] = _ffn_bwd_act(x, small["ffn1_norm"], dx1, g1, u1, *whole(FFN1), "ffn1_bwd")

    mine = _place_shard(_pack([gs[n] for n in SMALL] + [loss[:, :1]]), where, F32, "place_small_grads",
                        by_device=True)
    gb["ffn1_w_gate"], everyone = _ffn_wgrad(dg1, hb1, 1.0, "wgrad_gate_ffn1", _small_rider(mine))
    gb["ffn1_w_up"], slots["ffn1_w_gate"] = _ffn_wgrad(
        du1, hb1, 1.0, "wgrad_up_ffn1", _pair_sums(gb, ["ffn1_w_gate"], where))
    gb["ffn1_w_down"], slots["ffn1_w_up"] = _ffn_wgrad(
        ab1, dyb1, 0.5, "wgrad_down_ffn1", _pair_sums(gb, ["ffn1_w_up"], where))
    gb["w_in"], slots["ffn1_w_down"] = _wgrad_whole(
        hb2, dps, True, "wgrad_in", _pair_sums(gb, ["ffn1_w_down"], where))
    last = _pair_sums(gb, ["w_in"], where)
    slots["w_in"], = _chip_exchange(last.plain, last.inplace)
    return dx0, slots, gs, everyone


ANY = pl.BlockSpec(memory_space=pl.ANY)


def _place():
    x, y, c = lax.axis_index("x"), lax.axis_index("y"), lax.axis_index("c")
    other_chips = [(1 - x, y), (x, 1 - y), (1 - x, 1 - y)]
    return x, y, c, 2 * x + y, other_chips


def _remote(src, dst, send_sem, recv_sem, to):
    return pltpu.make_async_remote_copy(src_ref=src, dst_ref=dst, send_sem=send_sem, recv_sem=recv_sem,
                                        device_id=to, device_id_type=MESH)


def _copy_plan(pairs):
    sends = [functools.partial(_remote, *a) for a, _ in pairs]
    arrivals = [functools.partial(_remote, *b) for _, b in pairs]
    return sends, arrivals


class _Rider:
    def __init__(self, plan, plain, inplace, n_copies=None, relay=None, n_relay=0):
        self.plan, self.plain, self.inplace = plan, list(plain), list(inplace)
        self.n_copies = n_copies or 3 * len(self.inplace)
        self.relay, self.n_relay = relay, n_relay

    def operands(self):
        return self.plain + self.inplace

    def out_shape(self):
        return [jax.ShapeDtypeStruct(a.shape, a.dtype) for a in self.inplace]

    def aliases(self, inputs_before, outputs_before):
        return {inputs_before + len(self.plain) + k: outputs_before + k for k in range(len(self.inplace))}

    def scratch(self):
        relay = [pltpu.SemaphoreType.DMA((self.n_relay,))] * 2 if self.relay else []
        return [pltpu.SemaphoreType.DMA((self.n_copies,))] * 2 + relay


def _split_refs(refs, n_in, n_out, rider):
    if rider is None:
        return refs[:n_in], refs[n_in:n_in + n_out], refs[n_in + n_out:], None
    r_in, r_out = len(rider.operands()), len(rider.inplace)
    outs_at = n_in + r_in
    n_sems = len(rider.scratch())
    rest = refs[outs_at + n_out + r_out:]
    sems = rest[len(rest) - n_sems:]
    filled = refs[outs_at + n_out:outs_at + n_out + r_out]
    copies = functools.partial(rider.plan, refs[n_in:n_in + len(rider.plain)], filled, *sems[:2])
    relay = functools.partial(rider.relay, filled, *sems[2:]) if rider.relay else None
    return refs[:n_in], refs[outs_at:outs_at + n_out], rest[:len(rest) - n_sems], (copies, relay)


def _ride(copies, first, last, middle=None):
    if copies is None:
        return lambda: None
    copies, relay = copies

    @pl.when(first)
    def _():
        _start(copies()[0])

    def start_relay():
        for make in copies()[1]:
            make().wait_recv()
        _start(relay()[0])

    if relay is not None and middle is not None:
        pl.when(middle)(start_relay)

    def finish():
        @pl.when(last)
        def _():
            if relay is None:
                _finish(*copies())
            else:
                if middle is None:
                    start_relay()
                _finish(copies()[0] + relay()[0], relay()[1])

    return finish


def _gather_rider(split, whole):
    n_split = len(split)
    return _Rider(lambda plain, stacks, ss, rs: _gather_ici(stacks, n_split, ss, rs), [], list(split) + list(whole),
                  relay=lambda stacks, ss, rs: _gather_d2d(stacks[:n_split], ss, rs), n_relay=3 * n_split)


def _chip_rider(sums, slots):
    return _Rider(_chip_copies, sums, slots)


def _start(makers):
    for make in makers:
        make().start()


def _finish(sends, arrivals):
    for make in arrivals:
        make().wait_recv()
    for make in sends:
        make().wait_send()


def _half(rows, c):
    return pl.ds(pl.multiple_of(c * rows, BF16_ROWS), rows)


def _gather_weights(split, whole):
    arrs = list(split) + list(whole)
    n, ns = len(arrs), len(split)

    def body(*refs):
        outs = refs[n:2 * n]
        send_sems, recv_sems, fsend_sems, frecv_sems = refs[2 * n:]
        sends, arrivals = _gather_ici(outs, ns, send_sems, recv_sems)
        passes, passed = _gather_d2d(outs[:ns], fsend_sems, frecv_sems)
        _start(sends)
        for k, make in enumerate(arrivals):
            make().wait_recv()
            if k < 3 * ns:
                passes[k]().start()
        _finish(sends + passes, passed)

    return pl.pallas_call(
        body, name="gather_weights",
        in_specs=[ANY] * n, out_specs=[ANY] * n,
        out_shape=[jax.ShapeDtypeStruct(a.shape, a.dtype) for a in arrs],
        input_output_aliases={i: i for i in range(n)},
        scratch_shapes=[pltpu.SemaphoreType.DMA((3 * n,)), pltpu.SemaphoreType.DMA((3 * n,)),
                        pltpu.SemaphoreType.DMA((3 * ns,)), pltpu.SemaphoreType.DMA((3 * ns,))],
    )(*arrs)


def _gather_ici(stacks, n_split, send_sems, recv_sems):
    x, y, c, me, chips = _place()

    def region(i, chip):
        if i < n_split:
            return stacks[i].at[chip, _half(stacks[i].shape[1] // 2, c)]
        return stacks[i].at[chip]

    pairs = []
    for i in range(len(stacks)):
        for p, (cx, cy) in enumerate(chips):
            k = 3 * i + p
            mine, got = region(i, me), region(i, 2 * cx + cy)
            sems, to = (send_sems.at[k], recv_sems.at[k]), (cx, cy, c)
            pairs.append(((mine, mine, *sems, to), (got, got, *sems, to)))
    return _copy_plan(pairs)


def _gather_d2d(stacks, send_sems, recv_sems):
    x, y, c, _, chips = _place()
    sibling = (x, y, 1 - c)
    pairs = []
    for i, stack in enumerate(stacks):
        rows = stack.shape[1] // 2
        for p, (cx, cy) in enumerate(chips):
            k = 3 * i + p
            got, theirs = stack.at[2 * cx + cy, _half(rows, c)], stack.at[2 * cx + cy, _half(rows, 1 - c)]
            sems = (send_sems.at[k], recv_sems.at[k])
            pairs.append(((got, got, *sems, sibling), (theirs, theirs, *sems, sibling)))
    return _copy_plan(pairs)


def _pair_exchange(grads, name):
    n = len(grads)

    def body(*refs):
        ins, theirs = refs[:n], refs[n:2 * n]
        send_sems, recv_sems = refs[2 * n:]
        x, y, c, _, _ = _place()
        sibling = (x, y, 1 - c)
        sends = [_remote(ins[k].at[:, _half(grads[k].shape[1] // 2, 1 - c)], theirs[k],
                         send_sems.at[k], recv_sems.at[k], sibling) for k in range(n)]
        for cp in sends:
            cp.start()
        for k in range(n):
            _remote(theirs[k], theirs[k], send_sems.at[k], recv_sems.at[k], sibling).wait_recv()
        for cp in sends:
            cp.wait_send()

    return pl.pallas_call(
        body, name=name,
        in_specs=[ANY] * n, out_specs=[ANY] * n,
        out_shape=[jax.ShapeDtypeStruct((g.shape[0], g.shape[1] // 2, g.shape[2]), g.dtype) for g in grads],
        scratch_shapes=[pltpu.SemaphoreType.DMA((n,))] * 2,
    )(*grads)


def _chip_exchange(sums, slots):
    n = len(sums)

    def body(*refs):
        sends, arrivals = _chip_copies(refs[:n], refs[2 * n:3 * n], *refs[3 * n:])
        _start(sends)
        _finish(sends, arrivals)

    return pl.pallas_call(
        body, name="grad_chip_exchange",
        in_specs=[ANY] * (2 * n), out_specs=[ANY] * n,
        out_shape=[jax.ShapeDtypeStruct(a.shape, a.dtype) for a in slots],
        input_output_aliases={n + k: k for k in range(n)},
        scratch_shapes=[pltpu.SemaphoreType.DMA((3 * n,)), pltpu.SemaphoreType.DMA((3 * n,))],
    )(*sums, *slots)


def _chip_copies(sums, slots, send_sems, recv_sems):
    x, y, c, me, chips = _place()
    pairs = []
    for k in range(len(sums)):
        for p, (cx, cy) in enumerate(chips):
            j = 3 * k + p
            got = slots[k].at[2 * cx + cy]
            sems, to = (send_sems.at[j], recv_sems.at[j]), (cx, cy, c)
            pairs.append(((sums[k].at[2 * cx + cy], slots[k].at[me], *sems, to), (got, got, *sems, to)))
    return _copy_plan(pairs)


def _half_swap(halves):
    n = len(halves)

    def body(*refs):
        outs = refs[n:2 * n]
        send_sems, recv_sems = refs[2 * n:]
        x, y, c, _, _ = _place()
        sibling = (x, y, 1 - c)
        sends = [_remote(outs[k].at[c], outs[k].at[c], send_sems.at[k], recv_sems.at[k], sibling) for k in range(n)]
        for cp in sends:
            cp.start()
        for k in range(n):
            got = outs[k].at[1 - c]
            _remote(got, got, send_sems.at[k], recv_sems.at[k], sibling).wait_recv()
        for cp in sends:
            cp.wait_send()

    return pl.pallas_call(
        body, name="grad_half_swap",
        in_specs=[ANY] * n, out_specs=[ANY] * n,
        out_shape=[jax.ShapeDtypeStruct(a.shape, a.dtype) for a in halves],
        input_output_aliases={k: k for k in range(n)},
        scratch_shapes=[pltpu.SemaphoreType.DMA((n,))] * 2,
    )(*halves)


def _small_rider(stack):
    n_dev = 2 * N_CHIPS

    def plan(_, stacks, send_sems, recv_sems):
        x, y, c, _, _ = _place()
        mine = stacks[0].at[4 * x + 2 * y + c]
        pairs = []
        for k in range(1, n_dev):
            px, py, pc = x ^ ((k >> 2) & 1), y ^ ((k >> 1) & 1), c ^ (k & 1)
            got = stacks[0].at[4 * px + 2 * py + pc]
            sems = (send_sems.at[k - 1], recv_sems.at[k - 1])
            pairs.append(((mine, mine, *sems, (px, py, pc)), (got, got, *sems, (px, py, pc))))
        return _copy_plan(pairs)

    return _Rider(plan, [], [stack], n_dev - 1)


def _row_tile(r):
    return r // 4 if r >= 256 and (r // 4) % BF16_ROWS == 0 else r


def _prefetch_call(body, name, grid, in_specs, out_specs, out_shape):
    spec = pltpu.PrefetchScalarGridSpec(num_scalar_prefetch=1, grid=grid, in_specs=in_specs, out_specs=out_specs)
    return pl.pallas_call(body, name=name, grid_spec=spec, out_shape=out_shape,
                          compiler_params=_params(("arbitrary",) * len(grid)))


def _place_shard(w2d, where, dtype, name, by_device=False):
    r, c = w2d.shape
    tr = _row_tile(r)
    slots = 2 * N_CHIPS if by_device else N_CHIPS
    slot = (lambda s: 2 * s[1] + s[0]) if by_device else (lambda s: s[1])

    def body(where_ref, w_ref, out_ref):
        out_ref[...] = w_ref[...].astype(dtype)

    return _prefetch_call(
        body, name, (r // tr,), [pl.BlockSpec((tr, c), lambda i, s: (i, 0))],
        pl.BlockSpec((None, tr, c), lambda i, s: (slot(s), i, 0)),
        jax.ShapeDtypeStruct((slots, r, c), dtype))(where, w2d)


def _place_shards(w2ds, where, name):
    n = len(w2ds)
    steps = N_CHIPS
    assert all(w.shape[0] % (BF16_ROWS * steps) == 0 for w in w2ds)

    def body(where_ref, *refs):
        for k in range(n):
            refs[n + k][...] = refs[k][...].astype(BF16)

    tile = lambda w: (w.shape[0] // steps, w.shape[1])
    return _prefetch_call(
        body, name, (steps,), [pl.BlockSpec(tile(w), lambda i, s: (i, 0)) for w in w2ds],
        [pl.BlockSpec((None,) + tile(w), lambda i, s: (s[1], i, 0)) for w in w2ds],
        [jax.ShapeDtypeStruct((N_CHIPS,) + w.shape, BF16) for w in w2ds])(where, *w2ds)


def _pair_sum(fulls, theirs, where, name):
    n = len(fulls)

    def body(where_ref, *refs):
        for k in range(n):
            a_ref, b_ref, out_ref, own_ref = refs[k], refs[n + k], refs[2 * n + k], refs[3 * n + k]
            total = (a_ref[...].astype(F32) + b_ref[...].astype(F32)).astype(BF16)
            out_ref[...] = total

            @pl.when(pl.program_id(0) == where_ref[1])
            def _():
                own_ref[...] = total

    half = lambda t: pl.BlockSpec((None,) + t.shape[1:], lambda j, s: (j, s[0], 0))
    blk = lambda t: pl.BlockSpec((None,) + t.shape[1:], lambda j, s: (j, 0, 0))
    own = lambda t: pl.BlockSpec((None,) + t.shape[1:], lambda j, s: (s[1], 0, 0))
    shapes = [jax.ShapeDtypeStruct(t.shape, BF16) for t in theirs]
    outs = _prefetch_call(
        body, name, (N_CHIPS,), [half(t) for t in theirs] + [blk(t) for t in theirs],
        [blk(t) for t in theirs] + [own(t) for t in theirs], shapes + shapes)(where, *fulls, *theirs)
    return outs[:n], outs[n:]


def _chip_sum(slots, where, name):
    n = len(slots)
    steps = 2
    assert all(a.shape[1] % (BF16_ROWS * steps) == 0 for a in slots)

    def body(where_ref, *refs):
        for k in range(n):
            a_ref, out_ref = refs[k], refs[n + k]
            total = a_ref[0].astype(F32)
            for j in range(1, a_ref.shape[0]):
                total = total + a_ref[j].astype(F32)
            out_ref[...] = total

    tile = lambda a: (a.shape[1] // steps, a.shape[2])
    return _prefetch_call(
        body, name, (steps,), [pl.BlockSpec((a.shape[0],) + tile(a), lambda i, s: (0, i, 0)) for a in slots],
        [pl.BlockSpec((None,) + tile(a), lambda i, s: (s[0], i, 0)) for a in slots],
        [jax.ShapeDtypeStruct((2,) + a.shape[1:], F32) for a in slots])(where, *slots)


def _slot_sum(a, name):
    nb, r, c = a.shape
    tr = _row_tile(r)

    def body(a_ref, out_ref):
        total = a_ref[0].astype(F32)
        for j in range(1, nb):
            total = total + a_ref[j].astype(F32)
        out_ref[...] = total

    return pl.pallas_call(
        body, name=name, grid=(r // tr,),
        in_specs=[pl.BlockSpec((nb, tr, c), lambda i: (0, i, 0))],
        out_specs=pl.BlockSpec((tr, c), lambda i: (i, 0)),
        out_shape=jax.ShapeDtypeStruct((r, c), F32), compiler_params=_params(("arbitrary",)),
    )(a)


def _adamw(ws, gs, ms, vs, name, steps=1):
    n = len(ws)
    c1 = 1.0 - ADAM_B1 ** ADAM_STEP
    c2 = 1.0 - ADAM_B2 ** ADAM_STEP
    assert all(w.shape[0] % steps == 0 and (steps == 1 or w.shape[0] // steps % 8 == 0) for w in ws)

    def body(*refs):
        for k in range(n):
            w_ref, g_ref, m_ref, v_ref = (refs[j * n + k] for j in range(4))
            g_out, d_ref, m2_ref, v2_ref = (refs[(4 + j) * n + k] for j in range(4))
            gv = g_ref[...]
            g_out[...] = gv
            m2 = ADAM_B1 * m_ref[...] + (1.0 - ADAM_B1) * gv
            v2 = ADAM_B2 * v_ref[...] + (1.0 - ADAM_B2) * (gv * gv)
            m2_ref[...] = m2
            v2_ref[...] = v2
            d_ref[...] = -ADAM_LR * ((m2 / c1) / (jnp.sqrt(v2 / c2) + ADAM_EPS) + ADAM_WD * w_ref[...])

    blks = [pl.BlockSpec((w.shape[0] // steps, w.shape[1]), lambda i: (i, 0)) for w in ws]
    shapes = [jax.ShapeDtypeStruct(w.shape, F32) for w in ws]
    outs = pl.pallas_call(
        body, name=name, grid=(steps,), in_specs=blks * 4, out_specs=blks * 4, out_shape=shapes * 4,
        compiler_params=_params(("arbitrary",)),
    )(*ws, *gs, *ms, *vs)
    return [outs[j * n:(j + 1) * n] for j in range(4)]


WEIGHTS = ["ffn1_norm", "ffn1_w_gate", "ffn1_w_up", "ffn1_w_down", "mix_norm", "w_in", "conv_w", "conv_b",
           "rg_w_a", "rg_b_a", "rg_w_x", "rg_b_x", "rg_lambda", "q_norm", "k_norm", "rnn_out_norm",
           "attn_out_norm", "w_out", "ffn2_norm", "ffn2_w_gate", "ffn2_w_up", "ffn2_w_down"]
BIG = ["ffn1_w_gate", "ffn1_w_up", "ffn1_w_down", "w_in", "w_out", "ffn2_w_gate", "ffn2_w_up", "ffn2_w_down"]
SMALL = [n for n in WEIGHTS if n not in BIG]
PACK_LANES = 128
PACK_ROW_ALIGN = 8


def _hidden_major(name, a):
    return jnp.transpose(a) if name.endswith(("w_gate", "w_up")) else a


def _pack(parts):
    sizes = [math.prod(p.shape) for p in parts]
    unit = PACK_LANES * PACK_ROW_ALIGN
    padded = -(-sum(sizes) // unit) * unit
    flat, at = 0.0, 0
    for p, size in zip(parts, sizes):
        flat = flat + jnp.pad(p.reshape(-1), (at, padded - at - size))
        at += size
    return flat.reshape(-1, PACK_LANES)


def _unpack(packed, shapes):
    flat = packed.reshape(-1)
    out, at = [], 0
    for shp in shapes:
        size = math.prod(shp)
        out.append(flat[at:at + size].reshape(shp))
        at += size
    return out


def kernel(x, ffn1_norm, ffn1_w_gate, ffn1_w_up, ffn1_w_down, mix_norm, w_in, conv_w, conv_b, rg_w_a, rg_b_a, rg_w_x, rg_b_x, rg_lambda, q_norm, k_norm, rnn_out_norm, attn_out_norm, w_out, ffn2_norm, ffn2_w_gate, ffn2_w_up, ffn2_w_down, loss_target, m_ffn1_norm, m_ffn1_w_gate, m_ffn1_w_up, m_ffn1_w_down, m_mix_norm, m_w_in, m_conv_w, m_conv_b, m_rg_w_a, m_rg_b_a, m_rg_w_x, m_rg_b_x, m_rg_lambda, m_q_norm, m_k_norm, m_rnn_out_norm, m_attn_out_norm, m_w_out, m_ffn2_norm, m_ffn2_w_gate, m_ffn2_w_up, m_ffn2_w_down, v_ffn1_norm, v_ffn1_w_gate, v_ffn1_w_up, v_ffn1_w_down, v_mix_norm, v_w_in, v_conv_w, v_conv_b, v_rg_w_a, v_rg_b_a, v_rg_w_x, v_rg_b_x, v_rg_lambda, v_q_norm, v_k_norm, v_rnn_out_norm, v_attn_out_norm, v_w_out, v_ffn2_norm, v_ffn2_w_gate, v_ffn2_w_up, v_ffn2_w_down):
    given = dict(locals())
    w = {n: given[n] for n in WEIGHTS}
    m = {n: given["m_" + n] for n in WEIGHTS}
    v = {n: given["v_" + n] for n in WEIGHTS}
    chip = 2 * lax.axis_index("x") + lax.axis_index("y")

    where = jnp.stack([lax.axis_index("c"), chip]).astype(jnp.int32)

    stacks = dict(zip(BIG, _place_shards([_hidden_major(n, w[n][0]) for n in BIG], where, "place_weights")))
    conv_stack = _place_shard(w["conv_w"][0], where, F32, "place_conv_w")
    small = {n: (w[n][0] if w[n].ndim > 2 else w[n]) for n in SMALL if n != "conv_w"}

    grad_x, slots, gs, everyone = _local_step(x[0], loss_target[0], stacks, conv_stack, small, where)

    swapped = _half_swap(_chip_sum([slots[n] for n in BIG], where, "chip_sums"))
    g2s = [t.reshape(t.shape[0] * t.shape[1], t.shape[2]) for t in swapped]
    flat = lambda tree: [_hidden_major(n, tree[n][0]) for n in BIG]
    g2s, d2s, m2s, v2s = _adamw(flat(w), g2s, flat(m), flat(v), "adamw_weights", ADAMW_STEPS)
    grads, deltas, new_m, new_v = {}, {}, {}, {}
    for tree, parts in ((grads, g2s), (deltas, d2s), (new_m, m2s), (new_v, v2s)):
        tree.update({n: _hidden_major(n, a).reshape(w[n].shape) for n, a in zip(BIG, parts)})

    full_shapes = [gs[n].shape for n in SMALL]
    *summed, loss = _unpack(_slot_sum(everyone, "small_grad_sum"), full_shapes + [(1, 1)])
    g_parts = dict(zip(SMALL, summed))
    quarter = D_RNN // N_CHIPS
    g_parts["conv_w"] = lax.dynamic_slice_in_dim(g_parts["conv_w"], chip * quarter, quarter, axis=1)
    local_shapes = [w[n].shape for n in SMALL]
    pk = lambda tree: _pack([tree[n] for n in SMALL])
    (g_s,), (d_s,), (m_s,), (v_s,) = _adamw([pk(w)], [pk(g_parts)], [pk(m)], [pk(v)], "adamw_small")
    for tree, packed in ((grads, g_s), (deltas, d_s), (new_m, m_s), (new_v, v_s)):
        tree.update(zip(SMALL, _unpack(packed, local_shapes)))

    return (loss[0, 0], grad_x.reshape(x.shape), *[grads[n] for n in WEIGHTS], *[deltas[n] for n in WEIGHTS],
            *[new_m[n] for n in WEIGHTS], *[new_v[n] for n in WEIGHTS])
```

```python
import functools
import math

import jax
import jax.numpy as jnp
from jax import lax
from jax.experimental import pallas as pl
from jax.experimental.pallas import tpu as pltpu

F32 = jnp.float32
BF16 = jnp.bfloat16
MESH = pl.DeviceIdType.MESH

D_MODEL = 1024
N_CHIPS = 4
D_RNN = 512
D_ATT = 512
N_HEADS = 8
HEAD_DIM = 64
RNN_BLOCKS = 8
CONV_W = 4
RG_C = 8.0
N_IN = 2 * D_RNN + 3 * D_ATT
EPS = 1e-6
ATT_BLOCK = 128
ATT_WINDOW = 384
ATT_SPLIT = 256
EXP_ZERO = -105.0

ADAM_LR = 0.001
ADAM_B1 = 0.9
ADAM_B2 = 0.999
ADAM_EPS = 1e-08
ADAM_WD = 0.01
ADAM_STEP = 10

V7X_VMEM_LIMIT = 60 * 1024 * 1024
V7X_MXU_WIDTH = 256
TOKEN_TILE = 512
SUBLANES = 8
BF16_ROWS = 16
FFN_TILE = 256
WGRAD_TILE = 2048
WHOLE_TILE = 1024
ADAMW_STEPS = 8

GELU_K0 = math.sqrt(2.0 / math.pi)
GELU_K1 = 0.044715


def _params(sem=None):
    return pltpu.CompilerParams(dimension_semantics=sem, vmem_limit_bytes=V7X_VMEM_LIMIT)


def _dot(a, b):
    return jnp.dot(a, b, preferred_element_type=F32)


def _dot_nt(a, b):
    return lax.dot_general(a, b, (((1,), (1,)), ((), ())), preferred_element_type=F32)


def _dot_tn(a, b):
    return lax.dot_general(a, b, (((0,), (0,)), ((), ())), preferred_element_type=F32)


def _sigmoid(x):
    return 1.0 / (1.0 + jnp.exp(-x))


def _rms_r(xv):
    return lax.rsqrt(jnp.mean(xv * xv, axis=-1, keepdims=True) + EPS)


def _rms_bwd(xv, r, nw, dh):
    t = dh * nw
    dx = r * t - xv * (r * r * r * jnp.mean(t * xv, axis=-1, keepdims=True))
    dn = jnp.sum(dh * xv * r, axis=0, keepdims=True)
    return dx, dn


def _gelu(x):
    t = jnp.tanh(GELU_K0 * (x + GELU_K1 * x * x * x))
    return 0.5 * x * (1.0 + t)


def _gelu_grad(x):
    t = jnp.tanh(GELU_K0 * (x + GELU_K1 * x * x * x))
    return 0.5 * (1.0 + t) + 0.5 * x * (1.0 - t * t) * (GELU_K0 * (1.0 + 3.0 * GELU_K1 * x * x))


def _expm1_neg(x):
    p = 1.0 + x * (1.0 / 6.0)
    for k in (5.0, 4.0, 3.0, 2.0):
        p = 1.0 + x * (1.0 / k) * p
    return jnp.where(x > -0.25, x * p, jnp.exp(x) - 1.0)


def _log_sigmoid(x):
    return jnp.minimum(x, 0.0) - jnp.log(1.0 + jnp.exp(-jnp.abs(x)))


def _tile(s):
    return min(TOKEN_TILE, s)


def _ffn_chunks(f):
    cut = f // 2 // V7X_MXU_WIDTH * V7X_MXU_WIDTH
    return ((0, cut), (cut, f)) if 0 < cut < f else ((0, f),)


def _ffn_fwd_loss(x, nw, wg, wu, wd, tgt):
    s, d = x.shape
    f = wg.shape[0]
    tm = min(FFN_TILE, s)
    ni = s // tm
    assert s % tm == 0

    def body(x_ref, nw_ref, wg_ref, wu_ref, wd_ref, tgt_ref, out_ref, g_ref, u_ref, hb_ref, ab_ref, loss_ref):
        i = pl.program_id(0)
        xv = x_ref[...]
        hb = (xv * _rms_r(xv) * nw_ref[...]).astype(BF16)
        hb_ref[...] = hb
        y = jnp.zeros((tm, d), F32)
        for lo, hi in _ffn_chunks(f):
            g = _dot_nt(hb, wg_ref[lo:hi, :])
            u = _dot_nt(hb, wu_ref[lo:hi, :])
            g_ref[:, lo:hi] = g.astype(BF16)
            u_ref[:, lo:hi] = u.astype(BF16)
            ab = (g * _sigmoid(g) * u).astype(BF16)
            ab_ref[:, lo:hi] = ab
            y = y + _dot(ab, wd_ref[lo:hi, :])
        diff = xv + 0.5 * y - tgt_ref[...]
        out_ref[...] = diff * (1.0 / d)

        @pl.when(i == 0)
        def _():
            loss_ref[...] = jnp.zeros_like(loss_ref)

        loss_ref[...] += jnp.sum(diff * diff) * (0.5 / d)

    row = pl.BlockSpec((tm, d), lambda i: (i, 0))
    weight = pl.BlockSpec((f, d), lambda i: (0, 0), pipeline_mode=pl.Buffered(1))
    blk = pl.BlockSpec((tm, f), lambda i: (i, 0))
    wide = jax.ShapeDtypeStruct((s, f), BF16)
    return _call(body, "ffn_fwd_loss", (ni,),
                 [row, pl.BlockSpec((1, d), lambda i: (0, 0)), weight, weight, weight, row],
                 [row, blk, blk, row, blk, pl.BlockSpec((1, 128), lambda i: (0, 0))],
                 [jax.ShapeDtypeStruct((s, d), F32), wide, wide, jax.ShapeDtypeStruct((s, d), BF16), wide,
                  jax.ShapeDtypeStruct((1, 128), F32)], [x, nw, wg, wu, wd, tgt])


def _ffn_up(x, nw, wg, wu, rider=None):
    s, d = x.shape
    f = wg.shape[0]
    tm = min(FFN_TILE, s)
    ni = s // tm
    assert s % tm == 0

    def body(*refs):
        (x_ref, nw_ref, wg_ref, wu_ref), (g_ref, u_ref, hb_ref, ab_ref), _, copies = _split_refs(refs, 4, 4, rider)
        i = pl.program_id(0)
        finish = _ride(copies, i == 0, i == ni - 1)
        xv = x_ref[...]
        hb = (xv * _rms_r(xv) * nw_ref[...]).astype(BF16)
        hb_ref[...] = hb
        for lo, hi in _ffn_chunks(f):
            g = _dot_nt(hb, wg_ref[lo:hi, :])
            u = _dot_nt(hb, wu_ref[lo:hi, :])
            g_ref[:, lo:hi] = g.astype(BF16)
            u_ref[:, lo:hi] = u.astype(BF16)
            ab_ref[:, lo:hi] = (g * _sigmoid(g) * u).astype(BF16)
        finish()

    row = pl.BlockSpec((tm, d), lambda i: (i, 0))
    weight = pl.BlockSpec((f, d), lambda i: (0, 0), pipeline_mode=pl.Buffered(1))
    blk = pl.BlockSpec((tm, f), lambda i: (i, 0))
    wide = jax.ShapeDtypeStruct((s, f), BF16)
    return _call(body, "ffn_up", (ni,), [row, pl.BlockSpec((1, d), lambda i: (0, 0)), weight, weight],
                 [blk, blk, row, blk], [wide, wide, jax.ShapeDtypeStruct((s, d), BF16), wide], [x, nw, wg, wu],
                 rider=rider)


def _ffn_down(x, ab, wd):
    s, d = x.shape
    f = wd.shape[0]
    tm = _tile(s)
    assert s % tm == 0

    def body(x_ref, ab_ref, wd_ref, out_ref):
        out_ref[...] = x_ref[...] + 0.5 * _dot(ab_ref[...], wd_ref[...])

    row = pl.BlockSpec((tm, d), lambda i: (i, 0))
    return _call(body, "ffn_down", (s // tm,),
                 [row, pl.BlockSpec((tm, f), lambda i: (i, 0)),
                  pl.BlockSpec((f, d), lambda i: (0, 0), pipeline_mode=pl.Buffered(1))],
                 [row], [jax.ShapeDtypeStruct((s, d), F32)], [x, ab, wd])[0]


def _call(body, name, grid, in_specs, out_specs, out_shape, args, scratch=(), rider=None):
    in_specs, out_specs, out_shape, scratch = list(in_specs), list(out_specs), list(out_shape), list(scratch)
    extra, aliases = [], {}
    if rider is not None:
        extra = rider.operands()
        aliases = rider.aliases(len(args), len(out_shape))
        in_specs += [ANY] * len(extra)
        out_specs += [ANY] * len(rider.inplace)
        out_shape += rider.out_shape()
        scratch += rider.scratch()
    return pl.pallas_call(
        body, name=name, grid=grid, in_specs=in_specs, out_specs=out_specs, out_shape=out_shape,
        input_output_aliases=aliases, scratch_shapes=scratch,
        compiler_params=_params(("arbitrary",) * len(grid)),
    )(*args, *extra)


def _ffn_bwd_act(x, nw, dy, g, u, wg, wu, wd, name):
    s, d = x.shape
    f = wg.shape[0]
    tm = min(FFN_TILE, s)
    assert s % tm == 0

    def body(x_ref, nw_ref, dy_ref, g_ref, u_ref, wg_ref, wu_ref, wd_ref,
             dx_ref, dg_ref, du_ref, dyb_ref, dnw_ref):
        dyv = dy_ref[...]
        dyb = dyv.astype(BF16)
        dyb_ref[...] = dyb
        dh = jnp.zeros((tm, d), F32)
        for lo, hi in _ffn_chunks(f):
            da = 0.5 * _dot_nt(dyb, wd_ref[lo:hi, :])
            gv = g_ref[:, lo:hi].astype(F32)
            sg = _sigmoid(gv)
            dub = (da * (gv * sg)).astype(BF16)
            dgb = (da * u_ref[:, lo:hi].astype(F32) * (sg * (1.0 + gv * (1.0 - sg)))).astype(BF16)
            dg_ref[:, lo:hi] = dgb
            du_ref[:, lo:hi] = dub
            dh = dh + _dot(dgb, wg_ref[lo:hi, :]) + _dot(dub, wu_ref[lo:hi, :])
        xv = x_ref[...]
        dx, dn = _rms_bwd(xv, _rms_r(xv), nw_ref[...], dh)
        dx_ref[...] = dyv + dx

        @pl.when(pl.program_id(0) == 0)
        def _():
            dnw_ref[...] = jnp.zeros_like(dnw_ref)

        dnw_ref[...] += dn

    row = pl.BlockSpec((tm, d), lambda i: (i, 0))
    vec = pl.BlockSpec((1, d), lambda i: (0, 0))
    blk = pl.BlockSpec((tm, f), lambda i: (i, 0))
    weight = pl.BlockSpec((f, d), lambda i: (0, 0), pipeline_mode=pl.Buffered(1))
    return _call(
        body, name, (s // tm,), [row, vec, row, blk, blk, weight, weight, weight], [row, blk, blk, row, vec],
        [jax.ShapeDtypeStruct((s, d), F32), jax.ShapeDtypeStruct((s, f), BF16),
         jax.ShapeDtypeStruct((s, f), BF16), jax.ShapeDtypeStruct((s, d), BF16),
         jax.ShapeDtypeStruct((1, d), F32)],
        [x, nw, dy, g, u, wg, wu, wd])


def _wgrad(a, b, a_spec, b_spec, out_rows, out_cols, scale, name, tk, rider=None, per_step=1):
    s = a.shape[-2]
    nk = s // tk
    steps = N_CHIPS // per_step
    assert s % tk == 0

    def body(*refs):
        (a_ref, b_ref), (out_ref,), (acc,), copies = _split_refs(refs, 2, 1, rider)
        j, k = pl.program_id(0), pl.program_id(1)
        finish = _ride(copies, jnp.logical_and(j == 0, k == 0), jnp.logical_and(j == steps - 1, k == nk - 1))

        @pl.when(k == 0)
        def _():
            acc[...] = jnp.zeros_like(acc)

        acc[...] += _dot_tn(a_ref[...], b_ref[...])

        @pl.when(k == nk - 1)
        def _():
            for t in range(per_step):
                out_ref[t] = (acc[t * out_rows:(t + 1) * out_rows, :] * scale).astype(BF16)

        finish()

    outs = _call(
        body, name, (steps, nk), [a_spec(tk), b_spec(tk)],
        [pl.BlockSpec((per_step, out_rows, out_cols), lambda j, k: (j, 0, 0))],
        [jax.ShapeDtypeStruct((N_CHIPS, out_rows, out_cols), BF16)], [a, b],
        scratch=[pltpu.VMEM((per_step * out_rows, out_cols), F32)], rider=rider)
    return outs[0] if rider is None else outs


def _beside(refs):
    return jnp.concatenate([r[...] for r in refs], axis=1) if len(refs) > 1 else refs[0][...]


def _wgrad_whole(a, bs, col_blocks, name, rider=None):
    s, m = a.shape
    n = sum(b.shape[1] for b in bs)
    tk = min(WHOLE_TILE, s)
    nk = s // tk
    assert s % tk == 0
    out_shape = (N_CHIPS, m, n // N_CHIPS) if col_blocks else (N_CHIPS, m // N_CHIPS, n)

    def body(*refs):
        (a_ref, *b_refs), (out_ref,), (acc,), copies = _split_refs(refs, 1 + len(bs), 1, rider)
        k = pl.program_id(0)
        finish = _ride(copies, k == 0, k == nk - 1)

        @pl.when(k == 0)
        def _():
            acc[...] = jnp.zeros_like(acc)

        acc[...] += _dot_tn(a_ref[...], _beside(b_refs))

        @pl.when(k == nk - 1)
        def _():
            for j in range(N_CHIPS):
                if col_blocks:
                    out_ref[j] = acc[:, j * out_shape[2]:(j + 1) * out_shape[2]].astype(BF16)
                else:
                    out_ref[j] = acc[j * out_shape[1]:(j + 1) * out_shape[1], :].astype(BF16)

        finish()

    outs = _call(
        body, name, (nk,),
        [pl.BlockSpec((tk, m), lambda k: (k, 0))] + [pl.BlockSpec((tk, b.shape[1]), lambda k: (k, 0)) for b in bs],
        [pl.BlockSpec(out_shape, lambda k: (0, 0, 0))], [jax.ShapeDtypeStruct(out_shape, BF16)], [a, *bs],
        scratch=[pltpu.VMEM((m, n), F32)], rider=rider)
    return outs[0] if rider is None else outs


def _ffn_wgrad(hidden, shared, scale, name, rider=None):
    s, d = shared.shape
    half = hidden.shape[1] // 2
    return _wgrad(hidden, shared, lambda tk: pl.BlockSpec((tk, half), lambda j, k: (k, j)),
                  lambda tk: pl.BlockSpec((tk, d), lambda j, k: (k, 0)), half // 2, d, scale, name,
                  min(WGRAD_TILE, s), rider, per_step=2)


def _mix_pre(x, nw, win, rider=None):
    s, d = x.shape
    nb, _, cb = win.shape
    tm = _tile(s)
    ni = s // tm
    assert s % tm == 0

    def body(*refs):
        (x_ref, nw_ref, w_ref), (p_ref, hb_ref), _, copies = _split_refs(refs, 3, 2, rider)
        finish = _ride(copies, pl.program_id(0) == 0, pl.program_id(0) == ni - 1)
        xv = x_ref[...]
        hb = (xv * _rms_r(xv) * nw_ref[...]).astype(BF16)
        hb_ref[...] = hb
        for j in range(nb):
            p_ref[:, j * cb:(j + 1) * cb] = _dot(hb, w_ref[j])
        finish()

    row = pl.BlockSpec((tm, d), lambda i: (i, 0))
    return _call(
        body, "mix_pre", (ni,),
        [row, pl.BlockSpec((1, d), lambda i: (0, 0)),
         pl.BlockSpec((nb, d, cb), lambda i: (0, 0, 0), pipeline_mode=pl.Buffered(1))],
        [pl.BlockSpec((tm, nb * cb), lambda i: (i, 0)), row],
        [jax.ShapeDtypeStruct((s, nb * cb), F32), jax.ShapeDtypeStruct((s, d), BF16)], [x, nw, win], rider=rider)


def _mix_pre_bwd(x, nw, dres, dps, win):
    s, d = x.shape
    nb, _, cb = win.shape
    tm = _tile(s)
    assert s % tm == 0 and sum(p.shape[1] for p in dps) == nb * cb

    def body(x_ref, nw_ref, dres_ref, *rest):
        *dp_refs, w_ref, dx_ref, dnw_ref, w_all = rest

        @pl.when(pl.program_id(0) == 0)
        def _():
            dnw_ref[...] = jnp.zeros_like(dnw_ref)
            for j in range(nb):
                w_all[:, j * cb:(j + 1) * cb] = w_ref[j]

        dh = _dot_nt(_beside(dp_refs), w_all[...])
        xv = x_ref[...]
        dx, dn = _rms_bwd(xv, _rms_r(xv), nw_ref[...], dh)
        dx_ref[...] = dres_ref[...] + dx
        dnw_ref[...] += dn

    row = pl.BlockSpec((tm, d), lambda i: (i, 0))
    vec = pl.BlockSpec((1, d), lambda i: (0, 0))
    return pl.pallas_call(
        body, name="mix_pre_bwd", grid=(s // tm,),
        in_specs=[row, vec, row] + [pl.BlockSpec((tm, p.shape[1]), lambda i: (i, 0)) for p in dps]
        + [pl.BlockSpec((nb, d, cb), lambda i: (0, 0, 0), pipeline_mode=pl.Buffered(1))],
        out_specs=[row, vec],
        out_shape=[jax.ShapeDtypeStruct((s, d), F32), jax.ShapeDtypeStruct((1, d), F32)],
        scratch_shapes=[pltpu.VMEM((d, nb * cb), BF16)],
        compiler_params=_params(("arbitrary",)),
    )(x, nw, dres, *dps, win)


def _mix_post(x, yr, ya, nr, na, wout):
    s, d = x.shape
    h = yr.shape[1]
    tm = _tile(s)

    def body(x_ref, yr_ref, ya_ref, nr_ref, na_ref, w_ref, out_ref):
        yrv = yr_ref[...]
        yav = ya_ref[...]
        onb = (yrv * _rms_r(yrv) * nr_ref[...]).astype(BF16)
        oab = (yav * _rms_r(yav) * na_ref[...]).astype(BF16)
        out_ref[...] = x_ref[...] + _dot(jnp.concatenate([onb, oab], axis=1), w_ref[...])

    row = pl.BlockSpec((tm, d), lambda i: (i, 0))
    half = pl.BlockSpec((tm, h), lambda i: (i, 0))
    vec = pl.BlockSpec((1, h), lambda i: (0, 0))
    return pl.pallas_call(
        body, name="mix_post", grid=(s // tm,),
        in_specs=[row, half, half, vec, vec, pl.BlockSpec((2 * h, d), lambda i: (0, 0))],
        out_specs=row, out_shape=jax.ShapeDtypeStruct((s, d), F32),
        compiler_params=_params(("arbitrary",)),
    )(x, yr, ya, nr, na, wout)


def _mix_post_bwd(dx, yr, ya, nr, na, wout):
    s, d = dx.shape
    h = yr.shape[1]
    tm = _tile(s)

    def body(dx_ref, yr_ref, ya_ref, nr_ref, na_ref, w_ref,
             dyr_ref, dya_ref, yc_ref, dxb_ref, dnr_ref, dna_ref):
        i = pl.program_id(0)
        dxb = dx_ref[...].astype(BF16)
        dxb_ref[...] = dxb
        dyc = _dot_nt(dxb, w_ref[...])
        yrv = yr_ref[...]
        yav = ya_ref[...]
        rr = _rms_r(yrv)
        ra = _rms_r(yav)
        yc_ref[:, 0:h] = (yrv * rr * nr_ref[...]).astype(BF16)
        yc_ref[:, h:2 * h] = (yav * ra * na_ref[...]).astype(BF16)
        dyr, dnr = _rms_bwd(yrv, rr, nr_ref[...], dyc[:, 0:h])
        dya, dna = _rms_bwd(yav, ra, na_ref[...], dyc[:, h:2 * h])
        dyr_ref[...] = dyr
        dya_ref[...] = dya

        @pl.when(i == 0)
        def _():
            dnr_ref[...] = jnp.zeros_like(dnr_ref)
            dna_ref[...] = jnp.zeros_like(dna_ref)

        dnr_ref[...] += dnr
        dna_ref[...] += dna

    row = pl.BlockSpec((tm, d), lambda i: (i, 0))
    half = pl.BlockSpec((tm, h), lambda i: (i, 0))
    vec = pl.BlockSpec((1, h), lambda i: (0, 0))
    return pl.pallas_call(
        body, name="mix_post_bwd", grid=(s // tm,),
        in_specs=[row, half, half, vec, vec, pl.BlockSpec((2 * h, d), lambda i: (0, 0))],
        out_specs=[half, half, pl.BlockSpec((tm, 2 * h), lambda i: (i, 0)), row, vec, vec],
        out_shape=[jax.ShapeDtypeStruct((s, h), F32), jax.ShapeDtypeStruct((s, h), F32),
                   jax.ShapeDtypeStruct((s, 2 * h), BF16), jax.ShapeDtypeStruct((s, d), BF16),
                   jax.ShapeDtypeStruct((1, h), F32), jax.ShapeDtypeStruct((1, h), F32)],
        compiler_params=_params(("arbitrary",)),
    )(dx, yr, ya, nr, na, wout)


def _shift_down(xv, s, prev8):
    rolled = pltpu.roll(xv, s, 0)
    row8 = lax.broadcasted_iota(jnp.int32, prev8.shape, 0)
    head = jnp.where(row8 < s, pltpu.roll(prev8, s, 0), rolled[0:8, :])
    return jnp.concatenate([head, rolled[8:, :]], axis=0)


def _shift_up(xv, s, next8):
    n = xv.shape[0]
    rolled = pltpu.roll(xv, n - s, 0)
    row8 = lax.broadcasted_iota(jnp.int32, next8.shape, 0)
    tail = jnp.where(row8 >= 8 - s, pltpu.roll(next8, 8 - s, 0), rolled[n - 8:, :])
    return jnp.concatenate([rolled[:n - 8, :], tail], axis=0)


def _scan_fwd(a, b):
    n = a.shape[0]
    sub = lax.broadcasted_iota(jnp.int32, a.shape, 0) % SUBLANES
    s = 1
    while s < SUBLANES:
        ok = sub >= s
        b = jnp.where(ok, a * pltpu.roll(b, s, 0) + b, b)
        a = jnp.where(ok, a * pltpu.roll(a, s, 0), a)
        s *= 2
    groups = []
    before = jnp.zeros((1, a.shape[1]), F32)
    for g in range(n // SUBLANES):
        rows = slice(g * SUBLANES, (g + 1) * SUBLANES)
        groups.append(a[rows] * before + b[rows])
        before = groups[-1][SUBLANES - 1:]
    return jnp.concatenate(groups, axis=0)


def _scan_bwd(a, b):
    n = a.shape[0]
    sub = lax.broadcasted_iota(jnp.int32, a.shape, 0) % SUBLANES
    s = 1
    while s < SUBLANES:
        ok = sub < SUBLANES - s
        b = jnp.where(ok, a * pltpu.roll(b, n - s, 0) + b, b)
        a = jnp.where(ok, a * pltpu.roll(a, n - s, 0), a)
        s *= 2
    groups = []
    after = jnp.zeros((1, a.shape[1]), F32)
    for g in reversed(range(n // SUBLANES)):
        rows = slice(g * SUBLANES, (g + 1) * SUBLANES)
        groups.append(a[rows] * after + b[rows])
        after = groups[-1][:1]
    return jnp.concatenate(groups[::-1], axis=0)


def _rglru_gates(xv, prev8, cw_ref, cb_ref, wa_ref, ba_ref, wx_ref, bx_ref, lam_ref):
    x1 = _shift_down(xv, 1, prev8)
    x2 = _shift_down(xv, 2, prev8)
    x3 = _shift_down(xv, 3, prev8)
    xc = cw_ref[3:4, :] * xv + cw_ref[2:3, :] * x1 + cw_ref[1:2, :] * x2 + cw_ref[0:1, :] * x3 + cb_ref[...]
    xcb = xc.astype(BF16)
    r = _sigmoid(_dot(xcb, wa_ref[...]) + ba_ref[...])
    ig = _sigmoid(_dot(xcb, wx_ref[...]) + bx_ref[...])
    c = RG_C * _log_sigmoid(lam_ref[...])
    la = r * c
    a = jnp.exp(la)
    m = jnp.sqrt(-_expm1_neg(2.0 * la))
    return (x1, x2, x3), xc, xcb, r, ig, c, a, m


def _rglru_fwd(proj, cw, cb, wa, ba, wx, bx, lam, rider=None):
    s = proj.shape[0]
    w = D_RNN
    tm = _tile(s)
    ni = s // tm

    def body(*refs):
        ins, (y_ref, h_ref), (prev, hlast), copies = _split_refs(refs, 9, 2, rider)
        xr_ref, gate_ref, cw_ref, cb_ref, wa_ref, ba_ref, wx_ref, bx_ref, lam_ref = ins
        finish = _ride(copies, pl.program_id(0) == 0, pl.program_id(0) == ni - 1)

        @pl.when(pl.program_id(0) == 0)
        def _():
            prev[...] = jnp.zeros_like(prev)
            hlast[...] = jnp.zeros_like(hlast)

        xv = xr_ref[...]
        _, xc, _, _, ig, _, a, m = _rglru_gates(xv, prev[...], cw_ref, cb_ref, wa_ref, ba_ref,
                                                wx_ref, bx_ref, lam_ref)
        b = m * (ig * xc)
        row = lax.broadcasted_iota(jnp.int32, b.shape, 0)
        b = jnp.where(row == 0, b + a * hlast[...], b)
        h = _scan_fwd(a, b)
        h_ref[...] = h
        y_ref[...] = h * _gelu(gate_ref[...])
        prev[...] = xv[tm - 8:, :]
        hlast[...] = h[tm - 1:tm, :]
        finish()

    vec = pl.BlockSpec((1, w), lambda i: (0, 0))
    sq = pl.BlockSpec((w, w), lambda i: (0, 0))
    out = pl.BlockSpec((tm, w), lambda i: (i, 0))
    return _call(
        body, "rglru_fwd", (ni,),
        [pl.BlockSpec((tm, w), lambda i: (i, 0)), pl.BlockSpec((tm, w), lambda i: (i, 1)),
         pl.BlockSpec((CONV_W, w), lambda i: (0, 0)), vec, sq, vec, sq, vec, vec], [out, out],
        [jax.ShapeDtypeStruct((s, w), F32), jax.ShapeDtypeStruct((s, w), F32)],
        [proj, proj, cw, cb, wa, ba, wx, bx, lam],
        scratch=[pltpu.VMEM((8, w), F32), pltpu.VMEM((1, w), F32)], rider=rider)


def _rglru_bwd(proj, hseq, dyr, cw, cb, wa, ba, wx, bx, lam):
    s = proj.shape[0]
    w = D_RNN
    tm = _tile(s)
    nt = s // tm
    t8 = tm // 8

    def body(xr_ref, xp_ref, gate_ref, h_ref, hp_ref, dy_ref, cw_ref, cb_ref, wa_ref, ba_ref,
             wx_ref, bx_ref, lam_ref,
             dxr_ref, dgate_ref, dcw_ref, dcb_ref, dwa_ref, dba_ref, dwx_ref, dbx_ref, dlam_ref,
             carry, dxc_next):
        i = pl.program_id(0)
        first_tile = i == nt - 1

        @pl.when(i == 0)
        def _():
            carry[...] = jnp.zeros_like(carry)
            dxc_next[...] = jnp.zeros_like(dxc_next)
            for ref in (dcw_ref, dcb_ref, dwa_ref, dba_ref, dwx_ref, dbx_ref, dlam_ref):
                ref[...] = jnp.zeros_like(ref)

        xv = xr_ref[...]
        prev8 = jnp.where(first_tile, 0.0, xp_ref[...])
        hprev8 = jnp.where(first_tile, 0.0, hp_ref[...])
        (x1, x2, x3), xc, xcb, r, ig, c, a, m = _rglru_gates(
            xv, prev8, cw_ref, cb_ref, wa_ref, ba_ref, wx_ref, bx_ref, lam_ref)
        gv = gate_ref[...]
        hv = h_ref[...]
        dy = dy_ref[...]
        dgate_ref[...] = (dy * hv * _gelu_grad(gv)).astype(BF16)
        dh = dy * _gelu(gv)
        row = lax.broadcasted_iota(jnp.int32, dh.shape, 0)
        dh = jnp.where(row == tm - 1, dh + carry[...], dh)
        a_up = jnp.where(row == tm - 1, 0.0, pltpu.roll(a, tm - 1, 0))
        lam_t = _scan_bwd(a_up, dh)
        carry[...] = a[0:1, :] * lam_t[0:1, :]
        hm1 = _shift_down(hv, 1, hprev8)
        da = lam_t * hm1
        ixc = ig * xc
        dm = lam_t * ixc
        dig = lam_t * m * xc
        dxc = lam_t * m * ig
        dla = da * a - dm * (a * a) / m
        dr = dla * c
        dlam_ref[...] += jnp.sum(dla * r, axis=0, keepdims=True)
        dpa = dr * r * (1.0 - r)
        dpi = dig * ig * (1.0 - ig)
        dba_ref[...] += jnp.sum(dpa, axis=0, keepdims=True)
        dbx_ref[...] += jnp.sum(dpi, axis=0, keepdims=True)
        dpab = dpa.astype(BF16)
        dpib = dpi.astype(BF16)
        dwa_ref[...] += _dot_tn(xcb, dpab)
        dwx_ref[...] += _dot_tn(xcb, dpib)
        dxc = dxc + _dot_nt(dpab, wa_ref[...]) + _dot_nt(dpib, wx_ref[...])
        dcb_ref[...] += jnp.sum(dxc, axis=0, keepdims=True)
        dcw_ref[3:4, :] += jnp.sum(dxc * xv, axis=0, keepdims=True)
        dcw_ref[2:3, :] += jnp.sum(dxc * x1, axis=0, keepdims=True)
        dcw_ref[1:2, :] += jnp.sum(dxc * x2, axis=0, keepdims=True)
        dcw_ref[0:1, :] += jnp.sum(dxc * x3, axis=0, keepdims=True)
        nxt = dxc_next[...]
        dxr = (cw_ref[3:4, :] * dxc + cw_ref[2:3, :] * _shift_up(dxc, 1, nxt)
               + cw_ref[1:2, :] * _shift_up(dxc, 2, nxt) + cw_ref[0:1, :] * _shift_up(dxc, 3, nxt))
        dxr_ref[...] = dxr.astype(BF16)
        dxc_next[...] = dxc[0:8, :]

        @pl.when(first_tile)
        def _():
            lv = lam_ref[...]
            dlam_ref[...] = dlam_ref[...] * (RG_C * _sigmoid(-lv))

    rev = lambda i: nt - 1 - i
    vec = pl.BlockSpec((1, w), lambda i: (0, 0))
    sq = pl.BlockSpec((w, w), lambda i: (0, 0))
    cur = lambda col: pl.BlockSpec((tm, w), lambda i: (rev(i), col))
    before = lambda cols: pl.BlockSpec((8, w), lambda i: (jnp.maximum(rev(i) * t8 - 1, 0), 0))
    return pl.pallas_call(
        body, name="rglru_bwd", grid=(nt,),
        in_specs=[cur(0), before(None), cur(1), cur(0), before(None), cur(0),
                  pl.BlockSpec((CONV_W, w), lambda i: (0, 0)), vec, sq, vec, sq, vec, vec],
        out_specs=[cur(0), cur(0), pl.BlockSpec((CONV_W, w), lambda i: (0, 0)), vec, sq, vec, sq, vec, vec],
        out_shape=[jax.ShapeDtypeStruct((s, w), BF16), jax.ShapeDtypeStruct((s, w), BF16),
                   jax.ShapeDtypeStruct((CONV_W, w), F32), jax.ShapeDtypeStruct((1, w), F32),
                   jax.ShapeDtypeStruct((w, w), F32), jax.ShapeDtypeStruct((1, w), F32),
                   jax.ShapeDtypeStruct((w, w), F32), jax.ShapeDtypeStruct((1, w), F32),
                   jax.ShapeDtypeStruct((1, w), F32)],
        scratch_shapes=[pltpu.VMEM((1, w), F32), pltpu.VMEM((8, w), F32)],
        compiler_params=_params(("arbitrary",)),
    )(proj, proj, proj, hseq, hseq, dyr, cw, cb, wa, ba, wx, bx, lam)


def _sb_logs(z, valid):
    lb = jnp.minimum(z, 0.0) - jnp.log(1.0 + jnp.exp(-jnp.abs(z)))
    return lb, jnp.where(valid, lb - z, 0.0)


class _Window:
    def __init__(self):
        blk, win, cut = ATT_BLOCK, ATT_WINDOW, ATT_SPLIT
        self.row = lax.broadcasted_iota(jnp.int32, (blk, win), 0)
        self.col = lax.broadcasted_iota(jnp.int32, (blk, win), 1)

        def tri(n, later):
            j = lax.broadcasted_iota(jnp.int32, (n, n), 0)
            s = lax.broadcasted_iota(jnp.int32, (n, n), 1)
            return jnp.where((j > s) if later else (j < s), 1.0, 0.0).astype(BF16)

        self.later = (tri(cut, True), tri(win - cut, True))
        self.earlier = (tri(cut, False), tri(win - cut, False))

    def place(self, qi, g):
        end = (qi + 1) * ATT_BLOCK - g * ATT_WINDOW
        start = pl.multiple_of(jnp.maximum(end - ATT_WINDOW, 0), ATT_BLOCK)
        valid = self.col < jnp.minimum(self.row + (qi * ATT_BLOCK - start), end - start)
        return start, valid

    @staticmethod
    def _parts(xv):
        hi = xv.astype(BF16)
        lo = (xv - hi.astype(F32)).astype(BF16)
        cut = ATT_SPLIT
        sums = (jnp.sum(xv[:, :cut], axis=1, keepdims=True), jnp.sum(xv[:, cut:], axis=1, keepdims=True))
        return (hi[:, :cut], lo[:, :cut]), (hi[:, cut:], lo[:, cut:]), sums

    def sums_after(self, xv, carry):
        (h0, l0), (h1, l1), (s0, s1) = self._parts(xv)
        first = _dot(h0, self.later[0]) + _dot(l0, self.later[0]) + (s1 + carry)
        last = _dot(h1, self.later[1]) + _dot(l1, self.later[1]) + carry
        return jnp.concatenate([first, last], axis=1), s0 + s1

    def sums_before(self, xv, carry):
        (h0, l0), (h1, l1), (s0, s1) = self._parts(xv)
        first = _dot(h0, self.earlier[0]) + _dot(l0, self.earlier[0]) + carry
        last = _dot(h1, self.earlier[1]) + _dot(l1, self.earlier[1]) + (s0 + carry)
        return jnp.concatenate([first, last], axis=1), s0 + s1


class _HeadPair:
    def __init__(self):
        lanes = 2 * HEAD_DIM
        lane = lax.broadcasted_iota(jnp.int32, (1, lanes), 1)
        self.masks = [lane // HEAD_DIM == h for h in (0, 1)]
        i = lax.broadcasted_iota(jnp.int32, (lanes, lanes), 0) // HEAD_DIM
        j = lax.broadcasted_iota(jnp.int32, (lanes, lanes), 1) // HEAD_DIM
        self.same_head = jnp.where(i == j, 1.0, 0.0).astype(BF16)

    def only(self, h, xv):
        return jnp.where(self.masks[h], xv, jnp.zeros_like(xv))

    def merge(self, per_head):
        return jnp.where(self.masks[0], per_head[0], per_head[1])

    def mean(self, xv):
        hi = xv.astype(BF16)
        lo = (xv - hi.astype(F32)).astype(BF16)
        return (_dot(hi, self.same_head) + _dot(lo, self.same_head)) * (1.0 / HEAD_DIM)

    def rms_r(self, xv):
        return lax.rsqrt(self.mean(xv * xv) + EPS)

    def rms_bwd(self, xv, r, nw, dh):
        t = dh * nw
        dx = r * t - xv * (r * r * r * self.mean(t * xv))
        dn = jnp.sum(dh * xv * r, axis=0, keepdims=True)
        return dx, dn[:, :HEAD_DIM] + dn[:, HEAD_DIM:]


def _attn_fwd(proj, qg, kg, rider=None):
    s = proj.shape[0]
    blk, win, dh = ATT_BLOCK, ATT_WINDOW, HEAD_DIM
    nq = s // blk
    scale = 1.0 / math.sqrt(dh)
    heads = (0, 1)
    blocks = (0, 1)
    assert s >= win and s % (blk * len(blocks)) == 0

    def body(*refs):
        (q_ref, k_ref, v_ref, qg_ref, kg_ref), (o_ref,), (qn, kn, vb), copies = _split_refs(refs, 5, 1, rider)
        finish = _ride(copies, pl.program_id(0) == 0, pl.program_id(0) == N_HEADS // 2 - 1)
        wd, hp = _Window(), _HeadPair()
        qv = q_ref[...]
        qn[...] = (qv * hp.rms_r(qv) * qg_ref[...] * scale).astype(BF16)
        kv = k_ref[...]
        kn[...] = (kv * hp.rms_r(kv) * kg_ref[...]).astype(BF16)
        vb[...] = v_ref[...].astype(BF16)

        def q_step(pair_i, _):
            qis = [2 * pair_i + b for b in blocks]
            chains = [(b, h) for b in blocks for h in heads]
            qoffs = [pl.multiple_of(qi * blk, blk) for qi in qis]
            qtiles = [qn[pl.ds(qoff, blk), :] for qoff in qoffs]
            qts = [hp.only(h, qtiles[b]) for b, h in chains]

            def more(carry):
                g, live = carry[:2]
                return jnp.logical_and((qis[-1] + 1) * blk - g * win > 0, live > 0)

            def window(carry):
                g, _, accs, runs = carry
                places = [wd.place(qi, g) for qi in qis]
                kts = [kn[pl.ds(start, win), :] for start, _ in places]
                zs = [_dot_nt(qts[c], kts[b]) for c, (b, h) in enumerate(chains)]
                logs = [_sb_logs(zs[c], places[b][1]) for c, (b, h) in enumerate(chains)]
                sums = [wd.sums_after(logs[c][1], runs[c]) for c in range(len(chains))]
                wgts = [jnp.where(places[b][1], jnp.exp(logs[c][0] + sums[c][0]), 0.0).astype(BF16)
                        for c, (b, h) in enumerate(chains)]
                vts = [vb[pl.ds(start, win), :] for start, _ in places]
                accs = tuple(accs[c] + _dot(wgts[c], vts[b]) for c, (b, h) in enumerate(chains))
                runs = tuple(runs[c] + sums[c][1] for c in range(len(chains)))
                top = functools.reduce(jnp.maximum, [jnp.max(r) for r in runs])
                return g + 1, (top > EXP_ZERO).astype(jnp.int32), accs, runs

            zero = lambda cols: tuple(jnp.zeros((blk, cols), F32) for _ in chains)
            _, _, accs, _ = lax.while_loop(more, window, (jnp.int32(0), jnp.int32(1), zero(2 * dh), zero(1)))
            for b in blocks:
                o_ref[pl.ds(qoffs[b], blk), :] = hp.merge([accs[2 * b + h] for h in heads])
            return 0

        lax.fori_loop(0, nq // len(blocks), q_step, 0)
        finish()

    pair = lambda group: pl.BlockSpec((s, 2 * dh), lambda p: (0, group * (D_ATT // (2 * dh)) + p))
    vec = pl.BlockSpec((1, 2 * dh), lambda p: (0, 0))
    return _call(
        body, "attn_fwd", (N_HEADS // 2,), [pair(2), pair(3), pair(4), vec, vec], [pair(0)],
        [jax.ShapeDtypeStruct((s, D_ATT), F32)], [proj, proj, proj, jnp.tile(qg, (1, 2)), jnp.tile(kg, (1, 2))],
        scratch=[pltpu.VMEM((s, 2 * dh), BF16)] * 3, rider=rider)


def _attn_bwd(proj, dya, qg, kg, rider=None):
    s = proj.shape[0]
    blk, win, dh = ATT_BLOCK, ATT_WINDOW, HEAD_DIM
    nq = s // blk
    max_windows = -(-s // win) + 1
    scale = 1.0 / math.sqrt(dh)
    steps = N_HEADS // 2
    heads = (0, 1)
    blocks = (0, 1)
    assert s >= win and s % (blk * len(blocks)) == 0

    def body(*refs):
        ins, outs, scratch, copies = _split_refs(refs, 6, 5, rider)
        q_ref, k_ref, v_ref, do_ref, qg_ref, kg_ref = ins
        dq_ref, dk_ref, dv_ref, dqg_ref, dkg_ref = outs
        qn, kn, vb, dob, runs_ref, dqn, dkn, dvn = scratch
        finish = _ride(copies, pl.program_id(0) == 0, pl.program_id(0) == steps - 1)
        wd, hp = _Window(), _HeadPair()

        @pl.when(pl.program_id(0) == 0)
        def _():
            dqg_ref[...] = jnp.zeros_like(dqg_ref)
            dkg_ref[...] = jnp.zeros_like(dkg_ref)

        qv = q_ref[...]
        qn[...] = (qv * hp.rms_r(qv) * qg_ref[...] * scale).astype(BF16)
        kv = k_ref[...]
        kn[...] = (kv * hp.rms_r(kv) * kg_ref[...]).astype(BF16)
        vb[...] = v_ref[...].astype(BF16)
        dob[...] = do_ref[...].astype(BF16)
        dkn[...] = jnp.zeros_like(dkn)
        dvn[...] = jnp.zeros_like(dvn)

        def q_step(pair_i, _):
            qis = [2 * pair_i + b for b in blocks]
            chains = [(b, h) for b in blocks for h in heads]
            ids = range(len(chains))
            qoffs = [pl.multiple_of(qi * blk, blk) for qi in qis]
            qts = [hp.only(h, qn[pl.ds(qoffs[b], blk), :]) for b, h in chains]
            dots = [hp.only(h, dob[pl.ds(qoffs[b], blk), :]) for b, h in chains]

            zero = lambda cols: tuple(jnp.zeros((blk, cols), F32) for _ in chains)

            def logs_of(g):
                places = [wd.place(qi, g) for qi in qis]
                kts = [kn[pl.ds(start, win), :] for start, _ in places]
                return [_sb_logs(_dot_nt(qts[c], kts[b]), places[b][1]) for c, (b, h) in enumerate(chains)]

            def row_sums(logs):
                return tuple(jnp.sum(logs[c][1], axis=1, keepdims=True) for c in ids)

            def still_live(runs):
                return functools.reduce(jnp.maximum, [jnp.max(r) for r in runs]) > EXP_ZERO

            def window_grads(g, logs, runs, esums):
                places = [wd.place(qi, g) for qi in qis]
                kts = [kn[pl.ds(start, win), :] for start, _ in places]
                vts = [vb[pl.ds(start, win), :] for start, _ in places]
                dws = [_dot_nt(dots[c], vts[b]) for c, (b, h) in enumerate(chains)]
                tails = [wd.sums_after(logs[c][1], runs[c])[0] for c in ids]
                wgts = [jnp.where(places[b][1], jnp.exp(logs[c][0] + tails[c]), 0.0) for c, (b, h) in enumerate(chains)]
                es = [dws[c] * wgts[c] for c in ids]
                befores = [wd.sums_before(es[c], esums[c]) for c in ids]
                dzbs = []
                for c, (b, h) in enumerate(chains):
                    beta = jnp.exp(logs[c][0])
                    dz = jnp.where(places[b][1], es[c] * (1.0 - beta) - befores[c][0] * beta, 0.0)
                    dzbs.append(dz.astype(BF16))
                for b in blocks:
                    rows = pl.ds(places[b][0], win)
                    dkn[rows, :] += _dot_tn(dzbs[2 * b], qts[2 * b]) + _dot_tn(dzbs[2 * b + 1], qts[2 * b + 1])
                    dvn[rows, :] += (_dot_tn(wgts[2 * b].astype(BF16), dots[2 * b])
                                     + _dot_tn(wgts[2 * b + 1].astype(BF16), dots[2 * b + 1]))
                return (tuple(_dot(dzbs[c], kts[b]) for c, (b, h) in enumerate(chains)),
                        tuple(befores[c][1] for c in ids))

            logs0 = logs_of(0)
            runs1 = row_sums(logs0)

            def one_window():
                return window_grads(0, logs0, zero(1), zero(1))[0]

            def all_windows():
                def more(carry):
                    g, live = carry[:2]
                    return jnp.logical_and((qis[-1] + 1) * blk - g * win > 0, live > 0)

                def run_window(carry):
                    g, _, runs = carry
                    for c in ids:
                        runs_ref[c, g] = runs[c]
                    sums = row_sums(logs_of(g))
                    runs = tuple(runs[c] + sums[c] for c in ids)
                    return g + 1, still_live(runs).astype(jnp.int32), runs

                for c in ids:
                    runs_ref[c, 0] = jnp.zeros((blk, 1), F32)
                windows, _, _ = lax.while_loop(more, run_window, (jnp.int32(1), jnp.int32(1), runs1))

                def k_window(gg, carry):
                    dq_accs, esums = carry
                    g = windows - 1 - gg
                    parts, totals = window_grads(g, logs_of(g), [runs_ref[c, g] for c in ids], esums)
                    return (tuple(dq_accs[c] + parts[c] for c in ids), tuple(esums[c] + totals[c] for c in ids))

                return lax.fori_loop(0, windows, k_window, (zero(2 * dh), zero(1)))[0]

            earlier_keys = (qis[-1] + 1) * blk - win > 0
            dq_accs = lax.cond(jnp.logical_and(earlier_keys, still_live(runs1)), all_windows, one_window)
            for b in blocks:
                dqn[pl.ds(qoffs[b], blk), :] = hp.merge([dq_accs[2 * b + h] for h in heads])
            return 0

        lax.fori_loop(0, nq // len(blocks), q_step, 0)

        dq, dqg = hp.rms_bwd(qv, hp.rms_r(qv), qg_ref[...] * scale, dqn[...])
        dq_ref[...] = dq.astype(BF16)
        dqg_ref[...] += dqg * scale
        dk, dkg = hp.rms_bwd(kv, hp.rms_r(kv), kg_ref[...], dkn[...])
        dk_ref[...] = dk.astype(BF16)
        dkg_ref[...] += dkg
        dv_ref[...] = dvn[...].astype(BF16)
        finish()

    pair = lambda group: pl.BlockSpec((s, 2 * dh), lambda p: (0, group * (D_ATT // (2 * dh)) + p))
    vec2 = pl.BlockSpec((1, 2 * dh), lambda p: (0, 0))
    vec = pl.BlockSpec((1, dh), lambda p: (0, 0))
    return _call(
        body, "attn_bwd", (steps,), [pair(2), pair(3), pair(4), pair(0), vec2, vec2],
        [pair(0), pair(0), pair(0), vec, vec],
        [jax.ShapeDtypeStruct((s, D_ATT), BF16)] * 3 + [jax.ShapeDtypeStruct((1, dh), F32)] * 2,
        [proj, proj, proj, dya, jnp.tile(qg, (1, 2)), jnp.tile(kg, (1, 2))],
        scratch=[pltpu.VMEM((s, 2 * dh), BF16)] * 4 + [pltpu.VMEM((4, max_windows, blk, 1), F32)]
        + [pltpu.VMEM((s, 2 * dh), F32)] * 3, rider=rider)


def _block_diag(w):
    n, c, d = w.shape
    return jnp.einsum("ncd,nm->ncmd", w, jnp.eye(n, dtype=w.dtype)).reshape(n * c, n * d)


def _diag_blocks(full, n):
    c = full.shape[0] // n
    on_diagonal = jnp.eye(n, dtype=bool)[:, None, :, None]
    return jnp.sum(jnp.where(on_diagonal, full.reshape(n, c, n, c), 0.0), axis=2)


FFN1 = ["ffn1_w_gate", "ffn1_w_up", "ffn1_w_down"]
FFN2 = ["ffn2_w_gate", "ffn2_w_up", "ffn2_w_down"]


def _pair_sums(gb, names, where):
    theirs = _pair_exchange([gb[n] for n in names], "pair_exchange_" + names[0])
    pair, own = _pair_sum([gb[n] for n in names], theirs, where, "pair_sum_" + names[0])
    return _chip_rider(pair, own)


def _local_step(x, tgt, stacks, conv_stack, small, where):
    gate_up, down = FFN1[:2], FFN1[2:]
    big = dict(zip(gate_up, _gather_weights([stacks[n] for n in gate_up], [])))
    wa = _block_diag(small["rg_w_a"]).astype(BF16)
    wx = _block_diag(small["rg_w_x"]).astype(BF16)

    whole = lambda names: [big[n].reshape(-1, D_MODEL) for n in names]
    soon = down + ["w_in"]
    g1, u1, hb1, ab1, *landed = _ffn_up(x, small["ffn1_norm"], *whole(gate_up),
                                        rider=_gather_rider([stacks[n] for n in soon], [conv_stack]))
    big.update(zip(soon, landed))
    x1 = _ffn_down(x, ab1, *whole(down))
    conv_w = jnp.transpose(landed[-1], (1, 0, 2)).reshape(CONV_W, D_RNN)
    rg = (conv_w, small["conv_b"], wa, small["rg_b_a"], wx, small["rg_b_x"], small["rg_lambda"])
    riding = lambda names: _gather_rider([stacks[n] for n in names], [])
    proj, hb2, big["ffn2_w_gate"] = _mix_pre(x1, small["mix_norm"], big["w_in"], riding(["ffn2_w_gate"]))
    yr, hseq, big["ffn2_w_up"] = _rglru_fwd(proj, *rg, riding(["ffn2_w_up"]))
    ya, big["ffn2_w_down"], big["w_out"] = _attn_fwd(proj, small["q_norm"], small["k_norm"],
                                                     riding(["ffn2_w_down", "w_out"]))
    wout = big["w_out"].reshape(D_MODEL, D_MODEL)
    x2 = _mix_post(x1, yr, ya, small["rnn_out_norm"], small["attn_out_norm"], wout)
    dx3, g2, u2, hb3, ab3, loss = _ffn_fwd_loss(x2, small["ffn2_norm"], *whole(FFN2), tgt)

    gb, gs, slots = {}, {}, {}
    dx2, dg2, du2, dyb2, gs["ffn2_norm"] = _ffn_bwd_act(x2, small["ffn2_norm"], dx3, g2, u2, *whole(FFN2), "ffn2_bwd")
    gb["ffn2_w_gate"] = _ffn_wgrad(dg2, hb3, 1.0, "wgrad_gate_ffn2")
    gb["ffn2_w_up"] = _ffn_wgrad(du2, hb3, 1.0, "wgrad_up_ffn2")
    gb["ffn2_w_down"] = _ffn_wgrad(ab3, dyb2, 0.5, "wgrad_down_ffn2")
    dyr, dya, ycat, dxb2, gs["rnn_out_norm"], gs["attn_out_norm"] = _mix_post_bwd(
        dx2, yr, ya, small["rnn_out_norm"], small["attn_out_norm"], wout)
    gb["w_out"] = _wgrad_whole(ycat, [dxb2], False, "wgrad_out")
    early = FFN2 + ["w_out"]
    dq, dk, dv, gs["q_norm"], gs["k_norm"], *done = _attn_bwd(
        proj, dya, small["q_norm"], small["k_norm"], _pair_sums(gb, early, where))
    slots.update(zip(early, done))
    dxr, dgate, gs["conv_w"], gs["conv_b"], dwa, gs["rg_b_a"], dwx, gs["rg_b_x"], gs["rg_lambda"] = _rglru_bwd(
        proj, hseq, dyr, *rg)
    gs["rg_w_a"] = _diag_blocks(dwa, RNN_BLOCKS)
    gs["rg_w_x"] = _diag_blocks(dwx, RNN_BLOCKS)
    dps = [dxr, dgate, dq, dk, dv]
    dx1, gs["mix_norm"] = _mix_pre_bwd(x1, small["mix_norm"], dx2, dps, big["w_in"])
    dx0, dg1, du1, dyb1, gs["ffn1_norm"] = _ffn_bwd_act(x, small["ffn1_norm"], dx1, g1, u1, *whole(FFN1), "ffn1_bwd")

    mine = _place_shard(_pack([gs[n] for n in SMALL] + [loss[:, :1]]), where, F32, "place_small_grads",
                        by_device=True)
    gb["ffn1_w_gate"], everyone = _ffn_wgrad(dg1, hb1, 1.0, "wgrad_gate_ffn1", _small_rider(mine))
    gb["ffn1_w_up"], slots["ffn1_w_gate"] = _ffn_wgrad(
        du1, hb1, 1.0, "wgrad_up_ffn1", _pair_sums(gb, ["ffn1_w_gate"], where))
    gb["ffn1_w_down"], slots["ffn1_w_up"] = _ffn_wgrad(
        ab1, dyb1, 0.5, "wgrad_down_ffn1", _pair_sums(gb, ["ffn1_w_up"], where))
    gb["w_in"], slots["ffn1_w_down"] = _wgrad_whole(
        hb2, dps, True, "wgrad_in", _pair_sums(gb, ["ffn1_w_down"], where))
    last = _pair_sums(gb, ["w_in"], where)
    slots["w_in"], = _chip_exchange(last.plain, last.inplace)
    return dx0, slots, gs, everyone


ANY = pl.BlockSpec(memory_space=pl.ANY)


def _place():
    x, y, c = lax.axis_index("x"), lax.axis_index("y"), lax.axis_index("c")
    other_chips = [(1 - x, y), (x, 1 - y), (1 - x, 1 - y)]
    return x, y, c, 2 * x + y, other_chips


def _remote(src, dst, send_sem, recv_sem, to):
    return pltpu.make_async_remote_copy(src_ref=src, dst_ref=dst, send_sem=send_sem, recv_sem=recv_sem,
                                        device_id=to, device_id_type=MESH)


def _copy_plan(pairs):
    sends = [functools.partial(_remote, *a) for a, _ in pairs]
    arrivals = [functools.partial(_remote, *b) for _, b in pairs]
    return sends, arrivals


class _Rider:
    def __init__(self, plan, plain, inplace, n_copies=None, relay=None, n_relay=0):
        self.plan, self.plain, self.inplace = plan, list(plain), list(inplace)
        self.n_copies = n_copies or 3 * len(self.inplace)
        self.relay, self.n_relay = relay, n_relay

    def operands(self):
        return self.plain + self.inplace

    def out_shape(self):
        return [jax.ShapeDtypeStruct(a.shape, a.dtype) for a in self.inplace]

    def aliases(self, inputs_before, outputs_before):
        return {inputs_before + len(self.plain) + k: outputs_before + k for k in range(len(self.inplace))}

    def scratch(self):
        relay = [pltpu.SemaphoreType.DMA((self.n_relay,))] * 2 if self.relay else []
        return [pltpu.SemaphoreType.DMA((self.n_copies,))] * 2 + relay


def _split_refs(refs, n_in, n_out, rider):
    if rider is None:
        return refs[:n_in], refs[n_in:n_in + n_out], refs[n_in + n_out:], None
    r_in, r_out = len(rider.operands()), len(rider.inplace)
    outs_at = n_in + r_in
    n_sems = len(rider.scratch())
    rest = refs[outs_at + n_out + r_out:]
    sems = rest[len(rest) - n_sems:]
    filled = refs[outs_at + n_out:outs_at + n_out + r_out]
    copies = functools.partial(rider.plan, refs[n_in:n_in + len(rider.plain)], filled, *sems[:2])
    relay = functools.partial(rider.relay, filled, *sems[2:]) if rider.relay else None
    return refs[:n_in], refs[outs_at:outs_at + n_out], rest[:len(rest) - n_sems], (copies, relay)


def _ride(copies, first, last, middle=None):
    if copies is None:
        return lambda: None
    copies, relay = copies

    @pl.when(first)
    def _():
        _start(copies()[0])

    def start_relay():
        for make in copies()[1]:
            make().wait_recv()
        _start(relay()[0])

    if relay is not None and middle is not None:
        pl.when(middle)(start_relay)

    def finish():
        @pl.when(last)
        def _():
            if relay is None:
                _finish(*copies())
            else:
                if middle is None:
                    start_relay()
                _finish(copies()[0] + relay()[0], relay()[1])

    return finish


def _gather_rider(split, whole):
    n_split = len(split)
    return _Rider(lambda plain, stacks, ss, rs: _gather_ici(stacks, n_split, ss, rs), [], list(split) + list(whole),
                  relay=lambda stacks, ss, rs: _gather_d2d(stacks[:n_split], ss, rs), n_relay=3 * n_split)


def _chip_rider(sums, slots):
    return _Rider(_chip_copies, sums, slots)


def _start(makers):
    for make in makers:
        make().start()


def _finish(sends, arrivals):
    for make in arrivals:
        make().wait_recv()
    for make in sends:
        make().wait_send()


def _half(rows, c):
    return pl.ds(pl.multiple_of(c * rows, BF16_ROWS), rows)


def _gather_weights(split, whole):
    arrs = list(split) + list(whole)
    n, ns = len(arrs), len(split)

    def body(*refs):
        outs = refs[n:2 * n]
        send_sems, recv_sems, fsend_sems, frecv_sems = refs[2 * n:]
        sends, arrivals = _gather_ici(outs, ns, send_sems, recv_sems)
        passes, passed = _gather_d2d(outs[:ns], fsend_sems, frecv_sems)
        _start(sends)
        for k, make in enumerate(arrivals):
            make().wait_recv()
            if k < 3 * ns:
                passes[k]().start()
        _finish(sends + passes, passed)

    return pl.pallas_call(
        body, name="gather_weights",
        in_specs=[ANY] * n, out_specs=[ANY] * n,
        out_shape=[jax.ShapeDtypeStruct(a.shape, a.dtype) for a in arrs],
        input_output_aliases={i: i for i in range(n)},
        scratch_shapes=[pltpu.SemaphoreType.DMA((3 * n,)), pltpu.SemaphoreType.DMA((3 * n,)),
                        pltpu.SemaphoreType.DMA((3 * ns,)), pltpu.SemaphoreType.DMA((3 * ns,))],
    )(*arrs)


def _gather_ici(stacks, n_split, send_sems, recv_sems):
    x, y, c, me, chips = _place()

    def region(i, chip):
        if i < n_split:
            return stacks[i].at[chip, _half(stacks[i].shape[1] // 2, c)]
        return stacks[i].at[chip]

    pairs = []
    for i in range(len(stacks)):
        for p, (cx, cy) in enumerate(chips):
            k = 3 * i + p
            mine, got = region(i, me), region(i, 2 * cx + cy)
            sems, to = (send_sems.at[k], recv_sems.at[k]), (cx, cy, c)
            pairs.append(((mine, mine, *sems, to), (got, got, *sems, to)))
    return _copy_plan(pairs)


def _gather_d2d(stacks, send_sems, recv_sems):
    x, y, c, _, chips = _place()
    sibling = (x, y, 1 - c)
    pairs = []
    for i, stack in enumerate(stacks):
        rows = stack.shape[1] // 2
        for p, (cx, cy) in enumerate(chips):
            k = 3 * i + p
            got, theirs = stack.at[2 * cx + cy, _half(rows, c)], stack.at[2 * cx + cy, _half(rows, 1 - c)]
            sems = (send_sems.at[k], recv_sems.at[k])
            pairs.append(((got, got, *sems, sibling), (theirs, theirs, *sems, sibling)))
    return _copy_plan(pairs)


def _pair_exchange(grads, name):
    n = len(grads)

    def body(*refs):
        ins, theirs = refs[:n], refs[n:2 * n]
        send_sems, recv_sems = refs[2 * n:]
        x, y, c, _, _ = _place()
        sibling = (x, y, 1 - c)
        sends = [_remote(ins[k].at[:, _half(grads[k].shape[1] // 2, 1 - c)], theirs[k],
                         send_sems.at[k], recv_sems.at[k], sibling) for k in range(n)]
        for cp in sends:
            cp.start()
        for k in range(n):
            _remote(theirs[k], theirs[k], send_sems.at[k], recv_sems.at[k], sibling).wait_recv()
        for cp in sends:
            cp.wait_send()

    return pl.pallas_call(
        body, name=name,
        in_specs=[ANY] * n, out_specs=[ANY] * n,
        out_shape=[jax.ShapeDtypeStruct((g.shape[0], g.shape[1] // 2, g.shape[2]), g.dtype) for g in grads],
        scratch_shapes=[pltpu.SemaphoreType.DMA((n,))] * 2,
    )(*grads)


def _chip_exchange(sums, slots):
    n = len(sums)

    def body(*refs):
        sends, arrivals = _chip_copies(refs[:n], refs[2 * n:3 * n], *refs[3 * n:])
        _start(sends)
        _finish(sends, arrivals)

    return pl.pallas_call(
        body, name="grad_chip_exchange",
        in_specs=[ANY] * (2 * n), out_specs=[ANY] * n,
        out_shape=[jax.ShapeDtypeStruct(a.shape, a.dtype) for a in slots],
        input_output_aliases={n + k: k for k in range(n)},
        scratch_shapes=[pltpu.SemaphoreType.DMA((3 * n,)), pltpu.SemaphoreType.DMA((3 * n,))],
    )(*sums, *slots)


def _chip_copies(sums, slots, send_sems, recv_sems):
    x, y, c, me, chips = _place()
    pairs = []
    for k in range(len(sums)):
        for p, (cx, cy) in enumerate(chips):
            j = 3 * k + p
            got = slots[k].at[2 * cx + cy]
            sems, to = (send_sems.at[j], recv_sems.at[j]), (cx, cy, c)
            pairs.append(((sums[k].at[2 * cx + cy], slots[k].at[me], *sems, to), (got, got, *sems, to)))
    return _copy_plan(pairs)


def _half_swap(halves):
    n = len(halves)

    def body(*refs):
        outs = refs[n:2 * n]
        send_sems, recv_sems = refs[2 * n:]
        x, y, c, _, _ = _place()
        sibling = (x, y, 1 - c)
        sends = [_remote(outs[k].at[c], outs[k].at[c], send_sems.at[k], recv_sems.at[k], sibling) for k in range(n)]
        for cp in sends:
            cp.start()
        for k in range(n):
            got = outs[k].at[1 - c]
            _remote(got, got, send_sems.at[k], recv_sems.at[k], sibling).wait_recv()
        for cp in sends:
            cp.wait_send()

    return pl.pallas_call(
        body, name="grad_half_swap",
        in_specs=[ANY] * n, out_specs=[ANY] * n,
        out_shape=[jax.ShapeDtypeStruct(a.shape, a.dtype) for a in halves],
        input_output_aliases={k: k for k in range(n)},
        scratch_shapes=[pltpu.SemaphoreType.DMA((n,))] * 2,
    )(*halves)


def _small_rider(stack):
    n_dev = 2 * N_CHIPS

    def plan(_, stacks, send_sems, recv_sems):
        x, y, c, _, _ = _place()
        mine = stacks[0].at[4 * x + 2 * y + c]
        pairs = []
        for k in range(1, n_dev):
            px, py, pc = x ^ ((k >> 2) & 1), y ^ ((k >> 1) & 1), c ^ (k & 1)
            got = stacks[0].at[4 * px + 2 * py + pc]
            sems = (send_sems.at[k - 1], recv_sems.at[k - 1])
            pairs.append(((mine, mine, *sems, (px, py, pc)), (got, got, *sems, (px, py, pc))))
        return _copy_plan(pairs)

    return _Rider(plan, [], [stack], n_dev - 1)


def _row_tile(r):
    return r // 4 if r >= 256 and (r // 4) % BF16_ROWS == 0 else r


def _prefetch_call(body, name, grid, in_specs, out_specs, out_shape):
    spec = pltpu.PrefetchScalarGridSpec(num_scalar_prefetch=1, grid=grid, in_specs=in_specs, out_specs=out_specs)
    return pl.pallas_call(body, name=name, grid_spec=spec, out_shape=out_shape,
                          compiler_params=_params(("arbitrary",) * len(grid)))


def _place_shard(w2d, where, dtype, name, by_device=False):
    r, c = w2d.shape
    tr = _row_tile(r)
    slots = 2 * N_CHIPS if by_device else N_CHIPS
    slot = (lambda s: 2 * s[1] + s[0]) if by_device else (lambda s: s[1])

    def body(where_ref, w_ref, out_ref):
        out_ref[...] = w_ref[...].astype(dtype)

    return _prefetch_call(
        body, name, (r // tr,), [pl.BlockSpec((tr, c), lambda i, s: (i, 0))],
        pl.BlockSpec((None, tr, c), lambda i, s: (slot(s), i, 0)),
        jax.ShapeDtypeStruct((slots, r, c), dtype))(where, w2d)


def _place_shards(w2ds, where, name):
    n = len(w2ds)
    steps = N_CHIPS
    assert all(w.shape[0] % (BF16_ROWS * steps) == 0 for w in w2ds)

    def body(where_ref, *refs):
        for k in range(n):
            refs[n + k][...] = refs[k][...].astype(BF16)

    tile = lambda w: (w.shape[0] // steps, w.shape[1])
    return _prefetch_call(
        body, name, (steps,), [pl.BlockSpec(tile(w), lambda i, s: (i, 0)) for w in w2ds],
        [pl.BlockSpec((None,) + tile(w), lambda i, s: (s[1], i, 0)) for w in w2ds],
        [jax.ShapeDtypeStruct((N_CHIPS,) + w.shape, BF16) for w in w2ds])(where, *w2ds)


def _pair_sum(fulls, theirs, where, name):
    n = len(fulls)

    def body(where_ref, *refs):
        for k in range(n):
            a_ref, b_ref, out_ref, own_ref = refs[k], refs[n + k], refs[2 * n + k], refs[3 * n + k]
            total = (a_ref[...].astype(F32) + b_ref[...].astype(F32)).astype(BF16)
            out_ref[...] = total

            @pl.when(pl.program_id(0) == where_ref[1])
            def _():
                own_ref[...] = total

    half = lambda t: pl.BlockSpec((None,) + t.shape[1:], lambda j, s: (j, s[0], 0))
    blk = lambda t: pl.BlockSpec((None,) + t.shape[1:], lambda j, s: (j, 0, 0))
    own = lambda t: pl.BlockSpec((None,) + t.shape[1:], lambda j, s: (s[1], 0, 0))
    shapes = [jax.ShapeDtypeStruct(t.shape, BF16) for t in theirs]
    outs = _prefetch_call(
        body, name, (N_CHIPS,), [half(t) for t in theirs] + [blk(t) for t in theirs],
        [blk(t) for t in theirs] + [own(t) for t in theirs], shapes + shapes)(where, *fulls, *theirs)
    return outs[:n], outs[n:]


def _chip_sum(slots, where, name):
    n = len(slots)
    steps = 2
    assert all(a.shape[1] % (BF16_ROWS * steps) == 0 for a in slots)

    def body(where_ref, *refs):
        for k in range(n):
            a_ref, out_ref = refs[k], refs[n + k]
            total = a_ref[0].astype(F32)
            for j in range(1, a_ref.shape[0]):
                total = total + a_ref[j].astype(F32)
            out_ref[...] = total

    tile = lambda a: (a.shape[1] // steps, a.shape[2])
    return _prefetch_call(
        body, name, (steps,), [pl.BlockSpec((a.shape[0],) + tile(a), lambda i, s: (0, i, 0)) for a in slots],
        [pl.BlockSpec((None,) + tile(a), lambda i, s: (s[0], i, 0)) for a in slots],
        [jax.ShapeDtypeStruct((2,) + a.shape[1:], F32) for a in slots])(where, *slots)


def _slot_sum(a, name):
    nb, r, c = a.shape
    tr = _row_tile(r)

    def body(a_ref, out_ref):
        total = a_ref[0].astype(F32)
        for j in range(1, nb):
            total = total + a_ref[j].astype(F32)
        out_ref[...] = total

    return pl.pallas_call(
        body, name=name, grid=(r // tr,),
        in_specs=[pl.BlockSpec((nb, tr, c), lambda i: (0, i, 0))],
        out_specs=pl.BlockSpec((tr, c), lambda i: (i, 0)),
        out_shape=jax.ShapeDtypeStruct((r, c), F32), compiler_params=_params(("arbitrary",)),
    )(a)


def _adamw(ws, gs, ms, vs, name, steps=1):
    n = len(ws)
    c1 = 1.0 - ADAM_B1 ** ADAM_STEP
    c2 = 1.0 - ADAM_B2 ** ADAM_STEP
    assert all(w.shape[0] % steps == 0 and (steps == 1 or w.shape[0] // steps % 8 == 0) for w in ws)

    def body(*refs):
        for k in range(n):
            w_ref, g_ref, m_ref, v_ref = (refs[j * n + k] for j in range(4))
            g_out, d_ref, m2_ref, v2_ref = (refs[(4 + j) * n + k] for j in range(4))
            gv = g_ref[...]
            g_out[...] = gv
            m2 = ADAM_B1 * m_ref[...] + (1.0 - ADAM_B1) * gv
            v2 = ADAM_B2 * v_ref[...] + (1.0 - ADAM_B2) * (gv * gv)
            m2_ref[...] = m2
            v2_ref[...] = v2
            d_ref[...] = -ADAM_LR * ((m2 / c1) / (jnp.sqrt(v2 / c2) + ADAM_EPS) + ADAM_WD * w_ref[...])

    blks = [pl.BlockSpec((w.shape[0] // steps, w.shape[1]), lambda i: (i, 0)) for w in ws]
    shapes = [jax.ShapeDtypeStruct(w.shape, F32) for w in ws]
    outs = pl.pallas_call(
        body, name=name, grid=(steps,), in_specs=blks * 4, out_specs=blks * 4, out_shape=shapes * 4,
        compiler_params=_params(("arbitrary",)),
    )(*ws, *gs, *ms, *vs)
    return [outs[j * n:(j + 1) * n] for j in range(4)]


WEIGHTS = ["ffn1_norm", "ffn1_w_gate", "ffn1_w_up", "ffn1_w_down", "mix_norm", "w_in", "conv_w", "conv_b",
           "rg_w_a", "rg_b_a", "rg_w_x", "rg_b_x", "rg_lambda", "q_norm", "k_norm", "rnn_out_norm",
           "attn_out_norm", "w_out", "ffn2_norm", "ffn2_w_gate", "ffn2_w_up", "ffn2_w_down"]
BIG = ["ffn1_w_gate", "ffn1_w_up", "ffn1_w_down", "w_in", "w_out", "ffn2_w_gate", "ffn2_w_up", "ffn2_w_down"]
SMALL = [n for n in WEIGHTS if n not in BIG]
PACK_LANES = 128
PACK_ROW_ALIGN = 8


def _hidden_major(name, a):
    return jnp.transpose(a) if name.endswith(("w_gate", "w_up")) else a


def _pack(parts):
    sizes = [math.prod(p.shape) for p in parts]
    unit = PACK_LANES * PACK_ROW_ALIGN
    padded = -(-sum(sizes) // unit) * unit
    flat, at = 0.0, 0
    for p, size in zip(parts, sizes):
        flat = flat + jnp.pad(p.reshape(-1), (at, padded - at - size))
        at += size
    return flat.reshape(-1, PACK_LANES)


def _unpack(packed, shapes):
    flat = packed.reshape(-1)
    out, at = [], 0
    for shp in shapes:
        size = math.prod(shp)
        out.append(flat[at:at + size].reshape(shp))
        at += size
    return out


def kernel(x, ffn1_norm, ffn1_w_gate, ffn1_w_up, ffn1_w_down, mix_norm, w_in, conv_w, conv_b, rg_w_a, rg_b_a, rg_w_x, rg_b_x, rg_lambda, q_norm, k_norm, rnn_out_norm, attn_out_norm, w_out, ffn2_norm, ffn2_w_gate, ffn2_w_up, ffn2_w_down, loss_target, m_ffn1_norm, m_ffn1_w_gate, m_ffn1_w_up, m_ffn1_w_down, m_mix_norm, m_w_in, m_conv_w, m_conv_b, m_rg_w_a, m_rg_b_a, m_rg_w_x, m_rg_b_x, m_rg_lambda, m_q_norm, m_k_norm, m_rnn_out_norm, m_attn_out_norm, m_w_out, m_ffn2_norm, m_ffn2_w_gate, m_ffn2_w_up, m_ffn2_w_down, v_ffn1_norm, v_ffn1_w_gate, v_ffn1_w_up, v_ffn1_w_down, v_mix_norm, v_w_in, v_conv_w, v_conv_b, v_rg_w_a, v_rg_b_a, v_rg_w_x, v_rg_b_x, v_rg_lambda, v_q_norm, v_k_norm, v_rnn_out_norm, v_attn_out_norm, v_w_out, v_ffn2_norm, v_ffn2_w_gate, v_ffn2_w_up, v_ffn2_w_down):
    given = dict(locals())
    w = {n: given[n] for n in WEIGHTS}
    m = {n: given["m_" + n] for n in WEIGHTS}
    v = {n: given["v_" + n] for n in WEIGHTS}
    chip = 2 * lax.axis_index("x") + lax.axis_index("y")

    where = jnp.stack([lax.axis_index("c"), chip]).astype(jnp.int32)

    stacks = dict(zip(BIG, _place_shards([_hidden_major(n, w[n][0]) for n in BIG], where, "place_weights")))
    conv_stack = _place_shard(w["conv_w"][0], where, F32, "place_conv_w")
    small = {n: (w[n][0] if w[n].ndim > 2 else w[n]) for n in SMALL if n != "conv_w"}

    grad_x, slots, gs, everyone = _local_step(x[0], loss_target[0], stacks, conv_stack, small, where)

    swapped = _half_swap(_chip_sum([slots[n] for n in BIG], where, "chip_sums"))
    g2s = [t.reshape(t.shape[0] * t.shape[1], t.shape[2]) for t in swapped]
    flat = lambda tree: [_hidden_major(n, tree[n][0]) for n in BIG]
    g2s, d2s, m2s, v2s = _adamw(flat(w), g2s, flat(m), flat(v), "adamw_weights", ADAMW_STEPS)
    grads, deltas, new_m, new_v = {}, {}, {}, {}
    for tree, parts in ((grads, g2s), (deltas, d2s), (new_m, m2s), (new_v, v2s)):
        tree.update({n: _hidden_major(n, a).reshape(w[n].shape) for n, a in zip(BIG, parts)})

    full_shapes = [gs[n].shape for n in SMALL]
    *summed, loss = _unpack(_slot_sum(everyone, "small_grad_sum"), full_shapes + [(1, 1)])
    g_parts = dict(zip(SMALL, summed))
    quarter = D_RNN // N_CHIPS
    g_parts["conv_w"] = lax.dynamic_slice_in_dim(g_parts["conv_w"], chip * quarter, quarter, axis=1)
    local_shapes = [w[n].shape for n in SMALL]
    pk = lambda tree: _pack([tree[n] for n in SMALL])
    (g_s,), (d_s,), (m_s,), (v_s,) = _adamw([pk(w)], [pk(g_parts)], [pk(m)], [pk(v)], "adamw_small")
    for tree, packed in ((grads, g_s), (deltas, d_s), (new_m, m_s), (new_v, v_s)):
        tree.update(zip(SMALL, _unpack(packed, local_shapes)))

    return (loss[0, 0], grad_x.reshape(x.shape), *[grads[n] for n in WEIGHTS], *[deltas[n] for n in WEIGHTS],
            *[new_m[n] for n in WEIGHTS], *[new_v[n] for n in WEIGHTS])
```

```python
import functools
import math

import jax
import jax.numpy as jnp
from jax import lax
from jax.experimental import pallas as pl
from jax.experimental.pallas import tpu as pltpu

F32 = jnp.float32
BF16 = jnp.bfloat16
MESH = pl.DeviceIdType.MESH

D_MODEL = 1024
N_CHIPS = 4
D_RNN = 512
D_ATT = 512
N_HEADS = 8
HEAD_DIM = 64
RNN_BLOCKS = 8
CONV_W = 4
RG_C = 8.0
N_IN = 2 * D_RNN + 3 * D_ATT
EPS = 1e-6
ATT_BLOCK = 128
ATT_WINDOW = 384
ATT_SPLIT = 256
EXP_ZERO = -105.0

ADAM_LR = 0.001
ADAM_B1 = 0.9
ADAM_B2 = 0.999
ADAM_EPS = 1e-08
ADAM_WD = 0.01
ADAM_STEP = 10

V7X_VMEM_LIMIT = 60 * 1024 * 1024
V7X_MXU_WIDTH = 256
TOKEN_TILE = 512
SUBLANES = 8
BF16_ROWS = 16
FFN_TILE = 256
WGRAD_TILE = 2048
WHOLE_TILE = 1024
ADAMW_STEPS = 8

GELU_K0 = math.sqrt(2.0 / math.pi)
GELU_K1 = 0.044715


def _params(sem=None):
    return pltpu.CompilerParams(dimension_semantics=sem, vmem_limit_bytes=V7X_VMEM_LIMIT)


def _dot(a, b):
    return jnp.dot(a, b, preferred_element_type=F32)


def _dot_nt(a, b):
    return lax.dot_general(a, b, (((1,), (1,)), ((), ())), preferred_element_type=F32)


def _dot_tn(a, b):
    return lax.dot_general(a, b, (((0,), (0,)), ((), ())), preferred_element_type=F32)


def _sigmoid(x):
    return 1.0 / (1.0 + jnp.exp(-x))


def _rms_r(xv):
    return lax.rsqrt(jnp.mean(xv * xv, axis=-1, keepdims=True) + EPS)


def _rms_bwd(xv, r, nw, dh):
    t = dh * nw
    dx = r * t - xv * (r * r * r * jnp.mean(t * xv, axis=-1, keepdims=True))
    dn = jnp.sum(dh * xv * r, axis=0, keepdims=True)
    return dx, dn


def _gelu(x):
    t = jnp.tanh(GELU_K0 * (x + GELU_K1 * x * x * x))
    return 0.5 * x * (1.0 + t)


def _gelu_grad(x):
    t = jnp.tanh(GELU_K0 * (x + GELU_K1 * x * x * x))
    return 0.5 * (1.0 + t) + 0.5 * x * (1.0 - t * t) * (GELU_K0 * (1.0 + 3.0 * GELU_K1 * x * x))


def _expm1_neg(x):
    p = 1.0 + x * (1.0 / 6.0)
    for k in (5.0, 4.0, 3.0, 2.0):
        p = 1.0 + x * (1.0 / k) * p
    return jnp.where(x > -0.25, x * p, jnp.exp(x) - 1.0)


def _log_sigmoid(x):
    return jnp.minimum(x, 0.0) - jnp.log(1.0 + jnp.exp(-jnp.abs(x)))


def _tile(s):
    return min(TOKEN_TILE, s)


def _ffn_chunks(f):
    cut = f // 2 // V7X_MXU_WIDTH * V7X_MXU_WIDTH
    return ((0, cut), (cut, f)) if 0 < cut < f else ((0, f),)


def _ffn_fwd_loss(x, nw, wg, wu, wd, tgt):
    s, d = x.shape
    f = wg.shape[0]
    tm = min(FFN_TILE, s)
    ni = s // tm
    assert s % tm == 0

    def body(x_ref, nw_ref, wg_ref, wu_ref, wd_ref, tgt_ref, out_ref, g_ref, u_ref, hb_ref, ab_ref, loss_ref):
        i = pl.program_id(0)
        xv = x_ref[...]
        hb = (xv * _rms_r(xv) * nw_ref[...]).astype(BF16)
        hb_ref[...] = hb
        y = jnp.zeros((tm, d), F32)
        for lo, hi in _ffn_chunks(f):
            g = _dot_nt(hb, wg_ref[lo:hi, :])
            u = _dot_nt(hb, wu_ref[lo:hi, :])
            g_ref[:, lo:hi] = g.astype(BF16)
            u_ref[:, lo:hi] = u.astype(BF16)
            ab = (g * _sigmoid(g) * u).astype(BF16)
            ab_ref[:, lo:hi] = ab
            y = y + _dot(ab, wd_ref[lo:hi, :])
        diff = xv + 0.5 * y - tgt_ref[...]
        out_ref[...] = diff * (1.0 / d)

        @pl.when(i == 0)
        def _():
            loss_ref[...] = jnp.zeros_like(loss_ref)

        loss_ref[...] += jnp.sum(diff * diff) * (0.5 / d)

    row = pl.BlockSpec((tm, d), lambda i: (i, 0))
    weight = pl.BlockSpec((f, d), lambda i: (0, 0), pipeline_mode=pl.Buffered(1))
    blk = pl.BlockSpec((tm, f), lambda i: (i, 0))
    wide = jax.ShapeDtypeStruct((s, f), BF16)
    return _call(body, "ffn_fwd_loss", (ni,),
                 [row, pl.BlockSpec((1, d), lambda i: (0, 0)), weight, weight, weight, row],
                 [row, blk, blk, row, blk, pl.BlockSpec((1, 128), lambda i: (0, 0))],
                 [jax.ShapeDtypeStruct((s, d), F32), wide, wide, jax.ShapeDtypeStruct((s, d), BF16), wide,
                  jax.ShapeDtypeStruct((1, 128), F32)], [x, nw, wg, wu, wd, tgt])


def _ffn_up(x, nw, wg, wu, rider=None):
    s, d = x.shape
    f = wg.shape[0]
    tm = min(FFN_TILE, s)
    ni = s // tm
    assert s % tm == 0

    def body(*refs):
        (x_ref, nw_ref, wg_ref, wu_ref), (g_ref, u_ref, hb_ref, ab_ref), _, copies = _split_refs(refs, 4, 4, rider)
        i = pl.program_id(0)
        finish = _ride(copies, i == 0, i == ni - 1)
        xv = x_ref[...]
        hb = (xv * _rms_r(xv) * nw_ref[...]).astype(BF16)
        hb_ref[...] = hb
        for lo, hi in _ffn_chunks(f):
            g = _dot_nt(hb, wg_ref[lo:hi, :])
            u = _dot_nt(hb, wu_ref[lo:hi, :])
            g_ref[:, lo:hi] = g.astype(BF16)
            u_ref[:, lo:hi] = u.astype(BF16)
            ab_ref[:, lo:hi] = (g * _sigmoid(g) * u).astype(BF16)
        finish()

    row = pl.BlockSpec((tm, d), lambda i: (i, 0))
    weight = pl.BlockSpec((f, d), lambda i: (0, 0), pipeline_mode=pl.Buffered(1))
    blk = pl.BlockSpec((tm, f), lambda i: (i, 0))
    wide = jax.ShapeDtypeStruct((s, f), BF16)
    return _call(body, "ffn_up", (ni,), [row, pl.BlockSpec((1, d), lambda i: (0, 0)), weight, weight],
                 [blk, blk, row, blk], [wide, wide, jax.ShapeDtypeStruct((s, d), BF16), wide], [x, nw, wg, wu],
                 rider=rider)


def _ffn_down(x, ab, wd):
    s, d = x.shape
    f = wd.shape[0]
    tm = _tile(s)
    assert s % tm == 0

    def body(x_ref, ab_ref, wd_ref, out_ref):
        out_ref[...] = x_ref[...] + 0.5 * _dot(ab_ref[...], wd_ref[...])

    row = pl.BlockSpec((tm, d), lambda i: (i, 0))
    return _call(body, "ffn_down", (s // tm,),
                 [row, pl.BlockSpec((tm, f), lambda i: (i, 0)),
                  pl.BlockSpec((f, d), lambda i: (0, 0), pipeline_mode=pl.Buffered(1))],
                 [row], [jax.ShapeDtypeStruct((s, d), F32)], [x, ab, wd])[0]


def _call(body, name, grid, in_specs, out_specs, out_shape, args, scratch=(), rider=None):
    in_specs, out_specs, out_shape, scratch = list(in_specs), list(out_specs), list(out_shape), list(scratch)
    extra, aliases = [], {}
    if rider is not None:
        extra = rider.operands()
        aliases = rider.aliases(len(args), len(out_shape))
        in_specs += [ANY] * len(extra)
        out_specs += [ANY] * len(rider.inplace)
        out_shape += rider.out_shape()
        scratch += rider.scratch()
    return pl.pallas_call(
        body, name=name, grid=grid, in_specs=in_specs, out_specs=out_specs, out_shape=out_shape,
        input_output_aliases=aliases, scratch_shapes=scratch,
        compiler_params=_params(("arbitrary",) * len(grid)),
    )(*args, *extra)


def _ffn_bwd_act(x, nw, dy, g, u, wg, wu, wd, name):
    s, d = x.shape
    f = wg.shape[0]
    tm = min(FFN_TILE, s)
    assert s % tm == 0

    def body(x_ref, nw_ref, dy_ref, g_ref, u_ref, wg_ref, wu_ref, wd_ref,
             dx_ref, dg_ref, du_ref, dyb_ref, dnw_ref):
        dyv = dy_ref[...]
        dyb = dyv.astype(BF16)
        dyb_ref[...] = dyb
        dh = jnp.zeros((tm, d), F32)
        for lo, hi in _ffn_chunks(f):
            da = 0.5 * _dot_nt(dyb, wd_ref[lo:hi, :])
            gv = g_ref[:, lo:hi].astype(F32)
            sg = _sigmoid(gv)
            dub = (da * (gv * sg)).astype(BF16)
            dgb = (da * u_ref[:, lo:hi].astype(F32) * (sg * (1.0 + gv * (1.0 - sg)))).astype(BF16)
            dg_ref[:, lo:hi] = dgb
            du_ref[:, lo:hi] = dub
            dh = dh + _dot(dgb, wg_ref[lo:hi, :]) + _dot(dub, wu_ref[lo:hi, :])
        xv = x_ref[...]
        dx, dn = _rms_bwd(xv, _rms_r(xv), nw_ref[...], dh)
        dx_ref[...] = dyv + dx

        @pl.when(pl.program_id(0) == 0)
        def _():
            dnw_ref[...] = jnp.zeros_like(dnw_ref)

        dnw_ref[...] += dn

    row = pl.BlockSpec((tm, d), lambda i: (i, 0))
    vec = pl.BlockSpec((1, d), lambda i: (0, 0))
    blk = pl.BlockSpec((tm, f), lambda i: (i, 0))
    weight = pl.BlockSpec((f, d), lambda i: (0, 0), pipeline_mode=pl.Buffered(1))
    return _call(
        body, name, (s // tm,), [row, vec, row, blk, blk, weight, weight, weight], [row, blk, blk, row, vec],
        [jax.ShapeDtypeStruct((s, d), F32), jax.ShapeDtypeStruct((s, f), BF16),
         jax.ShapeDtypeStruct((s, f), BF16), jax.ShapeDtypeStruct((s, d), BF16),
         jax.ShapeDtypeStruct((1, d), F32)],
        [x, nw, dy, g, u, wg, wu, wd])


def _wgrad(a, b, a_spec, b_spec, out_rows, out_cols, scale, name, tk, rider=None, per_step=1):
    s = a.shape[-2]
    nk = s // tk
    steps = N_CHIPS // per_step
    assert s % tk == 0

    def body(*refs):
        (a_ref, b_ref), (out_ref,), (acc,), copies = _split_refs(refs, 2, 1, rider)
        j, k = pl.program_id(0), pl.program_id(1)
        finish = _ride(copies, jnp.logical_and(j == 0, k == 0), jnp.logical_and(j == steps - 1, k == nk - 1))

        @pl.when(k == 0)
        def _():
            acc[...] = jnp.zeros_like(acc)

        acc[...] += _dot_tn(a_ref[...], b_ref[...])

        @pl.when(k == nk - 1)
        def _():
            for t in range(per_step):
                out_ref[t] = (acc[t * out_rows:(t + 1) * out_rows, :] * scale).astype(BF16)

        finish()

    outs = _call(
        body, name, (steps, nk), [a_spec(tk), b_spec(tk)],
        [pl.BlockSpec((per_step, out_rows, out_cols), lambda j, k: (j, 0, 0))],
        [jax.ShapeDtypeStruct((N_CHIPS, out_rows, out_cols), BF16)], [a, b],
        scratch=[pltpu.VMEM((per_step * out_rows, out_cols), F32)], rider=rider)
    return outs[0] if rider is None else outs


def _beside(refs):
    return jnp.concatenate([r[...] for r in refs], axis=1) if len(refs) > 1 else refs[0][...]


def _wgrad_whole(a, bs, col_blocks, name, rider=None):
    s, m = a.shape
    n = sum(b.shape[1] for b in bs)
    tk = min(WHOLE_TILE, s)
    nk = s // tk
    assert s % tk == 0
    out_shape = (N_CHIPS, m, n // N_CHIPS) if col_blocks else (N_CHIPS, m // N_CHIPS, n)

    def body(*refs):
        (a_ref, *b_refs), (out_ref,), (acc,), copies = _split_refs(refs, 1 + len(bs), 1, rider)
        k = pl.program_id(0)
        finish = _ride(copies, k == 0, k == nk - 1)

        @pl.when(k == 0)
        def _():
            acc[...] = jnp.zeros_like(acc)

        acc[...] += _dot_tn(a_ref[...], _beside(b_refs))

        @pl.when(k == nk - 1)
        def _():
            for j in range(N_CHIPS):
                if col_blocks:
                    out_ref[j] = acc[:, j * out_shape[2]:(j + 1) * out_shape[2]].astype(BF16)
                else:
                    out_ref[j] = acc[j * out_shape[1]:(j + 1) * out_shape[1], :].astype(BF16)

        finish()

    outs = _call(
        body, name, (nk,),
        [pl.BlockSpec((tk, m), lambda k: (k, 0))] + [pl.BlockSpec((tk, b.shape[1]), lambda k: (k, 0)) for b in bs],
        [pl.BlockSpec(out_shape, lambda k: (0, 0, 0))], [jax.ShapeDtypeStruct(out_shape, BF16)], [a, *bs],
        scratch=[pltpu.VMEM((m, n), F32)], rider=rider)
    return outs[0] if rider is None else outs


def _ffn_wgrad(hidden, shared, scale, name, rider=None):
    s, d = shared.shape
    half = hidden.shape[1] // 2
    return _wgrad(hidden, shared, lambda tk: pl.BlockSpec((tk, half), lambda j, k: (k, j)),
                  lambda tk: pl.BlockSpec((tk, d), lambda j, k: (k, 0)), half // 2, d, scale, name,
                  min(WGRAD_TILE, s), rider, per_step=2)


def _mix_pre(x, nw, win, rider=None):
    s, d = x.shape
    nb, _, cb = win.shape
    tm = _tile(s)
    ni = s // tm
    assert s % tm == 0

    def body(*refs):
        (x_ref, nw_ref, w_ref), (p_ref, hb_ref), (w_all,), copies = _split_refs(refs, 3, 2, rider)
        finish = _ride(copies, pl.program_id(0) == 0, pl.program_id(0) == ni - 1)

        @pl.when(pl.program_id(0) == 0)
        def _():
            for j in range(nb):
                w_all[:, j * cb:(j + 1) * cb] = w_ref[j]

        xv = x_ref[...]
        hb = (xv * _rms_r(xv) * nw_ref[...]).astype(BF16)
        hb_ref[...] = hb
        p_ref[...] = _dot(hb, w_all[...])
        finish()

    row = pl.BlockSpec((tm, d), lambda i: (i, 0))
    return _call(
        body, "mix_pre", (ni,),
        [row, pl.BlockSpec((1, d), lambda i: (0, 0)),
         pl.BlockSpec((nb, d, cb), lambda i: (0, 0, 0), pipeline_mode=pl.Buffered(1))],
        [pl.BlockSpec((tm, nb * cb), lambda i: (i, 0)), row],
        [jax.ShapeDtypeStruct((s, nb * cb), F32), jax.ShapeDtypeStruct((s, d), BF16)], [x, nw, win],
        scratch=[pltpu.VMEM((d, nb * cb), BF16)], rider=rider)


def _mix_pre_bwd(x, nw, dres, dps, win):
    s, d = x.shape
    nb, _, cb = win.shape
    tm = _tile(s)
    assert s % tm == 0 and sum(p.shape[1] for p in dps) == nb * cb

    def body(x_ref, nw_ref, dres_ref, *rest):
        *dp_refs, w_ref, dx_ref, dnw_ref, w_all = rest

        @pl.when(pl.program_id(0) == 0)
        def _():
            dnw_ref[...] = jnp.zeros_like(dnw_ref)
            for j in range(nb):
                w_all[:, j * cb:(j + 1) * cb] = w_ref[j]

        dh = _dot_nt(_beside(dp_refs), w_all[...])
        xv = x_ref[...]
        dx, dn = _rms_bwd(xv, _rms_r(xv), nw_ref[...], dh)
        dx_ref[...] = dres_ref[...] + dx
        dnw_ref[...] += dn

    row = pl.BlockSpec((tm, d), lambda i: (i, 0))
    vec = pl.BlockSpec((1, d), lambda i: (0, 0))
    return pl.pallas_call(
        body, name="mix_pre_bwd", grid=(s // tm,),
        in_specs=[row, vec, row] + [pl.BlockSpec((tm, p.shape[1]), lambda i: (i, 0)) for p in dps]
        + [pl.BlockSpec((nb, d, cb), lambda i: (0, 0, 0), pipeline_mode=pl.Buffered(1))],
        out_specs=[row, vec],
        out_shape=[jax.ShapeDtypeStruct((s, d), F32), jax.ShapeDtypeStruct((1, d), F32)],
        scratch_shapes=[pltpu.VMEM((d, nb * cb), BF16)],
        compiler_params=_params(("arbitrary",)),
    )(x, nw, dres, *dps, win)


def _mix_post(x, yr, ya, nr, na, wout):
    s, d = x.shape
    h = yr.shape[1]
    tm = _tile(s)

    def body(x_ref, yr_ref, ya_ref, nr_ref, na_ref, w_ref, out_ref):
        yrv = yr_ref[...]
        yav = ya_ref[...]
        onb = (yrv * _rms_r(yrv) * nr_ref[...]).astype(BF16)
        oab = (yav * _rms_r(yav) * na_ref[...]).astype(BF16)
        out_ref[...] = x_ref[...] + _dot(onb, w_ref[0:h, :]) + _dot(oab, w_ref[h:2 * h, :])

    row = pl.BlockSpec((tm, d), lambda i: (i, 0))
    half = pl.BlockSpec((tm, h), lambda i: (i, 0))
    vec = pl.BlockSpec((1, h), lambda i: (0, 0))
    return pl.pallas_call(
        body, name="mix_post", grid=(s // tm,),
        in_specs=[row, half, half, vec, vec, pl.BlockSpec((2 * h, d), lambda i: (0, 0))],
        out_specs=row, out_shape=jax.ShapeDtypeStruct((s, d), F32),
        compiler_params=_params(("arbitrary",)),
    )(x, yr, ya, nr, na, wout)


def _mix_post_bwd(dx, yr, ya, nr, na, wout):
    s, d = dx.shape
    h = yr.shape[1]
    tm = _tile(s)

    def body(dx_ref, yr_ref, ya_ref, nr_ref, na_ref, w_ref,
             dyr_ref, dya_ref, yc_ref, dxb_ref, dnr_ref, dna_ref):
        i = pl.program_id(0)
        dxb = dx_ref[...].astype(BF16)
        dxb_ref[...] = dxb
        dyc = _dot_nt(dxb, w_ref[...])
        yrv = yr_ref[...]
        yav = ya_ref[...]
        rr = _rms_r(yrv)
        ra = _rms_r(yav)
        yc_ref[:, 0:h] = (yrv * rr * nr_ref[...]).astype(BF16)
        yc_ref[:, h:2 * h] = (yav * ra * na_ref[...]).astype(BF16)
        dyr, dnr = _rms_bwd(yrv, rr, nr_ref[...], dyc[:, 0:h])
        dya, dna = _rms_bwd(yav, ra, na_ref[...], dyc[:, h:2 * h])
        dyr_ref[...] = dyr
        dya_ref[...] = dya

        @pl.when(i == 0)
        def _():
            dnr_ref[...] = jnp.zeros_like(dnr_ref)
            dna_ref[...] = jnp.zeros_like(dna_ref)

        dnr_ref[...] += dnr
        dna_ref[...] += dna

    row = pl.BlockSpec((tm, d), lambda i: (i, 0))
    half = pl.BlockSpec((tm, h), lambda i: (i, 0))
    vec = pl.BlockSpec((1, h), lambda i: (0, 0))
    return pl.pallas_call(
        body, name="mix_post_bwd", grid=(s // tm,),
        in_specs=[row, half, half, vec, vec, pl.BlockSpec((2 * h, d), lambda i: (0, 0))],
        out_specs=[half, half, pl.BlockSpec((tm, 2 * h), lambda i: (i, 0)), row, vec, vec],
        out_shape=[jax.ShapeDtypeStruct((s, h), F32), jax.ShapeDtypeStruct((s, h), F32),
                   jax.ShapeDtypeStruct((s, 2 * h), BF16), jax.ShapeDtypeStruct((s, d), BF16),
                   jax.ShapeDtypeStruct((1, h), F32), jax.ShapeDtypeStruct((1, h), F32)],
        compiler_params=_params(("arbitrary",)),
    )(dx, yr, ya, nr, na, wout)


def _shift_down(xv, s, prev8):
    rolled = pltpu.roll(xv, s, 0)
    row8 = lax.broadcasted_iota(jnp.int32, prev8.shape, 0)
    head = jnp.where(row8 < s, pltpu.roll(prev8, s, 0), rolled[0:8, :])
    return jnp.concatenate([head, rolled[8:, :]], axis=0)


def _shift_up(xv, s, next8):
    n = xv.shape[0]
    rolled = pltpu.roll(xv, n - s, 0)
    row8 = lax.broadcasted_iota(jnp.int32, next8.shape, 0)
    tail = jnp.where(row8 >= 8 - s, pltpu.roll(next8, 8 - s, 0), rolled[n - 8:, :])
    return jnp.concatenate([rolled[:n - 8, :], tail], axis=0)


def _scan_fwd(a, b):
    n = a.shape[0]
    sub = lax.broadcasted_iota(jnp.int32, a.shape, 0) % SUBLANES
    s = 1
    while s < SUBLANES:
        ok = sub >= s
        b = jnp.where(ok, a * pltpu.roll(b, s, 0) + b, b)
        a = jnp.where(ok, a * pltpu.roll(a, s, 0), a)
        s *= 2
    groups = []
    before = jnp.zeros((1, a.shape[1]), F32)
    for g in range(n // SUBLANES):
        rows = slice(g * SUBLANES, (g + 1) * SUBLANES)
        groups.append(a[rows] * before + b[rows])
        before = groups[-1][SUBLANES - 1:]
    return jnp.concatenate(groups, axis=0)


def _scan_bwd(a, b):
    n = a.shape[0]
    sub = lax.broadcasted_iota(jnp.int32, a.shape, 0) % SUBLANES
    s = 1
    while s < SUBLANES:
        ok = sub < SUBLANES - s
        b = jnp.where(ok, a * pltpu.roll(b, n - s, 0) + b, b)
        a = jnp.where(ok, a * pltpu.roll(a, n - s, 0), a)
        s *= 2
    groups = []
    after = jnp.zeros((1, a.shape[1]), F32)
    for g in reversed(range(n // SUBLANES)):
        rows = slice(g * SUBLANES, (g + 1) * SUBLANES)
        groups.append(a[rows] * after + b[rows])
        after = groups[-1][:1]
    return jnp.concatenate(groups[::-1], axis=0)


def _rglru_gates(xv, prev8, cw_ref, cb_ref, wa_ref, ba_ref, wx_ref, bx_ref, lam_ref):
    x1 = _shift_down(xv, 1, prev8)
    x2 = _shift_down(xv, 2, prev8)
    x3 = _shift_down(xv, 3, prev8)
    xc = cw_ref[3:4, :] * xv + cw_ref[2:3, :] * x1 + cw_ref[1:2, :] * x2 + cw_ref[0:1, :] * x3 + cb_ref[...]
    xcb = xc.astype(BF16)
    r = _sigmoid(_dot(xcb, wa_ref[...]) + ba_ref[...])
    ig = _sigmoid(_dot(xcb, wx_ref[...]) + bx_ref[...])
    c = RG_C * _log_sigmoid(lam_ref[...])
    la = r * c
    a = jnp.exp(la)
    m = jnp.sqrt(-_expm1_neg(2.0 * la))
    return (x1, x2, x3), xc, xcb, r, ig, c, a, m


def _rglru_fwd(proj, cw, cb, wa, ba, wx, bx, lam, rider=None):
    s = proj.shape[0]
    w = D_RNN
    tm = _tile(s)
    ni = s // tm

    def body(*refs):
        ins, (y_ref, h_ref), (prev, hlast), copies = _split_refs(refs, 9, 2, rider)
        xr_ref, gate_ref, cw_ref, cb_ref, wa_ref, ba_ref, wx_ref, bx_ref, lam_ref = ins
        finish = _ride(copies, pl.program_id(0) == 0, pl.program_id(0) == ni - 1)

        @pl.when(pl.program_id(0) == 0)
        def _():
            prev[...] = jnp.zeros_like(prev)
            hlast[...] = jnp.zeros_like(hlast)

        xv = xr_ref[...]
        _, xc, _, _, ig, _, a, m = _rglru_gates(xv, prev[...], cw_ref, cb_ref, wa_ref, ba_ref,
                                                wx_ref, bx_ref, lam_ref)
        b = m * (ig * xc)
        row = lax.broadcasted_iota(jnp.int32, b.shape, 0)
        b = jnp.where(row == 0, b + a * hlast[...], b)
        h = _scan_fwd(a, b)
        h_ref[...] = h
        y_ref[...] = h * _gelu(gate_ref[...])
        prev[...] = xv[tm - 8:, :]
        hlast[...] = h[tm - 1:tm, :]
        finish()

    vec = pl.BlockSpec((1, w), lambda i: (0, 0))
    sq = pl.BlockSpec((w, w), lambda i: (0, 0))
    out = pl.BlockSpec((tm, w), lambda i: (i, 0))
    return _call(
        body, "rglru_fwd", (ni,),
        [pl.BlockSpec((tm, w), lambda i: (i, 0)), pl.BlockSpec((tm, w), lambda i: (i, 1)),
         pl.BlockSpec((CONV_W, w), lambda i: (0, 0)), vec, sq, vec, sq, vec, vec], [out, out],
        [jax.ShapeDtypeStruct((s, w), F32), jax.ShapeDtypeStruct((s, w), F32)],
        [proj, proj, cw, cb, wa, ba, wx, bx, lam],
        scratch=[pltpu.VMEM((8, w), F32), pltpu.VMEM((1, w), F32)], rider=rider)


def _rglru_bwd(proj, hseq, dyr, cw, cb, wa, ba, wx, bx, lam):
    s = proj.shape[0]
    w = D_RNN
    tm = _tile(s)
    nt = s // tm
    t8 = tm // 8

    def body(xr_ref, xp_ref, gate_ref, h_ref, hp_ref, dy_ref, cw_ref, cb_ref, wa_ref, ba_ref,
             wx_ref, bx_ref, lam_ref,
             dxr_ref, dgate_ref, dcw_ref, dcb_ref, dwa_ref, dba_ref, dwx_ref, dbx_ref, dlam_ref,
             carry, dxc_next):
        i = pl.program_id(0)
        first_tile = i == nt - 1

        @pl.when(i == 0)
        def _():
            carry[...] = jnp.zeros_like(carry)
            dxc_next[...] = jnp.zeros_like(dxc_next)
            for ref in (dcw_ref, dcb_ref, dwa_ref, dba_ref, dwx_ref, dbx_ref, dlam_ref):
                ref[...] = jnp.zeros_like(ref)

        xv = xr_ref[...]
        prev8 = jnp.where(first_tile, 0.0, xp_ref[...])
        hprev8 = jnp.where(first_tile, 0.0, hp_ref[...])
        (x1, x2, x3), xc, xcb, r, ig, c, a, m = _rglru_gates(
            xv, prev8, cw_ref, cb_ref, wa_ref, ba_ref, wx_ref, bx_ref, lam_ref)
        gv = gate_ref[...]
        hv = h_ref[...]
        dy = dy_ref[...]
        dgate_ref[...] = (dy * hv * _gelu_grad(gv)).astype(BF16)
        dh = dy * _gelu(gv)
        row = lax.broadcasted_iota(jnp.int32, dh.shape, 0)
        dh = jnp.where(row == tm - 1, dh + carry[...], dh)
        a_up = jnp.where(row == tm - 1, 0.0, pltpu.roll(a, tm - 1, 0))
        lam_t = _scan_bwd(a_up, dh)
        carry[...] = a[0:1, :] * lam_t[0:1, :]
        hm1 = _shift_down(hv, 1, hprev8)
        da = lam_t * hm1
        ixc = ig * xc
        dm = lam_t * ixc
        dig = lam_t * m * xc
        dxc = lam_t * m * ig
        dla = da * a - dm * (a * a) / m
        dr = dla * c
        dlam_ref[...] += jnp.sum(dla * r, axis=0, keepdims=True)
        dpa = dr * r * (1.0 - r)
        dpi = dig * ig * (1.0 - ig)
        dba_ref[...] += jnp.sum(dpa, axis=0, keepdims=True)
        dbx_ref[...] += jnp.sum(dpi, axis=0, keepdims=True)
        dpab = dpa.astype(BF16)
        dpib = dpi.astype(BF16)
        dwa_ref[...] += _dot_tn(xcb, dpab)
        dwx_ref[...] += _dot_tn(xcb, dpib)
        dxc = dxc + _dot_nt(dpab, wa_ref[...]) + _dot_nt(dpib, wx_ref[...])
        dcb_ref[...] += jnp.sum(dxc, axis=0, keepdims=True)
        dcw_ref[3:4, :] += jnp.sum(dxc * xv, axis=0, keepdims=True)
        dcw_ref[2:3, :] += jnp.sum(dxc * x1, axis=0, keepdims=True)
        dcw_ref[1:2, :] += jnp.sum(dxc * x2, axis=0, keepdims=True)
        dcw_ref[0:1, :] += jnp.sum(dxc * x3, axis=0, keepdims=True)
        nxt = dxc_next[...]
        dxr = (cw_ref[3:4, :] * dxc + cw_ref[2:3, :] * _shift_up(dxc, 1, nxt)
               + cw_ref[1:2, :] * _shift_up(dxc, 2, nxt) + cw_ref[0:1, :] * _shift_up(dxc, 3, nxt))
        dxr_ref[...] = dxr.astype(BF16)
        dxc_next[...] = dxc[0:8, :]

        @pl.when(first_tile)
        def _():
            lv = lam_ref[...]
            dlam_ref[...] = dlam_ref[...] * (RG_C * _sigmoid(-lv))

    rev = lambda i: nt - 1 - i
    vec = pl.BlockSpec((1, w), lambda i: (0, 0))
    sq = pl.BlockSpec((w, w), lambda i: (0, 0))
    cur = lambda col: pl.BlockSpec((tm, w), lambda i: (rev(i), col))
    before = lambda cols: pl.BlockSpec((8, w), lambda i: (jnp.maximum(rev(i) * t8 - 1, 0), 0))
    return pl.pallas_call(
        body, name="rglru_bwd", grid=(nt,),
        in_specs=[cur(0), before(None), cur(1), cur(0), before(None), cur(0),
                  pl.BlockSpec((CONV_W, w), lambda i: (0, 0)), vec, sq, vec, sq, vec, vec],
        out_specs=[cur(0), cur(0), pl.BlockSpec((CONV_W, w), lambda i: (0, 0)), vec, sq, vec, sq, vec, vec],
        out_shape=[jax.ShapeDtypeStruct((s, w), BF16), jax.ShapeDtypeStruct((s, w), BF16),
                   jax.ShapeDtypeStruct((CONV_W, w), F32), jax.ShapeDtypeStruct((1, w), F32),
                   jax.ShapeDtypeStruct((w, w), F32), jax.ShapeDtypeStruct((1, w), F32),
                   jax.ShapeDtypeStruct((w, w), F32), jax.ShapeDtypeStruct((1, w), F32),
                   jax.ShapeDtypeStruct((1, w), F32)],
        scratch_shapes=[pltpu.VMEM((1, w), F32), pltpu.VMEM((8, w), F32)],
        compiler_params=_params(("arbitrary",)),
    )(proj, proj, proj, hseq, hseq, dyr, cw, cb, wa, ba, wx, bx, lam)


def _sb_logs(z, valid):
    lb = jnp.minimum(z, 0.0) - jnp.log(1.0 + jnp.exp(-jnp.abs(z)))
    return lb, jnp.where(valid, lb - z, 0.0)


class _Window:
    def __init__(self):
        blk, win, cut = ATT_BLOCK, ATT_WINDOW, ATT_SPLIT
        self.row = lax.broadcasted_iota(jnp.int32, (blk, win), 0)
        self.col = lax.broadcasted_iota(jnp.int32, (blk, win), 1)

        def tri(n, later):
            j = lax.broadcasted_iota(jnp.int32, (n, n), 0)
            s = lax.broadcasted_iota(jnp.int32, (n, n), 1)
            return jnp.where((j > s) if later else (j < s), 1.0, 0.0).astype(BF16)

        self.later = (tri(cut, True), tri(win - cut, True))
        self.earlier = (tri(cut, False), tri(win - cut, False))

    def place(self, qi, g):
        end = (qi + 1) * ATT_BLOCK - g * ATT_WINDOW
        start = pl.multiple_of(jnp.maximum(end - ATT_WINDOW, 0), ATT_BLOCK)
        valid = self.col < jnp.minimum(self.row + (qi * ATT_BLOCK - start), end - start)
        return start, valid

    @staticmethod
    def _parts(xv):
        hi = xv.astype(BF16)
        lo = (xv - hi.astype(F32)).astype(BF16)
        cut = ATT_SPLIT
        sums = (jnp.sum(xv[:, :cut], axis=1, keepdims=True), jnp.sum(xv[:, cut:], axis=1, keepdims=True))
        return (hi[:, :cut], lo[:, :cut]), (hi[:, cut:], lo[:, cut:]), sums

    def sums_after(self, xv, carry):
        (h0, l0), (h1, l1), (s0, s1) = self._parts(xv)
        first = _dot(h0, self.later[0]) + _dot(l0, self.later[0]) + (s1 + carry)
        last = _dot(h1, self.later[1]) + _dot(l1, self.later[1]) + carry
        return jnp.concatenate([first, last], axis=1), s0 + s1

    def sums_before(self, xv, carry):
        (h0, l0), (h1, l1), (s0, s1) = self._parts(xv)
        first = _dot(h0, self.earlier[0]) + _dot(l0, self.earlier[0]) + carry
        last = _dot(h1, self.earlier[1]) + _dot(l1, self.earlier[1]) + (s0 + carry)
        return jnp.concatenate([first, last], axis=1), s0 + s1


class _HeadPair:
    def __init__(self):
        lanes = 2 * HEAD_DIM
        lane = lax.broadcasted_iota(jnp.int32, (1, lanes), 1)
        self.masks = [lane // HEAD_DIM == h for h in (0, 1)]
        i = lax.broadcasted_iota(jnp.int32, (lanes, lanes), 0) // HEAD_DIM
        j = lax.broadcasted_iota(jnp.int32, (lanes, lanes), 1) // HEAD_DIM
        self.same_head = jnp.where(i == j, 1.0, 0.0).astype(BF16)

    def only(self, h, xv):
        return jnp.where(self.masks[h], xv, jnp.zeros_like(xv))

    def merge(self, per_head):
        return jnp.where(self.masks[0], per_head[0], per_head[1])

    def mean(self, xv):
        hi = xv.astype(BF16)
        lo = (xv - hi.astype(F32)).astype(BF16)
        return (_dot(hi, self.same_head) + _dot(lo, self.same_head)) * (1.0 / HEAD_DIM)

    def rms_r(self, xv):
        return lax.rsqrt(self.mean(xv * xv) + EPS)

    def rms_bwd(self, xv, r, nw, dh):
        t = dh * nw
        dx = r * t - xv * (r * r * r * self.mean(t * xv))
        dn = jnp.sum(dh * xv * r, axis=0, keepdims=True)
        return dx, dn[:, :HEAD_DIM] + dn[:, HEAD_DIM:]


def _attn_fwd(proj, qg, kg, rider=None):
    s = proj.shape[0]
    blk, win, dh = ATT_BLOCK, ATT_WINDOW, HEAD_DIM
    nq = s // blk
    scale = 1.0 / math.sqrt(dh)
    heads = (0, 1)
    blocks = (0, 1)
    assert s >= win and s % (blk * len(blocks)) == 0

    def body(*refs):
        (q_ref, k_ref, v_ref, qg_ref, kg_ref), (o_ref,), (qn, kn, vb), copies = _split_refs(refs, 5, 1, rider)
        finish = _ride(copies, pl.program_id(0) == 0, pl.program_id(0) == N_HEADS // 2 - 1)
        wd, hp = _Window(), _HeadPair()
        qv = q_ref[...]
        qn[...] = (qv * hp.rms_r(qv) * qg_ref[...] * scale).astype(BF16)
        kv = k_ref[...]
        kn[...] = (kv * hp.rms_r(kv) * kg_ref[...]).astype(BF16)
        vb[...] = v_ref[...].astype(BF16)

        def q_step(pair_i, _):
            qis = [2 * pair_i + b for b in blocks]
            chains = [(b, h) for b in blocks for h in heads]
            qoffs = [pl.multiple_of(qi * blk, blk) for qi in qis]
            qtiles = [qn[pl.ds(qoff, blk), :] for qoff in qoffs]
            qts = [hp.only(h, qtiles[b]) for b, h in chains]

            def more(carry):
                g, live = carry[:2]
                return jnp.logical_and((qis[-1] + 1) * blk - g * win > 0, live > 0)

            def window(carry):
                g, _, accs, runs = carry
                places = [wd.place(qi, g) for qi in qis]
                kts = [kn[pl.ds(start, win), :] for start, _ in places]
                zs = [_dot_nt(qts[c], kts[b]) for c, (b, h) in enumerate(chains)]
                logs = [_sb_logs(zs[c], places[b][1]) for c, (b, h) in enumerate(chains)]
                sums = [wd.sums_after(logs[c][1], runs[c]) for c in range(len(chains))]
                wgts = [jnp.where(places[b][1], jnp.exp(logs[c][0] + sums[c][0]), 0.0).astype(BF16)
                        for c, (b, h) in enumerate(chains)]
                vts = [vb[pl.ds(start, win), :] for start, _ in places]
                accs = tuple(accs[c] + _dot(wgts[c], vts[b]) for c, (b, h) in enumerate(chains))
                runs = tuple(runs[c] + sums[c][1] for c in range(len(chains)))
                top = functools.reduce(jnp.maximum, [jnp.max(r) for r in runs])
                return g + 1, (top > EXP_ZERO).astype(jnp.int32), accs, runs

            zero = lambda cols: tuple(jnp.zeros((blk, cols), F32) for _ in chains)
            _, _, accs, _ = lax.while_loop(more, window, (jnp.int32(0), jnp.int32(1), zero(2 * dh), zero(1)))
            for b in blocks:
                o_ref[pl.ds(qoffs[b], blk), :] = hp.merge([accs[2 * b + h] for h in heads])
            return 0

        lax.fori_loop(0, nq // len(blocks), q_step, 0)
        finish()

    pair = lambda group: pl.BlockSpec((s, 2 * dh), lambda p: (0, group * (D_ATT // (2 * dh)) + p))
    vec = pl.BlockSpec((1, 2 * dh), lambda p: (0, 0))
    return _call(
        body, "attn_fwd", (N_HEADS // 2,), [pair(2), pair(3), pair(4), vec, vec], [pair(0)],
        [jax.ShapeDtypeStruct((s, D_ATT), F32)], [proj, proj, proj, jnp.tile(qg, (1, 2)), jnp.tile(kg, (1, 2))],
        scratch=[pltpu.VMEM((s, 2 * dh), BF16)] * 3, rider=rider)


def _attn_bwd(proj, dya, qg, kg, rider=None):
    s = proj.shape[0]
    blk, win, dh = ATT_BLOCK, ATT_WINDOW, HEAD_DIM
    nq = s // blk
    max_windows = -(-s // win) + 1
    scale = 1.0 / math.sqrt(dh)
    steps = N_HEADS // 2
    heads = (0, 1)
    blocks = (0, 1)
    assert s >= win and s % (blk * len(blocks)) == 0

    def body(*refs):
        ins, outs, scratch, copies = _split_refs(refs, 6, 5, rider)
        q_ref, k_ref, v_ref, do_ref, qg_ref, kg_ref = ins
        dq_ref, dk_ref, dv_ref, dqg_ref, dkg_ref = outs
        qn, kn, vb, dob, runs_ref, dqn, dkn, dvn = scratch
        finish = _ride(copies, pl.program_id(0) == 0, pl.program_id(0) == steps - 1)
        wd, hp = _Window(), _HeadPair()

        @pl.when(pl.program_id(0) == 0)
        def _():
            dqg_ref[...] = jnp.zeros_like(dqg_ref)
            dkg_ref[...] = jnp.zeros_like(dkg_ref)

        qv = q_ref[...]
        qn[...] = (qv * hp.rms_r(qv) * qg_ref[...] * scale).astype(BF16)
        kv = k_ref[...]
        kn[...] = (kv * hp.rms_r(kv) * kg_ref[...]).astype(BF16)
        vb[...] = v_ref[...].astype(BF16)
        dob[...] = do_ref[...].astype(BF16)
        dkn[...] = jnp.zeros_like(dkn)
        dvn[...] = jnp.zeros_like(dvn)

        def q_step(pair_i, _):
            qis = [2 * pair_i + b for b in blocks]
            chains = [(b, h) for b in blocks for h in heads]
            ids = range(len(chains))
            qoffs = [pl.multiple_of(qi * blk, blk) for qi in qis]
            qts = [hp.only(h, qn[pl.ds(qoffs[b], blk), :]) for b, h in chains]
            dots = [hp.only(h, dob[pl.ds(qoffs[b], blk), :]) for b, h in chains]

            zero = lambda cols: tuple(jnp.zeros((blk, cols), F32) for _ in chains)

            def logs_of(g):
                places = [wd.place(qi, g) for qi in qis]
                kts = [kn[pl.ds(start, win), :] for start, _ in places]
                return [_sb_logs(_dot_nt(qts[c], kts[b]), places[b][1]) for c, (b, h) in enumerate(chains)]

            def row_sums(logs):
                return tuple(jnp.sum(logs[c][1], axis=1, keepdims=True) for c in ids)

            def still_live(runs):
                return functools.reduce(jnp.maximum, [jnp.max(r) for r in runs]) > EXP_ZERO

            def window_grads(g, logs, runs, esums):
                places = [wd.place(qi, g) for qi in qis]
                kts = [kn[pl.ds(start, win), :] for start, _ in places]
                vts = [vb[pl.ds(start, win), :] for start, _ in places]
                dws = [_dot_nt(dots[c], vts[b]) for c, (b, h) in enumerate(chains)]
                tails = [wd.sums_after(logs[c][1], runs[c])[0] for c in ids]
                wgts = [jnp.where(places[b][1], jnp.exp(logs[c][0] + tails[c]), 0.0) for c, (b, h) in enumerate(chains)]
                es = [dws[c] * wgts[c] for c in ids]
                befores = [wd.sums_before(es[c], esums[c]) for c in ids]
                dzbs = []
                for c, (b, h) in enumerate(chains):
                    beta = jnp.exp(logs[c][0])
                    dz = jnp.where(places[b][1], es[c] * (1.0 - beta) - befores[c][0] * beta, 0.0)
                    dzbs.append(dz.astype(BF16))
                for b in blocks:
                    rows = pl.ds(places[b][0], win)
                    dkn[rows, :] += _dot_tn(dzbs[2 * b], qts[2 * b]) + _dot_tn(dzbs[2 * b + 1], qts[2 * b + 1])
                    dvn[rows, :] += (_dot_tn(wgts[2 * b].astype(BF16), dots[2 * b])
                                     + _dot_tn(wgts[2 * b + 1].astype(BF16), dots[2 * b + 1]))
                return (tuple(_dot(dzbs[c], kts[b]) for c, (b, h) in enumerate(chains)),
                        tuple(befores[c][1] for c in ids))

            logs0 = logs_of(0)
            runs1 = row_sums(logs0)

            def one_window():
                return window_grads(0, logs0, zero(1), zero(1))[0]

            def all_windows():
                def more(carry):
                    g, live = carry[:2]
                    return jnp.logical_and((qis[-1] + 1) * blk - g * win > 0, live > 0)

                def run_window(carry):
                    g, _, runs = carry
                    for c in ids:
                        runs_ref[c, g] = runs[c]
                    sums = row_sums(logs_of(g))
                    runs = tuple(runs[c] + sums[c] for c in ids)
                    return g + 1, still_live(runs).astype(jnp.int32), runs

                for c in ids:
                    runs_ref[c, 0] = jnp.zeros((blk, 1), F32)
                windows, _, _ = lax.while_loop(more, run_window, (jnp.int32(1), jnp.int32(1), runs1))

                def k_window(gg, carry):
                    dq_accs, esums = carry
                    g = windows - 1 - gg
                    parts, totals = window_grads(g, logs_of(g), [runs_ref[c, g] for c in ids], esums)
                    return (tuple(dq_accs[c] + parts[c] for c in ids), tuple(esums[c] + totals[c] for c in ids))

                return lax.fori_loop(0, windows, k_window, (zero(2 * dh), zero(1)))[0]

            earlier_keys = (qis[-1] + 1) * blk - win > 0
            dq_accs = lax.cond(jnp.logical_and(earlier_keys, still_live(runs1)), all_windows, one_window)
            for b in blocks:
                dqn[pl.ds(qoffs[b], blk), :] = hp.merge([dq_accs[2 * b + h] for h in heads])
            return 0

        lax.fori_loop(0, nq // len(blocks), q_step, 0)

        dq, dqg = hp.rms_bwd(qv, hp.rms_r(qv), qg_ref[...] * scale, dqn[...])
        dq_ref[...] = dq.astype(BF16)
        dqg_ref[...] += dqg * scale
        dk, dkg = hp.rms_bwd(kv, hp.rms_r(kv), kg_ref[...], dkn[...])
        dk_ref[...] = dk.astype(BF16)
        dkg_ref[...] += dkg
        dv_ref[...] = dvn[...].astype(BF16)
        finish()

    pair = lambda group: pl.BlockSpec((s, 2 * dh), lambda p: (0, group * (D_ATT // (2 * dh)) + p))
    vec2 = pl.BlockSpec((1, 2 * dh), lambda p: (0, 0))
    vec = pl.BlockSpec((1, dh), lambda p: (0, 0))
    return _call(
        body, "attn_bwd", (steps,), [pair(2), pair(3), pair(4), pair(0), vec2, vec2],
        [pair(0), pair(0), pair(0), vec, vec],
        [jax.ShapeDtypeStruct((s, D_ATT), BF16)] * 3 + [jax.ShapeDtypeStruct((1, dh), F32)] * 2,
        [proj, proj, proj, dya, jnp.tile(qg, (1, 2)), jnp.tile(kg, (1, 2))],
        scratch=[pltpu.VMEM((s, 2 * dh), BF16)] * 4 + [pltpu.VMEM((4, max_windows, blk, 1), F32)]
        + [pltpu.VMEM((s, 2 * dh), F32)] * 3, rider=rider)


def _block_diag(w):
    n, c, d = w.shape
    return jnp.einsum("ncd,nm->ncmd", w, jnp.eye(n, dtype=w.dtype)).reshape(n * c, n * d)


def _diag_blocks(full, n):
    c = full.shape[0] // n
    on_diagonal = jnp.eye(n, dtype=bool)[:, None, :, None]
    return jnp.sum(jnp.where(on_diagonal, full.reshape(n, c, n, c), 0.0), axis=2)


FFN1 = ["ffn1_w_gate", "ffn1_w_up", "ffn1_w_down"]
FFN2 = ["ffn2_w_gate", "ffn2_w_up", "ffn2_w_down"]


def _pair_sums(gb, names, where):
    theirs = _pair_exchange([gb[n] for n in names], "pair_exchange_" + names[0])
    pair, own = _pair_sum([gb[n] for n in names], theirs, where, "pair_sum_" + names[0])
    return _chip_rider(pair, own)


def _local_step(x, tgt, stacks, conv_stack, small, where):
    gate_up, down = FFN1[:2], FFN1[2:]
    big = dict(zip(gate_up, _gather_weights([stacks[n] for n in gate_up], [])))
    wa = _block_diag(small["rg_w_a"]).astype(BF16)
    wx = _block_diag(small["rg_w_x"]).astype(BF16)

    whole = lambda names: [big[n].reshape(-1, D_MODEL) for n in names]
    soon = down + ["w_in"]
    g1, u1, hb1, ab1, *landed = _ffn_up(x, small["ffn1_norm"], *whole(gate_up),
                                        rider=_gather_rider([stacks[n] for n in soon], [conv_stack]))
    big.update(zip(soon, landed))
    x1 = _ffn_down(x, ab1, *whole(down))
    conv_w = jnp.transpose(landed[-1], (1, 0, 2)).reshape(CONV_W, D_RNN)
    rg = (conv_w, small["conv_b"], wa, small["rg_b_a"], wx, small["rg_b_x"], small["rg_lambda"])
    riding = lambda names: _gather_rider([stacks[n] for n in names], [])
    proj, hb2, big["ffn2_w_gate"] = _mix_pre(x1, small["mix_norm"], big["w_in"], riding(["ffn2_w_gate"]))
    yr, hseq, big["ffn2_w_up"] = _rglru_fwd(proj, *rg, riding(["ffn2_w_up"]))
    ya, big["ffn2_w_down"], big["w_out"] = _attn_fwd(proj, small["q_norm"], small["k_norm"],
                                                     riding(["ffn2_w_down", "w_out"]))
    wout = big["w_out"].reshape(D_MODEL, D_MODEL)
    x2 = _mix_post(x1, yr, ya, small["rnn_out_norm"], small["attn_out_norm"], wout)
    dx3, g2, u2, hb3, ab3, loss = _ffn_fwd_loss(x2, small["ffn2_norm"], *whole(FFN2), tgt)

    gb, gs, slots = {}, {}, {}
    dx2, dg2, du2, dyb2, gs["ffn2_norm"] = _ffn_bwd_act(x2, small["ffn2_norm"], dx3, g2, u2, *whole(FFN2), "ffn2_bwd")
    gb["ffn2_w_gate"] = _ffn_wgrad(dg2, hb3, 1.0, "wgrad_gate_ffn2")
    gb["ffn2_w_up"] = _ffn_wgrad(du2, hb3, 1.0, "wgrad_up_ffn2")
    gb["ffn2_w_down"] = _ffn_wgrad(ab3, dyb2, 0.5, "wgrad_down_ffn2")
    dyr, dya, ycat, dxb2, gs["rnn_out_norm"], gs["attn_out_norm"] = _mix_post_bwd(
        dx2, yr, ya, small["rnn_out_norm"], small["attn_out_norm"], wout)
    gb["w_out"] = _wgrad_whole(ycat, [dxb2], False, "wgrad_out")
    early = FFN2 + ["w_out"]
    dq, dk, dv, gs["q_norm"], gs["k_norm"], *done = _attn_bwd(
        proj, dya, small["q_norm"], small["k_norm"], _pair_sums(gb, early, where))
    slots.update(zip(early, done))
    dxr, dgate, gs["conv_w"], gs["conv_b"], dwa, gs["rg_b_a"], dwx, gs["rg_b_x"], gs["rg_lambda"] = _rglru_bwd(
        proj, hseq, dyr, *rg)
    gs["rg_w_a"] = _diag_blocks(dwa, RNN_BLOCKS)
    gs["rg_w_x"] = _diag_blocks(dwx, RNN_BLOCKS)
    dps = [dxr, dgate, dq, dk, dv]
    dx1, gs["mix_norm"] = _mix_pre_bwd(x1, small["mix_norm"], dx2, dps, big["w_in"])
    dx0, dg1, du1, dyb1, gs["ffn1_norm"] = _ffn_bwd_act(x, small["ffn1_norm"], dx1, g1, u1, *whole(FFN1), "ffn1_bwd")

    mine = _place_shard(_pack([gs[n] for n in SMALL] + [loss[:, :1]]), where, F32, "place_small_grads",
                        by_device=True)
    gb["ffn1_w_gate"], everyone = _ffn_wgrad(dg1, hb1, 1.0, "wgrad_gate_ffn1", _small_rider(mine))
    gb["ffn1_w_up"], slots["ffn1_w_gate"] = _ffn_wgrad(
        du1, hb1, 1.0, "wgrad_up_ffn1", _pair_sums(gb, ["ffn1_w_gate"], where))
    gb["ffn1_w_down"], slots["ffn1_w_up"] = _ffn_wgrad(
        ab1, dyb1, 0.5, "wgrad_down_ffn1", _pair_sums(gb, ["ffn1_w_up"], where))
    gb["w_in"], slots["ffn1_w_down"] = _wgrad_whole(
        hb2, dps, True, "wgrad_in", _pair_sums(gb, ["ffn1_w_down"], where))
    last = _pair_sums(gb, ["w_in"], where)
    slots["w_in"], = _chip_exchange(last.plain, last.inplace)
    return dx0, slots, gs, everyone


ANY = pl.BlockSpec(memory_space=pl.ANY)


def _place():
    x, y, c = lax.axis_index("x"), lax.axis_index("y"), lax.axis_index("c")
    other_chips = [(1 - x, y), (x, 1 - y), (1 - x, 1 - y)]
    return x, y, c, 2 * x + y, other_chips


def _remote(src, dst, send_sem, recv_sem, to):
    return pltpu.make_async_remote_copy(src_ref=src, dst_ref=dst, send_sem=send_sem, recv_sem=recv_sem,
                                        device_id=to, device_id_type=MESH)


def _copy_plan(pairs):
    sends = [functools.partial(_remote, *a) for a, _ in pairs]
    arrivals = [functools.partial(_remote, *b) for _, b in pairs]
    return sends, arrivals


class _Rider:
    def __init__(self, plan, plain, inplace, n_copies=None, relay=None, n_relay=0):
        self.plan, self.plain, self.inplace = plan, list(plain), list(inplace)
        self.n_copies = n_copies or 3 * len(self.inplace)
        self.relay, self.n_relay = relay, n_relay

    def operands(self):
        return self.plain + self.inplace

    def out_shape(self):
        return [jax.ShapeDtypeStruct(a.shape, a.dtype) for a in self.inplace]

    def aliases(self, inputs_before, outputs_before):
        return {inputs_before + len(self.plain) + k: outputs_before + k for k in range(len(self.inplace))}

    def scratch(self):
        relay = [pltpu.SemaphoreType.DMA((self.n_relay,))] * 2 if self.relay else []
        return [pltpu.SemaphoreType.DMA((self.n_copies,))] * 2 + relay


def _split_refs(refs, n_in, n_out, rider):
    if rider is None:
        return refs[:n_in], refs[n_in:n_in + n_out], refs[n_in + n_out:], None
    r_in, r_out = len(rider.operands()), len(rider.inplace)
    outs_at = n_in + r_in
    n_sems = len(rider.scratch())
    rest = refs[outs_at + n_out + r_out:]
    sems = rest[len(rest) - n_sems:]
    filled = refs[outs_at + n_out:outs_at + n_out + r_out]
    copies = functools.partial(rider.plan, refs[n_in:n_in + len(rider.plain)], filled, *sems[:2])
    relay = functools.partial(rider.relay, filled, *sems[2:]) if rider.relay else None
    return refs[:n_in], refs[outs_at:outs_at + n_out], rest[:len(rest) - n_sems], (copies, relay)


def _ride(copies, first, last, middle=None):
    if copies is None:
        return lambda: None
    copies, relay = copies

    @pl.when(first)
    def _():
        _start(copies()[0])

    def start_relay():
        for make in copies()[1]:
            make().wait_recv()
        _start(relay()[0])

    if relay is not None and middle is not None:
        pl.when(middle)(start_relay)

    def finish():
        @pl.when(last)
        def _():
            if relay is None:
                _finish(*copies())
            else:
                if middle is None:
                    start_relay()
                _finish(copies()[0] + relay()[0], relay()[1])

    return finish


def _gather_rider(split, whole):
    n_split = len(split)
    return _Rider(lambda plain, stacks, ss, rs: _gather_ici(stacks, n_split, ss, rs), [], list(split) + list(whole),
                  relay=lambda stacks, ss, rs: _gather_d2d(stacks[:n_split], ss, rs), n_relay=3 * n_split)


def _chip_rider(sums, slots):
    return _Rider(_chip_copies, sums, slots)


def _start(makers):
    for make in makers:
        make().start()


def _finish(sends, arrivals):
    for make in arrivals:
        make().wait_recv()
    for make in sends:
        make().wait_send()


def _half(rows, c):
    return pl.ds(pl.multiple_of(c * rows, BF16_ROWS), rows)


def _gather_weights(split, whole):
    arrs = list(split) + list(whole)
    n, ns = len(arrs), len(split)

    def body(*refs):
        outs = refs[n:2 * n]
        send_sems, recv_sems, fsend_sems, frecv_sems = refs[2 * n:]
        sends, arrivals = _gather_ici(outs, ns, send_sems, recv_sems)
        passes, passed = _gather_d2d(outs[:ns], fsend_sems, frecv_sems)
        _start(sends)
        for k, make in enumerate(arrivals):
            make().wait_recv()
            if k < 3 * ns:
                passes[k]().start()
        _finish(sends + passes, passed)

    return pl.pallas_call(
        body, name="gather_weights",
        in_specs=[ANY] * n, out_specs=[ANY] * n,
        out_shape=[jax.ShapeDtypeStruct(a.shape, a.dtype) for a in arrs],
        input_output_aliases={i: i for i in range(n)},
        scratch_shapes=[pltpu.SemaphoreType.DMA((3 * n,)), pltpu.SemaphoreType.DMA((3 * n,)),
                        pltpu.SemaphoreType.DMA((3 * ns,)), pltpu.SemaphoreType.DMA((3 * ns,))],
    )(*arrs)


def _gather_ici(stacks, n_split, send_sems, recv_sems):
    x, y, c, me, chips = _place()

    def region(i, chip):
        if i < n_split:
            return stacks[i].at[chip, _half(stacks[i].shape[1] // 2, c)]
        return stacks[i].at[chip]

    pairs = []
    for i in range(len(stacks)):
        for p, (cx, cy) in enumerate(chips):
            k = 3 * i + p
            mine, got = region(i, me), region(i, 2 * cx + cy)
            sems, to = (send_sems.at[k], recv_sems.at[k]), (cx, cy, c)
            pairs.append(((mine, mine, *sems, to), (got, got, *sems, to)))
    return _copy_plan(pairs)


def _gather_d2d(stacks, send_sems, recv_sems):
    x, y, c, _, chips = _place()
    sibling = (x, y, 1 - c)
    pairs = []
    for i, stack in enumerate(stacks):
        rows = stack.shape[1] // 2
        for p, (cx, cy) in enumerate(chips):
            k = 3 * i + p
            got, theirs = stack.at[2 * cx + cy, _half(rows, c)], stack.at[2 * cx + cy, _half(rows, 1 - c)]
            sems = (send_sems.at[k], recv_sems.at[k])
            pairs.append(((got, got, *sems, sibling), (theirs, theirs, *sems, sibling)))
    return _copy_plan(pairs)


def _pair_exchange(grads, name):
    n = len(grads)

    def body(*refs):
        ins, theirs = refs[:n], refs[n:2 * n]
        send_sems, recv_sems = refs[2 * n:]
        x, y, c, _, _ = _place()
        sibling = (x, y, 1 - c)
        sends = [_remote(ins[k].at[:, _half(grads[k].shape[1] // 2, 1 - c)], theirs[k],
                         send_sems.at[k], recv_sems.at[k], sibling) for k in range(n)]
        for cp in sends:
            cp.start()
        for k in range(n):
            _remote(theirs[k], theirs[k], send_sems.at[k], recv_sems.at[k], sibling).wait_recv()
        for cp in sends:
            cp.wait_send()

    return pl.pallas_call(
        body, name=name,
        in_specs=[ANY] * n, out_specs=[ANY] * n,
        out_shape=[jax.ShapeDtypeStruct((g.shape[0], g.shape[1] // 2, g.shape[2]), g.dtype) for g in grads],
        scratch_shapes=[pltpu.SemaphoreType.DMA((n,))] * 2,
    )(*grads)


def _chip_exchange(sums, slots):
    n = len(sums)

    def body(*refs):
        sends, arrivals = _chip_copies(refs[:n], refs[2 * n:3 * n], *refs[3 * n:])
        _start(sends)
        _finish(sends, arrivals)

    return pl.pallas_call(
        body, name="grad_chip_exchange",
        in_specs=[ANY] * (2 * n), out_specs=[ANY] * n,
        out_shape=[jax.ShapeDtypeStruct(a.shape, a.dtype) for a in slots],
        input_output_aliases={n + k: k for k in range(n)},
        scratch_shapes=[pltpu.SemaphoreType.DMA((3 * n,)), pltpu.SemaphoreType.DMA((3 * n,))],
    )(*sums, *slots)


def _chip_copies(sums, slots, send_sems, recv_sems):
    x, y, c, me, chips = _place()
    pairs = []
    for k in range(len(sums)):
        for p, (cx, cy) in enumerate(chips):
            j = 3 * k + p
            got = slots[k].at[2 * cx + cy]
            sems, to = (send_sems.at[j], recv_sems.at[j]), (cx, cy, c)
            pairs.append(((sums[k].at[2 * cx + cy], slots[k].at[me], *sems, to), (got, got, *sems, to)))
    return _copy_plan(pairs)


def _half_swap(halves):
    n = len(halves)

    def body(*refs):
        outs = refs[n:2 * n]
        send_sems, recv_sems = refs[2 * n:]
        x, y, c, _, _ = _place()
        sibling = (x, y, 1 - c)
        sends = [_remote(outs[k].at[c], outs[k].at[c], send_sems.at[k], recv_sems.at[k], sibling) for k in range(n)]
        for cp in sends:
            cp.start()
        for k in range(n):
            got = outs[k].at[1 - c]
            _remote(got, got, send_sems.at[k], recv_sems.at[k], sibling).wait_recv()
        for cp in sends:
            cp.wait_send()

    return pl.pallas_call(
        body, name="grad_half_swap",
        in_specs=[ANY] * n, out_specs=[ANY] * n,
        out_shape=[jax.ShapeDtypeStruct(a.shape, a.dtype) for a in halves],
        input_output_aliases={k: k for k in range(n)},
        scratch_shapes=[pltpu.SemaphoreType.DMA((n,))] * 2,
    )(*halves)


def _small_rider(stack):
    n_dev = 2 * N_CHIPS

    def plan(_, stacks, send_sems, recv_sems):
        x, y, c, _, _ = _place()
        mine = stacks[0].at[4 * x + 2 * y + c]
        pairs = []
        for k in range(1, n_dev):
            px, py, pc = x ^ ((k >> 2) & 1), y ^ ((k >> 1) & 1), c ^ (k & 1)
            got = stacks[0].at[4 * px + 2 * py + pc]
            sems = (send_sems.at[k - 1], recv_sems.at[k - 1])
            pairs.append(((mine, mine, *sems, (px, py, pc)), (got, got, *sems, (px, py, pc))))
        return _copy_plan(pairs)

    return _Rider(plan, [], [stack], n_dev - 1)


def _row_tile(r):
    return r // 4 if r >= 256 and (r // 4) % BF16_ROWS == 0 else r


def _prefetch_call(body, name, grid, in_specs, out_specs, out_shape):
    spec = pltpu.PrefetchScalarGridSpec(num_scalar_prefetch=1, grid=grid, in_specs=in_specs, out_specs=out_specs)
    return pl.pallas_call(body, name=name, grid_spec=spec, out_shape=out_shape,
                          compiler_params=_params(("arbitrary",) * len(grid)))


def _place_shard(w2d, where, dtype, name, by_device=False):
    r, c = w2d.shape
    tr = _row_tile(r)
    slots = 2 * N_CHIPS if by_device else N_CHIPS
    slot = (lambda s: 2 * s[1] + s[0]) if by_device else (lambda s: s[1])

    def body(where_ref, w_ref, out_ref):
        out_ref[...] = w_ref[...].astype(dtype)

    return _prefetch_call(
        body, name, (r // tr,), [pl.BlockSpec((tr, c), lambda i, s: (i, 0))],
        pl.BlockSpec((None, tr, c), lambda i, s: (slot(s), i, 0)),
        jax.ShapeDtypeStruct((slots, r, c), dtype))(where, w2d)


def _place_shards(w2ds, where, name):
    n = len(w2ds)
    steps = N_CHIPS
    assert all(w.shape[0] % (BF16_ROWS * steps) == 0 for w in w2ds)

    def body(where_ref, *refs):
        for k in range(n):
            refs[n + k][...] = refs[k][...].astype(BF16)

    tile = lambda w: (w.shape[0] // steps, w.shape[1])
    return _prefetch_call(
        body, name, (steps,), [pl.BlockSpec(tile(w), lambda i, s: (i, 0)) for w in w2ds],
        [pl.BlockSpec((None,) + tile(w), lambda i, s: (s[1], i, 0)) for w in w2ds],
        [jax.ShapeDtypeStruct((N_CHIPS,) + w.shape, BF16) for w in w2ds])(where, *w2ds)


def _pair_sum(fulls, theirs, where, name):
    n = len(fulls)

    def body(where_ref, *refs):
        for k in range(n):
            a_ref, b_ref, out_ref, own_ref = refs[k], refs[n + k], refs[2 * n + k], refs[3 * n + k]
            total = (a_ref[...].astype(F32) + b_ref[...].astype(F32)).astype(BF16)
            out_ref[...] = total

            @pl.when(pl.program_id(0) == where_ref[1])
            def _():
                own_ref[...] = total

    half = lambda t: pl.BlockSpec((None,) + t.shape[1:], lambda j, s: (j, s[0], 0))
    blk = lambda t: pl.BlockSpec((None,) + t.shape[1:], lambda j, s: (j, 0, 0))
    own = lambda t: pl.BlockSpec((None,) + t.shape[1:], lambda j, s: (s[1], 0, 0))
    shapes = [jax.ShapeDtypeStruct(t.shape, BF16) for t in theirs]
    outs = _prefetch_call(
        body, name, (N_CHIPS,), [half(t) for t in theirs] + [blk(t) for t in theirs],
        [blk(t) for t in theirs] + [own(t) for t in theirs], shapes + shapes)(where, *fulls, *theirs)
    return outs[:n], outs[n:]


def _chip_sum(slots, where, name):
    n = len(slots)
    steps = 2
    assert all(a.shape[1] % (BF16_ROWS * steps) == 0 for a in slots)

    def body(where_ref, *refs):
        for k in range(n):
            a_ref, out_ref = refs[k], refs[n + k]
            total = a_ref[0].astype(F32)
            for j in range(1, a_ref.shape[0]):
                total = total + a_ref[j].astype(F32)
            out_ref[...] = total

    tile = lambda a: (a.shape[1] // steps, a.shape[2])
    return _prefetch_call(
        body, name, (steps,), [pl.BlockSpec((a.shape[0],) + tile(a), lambda i, s: (0, i, 0)) for a in slots],
        [pl.BlockSpec((None,) + tile(a), lambda i, s: (s[0], i, 0)) for a in slots],
        [jax.ShapeDtypeStruct((2,) + a.shape[1:], F32) for a in slots])(where, *slots)


def _slot_sum(a, name):
    nb, r, c = a.shape
    tr = _row_tile(r)

    def body(a_ref, out_ref):
        total = a_ref[0].astype(F32)
        for j in range(1, nb):
            total = total + a_ref[j].astype(F32)
        out_ref[...] = total

    return pl.pallas_call(
        body, name=name, grid=(r // tr,),
        in_specs=[pl.BlockSpec((nb, tr, c), lambda i: (0, i, 0))],
        out_specs=pl.BlockSpec((tr, c), lambda i: (i, 0)),
        out_shape=jax.ShapeDtypeStruct((r, c), F32), compiler_params=_params(("arbitrary",)),
    )(a)


def _adamw(ws, gs, ms, vs, name, steps=1):
    n = len(ws)
    c1 = 1.0 - ADAM_B1 ** ADAM_STEP
    c2 = 1.0 - ADAM_B2 ** ADAM_STEP
    assert all(w.shape[0] % steps == 0 and (steps == 1 or w.shape[0] // steps % 8 == 0) for w in ws)

    def body(*refs):
        for k in range(n):
            w_ref, g_ref, m_ref, v_ref = (refs[j * n + k] for j in range(4))
            g_out, d_ref, m2_ref, v2_ref = (refs[(4 + j) * n + k] for j in range(4))
            gv = g_ref[...]
            g_out[...] = gv
            m2 = ADAM_B1 * m_ref[...] + (1.0 - ADAM_B1) * gv
            v2 = ADAM_B2 * v_ref[...] + (1.0 - ADAM_B2) * (gv * gv)
            m2_ref[...] = m2
            v2_ref[...] = v2
            d_ref[...] = -ADAM_LR * ((m2 / c1) / (jnp.sqrt(v2 / c2) + ADAM_EPS) + ADAM_WD * w_ref[...])

    blks = [pl.BlockSpec((w.shape[0] // steps, w.shape[1]), lambda i: (i, 0)) for w in ws]
    shapes = [jax.ShapeDtypeStruct(w.shape, F32) for w in ws]
    outs = pl.pallas_call(
        body, name=name, grid=(steps,), in_specs=blks * 4, out_specs=blks * 4, out_shape=shapes * 4,
        compiler_params=_params(("arbitrary",)),
    )(*ws, *gs, *ms, *vs)
    return [outs[j * n:(j + 1) * n] for j in range(4)]


WEIGHTS = ["ffn1_norm", "ffn1_w_gate", "ffn1_w_up", "ffn1_w_down", "mix_norm", "w_in", "conv_w", "conv_b",
           "rg_w_a", "rg_b_a", "rg_w_x", "rg_b_x", "rg_lambda", "q_norm", "k_norm", "rnn_out_norm",
           "attn_out_norm", "w_out", "ffn2_norm", "ffn2_w_gate", "ffn2_w_up", "ffn2_w_down"]
BIG = ["ffn1_w_gate", "ffn1_w_up", "ffn1_w_down", "w_in", "w_out", "ffn2_w_gate", "ffn2_w_up", "ffn2_w_down"]
SMALL = [n for n in WEIGHTS if n not in BIG]
PACK_LANES = 128
PACK_ROW_ALIGN = 8


def _hidden_major(name, a):
    return jnp.transpose(a) if name.endswith(("w_gate", "w_up")) else a


def _pack(parts):
    sizes = [math.prod(p.shape) for p in parts]
    unit = PACK_LANES * PACK_ROW_ALIGN
    padded = -(-sum(sizes) // unit) * unit
    flat, at = 0.0, 0
    for p, size in zip(parts, sizes):
        flat = flat + jnp.pad(p.reshape(-1), (at, padded - at - size))
        at += size
    return flat.reshape(-1, PACK_LANES)


def _unpack(packed, shapes):
    flat = packed.reshape(-1)
    out, at = [], 0
    for shp in shapes:
        size = math.prod(shp)
        out.append(flat[at:at + size].reshape(shp))
        at += size
    return out


def kernel(x, ffn1_norm, ffn1_w_gate, ffn1_w_up, ffn1_w_down, mix_norm, w_in, conv_w, conv_b, rg_w_a, rg_b_a, rg_w_x, rg_b_x, rg_lambda, q_norm, k_norm, rnn_out_norm, attn_out_norm, w_out, ffn2_norm, ffn2_w_gate, ffn2_w_up, ffn2_w_down, loss_target, m_ffn1_norm, m_ffn1_w_gate, m_ffn1_w_up, m_ffn1_w_down, m_mix_norm, m_w_in, m_conv_w, m_conv_b, m_rg_w_a, m_rg_b_a, m_rg_w_x, m_rg_b_x, m_rg_lambda, m_q_norm, m_k_norm, m_rnn_out_norm, m_attn_out_norm, m_w_out, m_ffn2_norm, m_ffn2_w_gate, m_ffn2_w_up, m_ffn2_w_down, v_ffn1_norm, v_ffn1_w_gate, v_ffn1_w_up, v_ffn1_w_down, v_mix_norm, v_w_in, v_conv_w, v_conv_b, v_rg_w_a, v_rg_b_a, v_rg_w_x, v_rg_b_x, v_rg_lambda, v_q_norm, v_k_norm, v_rnn_out_norm, v_attn_out_norm, v_w_out, v_ffn2_norm, v_ffn2_w_gate, v_ffn2_w_up, v_ffn2_w_down):
    given = dict(locals())
    w = {n: given[n] for n in WEIGHTS}
    m = {n: given["m_" + n] for n in WEIGHTS}
    v = {n: given["v_" + n] for n in WEIGHTS}
    chip = 2 * lax.axis_index("x") + lax.axis_index("y")

    where = jnp.stack([lax.axis_index("c"), chip]).astype(jnp.int32)

    stacks = dict(zip(BIG, _place_shards([_hidden_major(n, w[n][0]) for n in BIG], where, "place_weights")))
    conv_stack = _place_shard(w["conv_w"][0], where, F32, "place_conv_w")
    small = {n: (w[n][0] if w[n].ndim > 2 else w[n]) for n in SMALL if n != "conv_w"}

    grad_x, slots, gs, everyone = _local_step(x[0], loss_target[0], stacks, conv_stack, small, where)

    swapped = _half_swap(_chip_sum([slots[n] for n in BIG], where, "chip_sums"))
    g2s = [t.reshape(t.shape[0] * t.shape[1], t.shape[2]) for t in swapped]
    flat = lambda tree: [_hidden_major(n, tree[n][0]) for n in BIG]
    g2s, d2s, m2s, v2s = _adamw(flat(w), g2s, flat(m), flat(v), "adamw_weights", ADAMW_STEPS)
    grads, deltas, new_m, new_v = {}, {}, {}, {}
    for tree, parts in ((grads, g2s), (deltas, d2s), (new_m, m2s), (new_v, v2s)):
        tree.update({n: _hidden_major(n, a).reshape(w[n].shape) for n, a in zip(BIG, parts)})

    full_shapes = [gs[n].shape for n in SMALL]
    *summed, loss = _unpack(_slot_sum(everyone, "small_grad_sum"), full_shapes + [(1, 1)])
    g_parts = dict(zip(SMALL, summed))
    quarter = D_RNN // N_CHIPS
    g_parts["conv_w"] = lax.dynamic_slice_in_dim(g_parts["conv_w"], chip * quarter, quarter, axis=1)
    local_shapes = [w[n].shape for n in SMALL]
    pk = lambda tree: _pack([tree[n] for n in SMALL])
    (g_s,), (d_s,), (m_s,), (v_s,) = _adamw([pk(w)], [pk(g_parts)], [pk(m)], [pk(v)], "adamw_small")
    for tree, packed in ((grads, g_s), (deltas, d_s), (new_m, m_s), (new_v, v_s)):
        tree.update(zip(SMALL, _unpack(packed, local_shapes)))

    return (loss[0, 0], grad_x.reshape(x.shape), *[grads[n] for n in WEIGHTS], *[deltas[n] for n in WEIGHTS],
            *[new_m[n] for n in WEIGHTS], *[new_v[n] for n in WEIGHTS])
```
